```python
import jax, jax.numpy as jnp
from jax import lax
import numpy as np

D_MODEL = 1024
BATCH = 8
SEQ = 4096
DEPTH = 1

ATT_HEAD_DIM = 64
ATT_HEADS_PER_GROUP = 8
DILATED_GROUPS = ((128, 1), (512, 4), (2048, 16))
N_ATT_GROUPS = len(DILATED_GROUPS)
ATT_WIDTH = N_ATT_GROUPS * ATT_HEADS_PER_GROUP * ATT_HEAD_DIM
ATT_OUT_WIDTH = ATT_HEADS_PER_GROUP * ATT_HEAD_DIM
BAND_BLOCK = 128
ROPE_THETA = 10000.0

RET_HEADS = 4
RET_QK_WIDTH = D_MODEL // 2
RET_V_WIDTH = D_MODEL
RET_KEY_DIM = RET_QK_WIDTH // RET_HEADS
RET_VALUE_DIM = RET_V_WIDTH // RET_HEADS
RET_CHUNK = 128

FFN_HIDDEN = ((8 * D_MODEL // 3 + 255) // 256) * 256
NORM_EPS = 1e-6

IN_SPLITS = (ATT_WIDTH, ATT_WIDTH, ATT_WIDTH,
             RET_QK_WIDTH, RET_QK_WIDTH, RET_V_WIDTH, RET_V_WIDTH,
             D_MODEL, D_MODEL)
IN_WIDTH = int(sum(IN_SPLITS))
IN_OFFSETS = tuple(int(o) for o in np.cumsum(IN_SPLITS)[:-1])

kernel_name = "hybrid_dilated_attn_retention_gated"


def rmsnorm(x, g):
    xf = x.astype(jnp.float32)
    y = xf * lax.rsqrt(jnp.mean(xf * xf, axis=-1, keepdims=True) + NORM_EPS)
    return (y * g.astype(jnp.float32)).astype(x.dtype)


def apply_rope(t, pos):
    hd = t.shape[-1]
    inv = ROPE_THETA ** (-jnp.arange(0, hd, 2, dtype=jnp.float32) / hd)
    ang = pos[:, None] * inv[None, :]
    c = jnp.cos(ang)[:, None, :]
    s = jnp.sin(ang)[:, None, :]
    tf = t.astype(jnp.float32)
    t1, t2 = tf[..., : hd // 2], tf[..., hd // 2:]
    out = jnp.concatenate([t1 * c - t2 * s, t2 * c + t1 * s], axis=-1)
    return out.astype(t.dtype)


def dilated_causal_group(q, k, v, window, dilation):
    B, S, H, hd = q.shape
    n_strides = window // dilation
    L = S // dilation
    nb = -(-L // BAND_BLOCK)
    Lp = nb * BAND_BLOCK

    def to_sub(t):
        t = t.reshape(B, L, dilation, H, hd).transpose(0, 2, 3, 1, 4)
        return jnp.pad(t, ((0, 0), (0, 0), (0, 0), (0, Lp - L), (0, 0)))

    qs, ks, vs = to_sub(q), to_sub(k), to_sub(v)
    qb = qs.reshape(B, dilation, H, nb, BAND_BLOCK, hd)

    def band(t):
        tp = jnp.pad(t, ((0, 0), (0, 0), (0, 0), (BAND_BLOCK, 0), (0, 0)))
        prev = tp[:, :, :, :Lp].reshape(B, dilation, H, nb, BAND_BLOCK, hd)
        cur = t.reshape(B, dilation, H, nb, BAND_BLOCK, hd)
        return jnp.concatenate([prev, cur], axis=4)

    kb, vb = band(ks), band(vs)
    scores = jnp.einsum('bdhnqc,bdhnkc->bdhnqk', qb, kb).astype(jnp.float32) * (hd ** -0.5)
    qi = jnp.arange(BAND_BLOCK)[:, None]
    kj = jnp.arange(2 * BAND_BLOCK)[None, :]
    dist = BAND_BLOCK + qi - kj
    key_idx = jnp.arange(nb)[:, None, None] * BAND_BLOCK + kj[None] - BAND_BLOCK
    mask = (dist >= 0)[None] & (dist <= n_strides)[None] & (key_idx >= 0)
    scores = jnp.where(mask, scores, jnp.float32(-1e30))
    m = jnp.max(scores, axis=-1, keepdims=True)
    p = jnp.exp(scores - m)
    l = jnp.sum(p, axis=-1, keepdims=True)
    o = jnp.einsum('bdhnqk,bdhnkc->bdhnqc', (p / l).astype(v.dtype), vb)
    lse = (m + jnp.log(l))[..., 0]
    o = o.reshape(B, dilation, H, Lp, hd)[:, :, :, :L].transpose(0, 3, 1, 2, 4).reshape(B, S, H, hd)
    lse = lse.reshape(B, dilation, H, Lp)[:, :, :, :L].transpose(0, 3, 1, 2).reshape(B, S, H)
    return o, lse


def dilated_attention(q, k, v, pos):
    B, S, _ = q.shape
    shp = (B, S, N_ATT_GROUPS, ATT_HEADS_PER_GROUP, ATT_HEAD_DIM)
    q = apply_rope(q.reshape(B, S, -1, ATT_HEAD_DIM), pos).reshape(shp)
    k = apply_rope(k.reshape(B, S, -1, ATT_HEAD_DIM), pos).reshape(shp)
    v = v.reshape(shp)
    outs, lses = [], []
    for g, (window, dilation) in enumerate(DILATED_GROUPS):
        o, lse = dilated_causal_group(q[:, :, g], k[:, :, g], v[:, :, g], window, dilation)
        outs.append(o)
        lses.append(lse)
    w = jax.nn.softmax(jnp.stack(lses, axis=0), axis=0)
    o = sum(w[g][..., None].astype(outs[g].dtype) * outs[g] for g in range(N_ATT_GROUPS))
    return o.reshape(B, S, ATT_OUT_WIDTH)


def retnet_theta_shift(t, pos):
    dk = t.shape[-1]
    ang_base = 1.0 / (ROPE_THETA ** jnp.linspace(0.0, 1.0, dk // 2, dtype=jnp.float32))
    ang = pos[:, None] * ang_base[None, :]
    c = jnp.cos(ang)[:, None, :]
    s = jnp.sin(ang)[:, None, :]
    t0, t1 = t[..., 0::2], t[..., 1::2]
    r0 = t0 * c - t1 * s
    r1 = t1 * c + t0 * s
    return jnp.stack([r0, r1], axis=-1).reshape(t.shape)


def retention(q, k, v, pos):
    B, S, _ = q.shape
    C = RET_CHUNK
    nc = S // C
    q = retnet_theta_shift(q.astype(jnp.float32).reshape(B, S, RET_HEADS, RET_KEY_DIM), pos)
    k = retnet_theta_shift(k.astype(jnp.float32).reshape(B, S, RET_HEADS, RET_KEY_DIM), pos)
    k = k * (RET_KEY_DIM ** -0.5)
    v = v.astype(jnp.float32).reshape(B, S, RET_HEADS, RET_VALUE_DIM)

    def chunks(t):
        return t.reshape(B, nc, C, RET_HEADS, t.shape[-1]).transpose(0, 3, 1, 2, 4)

    qc, kc, vc = chunks(q), chunks(k), chunks(v)
    log_g = jnp.log1p(-(2.0 ** (-5.0 - jnp.arange(RET_HEADS, dtype=jnp.float32))))
    idx = jnp.arange(C, dtype=jnp.float32)
    diff = idx[:, None] - idx[None, :]
    decay = jnp.where(diff[None] >= 0, jnp.exp(jnp.maximum(diff, 0.0)[None] * log_g[:, None, None]), 0.0)
    inner = jnp.einsum('bhnid,bhnjd->bhnij', qc, kc) * decay[None, :, None]
    inner = jnp.einsum('bhnij,bhnje->bhnie', inner, vc)
    zeta = jnp.exp((C - 1 - idx)[None, :] * log_g[:, None])
    xi = jnp.exp((idx + 1.0)[None, :] * log_g[:, None])
    kv = jnp.einsum('bhncd,bhnce->bhnde', kc * zeta[None, :, None, :, None], vc)
    chunk_decay = jnp.exp(C * log_g)

    def step(R, kv_c):
        return R * chunk_decay[None, :, None, None] + kv_c, R

    R0 = jnp.zeros((B, RET_HEADS, RET_KEY_DIM, RET_VALUE_DIM), jnp.float32)
    _, R_prev = lax.scan(step, R0, kv.transpose(2, 0, 1, 3, 4))
    cross = jnp.einsum('bhncd,nbhde->bhnce', qc * xi[None, :, None, :, None], R_prev)
    o = (inner + cross).transpose(0, 2, 3, 1, 4).reshape(B, S, RET_HEADS, RET_VALUE_DIM)
    mu = jnp.mean(o, axis=-1, keepdims=True)
    var = jnp.mean(jnp.square(o - mu), axis=-1, keepdims=True)
    o = (o - mu) * lax.rsqrt(var + NORM_EPS)
    return o.reshape(B, S, RET_V_WIDTH)


def _fwd_setup_inputs(seed: int = 0) -> dict:
    key = jax.random.key(seed)
    ks = jax.random.split(key, 12)
    f = jnp.float32

    def w(k, shape, fan_in):
        return jax.random.normal(k, shape, f) * (fan_in ** -0.5)

    def gain(k, shape):
        return 1.0 + 0.02 * jax.random.normal(k, shape, f)

    return {
        "x": jax.random.normal(ks[0], (BATCH, SEQ, D_MODEL), f),
        "norm_mix_g": gain(ks[1], (DEPTH, D_MODEL)),
        "w_in": w(ks[2], (DEPTH, D_MODEL, IN_WIDTH), D_MODEL),
        "w_out_attn": w(ks[3], (DEPTH, ATT_OUT_WIDTH, D_MODEL), ATT_OUT_WIDTH),
        "w_out_ret": w(ks[4], (DEPTH, RET_V_WIDTH, D_MODEL), RET_V_WIDTH),
        "w_out": w(ks[5], (DEPTH, D_MODEL, D_MODEL), D_MODEL),
        "norm_ffn_g": gain(ks[6], (DEPTH, D_MODEL)),
        "w_ffn_gate": w(ks[7], (DEPTH, D_MODEL, FFN_HIDDEN), D_MODEL),
        "w_ffn_up": w(ks[8], (DEPTH, D_MODEL, FFN_HIDDEN), D_MODEL),
        "w_ffn_down": w(ks[9], (DEPTH, FFN_HIDDEN, D_MODEL), FFN_HIDDEN),
        "norm_final_g": gain(ks[10], (D_MODEL,)),
    }


def _fwd_reference(x, norm_mix_g, w_in, w_out_attn, w_out_ret, w_out, norm_ffn_g,
              w_ffn_gate, w_ffn_up, w_ffn_down, norm_final_g):
    S = x.shape[1]
    pos = jnp.arange(S, dtype=jnp.float32)
    for layer in range(DEPTH):
        h = rmsnorm(x, norm_mix_g[layer])
        proj = h @ w_in[layer]
        (qa, ka, va, qr, kr, vr, gr, gate_a, gate_r) = jnp.split(proj, IN_OFFSETS, axis=-1)
        ya = dilated_attention(qa, ka, va, pos) @ w_out_attn[layer]
        yr = retention(qr, kr, vr, pos).astype(x.dtype) * jax.nn.silu(gr)
        yr = yr @ w_out_ret[layer]
        merged = jax.nn.sigmoid(gate_a) * ya + jax.nn.sigmoid(gate_r) * yr
        x = x + merged @ w_out[layer]
        h2 = rmsnorm(x, norm_ffn_g[layer])
        x = x + (jax.nn.silu(h2 @ w_ffn_gate[layer]) * (h2 @ w_ffn_up[layer])) @ w_ffn_down[layer]
    return rmsnorm(x, norm_final_g)


import jax as _jax
import jax.numpy as _jnp

TWIN_FORMAT = 'train_step'
FWD_PARAMS = ['x', 'norm_mix_g', 'w_in', 'w_out_attn', 'w_out_ret', 'w_out', 'norm_ffn_g', 'w_ffn_gate', 'w_ffn_up', 'w_ffn_down', 'norm_final_g']
TWIN_WEIGHTS = ['norm_mix_g', 'w_in', 'w_out_attn', 'w_out_ret', 'w_out', 'norm_ffn_g', 'w_ffn_gate', 'w_ffn_up', 'w_ffn_down', 'norm_final_g']
TWIN_DIFF_INPUT = 'x'
TWIN_INPUTS = ['x', 'norm_mix_g', 'w_in', 'w_out_attn', 'w_out_ret', 'w_out', 'norm_ffn_g', 'w_ffn_gate', 'w_ffn_up', 'w_ffn_down', 'norm_final_g', 'loss_target', 'm_norm_mix_g', 'm_w_in', 'm_w_out_attn', 'm_w_out_ret', 'm_w_out', 'm_norm_ffn_g', 'm_w_ffn_gate', 'm_w_ffn_up', 'm_w_ffn_down', 'm_norm_final_g', 'v_norm_mix_g', 'v_w_in', 'v_w_out_attn', 'v_w_out_ret', 'v_w_out', 'v_norm_ffn_g', 'v_w_ffn_gate', 'v_w_ffn_up', 'v_w_ffn_down', 'v_norm_final_g']
TWIN_OUTPUTS = ['loss', 'grad_x', 'grad_norm_mix_g', 'grad_w_in', 'grad_w_out_attn', 'grad_w_out_ret', 'grad_w_out', 'grad_norm_ffn_g', 'grad_w_ffn_gate', 'grad_w_ffn_up', 'grad_w_ffn_down', 'grad_norm_final_g', 'delta_norm_mix_g', 'delta_w_in', 'delta_w_out_attn', 'delta_w_out_ret', 'delta_w_out', 'delta_norm_ffn_g', 'delta_w_ffn_gate', 'delta_w_ffn_up', 'delta_w_ffn_down', 'delta_norm_final_g', 'new_m_norm_mix_g', 'new_m_w_in', 'new_m_w_out_attn', 'new_m_w_out_ret', 'new_m_w_out', 'new_m_norm_ffn_g', 'new_m_w_ffn_gate', 'new_m_w_ffn_up', 'new_m_w_ffn_down', 'new_m_norm_final_g', 'new_v_norm_mix_g', 'new_v_w_in', 'new_v_w_out_attn', 'new_v_w_out_ret', 'new_v_w_out', 'new_v_norm_ffn_g', 'new_v_w_ffn_gate', 'new_v_w_ffn_up', 'new_v_w_ffn_down', 'new_v_norm_final_g']
TWIN_LEAF_KINDS = {'loss': 'loss', 'grad_x': 'grad_x', 'grad_norm_mix_g': 'grad_w', 'grad_w_in': 'grad_w', 'grad_w_out_attn': 'grad_w', 'grad_w_out_ret': 'grad_w', 'grad_w_out': 'grad_w', 'grad_norm_ffn_g': 'grad_w', 'grad_w_ffn_gate': 'grad_w', 'grad_w_ffn_up': 'grad_w', 'grad_w_ffn_down': 'grad_w', 'grad_norm_final_g': 'grad_w', 'delta_norm_mix_g': 'delta_w', 'delta_w_in': 'delta_w', 'delta_w_out_attn': 'delta_w', 'delta_w_out_ret': 'delta_w', 'delta_w_out': 'delta_w', 'delta_norm_ffn_g': 'delta_w', 'delta_w_ffn_gate': 'delta_w', 'delta_w_ffn_up': 'delta_w', 'delta_w_ffn_down': 'delta_w', 'delta_norm_final_g': 'delta_w', 'new_m_norm_mix_g': 'new_m', 'new_m_w_in': 'new_m', 'new_m_w_out_attn': 'new_m', 'new_m_w_out_ret': 'new_m', 'new_m_w_out': 'new_m', 'new_m_norm_ffn_g': 'new_m', 'new_m_w_ffn_gate': 'new_m', 'new_m_w_ffn_up': 'new_m', 'new_m_w_ffn_down': 'new_m', 'new_m_norm_final_g': 'new_m', 'new_v_norm_mix_g': 'new_v', 'new_v_w_in': 'new_v', 'new_v_w_out_attn': 'new_v', 'new_v_w_out_ret': 'new_v', 'new_v_w_out': 'new_v', 'new_v_norm_ffn_g': 'new_v', 'new_v_w_ffn_gate': 'new_v', 'new_v_w_ffn_up': 'new_v', 'new_v_w_ffn_down': 'new_v', 'new_v_norm_final_g': 'new_v'}


def _forward(args):
    return _fwd_reference(*[args[k] for k in FWD_PARAMS])


def _output_shape():
    out = _jax.eval_shape(lambda: _forward(_fwd_setup_inputs(0)))
    return out.shape, out.dtype

N_MICROBATCH = 1
ADAM_LR = 0.001
ADAM_B1 = 0.9
ADAM_B2 = 0.999
ADAM_EPS = 1e-08
ADAM_WD = 0.01
ADAM_STEP = 10
PER_EXAMPLE_BATCH_AXIS = {'x': 0, 'loss_target': 0}
SHARED_INPUTS = []
_WEIGHT_DTYPES = {'norm_mix_g': _jnp.float32, 'w_in': _jnp.float32, 'w_out_attn': _jnp.float32, 'w_out_ret': _jnp.float32, 'w_out': _jnp.float32, 'norm_ffn_g': _jnp.float32, 'w_ffn_gate': _jnp.float32, 'w_ffn_up': _jnp.float32, 'w_ffn_down': _jnp.float32, 'norm_final_g': _jnp.float32}
MOMENT_SCALE = {'norm_mix_g': 1.354661e-01, 'w_in': 4.209615e-02, 'w_out_attn': 1.894209e-02, 'w_out_ret': 6.106622e-02, 'w_out': 6.343662e-02, 'norm_ffn_g': 1.319106e-01, 'w_ffn_gate': 5.440315e-02, 'w_ffn_up': 5.301004e-02, 'w_ffn_down': 8.767261e-02, 'norm_final_g': 3.198053e+01}


def _to_microbatches(a, axis):
    t = _jnp.moveaxis(a, axis, 0)
    t = t.reshape((N_MICROBATCH, t.shape[0] // N_MICROBATCH) + t.shape[1:])
    return _jnp.moveaxis(t, 1, axis + 1)


def setup_inputs(seed: int = 0) -> dict:
    inp = _fwd_setup_inputs(seed)
    key = _jax.random.fold_in(_jax.random.key(seed), 7919)
    shape, _ = _output_shape()
    out = dict(inp)
    out["loss_target"] = _jax.random.normal(_jax.random.fold_in(key, 0), shape, _jnp.float32)
    for i, name in enumerate(TWIN_WEIGHTS):
        w = inp[name].astype(_jnp.float32)
        if MOMENT_SCALE is None:
            s = _jnp.sqrt(_jnp.mean(_jnp.square(w)) + 1e-30)
        else:
            s = MOMENT_SCALE[name]
        km, kv = _jax.random.split(_jax.random.fold_in(key, i + 1))
        out[name] = w
        out["m_" + name] = s * _jax.random.normal(km, w.shape, _jnp.float32)
        out["v_" + name] = (s * s) * _jax.random.uniform(kv, w.shape, _jnp.float32, 0.5, 1.5)
    if N_MICROBATCH > 1:
        for name, axis in PER_EXAMPLE_BATCH_AXIS.items():
            out[name] = _to_microbatches(out[name], axis)
    return {'x': out['x'], 'norm_mix_g': out['norm_mix_g'], 'w_in': out['w_in'], 'w_out_attn': out['w_out_attn'], 'w_out_ret': out['w_out_ret'], 'w_out': out['w_out'], 'norm_ffn_g': out['norm_ffn_g'], 'w_ffn_gate': out['w_ffn_gate'], 'w_ffn_up': out['w_ffn_up'], 'w_ffn_down': out['w_ffn_down'], 'norm_final_g': out['norm_final_g'], 'loss_target': out['loss_target'], 'm_norm_mix_g': out['m_norm_mix_g'], 'm_w_in': out['m_w_in'], 'm_w_out_attn': out['m_w_out_attn'], 'm_w_out_ret': out['m_w_out_ret'], 'm_w_out': out['m_w_out'], 'm_norm_ffn_g': out['m_norm_ffn_g'], 'm_w_ffn_gate': out['m_w_ffn_gate'], 'm_w_ffn_up': out['m_w_ffn_up'], 'm_w_ffn_down': out['m_w_ffn_down'], 'm_norm_final_g': out['m_norm_final_g'], 'v_norm_mix_g': out['v_norm_mix_g'], 'v_w_in': out['v_w_in'], 'v_w_out_attn': out['v_w_out_attn'], 'v_w_out_ret': out['v_w_out_ret'], 'v_w_out': out['v_w_out'], 'v_norm_ffn_g': out['v_norm_ffn_g'], 'v_w_ffn_gate': out['v_w_ffn_gate'], 'v_w_ffn_up': out['v_w_ffn_up'], 'v_w_ffn_down': out['v_w_ffn_down'], 'v_norm_final_g': out['v_norm_final_g']}


def _loss(weights, diff, rest, loss_target):
    with _jax.named_scope("forward"):
        args = {**rest, TWIN_DIFF_INPUT: diff, **{k: w.astype(_WEIGHT_DTYPES[k]) for k, w in weights.items()}}
        y = _forward(args)
    with _jax.named_scope("loss_head"):
        err = _jnp.square(y.astype(_jnp.float32) - loss_target)
        return 0.5 * _jnp.sum(_jnp.mean(err, axis=-1)) if err.ndim else 0.5 * err


def _adamw(w, g, m, v):
    m = ADAM_B1 * m + (1.0 - ADAM_B1) * g
    v = ADAM_B2 * v + (1.0 - ADAM_B2) * _jnp.square(g)
    m_hat = m / (1.0 - ADAM_B1 ** ADAM_STEP)
    v_hat = v / (1.0 - ADAM_B2 ** ADAM_STEP)
    delta = -ADAM_LR * (m_hat / (_jnp.sqrt(v_hat) + ADAM_EPS) + ADAM_WD * w)
    return delta, m, v


def reference(x, norm_mix_g, w_in, w_out_attn, w_out_ret, w_out, norm_ffn_g, w_ffn_gate, w_ffn_up, w_ffn_down, norm_final_g, loss_target, m_norm_mix_g, m_w_in, m_w_out_attn, m_w_out_ret, m_w_out, m_norm_ffn_g, m_w_ffn_gate, m_w_ffn_up, m_w_ffn_down, m_norm_final_g, v_norm_mix_g, v_w_in, v_w_out_attn, v_w_out_ret, v_w_out, v_norm_ffn_g, v_w_ffn_gate, v_w_ffn_up, v_w_ffn_down, v_norm_final_g):
    given = dict(x=x, norm_mix_g=norm_mix_g, w_in=w_in, w_out_attn=w_out_attn, w_out_ret=w_out_ret, w_out=w_out, norm_ffn_g=norm_ffn_g, w_ffn_gate=w_ffn_gate, w_ffn_up=w_ffn_up, w_ffn_down=w_ffn_down, norm_final_g=norm_final_g, loss_target=loss_target, m_norm_mix_g=m_norm_mix_g, m_w_in=m_w_in, m_w_out_attn=m_w_out_attn, m_w_out_ret=m_w_out_ret, m_w_out=m_w_out, m_norm_ffn_g=m_norm_ffn_g, m_w_ffn_gate=m_w_ffn_gate, m_w_ffn_up=m_w_ffn_up, m_w_ffn_down=m_w_ffn_down, m_norm_final_g=m_norm_final_g, v_norm_mix_g=v_norm_mix_g, v_w_in=v_w_in, v_w_out_attn=v_w_out_attn, v_w_out_ret=v_w_out_ret, v_w_out=v_w_out, v_norm_ffn_g=v_norm_ffn_g, v_w_ffn_gate=v_w_ffn_gate, v_w_ffn_up=v_w_ffn_up, v_w_ffn_down=v_w_ffn_down, v_norm_final_g=v_norm_final_g)
    weights = {n: given[n] for n in TWIN_WEIGHTS}
    shared = {n: given[n] for n in SHARED_INPUTS}
    per_example = {n: given[n] for n in ['x']}
    grad_fn = _jax.value_and_grad(_loss, argnums=(0, 1))

    def one_microbatch(ex, loss_target):
        ex = dict(ex)
        diff = ex.pop(TWIN_DIFF_INPUT)
        return grad_fn(weights, diff, {**shared, **ex}, loss_target)

    if N_MICROBATCH == 1:
        loss, (grad_w, grad_x) = one_microbatch(per_example, given["loss_target"])
    else:
        def body(carry, xs):
            loss_sum, grad_sum = carry
            l_k, (gw_k, gx_k) = one_microbatch(xs[0], xs[1])
            with _jax.named_scope("update"):
                return (loss_sum + l_k, _jax.tree.map(_jnp.add, grad_sum, gw_k)), gx_k

        init = (_jnp.zeros((), _jnp.float32), _jax.tree.map(_jnp.zeros_like, weights))
        (loss, grad_w), grad_x = _jax.lax.scan(body, init, (per_example, given["loss_target"]))
    with _jax.named_scope("update"):
        delta_w, new_m, new_v = {}, {}, {}
        for n in TWIN_WEIGHTS:
            delta_w[n], new_m[n], new_v[n] = _adamw(weights[n], grad_w[n], given["m_" + n], given["v_" + n])
    return (loss, grad_x, *[grad_w[n] for n in TWIN_WEIGHTS], *[delta_w[n] for n in TWIN_WEIGHTS],
            *[new_m[n] for n in TWIN_WEIGHTS], *[new_v[n] for n in TWIN_WEIGHTS])
```

```python
import functools
import math

import numpy as np
import jax
import jax.numpy as jnp
from jax import lax
from jax.experimental import pallas as pl
from jax.experimental.pallas import tpu as pltpu

F32, BF16 = jnp.float32, jnp.bfloat16
SDS = jax.ShapeDtypeStruct
MESH = pl.DeviceIdType.MESH

D_MODEL = 1024
PROJ_W = 9728
COLB = 512
N_COLB = PROJ_W // COLB
QA_B, KA_B, VA_B = 0, 3, 6
QR_B, KR_B = 9, 10
FFN_HID = 2816
N_SHARD = 4
HID_S = FFN_HID // N_SHARD
W_IN_S = PROJ_W // N_SHARD
DILATIONS = (1, 4, 16)
BLK = 128
RET_HEADS = 4
ROPE_THETA = 10000.0
NORM_EPS = 1e-6
ADAM_LR, ADAM_B1, ADAM_B2, ADAM_EPS, ADAM_WD, ADAM_STEP = 0.001, 0.9, 0.999, 1e-08, 0.01, 10
VMEM_LIMIT = 56 << 20


def _cparams(*sem):
    return pltpu.CompilerParams(dimension_semantics=sem or None, vmem_limit_bytes=VMEM_LIMIT)


def _dot(a, b):
    return jnp.dot(a, b, preferred_element_type=F32)


def _dot_nt(a, b):
    return lax.dot_general(a, b, (((1,), (1,)), ((), ())), preferred_element_type=F32)


def _dot_tn(a, b):
    return lax.dot_general(a, b, (((0,), (0,)), ((), ())), preferred_element_type=F32)


def _sigmoid(z):
    return 1.0 / (1.0 + jnp.exp(-z))


def _tables(S):
    pos = jnp.arange(S, dtype=F32)
    lane = np.arange(128)
    inv = ROPE_THETA ** (-jnp.arange(0, 64, 2, dtype=F32) / 64)
    ang = pos[:, None] * inv[None, :]
    idx = (lane % 64) % 32
    c, s = jnp.cos(ang)[:, idx], jnp.sin(ang)[:, idx]
    first = jnp.asarray((lane % 64) < 32)[None, :]
    rope = jnp.stack([c, jnp.where(first, 0.0, s), jnp.where(first, -s, 0.0)])
    base = 1.0 / (ROPE_THETA ** jnp.linspace(0.0, 1.0, 64, dtype=F32))
    ang2 = pos[:, None] * base[None, :]
    c2, s2 = jnp.cos(ang2)[:, lane // 2], jnp.sin(ang2)[:, lane // 2]
    even = jnp.asarray(lane % 2 == 0)[None, :]
    th = jnp.stack([c2, jnp.where(even, 0.0, s2), jnp.where(even, -s2, 0.0)])
    return jnp.stack([rope, th, th * (128 ** -0.5)]).astype(F32)


def _rot(a, c, sa, sb, shift):
    return a * c + pltpu.roll(a, shift, 1) * sa + pltpu.roll(a, 128 - shift, 1) * sb


def _unrot(g, c, sa, sb, shift):
    return g * c + pltpu.roll(g * sa, 128 - shift, 1) + pltpu.roll(g * sb, shift, 1)


def _ret_consts():
    h = np.arange(RET_HEADS, dtype=np.float64)
    log_g = np.log1p(-(2.0 ** (-5.0 - h)))
    idx = np.arange(BLK, dtype=np.float64)
    diff = idx[:, None] - idx[None, :]
    dmask = np.where(diff[None] >= 0, np.exp(np.maximum(diff, 0.0)[None] * log_g[:, None, None]), 0.0)
    zeta = np.exp((BLK - 1 - idx)[None, :] * log_g[:, None])
    xi = np.exp((idx + 1.0)[None, :] * log_g[:, None])
    dec = np.exp(BLK * log_g)
    rep = lambda v: np.broadcast_to(v[:, :, None], (RET_HEADS, BLK, 128))
    return (jnp.asarray(dmask, F32), jnp.asarray(rep(zeta), F32), jnp.asarray(rep(xi), F32),
            jnp.asarray(np.broadcast_to(dec[:, None, None], (RET_HEADS, 8, 256)), F32))


def _rms_fwd(x, g):
    S = x.shape[0]
    tm = 512

    def body(x_ref, g_ref, h_ref):
        xv = x_ref[...]
        r = lax.rsqrt(jnp.mean(xv * xv, axis=-1, keepdims=True) + NORM_EPS)
        h_ref[...] = (xv * r * g_ref[...]).astype(BF16)

    return pl.pallas_call(
        body, out_shape=SDS((S, D_MODEL), BF16), grid=(S // tm,),
        in_specs=[pl.BlockSpec((tm, D_MODEL), lambda i: (i, 0)), pl.BlockSpec((1, D_MODEL), lambda i: (0, 0))],
        out_specs=pl.BlockSpec((tm, D_MODEL), lambda i: (i, 0)),
        compiler_params=_cparams("parallel"), name="rms_fwd")(x, g)


def _in_proj(h, w_in, tab):
    S = h.shape[0]
    tm = min(S, 2048)

    def body(h_ref, w_ref, t_ref, o_ref):
        j = pl.program_id(1)
        acc = _dot(h_ref[...], w_ref[...])
        is_rope = j < 6
        is_theta = (j == QR_B) | (j == KR_B)

        def rotated(shift):
            c, sa, sb = t_ref[0, 0], t_ref[0, 1], t_ref[0, 2]
            for k in range(COLB // 128):
                sl = slice(k * 128, (k + 1) * 128)
                o_ref[:, sl] = _rot(acc[:, sl], c, sa, sb, shift).astype(BF16)

        @pl.when(is_rope)
        def _():
            rotated(32)

        @pl.when(is_theta)
        def _():
            rotated(1)

        @pl.when(jnp.logical_not(is_rope | is_theta))
        def _():
            o_ref[...] = acc.astype(BF16)

    def tab_map(i, j):
        return (jnp.where(j == QR_B, 1, jnp.where(j == KR_B, 2, 0)), 0, i, 0)

    return pl.pallas_call(
        body, out_shape=SDS((S, PROJ_W), BF16), grid=(S // tm, N_COLB),
        in_specs=[pl.BlockSpec((tm, D_MODEL), lambda i, j: (i, 0)),
                  pl.BlockSpec((D_MODEL, COLB), lambda i, j: (0, j)),
                  pl.BlockSpec((1, 3, tm, 128), tab_map)],
        out_specs=pl.BlockSpec((tm, COLB), lambda i, j: (i, j)),
        compiler_params=_cparams("parallel", "arbitrary"), name="in_proj")(h, w_in, tab)


def _band_mask(n):
    qi = lax.broadcasted_iota(jnp.int32, (BLK, 2 * BLK), 0)
    kj = lax.broadcasted_iota(jnp.int32, (BLK, 2 * BLK), 1)
    dist = BLK + qi - kj
    return (dist >= 0) & (dist <= BLK) & ((kj >= BLK) | (n > 0))


def _attn_fwd(qkv, d, gi):
    L = qkv.shape[0]
    nb = L // BLK

    def body(q_ref, kc_ref, kp_ref, vc_ref, vp_ref, o_ref, lse_ref):
        n = pl.program_id(1)
        mask = _band_mask(n)
        lo = lax.broadcasted_iota(jnp.int32, (BLK, 128), 1) < 64
        for c in range(4):
            sl = slice(c * 128, (c + 1) * 128)
            q = q_ref[:, sl]
            k = jnp.concatenate([kp_ref[:, sl], kc_ref[:, sl]], axis=0)
            v = jnp.concatenate([vp_ref[:, sl], vc_ref[:, sl]], axis=0)
            o_c = None
            for hh in range(2):
                hm = lo if hh == 0 else jnp.logical_not(lo)
                qm = jnp.where(hm, q, jnp.zeros_like(q))
                s = _dot_nt(qm, k) * 0.125
                s = jnp.where(mask, s, jnp.float32(-1e30))
                m = jnp.max(s, axis=-1, keepdims=True)
                p = jnp.exp(s - m)
                l = jnp.sum(p, axis=-1, keepdims=True)
                o_h = _dot((p / l).astype(BF16), v)
                o_c = o_h if hh == 0 else jnp.where(lo, o_c, o_h)
                hs = slice((2 * c + hh) * 128, (2 * c + hh + 1) * 128)
                lse_ref[:, hs] = jnp.broadcast_to(m + jnp.log(l), (BLK, 128))
            o_ref[:, sl] = o_c

    prev = lambda n: jnp.maximum(n - 1, 0)
    return pl.pallas_call(
        body, out_shape=(SDS((L, d * 512), F32), SDS((L, d * 1024), F32)), grid=(d, nb),
        in_specs=[pl.BlockSpec((BLK, 512), lambda r, n: (n, 3 * r)),
                  pl.BlockSpec((BLK, 512), lambda r, n: (n, 3 * r + 1)),
                  pl.BlockSpec((BLK, 512), lambda r, n: (prev(n), 3 * r + 1)),
                  pl.BlockSpec((BLK, 512), lambda r, n: (n, 3 * r + 2)),
                  pl.BlockSpec((BLK, 512), lambda r, n: (prev(n), 3 * r + 2))],
        out_specs=(pl.BlockSpec((BLK, 512), lambda r, n: (n, r)),
                   pl.BlockSpec((BLK, 1024), lambda r, n: (n, r))),
        compiler_params=_cparams("parallel", "arbitrary"), name=f"attn_fwd_g{gi}")(qkv, qkv, qkv, qkv, qkv)


def _attn_merge(os_, lses):
    S = os_[0].shape[0]
    tm = 512

    def body(o0, o1, o2, l0, l1, l2, att_ref, lt_ref):
        lo = lax.broadcasted_iota(jnp.int32, (tm, 128), 1) < 64
        o_refs, l_refs = (o0, o1, o2), (l0, l1, l2)
        for c in range(4):
            sl = slice(c * 128, (c + 1) * 128)
            ws = []
            for hh in range(2):
                hs = slice((2 * c + hh) * 128, (2 * c + hh + 1) * 128)
                ls = [r[:, hs] for r in l_refs]
                m = jnp.maximum(jnp.maximum(ls[0], ls[1]), ls[2])
                es = [jnp.exp(v - m) for v in ls]
                z = es[0] + es[1] + es[2]
                lt_ref[:, hs] = m + jnp.log(z)
                ws.append([e / z for e in es])
            acc = jnp.zeros((tm, 128), F32)
            for g in range(3):
                acc = acc + jnp.where(lo, ws[0][g], ws[1][g]) * o_refs[g][:, sl]
            att_ref[:, sl] = acc.astype(BF16)

    ospec = pl.BlockSpec((tm, 512), lambda i: (i, 0))
    lspec = pl.BlockSpec((tm, 1024), lambda i: (i, 0))
    return pl.pallas_call(
        body, out_shape=(SDS((S, 512), BF16), SDS((S, 1024), F32)), grid=(S // tm,),
        in_specs=[ospec] * 3 + [lspec] * 3, out_specs=(ospec, lspec),
        compiler_params=_cparams("parallel"), name="attn_merge")(*os_, *lses)


def _ret_fwd(proj, consts):
    S = proj.shape[0]
    nc = S // BLK
    dmask, zeta, xi, dec = consts

    def body(q_ref, k_ref, v_ref, gr_ref, dm_ref, z_ref, x_ref, dec_ref, y_ref, rn_ref, rs_ref, st_ref, R):
        @pl.when(pl.program_id(1) == 0)
        def _():
            R[...] = jnp.zeros_like(R)

        q, k, v = q_ref[...], k_ref[...], v_ref[...]
        Rb = R[...].astype(BF16)
        st_ref[...] = Rb
        s = _dot_nt(q, k) * dm_ref[...]
        o = _dot(s.astype(BF16), v) + _dot((q.astype(F32) * x_ref[...]).astype(BF16), Rb)
        kz = (k.astype(F32) * z_ref[...]).astype(BF16)
        R[...] = R[...] * dec_ref[0:1, :] + _dot_tn(kz, v)
        mu = jnp.mean(o, axis=-1, keepdims=True)
        oc = o - mu
        rstd = lax.rsqrt(jnp.mean(oc * oc, axis=-1, keepdims=True) + NORM_EPS)
        rn = oc * rstd
        gr = gr_ref[...].astype(F32)
        y_ref[...] = (rn * gr * _sigmoid(gr)).astype(BF16)
        rn_ref[...] = rn.astype(BF16)
        rs_ref[...] = jnp.broadcast_to(rstd, (BLK, 128))

    hc = lambda h, c: (h, 0, 0)
    return pl.pallas_call(
        body,
        out_shape=(SDS((S, 1024), BF16), SDS((S, 1024), BF16), SDS((S, 512), F32), SDS((RET_HEADS, nc, BLK, 256), BF16)),
        grid=(RET_HEADS, nc),
        in_specs=[pl.BlockSpec((BLK, 128), lambda h, c: (c, 36 + h)),
                  pl.BlockSpec((BLK, 128), lambda h, c: (c, 40 + h)),
                  pl.BlockSpec((BLK, 256), lambda h, c: (c, 22 + h)),
                  pl.BlockSpec((BLK, 256), lambda h, c: (c, 26 + h)),
                  pl.BlockSpec((None, BLK, BLK), hc), pl.BlockSpec((None, BLK, 128), hc),
                  pl.BlockSpec((None, BLK, 128), hc), pl.BlockSpec((None, 8, 256), hc)],
        out_specs=(pl.BlockSpec((BLK, 256), lambda h, c: (c, h)), pl.BlockSpec((BLK, 256), lambda h, c: (c, h)),
                   pl.BlockSpec((BLK, 128), lambda h, c: (c, h)),
                   pl.BlockSpec((None, None, BLK, 256), lambda h, c: (h, c, 0, 0))),
        scratch_shapes=[pltpu.VMEM((BLK, 256), F32)],
        compiler_params=_cparams("parallel", "arbitrary"), name="ret_fwd")(proj, proj, proj, proj, dmask, zeta, xi, dec)


def _branch_merge(att, yrin, proj, wa, wr):
    S = att.shape[0]
    tm = min(S, 2048)

    def body(a_ref, y_ref, ga_ref, gr_ref, wa_ref, wr_ref, m_ref, ya_ref, yr_ref):
        ya = _dot(a_ref[...], wa_ref[...])
        yr = _dot(y_ref[...], wr_ref[...])
        m_ref[...] = (_sigmoid(ga_ref[...].astype(F32)) * ya + _sigmoid(gr_ref[...].astype(F32)) * yr).astype(BF16)
        ya_ref[...] = ya.astype(BF16)
        yr_ref[...] = yr.astype(BF16)

    ospec = pl.BlockSpec((tm, 512), lambda i, j: (i, j))
    return pl.pallas_call(
        body, out_shape=(SDS((S, D_MODEL), BF16),) * 3, grid=(S // tm, 2),
        in_specs=[pl.BlockSpec((tm, 512), lambda i, j: (i, 0)), pl.BlockSpec((tm, 1024), lambda i, j: (i, 0)),
                  pl.BlockSpec((tm, 512), lambda i, j: (i, 15 + j)), pl.BlockSpec((tm, 512), lambda i, j: (i, 17 + j)),
                  pl.BlockSpec((512, 512), lambda i, j: (0, j)), pl.BlockSpec((1024, 512), lambda i, j: (0, j))],
        out_specs=(ospec, ospec, ospec),
        compiler_params=_cparams("parallel", "arbitrary"), name="branch_merge")(att, yrin, proj, proj, wa, wr)


def _out_proj(merged, wo, x, g2):
    S = x.shape[0]
    tm = 1024

    def body(m_ref, w_ref, x_ref, g_ref, x1_ref, h2_ref):
        x1 = x_ref[...] + _dot(m_ref[...], w_ref[...])
        x1_ref[...] = x1
        r = lax.rsqrt(jnp.mean(x1 * x1, axis=-1, keepdims=True) + NORM_EPS)
        h2_ref[...] = (x1 * r * g_ref[...]).astype(BF16)

    row = pl.BlockSpec((tm, D_MODEL), lambda i: (i, 0))
    return pl.pallas_call(
        body, out_shape=(SDS((S, D_MODEL), F32), SDS((S, D_MODEL), BF16)), grid=(S // tm,),
        in_specs=[row, pl.BlockSpec((D_MODEL, D_MODEL), lambda i: (0, 0)), row, pl.BlockSpec((1, D_MODEL), lambda i: (0, 0))],
        out_specs=(row, row), compiler_params=_cparams("parallel"), name="out_proj")(merged, wo, x, g2)


def _ffn_up(h2, wg, wu):
    S = h2.shape[0]
    tm = min(S, 2048)

    def body(h_ref, wg_ref, wu_ref, g_ref, u_ref, a_ref):
        hv = h_ref[...]
        g = _dot(hv, wg_ref[...])
        u = _dot(hv, wu_ref[...])
        g_ref[...] = g.astype(BF16)
        u_ref[...] = u.astype(BF16)
        a_ref[...] = (g * _sigmoid(g) * u).astype(BF16)

    wspec = pl.BlockSpec((None, D_MODEL, HID_S), lambda i, s: (s, 0, 0))
    ospec = pl.BlockSpec((None, tm, HID_S), lambda i, s: (s, i, 0))
    return pl.pallas_call(
        body, out_shape=(SDS((N_SHARD, S, HID_S), BF16),) * 3, grid=(S // tm, N_SHARD),
        in_specs=[pl.BlockSpec((tm, D_MODEL), lambda i, s: (i, 0)), wspec, wspec],
        out_specs=(ospec, ospec, ospec),
        compiler_params=_cparams("parallel", "arbitrary"), name="ffn_up")(h2, wg, wu)


def _ffn_down_loss(act, wd, x1, g3, tgt):
    S = x1.shape[0]
    tm = 1024

    def body(a_ref, w_ref, x_ref, g_ref, t_ref, dx_ref, dxb_ref, dg_ref, ls_ref, acc):
        i, s = pl.program_id(0), pl.program_id(1)

        @pl.when(s == 0)
        def _():
            acc[...] = jnp.zeros_like(acc)

        @pl.when((i == 0) & (s == 0))
        def _():
            dg_ref[...] = jnp.zeros_like(dg_ref)
            ls_ref[...] = jnp.zeros_like(ls_ref)

        acc[...] += _dot(a_ref[...], w_ref[...])

        @pl.when(s == N_SHARD - 1)
        def _():
            x2 = x_ref[...] + acc[...]
            r = lax.rsqrt(jnp.mean(x2 * x2, axis=-1, keepdims=True) + NORM_EPS)
            xh = x2 * r
            g = g_ref[...]
            err = xh * g - t_ref[...]
            ls_ref[...] += jnp.sum(jnp.sum(err * err, axis=-1, keepdims=True), axis=0, keepdims=True) * (0.5 / D_MODEL)
            dy = err * (1.0 / D_MODEL)
            dg_ref[...] += jnp.sum(dy * xh, axis=0, keepdims=True)
            dxh = dy * g
            dx = r * (dxh - xh * jnp.mean(dxh * xh, axis=-1, keepdims=True))
            dx_ref[...] = dx
            dxb_ref[...] = dx.astype(BF16)

    row = pl.BlockSpec((tm, D_MODEL), lambda i, s: (i, 0))
    vec = pl.BlockSpec((1, D_MODEL), lambda i, s: (0, 0))
    return pl.pallas_call(
        body, out_shape=(SDS((S, D_MODEL), F32), SDS((S, D_MODEL), BF16), SDS((1, D_MODEL), F32), SDS((8, 128), F32)),
        grid=(S // tm, N_SHARD),
        in_specs=[pl.BlockSpec((None, tm, HID_S), lambda i, s: (s, i, 0)),
                  pl.BlockSpec((None, HID_S, D_MODEL), lambda i, s: (s, 0, 0)), row, vec, row],
        out_specs=(row, row, vec, pl.BlockSpec((8, 128), lambda i, s: (0, 0))),
        scratch_shapes=[pltpu.VMEM((tm, D_MODEL), F32)],
        compiler_params=_cparams("arbitrary", "arbitrary"), name="ffn_down_loss")(act, wd, x1, g3, tgt)


def _ffn_down_bwd(dx2b, wd, gte, up):
    S = dx2b.shape[0]
    tm = min(S, 2048)

    def body(d_ref, w_ref, g_ref, u_ref, dg_ref, du_ref):
        da = _dot_nt(d_ref[...], w_ref[...])
        g = g_ref[...].astype(F32)
        sg = _sigmoid(g)
        dg_ref[...] = (da * u_ref[...].astype(F32) * sg * (1.0 + g * (1.0 - sg))).astype(BF16)
        du_ref[...] = (da * g * sg).astype(BF16)

    aspec = pl.BlockSpec((None, tm, HID_S), lambda i, s: (s, i, 0))
    return pl.pallas_call(
        body, out_shape=(SDS((N_SHARD, S, HID_S), BF16),) * 2, grid=(S // tm, N_SHARD),
        in_specs=[pl.BlockSpec((tm, D_MODEL), lambda i, s: (i, 0)),
                  pl.BlockSpec((None, HID_S, D_MODEL), lambda i, s: (s, 0, 0)), aspec, aspec],
        out_specs=(aspec, aspec),
        compiler_params=_cparams("parallel", "arbitrary"), name="ffn_down_bwd")(dx2b, wd, gte, up)


def _wgrad(name, a, b, a_spec, b_spec, out_shape, out_spec, n_par, S):
    tk = 1024

    def body(a_ref, b_ref, o_ref):
        @pl.when(pl.program_id(1) == 0)
        def _():
            o_ref[...] = jnp.zeros_like(o_ref)

        o_ref[...] += _dot_tn(a_ref[...], b_ref[...])

    return pl.pallas_call(
        body, out_shape=SDS(out_shape, F32), grid=(n_par, S // tk),
        in_specs=[a_spec(tk), b_spec(tk)], out_specs=out_spec,
        compiler_params=_cparams("parallel", "arbitrary"), name=name)(a, b)


def _ffn_up_bwd(dgte, dup, wg, wu, x1, g2, dx2):
    S = x1.shape[0]
    tm = 1024

    def body(dg_ref, du_ref, wg_ref, wu_ref, x_ref, g_ref, dx2_ref, dx_ref, dxb_ref, dgn_ref, acc):
        i, s = pl.program_id(0), pl.program_id(1)

        @pl.when(s == 0)
        def _():
            acc[...] = jnp.zeros_like(acc)

        @pl.when((i == 0) & (s == 0))
        def _():
            dgn_ref[...] = jnp.zeros_like(dgn_ref)

        acc[...] += _dot_nt(dg_ref[...], wg_ref[...]) + _dot_nt(du_ref[...], wu_ref[...])

        @pl.when(s == N_SHARD - 1)
        def _():
            xv = x_ref[...]
            r = lax.rsqrt(jnp.mean(xv * xv, axis=-1, keepdims=True) + NORM_EPS)
            xh = xv * r
            dh = acc[...]
            dgn_ref[...] += jnp.sum(dh * xh, axis=0, keepdims=True)
            dxh = dh * g_ref[...]
            dx = dx2_ref[...] + r * (dxh - xh * jnp.mean(dxh * xh, axis=-1, keepdims=True))
            dx_ref[...] = dx
            dxb_ref[...] = dx.astype(BF16)

    row = pl.BlockSpec((tm, D_MODEL), lambda i, s: (i, 0))
    vec = pl.BlockSpec((1, D_MODEL), lambda i, s: (0, 0))
    aspec = pl.BlockSpec((None, tm, HID_S), lambda i, s: (s, i, 0))
    wspec = pl.BlockSpec((None, D_MODEL, HID_S), lambda i, s: (s, 0, 0))
    return pl.pallas_call(
        body, out_shape=(SDS((S, D_MODEL), F32), SDS((S, D_MODEL), BF16), SDS((1, D_MODEL), F32)),
        grid=(S // tm, N_SHARD),
        in_specs=[aspec, aspec, wspec, wspec, row, vec, row], out_specs=(row, row, vec),
        scratch_shapes=[pltpu.VMEM((tm, D_MODEL), F32)],
        compiler_params=_cparams("arbitrary", "arbitrary"), name="ffn_up_bwd")(dgte, dup, wg, wu, x1, g2, dx2)


def _out_proj_bwd(dx1b, wo, proj, ya, yr):
    S = dx1b.shape[0]
    tm = min(S, 2048)

    def body(d_ref, w_ref, ga_ref, gr_ref, ya_ref, yr_ref, dya_ref, dyr_ref, dga_ref, dgr_ref):
        dm = _dot_nt(d_ref[...], w_ref[...])
        sa = _sigmoid(ga_ref[...].astype(F32))
        sr = _sigmoid(gr_ref[...].astype(F32))
        dya_ref[...] = (dm * sa).astype(BF16)
        dyr_ref[...] = (dm * sr).astype(BF16)
        dga_ref[...] = (dm * ya_ref[...].astype(F32) * sa * (1.0 - sa)).astype(BF16)
        dgr_ref[...] = (dm * yr_ref[...].astype(F32) * sr * (1.0 - sr)).astype(BF16)

    blk = pl.BlockSpec((tm, 512), lambda i, j: (i, j))
    return pl.pallas_call(
        body, out_shape=(SDS((S, D_MODEL), BF16),) * 4, grid=(S // tm, 2),
        in_specs=[pl.BlockSpec((tm, D_MODEL), lambda i, j: (i, 0)), pl.BlockSpec((512, D_MODEL), lambda i, j: (j, 0)),
                  pl.BlockSpec((tm, 512), lambda i, j: (i, 15 + j)), pl.BlockSpec((tm, 512), lambda i, j: (i, 17 + j)),
                  blk, blk],
        out_specs=(blk,) * 4,
        compiler_params=_cparams("parallel", "arbitrary"), name="out_proj_bwd")(dx1b, wo, proj, proj, ya, yr)


def _branch_bwd(dya, dyr, wa, wr, att):
    S = dya.shape[0]
    tm = 1024

    def body(da_ref, dr_ref, wa_ref, wr_ref, att_ref, datt_ref, rho_ref, dyi_ref):
        datt = _dot_nt(da_ref[...], wa_ref[...])
        datt_ref[...] = datt.astype(BF16)
        dyi_ref[...] = _dot_nt(dr_ref[...], wr_ref[...]).astype(BF16)
        prod = datt * att_ref[...].astype(F32)
        lo = lax.broadcasted_iota(jnp.int32, (tm, 128), 1) < 64
        for c in range(4):
            pc = prod[:, c * 128:(c + 1) * 128]
            tot = jnp.sum(pc, axis=-1, keepdims=True)
            low = jnp.sum(jnp.where(lo, pc, 0.0), axis=-1, keepdims=True)
            rho_ref[:, (2 * c) * 128:(2 * c + 1) * 128] = jnp.broadcast_to(low, (tm, 128))
            rho_ref[:, (2 * c + 1) * 128:(2 * c + 2) * 128] = jnp.broadcast_to(tot - low, (tm, 128))

    row = lambda w: pl.BlockSpec((tm, w), lambda i: (i, 0))
    return pl.pallas_call(
        body, out_shape=(SDS((S, 512), BF16), SDS((S, 1024), F32), SDS((S, 1024), BF16)), grid=(S // tm,),
        in_specs=[row(1024), row(1024), pl.BlockSpec((512, 1024), lambda i: (0, 0)),
                  pl.BlockSpec((1024, 1024), lambda i: (0, 0)), row(512)],
        out_specs=(row(512), row(1024), row(1024)),
        compiler_params=_cparams("parallel"), name="branch_bwd")(dya, dyr, wa, wr, att)


def _attn_bwd(qkv, datt, lse, rho, rtab, d, gi):
    L = qkv.shape[0]
    nb = L // BLK

    def body(q_ref, kc_ref, kp_ref, vc_ref, vp_ref, do_ref, lse_ref, rho_ref, tq_ref, tk_ref,
             dq_ref, dk_ref, dv_ref, ck, cv):
        n = pl.program_id(1)

        @pl.when(n == 0)
        def _():
            ck[...] = jnp.zeros_like(ck)
            cv[...] = jnp.zeros_like(cv)

        def store_rot(ref, val, t_ref, c):
            sl = slice(c * 128, (c + 1) * 128)
            ref[:, sl] = _unrot(val, t_ref[0], t_ref[1], t_ref[2], 32).astype(BF16)

        @pl.when(n < nb)
        def _():
            mask = _band_mask(n)
            lo = lax.broadcasted_iota(jnp.int32, (BLK, 128), 1) < 64
            for c in range(4):
                sl = slice(c * 128, (c + 1) * 128)
                q, do = q_ref[:, sl], do_ref[:, sl]
                k = jnp.concatenate([kp_ref[:, sl], kc_ref[:, sl]], axis=0)
                v = jnp.concatenate([vp_ref[:, sl], vc_ref[:, sl]], axis=0)
                dq_c = jnp.zeros((BLK, 128), F32)
                dk_c = jnp.zeros((2 * BLK, 128), F32)
                dv_c = jnp.zeros((2 * BLK, 128), F32)
                for hh in range(2):
                    hm = lo if hh == 0 else jnp.logical_not(lo)
                    qm = jnp.where(hm, q, jnp.zeros_like(q))
                    dom = jnp.where(hm, do, jnp.zeros_like(do))
                    hs = slice((2 * c + hh) * 128, (2 * c + hh + 1) * 128)
                    lse_h = jnp.concatenate([lse_ref[:, hs]] * 2, axis=1)
                    rho_h = jnp.concatenate([rho_ref[:, hs]] * 2, axis=1)
                    s = _dot_nt(qm, k) * 0.125
                    p = jnp.where(mask, jnp.exp(s - lse_h), 0.0)
                    dp = _dot_nt(dom, v)
                    ds = (p * (dp - rho_h) * 0.125).astype(BF16)
                    dq_c = dq_c + jnp.where(hm, _dot(ds, k), 0.0)
                    dk_c = dk_c + _dot_tn(ds, qm)
                    dv_c = dv_c + _dot_tn(p.astype(BF16), dom)
                store_rot(dq_ref, dq_c, tq_ref, c)
                store_rot(dk_ref, ck[:, sl] + dk_c[:BLK], tk_ref, c)
                dv_ref[:, sl] = (cv[:, sl] + dv_c[:BLK]).astype(BF16)
                ck[:, sl] = dk_c[BLK:]
                cv[:, sl] = dv_c[BLK:]

        @pl.when(n == nb)
        def _():
            for c in range(4):
                sl = slice(c * 128, (c + 1) * 128)
                store_rot(dk_ref, ck[:, sl], tk_ref, c)
            dv_ref[...] = cv[...].astype(BF16)

    cur = lambda n: jnp.minimum(n, nb - 1)
    prev = lambda n: jnp.maximum(jnp.minimum(n, nb - 1) - 1, 0)
    fin = lambda n: jnp.maximum(n - 1, 0)
    return pl.pallas_call(
        body, out_shape=(SDS((L, d * 512), BF16),) * 3, grid=(d, nb + 1),
        in_specs=[pl.BlockSpec((BLK, 512), lambda r, n: (cur(n), 3 * r)),
                  pl.BlockSpec((BLK, 512), lambda r, n: (cur(n), 3 * r + 1)),
                  pl.BlockSpec((BLK, 512), lambda r, n: (prev(n), 3 * r + 1)),
                  pl.BlockSpec((BLK, 512), lambda r, n: (cur(n), 3 * r + 2)),
                  pl.BlockSpec((BLK, 512), lambda r, n: (prev(n), 3 * r + 2)),
                  pl.BlockSpec((BLK, 512), lambda r, n: (cur(n), r)),
                  pl.BlockSpec((BLK, 1024), lambda r, n: (cur(n), r)),
                  pl.BlockSpec((BLK, 1024), lambda r, n: (cur(n), r)),
                  pl.BlockSpec((3, BLK, 128), lambda r, n: (0, cur(n), r)),
                  pl.BlockSpec((3, BLK, 128), lambda r, n: (0, fin(n), r))],
        out_specs=(pl.BlockSpec((BLK, 512), lambda r, n: (cur(n), r)),
                   pl.BlockSpec((BLK, 512), lambda r, n: (fin(n), r)),
                   pl.BlockSpec((BLK, 512), lambda r, n: (fin(n), r))),
        scratch_shapes=[pltpu.VMEM((BLK, 512), F32), pltpu.VMEM((BLK, 512), F32)],
        compiler_params=_cparams("parallel", "arbitrary"), name=f"attn_bwd_g{gi}")(
            qkv, qkv, qkv, qkv, qkv, datt, lse, rho, rtab, rtab)


def _ret_bwd(proj, rn, rstd, dyrin, states, tab, consts):
    S = proj.shape[0]
    nc = S // BLK
    dmask, zeta, xi, dec = consts

    def body(q_ref, k_ref, v_ref, gr_ref, rn_ref, rs_ref, dy_ref, st_ref, tq_ref, tk_ref,
             dm_ref, z_ref, x_ref, dec_ref, dq_ref, dk_ref, dv_ref, dgr_ref, dR):
        @pl.when(pl.program_id(1) == 0)
        def _():
            dR[...] = jnp.zeros_like(dR)

        q, k, v = q_ref[...], k_ref[...], v_ref[...]
        gr = gr_ref[...].astype(F32)
        sg = _sigmoid(gr)
        rn_v = rn_ref[...].astype(F32)
        dyi = dy_ref[...].astype(F32)
        dgr_ref[...] = (dyi * rn_v * sg * (1.0 + gr * (1.0 - sg))).astype(BF16)
        drn = dyi * gr * sg
        rstd = jnp.concatenate([rs_ref[...]] * 2, axis=1)
        do = rstd * (drn - jnp.mean(drn, axis=-1, keepdims=True) - rn_v * jnp.mean(drn * rn_v, axis=-1, keepdims=True))
        dob = do.astype(BF16)
        Rb = st_ref[...]
        dRb = dR[...].astype(BF16)
        dm, zt, xt = dm_ref[...], z_ref[...], x_ref[...]
        sD = (_dot_nt(q, k) * dm).astype(BF16)
        kz = (k.astype(F32) * zt).astype(BF16)
        qx = (q.astype(F32) * xt).astype(BF16)
        dv_ref[...] = (_dot_tn(sD, dob) + _dot(kz, dRb)).astype(BF16)
        dS = (_dot_nt(dob, v) * dm).astype(BF16)
        dq = _dot(dS, k) + _dot_nt(dob, Rb) * xt
        dk = _dot_tn(dS, q) + _dot_nt(v, dRb) * zt
        dR[...] = dR[...] * dec_ref[0:1, :] + _dot_tn(qx, dob)
        dq_ref[...] = _unrot(dq, tq_ref[0], tq_ref[1], tq_ref[2], 1).astype(BF16)
        dk_ref[...] = _unrot(dk, tk_ref[0], tk_ref[1], tk_ref[2], 1).astype(BF16)

    rc = lambda c: nc - 1 - c
    hc = lambda h, c: (h, 0, 0)
    b128 = pl.BlockSpec((BLK, 128), lambda h, c: (rc(c), h))
    b256 = pl.BlockSpec((BLK, 256), lambda h, c: (rc(c), h))
    return pl.pallas_call(
        body, out_shape=(SDS((S, 512), BF16), SDS((S, 512), BF16), SDS((S, 1024), BF16), SDS((S, 1024), BF16)),
        grid=(RET_HEADS, nc),
        in_specs=[pl.BlockSpec((BLK, 128), lambda h, c: (rc(c), 36 + h)),
                  pl.BlockSpec((BLK, 128), lambda h, c: (rc(c), 40 + h)),
                  pl.BlockSpec((BLK, 256), lambda h, c: (rc(c), 22 + h)),
                  pl.BlockSpec((BLK, 256), lambda h, c: (rc(c), 26 + h)),
                  b256, b128, b256,
                  pl.BlockSpec((None, None, BLK, 256), lambda h, c: (h, rc(c), 0, 0)),
                  pl.BlockSpec((None, 3, BLK, 128), lambda h, c: (1, 0, rc(c), 0)),
                  pl.BlockSpec((None, 3, BLK, 128), lambda h, c: (2, 0, rc(c), 0)),
                  pl.BlockSpec((None, BLK, BLK), hc), pl.BlockSpec((None, BLK, 128), hc),
                  pl.BlockSpec((None, BLK, 128), hc), pl.BlockSpec((None, 8, 256), hc)],
        out_specs=(b128, b128, b256, b256),
        scratch_shapes=[pltpu.VMEM((BLK, 256), F32)],
        compiler_params=_cparams("parallel", "arbitrary"), name="ret_bwd")(
            proj, proj, proj, proj, rn, rstd, dyrin, states, tab, tab, dmask, zeta, xi, dec)


def _in_proj_bwd(dproj, w_in, x, g1, dx1):
    S = x.shape[0]
    tm = 512

    def body(d_ref, w_ref, x_ref, g_ref, dx1_ref, dx_ref, dgn_ref, acc):
        i, s = pl.program_id(0), pl.program_id(1)

        @pl.when(s == 0)
        def _():
            acc[...] = jnp.zeros_like(acc)

        @pl.when((i == 0) & (s == 0))
        def _():
            dgn_ref[...] = jnp.zeros_like(dgn_ref)

        acc[...] += _dot_nt(d_ref[...], w_ref[...])

        @pl.when(s == N_SHARD - 1)
        def _():
            xv = x_ref[...]
            r = lax.rsqrt(jnp.mean(xv * xv, axis=-1, keepdims=True) + NORM_EPS)
            xh = xv * r
            dh = acc[...]
            dgn_ref[...] += jnp.sum(dh * xh, axis=0, keepdims=True)
            dxh = dh * g_ref[...]
            dx_ref[...] = dx1_ref[...] + r * (dxh - xh * jnp.mean(dxh * xh, axis=-1, keepdims=True))

    row = pl.BlockSpec((tm, D_MODEL), lambda i, s: (i, 0))
    vec = pl.BlockSpec((1, D_MODEL), lambda i, s: (0, 0))
    return pl.pallas_call(
        body, out_shape=(SDS((S, D_MODEL), F32), SDS((1, D_MODEL), F32)), grid=(S // tm, N_SHARD),
        in_specs=[pl.BlockSpec((tm, W_IN_S), lambda i, s: (i, s)), pl.BlockSpec((D_MODEL, W_IN_S), lambda i, s: (0, s)),
                  row, vec, row],
        out_specs=(row, vec), scratch_shapes=[pltpu.VMEM((tm, D_MODEL), F32)],
        compiler_params=_cparams("arbitrary", "arbitrary"), name="in_proj_bwd")(dproj, w_in, x, g1, dx1)


def _sub_view(a, d):
    S, W = a.shape
    return a.reshape(S // d, d * W)


def _local_step(x, tgt, g1, g2, g3, w_in, wa, wr, wo, wg, wu, wd):
    S = x.shape[0]
    tab = _tables(S)
    consts = _ret_consts()

    h = _rms_fwd(x, g1)
    proj = _in_proj(h, w_in, tab)
    qkvs, o_parts, lse_parts = [], [], []
    for gi, d in enumerate(DILATIONS):
        cols = [proj[:, (b + gi) * COLB:(b + gi + 1) * COLB] for b in (QA_B, KA_B, VA_B)]
        qkv = _sub_view(jnp.concatenate(cols, axis=1), d)
        o_g, lse_g = _attn_fwd(qkv, d, gi)
        qkvs.append(qkv)
        o_parts.append(o_g.reshape(S, 512))
        lse_parts.append(lse_g.reshape(S, 1024))
    att, lse_tot = _attn_merge(o_parts, lse_parts)
    yrin, rn, rstd, states = _ret_fwd(proj, consts)
    merged, ya, yr = _branch_merge(att, yrin, proj, wa, wr)
    x1, h2 = _out_proj(merged, wo, x, g2)
    gte, up, act = _ffn_up(h2, wg, wu)
    dx2, dx2b, dg3, loss_p = _ffn_down_loss(act, wd, x1, g3, tgt)

    dgte, dup = _ffn_down_bwd(dx2b, wd, gte, up)
    tok3 = lambda w: (lambda tk: pl.BlockSpec((None, tk, w), lambda p, k: (p, k, 0)))
    tok2 = lambda w: (lambda tk: pl.BlockSpec((tk, w), lambda p, k: (k, 0)))
    g_d = _wgrad("wgrad_down", act, dx2b, tok3(HID_S), tok2(D_MODEL), (N_SHARD, HID_S, D_MODEL),
                 pl.BlockSpec((None, HID_S, D_MODEL), lambda p, k: (p, 0, 0)), N_SHARD, S)
    g_g = _wgrad("wgrad_gate", h2, dgte, tok2(D_MODEL), tok3(HID_S), (N_SHARD, D_MODEL, HID_S),
                 pl.BlockSpec((None, D_MODEL, HID_S), lambda p, k: (p, 0, 0)), N_SHARD, S)
    g_u = _wgrad("wgrad_up", h2, dup, tok2(D_MODEL), tok3(HID_S), (N_SHARD, D_MODEL, HID_S),
                 pl.BlockSpec((None, D_MODEL, HID_S), lambda p, k: (p, 0, 0)), N_SHARD, S)
    dx1, dx1b, dg2 = _ffn_up_bwd(dgte, dup, wg, wu, x1, g2, dx2)
    dya, dyr, dga, dgrr = _out_proj_bwd(dx1b, wo, proj, ya, yr)
    colblk = lambda w: (lambda tk: pl.BlockSpec((tk, w), lambda p, k: (k, p)))
    g_o = _wgrad("wgrad_out", merged, dx1b, colblk(256), tok2(D_MODEL), (D_MODEL, D_MODEL),
                 pl.BlockSpec((256, D_MODEL), lambda p, k: (p, 0)), 4, S)
    datt, rho, dyrin = _branch_bwd(dya, dyr, wa, wr, att)
    g_a = _wgrad("wgrad_attn", att, dya, tok2(512), colblk(512), (512, D_MODEL),
                 pl.BlockSpec((512, 512), lambda p, k: (0, p)), 2, S)
    g_r = _wgrad("wgrad_ret", yrin, dyr, colblk(256), tok2(D_MODEL), (D_MODEL, D_MODEL),
                 pl.BlockSpec((256, D_MODEL), lambda p, k: (p, 0)), 4, S)
    dqs, dks, dvs = [], [], []
    for gi, d in enumerate(DILATIONS):
        rtab = tab[0].reshape(3, S // d, d * 128)
        dq, dk, dv = _attn_bwd(qkvs[gi], _sub_view(datt, d), _sub_view(lse_tot, d), _sub_view(rho, d), rtab, d, gi)
        dqs.append(dq.reshape(S, 512))
        dks.append(dk.reshape(S, 512))
        dvs.append(dv.reshape(S, 512))
    dqr, dkr, dvr, dgr = _ret_bwd(proj, rn, rstd, dyrin, states, tab, consts)
    dproj = jnp.concatenate(dqs + dks + dvs + [dqr, dkr, dvr, dgr, dga, dgrr], axis=1)
    g_in = _wgrad("wgrad_in", h, dproj, tok2(D_MODEL), colblk(COLB), (D_MODEL, PROJ_W),
                  pl.BlockSpec((D_MODEL, COLB), lambda p, k: (0, p)), N_COLB, S)
    grad_x, dg1 = _in_proj_bwd(dproj, w_in, x, g1, dx1)
    return loss_p[0, 0], grad_x, (dg1, dg2, dg3), (g_in, g_a, g_r, g_o, g_g, g_u, g_d)


W_KINDS = ("col", "col", "lead", "lead", "lead", "lead", "lead")
W_SHARD = ((1024, W_IN_S), (512, 256), (256, 1024), (256, 1024), (1024, HID_S), (1024, HID_S), (HID_S, 1024))
N_W = len(W_KINDS)
ANY = pl.BlockSpec(memory_space=pl.ANY)


def _full_shape(wi):
    R, C = W_SHARD[wi]
    return (R, N_SHARD * C) if W_KINDS[wi] == "col" else (N_SHARD, R, C)


def _view(ref, wi, s, half):
    R, C = W_SHARD[wi]
    rows = pl.ds(half * (R // 2), R // 2)
    if W_KINDS[wi] == "col":
        return ref.at[rows, pl.ds(pl.multiple_of(s * C, 128), C)]
    return ref.at[s, rows, :]


def _mesh_pos():
    x, y, c = lax.axis_index("x"), lax.axis_index("y"), lax.axis_index("c")
    chips = [(1 - x, y), (x, 1 - y), (1 - x, 1 - y)]
    return x, y, c, chips


def _cast_bf16(a):
    R, C = a.shape
    tr = R // 2 if R % 32 == 0 else R

    def body(a_ref, o_ref):
        o_ref[...] = a_ref[...].astype(BF16)

    spec = pl.BlockSpec((tr, C), lambda i: (i, 0))
    return pl.pallas_call(body, out_shape=SDS((R, C), BF16), grid=(R // tr,), in_specs=[spec], out_specs=spec,
                          compiler_params=_cparams("parallel"), name=f"cast_{R}x{C}")(a)


def _gather_weights(shards):
    def body(*refs):
        sh, full = refs[:N_W], refs[N_W:2 * N_W]
        send, recv, loc = refs[2 * N_W:]
        x, y, c, chips = _mesh_pos()
        s_me = 2 * x + y
        sib = (x, y, 1 - c)

        def remote(k, src, dst, to):
            return pltpu.make_async_remote_copy(src_ref=src, dst_ref=dst, send_sem=send.at[k], recv_sem=recv.at[k],
                                                device_id=to, device_id_type=MESH)

        own = []
        for wi in range(N_W):
            R = W_SHARD[wi][0]
            for hf in range(2):
                cp = pltpu.make_async_copy(sh[wi].at[pl.ds(hf * (R // 2), R // 2), :], _view(full[wi], wi, s_me, hf),
                                           loc.at[2 * wi + hf])
                cp.start()
                own.append(cp)
        started = []
        for wi in range(N_W):
            R = W_SHARD[wi][0]
            for j, chip in enumerate(chips):
                cp = remote(3 * wi + j, sh[wi].at[pl.ds(c * (R // 2), R // 2), :], _view(full[wi], wi, s_me, c), (*chip, c))
                cp.start()
                started.append(cp)
        for wi in range(N_W):
            for j, chip in enumerate(chips):
                land = _view(full[wi], wi, 2 * chip[0] + chip[1], c)
                remote(3 * wi + j, land, land, (*chip, c)).wait_recv()
                fw = remote(3 * N_W + 3 * wi + j, land, land, sib)
                fw.start()
                started.append(fw)
        for wi in range(N_W):
            for j, chip in enumerate(chips):
                land = _view(full[wi], wi, 2 * chip[0] + chip[1], 1 - c)
                remote(3 * N_W + 3 * wi + j, land, land, sib).wait_recv()
        for cp in started:
            cp.wait_send()
        for cp in own:
            cp.wait()

    return pl.pallas_call(
        body, out_shape=tuple(SDS(_full_shape(wi), BF16) for wi in range(N_W)),
        in_specs=[ANY] * N_W, out_specs=tuple([ANY] * N_W),
        scratch_shapes=[pltpu.SemaphoreType.DMA((6 * N_W,)), pltpu.SemaphoreType.DMA((6 * N_W,)),
                        pltpu.SemaphoreType.DMA((2 * N_W,))],
        name="gather_weights")(*shards)


def _half_shape(wi):
    R, C = W_SHARD[wi]
    return (R // 2, N_SHARD * C) if W_KINDS[wi] == "col" else (N_SHARD, R // 2, C)


def _pair_exchange(grads):
    def body(*refs):
        g, ra = refs[:N_W], refs[N_W:2 * N_W]
        send, recv = refs[2 * N_W:]
        x, y, c, _ = _mesh_pos()
        cps = []
        for wi in range(N_W):
            Rh = W_SHARD[wi][0] // 2
            rows = pl.ds((1 - c) * Rh, Rh)
            src = g[wi].at[rows, :] if W_KINDS[wi] == "col" else g[wi].at[:, rows, :]
            cp = pltpu.make_async_remote_copy(src_ref=src, dst_ref=ra[wi], send_sem=send.at[wi], recv_sem=recv.at[wi],
                                              device_id=(x, y, 1 - c), device_id_type=MESH)
            cp.start()
            cps.append(cp)
        for cp in cps:
            cp.wait()

    return pl.pallas_call(
        body, out_shape=tuple(SDS(_half_shape(wi), F32) for wi in range(N_W)),
        in_specs=[ANY] * N_W, out_specs=tuple([ANY] * N_W),
        scratch_shapes=[pltpu.SemaphoreType.DMA((N_W,)), pltpu.SemaphoreType.DMA((N_W,))],
        name="grad_pair_exchange")(*grads)


def _row_tile(rh, C):
    best = 16
    for t in range(16, rh + 1, 16):
        if rh % t == 0 and t * C * 4 <= (3 << 19):
            best = t
    return best


def _pair_sum(wi, g, ra, sidx):
    R, C = W_SHARD[wi]
    Rh = R // 2
    tr = _row_tile(Rh, C)
    nt = Rh // tr
    col = W_KINDS[wi] == "col"

    def body(sidx_ref, *refs):
        gs, rs = refs[:4], refs[4:8]
        own_ref, pb_ref = refs[8:]
        own_ref[...] = gs[0][...] + rs[0][...]
        for j in range(3):
            pb_ref[j] = (gs[1 + j][...] + rs[1 + j][...]).astype(BF16)

    def gspec(slot):
        if col:
            return pl.BlockSpec((tr, C), lambda i, sx: (sx[4] * nt + i, sx[slot]))
        return pl.BlockSpec((None, tr, C), lambda i, sx: (sx[slot], sx[4] * nt + i, 0))

    def rspec(slot):
        if col:
            return pl.BlockSpec((tr, C), lambda i, sx: (i, sx[slot]))
        return pl.BlockSpec((None, tr, C), lambda i, sx: (sx[slot], i, 0))

    return pl.pallas_call(
        body, out_shape=(SDS((Rh, C), F32), SDS((3, Rh, C), BF16)),
        grid_spec=pltpu.PrefetchScalarGridSpec(
            num_scalar_prefetch=1, grid=(nt,),
            in_specs=[gspec(k) for k in range(4)] + [rspec(k) for k in range(4)],
            out_specs=(pl.BlockSpec((tr, C), lambda i, sx: (i, 0)), pl.BlockSpec((3, tr, C), lambda i, sx: (0, i, 0)))),
        compiler_params=_cparams("arbitrary"), name=f"pair_sum_w{wi}")(sidx, g, g, g, g, ra, ra, ra, ra)


def _chip_exchange(pbs):
    def body(*refs):
        pb, rb = refs[:N_W], refs[N_W:2 * N_W]
        send, recv = refs[2 * N_W:]
        x, y, c, chips = _mesh_pos()
        cps = []
        for wi in range(N_W):
            for j, chip in enumerate(chips):
                cp = pltpu.make_async_remote_copy(src_ref=pb[wi].at[j], dst_ref=rb[wi].at[j], send_sem=send.at[3 * wi + j],
                                                  recv_sem=recv.at[3 * wi + j], device_id=(*chip, c), device_id_type=MESH)
                cp.start()
                cps.append(cp)
        for cp in cps:
            cp.wait()

    return pl.pallas_call(
        body, out_shape=tuple(SDS((3, W_SHARD[wi][0] // 2, W_SHARD[wi][1]), BF16) for wi in range(N_W)),
        in_specs=[ANY] * N_W, out_specs=tuple([ANY] * N_W),
        scratch_shapes=[pltpu.SemaphoreType.DMA((3 * N_W,)), pltpu.SemaphoreType.DMA((3 * N_W,))],
        name="grad_chip_exchange")(*pbs)


def _chip_sum(wi, own, rb, sidx):
    R, C = W_SHARD[wi]
    Rh = R // 2
    tr = _row_tile(Rh, C)

    def body(sidx_ref, own_ref, rb_ref, o_ref):
        o_ref[...] = ((own_ref[...] + rb_ref[0].astype(F32)) + rb_ref[1].astype(F32)) + rb_ref[2].astype(F32)

    return pl.pallas_call(
        body, out_shape=SDS((2, Rh, C), F32),
        grid_spec=pltpu.PrefetchScalarGridSpec(
            num_scalar_prefetch=1, grid=(Rh // tr,),
            in_specs=[pl.BlockSpec((tr, C), lambda i, sx: (i, 0)), pl.BlockSpec((3, tr, C), lambda i, sx: (0, i, 0))],
            out_specs=pl.BlockSpec((None, tr, C), lambda i, sx: (sx[4], i, 0))),
        compiler_params=_cparams("arbitrary"), name=f"chip_sum_w{wi}")(sidx, own, rb)


def _share_halves(gfull):
    def body(*refs):
        src, dst = refs[:N_W], refs[N_W:2 * N_W]
        send, recv = refs[2 * N_W:]
        x, y, c, _ = _mesh_pos()
        cps = []
        for wi in range(N_W):
            cp = pltpu.make_async_remote_copy(src_ref=dst[wi].at[c], dst_ref=dst[wi].at[c], send_sem=send.at[wi],
                                              recv_sem=recv.at[wi], device_id=(x, y, 1 - c), device_id_type=MESH)
            cp.start()
            cps.append(cp)
        for wi, cp in enumerate(cps):
            cp.wait_send()
            pltpu.make_async_remote_copy(src_ref=dst[wi].at[1 - c], dst_ref=dst[wi].at[1 - c], send_sem=send.at[wi],
                                         recv_sem=recv.at[wi], device_id=(x, y, 1 - c), device_id_type=MESH).wait_recv()

    return pl.pallas_call(
        body, out_shape=tuple(SDS(g.shape, F32) for g in gfull),
        in_specs=[ANY] * N_W, out_specs=tuple([ANY] * N_W),
        input_output_aliases={wi: wi for wi in range(N_W)},
        scratch_shapes=[pltpu.SemaphoreType.DMA((N_W,)), pltpu.SemaphoreType.DMA((N_W,))],
        name="grad_share_halves")(*gfull)


def _gain_allgather(blk):
    m_per, n = blk.shape

    def body(x_ref, out_ref, send_sems, recv_sems, local_sem):
        x, y, c, chips = _mesh_pos()
        me, sibling = (x, y, c), (x, y, 1 - c)

        def rows(px, py, pc):
            return out_ref.at[pl.ds((4 * px + 2 * py + pc) * m_per, m_per), :]

        def copy(k, block, to, src=None):
            return pltpu.make_async_remote_copy(
                src_ref=rows(*block) if src is None else src, dst_ref=rows(*block),
                send_sem=send_sems.at[k], recv_sem=recv_sems.at[k], device_id=to, device_id_type=MESH)

        mine = pltpu.make_async_copy(x_ref, rows(*me), local_sem)
        mine.start()
        first = [copy(0, me, sibling, src=x_ref)]
        first += [copy(1 + j, me, (*chip, c), src=x_ref) for j, chip in enumerate(chips)]
        for cp in first:
            cp.start()
        passed = [copy(4 + j, (*chip, c), sibling) for j, chip in enumerate(chips)]
        for j, chip in enumerate(chips):
            copy(1 + j, (*chip, c), me).wait_recv()
            passed[j].start()
        copy(0, sibling, me).wait_recv()
        for j, chip in enumerate(chips):
            copy(4 + j, (*chip, 1 - c), me).wait_recv()
        for cp in first + passed:
            cp.wait_send()
        mine.wait()

    vm = pl.BlockSpec(memory_space=pltpu.VMEM)
    return pl.pallas_call(
        body, out_shape=SDS((8 * m_per, n), blk.dtype), in_specs=[vm], out_specs=vm,
        scratch_shapes=[pltpu.SemaphoreType.DMA((7,)), pltpu.SemaphoreType.DMA((7,)), pltpu.SemaphoreType.DMA],
        name="gain_allgather")(blk)


def _adam_math(w, g, m, v):
    mn = ADAM_B1 * m + (1.0 - ADAM_B1) * g
    vn = ADAM_B2 * v + (1.0 - ADAM_B2) * (g * g)
    mh = mn / (1.0 - ADAM_B1 ** ADAM_STEP)
    vh = vn / (1.0 - ADAM_B2 ** ADAM_STEP)
    return -ADAM_LR * (mh / (jnp.sqrt(vh) + ADAM_EPS) + ADAM_WD * w), mn, vn


def _adamw(wi, w, g, m, v):
    R, C = w.shape
    tr = _row_tile(R, C)

    def body(w_ref, g_ref, m_ref, v_ref, d_ref, mn_ref, vn_ref):
        d_ref[...], mn_ref[...], vn_ref[...] = _adam_math(w_ref[...], g_ref[...], m_ref[...], v_ref[...])

    spec = pl.BlockSpec((tr, C), lambda i: (i, 0))
    return pl.pallas_call(body, out_shape=(SDS((R, C), F32),) * 3, grid=(R // tr,), in_specs=[spec] * 4,
                          out_specs=(spec,) * 3, compiler_params=_cparams("parallel"), name=f"adamw_w{wi}")(w, g, m, v)


def _gain_update(gathered, w, m, v):
    def body(ga_ref, w_ref, m_ref, v_ref, g_ref, d_ref, mn_ref, vn_ref):
        g = ga_ref[0:8, :]
        for dev in range(1, 8):
            g = g + ga_ref[8 * dev:8 * dev + 8, :]
        g_ref[...] = g
        d_ref[...], mn_ref[...], vn_ref[...] = _adam_math(w_ref[...], g, m_ref[...], v_ref[...])

    return pl.pallas_call(body, out_shape=(SDS((8, 1024), F32),) * 4, name="gain_update")(gathered, w, m, v)


def kernel(x, norm_mix_g, w_in, w_out_attn, w_out_ret, w_out, norm_ffn_g, w_ffn_gate, w_ffn_up, w_ffn_down, norm_final_g, loss_target, m_norm_mix_g, m_w_in, m_w_out_attn, m_w_out_ret, m_w_out, m_norm_ffn_g, m_w_ffn_gate, m_w_ffn_up, m_w_ffn_down, m_norm_final_g, v_norm_mix_g, v_w_in, v_w_out_attn, v_w_out_ret, v_w_out, v_norm_ffn_g, v_w_ffn_gate, v_w_ffn_up, v_w_ffn_down, v_norm_final_g):
    ws = (w_in, w_out_attn, w_out_ret, w_out, w_ffn_gate, w_ffn_up, w_ffn_down)
    ms = (m_w_in, m_w_out_attn, m_w_out_ret, m_w_out, m_w_ffn_gate, m_w_ffn_up, m_w_ffn_down)
    vs = (v_w_in, v_w_out_attn, v_w_out_ret, v_w_out, v_w_ffn_gate, v_w_ffn_up, v_w_ffn_down)
    shard2d = lambda a, wi: a.reshape(W_SHARD[wi])

    full = _gather_weights([_cast_bf16(shard2d(w, wi)) for wi, w in enumerate(ws)])
    wf_in, wf_a, wf_r, wf_o, wf_g, wf_u, wf_d = full
    wf_r = wf_r.reshape(D_MODEL, D_MODEL)
    wf_o = wf_o.reshape(D_MODEL, D_MODEL)

    g3 = norm_final_g.reshape(1, D_MODEL)
    loss_p, grad_x, gain_g, wg = _local_step(x[0], loss_target[0], norm_mix_g, norm_ffn_g, g3,
                                             wf_in, wf_a, wf_r, wf_o, wf_g, wf_u, wf_d)
    g_in, g_a, g_r, g_o, g_g, g_u, g_d = wg
    grads = (g_in, g_a, g_r.reshape(N_SHARD, 256, D_MODEL), g_o.reshape(N_SHARD, 256, D_MODEL), g_g, g_u, g_d)

    xi, yi, ci = lax.axis_index("x"), lax.axis_index("y"), lax.axis_index("c")
    sidx = jnp.stack([2 * xi + yi, 2 * (1 - xi) + yi, 2 * xi + (1 - yi), 2 * (1 - xi) + (1 - yi), ci]).astype(jnp.int32)
    ras = _pair_exchange(grads)
    owns, pbs = zip(*[_pair_sum(wi, grads[wi], ras[wi], sidx) for wi in range(N_W)])
    rbs = _chip_exchange(pbs)
    halves = [_chip_sum(wi, owns[wi], rbs[wi], sidx) for wi in range(N_W)]
    gred = _share_halves(halves)

    outs_g, outs_d, outs_m, outs_v = [], [], [], []
    for wi in range(N_W):
        g2d = gred[wi].reshape(W_SHARD[wi])
        dlt, mn, vn = _adamw(wi, shard2d(ws[wi], wi), g2d, shard2d(ms[wi], wi), shard2d(vs[wi], wi))
        for lst, a in ((outs_g, g2d), (outs_d, dlt), (outs_m, mn), (outs_v, vn)):
            lst.append(a.reshape(ws[wi].shape))

    pad8 = lambda rows: jnp.concatenate([r.reshape(1, D_MODEL) for r in rows] + [jnp.zeros((5, D_MODEL), F32)], axis=0)
    gathered = _gain_allgather(pad8(gain_g))
    gg, gd, gm, gv = _gain_update(gathered, pad8((norm_mix_g, norm_ffn_g, norm_final_g)),
                                  pad8((m_norm_mix_g, m_norm_ffn_g, m_norm_final_g)),
                                  pad8((v_norm_mix_g, v_norm_ffn_g, v_norm_final_g)))
    loss = lax.psum(loss_p, ("x", "y", "c"))

    def assemble(gain_rows, wlist):
        return (gain_rows[0:1], wlist[0], wlist[1], wlist[2], wlist[3], gain_rows[1:2],
                wlist[4], wlist[5], wlist[6], gain_rows[2])

    return (loss, grad_x[None], *assemble(gg, outs_g), *assemble(gd, outs_d), *assemble(gm, outs_m), *assemble(gv, outs_v))
```

```python
import functools
import math

import numpy as np
import jax
import jax.numpy as jnp
from jax import lax
from jax.experimental import pallas as pl
from jax.experimental.pallas import tpu as pltpu

F32, BF16 = jnp.float32, jnp.bfloat16
SDS = jax.ShapeDtypeStruct
MESH = pl.DeviceIdType.MESH

D_MODEL = 1024
PROJ_W = 9728
COLB = 512
N_COLB = PROJ_W // COLB
QA_B, KA_B, VA_B = 0, 3, 6
QR_B, KR_B = 9, 10
FFN_HID = 2816
N_SHARD = 4
HID_S = FFN_HID // N_SHARD
W_IN_S = PROJ_W // N_SHARD
DILATIONS = (1, 4, 16)
BLK = 128
RET_HEADS = 4
ROPE_THETA = 10000.0
NORM_EPS = 1e-6
ADAM_LR, ADAM_B1, ADAM_B2, ADAM_EPS, ADAM_WD, ADAM_STEP = 0.001, 0.9, 0.999, 1e-08, 0.01, 10
VMEM_LIMIT = 56 << 20


def _cparams(*sem):
    return pltpu.CompilerParams(dimension_semantics=sem or None, vmem_limit_bytes=VMEM_LIMIT)


def _dot(a, b):
    return jnp.dot(a, b, preferred_element_type=F32)


def _dot_nt(a, b):
    return lax.dot_general(a, b, (((1,), (1,)), ((), ())), preferred_element_type=F32)


def _dot_tn(a, b):
    return lax.dot_general(a, b, (((0,), (0,)), ((), ())), preferred_element_type=F32)


def _sigmoid(z):
    return 1.0 / (1.0 + jnp.exp(-z))


ANY = pl.BlockSpec(memory_space=pl.ANY)


class _Exchange:
    def __init__(self, ins, out_shapes, aliases, n_sem, n_loc, build):
        self.ins, self.out_shapes, self.aliases = list(ins), list(out_shapes), dict(aliases)
        self.n_sem, self.n_loc, self.build = n_sem, n_loc, build

    def sems(self):
        return [pltpu.SemaphoreType.DMA((self.n_sem,)), pltpu.SemaphoreType.DMA((self.n_sem,)),
                pltpu.SemaphoreType.DMA((max(self.n_loc, 1),))]


def _exchange_call(ex, name):
    n_in, n_out = len(ex.ins), len(ex.out_shapes)

    def body(*refs):
        starts, waits = ex.build(refs[:n_in], refs[n_in:n_in + n_out], *refs[n_in + n_out:])
        for cp in starts:
            cp.start()
        for w in waits:
            w()

    return pl.pallas_call(body, out_shape=tuple(ex.out_shapes), in_specs=[ANY] * n_in, out_specs=tuple([ANY] * n_out),
                          input_output_aliases=ex.aliases, scratch_shapes=ex.sems(), name=name)(*ex.ins)


def _carrier_call(body, args, *, out_shape, grid, in_specs, out_specs, scratch_shapes=(), sem, name, exchanges=(),
                  in_out_aliases=None):
    out_shape, out_specs = tuple(out_shape), tuple(out_specs)
    n_in, n_out, n_scr = len(args), len(out_shape), len(scratch_shapes)
    x_args, x_outs, aliases, x_scr, spans = [], [], dict(in_out_aliases or {}), [], []
    for ex in exchanges:
        i0, o0 = len(x_args), len(x_outs)
        for a, o in ex.aliases.items():
            aliases[n_in + i0 + a] = n_out + o0 + o
        x_args += ex.ins
        x_outs += ex.out_shapes
        x_scr += ex.sems()
        spans.append((i0, len(ex.ins), o0, len(ex.out_shapes)))
    nx_in, nx_out = len(x_args), len(x_outs)

    def wrapped(*refs):
        ins, xin = refs[:n_in], refs[n_in:n_in + nx_in]
        o_base = n_in + nx_in
        outs, xout = refs[o_base:o_base + n_out], refs[o_base + n_out:o_base + n_out + nx_out]
        s_base = o_base + n_out + nx_out
        scr, xs = refs[s_base:s_base + n_scr], refs[s_base + n_scr:]

        def built(e):
            i0, ni, o0, no = spans[e]
            return exchanges[e].build(xin[i0:i0 + ni], xout[o0:o0 + no], *xs[3 * e:3 * e + 3])

        if exchanges:
            first = functools.reduce(jnp.logical_and, [pl.program_id(k) == 0 for k in range(len(grid))])
            last = functools.reduce(jnp.logical_and, [pl.program_id(k) == grid[k] - 1 for k in range(len(grid))])

            @pl.when(first)
            def _():
                for e in range(len(exchanges)):
                    for cp in built(e)[0]:
                        cp.start()

        body(*ins, *outs, *scr)

        if exchanges:
            @pl.when(last)
            def _():
                for e in range(len(exchanges)):
                    for w in built(e)[1]:
                        w()

    res = pl.pallas_call(
        wrapped, out_shape=out_shape + tuple(x_outs), grid=grid,
        in_specs=list(in_specs) + [ANY] * nx_in, out_specs=out_specs + tuple([ANY] * nx_out),
        scratch_shapes=list(scratch_shapes) + x_scr, input_output_aliases=aliases,
        compiler_params=_cparams(*(sem if not exchanges else ("arbitrary",) * len(grid))), name=name)(*args, *x_args)
    xres = [tuple(res[n_out + o0:n_out + o0 + no]) for (_, _, o0, no) in spans]
    return tuple(res[:n_out]), xres


def _tables(S):
    pos = jnp.arange(S, dtype=F32)
    lane = np.arange(128)
    inv = ROPE_THETA ** (-jnp.arange(0, 64, 2, dtype=F32) / 64)
    ang = pos[:, None] * inv[None, :]
    idx = (lane % 64) % 32
    c, s = jnp.cos(ang)[:, idx], jnp.sin(ang)[:, idx]
    first = jnp.asarray((lane % 64) < 32)[None, :]
    rope = jnp.stack([c, jnp.where(first, 0.0, s), jnp.where(first, -s, 0.0)])
    base = 1.0 / (ROPE_THETA ** jnp.linspace(0.0, 1.0, 64, dtype=F32))
    ang2 = pos[:, None] * base[None, :]
    c2, s2 = jnp.cos(ang2)[:, lane // 2], jnp.sin(ang2)[:, lane // 2]
    even = jnp.asarray(lane % 2 == 0)[None, :]
    th = jnp.stack([c2, jnp.where(even, 0.0, s2), jnp.where(even, -s2, 0.0)])
    return jnp.stack([rope, th, th * (128 ** -0.5)]).astype(F32)


def _rot(a, c, sa, sb, shift):
    return a * c + pltpu.roll(a, shift, 1) * sa + pltpu.roll(a, 128 - shift, 1) * sb


def _unrot(g, c, sa, sb, shift):
    return g * c + pltpu.roll(g * sa, 128 - shift, 1) + pltpu.roll(g * sb, shift, 1)


def _ret_consts():
    h = np.arange(RET_HEADS, dtype=np.float64)
    log_g = np.log1p(-(2.0 ** (-5.0 - h)))
    idx = np.arange(BLK, dtype=np.float64)
    diff = idx[:, None] - idx[None, :]
    dmask = np.where(diff[None] >= 0, np.exp(np.maximum(diff, 0.0)[None] * log_g[:, None, None]), 0.0)
    zeta = np.exp((BLK - 1 - idx)[None, :] * log_g[:, None])
    xi = np.exp((idx + 1.0)[None, :] * log_g[:, None])
    dec = np.exp(BLK * log_g)
    rep = lambda v: np.broadcast_to(v[:, :, None], (RET_HEADS, BLK, 128))
    return (jnp.asarray(dmask, F32), jnp.asarray(rep(zeta), F32), jnp.asarray(rep(xi), F32),
            jnp.asarray(np.broadcast_to(dec[:, None, None], (RET_HEADS, 8, 256)), F32))


def _rms_fwd(x, g):
    S = x.shape[0]
    tm = 512

    def body(x_ref, g_ref, h_ref):
        xv = x_ref[...]
        r = lax.rsqrt(jnp.mean(xv * xv, axis=-1, keepdims=True) + NORM_EPS)
        h_ref[...] = (xv * r * g_ref[...]).astype(BF16)

    return pl.pallas_call(
        body, out_shape=SDS((S, D_MODEL), BF16), grid=(S // tm,),
        in_specs=[pl.BlockSpec((tm, D_MODEL), lambda i: (i, 0)), pl.BlockSpec((1, D_MODEL), lambda i: (0, 0))],
        out_specs=pl.BlockSpec((tm, D_MODEL), lambda i: (i, 0)),
        compiler_params=_cparams("parallel"), name="rms_fwd")(x, g)


def _in_proj(h, w_in, tab, exchanges=()):
    S = h.shape[0]
    tm = min(S, 2048)

    def body(h_ref, w_ref, t_ref, o_ref):
        j = pl.program_id(1)
        acc = _dot(h_ref[...], w_ref[...])
        is_rope = j < 6
        is_theta = (j == QR_B) | (j == KR_B)

        def rotated(shift):
            c, sa, sb = t_ref[0, 0], t_ref[0, 1], t_ref[0, 2]
            for k in range(COLB // 128):
                sl = slice(k * 128, (k + 1) * 128)
                o_ref[:, sl] = _rot(acc[:, sl], c, sa, sb, shift).astype(BF16)

        @pl.when(is_rope)
        def _():
            rotated(32)

        @pl.when(is_theta)
        def _():
            rotated(1)

        @pl.when(jnp.logical_not(is_rope | is_theta))
        def _():
            o_ref[...] = acc.astype(BF16)

    def tab_map(i, j):
        return (jnp.where(j == QR_B, 1, jnp.where(j == KR_B, 2, 0)), 0, i, 0)

    (proj,), xres = _carrier_call(
        body, (h, w_in, tab), out_shape=(SDS((S, PROJ_W), BF16),), grid=(S // tm, N_COLB),
        in_specs=[pl.BlockSpec((tm, D_MODEL), lambda i, j: (i, 0)),
                  pl.BlockSpec((D_MODEL, COLB), lambda i, j: (0, j)),
                  pl.BlockSpec((1, 3, tm, 128), tab_map)],
        out_specs=(pl.BlockSpec((tm, COLB), lambda i, j: (i, j)),),
        sem=("parallel", "arbitrary"), name="in_proj", exchanges=exchanges)
    return proj, xres


def _band_mask(n):
    qi = lax.broadcasted_iota(jnp.int32, (BLK, 2 * BLK), 0)
    kj = lax.broadcasted_iota(jnp.int32, (BLK, 2 * BLK), 1)
    dist = BLK + qi - kj
    return (dist >= 0) & (dist <= BLK) & ((kj >= BLK) | (n > 0))


def _attn_fwd(qkv, d, gi, exchanges=()):
    L = qkv.shape[0]
    nb = L // BLK

    def body(q_ref, kc_ref, kp_ref, vc_ref, vp_ref, o_ref, lse_ref):
        n = pl.program_id(1)
        mask = _band_mask(n)
        lo = lax.broadcasted_iota(jnp.int32, (BLK, 128), 1) < 64
        for c in range(4):
            sl = slice(c * 128, (c + 1) * 128)
            q = q_ref[:, sl]
            k = jnp.concatenate([kp_ref[:, sl], kc_ref[:, sl]], axis=0)
            v = jnp.concatenate([vp_ref[:, sl], vc_ref[:, sl]], axis=0)
            o_c = None
            for hh in range(2):
                hm = lo if hh == 0 else jnp.logical_not(lo)
                qm = jnp.where(hm, q, jnp.zeros_like(q))
                s = _dot_nt(qm, k) * 0.125
                s = jnp.where(mask, s, jnp.float32(-1e30))
                m = jnp.max(s, axis=-1, keepdims=True)
                p = jnp.exp(s - m)
                l = jnp.sum(p, axis=-1, keepdims=True)
                o_h = _dot((p / l).astype(BF16), v)
                o_c = o_h if hh == 0 else jnp.where(lo, o_c, o_h)
                hs = slice((2 * c + hh) * 128, (2 * c + hh + 1) * 128)
                lse_ref[:, hs] = jnp.broadcast_to(m + jnp.log(l), (BLK, 128))
            o_ref[:, sl] = o_c

    prev = lambda n: jnp.maximum(n - 1, 0)
    return _carrier_call(
        body, (qkv,) * 5, out_shape=(SDS((L, d * 512), F32), SDS((L, d * 1024), F32)), grid=(d, nb),
        in_specs=[pl.BlockSpec((BLK, 512), lambda r, n: (n, 3 * r)),
                  pl.BlockSpec((BLK, 512), lambda r, n: (n, 3 * r + 1)),
                  pl.BlockSpec((BLK, 512), lambda r, n: (prev(n), 3 * r + 1)),
                  pl.BlockSpec((BLK, 512), lambda r, n: (n, 3 * r + 2)),
                  pl.BlockSpec((BLK, 512), lambda r, n: (prev(n), 3 * r + 2))],
        out_specs=(pl.BlockSpec((BLK, 512), lambda r, n: (n, r)),
                   pl.BlockSpec((BLK, 1024), lambda r, n: (n, r))),
        sem=("parallel", "arbitrary"), name=f"attn_fwd_g{gi}", exchanges=exchanges)


def _attn_merge(os_, lses):
    S = os_[0].shape[0]
    tm = 512

    def body(o0, o1, o2, l0, l1, l2, att_ref, lt_ref):
        lo = lax.broadcasted_iota(jnp.int32, (tm, 128), 1) < 64
        o_refs, l_refs = (o0, o1, o2), (l0, l1, l2)
        for c in range(4):
            sl = slice(c * 128, (c + 1) * 128)
            ws = []
            for hh in range(2):
                hs = slice((2 * c + hh) * 128, (2 * c + hh + 1) * 128)
                ls = [r[:, hs] for r in l_refs]
                m = jnp.maximum(jnp.maximum(ls[0], ls[1]), ls[2])
                es = [jnp.exp(v - m) for v in ls]
                z = es[0] + es[1] + es[2]
                lt_ref[:, hs] = m + jnp.log(z)
                ws.append([e / z for e in es])
            acc = jnp.zeros((tm, 128), F32)
            for g in range(3):
                acc = acc + jnp.where(lo, ws[0][g], ws[1][g]) * o_refs[g][:, sl]
            att_ref[:, sl] = acc.astype(BF16)

    ospec = pl.BlockSpec((tm, 512), lambda i: (i, 0))
    lspec = pl.BlockSpec((tm, 1024), lambda i: (i, 0))
    return pl.pallas_call(
        body, out_shape=(SDS((S, 512), BF16), SDS((S, 1024), F32)), grid=(S // tm,),
        in_specs=[ospec] * 3 + [lspec] * 3, out_specs=(ospec, lspec),
        compiler_params=_cparams("parallel"), name="attn_merge")(*os_, *lses)


def _ret_fwd(proj, consts):
    S = proj.shape[0]
    nc = S // BLK
    dmask, zeta, xi, dec = consts

    def body(q_ref, k_ref, v_ref, gr_ref, dm_ref, z_ref, x_ref, dec_ref, y_ref, rn_ref, rs_ref, st_ref, R):
        @pl.when(pl.program_id(1) == 0)
        def _():
            R[...] = jnp.zeros_like(R)

        q, k, v = q_ref[...], k_ref[...], v_ref[...]
        Rb = R[...].astype(BF16)
        st_ref[...] = Rb
        s = _dot_nt(q, k) * dm_ref[...]
        o = _dot(s.astype(BF16), v) + _dot((q.astype(F32) * x_ref[...]).astype(BF16), Rb)
        kz = (k.astype(F32) * z_ref[...]).astype(BF16)
        R[...] = R[...] * dec_ref[0:1, :] + _dot_tn(kz, v)
        mu = jnp.mean(o, axis=-1, keepdims=True)
        oc = o - mu
        rstd = lax.rsqrt(jnp.mean(oc * oc, axis=-1, keepdims=True) + NORM_EPS)
        rn = oc * rstd
        gr = gr_ref[...].astype(F32)
        y_ref[...] = (rn * gr * _sigmoid(gr)).astype(BF16)
        rn_ref[...] = rn.astype(BF16)
        rs_ref[...] = jnp.broadcast_to(rstd, (BLK, 128))

    hc = lambda h, c: (h, 0, 0)
    return pl.pallas_call(
        body,
        out_shape=(SDS((S, 1024), BF16), SDS((S, 1024), BF16), SDS((S, 512), F32), SDS((RET_HEADS, nc, BLK, 256), BF16)),
        grid=(RET_HEADS, nc),
        in_specs=[pl.BlockSpec((BLK, 128), lambda h, c: (c, 36 + h)),
                  pl.BlockSpec((BLK, 128), lambda h, c: (c, 40 + h)),
                  pl.BlockSpec((BLK, 256), lambda h, c: (c, 22 + h)),
                  pl.BlockSpec((BLK, 256), lambda h, c: (c, 26 + h)),
                  pl.BlockSpec((None, BLK, BLK), hc), pl.BlockSpec((None, BLK, 128), hc),
                  pl.BlockSpec((None, BLK, 128), hc), pl.BlockSpec((None, 8, 256), hc)],
        out_specs=(pl.BlockSpec((BLK, 256), lambda h, c: (c, h)), pl.BlockSpec((BLK, 256), lambda h, c: (c, h)),
                   pl.BlockSpec((BLK, 128), lambda h, c: (c, h)),
                   pl.BlockSpec((None, None, BLK, 256), lambda h, c: (h, c, 0, 0))),
        scratch_shapes=[pltpu.VMEM((BLK, 256), F32)],
        compiler_params=_cparams("parallel", "arbitrary"), name="ret_fwd")(proj, proj, proj, proj, dmask, zeta, xi, dec)


def _branch_merge(att, yrin, proj, wa, wr):
    S = att.shape[0]
    tm = min(S, 2048)

    def body(a_ref, y_ref, ga_ref, gr_ref, wa_ref, wr_ref, m_ref, ya_ref, yr_ref):
        ya = _dot(a_ref[...], wa_ref[...])
        yr = _dot(y_ref[...], wr_ref[...])
        m_ref[...] = (_sigmoid(ga_ref[...].astype(F32)) * ya + _sigmoid(gr_ref[...].astype(F32)) * yr).astype(BF16)
        ya_ref[...] = ya.astype(BF16)
        yr_ref[...] = yr.astype(BF16)

    ospec = pl.BlockSpec((tm, 512), lambda i, j: (i, j))
    return pl.pallas_call(
        body, out_shape=(SDS((S, D_MODEL), BF16),) * 3, grid=(S // tm, 2),
        in_specs=[pl.BlockSpec((tm, 512), lambda i, j: (i, 0)), pl.BlockSpec((tm, 1024), lambda i, j: (i, 0)),
                  pl.BlockSpec((tm, 512), lambda i, j: (i, 15 + j)), pl.BlockSpec((tm, 512), lambda i, j: (i, 17 + j)),
                  pl.BlockSpec((512, 512), lambda i, j: (0, j)), pl.BlockSpec((1024, 512), lambda i, j: (0, j))],
        out_specs=(ospec, ospec, ospec),
        compiler_params=_cparams("parallel", "arbitrary"), name="branch_merge")(att, yrin, proj, proj, wa, wr)


def _out_proj(merged, wo, x, g2):
    S = x.shape[0]
    tm = 1024

    def body(m_ref, w_ref, x_ref, g_ref, x1_ref, h2_ref):
        x1 = x_ref[...] + _dot(m_ref[...], w_ref[...])
        x1_ref[...] = x1
        r = lax.rsqrt(jnp.mean(x1 * x1, axis=-1, keepdims=True) + NORM_EPS)
        h2_ref[...] = (x1 * r * g_ref[...]).astype(BF16)

    row = pl.BlockSpec((tm, D_MODEL), lambda i: (i, 0))
    return pl.pallas_call(
        body, out_shape=(SDS((S, D_MODEL), F32), SDS((S, D_MODEL), BF16)), grid=(S // tm,),
        in_specs=[row, pl.BlockSpec((D_MODEL, D_MODEL), lambda i: (0, 0)), row, pl.BlockSpec((1, D_MODEL), lambda i: (0, 0))],
        out_specs=(row, row), compiler_params=_cparams("parallel"), name="out_proj")(merged, wo, x, g2)


def _ffn_up(h2, wg, wu):
    S = h2.shape[0]
    tm = min(S, 2048)

    def body(h_ref, wg_ref, wu_ref, g_ref, u_ref, a_ref):
        hv = h_ref[...]
        g = _dot(hv, wg_ref[...])
        u = _dot(hv, wu_ref[...])
        g_ref[...] = g.astype(BF16)
        u_ref[...] = u.astype(BF16)
        a_ref[...] = (g * _sigmoid(g) * u).astype(BF16)

    wspec = pl.BlockSpec((None, D_MODEL, HID_S), lambda i, s: (s, 0, 0))
    ospec = pl.BlockSpec((None, tm, HID_S), lambda i, s: (s, i, 0))
    return pl.pallas_call(
        body, out_shape=(SDS((N_SHARD, S, HID_S), BF16),) * 3, grid=(S // tm, N_SHARD),
        in_specs=[pl.BlockSpec((tm, D_MODEL), lambda i, s: (i, 0)), wspec, wspec],
        out_specs=(ospec, ospec, ospec),
        compiler_params=_cparams("parallel", "arbitrary"), name="ffn_up")(h2, wg, wu)


def _ffn_down_loss(act, wd, x1, g3, tgt):
    S = x1.shape[0]
    tm = 1024

    def body(a_ref, w_ref, x_ref, g_ref, t_ref, dx_ref, dxb_ref, dg_ref, ls_ref, acc):
        i, s = pl.program_id(0), pl.program_id(1)

        @pl.when(s == 0)
        def _():
            acc[...] = jnp.zeros_like(acc)

        @pl.when((i == 0) & (s == 0))
        def _():
            dg_ref[...] = jnp.zeros_like(dg_ref)
            ls_ref[...] = jnp.zeros_like(ls_ref)

        acc[...] += _dot(a_ref[...], w_ref[...])

        @pl.when(s == N_SHARD - 1)
        def _():
            x2 = x_ref[...] + acc[...]
            r = lax.rsqrt(jnp.mean(x2 * x2, axis=-1, keepdims=True) + NORM_EPS)
            xh = x2 * r
            g = g_ref[...]
            err = xh * g - t_ref[...]
            ls_ref[...] += jnp.sum(jnp.sum(err * err, axis=-1, keepdims=True), axis=0, keepdims=True) * (0.5 / D_MODEL)
            dy = err * (1.0 / D_MODEL)
            dg_ref[...] += jnp.sum(dy * xh, axis=0, keepdims=True)
            dxh = dy * g
            dx = r * (dxh - xh * jnp.mean(dxh * xh, axis=-1, keepdims=True))
            dx_ref[...] = dx
            dxb_ref[...] = dx.astype(BF16)

    row = pl.BlockSpec((tm, D_MODEL), lambda i, s: (i, 0))
    vec = pl.BlockSpec((1, D_MODEL), lambda i, s: (0, 0))
    return pl.pallas_call(
        body, out_shape=(SDS((S, D_MODEL), F32), SDS((S, D_MODEL), BF16), SDS((1, D_MODEL), F32), SDS((8, 128), F32)),
        grid=(S // tm, N_SHARD),
        in_specs=[pl.BlockSpec((None, tm, HID_S), lambda i, s: (s, i, 0)),
                  pl.BlockSpec((None, HID_S, D_MODEL), lambda i, s: (s, 0, 0)), row, vec, row],
        out_specs=(row, row, vec, pl.BlockSpec((8, 128), lambda i, s: (0, 0))),
        scratch_shapes=[pltpu.VMEM((tm, D_MODEL), F32)],
        compiler_params=_cparams("arbitrary", "arbitrary"), name="ffn_down_loss")(act, wd, x1, g3, tgt)


def _ffn_down_bwd(dx2b, wd, gte, up):
    S = dx2b.shape[0]
    tm = min(S, 2048)

    def body(d_ref, w_ref, g_ref, u_ref, dg_ref, du_ref):
        da = _dot_nt(d_ref[...], w_ref[...])
        g = g_ref[...].astype(F32)
        sg = _sigmoid(g)
        dg_ref[...] = (da * u_ref[...].astype(F32) * sg * (1.0 + g * (1.0 - sg))).astype(BF16)
        du_ref[...] = (da * g * sg).astype(BF16)

    aspec = pl.BlockSpec((None, tm, HID_S), lambda i, s: (s, i, 0))
    return pl.pallas_call(
        body, out_shape=(SDS((N_SHARD, S, HID_S), BF16),) * 2, grid=(S // tm, N_SHARD),
        in_specs=[pl.BlockSpec((tm, D_MODEL), lambda i, s: (i, 0)),
                  pl.BlockSpec((None, HID_S, D_MODEL), lambda i, s: (s, 0, 0)), aspec, aspec],
        out_specs=(aspec, aspec),
        compiler_params=_cparams("parallel", "arbitrary"), name="ffn_down_bwd")(dx2b, wd, gte, up)


def _wgrad(name, a, b, a_spec, b_spec, out_shape, out_spec, n_par, S):
    tk = 1024

    def body(a_ref, b_ref, o_ref):
        @pl.when(pl.program_id(1) == 0)
        def _():
            o_ref[...] = jnp.zeros_like(o_ref)

        o_ref[...] += _dot_tn(a_ref[...], b_ref[...])

    return pl.pallas_call(
        body, out_shape=SDS(out_shape, F32), grid=(n_par, S // tk),
        in_specs=[a_spec(tk), b_spec(tk)], out_specs=out_spec,
        compiler_params=_cparams("parallel", "arbitrary"), name=name)(a, b)


def _ffn_up_bwd(dgte, dup, wg, wu, x1, g2, dx2, exchanges=()):
    S = x1.shape[0]
    tm = 1024

    def body(dg_ref, du_ref, wg_ref, wu_ref, x_ref, g_ref, dx2_ref, dx_ref, dxb_ref, dgn_ref, acc):
        i, s = pl.program_id(0), pl.program_id(1)

        @pl.when(s == 0)
        def _():
            acc[...] = jnp.zeros_like(acc)

        @pl.when((i == 0) & (s == 0))
        def _():
            dgn_ref[...] = jnp.zeros_like(dgn_ref)

        acc[...] += _dot_nt(dg_ref[...], wg_ref[...]) + _dot_nt(du_ref[...], wu_ref[...])

        @pl.when(s == N_SHARD - 1)
        def _():
            xv = x_ref[...]
            r = lax.rsqrt(jnp.mean(xv * xv, axis=-1, keepdims=True) + NORM_EPS)
            xh = xv * r
            dh = acc[...]
            dgn_ref[...] += jnp.sum(dh * xh, axis=0, keepdims=True)
            dxh = dh * g_ref[...]
            dx = dx2_ref[...] + r * (dxh - xh * jnp.mean(dxh * xh, axis=-1, keepdims=True))
            dx_ref[...] = dx
            dxb_ref[...] = dx.astype(BF16)

    row = pl.BlockSpec((tm, D_MODEL), lambda i, s: (i, 0))
    vec = pl.BlockSpec((1, D_MODEL), lambda i, s: (0, 0))
    aspec = pl.BlockSpec((None, tm, HID_S), lambda i, s: (s, i, 0))
    wspec = pl.BlockSpec((None, D_MODEL, HID_S), lambda i, s: (s, 0, 0))
    return _carrier_call(
        body, (dgte, dup, wg, wu, x1, g2, dx2),
        out_shape=(SDS((S, D_MODEL), F32), SDS((S, D_MODEL), BF16), SDS((1, D_MODEL), F32)),
        grid=(S // tm, N_SHARD),
        in_specs=[aspec, aspec, wspec, wspec, row, vec, row], out_specs=(row, row, vec),
        scratch_shapes=[pltpu.VMEM((tm, D_MODEL), F32)],
        sem=("arbitrary", "arbitrary"), name="ffn_up_bwd", exchanges=exchanges)


def _out_proj_bwd(dx1b, wo, proj, ya, yr):
    S = dx1b.shape[0]
    tm = min(S, 2048)

    def body(d_ref, w_ref, ga_ref, gr_ref, ya_ref, yr_ref, dya_ref, dyr_ref, dga_ref, dgr_ref):
        dm = _dot_nt(d_ref[...], w_ref[...])
        sa = _sigmoid(ga_ref[...].astype(F32))
        sr = _sigmoid(gr_ref[...].astype(F32))
        dya_ref[...] = (dm * sa).astype(BF16)
        dyr_ref[...] = (dm * sr).astype(BF16)
        dga_ref[...] = (dm * ya_ref[...].astype(F32) * sa * (1.0 - sa)).astype(BF16)
        dgr_ref[...] = (dm * yr_ref[...].astype(F32) * sr * (1.0 - sr)).astype(BF16)

    blk = pl.BlockSpec((tm, 512), lambda i, j: (i, j))
    return pl.pallas_call(
        body, out_shape=(SDS((S, D_MODEL), BF16),) * 4, grid=(S // tm, 2),
        in_specs=[pl.BlockSpec((tm, D_MODEL), lambda i, j: (i, 0)), pl.BlockSpec((512, D_MODEL), lambda i, j: (j, 0)),
                  pl.BlockSpec((tm, 512), lambda i, j: (i, 15 + j)), pl.BlockSpec((tm, 512), lambda i, j: (i, 17 + j)),
                  blk, blk],
        out_specs=(blk,) * 4,
        compiler_params=_cparams("parallel", "arbitrary"), name="out_proj_bwd")(dx1b, wo, proj, proj, ya, yr)


def _branch_bwd(dya, dyr, wa, wr, att):
    S = dya.shape[0]
    tm = 1024

    def body(da_ref, dr_ref, wa_ref, wr_ref, att_ref, datt_ref, rho_ref, dyi_ref):
        datt = _dot_nt(da_ref[...], wa_ref[...])
        datt_ref[...] = datt.astype(BF16)
        dyi_ref[...] = _dot_nt(dr_ref[...], wr_ref[...]).astype(BF16)
        prod = datt * att_ref[...].astype(F32)
        lo = lax.broadcasted_iota(jnp.int32, (tm, 128), 1) < 64
        for c in range(4):
            pc = prod[:, c * 128:(c + 1) * 128]
            tot = jnp.sum(pc, axis=-1, keepdims=True)
            low = jnp.sum(jnp.where(lo, pc, 0.0), axis=-1, keepdims=True)
            rho_ref[:, (2 * c) * 128:(2 * c + 1) * 128] = jnp.broadcast_to(low, (tm, 128))
            rho_ref[:, (2 * c + 1) * 128:(2 * c + 2) * 128] = jnp.broadcast_to(tot - low, (tm, 128))

    row = lambda w: pl.BlockSpec((tm, w), lambda i: (i, 0))
    return pl.pallas_call(
        body, out_shape=(SDS((S, 512), BF16), SDS((S, 1024), F32), SDS((S, 1024), BF16)), grid=(S // tm,),
        in_specs=[row(1024), row(1024), pl.BlockSpec((512, 1024), lambda i: (0, 0)),
                  pl.BlockSpec((1024, 1024), lambda i: (0, 0)), row(512)],
        out_specs=(row(512), row(1024), row(1024)),
        compiler_params=_cparams("parallel"), name="branch_bwd")(dya, dyr, wa, wr, att)


def _attn_bwd(qkv, datt, lse, rho, rtab, d, gi, exchanges=()):
    L = qkv.shape[0]
    nb = L // BLK

    def body(q_ref, kc_ref, kp_ref, vc_ref, vp_ref, do_ref, lse_ref, rho_ref, tq_ref, tk_ref,
             dq_ref, dk_ref, dv_ref, ck, cv):
        n = pl.program_id(1)

        @pl.when(n == 0)
        def _():
            ck[...] = jnp.zeros_like(ck)
            cv[...] = jnp.zeros_like(cv)

        def store_rot(ref, val, t_ref, c):
            sl = slice(c * 128, (c + 1) * 128)
            ref[:, sl] = _unrot(val, t_ref[0], t_ref[1], t_ref[2], 32).astype(BF16)

        @pl.when(n < nb)
        def _():
            mask = _band_mask(n)
            lo = lax.broadcasted_iota(jnp.int32, (BLK, 128), 1) < 64
            for c in range(4):
                sl = slice(c * 128, (c + 1) * 128)
                q, do = q_ref[:, sl], do_ref[:, sl]
                k = jnp.concatenate([kp_ref[:, sl], kc_ref[:, sl]], axis=0)
                v = jnp.concatenate([vp_ref[:, sl], vc_ref[:, sl]], axis=0)
                dq_c = jnp.zeros((BLK, 128), F32)
                dk_c = jnp.zeros((2 * BLK, 128), F32)
                dv_c = jnp.zeros((2 * BLK, 128), F32)
                for hh in range(2):
                    hm = lo if hh == 0 else jnp.logical_not(lo)
                    qm = jnp.where(hm, q, jnp.zeros_like(q))
                    dom = jnp.where(hm, do, jnp.zeros_like(do))
                    hs = slice((2 * c + hh) * 128, (2 * c + hh + 1) * 128)
                    lse_h = jnp.concatenate([lse_ref[:, hs]] * 2, axis=1)
                    rho_h = jnp.concatenate([rho_ref[:, hs]] * 2, axis=1)
                    s = _dot_nt(qm, k) * 0.125
                    p = jnp.where(mask, jnp.exp(s - lse_h), 0.0)
                    dp = _dot_nt(dom, v)
                    ds = (p * (dp - rho_h) * 0.125).astype(BF16)
                    dq_c = dq_c + jnp.where(hm, _dot(ds, k), 0.0)
                    dk_c = dk_c + _dot_tn(ds, qm)
                    dv_c = dv_c + _dot_tn(p.astype(BF16), dom)
                store_rot(dq_ref, dq_c, tq_ref, c)
                store_rot(dk_ref, ck[:, sl] + dk_c[:BLK], tk_ref, c)
                dv_ref[:, sl] = (cv[:, sl] + dv_c[:BLK]).astype(BF16)
                ck[:, sl] = dk_c[BLK:]
                cv[:, sl] = dv_c[BLK:]

        @pl.when(n == nb)
        def _():
            for c in range(4):
                sl = slice(c * 128, (c + 1) * 128)
                store_rot(dk_ref, ck[:, sl], tk_ref, c)
            dv_ref[...] = cv[...].astype(BF16)

    cur = lambda n: jnp.minimum(n, nb - 1)
    prev = lambda n: jnp.maximum(jnp.minimum(n, nb - 1) - 1, 0)
    fin = lambda n: jnp.maximum(n - 1, 0)
    return _carrier_call(
        body, (qkv, qkv, qkv, qkv, qkv, datt, lse, rho, rtab, rtab),
        out_shape=(SDS((L, d * 512), BF16),) * 3, grid=(d, nb + 1),
        in_specs=[pl.BlockSpec((BLK, 512), lambda r, n: (cur(n), 3 * r)),
                  pl.BlockSpec((BLK, 512), lambda r, n: (cur(n), 3 * r + 1)),
                  pl.BlockSpec((BLK, 512), lambda r, n: (prev(n), 3 * r + 1)),
                  pl.BlockSpec((BLK, 512), lambda r, n: (cur(n), 3 * r + 2)),
                  pl.BlockSpec((BLK, 512), lambda r, n: (prev(n), 3 * r + 2)),
                  pl.BlockSpec((BLK, 512), lambda r, n: (cur(n), r)),
                  pl.BlockSpec((BLK, 1024), lambda r, n: (cur(n), r)),
                  pl.BlockSpec((BLK, 1024), lambda r, n: (cur(n), r)),
                  pl.BlockSpec((3, BLK, 128), lambda r, n: (0, cur(n), r)),
                  pl.BlockSpec((3, BLK, 128), lambda r, n: (0, fin(n), r))],
        out_specs=(pl.BlockSpec((BLK, 512), lambda r, n: (cur(n), r)),
                   pl.BlockSpec((BLK, 512), lambda r, n: (fin(n), r)),
                   pl.BlockSpec((BLK, 512), lambda r, n: (fin(n), r))),
        scratch_shapes=[pltpu.VMEM((BLK, 512), F32), pltpu.VMEM((BLK, 512), F32)],
        sem=("parallel", "arbitrary"), name=f"attn_bwd_g{gi}", exchanges=exchanges)


def _ret_bwd(proj, rn, rstd, dyrin, states, tab, consts, exchanges=()):
    S = proj.shape[0]
    nc = S // BLK
    dmask, zeta, xi, dec = consts

    def body(q_ref, k_ref, v_ref, gr_ref, rn_ref, rs_ref, dy_ref, st_ref, tq_ref, tk_ref,
             dm_ref, z_ref, x_ref, dec_ref, dq_ref, dk_ref, dv_ref, dgr_ref, dR):
        @pl.when(pl.program_id(1) == 0)
        def _():
            dR[...] = jnp.zeros_like(dR)

        q, k, v = q_ref[...], k_ref[...], v_ref[...]
        gr = gr_ref[...].astype(F32)
        sg = _sigmoid(gr)
        rn_v = rn_ref[...].astype(F32)
        dyi = dy_ref[...].astype(F32)
        dgr_ref[...] = (dyi * rn_v * sg * (1.0 + gr * (1.0 - sg))).astype(BF16)
        drn = dyi * gr * sg
        rstd = jnp.concatenate([rs_ref[...]] * 2, axis=1)
        do = rstd * (drn - jnp.mean(drn, axis=-1, keepdims=True) - rn_v * jnp.mean(drn * rn_v, axis=-1, keepdims=True))
        dob = do.astype(BF16)
        Rb = st_ref[...]
        dRb = dR[...].astype(BF16)
        dm, zt, xt = dm_ref[...], z_ref[...], x_ref[...]
        sD = (_dot_nt(q, k) * dm).astype(BF16)
        kz = (k.astype(F32) * zt).astype(BF16)
        qx = (q.astype(F32) * xt).astype(BF16)
        dv_ref[...] = (_dot_tn(sD, dob) + _dot(kz, dRb)).astype(BF16)
        dS = (_dot_nt(dob, v) * dm).astype(BF16)
        dq = _dot(dS, k) + _dot_nt(dob, Rb) * xt
        dk = _dot_tn(dS, q) + _dot_nt(v, dRb) * zt
        dR[...] = dR[...] * dec_ref[0:1, :] + _dot_tn(qx, dob)
        dq_ref[...] = _unrot(dq, tq_ref[0], tq_ref[1], tq_ref[2], 1).astype(BF16)
        dk_ref[...] = _unrot(dk, tk_ref[0], tk_ref[1], tk_ref[2], 1).astype(BF16)

    rc = lambda c: nc - 1 - c
    hc = lambda h, c: (h, 0, 0)
    b128 = pl.BlockSpec((BLK, 128), lambda h, c: (rc(c), h))
    b256 = pl.BlockSpec((BLK, 256), lambda h, c: (rc(c), h))
    return _carrier_call(
        body, (proj, proj, proj, proj, rn, rstd, dyrin, states, tab, tab, dmask, zeta, xi, dec),
        out_shape=(SDS((S, 512), BF16), SDS((S, 512), BF16), SDS((S, 1024), BF16), SDS((S, 1024), BF16)),
        grid=(RET_HEADS, nc),
        in_specs=[pl.BlockSpec((BLK, 128), lambda h, c: (rc(c), 36 + h)),
                  pl.BlockSpec((BLK, 128), lambda h, c: (rc(c), 40 + h)),
                  pl.BlockSpec((BLK, 256), lambda h, c: (rc(c), 22 + h)),
                  pl.BlockSpec((BLK, 256), lambda h, c: (rc(c), 26 + h)),
                  b256, b128, b256,
                  pl.BlockSpec((None, None, BLK, 256), lambda h, c: (h, rc(c), 0, 0)),
                  pl.BlockSpec((None, 3, BLK, 128), lambda h, c: (1, 0, rc(c), 0)),
                  pl.BlockSpec((None, 3, BLK, 128), lambda h, c: (2, 0, rc(c), 0)),
                  pl.BlockSpec((None, BLK, BLK), hc), pl.BlockSpec((None, BLK, 128), hc),
                  pl.BlockSpec((None, BLK, 128), hc), pl.BlockSpec((None, 8, 256), hc)],
        out_specs=(b128, b128, b256, b256),
        scratch_shapes=[pltpu.VMEM((BLK, 256), F32)],
        sem=("parallel", "arbitrary"), name="ret_bwd", exchanges=exchanges)


def _in_proj_bwd(dproj, w_in, x, g1, dx1, part, gx_prev, dg_prev, exchanges=()):
    S = x.shape[0]
    tm = 512
    nh = S // tm // 2
    off = part * nh

    def body(d_ref, w_ref, x_ref, g_ref, dx1_ref, gxp_ref, dgp_ref, dx_ref, dgn_ref, acc):
        i, s = pl.program_id(0), pl.program_id(1)

        @pl.when(s == 0)
        def _():
            acc[...] = jnp.zeros_like(acc)

        @pl.when((i == 0) & (s == 0))
        def _():
            dgn_ref[...] = dgp_ref[...]

        acc[...] += _dot_nt(d_ref[...], w_ref[...])

        @pl.when(s == N_SHARD - 1)
        def _():
            xv = x_ref[...]
            r = lax.rsqrt(jnp.mean(xv * xv, axis=-1, keepdims=True) + NORM_EPS)
            xh = xv * r
            dh = acc[...]
            dgn_ref[...] += jnp.sum(dh * xh, axis=0, keepdims=True)
            dxh = dh * g_ref[...]
            dx_ref[...] = dx1_ref[...] + r * (dxh - xh * jnp.mean(dxh * xh, axis=-1, keepdims=True))

    row = pl.BlockSpec((tm, D_MODEL), lambda i, s: (i + off, 0))
    vec = pl.BlockSpec((1, D_MODEL), lambda i, s: (0, 0))
    (gx, dg), xres = _carrier_call(
        body, (dproj, w_in, x, g1, dx1, gx_prev, dg_prev),
        out_shape=(SDS((S, D_MODEL), F32), SDS((1, D_MODEL), F32)), grid=(nh, N_SHARD),
        in_specs=[pl.BlockSpec((tm, W_IN_S), lambda i, s: (i + off, s)),
                  pl.BlockSpec((D_MODEL, W_IN_S), lambda i, s: (0, s)), row, vec, row, ANY, vec],
        out_specs=(row, vec), scratch_shapes=[pltpu.VMEM((tm, D_MODEL), F32)],
        sem=("arbitrary", "arbitrary"), name=f"in_proj_bwd_{part}", exchanges=exchanges, in_out_aliases={5: 0})
    return gx, dg, xres


def _sub_view(a, d):
    S, W = a.shape
    return a.reshape(S // d, d * W)


def _step(x, tgt, g1, g2, g3, comm):
    S = x.shape[0]
    tab = _tables(S)
    consts = _ret_consts()

    h = _rms_fwd(x, g1)
    w_in = comm.w_in()
    proj, xres = _in_proj(h, w_in, tab, comm.carry("in_proj"))
    comm.took("in_proj", xres)
    qkvs, o_parts, lse_parts = [], [], []
    for gi, d in enumerate(DILATIONS):
        cols = [proj[:, (b + gi) * COLB:(b + gi + 1) * COLB] for b in (QA_B, KA_B, VA_B)]
        qkv = _sub_view(jnp.concatenate(cols, axis=1), d)
        (o_g, lse_g), xres = _attn_fwd(qkv, d, gi, comm.carry(f"attn_fwd_g{gi}"))
        comm.took(f"attn_fwd_g{gi}", xres)
        qkvs.append(qkv)
        o_parts.append(o_g.reshape(S, 512))
        lse_parts.append(lse_g.reshape(S, 1024))
    att, lse_tot = _attn_merge(o_parts, lse_parts)
    yrin, rn, rstd, states = _ret_fwd(proj, consts)
    wa, wr, wo, wg, wu, wd = comm.w_rest()
    merged, ya, yr = _branch_merge(att, yrin, proj, wa, wr)
    x1, h2 = _out_proj(merged, wo, x, g2)
    gte, up, act = _ffn_up(h2, wg, wu)
    dx2, dx2b, dg3, loss_p = _ffn_down_loss(act, wd, x1, g3, tgt)

    dgte, dup = _ffn_down_bwd(dx2b, wd, gte, up)
    tok3 = lambda w: (lambda tk: pl.BlockSpec((None, tk, w), lambda p, k: (p, k, 0)))
    tok2 = lambda w: (lambda tk: pl.BlockSpec((tk, w), lambda p, k: (k, 0)))
    g_d = _wgrad("wgrad_down", act, dx2b, tok3(HID_S), tok2(D_MODEL), (N_SHARD, HID_S, D_MODEL),
                 pl.BlockSpec((None, HID_S, D_MODEL), lambda p, k: (p, 0, 0)), N_SHARD, S)
    g_g = _wgrad("wgrad_gate", h2, dgte, tok2(D_MODEL), tok3(HID_S), (N_SHARD, D_MODEL, HID_S),
                 pl.BlockSpec((None, D_MODEL, HID_S), lambda p, k: (p, 0, 0)), N_SHARD, S)
    g_u = _wgrad("wgrad_up", h2, dup, tok2(D_MODEL), tok3(HID_S), (N_SHARD, D_MODEL, HID_S),
                 pl.BlockSpec((None, D_MODEL, HID_S), lambda p, k: (p, 0, 0)), N_SHARD, S)
    comm.grads({4: g_g, 5: g_u, 6: g_d})
    (dx1, dx1b, dg2), xres = _ffn_up_bwd(dgte, dup, wg, wu, x1, g2, dx2, comm.carry("ffn_up_bwd"))
    comm.took("ffn_up_bwd", xres)
    dya, dyr, dga, dgrr = _out_proj_bwd(dx1b, wo, proj, ya, yr)
    colblk = lambda w: (lambda tk: pl.BlockSpec((tk, w), lambda p, k: (k, p)))
    g_o = _wgrad("wgrad_out", merged, dx1b, colblk(256), tok2(D_MODEL), (D_MODEL, D_MODEL),
                 pl.BlockSpec((256, D_MODEL), lambda p, k: (p, 0)), 4, S)
    datt, rho, dyrin = _branch_bwd(dya, dyr, wa, wr, att)
    g_a = _wgrad("wgrad_attn", att, dya, tok2(512), colblk(512), (512, D_MODEL),
                 pl.BlockSpec((512, 512), lambda p, k: (0, p)), 2, S)
    g_r = _wgrad("wgrad_ret", yrin, dyr, colblk(256), tok2(D_MODEL), (D_MODEL, D_MODEL),
                 pl.BlockSpec((256, D_MODEL), lambda p, k: (p, 0)), 4, S)
    comm.grads({1: g_a, 2: g_r.reshape(N_SHARD, 256, D_MODEL), 3: g_o.reshape(N_SHARD, 256, D_MODEL)})
    (dqr, dkr, dvr, dgr), xres = _ret_bwd(proj, rn, rstd, dyrin, states, tab, consts, comm.carry("ret_bwd"))
    comm.took("ret_bwd", xres)
    dqs, dks, dvs = [], [], []
    for gi, d in enumerate(DILATIONS):
        rtab = tab[0].reshape(3, S // d, d * 128)
        (dq, dk, dv), xres = _attn_bwd(qkvs[gi], _sub_view(datt, d), _sub_view(lse_tot, d), _sub_view(rho, d), rtab, d, gi,
                                       comm.carry(f"attn_bwd_g{gi}"))
        comm.took(f"attn_bwd_g{gi}", xres)
        dqs.append(dq.reshape(S, 512))
        dks.append(dk.reshape(S, 512))
        dvs.append(dv.reshape(S, 512))
    dproj = jnp.concatenate(dqs + dks + dvs + [dqr, dkr, dvr, dgr, dga, dgrr], axis=1)
    g_in = _wgrad("wgrad_in", h, dproj, tok2(D_MODEL), colblk(COLB), (D_MODEL, PROJ_W),
                  pl.BlockSpec((D_MODEL, COLB), lambda p, k: (0, p)), N_COLB, S)
    comm.grads({0: g_in})
    grad_x, dg1 = lax.empty((S, D_MODEL), F32), jnp.zeros((1, D_MODEL), F32)
    for part in range(2):
        grad_x, dg1, xres = _in_proj_bwd(dproj, w_in, x, g1, dx1, part, grad_x, dg1, comm.carry(f"in_proj_bwd_{part}"))
        comm.took(f"in_proj_bwd_{part}", xres)
    return loss_p[0, 0], grad_x, (dg1, dg2, dg3)


W_KINDS = ("col", "col", "lead", "lead", "lead", "lead", "lead")
W_SHARD = ((1024, W_IN_S), (512, 256), (256, 1024), (256, 1024), (1024, HID_S), (1024, HID_S), (HID_S, 1024))
N_W = len(W_KINDS)


def _full_shape(wi):
    R, C = W_SHARD[wi]
    return (R, N_SHARD * C) if W_KINDS[wi] == "col" else (N_SHARD, R, C)


def _view(ref, wi, s, half):
    R, C = W_SHARD[wi]
    rows = pl.ds(half * (R // 2), R // 2)
    if W_KINDS[wi] == "col":
        return ref.at[rows, pl.ds(pl.multiple_of(s * C, 128), C)]
    return ref.at[s, rows, :]


def _mesh_pos():
    x, y, c = lax.axis_index("x"), lax.axis_index("y"), lax.axis_index("c")
    chips = [(1 - x, y), (x, 1 - y), (1 - x, 1 - y)]
    return x, y, c, chips


def _cast_bf16(a):
    R, C = a.shape
    tr = R // 2 if R % 32 == 0 else R

    def body(a_ref, o_ref):
        o_ref[...] = a_ref[...].astype(BF16)

    spec = pl.BlockSpec((tr, C), lambda i: (i, 0))
    return pl.pallas_call(body, out_shape=SDS((R, C), BF16), grid=(R // tr,), in_specs=[spec], out_specs=spec,
                          compiler_params=_cparams("parallel"), name=f"cast_{R}x{C}")(a)


def _remote(send, recv, k, src, dst, to):
    return pltpu.make_async_remote_copy(src_ref=src, dst_ref=dst, send_sem=send.at[k], recv_sem=recv.at[k],
                                        device_id=to, device_id_type=MESH)


def _gather_now(wis, shards):
    n = len(wis)

    def body(*refs):
        sh, full = refs[:n], refs[n:2 * n]
        send, recv, loc = refs[2 * n:]
        x, y, c, chips = _mesh_pos()
        s_me = 2 * x + y
        sib = (x, y, 1 - c)
        own, started = [], []
        for i, wi in enumerate(wis):
            Rh = W_SHARD[wi][0] // 2
            for hf in range(2):
                cp = pltpu.make_async_copy(sh[i].at[pl.ds(hf * Rh, Rh), :], _view(full[i], wi, s_me, hf), loc.at[2 * i + hf])
                cp.start()
                own.append(cp)
            for j, chip in enumerate(chips):
                cp = _remote(send, recv, 3 * i + j, sh[i].at[pl.ds(c * Rh, Rh), :], _view(full[i], wi, s_me, c), (*chip, c))
                cp.start()
                started.append(cp)
        for i, wi in enumerate(wis):
            for j, chip in enumerate(chips):
                land = _view(full[i], wi, 2 * chip[0] + chip[1], c)
                _remote(send, recv, 3 * i + j, land, land, (*chip, c)).wait_recv()
                fw = _remote(send, recv, 3 * n + 3 * i + j, land, land, sib)
                fw.start()
                started.append(fw)
        for i, wi in enumerate(wis):
            for j, chip in enumerate(chips):
                land = _view(full[i], wi, 2 * chip[0] + chip[1], 1 - c)
                _remote(send, recv, 3 * n + 3 * i + j, land, land, sib).wait_recv()
        for cp in started:
            cp.wait_send()
        for cp in own:
            cp.wait()

    return pl.pallas_call(
        body, out_shape=tuple(SDS(_full_shape(wi), BF16) for wi in wis),
        in_specs=[ANY] * n, out_specs=tuple([ANY] * n),
        scratch_shapes=[pltpu.SemaphoreType.DMA((6 * n,)), pltpu.SemaphoreType.DMA((6 * n,)),
                        pltpu.SemaphoreType.DMA((2 * n,))],
        name="gather_now")(*shards)


def _ex_gather_ici(wis, shards):
    def build(ins, outs, send, recv, loc):
        x, y, c, chips = _mesh_pos()
        s_me = 2 * x + y
        starts, waits = [], []
        for i, wi in enumerate(wis):
            Rh = W_SHARD[wi][0] // 2
            for hf in range(2):
                cp = pltpu.make_async_copy(ins[i].at[pl.ds(hf * Rh, Rh), :], _view(outs[i], wi, s_me, hf), loc.at[2 * i + hf])
                starts.append(cp)
                waits.append(cp.wait)
            for j, chip in enumerate(chips):
                cp = _remote(send, recv, 3 * i + j, ins[i].at[pl.ds(c * Rh, Rh), :], _view(outs[i], wi, s_me, c), (*chip, c))
                land = _view(outs[i], wi, 2 * chip[0] + chip[1], c)
                starts.append(cp)
                waits += [cp.wait_send, _remote(send, recv, 3 * i + j, land, land, (*chip, c)).wait_recv]
        return starts, waits

    return _Exchange(shards, [SDS(_full_shape(wi), BF16) for wi in wis], {}, 3 * len(wis), 2 * len(wis), build)


def _ex_gather_d2d(wis, fulls):
    def build(ins, outs, send, recv, loc):
        x, y, c, chips = _mesh_pos()
        sib = (x, y, 1 - c)
        starts, waits = [], []
        for i, wi in enumerate(wis):
            for j, chip in enumerate(chips):
                mine = _view(outs[i], wi, 2 * chip[0] + chip[1], c)
                theirs = _view(outs[i], wi, 2 * chip[0] + chip[1], 1 - c)
                cp = _remote(send, recv, 3 * i + j, mine, mine, sib)
                starts.append(cp)
                waits += [cp.wait_send, _remote(send, recv, 3 * i + j, theirs, theirs, sib).wait_recv]
        return starts, waits

    return _Exchange(fulls, [SDS(f.shape, BF16) for f in fulls], {i: i for i in range(len(wis))}, 3 * len(wis), 0, build)


def _half_shape(wi):
    R, C = W_SHARD[wi]
    return (R // 2, N_SHARD * C) if W_KINDS[wi] == "col" else (N_SHARD, R // 2, C)


def _ex_pair(wis, grads):
    def build(ins, outs, send, recv, loc):
        x, y, c, _ = _mesh_pos()
        starts, waits = [], []
        for i, wi in enumerate(wis):
            Rh = W_SHARD[wi][0] // 2
            rows = pl.ds((1 - c) * Rh, Rh)
            src = ins[i].at[rows, :] if W_KINDS[wi] == "col" else ins[i].at[:, rows, :]
            cp = _remote(send, recv, i, src, outs[i], (x, y, 1 - c))
            starts.append(cp)
            waits.append(cp.wait)
        return starts, waits

    return _Exchange(grads, [SDS(_half_shape(wi), F32) for wi in wis], {}, len(wis), 0, build)


def _ex_chip(wis, pbs):
    def build(ins, outs, send, recv, loc):
        x, y, c, chips = _mesh_pos()
        starts, waits = [], []
        for i, wi in enumerate(wis):
            for j, chip in enumerate(chips):
                cp = _remote(send, recv, 3 * i + j, ins[i].at[j], outs[i].at[j], (*chip, c))
                starts.append(cp)
                waits.append(cp.wait)
        return starts, waits

    shapes = [SDS((3, W_SHARD[wi][0] // 2, W_SHARD[wi][1]), BF16) for wi in wis]
    return _Exchange(pbs, shapes, {}, 3 * len(wis), 0, build)


def _ex_share(wis, halves):
    def build(ins, outs, send, recv, loc):
        x, y, c, _ = _mesh_pos()
        sib = (x, y, 1 - c)
        starts, waits = [], []
        for i, wi in enumerate(wis):
            cp = _remote(send, recv, i, outs[i].at[c], outs[i].at[c], sib)
            starts.append(cp)
            waits += [cp.wait_send, _remote(send, recv, i, outs[i].at[1 - c], outs[i].at[1 - c], sib).wait_recv]
        return starts, waits

    return _Exchange(halves, [SDS(h.shape, F32) for h in halves], {i: i for i in range(len(wis))}, len(wis), 0, build)


def _row_tile(rh, C):
    best = 16
    for t in range(16, rh + 1, 16):
        if rh % t == 0 and t * C * 4 <= (3 << 19):
            best = t
    return best


def _pair_sum(wi, g, ra, sidx):
    R, C = W_SHARD[wi]
    Rh = R // 2
    tr = _row_tile(Rh, C)
    nt = Rh // tr
    col = W_KINDS[wi] == "col"

    def body(sidx_ref, *refs):
        gs, rs = refs[:4], refs[4:8]
        own_ref, pb_ref = refs[8:]
        own_ref[...] = gs[0][...] + rs[0][...]
        for j in range(3):
            pb_ref[j] = (gs[1 + j][...] + rs[1 + j][...]).astype(BF16)

    def gspec(slot):
        if col:
            return pl.BlockSpec((tr, C), lambda i, sx: (sx[4] * nt + i, sx[slot]))
        return pl.BlockSpec((None, tr, C), lambda i, sx: (sx[slot], sx[4] * nt + i, 0))

    def rspec(slot):
        if col:
            return pl.BlockSpec((tr, C), lambda i, sx: (i, sx[slot]))
        return pl.BlockSpec((None, tr, C), lambda i, sx: (sx[slot], i, 0))

    return pl.pallas_call(
        body, out_shape=(SDS((Rh, C), F32), SDS((3, Rh, C), BF16)),
        grid_spec=pltpu.PrefetchScalarGridSpec(
            num_scalar_prefetch=1, grid=(nt,),
            in_specs=[gspec(k) for k in range(4)] + [rspec(k) for k in range(4)],
            out_specs=(pl.BlockSpec((tr, C), lambda i, sx: (i, 0)), pl.BlockSpec((3, tr, C), lambda i, sx: (0, i, 0)))),
        compiler_params=_cparams("arbitrary"), name=f"pair_sum_w{wi}")(sidx, g, g, g, g, ra, ra, ra, ra)


def _chip_sum(wi, own, rb, sidx):
    R, C = W_SHARD[wi]
    Rh = R // 2
    tr = _row_tile(Rh, C)

    def body(sidx_ref, own_ref, rb_ref, o_ref):
        o_ref[...] = ((own_ref[...] + rb_ref[0].astype(F32)) + rb_ref[1].astype(F32)) + rb_ref[2].astype(F32)

    return pl.pallas_call(
        body, out_shape=SDS((2, Rh, C), F32),
        grid_spec=pltpu.PrefetchScalarGridSpec(
            num_scalar_prefetch=1, grid=(Rh // tr,),
            in_specs=[pl.BlockSpec((tr, C), lambda i, sx: (i, 0)), pl.BlockSpec((3, tr, C), lambda i, sx: (0, i, 0))],
            out_specs=pl.BlockSpec((None, tr, C), lambda i, sx: (sx[4], i, 0))),
        compiler_params=_cparams("arbitrary"), name=f"chip_sum_w{wi}")(sidx, own, rb)


def _gain_allgather(blk):
    m_per, n = blk.shape

    def body(x_ref, out_ref, send_sems, recv_sems, local_sem):
        x, y, c, chips = _mesh_pos()
        me, sibling = (x, y, c), (x, y, 1 - c)

        def rows(px, py, pc):
            return out_ref.at[pl.ds((4 * px + 2 * py + pc) * m_per, m_per), :]

        def copy(k, block, to, src=None):
            return pltpu.make_async_remote_copy(
                src_ref=rows(*block) if src is None else src, dst_ref=rows(*block),
                send_sem=send_sems.at[k], recv_sem=recv_sems.at[k], device_id=to, device_id_type=MESH)

        mine = pltpu.make_async_copy(x_ref, rows(*me), local_sem)
        mine.start()
        first = [copy(0, me, sibling, src=x_ref)]
        first += [copy(1 + j, me, (*chip, c), src=x_ref) for j, chip in enumerate(chips)]
        for cp in first:
            cp.start()
        passed = [copy(4 + j, (*chip, c), sibling) for j, chip in enumerate(chips)]
        for j, chip in enumerate(chips):
            copy(1 + j, (*chip, c), me).wait_recv()
            passed[j].start()
        copy(0, sibling, me).wait_recv()
        for j, chip in enumerate(chips):
            copy(4 + j, (*chip, 1 - c), me).wait_recv()
        for cp in first + passed:
            cp.wait_send()
        mine.wait()

    vm = pl.BlockSpec(memory_space=pltpu.VMEM)
    return pl.pallas_call(
        body, out_shape=SDS((8 * m_per, n), blk.dtype), in_specs=[vm], out_specs=vm,
        scratch_shapes=[pltpu.SemaphoreType.DMA((7,)), pltpu.SemaphoreType.DMA((7,)), pltpu.SemaphoreType.DMA],
        name="gain_allgather")(blk)


def _adam_math(w, g, m, v):
    mn = ADAM_B1 * m + (1.0 - ADAM_B1) * g
    vn = ADAM_B2 * v + (1.0 - ADAM_B2) * (g * g)
    mh = mn / (1.0 - ADAM_B1 ** ADAM_STEP)
    vh = vn / (1.0 - ADAM_B2 ** ADAM_STEP)
    return -ADAM_LR * (mh / (jnp.sqrt(vh) + ADAM_EPS) + ADAM_WD * w), mn, vn


def _adamw(wi, w, g, m, v):
    R, C = w.shape
    tr = _row_tile(R, C)

    def body(w_ref, g_ref, m_ref, v_ref, d_ref, mn_ref, vn_ref):
        d_ref[...], mn_ref[...], vn_ref[...] = _adam_math(w_ref[...], g_ref[...], m_ref[...], v_ref[...])

    spec = pl.BlockSpec((tr, C), lambda i: (i, 0))
    return pl.pallas_call(body, out_shape=(SDS((R, C), F32),) * 3, grid=(R // tr,), in_specs=[spec] * 4,
                          out_specs=(spec,) * 3, compiler_params=_cparams("parallel"), name=f"adamw_w{wi}")(w, g, m, v)


def _gain_update(gathered, w, m, v):
    def body(ga_ref, w_ref, m_ref, v_ref, g_ref, d_ref, mn_ref, vn_ref):
        g = ga_ref[0:8, :]
        for dev in range(1, 8):
            g = g + ga_ref[8 * dev:8 * dev + 8, :]
        g_ref[...] = g
        d_ref[...], mn_ref[...], vn_ref[...] = _adam_math(w_ref[...], g, m_ref[...], v_ref[...])

    return pl.pallas_call(body, out_shape=(SDS((8, 1024), F32),) * 4, name="gain_update")(gathered, w, m, v)


GROUP_FFN, GROUP_MIX, GROUP_IN = (4, 5, 6), (1, 2, 3), (0,)
REST = GROUP_MIX + GROUP_FFN


class _MeshComm:
    def __init__(self, shards):
        xi, yi, ci = lax.axis_index("x"), lax.axis_index("y"), lax.axis_index("c")
        self.sidx = jnp.stack([2 * xi + yi, 2 * (1 - xi) + yi, 2 * xi + (1 - yi), 2 * (1 - xi) + (1 - yi), ci]).astype(jnp.int32)
        self.shards, self.fulls = shards, None
        self.g, self.own, self.pb, self.half, self.red = {}, {}, {}, {}, {}

    def w_in(self):
        return _gather_now(GROUP_IN, [self.shards[0]])[0]

    def w_rest(self):
        f = dict(zip(REST, self.fulls))
        return f[1], f[2].reshape(D_MODEL, D_MODEL), f[3].reshape(D_MODEL, D_MODEL), f[4], f[5], f[6]

    def grads(self, by_wi):
        self.g.update(by_wi)

    def _pick(self, table, wis):
        return [table[wi] for wi in wis]

    def _pair_sums(self, wis, ras):
        for wi, ra in zip(wis, ras):
            self.own[wi], self.pb[wi] = _pair_sum(wi, self.g[wi], ra, self.sidx)

    def _chip_sums(self, wis, rbs):
        for wi, rb in zip(wis, rbs):
            self.half[wi] = _chip_sum(wi, self.own[wi], rb, self.sidx)

    def carry(self, point):
        if point == "in_proj":
            return [_ex_gather_ici(REST, self._pick(self.shards, REST))]
        if point == "attn_fwd_g0":
            return [_ex_gather_d2d(REST, list(self.fulls))]
        if point == "ffn_up_bwd":
            return [_ex_pair(GROUP_FFN, self._pick(self.g, GROUP_FFN))]
        if point == "ret_bwd":
            return [_ex_chip(GROUP_FFN, self._pick(self.pb, GROUP_FFN))]
        if point == "attn_bwd_g0":
            return [_ex_pair(GROUP_MIX, self._pick(self.g, GROUP_MIX)), _ex_share(GROUP_FFN, self._pick(self.half, GROUP_FFN))]
        if point == "attn_bwd_g1":
            return [_ex_chip(GROUP_MIX, self._pick(self.pb, GROUP_MIX))]
        if point == "attn_bwd_g2":
            return [_ex_share(GROUP_MIX, self._pick(self.half, GROUP_MIX))]
        if point == "in_proj_bwd_0":
            return [_ex_pair(GROUP_IN, self._pick(self.g, GROUP_IN))]
        if point == "in_proj_bwd_1":
            return [_ex_chip(GROUP_IN, self._pick(self.pb, GROUP_IN))]
        return []

    def took(self, point, xres):
        if point in ("in_proj", "attn_fwd_g0"):
            self.fulls = xres[0]
        elif point == "ffn_up_bwd":
            self._pair_sums(GROUP_FFN, xres[0])
        elif point == "ret_bwd":
            self._chip_sums(GROUP_FFN, xres[0])
        elif point == "attn_bwd_g0":
            self._pair_sums(GROUP_MIX, xres[0])
            self.red.update(zip(GROUP_FFN, xres[1]))
        elif point == "attn_bwd_g1":
            self._chip_sums(GROUP_MIX, xres[0])
        elif point == "attn_bwd_g2":
            self.red.update(zip(GROUP_MIX, xres[0]))
        elif point == "in_proj_bwd_0":
            self._pair_sums(GROUP_IN, xres[0])
        elif point == "in_proj_bwd_1":
            self._chip_sums(GROUP_IN, xres[0])
            self.red.update(zip(GROUP_IN, _exchange_call(_ex_share(GROUP_IN, self._pick(self.half, GROUP_IN)), "share_w_in")))

    def reduced(self):
        return [self.red[wi] for wi in range(N_W)]


def kernel(x, norm_mix_g, w_in, w_out_attn, w_out_ret, w_out, norm_ffn_g, w_ffn_gate, w_ffn_up, w_ffn_down, norm_final_g, loss_target, m_norm_mix_g, m_w_in, m_w_out_attn, m_w_out_ret, m_w_out, m_norm_ffn_g, m_w_ffn_gate, m_w_ffn_up, m_w_ffn_down, m_norm_final_g, v_norm_mix_g, v_w_in, v_w_out_attn, v_w_out_ret, v_w_out, v_norm_ffn_g, v_w_ffn_gate, v_w_ffn_up, v_w_ffn_down, v_norm_final_g):
    ws = (w_in, w_out_attn, w_out_ret, w_out, w_ffn_gate, w_ffn_up, w_ffn_down)
    ms = (m_w_in, m_w_out_attn, m_w_out_ret, m_w_out, m_w_ffn_gate, m_w_ffn_up, m_w_ffn_down)
    vs = (v_w_in, v_w_out_attn, v_w_out_ret, v_w_out, v_w_ffn_gate, v_w_ffn_up, v_w_ffn_down)
    shard2d = lambda a, wi: a.reshape(W_SHARD[wi])

    comm = _MeshComm([_cast_bf16(shard2d(w, wi)) for wi, w in enumerate(ws)])
    g3 = norm_final_g.reshape(1, D_MODEL)
    loss_p, grad_x, gain_g = _step(x[0], loss_target[0], norm_mix_g, norm_ffn_g, g3, comm)
    gred = comm.reduced()

    outs_g, outs_d, outs_m, outs_v = [], [], [], []
    for wi in range(N_W):
        g2d = gred[wi].reshape(W_SHARD[wi])
        dlt, mn, vn = _adamw(wi, shard2d(ws[wi], wi), g2d, shard2d(ms[wi], wi), shard2d(vs[wi], wi))
        for lst, a in ((outs_g, g2d), (outs_d, dlt), (outs_m, mn), (outs_v, vn)):
            lst.append(a.reshape(ws[wi].shape))

    pad8 = lambda rows: jnp.concatenate([r.reshape(1, D_MODEL) for r in rows] + [jnp.zeros((5, D_MODEL), F32)], axis=0)
    gathered = _gain_allgather(pad8(gain_g))
    gg, gd, gm, gv = _gain_update(gathered, pad8((norm_mix_g, norm_ffn_g, norm_final_g)),
                                  pad8((m_norm_mix_g, m_norm_ffn_g, m_norm_final_g)),
                                  pad8((v_norm_mix_g, v_norm_ffn_g, v_norm_final_g)))
    loss = lax.psum(loss_p, ("x", "y", "c"))

    def assemble(gain_rows, wlist):
        return (gain_rows[0:1], wlist[0], wlist[1], wlist[2], wlist[3], gain_rows[1:2],
                wlist[4], wlist[5], wlist[6], gain_rows[2])

    return (loss, grad_x[None], *assemble(gg, outs_g), *assemble(gd, outs_d), *assemble(gm, outs_m), *assemble(gv, outs_v))
```

```python
import functools
import math

import numpy as np
import jax
import jax.numpy as jnp
from jax import lax
from jax.experimental import pallas as pl
from jax.experimental.pallas import tpu as pltpu

F32, BF16 = jnp.float32, jnp.bfloat16
SDS = jax.ShapeDtypeStruct
MESH = pl.DeviceIdType.MESH

D_MODEL = 1024
PROJ_W = 9728
COLB = 512
N_COLB = PROJ_W // COLB
QA_B, KA_B, VA_B = 0, 3, 6
QR_B, KR_B = 9, 10
FFN_HID = 2816
N_SHARD = 4
HID_S = FFN_HID // N_SHARD
W_IN_S = PROJ_W // N_SHARD
DILATIONS = (1, 4, 16)
BLK = 128
RET_HEADS = 4
ROPE_THETA = 10000.0
NORM_EPS = 1e-6
ADAM_LR, ADAM_B1, ADAM_B2, ADAM_EPS, ADAM_WD, ADAM_STEP = 0.001, 0.9, 0.999, 1e-08, 0.01, 10
VMEM_LIMIT = 56 << 20


def _cparams(*sem):
    return pltpu.CompilerParams(dimension_semantics=sem or None, vmem_limit_bytes=VMEM_LIMIT)


def _dot(a, b):
    return jnp.dot(a, b, preferred_element_type=F32)


def _dot_nt(a, b):
    return lax.dot_general(a, b, (((1,), (1,)), ((), ())), preferred_element_type=F32)


def _dot_tn(a, b):
    return lax.dot_general(a, b, (((0,), (0,)), ((), ())), preferred_element_type=F32)


def _sigmoid(z):
    return 1.0 / (1.0 + jnp.exp(-z))


ANY = pl.BlockSpec(memory_space=pl.ANY)


class _Exchange:
    def __init__(self, ins, out_shapes, aliases, n_sem, n_loc, build):
        self.ins, self.out_shapes, self.aliases = list(ins), list(out_shapes), dict(aliases)
        self.n_sem, self.n_loc, self.build = n_sem, n_loc, build

    def sems(self):
        return [pltpu.SemaphoreType.DMA((self.n_sem,)), pltpu.SemaphoreType.DMA((self.n_sem,)),
                pltpu.SemaphoreType.DMA((max(self.n_loc, 1),))]


def _exchange_call(ex, name):
    n_in, n_out = len(ex.ins), len(ex.out_shapes)

    def body(*refs):
        starts, waits = ex.build(refs[:n_in], refs[n_in:n_in + n_out], *refs[n_in + n_out:])
        for cp in starts:
            cp.start()
        for w in waits:
            w()

    return pl.pallas_call(body, out_shape=tuple(ex.out_shapes), in_specs=[ANY] * n_in, out_specs=tuple([ANY] * n_out),
                          input_output_aliases=ex.aliases, scratch_shapes=ex.sems(), name=name)(*ex.ins)


def _carrier_call(body, args, *, out_shape, grid, in_specs, out_specs, scratch_shapes=(), sem, name, exchanges=(),
                  in_out_aliases=None):
    out_shape, out_specs = tuple(out_shape), tuple(out_specs)
    n_in, n_out, n_scr = len(args), len(out_shape), len(scratch_shapes)
    x_args, x_outs, aliases, x_scr, spans = [], [], dict(in_out_aliases or {}), [], []
    for ex in exchanges:
        i0, o0 = len(x_args), len(x_outs)
        for a, o in ex.aliases.items():
            aliases[n_in + i0 + a] = n_out + o0 + o
        x_args += ex.ins
        x_outs += ex.out_shapes
        x_scr += ex.sems()
        spans.append((i0, len(ex.ins), o0, len(ex.out_shapes)))
    nx_in, nx_out = len(x_args), len(x_outs)

    def wrapped(*refs):
        ins, xin = refs[:n_in], refs[n_in:n_in + nx_in]
        o_base = n_in + nx_in
        outs, xout = refs[o_base:o_base + n_out], refs[o_base + n_out:o_base + n_out + nx_out]
        s_base = o_base + n_out + nx_out
        scr, xs = refs[s_base:s_base + n_scr], refs[s_base + n_scr:]

        def built(e):
            i0, ni, o0, no = spans[e]
            return exchanges[e].build(xin[i0:i0 + ni], xout[o0:o0 + no], *xs[3 * e:3 * e + 3])

        if exchanges:
            first = functools.reduce(jnp.logical_and, [pl.program_id(k) == 0 for k in range(len(grid))])
            last = functools.reduce(jnp.logical_and, [pl.program_id(k) == grid[k] - 1 for k in range(len(grid))])

            @pl.when(first)
            def _():
                for e in range(len(exchanges)):
                    for cp in built(e)[0]:
                        cp.start()

        body(*ins, *outs, *scr)

        if exchanges:
            @pl.when(last)
            def _():
                for e in range(len(exchanges)):
                    for w in built(e)[1]:
                        w()

    res = pl.pallas_call(
        wrapped, out_shape=out_shape + tuple(x_outs), grid=grid,
        in_specs=list(in_specs) + [ANY] * nx_in, out_specs=out_specs + tuple([ANY] * nx_out),
        scratch_shapes=list(scratch_shapes) + x_scr, input_output_aliases=aliases,
        compiler_params=_cparams(*(sem if not exchanges else ("arbitrary",) * len(grid))), name=name)(*args, *x_args)
    xres = [tuple(res[n_out + o0:n_out + o0 + no]) for (_, _, o0, no) in spans]
    return tuple(res[:n_out]), xres


def _tables(S):
    pos = jnp.arange(S, dtype=F32)
    lane = np.arange(128)
    inv = ROPE_THETA ** (-jnp.arange(0, 64, 2, dtype=F32) / 64)
    ang = pos[:, None] * inv[None, :]
    idx = (lane % 64) % 32
    c, s = jnp.cos(ang)[:, idx], jnp.sin(ang)[:, idx]
    first = jnp.asarray((lane % 64) < 32)[None, :]
    rope = jnp.stack([c, jnp.where(first, 0.0, s), jnp.where(first, -s, 0.0)])
    base = 1.0 / (ROPE_THETA ** jnp.linspace(0.0, 1.0, 64, dtype=F32))
    ang2 = pos[:, None] * base[None, :]
    c2, s2 = jnp.cos(ang2)[:, lane // 2], jnp.sin(ang2)[:, lane // 2]
    even = jnp.asarray(lane % 2 == 0)[None, :]
    th = jnp.stack([c2, jnp.where(even, 0.0, s2), jnp.where(even, -s2, 0.0)])
    return jnp.stack([rope, th, th * (128 ** -0.5)]).astype(F32)


def _rot(a, c, sa, sb, shift):
    return a * c + pltpu.roll(a, shift, 1) * sa + pltpu.roll(a, 128 - shift, 1) * sb


def _unrot(g, c, sa, sb, shift):
    return g * c + pltpu.roll(g * sa, 128 - shift, 1) + pltpu.roll(g * sb, shift, 1)


def _ret_consts():
    h = np.arange(RET_HEADS, dtype=np.float64)
    log_g = np.log1p(-(2.0 ** (-5.0 - h)))
    idx = np.arange(BLK, dtype=np.float64)
    diff = idx[:, None] - idx[None, :]
    dmask = np.where(diff[None] >= 0, np.exp(np.maximum(diff, 0.0)[None] * log_g[:, None, None]), 0.0)
    zeta = np.exp((BLK - 1 - idx)[None, :] * log_g[:, None])
    xi = np.exp((idx + 1.0)[None, :] * log_g[:, None])
    dec = np.exp(BLK * log_g)
    rep = lambda v: np.broadcast_to(v[:, :, None], (RET_HEADS, BLK, 128))
    return (jnp.asarray(dmask, F32), jnp.asarray(rep(zeta), F32), jnp.asarray(rep(xi), F32),
            jnp.asarray(np.broadcast_to(dec[:, None, None], (RET_HEADS, 8, 256)), F32))


def _rms_fwd(x, g):
    S = x.shape[0]
    tm = 512

    def body(x_ref, g_ref, h_ref):
        xv = x_ref[...]
        r = lax.rsqrt(jnp.mean(xv * xv, axis=-1, keepdims=True) + NORM_EPS)
        h_ref[...] = (xv * r * g_ref[...]).astype(BF16)

    return pl.pallas_call(
        body, out_shape=SDS((S, D_MODEL), BF16), grid=(S // tm,),
        in_specs=[pl.BlockSpec((tm, D_MODEL), lambda i: (i, 0)), pl.BlockSpec((1, D_MODEL), lambda i: (0, 0))],
        out_specs=pl.BlockSpec((tm, D_MODEL), lambda i: (i, 0)),
        compiler_params=_cparams("parallel"), name="rms_fwd")(x, g)


def _in_proj(h, w_in, tab, exchanges=()):
    S = h.shape[0]
    tm = min(S, 2048)

    def body(h_ref, w_ref, t_ref, o_ref):
        j = pl.program_id(1)
        acc = _dot(h_ref[...], w_ref[...])
        is_rope = j < 6
        is_theta = (j == QR_B) | (j == KR_B)

        def rotated(shift):
            c, sa, sb = t_ref[0, 0], t_ref[0, 1], t_ref[0, 2]
            for k in range(COLB // 128):
                sl = slice(k * 128, (k + 1) * 128)
                o_ref[:, sl] = _rot(acc[:, sl], c, sa, sb, shift).astype(BF16)

        @pl.when(is_rope)
        def _():
            rotated(32)

        @pl.when(is_theta)
        def _():
            rotated(1)

        @pl.when(jnp.logical_not(is_rope | is_theta))
        def _():
            o_ref[...] = acc.astype(BF16)

    def tab_map(i, j):
        return (jnp.where(j == QR_B, 1, jnp.where(j == KR_B, 2, 0)), 0, i, 0)

    (proj,), xres = _carrier_call(
        body, (h, w_in, tab), out_shape=(SDS((S, PROJ_W), BF16),), grid=(S // tm, N_COLB),
        in_specs=[pl.BlockSpec((tm, D_MODEL), lambda i, j: (i, 0)),
                  pl.BlockSpec((D_MODEL, COLB), lambda i, j: (0, j)),
                  pl.BlockSpec((1, 3, tm, 128), tab_map)],
        out_specs=(pl.BlockSpec((tm, COLB), lambda i, j: (i, j)),),
        sem=("parallel", "arbitrary"), name="in_proj", exchanges=exchanges)
    return proj, xres


def _band_mask(n):
    qi = lax.broadcasted_iota(jnp.int32, (BLK, 2 * BLK), 0)
    kj = lax.broadcasted_iota(jnp.int32, (BLK, 2 * BLK), 1)
    dist = BLK + qi - kj
    return (dist >= 0) & (dist <= BLK) & ((kj >= BLK) | (n > 0))


def _attn_fwd(qkv, d, gi, exchanges=()):
    L = qkv.shape[0]
    nb = L // BLK

    def body(q_ref, kc_ref, kp_ref, vc_ref, vp_ref, o_ref, lse_ref):
        n = pl.program_id(1)
        mask = _band_mask(n)
        mask2 = jnp.concatenate([mask, mask], axis=0)
        lane = lax.broadcasted_iota(jnp.int32, (BLK, 128), 1)
        lo = lane < 64
        lse_all = jnp.zeros((BLK, 128), F32)
        for c in range(4):
            sl = slice(c * 128, (c + 1) * 128)
            q = q_ref[:, sl]
            k = jnp.concatenate([kp_ref[:, sl], kc_ref[:, sl]], axis=0)
            v = jnp.concatenate([vp_ref[:, sl], vc_ref[:, sl]], axis=0)
            q2 = jnp.concatenate([jnp.where(lo, q, jnp.zeros_like(q)), jnp.where(lo, jnp.zeros_like(q), q)], axis=0)
            s = jnp.where(mask2, _dot_nt(q2, k) * 0.125, jnp.float32(-1e30))
            m = jnp.max(s, axis=-1, keepdims=True)
            p = jnp.exp(s - m)
            l = jnp.sum(p, axis=-1, keepdims=True)
            o2 = _dot((p / l).astype(BF16), v)
            o_ref[:, sl] = jnp.where(lo, o2[:BLK], o2[BLK:])
            lse = m + jnp.log(l)
            lse_all = jnp.where(lane // 16 == 2 * c, lse[:BLK], jnp.where(lane // 16 == 2 * c + 1, lse[BLK:], lse_all))
        lse_ref[...] = lse_all

    prev = lambda n: jnp.maximum(n - 1, 0)
    return _carrier_call(
        body, (qkv,) * 5, out_shape=(SDS((L, d * 512), F32), SDS((L, d * 128), F32)), grid=(d, nb),
        in_specs=[pl.BlockSpec((BLK, 512), lambda r, n: (n, 3 * r)),
                  pl.BlockSpec((BLK, 512), lambda r, n: (n, 3 * r + 1)),
                  pl.BlockSpec((BLK, 512), lambda r, n: (prev(n), 3 * r + 1)),
                  pl.BlockSpec((BLK, 512), lambda r, n: (n, 3 * r + 2)),
                  pl.BlockSpec((BLK, 512), lambda r, n: (prev(n), 3 * r + 2))],
        out_specs=(pl.BlockSpec((BLK, 512), lambda r, n: (n, r)),
                   pl.BlockSpec((BLK, 128), lambda r, n: (n, r))),
        sem=("parallel", "arbitrary"), name=f"attn_fwd_g{gi}", exchanges=exchanges)


def _attn_merge(os_, lses):
    S = os_[0].shape[0]
    tm = 512

    def body(o0, o1, o2, l0, l1, l2, att_ref, lt_ref):
        lo = lax.broadcasted_iota(jnp.int32, (tm, 128), 1) < 64
        ls = [l0[...], l1[...], l2[...]]
        m = jnp.maximum(jnp.maximum(ls[0], ls[1]), ls[2])
        es = [jnp.exp(v - m) for v in ls]
        z = es[0] + es[1] + es[2]
        lt_ref[...] = m + jnp.log(z)
        ws = [e / z for e in es]
        for c in range(4):
            sl = slice(c * 128, (c + 1) * 128)
            acc = jnp.zeros((tm, 128), F32)
            for g, o_g in enumerate((o0, o1, o2)):
                w_lo = jnp.broadcast_to(ws[g][:, 32 * c:32 * c + 1], (tm, 128))
                w_hi = jnp.broadcast_to(ws[g][:, 32 * c + 16:32 * c + 17], (tm, 128))
                acc = acc + jnp.where(lo, w_lo, w_hi) * o_g[:, sl]
            att_ref[:, sl] = acc.astype(BF16)

    ospec = pl.BlockSpec((tm, 512), lambda i: (i, 0))
    lspec = pl.BlockSpec((tm, 128), lambda i: (i, 0))
    return pl.pallas_call(
        body, out_shape=(SDS((S, 512), BF16), SDS((S, 128), F32)), grid=(S // tm,),
        in_specs=[ospec] * 3 + [lspec] * 3, out_specs=(ospec, lspec),
        compiler_params=_cparams("parallel"), name="attn_merge")(*os_, *lses)


def _ret_fwd(proj, consts):
    S = proj.shape[0]
    nc = S // BLK
    dmask, zeta, xi, dec = consts

    def body(q_ref, k_ref, v0_ref, v1_ref, g0_ref, g1_ref, dm_ref, z_ref, x_ref, dec_ref,
             y_ref, rn_ref, rs_ref, st_ref, R):
        @pl.when(pl.program_id(0) == 0)
        def _():
            R[...] = jnp.zeros_like(R)

        lane16 = lax.broadcasted_iota(jnp.int32, (BLK, 128), 1) // 16
        rs_all = jnp.zeros((BLK, 128), F32)
        for h in range(RET_HEADS):
            hs = slice(h * 128, (h + 1) * 128)
            vs = slice((h % 2) * 256, (h % 2 + 1) * 256)
            os_ = slice(h * 256, (h + 1) * 256)
            q, k = q_ref[:, hs], k_ref[:, hs]
            v = (v0_ref if h < 2 else v1_ref)[:, vs]
            Rb = R[h].astype(BF16)
            st_ref[h] = Rb
            s = _dot_nt(q, k) * dm_ref[h]
            o = _dot(s.astype(BF16), v) + _dot((q.astype(F32) * x_ref[h]).astype(BF16), Rb)
            kz = (k.astype(F32) * z_ref[h]).astype(BF16)
            R[h] = R[h] * dec_ref[h, 0:1, :] + _dot_tn(kz, v)
            mu = jnp.mean(o, axis=-1, keepdims=True)
            oc = o - mu
            rstd = lax.rsqrt(jnp.mean(oc * oc, axis=-1, keepdims=True) + NORM_EPS)
            rn = oc * rstd
            gr = (g0_ref if h < 2 else g1_ref)[:, vs].astype(F32)
            y_ref[:, os_] = (rn * gr * _sigmoid(gr)).astype(BF16)
            rn_ref[:, os_] = rn.astype(BF16)
            rs_all = jnp.where(lane16 == h, rstd, rs_all)
        rs_ref[...] = rs_all

    cst = lambda shape: pl.BlockSpec(shape, lambda c: (0, 0, 0))
    blk = lambda j: pl.BlockSpec((BLK, 512), lambda c: (c, j))
    return pl.pallas_call(
        body,
        out_shape=(SDS((S, 1024), BF16), SDS((S, 1024), BF16), SDS((S, 128), F32), SDS((RET_HEADS, nc, BLK, 256), BF16)),
        grid=(nc,),
        in_specs=[blk(QR_B), blk(KR_B), blk(11), blk(12), blk(13), blk(14),
                  cst((RET_HEADS, BLK, BLK)), cst((RET_HEADS, BLK, 128)), cst((RET_HEADS, BLK, 128)), cst((RET_HEADS, 8, 256))],
        out_specs=(pl.BlockSpec((BLK, 1024), lambda c: (c, 0)), pl.BlockSpec((BLK, 1024), lambda c: (c, 0)),
                   pl.BlockSpec((BLK, 128), lambda c: (c, 0)),
                   pl.BlockSpec((RET_HEADS, None, BLK, 256), lambda c: (0, c, 0, 0))),
        scratch_shapes=[pltpu.VMEM((RET_HEADS, BLK, 256), F32)],
        compiler_params=_cparams("arbitrary"), name="ret_fwd")(proj, proj, proj, proj, proj, proj, dmask, zeta, xi, dec)


def _branch_merge(att, yrin, proj, wa, wr):
    S = att.shape[0]
    tm = min(S, 2048)

    def body(a_ref, y_ref, ga_ref, gr_ref, wa_ref, wr_ref, m_ref, ya_ref, yr_ref):
        ya = _dot(a_ref[...], wa_ref[...])
        yr = _dot(y_ref[...], wr_ref[...])
        m_ref[...] = (_sigmoid(ga_ref[...].astype(F32)) * ya + _sigmoid(gr_ref[...].astype(F32)) * yr).astype(BF16)
        ya_ref[...] = ya.astype(BF16)
        yr_ref[...] = yr.astype(BF16)

    ospec = pl.BlockSpec((tm, 512), lambda i, j: (i, j))
    return pl.pallas_call(
        body, out_shape=(SDS((S, D_MODEL), BF16),) * 3, grid=(S // tm, 2),
        in_specs=[pl.BlockSpec((tm, 512), lambda i, j: (i, 0)), pl.BlockSpec((tm, 1024), lambda i, j: (i, 0)),
                  pl.BlockSpec((tm, 512), lambda i, j: (i, 15 + j)), pl.BlockSpec((tm, 512), lambda i, j: (i, 17 + j)),
                  pl.BlockSpec((512, 512), lambda i, j: (0, j)), pl.BlockSpec((1024, 512), lambda i, j: (0, j))],
        out_specs=(ospec, ospec, ospec),
        compiler_params=_cparams("parallel", "arbitrary"), name="branch_merge")(att, yrin, proj, proj, wa, wr)


def _out_proj(merged, wo, x, g2):
    S = x.shape[0]
    tm = 1024

    def body(m_ref, w_ref, x_ref, g_ref, x1_ref, h2_ref):
        x1 = x_ref[...] + _dot(m_ref[...], w_ref[...])
        x1_ref[...] = x1
        r = lax.rsqrt(jnp.mean(x1 * x1, axis=-1, keepdims=True) + NORM_EPS)
        h2_ref[...] = (x1 * r * g_ref[...]).astype(BF16)

    row = pl.BlockSpec((tm, D_MODEL), lambda i: (i, 0))
    return pl.pallas_call(
        body, out_shape=(SDS((S, D_MODEL), F32), SDS((S, D_MODEL), BF16)), grid=(S // tm,),
        in_specs=[row, pl.BlockSpec((D_MODEL, D_MODEL), lambda i: (0, 0)), row, pl.BlockSpec((1, D_MODEL), lambda i: (0, 0))],
        out_specs=(row, row), compiler_params=_cparams("parallel"), name="out_proj")(merged, wo, x, g2)


def _ffn_up(h2, wg, wu):
    S = h2.shape[0]
    tm = min(S, 2048)

    def body(h_ref, wg_ref, wu_ref, g_ref, u_ref, a_ref):
        hv = h_ref[...]
        g = _dot(hv, wg_ref[...])
        u = _dot(hv, wu_ref[...])
        g_ref[...] = g.astype(BF16)
        u_ref[...] = u.astype(BF16)
        a_ref[...] = (g * _sigmoid(g) * u).astype(BF16)

    wspec = pl.BlockSpec((None, D_MODEL, HID_S), lambda i, s: (s, 0, 0))
    ospec = pl.BlockSpec((None, tm, HID_S), lambda i, s: (s, i, 0))
    return pl.pallas_call(
        body, out_shape=(SDS((N_SHARD, S, HID_S), BF16),) * 3, grid=(S // tm, N_SHARD),
        in_specs=[pl.BlockSpec((tm, D_MODEL), lambda i, s: (i, 0)), wspec, wspec],
        out_specs=(ospec, ospec, ospec),
        compiler_params=_cparams("parallel", "arbitrary"), name="ffn_up")(h2, wg, wu)


def _ffn_down_loss(act, wd, x1, g3, tgt):
    S = x1.shape[0]
    tm = 1024

    def body(a_ref, w_ref, x_ref, g_ref, t_ref, dx_ref, dxb_ref, dg_ref, ls_ref, acc):
        i, s = pl.program_id(0), pl.program_id(1)

        @pl.when(s == 0)
        def _():
            acc[...] = jnp.zeros_like(acc)

        @pl.when((i == 0) & (s == 0))
        def _():
            dg_ref[...] = jnp.zeros_like(dg_ref)
            ls_ref[...] = jnp.zeros_like(ls_ref)

        acc[...] += _dot(a_ref[...], w_ref[...])

        @pl.when(s == N_SHARD - 1)
        def _():
            x2 = x_ref[...] + acc[...]
            r = lax.rsqrt(jnp.mean(x2 * x2, axis=-1, keepdims=True) + NORM_EPS)
            xh = x2 * r
            g = g_ref[...]
            err = xh * g - t_ref[...]
            ls_ref[...] += jnp.sum(jnp.sum(err * err, axis=-1, keepdims=True), axis=0, keepdims=True) * (0.5 / D_MODEL)
            dy = err * (1.0 / D_MODEL)
            dg_ref[...] += jnp.sum(dy * xh, axis=0, keepdims=True)
            dxh = dy * g
            dx = r * (dxh - xh * jnp.mean(dxh * xh, axis=-1, keepdims=True))
            dx_ref[...] = dx
            dxb_ref[...] = dx.astype(BF16)

    row = pl.BlockSpec((tm, D_MODEL), lambda i, s: (i, 0))
    vec = pl.BlockSpec((1, D_MODEL), lambda i, s: (0, 0))
    return pl.pallas_call(
        body, out_shape=(SDS((S, D_MODEL), F32), SDS((S, D_MODEL), BF16), SDS((1, D_MODEL), F32), SDS((8, 128), F32)),
        grid=(S // tm, N_SHARD),
        in_specs=[pl.BlockSpec((None, tm, HID_S), lambda i, s: (s, i, 0)),
                  pl.BlockSpec((None, HID_S, D_MODEL), lambda i, s: (s, 0, 0)), row, vec, row],
        out_specs=(row, row, vec, pl.BlockSpec((8, 128), lambda i, s: (0, 0))),
        scratch_shapes=[pltpu.VMEM((tm, D_MODEL), F32)],
        compiler_params=_cparams("arbitrary", "arbitrary"), name="ffn_down_loss")(act, wd, x1, g3, tgt)


def _ffn_down_bwd(dx2b, wd, gte, up):
    S = dx2b.shape[0]
    tm = min(S, 2048)

    def body(d_ref, w_ref, g_ref, u_ref, dg_ref, du_ref):
        da = _dot_nt(d_ref[...], w_ref[...])
        g = g_ref[...].astype(F32)
        sg = _sigmoid(g)
        dg_ref[...] = (da * u_ref[...].astype(F32) * sg * (1.0 + g * (1.0 - sg))).astype(BF16)
        du_ref[...] = (da * g * sg).astype(BF16)

    aspec = pl.BlockSpec((None, tm, HID_S), lambda i, s: (s, i, 0))
    return pl.pallas_call(
        body, out_shape=(SDS((N_SHARD, S, HID_S), BF16),) * 2, grid=(S // tm, N_SHARD),
        in_specs=[pl.BlockSpec((tm, D_MODEL), lambda i, s: (i, 0)),
                  pl.BlockSpec((None, HID_S, D_MODEL), lambda i, s: (s, 0, 0)), aspec, aspec],
        out_specs=(aspec, aspec),
        compiler_params=_cparams("parallel", "arbitrary"), name="ffn_down_bwd")(dx2b, wd, gte, up)


def _wgrad(name, a, b, a_spec, b_spec, out_shape, out_spec, n_par, S):
    tk = 1024

    def body(a_ref, b_ref, o_ref):
        @pl.when(pl.program_id(1) == 0)
        def _():
            o_ref[...] = jnp.zeros_like(o_ref)

        o_ref[...] += _dot_tn(a_ref[...], b_ref[...])

    return pl.pallas_call(
        body, out_shape=SDS(out_shape, F32), grid=(n_par, S // tk),
        in_specs=[a_spec(tk), b_spec(tk)], out_specs=out_spec,
        compiler_params=_cparams("parallel", "arbitrary"), name=name)(a, b)


def _ffn_up_bwd(dgte, dup, wg, wu, x1, g2, dx2, exchanges=()):
    S = x1.shape[0]
    tm = 1024

    def body(dg_ref, du_ref, wg_ref, wu_ref, x_ref, g_ref, dx2_ref, dx_ref, dxb_ref, dgn_ref, acc):
        i, s = pl.program_id(0), pl.program_id(1)

        @pl.when(s == 0)
        def _():
            acc[...] = jnp.zeros_like(acc)

        @pl.when((i == 0) & (s == 0))
        def _():
            dgn_ref[...] = jnp.zeros_like(dgn_ref)

        acc[...] += _dot_nt(dg_ref[...], wg_ref[...]) + _dot_nt(du_ref[...], wu_ref[...])

        @pl.when(s == N_SHARD - 1)
        def _():
            xv = x_ref[...]
            r = lax.rsqrt(jnp.mean(xv * xv, axis=-1, keepdims=True) + NORM_EPS)
            xh = xv * r
            dh = acc[...]
            dgn_ref[...] += jnp.sum(dh * xh, axis=0, keepdims=True)
            dxh = dh * g_ref[...]
            dx = dx2_ref[...] + r * (dxh - xh * jnp.mean(dxh * xh, axis=-1, keepdims=True))
            dx_ref[...] = dx
            dxb_ref[...] = dx.astype(BF16)

    row = pl.BlockSpec((tm, D_MODEL), lambda i, s: (i, 0))
    vec = pl.BlockSpec((1, D_MODEL), lambda i, s: (0, 0))
    aspec = pl.BlockSpec((None, tm, HID_S), lambda i, s: (s, i, 0))
    wspec = pl.BlockSpec((None, D_MODEL, HID_S), lambda i, s: (s, 0, 0))
    return _carrier_call(
        body, (dgte, dup, wg, wu, x1, g2, dx2),
        out_shape=(SDS((S, D_MODEL), F32), SDS((S, D_MODEL), BF16), SDS((1, D_MODEL), F32)),
        grid=(S // tm, N_SHARD),
        in_specs=[aspec, aspec, wspec, wspec, row, vec, row], out_specs=(row, row, vec),
        scratch_shapes=[pltpu.VMEM((tm, D_MODEL), F32)],
        sem=("arbitrary", "arbitrary"), name="ffn_up_bwd", exchanges=exchanges)


def _out_proj_bwd(dx1b, wo, proj, ya, yr):
    S = dx1b.shape[0]
    tm = min(S, 2048)

    def body(d_ref, w_ref, ga_ref, gr_ref, ya_ref, yr_ref, dya_ref, dyr_ref, dga_ref, dgr_ref):
        dm = _dot_nt(d_ref[...], w_ref[...])
        sa = _sigmoid(ga_ref[...].astype(F32))
        sr = _sigmoid(gr_ref[...].astype(F32))
        dya_ref[...] = (dm * sa).astype(BF16)
        dyr_ref[...] = (dm * sr).astype(BF16)
        dga_ref[...] = (dm * ya_ref[...].astype(F32) * sa * (1.0 - sa)).astype(BF16)
        dgr_ref[...] = (dm * yr_ref[...].astype(F32) * sr * (1.0 - sr)).astype(BF16)

    blk = pl.BlockSpec((tm, 512), lambda i, j: (i, j))
    return pl.pallas_call(
        body, out_shape=(SDS((S, D_MODEL), BF16),) * 4, grid=(S // tm, 2),
        in_specs=[pl.BlockSpec((tm, D_MODEL), lambda i, j: (i, 0)), pl.BlockSpec((512, D_MODEL), lambda i, j: (j, 0)),
                  pl.BlockSpec((tm, 512), lambda i, j: (i, 15 + j)), pl.BlockSpec((tm, 512), lambda i, j: (i, 17 + j)),
                  blk, blk],
        out_specs=(blk,) * 4,
        compiler_params=_cparams("parallel", "arbitrary"), name="out_proj_bwd")(dx1b, wo, proj, proj, ya, yr)


def _branch_bwd(dya, dyr, wa, wr, att):
    S = dya.shape[0]
    tm = 1024

    def body(da_ref, dr_ref, wa_ref, wr_ref, att_ref, datt_ref, rho_ref, dyi_ref):
        datt = _dot_nt(da_ref[...], wa_ref[...])
        datt_ref[...] = datt.astype(BF16)
        dyi_ref[...] = _dot_nt(dr_ref[...], wr_ref[...]).astype(BF16)
        prod = datt * att_ref[...].astype(F32)
        lane = lax.broadcasted_iota(jnp.int32, (tm, 128), 1)
        lo = lane < 64
        rho = jnp.zeros((tm, 128), F32)
        for c in range(4):
            pc = prod[:, c * 128:(c + 1) * 128]
            tot = jnp.sum(pc, axis=-1, keepdims=True)
            low = jnp.sum(jnp.where(lo, pc, 0.0), axis=-1, keepdims=True)
            rho = jnp.where(lane // 16 == 2 * c, low, jnp.where(lane // 16 == 2 * c + 1, tot - low, rho))
        rho_ref[...] = rho

    row = lambda w: pl.BlockSpec((tm, w), lambda i: (i, 0))
    return pl.pallas_call(
        body, out_shape=(SDS((S, 512), BF16), SDS((S, 128), F32), SDS((S, 1024), BF16)), grid=(S // tm,),
        in_specs=[row(1024), row(1024), pl.BlockSpec((512, 1024), lambda i: (0, 0)),
                  pl.BlockSpec((1024, 1024), lambda i: (0, 0)), row(512)],
        out_specs=(row(512), row(128), row(1024)),
        compiler_params=_cparams("parallel"), name="branch_bwd")(dya, dyr, wa, wr, att)


def _attn_bwd(qkv, datt, lse, rho, rtab, d, gi, exchanges=()):
    L = qkv.shape[0]
    nb = L // BLK

    def body(q_ref, kc_ref, kp_ref, vc_ref, vp_ref, do_ref, lse_ref, rho_ref, tq_ref, tk_ref,
             dq_ref, dk_ref, dv_ref, ck, cv):
        n = pl.program_id(1)

        @pl.when(n == 0)
        def _():
            ck[...] = jnp.zeros_like(ck)
            cv[...] = jnp.zeros_like(cv)

        def store_rot(ref, val, t_ref, c):
            sl = slice(c * 128, (c + 1) * 128)
            ref[:, sl] = _unrot(val, t_ref[0], t_ref[1], t_ref[2], 32).astype(BF16)

        @pl.when(n < nb)
        def _():
            mask = _band_mask(n)
            mask2 = jnp.concatenate([mask, mask], axis=0)
            lo = lax.broadcasted_iota(jnp.int32, (BLK, 128), 1) < 64

            def stacked(a):
                return jnp.concatenate([jnp.where(lo, a, jnp.zeros_like(a)), jnp.where(lo, jnp.zeros_like(a), a)], axis=0)

            def head_cols(ref, c):
                return jnp.concatenate([jnp.broadcast_to(ref[:, 32 * c:32 * c + 1], (BLK, 2 * BLK)),
                                        jnp.broadcast_to(ref[:, 32 * c + 16:32 * c + 17], (BLK, 2 * BLK))], axis=0)

            for c in range(4):
                sl = slice(c * 128, (c + 1) * 128)
                q2, do2 = stacked(q_ref[:, sl]), stacked(do_ref[:, sl])
                k = jnp.concatenate([kp_ref[:, sl], kc_ref[:, sl]], axis=0)
                v = jnp.concatenate([vp_ref[:, sl], vc_ref[:, sl]], axis=0)
                s = _dot_nt(q2, k) * 0.125
                p = jnp.where(mask2, jnp.exp(s - head_cols(lse_ref, c)), 0.0)
                dp = _dot_nt(do2, v)
                ds = (p * (dp - head_cols(rho_ref, c)) * 0.125).astype(BF16)
                dq2 = _dot(ds, k)
                dq_c = jnp.where(lo, dq2[:BLK], dq2[BLK:])
                dk_c = _dot_tn(ds, q2)
                dv_c = _dot_tn(p.astype(BF16), do2)
                store_rot(dq_ref, dq_c, tq_ref, c)
                store_rot(dk_ref, ck[:, sl] + dk_c[:BLK], tk_ref, c)
                dv_ref[:, sl] = (cv[:, sl] + dv_c[:BLK]).astype(BF16)
                ck[:, sl] = dk_c[BLK:]
                cv[:, sl] = dv_c[BLK:]

        @pl.when(n == nb)
        def _():
            for c in range(4):
                sl = slice(c * 128, (c + 1) * 128)
                store_rot(dk_ref, ck[:, sl], tk_ref, c)
            dv_ref[...] = cv[...].astype(BF16)

    cur = lambda n: jnp.minimum(n, nb - 1)
    prev = lambda n: jnp.maximum(jnp.minimum(n, nb - 1) - 1, 0)
    fin = lambda n: jnp.maximum(n - 1, 0)
    return _carrier_call(
        body, (qkv, qkv, qkv, qkv, qkv, datt, lse, rho, rtab, rtab),
        out_shape=(SDS((L, d * 512), BF16),) * 3, grid=(d, nb + 1),
        in_specs=[pl.BlockSpec((BLK, 512), lambda r, n: (cur(n), 3 * r)),
                  pl.BlockSpec((BLK, 512), lambda r, n: (cur(n), 3 * r + 1)),
                  pl.BlockSpec((BLK, 512), lambda r, n: (prev(n), 3 * r + 1)),
                  pl.BlockSpec((BLK, 512), lambda r, n: (cur(n), 3 * r + 2)),
                  pl.BlockSpec((BLK, 512), lambda r, n: (prev(n), 3 * r + 2)),
                  pl.BlockSpec((BLK, 512), lambda r, n: (cur(n), r)),
                  pl.BlockSpec((BLK, 128), lambda r, n: (cur(n), r)),
                  pl.BlockSpec((BLK, 128), lambda r, n: (cur(n), r)),
                  pl.BlockSpec((3, BLK, 128), lambda r, n: (0, cur(n), r)),
                  pl.BlockSpec((3, BLK, 128), lambda r, n: (0, fin(n), r))],
        out_specs=(pl.BlockSpec((BLK, 512), lambda r, n: (cur(n), r)),
                   pl.BlockSpec((BLK, 512), lambda r, n: (fin(n), r)),
                   pl.BlockSpec((BLK, 512), lambda r, n: (fin(n), r))),
        scratch_shapes=[pltpu.VMEM((BLK, 512), F32), pltpu.VMEM((BLK, 512), F32)],
        sem=("parallel", "arbitrary"), name=f"attn_bwd_g{gi}", exchanges=exchanges)


def _ret_bwd(proj, rn, rstd, dyrin, states, tab, consts, exchanges=()):
    S = proj.shape[0]
    nc = S // BLK
    dmask, zeta, xi, dec = consts

    def body(q_ref, k_ref, v0_ref, v1_ref, g0_ref, g1_ref, rn_ref, rs_ref, dy_ref, st_ref, tq_ref, tk_ref,
             dm_ref, z_ref, x_ref, dec_ref, dq_ref, dk_ref, dv_ref, dgr_ref, dR):
        @pl.when(pl.program_id(0) == 0)
        def _():
            dR[...] = jnp.zeros_like(dR)

        for h in range(RET_HEADS):
            hs = slice(h * 128, (h + 1) * 128)
            vs = slice((h % 2) * 256, (h % 2 + 1) * 256)
            os_ = slice(h * 256, (h + 1) * 256)
            q, k = q_ref[:, hs], k_ref[:, hs]
            v = (v0_ref if h < 2 else v1_ref)[:, vs]
            gr = (g0_ref if h < 2 else g1_ref)[:, vs].astype(F32)
            sg = _sigmoid(gr)
            rn_v = rn_ref[:, os_].astype(F32)
            dyi = dy_ref[:, os_].astype(F32)
            dgr_ref[:, os_] = (dyi * rn_v * sg * (1.0 + gr * (1.0 - sg))).astype(BF16)
            drn = dyi * gr * sg
            rstd = jnp.broadcast_to(rs_ref[:, 16 * h:16 * h + 1], (BLK, 256))
            do = rstd * (drn - jnp.mean(drn, axis=-1, keepdims=True) - rn_v * jnp.mean(drn * rn_v, axis=-1, keepdims=True))
            dob = do.astype(BF16)
            Rb = st_ref[h]
            dRb = dR[h].astype(BF16)
            dm, zt, xt = dm_ref[h], z_ref[h], x_ref[h]
            sD = (_dot_nt(q, k) * dm).astype(BF16)
            kz = (k.astype(F32) * zt).astype(BF16)
            qx = (q.astype(F32) * xt).astype(BF16)
            dv_ref[:, os_] = (_dot_tn(sD, dob) + _dot(kz, dRb)).astype(BF16)
            dS = (_dot_nt(dob, v) * dm).astype(BF16)
            dq = _dot(dS, k) + _dot_nt(dob, Rb) * xt
            dk = _dot_tn(dS, q) + _dot_nt(v, dRb) * zt
            dR[h] = dR[h] * dec_ref[h, 0:1, :] + _dot_tn(qx, dob)
            dq_ref[:, hs] = _unrot(dq, tq_ref[0], tq_ref[1], tq_ref[2], 1).astype(BF16)
            dk_ref[:, hs] = _unrot(dk, tk_ref[0], tk_ref[1], tk_ref[2], 1).astype(BF16)

    rc = lambda c: nc - 1 - c
    cst = lambda shape: pl.BlockSpec(shape, lambda c: (0, 0, 0))
    blk = lambda j: pl.BlockSpec((BLK, 512), lambda c: (rc(c), j))
    row = lambda w: pl.BlockSpec((BLK, w), lambda c: (rc(c), 0))
    return _carrier_call(
        body, (proj, proj, proj, proj, proj, proj, rn, rstd, dyrin, states, tab, tab, dmask, zeta, xi, dec),
        out_shape=(SDS((S, 512), BF16), SDS((S, 512), BF16), SDS((S, 1024), BF16), SDS((S, 1024), BF16)),
        grid=(nc,),
        in_specs=[blk(QR_B), blk(KR_B), blk(11), blk(12), blk(13), blk(14), row(1024), row(128), row(1024),
                  pl.BlockSpec((RET_HEADS, None, BLK, 256), lambda c: (0, rc(c), 0, 0)),
                  pl.BlockSpec((None, 3, BLK, 128), lambda c: (1, 0, rc(c), 0)),
                  pl.BlockSpec((None, 3, BLK, 128), lambda c: (2, 0, rc(c), 0)),
                  cst((RET_HEADS, BLK, BLK)), cst((RET_HEADS, BLK, 128)), cst((RET_HEADS, BLK, 128)), cst((RET_HEADS, 8, 256))],
        out_specs=(row(512), row(512), row(1024), row(1024)),
        scratch_shapes=[pltpu.VMEM((RET_HEADS, BLK, 256), F32)],
        sem=("arbitrary",), name="ret_bwd", exchanges=exchanges)


def _in_proj_bwd(dproj, w_in, x, g1, dx1, part, gx_prev, dg_prev, exchanges=()):
    S = x.shape[0]
    tm = 512
    nh = S // tm // 2
    off = part * nh

    def body(d_ref, w_ref, x_ref, g_ref, dx1_ref, gxp_ref, dgp_ref, dx_ref, dgn_ref, acc):
        i, s = pl.program_id(0), pl.program_id(1)

        @pl.when(s == 0)
        def _():
            acc[...] = jnp.zeros_like(acc)

        @pl.when((i == 0) & (s == 0))
        def _():
            dgn_ref[...] = dgp_ref[...]

        acc[...] += _dot_nt(d_ref[...], w_ref[...])

        @pl.when(s == N_SHARD - 1)
        def _():
            xv = x_ref[...]
            r = lax.rsqrt(jnp.mean(xv * xv, axis=-1, keepdims=True) + NORM_EPS)
            xh = xv * r
            dh = acc[...]
            dgn_ref[...] += jnp.sum(dh * xh, axis=0, keepdims=True)
            dxh = dh * g_ref[...]
            dx_ref[...] = dx1_ref[...] + r * (dxh - xh * jnp.mean(dxh * xh, axis=-1, keepdims=True))

    row = pl.BlockSpec((tm, D_MODEL), lambda i, s: (i + off, 0))
    vec = pl.BlockSpec((1, D_MODEL), lambda i, s: (0, 0))
    (gx, dg), xres = _carrier_call(
        body, (dproj, w_in, x, g1, dx1, gx_prev, dg_prev),
        out_shape=(SDS((S, D_MODEL), F32), SDS((1, D_MODEL), F32)), grid=(nh, N_SHARD),
        in_specs=[pl.BlockSpec((tm, W_IN_S), lambda i, s: (i + off, s)),
                  pl.BlockSpec((D_MODEL, W_IN_S), lambda i, s: (0, s)), row, vec, row, ANY, vec],
        out_specs=(row, vec), scratch_shapes=[pltpu.VMEM((tm, D_MODEL), F32)],
        sem=("arbitrary", "arbitrary"), name=f"in_proj_bwd_{part}", exchanges=exchanges, in_out_aliases={5: 0})
    return gx, dg, xres


def _sub_view(a, d):
    S, W = a.shape
    return a.reshape(S // d, d * W)


def _step(x, tgt, g1, g2, g3, comm):
    S = x.shape[0]
    tab = _tables(S)
    consts = _ret_consts()

    h = _rms_fwd(x, g1)
    w_in = comm.w_in()
    proj, xres = _in_proj(h, w_in, tab, comm.carry("in_proj"))
    comm.took("in_proj", xres)
    qkvs, o_parts, lse_parts = [], [], []
    for gi, d in enumerate(DILATIONS):
        cols = [proj[:, (b + gi) * COLB:(b + gi + 1) * COLB] for b in (QA_B, KA_B, VA_B)]
        qkv = _sub_view(jnp.concatenate(cols, axis=1), d)
        (o_g, lse_g), xres = _attn_fwd(qkv, d, gi, comm.carry(f"attn_fwd_g{gi}"))
        comm.took(f"attn_fwd_g{gi}", xres)
        qkvs.append(qkv)
        o_parts.append(o_g.reshape(S, 512))
        lse_parts.append(lse_g.reshape(S, 128))
    att, lse_tot = _attn_merge(o_parts, lse_parts)
    yrin, rn, rstd, states = _ret_fwd(proj, consts)
    wa, wr, wo, wg, wu, wd = comm.w_rest()
    merged, ya, yr = _branch_merge(att, yrin, proj, wa, wr)
    x1, h2 = _out_proj(merged, wo, x, g2)
    gte, up, act = _ffn_up(h2, wg, wu)
    dx2, dx2b, dg3, loss_p = _ffn_down_loss(act, wd, x1, g3, tgt)

    dgte, dup = _ffn_down_bwd(dx2b, wd, gte, up)
    tok3 = lambda w: (lambda tk: pl.BlockSpec((None, tk, w), lambda p, k: (p, k, 0)))
    tok2 = lambda w: (lambda tk: pl.BlockSpec((tk, w), lambda p, k: (k, 0)))
    g_d = _wgrad("wgrad_down", act, dx2b, tok3(HID_S), tok2(D_MODEL), (N_SHARD, HID_S, D_MODEL),
                 pl.BlockSpec((None, HID_S, D_MODEL), lambda p, k: (p, 0, 0)), N_SHARD, S)
    g_g = _wgrad("wgrad_gate", h2, dgte, tok2(D_MODEL), tok3(HID_S), (N_SHARD, D_MODEL, HID_S),
                 pl.BlockSpec((None, D_MODEL, HID_S), lambda p, k: (p, 0, 0)), N_SHARD, S)
    g_u = _wgrad("wgrad_up", h2, dup, tok2(D_MODEL), tok3(HID_S), (N_SHARD, D_MODEL, HID_S),
                 pl.BlockSpec((None, D_MODEL, HID_S), lambda p, k: (p, 0, 0)), N_SHARD, S)
    comm.grads({4: g_g, 5: g_u, 6: g_d})
    (dx1, dx1b, dg2), xres = _ffn_up_bwd(dgte, dup, wg, wu, x1, g2, dx2, comm.carry("ffn_up_bwd"))
    comm.took("ffn_up_bwd", xres)
    dya, dyr, dga, dgrr = _out_proj_bwd(dx1b, wo, proj, ya, yr)
    colblk = lambda w: (lambda tk: pl.BlockSpec((tk, w), lambda p, k: (k, p)))
    g_o = _wgrad("wgrad_out", merged, dx1b, colblk(256), tok2(D_MODEL), (D_MODEL, D_MODEL),
                 pl.BlockSpec((256, D_MODEL), lambda p, k: (p, 0)), 4, S)
    datt, rho, dyrin = _branch_bwd(dya, dyr, wa, wr, att)
    g_a = _wgrad("wgrad_attn", att, dya, tok2(512), colblk(512), (512, D_MODEL),
                 pl.BlockSpec((512, 512), lambda p, k: (0, p)), 2, S)
    g_r = _wgrad("wgrad_ret", yrin, dyr, colblk(256), tok2(D_MODEL), (D_MODEL, D_MODEL),
                 pl.BlockSpec((256, D_MODEL), lambda p, k: (p, 0)), 4, S)
    comm.grads({1: g_a, 2: g_r.reshape(N_SHARD, 256, D_MODEL), 3: g_o.reshape(N_SHARD, 256, D_MODEL)})
    (dqr, dkr, dvr, dgr), xres = _ret_bwd(proj, rn, rstd, dyrin, states, tab, consts, comm.carry("ret_bwd"))
    comm.took("ret_bwd", xres)
    dqs, dks, dvs = [], [], []
    for gi, d in enumerate(DILATIONS):
        rtab = tab[0].reshape(3, S // d, d * 128)
        (dq, dk, dv), xres = _attn_bwd(qkvs[gi], _sub_view(datt, d), _sub_view(lse_tot, d), _sub_view(rho, d), rtab, d, gi,
                                       comm.carry(f"attn_bwd_g{gi}"))
        comm.took(f"attn_bwd_g{gi}", xres)
        dqs.append(dq.reshape(S, 512))
        dks.append(dk.reshape(S, 512))
        dvs.append(dv.reshape(S, 512))
    dproj = jnp.concatenate(dqs + dks + dvs + [dqr, dkr, dvr, dgr, dga, dgrr], axis=1)
    g_in = _wgrad("wgrad_in", h, dproj, tok2(D_MODEL), colblk(COLB), (D_MODEL, PROJ_W),
                  pl.BlockSpec((D_MODEL, COLB), lambda p, k: (0, p)), N_COLB, S)
    comm.grads({0: g_in})
    grad_x, dg1 = lax.empty((S, D_MODEL), F32), jnp.zeros((1, D_MODEL), F32)
    for part in range(2):
        grad_x, dg1, xres = _in_proj_bwd(dproj, w_in, x, g1, dx1, part, grad_x, dg1, comm.carry(f"in_proj_bwd_{part}"))
        comm.took(f"in_proj_bwd_{part}", xres)
    return loss_p[0, 0], grad_x, (dg1, dg2, dg3)


W_KINDS = ("col", "col", "lead", "lead", "lead", "lead", "lead")
W_SHARD = ((1024, W_IN_S), (512, 256), (256, 1024), (256, 1024), (1024, HID_S), (1024, HID_S), (HID_S, 1024))
N_W = len(W_KINDS)


def _full_shape(wi):
    R, C = W_SHARD[wi]
    return (R, N_SHARD * C) if W_KINDS[wi] == "col" else (N_SHARD, R, C)


def _view(ref, wi, s, half):
    R, C = W_SHARD[wi]
    rows = pl.ds(half * (R // 2), R // 2)
    if W_KINDS[wi] == "col":
        return ref.at[rows, pl.ds(pl.multiple_of(s * C, 128), C)]
    return ref.at[s, rows, :]


def _mesh_pos():
    x, y, c = lax.axis_index("x"), lax.axis_index("y"), lax.axis_index("c")
    chips = [(1 - x, y), (x, 1 - y), (1 - x, 1 - y)]
    return x, y, c, chips


def _cast_bf16(a):
    R, C = a.shape
    tr = R // 2 if R % 32 == 0 else R

    def body(a_ref, o_ref):
        o_ref[...] = a_ref[...].astype(BF16)

    spec = pl.BlockSpec((tr, C), lambda i: (i, 0))
    return pl.pallas_call(body, out_shape=SDS((R, C), BF16), grid=(R // tr,), in_specs=[spec], out_specs=spec,
                          compiler_params=_cparams("parallel"), name=f"cast_{R}x{C}")(a)


def _remote(send, recv, k, src, dst, to):
    return pltpu.make_async_remote_copy(src_ref=src, dst_ref=dst, send_sem=send.at[k], recv_sem=recv.at[k],
                                        device_id=to, device_id_type=MESH)


def _gather_now(wis, shards):
    n = len(wis)

    def body(*refs):
        sh, full = refs[:n], refs[n:2 * n]
        send, recv, loc = refs[2 * n:]
        x, y, c, chips = _mesh_pos()
        s_me = 2 * x + y
        sib = (x, y, 1 - c)
        own, started = [], []
        for i, wi in enumerate(wis):
            Rh = W_SHARD[wi][0] // 2
            for hf in range(2):
                cp = pltpu.make_async_copy(sh[i].at[pl.ds(hf * Rh, Rh), :], _view(full[i], wi, s_me, hf), loc.at[2 * i + hf])
                cp.start()
                own.append(cp)
            for j, chip in enumerate(chips):
                cp = _remote(send, recv, 3 * i + j, sh[i].at[pl.ds(c * Rh, Rh), :], _view(full[i], wi, s_me, c), (*chip, c))
                cp.start()
                started.append(cp)
        for i, wi in enumerate(wis):
            for j, chip in enumerate(chips):
                land = _view(full[i], wi, 2 * chip[0] + chip[1], c)
                _remote(send, recv, 3 * i + j, land, land, (*chip, c)).wait_recv()
                fw = _remote(send, recv, 3 * n + 3 * i + j, land, land, sib)
                fw.start()
                started.append(fw)
        for i, wi in enumerate(wis):
            for j, chip in enumerate(chips):
                land = _view(full[i], wi, 2 * chip[0] + chip[1], 1 - c)
                _remote(send, recv, 3 * n + 3 * i + j, land, land, sib).wait_recv()
        for cp in started:
            cp.wait_send()
        for cp in own:
            cp.wait()

    return pl.pallas_call(
        body, out_shape=tuple(SDS(_full_shape(wi), BF16) for wi in wis),
        in_specs=[ANY] * n, out_specs=tuple([ANY] * n),
        scratch_shapes=[pltpu.SemaphoreType.DMA((6 * n,)), pltpu.SemaphoreType.DMA((6 * n,)),
                        pltpu.SemaphoreType.DMA((2 * n,))],
        name="gather_now")(*shards)


def _ex_gather_ici(wis, shards):
    def build(ins, outs, send, recv, loc):
        x, y, c, chips = _mesh_pos()
        s_me = 2 * x + y
        starts, waits = [], []
        for i, wi in enumerate(wis):
            Rh = W_SHARD[wi][0] // 2
            for hf in range(2):
                cp = pltpu.make_async_copy(ins[i].at[pl.ds(hf * Rh, Rh), :], _view(outs[i], wi, s_me, hf), loc.at[2 * i + hf])
                starts.append(cp)
                waits.append(cp.wait)
            for j, chip in enumerate(chips):
                cp = _remote(send, recv, 3 * i + j, ins[i].at[pl.ds(c * Rh, Rh), :], _view(outs[i], wi, s_me, c), (*chip, c))
                land = _view(outs[i], wi, 2 * chip[0] + chip[1], c)
                starts.append(cp)
                waits += [cp.wait_send, _remote(send, recv, 3 * i + j, land, land, (*chip, c)).wait_recv]
        return starts, waits

    return _Exchange(shards, [SDS(_full_shape(wi), BF16) for wi in wis], {}, 3 * len(wis), 2 * len(wis), build)


def _ex_gather_d2d(wis, fulls):
    def build(ins, outs, send, recv, loc):
        x, y, c, chips = _mesh_pos()
        sib = (x, y, 1 - c)
        starts, waits = [], []
        for i, wi in enumerate(wis):
            for j, chip in enumerate(chips):
                mine = _view(outs[i], wi, 2 * chip[0] + chip[1], c)
                theirs = _view(outs[i], wi, 2 * chip[0] + chip[1], 1 - c)
                cp = _remote(send, recv, 3 * i + j, mine, mine, sib)
                starts.append(cp)
                waits += [cp.wait_send, _remote(send, recv, 3 * i + j, theirs, theirs, sib).wait_recv]
        return starts, waits

    return _Exchange(fulls, [SDS(f.shape, BF16) for f in fulls], {i: i for i in range(len(wis))}, 3 * len(wis), 0, build)


def _half_shape(wi):
    R, C = W_SHARD[wi]
    return (R // 2, N_SHARD * C) if W_KINDS[wi] == "col" else (N_SHARD, R // 2, C)


def _ex_pair(wis, grads):
    def build(ins, outs, send, recv, loc):
        x, y, c, _ = _mesh_pos()
        starts, waits = [], []
        for i, wi in enumerate(wis):
            Rh = W_SHARD[wi][0] // 2
            rows = pl.ds((1 - c) * Rh, Rh)
            src = ins[i].at[rows, :] if W_KINDS[wi] == "col" else ins[i].at[:, rows, :]
            cp = _remote(send, recv, i, src, outs[i], (x, y, 1 - c))
            starts.append(cp)
            waits.append(cp.wait)
        return starts, waits

    return _Exchange(grads, [SDS(_half_shape(wi), F32) for wi in wis], {}, len(wis), 0, build)


def _ex_chip(wis, pbs):
    def build(ins, outs, send, recv, loc):
        x, y, c, chips = _mesh_pos()
        starts, waits = [], []
        for i, wi in enumerate(wis):
            for j, chip in enumerate(chips):
                cp = _remote(send, recv, 3 * i + j, ins[i].at[j], outs[i].at[j], (*chip, c))
                starts.append(cp)
                waits.append(cp.wait)
        return starts, waits

    shapes = [SDS((3, W_SHARD[wi][0] // 2, W_SHARD[wi][1]), BF16) for wi in wis]
    return _Exchange(pbs, shapes, {}, 3 * len(wis), 0, build)


def _ex_share(wis, halves):
    def build(ins, outs, send, recv, loc):
        x, y, c, _ = _mesh_pos()
        sib = (x, y, 1 - c)
        starts, waits = [], []
        for i, wi in enumerate(wis):
            cp = _remote(send, recv, i, outs[i].at[c], outs[i].at[c], sib)
            starts.append(cp)
            waits += [cp.wait_send, _remote(send, recv, i, outs[i].at[1 - c], outs[i].at[1 - c], sib).wait_recv]
        return starts, waits

    return _Exchange(halves, [SDS(h.shape, F32) for h in halves], {i: i for i in range(len(wis))}, len(wis), 0, build)


def _row_tile(rh, C):
    best = 16
    for t in range(16, rh + 1, 16):
        if rh % t == 0 and t * C * 4 <= (3 << 19):
            best = t
    return best


def _pair_sum(wi, g, ra, sidx):
    R, C = W_SHARD[wi]
    Rh = R // 2
    tr = _row_tile(Rh, C)
    nt = Rh // tr
    col = W_KINDS[wi] == "col"

    def body(sidx_ref, *refs):
        gs, rs = refs[:4], refs[4:8]
        own_ref, pb_ref = refs[8:]
        own_ref[...] = gs[0][...] + rs[0][...]
        for j in range(3):
            pb_ref[j] = (gs[1 + j][...] + rs[1 + j][...]).astype(BF16)

    def gspec(slot):
        if col:
            return pl.BlockSpec((tr, C), lambda i, sx: (sx[4] * nt + i, sx[slot]))
        return pl.BlockSpec((None, tr, C), lambda i, sx: (sx[slot], sx[4] * nt + i, 0))

    def rspec(slot):
        if col:
            return pl.BlockSpec((tr, C), lambda i, sx: (i, sx[slot]))
        return pl.BlockSpec((None, tr, C), lambda i, sx: (sx[slot], i, 0))

    return pl.pallas_call(
        body, out_shape=(SDS((Rh, C), F32), SDS((3, Rh, C), BF16)),
        grid_spec=pltpu.PrefetchScalarGridSpec(
            num_scalar_prefetch=1, grid=(nt,),
            in_specs=[gspec(k) for k in range(4)] + [rspec(k) for k in range(4)],
            out_specs=(pl.BlockSpec((tr, C), lambda i, sx: (i, 0)), pl.BlockSpec((3, tr, C), lambda i, sx: (0, i, 0)))),
        compiler_params=_cparams("arbitrary"), name=f"pair_sum_w{wi}")(sidx, g, g, g, g, ra, ra, ra, ra)


def _chip_sum(wi, own, rb, sidx):
    R, C = W_SHARD[wi]
    Rh = R // 2
    tr = _row_tile(Rh, C)

    def body(sidx_ref, own_ref, rb_ref, o_ref):
        o_ref[...] = ((own_ref[...] + rb_ref[0].astype(F32)) + rb_ref[1].astype(F32)) + rb_ref[2].astype(F32)

    return pl.pallas_call(
        body, out_shape=SDS((2, Rh, C), F32),
        grid_spec=pltpu.PrefetchScalarGridSpec(
            num_scalar_prefetch=1, grid=(Rh // tr,),
            in_specs=[pl.BlockSpec((tr, C), lambda i, sx: (i, 0)), pl.BlockSpec((3, tr, C), lambda i, sx: (0, i, 0))],
            out_specs=pl.BlockSpec((None, tr, C), lambda i, sx: (sx[4], i, 0))),
        compiler_params=_cparams("arbitrary"), name=f"chip_sum_w{wi}")(sidx, own, rb)


def _gain_allgather(blk):
    m_per, n = blk.shape

    def body(x_ref, out_ref, send_sems, recv_sems, local_sem):
        x, y, c, chips = _mesh_pos()
        me, sibling = (x, y, c), (x, y, 1 - c)

        def rows(px, py, pc):
            return out_ref.at[pl.ds((4 * px + 2 * py + pc) * m_per, m_per), :]

        def copy(k, block, to, src=None):
            return pltpu.make_async_remote_copy(
                src_ref=rows(*block) if src is None else src, dst_ref=rows(*block),
                send_sem=send_sems.at[k], recv_sem=recv_sems.at[k], device_id=to, device_id_type=MESH)

        mine = pltpu.make_async_copy(x_ref, rows(*me), local_sem)
        mine.start()
        first = [copy(0, me, sibling, src=x_ref)]
        first += [copy(1 + j, me, (*chip, c), src=x_ref) for j, chip in enumerate(chips)]
        for cp in first:
            cp.start()
        passed = [copy(4 + j, (*chip, c), sibling) for j, chip in enumerate(chips)]
        for j, chip in enumerate(chips):
            copy(1 + j, (*chip, c), me).wait_recv()
            passed[j].start()
        copy(0, sibling, me).wait_recv()
        for j, chip in enumerate(chips):
            copy(4 + j, (*chip, 1 - c), me).wait_recv()
        for cp in first + passed:
            cp.wait_send()
        mine.wait()

    vm = pl.BlockSpec(memory_space=pltpu.VMEM)
    return pl.pallas_call(
        body, out_shape=SDS((8 * m_per, n), blk.dtype), in_specs=[vm], out_specs=vm,
        scratch_shapes=[pltpu.SemaphoreType.DMA((7,)), pltpu.SemaphoreType.DMA((7,)), pltpu.SemaphoreType.DMA],
        name="gain_allgather")(blk)


def _adam_math(w, g, m, v):
    mn = ADAM_B1 * m + (1.0 - ADAM_B1) * g
    vn = ADAM_B2 * v + (1.0 - ADAM_B2) * (g * g)
    mh = mn / (1.0 - ADAM_B1 ** ADAM_STEP)
    vh = vn / (1.0 - ADAM_B2 ** ADAM_STEP)
    return -ADAM_LR * (mh / (jnp.sqrt(vh) + ADAM_EPS) + ADAM_WD * w), mn, vn


def _adamw(wi, w, g, m, v):
    R, C = w.shape
    tr = _row_tile(R, C)

    def body(w_ref, g_ref, m_ref, v_ref, d_ref, mn_ref, vn_ref):
        d_ref[...], mn_ref[...], vn_ref[...] = _adam_math(w_ref[...], g_ref[...], m_ref[...], v_ref[...])

    spec = pl.BlockSpec((tr, C), lambda i: (i, 0))
    return pl.pallas_call(body, out_shape=(SDS((R, C), F32),) * 3, grid=(R // tr,), in_specs=[spec] * 4,
                          out_specs=(spec,) * 3, compiler_params=_cparams("parallel"), name=f"adamw_w{wi}")(w, g, m, v)


def _gain_update(gathered, w, m, v):
    def body(ga_ref, w_ref, m_ref, v_ref, g_ref, d_ref, mn_ref, vn_ref):
        g = ga_ref[0:8, :]
        for dev in range(1, 8):
            g = g + ga_ref[8 * dev:8 * dev + 8, :]
        g_ref[...] = g
        d_ref[...], mn_ref[...], vn_ref[...] = _adam_math(w_ref[...], g, m_ref[...], v_ref[...])

    return pl.pallas_call(body, out_shape=(SDS((8, 1024), F32),) * 4, name="gain_update")(gathered, w, m, v)


GROUP_FFN, GROUP_MIX, GROUP_IN = (4, 5, 6), (1, 2, 3), (0,)
REST = GROUP_MIX + GROUP_FFN


class _MeshComm:
    def __init__(self, shards):
        xi, yi, ci = lax.axis_index("x"), lax.axis_index("y"), lax.axis_index("c")
        self.sidx = jnp.stack([2 * xi + yi, 2 * (1 - xi) + yi, 2 * xi + (1 - yi), 2 * (1 - xi) + (1 - yi), ci]).astype(jnp.int32)
        self.shards, self.fulls = shards, None
        self.g, self.own, self.pb, self.half, self.red = {}, {}, {}, {}, {}

    def w_in(self):
        return _gather_now(GROUP_IN, [self.shards[0]])[0]

    def w_rest(self):
        f = dict(zip(REST, self.fulls))
        return f[1], f[2].reshape(D_MODEL, D_MODEL), f[3].reshape(D_MODEL, D_MODEL), f[4], f[5], f[6]

    def grads(self, by_wi):
        self.g.update(by_wi)

    def _pick(self, table, wis):
        return [table[wi] for wi in wis]

    def _pair_sums(self, wis, ras):
        for wi, ra in zip(wis, ras):
            self.own[wi], self.pb[wi] = _pair_sum(wi, self.g[wi], ra, self.sidx)

    def _chip_sums(self, wis, rbs):
        for wi, rb in zip(wis, rbs):
            self.half[wi] = _chip_sum(wi, self.own[wi], rb, self.sidx)

    def carry(self, point):
        if point == "in_proj":
            return [_ex_gather_ici(REST, self._pick(self.shards, REST))]
        if point == "attn_fwd_g0":
            return [_ex_gather_d2d(REST, list(self.fulls))]
        if point == "ffn_up_bwd":
            return [_ex_pair(GROUP_FFN, self._pick(self.g, GROUP_FFN))]
        if point == "ret_bwd":
            return [_ex_chip(GROUP_FFN, self._pick(self.pb, GROUP_FFN))]
        if point == "attn_bwd_g0":
            return [_ex_pair(GROUP_MIX, self._pick(self.g, GROUP_MIX)), _ex_share(GROUP_FFN, self._pick(self.half, GROUP_FFN))]
        if point == "attn_bwd_g1":
            return [_ex_chip(GROUP_MIX, self._pick(self.pb, GROUP_MIX))]
        if point == "attn_bwd_g2":
            return [_ex_share(GROUP_MIX, self._pick(self.half, GROUP_MIX))]
        if point == "in_proj_bwd_0":
            return [_ex_pair(GROUP_IN, self._pick(self.g, GROUP_IN))]
        if point == "in_proj_bwd_1":
            return [_ex_chip(GROUP_IN, self._pick(self.pb, GROUP_IN))]
        return []

    def took(self, point, xres):
        if point in ("in_proj", "attn_fwd_g0"):
            self.fulls = xres[0]
        elif point == "ffn_up_bwd":
            self._pair_sums(GROUP_FFN, xres[0])
        elif point == "ret_bwd":
            self._chip_sums(GROUP_FFN, xres[0])
        elif point == "attn_bwd_g0":
            self._pair_sums(GROUP_MIX, xres[0])
            self.red.update(zip(GROUP_FFN, xres[1]))
        elif point == "attn_bwd_g1":
            self._chip_sums(GROUP_MIX, xres[0])
        elif point == "attn_bwd_g2":
            self.red.update(zip(GROUP_MIX, xres[0]))
        elif point == "in_proj_bwd_0":
            self._pair_sums(GROUP_IN, xres[0])
        elif point == "in_proj_bwd_1":
            self._chip_sums(GROUP_IN, xres[0])
            self.red.update(zip(GROUP_IN, _exchange_call(_ex_share(GROUP_IN, self._pick(self.half, GROUP_IN)), "share_w_in")))

    def reduced(self):
        return [self.red[wi] for wi in range(N_W)]


def kernel(x, norm_mix_g, w_in, w_out_attn, w_out_ret, w_out, norm_ffn_g, w_ffn_gate, w_ffn_up, w_ffn_down, norm_final_g, loss_target, m_norm_mix_g, m_w_in, m_w_out_attn, m_w_out_ret, m_w_out, m_norm_ffn_g, m_w_ffn_gate, m_w_ffn_up, m_w_ffn_down, m_norm_final_g, v_norm_mix_g, v_w_in, v_w_out_attn, v_w_out_ret, v_w_out, v_norm_ffn_g, v_w_ffn_gate, v_w_ffn_up, v_w_ffn_down, v_norm_final_g):
    ws = (w_in, w_out_attn, w_out_ret, w_out, w_ffn_gate, w_ffn_up, w_ffn_down)
    ms = (m_w_in, m_w_out_attn, m_w_out_ret, m_w_out, m_w_ffn_gate, m_w_ffn_up, m_w_ffn_down)
    vs = (v_w_in, v_w_out_attn, v_w_out_ret, v_w_out, v_w_ffn_gate, v_w_ffn_up, v_w_ffn_down)
    shard2d = lambda a, wi: a.reshape(W_SHARD[wi])

    comm = _MeshComm([_cast_bf16(shard2d(w, wi)) for wi, w in enumerate(ws)])
    g3 = norm_final_g.reshape(1, D_MODEL)
    loss_p, grad_x, gain_g = _step(x[0], loss_target[0], norm_mix_g, norm_ffn_g, g3, comm)
    gred = comm.reduced()

    outs_g, outs_d, outs_m, outs_v = [], [], [], []
    for wi in range(N_W):
        g2d = gred[wi].reshape(W_SHARD[wi])
        dlt, mn, vn = _adamw(wi, shard2d(ws[wi], wi), g2d, shard2d(ms[wi], wi), shard2d(vs[wi], wi))
        for lst, a in ((outs_g, g2d), (outs_d, dlt), (outs_m, mn), (outs_v, vn)):
            lst.append(a.reshape(ws[wi].shape))

    pad8 = lambda rows: jnp.concatenate([r.reshape(1, D_MODEL) for r in rows] + [jnp.zeros((5, D_MODEL), F32)], axis=0)
    gathered = _gain_allgather(pad8(gain_g))
    gg, gd, gm, gv = _gain_update(gathered, pad8((norm_mix_g, norm_ffn_g, norm_final_g)),
                                  pad8((m_norm_mix_g, m_norm_ffn_g, m_norm_final_g)),
                                  pad8((v_norm_mix_g, v_norm_ffn_g, v_norm_final_g)))
    loss = lax.psum(loss_p, ("x", "y", "c"))

    def assemble(gain_rows, wlist):
        return (gain_rows[0:1], wlist[0], wlist[1], wlist[2], wlist[3], gain_rows[1:2],
                wlist[4], wlist[5], wlist[6], gain_rows[2])

    return (loss, grad_x[None], *assemble(gg, outs_g), *assemble(gd, outs_d), *assemble(gm, outs_m), *assemble(gv, outs_v))
```

```python
import functools
import math

import numpy as np
import jax
import jax.numpy as jnp
from jax import lax
from jax.experimental import pallas as pl
from jax.experimental.pallas import tpu as pltpu

F32, BF16 = jnp.float32, jnp.bfloat16
SDS = jax.ShapeDtypeStruct
MESH = pl.DeviceIdType.MESH

D_MODEL = 1024
PROJ_W = 9728
COLB = 512
N_COLB = PROJ_W // COLB
QA_B, KA_B, VA_B = 0, 3, 6
QR_B, KR_B = 9, 10
FFN_HID = 2816
N_SHARD = 4
HID_S = FFN_HID // N_SHARD
W_IN_S = PROJ_W // N_SHARD
DILATIONS = (1, 4, 16)
BLK = 128
RET_HEADS = 4
ROPE_THETA = 10000.0
NORM_EPS = 1e-6
ADAM_LR, ADAM_B1, ADAM_B2, ADAM_EPS, ADAM_WD, ADAM_STEP = 0.001, 0.9, 0.999, 1e-08, 0.01, 10
VMEM_LIMIT = 56 << 20


def _cparams(*sem):
    return pltpu.CompilerParams(dimension_semantics=sem or None, vmem_limit_bytes=VMEM_LIMIT)


def _dot(a, b):
    return jnp.dot(a, b, preferred_element_type=F32)


def _dot_nt(a, b):
    return lax.dot_general(a, b, (((1,), (1,)), ((), ())), preferred_element_type=F32)


def _dot_tn(a, b):
    return lax.dot_general(a, b, (((0,), (0,)), ((), ())), preferred_element_type=F32)


def _sigmoid(z):
    return 1.0 / (1.0 + jnp.exp(-z))


ANY = pl.BlockSpec(memory_space=pl.ANY)


class _Exchange:
    def __init__(self, ins, out_shapes, aliases, n_sem, n_loc, build):
        self.ins, self.out_shapes, self.aliases = list(ins), list(out_shapes), dict(aliases)
        self.n_sem, self.n_loc, self.build = n_sem, n_loc, build

    def sems(self):
        return [pltpu.SemaphoreType.DMA((self.n_sem,)), pltpu.SemaphoreType.DMA((self.n_sem,)),
                pltpu.SemaphoreType.DMA((max(self.n_loc, 1),))]


def _exchange_call(ex, name):
    n_in, n_out = len(ex.ins), len(ex.out_shapes)

    def body(*refs):
        starts, waits = ex.build(refs[:n_in], refs[n_in:n_in + n_out], *refs[n_in + n_out:])
        for cp in starts:
            cp.start()
        for w in waits:
            w()

    return pl.pallas_call(body, out_shape=tuple(ex.out_shapes), in_specs=[ANY] * n_in, out_specs=tuple([ANY] * n_out),
                          input_output_aliases=ex.aliases, scratch_shapes=ex.sems(), name=name)(*ex.ins)


def _carrier_call(body, args, *, out_shape, grid, in_specs, out_specs, scratch_shapes=(), sem, name, exchanges=(),
                  prefetch=None):
    out_shape, out_specs = tuple(out_shape), tuple(out_specs)
    n_in, n_out, n_scr = len(args), len(out_shape), len(scratch_shapes)
    n_pre = 0 if prefetch is None else 1
    x_args, x_outs, aliases, x_scr, spans = [], [], {}, [], []
    for ex in exchanges:
        i0, o0 = len(x_args), len(x_outs)
        for a, o in ex.aliases.items():
            aliases[n_pre + n_in + i0 + a] = n_out + o0 + o
        x_args += ex.ins
        x_outs += ex.out_shapes
        x_scr += ex.sems()
        spans.append((i0, len(ex.ins), o0, len(ex.out_shapes)))
    nx_in, nx_out = len(x_args), len(x_outs)

    def wrapped(*refs):
        refs = refs[n_pre:]
        ins, xin = refs[:n_in], refs[n_in:n_in + nx_in]
        o_base = n_in + nx_in
        outs, xout = refs[o_base:o_base + n_out], refs[o_base + n_out:o_base + n_out + nx_out]
        s_base = o_base + n_out + nx_out
        scr, xs = refs[s_base:s_base + n_scr], refs[s_base + n_scr:]

        def built(e):
            i0, ni, o0, no = spans[e]
            return exchanges[e].build(xin[i0:i0 + ni], xout[o0:o0 + no], *xs[3 * e:3 * e + 3])

        if exchanges:
            first = functools.reduce(jnp.logical_and, [pl.program_id(k) == 0 for k in range(len(grid))])
            last = functools.reduce(jnp.logical_and, [pl.program_id(k) == grid[k] - 1 for k in range(len(grid))])

            @pl.when(first)
            def _():
                for e in range(len(exchanges)):
                    for cp in built(e)[0]:
                        cp.start()

        body(*ins, *outs, *scr)

        if exchanges:
            @pl.when(last)
            def _():
                for e in range(len(exchanges)):
                    for w in built(e)[1]:
                        w()

    all_in, all_out = list(in_specs) + [ANY] * nx_in, out_specs + tuple([ANY] * nx_out)
    all_scr = list(scratch_shapes) + x_scr
    cparams = _cparams(*(sem if not exchanges else ("arbitrary",) * len(grid)))
    if prefetch is None:
        res = pl.pallas_call(wrapped, out_shape=out_shape + tuple(x_outs), grid=grid, in_specs=all_in, out_specs=all_out,
                             scratch_shapes=all_scr, input_output_aliases=aliases, compiler_params=cparams,
                             name=name)(*args, *x_args)
    else:
        gs = pltpu.PrefetchScalarGridSpec(num_scalar_prefetch=1, grid=grid, in_specs=all_in, out_specs=all_out,
                                          scratch_shapes=all_scr)
        res = pl.pallas_call(wrapped, out_shape=out_shape + tuple(x_outs), grid_spec=gs, input_output_aliases=aliases,
                             compiler_params=cparams, name=name)(prefetch, *args, *x_args)
    xres = [tuple(res[n_out + o0:n_out + o0 + no]) for (_, _, o0, no) in spans]
    return tuple(res[:n_out]), xres


def _tables(S):
    pos = jnp.arange(S, dtype=F32)
    lane = np.arange(128)
    inv = ROPE_THETA ** (-jnp.arange(0, 64, 2, dtype=F32) / 64)
    ang = pos[:, None] * inv[None, :]
    idx = (lane % 64) % 32
    c, s = jnp.cos(ang)[:, idx], jnp.sin(ang)[:, idx]
    first = jnp.asarray((lane % 64) < 32)[None, :]
    rope = jnp.stack([c, jnp.where(first, 0.0, s), jnp.where(first, -s, 0.0)])
    base = 1.0 / (ROPE_THETA ** jnp.linspace(0.0, 1.0, 64, dtype=F32))
    ang2 = pos[:, None] * base[None, :]
    c2, s2 = jnp.cos(ang2)[:, lane // 2], jnp.sin(ang2)[:, lane // 2]
    even = jnp.asarray(lane % 2 == 0)[None, :]
    th = jnp.stack([c2, jnp.where(even, 0.0, s2), jnp.where(even, -s2, 0.0)])
    return jnp.stack([rope, th, th * (128 ** -0.5)]).astype(F32)


def _rot(a, c, sa, sb, shift):
    return a * c + pltpu.roll(a, shift, 1) * sa + pltpu.roll(a, 128 - shift, 1) * sb


def _unrot(g, c, sa, sb, shift):
    return g * c + pltpu.roll(g * sa, 128 - shift, 1) + pltpu.roll(g * sb, shift, 1)


def _ret_consts():
    h = np.arange(RET_HEADS, dtype=np.float64)
    log_g = np.log1p(-(2.0 ** (-5.0 - h)))
    idx = np.arange(BLK, dtype=np.float64)
    diff = idx[:, None] - idx[None, :]
    dmask = np.where(diff[None] >= 0, np.exp(np.maximum(diff, 0.0)[None] * log_g[:, None, None]), 0.0)
    zeta = np.exp((BLK - 1 - idx)[None, :] * log_g[:, None])
    xi = np.exp((idx + 1.0)[None, :] * log_g[:, None])
    dec = np.exp(BLK * log_g)
    rep = lambda v: np.broadcast_to(v[:, :, None], (RET_HEADS, BLK, 128))
    return (jnp.asarray(dmask, F32), jnp.asarray(rep(zeta), F32), jnp.asarray(rep(xi), F32),
            jnp.asarray(np.broadcast_to(dec[:, None, None], (RET_HEADS, 8, 256)), F32))


def _rms_fwd(x, g):
    S = x.shape[0]
    tm = 512

    def body(x_ref, g_ref, h_ref, ht_ref):
        xv = x_ref[...]
        r = lax.rsqrt(jnp.mean(xv * xv, axis=-1, keepdims=True) + NORM_EPS)
        h = xv * r * g_ref[...]
        h_ref[...] = h.astype(BF16)
        ht_ref[...] = h.T.astype(BF16)

    return pl.pallas_call(
        body, out_shape=(SDS((S, D_MODEL), BF16), SDS((D_MODEL, S), BF16)), grid=(S // tm,),
        in_specs=[pl.BlockSpec((tm, D_MODEL), lambda i: (i, 0)), pl.BlockSpec((1, D_MODEL), lambda i: (0, 0))],
        out_specs=(pl.BlockSpec((tm, D_MODEL), lambda i: (i, 0)), pl.BlockSpec((D_MODEL, tm), lambda i: (0, i))),
        compiler_params=_cparams("parallel"), name="rms_fwd")(x, g)


def _in_proj(h, w_in, tab, exchanges=()):
    S = h.shape[0]
    tm = min(S, 2048)

    def body(h_ref, w_ref, t_ref, o_ref):
        j = pl.program_id(1)
        acc = _dot(h_ref[...], w_ref[...])
        is_rope = j < 6
        is_theta = (j == QR_B) | (j == KR_B)

        def rotated(shift):
            c, sa, sb = t_ref[0, 0], t_ref[0, 1], t_ref[0, 2]
            for k in range(COLB // 128):
                sl = slice(k * 128, (k + 1) * 128)
                o_ref[:, sl] = _rot(acc[:, sl], c, sa, sb, shift).astype(BF16)

        @pl.when(is_rope)
        def _():
            rotated(32)

        @pl.when(is_theta)
        def _():
            rotated(1)

        @pl.when(jnp.logical_not(is_rope | is_theta))
        def _():
            o_ref[...] = acc.astype(BF16)

    def tab_map(i, j):
        return (jnp.where(j == QR_B, 1, jnp.where(j == KR_B, 2, 0)), 0, i, 0)

    (proj,), xres = _carrier_call(
        body, (h, w_in, tab), out_shape=(SDS((S, PROJ_W), BF16),), grid=(S // tm, N_COLB),
        in_specs=[pl.BlockSpec((tm, D_MODEL), lambda i, j: (i, 0)),
                  pl.BlockSpec((D_MODEL, COLB), lambda i, j: (0, j)),
                  pl.BlockSpec((1, 3, tm, 128), tab_map)],
        out_specs=(pl.BlockSpec((tm, COLB), lambda i, j: (i, j)),),
        sem=("parallel", "arbitrary"), name="in_proj", exchanges=exchanges)
    return proj, xres


def _band_mask(n):
    qi = lax.broadcasted_iota(jnp.int32, (BLK, 2 * BLK), 0)
    kj = lax.broadcasted_iota(jnp.int32, (BLK, 2 * BLK), 1)
    dist = BLK + qi - kj
    return (dist >= 0) & (dist <= BLK) & ((kj >= BLK) | (n > 0))


def _attn_fwd(qkv, d, gi, exchanges=()):
    L = qkv.shape[0]
    nb = L // BLK

    def body(q_ref, kc_ref, kp_ref, vc_ref, vp_ref, o_ref, lse_ref):
        n = pl.program_id(1)
        mask = _band_mask(n)
        mask2 = jnp.concatenate([mask, mask], axis=0)
        lane = lax.broadcasted_iota(jnp.int32, (BLK, 128), 1)
        lo = lane < 64
        lse_all = jnp.zeros((BLK, 128), F32)
        for c in range(4):
            sl = slice(c * 128, (c + 1) * 128)
            q = q_ref[:, sl]
            k = jnp.concatenate([kp_ref[:, sl], kc_ref[:, sl]], axis=0)
            v = jnp.concatenate([vp_ref[:, sl], vc_ref[:, sl]], axis=0)
            q2 = jnp.concatenate([jnp.where(lo, q, jnp.zeros_like(q)), jnp.where(lo, jnp.zeros_like(q), q)], axis=0)
            s = jnp.where(mask2, _dot_nt(q2, k) * 0.125, jnp.float32(-1e30))
            m = jnp.max(s, axis=-1, keepdims=True)
            p = jnp.exp(s - m)
            l = jnp.sum(p, axis=-1, keepdims=True)
            o2 = _dot((p / l).astype(BF16), v)
            o_ref[:, sl] = jnp.where(lo, o2[:BLK], o2[BLK:])
            lse = m + jnp.log(l)
            lse_all = jnp.where(lane // 16 == 2 * c, lse[:BLK], jnp.where(lane // 16 == 2 * c + 1, lse[BLK:], lse_all))
        lse_ref[...] = lse_all

    prev = lambda n: jnp.maximum(n - 1, 0)
    return _carrier_call(
        body, (qkv,) * 5, out_shape=(SDS((L, d * 512), F32), SDS((L, d * 128), F32)), grid=(d, nb),
        in_specs=[pl.BlockSpec((BLK, 512), lambda r, n: (n, 3 * r)),
                  pl.BlockSpec((BLK, 512), lambda r, n: (n, 3 * r + 1)),
                  pl.BlockSpec((BLK, 512), lambda r, n: (prev(n), 3 * r + 1)),
                  pl.BlockSpec((BLK, 512), lambda r, n: (n, 3 * r + 2)),
                  pl.BlockSpec((BLK, 512), lambda r, n: (prev(n), 3 * r + 2))],
        out_specs=(pl.BlockSpec((BLK, 512), lambda r, n: (n, r)),
                   pl.BlockSpec((BLK, 128), lambda r, n: (n, r))),
        sem=("parallel", "arbitrary"), name=f"attn_fwd_g{gi}", exchanges=exchanges)


def _attn_merge(os_, lses):
    S = os_[0].shape[0]
    tm = 512

    def body(o0, o1, o2, l0, l1, l2, att_ref, lt_ref):
        lo = lax.broadcasted_iota(jnp.int32, (tm, 128), 1) < 64
        ls = [l0[...], l1[...], l2[...]]
        m = jnp.maximum(jnp.maximum(ls[0], ls[1]), ls[2])
        es = [jnp.exp(v - m) for v in ls]
        z = es[0] + es[1] + es[2]
        lt_ref[...] = m + jnp.log(z)
        ws = [e / z for e in es]
        for c in range(4):
            sl = slice(c * 128, (c + 1) * 128)
            acc = jnp.zeros((tm, 128), F32)
            for g, o_g in enumerate((o0, o1, o2)):
                w_lo = jnp.broadcast_to(ws[g][:, 32 * c:32 * c + 1], (tm, 128))
                w_hi = jnp.broadcast_to(ws[g][:, 32 * c + 16:32 * c + 17], (tm, 128))
                acc = acc + jnp.where(lo, w_lo, w_hi) * o_g[:, sl]
            att_ref[:, sl] = acc.astype(BF16)

    ospec = pl.BlockSpec((tm, 512), lambda i: (i, 0))
    lspec = pl.BlockSpec((tm, 128), lambda i: (i, 0))
    return pl.pallas_call(
        body, out_shape=(SDS((S, 512), BF16), SDS((S, 128), F32)), grid=(S // tm,),
        in_specs=[ospec] * 3 + [lspec] * 3, out_specs=(ospec, lspec),
        compiler_params=_cparams("parallel"), name="attn_merge")(*os_, *lses)


def _ret_fwd(proj, consts):
    S = proj.shape[0]
    nc = S // BLK
    dmask, zeta, xi, dec = consts

    def body(q_ref, k_ref, v0_ref, v1_ref, g0_ref, g1_ref, dm_ref, z_ref, x_ref, dec_ref,
             y_ref, rn_ref, rs_ref, st_ref, R):
        @pl.when(pl.program_id(0) == 0)
        def _():
            R[...] = jnp.zeros_like(R)

        lane16 = lax.broadcasted_iota(jnp.int32, (BLK, 128), 1) // 16
        rs_all = jnp.zeros((BLK, 128), F32)
        for h in range(RET_HEADS):
            hs = slice(h * 128, (h + 1) * 128)
            vs = slice((h % 2) * 256, (h % 2 + 1) * 256)
            os_ = slice(h * 256, (h + 1) * 256)
            q, k = q_ref[:, hs], k_ref[:, hs]
            v = (v0_ref if h < 2 else v1_ref)[:, vs]
            Rb = R[h].astype(BF16)
            st_ref[h] = Rb
            s = _dot_nt(q, k) * dm_ref[h]
            o = _dot(s.astype(BF16), v) + _dot((q.astype(F32) * x_ref[h]).astype(BF16), Rb)
            kz = (k.astype(F32) * z_ref[h]).astype(BF16)
            R[h] = R[h] * dec_ref[h, 0:1, :] + _dot_tn(kz, v)
            mu = jnp.mean(o, axis=-1, keepdims=True)
            oc = o - mu
            rstd = lax.rsqrt(jnp.mean(oc * oc, axis=-1, keepdims=True) + NORM_EPS)
            rn = oc * rstd
            gr = (g0_ref if h < 2 else g1_ref)[:, vs].astype(F32)
            y_ref[:, os_] = (rn * gr * _sigmoid(gr)).astype(BF16)
            rn_ref[:, os_] = rn.astype(BF16)
            rs_all = jnp.where(lane16 == h, rstd, rs_all)
        rs_ref[...] = rs_all

    cst = lambda shape: pl.BlockSpec(shape, lambda c: (0, 0, 0))
    blk = lambda j: pl.BlockSpec((BLK, 512), lambda c: (c, j))
    return pl.pallas_call(
        body,
        out_shape=(SDS((S, 1024), BF16), SDS((S, 1024), BF16), SDS((S, 128), F32), SDS((RET_HEADS, nc, BLK, 256), BF16)),
        grid=(nc,),
        in_specs=[blk(QR_B), blk(KR_B), blk(11), blk(12), blk(13), blk(14),
                  cst((RET_HEADS, BLK, BLK)), cst((RET_HEADS, BLK, 128)), cst((RET_HEADS, BLK, 128)), cst((RET_HEADS, 8, 256))],
        out_specs=(pl.BlockSpec((BLK, 1024), lambda c: (c, 0)), pl.BlockSpec((BLK, 1024), lambda c: (c, 0)),
                   pl.BlockSpec((BLK, 128), lambda c: (c, 0)),
                   pl.BlockSpec((RET_HEADS, None, BLK, 256), lambda c: (0, c, 0, 0))),
        scratch_shapes=[pltpu.VMEM((RET_HEADS, BLK, 256), F32)],
        compiler_params=_cparams("arbitrary"), name="ret_fwd")(proj, proj, proj, proj, proj, proj, dmask, zeta, xi, dec)


def _branch_merge(att, yrin, proj, wa, wr):
    S = att.shape[0]
    tm = min(S, 2048)

    def body(a_ref, y_ref, ga_ref, gr_ref, wa_ref, wr_ref, m_ref, ya_ref, yr_ref):
        ya = _dot(a_ref[...], wa_ref[...])
        yr = _dot(y_ref[...], wr_ref[...])
        m_ref[...] = (_sigmoid(ga_ref[...].astype(F32)) * ya + _sigmoid(gr_ref[...].astype(F32)) * yr).astype(BF16)
        ya_ref[...] = ya.astype(BF16)
        yr_ref[...] = yr.astype(BF16)

    ospec = pl.BlockSpec((tm, 512), lambda i, j: (i, j))
    return pl.pallas_call(
        body, out_shape=(SDS((S, D_MODEL), BF16),) * 3, grid=(S // tm, 2),
        in_specs=[pl.BlockSpec((tm, 512), lambda i, j: (i, 0)), pl.BlockSpec((tm, 1024), lambda i, j: (i, 0)),
                  pl.BlockSpec((tm, 512), lambda i, j: (i, 15 + j)), pl.BlockSpec((tm, 512), lambda i, j: (i, 17 + j)),
                  pl.BlockSpec((512, 512), lambda i, j: (0, j)), pl.BlockSpec((1024, 512), lambda i, j: (0, j))],
        out_specs=(ospec, ospec, ospec),
        compiler_params=_cparams("parallel", "arbitrary"), name="branch_merge")(att, yrin, proj, proj, wa, wr)


def _out_proj(merged, wo, x, g2):
    S = x.shape[0]
    tm = 1024

    def body(m_ref, w_ref, x_ref, g_ref, x1_ref, h2_ref):
        x1 = x_ref[...] + _dot(m_ref[...], w_ref[...])
        x1_ref[...] = x1
        r = lax.rsqrt(jnp.mean(x1 * x1, axis=-1, keepdims=True) + NORM_EPS)
        h2_ref[...] = (x1 * r * g_ref[...]).astype(BF16)

    row = pl.BlockSpec((tm, D_MODEL), lambda i: (i, 0))
    return pl.pallas_call(
        body, out_shape=(SDS((S, D_MODEL), F32), SDS((S, D_MODEL), BF16)), grid=(S // tm,),
        in_specs=[row, pl.BlockSpec((D_MODEL, D_MODEL), lambda i: (0, 0)), row, pl.BlockSpec((1, D_MODEL), lambda i: (0, 0))],
        out_specs=(row, row), compiler_params=_cparams("parallel"), name="out_proj")(merged, wo, x, g2)


def _ffn_up(h2, wg, wu):
    S = h2.shape[0]
    tm = min(S, 2048)

    def body(h_ref, wg_ref, wu_ref, g_ref, u_ref, a_ref):
        hv = h_ref[...]
        g = _dot(hv, wg_ref[...])
        u = _dot(hv, wu_ref[...])
        g_ref[...] = g.astype(BF16)
        u_ref[...] = u.astype(BF16)
        a_ref[...] = (g * _sigmoid(g) * u).astype(BF16)

    wspec = pl.BlockSpec((None, D_MODEL, HID_S), lambda i, s: (s, 0, 0))
    ospec = pl.BlockSpec((None, tm, HID_S), lambda i, s: (s, i, 0))
    return pl.pallas_call(
        body, out_shape=(SDS((N_SHARD, S, HID_S), BF16),) * 3, grid=(S // tm, N_SHARD),
        in_specs=[pl.BlockSpec((tm, D_MODEL), lambda i, s: (i, 0)), wspec, wspec],
        out_specs=(ospec, ospec, ospec),
        compiler_params=_cparams("parallel", "arbitrary"), name="ffn_up")(h2, wg, wu)


def _ffn_down_loss(act, wd, x1, g3, tgt):
    S = x1.shape[0]
    tm = 1024

    def body(a_ref, w_ref, x_ref, g_ref, t_ref, dx_ref, dxb_ref, dg_ref, ls_ref, acc):
        i, s = pl.program_id(0), pl.program_id(1)

        @pl.when(s == 0)
        def _():
            acc[...] = jnp.zeros_like(acc)

        @pl.when((i == 0) & (s == 0))
        def _():
            dg_ref[...] = jnp.zeros_like(dg_ref)
            ls_ref[...] = jnp.zeros_like(ls_ref)

        acc[...] += _dot(a_ref[...], w_ref[...])

        @pl.when(s == N_SHARD - 1)
        def _():
            x2 = x_ref[...] + acc[...]
            r = lax.rsqrt(jnp.mean(x2 * x2, axis=-1, keepdims=True) + NORM_EPS)
            xh = x2 * r
            g = g_ref[...]
            err = xh * g - t_ref[...]
            ls_ref[...] += jnp.sum(jnp.sum(err * err, axis=-1, keepdims=True), axis=0, keepdims=True) * (0.5 / D_MODEL)
            dy = err * (1.0 / D_MODEL)
            dg_ref[...] += jnp.sum(dy * xh, axis=0, keepdims=True)
            dxh = dy * g
            dx = r * (dxh - xh * jnp.mean(dxh * xh, axis=-1, keepdims=True))
            dx_ref[...] = dx
            dxb_ref[...] = dx.astype(BF16)

    row = pl.BlockSpec((tm, D_MODEL), lambda i, s: (i, 0))
    vec = pl.BlockSpec((1, D_MODEL), lambda i, s: (0, 0))
    return pl.pallas_call(
        body, out_shape=(SDS((S, D_MODEL), F32), SDS((S, D_MODEL), BF16), SDS((1, D_MODEL), F32), SDS((8, 128), F32)),
        grid=(S // tm, N_SHARD),
        in_specs=[pl.BlockSpec((None, tm, HID_S), lambda i, s: (s, i, 0)),
                  pl.BlockSpec((None, HID_S, D_MODEL), lambda i, s: (s, 0, 0)), row, vec, row],
        out_specs=(row, row, vec, pl.BlockSpec((8, 128), lambda i, s: (0, 0))),
        scratch_shapes=[pltpu.VMEM((tm, D_MODEL), F32)],
        compiler_params=_cparams("arbitrary", "arbitrary"), name="ffn_down_loss")(act, wd, x1, g3, tgt)


def _ffn_down_bwd(dx2b, wd, gte, up):
    S = dx2b.shape[0]
    tm = min(S, 2048)

    def body(d_ref, w_ref, g_ref, u_ref, dg_ref, du_ref):
        da = _dot_nt(d_ref[...], w_ref[...])
        g = g_ref[...].astype(F32)
        sg = _sigmoid(g)
        dg_ref[...] = (da * u_ref[...].astype(F32) * sg * (1.0 + g * (1.0 - sg))).astype(BF16)
        du_ref[...] = (da * g * sg).astype(BF16)

    aspec = pl.BlockSpec((None, tm, HID_S), lambda i, s: (s, i, 0))
    return pl.pallas_call(
        body, out_shape=(SDS((N_SHARD, S, HID_S), BF16),) * 2, grid=(S // tm, N_SHARD),
        in_specs=[pl.BlockSpec((tm, D_MODEL), lambda i, s: (i, 0)),
                  pl.BlockSpec((None, HID_S, D_MODEL), lambda i, s: (s, 0, 0)), aspec, aspec],
        out_specs=(aspec, aspec),
        compiler_params=_cparams("parallel", "arbitrary"), name="ffn_down_bwd")(dx2b, wd, gte, up)


def _wgrad(name, a, b, a_spec, b_spec, out_shape, out_spec, n_par, S):
    tk = 1024

    def body(a_ref, b_ref, o_ref):
        @pl.when(pl.program_id(1) == 0)
        def _():
            o_ref[...] = jnp.zeros_like(o_ref)

        o_ref[...] += _dot_tn(a_ref[...], b_ref[...])

    return pl.pallas_call(
        body, out_shape=SDS(out_shape, F32), grid=(n_par, S // tk),
        in_specs=[a_spec(tk), b_spec(tk)], out_specs=out_spec,
        compiler_params=_cparams("parallel", "arbitrary"), name=name)(a, b)


def _ffn_up_bwd(dgte, dup, wg, wu, x1, g2, dx2, exchanges=()):
    S = x1.shape[0]
    tm = 1024

    def body(dg_ref, du_ref, wg_ref, wu_ref, x_ref, g_ref, dx2_ref, dx_ref, dxb_ref, dgn_ref, acc):
        i, s = pl.program_id(0), pl.program_id(1)

        @pl.when(s == 0)
        def _():
            acc[...] = jnp.zeros_like(acc)

        @pl.when((i == 0) & (s == 0))
        def _():
            dgn_ref[...] = jnp.zeros_like(dgn_ref)

        acc[...] += _dot_nt(dg_ref[...], wg_ref[...]) + _dot_nt(du_ref[...], wu_ref[...])

        @pl.when(s == N_SHARD - 1)
        def _():
            xv = x_ref[...]
            r = lax.rsqrt(jnp.mean(xv * xv, axis=-1, keepdims=True) + NORM_EPS)
            xh = xv * r
            dh = acc[...]
            dgn_ref[...] += jnp.sum(dh * xh, axis=0, keepdims=True)
            dxh = dh * g_ref[...]
            dx = dx2_ref[...] + r * (dxh - xh * jnp.mean(dxh * xh, axis=-1, keepdims=True))
            dx_ref[...] = dx
            dxb_ref[...] = dx.astype(BF16)

    row = pl.BlockSpec((tm, D_MODEL), lambda i, s: (i, 0))
    vec = pl.BlockSpec((1, D_MODEL), lambda i, s: (0, 0))
    aspec = pl.BlockSpec((None, tm, HID_S), lambda i, s: (s, i, 0))
    wspec = pl.BlockSpec((None, D_MODEL, HID_S), lambda i, s: (s, 0, 0))
    return _carrier_call(
        body, (dgte, dup, wg, wu, x1, g2, dx2),
        out_shape=(SDS((S, D_MODEL), F32), SDS((S, D_MODEL), BF16), SDS((1, D_MODEL), F32)),
        grid=(S // tm, N_SHARD),
        in_specs=[aspec, aspec, wspec, wspec, row, vec, row], out_specs=(row, row, vec),
        scratch_shapes=[pltpu.VMEM((tm, D_MODEL), F32)],
        sem=("arbitrary", "arbitrary"), name="ffn_up_bwd", exchanges=exchanges)


def _out_proj_bwd(dx1b, wo, proj, ya, yr):
    S = dx1b.shape[0]
    tm = min(S, 2048)

    def body(d_ref, w_ref, ga_ref, gr_ref, ya_ref, yr_ref, dya_ref, dyr_ref, dga_ref, dgr_ref):
        dm = _dot_nt(d_ref[...], w_ref[...])
        sa = _sigmoid(ga_ref[...].astype(F32))
        sr = _sigmoid(gr_ref[...].astype(F32))
        dya_ref[...] = (dm * sa).astype(BF16)
        dyr_ref[...] = (dm * sr).astype(BF16)
        dga_ref[...] = (dm * ya_ref[...].astype(F32) * sa * (1.0 - sa)).astype(BF16)
        dgr_ref[...] = (dm * yr_ref[...].astype(F32) * sr * (1.0 - sr)).astype(BF16)

    blk = pl.BlockSpec((tm, 512), lambda i, j: (i, j))
    return pl.pallas_call(
        body, out_shape=(SDS((S, D_MODEL), BF16),) * 4, grid=(S // tm, 2),
        in_specs=[pl.BlockSpec((tm, D_MODEL), lambda i, j: (i, 0)), pl.BlockSpec((512, D_MODEL), lambda i, j: (j, 0)),
                  pl.BlockSpec((tm, 512), lambda i, j: (i, 15 + j)), pl.BlockSpec((tm, 512), lambda i, j: (i, 17 + j)),
                  blk, blk],
        out_specs=(blk,) * 4,
        compiler_params=_cparams("parallel", "arbitrary"), name="out_proj_bwd")(dx1b, wo, proj, proj, ya, yr)


def _branch_bwd(dya, dyr, wa, wr, att):
    S = dya.shape[0]
    tm = 1024

    def body(da_ref, dr_ref, wa_ref, wr_ref, att_ref, datt_ref, rho_ref, dyi_ref):
        datt = _dot_nt(da_ref[...], wa_ref[...])
        datt_ref[...] = datt.astype(BF16)
        dyi_ref[...] = _dot_nt(dr_ref[...], wr_ref[...]).astype(BF16)
        prod = datt * att_ref[...].astype(F32)
        lane = lax.broadcasted_iota(jnp.int32, (tm, 128), 1)
        lo = lane < 64
        rho = jnp.zeros((tm, 128), F32)
        for c in range(4):
            pc = prod[:, c * 128:(c + 1) * 128]
            tot = jnp.sum(pc, axis=-1, keepdims=True)
            low = jnp.sum(jnp.where(lo, pc, 0.0), axis=-1, keepdims=True)
            rho = jnp.where(lane // 16 == 2 * c, low, jnp.where(lane // 16 == 2 * c + 1, tot - low, rho))
        rho_ref[...] = rho

    row = lambda w: pl.BlockSpec((tm, w), lambda i: (i, 0))
    return pl.pallas_call(
        body, out_shape=(SDS((S, 512), BF16), SDS((S, 128), F32), SDS((S, 1024), BF16)), grid=(S // tm,),
        in_specs=[row(1024), row(1024), pl.BlockSpec((512, 1024), lambda i: (0, 0)),
                  pl.BlockSpec((1024, 1024), lambda i: (0, 0)), row(512)],
        out_specs=(row(512), row(128), row(1024)),
        compiler_params=_cparams("parallel"), name="branch_bwd")(dya, dyr, wa, wr, att)


def _attn_bwd(qkv, datt, lse, rho, rtab, d, gi, exchanges=()):
    L = qkv.shape[0]
    nb = L // BLK

    def body(q_ref, kc_ref, kp_ref, vc_ref, vp_ref, do_ref, lse_ref, rho_ref, tq_ref, tk_ref,
             dq_ref, dk_ref, dv_ref, ck, cv):
        n = pl.program_id(1)

        @pl.when(n == 0)
        def _():
            ck[...] = jnp.zeros_like(ck)
            cv[...] = jnp.zeros_like(cv)

        def store_rot(ref, val, t_ref, c):
            sl = slice(c * 128, (c + 1) * 128)
            ref[:, sl] = _unrot(val, t_ref[0], t_ref[1], t_ref[2], 32).astype(BF16)

        @pl.when(n < nb)
        def _():
            mask = _band_mask(n)
            mask2 = jnp.concatenate([mask, mask], axis=0)
            lo = lax.broadcasted_iota(jnp.int32, (BLK, 128), 1) < 64

            def stacked(a):
                return jnp.concatenate([jnp.where(lo, a, jnp.zeros_like(a)), jnp.where(lo, jnp.zeros_like(a), a)], axis=0)

            def head_cols(ref, c):
                return jnp.concatenate([jnp.broadcast_to(ref[:, 32 * c:32 * c + 1], (BLK, 2 * BLK)),
                                        jnp.broadcast_to(ref[:, 32 * c + 16:32 * c + 17], (BLK, 2 * BLK))], axis=0)

            for c in range(4):
                sl = slice(c * 128, (c + 1) * 128)
                q2, do2 = stacked(q_ref[:, sl]), stacked(do_ref[:, sl])
                k = jnp.concatenate([kp_ref[:, sl], kc_ref[:, sl]], axis=0)
                v = jnp.concatenate([vp_ref[:, sl], vc_ref[:, sl]], axis=0)
                s = _dot_nt(q2, k) * 0.125
                p = jnp.where(mask2, jnp.exp(s - head_cols(lse_ref, c)), 0.0)
                dp = _dot_nt(do2, v)
                ds = (p * (dp - head_cols(rho_ref, c)) * 0.125).astype(BF16)
                dq2 = _dot(ds, k)
                dq_c = jnp.where(lo, dq2[:BLK], dq2[BLK:])
                dk_c = _dot_tn(ds, q2)
                dv_c = _dot_tn(p.astype(BF16), do2)
                store_rot(dq_ref, dq_c, tq_ref, c)
                store_rot(dk_ref, ck[:, sl] + dk_c[:BLK], tk_ref, c)
                dv_ref[:, sl] = (cv[:, sl] + dv_c[:BLK]).astype(BF16)
                ck[:, sl] = dk_c[BLK:]
                cv[:, sl] = dv_c[BLK:]

        @pl.when(n == nb)
        def _():
            for c in range(4):
                sl = slice(c * 128, (c + 1) * 128)
                store_rot(dk_ref, ck[:, sl], tk_ref, c)
            dv_ref[...] = cv[...].astype(BF16)

    cur = lambda n: jnp.minimum(n, nb - 1)
    prev = lambda n: jnp.maximum(jnp.minimum(n, nb - 1) - 1, 0)
    fin = lambda n: jnp.maximum(n - 1, 0)
    return _carrier_call(
        body, (qkv, qkv, qkv, qkv, qkv, datt, lse, rho, rtab, rtab),
        out_shape=(SDS((L, d * 512), BF16),) * 3, grid=(d, nb + 1),
        in_specs=[pl.BlockSpec((BLK, 512), lambda r, n: (cur(n), 3 * r)),
                  pl.BlockSpec((BLK, 512), lambda r, n: (cur(n), 3 * r + 1)),
                  pl.BlockSpec((BLK, 512), lambda r, n: (prev(n), 3 * r + 1)),
                  pl.BlockSpec((BLK, 512), lambda r, n: (cur(n), 3 * r + 2)),
                  pl.BlockSpec((BLK, 512), lambda r, n: (prev(n), 3 * r + 2)),
                  pl.BlockSpec((BLK, 512), lambda r, n: (cur(n), r)),
                  pl.BlockSpec((BLK, 128), lambda r, n: (cur(n), r)),
                  pl.BlockSpec((BLK, 128), lambda r, n: (cur(n), r)),
                  pl.BlockSpec((3, BLK, 128), lambda r, n: (0, cur(n), r)),
                  pl.BlockSpec((3, BLK, 128), lambda r, n: (0, fin(n), r))],
        out_specs=(pl.BlockSpec((BLK, 512), lambda r, n: (cur(n), r)),
                   pl.BlockSpec((BLK, 512), lambda r, n: (fin(n), r)),
                   pl.BlockSpec((BLK, 512), lambda r, n: (fin(n), r))),
        scratch_shapes=[pltpu.VMEM((BLK, 512), F32), pltpu.VMEM((BLK, 512), F32)],
        sem=("parallel", "arbitrary"), name=f"attn_bwd_g{gi}", exchanges=exchanges)


def _ret_bwd(proj, rn, rstd, dyrin, states, tab, consts, exchanges=()):
    S = proj.shape[0]
    nc = S // BLK
    dmask, zeta, xi, dec = consts

    def body(q_ref, k_ref, v0_ref, v1_ref, g0_ref, g1_ref, rn_ref, rs_ref, dy_ref, st_ref, tq_ref, tk_ref,
             dm_ref, z_ref, x_ref, dec_ref, dq_ref, dk_ref, dv_ref, dgr_ref, dR):
        @pl.when(pl.program_id(0) == 0)
        def _():
            dR[...] = jnp.zeros_like(dR)

        for h in range(RET_HEADS):
            hs = slice(h * 128, (h + 1) * 128)
            vs = slice((h % 2) * 256, (h % 2 + 1) * 256)
            os_ = slice(h * 256, (h + 1) * 256)
            q, k = q_ref[:, hs], k_ref[:, hs]
            v = (v0_ref if h < 2 else v1_ref)[:, vs]
            gr = (g0_ref if h < 2 else g1_ref)[:, vs].astype(F32)
            sg = _sigmoid(gr)
            rn_v = rn_ref[:, os_].astype(F32)
            dyi = dy_ref[:, os_].astype(F32)
            dgr_ref[:, os_] = (dyi * rn_v * sg * (1.0 + gr * (1.0 - sg))).astype(BF16)
            drn = dyi * gr * sg
            rstd = jnp.broadcast_to(rs_ref[:, 16 * h:16 * h + 1], (BLK, 256))
            do = rstd * (drn - jnp.mean(drn, axis=-1, keepdims=True) - rn_v * jnp.mean(drn * rn_v, axis=-1, keepdims=True))
            dob = do.astype(BF16)
            Rb = st_ref[h]
            dRb = dR[h].astype(BF16)
            dm, zt, xt = dm_ref[h], z_ref[h], x_ref[h]
            sD = (_dot_nt(q, k) * dm).astype(BF16)
            kz = (k.astype(F32) * zt).astype(BF16)
            qx = (q.astype(F32) * xt).astype(BF16)
            dv_ref[:, os_] = (_dot_tn(sD, dob) + _dot(kz, dRb)).astype(BF16)
            dS = (_dot_nt(dob, v) * dm).astype(BF16)
            dq = _dot(dS, k) + _dot_nt(dob, Rb) * xt
            dk = _dot_tn(dS, q) + _dot_nt(v, dRb) * zt
            dR[h] = dR[h] * dec_ref[h, 0:1, :] + _dot_tn(qx, dob)
            dq_ref[:, hs] = _unrot(dq, tq_ref[0], tq_ref[1], tq_ref[2], 1).astype(BF16)
            dk_ref[:, hs] = _unrot(dk, tk_ref[0], tk_ref[1], tk_ref[2], 1).astype(BF16)

    rc = lambda c: nc - 1 - c
    cst = lambda shape: pl.BlockSpec(shape, lambda c: (0, 0, 0))
    blk = lambda j: pl.BlockSpec((BLK, 512), lambda c: (rc(c), j))
    row = lambda w: pl.BlockSpec((BLK, w), lambda c: (rc(c), 0))
    return _carrier_call(
        body, (proj, proj, proj, proj, proj, proj, rn, rstd, dyrin, states, tab, tab, dmask, zeta, xi, dec),
        out_shape=(SDS((S, 512), BF16), SDS((S, 512), BF16), SDS((S, 1024), BF16), SDS((S, 1024), BF16)),
        grid=(nc,),
        in_specs=[blk(QR_B), blk(KR_B), blk(11), blk(12), blk(13), blk(14), row(1024), row(128), row(1024),
                  pl.BlockSpec((RET_HEADS, None, BLK, 256), lambda c: (0, rc(c), 0, 0)),
                  pl.BlockSpec((None, 3, BLK, 128), lambda c: (1, 0, rc(c), 0)),
                  pl.BlockSpec((None, 3, BLK, 128), lambda c: (2, 0, rc(c), 0)),
                  cst((RET_HEADS, BLK, BLK)), cst((RET_HEADS, BLK, 128)), cst((RET_HEADS, BLK, 128)), cst((RET_HEADS, 8, 256))],
        out_specs=(row(512), row(512), row(1024), row(1024)),
        scratch_shapes=[pltpu.VMEM((RET_HEADS, BLK, 256), F32)],
        sem=("arbitrary",), name="ret_bwd", exchanges=exchanges)


def _wgrad_in_half(ht, dproj, sidx, kept, exchanges=()):
    S = dproj.shape[0]
    tk = 1024
    half = (lambda sx: sx[4]) if kept else (lambda sx: 1 - sx[4])

    def body(a_ref, b_ref, o_ref):
        @pl.when(pl.program_id(1) == 0)
        def _():
            o_ref[...] = jnp.zeros_like(o_ref)

        o_ref[...] += _dot(a_ref[...], b_ref[...])

    (g,), xres = _carrier_call(
        body, (ht, dproj), out_shape=(SDS((D_MODEL // 2, PROJ_W), F32),), grid=(N_SHARD, S // tk),
        in_specs=[pl.BlockSpec((D_MODEL // 2, tk), lambda s, k, sx: (half(sx), k)),
                  pl.BlockSpec((tk, W_IN_S), lambda s, k, sx: (k, s))],
        out_specs=(pl.BlockSpec((D_MODEL // 2, W_IN_S), lambda s, k, sx: (0, s)),),
        sem=("parallel", "arbitrary"), name="wgrad_in_kept" if kept else "wgrad_in_sent", exchanges=exchanges,
        prefetch=sidx)
    return g, xres


def _in_proj_bwd(dproj, w_in, x, g1, dx1, exchanges=()):
    S = x.shape[0]
    tm = 512

    def body(d_ref, w_ref, x_ref, g_ref, dx1_ref, dx_ref, dgn_ref, acc):
        i, s = pl.program_id(0), pl.program_id(1)

        @pl.when(s == 0)
        def _():
            acc[...] = jnp.zeros_like(acc)

        @pl.when((i == 0) & (s == 0))
        def _():
            dgn_ref[...] = jnp.zeros_like(dgn_ref)

        acc[...] += _dot_nt(d_ref[...], w_ref[...])

        @pl.when(s == N_SHARD - 1)
        def _():
            xv = x_ref[...]
            r = lax.rsqrt(jnp.mean(xv * xv, axis=-1, keepdims=True) + NORM_EPS)
            xh = xv * r
            dh = acc[...]
            dgn_ref[...] += jnp.sum(dh * xh, axis=0, keepdims=True)
            dxh = dh * g_ref[...]
            dx_ref[...] = dx1_ref[...] + r * (dxh - xh * jnp.mean(dxh * xh, axis=-1, keepdims=True))

    row = pl.BlockSpec((tm, D_MODEL), lambda i, s: (i, 0))
    vec = pl.BlockSpec((1, D_MODEL), lambda i, s: (0, 0))
    (gx, dg), xres = _carrier_call(
        body, (dproj, w_in, x, g1, dx1),
        out_shape=(SDS((S, D_MODEL), F32), SDS((1, D_MODEL), F32)), grid=(S // tm, N_SHARD),
        in_specs=[pl.BlockSpec((tm, W_IN_S), lambda i, s: (i, s)),
                  pl.BlockSpec((D_MODEL, W_IN_S), lambda i, s: (0, s)), row, vec, row],
        out_specs=(row, vec), scratch_shapes=[pltpu.VMEM((tm, D_MODEL), F32)],
        sem=("arbitrary", "arbitrary"), name="in_proj_bwd", exchanges=exchanges)
    return gx, dg, xres


def _sub_view(a, d):
    S, W = a.shape
    return a.reshape(S // d, d * W)


def _step(x, tgt, g1, g2, g3, comm):
    S = x.shape[0]
    tab = _tables(S)
    consts = _ret_consts()

    h, ht = _rms_fwd(x, g1)
    w_in = comm.w_in()
    proj, xres = _in_proj(h, w_in, tab, comm.carry("in_proj"))
    comm.took("in_proj", xres)
    qkvs, o_parts, lse_parts = [], [], []
    for gi, d in enumerate(DILATIONS):
        cols = [proj[:, (b + gi) * COLB:(b + gi + 1) * COLB] for b in (QA_B, KA_B, VA_B)]
        qkv = _sub_view(jnp.concatenate(cols, axis=1), d)
        (o_g, lse_g), xres = _attn_fwd(qkv, d, gi, comm.carry(f"attn_fwd_g{gi}"))
        comm.took(f"attn_fwd_g{gi}", xres)
        qkvs.append(qkv)
        o_parts.append(o_g.reshape(S, 512))
        lse_parts.append(lse_g.reshape(S, 128))
    att, lse_tot = _attn_merge(o_parts, lse_parts)
    yrin, rn, rstd, states = _ret_fwd(proj, consts)
    wa, wr, wo, wg, wu, wd = comm.w_rest()
    merged, ya, yr = _branch_merge(att, yrin, proj, wa, wr)
    x1, h2 = _out_proj(merged, wo, x, g2)
    gte, up, act = _ffn_up(h2, wg, wu)
    dx2, dx2b, dg3, loss_p = _ffn_down_loss(act, wd, x1, g3, tgt)

    dgte, dup = _ffn_down_bwd(dx2b, wd, gte, up)
    tok3 = lambda w: (lambda tk: pl.BlockSpec((None, tk, w), lambda p, k: (p, k, 0)))
    tok2 = lambda w: (lambda tk: pl.BlockSpec((tk, w), lambda p, k: (k, 0)))
    g_d = _wgrad("wgrad_down", act, dx2b, tok3(HID_S), tok2(D_MODEL), (N_SHARD, HID_S, D_MODEL),
                 pl.BlockSpec((None, HID_S, D_MODEL), lambda p, k: (p, 0, 0)), N_SHARD, S)
    g_g = _wgrad("wgrad_gate", h2, dgte, tok2(D_MODEL), tok3(HID_S), (N_SHARD, D_MODEL, HID_S),
                 pl.BlockSpec((None, D_MODEL, HID_S), lambda p, k: (p, 0, 0)), N_SHARD, S)
    g_u = _wgrad("wgrad_up", h2, dup, tok2(D_MODEL), tok3(HID_S), (N_SHARD, D_MODEL, HID_S),
                 pl.BlockSpec((None, D_MODEL, HID_S), lambda p, k: (p, 0, 0)), N_SHARD, S)
    comm.grads({4: g_g, 5: g_u, 6: g_d})
    (dx1, dx1b, dg2), xres = _ffn_up_bwd(dgte, dup, wg, wu, x1, g2, dx2, comm.carry("ffn_up_bwd"))
    comm.took("ffn_up_bwd", xres)
    dya, dyr, dga, dgrr = _out_proj_bwd(dx1b, wo, proj, ya, yr)
    colblk = lambda w: (lambda tk: pl.BlockSpec((tk, w), lambda p, k: (k, p)))
    g_o = _wgrad("wgrad_out", merged, dx1b, colblk(256), tok2(D_MODEL), (D_MODEL, D_MODEL),
                 pl.BlockSpec((256, D_MODEL), lambda p, k: (p, 0)), 4, S)
    datt, rho, dyrin = _branch_bwd(dya, dyr, wa, wr, att)
    g_a = _wgrad("wgrad_attn", att, dya, tok2(512), colblk(512), (512, D_MODEL),
                 pl.BlockSpec((512, 512), lambda p, k: (0, p)), 2, S)
    g_r = _wgrad("wgrad_ret", yrin, dyr, colblk(256), tok2(D_MODEL), (D_MODEL, D_MODEL),
                 pl.BlockSpec((256, D_MODEL), lambda p, k: (p, 0)), 4, S)
    comm.grads({1: g_a, 2: g_r.reshape(N_SHARD, 256, D_MODEL), 3: g_o.reshape(N_SHARD, 256, D_MODEL)})
    (dqr, dkr, dvr, dgr), xres = _ret_bwd(proj, rn, rstd, dyrin, states, tab, consts, comm.carry("ret_bwd"))
    comm.took("ret_bwd", xres)
    dqs, dks, dvs = [], [], []
    for gi, d in enumerate(DILATIONS):
        rtab = tab[0].reshape(3, S // d, d * 128)
        (dq, dk, dv), xres = _attn_bwd(qkvs[gi], _sub_view(datt, d), _sub_view(lse_tot, d), _sub_view(rho, d), rtab, d, gi,
                                       comm.carry(f"attn_bwd_g{gi}"))
        comm.took(f"attn_bwd_g{gi}", xres)
        dqs.append(dq.reshape(S, 512))
        dks.append(dk.reshape(S, 512))
        dvs.append(dv.reshape(S, 512))
    dproj = jnp.concatenate(dqs + dks + dvs + [dqr, dkr, dvr, dgr, dga, dgrr], axis=1)
    g_sent, xres = _wgrad_in_half(ht, dproj, comm.sidx, False, comm.carry("wgrad_in_sent"))
    comm.took("wgrad_in_sent", xres)
    comm.grads({"in_sent": g_sent})
    g_kept, xres = _wgrad_in_half(ht, dproj, comm.sidx, True, comm.carry("wgrad_in_kept"))
    comm.grads({"in_kept": g_kept})
    comm.took("wgrad_in_kept", xres)
    grad_x, dg1, xres = _in_proj_bwd(dproj, w_in, x, g1, dx1, comm.carry("in_proj_bwd"))
    comm.took("in_proj_bwd", xres)
    return loss_p[0, 0], grad_x, (dg1, dg2, dg3)


W_KINDS = ("col", "col", "lead", "lead", "lead", "lead", "lead")
W_SHARD = ((1024, W_IN_S), (512, 256), (256, 1024), (256, 1024), (1024, HID_S), (1024, HID_S), (HID_S, 1024))
N_W = len(W_KINDS)


def _full_shape(wi):
    R, C = W_SHARD[wi]
    return (R, N_SHARD * C) if W_KINDS[wi] == "col" else (N_SHARD, R, C)


def _view(ref, wi, s, half):
    R, C = W_SHARD[wi]
    rows = pl.ds(half * (R // 2), R // 2)
    if W_KINDS[wi] == "col":
        return ref.at[rows, pl.ds(pl.multiple_of(s * C, 128), C)]
    return ref.at[s, rows, :]


def _mesh_pos():
    x, y, c = lax.axis_index("x"), lax.axis_index("y"), lax.axis_index("c")
    chips = [(1 - x, y), (x, 1 - y), (1 - x, 1 - y)]
    return x, y, c, chips


def _cast_bf16(a):
    R, C = a.shape
    tr = R // 2 if R % 32 == 0 else R

    def body(a_ref, o_ref):
        o_ref[...] = a_ref[...].astype(BF16)

    spec = pl.BlockSpec((tr, C), lambda i: (i, 0))
    return pl.pallas_call(body, out_shape=SDS((R, C), BF16), grid=(R // tr,), in_specs=[spec], out_specs=spec,
                          compiler_params=_cparams("parallel"), name=f"cast_{R}x{C}")(a)


def _remote(send, recv, k, src, dst, to):
    return pltpu.make_async_remote_copy(src_ref=src, dst_ref=dst, send_sem=send.at[k], recv_sem=recv.at[k],
                                        device_id=to, device_id_type=MESH)


def _gather_now(wis, shards):
    n = len(wis)

    def body(*refs):
        sh, full = refs[:n], refs[n:2 * n]
        send, recv, loc = refs[2 * n:]
        x, y, c, chips = _mesh_pos()
        s_me = 2 * x + y
        sib = (x, y, 1 - c)
        own, started = [], []
        for i, wi in enumerate(wis):
            Rh = W_SHARD[wi][0] // 2
            for hf in range(2):
                cp = pltpu.make_async_copy(sh[i].at[pl.ds(hf * Rh, Rh), :], _view(full[i], wi, s_me, hf), loc.at[2 * i + hf])
                cp.start()
                own.append(cp)
            for j, chip in enumerate(chips):
                cp = _remote(send, recv, 3 * i + j, sh[i].at[pl.ds(c * Rh, Rh), :], _view(full[i], wi, s_me, c), (*chip, c))
                cp.start()
                started.append(cp)
        for i, wi in enumerate(wis):
            for j, chip in enumerate(chips):
                land = _view(full[i], wi, 2 * chip[0] + chip[1], c)
                _remote(send, recv, 3 * i + j, land, land, (*chip, c)).wait_recv()
                fw = _remote(send, recv, 3 * n + 3 * i + j, land, land, sib)
                fw.start()
                started.append(fw)
        for i, wi in enumerate(wis):
            for j, chip in enumerate(chips):
                land = _view(full[i], wi, 2 * chip[0] + chip[1], 1 - c)
                _remote(send, recv, 3 * n + 3 * i + j, land, land, sib).wait_recv()
        for cp in started:
            cp.wait_send()
        for cp in own:
            cp.wait()

    return pl.pallas_call(
        body, out_shape=tuple(SDS(_full_shape(wi), BF16) for wi in wis),
        in_specs=[ANY] * n, out_specs=tuple([ANY] * n),
        scratch_shapes=[pltpu.SemaphoreType.DMA((6 * n,)), pltpu.SemaphoreType.DMA((6 * n,)),
                        pltpu.SemaphoreType.DMA((2 * n,))],
        name="gather_now")(*shards)


def _ex_gather_ici(wis, shards):
    def build(ins, outs, send, recv, loc):
        x, y, c, chips = _mesh_pos()
        s_me = 2 * x + y
        starts, waits = [], []
        for i, wi in enumerate(wis):
            Rh = W_SHARD[wi][0] // 2
            for hf in range(2):
                cp = pltpu.make_async_copy(ins[i].at[pl.ds(hf * Rh, Rh), :], _view(outs[i], wi, s_me, hf), loc.at[2 * i + hf])
                starts.append(cp)
                waits.append(cp.wait)
            for j, chip in enumerate(chips):
                cp = _remote(send, recv, 3 * i + j, ins[i].at[pl.ds(c * Rh, Rh), :], _view(outs[i], wi, s_me, c), (*chip, c))
                land = _view(outs[i], wi, 2 * chip[0] + chip[1], c)
                starts.append(cp)
                waits += [cp.wait_send, _remote(send, recv, 3 * i + j, land, land, (*chip, c)).wait_recv]
        return starts, waits

    return _Exchange(shards, [SDS(_full_shape(wi), BF16) for wi in wis], {}, 3 * len(wis), 2 * len(wis), build)


def _ex_gather_d2d(wis, fulls):
    def build(ins, outs, send, recv, loc):
        x, y, c, chips = _mesh_pos()
        sib = (x, y, 1 - c)
        starts, waits = [], []
        for i, wi in enumerate(wis):
            for j, chip in enumerate(chips):
                mine = _view(outs[i], wi, 2 * chip[0] + chip[1], c)
                theirs = _view(outs[i], wi, 2 * chip[0] + chip[1], 1 - c)
                cp = _remote(send, recv, 3 * i + j, mine, mine, sib)
                starts.append(cp)
                waits += [cp.wait_send, _remote(send, recv, 3 * i + j, theirs, theirs, sib).wait_recv]
        return starts, waits

    return _Exchange(fulls, [SDS(f.shape, BF16) for f in fulls], {i: i for i in range(len(wis))}, 3 * len(wis), 0, build)


def _half_shape(wi):
    R, C = W_SHARD[wi]
    return (R // 2, N_SHARD * C) if W_KINDS[wi] == "col" else (N_SHARD, R // 2, C)


def _ex_pair(wis, grads):
    def build(ins, outs, send, recv, loc):
        x, y, c, _ = _mesh_pos()
        starts, waits = [], []
        for i, wi in enumerate(wis):
            Rh = W_SHARD[wi][0] // 2
            rows = pl.ds((1 - c) * Rh, Rh)
            if tuple(ins[i].shape) == _half_shape(wi):
                src = ins[i]
            else:
                src = ins[i].at[rows, :] if W_KINDS[wi] == "col" else ins[i].at[:, rows, :]
            cp = _remote(send, recv, i, src, outs[i], (x, y, 1 - c))
            starts.append(cp)
            waits.append(cp.wait)
        return starts, waits

    return _Exchange(grads, [SDS(_half_shape(wi), F32) for wi in wis], {}, len(wis), 0, build)


def _ex_chip(wis, pbs):
    def build(ins, outs, send, recv, loc):
        x, y, c, chips = _mesh_pos()
        starts, waits = [], []
        for i, wi in enumerate(wis):
            for j, chip in enumerate(chips):
                cp = _remote(send, recv, 3 * i + j, ins[i].at[j], outs[i].at[j], (*chip, c))
                starts.append(cp)
                waits.append(cp.wait)
        return starts, waits

    shapes = [SDS((3, W_SHARD[wi][0] // 2, W_SHARD[wi][1]), BF16) for wi in wis]
    return _Exchange(pbs, shapes, {}, 3 * len(wis), 0, build)


def _ex_share(wis, halves):
    def build(ins, outs, send, recv, loc):
        x, y, c, _ = _mesh_pos()
        sib = (x, y, 1 - c)
        starts, waits = [], []
        for i, wi in enumerate(wis):
            cp = _remote(send, recv, i, outs[i].at[c], outs[i].at[c], sib)
            starts.append(cp)
            waits += [cp.wait_send, _remote(send, recv, i, outs[i].at[1 - c], outs[i].at[1 - c], sib).wait_recv]
        return starts, waits

    return _Exchange(halves, [SDS(h.shape, F32) for h in halves], {i: i for i in range(len(wis))}, len(wis), 0, build)


def _row_tile(rh, C):
    best = 16
    for t in range(16, rh + 1, 16):
        if rh % t == 0 and t * C * 4 <= (3 << 19):
            best = t
    return best


def _pair_sum(wi, g, ra, sidx):
    R, C = W_SHARD[wi]
    Rh = R // 2
    tr = _row_tile(Rh, C)
    nt = Rh // tr
    off = 0 if tuple(g.shape) == _half_shape(wi) else nt
    col = W_KINDS[wi] == "col"

    def body(sidx_ref, *refs):
        gs, rs = refs[:4], refs[4:8]
        own_ref, pb_ref = refs[8:]
        own_ref[...] = gs[0][...] + rs[0][...]
        for j in range(3):
            pb_ref[j] = (gs[1 + j][...] + rs[1 + j][...]).astype(BF16)

    def gspec(slot):
        if col:
            return pl.BlockSpec((tr, C), lambda i, sx: (sx[4] * off + i, sx[slot]))
        return pl.BlockSpec((None, tr, C), lambda i, sx: (sx[slot], sx[4] * off + i, 0))

    def rspec(slot):
        if col:
            return pl.BlockSpec((tr, C), lambda i, sx: (i, sx[slot]))
        return pl.BlockSpec((None, tr, C), lambda i, sx: (sx[slot], i, 0))

    return pl.pallas_call(
        body, out_shape=(SDS((Rh, C), F32), SDS((3, Rh, C), BF16)),
        grid_spec=pltpu.PrefetchScalarGridSpec(
            num_scalar_prefetch=1, grid=(nt,),
            in_specs=[gspec(k) for k in range(4)] + [rspec(k) for k in range(4)],
            out_specs=(pl.BlockSpec((tr, C), lambda i, sx: (i, 0)), pl.BlockSpec((3, tr, C), lambda i, sx: (0, i, 0)))),
        compiler_params=_cparams("arbitrary"), name=f"pair_sum_w{wi}")(sidx, g, g, g, g, ra, ra, ra, ra)


def _chip_sum(wi, own, rb, sidx):
    R, C = W_SHARD[wi]
    Rh = R // 2
    tr = _row_tile(Rh, C)

    def body(sidx_ref, own_ref, rb_ref, o_ref):
        o_ref[...] = ((own_ref[...] + rb_ref[0].astype(F32)) + rb_ref[1].astype(F32)) + rb_ref[2].astype(F32)

    return pl.pallas_call(
        body, out_shape=SDS((2, Rh, C), F32),
        grid_spec=pltpu.PrefetchScalarGridSpec(
            num_scalar_prefetch=1, grid=(Rh // tr,),
            in_specs=[pl.BlockSpec((tr, C), lambda i, sx: (i, 0)), pl.BlockSpec((3, tr, C), lambda i, sx: (0, i, 0))],
            out_specs=pl.BlockSpec((None, tr, C), lambda i, sx: (sx[4], i, 0))),
        compiler_params=_cparams("arbitrary"), name=f"chip_sum_w{wi}")(sidx, own, rb)


def _gain_allgather(blk):
    m_per, n = blk.shape

    def body(x_ref, out_ref, send_sems, recv_sems, local_sem):
        x, y, c, chips = _mesh_pos()
        me, sibling = (x, y, c), (x, y, 1 - c)

        def rows(px, py, pc):
            return out_ref.at[pl.ds((4 * px + 2 * py + pc) * m_per, m_per), :]

        def copy(k, block, to, src=None):
            return pltpu.make_async_remote_copy(
                src_ref=rows(*block) if src is None else src, dst_ref=rows(*block),
                send_sem=send_sems.at[k], recv_sem=recv_sems.at[k], device_id=to, device_id_type=MESH)

        mine = pltpu.make_async_copy(x_ref, rows(*me), local_sem)
        mine.start()
        first = [copy(0, me, sibling, src=x_ref)]
        first += [copy(1 + j, me, (*chip, c), src=x_ref) for j, chip in enumerate(chips)]
        for cp in first:
            cp.start()
        passed = [copy(4 + j, (*chip, c), sibling) for j, chip in enumerate(chips)]
        for j, chip in enumerate(chips):
            copy(1 + j, (*chip, c), me).wait_recv()
            passed[j].start()
        copy(0, sibling, me).wait_recv()
        for j, chip in enumerate(chips):
            copy(4 + j, (*chip, 1 - c), me).wait_recv()
        for cp in first + passed:
            cp.wait_send()
        mine.wait()

    vm = pl.BlockSpec(memory_space=pltpu.VMEM)
    return pl.pallas_call(
        body, out_shape=SDS((8 * m_per, n), blk.dtype), in_specs=[vm], out_specs=vm,
        scratch_shapes=[pltpu.SemaphoreType.DMA((7,)), pltpu.SemaphoreType.DMA((7,)), pltpu.SemaphoreType.DMA],
        name="gain_allgather")(blk)


def _adam_math(w, g, m, v):
    mn = ADAM_B1 * m + (1.0 - ADAM_B1) * g
    vn = ADAM_B2 * v + (1.0 - ADAM_B2) * (g * g)
    mh = mn / (1.0 - ADAM_B1 ** ADAM_STEP)
    vh = vn / (1.0 - ADAM_B2 ** ADAM_STEP)
    return -ADAM_LR * (mh / (jnp.sqrt(vh) + ADAM_EPS) + ADAM_WD * w), mn, vn


def _adamw(wi, w, g, m, v):
    R, C = w.shape
    tr = _row_tile(R, C)

    def body(w_ref, g_ref, m_ref, v_ref, d_ref, mn_ref, vn_ref):
        d_ref[...], mn_ref[...], vn_ref[...] = _adam_math(w_ref[...], g_ref[...], m_ref[...], v_ref[...])

    spec = pl.BlockSpec((tr, C), lambda i: (i, 0))
    return pl.pallas_call(body, out_shape=(SDS((R, C), F32),) * 3, grid=(R // tr,), in_specs=[spec] * 4,
                          out_specs=(spec,) * 3, compiler_params=_cparams("parallel"), name=f"adamw_w{wi}")(w, g, m, v)


def _gain_update(gathered, w, m, v):
    def body(ga_ref, w_ref, m_ref, v_ref, g_ref, d_ref, mn_ref, vn_ref):
        g = ga_ref[0:8, :]
        for dev in range(1, 8):
            g = g + ga_ref[8 * dev:8 * dev + 8, :]
        g_ref[...] = g
        d_ref[...], mn_ref[...], vn_ref[...] = _adam_math(w_ref[...], g, m_ref[...], v_ref[...])

    return pl.pallas_call(body, out_shape=(SDS((8, 1024), F32),) * 4, name="gain_update")(gathered, w, m, v)


GROUP_FFN, GROUP_MIX, GROUP_IN = (4, 5, 6), (1, 2, 3), (0,)
REST = GROUP_MIX + GROUP_FFN


class _MeshComm:
    SCHEDULE = {
        "in_proj": [("ici", (1, 2, 3, 4))],
        "attn_fwd_g0": [("d2d", (1, 2, 3, 4)), ("ici", (5,))],
        "attn_fwd_g1": [("d2d", (5,)), ("ici", (6,))],
        "attn_fwd_g2": [("d2d", (6,))],
        "ffn_up_bwd": [("pair", GROUP_FFN)],
        "ret_bwd": [("pair", GROUP_MIX)],
        "attn_bwd_g0": [("chip", (4,))],
        "attn_bwd_g1": [("chip", (5, 1, 2, 3))],
        "attn_bwd_g2": [("chip", (6,))],
        "wgrad_in_sent": [("share", GROUP_FFN + GROUP_MIX)],
        "wgrad_in_kept": [("pair", GROUP_IN)],
        "in_proj_bwd": [("chip", GROUP_IN)],
    }

    def __init__(self, shards):
        xi, yi, ci = lax.axis_index("x"), lax.axis_index("y"), lax.axis_index("c")
        self.sidx = jnp.stack([2 * xi + yi, 2 * (1 - xi) + yi, 2 * xi + (1 - yi), 2 * (1 - xi) + (1 - yi), ci]).astype(jnp.int32)
        self.shards, self.full = shards, {}
        self.g, self.own, self.pb, self.half, self.red = {}, {}, {}, {}, {}

    def w_in(self):
        return _gather_now(GROUP_IN, [self.shards[0]])[0]

    def w_rest(self):
        f = self.full
        return f[1], f[2].reshape(D_MODEL, D_MODEL), f[3].reshape(D_MODEL, D_MODEL), f[4], f[5], f[6]

    def grads(self, by_wi):
        self.g.update(by_wi)

    def _exchange(self, stage, wis):
        pick = lambda table: [table[wi] for wi in wis]
        if stage == "ici":
            return _ex_gather_ici(wis, pick(self.shards))
        if stage == "d2d":
            return _ex_gather_d2d(wis, pick(self.full))
        if stage == "pair":
            return _ex_pair(wis, [self.g["in_sent"] if wi == 0 else self.g[wi] for wi in wis])
        if stage == "chip":
            return _ex_chip(wis, pick(self.pb))
        return _ex_share(wis, pick(self.half))

    def _landed(self, stage, wis, res):
        for wi, r in zip(wis, res):
            if stage in ("ici", "d2d"):
                self.full[wi] = r
            elif stage == "pair":
                self.own[wi], self.pb[wi] = _pair_sum(wi, self.g["in_kept"] if wi == 0 else self.g[wi], r, self.sidx)
            elif stage == "chip":
                self.half[wi] = _chip_sum(wi, self.own[wi], r, self.sidx)
            else:
                self.red[wi] = r

    def carry(self, point):
        return [self._exchange(stage, wis) for stage, wis in self.SCHEDULE.get(point, ())]

    def took(self, point, xres):
        for (stage, wis), res in zip(self.SCHEDULE.get(point, ()), xres):
            self._landed(stage, wis, res)

    def reduced(self):
        self._landed("share", GROUP_IN, _exchange_call(self._exchange("share", GROUP_IN), "share_w_in"))
        return [self.red[wi] for wi in range(N_W)]


def kernel(x, norm_mix_g, w_in, w_out_attn, w_out_ret, w_out, norm_ffn_g, w_ffn_gate, w_ffn_up, w_ffn_down, norm_final_g, loss_target, m_norm_mix_g, m_w_in, m_w_out_attn, m_w_out_ret, m_w_out, m_norm_ffn_g, m_w_ffn_gate, m_w_ffn_up, m_w_ffn_down, m_norm_final_g, v_norm_mix_g, v_w_in, v_w_out_attn, v_w_out_ret, v_w_out, v_norm_ffn_g, v_w_ffn_gate, v_w_ffn_up, v_w_ffn_down, v_norm_final_g):
    ws = (w_in, w_out_attn, w_out_ret, w_out, w_ffn_gate, w_ffn_up, w_ffn_down)
    ms = (m_w_in, m_w_out_attn, m_w_out_ret, m_w_out, m_w_ffn_gate, m_w_ffn_up, m_w_ffn_down)
    vs = (v_w_in, v_w_out_attn, v_w_out_ret, v_w_out, v_w_ffn_gate, v_w_ffn_up, v_w_ffn_down)
    shard2d = lambda a, wi: a.reshape(W_SHARD[wi])

    comm = _MeshComm([_cast_bf16(shard2d(w, wi)) for wi, w in enumerate(ws)])
    g3 = norm_final_g.reshape(1, D_MODEL)
    loss_p, grad_x, gain_g = _step(x[0], loss_target[0], norm_mix_g, norm_ffn_g, g3, comm)
    gred = comm.reduced()

    outs_g, outs_d, outs_m, outs_v = [], [], [], []
    for wi in range(N_W):
        g2d = gred[wi].reshape(W_SHARD[wi])
        dlt, mn, vn = _adamw(wi, shard2d(ws[wi], wi), g2d, shard2d(ms[wi], wi), shard2d(vs[wi], wi))
        for lst, a in ((outs_g, g2d), (outs_d, dlt), (outs_m, mn), (outs_v, vn)):
            lst.append(a.reshape(ws[wi].shape))

    pad8 = lambda rows: jnp.concatenate([r.reshape(1, D_MODEL) for r in rows] + [jnp.zeros((5, D_MODEL), F32)], axis=0)
    gathered = _gain_allgather(pad8(gain_g))
    gg, gd, gm, gv = _gain_update(gathered, pad8((norm_mix_g, norm_ffn_g, norm_final_g)),
                                  pad8((m_norm_mix_g, m_norm_ffn_g, m_norm_final_g)),
                                  pad8((v_norm_mix_g, v_norm_ffn_g, v_norm_final_g)))
    loss = lax.psum(loss_p, ("x", "y", "c"))

    def assemble(gain_rows, wlist):
        return (gain_rows[0:1], wlist[0], wlist[1], wlist[2], wlist[3], gain_rows[1:2],
                wlist[4], wlist[5], wlist[6], gain_rows[2])

    return (loss, grad_x[None], *assemble(gg, outs_g), *assemble(gd, outs_d), *assemble(gm, outs_m), *assemble(gv, outs_v))
```

```python
import functools
import math

import numpy as np
import jax
import jax.numpy as jnp
from jax import lax
from jax.experimental import pallas as pl
from jax.experimental.pallas import tpu as pltpu

F32, BF16 = jnp.float32, jnp.bfloat16
SDS = jax.ShapeDtypeStruct
MESH = pl.DeviceIdType.MESH

D_MODEL = 1024
PROJ_W = 9728
COLB = 512
N_COLB = PROJ_W // COLB
QA_B, KA_B, VA_B = 0, 3, 6
QR_B, KR_B = 9, 10
FFN_HID = 2816
N_SHARD = 4
HID_S = FFN_HID // N_SHARD
W_IN_S = PROJ_W // N_SHARD
DILATIONS = (1, 4, 16)
BLK = 128
RET_HEADS = 4
ROPE_THETA = 10000.0
NORM_EPS = 1e-6
ADAM_LR, ADAM_B1, ADAM_B2, ADAM_EPS, ADAM_WD, ADAM_STEP = 0.001, 0.9, 0.999, 1e-08, 0.01, 10
VMEM_LIMIT = 56 << 20


def _cparams(*sem):
    return pltpu.CompilerParams(dimension_semantics=sem or None, vmem_limit_bytes=VMEM_LIMIT)


def _dot(a, b):
    return jnp.dot(a, b, preferred_element_type=F32)


def _dot_nt(a, b):
    return lax.dot_general(a, b, (((1,), (1,)), ((), ())), preferred_element_type=F32)


def _dot_tn(a, b):
    return lax.dot_general(a, b, (((0,), (0,)), ((), ())), preferred_element_type=F32)


def _sigmoid(z):
    return 1.0 / (1.0 + jnp.exp(-z))


ANY = pl.BlockSpec(memory_space=pl.ANY)


class _Exchange:
    def __init__(self, ins, out_shapes, aliases, n_sem, n_loc, build):
        self.ins, self.out_shapes, self.aliases = list(ins), list(out_shapes), dict(aliases)
        self.n_sem, self.n_loc, self.build = n_sem, n_loc, build

    def sems(self):
        return [pltpu.SemaphoreType.DMA((self.n_sem,)), pltpu.SemaphoreType.DMA((self.n_sem,)),
                pltpu.SemaphoreType.DMA((max(self.n_loc, 1),))]


def _exchange_call(ex, name):
    n_in, n_out = len(ex.ins), len(ex.out_shapes)

    def body(*refs):
        starts, waits = ex.build(refs[:n_in], refs[n_in:n_in + n_out], *refs[n_in + n_out:])
        for cp in starts:
            cp.start()
        for w in waits:
            w()

    return pl.pallas_call(body, out_shape=tuple(ex.out_shapes), in_specs=[ANY] * n_in, out_specs=tuple([ANY] * n_out),
                          input_output_aliases=ex.aliases, scratch_shapes=ex.sems(), name=name)(*ex.ins)


def _carrier_call(body, args, *, out_shape, grid, in_specs, out_specs, scratch_shapes=(), sem, name, exchanges=(),
                  prefetch=None):
    out_shape, out_specs = tuple(out_shape), tuple(out_specs)
    n_in, n_out, n_scr = len(args), len(out_shape), len(scratch_shapes)
    n_pre = 0 if prefetch is None else 1
    x_args, x_outs, aliases, x_scr, spans = [], [], {}, [], []
    for ex in exchanges:
        i0, o0 = len(x_args), len(x_outs)
        for a, o in ex.aliases.items():
            aliases[n_pre + n_in + i0 + a] = n_out + o0 + o
        x_args += ex.ins
        x_outs += ex.out_shapes
        x_scr += ex.sems()
        spans.append((i0, len(ex.ins), o0, len(ex.out_shapes)))
    nx_in, nx_out = len(x_args), len(x_outs)

    def wrapped(*refs):
        refs = refs[n_pre:]
        ins, xin = refs[:n_in], refs[n_in:n_in + nx_in]
        o_base = n_in + nx_in
        outs, xout = refs[o_base:o_base + n_out], refs[o_base + n_out:o_base + n_out + nx_out]
        s_base = o_base + n_out + nx_out
        scr, xs = refs[s_base:s_base + n_scr], refs[s_base + n_scr:]

        def built(e):
            i0, ni, o0, no = spans[e]
            return exchanges[e].build(xin[i0:i0 + ni], xout[o0:o0 + no], *xs[3 * e:3 * e + 3])

        if exchanges:
            first = functools.reduce(jnp.logical_and, [pl.program_id(k) == 0 for k in range(len(grid))])
            last = functools.reduce(jnp.logical_and, [pl.program_id(k) == grid[k] - 1 for k in range(len(grid))])

            @pl.when(first)
            def _():
                for e in range(len(exchanges)):
                    for cp in built(e)[0]:
                        cp.start()

        body(*ins, *outs, *scr)

        if exchanges:
            @pl.when(last)
            def _():
                for e in range(len(exchanges)):
                    for w in built(e)[1]:
                        w()

    all_in, all_out = list(in_specs) + [ANY] * nx_in, out_specs + tuple([ANY] * nx_out)
    all_scr = list(scratch_shapes) + x_scr
    cparams = _cparams(*(sem if not exchanges else ("arbitrary",) * len(grid)))
    if prefetch is None:
        res = pl.pallas_call(wrapped, out_shape=out_shape + tuple(x_outs), grid=grid, in_specs=all_in, out_specs=all_out,
                             scratch_shapes=all_scr, input_output_aliases=aliases, compiler_params=cparams,
                             name=name)(*args, *x_args)
    else:
        gs = pltpu.PrefetchScalarGridSpec(num_scalar_prefetch=1, grid=grid, in_specs=all_in, out_specs=all_out,
                                          scratch_shapes=all_scr)
        res = pl.pallas_call(wrapped, out_shape=out_shape + tuple(x_outs), grid_spec=gs, input_output_aliases=aliases,
                             compiler_params=cparams, name=name)(prefetch, *args, *x_args)
    xres = [tuple(res[n_out + o0:n_out + o0 + no]) for (_, _, o0, no) in spans]
    return tuple(res[:n_out]), xres


def _tables(S):
    pos = jnp.arange(S, dtype=F32)
    lane = np.arange(128)
    inv = ROPE_THETA ** (-jnp.arange(0, 64, 2, dtype=F32) / 64)
    ang = pos[:, None] * inv[None, :]
    idx = (lane % 64) % 32
    c, s = jnp.cos(ang)[:, idx], jnp.sin(ang)[:, idx]
    first = jnp.asarray((lane % 64) < 32)[None, :]
    rope = jnp.stack([c, jnp.where(first, 0.0, s), jnp.where(first, -s, 0.0)])
    base = 1.0 / (ROPE_THETA ** jnp.linspace(0.0, 1.0, 64, dtype=F32))
    ang2 = pos[:, None] * base[None, :]
    c2, s2 = jnp.cos(ang2)[:, lane // 2], jnp.sin(ang2)[:, lane // 2]
    even = jnp.asarray(lane % 2 == 0)[None, :]
    th = jnp.stack([c2, jnp.where(even, 0.0, s2), jnp.where(even, -s2, 0.0)])
    return jnp.stack([rope, th, th * (128 ** -0.5)]).astype(F32)


def _rot(a, c, sa, sb, shift):
    return a * c + pltpu.roll(a, shift, 1) * sa + pltpu.roll(a, 128 - shift, 1) * sb


def _unrot(g, c, sa, sb, shift):
    return g * c + pltpu.roll(g * sa, 128 - shift, 1) + pltpu.roll(g * sb, shift, 1)


def _ret_consts():
    h = np.arange(RET_HEADS, dtype=np.float64)
    log_g = np.log1p(-(2.0 ** (-5.0 - h)))
    idx = np.arange(BLK, dtype=np.float64)
    diff = idx[:, None] - idx[None, :]
    dmask = np.where(diff[None] >= 0, np.exp(np.maximum(diff, 0.0)[None] * log_g[:, None, None]), 0.0)
    zeta = np.exp((BLK - 1 - idx)[None, :] * log_g[:, None])
    xi = np.exp((idx + 1.0)[None, :] * log_g[:, None])
    dec = np.exp(BLK * log_g)
    rep = lambda v: np.broadcast_to(v[:, :, None], (RET_HEADS, BLK, 128))
    return (jnp.asarray(dmask, F32), jnp.asarray(rep(zeta), F32), jnp.asarray(rep(xi), F32),
            jnp.asarray(np.broadcast_to(dec[:, None, None], (RET_HEADS, 8, 256)), F32))


def _rms_fwd(x, g):
    S = x.shape[0]
    tm = 512

    def body(x_ref, g_ref, h_ref, ht_ref):
        xv = x_ref[...]
        r = lax.rsqrt(jnp.mean(xv * xv, axis=-1, keepdims=True) + NORM_EPS)
        h = xv * r * g_ref[...]
        h_ref[...] = h.astype(BF16)
        ht_ref[...] = h.T.astype(BF16)

    return pl.pallas_call(
        body, out_shape=(SDS((S, D_MODEL), BF16), SDS((D_MODEL, S), BF16)), grid=(S // tm,),
        in_specs=[pl.BlockSpec((tm, D_MODEL), lambda i: (i, 0)), pl.BlockSpec((1, D_MODEL), lambda i: (0, 0))],
        out_specs=(pl.BlockSpec((tm, D_MODEL), lambda i: (i, 0)), pl.BlockSpec((D_MODEL, tm), lambda i: (0, i))),
        compiler_params=_cparams("parallel"), name="rms_fwd")(x, g)


def _in_proj(h, w_in, tab, exchanges=()):
    S = h.shape[0]
    tm = min(S, 2048)

    def body(h_ref, w_ref, t_ref, o_ref):
        j = pl.program_id(1)
        acc = _dot(h_ref[...], w_ref[...])
        is_rope = j < 6
        is_theta = (j == QR_B) | (j == KR_B)

        def rotated(shift):
            c, sa, sb = t_ref[0, 0], t_ref[0, 1], t_ref[0, 2]
            for k in range(COLB // 128):
                sl = slice(k * 128, (k + 1) * 128)
                o_ref[:, sl] = _rot(acc[:, sl], c, sa, sb, shift).astype(BF16)

        @pl.when(is_rope)
        def _():
            rotated(32)

        @pl.when(is_theta)
        def _():
            rotated(1)

        @pl.when(jnp.logical_not(is_rope | is_theta))
        def _():
            o_ref[...] = acc.astype(BF16)

    def tab_map(i, j):
        return (jnp.where(j == QR_B, 1, jnp.where(j == KR_B, 2, 0)), 0, i, 0)

    (proj,), xres = _carrier_call(
        body, (h, w_in, tab), out_shape=(SDS((S, PROJ_W), BF16),), grid=(S // tm, N_COLB),
        in_specs=[pl.BlockSpec((tm, D_MODEL), lambda i, j: (i, 0)),
                  pl.BlockSpec((D_MODEL, COLB), lambda i, j: (0, j)),
                  pl.BlockSpec((1, 3, tm, 128), tab_map)],
        out_specs=(pl.BlockSpec((tm, COLB), lambda i, j: (i, j)),),
        sem=("parallel", "arbitrary"), name="in_proj", exchanges=exchanges)
    return proj, xres


def _band_mask(n):
    qi = lax.broadcasted_iota(jnp.int32, (BLK, 2 * BLK), 0)
    kj = lax.broadcasted_iota(jnp.int32, (BLK, 2 * BLK), 1)
    dist = BLK + qi - kj
    return (dist >= 0) & (dist <= BLK) & ((kj >= BLK) | (n > 0))


def _qkv_col(d, gi):
    if d == 1:
        return lambda t, r: 3 * t + gi
    return lambda t, r: 3 * r + t


def _attn_fwd(qkv, d, gi, exchanges=()):
    L = qkv.shape[0]
    nb = L // BLK

    def body(q_ref, kc_ref, kp_ref, vc_ref, vp_ref, o_ref, lse_ref):
        n = pl.program_id(1)
        mask = _band_mask(n)
        mask2 = jnp.concatenate([mask, mask], axis=0)
        lane = lax.broadcasted_iota(jnp.int32, (BLK, 128), 1)
        lo = lane < 64
        lse_all = jnp.zeros((BLK, 128), F32)
        for c in range(4):
            sl = slice(c * 128, (c + 1) * 128)
            q = q_ref[:, sl]
            k = jnp.concatenate([kp_ref[:, sl], kc_ref[:, sl]], axis=0)
            v = jnp.concatenate([vp_ref[:, sl], vc_ref[:, sl]], axis=0)
            q2 = jnp.concatenate([jnp.where(lo, q, jnp.zeros_like(q)), jnp.where(lo, jnp.zeros_like(q), q)], axis=0)
            s = jnp.where(mask2, _dot_nt(q2, k) * 0.125, jnp.float32(-1e30))
            m = jnp.max(s, axis=-1, keepdims=True)
            p = jnp.exp(s - m)
            l = jnp.sum(p, axis=-1, keepdims=True)
            o2 = _dot((p / l).astype(BF16), v)
            o_ref[:, sl] = jnp.where(lo, o2[:BLK], o2[BLK:])
            lse = m + jnp.log(l)
            lse_all = jnp.where(lane // 16 == 2 * c, lse[:BLK], jnp.where(lane // 16 == 2 * c + 1, lse[BLK:], lse_all))
        lse_ref[...] = lse_all

    prev = lambda n: jnp.maximum(n - 1, 0)
    col = _qkv_col(d, gi)
    return _carrier_call(
        body, (qkv,) * 5, out_shape=(SDS((L, d * 512), F32), SDS((L, d * 128), F32)), grid=(d, nb),
        in_specs=[pl.BlockSpec((BLK, 512), lambda r, n: (n, col(0, r))),
                  pl.BlockSpec((BLK, 512), lambda r, n: (n, col(1, r))),
                  pl.BlockSpec((BLK, 512), lambda r, n: (prev(n), col(1, r))),
                  pl.BlockSpec((BLK, 512), lambda r, n: (n, col(2, r))),
                  pl.BlockSpec((BLK, 512), lambda r, n: (prev(n), col(2, r)))],
        out_specs=(pl.BlockSpec((BLK, 512), lambda r, n: (n, r)),
                   pl.BlockSpec((BLK, 128), lambda r, n: (n, r))),
        sem=("parallel", "arbitrary"), name=f"attn_fwd_g{gi}", exchanges=exchanges)


def _qkv_to_sub(proj, d, gi):
    S = proj.shape[0]
    tm = 512
    n = tm // d

    def body(q_ref, k_ref, v_ref, o_ref, scr):
        for t, ref in enumerate((q_ref, k_ref, v_ref)):
            for c in range(4):
                scr[c] = ref[:, c * 128:(c + 1) * 128].astype(F32)
            for r in range(d):
                for c in range(4):
                    col = (3 * r + t) * 512 + c * 128
                    o_ref[:, col:col + 128] = scr[c, pl.ds(r, n, stride=d), :].astype(BF16)

    return pl.pallas_call(
        body, out_shape=SDS((S // d, d * 1536), BF16), grid=(S // tm,),
        in_specs=[pl.BlockSpec((tm, 512), lambda i, b=b: (i, b + gi)) for b in (QA_B, KA_B, VA_B)],
        out_specs=pl.BlockSpec((n, d * 1536), lambda i: (i, 0)),
        scratch_shapes=[pltpu.VMEM((4, tm, 128), F32)],
        compiler_params=_cparams("parallel"), name=f"qkv_to_sub_g{gi}")(proj, proj, proj)


def _attn_merge(os_, lses):
    S = os_[0].shape[0]
    tm = 512

    def body(o0, o1, o2, l0, l1, l2, att_ref, lt_ref, so1, so2, sl1, sl2):
        lo = lax.broadcasted_iota(jnp.int32, (tm, 128), 1) < 64

        def natural(ref, d, scr, width):
            nch = width // 128
            if d == 1:
                return [ref[:, c * 128:(c + 1) * 128] for c in range(nch)]
            for r in range(d):
                for c in range(nch):
                    scr[c, pl.ds(r, tm // d, stride=d), :] = ref[:, r * width + c * 128:r * width + (c + 1) * 128]
            return [scr[c] for c in range(nch)]

        ls = [natural(l, d, s, 128)[0] for l, d, s in zip((l0, l1, l2), DILATIONS, (None, sl1, sl2))]
        m = jnp.maximum(jnp.maximum(ls[0], ls[1]), ls[2])
        es = [jnp.exp(v - m) for v in ls]
        z = es[0] + es[1] + es[2]
        lt_ref[...] = m + jnp.log(z)
        ws = [e / z for e in es]
        o_nat = [natural(o, d, s, 512) for o, d, s in zip((o0, o1, o2), DILATIONS, (None, so1, so2))]
        for c in range(4):
            acc = jnp.zeros((tm, 128), F32)
            for g in range(3):
                w_lo = jnp.broadcast_to(ws[g][:, 32 * c:32 * c + 1], (tm, 128))
                w_hi = jnp.broadcast_to(ws[g][:, 32 * c + 16:32 * c + 17], (tm, 128))
                acc = acc + jnp.where(lo, w_lo, w_hi) * o_nat[g][c]
            att_ref[:, c * 128:(c + 1) * 128] = acc.astype(BF16)

    sub = lambda w: [pl.BlockSpec((tm // d, d * w), lambda i: (i, 0)) for d in DILATIONS]
    return pl.pallas_call(
        body, out_shape=(SDS((S, 512), BF16), SDS((S, 128), F32)), grid=(S // tm,),
        in_specs=sub(512) + sub(128),
        out_specs=(pl.BlockSpec((tm, 512), lambda i: (i, 0)), pl.BlockSpec((tm, 128), lambda i: (i, 0))),
        scratch_shapes=[pltpu.VMEM((4, tm, 128), F32), pltpu.VMEM((4, tm, 128), F32),
                        pltpu.VMEM((1, tm, 128), F32), pltpu.VMEM((1, tm, 128), F32)],
        compiler_params=_cparams("parallel"), name="attn_merge")(*os_, *lses)


def _assemble_dproj(att_grads, dqr, dkr, dvr, dgr, dga, dgrr):
    S = dqr.shape[0]
    tm = 256

    def body(*refs):
        a = [refs[3 * t:3 * t + 3] for t in range(3)]
        dqr_ref, dkr_ref, dvr_ref, dgr_ref, dga_ref, dgrr_ref, o_ref, scr = refs[9:]
        for t in range(3):
            for g, d in enumerate(DILATIONS):
                base = (3 * t + g) * COLB
                if d == 1:
                    o_ref[:, base:base + COLB] = a[t][g][...]
                    continue
                for c in range(4):
                    for r in range(d):
                        scr[c, pl.ds(r, tm // d, stride=d), :] = a[t][g][:, r * 512 + c * 128:r * 512 + (c + 1) * 128].astype(F32)
                    o_ref[:, base + c * 128:base + (c + 1) * 128] = scr[c].astype(BF16)
        o_ref[:, 9 * COLB:10 * COLB] = dqr_ref[...]
        o_ref[:, 10 * COLB:11 * COLB] = dkr_ref[...]
        o_ref[:, 11 * COLB:13 * COLB] = dvr_ref[...]
        o_ref[:, 13 * COLB:15 * COLB] = dgr_ref[...]
        o_ref[:, 15 * COLB:17 * COLB] = dga_ref[...]
        o_ref[:, 17 * COLB:19 * COLB] = dgrr_ref[...]

    sub = [pl.BlockSpec((tm // d, d * 512), lambda i: (i, 0)) for d in DILATIONS]
    row = lambda w: pl.BlockSpec((tm, w), lambda i: (i, 0))
    flat = [att_grads[t][g] for t in range(3) for g in range(3)]
    return pl.pallas_call(
        body, out_shape=SDS((S, PROJ_W), BF16), grid=(S // tm,),
        in_specs=sub * 3 + [row(512), row(512), row(1024), row(1024), row(1024), row(1024)],
        out_specs=row(PROJ_W), scratch_shapes=[pltpu.VMEM((4, tm, 128), F32)],
        compiler_params=_cparams("parallel"), name="assemble_dproj")(*flat, dqr, dkr, dvr, dgr, dga, dgrr)


def _ret_fwd(proj, consts):
    S = proj.shape[0]
    nc = S // BLK
    dmask, zeta, xi, dec = consts

    def body(q_ref, k_ref, v0_ref, v1_ref, g0_ref, g1_ref, dm_ref, z_ref, x_ref, dec_ref,
             y_ref, rn_ref, rs_ref, st_ref, R):
        @pl.when(pl.program_id(0) == 0)
        def _():
            R[...] = jnp.zeros_like(R)

        lane16 = lax.broadcasted_iota(jnp.int32, (BLK, 128), 1) // 16
        rs_all = jnp.zeros((BLK, 128), F32)
        for h in range(RET_HEADS):
            hs = slice(h * 128, (h + 1) * 128)
            vs = slice((h % 2) * 256, (h % 2 + 1) * 256)
            os_ = slice(h * 256, (h + 1) * 256)
            q, k = q_ref[:, hs], k_ref[:, hs]
            v = (v0_ref if h < 2 else v1_ref)[:, vs]
            Rb = R[h].astype(BF16)
            st_ref[h] = Rb
            s = _dot_nt(q, k) * dm_ref[h]
            o = _dot(s.astype(BF16), v) + _dot((q.astype(F32) * x_ref[h]).astype(BF16), Rb)
            kz = (k.astype(F32) * z_ref[h]).astype(BF16)
            R[h] = R[h] * dec_ref[h, 0:1, :] + _dot_tn(kz, v)
            mu = jnp.mean(o, axis=-1, keepdims=True)
            oc = o - mu
            rstd = lax.rsqrt(jnp.mean(oc * oc, axis=-1, keepdims=True) + NORM_EPS)
            rn = oc * rstd
            gr = (g0_ref if h < 2 else g1_ref)[:, vs].astype(F32)
            y_ref[:, os_] = (rn * gr * _sigmoid(gr)).astype(BF16)
            rn_ref[:, os_] = rn.astype(BF16)
            rs_all = jnp.where(lane16 == h, rstd, rs_all)
        rs_ref[...] = rs_all

    cst = lambda shape: pl.BlockSpec(shape, lambda c: (0, 0, 0))
    blk = lambda j: pl.BlockSpec((BLK, 512), lambda c: (c, j))
    return pl.pallas_call(
        body,
        out_shape=(SDS((S, 1024), BF16), SDS((S, 1024), BF16), SDS((S, 128), F32), SDS((RET_HEADS, nc, BLK, 256), BF16)),
        grid=(nc,),
        in_specs=[blk(QR_B), blk(KR_B), blk(11), blk(12), blk(13), blk(14),
                  cst((RET_HEADS, BLK, BLK)), cst((RET_HEADS, BLK, 128)), cst((RET_HEADS, BLK, 128)), cst((RET_HEADS, 8, 256))],
        out_specs=(pl.BlockSpec((BLK, 1024), lambda c: (c, 0)), pl.BlockSpec((BLK, 1024), lambda c: (c, 0)),
                   pl.BlockSpec((BLK, 128), lambda c: (c, 0)),
                   pl.BlockSpec((RET_HEADS, None, BLK, 256), lambda c: (0, c, 0, 0))),
        scratch_shapes=[pltpu.VMEM((RET_HEADS, BLK, 256), F32)],
        compiler_params=_cparams("arbitrary"), name="ret_fwd")(proj, proj, proj, proj, proj, proj, dmask, zeta, xi, dec)


def _branch_merge(att, yrin, proj, wa, wr):
    S = att.shape[0]
    tm = min(S, 2048)

    def body(a_ref, y_ref, ga_ref, gr_ref, wa_ref, wr_ref, m_ref, ya_ref, yr_ref):
        ya = _dot(a_ref[...], wa_ref[...])
        yr = _dot(y_ref[...], wr_ref[...])
        m_ref[...] = (_sigmoid(ga_ref[...].astype(F32)) * ya + _sigmoid(gr_ref[...].astype(F32)) * yr).astype(BF16)
        ya_ref[...] = ya.astype(BF16)
        yr_ref[...] = yr.astype(BF16)

    ospec = pl.BlockSpec((tm, 512), lambda i, j: (i, j))
    return pl.pallas_call(
        body, out_shape=(SDS((S, D_MODEL), BF16),) * 3, grid=(S // tm, 2),
        in_specs=[pl.BlockSpec((tm, 512), lambda i, j: (i, 0)), pl.BlockSpec((tm, 1024), lambda i, j: (i, 0)),
                  pl.BlockSpec((tm, 512), lambda i, j: (i, 15 + j)), pl.BlockSpec((tm, 512), lambda i, j: (i, 17 + j)),
                  pl.BlockSpec((512, 512), lambda i, j: (0, j)), pl.BlockSpec((1024, 512), lambda i, j: (0, j))],
        out_specs=(ospec, ospec, ospec),
        compiler_params=_cparams("parallel", "arbitrary"), name="branch_merge")(att, yrin, proj, proj, wa, wr)


def _out_proj(merged, wo, x, g2):
    S = x.shape[0]
    tm = 1024

    def body(m_ref, w_ref, x_ref, g_ref, x1_ref, h2_ref):
        x1 = x_ref[...] + _dot(m_ref[...], w_ref[...])
        x1_ref[...] = x1
        r = lax.rsqrt(jnp.mean(x1 * x1, axis=-1, keepdims=True) + NORM_EPS)
        h2_ref[...] = (x1 * r * g_ref[...]).astype(BF16)

    row = pl.BlockSpec((tm, D_MODEL), lambda i: (i, 0))
    return pl.pallas_call(
        body, out_shape=(SDS((S, D_MODEL), F32), SDS((S, D_MODEL), BF16)), grid=(S // tm,),
        in_specs=[row, pl.BlockSpec((D_MODEL, D_MODEL), lambda i: (0, 0)), row, pl.BlockSpec((1, D_MODEL), lambda i: (0, 0))],
        out_specs=(row, row), compiler_params=_cparams("parallel"), name="out_proj")(merged, wo, x, g2)


def _ffn_up(h2, wg, wu):
    S = h2.shape[0]
    tm = min(S, 2048)

    def body(h_ref, wg_ref, wu_ref, g_ref, u_ref, a_ref):
        hv = h_ref[...]
        g = _dot(hv, wg_ref[...])
        u = _dot(hv, wu_ref[...])
        g_ref[...] = g.astype(BF16)
        u_ref[...] = u.astype(BF16)
        a_ref[...] = (g * _sigmoid(g) * u).astype(BF16)

    wspec = pl.BlockSpec((None, D_MODEL, HID_S), lambda i, s: (s, 0, 0))
    ospec = pl.BlockSpec((None, tm, HID_S), lambda i, s: (s, i, 0))
    return pl.pallas_call(
        body, out_shape=(SDS((N_SHARD, S, HID_S), BF16),) * 3, grid=(S // tm, N_SHARD),
        in_specs=[pl.BlockSpec((tm, D_MODEL), lambda i, s: (i, 0)), wspec, wspec],
        out_specs=(ospec, ospec, ospec),
        compiler_params=_cparams("parallel", "arbitrary"), name="ffn_up")(h2, wg, wu)


def _ffn_down_loss(act, wd, x1, g3, tgt):
    S = x1.shape[0]
    tm = 1024

    def body(a_ref, w_ref, x_ref, g_ref, t_ref, dx_ref, dxb_ref, dg_ref, ls_ref, acc):
        i, s = pl.program_id(0), pl.program_id(1)

        @pl.when(s == 0)
        def _():
            acc[...] = jnp.zeros_like(acc)

        @pl.when((i == 0) & (s == 0))
        def _():
            dg_ref[...] = jnp.zeros_like(dg_ref)
            ls_ref[...] = jnp.zeros_like(ls_ref)

        acc[...] += _dot(a_ref[...], w_ref[...])

        @pl.when(s == N_SHARD - 1)
        def _():
            x2 = x_ref[...] + acc[...]
            r = lax.rsqrt(jnp.mean(x2 * x2, axis=-1, keepdims=True) + NORM_EPS)
            xh = x2 * r
            g = g_ref[...]
            err = xh * g - t_ref[...]
            ls_ref[...] += jnp.sum(jnp.sum(err * err, axis=-1, keepdims=True), axis=0, keepdims=True) * (0.5 / D_MODEL)
            dy = err * (1.0 / D_MODEL)
            dg_ref[...] += jnp.sum(dy * xh, axis=0, keepdims=True)
            dxh = dy * g
            dx = r * (dxh - xh * jnp.mean(dxh * xh, axis=-1, keepdims=True))
            dx_ref[...] = dx
            dxb_ref[...] = dx.astype(BF16)

    row = pl.BlockSpec((tm, D_MODEL), lambda i, s: (i, 0))
    vec = pl.BlockSpec((1, D_MODEL), lambda i, s: (0, 0))
    return pl.pallas_call(
        body, out_shape=(SDS((S, D_MODEL), F32), SDS((S, D_MODEL), BF16), SDS((1, D_MODEL), F32), SDS((8, 128), F32)),
        grid=(S // tm, N_SHARD),
        in_specs=[pl.BlockSpec((None, tm, HID_S), lambda i, s: (s, i, 0)),
                  pl.BlockSpec((None, HID_S, D_MODEL), lambda i, s: (s, 0, 0)), row, vec, row],
        out_specs=(row, row, vec, pl.BlockSpec((8, 128), lambda i, s: (0, 0))),
        scratch_shapes=[pltpu.VMEM((tm, D_MODEL), F32)],
        compiler_params=_cparams("arbitrary", "arbitrary"), name="ffn_down_loss")(act, wd, x1, g3, tgt)


def _ffn_down_bwd(dx2b, wd, gte, up):
    S = dx2b.shape[0]
    tm = min(S, 2048)

    def body(d_ref, w_ref, g_ref, u_ref, dg_ref, du_ref):
        da = _dot_nt(d_ref[...], w_ref[...])
        g = g_ref[...].astype(F32)
        sg = _sigmoid(g)
        dg_ref[...] = (da * u_ref[...].astype(F32) * sg * (1.0 + g * (1.0 - sg))).astype(BF16)
        du_ref[...] = (da * g * sg).astype(BF16)

    aspec = pl.BlockSpec((None, tm, HID_S), lambda i, s: (s, i, 0))
    return pl.pallas_call(
        body, out_shape=(SDS((N_SHARD, S, HID_S), BF16),) * 2, grid=(S // tm, N_SHARD),
        in_specs=[pl.BlockSpec((tm, D_MODEL), lambda i, s: (i, 0)),
                  pl.BlockSpec((None, HID_S, D_MODEL), lambda i, s: (s, 0, 0)), aspec, aspec],
        out_specs=(aspec, aspec),
        compiler_params=_cparams("parallel", "arbitrary"), name="ffn_down_bwd")(dx2b, wd, gte, up)


def _wgrad(name, a, b, a_spec, b_spec, out_shape, out_spec, n_par, S):
    tk = 1024

    def body(a_ref, b_ref, o_ref):
        @pl.when(pl.program_id(1) == 0)
        def _():
            o_ref[...] = jnp.zeros_like(o_ref)

        o_ref[...] += _dot_tn(a_ref[...], b_ref[...])

    return pl.pallas_call(
        body, out_shape=SDS(out_shape, F32), grid=(n_par, S // tk),
        in_specs=[a_spec(tk), b_spec(tk)], out_specs=out_spec,
        compiler_params=_cparams("parallel", "arbitrary"), name=name)(a, b)


def _ffn_up_bwd(dgte, dup, wg, wu, x1, g2, dx2, exchanges=()):
    S = x1.shape[0]
    tm = 1024

    def body(dg_ref, du_ref, wg_ref, wu_ref, x_ref, g_ref, dx2_ref, dx_ref, dxb_ref, dgn_ref, acc):
        i, s = pl.program_id(0), pl.program_id(1)

        @pl.when(s == 0)
        def _():
            acc[...] = jnp.zeros_like(acc)

        @pl.when((i == 0) & (s == 0))
        def _():
            dgn_ref[...] = jnp.zeros_like(dgn_ref)

        acc[...] += _dot_nt(dg_ref[...], wg_ref[...]) + _dot_nt(du_ref[...], wu_ref[...])

        @pl.when(s == N_SHARD - 1)
        def _():
            xv = x_ref[...]
            r = lax.rsqrt(jnp.mean(xv * xv, axis=-1, keepdims=True) + NORM_EPS)
            xh = xv * r
            dh = acc[...]
            dgn_ref[...] += jnp.sum(dh * xh, axis=0, keepdims=True)
            dxh = dh * g_ref[...]
            dx = dx2_ref[...] + r * (dxh - xh * jnp.mean(dxh * xh, axis=-1, keepdims=True))
            dx_ref[...] = dx
            dxb_ref[...] = dx.astype(BF16)

    row = pl.BlockSpec((tm, D_MODEL), lambda i, s: (i, 0))
    vec = pl.BlockSpec((1, D_MODEL), lambda i, s: (0, 0))
    aspec = pl.BlockSpec((None, tm, HID_S), lambda i, s: (s, i, 0))
    wspec = pl.BlockSpec((None, D_MODEL, HID_S), lambda i, s: (s, 0, 0))
    return _carrier_call(
        body, (dgte, dup, wg, wu, x1, g2, dx2),
        out_shape=(SDS((S, D_MODEL), F32), SDS((S, D_MODEL), BF16), SDS((1, D_MODEL), F32)),
        grid=(S // tm, N_SHARD),
        in_specs=[aspec, aspec, wspec, wspec, row, vec, row], out_specs=(row, row, vec),
        scratch_shapes=[pltpu.VMEM((tm, D_MODEL), F32)],
        sem=("arbitrary", "arbitrary"), name="ffn_up_bwd", exchanges=exchanges)


def _out_proj_bwd(dx1b, wo, proj, ya, yr):
    S = dx1b.shape[0]
    tm = min(S, 2048)

    def body(d_ref, w_ref, ga_ref, gr_ref, ya_ref, yr_ref, dya_ref, dyr_ref, dga_ref, dgr_ref):
        dm = _dot_nt(d_ref[...], w_ref[...])
        sa = _sigmoid(ga_ref[...].astype(F32))
        sr = _sigmoid(gr_ref[...].astype(F32))
        dya_ref[...] = (dm * sa).astype(BF16)
        dyr_ref[...] = (dm * sr).astype(BF16)
        dga_ref[...] = (dm * ya_ref[...].astype(F32) * sa * (1.0 - sa)).astype(BF16)
        dgr_ref[...] = (dm * yr_ref[...].astype(F32) * sr * (1.0 - sr)).astype(BF16)

    blk = pl.BlockSpec((tm, 512), lambda i, j: (i, j))
    return pl.pallas_call(
        body, out_shape=(SDS((S, D_MODEL), BF16),) * 4, grid=(S // tm, 2),
        in_specs=[pl.BlockSpec((tm, D_MODEL), lambda i, j: (i, 0)), pl.BlockSpec((512, D_MODEL), lambda i, j: (j, 0)),
                  pl.BlockSpec((tm, 512), lambda i, j: (i, 15 + j)), pl.BlockSpec((tm, 512), lambda i, j: (i, 17 + j)),
                  blk, blk],
        out_specs=(blk,) * 4,
        compiler_params=_cparams("parallel", "arbitrary"), name="out_proj_bwd")(dx1b, wo, proj, proj, ya, yr)


def _branch_bwd(dya, dyr, wa, wr, att):
    S = dya.shape[0]
    tm = 1024

    def body(da_ref, dr_ref, wa_ref, wr_ref, att_ref, datt_ref, rho_ref, dyi_ref):
        datt = _dot_nt(da_ref[...], wa_ref[...])
        datt_ref[...] = datt.astype(BF16)
        dyi_ref[...] = _dot_nt(dr_ref[...], wr_ref[...]).astype(BF16)
        prod = datt * att_ref[...].astype(F32)
        lane = lax.broadcasted_iota(jnp.int32, (tm, 128), 1)
        lo = lane < 64
        rho = jnp.zeros((tm, 128), F32)
        for c in range(4):
            pc = prod[:, c * 128:(c + 1) * 128]
            tot = jnp.sum(pc, axis=-1, keepdims=True)
            low = jnp.sum(jnp.where(lo, pc, 0.0), axis=-1, keepdims=True)
            rho = jnp.where(lane // 16 == 2 * c, low, jnp.where(lane // 16 == 2 * c + 1, tot - low, rho))
        rho_ref[...] = rho

    row = lambda w: pl.BlockSpec((tm, w), lambda i: (i, 0))
    return pl.pallas_call(
        body, out_shape=(SDS((S, 512), BF16), SDS((S, 128), F32), SDS((S, 1024), BF16)), grid=(S // tm,),
        in_specs=[row(1024), row(1024), pl.BlockSpec((512, 1024), lambda i: (0, 0)),
                  pl.BlockSpec((1024, 1024), lambda i: (0, 0)), row(512)],
        out_specs=(row(512), row(128), row(1024)),
        compiler_params=_cparams("parallel"), name="branch_bwd")(dya, dyr, wa, wr, att)


def _attn_bwd(qkv, datt, lse, rho, rtab, d, gi, exchanges=()):
    L = qkv.shape[0]
    nb = L // BLK

    def body(q_ref, kc_ref, kp_ref, vc_ref, vp_ref, do_ref, lse_ref, rho_ref, tq_ref, tk_ref,
             dq_ref, dk_ref, dv_ref, ck, cv):
        n = pl.program_id(1)

        @pl.when(n == 0)
        def _():
            ck[...] = jnp.zeros_like(ck)
            cv[...] = jnp.zeros_like(cv)

        def store_rot(ref, val, t_ref, c):
            sl = slice(c * 128, (c + 1) * 128)
            ref[:, sl] = _unrot(val, t_ref[0], t_ref[1], t_ref[2], 32).astype(BF16)

        @pl.when(n < nb)
        def _():
            mask = _band_mask(n)
            mask2 = jnp.concatenate([mask, mask], axis=0)
            lo = lax.broadcasted_iota(jnp.int32, (BLK, 128), 1) < 64

            def stacked(a):
                return jnp.concatenate([jnp.where(lo, a, jnp.zeros_like(a)), jnp.where(lo, jnp.zeros_like(a), a)], axis=0)

            def head_cols(ref, c):
                return jnp.concatenate([jnp.broadcast_to(ref[:, 32 * c:32 * c + 1], (BLK, 2 * BLK)),
                                        jnp.broadcast_to(ref[:, 32 * c + 16:32 * c + 17], (BLK, 2 * BLK))], axis=0)

            for c in range(4):
                sl = slice(c * 128, (c + 1) * 128)
                q2, do2 = stacked(q_ref[:, sl]), stacked(do_ref[:, sl])
                k = jnp.concatenate([kp_ref[:, sl], kc_ref[:, sl]], axis=0)
                v = jnp.concatenate([vp_ref[:, sl], vc_ref[:, sl]], axis=0)
                s = _dot_nt(q2, k) * 0.125
                p = jnp.where(mask2, jnp.exp(s - head_cols(lse_ref, c)), 0.0)
                dp = _dot_nt(do2, v)
                ds = (p * (dp - head_cols(rho_ref, c)) * 0.125).astype(BF16)
                dq2 = _dot(ds, k)
                dq_c = jnp.where(lo, dq2[:BLK], dq2[BLK:])
                dk_c = _dot_tn(ds, q2)
                dv_c = _dot_tn(p.astype(BF16), do2)
                store_rot(dq_ref, dq_c, tq_ref, c)
                store_rot(dk_ref, ck[:, sl] + dk_c[:BLK], tk_ref, c)
                dv_ref[:, sl] = (cv[:, sl] + dv_c[:BLK]).astype(BF16)
                ck[:, sl] = dk_c[BLK:]
                cv[:, sl] = dv_c[BLK:]

        @pl.when(n == nb)
        def _():
            for c in range(4):
                sl = slice(c * 128, (c + 1) * 128)
                store_rot(dk_ref, ck[:, sl], tk_ref, c)
            dv_ref[...] = cv[...].astype(BF16)

    cur = lambda n: jnp.minimum(n, nb - 1)
    prev = lambda n: jnp.maximum(jnp.minimum(n, nb - 1) - 1, 0)
    fin = lambda n: jnp.maximum(n - 1, 0)
    col = _qkv_col(d, gi)
    return _carrier_call(
        body, (qkv, qkv, qkv, qkv, qkv, datt, lse, rho, rtab, rtab),
        out_shape=(SDS((L, d * 512), BF16),) * 3, grid=(d, nb + 1),
        in_specs=[pl.BlockSpec((BLK, 512), lambda r, n: (cur(n), col(0, r))),
                  pl.BlockSpec((BLK, 512), lambda r, n: (cur(n), col(1, r))),
                  pl.BlockSpec((BLK, 512), lambda r, n: (prev(n), col(1, r))),
                  pl.BlockSpec((BLK, 512), lambda r, n: (cur(n), col(2, r))),
                  pl.BlockSpec((BLK, 512), lambda r, n: (prev(n), col(2, r))),
                  pl.BlockSpec((BLK, 512), lambda r, n: (cur(n), r)),
                  pl.BlockSpec((BLK, 128), lambda r, n: (cur(n), r)),
                  pl.BlockSpec((BLK, 128), lambda r, n: (cur(n), r)),
                  pl.BlockSpec((3, BLK, 128), lambda r, n: (0, cur(n), r)),
                  pl.BlockSpec((3, BLK, 128), lambda r, n: (0, fin(n), r))],
        out_specs=(pl.BlockSpec((BLK, 512), lambda r, n: (cur(n), r)),
                   pl.BlockSpec((BLK, 512), lambda r, n: (fin(n), r)),
                   pl.BlockSpec((BLK, 512), lambda r, n: (fin(n), r))),
        scratch_shapes=[pltpu.VMEM((BLK, 512), F32), pltpu.VMEM((BLK, 512), F32)],
        sem=("parallel", "arbitrary"), name=f"attn_bwd_g{gi}", exchanges=exchanges)


def _ret_bwd(proj, rn, rstd, dyrin, states, tab, consts, exchanges=()):
    S = proj.shape[0]
    nc = S // BLK
    dmask, zeta, xi, dec = consts

    def body(q_ref, k_ref, v0_ref, v1_ref, g0_ref, g1_ref, rn_ref, rs_ref, dy_ref, st_ref, tq_ref, tk_ref,
             dm_ref, z_ref, x_ref, dec_ref, dq_ref, dk_ref, dv_ref, dgr_ref, dR):
        @pl.when(pl.program_id(0) == 0)
        def _():
            dR[...] = jnp.zeros_like(dR)

        for h in range(RET_HEADS):
            hs = slice(h * 128, (h + 1) * 128)
            vs = slice((h % 2) * 256, (h % 2 + 1) * 256)
            os_ = slice(h * 256, (h + 1) * 256)
            q, k = q_ref[:, hs], k_ref[:, hs]
            v = (v0_ref if h < 2 else v1_ref)[:, vs]
            gr = (g0_ref if h < 2 else g1_ref)[:, vs].astype(F32)
            sg = _sigmoid(gr)
            rn_v = rn_ref[:, os_].astype(F32)
            dyi = dy_ref[:, os_].astype(F32)
            dgr_ref[:, os_] = (dyi * rn_v * sg * (1.0 + gr * (1.0 - sg))).astype(BF16)
            drn = dyi * gr * sg
            rstd = jnp.broadcast_to(rs_ref[:, 16 * h:16 * h + 1], (BLK, 256))
            do = rstd * (drn - jnp.mean(drn, axis=-1, keepdims=True) - rn_v * jnp.mean(drn * rn_v, axis=-1, keepdims=True))
            dob = do.astype(BF16)
            Rb = st_ref[h]
            dRb = dR[h].astype(BF16)
            dm, zt, xt = dm_ref[h], z_ref[h], x_ref[h]
            sD = (_dot_nt(q, k) * dm).astype(BF16)
            kz = (k.astype(F32) * zt).astype(BF16)
            qx = (q.astype(F32) * xt).astype(BF16)
            dv_ref[:, os_] = (_dot_tn(sD, dob) + _dot(kz, dRb)).astype(BF16)
            dS = (_dot_nt(dob, v) * dm).astype(BF16)
            dq = _dot(dS, k) + _dot_nt(dob, Rb) * xt
            dk = _dot_tn(dS, q) + _dot_nt(v, dRb) * zt
            dR[h] = dR[h] * dec_ref[h, 0:1, :] + _dot_tn(qx, dob)
            dq_ref[:, hs] = _unrot(dq, tq_ref[0], tq_ref[1], tq_ref[2], 1).astype(BF16)
            dk_ref[:, hs] = _unrot(dk, tk_ref[0], tk_ref[1], tk_ref[2], 1).astype(BF16)

    rc = lambda c: nc - 1 - c
    cst = lambda shape: pl.BlockSpec(shape, lambda c: (0, 0, 0))
    blk = lambda j: pl.BlockSpec((BLK, 512), lambda c: (rc(c), j))
    row = lambda w: pl.BlockSpec((BLK, w), lambda c: (rc(c), 0))
    return _carrier_call(
        body, (proj, proj, proj, proj, proj, proj, rn, rstd, dyrin, states, tab, tab, dmask, zeta, xi, dec),
        out_shape=(SDS((S, 512), BF16), SDS((S, 512), BF16), SDS((S, 1024), BF16), SDS((S, 1024), BF16)),
        grid=(nc,),
        in_specs=[blk(QR_B), blk(KR_B), blk(11), blk(12), blk(13), blk(14), row(1024), row(128), row(1024),
                  pl.BlockSpec((RET_HEADS, None, BLK, 256), lambda c: (0, rc(c), 0, 0)),
                  pl.BlockSpec((None, 3, BLK, 128), lambda c: (1, 0, rc(c), 0)),
                  pl.BlockSpec((None, 3, BLK, 128), lambda c: (2, 0, rc(c), 0)),
                  cst((RET_HEADS, BLK, BLK)), cst((RET_HEADS, BLK, 128)), cst((RET_HEADS, BLK, 128)), cst((RET_HEADS, 8, 256))],
        out_specs=(row(512), row(512), row(1024), row(1024)),
        scratch_shapes=[pltpu.VMEM((RET_HEADS, BLK, 256), F32)],
        sem=("arbitrary",), name="ret_bwd", exchanges=exchanges)


def _wgrad_in_half(ht, dproj, sidx, kept, exchanges=()):
    S = dproj.shape[0]
    tk = 1024
    half = (lambda sx: sx[4]) if kept else (lambda sx: 1 - sx[4])

    def body(a_ref, b_ref, o_ref):
        @pl.when(pl.program_id(1) == 0)
        def _():
            o_ref[...] = jnp.zeros_like(o_ref)

        o_ref[...] += _dot(a_ref[...], b_ref[...])

    (g,), xres = _carrier_call(
        body, (ht, dproj), out_shape=(SDS((D_MODEL // 2, PROJ_W), F32),), grid=(N_SHARD, S // tk),
        in_specs=[pl.BlockSpec((D_MODEL // 2, tk), lambda s, k, sx: (half(sx), k)),
                  pl.BlockSpec((tk, W_IN_S), lambda s, k, sx: (k, s))],
        out_specs=(pl.BlockSpec((D_MODEL // 2, W_IN_S), lambda s, k, sx: (0, s)),),
        sem=("parallel", "arbitrary"), name="wgrad_in_kept" if kept else "wgrad_in_sent", exchanges=exchanges,
        prefetch=sidx)
    return g, xres


def _in_proj_bwd(dproj, w_in, x, g1, dx1, exchanges=()):
    S = x.shape[0]
    tm = 512

    def body(d_ref, w_ref, x_ref, g_ref, dx1_ref, dx_ref, dgn_ref, acc):
        i, s = pl.program_id(0), pl.program_id(1)

        @pl.when(s == 0)
        def _():
            acc[...] = jnp.zeros_like(acc)

        @pl.when((i == 0) & (s == 0))
        def _():
            dgn_ref[...] = jnp.zeros_like(dgn_ref)

        acc[...] += _dot_nt(d_ref[...], w_ref[...])

        @pl.when(s == N_SHARD - 1)
        def _():
            xv = x_ref[...]
            r = lax.rsqrt(jnp.mean(xv * xv, axis=-1, keepdims=True) + NORM_EPS)
            xh = xv * r
            dh = acc[...]
            dgn_ref[...] += jnp.sum(dh * xh, axis=0, keepdims=True)
            dxh = dh * g_ref[...]
            dx_ref[...] = dx1_ref[...] + r * (dxh - xh * jnp.mean(dxh * xh, axis=-1, keepdims=True))

    row = pl.BlockSpec((tm, D_MODEL), lambda i, s: (i, 0))
    vec = pl.BlockSpec((1, D_MODEL), lambda i, s: (0, 0))
    (gx, dg), xres = _carrier_call(
        body, (dproj, w_in, x, g1, dx1),
        out_shape=(SDS((S, D_MODEL), F32), SDS((1, D_MODEL), F32)), grid=(S // tm, N_SHARD),
        in_specs=[pl.BlockSpec((tm, W_IN_S), lambda i, s: (i, s)),
                  pl.BlockSpec((D_MODEL, W_IN_S), lambda i, s: (0, s)), row, vec, row],
        out_specs=(row, vec), scratch_shapes=[pltpu.VMEM((tm, D_MODEL), F32)],
        sem=("arbitrary", "arbitrary"), name="in_proj_bwd", exchanges=exchanges)
    return gx, dg, xres


def _sub_view(a, d):
    S, W = a.shape
    return a.reshape(S // d, d * W)


def _step(x, tgt, g1, g2, g3, comm):
    S = x.shape[0]
    tab = _tables(S)
    consts = _ret_consts()

    h, ht = _rms_fwd(x, g1)
    w_in = comm.w_in()
    proj, xres = _in_proj(h, w_in, tab, comm.carry("in_proj"))
    comm.took("in_proj", xres)
    qkvs, o_parts, lse_parts = [], [], []
    for gi, d in enumerate(DILATIONS):
        qkv = proj if d == 1 else _qkv_to_sub(proj, d, gi)
        (o_g, lse_g), xres = _attn_fwd(qkv, d, gi, comm.carry(f"attn_fwd_g{gi}"))
        comm.took(f"attn_fwd_g{gi}", xres)
        qkvs.append(qkv)
        o_parts.append(o_g)
        lse_parts.append(lse_g)
    att, lse_tot = _attn_merge(o_parts, lse_parts)
    yrin, rn, rstd, states = _ret_fwd(proj, consts)
    wa, wr, wo, wg, wu, wd = comm.w_rest()
    merged, ya, yr = _branch_merge(att, yrin, proj, wa, wr)
    x1, h2 = _out_proj(merged, wo, x, g2)
    gte, up, act = _ffn_up(h2, wg, wu)
    dx2, dx2b, dg3, loss_p = _ffn_down_loss(act, wd, x1, g3, tgt)

    dgte, dup = _ffn_down_bwd(dx2b, wd, gte, up)
    tok3 = lambda w: (lambda tk: pl.BlockSpec((None, tk, w), lambda p, k: (p, k, 0)))
    tok2 = lambda w: (lambda tk: pl.BlockSpec((tk, w), lambda p, k: (k, 0)))
    g_d = _wgrad("wgrad_down", act, dx2b, tok3(HID_S), tok2(D_MODEL), (N_SHARD, HID_S, D_MODEL),
                 pl.BlockSpec((None, HID_S, D_MODEL), lambda p, k: (p, 0, 0)), N_SHARD, S)
    g_g = _wgrad("wgrad_gate", h2, dgte, tok2(D_MODEL), tok3(HID_S), (N_SHARD, D_MODEL, HID_S),
                 pl.BlockSpec((None, D_MODEL, HID_S), lambda p, k: (p, 0, 0)), N_SHARD, S)
    g_u = _wgrad("wgrad_up", h2, dup, tok2(D_MODEL), tok3(HID_S), (N_SHARD, D_MODEL, HID_S),
                 pl.BlockSpec((None, D_MODEL, HID_S), lambda p, k: (p, 0, 0)), N_SHARD, S)
    comm.grads({4: g_g, 5: g_u, 6: g_d})
    (dx1, dx1b, dg2), xres = _ffn_up_bwd(dgte, dup, wg, wu, x1, g2, dx2, comm.carry("ffn_up_bwd"))
    comm.took("ffn_up_bwd", xres)
    dya, dyr, dga, dgrr = _out_proj_bwd(dx1b, wo, proj, ya, yr)
    colblk = lambda w: (lambda tk: pl.BlockSpec((tk, w), lambda p, k: (k, p)))
    g_o = _wgrad("wgrad_out", merged, dx1b, colblk(256), tok2(D_MODEL), (D_MODEL, D_MODEL),
                 pl.BlockSpec((256, D_MODEL), lambda p, k: (p, 0)), 4, S)
    datt, rho, dyrin = _branch_bwd(dya, dyr, wa, wr, att)
    g_a = _wgrad("wgrad_attn", att, dya, tok2(512), colblk(512), (512, D_MODEL),
                 pl.BlockSpec((512, 512), lambda p, k: (0, p)), 2, S)
    g_r = _wgrad("wgrad_ret", yrin, dyr, colblk(256), tok2(D_MODEL), (D_MODEL, D_MODEL),
                 pl.BlockSpec((256, D_MODEL), lambda p, k: (p, 0)), 4, S)
    comm.grads({1: g_a, 2: g_r.reshape(N_SHARD, 256, D_MODEL), 3: g_o.reshape(N_SHARD, 256, D_MODEL)})
    (dqr, dkr, dvr, dgr), xres = _ret_bwd(proj, rn, rstd, dyrin, states, tab, consts, comm.carry("ret_bwd"))
    comm.took("ret_bwd", xres)
    dqs, dks, dvs = [], [], []
    for gi, d in enumerate(DILATIONS):
        rtab = tab[0].reshape(3, S // d, d * 128)
        (dq, dk, dv), xres = _attn_bwd(qkvs[gi], _sub_view(datt, d), _sub_view(lse_tot, d), _sub_view(rho, d), rtab, d, gi,
                                       comm.carry(f"attn_bwd_g{gi}"))
        comm.took(f"attn_bwd_g{gi}", xres)
        dqs.append(dq)
        dks.append(dk)
        dvs.append(dv)
    dproj = _assemble_dproj((dqs, dks, dvs), dqr, dkr, dvr, dgr, dga, dgrr)
    g_sent, xres = _wgrad_in_half(ht, dproj, comm.sidx, False, comm.carry("wgrad_in_sent"))
    comm.took("wgrad_in_sent", xres)
    comm.grads({"in_sent": g_sent})
    g_kept, xres = _wgrad_in_half(ht, dproj, comm.sidx, True, comm.carry("wgrad_in_kept"))
    comm.grads({"in_kept": g_kept})
    comm.took("wgrad_in_kept", xres)
    grad_x, dg1, xres = _in_proj_bwd(dproj, w_in, x, g1, dx1, comm.carry("in_proj_bwd"))
    comm.took("in_proj_bwd", xres)
    return loss_p[0, 0], grad_x, (dg1, dg2, dg3)


W_KINDS = ("col", "col", "lead", "lead", "lead", "lead", "lead")
W_SHARD = ((1024, W_IN_S), (512, 256), (256, 1024), (256, 1024), (1024, HID_S), (1024, HID_S), (HID_S, 1024))
N_W = len(W_KINDS)


def _full_shape(wi):
    R, C = W_SHARD[wi]
    return (R, N_SHARD * C) if W_KINDS[wi] == "col" else (N_SHARD, R, C)


def _view(ref, wi, s, half):
    R, C = W_SHARD[wi]
    rows = pl.ds(half * (R // 2), R // 2)
    if W_KINDS[wi] == "col":
        return ref.at[rows, pl.ds(pl.multiple_of(s * C, 128), C)]
    return ref.at[s, rows, :]


def _mesh_pos():
    x, y, c = lax.axis_index("x"), lax.axis_index("y"), lax.axis_index("c")
    chips = [(1 - x, y), (x, 1 - y), (1 - x, 1 - y)]
    return x, y, c, chips


def _cast_bf16(a):
    R, C = a.shape
    tr = R // 2 if R % 32 == 0 else R

    def body(a_ref, o_ref):
        o_ref[...] = a_ref[...].astype(BF16)

    spec = pl.BlockSpec((tr, C), lambda i: (i, 0))
    return pl.pallas_call(body, out_shape=SDS((R, C), BF16), grid=(R // tr,), in_specs=[spec], out_specs=spec,
                          compiler_params=_cparams("parallel"), name=f"cast_{R}x{C}")(a)


def _remote(send, recv, k, src, dst, to):
    return pltpu.make_async_remote_copy(src_ref=src, dst_ref=dst, send_sem=send.at[k], recv_sem=recv.at[k],
                                        device_id=to, device_id_type=MESH)


def _gather_now(wis, shards):
    n = len(wis)

    def body(*refs):
        sh, full = refs[:n], refs[n:2 * n]
        send, recv, loc = refs[2 * n:]
        x, y, c, chips = _mesh_pos()
        s_me = 2 * x + y
        sib = (x, y, 1 - c)
        own, started = [], []
        for i, wi in enumerate(wis):
            Rh = W_SHARD[wi][0] // 2
            for hf in range(2):
                cp = pltpu.make_async_copy(sh[i].at[pl.ds(hf * Rh, Rh), :], _view(full[i], wi, s_me, hf), loc.at[2 * i + hf])
                cp.start()
                own.append(cp)
            for j, chip in enumerate(chips):
                cp = _remote(send, recv, 3 * i + j, sh[i].at[pl.ds(c * Rh, Rh), :], _view(full[i], wi, s_me, c), (*chip, c))
                cp.start()
                started.append(cp)
        for i, wi in enumerate(wis):
            for j, chip in enumerate(chips):
                land = _view(full[i], wi, 2 * chip[0] + chip[1], c)
                _remote(send, recv, 3 * i + j, land, land, (*chip, c)).wait_recv()
                fw = _remote(send, recv, 3 * n + 3 * i + j, land, land, sib)
                fw.start()
                started.append(fw)
        for i, wi in enumerate(wis):
            for j, chip in enumerate(chips):
                land = _view(full[i], wi, 2 * chip[0] + chip[1], 1 - c)
                _remote(send, recv, 3 * n + 3 * i + j, land, land, sib).wait_recv()
        for cp in started:
            cp.wait_send()
        for cp in own:
            cp.wait()

    return pl.pallas_call(
        body, out_shape=tuple(SDS(_full_shape(wi), BF16) for wi in wis),
        in_specs=[ANY] * n, out_specs=tuple([ANY] * n),
        scratch_shapes=[pltpu.SemaphoreType.DMA((6 * n,)), pltpu.SemaphoreType.DMA((6 * n,)),
                        pltpu.SemaphoreType.DMA((2 * n,))],
        name="gather_now")(*shards)


def _ex_gather_ici(wis, shards):
    def build(ins, outs, send, recv, loc):
        x, y, c, chips = _mesh_pos()
        s_me = 2 * x + y
        starts, waits = [], []
        for i, wi in enumerate(wis):
            Rh = W_SHARD[wi][0] // 2
            for hf in range(2):
                cp = pltpu.make_async_copy(ins[i].at[pl.ds(hf * Rh, Rh), :], _view(outs[i], wi, s_me, hf), loc.at[2 * i + hf])
                starts.append(cp)
                waits.append(cp.wait)
            for j, chip in enumerate(chips):
                cp = _remote(send, recv, 3 * i + j, ins[i].at[pl.ds(c * Rh, Rh), :], _view(outs[i], wi, s_me, c), (*chip, c))
                land = _view(outs[i], wi, 2 * chip[0] + chip[1], c)
                starts.append(cp)
                waits += [cp.wait_send, _remote(send, recv, 3 * i + j, land, land, (*chip, c)).wait_recv]
        return starts, waits

    return _Exchange(shards, [SDS(_full_shape(wi), BF16) for wi in wis], {}, 3 * len(wis), 2 * len(wis), build)


def _ex_gather_d2d(wis, fulls):
    def build(ins, outs, send, recv, loc):
        x, y, c, chips = _mesh_pos()
        sib = (x, y, 1 - c)
        starts, waits = [], []
        for i, wi in enumerate(wis):
            for j, chip in enumerate(chips):
                mine = _view(outs[i], wi, 2 * chip[0] + chip[1], c)
                theirs = _view(outs[i], wi, 2 * chip[0] + chip[1], 1 - c)
                cp = _remote(send, recv, 3 * i + j, mine, mine, sib)
                starts.append(cp)
                waits += [cp.wait_send, _remote(send, recv, 3 * i + j, theirs, theirs, sib).wait_recv]
        return starts, waits

    return _Exchange(fulls, [SDS(f.shape, BF16) for f in fulls], {i: i for i in range(len(wis))}, 3 * len(wis), 0, build)


def _half_shape(wi):
    R, C = W_SHARD[wi]
    return (R // 2, N_SHARD * C) if W_KINDS[wi] == "col" else (N_SHARD, R // 2, C)


def _ex_pair(wis, grads):
    def build(ins, outs, send, recv, loc):
        x, y, c, _ = _mesh_pos()
        starts, waits = [], []
        for i, wi in enumerate(wis):
            Rh = W_SHARD[wi][0] // 2
            rows = pl.ds((1 - c) * Rh, Rh)
            if tuple(ins[i].shape) == _half_shape(wi):
                src = ins[i]
            else:
                src = ins[i].at[rows, :] if W_KINDS[wi] == "col" else ins[i].at[:, rows, :]
            cp = _remote(send, recv, i, src, outs[i], (x, y, 1 - c))
            starts.append(cp)
            waits.append(cp.wait)
        return starts, waits

    return _Exchange(grads, [SDS(_half_shape(wi), F32) for wi in wis], {}, len(wis), 0, build)


def _ex_chip(wis, pbs):
    def build(ins, outs, send, recv, loc):
        x, y, c, chips = _mesh_pos()
        starts, waits = [], []
        for i, wi in enumerate(wis):
            for j, chip in enumerate(chips):
                cp = _remote(send, recv, 3 * i + j, ins[i].at[j], outs[i].at[j], (*chip, c))
                starts.append(cp)
                waits.append(cp.wait)
        return starts, waits

    shapes = [SDS((3, W_SHARD[wi][0] // 2, W_SHARD[wi][1]), BF16) for wi in wis]
    return _Exchange(pbs, shapes, {}, 3 * len(wis), 0, build)


def _ex_share(wis, halves):
    def build(ins, outs, send, recv, loc):
        x, y, c, _ = _mesh_pos()
        sib = (x, y, 1 - c)
        starts, waits = [], []
        for i, wi in enumerate(wis):
            cp = _remote(send, recv, i, outs[i].at[c], outs[i].at[c], sib)
            starts.append(cp)
            waits += [cp.wait_send, _remote(send, recv, i, outs[i].at[1 - c], outs[i].at[1 - c], sib).wait_recv]
        return starts, waits

    return _Exchange(halves, [SDS(h.shape, F32) for h in halves], {i: i for i in range(len(wis))}, len(wis), 0, build)


def _row_tile(rh, C):
    best = 16
    for t in range(16, rh + 1, 16):
        if rh % t == 0 and t * C * 4 <= (3 << 19):
            best = t
    return best


def _pair_sum(wi, g, ra, sidx):
    R, C = W_SHARD[wi]
    Rh = R // 2
    tr = _row_tile(Rh, C)
    nt = Rh // tr
    off = 0 if tuple(g.shape) == _half_shape(wi) else nt
    col = W_KINDS[wi] == "col"

    def body(sidx_ref, *refs):
        gs, rs = refs[:4], refs[4:8]
        own_ref, pb_ref = refs[8:]
        own_ref[...] = gs[0][...] + rs[0][...]
        for j in range(3):
            pb_ref[j] = (gs[1 + j][...] + rs[1 + j][...]).astype(BF16)

    def gspec(slot):
        if col:
            return pl.BlockSpec((tr, C), lambda i, sx: (sx[4] * off + i, sx[slot]))
        return pl.BlockSpec((None, tr, C), lambda i, sx: (sx[slot], sx[4] * off + i, 0))

    def rspec(slot):
        if col:
            return pl.BlockSpec((tr, C), lambda i, sx: (i, sx[slot]))
        return pl.BlockSpec((None, tr, C), lambda i, sx: (sx[slot], i, 0))

    return pl.pallas_call(
        body, out_shape=(SDS((Rh, C), F32), SDS((3, Rh, C), BF16)),
        grid_spec=pltpu.PrefetchScalarGridSpec(
            num_scalar_prefetch=1, grid=(nt,),
            in_specs=[gspec(k) for k in range(4)] + [rspec(k) for k in range(4)],
            out_specs=(pl.BlockSpec((tr, C), lambda i, sx: (i, 0)), pl.BlockSpec((3, tr, C), lambda i, sx: (0, i, 0)))),
        compiler_params=_cparams("arbitrary"), name=f"pair_sum_w{wi}")(sidx, g, g, g, g, ra, ra, ra, ra)


def _chip_sum(wi, own, rb, sidx):
    R, C = W_SHARD[wi]
    Rh = R // 2
    tr = _row_tile(Rh, C)

    def body(sidx_ref, own_ref, rb_ref, o_ref):
        o_ref[...] = ((own_ref[...] + rb_ref[0].astype(F32)) + rb_ref[1].astype(F32)) + rb_ref[2].astype(F32)

    return pl.pallas_call(
        body, out_shape=SDS((2, Rh, C), F32),
        grid_spec=pltpu.PrefetchScalarGridSpec(
            num_scalar_prefetch=1, grid=(Rh // tr,),
            in_specs=[pl.BlockSpec((tr, C), lambda i, sx: (i, 0)), pl.BlockSpec((3, tr, C), lambda i, sx: (0, i, 0))],
            out_specs=pl.BlockSpec((None, tr, C), lambda i, sx: (sx[4], i, 0))),
        compiler_params=_cparams("arbitrary"), name=f"chip_sum_w{wi}")(sidx, own, rb)


def _gain_allgather(blk):
    m_per, n = blk.shape

    def body(x_ref, out_ref, send_sems, recv_sems, local_sem):
        x, y, c, chips = _mesh_pos()
        me, sibling = (x, y, c), (x, y, 1 - c)

        def rows(px, py, pc):
            return out_ref.at[pl.ds((4 * px + 2 * py + pc) * m_per, m_per), :]

        def copy(k, block, to, src=None):
            return pltpu.make_async_remote_copy(
                src_ref=rows(*block) if src is None else src, dst_ref=rows(*block),
                send_sem=send_sems.at[k], recv_sem=recv_sems.at[k], device_id=to, device_id_type=MESH)

        mine = pltpu.make_async_copy(x_ref, rows(*me), local_sem)
        mine.start()
        first = [copy(0, me, sibling, src=x_ref)]
        first += [copy(1 + j, me, (*chip, c), src=x_ref) for j, chip in enumerate(chips)]
        for cp in first:
            cp.start()
        passed = [copy(4 + j, (*chip, c), sibling) for j, chip in enumerate(chips)]
        for j, chip in enumerate(chips):
            copy(1 + j, (*chip, c), me).wait_recv()
            passed[j].start()
        copy(0, sibling, me).wait_recv()
        for j, chip in enumerate(chips):
            copy(4 + j, (*chip, 1 - c), me).wait_recv()
        for cp in first + passed:
            cp.wait_send()
        mine.wait()

    vm = pl.BlockSpec(memory_space=pltpu.VMEM)
    return pl.pallas_call(
        body, out_shape=SDS((8 * m_per, n), blk.dtype), in_specs=[vm], out_specs=vm,
        scratch_shapes=[pltpu.SemaphoreType.DMA((7,)), pltpu.SemaphoreType.DMA((7,)), pltpu.SemaphoreType.DMA],
        name="gain_allgather")(blk)


def _adam_math(w, g, m, v):
    mn = ADAM_B1 * m + (1.0 - ADAM_B1) * g
    vn = ADAM_B2 * v + (1.0 - ADAM_B2) * (g * g)
    mh = mn / (1.0 - ADAM_B1 ** ADAM_STEP)
    vh = vn / (1.0 - ADAM_B2 ** ADAM_STEP)
    return -ADAM_LR * (mh / (jnp.sqrt(vh) + ADAM_EPS) + ADAM_WD * w), mn, vn


def _adamw(wi, w, g, m, v):
    R, C = w.shape
    tr = _row_tile(R, C)

    def body(w_ref, g_ref, m_ref, v_ref, d_ref, mn_ref, vn_ref):
        d_ref[...], mn_ref[...], vn_ref[...] = _adam_math(w_ref[...], g_ref[...], m_ref[...], v_ref[...])

    spec = pl.BlockSpec((tr, C), lambda i: (i, 0))
    return pl.pallas_call(body, out_shape=(SDS((R, C), F32),) * 3, grid=(R // tr,), in_specs=[spec] * 4,
                          out_specs=(spec,) * 3, compiler_params=_cparams("parallel"), name=f"adamw_w{wi}")(w, g, m, v)


def _gain_update(gathered, w, m, v):
    def body(ga_ref, w_ref, m_ref, v_ref, g_ref, d_ref, mn_ref, vn_ref):
        g = ga_ref[0:8, :]
        for dev in range(1, 8):
            g = g + ga_ref[8 * dev:8 * dev + 8, :]
        g_ref[...] = g
        d_ref[...], mn_ref[...], vn_ref[...] = _adam_math(w_ref[...], g, m_ref[...], v_ref[...])

    return pl.pallas_call(body, out_shape=(SDS((8, 1024), F32),) * 4, name="gain_update")(gathered, w, m, v)


GROUP_FFN, GROUP_MIX, GROUP_IN = (4, 5, 6), (1, 2, 3), (0,)
REST = GROUP_MIX + GROUP_FFN


class _MeshComm:
    SCHEDULE = {
        "in_proj": [("ici", (1, 2, 3, 4))],
        "attn_fwd_g0": [("d2d", (1, 2, 3, 4)), ("ici", (5,))],
        "attn_fwd_g1": [("d2d", (5,)), ("ici", (6,))],
        "attn_fwd_g2": [("d2d", (6,))],
        "ffn_up_bwd": [("pair", GROUP_FFN)],
        "ret_bwd": [("pair", GROUP_MIX)],
        "attn_bwd_g0": [("chip", (4,))],
        "attn_bwd_g1": [("chip", (5, 1, 2, 3))],
        "attn_bwd_g2": [("chip", (6,))],
        "wgrad_in_sent": [("share", GROUP_FFN + GROUP_MIX)],
        "wgrad_in_kept": [("pair", GROUP_IN)],
        "in_proj_bwd": [("chip", GROUP_IN)],
    }

    def __init__(self, shards):
        xi, yi, ci = lax.axis_index("x"), lax.axis_index("y"), lax.axis_index("c")
        self.sidx = jnp.stack([2 * xi + yi, 2 * (1 - xi) + yi, 2 * xi + (1 - yi), 2 * (1 - xi) + (1 - yi), ci]).astype(jnp.int32)
        self.shards, self.full = shards, {}
        self.g, self.own, self.pb, self.half, self.red = {}, {}, {}, {}, {}

    def w_in(self):
        return _gather_now(GROUP_IN, [self.shards[0]])[0]

    def w_rest(self):
        f = self.full
        return f[1], f[2].reshape(D_MODEL, D_MODEL), f[3].reshape(D_MODEL, D_MODEL), f[4], f[5], f[6]

    def grads(self, by_wi):
        self.g.update(by_wi)

    def _exchange(self, stage, wis):
        pick = lambda table: [table[wi] for wi in wis]
        if stage == "ici":
            return _ex_gather_ici(wis, pick(self.shards))
        if stage == "d2d":
            return _ex_gather_d2d(wis, pick(self.full))
        if stage == "pair":
            return _ex_pair(wis, [self.g["in_sent"] if wi == 0 else self.g[wi] for wi in wis])
        if stage == "chip":
            return _ex_chip(wis, pick(self.pb))
        return _ex_share(wis, pick(self.half))

    def _landed(self, stage, wis, res):
        for wi, r in zip(wis, res):
            if stage in ("ici", "d2d"):
                self.full[wi] = r
            elif stage == "pair":
                self.own[wi], self.pb[wi] = _pair_sum(wi, self.g["in_kept"] if wi == 0 else self.g[wi], r, self.sidx)
            elif stage == "chip":
                self.half[wi] = _chip_sum(wi, self.own[wi], r, self.sidx)
            else:
                self.red[wi] = r

    def carry(self, point):
        return [self._exchange(stage, wis) for stage, wis in self.SCHEDULE.get(point, ())]

    def took(self, point, xres):
        for (stage, wis), res in zip(self.SCHEDULE.get(point, ()), xres):
            self._landed(stage, wis, res)

    def reduced(self):
        self._landed("share", GROUP_IN, _exchange_call(self._exchange("share", GROUP_IN), "share_w_in"))
        return [self.red[wi] for wi in range(N_W)]


def kernel(x, norm_mix_g, w_in, w_out_attn, w_out_ret, w_out, norm_ffn_g, w_ffn_gate, w_ffn_up, w_ffn_down, norm_final_g, loss_target, m_norm_mix_g, m_w_in, m_w_out_attn, m_w_out_ret, m_w_out, m_norm_ffn_g, m_w_ffn_gate, m_w_ffn_up, m_w_ffn_down, m_norm_final_g, v_norm_mix_g, v_w_in, v_w_out_attn, v_w_out_ret, v_w_out, v_norm_ffn_g, v_w_ffn_gate, v_w_ffn_up, v_w_ffn_down, v_norm_final_g):
    ws = (w_in, w_out_attn, w_out_ret, w_out, w_ffn_gate, w_ffn_up, w_ffn_down)
    ms = (m_w_in, m_w_out_attn, m_w_out_ret, m_w_out, m_w_ffn_gate, m_w_ffn_up, m_w_ffn_down)
    vs = (v_w_in, v_w_out_attn, v_w_out_ret, v_w_out, v_w_ffn_gate, v_w_ffn_up, v_w_ffn_down)
    shard2d = lambda a, wi: a.reshape(W_SHARD[wi])

    comm = _MeshComm([_cast_bf16(shard2d(w, wi)) for wi, w in enumerate(ws)])
    g3 = norm_final_g.reshape(1, D_MODEL)
    loss_p, grad_x, gain_g = _step(x[0], loss_target[0], norm_mix_g, norm_ffn_g, g3, comm)
    gred = comm.reduced()

    outs_g, outs_d, outs_m, outs_v = [], [], [], []
    for wi in range(N_W):
        g2d = gred[wi].reshape(W_SHARD[wi])
        dlt, mn, vn = _adamw(wi, shard2d(ws[wi], wi), g2d, shard2d(ms[wi], wi), shard2d(vs[wi], wi))
        for lst, a in ((outs_g, g2d), (outs_d, dlt), (outs_m, mn), (outs_v, vn)):
            lst.append(a.reshape(ws[wi].shape))

    pad8 = lambda rows: jnp.concatenate([r.reshape(1, D_MODEL) for r in rows] + [jnp.zeros((5, D_MODEL), F32)], axis=0)
    gathered = _gain_allgather(pad8(gain_g))
    gg, gd, gm, gv = _gain_update(gathered, pad8((norm_mix_g, norm_ffn_g, norm_final_g)),
                                  pad8((m_norm_mix_g, m_norm_ffn_g, m_norm_final_g)),
                                  pad8((v_norm_mix_g, v_norm_ffn_g, v_norm_final_g)))
    loss = lax.psum(loss_p, ("x", "y", "c"))

    def assemble(gain_rows, wlist):
        return (gain_rows[0:1], wlist[0], wlist[1], wlist[2], wlist[3], gain_rows[1:2],
                wlist[4], wlist[5], wlist[6], gain_rows[2])

    return (loss, grad_x[None], *assemble(gg, outs_g), *assemble(gd, outs_d), *assemble(gm, outs_m), *assemble(gv, outs_v))
```

```python
import functools
import math

import numpy as np
import jax
import jax.numpy as jnp
from jax import lax
from jax.experimental import pallas as pl
from jax.experimental.pallas import tpu as pltpu

F32, BF16 = jnp.float32, jnp.bfloat16
SDS = jax.ShapeDtypeStruct
MESH = pl.DeviceIdType.MESH

D_MODEL = 1024
PROJ_W = 9728
COLB = 512
N_COLB = PROJ_W // COLB
QA_B, KA_B, VA_B = 0, 3, 6
QR_B, KR_B = 9, 10
FFN_HID = 2816
N_SHARD = 4
HID_S = FFN_HID // N_SHARD
W_IN_S = PROJ_W // N_SHARD
DILATIONS = (1, 4, 16)
BLK = 128
RET_HEADS = 4
ROPE_THETA = 10000.0
NORM_EPS = 1e-6
ADAM_LR, ADAM_B1, ADAM_B2, ADAM_EPS, ADAM_WD, ADAM_STEP = 0.001, 0.9, 0.999, 1e-08, 0.01, 10
VMEM_LIMIT = 56 << 20


def _cparams(*sem):
    return pltpu.CompilerParams(dimension_semantics=sem or None, vmem_limit_bytes=VMEM_LIMIT)


def _dot(a, b):
    return jnp.dot(a, b, preferred_element_type=F32)


def _dot_nt(a, b):
    return lax.dot_general(a, b, (((1,), (1,)), ((), ())), preferred_element_type=F32)


def _dot_tn(a, b):
    return lax.dot_general(a, b, (((0,), (0,)), ((), ())), preferred_element_type=F32)


def _row_pieces(tm, sub=512):
    return [slice(i, i + sub) for i in range(0, tm, sub)]


def _sigmoid(z):
    return 0.5 * jnp.tanh(0.5 * z) + 0.5


ANY = pl.BlockSpec(memory_space=pl.ANY)


class _Exchange:
    def __init__(self, ins, out_shapes, aliases, n_sem, n_loc, build):
        self.ins, self.out_shapes, self.aliases = list(ins), list(out_shapes), dict(aliases)
        self.n_sem, self.n_loc, self.build = n_sem, n_loc, build

    def sems(self):
        return [pltpu.SemaphoreType.DMA((self.n_sem,)), pltpu.SemaphoreType.DMA((self.n_sem,)),
                pltpu.SemaphoreType.DMA((max(self.n_loc, 1),))]


def _exchange_call(ex, name):
    n_in, n_out = len(ex.ins), len(ex.out_shapes)

    def body(*refs):
        starts, waits = ex.build(refs[:n_in], refs[n_in:n_in + n_out], *refs[n_in + n_out:])
        for cp in starts:
            cp.start()
        for w in waits:
            w()

    return pl.pallas_call(body, out_shape=tuple(ex.out_shapes), in_specs=[ANY] * n_in, out_specs=tuple([ANY] * n_out),
                          input_output_aliases=ex.aliases, scratch_shapes=ex.sems(), name=name)(*ex.ins)


def _carrier_call(body, args, *, out_shape, grid, in_specs, out_specs, scratch_shapes=(), sem, name, exchanges=(),
                  prefetch=None):
    out_shape, out_specs = tuple(out_shape), tuple(out_specs)
    n_in, n_out, n_scr = len(args), len(out_shape), len(scratch_shapes)
    n_pre = 0 if prefetch is None else 1
    x_args, x_outs, aliases, x_scr, spans = [], [], {}, [], []
    for ex in exchanges:
        i0, o0 = len(x_args), len(x_outs)
        for a, o in ex.aliases.items():
            aliases[n_pre + n_in + i0 + a] = n_out + o0 + o
        x_args += ex.ins
        x_outs += ex.out_shapes
        x_scr += ex.sems()
        spans.append((i0, len(ex.ins), o0, len(ex.out_shapes)))
    nx_in, nx_out = len(x_args), len(x_outs)

    def wrapped(*refs):
        refs = refs[n_pre:]
        ins, xin = refs[:n_in], refs[n_in:n_in + nx_in]
        o_base = n_in + nx_in
        outs, xout = refs[o_base:o_base + n_out], refs[o_base + n_out:o_base + n_out + nx_out]
        s_base = o_base + n_out + nx_out
        scr, xs = refs[s_base:s_base + n_scr], refs[s_base + n_scr:]

        def built(e):
            i0, ni, o0, no = spans[e]
            return exchanges[e].build(xin[i0:i0 + ni], xout[o0:o0 + no], *xs[3 * e:3 * e + 3])

        if exchanges:
            first = functools.reduce(jnp.logical_and, [pl.program_id(k) == 0 for k in range(len(grid))])
            last = functools.reduce(jnp.logical_and, [pl.program_id(k) == grid[k] - 1 for k in range(len(grid))])

            @pl.when(first)
            def _():
                for e in range(len(exchanges)):
                    for cp in built(e)[0]:
                        cp.start()

        body(*ins, *outs, *scr)

        if exchanges:
            @pl.when(last)
            def _():
                for e in range(len(exchanges)):
                    for w in built(e)[1]:
                        w()

    all_in, all_out = list(in_specs) + [ANY] * nx_in, out_specs + tuple([ANY] * nx_out)
    all_scr = list(scratch_shapes) + x_scr
    cparams = _cparams(*(sem if not exchanges else ("arbitrary",) * len(grid)))
    if prefetch is None:
        res = pl.pallas_call(wrapped, out_shape=out_shape + tuple(x_outs), grid=grid, in_specs=all_in, out_specs=all_out,
                             scratch_shapes=all_scr, input_output_aliases=aliases, compiler_params=cparams,
                             name=name)(*args, *x_args)
    else:
        gs = pltpu.PrefetchScalarGridSpec(num_scalar_prefetch=1, grid=grid, in_specs=all_in, out_specs=all_out,
                                          scratch_shapes=all_scr)
        res = pl.pallas_call(wrapped, out_shape=out_shape + tuple(x_outs), grid_spec=gs, input_output_aliases=aliases,
                             compiler_params=cparams, name=name)(prefetch, *args, *x_args)
    xres = [tuple(res[n_out + o0:n_out + o0 + no]) for (_, _, o0, no) in spans]
    return tuple(res[:n_out]), xres


def _tables(S):
    pos = jnp.arange(S, dtype=F32)
    lane = np.arange(128)
    inv = ROPE_THETA ** (-jnp.arange(0, 64, 2, dtype=F32) / 64)
    ang = pos[:, None] * inv[None, :]
    idx = (lane % 64) % 32
    c, s = jnp.cos(ang)[:, idx], jnp.sin(ang)[:, idx]
    first = jnp.asarray((lane % 64) < 32)[None, :]
    rope = jnp.stack([c, jnp.where(first, 0.0, s), jnp.where(first, -s, 0.0)])
    base = 1.0 / (ROPE_THETA ** jnp.linspace(0.0, 1.0, 64, dtype=F32))
    ang2 = pos[:, None] * base[None, :]
    c2, s2 = jnp.cos(ang2)[:, lane // 2], jnp.sin(ang2)[:, lane // 2]
    even = jnp.asarray(lane % 2 == 0)[None, :]
    th = jnp.stack([c2, jnp.where(even, 0.0, s2), jnp.where(even, -s2, 0.0)])
    return jnp.stack([rope, th, th * (128 ** -0.5)]).astype(F32)


def _rot(a, c, sa, sb, shift):
    return a * c + pltpu.roll(a, shift, 1) * sa + pltpu.roll(a, 128 - shift, 1) * sb


def _unrot(g, c, sa, sb, shift):
    return g * c + pltpu.roll(g * sa, 128 - shift, 1) + pltpu.roll(g * sb, shift, 1)


def _ret_consts():
    h = np.arange(RET_HEADS, dtype=np.float64)
    log_g = np.log1p(-(2.0 ** (-5.0 - h)))
    idx = np.arange(BLK, dtype=np.float64)
    diff = idx[:, None] - idx[None, :]
    dmask = np.where(diff[None] >= 0, np.exp(np.maximum(diff, 0.0)[None] * log_g[:, None, None]), 0.0)
    zeta = np.exp((BLK - 1 - idx)[None, :] * log_g[:, None])
    xi = np.exp((idx + 1.0)[None, :] * log_g[:, None])
    dec = np.exp(BLK * log_g)
    rep = lambda v: np.broadcast_to(v[:, :, None], (RET_HEADS, BLK, 128))
    return (jnp.asarray(dmask, F32), jnp.asarray(rep(zeta), F32), jnp.asarray(rep(xi), F32),
            jnp.asarray(np.broadcast_to(dec[:, None, None], (RET_HEADS, 8, 256)), F32))


def _rms_fwd(x, g):
    S = x.shape[0]
    tm = 512

    def body(x_ref, g_ref, h_ref, ht_ref):
        xv = x_ref[...]
        r = lax.rsqrt(jnp.mean(xv * xv, axis=-1, keepdims=True) + NORM_EPS)
        h = xv * r * g_ref[...]
        h_ref[...] = h.astype(BF16)
        ht_ref[...] = h.T.astype(BF16)

    return pl.pallas_call(
        body, out_shape=(SDS((S, D_MODEL), BF16), SDS((D_MODEL, S), BF16)), grid=(S // tm,),
        in_specs=[pl.BlockSpec((tm, D_MODEL), lambda i: (i, 0)), pl.BlockSpec((1, D_MODEL), lambda i: (0, 0))],
        out_specs=(pl.BlockSpec((tm, D_MODEL), lambda i: (i, 0)), pl.BlockSpec((D_MODEL, tm), lambda i: (0, i))),
        compiler_params=_cparams("parallel"), name="rms_fwd")(x, g)


def _in_proj(h, w_in, tab, exchanges=()):
    S = h.shape[0]
    tm = min(S, 2048)

    def body(h_ref, w_ref, t_ref, o_ref):
        j = pl.program_id(1)
        is_rope = j < 6
        is_theta = (j == QR_B) | (j == KR_B)
        sub = 512

        def rotated(shift):
            for i in range(tm // sub):
                rows = slice(i * sub, (i + 1) * sub)
                acc = _dot(h_ref[rows, :], w_ref[...])
                c, sa, sb = t_ref[0, 0, rows, :], t_ref[0, 1, rows, :], t_ref[0, 2, rows, :]
                for k in range(COLB // 128):
                    sl = slice(k * 128, (k + 1) * 128)
                    o_ref[rows, sl] = _rot(acc[:, sl], c, sa, sb, shift).astype(BF16)

        @pl.when(is_rope)
        def _():
            rotated(32)

        @pl.when(is_theta)
        def _():
            rotated(1)

        @pl.when(jnp.logical_not(is_rope | is_theta))
        def _():
            o_ref[...] = _dot(h_ref[...], w_ref[...]).astype(BF16)

    def tab_map(i, j):
        return (jnp.where(j == QR_B, 1, jnp.where(j == KR_B, 2, 0)), 0, i, 0)

    (proj,), xres = _carrier_call(
        body, (h, w_in, tab), out_shape=(SDS((S, PROJ_W), BF16),), grid=(S // tm, N_COLB),
        in_specs=[pl.BlockSpec((tm, D_MODEL), lambda i, j: (i, 0)),
                  pl.BlockSpec((D_MODEL, COLB), lambda i, j: (0, j)),
                  pl.BlockSpec((1, 3, tm, 128), tab_map)],
        out_specs=(pl.BlockSpec((tm, COLB), lambda i, j: (i, j)),),
        sem=("parallel", "arbitrary"), name="in_proj", exchanges=exchanges)
    return proj, xres


def _band_mask(n):
    qi = lax.broadcasted_iota(jnp.int32, (BLK, 2 * BLK), 0)
    kj = lax.broadcasted_iota(jnp.int32, (BLK, 2 * BLK), 1)
    dist = BLK + qi - kj
    return (dist >= 0) & (dist <= BLK) & ((kj >= BLK) | (n > 0))


def _qkv_col(d, gi):
    if d == 1:
        return lambda t, r: 3 * t + gi
    return lambda t, r: 3 * r + t


def _attn_fwd(qkv, d, gi, exchanges=()):
    L = qkv.shape[0]
    nb = L // BLK

    def body(q_ref, kc_ref, kp_ref, vc_ref, vp_ref, o_ref, lse_ref):
        n = pl.program_id(1)
        mask = _band_mask(n)
        mask2 = jnp.concatenate([mask, mask], axis=0)
        lane = lax.broadcasted_iota(jnp.int32, (BLK, 128), 1)
        lo = lane < 64
        lse_all = jnp.zeros((BLK, 128), F32)
        for c in range(4):
            sl = slice(c * 128, (c + 1) * 128)
            q = q_ref[:, sl]
            k = jnp.concatenate([kp_ref[:, sl], kc_ref[:, sl]], axis=0)
            v = jnp.concatenate([vp_ref[:, sl], vc_ref[:, sl]], axis=0)
            q2 = jnp.concatenate([jnp.where(lo, q, jnp.zeros_like(q)), jnp.where(lo, jnp.zeros_like(q), q)], axis=0)
            s = jnp.where(mask2, _dot_nt(q2, k) * 0.125, jnp.float32(-1e30))
            m = jnp.max(s, axis=-1, keepdims=True)
            p = jnp.exp(s - m)
            l = jnp.sum(p, axis=-1, keepdims=True)
            o2 = _dot((p / l).astype(BF16), v)
            o_ref[:, sl] = jnp.where(lo, o2[:BLK], o2[BLK:])
            lse = m + jnp.log(l)
            lse_all = jnp.where(lane // 16 == 2 * c, lse[:BLK], jnp.where(lane // 16 == 2 * c + 1, lse[BLK:], lse_all))
        lse_ref[...] = lse_all

    prev = lambda n: jnp.maximum(n - 1, 0)
    col = _qkv_col(d, gi)
    return _carrier_call(
        body, (qkv,) * 5, out_shape=(SDS((L, d * 512), F32), SDS((L, d * 128), F32)), grid=(d, nb),
        in_specs=[pl.BlockSpec((BLK, 512), lambda r, n: (n, col(0, r))),
                  pl.BlockSpec((BLK, 512), lambda r, n: (n, col(1, r))),
                  pl.BlockSpec((BLK, 512), lambda r, n: (prev(n), col(1, r))),
                  pl.BlockSpec((BLK, 512), lambda r, n: (n, col(2, r))),
                  pl.BlockSpec((BLK, 512), lambda r, n: (prev(n), col(2, r)))],
        out_specs=(pl.BlockSpec((BLK, 512), lambda r, n: (n, r)),
                   pl.BlockSpec((BLK, 128), lambda r, n: (n, r))),
        sem=("parallel", "arbitrary"), name=f"attn_fwd_g{gi}", exchanges=exchanges)


def _qkv_to_sub(proj, d, gi):
    S = proj.shape[0]
    tm = 512
    n = tm // d

    def body(q_ref, k_ref, v_ref, o_ref, scr):
        for t, ref in enumerate((q_ref, k_ref, v_ref)):
            for c in range(4):
                scr[c] = ref[:, c * 128:(c + 1) * 128].astype(F32)
            for r in range(d):
                for c in range(4):
                    col = (3 * r + t) * 512 + c * 128
                    o_ref[:, col:col + 128] = scr[c, pl.ds(r, n, stride=d), :].astype(BF16)

    return pl.pallas_call(
        body, out_shape=SDS((S // d, d * 1536), BF16), grid=(S // tm,),
        in_specs=[pl.BlockSpec((tm, 512), lambda i, b=b: (i, b + gi)) for b in (QA_B, KA_B, VA_B)],
        out_specs=pl.BlockSpec((n, d * 1536), lambda i: (i, 0)),
        scratch_shapes=[pltpu.VMEM((4, tm, 128), F32)],
        compiler_params=_cparams("parallel"), name=f"qkv_to_sub_g{gi}")(proj, proj, proj)


def _attn_merge(os_, lses):
    S = os_[0].shape[0]
    tm = 512

    def body(o0, o1, o2, l0, l1, l2, att_ref, lt_ref, so1, so2, sl1, sl2):
        lo = lax.broadcasted_iota(jnp.int32, (tm, 128), 1) < 64

        def natural(ref, d, scr, width):
            nch = width // 128
            if d == 1:
                return [ref[:, c * 128:(c + 1) * 128] for c in range(nch)]
            for r in range(d):
                for c in range(nch):
                    scr[c, pl.ds(r, tm // d, stride=d), :] = ref[:, r * width + c * 128:r * width + (c + 1) * 128]
            return [scr[c] for c in range(nch)]

        ls = [natural(l, d, s, 128)[0] for l, d, s in zip((l0, l1, l2), DILATIONS, (None, sl1, sl2))]
        m = jnp.maximum(jnp.maximum(ls[0], ls[1]), ls[2])
        es = [jnp.exp(v - m) for v in ls]
        z = es[0] + es[1] + es[2]
        lt_ref[...] = m + jnp.log(z)
        ws = [e / z for e in es]
        o_nat = [natural(o, d, s, 512) for o, d, s in zip((o0, o1, o2), DILATIONS, (None, so1, so2))]
        for c in range(4):
            acc = jnp.zeros((tm, 128), F32)
            for g in range(3):
                w_lo = jnp.broadcast_to(ws[g][:, 32 * c:32 * c + 1], (tm, 128))
                w_hi = jnp.broadcast_to(ws[g][:, 32 * c + 16:32 * c + 17], (tm, 128))
                acc = acc + jnp.where(lo, w_lo, w_hi) * o_nat[g][c]
            att_ref[:, c * 128:(c + 1) * 128] = acc.astype(BF16)

    sub = lambda w: [pl.BlockSpec((tm // d, d * w), lambda i: (i, 0)) for d in DILATIONS]
    return pl.pallas_call(
        body, out_shape=(SDS((S, 512), BF16), SDS((S, 128), F32)), grid=(S // tm,),
        in_specs=sub(512) + sub(128),
        out_specs=(pl.BlockSpec((tm, 512), lambda i: (i, 0)), pl.BlockSpec((tm, 128), lambda i: (i, 0))),
        scratch_shapes=[pltpu.VMEM((4, tm, 128), F32), pltpu.VMEM((4, tm, 128), F32),
                        pltpu.VMEM((1, tm, 128), F32), pltpu.VMEM((1, tm, 128), F32)],
        compiler_params=_cparams("parallel"), name="attn_merge")(*os_, *lses)


def _assemble_dproj(att_grads, dqr, dkr, dvr, dgr, dga, dgrr):
    S = dqr.shape[0]
    tm = 256

    def body(*refs):
        a = [refs[3 * t:3 * t + 3] for t in range(3)]
        dqr_ref, dkr_ref, dvr_ref, dgr_ref, dga_ref, dgrr_ref, o_ref, scr = refs[9:]
        for t in range(3):
            for g, d in enumerate(DILATIONS):
                base = (3 * t + g) * COLB
                if d == 1:
                    o_ref[:, base:base + COLB] = a[t][g][...]
                    continue
                for c in range(4):
                    for r in range(d):
                        scr[c, pl.ds(r, tm // d, stride=d), :] = a[t][g][:, r * 512 + c * 128:r * 512 + (c + 1) * 128].astype(F32)
                    o_ref[:, base + c * 128:base + (c + 1) * 128] = scr[c].astype(BF16)
        o_ref[:, 9 * COLB:10 * COLB] = dqr_ref[...]
        o_ref[:, 10 * COLB:11 * COLB] = dkr_ref[...]
        o_ref[:, 11 * COLB:13 * COLB] = dvr_ref[...]
        o_ref[:, 13 * COLB:15 * COLB] = dgr_ref[...]
        o_ref[:, 15 * COLB:17 * COLB] = dga_ref[...]
        o_ref[:, 17 * COLB:19 * COLB] = dgrr_ref[...]

    sub = [pl.BlockSpec((tm // d, d * 512), lambda i: (i, 0)) for d in DILATIONS]
    row = lambda w: pl.BlockSpec((tm, w), lambda i: (i, 0))
    flat = [att_grads[t][g] for t in range(3) for g in range(3)]
    return pl.pallas_call(
        body, out_shape=SDS((S, PROJ_W), BF16), grid=(S // tm,),
        in_specs=sub * 3 + [row(512), row(512), row(1024), row(1024), row(1024), row(1024)],
        out_specs=row(PROJ_W), scratch_shapes=[pltpu.VMEM((4, tm, 128), F32)],
        compiler_params=_cparams("parallel"), name="assemble_dproj")(*flat, dqr, dkr, dvr, dgr, dga, dgrr)


def _ret_fwd(proj, consts):
    S = proj.shape[0]
    nc = S // BLK
    dmask, zeta, xi, dec = consts

    def body(q_ref, k_ref, v0_ref, v1_ref, g0_ref, g1_ref, dm_ref, z_ref, x_ref, dec_ref,
             y_ref, rn_ref, rs_ref, st_ref, R):
        @pl.when(pl.program_id(0) == 0)
        def _():
            R[...] = jnp.zeros_like(R)

        lane16 = lax.broadcasted_iota(jnp.int32, (BLK, 128), 1) // 16
        rs_all = jnp.zeros((BLK, 128), F32)
        for h in range(RET_HEADS):
            hs = slice(h * 128, (h + 1) * 128)
            vs = slice((h % 2) * 256, (h % 2 + 1) * 256)
            os_ = slice(h * 256, (h + 1) * 256)
            q, k = q_ref[:, hs], k_ref[:, hs]
            v = (v0_ref if h < 2 else v1_ref)[:, vs]
            Rb = R[h].astype(BF16)
            st_ref[h] = Rb
            s = _dot_nt(q, k) * dm_ref[h]
            o = _dot(s.astype(BF16), v) + _dot((q.astype(F32) * x_ref[h]).astype(BF16), Rb)
            kz = (k.astype(F32) * z_ref[h]).astype(BF16)
            R[h] = R[h] * dec_ref[h, 0:1, :] + _dot_tn(kz, v)
            mu = jnp.mean(o, axis=-1, keepdims=True)
            oc = o - mu
            rstd = lax.rsqrt(jnp.mean(oc * oc, axis=-1, keepdims=True) + NORM_EPS)
            rn = oc * rstd
            gr = (g0_ref if h < 2 else g1_ref)[:, vs].astype(F32)
            y_ref[:, os_] = (rn * gr * _sigmoid(gr)).astype(BF16)
            rn_ref[:, os_] = rn.astype(BF16)
            rs_all = jnp.where(lane16 == h, rstd, rs_all)
        rs_ref[...] = rs_all

    cst = lambda shape: pl.BlockSpec(shape, lambda c: (0, 0, 0))
    blk = lambda j: pl.BlockSpec((BLK, 512), lambda c: (c, j))
    return pl.pallas_call(
        body,
        out_shape=(SDS((S, 1024), BF16), SDS((S, 1024), BF16), SDS((S, 128), F32), SDS((RET_HEADS, nc, BLK, 256), BF16)),
        grid=(nc,),
        in_specs=[blk(QR_B), blk(KR_B), blk(11), blk(12), blk(13), blk(14),
                  cst((RET_HEADS, BLK, BLK)), cst((RET_HEADS, BLK, 128)), cst((RET_HEADS, BLK, 128)), cst((RET_HEADS, 8, 256))],
        out_specs=(pl.BlockSpec((BLK, 1024), lambda c: (c, 0)), pl.BlockSpec((BLK, 1024), lambda c: (c, 0)),
                   pl.BlockSpec((BLK, 128), lambda c: (c, 0)),
                   pl.BlockSpec((RET_HEADS, None, BLK, 256), lambda c: (0, c, 0, 0))),
        scratch_shapes=[pltpu.VMEM((RET_HEADS, BLK, 256), F32)],
        compiler_params=_cparams("arbitrary"), name="ret_fwd")(proj, proj, proj, proj, proj, proj, dmask, zeta, xi, dec)


def _branch_merge(att, yrin, proj, wa, wr):
    S = att.shape[0]
    tm = min(S, 2048)

    def body(a_ref, y_ref, ga_ref, gr_ref, wa_ref, wr_ref, m_ref, ya_ref, yr_ref):
        for rows in _row_pieces(tm):
            ya = _dot(a_ref[rows, :], wa_ref[...])
            yr = _dot(y_ref[rows, :], wr_ref[...])
            m_ref[rows, :] = (_sigmoid(ga_ref[rows, :].astype(F32)) * ya
                              + _sigmoid(gr_ref[rows, :].astype(F32)) * yr).astype(BF16)
            ya_ref[rows, :] = ya.astype(BF16)
            yr_ref[rows, :] = yr.astype(BF16)

    ospec = pl.BlockSpec((tm, 512), lambda i, j: (i, j))
    return pl.pallas_call(
        body, out_shape=(SDS((S, D_MODEL), BF16),) * 3, grid=(S // tm, 2),
        in_specs=[pl.BlockSpec((tm, 512), lambda i, j: (i, 0)), pl.BlockSpec((tm, 1024), lambda i, j: (i, 0)),
                  pl.BlockSpec((tm, 512), lambda i, j: (i, 15 + j)), pl.BlockSpec((tm, 512), lambda i, j: (i, 17 + j)),
                  pl.BlockSpec((512, 512), lambda i, j: (0, j)), pl.BlockSpec((1024, 512), lambda i, j: (0, j))],
        out_specs=(ospec, ospec, ospec),
        compiler_params=_cparams("parallel", "arbitrary"), name="branch_merge")(att, yrin, proj, proj, wa, wr)


def _out_proj(merged, wo, x, g2):
    S = x.shape[0]
    tm = 1024

    def body(m_ref, w_ref, x_ref, g_ref, x1_ref, h2_ref):
        x1 = x_ref[...] + _dot(m_ref[...], w_ref[...])
        x1_ref[...] = x1
        r = lax.rsqrt(jnp.mean(x1 * x1, axis=-1, keepdims=True) + NORM_EPS)
        h2_ref[...] = (x1 * r * g_ref[...]).astype(BF16)

    row = pl.BlockSpec((tm, D_MODEL), lambda i: (i, 0))
    return pl.pallas_call(
        body, out_shape=(SDS((S, D_MODEL), F32), SDS((S, D_MODEL), BF16)), grid=(S // tm,),
        in_specs=[row, pl.BlockSpec((D_MODEL, D_MODEL), lambda i: (0, 0)), row, pl.BlockSpec((1, D_MODEL), lambda i: (0, 0))],
        out_specs=(row, row), compiler_params=_cparams("parallel"), name="out_proj")(merged, wo, x, g2)


def _ffn_up(h2, wg, wu):
    S = h2.shape[0]
    tm = min(S, 2048)

    def body(h_ref, wg_ref, wu_ref, g_ref, u_ref, a_ref):
        for rows in _row_pieces(tm):
            hv = h_ref[rows, :]
            g = _dot(hv, wg_ref[...])
            u = _dot(hv, wu_ref[...])
            g_ref[rows, :] = g.astype(BF16)
            u_ref[rows, :] = u.astype(BF16)
            a_ref[rows, :] = (g * _sigmoid(g) * u).astype(BF16)

    wspec = pl.BlockSpec((None, D_MODEL, HID_S), lambda i, s: (s, 0, 0))
    ospec = pl.BlockSpec((None, tm, HID_S), lambda i, s: (s, i, 0))
    return pl.pallas_call(
        body, out_shape=(SDS((N_SHARD, S, HID_S), BF16),) * 3, grid=(S // tm, N_SHARD),
        in_specs=[pl.BlockSpec((tm, D_MODEL), lambda i, s: (i, 0)), wspec, wspec],
        out_specs=(ospec, ospec, ospec),
        compiler_params=_cparams("parallel", "arbitrary"), name="ffn_up")(h2, wg, wu)


def _ffn_down_loss(act, wd, x1, g3, tgt):
    S = x1.shape[0]
    tm = 512

    def body(a_ref, w_ref, x_ref, g_ref, t_ref, dx_ref, dxb_ref, dg_ref, ls_ref):
        @pl.when(pl.program_id(0) == 0)
        def _():
            dg_ref[...] = jnp.zeros_like(dg_ref)
            ls_ref[...] = jnp.zeros_like(ls_ref)

        g = g_ref[...]
        for rows in _row_pieces(tm, 256):
            y = _dot(a_ref[0, rows, :], w_ref[0])
            for s in range(1, N_SHARD):
                y = y + _dot(a_ref[s, rows, :], w_ref[s])
            x2 = x_ref[rows, :] + y
            r = lax.rsqrt(jnp.mean(x2 * x2, axis=-1, keepdims=True) + NORM_EPS)
            xh = x2 * r
            err = xh * g - t_ref[rows, :]
            ls_ref[...] += jnp.sum(jnp.sum(err * err, axis=-1, keepdims=True), axis=0, keepdims=True) * (0.5 / D_MODEL)
            dy = err * (1.0 / D_MODEL)
            dg_ref[...] += jnp.sum(dy * xh, axis=0, keepdims=True)
            dxh = dy * g
            dx = r * (dxh - xh * jnp.mean(dxh * xh, axis=-1, keepdims=True))
            dx_ref[rows, :] = dx
            dxb_ref[rows, :] = dx.astype(BF16)

    row = pl.BlockSpec((tm, D_MODEL), lambda i: (i, 0))
    vec = pl.BlockSpec((1, D_MODEL), lambda i: (0, 0))
    return pl.pallas_call(
        body, out_shape=(SDS((S, D_MODEL), F32), SDS((S, D_MODEL), BF16), SDS((1, D_MODEL), F32), SDS((8, 128), F32)),
        grid=(S // tm,),
        in_specs=[pl.BlockSpec((N_SHARD, tm, HID_S), lambda i: (0, i, 0)),
                  pl.BlockSpec((N_SHARD, HID_S, D_MODEL), lambda i: (0, 0, 0), pipeline_mode=pl.Buffered(1)),
                  row, vec, row],
        out_specs=(row, row, vec, pl.BlockSpec((8, 128), lambda i: (0, 0))),
        compiler_params=_cparams("arbitrary"), name="ffn_down_loss")(act, wd, x1, g3, tgt)


def _ffn_down_bwd(dx2b, wd, gte, up):
    S = dx2b.shape[0]
    tm = min(S, 2048)

    def body(d_ref, w_ref, g_ref, u_ref, dg_ref, du_ref):
        for rows in _row_pieces(tm, 256):
            da = _dot_nt(d_ref[rows, :], w_ref[...])
            g = g_ref[rows, :].astype(F32)
            sg = _sigmoid(g)
            dg_ref[rows, :] = (da * u_ref[rows, :].astype(F32) * sg * (1.0 + g * (1.0 - sg))).astype(BF16)
            du_ref[rows, :] = (da * g * sg).astype(BF16)

    aspec = pl.BlockSpec((None, tm, HID_S), lambda i, s: (s, i, 0))
    return pl.pallas_call(
        body, out_shape=(SDS((N_SHARD, S, HID_S), BF16),) * 2, grid=(S // tm, N_SHARD),
        in_specs=[pl.BlockSpec((tm, D_MODEL), lambda i, s: (i, 0)),
                  pl.BlockSpec((None, HID_S, D_MODEL), lambda i, s: (s, 0, 0)), aspec, aspec],
        out_specs=(aspec, aspec),
        compiler_params=_cparams("parallel", "arbitrary"), name="ffn_down_bwd")(dx2b, wd, gte, up)


def _wgrad(name, a, b, a_spec, b_spec, out_shape, out_spec, n_par, S):
    tk = 2048

    def body(a_ref, b_ref, o_ref):
        @pl.when(pl.program_id(1) == 0)
        def _():
            o_ref[...] = jnp.zeros_like(o_ref)

        o_ref[...] += _dot_tn(a_ref[...], b_ref[...])

    return pl.pallas_call(
        body, out_shape=SDS(out_shape, F32), grid=(n_par, S // tk),
        in_specs=[a_spec(tk), b_spec(tk)], out_specs=out_spec,
        compiler_params=_cparams("parallel", "arbitrary"), name=name)(a, b)


def _ffn_up_bwd(dgte, dup, wg, wu, x1, g2, dx2, exchanges=()):
    S = x1.shape[0]
    tm = 512

    def body(dg_ref, du_ref, wg_ref, wu_ref, x_ref, g_ref, dx2_ref, dx_ref, dxb_ref, dgn_ref):
        @pl.when(pl.program_id(0) == 0)
        def _():
            dgn_ref[...] = jnp.zeros_like(dgn_ref)

        for rows in _row_pieces(tm, 256):
            dh = _dot_nt(dg_ref[0, rows, :], wg_ref[0]) + _dot_nt(du_ref[0, rows, :], wu_ref[0])
            for s in range(1, N_SHARD):
                dh = dh + _dot_nt(dg_ref[s, rows, :], wg_ref[s]) + _dot_nt(du_ref[s, rows, :], wu_ref[s])
            xv = x_ref[rows, :]
            r = lax.rsqrt(jnp.mean(xv * xv, axis=-1, keepdims=True) + NORM_EPS)
            xh = xv * r
            dgn_ref[...] += jnp.sum(dh * xh, axis=0, keepdims=True)
            dxh = dh * g_ref[...]
            dx = dx2_ref[rows, :] + r * (dxh - xh * jnp.mean(dxh * xh, axis=-1, keepdims=True))
            dx_ref[rows, :] = dx
            dxb_ref[rows, :] = dx.astype(BF16)

    row = pl.BlockSpec((tm, D_MODEL), lambda i: (i, 0))
    vec = pl.BlockSpec((1, D_MODEL), lambda i: (0, 0))
    aspec = pl.BlockSpec((N_SHARD, tm, HID_S), lambda i: (0, i, 0))
    wspec = pl.BlockSpec((N_SHARD, D_MODEL, HID_S), lambda i: (0, 0, 0), pipeline_mode=pl.Buffered(1))
    return _carrier_call(
        body, (dgte, dup, wg, wu, x1, g2, dx2),
        out_shape=(SDS((S, D_MODEL), F32), SDS((S, D_MODEL), BF16), SDS((1, D_MODEL), F32)),
        grid=(S // tm,),
        in_specs=[aspec, aspec, wspec, wspec, row, vec, row], out_specs=(row, row, vec),
        sem=("arbitrary",), name="ffn_up_bwd", exchanges=exchanges)


def _out_proj_bwd(dx1b, wo, proj, ya, yr):
    S = dx1b.shape[0]
    tm = min(S, 2048)

    def body(d_ref, w_ref, ga_ref, gr_ref, ya_ref, yr_ref, dya_ref, dyr_ref, dga_ref, dgr_ref):
        for rows in _row_pieces(tm, 256):
            dm = _dot_nt(d_ref[rows, :], w_ref[...])
            sa = _sigmoid(ga_ref[rows, :].astype(F32))
            sr = _sigmoid(gr_ref[rows, :].astype(F32))
            dya_ref[rows, :] = (dm * sa).astype(BF16)
            dyr_ref[rows, :] = (dm * sr).astype(BF16)
            dga_ref[rows, :] = (dm * ya_ref[rows, :].astype(F32) * sa * (1.0 - sa)).astype(BF16)
            dgr_ref[rows, :] = (dm * yr_ref[rows, :].astype(F32) * sr * (1.0 - sr)).astype(BF16)

    blk = pl.BlockSpec((tm, 512), lambda i, j: (i, j))
    return pl.pallas_call(
        body, out_shape=(SDS((S, D_MODEL), BF16),) * 4, grid=(S // tm, 2),
        in_specs=[pl.BlockSpec((tm, D_MODEL), lambda i, j: (i, 0)), pl.BlockSpec((512, D_MODEL), lambda i, j: (j, 0)),
                  pl.BlockSpec((tm, 512), lambda i, j: (i, 15 + j)), pl.BlockSpec((tm, 512), lambda i, j: (i, 17 + j)),
                  blk, blk],
        out_specs=(blk,) * 4,
        compiler_params=_cparams("parallel", "arbitrary"), name="out_proj_bwd")(dx1b, wo, proj, proj, ya, yr)


def _branch_bwd(dya, dyr, wa, wr, att):
    S = dya.shape[0]
    tm = 1024

    def body(da_ref, dr_ref, wa_ref, wr_ref, att_ref, datt_ref, rho_ref, dyi_ref):
        datt = _dot_nt(da_ref[...], wa_ref[...])
        datt_ref[...] = datt.astype(BF16)
        dyi_ref[...] = _dot_nt(dr_ref[...], wr_ref[...]).astype(BF16)
        prod = datt * att_ref[...].astype(F32)
        lane = lax.broadcasted_iota(jnp.int32, (tm, 128), 1)
        lo = lane < 64
        rho = jnp.zeros((tm, 128), F32)
        for c in range(4):
            pc = prod[:, c * 128:(c + 1) * 128]
            tot = jnp.sum(pc, axis=-1, keepdims=True)
            low = jnp.sum(jnp.where(lo, pc, 0.0), axis=-1, keepdims=True)
            rho = jnp.where(lane // 16 == 2 * c, low, jnp.where(lane // 16 == 2 * c + 1, tot - low, rho))
        rho_ref[...] = rho

    row = lambda w: pl.BlockSpec((tm, w), lambda i: (i, 0))
    return pl.pallas_call(
        body, out_shape=(SDS((S, 512), BF16), SDS((S, 128), F32), SDS((S, 1024), BF16)), grid=(S // tm,),
        in_specs=[row(1024), row(1024), pl.BlockSpec((512, 1024), lambda i: (0, 0)),
                  pl.BlockSpec((1024, 1024), lambda i: (0, 0)), row(512)],
        out_specs=(row(512), row(128), row(1024)),
        compiler_params=_cparams("parallel"), name="branch_bwd")(dya, dyr, wa, wr, att)


def _attn_bwd(qkv, datt, lse, rho, rtab, d, gi, exchanges=()):
    L = qkv.shape[0]
    nb = L // BLK

    def body(q_ref, kc_ref, kp_ref, vc_ref, vp_ref, do_ref, lse_ref, rho_ref, tq_ref, tk_ref,
             dq_ref, dk_ref, dv_ref, ck, cv):
        n = pl.program_id(1)

        @pl.when(n == 0)
        def _():
            ck[...] = jnp.zeros_like(ck)
            cv[...] = jnp.zeros_like(cv)

        def store_rot(ref, val, t_ref, c):
            sl = slice(c * 128, (c + 1) * 128)
            ref[:, sl] = _unrot(val, t_ref[0], t_ref[1], t_ref[2], 32).astype(BF16)

        @pl.when(n < nb)
        def _():
            mask = _band_mask(n)
            mask2 = jnp.concatenate([mask, mask], axis=0)
            lo = lax.broadcasted_iota(jnp.int32, (BLK, 128), 1) < 64

            def stacked(a):
                return jnp.concatenate([jnp.where(lo, a, jnp.zeros_like(a)), jnp.where(lo, jnp.zeros_like(a), a)], axis=0)

            def head_cols(ref, c):
                return jnp.concatenate([jnp.broadcast_to(ref[:, 32 * c:32 * c + 1], (BLK, 2 * BLK)),
                                        jnp.broadcast_to(ref[:, 32 * c + 16:32 * c + 17], (BLK, 2 * BLK))], axis=0)

            for c in range(4):
                sl = slice(c * 128, (c + 1) * 128)
                q2, do2 = stacked(q_ref[:, sl]), stacked(do_ref[:, sl])
                k = jnp.concatenate([kp_ref[:, sl], kc_ref[:, sl]], axis=0)
                v = jnp.concatenate([vp_ref[:, sl], vc_ref[:, sl]], axis=0)
                s = _dot_nt(q2, k) * 0.125
                p = jnp.where(mask2, jnp.exp(s - head_cols(lse_ref, c)), 0.0)
                dp = _dot_nt(do2, v)
                ds = (p * (dp - head_cols(rho_ref, c)) * 0.125).astype(BF16)
                dq2 = _dot(ds, k)
                dq_c = jnp.where(lo, dq2[:BLK], dq2[BLK:])
                dk_c = _dot_tn(ds, q2)
                dv_c = _dot_tn(p.astype(BF16), do2)
                store_rot(dq_ref, dq_c, tq_ref, c)
                store_rot(dk_ref, ck[:, sl] + dk_c[:BLK], tk_ref, c)
                dv_ref[:, sl] = (cv[:, sl] + dv_c[:BLK]).astype(BF16)
                ck[:, sl] = dk_c[BLK:]
                cv[:, sl] = dv_c[BLK:]

        @pl.when(n == nb)
        def _():
            for c in range(4):
                sl = slice(c * 128, (c + 1) * 128)
                store_rot(dk_ref, ck[:, sl], tk_ref, c)
            dv_ref[...] = cv[...].astype(BF16)

    cur = lambda n: jnp.minimum(n, nb - 1)
    prev = lambda n: jnp.maximum(jnp.minimum(n, nb - 1) - 1, 0)
    fin = lambda n: jnp.maximum(n - 1, 0)
    col = _qkv_col(d, gi)
    return _carrier_call(
        body, (qkv, qkv, qkv, qkv, qkv, datt, lse, rho, rtab, rtab),
        out_shape=(SDS((L, d * 512), BF16),) * 3, grid=(d, nb + 1),
        in_specs=[pl.BlockSpec((BLK, 512), lambda r, n: (cur(n), col(0, r))),
                  pl.BlockSpec((BLK, 512), lambda r, n: (cur(n), col(1, r))),
                  pl.BlockSpec((BLK, 512), lambda r, n: (prev(n), col(1, r))),
                  pl.BlockSpec((BLK, 512), lambda r, n: (cur(n), col(2, r))),
                  pl.BlockSpec((BLK, 512), lambda r, n: (prev(n), col(2, r))),
                  pl.BlockSpec((BLK, 512), lambda r, n: (cur(n), r)),
                  pl.BlockSpec((BLK, 128), lambda r, n: (cur(n), r)),
                  pl.BlockSpec((BLK, 128), lambda r, n: (cur(n), r)),
                  pl.BlockSpec((3, BLK, 128), lambda r, n: (0, cur(n), r)),
                  pl.BlockSpec((3, BLK, 128), lambda r, n: (0, fin(n), r))],
        out_specs=(pl.BlockSpec((BLK, 512), lambda r, n: (cur(n), r)),
                   pl.BlockSpec((BLK, 512), lambda r, n: (fin(n), r)),
                   pl.BlockSpec((BLK, 512), lambda r, n: (fin(n), r))),
        scratch_shapes=[pltpu.VMEM((BLK, 512), F32), pltpu.VMEM((BLK, 512), F32)],
        sem=("parallel", "arbitrary"), name=f"attn_bwd_g{gi}", exchanges=exchanges)


def _ret_bwd(proj, rn, rstd, dyrin, states, tab, consts, exchanges=()):
    S = proj.shape[0]
    nc = S // BLK
    dmask, zeta, xi, dec = consts

    def body(q_ref, k_ref, v0_ref, v1_ref, g0_ref, g1_ref, rn_ref, rs_ref, dy_ref, st_ref, tq_ref, tk_ref,
             dm_ref, z_ref, x_ref, dec_ref, dq_ref, dk_ref, dv_ref, dgr_ref, dR):
        @pl.when(pl.program_id(0) == 0)
        def _():
            dR[...] = jnp.zeros_like(dR)

        for h in range(RET_HEADS):
            hs = slice(h * 128, (h + 1) * 128)
            vs = slice((h % 2) * 256, (h % 2 + 1) * 256)
            os_ = slice(h * 256, (h + 1) * 256)
            q, k = q_ref[:, hs], k_ref[:, hs]
            v = (v0_ref if h < 2 else v1_ref)[:, vs]
            gr = (g0_ref if h < 2 else g1_ref)[:, vs].astype(F32)
            sg = _sigmoid(gr)
            rn_v = rn_ref[:, os_].astype(F32)
            dyi = dy_ref[:, os_].astype(F32)
            dgr_ref[:, os_] = (dyi * rn_v * sg * (1.0 + gr * (1.0 - sg))).astype(BF16)
            drn = dyi * gr * sg
            rstd = jnp.broadcast_to(rs_ref[:, 16 * h:16 * h + 1], (BLK, 256))
            do = rstd * (drn - jnp.mean(drn, axis=-1, keepdims=True) - rn_v * jnp.mean(drn * rn_v, axis=-1, keepdims=True))
            dob = do.astype(BF16)
            Rb = st_ref[h]
            dRb = dR[h].astype(BF16)
            dm, zt, xt = dm_ref[h], z_ref[h], x_ref[h]
            sD = (_dot_nt(q, k) * dm).astype(BF16)
            kz = (k.astype(F32) * zt).astype(BF16)
            qx = (q.astype(F32) * xt).astype(BF16)
            dv_ref[:, os_] = (_dot_tn(sD, dob) + _dot(kz, dRb)).astype(BF16)
            dS = (_dot_nt(dob, v) * dm).astype(BF16)
            dq = _dot(dS, k) + _dot_nt(dob, Rb) * xt
            dk = _dot_tn(dS, q) + _dot_nt(v, dRb) * zt
            dR[h] = dR[h] * dec_ref[h, 0:1, :] + _dot_tn(qx, dob)
            dq_ref[:, hs] = _unrot(dq, tq_ref[0], tq_ref[1], tq_ref[2], 1).astype(BF16)
            dk_ref[:, hs] = _unrot(dk, tk_ref[0], tk_ref[1], tk_ref[2], 1).astype(BF16)

    rc = lambda c: nc - 1 - c
    cst = lambda shape: pl.BlockSpec(shape, lambda c: (0, 0, 0))
    blk = lambda j: pl.BlockSpec((BLK, 512), lambda c: (rc(c), j))
    row = lambda w: pl.BlockSpec((BLK, w), lambda c: (rc(c), 0))
    return _carrier_call(
        body, (proj, proj, proj, proj, proj, proj, rn, rstd, dyrin, states, tab, tab, dmask, zeta, xi, dec),
        out_shape=(SDS((S, 512), BF16), SDS((S, 512), BF16), SDS((S, 1024), BF16), SDS((S, 1024), BF16)),
        grid=(nc,),
        in_specs=[blk(QR_B), blk(KR_B), blk(11), blk(12), blk(13), blk(14), row(1024), row(128), row(1024),
                  pl.BlockSpec((RET_HEADS, None, BLK, 256), lambda c: (0, rc(c), 0, 0)),
                  pl.BlockSpec((None, 3, BLK, 128), lambda c: (1, 0, rc(c), 0)),
                  pl.BlockSpec((None, 3, BLK, 128), lambda c: (2, 0, rc(c), 0)),
                  cst((RET_HEADS, BLK, BLK)), cst((RET_HEADS, BLK, 128)), cst((RET_HEADS, BLK, 128)), cst((RET_HEADS, 8, 256))],
        out_specs=(row(512), row(512), row(1024), row(1024)),
        scratch_shapes=[pltpu.VMEM((RET_HEADS, BLK, 256), F32)],
        sem=("arbitrary",), name="ret_bwd", exchanges=exchanges)


def _wgrad_in_half(ht, dproj, sidx, kept, exchanges=()):
    S = dproj.shape[0]
    tk = 2048
    half = (lambda sx: sx[4]) if kept else (lambda sx: 1 - sx[4])

    def body(a_ref, b_ref, o_ref):
        @pl.when(pl.program_id(1) == 0)
        def _():
            o_ref[...] = jnp.zeros_like(o_ref)

        o_ref[...] += _dot(a_ref[...], b_ref[...])

    (g,), xres = _carrier_call(
        body, (ht, dproj), out_shape=(SDS((D_MODEL // 2, PROJ_W), F32),), grid=(N_SHARD, S // tk),
        in_specs=[pl.BlockSpec((D_MODEL // 2, tk), lambda s, k, sx: (half(sx), k)),
                  pl.BlockSpec((tk, W_IN_S), lambda s, k, sx: (k, s))],
        out_specs=(pl.BlockSpec((D_MODEL // 2, W_IN_S), lambda s, k, sx: (0, s)),),
        sem=("parallel", "arbitrary"), name="wgrad_in_kept" if kept else "wgrad_in_sent", exchanges=exchanges,
        prefetch=sidx)
    return g, xres


def _in_proj_bwd(dproj, w_in, x, g1, dx1, exchanges=()):
    S = x.shape[0]
    tm = 512

    def body(d_ref, w_ref, x_ref, g_ref, dx1_ref, dx_ref, dgn_ref, acc):
        i, s = pl.program_id(0), pl.program_id(1)

        @pl.when(s == 0)
        def _():
            acc[...] = jnp.zeros_like(acc)

        @pl.when((i == 0) & (s == 0))
        def _():
            dgn_ref[...] = jnp.zeros_like(dgn_ref)

        acc[...] += _dot_nt(d_ref[...], w_ref[...])

        @pl.when(s == N_SHARD - 1)
        def _():
            xv = x_ref[...]
            r = lax.rsqrt(jnp.mean(xv * xv, axis=-1, keepdims=True) + NORM_EPS)
            xh = xv * r
            dh = acc[...]
            dgn_ref[...] += jnp.sum(dh * xh, axis=0, keepdims=True)
            dxh = dh * g_ref[...]
            dx_ref[...] = dx1_ref[...] + r * (dxh - xh * jnp.mean(dxh * xh, axis=-1, keepdims=True))

    row = pl.BlockSpec((tm, D_MODEL), lambda i, s: (i, 0))
    vec = pl.BlockSpec((1, D_MODEL), lambda i, s: (0, 0))
    (gx, dg), xres = _carrier_call(
        body, (dproj, w_in, x, g1, dx1),
        out_shape=(SDS((S, D_MODEL), F32), SDS((1, D_MODEL), F32)), grid=(S // tm, N_SHARD),
        in_specs=[pl.BlockSpec((tm, W_IN_S), lambda i, s: (i, s)),
                  pl.BlockSpec((D_MODEL, W_IN_S), lambda i, s: (0, s)), row, vec, row],
        out_specs=(row, vec), scratch_shapes=[pltpu.VMEM((tm, D_MODEL), F32)],
        sem=("arbitrary", "arbitrary"), name="in_proj_bwd", exchanges=exchanges)
    return gx, dg, xres


def _sub_view(a, d):
    S, W = a.shape
    return a.reshape(S // d, d * W)


def _step(x, tgt, g1, g2, g3, comm):
    S = x.shape[0]
    tab = _tables(S)
    consts = _ret_consts()

    h, ht = _rms_fwd(x, g1)
    w_in = comm.w_in()
    proj, xres = _in_proj(h, w_in, tab, comm.carry("in_proj"))
    comm.took("in_proj", xres)
    qkvs, o_parts, lse_parts = [], [], []
    for gi, d in enumerate(DILATIONS):
        qkv = proj if d == 1 else _qkv_to_sub(proj, d, gi)
        (o_g, lse_g), xres = _attn_fwd(qkv, d, gi, comm.carry(f"attn_fwd_g{gi}"))
        comm.took(f"attn_fwd_g{gi}", xres)
        qkvs.append(qkv)
        o_parts.append(o_g)
        lse_parts.append(lse_g)
    att, lse_tot = _attn_merge(o_parts, lse_parts)
    yrin, rn, rstd, states = _ret_fwd(proj, consts)
    wa, wr, wo, wg, wu, wd = comm.w_rest()
    merged, ya, yr = _branch_merge(att, yrin, proj, wa, wr)
    x1, h2 = _out_proj(merged, wo, x, g2)
    gte, up, act = _ffn_up(h2, wg, wu)
    dx2, dx2b, dg3, loss_p = _ffn_down_loss(act, wd, x1, g3, tgt)

    dgte, dup = _ffn_down_bwd(dx2b, wd, gte, up)
    tok3 = lambda w: (lambda tk: pl.BlockSpec((None, tk, w), lambda p, k: (p, k, 0)))
    tok2 = lambda w: (lambda tk: pl.BlockSpec((tk, w), lambda p, k: (k, 0)))
    g_d = _wgrad("wgrad_down", act, dx2b, tok3(HID_S), tok2(D_MODEL), (N_SHARD, HID_S, D_MODEL),
                 pl.BlockSpec((None, HID_S, D_MODEL), lambda p, k: (p, 0, 0)), N_SHARD, S)
    g_g = _wgrad("wgrad_gate", h2, dgte, tok2(D_MODEL), tok3(HID_S), (N_SHARD, D_MODEL, HID_S),
                 pl.BlockSpec((None, D_MODEL, HID_S), lambda p, k: (p, 0, 0)), N_SHARD, S)
    g_u = _wgrad("wgrad_up", h2, dup, tok2(D_MODEL), tok3(HID_S), (N_SHARD, D_MODEL, HID_S),
                 pl.BlockSpec((None, D_MODEL, HID_S), lambda p, k: (p, 0, 0)), N_SHARD, S)
    comm.grads({4: g_g, 5: g_u, 6: g_d})
    (dx1, dx1b, dg2), xres = _ffn_up_bwd(dgte, dup, wg, wu, x1, g2, dx2, comm.carry("ffn_up_bwd"))
    comm.took("ffn_up_bwd", xres)
    dya, dyr, dga, dgrr = _out_proj_bwd(dx1b, wo, proj, ya, yr)
    colblk = lambda w: (lambda tk: pl.BlockSpec((tk, w), lambda p, k: (k, p)))
    g_o = _wgrad("wgrad_out", merged, dx1b, colblk(256), tok2(D_MODEL), (D_MODEL, D_MODEL),
                 pl.BlockSpec((256, D_MODEL), lambda p, k: (p, 0)), 4, S)
    datt, rho, dyrin = _branch_bwd(dya, dyr, wa, wr, att)
    g_a = _wgrad("wgrad_attn", att, dya, tok2(512), colblk(512), (512, D_MODEL),
                 pl.BlockSpec((512, 512), lambda p, k: (0, p)), 2, S)
    g_r = _wgrad("wgrad_ret", yrin, dyr, colblk(256), tok2(D_MODEL), (D_MODEL, D_MODEL),
                 pl.BlockSpec((256, D_MODEL), lambda p, k: (p, 0)), 4, S)
    comm.grads({1: g_a, 2: g_r.reshape(N_SHARD, 256, D_MODEL), 3: g_o.reshape(N_SHARD, 256, D_MODEL)})
    (dqr, dkr, dvr, dgr), xres = _ret_bwd(proj, rn, rstd, dyrin, states, tab, consts, comm.carry("ret_bwd"))
    comm.took("ret_bwd", xres)
    dqs, dks, dvs = [], [], []
    for gi, d in enumerate(DILATIONS):
        rtab = tab[0].reshape(3, S // d, d * 128)
        (dq, dk, dv), xres = _attn_bwd(qkvs[gi], _sub_view(datt, d), _sub_view(lse_tot, d), _sub_view(rho, d), rtab, d, gi,
                                       comm.carry(f"attn_bwd_g{gi}"))
        comm.took(f"attn_bwd_g{gi}", xres)
        dqs.append(dq)
        dks.append(dk)
        dvs.append(dv)
    dproj = _assemble_dproj((dqs, dks, dvs), dqr, dkr, dvr, dgr, dga, dgrr)
    g_sent, xres = _wgrad_in_half(ht, dproj, comm.sidx, False, comm.carry("wgrad_in_sent"))
    comm.took("wgrad_in_sent", xres)
    comm.grads({"in_sent": g_sent})
    g_kept, xres = _wgrad_in_half(ht, dproj, comm.sidx, True, comm.carry("wgrad_in_kept"))
    comm.grads({"in_kept": g_kept})
    comm.took("wgrad_in_kept", xres)
    grad_x, dg1, xres = _in_proj_bwd(dproj, w_in, x, g1, dx1, comm.carry("in_proj_bwd"))
    comm.took("in_proj_bwd", xres)
    return loss_p[0, 0], grad_x, (dg1, dg2, dg3)


W_KINDS = ("col", "col", "lead", "lead", "lead", "lead", "lead")
W_SHARD = ((1024, W_IN_S), (512, 256), (256, 1024), (256, 1024), (1024, HID_S), (1024, HID_S), (HID_S, 1024))
N_W = len(W_KINDS)


def _full_shape(wi):
    R, C = W_SHARD[wi]
    return (R, N_SHARD * C) if W_KINDS[wi] == "col" else (N_SHARD, R, C)


def _view(ref, wi, s, half):
    R, C = W_SHARD[wi]
    rows = pl.ds(half * (R // 2), R // 2)
    if W_KINDS[wi] == "col":
        return ref.at[rows, pl.ds(pl.multiple_of(s * C, 128), C)]
    return ref.at[s, rows, :]


def _mesh_pos():
    x, y, c = lax.axis_index("x"), lax.axis_index("y"), lax.axis_index("c")
    chips = [(1 - x, y), (x, 1 - y), (1 - x, 1 - y)]
    return x, y, c, chips


def _cast_bf16(a):
    R, C = a.shape
    tr = R // 2 if R % 32 == 0 else R

    def body(a_ref, o_ref):
        o_ref[...] = a_ref[...].astype(BF16)

    spec = pl.BlockSpec((tr, C), lambda i: (i, 0))
    return pl.pallas_call(body, out_shape=SDS((R, C), BF16), grid=(R // tr,), in_specs=[spec], out_specs=spec,
                          compiler_params=_cparams("parallel"), name=f"cast_{R}x{C}")(a)


def _remote(send, recv, k, src, dst, to):
    return pltpu.make_async_remote_copy(src_ref=src, dst_ref=dst, send_sem=send.at[k], recv_sem=recv.at[k],
                                        device_id=to, device_id_type=MESH)


def _gather_now(wis, shards):
    n = len(wis)

    def body(*refs):
        sh, full = refs[:n], refs[n:2 * n]
        send, recv, loc = refs[2 * n:]
        x, y, c, chips = _mesh_pos()
        s_me = 2 * x + y
        sib = (x, y, 1 - c)
        own, started = [], []
        for i, wi in enumerate(wis):
            Rh = W_SHARD[wi][0] // 2
            for hf in range(2):
                cp = pltpu.make_async_copy(sh[i].at[pl.ds(hf * Rh, Rh), :], _view(full[i], wi, s_me, hf), loc.at[2 * i + hf])
                cp.start()
                own.append(cp)
            for j, chip in enumerate(chips):
                cp = _remote(send, recv, 3 * i + j, sh[i].at[pl.ds(c * Rh, Rh), :], _view(full[i], wi, s_me, c), (*chip, c))
                cp.start()
                started.append(cp)
        for i, wi in enumerate(wis):
            for j, chip in enumerate(chips):
                land = _view(full[i], wi, 2 * chip[0] + chip[1], c)
                _remote(send, recv, 3 * i + j, land, land, (*chip, c)).wait_recv()
                fw = _remote(send, recv, 3 * n + 3 * i + j, land, land, sib)
                fw.start()
                started.append(fw)
        for i, wi in enumerate(wis):
            for j, chip in enumerate(chips):
                land = _view(full[i], wi, 2 * chip[0] + chip[1], 1 - c)
                _remote(send, recv, 3 * n + 3 * i + j, land, land, sib).wait_recv()
        for cp in started:
            cp.wait_send()
        for cp in own:
            cp.wait()

    return pl.pallas_call(
        body, out_shape=tuple(SDS(_full_shape(wi), BF16) for wi in wis),
        in_specs=[ANY] * n, out_specs=tuple([ANY] * n),
        scratch_shapes=[pltpu.SemaphoreType.DMA((6 * n,)), pltpu.SemaphoreType.DMA((6 * n,)),
                        pltpu.SemaphoreType.DMA((2 * n,))],
        name="gather_now")(*shards)


def _ex_gather_ici(wis, shards):
    def build(ins, outs, send, recv, loc):
        x, y, c, chips = _mesh_pos()
        s_me = 2 * x + y
        starts, waits = [], []
        for i, wi in enumerate(wis):
            Rh = W_SHARD[wi][0] // 2
            for hf in range(2):
                cp = pltpu.make_async_copy(ins[i].at[pl.ds(hf * Rh, Rh), :], _view(outs[i], wi, s_me, hf), loc.at[2 * i + hf])
                starts.append(cp)
                waits.append(cp.wait)
            for j, chip in enumerate(chips):
                cp = _remote(send, recv, 3 * i + j, ins[i].at[pl.ds(c * Rh, Rh), :], _view(outs[i], wi, s_me, c), (*chip, c))
                land = _view(outs[i], wi, 2 * chip[0] + chip[1], c)
                starts.append(cp)
                waits += [cp.wait_send, _remote(send, recv, 3 * i + j, land, land, (*chip, c)).wait_recv]
        return starts, waits

    return _Exchange(shards, [SDS(_full_shape(wi), BF16) for wi in wis], {}, 3 * len(wis), 2 * len(wis), build)


def _ex_gather_d2d(wis, fulls):
    def build(ins, outs, send, recv, loc):
        x, y, c, chips = _mesh_pos()
        sib = (x, y, 1 - c)
        starts, waits = [], []
        for i, wi in enumerate(wis):
            for j, chip in enumerate(chips):
                mine = _view(outs[i], wi, 2 * chip[0] + chip[1], c)
                theirs = _view(outs[i], wi, 2 * chip[0] + chip[1], 1 - c)
                cp = _remote(send, recv, 3 * i + j, mine, mine, sib)
                starts.append(cp)
                waits += [cp.wait_send, _remote(send, recv, 3 * i + j, theirs, theirs, sib).wait_recv]
        return starts, waits

    return _Exchange(fulls, [SDS(f.shape, BF16) for f in fulls], {i: i for i in range(len(wis))}, 3 * len(wis), 0, build)


def _half_shape(wi):
    R, C = W_SHARD[wi]
    return (R // 2, N_SHARD * C) if W_KINDS[wi] == "col" else (N_SHARD, R // 2, C)


def _ex_pair(wis, grads):
    def build(ins, outs, send, recv, loc):
        x, y, c, _ = _mesh_pos()
        starts, waits = [], []
        for i, wi in enumerate(wis):
            Rh = W_SHARD[wi][0] // 2
            rows = pl.ds((1 - c) * Rh, Rh)
            if tuple(ins[i].shape) == _half_shape(wi):
                src = ins[i]
            else:
                src = ins[i].at[rows, :] if W_KINDS[wi] == "col" else ins[i].at[:, rows, :]
            cp = _remote(send, recv, i, src, outs[i], (x, y, 1 - c))
            starts.append(cp)
            waits.append(cp.wait)
        return starts, waits

    return _Exchange(grads, [SDS(_half_shape(wi), F32) for wi in wis], {}, len(wis), 0, build)


def _ex_chip(wis, pbs):
    def build(ins, outs, send, recv, loc):
        x, y, c, chips = _mesh_pos()
        starts, waits = [], []
        for i, wi in enumerate(wis):
            for j, chip in enumerate(chips):
                cp = _remote(send, recv, 3 * i + j, ins[i].at[j], outs[i].at[j], (*chip, c))
                starts.append(cp)
                waits.append(cp.wait)
        return starts, waits

    shapes = [SDS((3, W_SHARD[wi][0] // 2, W_SHARD[wi][1]), BF16) for wi in wis]
    return _Exchange(pbs, shapes, {}, 3 * len(wis), 0, build)


def _ex_share(wis, halves):
    def build(ins, outs, send, recv, loc):
        x, y, c, _ = _mesh_pos()
        sib = (x, y, 1 - c)
        starts, waits = [], []
        for i, wi in enumerate(wis):
            cp = _remote(send, recv, i, outs[i].at[c], outs[i].at[c], sib)
            starts.append(cp)
            waits += [cp.wait_send, _remote(send, recv, i, outs[i].at[1 - c], outs[i].at[1 - c], sib).wait_recv]
        return starts, waits

    return _Exchange(halves, [SDS(h.shape, F32) for h in halves], {i: i for i in range(len(wis))}, len(wis), 0, build)


def _row_tile(rh, C):
    best = 16
    for t in range(16, rh + 1, 16):
        if rh % t == 0 and t * C * 4 <= (3 << 19):
            best = t
    return best


def _pair_sum(wi, g, ra, sidx):
    R, C = W_SHARD[wi]
    Rh = R // 2
    tr = _row_tile(Rh, C)
    nt = Rh // tr
    off = 0 if tuple(g.shape) == _half_shape(wi) else nt
    col = W_KINDS[wi] == "col"

    def body(sidx_ref, *refs):
        gs, rs = refs[:4], refs[4:8]
        own_ref, pb_ref = refs[8:]
        own_ref[...] = gs[0][...] + rs[0][...]
        for j in range(3):
            pb_ref[j] = (gs[1 + j][...] + rs[1 + j][...]).astype(BF16)

    def gspec(slot):
        if col:
            return pl.BlockSpec((tr, C), lambda i, sx: (sx[4] * off + i, sx[slot]))
        return pl.BlockSpec((None, tr, C), lambda i, sx: (sx[slot], sx[4] * off + i, 0))

    def rspec(slot):
        if col:
            return pl.BlockSpec((tr, C), lambda i, sx: (i, sx[slot]))
        return pl.BlockSpec((None, tr, C), lambda i, sx: (sx[slot], i, 0))

    return pl.pallas_call(
        body, out_shape=(SDS((Rh, C), F32), SDS((3, Rh, C), BF16)),
        grid_spec=pltpu.PrefetchScalarGridSpec(
            num_scalar_prefetch=1, grid=(nt,),
            in_specs=[gspec(k) for k in range(4)] + [rspec(k) for k in range(4)],
            out_specs=(pl.BlockSpec((tr, C), lambda i, sx: (i, 0)), pl.BlockSpec((3, tr, C), lambda i, sx: (0, i, 0)))),
        compiler_params=_cparams("arbitrary"), name=f"pair_sum_w{wi}")(sidx, g, g, g, g, ra, ra, ra, ra)


def _chip_sum(wi, own, rb, sidx):
    R, C = W_SHARD[wi]
    Rh = R // 2
    tr = _row_tile(Rh, C)

    def body(sidx_ref, own_ref, rb_ref, o_ref):
        o_ref[...] = ((own_ref[...] + rb_ref[0].astype(F32)) + rb_ref[1].astype(F32)) + rb_ref[2].astype(F32)

    return pl.pallas_call(
        body, out_shape=SDS((2, Rh, C), F32),
        grid_spec=pltpu.PrefetchScalarGridSpec(
            num_scalar_prefetch=1, grid=(Rh // tr,),
            in_specs=[pl.BlockSpec((tr, C), lambda i, sx: (i, 0)), pl.BlockSpec((3, tr, C), lambda i, sx: (0, i, 0))],
            out_specs=pl.BlockSpec((None, tr, C), lambda i, sx: (sx[4], i, 0))),
        compiler_params=_cparams("arbitrary"), name=f"chip_sum_w{wi}")(sidx, own, rb)


def _gain_allgather(blk):
    m_per, n = blk.shape

    def body(x_ref, out_ref, send_sems, recv_sems, local_sem):
        x, y, c, chips = _mesh_pos()
        me, sibling = (x, y, c), (x, y, 1 - c)

        def rows(px, py, pc):
            return out_ref.at[pl.ds((4 * px + 2 * py + pc) * m_per, m_per), :]

        def copy(k, block, to, src=None):
            return pltpu.make_async_remote_copy(
                src_ref=rows(*block) if src is None else src, dst_ref=rows(*block),
                send_sem=send_sems.at[k], recv_sem=recv_sems.at[k], device_id=to, device_id_type=MESH)

        mine = pltpu.make_async_copy(x_ref, rows(*me), local_sem)
        mine.start()
        first = [copy(0, me, sibling, src=x_ref)]
        first += [copy(1 + j, me, (*chip, c), src=x_ref) for j, chip in enumerate(chips)]
        for cp in first:
            cp.start()
        passed = [copy(4 + j, (*chip, c), sibling) for j, chip in enumerate(chips)]
        for j, chip in enumerate(chips):
            copy(1 + j, (*chip, c), me).wait_recv()
            passed[j].start()
        copy(0, sibling, me).wait_recv()
        for j, chip in enumerate(chips):
            copy(4 + j, (*chip, 1 - c), me).wait_recv()
        for cp in first + passed:
            cp.wait_send()
        mine.wait()

    vm = pl.BlockSpec(memory_space=pltpu.VMEM)
    return pl.pallas_call(
        body, out_shape=SDS((8 * m_per, n), blk.dtype), in_specs=[vm], out_specs=vm,
        scratch_shapes=[pltpu.SemaphoreType.DMA((7,)), pltpu.SemaphoreType.DMA((7,)), pltpu.SemaphoreType.DMA],
        name="gain_allgather")(blk)


def _adam_math(w, g, m, v):
    mn = ADAM_B1 * m + (1.0 - ADAM_B1) * g
    vn = ADAM_B2 * v + (1.0 - ADAM_B2) * (g * g)
    mh = mn / (1.0 - ADAM_B1 ** ADAM_STEP)
    vh = vn / (1.0 - ADAM_B2 ** ADAM_STEP)
    return -ADAM_LR * (mh / (jnp.sqrt(vh) + ADAM_EPS) + ADAM_WD * w), mn, vn


def _adamw(wi, w, g, m, v):
    R, C = w.shape
    tr = _row_tile(R, C)

    def body(w_ref, g_ref, m_ref, v_ref, d_ref, mn_ref, vn_ref):
        d_ref[...], mn_ref[...], vn_ref[...] = _adam_math(w_ref[...], g_ref[...], m_ref[...], v_ref[...])

    spec = pl.BlockSpec((tr, C), lambda i: (i, 0))
    return pl.pallas_call(body, out_shape=(SDS((R, C), F32),) * 3, grid=(R // tr,), in_specs=[spec] * 4,
                          out_specs=(spec,) * 3, compiler_params=_cparams("parallel"), name=f"adamw_w{wi}")(w, g, m, v)


def _gain_update(gathered, w, m, v):
    def body(ga_ref, w_ref, m_ref, v_ref, g_ref, d_ref, mn_ref, vn_ref):
        g = ga_ref[0:8, :]
        for dev in range(1, 8):
            g = g + ga_ref[8 * dev:8 * dev + 8, :]
        g_ref[...] = g
        d_ref[...], mn_ref[...], vn_ref[...] = _adam_math(w_ref[...], g, m_ref[...], v_ref[...])

    return pl.pallas_call(body, out_shape=(SDS((8, 1024), F32),) * 4, name="gain_update")(gathered, w, m, v)


GROUP_FFN, GROUP_MIX, GROUP_IN = (4, 5, 6), (1, 2, 3), (0,)
REST = GROUP_MIX + GROUP_FFN


class _MeshComm:
    SCHEDULE = {
        "in_proj": [("ici", (1, 2, 3, 4))],
        "attn_fwd_g0": [("d2d", (1, 2, 3, 4)), ("ici", (5,))],
        "attn_fwd_g1": [("d2d", (5,)), ("ici", (6,))],
        "attn_fwd_g2": [("d2d", (6,))],
        "ffn_up_bwd": [("pair", GROUP_FFN)],
        "ret_bwd": [("pair", GROUP_MIX)],
        "attn_bwd_g0": [("chip", (4,))],
        "attn_bwd_g1": [("chip", (5, 1, 2, 3))],
        "attn_bwd_g2": [("chip", (6,))],
        "wgrad_in_sent": [("share", GROUP_FFN + GROUP_MIX)],
        "wgrad_in_kept": [("pair", GROUP_IN)],
        "in_proj_bwd": [("chip", GROUP_IN)],
    }

    def __init__(self, shards):
        xi, yi, ci = lax.axis_index("x"), lax.axis_index("y"), lax.axis_index("c")
        self.sidx = jnp.stack([2 * xi + yi, 2 * (1 - xi) + yi, 2 * xi + (1 - yi), 2 * (1 - xi) + (1 - yi), ci]).astype(jnp.int32)
        self.shards, self.full = shards, {}
        self.g, self.own, self.pb, self.half, self.red = {}, {}, {}, {}, {}

    def w_in(self):
        return _gather_now(GROUP_IN, [self.shards[0]])[0]

    def w_rest(self):
        f = self.full
        return f[1], f[2].reshape(D_MODEL, D_MODEL), f[3].reshape(D_MODEL, D_MODEL), f[4], f[5], f[6]

    def grads(self, by_wi):
        self.g.update(by_wi)

    def _exchange(self, stage, wis):
        pick = lambda table: [table[wi] for wi in wis]
        if stage == "ici":
            return _ex_gather_ici(wis, pick(self.shards))
        if stage == "d2d":
            return _ex_gather_d2d(wis, pick(self.full))
        if stage == "pair":
            return _ex_pair(wis, [self.g["in_sent"] if wi == 0 else self.g[wi] for wi in wis])
        if stage == "chip":
            return _ex_chip(wis, pick(self.pb))
        return _ex_share(wis, pick(self.half))

    def _landed(self, stage, wis, res):
        for wi, r in zip(wis, res):
            if stage in ("ici", "d2d"):
                self.full[wi] = r
            elif stage == "pair":
                self.own[wi], self.pb[wi] = _pair_sum(wi, self.g["in_kept"] if wi == 0 else self.g[wi], r, self.sidx)
            elif stage == "chip":
                self.half[wi] = _chip_sum(wi, self.own[wi], r, self.sidx)
            else:
                self.red[wi] = r

    def carry(self, point):
        return [self._exchange(stage, wis) for stage, wis in self.SCHEDULE.get(point, ())]

    def took(self, point, xres):
        for (stage, wis), res in zip(self.SCHEDULE.get(point, ()), xres):
            self._landed(stage, wis, res)

    def reduced(self):
        self._landed("share", GROUP_IN, _exchange_call(self._exchange("share", GROUP_IN), "share_w_in"))
        return [self.red[wi] for wi in range(N_W)]


def kernel(x, norm_mix_g, w_in, w_out_attn, w_out_ret, w_out, norm_ffn_g, w_ffn_gate, w_ffn_up, w_ffn_down, norm_final_g, loss_target, m_norm_mix_g, m_w_in, m_w_out_attn, m_w_out_ret, m_w_out, m_norm_ffn_g, m_w_ffn_gate, m_w_ffn_up, m_w_ffn_down, m_norm_final_g, v_norm_mix_g, v_w_in, v_w_out_attn, v_w_out_ret, v_w_out, v_norm_ffn_g, v_w_ffn_gate, v_w_ffn_up, v_w_ffn_down, v_norm_final_g):
    ws = (w_in, w_out_attn, w_out_ret, w_out, w_ffn_gate, w_ffn_up, w_ffn_down)
    ms = (m_w_in, m_w_out_attn, m_w_out_ret, m_w_out, m_w_ffn_gate, m_w_ffn_up, m_w_ffn_down)
    vs = (v_w_in, v_w_out_attn, v_w_out_ret, v_w_out, v_w_ffn_gate, v_w_ffn_up, v_w_ffn_down)
    shard2d = lambda a, wi: a.reshape(W_SHARD[wi])

    comm = _MeshComm([_cast_bf16(shard2d(w, wi)) for wi, w in enumerate(ws)])
    g3 = norm_final_g.reshape(1, D_MODEL)
    loss_p, grad_x, gain_g = _step(x[0], loss_target[0], norm_mix_g, norm_ffn_g, g3, comm)
    gred = comm.reduced()

    outs_g, outs_d, outs_m, outs_v = [], [], [], []
    for wi in range(N_W):
        g2d = gred[wi].reshape(W_SHARD[wi])
        dlt, mn, vn = _adamw(wi, shard2d(ws[wi], wi), g2d, shard2d(ms[wi], wi), shard2d(vs[wi], wi))
        for lst, a in ((outs_g, g2d), (outs_d, dlt), (outs_m, mn), (outs_v, vn)):
            lst.append(a.reshape(ws[wi].shape))

    pad8 = lambda rows: jnp.concatenate([r.reshape(1, D_MODEL) for r in rows] + [jnp.zeros((5, D_MODEL), F32)], axis=0)
    gathered = _gain_allgather(pad8(gain_g))
    gg, gd, gm, gv = _gain_update(gathered, pad8((norm_mix_g, norm_ffn_g, norm_final_g)),
                                  pad8((m_norm_mix_g, m_norm_ffn_g, m_norm_final_g)),
                                  pad8((v_norm_mix_g, v_norm_ffn_g, v_norm_final_g)))
    loss = lax.psum(loss_p, ("x", "y", "c"))

    def assemble(gain_rows, wlist):
        return (gain_rows[0:1], wlist[0], wlist[1], wlist[2], wlist[3], gain_rows[1:2],
                wlist[4], wlist[5], wlist[6], gain_rows[2])

    return (loss, grad_x[None], *assemble(gg, outs_g), *assemble(gd, outs_d), *assemble(gm, outs_m), *assemble(gv, outs_v))
```

```python
import functools
import math

import numpy as np
import jax
import jax.numpy as jnp
from jax import lax
from jax.experimental import pallas as pl
from jax.experimental.pallas import tpu as pltpu

F32, BF16 = jnp.float32, jnp.bfloat16
SDS = jax.ShapeDtypeStruct
MESH = pl.DeviceIdType.MESH

D_MODEL = 1024
PROJ_W = 9728
COLB = 512
N_COLB = PROJ_W // COLB
QA_B, KA_B, VA_B = 0, 3, 6
QR_B, KR_B = 9, 10
FFN_HID = 2816
N_SHARD = 4
HID_S = FFN_HID // N_SHARD
W_IN_S = PROJ_W // N_SHARD
DILATIONS = (1, 4, 16)
BLK = 128
RET_HEADS = 4
ROPE_THETA = 10000.0
NORM_EPS = 1e-6
ADAM_LR, ADAM_B1, ADAM_B2, ADAM_EPS, ADAM_WD, ADAM_STEP = 0.001, 0.9, 0.999, 1e-08, 0.01, 10
VMEM_LIMIT = 56 << 20


def _cparams(*sem):
    return pltpu.CompilerParams(dimension_semantics=sem or None, vmem_limit_bytes=VMEM_LIMIT)


def _dot(a, b):
    return jnp.dot(a, b, preferred_element_type=F32)


def _dot_nt(a, b):
    return lax.dot_general(a, b, (((1,), (1,)), ((), ())), preferred_element_type=F32)


def _dot_tn(a, b):
    return lax.dot_general(a, b, (((0,), (0,)), ((), ())), preferred_element_type=F32)


def _row_pieces(tm, sub=512):
    return [slice(i, i + sub) for i in range(0, tm, sub)]


def _sigmoid(z):
    return 0.5 * jnp.tanh(0.5 * z) + 0.5


ANY = pl.BlockSpec(memory_space=pl.ANY)


class _Exchange:
    def __init__(self, ins, out_shapes, aliases, n_sem, n_loc, build):
        self.ins, self.out_shapes, self.aliases = list(ins), list(out_shapes), dict(aliases)
        self.n_sem, self.n_loc, self.build = n_sem, n_loc, build

    def sems(self):
        return [pltpu.SemaphoreType.DMA((self.n_sem,)), pltpu.SemaphoreType.DMA((self.n_sem,)),
                pltpu.SemaphoreType.DMA((max(self.n_loc, 1),))]


def _exchange_call(ex, name):
    n_in, n_out = len(ex.ins), len(ex.out_shapes)

    def body(*refs):
        starts, waits = ex.build(refs[:n_in], refs[n_in:n_in + n_out], *refs[n_in + n_out:])
        for cp in starts:
            cp.start()
        for w in waits:
            w()

    return pl.pallas_call(body, out_shape=tuple(ex.out_shapes), in_specs=[ANY] * n_in, out_specs=tuple([ANY] * n_out),
                          input_output_aliases=ex.aliases, scratch_shapes=ex.sems(), name=name)(*ex.ins)


def _carrier_call(body, args, *, out_shape, grid, in_specs, out_specs, scratch_shapes=(), sem, name, exchanges=(),
                  prefetch=None):
    out_shape, out_specs = tuple(out_shape), tuple(out_specs)
    n_in, n_out, n_scr = len(args), len(out_shape), len(scratch_shapes)
    n_pre = 0 if prefetch is None else 1
    x_args, x_outs, aliases, x_scr, spans = [], [], {}, [], []
    for ex in exchanges:
        i0, o0 = len(x_args), len(x_outs)
        for a, o in ex.aliases.items():
            aliases[n_pre + n_in + i0 + a] = n_out + o0 + o
        x_args += ex.ins
        x_outs += ex.out_shapes
        x_scr += ex.sems()
        spans.append((i0, len(ex.ins), o0, len(ex.out_shapes)))
    nx_in, nx_out = len(x_args), len(x_outs)

    def wrapped(*refs):
        refs = refs[n_pre:]
        ins, xin = refs[:n_in], refs[n_in:n_in + nx_in]
        o_base = n_in + nx_in
        outs, xout = refs[o_base:o_base + n_out], refs[o_base + n_out:o_base + n_out + nx_out]
        s_base = o_base + n_out + nx_out
        scr, xs = refs[s_base:s_base + n_scr], refs[s_base + n_scr:]

        def built(e):
            i0, ni, o0, no = spans[e]
            return exchanges[e].build(xin[i0:i0 + ni], xout[o0:o0 + no], *xs[3 * e:3 * e + 3])

        if exchanges:
            first = functools.reduce(jnp.logical_and, [pl.program_id(k) == 0 for k in range(len(grid))])
            last = functools.reduce(jnp.logical_and, [pl.program_id(k) == grid[k] - 1 for k in range(len(grid))])

            @pl.when(first)
            def _():
                for e in range(len(exchanges)):
                    for cp in built(e)[0]:
                        cp.start()

        body(*ins, *outs, *scr)

        if exchanges:
            @pl.when(last)
            def _():
                for e in range(len(exchanges)):
                    for w in built(e)[1]:
                        w()

    all_in, all_out = list(in_specs) + [ANY] * nx_in, out_specs + tuple([ANY] * nx_out)
    all_scr = list(scratch_shapes) + x_scr
    cparams = _cparams(*(sem if not exchanges else ("arbitrary",) * len(grid)))
    if prefetch is None:
        res = pl.pallas_call(wrapped, out_shape=out_shape + tuple(x_outs), grid=grid, in_specs=all_in, out_specs=all_out,
                             scratch_shapes=all_scr, input_output_aliases=aliases, compiler_params=cparams,
                             name=name)(*args, *x_args)
    else:
        gs = pltpu.PrefetchScalarGridSpec(num_scalar_prefetch=1, grid=grid, in_specs=all_in, out_specs=all_out,
                                          scratch_shapes=all_scr)
        res = pl.pallas_call(wrapped, out_shape=out_shape + tuple(x_outs), grid_spec=gs, input_output_aliases=aliases,
                             compiler_params=cparams, name=name)(prefetch, *args, *x_args)
    xres = [tuple(res[n_out + o0:n_out + o0 + no]) for (_, _, o0, no) in spans]
    return tuple(res[:n_out]), xres


def _tables(S):
    f32 = np.float32
    pos = np.arange(S, dtype=f32)
    lane = np.arange(128)
    inv = (f32(ROPE_THETA) ** (-np.arange(0, 64, 2, dtype=f32) / f32(64))).astype(f32)
    ang = (pos[:, None] * inv[None, :]).astype(np.float64)
    idx = (lane % 64) % 32
    c, s = np.cos(ang)[:, idx], np.sin(ang)[:, idx]
    first = ((lane % 64) < 32)[None, :]
    rope = np.stack([c, np.where(first, 0.0, s), np.where(first, -s, 0.0)])
    base = (f32(1.0) / (f32(ROPE_THETA) ** np.linspace(0.0, 1.0, 64, dtype=f32))).astype(f32)
    ang2 = (pos[:, None] * base[None, :]).astype(np.float64)
    c2, s2 = np.cos(ang2)[:, lane // 2], np.sin(ang2)[:, lane // 2]
    even = (lane % 2 == 0)[None, :]
    th = np.stack([c2, np.where(even, 0.0, s2), np.where(even, -s2, 0.0)])
    return np.stack([rope, th, th * (128 ** -0.5)]).astype(f32)


def _rot(a, c, sa, sb, shift):
    return a * c + pltpu.roll(a, shift, 1) * sa + pltpu.roll(a, 128 - shift, 1) * sb


def _unrot(g, c, sa, sb, shift):
    return g * c + pltpu.roll(g * sa, 128 - shift, 1) + pltpu.roll(g * sb, shift, 1)


def _ret_consts():
    h = np.arange(RET_HEADS, dtype=np.float64)
    log_g = np.log1p(-(2.0 ** (-5.0 - h)))
    idx = np.arange(BLK, dtype=np.float64)
    diff = idx[:, None] - idx[None, :]
    dmask = np.where(diff[None] >= 0, np.exp(np.maximum(diff, 0.0)[None] * log_g[:, None, None]), 0.0)
    zeta = np.exp((BLK - 1 - idx)[None, :] * log_g[:, None])
    xi = np.exp((idx + 1.0)[None, :] * log_g[:, None])
    dec = np.exp(BLK * log_g)
    rep = lambda v: np.broadcast_to(v[:, :, None], (RET_HEADS, BLK, 128))
    return (jnp.asarray(dmask, F32), jnp.asarray(rep(zeta), F32), jnp.asarray(rep(xi), F32),
            jnp.asarray(np.broadcast_to(dec[:, None, None], (RET_HEADS, 8, 256)), F32))


def _rms_fwd(x, g):
    S = x.shape[0]
    tm = 512

    def body(x_ref, g_ref, h_ref, ht_ref):
        xv = x_ref[...]
        r = lax.rsqrt(jnp.mean(xv * xv, axis=-1, keepdims=True) + NORM_EPS)
        h = xv * r * g_ref[...]
        h_ref[...] = h.astype(BF16)
        ht_ref[...] = h.T.astype(BF16)

    return pl.pallas_call(
        body, out_shape=(SDS((S, D_MODEL), BF16), SDS((D_MODEL, S), BF16)), grid=(S // tm,),
        in_specs=[pl.BlockSpec((tm, D_MODEL), lambda i: (i, 0)), pl.BlockSpec((1, D_MODEL), lambda i: (0, 0))],
        out_specs=(pl.BlockSpec((tm, D_MODEL), lambda i: (i, 0)), pl.BlockSpec((D_MODEL, tm), lambda i: (0, i))),
        compiler_params=_cparams("parallel"), name="rms_fwd")(x, g)


def _in_proj(h, w_in, tab, exchanges=()):
    S = h.shape[0]
    tm = min(S, 2048)

    def body(h_ref, w_ref, t_ref, o_ref):
        j = pl.program_id(1)
        is_rope = j < 6
        is_theta = (j == QR_B) | (j == KR_B)
        sub = 512

        def rotated(shift):
            for i in range(tm // sub):
                rows = slice(i * sub, (i + 1) * sub)
                acc = _dot(h_ref[rows, :], w_ref[...])
                c, sa, sb = t_ref[0, 0, rows, :], t_ref[0, 1, rows, :], t_ref[0, 2, rows, :]
                for k in range(COLB // 128):
                    sl = slice(k * 128, (k + 1) * 128)
                    o_ref[rows, sl] = _rot(acc[:, sl], c, sa, sb, shift).astype(BF16)

        @pl.when(is_rope)
        def _():
            rotated(32)

        @pl.when(is_theta)
        def _():
            rotated(1)

        @pl.when(jnp.logical_not(is_rope | is_theta))
        def _():
            o_ref[...] = _dot(h_ref[...], w_ref[...]).astype(BF16)

    def tab_map(i, j):
        return (jnp.where(j == QR_B, 1, jnp.where(j == KR_B, 2, 0)), 0, i, 0)

    (proj,), xres = _carrier_call(
        body, (h, w_in, tab), out_shape=(SDS((S, PROJ_W), BF16),), grid=(S // tm, N_COLB),
        in_specs=[pl.BlockSpec((tm, D_MODEL), lambda i, j: (i, 0)),
                  pl.BlockSpec((D_MODEL, COLB), lambda i, j: (0, j)),
                  pl.BlockSpec((1, 3, tm, 128), tab_map)],
        out_specs=(pl.BlockSpec((tm, COLB), lambda i, j: (i, j)),),
        sem=("parallel", "arbitrary"), name="in_proj", exchanges=exchanges)
    return proj, xres


def _band_mask(n):
    qi = lax.broadcasted_iota(jnp.int32, (BLK, 2 * BLK), 0)
    kj = lax.broadcasted_iota(jnp.int32, (BLK, 2 * BLK), 1)
    dist = BLK + qi - kj
    return (dist >= 0) & (dist <= BLK) & ((kj >= BLK) | (n > 0))


def _qkv_col(d, gi):
    if d == 1:
        return lambda t, r: 3 * t + gi
    return lambda t, r: 3 * r + t


def _attn_fwd(qkv, d, gi, exchanges=()):
    L = qkv.shape[0]
    nb = L // BLK

    def body(q_ref, kc_ref, kp_ref, vc_ref, vp_ref, o_ref, lse_ref):
        n = pl.program_id(1)
        mask = _band_mask(n)
        mask2 = jnp.concatenate([mask, mask], axis=0)
        lane = lax.broadcasted_iota(jnp.int32, (BLK, 128), 1)
        lo = lane < 64
        lse_all = jnp.zeros((BLK, 128), F32)
        for c in range(4):
            sl = slice(c * 128, (c + 1) * 128)
            q = q_ref[:, sl]
            k = jnp.concatenate([kp_ref[:, sl], kc_ref[:, sl]], axis=0)
            v = jnp.concatenate([vp_ref[:, sl], vc_ref[:, sl]], axis=0)
            q2 = jnp.concatenate([jnp.where(lo, q, jnp.zeros_like(q)), jnp.where(lo, jnp.zeros_like(q), q)], axis=0)
            s = jnp.where(mask2, _dot_nt(q2, k) * 0.125, jnp.float32(-1e30))
            m = jnp.max(s, axis=-1, keepdims=True)
            p = jnp.exp(s - m)
            l = jnp.sum(p, axis=-1, keepdims=True)
            o2 = _dot((p / l).astype(BF16), v)
            o_ref[:, sl] = jnp.where(lo, o2[:BLK], o2[BLK:])
            lse = m + jnp.log(l)
            lse_all = jnp.where(lane // 16 == 2 * c, lse[:BLK], jnp.where(lane // 16 == 2 * c + 1, lse[BLK:], lse_all))
        lse_ref[...] = lse_all

    prev = lambda n: jnp.maximum(n - 1, 0)
    col = _qkv_col(d, gi)
    return _carrier_call(
        body, (qkv,) * 5, out_shape=(SDS((L, d * 512), F32), SDS((L, d * 128), F32)), grid=(d, nb),
        in_specs=[pl.BlockSpec((BLK, 512), lambda r, n: (n, col(0, r))),
                  pl.BlockSpec((BLK, 512), lambda r, n: (n, col(1, r))),
                  pl.BlockSpec((BLK, 512), lambda r, n: (prev(n), col(1, r))),
                  pl.BlockSpec((BLK, 512), lambda r, n: (n, col(2, r))),
                  pl.BlockSpec((BLK, 512), lambda r, n: (prev(n), col(2, r)))],
        out_specs=(pl.BlockSpec((BLK, 512), lambda r, n: (n, r)),
                   pl.BlockSpec((BLK, 128), lambda r, n: (n, r))),
        sem=("parallel", "arbitrary"), name=f"attn_fwd_g{gi}", exchanges=exchanges)


def _qkv_to_sub(proj, d, gi):
    S = proj.shape[0]
    tm = 512
    n = tm // d

    def body(q_ref, k_ref, v_ref, o_ref, scr):
        for t, ref in enumerate((q_ref, k_ref, v_ref)):
            for c in range(4):
                scr[c] = ref[:, c * 128:(c + 1) * 128].astype(F32)
            for r in range(d):
                for c in range(4):
                    col = (3 * r + t) * 512 + c * 128
                    o_ref[:, col:col + 128] = scr[c, pl.ds(r, n, stride=d), :].astype(BF16)

    return pl.pallas_call(
        body, out_shape=SDS((S // d, d * 1536), BF16), grid=(S // tm,),
        in_specs=[pl.BlockSpec((tm, 512), lambda i, b=b: (i, b + gi)) for b in (QA_B, KA_B, VA_B)],
        out_specs=pl.BlockSpec((n, d * 1536), lambda i: (i, 0)),
        scratch_shapes=[pltpu.VMEM((4, tm, 128), F32)],
        compiler_params=_cparams("parallel"), name=f"qkv_to_sub_g{gi}")(proj, proj, proj)


def _attn_merge(os_, lses):
    S = os_[0].shape[0]
    tm = 512

    def body(o0, o1, o2, l0, l1, l2, att_ref, lt_ref, so1, so2, sl1, sl2):
        lo = lax.broadcasted_iota(jnp.int32, (tm, 128), 1) < 64

        def natural(ref, d, scr, width):
            nch = width // 128
            if d == 1:
                return [ref[:, c * 128:(c + 1) * 128] for c in range(nch)]
            for r in range(d):
                for c in range(nch):
                    scr[c, pl.ds(r, tm // d, stride=d), :] = ref[:, r * width + c * 128:r * width + (c + 1) * 128]
            return [scr[c] for c in range(nch)]

        ls = [natural(l, d, s, 128)[0] for l, d, s in zip((l0, l1, l2), DILATIONS, (None, sl1, sl2))]
        m = jnp.maximum(jnp.maximum(ls[0], ls[1]), ls[2])
        es = [jnp.exp(v - m) for v in ls]
        z = es[0] + es[1] + es[2]
        lt_ref[...] = m + jnp.log(z)
        ws = [e / z for e in es]
        o_nat = [natural(o, d, s, 512) for o, d, s in zip((o0, o1, o2), DILATIONS, (None, so1, so2))]
        for c in range(4):
            acc = jnp.zeros((tm, 128), F32)
            for g in range(3):
                w_lo = jnp.broadcast_to(ws[g][:, 32 * c:32 * c + 1], (tm, 128))
                w_hi = jnp.broadcast_to(ws[g][:, 32 * c + 16:32 * c + 17], (tm, 128))
                acc = acc + jnp.where(lo, w_lo, w_hi) * o_nat[g][c]
            att_ref[:, c * 128:(c + 1) * 128] = acc.astype(BF16)

    sub = lambda w: [pl.BlockSpec((tm // d, d * w), lambda i: (i, 0)) for d in DILATIONS]
    return pl.pallas_call(
        body, out_shape=(SDS((S, 512), BF16), SDS((S, 128), F32)), grid=(S // tm,),
        in_specs=sub(512) + sub(128),
        out_specs=(pl.BlockSpec((tm, 512), lambda i: (i, 0)), pl.BlockSpec((tm, 128), lambda i: (i, 0))),
        scratch_shapes=[pltpu.VMEM((4, tm, 128), F32), pltpu.VMEM((4, tm, 128), F32),
                        pltpu.VMEM((1, tm, 128), F32), pltpu.VMEM((1, tm, 128), F32)],
        compiler_params=_cparams("parallel"), name="attn_merge")(*os_, *lses)


def _assemble_dproj(att_grads, dqr, dkr, dvr, dgr, dga, dgrr):
    S = dqr.shape[0]
    tm = 256

    def body(*refs):
        a = [refs[3 * t:3 * t + 3] for t in range(3)]
        dqr_ref, dkr_ref, dvr_ref, dgr_ref, dga_ref, dgrr_ref, o_ref, scr = refs[9:]
        for t in range(3):
            for g, d in enumerate(DILATIONS):
                base = (3 * t + g) * COLB
                if d == 1:
                    o_ref[:, base:base + COLB] = a[t][g][...]
                    continue
                for c in range(4):
                    for r in range(d):
                        scr[c, pl.ds(r, tm // d, stride=d), :] = a[t][g][:, r * 512 + c * 128:r * 512 + (c + 1) * 128].astype(F32)
                    o_ref[:, base + c * 128:base + (c + 1) * 128] = scr[c].astype(BF16)
        o_ref[:, 9 * COLB:10 * COLB] = dqr_ref[...]
        o_ref[:, 10 * COLB:11 * COLB] = dkr_ref[...]
        o_ref[:, 11 * COLB:13 * COLB] = dvr_ref[...]
        o_ref[:, 13 * COLB:15 * COLB] = dgr_ref[...]
        o_ref[:, 15 * COLB:17 * COLB] = dga_ref[...]
        o_ref[:, 17 * COLB:19 * COLB] = dgrr_ref[...]

    sub = [pl.BlockSpec((tm // d, d * 512), lambda i: (i, 0)) for d in DILATIONS]
    row = lambda w: pl.BlockSpec((tm, w), lambda i: (i, 0))
    flat = [att_grads[t][g] for t in range(3) for g in range(3)]
    return pl.pallas_call(
        body, out_shape=SDS((S, PROJ_W), BF16), grid=(S // tm,),
        in_specs=sub * 3 + [row(512), row(512), row(1024), row(1024), row(1024), row(1024)],
        out_specs=row(PROJ_W), scratch_shapes=[pltpu.VMEM((4, tm, 128), F32)],
        compiler_params=_cparams("parallel"), name="assemble_dproj")(*flat, dqr, dkr, dvr, dgr, dga, dgrr)


def _ret_fwd(proj, consts):
    S = proj.shape[0]
    nc = S // BLK
    dmask, zeta, xi, dec = consts

    def body(q_ref, k_ref, v0_ref, v1_ref, g0_ref, g1_ref, dm_ref, z_ref, x_ref, dec_ref,
             y_ref, rn_ref, rs_ref, st_ref, R):
        @pl.when(pl.program_id(0) == 0)
        def _():
            R[...] = jnp.zeros_like(R)

        lane16 = lax.broadcasted_iota(jnp.int32, (BLK, 128), 1) // 16
        rs_all = jnp.zeros((BLK, 128), F32)
        for h in range(RET_HEADS):
            hs = slice(h * 128, (h + 1) * 128)
            vs = slice((h % 2) * 256, (h % 2 + 1) * 256)
            os_ = slice(h * 256, (h + 1) * 256)
            q, k = q_ref[:, hs], k_ref[:, hs]
            v = (v0_ref if h < 2 else v1_ref)[:, vs]
            Rb = R[h].astype(BF16)
            st_ref[h] = Rb
            s = _dot_nt(q, k) * dm_ref[h]
            o = _dot(s.astype(BF16), v) + _dot((q.astype(F32) * x_ref[h]).astype(BF16), Rb)
            kz = (k.astype(F32) * z_ref[h]).astype(BF16)
            R[h] = R[h] * dec_ref[h, 0:1, :] + _dot_tn(kz, v)
            mu = jnp.mean(o, axis=-1, keepdims=True)
            oc = o - mu
            rstd = lax.rsqrt(jnp.mean(oc * oc, axis=-1, keepdims=True) + NORM_EPS)
            rn = oc * rstd
            gr = (g0_ref if h < 2 else g1_ref)[:, vs].astype(F32)
            y_ref[:, os_] = (rn * gr * _sigmoid(gr)).astype(BF16)
            rn_ref[:, os_] = rn.astype(BF16)
            rs_all = jnp.where(lane16 == h, rstd, rs_all)
        rs_ref[...] = rs_all

    cst = lambda shape: pl.BlockSpec(shape, lambda c: (0, 0, 0))
    blk = lambda j: pl.BlockSpec((BLK, 512), lambda c: (c, j))
    return pl.pallas_call(
        body,
        out_shape=(SDS((S, 1024), BF16), SDS((S, 1024), BF16), SDS((S, 128), F32), SDS((RET_HEADS, nc, BLK, 256), BF16)),
        grid=(nc,),
        in_specs=[blk(QR_B), blk(KR_B), blk(11), blk(12), blk(13), blk(14),
                  cst((RET_HEADS, BLK, BLK)), cst((RET_HEADS, BLK, 128)), cst((RET_HEADS, BLK, 128)), cst((RET_HEADS, 8, 256))],
        out_specs=(pl.BlockSpec((BLK, 1024), lambda c: (c, 0)), pl.BlockSpec((BLK, 1024), lambda c: (c, 0)),
                   pl.BlockSpec((BLK, 128), lambda c: (c, 0)),
                   pl.BlockSpec((RET_HEADS, None, BLK, 256), lambda c: (0, c, 0, 0))),
        scratch_shapes=[pltpu.VMEM((RET_HEADS, BLK, 256), F32)],
        compiler_params=_cparams("arbitrary"), name="ret_fwd")(proj, proj, proj, proj, proj, proj, dmask, zeta, xi, dec)


def _branch_merge(att, yrin, proj, wa, wr):
    S = att.shape[0]
    tm = min(S, 2048)

    def body(a_ref, y_ref, ga_ref, gr_ref, wa_ref, wr_ref, m_ref, ya_ref, yr_ref):
        for rows in _row_pieces(tm):
            ya = _dot(a_ref[rows, :], wa_ref[...])
            yr = _dot(y_ref[rows, :], wr_ref[...])
            m_ref[rows, :] = (_sigmoid(ga_ref[rows, :].astype(F32)) * ya
                              + _sigmoid(gr_ref[rows, :].astype(F32)) * yr).astype(BF16)
            ya_ref[rows, :] = ya.astype(BF16)
            yr_ref[rows, :] = yr.astype(BF16)

    ospec = pl.BlockSpec((tm, 512), lambda i, j: (i, j))
    return pl.pallas_call(
        body, out_shape=(SDS((S, D_MODEL), BF16),) * 3, grid=(S // tm, 2),
        in_specs=[pl.BlockSpec((tm, 512), lambda i, j: (i, 0)), pl.BlockSpec((tm, 1024), lambda i, j: (i, 0)),
                  pl.BlockSpec((tm, 512), lambda i, j: (i, 15 + j)), pl.BlockSpec((tm, 512), lambda i, j: (i, 17 + j)),
                  pl.BlockSpec((512, 512), lambda i, j: (0, j)), pl.BlockSpec((1024, 512), lambda i, j: (0, j))],
        out_specs=(ospec, ospec, ospec),
        compiler_params=_cparams("parallel", "arbitrary"), name="branch_merge")(att, yrin, proj, proj, wa, wr)


def _out_proj(merged, wo, x, g2):
    S = x.shape[0]
    tm = 1024

    def body(m_ref, w_ref, x_ref, g_ref, x1_ref, h2_ref):
        x1 = x_ref[...] + _dot(m_ref[...], w_ref[...])
        x1_ref[...] = x1
        r = lax.rsqrt(jnp.mean(x1 * x1, axis=-1, keepdims=True) + NORM_EPS)
        h2_ref[...] = (x1 * r * g_ref[...]).astype(BF16)

    row = pl.BlockSpec((tm, D_MODEL), lambda i: (i, 0))
    return pl.pallas_call(
        body, out_shape=(SDS((S, D_MODEL), F32), SDS((S, D_MODEL), BF16)), grid=(S // tm,),
        in_specs=[row, pl.BlockSpec((D_MODEL, D_MODEL), lambda i: (0, 0)), row, pl.BlockSpec((1, D_MODEL), lambda i: (0, 0))],
        out_specs=(row, row), compiler_params=_cparams("parallel"), name="out_proj")(merged, wo, x, g2)


def _ffn_up(h2, wg, wu):
    S = h2.shape[0]
    tm = min(S, 2048)

    def body(h_ref, wg_ref, wu_ref, g_ref, u_ref, a_ref):
        for rows in _row_pieces(tm):
            hv = h_ref[rows, :]
            g = _dot(hv, wg_ref[...])
            u = _dot(hv, wu_ref[...])
            g_ref[rows, :] = g.astype(BF16)
            u_ref[rows, :] = u.astype(BF16)
            a_ref[rows, :] = (g * _sigmoid(g) * u).astype(BF16)

    wspec = pl.BlockSpec((None, D_MODEL, HID_S), lambda i, s: (s, 0, 0))
    ospec = pl.BlockSpec((None, tm, HID_S), lambda i, s: (s, i, 0))
    return pl.pallas_call(
        body, out_shape=(SDS((N_SHARD, S, HID_S), BF16),) * 3, grid=(S // tm, N_SHARD),
        in_specs=[pl.BlockSpec((tm, D_MODEL), lambda i, s: (i, 0)), wspec, wspec],
        out_specs=(ospec, ospec, ospec),
        compiler_params=_cparams("parallel", "arbitrary"), name="ffn_up")(h2, wg, wu)


def _ffn_down_loss(act, wd, x1, g3, tgt):
    S = x1.shape[0]
    tm = 512

    def body(a_ref, w_ref, x_ref, g_ref, t_ref, dx_ref, dxb_ref, dg_ref, ls_ref):
        @pl.when(pl.program_id(0) == 0)
        def _():
            dg_ref[...] = jnp.zeros_like(dg_ref)
            ls_ref[...] = jnp.zeros_like(ls_ref)

        g = g_ref[...]
        for rows in _row_pieces(tm, 256):
            y = _dot(a_ref[0, rows, :], w_ref[0])
            for s in range(1, N_SHARD):
                y = y + _dot(a_ref[s, rows, :], w_ref[s])
            x2 = x_ref[rows, :] + y
            r = lax.rsqrt(jnp.mean(x2 * x2, axis=-1, keepdims=True) + NORM_EPS)
            xh = x2 * r
            err = xh * g - t_ref[rows, :]
            ls_ref[...] += jnp.sum(jnp.sum(err * err, axis=-1, keepdims=True), axis=0, keepdims=True) * (0.5 / D_MODEL)
            dy = err * (1.0 / D_MODEL)
            dg_ref[...] += jnp.sum(dy * xh, axis=0, keepdims=True)
            dxh = dy * g
            dx = r * (dxh - xh * jnp.mean(dxh * xh, axis=-1, keepdims=True))
            dx_ref[rows, :] = dx
            dxb_ref[rows, :] = dx.astype(BF16)

    row = pl.BlockSpec((tm, D_MODEL), lambda i: (i, 0))
    vec = pl.BlockSpec((1, D_MODEL), lambda i: (0, 0))
    return pl.pallas_call(
        body, out_shape=(SDS((S, D_MODEL), F32), SDS((S, D_MODEL), BF16), SDS((1, D_MODEL), F32), SDS((8, 128), F32)),
        grid=(S // tm,),
        in_specs=[pl.BlockSpec((N_SHARD, tm, HID_S), lambda i: (0, i, 0)),
                  pl.BlockSpec((N_SHARD, HID_S, D_MODEL), lambda i: (0, 0, 0), pipeline_mode=pl.Buffered(1)),
                  row, vec, row],
        out_specs=(row, row, vec, pl.BlockSpec((8, 128), lambda i: (0, 0))),
        compiler_params=_cparams("arbitrary"), name="ffn_down_loss")(act, wd, x1, g3, tgt)


def _ffn_down_bwd(dx2b, wd, gte, up):
    S = dx2b.shape[0]
    tm = min(S, 2048)

    def body(d_ref, w_ref, g_ref, u_ref, dg_ref, du_ref):
        for rows in _row_pieces(tm, 256):
            da = _dot_nt(d_ref[rows, :], w_ref[...])
            g = g_ref[rows, :].astype(F32)
            sg = _sigmoid(g)
            dg_ref[rows, :] = (da * u_ref[rows, :].astype(F32) * sg * (1.0 + g * (1.0 - sg))).astype(BF16)
            du_ref[rows, :] = (da * g * sg).astype(BF16)

    aspec = pl.BlockSpec((None, tm, HID_S), lambda i, s: (s, i, 0))
    return pl.pallas_call(
        body, out_shape=(SDS((N_SHARD, S, HID_S), BF16),) * 2, grid=(S // tm, N_SHARD),
        in_specs=[pl.BlockSpec((tm, D_MODEL), lambda i, s: (i, 0)),
                  pl.BlockSpec((None, HID_S, D_MODEL), lambda i, s: (s, 0, 0)), aspec, aspec],
        out_specs=(aspec, aspec),
        compiler_params=_cparams("parallel", "arbitrary"), name="ffn_down_bwd")(dx2b, wd, gte, up)


def _wgrad(name, a, b, a_spec, b_spec, out_shape, out_spec, n_par, S):
    tk = 2048

    def body(a_ref, b_ref, o_ref):
        @pl.when(pl.program_id(1) == 0)
        def _():
            o_ref[...] = jnp.zeros_like(o_ref)

        o_ref[...] += _dot_tn(a_ref[...], b_ref[...])

    return pl.pallas_call(
        body, out_shape=SDS(out_shape, F32), grid=(n_par, S // tk),
        in_specs=[a_spec(tk), b_spec(tk)], out_specs=out_spec,
        compiler_params=_cparams("parallel", "arbitrary"), name=name)(a, b)


def _ffn_up_bwd(dgte, dup, wg, wu, x1, g2, dx2, exchanges=()):
    S = x1.shape[0]
    tm = 512

    def body(dg_ref, du_ref, wg_ref, wu_ref, x_ref, g_ref, dx2_ref, dx_ref, dxb_ref, dgn_ref):
        @pl.when(pl.program_id(0) == 0)
        def _():
            dgn_ref[...] = jnp.zeros_like(dgn_ref)

        for rows in _row_pieces(tm, 256):
            dh = _dot_nt(dg_ref[0, rows, :], wg_ref[0]) + _dot_nt(du_ref[0, rows, :], wu_ref[0])
            for s in range(1, N_SHARD):
                dh = dh + _dot_nt(dg_ref[s, rows, :], wg_ref[s]) + _dot_nt(du_ref[s, rows, :], wu_ref[s])
            xv = x_ref[rows, :]
            r = lax.rsqrt(jnp.mean(xv * xv, axis=-1, keepdims=True) + NORM_EPS)
            xh = xv * r
            dgn_ref[...] += jnp.sum(dh * xh, axis=0, keepdims=True)
            dxh = dh * g_ref[...]
            dx = dx2_ref[rows, :] + r * (dxh - xh * jnp.mean(dxh * xh, axis=-1, keepdims=True))
            dx_ref[rows, :] = dx
            dxb_ref[rows, :] = dx.astype(BF16)

    row = pl.BlockSpec((tm, D_MODEL), lambda i: (i, 0))
    vec = pl.BlockSpec((1, D_MODEL), lambda i: (0, 0))
    aspec = pl.BlockSpec((N_SHARD, tm, HID_S), lambda i: (0, i, 0))
    wspec = pl.BlockSpec((N_SHARD, D_MODEL, HID_S), lambda i: (0, 0, 0), pipeline_mode=pl.Buffered(1))
    return _carrier_call(
        body, (dgte, dup, wg, wu, x1, g2, dx2),
        out_shape=(SDS((S, D_MODEL), F32), SDS((S, D_MODEL), BF16), SDS((1, D_MODEL), F32)),
        grid=(S // tm,),
        in_specs=[aspec, aspec, wspec, wspec, row, vec, row], out_specs=(row, row, vec),
        sem=("arbitrary",), name="ffn_up_bwd", exchanges=exchanges)


def _out_proj_bwd(dx1b, wo, proj, ya, yr):
    S = dx1b.shape[0]
    tm = min(S, 2048)

    def body(d_ref, w_ref, ga_ref, gr_ref, ya_ref, yr_ref, dya_ref, dyr_ref, dga_ref, dgr_ref):
        for rows in _row_pieces(tm, 256):
            dm = _dot_nt(d_ref[rows, :], w_ref[...])
            sa = _sigmoid(ga_ref[rows, :].astype(F32))
            sr = _sigmoid(gr_ref[rows, :].astype(F32))
            dya_ref[rows, :] = (dm * sa).astype(BF16)
            dyr_ref[rows, :] = (dm * sr).astype(BF16)
            dga_ref[rows, :] = (dm * ya_ref[rows, :].astype(F32) * sa * (1.0 - sa)).astype(BF16)
            dgr_ref[rows, :] = (dm * yr_ref[rows, :].astype(F32) * sr * (1.0 - sr)).astype(BF16)

    blk = pl.BlockSpec((tm, 512), lambda i, j: (i, j))
    return pl.pallas_call(
        body, out_shape=(SDS((S, D_MODEL), BF16),) * 4, grid=(S // tm, 2),
        in_specs=[pl.BlockSpec((tm, D_MODEL), lambda i, j: (i, 0)), pl.BlockSpec((512, D_MODEL), lambda i, j: (j, 0)),
                  pl.BlockSpec((tm, 512), lambda i, j: (i, 15 + j)), pl.BlockSpec((tm, 512), lambda i, j: (i, 17 + j)),
                  blk, blk],
        out_specs=(blk,) * 4,
        compiler_params=_cparams("parallel", "arbitrary"), name="out_proj_bwd")(dx1b, wo, proj, proj, ya, yr)


def _branch_bwd(dya, dyr, wa, wr, att):
    S = dya.shape[0]
    tm = 1024

    def body(da_ref, dr_ref, wa_ref, wr_ref, att_ref, datt_ref, rho_ref, dyi_ref):
        datt = _dot_nt(da_ref[...], wa_ref[...])
        datt_ref[...] = datt.astype(BF16)
        dyi_ref[...] = _dot_nt(dr_ref[...], wr_ref[...]).astype(BF16)
        prod = datt * att_ref[...].astype(F32)
        lane = lax.broadcasted_iota(jnp.int32, (tm, 128), 1)
        lo = lane < 64
        rho = jnp.zeros((tm, 128), F32)
        for c in range(4):
            pc = prod[:, c * 128:(c + 1) * 128]
            tot = jnp.sum(pc, axis=-1, keepdims=True)
            low = jnp.sum(jnp.where(lo, pc, 0.0), axis=-1, keepdims=True)
            rho = jnp.where(lane // 16 == 2 * c, low, jnp.where(lane // 16 == 2 * c + 1, tot - low, rho))
        rho_ref[...] = rho

    row = lambda w: pl.BlockSpec((tm, w), lambda i: (i, 0))
    return pl.pallas_call(
        body, out_shape=(SDS((S, 512), BF16), SDS((S, 128), F32), SDS((S, 1024), BF16)), grid=(S // tm,),
        in_specs=[row(1024), row(1024), pl.BlockSpec((512, 1024), lambda i: (0, 0)),
                  pl.BlockSpec((1024, 1024), lambda i: (0, 0)), row(512)],
        out_specs=(row(512), row(128), row(1024)),
        compiler_params=_cparams("parallel"), name="branch_bwd")(dya, dyr, wa, wr, att)


def _attn_bwd(qkv, datt, lse, rho, rtab, d, gi, exchanges=()):
    L = qkv.shape[0]
    nb = L // BLK

    def body(q_ref, kc_ref, kp_ref, vc_ref, vp_ref, do_ref, lse_ref, rho_ref, tq_ref, tk_ref,
             dq_ref, dk_ref, dv_ref, ck, cv):
        n = pl.program_id(1)

        @pl.when(n == 0)
        def _():
            ck[...] = jnp.zeros_like(ck)
            cv[...] = jnp.zeros_like(cv)

        def store_rot(ref, val, t_ref, c):
            sl = slice(c * 128, (c + 1) * 128)
            ref[:, sl] = _unrot(val, t_ref[0], t_ref[1], t_ref[2], 32).astype(BF16)

        @pl.when(n < nb)
        def _():
            mask = _band_mask(n)
            mask2 = jnp.concatenate([mask, mask], axis=0)
            lo = lax.broadcasted_iota(jnp.int32, (BLK, 128), 1) < 64

            def stacked(a):
                return jnp.concatenate([jnp.where(lo, a, jnp.zeros_like(a)), jnp.where(lo, jnp.zeros_like(a), a)], axis=0)

            def head_cols(ref, c):
                return jnp.concatenate([jnp.broadcast_to(ref[:, 32 * c:32 * c + 1], (BLK, 2 * BLK)),
                                        jnp.broadcast_to(ref[:, 32 * c + 16:32 * c + 17], (BLK, 2 * BLK))], axis=0)

            for c in range(4):
                sl = slice(c * 128, (c + 1) * 128)
                q2, do2 = stacked(q_ref[:, sl]), stacked(do_ref[:, sl])
                k = jnp.concatenate([kp_ref[:, sl], kc_ref[:, sl]], axis=0)
                v = jnp.concatenate([vp_ref[:, sl], vc_ref[:, sl]], axis=0)
                s = _dot_nt(q2, k) * 0.125
                p = jnp.where(mask2, jnp.exp(s - head_cols(lse_ref, c)), 0.0)
                dp = _dot_nt(do2, v)
                ds = (p * (dp - head_cols(rho_ref, c)) * 0.125).astype(BF16)
                dq2 = _dot(ds, k)
                dq_c = jnp.where(lo, dq2[:BLK], dq2[BLK:])
                dk_c = _dot_tn(ds, q2)
                dv_c = _dot_tn(p.astype(BF16), do2)
                store_rot(dq_ref, dq_c, tq_ref, c)
                store_rot(dk_ref, ck[:, sl] + dk_c[:BLK], tk_ref, c)
                dv_ref[:, sl] = (cv[:, sl] + dv_c[:BLK]).astype(BF16)
                ck[:, sl] = dk_c[BLK:]
                cv[:, sl] = dv_c[BLK:]

        @pl.when(n == nb)
        def _():
            for c in range(4):
                sl = slice(c * 128, (c + 1) * 128)
                store_rot(dk_ref, ck[:, sl], tk_ref, c)
            dv_ref[...] = cv[...].astype(BF16)

    cur = lambda n: jnp.minimum(n, nb - 1)
    prev = lambda n: jnp.maximum(jnp.minimum(n, nb - 1) - 1, 0)
    fin = lambda n: jnp.maximum(n - 1, 0)
    col = _qkv_col(d, gi)
    return _carrier_call(
        body, (qkv, qkv, qkv, qkv, qkv, datt, lse, rho, rtab, rtab),
        out_shape=(SDS((L, d * 512), BF16),) * 3, grid=(d, nb + 1),
        in_specs=[pl.BlockSpec((BLK, 512), lambda r, n: (cur(n), col(0, r))),
                  pl.BlockSpec((BLK, 512), lambda r, n: (cur(n), col(1, r))),
                  pl.BlockSpec((BLK, 512), lambda r, n: (prev(n), col(1, r))),
                  pl.BlockSpec((BLK, 512), lambda r, n: (cur(n), col(2, r))),
                  pl.BlockSpec((BLK, 512), lambda r, n: (prev(n), col(2, r))),
                  pl.BlockSpec((BLK, 512), lambda r, n: (cur(n), r)),
                  pl.BlockSpec((BLK, 128), lambda r, n: (cur(n), r)),
                  pl.BlockSpec((BLK, 128), lambda r, n: (cur(n), r)),
                  pl.BlockSpec((3, BLK, 128), lambda r, n: (0, cur(n), r)),
                  pl.BlockSpec((3, BLK, 128), lambda r, n: (0, fin(n), r))],
        out_specs=(pl.BlockSpec((BLK, 512), lambda r, n: (cur(n), r)),
                   pl.BlockSpec((BLK, 512), lambda r, n: (fin(n), r)),
                   pl.BlockSpec((BLK, 512), lambda r, n: (fin(n), r))),
        scratch_shapes=[pltpu.VMEM((BLK, 512), F32), pltpu.VMEM((BLK, 512), F32)],
        sem=("parallel", "arbitrary"), name=f"attn_bwd_g{gi}", exchanges=exchanges)


def _ret_bwd(proj, rn, rstd, dyrin, states, tab, consts, exchanges=()):
    S = proj.shape[0]
    nc = S // BLK
    dmask, zeta, xi, dec = consts

    def body(q_ref, k_ref, v0_ref, v1_ref, g0_ref, g1_ref, rn_ref, rs_ref, dy_ref, st_ref, tq_ref, tk_ref,
             dm_ref, z_ref, x_ref, dec_ref, dq_ref, dk_ref, dv_ref, dgr_ref, dR):
        @pl.when(pl.program_id(0) == 0)
        def _():
            dR[...] = jnp.zeros_like(dR)

        for h in range(RET_HEADS):
            hs = slice(h * 128, (h + 1) * 128)
            vs = slice((h % 2) * 256, (h % 2 + 1) * 256)
            os_ = slice(h * 256, (h + 1) * 256)
            q, k = q_ref[:, hs], k_ref[:, hs]
            v = (v0_ref if h < 2 else v1_ref)[:, vs]
            gr = (g0_ref if h < 2 else g1_ref)[:, vs].astype(F32)
            sg = _sigmoid(gr)
            rn_v = rn_ref[:, os_].astype(F32)
            dyi = dy_ref[:, os_].astype(F32)
            dgr_ref[:, os_] = (dyi * rn_v * sg * (1.0 + gr * (1.0 - sg))).astype(BF16)
            drn = dyi * gr * sg
            rstd = jnp.broadcast_to(rs_ref[:, 16 * h:16 * h + 1], (BLK, 256))
            do = rstd * (drn - jnp.mean(drn, axis=-1, keepdims=True) - rn_v * jnp.mean(drn * rn_v, axis=-1, keepdims=True))
            dob = do.astype(BF16)
            Rb = st_ref[h]
            dRb = dR[h].astype(BF16)
            dm, zt, xt = dm_ref[h], z_ref[h], x_ref[h]
            sD = (_dot_nt(q, k) * dm).astype(BF16)
            kz = (k.astype(F32) * zt).astype(BF16)
            qx = (q.astype(F32) * xt).astype(BF16)
            dv_ref[:, os_] = (_dot_tn(sD, dob) + _dot(kz, dRb)).astype(BF16)
            dS = (_dot_nt(dob, v) * dm).astype(BF16)
            dq = _dot(dS, k) + _dot_nt(dob, Rb) * xt
            dk = _dot_tn(dS, q) + _dot_nt(v, dRb) * zt
            dR[h] = dR[h] * dec_ref[h, 0:1, :] + _dot_tn(qx, dob)
            dq_ref[:, hs] = _unrot(dq, tq_ref[0], tq_ref[1], tq_ref[2], 1).astype(BF16)
            dk_ref[:, hs] = _unrot(dk, tk_ref[0], tk_ref[1], tk_ref[2], 1).astype(BF16)

    rc = lambda c: nc - 1 - c
    cst = lambda shape: pl.BlockSpec(shape, lambda c: (0, 0, 0))
    blk = lambda j: pl.BlockSpec((BLK, 512), lambda c: (rc(c), j))
    row = lambda w: pl.BlockSpec((BLK, w), lambda c: (rc(c), 0))
    return _carrier_call(
        body, (proj, proj, proj, proj, proj, proj, rn, rstd, dyrin, states, tab, tab, dmask, zeta, xi, dec),
        out_shape=(SDS((S, 512), BF16), SDS((S, 512), BF16), SDS((S, 1024), BF16), SDS((S, 1024), BF16)),
        grid=(nc,),
        in_specs=[blk(QR_B), blk(KR_B), blk(11), blk(12), blk(13), blk(14), row(1024), row(128), row(1024),
                  pl.BlockSpec((RET_HEADS, None, BLK, 256), lambda c: (0, rc(c), 0, 0)),
                  pl.BlockSpec((None, 3, BLK, 128), lambda c: (1, 0, rc(c), 0)),
                  pl.BlockSpec((None, 3, BLK, 128), lambda c: (2, 0, rc(c), 0)),
                  cst((RET_HEADS, BLK, BLK)), cst((RET_HEADS, BLK, 128)), cst((RET_HEADS, BLK, 128)), cst((RET_HEADS, 8, 256))],
        out_specs=(row(512), row(512), row(1024), row(1024)),
        scratch_shapes=[pltpu.VMEM((RET_HEADS, BLK, 256), F32)],
        sem=("arbitrary",), name="ret_bwd", exchanges=exchanges)


def _wgrad_in_half(ht, dproj, sidx, kept, exchanges=()):
    S = dproj.shape[0]
    tk = 2048
    half = (lambda sx: sx[4]) if kept else (lambda sx: 1 - sx[4])

    def body(a_ref, b_ref, o_ref):
        @pl.when(pl.program_id(1) == 0)
        def _():
            o_ref[...] = jnp.zeros_like(o_ref)

        o_ref[...] += _dot(a_ref[...], b_ref[...])

    (g,), xres = _carrier_call(
        body, (ht, dproj), out_shape=(SDS((D_MODEL // 2, PROJ_W), F32),), grid=(N_SHARD, S // tk),
        in_specs=[pl.BlockSpec((D_MODEL // 2, tk), lambda s, k, sx: (half(sx), k)),
                  pl.BlockSpec((tk, W_IN_S), lambda s, k, sx: (k, s))],
        out_specs=(pl.BlockSpec((D_MODEL // 2, W_IN_S), lambda s, k, sx: (0, s)),),
        sem=("parallel", "arbitrary"), name="wgrad_in_kept" if kept else "wgrad_in_sent", exchanges=exchanges,
        prefetch=sidx)
    return g, xres


def _in_proj_bwd(dproj, w_in, x, g1, dx1, exchanges=()):
    S = x.shape[0]
    tm = 512

    def body(d_ref, w_ref, x_ref, g_ref, dx1_ref, dx_ref, dgn_ref, acc):
        i, s = pl.program_id(0), pl.program_id(1)

        @pl.when(s == 0)
        def _():
            acc[...] = jnp.zeros_like(acc)

        @pl.when((i == 0) & (s == 0))
        def _():
            dgn_ref[...] = jnp.zeros_like(dgn_ref)

        acc[...] += _dot_nt(d_ref[...], w_ref[...])

        @pl.when(s == N_SHARD - 1)
        def _():
            xv = x_ref[...]
            r = lax.rsqrt(jnp.mean(xv * xv, axis=-1, keepdims=True) + NORM_EPS)
            xh = xv * r
            dh = acc[...]
            dgn_ref[...] += jnp.sum(dh * xh, axis=0, keepdims=True)
            dxh = dh * g_ref[...]
            dx_ref[...] = dx1_ref[...] + r * (dxh - xh * jnp.mean(dxh * xh, axis=-1, keepdims=True))

    row = pl.BlockSpec((tm, D_MODEL), lambda i, s: (i, 0))
    vec = pl.BlockSpec((1, D_MODEL), lambda i, s: (0, 0))
    (gx, dg), xres = _carrier_call(
        body, (dproj, w_in, x, g1, dx1),
        out_shape=(SDS((S, D_MODEL), F32), SDS((1, D_MODEL), F32)), grid=(S // tm, N_SHARD),
        in_specs=[pl.BlockSpec((tm, W_IN_S), lambda i, s: (i, s)),
                  pl.BlockSpec((D_MODEL, W_IN_S), lambda i, s: (0, s)), row, vec, row],
        out_specs=(row, vec), scratch_shapes=[pltpu.VMEM((tm, D_MODEL), F32)],
        sem=("arbitrary", "arbitrary"), name="in_proj_bwd", exchanges=exchanges)
    return gx, dg, xres


def _sub_view(a, d):
    S, W = a.shape
    return a.reshape(S // d, d * W)


def _step(x, tgt, g1, g2, g3, comm):
    S = x.shape[0]
    tab_np = _tables(S)
    tab = jnp.asarray(tab_np)
    consts = _ret_consts()

    h, ht = _rms_fwd(x, g1)
    w_in = comm.w_in()
    proj, xres = _in_proj(h, w_in, tab, comm.carry("in_proj"))
    comm.took("in_proj", xres)
    qkvs, o_parts, lse_parts = [], [], []
    for gi, d in enumerate(DILATIONS):
        qkv = proj if d == 1 else _qkv_to_sub(proj, d, gi)
        (o_g, lse_g), xres = _attn_fwd(qkv, d, gi, comm.carry(f"attn_fwd_g{gi}"))
        comm.took(f"attn_fwd_g{gi}", xres)
        qkvs.append(qkv)
        o_parts.append(o_g)
        lse_parts.append(lse_g)
    att, lse_tot = _attn_merge(o_parts, lse_parts)
    yrin, rn, rstd, states = _ret_fwd(proj, consts)
    wa, wr, wo, wg, wu, wd = comm.w_rest()
    merged, ya, yr = _branch_merge(att, yrin, proj, wa, wr)
    x1, h2 = _out_proj(merged, wo, x, g2)
    gte, up, act = _ffn_up(h2, wg, wu)
    dx2, dx2b, dg3, loss_p = _ffn_down_loss(act, wd, x1, g3, tgt)

    dgte, dup = _ffn_down_bwd(dx2b, wd, gte, up)
    tok3 = lambda w: (lambda tk: pl.BlockSpec((None, tk, w), lambda p, k: (p, k, 0)))
    tok2 = lambda w: (lambda tk: pl.BlockSpec((tk, w), lambda p, k: (k, 0)))
    g_d = _wgrad("wgrad_down", act, dx2b, tok3(HID_S), tok2(D_MODEL), (N_SHARD, HID_S, D_MODEL),
                 pl.BlockSpec((None, HID_S, D_MODEL), lambda p, k: (p, 0, 0)), N_SHARD, S)
    g_g = _wgrad("wgrad_gate", h2, dgte, tok2(D_MODEL), tok3(HID_S), (N_SHARD, D_MODEL, HID_S),
                 pl.BlockSpec((None, D_MODEL, HID_S), lambda p, k: (p, 0, 0)), N_SHARD, S)
    g_u = _wgrad("wgrad_up", h2, dup, tok2(D_MODEL), tok3(HID_S), (N_SHARD, D_MODEL, HID_S),
                 pl.BlockSpec((None, D_MODEL, HID_S), lambda p, k: (p, 0, 0)), N_SHARD, S)
    comm.grads({4: g_g, 5: g_u, 6: g_d})
    (dx1, dx1b, dg2), xres = _ffn_up_bwd(dgte, dup, wg, wu, x1, g2, dx2, comm.carry("ffn_up_bwd"))
    comm.took("ffn_up_bwd", xres)
    dya, dyr, dga, dgrr = _out_proj_bwd(dx1b, wo, proj, ya, yr)
    colblk = lambda w: (lambda tk: pl.BlockSpec((tk, w), lambda p, k: (k, p)))
    g_o = _wgrad("wgrad_out", merged, dx1b, colblk(256), tok2(D_MODEL), (D_MODEL, D_MODEL),
                 pl.BlockSpec((256, D_MODEL), lambda p, k: (p, 0)), 4, S)
    datt, rho, dyrin = _branch_bwd(dya, dyr, wa, wr, att)
    g_a = _wgrad("wgrad_attn", att, dya, tok2(512), colblk(512), (512, D_MODEL),
                 pl.BlockSpec((512, 512), lambda p, k: (0, p)), 2, S)
    g_r = _wgrad("wgrad_ret", yrin, dyr, colblk(256), tok2(D_MODEL), (D_MODEL, D_MODEL),
                 pl.BlockSpec((256, D_MODEL), lambda p, k: (p, 0)), 4, S)
    comm.grads({1: g_a, 2: g_r.reshape(N_SHARD, 256, D_MODEL), 3: g_o.reshape(N_SHARD, 256, D_MODEL)})
    (dqr, dkr, dvr, dgr), xres = _ret_bwd(proj, rn, rstd, dyrin, states, tab, consts, comm.carry("ret_bwd"))
    comm.took("ret_bwd", xres)
    dqs, dks, dvs = [], [], []
    for gi, d in enumerate(DILATIONS):
        rtab = jnp.asarray(tab_np[0].reshape(3, S // d, d * 128))
        (dq, dk, dv), xres = _attn_bwd(qkvs[gi], _sub_view(datt, d), _sub_view(lse_tot, d), _sub_view(rho, d), rtab, d, gi,
                                       comm.carry(f"attn_bwd_g{gi}"))
        comm.took(f"attn_bwd_g{gi}", xres)
        dqs.append(dq)
        dks.append(dk)
        dvs.append(dv)
    dproj = _assemble_dproj((dqs, dks, dvs), dqr, dkr, dvr, dgr, dga, dgrr)
    g_sent, xres = _wgrad_in_half(ht, dproj, comm.sidx, False, comm.carry("wgrad_in_sent"))
    comm.took("wgrad_in_sent", xres)
    comm.grads({"in_sent": g_sent})
    g_kept, xres = _wgrad_in_half(ht, dproj, comm.sidx, True, comm.carry("wgrad_in_kept"))
    comm.grads({"in_kept": g_kept})
    comm.took("wgrad_in_kept", xres)
    grad_x, dg1, xres = _in_proj_bwd(dproj, w_in, x, g1, dx1, comm.carry("in_proj_bwd"))
    comm.took("in_proj_bwd", xres)
    return loss_p, grad_x, (dg1, dg2, dg3)


W_KINDS = ("col", "col", "lead", "lead", "lead", "lead", "lead")
W_SHARD = ((1024, W_IN_S), (512, 256), (256, 1024), (256, 1024), (1024, HID_S), (1024, HID_S), (HID_S, 1024))
N_W = len(W_KINDS)


def _full_shape(wi):
    R, C = W_SHARD[wi]
    return (R, N_SHARD * C) if W_KINDS[wi] == "col" else (N_SHARD, R, C)


def _view(ref, wi, s, half):
    R, C = W_SHARD[wi]
    rows = pl.ds(half * (R // 2), R // 2)
    if W_KINDS[wi] == "col":
        return ref.at[rows, pl.ds(pl.multiple_of(s * C, 128), C)]
    return ref.at[s, rows, :]


def _mesh_pos():
    x, y, c = lax.axis_index("x"), lax.axis_index("y"), lax.axis_index("c")
    chips = [(1 - x, y), (x, 1 - y), (1 - x, 1 - y)]
    return x, y, c, chips


def _cast_bf16(a):
    R, C = a.shape
    tr = R // 2 if R % 32 == 0 else R

    def body(a_ref, o_ref):
        o_ref[...] = a_ref[...].astype(BF16)

    spec = pl.BlockSpec((tr, C), lambda i: (i, 0))
    return pl.pallas_call(body, out_shape=SDS((R, C), BF16), grid=(R // tr,), in_specs=[spec], out_specs=spec,
                          compiler_params=_cparams("parallel"), name=f"cast_{R}x{C}")(a)


def _remote(send, recv, k, src, dst, to):
    return pltpu.make_async_remote_copy(src_ref=src, dst_ref=dst, send_sem=send.at[k], recv_sem=recv.at[k],
                                        device_id=to, device_id_type=MESH)


def _gather_now(wis, shards):
    n = len(wis)

    def body(*refs):
        sh, full = refs[:n], refs[n:2 * n]
        send, recv, loc = refs[2 * n:]
        x, y, c, _ = _mesh_pos()
        s_me, sib = 2 * x + y, (x, y, 1 - c)
        xn, yn = (1 - x, y), (x, 1 - y)
        flip = lambda a, b: a + b - 2 * a * b
        via = (flip(x, 1 - c), flip(y, c))
        onto = (flip(x, c), flip(y, 1 - c))
        shard_of = lambda chip: 2 * chip[0] + chip[1]
        own, started = [], []
        for i, wi in enumerate(wis):
            Rh = W_SHARD[wi][0] // 2
            for hf in range(2):
                cp = pltpu.make_async_copy(sh[i].at[pl.ds(hf * Rh, Rh), :], _view(full[i], wi, s_me, hf), loc.at[2 * i + hf])
                cp.start()
                own.append(cp)
            for j, chip in enumerate((xn, yn)):
                cp = _remote(send, recv, 6 * i + j, sh[i].at[pl.ds(c * Rh, Rh), :], _view(full[i], wi, s_me, c), (*chip, c))
                cp.start()
                started.append(cp)

        def pass_to_sibling(i, wi, k, s):
            mine = _view(full[i], wi, s, c)
            fw = _remote(send, recv, 6 * i + k, mine, mine, sib)
            fw.start()
            started.append(fw)

        for i, wi in enumerate(wis):
            for j, chip in enumerate((xn, yn)):
                land = _view(full[i], wi, shard_of(chip), c)
                _remote(send, recv, 6 * i + j, land, land, (*chip, c)).wait_recv()
                pass_to_sibling(i, wi, 3 + j, shard_of(chip))
            relay = _view(full[i], wi, shard_of(via), c)
            fw = _remote(send, recv, 6 * i + 2, relay, relay, (*onto, c))
            fw.start()
            started.append(fw)
        s_diag = 2 * (1 - x) + (1 - y)
        for i, wi in enumerate(wis):
            land = _view(full[i], wi, s_diag, c)
            _remote(send, recv, 6 * i + 2, land, land, (*onto, c)).wait_recv()
            pass_to_sibling(i, wi, 5, s_diag)
        for i, wi in enumerate(wis):
            for k, s in ((3, shard_of(xn)), (4, shard_of(yn)), (5, s_diag)):
                land = _view(full[i], wi, s, 1 - c)
                _remote(send, recv, 6 * i + k, land, land, sib).wait_recv()
        for cp in started:
            cp.wait_send()
        for cp in own:
            cp.wait()

    return pl.pallas_call(
        body, out_shape=tuple(SDS(_full_shape(wi), BF16) for wi in wis),
        in_specs=[ANY] * n, out_specs=tuple([ANY] * n),
        scratch_shapes=[pltpu.SemaphoreType.DMA((6 * n,)), pltpu.SemaphoreType.DMA((6 * n,)),
                        pltpu.SemaphoreType.DMA((2 * n,))],
        name="gather_now")(*shards)


def _ex_gather_ici(wis, shards):
    def build(ins, outs, send, recv, loc):
        x, y, c, chips = _mesh_pos()
        s_me = 2 * x + y
        starts, waits = [], []
        for i, wi in enumerate(wis):
            Rh = W_SHARD[wi][0] // 2
            for hf in range(2):
                cp = pltpu.make_async_copy(ins[i].at[pl.ds(hf * Rh, Rh), :], _view(outs[i], wi, s_me, hf), loc.at[2 * i + hf])
                starts.append(cp)
                waits.append(cp.wait)
            for j, chip in enumerate(chips):
                cp = _remote(send, recv, 3 * i + j, ins[i].at[pl.ds(c * Rh, Rh), :], _view(outs[i], wi, s_me, c), (*chip, c))
                land = _view(outs[i], wi, 2 * chip[0] + chip[1], c)
                starts.append(cp)
                waits += [cp.wait_send, _remote(send, recv, 3 * i + j, land, land, (*chip, c)).wait_recv]
        return starts, waits

    return _Exchange(shards, [SDS(_full_shape(wi), BF16) for wi in wis], {}, 3 * len(wis), 2 * len(wis), build)


def _ex_gather_d2d(wis, fulls):
    def build(ins, outs, send, recv, loc):
        x, y, c, chips = _mesh_pos()
        sib = (x, y, 1 - c)
        starts, waits = [], []
        for i, wi in enumerate(wis):
            for j, chip in enumerate(chips):
                mine = _view(outs[i], wi, 2 * chip[0] + chip[1], c)
                theirs = _view(outs[i], wi, 2 * chip[0] + chip[1], 1 - c)
                cp = _remote(send, recv, 3 * i + j, mine, mine, sib)
                starts.append(cp)
                waits += [cp.wait_send, _remote(send, recv, 3 * i + j, theirs, theirs, sib).wait_recv]
        return starts, waits

    return _Exchange(fulls, [SDS(f.shape, BF16) for f in fulls], {i: i for i in range(len(wis))}, 3 * len(wis), 0, build)


def _half_shape(wi):
    R, C = W_SHARD[wi]
    return (R // 2, N_SHARD * C) if W_KINDS[wi] == "col" else (N_SHARD, R // 2, C)


def _ex_pair(wis, grads):
    def build(ins, outs, send, recv, loc):
        x, y, c, _ = _mesh_pos()
        starts, waits = [], []
        for i, wi in enumerate(wis):
            Rh = W_SHARD[wi][0] // 2
            rows = pl.ds((1 - c) * Rh, Rh)
            if tuple(ins[i].shape) == _half_shape(wi):
                src = ins[i]
            else:
                src = ins[i].at[rows, :] if W_KINDS[wi] == "col" else ins[i].at[:, rows, :]
            cp = _remote(send, recv, i, src, outs[i], (x, y, 1 - c))
            starts.append(cp)
            waits.append(cp.wait)
        return starts, waits

    return _Exchange(grads, [SDS(_half_shape(wi), F32) for wi in wis], {}, len(wis), 0, build)


def _ex_chip(wis, pbs):
    def build(ins, outs, send, recv, loc):
        x, y, c, chips = _mesh_pos()
        starts, waits = [], []
        for i, wi in enumerate(wis):
            for j, chip in enumerate(chips):
                cp = _remote(send, recv, 3 * i + j, ins[i].at[j], outs[i].at[j], (*chip, c))
                starts.append(cp)
                waits.append(cp.wait)
        return starts, waits

    shapes = [SDS((3, W_SHARD[wi][0] // 2, W_SHARD[wi][1]), BF16) for wi in wis]
    return _Exchange(pbs, shapes, {}, 3 * len(wis), 0, build)


def _ex_share(wis, halves):
    def build(ins, outs, send, recv, loc):
        x, y, c, _ = _mesh_pos()
        sib = (x, y, 1 - c)
        starts, waits = [], []
        for i, wi in enumerate(wis):
            cp = _remote(send, recv, i, outs[i].at[c], outs[i].at[c], sib)
            starts.append(cp)
            waits += [cp.wait_send, _remote(send, recv, i, outs[i].at[1 - c], outs[i].at[1 - c], sib).wait_recv]
        return starts, waits

    return _Exchange(halves, [SDS(h.shape, F32) for h in halves], {i: i for i in range(len(wis))}, len(wis), 0, build)


def _row_tile(rh, C):
    best = 16
    for t in range(16, rh + 1, 16):
        if rh % t == 0 and t * C * 4 <= (3 << 19):
            best = t
    return best


def _pair_sum(wi, g, ra, sidx):
    R, C = W_SHARD[wi]
    Rh = R // 2
    tr = _row_tile(Rh, C)
    nt = Rh // tr
    off = 0 if tuple(g.shape) == _half_shape(wi) else nt
    col = W_KINDS[wi] == "col"

    def body(sidx_ref, *refs):
        gs, rs = refs[:4], refs[4:8]
        own_ref, pb_ref = refs[8:]
        own_ref[...] = gs[0][...] + rs[0][...]
        for j in range(3):
            pb_ref[j] = (gs[1 + j][...] + rs[1 + j][...]).astype(BF16)

    def gspec(slot):
        if col:
            return pl.BlockSpec((tr, C), lambda i, sx: (sx[4] * off + i, sx[slot]))
        return pl.BlockSpec((None, tr, C), lambda i, sx: (sx[slot], sx[4] * off + i, 0))

    def rspec(slot):
        if col:
            return pl.BlockSpec((tr, C), lambda i, sx: (i, sx[slot]))
        return pl.BlockSpec((None, tr, C), lambda i, sx: (sx[slot], i, 0))

    return pl.pallas_call(
        body, out_shape=(SDS((Rh, C), F32), SDS((3, Rh, C), BF16)),
        grid_spec=pltpu.PrefetchScalarGridSpec(
            num_scalar_prefetch=1, grid=(nt,),
            in_specs=[gspec(k) for k in range(4)] + [rspec(k) for k in range(4)],
            out_specs=(pl.BlockSpec((tr, C), lambda i, sx: (i, 0)), pl.BlockSpec((3, tr, C), lambda i, sx: (0, i, 0)))),
        compiler_params=_cparams("arbitrary"), name=f"pair_sum_w{wi}")(sidx, g, g, g, g, ra, ra, ra, ra)


def _chip_sum(wi, own, rb, sidx):
    R, C = W_SHARD[wi]
    Rh = R // 2
    tr = _row_tile(Rh, C)

    def body(sidx_ref, own_ref, rb_ref, o_ref):
        o_ref[...] = ((own_ref[...] + rb_ref[0].astype(F32)) + rb_ref[1].astype(F32)) + rb_ref[2].astype(F32)

    return pl.pallas_call(
        body, out_shape=SDS((2, Rh, C), F32),
        grid_spec=pltpu.PrefetchScalarGridSpec(
            num_scalar_prefetch=1, grid=(Rh // tr,),
            in_specs=[pl.BlockSpec((tr, C), lambda i, sx: (i, 0)), pl.BlockSpec((3, tr, C), lambda i, sx: (0, i, 0))],
            out_specs=pl.BlockSpec((None, tr, C), lambda i, sx: (sx[4], i, 0))),
        compiler_params=_cparams("arbitrary"), name=f"chip_sum_w{wi}")(sidx, own, rb)


def _gain_allgather(blk):
    m_per, n = blk.shape

    def body(x_ref, out_ref, send_sems, recv_sems, local_sem):
        x, y, c, chips = _mesh_pos()
        me, sibling = (x, y, c), (x, y, 1 - c)

        def rows(px, py, pc):
            return out_ref.at[pl.ds((4 * px + 2 * py + pc) * m_per, m_per), :]

        def copy(k, block, to, src=None):
            return pltpu.make_async_remote_copy(
                src_ref=rows(*block) if src is None else src, dst_ref=rows(*block),
                send_sem=send_sems.at[k], recv_sem=recv_sems.at[k], device_id=to, device_id_type=MESH)

        mine = pltpu.make_async_copy(x_ref, rows(*me), local_sem)
        mine.start()
        first = [copy(0, me, sibling, src=x_ref)]
        first += [copy(1 + j, me, (*chip, c), src=x_ref) for j, chip in enumerate(chips)]
        for cp in first:
            cp.start()
        passed = [copy(4 + j, (*chip, c), sibling) for j, chip in enumerate(chips)]
        for j, chip in enumerate(chips):
            copy(1 + j, (*chip, c), me).wait_recv()
            passed[j].start()
        copy(0, sibling, me).wait_recv()
        for j, chip in enumerate(chips):
            copy(4 + j, (*chip, 1 - c), me).wait_recv()
        for cp in first + passed:
            cp.wait_send()
        mine.wait()

    vm = pl.BlockSpec(memory_space=pltpu.VMEM)
    return pl.pallas_call(
        body, out_shape=SDS((8 * m_per, n), blk.dtype), in_specs=[vm], out_specs=vm,
        scratch_shapes=[pltpu.SemaphoreType.DMA((7,)), pltpu.SemaphoreType.DMA((7,)), pltpu.SemaphoreType.DMA],
        name="gain_allgather")(blk)


def _adam_math(w, g, m, v):
    mn = ADAM_B1 * m + (1.0 - ADAM_B1) * g
    vn = ADAM_B2 * v + (1.0 - ADAM_B2) * (g * g)
    mh = mn / (1.0 - ADAM_B1 ** ADAM_STEP)
    vh = vn / (1.0 - ADAM_B2 ** ADAM_STEP)
    return -ADAM_LR * (mh / (jnp.sqrt(vh) + ADAM_EPS) + ADAM_WD * w), mn, vn


def _adamw(wi, w, g, m, v):
    R, C = w.shape
    tr = _row_tile(R, C)

    def body(w_ref, g_ref, m_ref, v_ref, d_ref, mn_ref, vn_ref):
        d_ref[...], mn_ref[...], vn_ref[...] = _adam_math(w_ref[...], g_ref[...], m_ref[...], v_ref[...])

    spec = pl.BlockSpec((tr, C), lambda i: (i, 0))
    return pl.pallas_call(body, out_shape=(SDS((R, C), F32),) * 3, grid=(R // tr,), in_specs=[spec] * 4,
                          out_specs=(spec,) * 3, compiler_params=_cparams("parallel"), name=f"adamw_w{wi}")(w, g, m, v)


def _gain_update(gathered, w, m, v):
    def body(ga_ref, w_ref, m_ref, v_ref, g_ref, d_ref, mn_ref, vn_ref):
        g = ga_ref[0:8, :]
        for dev in range(1, 8):
            g = g + ga_ref[8 * dev:8 * dev + 8, :]
        g_ref[...] = g
        d_ref[...], mn_ref[...], vn_ref[...] = _adam_math(w_ref[...], g, m_ref[...], v_ref[...])

    return pl.pallas_call(body, out_shape=(SDS((8, 1024), F32),) * 4, name="gain_update")(gathered, w, m, v)


GROUP_FFN, GROUP_MIX, GROUP_IN = (4, 5, 6), (1, 2, 3), (0,)
REST = GROUP_MIX + GROUP_FFN


class _MeshComm:
    SCHEDULE = {
        "in_proj": [("ici", (1, 2, 3, 4))],
        "attn_fwd_g0": [("d2d", (1, 2, 3, 4)), ("ici", (5,))],
        "attn_fwd_g1": [("d2d", (5,)), ("ici", (6,))],
        "attn_fwd_g2": [("d2d", (6,))],
        "ffn_up_bwd": [("pair", GROUP_FFN)],
        "ret_bwd": [("pair", GROUP_MIX)],
        "attn_bwd_g0": [("chip", (4,))],
        "attn_bwd_g1": [("chip", (5, 1, 2, 3))],
        "attn_bwd_g2": [("chip", (6,))],
        "wgrad_in_sent": [("share", GROUP_FFN + GROUP_MIX)],
        "wgrad_in_kept": [("pair", GROUP_IN)],
        "in_proj_bwd": [("chip", GROUP_IN)],
    }

    def __init__(self, shards):
        xi, yi, ci = lax.axis_index("x"), lax.axis_index("y"), lax.axis_index("c")
        self.sidx = jnp.stack([2 * xi + yi, 2 * (1 - xi) + yi, 2 * xi + (1 - yi), 2 * (1 - xi) + (1 - yi), ci]).astype(jnp.int32)
        self.shards, self.full = shards, {}
        self.g, self.own, self.pb, self.half, self.red = {}, {}, {}, {}, {}

    def w_in(self):
        return _gather_now(GROUP_IN, [self.shards[0]])[0]

    def w_rest(self):
        f = self.full
        return f[1], f[2].reshape(D_MODEL, D_MODEL), f[3].reshape(D_MODEL, D_MODEL), f[4], f[5], f[6]

    def grads(self, by_wi):
        self.g.update(by_wi)

    def _exchange(self, stage, wis):
        pick = lambda table: [table[wi] for wi in wis]
        if stage == "ici":
            return _ex_gather_ici(wis, pick(self.shards))
        if stage == "d2d":
            return _ex_gather_d2d(wis, pick(self.full))
        if stage == "pair":
            return _ex_pair(wis, [self.g["in_sent"] if wi == 0 else self.g[wi] for wi in wis])
        if stage == "chip":
            return _ex_chip(wis, pick(self.pb))
        return _ex_share(wis, pick(self.half))

    def _landed(self, stage, wis, res):
        for wi, r in zip(wis, res):
            if stage in ("ici", "d2d"):
                self.full[wi] = r
            elif stage == "pair":
                self.own[wi], self.pb[wi] = _pair_sum(wi, self.g["in_kept"] if wi == 0 else self.g[wi], r, self.sidx)
            elif stage == "chip":
                self.half[wi] = _chip_sum(wi, self.own[wi], r, self.sidx)
            else:
                self.red[wi] = r

    def carry(self, point):
        return [self._exchange(stage, wis) for stage, wis in self.SCHEDULE.get(point, ())]

    def took(self, point, xres):
        for (stage, wis), res in zip(self.SCHEDULE.get(point, ()), xres):
            self._landed(stage, wis, res)

    def reduced(self):
        self._landed("share", GROUP_IN, _exchange_call(self._exchange("share", GROUP_IN), "share_w_in"))
        return [self.red[wi] for wi in range(N_W)]


def kernel(x, norm_mix_g, w_in, w_out_attn, w_out_ret, w_out, norm_ffn_g, w_ffn_gate, w_ffn_up, w_ffn_down, norm_final_g, loss_target, m_norm_mix_g, m_w_in, m_w_out_attn, m_w_out_ret, m_w_out, m_norm_ffn_g, m_w_ffn_gate, m_w_ffn_up, m_w_ffn_down, m_norm_final_g, v_norm_mix_g, v_w_in, v_w_out_attn, v_w_out_ret, v_w_out, v_norm_ffn_g, v_w_ffn_gate, v_w_ffn_up, v_w_ffn_down, v_norm_final_g):
    ws = (w_in, w_out_attn, w_out_ret, w_out, w_ffn_gate, w_ffn_up, w_ffn_down)
    ms = (m_w_in, m_w_out_attn, m_w_out_ret, m_w_out, m_w_ffn_gate, m_w_ffn_up, m_w_ffn_down)
    vs = (v_w_in, v_w_out_attn, v_w_out_ret, v_w_out, v_w_ffn_gate, v_w_ffn_up, v_w_ffn_down)
    shard2d = lambda a, wi: a.reshape(W_SHARD[wi])

    comm = _MeshComm([_cast_bf16(shard2d(w, wi)) for wi, w in enumerate(ws)])
    g3 = norm_final_g.reshape(1, D_MODEL)
    loss_p, grad_x, gain_g = _step(x[0], loss_target[0], norm_mix_g, norm_ffn_g, g3, comm)
    gred = comm.reduced()

    outs_g, outs_d, outs_m, outs_v = [], [], [], []
    for wi in range(N_W):
        g2d = gred[wi].reshape(W_SHARD[wi])
        dlt, mn, vn = _adamw(wi, shard2d(ws[wi], wi), g2d, shard2d(ms[wi], wi), shard2d(vs[wi], wi))
        for lst, a in ((outs_g, g2d), (outs_d, dlt), (outs_m, mn), (outs_v, vn)):
            lst.append(a.reshape(ws[wi].shape))

    pad8 = lambda rows: jnp.concatenate([r.reshape(1, D_MODEL) for r in rows]
                                        + [jnp.zeros((8 - len(rows), D_MODEL), F32)], axis=0)
    gathered = _gain_allgather(pad8((*gain_g, jnp.tile(loss_p[0:1], (1, D_MODEL // 128)))))
    gg, gd, gm, gv = _gain_update(gathered, pad8((norm_mix_g, norm_ffn_g, norm_final_g)),
                                  pad8((m_norm_mix_g, m_norm_ffn_g, m_norm_final_g)),
                                  pad8((v_norm_mix_g, v_norm_ffn_g, v_norm_final_g)))
    loss = gg[3, 0]

    def assemble(gain_rows, wlist):
        return (gain_rows[0:1], wlist[0], wlist[1], wlist[2], wlist[3], gain_rows[1:2],
                wlist[4], wlist[5], wlist[6], gain_rows[2])

    return (loss, grad_x[None], *assemble(gg, outs_g), *assemble(gd, outs_d), *assemble(gm, outs_m), *assemble(gv, outs_v))
```

```python
import functools
import math

import numpy as np
import jax
import jax.numpy as jnp
from jax import lax
from jax.experimental import pallas as pl
from jax.experimental.pallas import tpu as pltpu

F32, BF16 = jnp.float32, jnp.bfloat16
SDS = jax.ShapeDtypeStruct
MESH = pl.DeviceIdType.MESH

D_MODEL = 1024
PROJ_W = 9728
COLB = 512
N_COLB = PROJ_W // COLB
QA_B, KA_B, VA_B = 0, 3, 6
QR_B, KR_B = 9, 10
FFN_HID = 2816
N_SHARD = 4
HID_S = FFN_HID // N_SHARD
W_IN_S = PROJ_W // N_SHARD
DILATIONS = (1, 4, 16)
BLK = 128
RET_HEADS = 4
ROPE_THETA = 10000.0
NORM_EPS = 1e-6
ADAM_LR, ADAM_B1, ADAM_B2, ADAM_EPS, ADAM_WD, ADAM_STEP = 0.001, 0.9, 0.999, 1e-08, 0.01, 10
VMEM_LIMIT = 56 << 20


def _cparams(*sem):
    return pltpu.CompilerParams(dimension_semantics=sem or None, vmem_limit_bytes=VMEM_LIMIT)


def _dot(a, b):
    return jnp.dot(a, b, preferred_element_type=F32)


def _dot_nt(a, b):
    return lax.dot_general(a, b, (((1,), (1,)), ((), ())), preferred_element_type=F32)


def _dot_tn(a, b):
    return lax.dot_general(a, b, (((0,), (0,)), ((), ())), preferred_element_type=F32)


def _row_pieces(tm, sub=512):
    return [slice(i, i + sub) for i in range(0, tm, sub)]


def _sigmoid(z):
    return 0.5 * jnp.tanh(0.5 * z) + 0.5


ANY = pl.BlockSpec(memory_space=pl.ANY)


class _Exchange:
    def __init__(self, ins, out_shapes, aliases, n_sem, n_loc, build):
        self.ins, self.out_shapes, self.aliases = list(ins), list(out_shapes), dict(aliases)
        self.n_sem, self.n_loc, self.build = n_sem, n_loc, build

    def sems(self):
        return [pltpu.SemaphoreType.DMA((self.n_sem,)), pltpu.SemaphoreType.DMA((self.n_sem,)),
                pltpu.SemaphoreType.DMA((max(self.n_loc, 1),))]


def _exchange_call(ex, name):
    n_in, n_out = len(ex.ins), len(ex.out_shapes)

    def body(*refs):
        starts, waits = ex.build(refs[:n_in], refs[n_in:n_in + n_out], *refs[n_in + n_out:])
        for cp in starts:
            cp.start()
        for w in waits:
            w()

    return pl.pallas_call(body, out_shape=tuple(ex.out_shapes), in_specs=[ANY] * n_in, out_specs=tuple([ANY] * n_out),
                          input_output_aliases=ex.aliases, scratch_shapes=ex.sems(), name=name)(*ex.ins)


def _carrier_call(body, args, *, out_shape, grid, in_specs, out_specs, scratch_shapes=(), sem, name, exchanges=(),
                  prefetch=None):
    out_shape, out_specs = tuple(out_shape), tuple(out_specs)
    n_in, n_out, n_scr = len(args), len(out_shape), len(scratch_shapes)
    n_pre = 0 if prefetch is None else 1
    x_args, x_outs, aliases, x_scr, spans = [], [], {}, [], []
    for ex in exchanges:
        i0, o0 = len(x_args), len(x_outs)
        for a, o in ex.aliases.items():
            aliases[n_pre + n_in + i0 + a] = n_out + o0 + o
        x_args += ex.ins
        x_outs += ex.out_shapes
        x_scr += ex.sems()
        spans.append((i0, len(ex.ins), o0, len(ex.out_shapes)))
    nx_in, nx_out = len(x_args), len(x_outs)

    def wrapped(*refs):
        refs = refs[n_pre:]
        ins, xin = refs[:n_in], refs[n_in:n_in + nx_in]
        o_base = n_in + nx_in
        outs, xout = refs[o_base:o_base + n_out], refs[o_base + n_out:o_base + n_out + nx_out]
        s_base = o_base + n_out + nx_out
        scr, xs = refs[s_base:s_base + n_scr], refs[s_base + n_scr:]

        def built(e):
            i0, ni, o0, no = spans[e]
            return exchanges[e].build(xin[i0:i0 + ni], xout[o0:o0 + no], *xs[3 * e:3 * e + 3])

        if exchanges:
            first = functools.reduce(jnp.logical_and, [pl.program_id(k) == 0 for k in range(len(grid))])
            last = functools.reduce(jnp.logical_and, [pl.program_id(k) == grid[k] - 1 for k in range(len(grid))])

            @pl.when(first)
            def _():
                for e in range(len(exchanges)):
                    for cp in built(e)[0]:
                        cp.start()

        body(*ins, *outs, *scr)

        if exchanges:
            @pl.when(last)
            def _():
                for e in range(len(exchanges)):
                    for w in built(e)[1]:
                        w()

    all_in, all_out = list(in_specs) + [ANY] * nx_in, out_specs + tuple([ANY] * nx_out)
    all_scr = list(scratch_shapes) + x_scr
    cparams = _cparams(*(sem if not exchanges else ("arbitrary",) * len(grid)))
    if prefetch is None:
        res = pl.pallas_call(wrapped, out_shape=out_shape + tuple(x_outs), grid=grid, in_specs=all_in, out_specs=all_out,
                             scratch_shapes=all_scr, input_output_aliases=aliases, compiler_params=cparams,
                             name=name)(*args, *x_args)
    else:
        gs = pltpu.PrefetchScalarGridSpec(num_scalar_prefetch=1, grid=grid, in_specs=all_in, out_specs=all_out,
                                          scratch_shapes=all_scr)
        res = pl.pallas_call(wrapped, out_shape=out_shape + tuple(x_outs), grid_spec=gs, input_output_aliases=aliases,
                             compiler_params=cparams, name=name)(prefetch, *args, *x_args)
    xres = [tuple(res[n_out + o0:n_out + o0 + no]) for (_, _, o0, no) in spans]
    return tuple(res[:n_out]), xres


def _tables(S):
    f32 = np.float32
    pos = np.arange(S, dtype=f32)
    lane = np.arange(128)
    inv = (f32(ROPE_THETA) ** (-np.arange(0, 64, 2, dtype=f32) / f32(64))).astype(f32)
    ang = (pos[:, None] * inv[None, :]).astype(np.float64)
    idx = (lane % 64) % 32
    c, s = np.cos(ang)[:, idx], np.sin(ang)[:, idx]
    first = ((lane % 64) < 32)[None, :]
    rope = np.stack([c, np.where(first, 0.0, s), np.where(first, -s, 0.0)])
    base = (f32(1.0) / (f32(ROPE_THETA) ** np.linspace(0.0, 1.0, 64, dtype=f32))).astype(f32)
    ang2 = (pos[:, None] * base[None, :]).astype(np.float64)
    c2, s2 = np.cos(ang2)[:, lane // 2], np.sin(ang2)[:, lane // 2]
    even = (lane % 2 == 0)[None, :]
    th = np.stack([c2, np.where(even, 0.0, s2), np.where(even, -s2, 0.0)])
    return np.stack([rope, th, th * (128 ** -0.5)]).astype(f32)


def _rot(a, c, sa, sb, shift):
    return a * c + pltpu.roll(a, shift, 1) * sa + pltpu.roll(a, 128 - shift, 1) * sb


def _unrot(g, c, sa, sb, shift):
    return g * c + pltpu.roll(g * sa, 128 - shift, 1) + pltpu.roll(g * sb, shift, 1)


def _ret_consts():
    h = np.arange(RET_HEADS, dtype=np.float64)
    log_g = np.log1p(-(2.0 ** (-5.0 - h)))
    idx = np.arange(BLK, dtype=np.float64)
    diff = idx[:, None] - idx[None, :]
    dmask = np.where(diff[None] >= 0, np.exp(np.maximum(diff, 0.0)[None] * log_g[:, None, None]), 0.0)
    zeta = np.exp((BLK - 1 - idx)[None, :] * log_g[:, None])
    xi = np.exp((idx + 1.0)[None, :] * log_g[:, None])
    dec = np.exp(BLK * log_g)
    rep = lambda v: np.broadcast_to(v[:, :, None], (RET_HEADS, BLK, 128))
    return (jnp.asarray(dmask, F32), jnp.asarray(rep(zeta), F32), jnp.asarray(rep(xi), F32),
            jnp.asarray(np.broadcast_to(dec[:, None, None], (RET_HEADS, 8, 256)), F32))


def _rms_fwd(x, g):
    S = x.shape[0]
    tm = 512

    def body(x_ref, g_ref, h_ref, ht_ref):
        xv = x_ref[...]
        r = lax.rsqrt(jnp.mean(xv * xv, axis=-1, keepdims=True) + NORM_EPS)
        h = xv * r * g_ref[...]
        h_ref[...] = h.astype(BF16)
        ht_ref[...] = h.T.astype(BF16)

    return pl.pallas_call(
        body, out_shape=(SDS((S, D_MODEL), BF16), SDS((D_MODEL, S), BF16)), grid=(S // tm,),
        in_specs=[pl.BlockSpec((tm, D_MODEL), lambda i: (i, 0)), pl.BlockSpec((1, D_MODEL), lambda i: (0, 0))],
        out_specs=(pl.BlockSpec((tm, D_MODEL), lambda i: (i, 0)), pl.BlockSpec((D_MODEL, tm), lambda i: (0, i))),
        compiler_params=_cparams("parallel"), name="rms_fwd")(x, g)


def _in_proj(h, w_in, tab, exchanges=()):
    S = h.shape[0]
    tm = min(S, 2048)

    def body(h_ref, w_ref, t_ref, o_ref):
        j = pl.program_id(1)
        is_rope = j < 6
        is_theta = (j == QR_B) | (j == KR_B)
        sub = 512

        def rotated(shift):
            for i in range(tm // sub):
                rows = slice(i * sub, (i + 1) * sub)
                acc = _dot(h_ref[rows, :], w_ref[...])
                c, sa, sb = t_ref[0, 0, rows, :], t_ref[0, 1, rows, :], t_ref[0, 2, rows, :]
                for k in range(COLB // 128):
                    sl = slice(k * 128, (k + 1) * 128)
                    o_ref[rows, sl] = _rot(acc[:, sl], c, sa, sb, shift).astype(BF16)

        @pl.when(is_rope)
        def _():
            rotated(32)

        @pl.when(is_theta)
        def _():
            rotated(1)

        @pl.when(jnp.logical_not(is_rope | is_theta))
        def _():
            o_ref[...] = _dot(h_ref[...], w_ref[...]).astype(BF16)

    def tab_map(i, j):
        return (jnp.where(j == QR_B, 1, jnp.where(j == KR_B, 2, 0)), 0, i, 0)

    (proj,), xres = _carrier_call(
        body, (h, w_in, tab), out_shape=(SDS((S, PROJ_W), BF16),), grid=(S // tm, N_COLB),
        in_specs=[pl.BlockSpec((tm, D_MODEL), lambda i, j: (i, 0)),
                  pl.BlockSpec((D_MODEL, COLB), lambda i, j: (0, j)),
                  pl.BlockSpec((1, 3, tm, 128), tab_map)],
        out_specs=(pl.BlockSpec((tm, COLB), lambda i, j: (i, j)),),
        sem=("parallel", "arbitrary"), name="in_proj", exchanges=exchanges)
    return proj, xres


def _band_mask(n):
    qi = lax.broadcasted_iota(jnp.int32, (BLK, 2 * BLK), 0)
    kj = lax.broadcasted_iota(jnp.int32, (BLK, 2 * BLK), 1)
    dist = BLK + qi - kj
    return (dist >= 0) & (dist <= BLK) & ((kj >= BLK) | (n > 0))


def _qkv_col(d, gi):
    if d == 1:
        return lambda t, r: 3 * t + gi
    return lambda t, r: 3 * r + t


def _attn_fwd(qkv, d, gi, exchanges=()):
    L = qkv.shape[0]
    nb = L // BLK

    def body(q_ref, kc_ref, kp_ref, vc_ref, vp_ref, o_ref, lse_ref):
        n = pl.program_id(1)
        mask = _band_mask(n)
        mask2 = jnp.concatenate([mask, mask], axis=0)
        lane = lax.broadcasted_iota(jnp.int32, (BLK, 128), 1)
        lo = lane < 64
        lse_all = jnp.zeros((BLK, 128), F32)
        chunks = [slice(c * 128, (c + 1) * 128) for c in range(4)]
        scores, vals = [], []
        for sl in chunks:
            q = q_ref[:, sl]
            k = jnp.concatenate([kp_ref[:, sl], kc_ref[:, sl]], axis=0)
            vals.append(jnp.concatenate([vp_ref[:, sl], vc_ref[:, sl]], axis=0))
            q2 = jnp.concatenate([jnp.where(lo, q, jnp.zeros_like(q)), jnp.where(lo, jnp.zeros_like(q), q)], axis=0)
            scores.append(_dot_nt(q2, k))
        probs = []
        for c, s in enumerate(scores):
            s = jnp.where(mask2, s * 0.125, jnp.float32(-1e30))
            m = jnp.max(s, axis=-1, keepdims=True)
            p = jnp.exp(s - m)
            l = jnp.sum(p, axis=-1, keepdims=True)
            probs.append((p / l).astype(BF16))
            lse = m + jnp.log(l)
            lse_all = jnp.where(lane // 16 == 2 * c, lse[:BLK], jnp.where(lane // 16 == 2 * c + 1, lse[BLK:], lse_all))
        for sl, p, v in zip(chunks, probs, vals):
            o2 = _dot(p, v)
            o_ref[:, sl] = jnp.where(lo, o2[:BLK], o2[BLK:])
        lse_ref[...] = lse_all

    prev = lambda n: jnp.maximum(n - 1, 0)
    col = _qkv_col(d, gi)
    return _carrier_call(
        body, (qkv,) * 5, out_shape=(SDS((L, d * 512), F32), SDS((L, d * 128), F32)), grid=(d, nb),
        in_specs=[pl.BlockSpec((BLK, 512), lambda r, n: (n, col(0, r))),
                  pl.BlockSpec((BLK, 512), lambda r, n: (n, col(1, r))),
                  pl.BlockSpec((BLK, 512), lambda r, n: (prev(n), col(1, r))),
                  pl.BlockSpec((BLK, 512), lambda r, n: (n, col(2, r))),
                  pl.BlockSpec((BLK, 512), lambda r, n: (prev(n), col(2, r)))],
        out_specs=(pl.BlockSpec((BLK, 512), lambda r, n: (n, r)),
                   pl.BlockSpec((BLK, 128), lambda r, n: (n, r))),
        sem=("parallel", "arbitrary"), name=f"attn_fwd_g{gi}", exchanges=exchanges)


def _qkv_to_sub(proj, d, gi):
    S = proj.shape[0]
    tm = 512
    n = tm // d

    def body(q_ref, k_ref, v_ref, o_ref, scr):
        for t, ref in enumerate((q_ref, k_ref, v_ref)):
            for c in range(4):
                scr[c] = ref[:, c * 128:(c + 1) * 128].astype(F32)
            for r in range(d):
                for c in range(4):
                    col = (3 * r + t) * 512 + c * 128
                    o_ref[:, col:col + 128] = scr[c, pl.ds(r, n, stride=d), :].astype(BF16)

    return pl.pallas_call(
        body, out_shape=SDS((S // d, d * 1536), BF16), grid=(S // tm,),
        in_specs=[pl.BlockSpec((tm, 512), lambda i, b=b: (i, b + gi)) for b in (QA_B, KA_B, VA_B)],
        out_specs=pl.BlockSpec((n, d * 1536), lambda i: (i, 0)),
        scratch_shapes=[pltpu.VMEM((4, tm, 128), F32)],
        compiler_params=_cparams("parallel"), name=f"qkv_to_sub_g{gi}")(proj, proj, proj)


def _attn_merge(os_, lses):
    S = os_[0].shape[0]
    tm = 512

    def body(o0, o1, o2, l0, l1, l2, att_ref, lt_ref, so1, so2, sl1, sl2):
        lo = lax.broadcasted_iota(jnp.int32, (tm, 128), 1) < 64

        def natural(ref, d, scr, width):
            nch = width // 128
            if d == 1:
                return [ref[:, c * 128:(c + 1) * 128] for c in range(nch)]
            for r in range(d):
                for c in range(nch):
                    scr[c, pl.ds(r, tm // d, stride=d), :] = ref[:, r * width + c * 128:r * width + (c + 1) * 128]
            return [scr[c] for c in range(nch)]

        ls = [natural(l, d, s, 128)[0] for l, d, s in zip((l0, l1, l2), DILATIONS, (None, sl1, sl2))]
        m = jnp.maximum(jnp.maximum(ls[0], ls[1]), ls[2])
        es = [jnp.exp(v - m) for v in ls]
        z = es[0] + es[1] + es[2]
        lt_ref[...] = m + jnp.log(z)
        ws = [e / z for e in es]
        o_nat = [natural(o, d, s, 512) for o, d, s in zip((o0, o1, o2), DILATIONS, (None, so1, so2))]
        for c in range(4):
            acc = jnp.zeros((tm, 128), F32)
            for g in range(3):
                w_lo = jnp.broadcast_to(ws[g][:, 32 * c:32 * c + 1], (tm, 128))
                w_hi = jnp.broadcast_to(ws[g][:, 32 * c + 16:32 * c + 17], (tm, 128))
                acc = acc + jnp.where(lo, w_lo, w_hi) * o_nat[g][c]
            att_ref[:, c * 128:(c + 1) * 128] = acc.astype(BF16)

    sub = lambda w: [pl.BlockSpec((tm // d, d * w), lambda i: (i, 0)) for d in DILATIONS]
    return pl.pallas_call(
        body, out_shape=(SDS((S, 512), BF16), SDS((S, 128), F32)), grid=(S // tm,),
        in_specs=sub(512) + sub(128),
        out_specs=(pl.BlockSpec((tm, 512), lambda i: (i, 0)), pl.BlockSpec((tm, 128), lambda i: (i, 0))),
        scratch_shapes=[pltpu.VMEM((4, tm, 128), F32), pltpu.VMEM((4, tm, 128), F32),
                        pltpu.VMEM((1, tm, 128), F32), pltpu.VMEM((1, tm, 128), F32)],
        compiler_params=_cparams("parallel"), name="attn_merge")(*os_, *lses)


def _assemble_dproj(att_grads, dqr, dkr, dvr, dgr, dga, dgrr):
    S = dqr.shape[0]
    tm = 256

    def body(*refs):
        a = [refs[3 * t:3 * t + 3] for t in range(3)]
        dqr_ref, dkr_ref, dvr_ref, dgr_ref, dga_ref, dgrr_ref, o_ref, scr = refs[9:]
        for t in range(3):
            for g, d in enumerate(DILATIONS):
                base = (3 * t + g) * COLB
                if d == 1:
                    o_ref[:, base:base + COLB] = a[t][g][...]
                    continue
                for c in range(4):
                    for r in range(d):
                        scr[c, pl.ds(r, tm // d, stride=d), :] = a[t][g][:, r * 512 + c * 128:r * 512 + (c + 1) * 128].astype(F32)
                    o_ref[:, base + c * 128:base + (c + 1) * 128] = scr[c].astype(BF16)
        o_ref[:, 9 * COLB:10 * COLB] = dqr_ref[...]
        o_ref[:, 10 * COLB:11 * COLB] = dkr_ref[...]
        o_ref[:, 11 * COLB:13 * COLB] = dvr_ref[...]
        o_ref[:, 13 * COLB:15 * COLB] = dgr_ref[...]
        o_ref[:, 15 * COLB:17 * COLB] = dga_ref[...]
        o_ref[:, 17 * COLB:19 * COLB] = dgrr_ref[...]

    sub = [pl.BlockSpec((tm // d, d * 512), lambda i: (i, 0)) for d in DILATIONS]
    row = lambda w: pl.BlockSpec((tm, w), lambda i: (i, 0))
    flat = [att_grads[t][g] for t in range(3) for g in range(3)]
    return pl.pallas_call(
        body, out_shape=SDS((S, PROJ_W), BF16), grid=(S // tm,),
        in_specs=sub * 3 + [row(512), row(512), row(1024), row(1024), row(1024), row(1024)],
        out_specs=row(PROJ_W), scratch_shapes=[pltpu.VMEM((4, tm, 128), F32)],
        compiler_params=_cparams("parallel"), name="assemble_dproj")(*flat, dqr, dkr, dvr, dgr, dga, dgrr)


def _ret_fwd(proj, consts):
    S = proj.shape[0]
    nc = S // BLK
    dmask, zeta, xi, dec = consts

    def body(q_ref, k_ref, v0_ref, v1_ref, g0_ref, g1_ref, dm_ref, z_ref, x_ref, dec_ref,
             y_ref, rn_ref, rs_ref, st_ref, R):
        @pl.when(pl.program_id(0) == 0)
        def _():
            R[...] = jnp.zeros_like(R)

        lane16 = lax.broadcasted_iota(jnp.int32, (BLK, 128), 1) // 16
        rs_all = jnp.zeros((BLK, 128), F32)
        first = []
        for h in range(RET_HEADS):
            hs = slice(h * 128, (h + 1) * 128)
            q, k = q_ref[:, hs], k_ref[:, hs]
            v = (v0_ref if h < 2 else v1_ref)[:, (h % 2) * 256:(h % 2 + 1) * 256]
            Rb = R[h].astype(BF16)
            st_ref[h] = Rb
            kz = (k.astype(F32) * z_ref[h]).astype(BF16)
            first.append((v, _dot_nt(q, k), _dot((q.astype(F32) * x_ref[h]).astype(BF16), Rb), _dot_tn(kz, v)))
        masked = [(s * dm_ref[h]).astype(BF16) for h, (_, s, _, _) in enumerate(first)]
        for h in range(RET_HEADS):
            vs = slice((h % 2) * 256, (h % 2 + 1) * 256)
            os_ = slice(h * 256, (h + 1) * 256)
            v, _, cross, kv = first[h]
            o = _dot(masked[h], v) + cross
            R[h] = R[h] * dec_ref[h, 0:1, :] + kv
            mu = jnp.mean(o, axis=-1, keepdims=True)
            oc = o - mu
            rstd = lax.rsqrt(jnp.mean(oc * oc, axis=-1, keepdims=True) + NORM_EPS)
            rn = oc * rstd
            gr = (g0_ref if h < 2 else g1_ref)[:, vs].astype(F32)
            y_ref[:, os_] = (rn * gr * _sigmoid(gr)).astype(BF16)
            rn_ref[:, os_] = rn.astype(BF16)
            rs_all = jnp.where(lane16 == h, rstd, rs_all)
        rs_ref[...] = rs_all

    cst = lambda shape: pl.BlockSpec(shape, lambda c: (0, 0, 0))
    blk = lambda j: pl.BlockSpec((BLK, 512), lambda c: (c, j))
    return pl.pallas_call(
        body,
        out_shape=(SDS((S, 1024), BF16), SDS((S, 1024), BF16), SDS((S, 128), F32), SDS((RET_HEADS, nc, BLK, 256), BF16)),
        grid=(nc,),
        in_specs=[blk(QR_B), blk(KR_B), blk(11), blk(12), blk(13), blk(14),
                  cst((RET_HEADS, BLK, BLK)), cst((RET_HEADS, BLK, 128)), cst((RET_HEADS, BLK, 128)), cst((RET_HEADS, 8, 256))],
        out_specs=(pl.BlockSpec((BLK, 1024), lambda c: (c, 0)), pl.BlockSpec((BLK, 1024), lambda c: (c, 0)),
                   pl.BlockSpec((BLK, 128), lambda c: (c, 0)),
                   pl.BlockSpec((RET_HEADS, None, BLK, 256), lambda c: (0, c, 0, 0))),
        scratch_shapes=[pltpu.VMEM((RET_HEADS, BLK, 256), F32)],
        compiler_params=_cparams("arbitrary"), name="ret_fwd")(proj, proj, proj, proj, proj, proj, dmask, zeta, xi, dec)


def _branch_merge(att, yrin, proj, wa, wr):
    S = att.shape[0]
    tm = min(S, 2048)

    def body(a_ref, y_ref, ga_ref, gr_ref, wa_ref, wr_ref, m_ref, ya_ref, yr_ref):
        for rows in _row_pieces(tm):
            ya = _dot(a_ref[rows, :], wa_ref[...])
            yr = _dot(y_ref[rows, :], wr_ref[...])
            m_ref[rows, :] = (_sigmoid(ga_ref[rows, :].astype(F32)) * ya
                              + _sigmoid(gr_ref[rows, :].astype(F32)) * yr).astype(BF16)
            ya_ref[rows, :] = ya.astype(BF16)
            yr_ref[rows, :] = yr.astype(BF16)

    ospec = pl.BlockSpec((tm, 512), lambda i, j: (i, j))
    return pl.pallas_call(
        body, out_shape=(SDS((S, D_MODEL), BF16),) * 3, grid=(S // tm, 2),
        in_specs=[pl.BlockSpec((tm, 512), lambda i, j: (i, 0)), pl.BlockSpec((tm, 1024), lambda i, j: (i, 0)),
                  pl.BlockSpec((tm, 512), lambda i, j: (i, 15 + j)), pl.BlockSpec((tm, 512), lambda i, j: (i, 17 + j)),
                  pl.BlockSpec((512, 512), lambda i, j: (0, j)), pl.BlockSpec((1024, 512), lambda i, j: (0, j))],
        out_specs=(ospec, ospec, ospec),
        compiler_params=_cparams("parallel", "arbitrary"), name="branch_merge")(att, yrin, proj, proj, wa, wr)


def _out_proj(merged, wo, x, g2):
    S = x.shape[0]
    tm = 1024

    def body(m_ref, w_ref, x_ref, g_ref, x1_ref, h2_ref):
        x1 = x_ref[...] + _dot(m_ref[...], w_ref[...])
        x1_ref[...] = x1
        r = lax.rsqrt(jnp.mean(x1 * x1, axis=-1, keepdims=True) + NORM_EPS)
        h2_ref[...] = (x1 * r * g_ref[...]).astype(BF16)

    row = pl.BlockSpec((tm, D_MODEL), lambda i: (i, 0))
    return pl.pallas_call(
        body, out_shape=(SDS((S, D_MODEL), F32), SDS((S, D_MODEL), BF16)), grid=(S // tm,),
        in_specs=[row, pl.BlockSpec((D_MODEL, D_MODEL), lambda i: (0, 0)), row, pl.BlockSpec((1, D_MODEL), lambda i: (0, 0))],
        out_specs=(row, row), compiler_params=_cparams("parallel"), name="out_proj")(merged, wo, x, g2)


def _ffn_up(h2, wg, wu):
    S = h2.shape[0]
    tm = min(S, 2048)

    def body(h_ref, wg_ref, wu_ref, g_ref, u_ref, a_ref):
        for rows in _row_pieces(tm):
            hv = h_ref[rows, :]
            g = _dot(hv, wg_ref[...])
            u = _dot(hv, wu_ref[...])
            g_ref[rows, :] = g.astype(BF16)
            u_ref[rows, :] = u.astype(BF16)
            a_ref[rows, :] = (g * _sigmoid(g) * u).astype(BF16)

    wspec = pl.BlockSpec((None, D_MODEL, HID_S), lambda i, s: (s, 0, 0))
    ospec = pl.BlockSpec((None, tm, HID_S), lambda i, s: (s, i, 0))
    return pl.pallas_call(
        body, out_shape=(SDS((N_SHARD, S, HID_S), BF16),) * 3, grid=(S // tm, N_SHARD),
        in_specs=[pl.BlockSpec((tm, D_MODEL), lambda i, s: (i, 0)), wspec, wspec],
        out_specs=(ospec, ospec, ospec),
        compiler_params=_cparams("parallel", "arbitrary"), name="ffn_up")(h2, wg, wu)


def _ffn_down_loss(act, wd, x1, g3, tgt):
    S = x1.shape[0]
    tm = 512

    def body(a_ref, w_ref, x_ref, g_ref, t_ref, dx_ref, dxb_ref, dg_ref, ls_ref):
        @pl.when(pl.program_id(0) == 0)
        def _():
            dg_ref[...] = jnp.zeros_like(dg_ref)
            ls_ref[...] = jnp.zeros_like(ls_ref)

        g = g_ref[...]
        for rows in _row_pieces(tm, 256):
            y = _dot(a_ref[0, rows, :], w_ref[0])
            for s in range(1, N_SHARD):
                y = y + _dot(a_ref[s, rows, :], w_ref[s])
            x2 = x_ref[rows, :] + y
            r = lax.rsqrt(jnp.mean(x2 * x2, axis=-1, keepdims=True) + NORM_EPS)
            xh = x2 * r
            err = xh * g - t_ref[rows, :]
            ls_ref[...] += jnp.sum(jnp.sum(err * err, axis=-1, keepdims=True), axis=0, keepdims=True) * (0.5 / D_MODEL)
            dy = err * (1.0 / D_MODEL)
            dg_ref[...] += jnp.sum(dy * xh, axis=0, keepdims=True)
            dxh = dy * g
            dx = r * (dxh - xh * jnp.mean(dxh * xh, axis=-1, keepdims=True))
            dx_ref[rows, :] = dx
            dxb_ref[rows, :] = dx.astype(BF16)

    row = pl.BlockSpec((tm, D_MODEL), lambda i: (i, 0))
    vec = pl.BlockSpec((1, D_MODEL), lambda i: (0, 0))
    return pl.pallas_call(
        body, out_shape=(SDS((S, D_MODEL), F32), SDS((S, D_MODEL), BF16), SDS((1, D_MODEL), F32), SDS((8, 128), F32)),
        grid=(S // tm,),
        in_specs=[pl.BlockSpec((N_SHARD, tm, HID_S), lambda i: (0, i, 0)),
                  pl.BlockSpec((N_SHARD, HID_S, D_MODEL), lambda i: (0, 0, 0), pipeline_mode=pl.Buffered(1)),
                  row, vec, row],
        out_specs=(row, row, vec, pl.BlockSpec((8, 128), lambda i: (0, 0))),
        compiler_params=_cparams("arbitrary"), name="ffn_down_loss")(act, wd, x1, g3, tgt)


def _ffn_down_bwd(dx2b, wd, gte, up):
    S = dx2b.shape[0]
    tm = min(S, 2048)

    def body(d_ref, w_ref, g_ref, u_ref, dg_ref, du_ref):
        for rows in _row_pieces(tm, 256):
            da = _dot_nt(d_ref[rows, :], w_ref[...])
            g = g_ref[rows, :].astype(F32)
            sg = _sigmoid(g)
            dg_ref[rows, :] = (da * u_ref[rows, :].astype(F32) * sg * (1.0 + g * (1.0 - sg))).astype(BF16)
            du_ref[rows, :] = (da * g * sg).astype(BF16)

    aspec = pl.BlockSpec((None, tm, HID_S), lambda i, s: (s, i, 0))
    return pl.pallas_call(
        body, out_shape=(SDS((N_SHARD, S, HID_S), BF16),) * 2, grid=(S // tm, N_SHARD),
        in_specs=[pl.BlockSpec((tm, D_MODEL), lambda i, s: (i, 0)),
                  pl.BlockSpec((None, HID_S, D_MODEL), lambda i, s: (s, 0, 0)), aspec, aspec],
        out_specs=(aspec, aspec),
        compiler_params=_cparams("parallel", "arbitrary"), name="ffn_down_bwd")(dx2b, wd, gte, up)


def _wgrad(name, a, b, a_spec, b_spec, out_shape, out_spec, n_par, S):
    tk = 2048

    def body(a_ref, b_ref, o_ref):
        @pl.when(pl.program_id(1) == 0)
        def _():
            o_ref[...] = jnp.zeros_like(o_ref)

        o_ref[...] += _dot_tn(a_ref[...], b_ref[...])

    return pl.pallas_call(
        body, out_shape=SDS(out_shape, F32), grid=(n_par, S // tk),
        in_specs=[a_spec(tk), b_spec(tk)], out_specs=out_spec,
        compiler_params=_cparams("parallel", "arbitrary"), name=name)(a, b)


def _ffn_up_bwd(dgte, dup, wg, wu, x1, g2, dx2, exchanges=()):
    S = x1.shape[0]
    tm = 512

    def body(dg_ref, du_ref, wg_ref, wu_ref, x_ref, g_ref, dx2_ref, dx_ref, dxb_ref, dgn_ref):
        @pl.when(pl.program_id(0) == 0)
        def _():
            dgn_ref[...] = jnp.zeros_like(dgn_ref)

        for rows in _row_pieces(tm, 256):
            dh = _dot_nt(dg_ref[0, rows, :], wg_ref[0]) + _dot_nt(du_ref[0, rows, :], wu_ref[0])
            for s in range(1, N_SHARD):
                dh = dh + _dot_nt(dg_ref[s, rows, :], wg_ref[s]) + _dot_nt(du_ref[s, rows, :], wu_ref[s])
            xv = x_ref[rows, :]
            r = lax.rsqrt(jnp.mean(xv * xv, axis=-1, keepdims=True) + NORM_EPS)
            xh = xv * r
            dgn_ref[...] += jnp.sum(dh * xh, axis=0, keepdims=True)
            dxh = dh * g_ref[...]
            dx = dx2_ref[rows, :] + r * (dxh - xh * jnp.mean(dxh * xh, axis=-1, keepdims=True))
            dx_ref[rows, :] = dx
            dxb_ref[rows, :] = dx.astype(BF16)

    row = pl.BlockSpec((tm, D_MODEL), lambda i: (i, 0))
    vec = pl.BlockSpec((1, D_MODEL), lambda i: (0, 0))
    aspec = pl.BlockSpec((N_SHARD, tm, HID_S), lambda i: (0, i, 0))
    wspec = pl.BlockSpec((N_SHARD, D_MODEL, HID_S), lambda i: (0, 0, 0), pipeline_mode=pl.Buffered(1))
    return _carrier_call(
        body, (dgte, dup, wg, wu, x1, g2, dx2),
        out_shape=(SDS((S, D_MODEL), F32), SDS((S, D_MODEL), BF16), SDS((1, D_MODEL), F32)),
        grid=(S // tm,),
        in_specs=[aspec, aspec, wspec, wspec, row, vec, row], out_specs=(row, row, vec),
        sem=("arbitrary",), name="ffn_up_bwd", exchanges=exchanges)


def _out_proj_bwd(dx1b, wo, proj, ya, yr):
    S = dx1b.shape[0]
    tm = min(S, 2048)

    def body(d_ref, w_ref, ga_ref, gr_ref, ya_ref, yr_ref, dya_ref, dyr_ref, dga_ref, dgr_ref):
        for rows in _row_pieces(tm, 256):
            dm = _dot_nt(d_ref[rows, :], w_ref[...])
            sa = _sigmoid(ga_ref[rows, :].astype(F32))
            sr = _sigmoid(gr_ref[rows, :].astype(F32))
            dya_ref[rows, :] = (dm * sa).astype(BF16)
            dyr_ref[rows, :] = (dm * sr).astype(BF16)
            dga_ref[rows, :] = (dm * ya_ref[rows, :].astype(F32) * sa * (1.0 - sa)).astype(BF16)
            dgr_ref[rows, :] = (dm * yr_ref[rows, :].astype(F32) * sr * (1.0 - sr)).astype(BF16)

    blk = pl.BlockSpec((tm, 512), lambda i, j: (i, j))
    return pl.pallas_call(
        body, out_shape=(SDS((S, D_MODEL), BF16),) * 4, grid=(S // tm, 2),
        in_specs=[pl.BlockSpec((tm, D_MODEL), lambda i, j: (i, 0)), pl.BlockSpec((512, D_MODEL), lambda i, j: (j, 0)),
                  pl.BlockSpec((tm, 512), lambda i, j: (i, 15 + j)), pl.BlockSpec((tm, 512), lambda i, j: (i, 17 + j)),
                  blk, blk],
        out_specs=(blk,) * 4,
        compiler_params=_cparams("parallel", "arbitrary"), name="out_proj_bwd")(dx1b, wo, proj, proj, ya, yr)


def _branch_bwd(dya, dyr, wa, wr, att):
    S = dya.shape[0]
    tm = 1024

    def body(da_ref, dr_ref, wa_ref, wr_ref, att_ref, datt_ref, rho_ref, dyi_ref):
        datt = _dot_nt(da_ref[...], wa_ref[...])
        datt_ref[...] = datt.astype(BF16)
        dyi_ref[...] = _dot_nt(dr_ref[...], wr_ref[...]).astype(BF16)
        prod = datt * att_ref[...].astype(F32)
        lane = lax.broadcasted_iota(jnp.int32, (tm, 128), 1)
        lo = lane < 64
        rho = jnp.zeros((tm, 128), F32)
        for c in range(4):
            pc = prod[:, c * 128:(c + 1) * 128]
            tot = jnp.sum(pc, axis=-1, keepdims=True)
            low = jnp.sum(jnp.where(lo, pc, 0.0), axis=-1, keepdims=True)
            rho = jnp.where(lane // 16 == 2 * c, low, jnp.where(lane // 16 == 2 * c + 1, tot - low, rho))
        rho_ref[...] = rho

    row = lambda w: pl.BlockSpec((tm, w), lambda i: (i, 0))
    return pl.pallas_call(
        body, out_shape=(SDS((S, 512), BF16), SDS((S, 128), F32), SDS((S, 1024), BF16)), grid=(S // tm,),
        in_specs=[row(1024), row(1024), pl.BlockSpec((512, 1024), lambda i: (0, 0)),
                  pl.BlockSpec((1024, 1024), lambda i: (0, 0)), row(512)],
        out_specs=(row(512), row(128), row(1024)),
        compiler_params=_cparams("parallel"), name="branch_bwd")(dya, dyr, wa, wr, att)


def _attn_bwd(qkv, datt, lse, rho, rtab, d, gi, exchanges=()):
    L = qkv.shape[0]
    nb = L // BLK

    def body(q_ref, kc_ref, kp_ref, vc_ref, vp_ref, do_ref, lse_ref, rho_ref, tq_ref, tk_ref,
             dq_ref, dk_ref, dv_ref, ck, cv):
        n = pl.program_id(1)

        @pl.when(n == 0)
        def _():
            ck[...] = jnp.zeros_like(ck)
            cv[...] = jnp.zeros_like(cv)

        def store_rot(ref, val, t_ref, c):
            sl = slice(c * 128, (c + 1) * 128)
            ref[:, sl] = _unrot(val, t_ref[0], t_ref[1], t_ref[2], 32).astype(BF16)

        @pl.when(n < nb)
        def _():
            mask = _band_mask(n)
            mask2 = jnp.concatenate([mask, mask], axis=0)
            lo = lax.broadcasted_iota(jnp.int32, (BLK, 128), 1) < 64

            def stacked(a):
                return jnp.concatenate([jnp.where(lo, a, jnp.zeros_like(a)), jnp.where(lo, jnp.zeros_like(a), a)], axis=0)

            def head_cols(ref, c):
                return jnp.concatenate([jnp.broadcast_to(ref[:, 32 * c:32 * c + 1], (BLK, 2 * BLK)),
                                        jnp.broadcast_to(ref[:, 32 * c + 16:32 * c + 17], (BLK, 2 * BLK))], axis=0)

            ops, raw = [], []
            for c in range(4):
                sl = slice(c * 128, (c + 1) * 128)
                q2, do2 = stacked(q_ref[:, sl]), stacked(do_ref[:, sl])
                k = jnp.concatenate([kp_ref[:, sl], kc_ref[:, sl]], axis=0)
                v = jnp.concatenate([vp_ref[:, sl], vc_ref[:, sl]], axis=0)
                ops.append((q2, do2, k))
                raw.append((_dot_nt(q2, k), _dot_nt(do2, v)))
            grads = []
            for c, (s, dp) in enumerate(raw):
                p = jnp.where(mask2, jnp.exp(s * 0.125 - head_cols(lse_ref, c)), 0.0)
                grads.append(((p * (dp - head_cols(rho_ref, c)) * 0.125).astype(BF16), p.astype(BF16)))
            for c, ((q2, do2, k), (ds, pb)) in enumerate(zip(ops, grads)):
                sl = slice(c * 128, (c + 1) * 128)
                dq2 = _dot(ds, k)
                dq_c = jnp.where(lo, dq2[:BLK], dq2[BLK:])
                dk_c = _dot_tn(ds, q2)
                dv_c = _dot_tn(pb, do2)
                store_rot(dq_ref, dq_c, tq_ref, c)
                store_rot(dk_ref, ck[:, sl] + dk_c[:BLK], tk_ref, c)
                dv_ref[:, sl] = (cv[:, sl] + dv_c[:BLK]).astype(BF16)
                ck[:, sl] = dk_c[BLK:]
                cv[:, sl] = dv_c[BLK:]

        @pl.when(n == nb)
        def _():
            for c in range(4):
                sl = slice(c * 128, (c + 1) * 128)
                store_rot(dk_ref, ck[:, sl], tk_ref, c)
            dv_ref[...] = cv[...].astype(BF16)

    cur = lambda n: jnp.minimum(n, nb - 1)
    prev = lambda n: jnp.maximum(jnp.minimum(n, nb - 1) - 1, 0)
    fin = lambda n: jnp.maximum(n - 1, 0)
    col = _qkv_col(d, gi)
    return _carrier_call(
        body, (qkv, qkv, qkv, qkv, qkv, datt, lse, rho, rtab, rtab),
        out_shape=(SDS((L, d * 512), BF16),) * 3, grid=(d, nb + 1),
        in_specs=[pl.BlockSpec((BLK, 512), lambda r, n: (cur(n), col(0, r))),
                  pl.BlockSpec((BLK, 512), lambda r, n: (cur(n), col(1, r))),
                  pl.BlockSpec((BLK, 512), lambda r, n: (prev(n), col(1, r))),
                  pl.BlockSpec((BLK, 512), lambda r, n: (cur(n), col(2, r))),
                  pl.BlockSpec((BLK, 512), lambda r, n: (prev(n), col(2, r))),
                  pl.BlockSpec((BLK, 512), lambda r, n: (cur(n), r)),
                  pl.BlockSpec((BLK, 128), lambda r, n: (cur(n), r)),
                  pl.BlockSpec((BLK, 128), lambda r, n: (cur(n), r)),
                  pl.BlockSpec((3, BLK, 128), lambda r, n: (0, cur(n), r)),
                  pl.BlockSpec((3, BLK, 128), lambda r, n: (0, fin(n), r))],
        out_specs=(pl.BlockSpec((BLK, 512), lambda r, n: (cur(n), r)),
                   pl.BlockSpec((BLK, 512), lambda r, n: (fin(n), r)),
                   pl.BlockSpec((BLK, 512), lambda r, n: (fin(n), r))),
        scratch_shapes=[pltpu.VMEM((BLK, 512), F32), pltpu.VMEM((BLK, 512), F32)],
        sem=("parallel", "arbitrary"), name=f"attn_bwd_g{gi}", exchanges=exchanges)


def _ret_bwd(proj, rn, rstd, dyrin, states, tab, consts, exchanges=()):
    S = proj.shape[0]
    nc = S // BLK
    dmask, zeta, xi, dec = consts

    def body(q_ref, k_ref, v0_ref, v1_ref, g0_ref, g1_ref, rn_ref, rs_ref, dy_ref, st_ref, tq_ref, tk_ref,
             dm_ref, z_ref, x_ref, dec_ref, dq_ref, dk_ref, dv_ref, dgr_ref, dR):
        @pl.when(pl.program_id(0) == 0)
        def _():
            dR[...] = jnp.zeros_like(dR)

        dobs = []
        for h in range(RET_HEADS):
            vs = slice((h % 2) * 256, (h % 2 + 1) * 256)
            os_ = slice(h * 256, (h + 1) * 256)
            gr = (g0_ref if h < 2 else g1_ref)[:, vs].astype(F32)
            sg = _sigmoid(gr)
            rn_v = rn_ref[:, os_].astype(F32)
            dyi = dy_ref[:, os_].astype(F32)
            dgr_ref[:, os_] = (dyi * rn_v * sg * (1.0 + gr * (1.0 - sg))).astype(BF16)
            drn = dyi * gr * sg
            rstd = jnp.broadcast_to(rs_ref[:, 16 * h:16 * h + 1], (BLK, 256))
            do = rstd * (drn - jnp.mean(drn, axis=-1, keepdims=True) - rn_v * jnp.mean(drn * rn_v, axis=-1, keepdims=True))
            dobs.append(do.astype(BF16))
        first = []
        for h in range(RET_HEADS):
            hs = slice(h * 128, (h + 1) * 128)
            q, k = q_ref[:, hs], k_ref[:, hs]
            v = (v0_ref if h < 2 else v1_ref)[:, (h % 2) * 256:(h % 2 + 1) * 256]
            dob, dRb = dobs[h], dR[h].astype(BF16)
            kz = (k.astype(F32) * z_ref[h]).astype(BF16)
            qx = (q.astype(F32) * x_ref[h]).astype(BF16)
            first.append((q, k, _dot_nt(q, k), _dot_nt(dob, v), _dot(kz, dRb), _dot_nt(dob, st_ref[h]),
                          _dot_nt(v, dRb), _dot_tn(qx, dob)))
        masked = [((s * dm_ref[h]).astype(BF16), (dsr * dm_ref[h]).astype(BF16))
                  for h, (_, _, s, dsr, _, _, _, _) in enumerate(first)]
        for h in range(RET_HEADS):
            hs = slice(h * 128, (h + 1) * 128)
            os_ = slice(h * 256, (h + 1) * 256)
            q, k, _, _, dv_state, dq_state, dk_state, dr_new = first[h]
            sD, dS = masked[h]
            dv_ref[:, os_] = (_dot_tn(sD, dobs[h]) + dv_state).astype(BF16)
            dq = _dot(dS, k) + dq_state * x_ref[h]
            dk = _dot_tn(dS, q) + dk_state * z_ref[h]
            dR[h] = dR[h] * dec_ref[h, 0:1, :] + dr_new
            dq_ref[:, hs] = _unrot(dq, tq_ref[0], tq_ref[1], tq_ref[2], 1).astype(BF16)
            dk_ref[:, hs] = _unrot(dk, tk_ref[0], tk_ref[1], tk_ref[2], 1).astype(BF16)

    rc = lambda c: nc - 1 - c
    cst = lambda shape: pl.BlockSpec(shape, lambda c: (0, 0, 0))
    blk = lambda j: pl.BlockSpec((BLK, 512), lambda c: (rc(c), j))
    row = lambda w: pl.BlockSpec((BLK, w), lambda c: (rc(c), 0))
    return _carrier_call(
        body, (proj, proj, proj, proj, proj, proj, rn, rstd, dyrin, states, tab, tab, dmask, zeta, xi, dec),
        out_shape=(SDS((S, 512), BF16), SDS((S, 512), BF16), SDS((S, 1024), BF16), SDS((S, 1024), BF16)),
        grid=(nc,),
        in_specs=[blk(QR_B), blk(KR_B), blk(11), blk(12), blk(13), blk(14), row(1024), row(128), row(1024),
                  pl.BlockSpec((RET_HEADS, None, BLK, 256), lambda c: (0, rc(c), 0, 0)),
                  pl.BlockSpec((None, 3, BLK, 128), lambda c: (1, 0, rc(c), 0)),
                  pl.BlockSpec((None, 3, BLK, 128), lambda c: (2, 0, rc(c), 0)),
                  cst((RET_HEADS, BLK, BLK)), cst((RET_HEADS, BLK, 128)), cst((RET_HEADS, BLK, 128)), cst((RET_HEADS, 8, 256))],
        out_specs=(row(512), row(512), row(1024), row(1024)),
        scratch_shapes=[pltpu.VMEM((RET_HEADS, BLK, 256), F32)],
        sem=("arbitrary",), name="ret_bwd", exchanges=exchanges)


def _wgrad_in_half(ht, dproj, sidx, kept, exchanges=()):
    S = dproj.shape[0]
    tk = 2048
    half = (lambda sx: sx[4]) if kept else (lambda sx: 1 - sx[4])

    def body(a_ref, b_ref, o_ref):
        @pl.when(pl.program_id(1) == 0)
        def _():
            o_ref[...] = jnp.zeros_like(o_ref)

        o_ref[...] += _dot(a_ref[...], b_ref[...])

    (g,), xres = _carrier_call(
        body, (ht, dproj), out_shape=(SDS((D_MODEL // 2, PROJ_W), F32),), grid=(N_SHARD, S // tk),
        in_specs=[pl.BlockSpec((D_MODEL // 2, tk), lambda s, k, sx: (half(sx), k)),
                  pl.BlockSpec((tk, W_IN_S), lambda s, k, sx: (k, s))],
        out_specs=(pl.BlockSpec((D_MODEL // 2, W_IN_S), lambda s, k, sx: (0, s)),),
        sem=("parallel", "arbitrary"), name="wgrad_in_kept" if kept else "wgrad_in_sent", exchanges=exchanges,
        prefetch=sidx)
    return g, xres


def _in_proj_bwd(dproj, w_in, x, g1, dx1, exchanges=()):
    S = x.shape[0]
    tm = 512

    def body(d_ref, w_ref, x_ref, g_ref, dx1_ref, dx_ref, dgn_ref, acc):
        i, s = pl.program_id(0), pl.program_id(1)

        @pl.when(s == 0)
        def _():
            acc[...] = jnp.zeros_like(acc)

        @pl.when((i == 0) & (s == 0))
        def _():
            dgn_ref[...] = jnp.zeros_like(dgn_ref)

        acc[...] += _dot_nt(d_ref[...], w_ref[...])

        @pl.when(s == N_SHARD - 1)
        def _():
            xv = x_ref[...]
            r = lax.rsqrt(jnp.mean(xv * xv, axis=-1, keepdims=True) + NORM_EPS)
            xh = xv * r
            dh = acc[...]
            dgn_ref[...] += jnp.sum(dh * xh, axis=0, keepdims=True)
            dxh = dh * g_ref[...]
            dx_ref[...] = dx1_ref[...] + r * (dxh - xh * jnp.mean(dxh * xh, axis=-1, keepdims=True))

    row = pl.BlockSpec((tm, D_MODEL), lambda i, s: (i, 0))
    vec = pl.BlockSpec((1, D_MODEL), lambda i, s: (0, 0))
    (gx, dg), xres = _carrier_call(
        body, (dproj, w_in, x, g1, dx1),
        out_shape=(SDS((S, D_MODEL), F32), SDS((1, D_MODEL), F32)), grid=(S // tm, N_SHARD),
        in_specs=[pl.BlockSpec((tm, W_IN_S), lambda i, s: (i, s)),
                  pl.BlockSpec((D_MODEL, W_IN_S), lambda i, s: (0, s)), row, vec, row],
        out_specs=(row, vec), scratch_shapes=[pltpu.VMEM((tm, D_MODEL), F32)],
        sem=("arbitrary", "arbitrary"), name="in_proj_bwd", exchanges=exchanges)
    return gx, dg, xres


def _sub_view(a, d):
    S, W = a.shape
    return a.reshape(S // d, d * W)


def _step(x, tgt, g1, g2, g3, comm):
    S = x.shape[0]
    tab_np = _tables(S)
    tab = jnp.asarray(tab_np)
    consts = _ret_consts()

    h, ht = _rms_fwd(x, g1)
    w_in = comm.w_in()
    proj, xres = _in_proj(h, w_in, tab, comm.carry("in_proj"))
    comm.took("in_proj", xres)
    qkvs, o_parts, lse_parts = [], [], []
    for gi, d in enumerate(DILATIONS):
        qkv = proj if d == 1 else _qkv_to_sub(proj, d, gi)
        (o_g, lse_g), xres = _attn_fwd(qkv, d, gi, comm.carry(f"attn_fwd_g{gi}"))
        comm.took(f"attn_fwd_g{gi}", xres)
        qkvs.append(qkv)
        o_parts.append(o_g)
        lse_parts.append(lse_g)
    att, lse_tot = _attn_merge(o_parts, lse_parts)
    yrin, rn, rstd, states = _ret_fwd(proj, consts)
    wa, wr, wo, wg, wu, wd = comm.w_rest()
    merged, ya, yr = _branch_merge(att, yrin, proj, wa, wr)
    x1, h2 = _out_proj(merged, wo, x, g2)
    gte, up, act = _ffn_up(h2, wg, wu)
    dx2, dx2b, dg3, loss_p = _ffn_down_loss(act, wd, x1, g3, tgt)

    dgte, dup = _ffn_down_bwd(dx2b, wd, gte, up)
    tok3 = lambda w: (lambda tk: pl.BlockSpec((None, tk, w), lambda p, k: (p, k, 0)))
    tok2 = lambda w: (lambda tk: pl.BlockSpec((tk, w), lambda p, k: (k, 0)))
    g_d = _wgrad("wgrad_down", act, dx2b, tok3(HID_S), tok2(D_MODEL), (N_SHARD, HID_S, D_MODEL),
                 pl.BlockSpec((None, HID_S, D_MODEL), lambda p, k: (p, 0, 0)), N_SHARD, S)
    g_g = _wgrad("wgrad_gate", h2, dgte, tok2(D_MODEL), tok3(HID_S), (N_SHARD, D_MODEL, HID_S),
                 pl.BlockSpec((None, D_MODEL, HID_S), lambda p, k: (p, 0, 0)), N_SHARD, S)
    g_u = _wgrad("wgrad_up", h2, dup, tok2(D_MODEL), tok3(HID_S), (N_SHARD, D_MODEL, HID_S),
                 pl.BlockSpec((None, D_MODEL, HID_S), lambda p, k: (p, 0, 0)), N_SHARD, S)
    comm.grads({4: g_g, 5: g_u, 6: g_d})
    (dx1, dx1b, dg2), xres = _ffn_up_bwd(dgte, dup, wg, wu, x1, g2, dx2, comm.carry("ffn_up_bwd"))
    comm.took("ffn_up_bwd", xres)
    dya, dyr, dga, dgrr = _out_proj_bwd(dx1b, wo, proj, ya, yr)
    colblk = lambda w: (lambda tk: pl.BlockSpec((tk, w), lambda p, k: (k, p)))
    g_o = _wgrad("wgrad_out", merged, dx1b, colblk(256), tok2(D_MODEL), (D_MODEL, D_MODEL),
                 pl.BlockSpec((256, D_MODEL), lambda p, k: (p, 0)), 4, S)
    datt, rho, dyrin = _branch_bwd(dya, dyr, wa, wr, att)
    g_a = _wgrad("wgrad_attn", att, dya, tok2(512), colblk(512), (512, D_MODEL),
                 pl.BlockSpec((512, 512), lambda p, k: (0, p)), 2, S)
    g_r = _wgrad("wgrad_ret", yrin, dyr, colblk(256), tok2(D_MODEL), (D_MODEL, D_MODEL),
                 pl.BlockSpec((256, D_MODEL), lambda p, k: (p, 0)), 4, S)
    comm.grads({1: g_a, 2: g_r.reshape(N_SHARD, 256, D_MODEL), 3: g_o.reshape(N_SHARD, 256, D_MODEL)})
    (dqr, dkr, dvr, dgr), xres = _ret_bwd(proj, rn, rstd, dyrin, states, tab, consts, comm.carry("ret_bwd"))
    comm.took("ret_bwd", xres)
    dqs, dks, dvs = [], [], []
    for gi, d in enumerate(DILATIONS):
        rtab = jnp.asarray(tab_np[0].reshape(3, S // d, d * 128))
        (dq, dk, dv), xres = _attn_bwd(qkvs[gi], _sub_view(datt, d), _sub_view(lse_tot, d), _sub_view(rho, d), rtab, d, gi,
                                       comm.carry(f"attn_bwd_g{gi}"))
        comm.took(f"attn_bwd_g{gi}", xres)
        dqs.append(dq)
        dks.append(dk)
        dvs.append(dv)
    dproj = _assemble_dproj((dqs, dks, dvs), dqr, dkr, dvr, dgr, dga, dgrr)
    g_sent, xres = _wgrad_in_half(ht, dproj, comm.sidx, False, comm.carry("wgrad_in_sent"))
    comm.took("wgrad_in_sent", xres)
    comm.grads({"in_sent": g_sent})
    g_kept, xres = _wgrad_in_half(ht, dproj, comm.sidx, True, comm.carry("wgrad_in_kept"))
    comm.grads({"in_kept": g_kept})
    comm.took("wgrad_in_kept", xres)
    grad_x, dg1, xres = _in_proj_bwd(dproj, w_in, x, g1, dx1, comm.carry("in_proj_bwd"))
    comm.took("in_proj_bwd", xres)
    return loss_p, grad_x, (dg1, dg2, dg3)


W_KINDS = ("col", "col", "lead", "lead", "lead", "lead", "lead")
W_SHARD = ((1024, W_IN_S), (512, 256), (256, 1024), (256, 1024), (1024, HID_S), (1024, HID_S), (HID_S, 1024))
N_W = len(W_KINDS)


def _full_shape(wi):
    R, C = W_SHARD[wi]
    return (R, N_SHARD * C) if W_KINDS[wi] == "col" else (N_SHARD, R, C)


def _view(ref, wi, s, half):
    R, C = W_SHARD[wi]
    rows = pl.ds(half * (R // 2), R // 2)
    if W_KINDS[wi] == "col":
        return ref.at[rows, pl.ds(pl.multiple_of(s * C, 128), C)]
    return ref.at[s, rows, :]


def _mesh_pos():
    x, y, c = lax.axis_index("x"), lax.axis_index("y"), lax.axis_index("c")
    chips = [(1 - x, y), (x, 1 - y), (1 - x, 1 - y)]
    return x, y, c, chips


def _cast_bf16(a):
    R, C = a.shape
    tr = R // 2 if R % 32 == 0 else R

    def body(a_ref, o_ref):
        o_ref[...] = a_ref[...].astype(BF16)

    spec = pl.BlockSpec((tr, C), lambda i: (i, 0))
    return pl.pallas_call(body, out_shape=SDS((R, C), BF16), grid=(R // tr,), in_specs=[spec], out_specs=spec,
                          compiler_params=_cparams("parallel"), name=f"cast_{R}x{C}")(a)


def _remote(send, recv, k, src, dst, to):
    return pltpu.make_async_remote_copy(src_ref=src, dst_ref=dst, send_sem=send.at[k], recv_sem=recv.at[k],
                                        device_id=to, device_id_type=MESH)


def _gather_now(wis, shards):
    n = len(wis)

    def body(*refs):
        sh, full = refs[:n], refs[n:2 * n]
        send, recv, loc = refs[2 * n:]
        x, y, c, _ = _mesh_pos()
        s_me, sib = 2 * x + y, (x, y, 1 - c)
        xn, yn = (1 - x, y), (x, 1 - y)
        flip = lambda a, b: a + b - 2 * a * b
        via = (flip(x, 1 - c), flip(y, c))
        onto = (flip(x, c), flip(y, 1 - c))
        shard_of = lambda chip: 2 * chip[0] + chip[1]
        own, started = [], []
        for i, wi in enumerate(wis):
            Rh = W_SHARD[wi][0] // 2
            for hf in range(2):
                cp = pltpu.make_async_copy(sh[i].at[pl.ds(hf * Rh, Rh), :], _view(full[i], wi, s_me, hf), loc.at[2 * i + hf])
                cp.start()
                own.append(cp)
            for j, chip in enumerate((xn, yn)):
                cp = _remote(send, recv, 6 * i + j, sh[i].at[pl.ds(c * Rh, Rh), :], _view(full[i], wi, s_me, c), (*chip, c))
                cp.start()
                started.append(cp)

        def pass_to_sibling(i, wi, k, s):
            mine = _view(full[i], wi, s, c)
            fw = _remote(send, recv, 6 * i + k, mine, mine, sib)
            fw.start()
            started.append(fw)

        for i, wi in enumerate(wis):
            for j, chip in enumerate((xn, yn)):
                land = _view(full[i], wi, shard_of(chip), c)
                _remote(send, recv, 6 * i + j, land, land, (*chip, c)).wait_recv()
                pass_to_sibling(i, wi, 3 + j, shard_of(chip))
            relay = _view(full[i], wi, shard_of(via), c)
            fw = _remote(send, recv, 6 * i + 2, relay, relay, (*onto, c))
            fw.start()
            started.append(fw)
        s_diag = 2 * (1 - x) + (1 - y)
        for i, wi in enumerate(wis):
            land = _view(full[i], wi, s_diag, c)
            _remote(send, recv, 6 * i + 2, land, land, (*onto, c)).wait_recv()
            pass_to_sibling(i, wi, 5, s_diag)
        for i, wi in enumerate(wis):
            for k, s in ((3, shard_of(xn)), (4, shard_of(yn)), (5, s_diag)):
                land = _view(full[i], wi, s, 1 - c)
                _remote(send, recv, 6 * i + k, land, land, sib).wait_recv()
        for cp in started:
            cp.wait_send()
        for cp in own:
            cp.wait()

    return pl.pallas_call(
        body, out_shape=tuple(SDS(_full_shape(wi), BF16) for wi in wis),
        in_specs=[ANY] * n, out_specs=tuple([ANY] * n),
        scratch_shapes=[pltpu.SemaphoreType.DMA((6 * n,)), pltpu.SemaphoreType.DMA((6 * n,)),
                        pltpu.SemaphoreType.DMA((2 * n,))],
        name="gather_now")(*shards)


def _ex_gather_ici(wis, shards):
    def build(ins, outs, send, recv, loc):
        x, y, c, chips = _mesh_pos()
        s_me = 2 * x + y
        starts, waits = [], []
        for i, wi in enumerate(wis):
            Rh = W_SHARD[wi][0] // 2
            for hf in range(2):
                cp = pltpu.make_async_copy(ins[i].at[pl.ds(hf * Rh, Rh), :], _view(outs[i], wi, s_me, hf), loc.at[2 * i + hf])
                starts.append(cp)
                waits.append(cp.wait)
            for j, chip in enumerate(chips):
                cp = _remote(send, recv, 3 * i + j, ins[i].at[pl.ds(c * Rh, Rh), :], _view(outs[i], wi, s_me, c), (*chip, c))
                land = _view(outs[i], wi, 2 * chip[0] + chip[1], c)
                starts.append(cp)
                waits += [cp.wait_send, _remote(send, recv, 3 * i + j, land, land, (*chip, c)).wait_recv]
        return starts, waits

    return _Exchange(shards, [SDS(_full_shape(wi), BF16) for wi in wis], {}, 3 * len(wis), 2 * len(wis), build)


def _ex_gather_d2d(wis, fulls):
    def build(ins, outs, send, recv, loc):
        x, y, c, chips = _mesh_pos()
        sib = (x, y, 1 - c)
        starts, waits = [], []
        for i, wi in enumerate(wis):
            for j, chip in enumerate(chips):
                mine = _view(outs[i], wi, 2 * chip[0] + chip[1], c)
                theirs = _view(outs[i], wi, 2 * chip[0] + chip[1], 1 - c)
                cp = _remote(send, recv, 3 * i + j, mine, mine, sib)
                starts.append(cp)
                waits += [cp.wait_send, _remote(send, recv, 3 * i + j, theirs, theirs, sib).wait_recv]
        return starts, waits

    return _Exchange(fulls, [SDS(f.shape, BF16) for f in fulls], {i: i for i in range(len(wis))}, 3 * len(wis), 0, build)


def _half_shape(wi):
    R, C = W_SHARD[wi]
    return (R // 2, N_SHARD * C) if W_KINDS[wi] == "col" else (N_SHARD, R // 2, C)


def _ex_pair(wis, grads):
    def build(ins, outs, send, recv, loc):
        x, y, c, _ = _mesh_pos()
        starts, waits = [], []
        for i, wi in enumerate(wis):
            Rh = W_SHARD[wi][0] // 2
            rows = pl.ds((1 - c) * Rh, Rh)
            if tuple(ins[i].shape) == _half_shape(wi):
                src = ins[i]
            else:
                src = ins[i].at[rows, :] if W_KINDS[wi] == "col" else ins[i].at[:, rows, :]
            cp = _remote(send, recv, i, src, outs[i], (x, y, 1 - c))
            starts.append(cp)
            waits.append(cp.wait)
        return starts, waits

    return _Exchange(grads, [SDS(_half_shape(wi), F32) for wi in wis], {}, len(wis), 0, build)


def _ex_chip(wis, pbs):
    def build(ins, outs, send, recv, loc):
        x, y, c, chips = _mesh_pos()
        starts, waits = [], []
        for i, wi in enumerate(wis):
            for j, chip in enumerate(chips):
                cp = _remote(send, recv, 3 * i + j, ins[i].at[j], outs[i].at[j], (*chip, c))
                starts.append(cp)
                waits.append(cp.wait)
        return starts, waits

    shapes = [SDS((3, W_SHARD[wi][0] // 2, W_SHARD[wi][1]), BF16) for wi in wis]
    return _Exchange(pbs, shapes, {}, 3 * len(wis), 0, build)


def _ex_share(wis, halves):
    def build(ins, outs, send, recv, loc):
        x, y, c, _ = _mesh_pos()
        sib = (x, y, 1 - c)
        starts, waits = [], []
        for i, wi in enumerate(wis):
            cp = _remote(send, recv, i, outs[i].at[c], outs[i].at[c], sib)
            starts.append(cp)
            waits += [cp.wait_send, _remote(send, recv, i, outs[i].at[1 - c], outs[i].at[1 - c], sib).wait_recv]
        return starts, waits

    return _Exchange(halves, [SDS(h.shape, F32) for h in halves], {i: i for i in range(len(wis))}, len(wis), 0, build)


def _row_tile(rh, C):
    best = 16
    for t in range(16, rh + 1, 16):
        if rh % t == 0 and t * C * 4 <= (3 << 19):
            best = t
    return best


def _pair_sum(wi, g, ra, sidx):
    R, C = W_SHARD[wi]
    Rh = R // 2
    tr = _row_tile(Rh, C)
    nt = Rh // tr
    off = 0 if tuple(g.shape) == _half_shape(wi) else nt
    col = W_KINDS[wi] == "col"

    def body(sidx_ref, *refs):
        gs, rs = refs[:4], refs[4:8]
        own_ref, pb_ref = refs[8:]
        own_ref[...] = gs[0][...] + rs[0][...]
        for j in range(3):
            pb_ref[j] = (gs[1 + j][...] + rs[1 + j][...]).astype(BF16)

    def gspec(slot):
        if col:
            return pl.BlockSpec((tr, C), lambda i, sx: (sx[4] * off + i, sx[slot]))
        return pl.BlockSpec((None, tr, C), lambda i, sx: (sx[slot], sx[4] * off + i, 0))

    def rspec(slot):
        if col:
            return pl.BlockSpec((tr, C), lambda i, sx: (i, sx[slot]))
        return pl.BlockSpec((None, tr, C), lambda i, sx: (sx[slot], i, 0))

    return pl.pallas_call(
        body, out_shape=(SDS((Rh, C), F32), SDS((3, Rh, C), BF16)),
        grid_spec=pltpu.PrefetchScalarGridSpec(
            num_scalar_prefetch=1, grid=(nt,),
            in_specs=[gspec(k) for k in range(4)] + [rspec(k) for k in range(4)],
            out_specs=(pl.BlockSpec((tr, C), lambda i, sx: (i, 0)), pl.BlockSpec((3, tr, C), lambda i, sx: (0, i, 0)))),
        compiler_params=_cparams("arbitrary"), name=f"pair_sum_w{wi}")(sidx, g, g, g, g, ra, ra, ra, ra)


def _chip_sum(wi, own, rb, sidx):
    R, C = W_SHARD[wi]
    Rh = R // 2
    tr = _row_tile(Rh, C)

    def body(sidx_ref, own_ref, rb_ref, o_ref):
        o_ref[...] = ((own_ref[...] + rb_ref[0].astype(F32)) + rb_ref[1].astype(F32)) + rb_ref[2].astype(F32)

    return pl.pallas_call(
        body, out_shape=SDS((2, Rh, C), F32),
        grid_spec=pltpu.PrefetchScalarGridSpec(
            num_scalar_prefetch=1, grid=(Rh // tr,),
            in_specs=[pl.BlockSpec((tr, C), lambda i, sx: (i, 0)), pl.BlockSpec((3, tr, C), lambda i, sx: (0, i, 0))],
            out_specs=pl.BlockSpec((None, tr, C), lambda i, sx: (sx[4], i, 0))),
        compiler_params=_cparams("arbitrary"), name=f"chip_sum_w{wi}")(sidx, own, rb)


def _gain_allgather(blk):
    m_per, n = blk.shape

    def body(x_ref, out_ref, send_sems, recv_sems, local_sem):
        x, y, c, chips = _mesh_pos()
        me, sibling = (x, y, c), (x, y, 1 - c)

        def rows(px, py, pc):
            return out_ref.at[pl.ds((4 * px + 2 * py + pc) * m_per, m_per), :]

        def copy(k, block, to, src=None):
            return pltpu.make_async_remote_copy(
                src_ref=rows(*block) if src is None else src, dst_ref=rows(*block),
                send_sem=send_sems.at[k], recv_sem=recv_sems.at[k], device_id=to, device_id_type=MESH)

        mine = pltpu.make_async_copy(x_ref, rows(*me), local_sem)
        mine.start()
        first = [copy(0, me, sibling, src=x_ref)]
        first += [copy(1 + j, me, (*chip, c), src=x_ref) for j, chip in enumerate(chips)]
        for cp in first:
            cp.start()
        passed = [copy(4 + j, (*chip, c), sibling) for j, chip in enumerate(chips)]
        for j, chip in enumerate(chips):
            copy(1 + j, (*chip, c), me).wait_recv()
            passed[j].start()
        copy(0, sibling, me).wait_recv()
        for j, chip in enumerate(chips):
            copy(4 + j, (*chip, 1 - c), me).wait_recv()
        for cp in first + passed:
            cp.wait_send()
        mine.wait()

    vm = pl.BlockSpec(memory_space=pltpu.VMEM)
    return pl.pallas_call(
        body, out_shape=SDS((8 * m_per, n), blk.dtype), in_specs=[vm], out_specs=vm,
        scratch_shapes=[pltpu.SemaphoreType.DMA((7,)), pltpu.SemaphoreType.DMA((7,)), pltpu.SemaphoreType.DMA],
        name="gain_allgather")(blk)


def _adam_math(w, g, m, v):
    mn = ADAM_B1 * m + (1.0 - ADAM_B1) * g
    vn = ADAM_B2 * v + (1.0 - ADAM_B2) * (g * g)
    mh = mn / (1.0 - ADAM_B1 ** ADAM_STEP)
    vh = vn / (1.0 - ADAM_B2 ** ADAM_STEP)
    return -ADAM_LR * (mh / (jnp.sqrt(vh) + ADAM_EPS) + ADAM_WD * w), mn, vn


def _adamw(wi, w, g, m, v):
    R, C = w.shape
    tr = _row_tile(R, C)

    def body(w_ref, g_ref, m_ref, v_ref, d_ref, mn_ref, vn_ref):
        d_ref[...], mn_ref[...], vn_ref[...] = _adam_math(w_ref[...], g_ref[...], m_ref[...], v_ref[...])

    spec = pl.BlockSpec((tr, C), lambda i: (i, 0))
    return pl.pallas_call(body, out_shape=(SDS((R, C), F32),) * 3, grid=(R // tr,), in_specs=[spec] * 4,
                          out_specs=(spec,) * 3, compiler_params=_cparams("parallel"), name=f"adamw_w{wi}")(w, g, m, v)


def _gain_update(gathered, w, m, v):
    def body(ga_ref, w_ref, m_ref, v_ref, g_ref, d_ref, mn_ref, vn_ref):
        g = ga_ref[0:8, :]
        for dev in range(1, 8):
            g = g + ga_ref[8 * dev:8 * dev + 8, :]
        g_ref[...] = g
        d_ref[...], mn_ref[...], vn_ref[...] = _adam_math(w_ref[...], g, m_ref[...], v_ref[...])

    return pl.pallas_call(body, out_shape=(SDS((8, 1024), F32),) * 4, name="gain_update")(gathered, w, m, v)


GROUP_FFN, GROUP_MIX, GROUP_IN = (4, 5, 6), (1, 2, 3), (0,)
REST = GROUP_MIX + GROUP_FFN


class _MeshComm:
    SCHEDULE = {
        "in_proj": [("ici", (1, 2, 3, 4))],
        "attn_fwd_g0": [("d2d", (1, 2, 3, 4)), ("ici", (5,))],
        "attn_fwd_g1": [("d2d", (5,)), ("ici", (6,))],
        "attn_fwd_g2": [("d2d", (6,))],
        "ffn_up_bwd": [("pair", GROUP_FFN)],
        "ret_bwd": [("pair", GROUP_MIX)],
        "attn_bwd_g0": [("chip", (4,))],
        "attn_bwd_g1": [("chip", (5, 1, 2, 3))],
        "attn_bwd_g2": [("chip", (6,))],
        "wgrad_in_sent": [("share", GROUP_FFN + GROUP_MIX)],
        "wgrad_in_kept": [("pair", GROUP_IN)],
        "in_proj_bwd": [("chip", GROUP_IN)],
    }

    def __init__(self, shards):
        xi, yi, ci = lax.axis_index("x"), lax.axis_index("y"), lax.axis_index("c")
        self.sidx = jnp.stack([2 * xi + yi, 2 * (1 - xi) + yi, 2 * xi + (1 - yi), 2 * (1 - xi) + (1 - yi), ci]).astype(jnp.int32)
        self.shards, self.full = shards, {}
        self.g, self.own, self.pb, self.half, self.red = {}, {}, {}, {}, {}

    def w_in(self):
        return _gather_now(GROUP_IN, [self.shards[0]])[0]

    def w_rest(self):
        f = self.full
        return f[1], f[2].reshape(D_MODEL, D_MODEL), f[3].reshape(D_MODEL, D_MODEL), f[4], f[5], f[6]

    def grads(self, by_wi):
        self.g.update(by_wi)

    def _exchange(self, stage, wis):
        pick = lambda table: [table[wi] for wi in wis]
        if stage == "ici":
            return _ex_gather_ici(wis, pick(self.shards))
        if stage == "d2d":
            return _ex_gather_d2d(wis, pick(self.full))
        if stage == "pair":
            return _ex_pair(wis, [self.g["in_sent"] if wi == 0 else self.g[wi] for wi in wis])
        if stage == "chip":
            return _ex_chip(wis, pick(self.pb))
        return _ex_share(wis, pick(self.half))

    def _landed(self, stage, wis, res):
        for wi, r in zip(wis, res):
            if stage in ("ici", "d2d"):
                self.full[wi] = r
            elif stage == "pair":
                self.own[wi], self.pb[wi] = _pair_sum(wi, self.g["in_kept"] if wi == 0 else self.g[wi], r, self.sidx)
            elif stage == "chip":
                self.half[wi] = _chip_sum(wi, self.own[wi], r, self.sidx)
            else:
                self.red[wi] = r

    def carry(self, point):
        return [self._exchange(stage, wis) for stage, wis in self.SCHEDULE.get(point, ())]

    def took(self, point, xres):
        for (stage, wis), res in zip(self.SCHEDULE.get(point, ()), xres):
            self._landed(stage, wis, res)

    def reduced(self):
        self._landed("share", GROUP_IN, _exchange_call(self._exchange("share", GROUP_IN), "share_w_in"))
        return [self.red[wi] for wi in range(N_W)]


def kernel(x, norm_mix_g, w_in, w_out_attn, w_out_ret, w_out, norm_ffn_g, w_ffn_gate, w_ffn_up, w_ffn_down, norm_final_g, loss_target, m_norm_mix_g, m_w_in, m_w_out_attn, m_w_out_ret, m_w_out, m_norm_ffn_g, m_w_ffn_gate, m_w_ffn_up, m_w_ffn_down, m_norm_final_g, v_norm_mix_g, v_w_in, v_w_out_attn, v_w_out_ret, v_w_out, v_norm_ffn_g, v_w_ffn_gate, v_w_ffn_up, v_w_ffn_down, v_norm_final_g):
    ws = (w_in, w_out_attn, w_out_ret, w_out, w_ffn_gate, w_ffn_up, w_ffn_down)
    ms = (m_w_in, m_w_out_attn, m_w_out_ret, m_w_out, m_w_ffn_gate, m_w_ffn_up, m_w_ffn_down)
    vs = (v_w_in, v_w_out_attn, v_w_out_ret, v_w_out, v_w_ffn_gate, v_w_ffn_up, v_w_ffn_down)
    shard2d = lambda a, wi: a.reshape(W_SHARD[wi])

    comm = _MeshComm([_cast_bf16(shard2d(w, wi)) for wi, w in enumerate(ws)])
    g3 = norm_final_g.reshape(1, D_MODEL)
    loss_p, grad_x, gain_g = _step(x[0], loss_target[0], norm_mix_g, norm_ffn_g, g3, comm)
    gred = comm.reduced()

    outs_g, outs_d, outs_m, outs_v = [], [], [], []
    for wi in range(N_W):
        g2d = gred[wi].reshape(W_SHARD[wi])
        dlt, mn, vn = _adamw(wi, shard2d(ws[wi], wi), g2d, shard2d(ms[wi], wi), shard2d(vs[wi], wi))
        for lst, a in ((outs_g, g2d), (outs_d, dlt), (outs_m, mn), (outs_v, vn)):
            lst.append(a.reshape(ws[wi].shape))

    pad8 = lambda rows: jnp.concatenate([r.reshape(1, D_MODEL) for r in rows]
                                        + [jnp.zeros((8 - len(rows), D_MODEL), F32)], axis=0)
    gathered = _gain_allgather(pad8((*gain_g, jnp.tile(loss_p[0:1], (1, D_MODEL // 128)))))
    gg, gd, gm, gv = _gain_update(gathered, pad8((norm_mix_g, norm_ffn_g, norm_final_g)),
                                  pad8((m_norm_mix_g, m_norm_ffn_g, m_norm_final_g)),
                                  pad8((v_norm_mix_g, v_norm_ffn_g, v_norm_final_g)))
    loss = gg[3, 0]

    def assemble(gain_rows, wlist):
        return (gain_rows[0:1], wlist[0], wlist[1], wlist[2], wlist[3], gain_rows[1:2],
                wlist[4], wlist[5], wlist[6], gain_rows[2])

    return (loss, grad_x[None], *assemble(gg, outs_g), *assemble(gd, outs_d), *assemble(gm, outs_m), *assemble(gv, outs_v))
```

```python
import functools
import math

import numpy as np
import jax
import jax.numpy as jnp
from jax import lax
from jax.experimental import pallas as pl
from jax.experimental.pallas import tpu as pltpu

F32, BF16 = jnp.float32, jnp.bfloat16
SDS = jax.ShapeDtypeStruct
MESH = pl.DeviceIdType.MESH

D_MODEL = 1024
PROJ_W = 9728
COLB = 512
N_COLB = PROJ_W // COLB
QA_B, KA_B, VA_B = 0, 3, 6
QR_B, KR_B = 9, 10
FFN_HID = 2816
N_SHARD = 4
HID_S = FFN_HID // N_SHARD
W_IN_S = PROJ_W // N_SHARD
DILATIONS = (1, 4, 16)
BLK = 128
RET_HEADS = 4
ROPE_THETA = 10000.0
NORM_EPS = 1e-6
ADAM_LR, ADAM_B1, ADAM_B2, ADAM_EPS, ADAM_WD, ADAM_STEP = 0.001, 0.9, 0.999, 1e-08, 0.01, 10
VMEM_LIMIT = 56 << 20


def _cparams(*sem):
    return pltpu.CompilerParams(dimension_semantics=sem or None, vmem_limit_bytes=VMEM_LIMIT)


def _dot(a, b):
    return jnp.dot(a, b, preferred_element_type=F32)


def _dot_nt(a, b):
    return lax.dot_general(a, b, (((1,), (1,)), ((), ())), preferred_element_type=F32)


def _dot_tn(a, b):
    return lax.dot_general(a, b, (((0,), (0,)), ((), ())), preferred_element_type=F32)


def _row_pieces(tm, sub=512):
    return [slice(i, i + sub) for i in range(0, tm, sub)]


def _sigmoid(z):
    return 0.5 * jnp.tanh(0.5 * z) + 0.5


ANY = pl.BlockSpec(memory_space=pl.ANY)


class _Exchange:
    def __init__(self, ins, out_shapes, aliases, n_sem, n_loc, build):
        self.ins, self.out_shapes, self.aliases = list(ins), list(out_shapes), dict(aliases)
        self.n_sem, self.n_loc, self.build = n_sem, n_loc, build

    def sems(self):
        return [pltpu.SemaphoreType.DMA((self.n_sem,)), pltpu.SemaphoreType.DMA((self.n_sem,)),
                pltpu.SemaphoreType.DMA((max(self.n_loc, 1),))]


def _exchange_call(ex, name):
    n_in, n_out = len(ex.ins), len(ex.out_shapes)

    def body(*refs):
        starts, waits = ex.build(refs[:n_in], refs[n_in:n_in + n_out], *refs[n_in + n_out:])
        for cp in starts:
            cp.start()
        for w in waits:
            w()

    return pl.pallas_call(body, out_shape=tuple(ex.out_shapes), in_specs=[ANY] * n_in, out_specs=tuple([ANY] * n_out),
                          input_output_aliases=ex.aliases, scratch_shapes=ex.sems(), name=name)(*ex.ins)


def _carrier_call(body, args, *, out_shape, grid, in_specs, out_specs, scratch_shapes=(), sem, name, exchanges=(),
                  prefetch=None):
    out_shape, out_specs = tuple(out_shape), tuple(out_specs)
    n_in, n_out, n_scr = len(args), len(out_shape), len(scratch_shapes)
    n_pre = 0 if prefetch is None else 1
    x_args, x_outs, aliases, x_scr, spans = [], [], {}, [], []
    for ex in exchanges:
        i0, o0 = len(x_args), len(x_outs)
        for a, o in ex.aliases.items():
            aliases[n_pre + n_in + i0 + a] = n_out + o0 + o
        x_args += ex.ins
        x_outs += ex.out_shapes
        x_scr += ex.sems()
        spans.append((i0, len(ex.ins), o0, len(ex.out_shapes)))
    nx_in, nx_out = len(x_args), len(x_outs)

    def wrapped(*refs):
        refs = refs[n_pre:]
        ins, xin = refs[:n_in], refs[n_in:n_in + nx_in]
        o_base = n_in + nx_in
        outs, xout = refs[o_base:o_base + n_out], refs[o_base + n_out:o_base + n_out + nx_out]
        s_base = o_base + n_out + nx_out
        scr, xs = refs[s_base:s_base + n_scr], refs[s_base + n_scr:]

        def built(e):
            i0, ni, o0, no = spans[e]
            return exchanges[e].build(xin[i0:i0 + ni], xout[o0:o0 + no], *xs[3 * e:3 * e + 3])

        if exchanges:
            first = functools.reduce(jnp.logical_and, [pl.program_id(k) == 0 for k in range(len(grid))])
            last = functools.reduce(jnp.logical_and, [pl.program_id(k) == grid[k] - 1 for k in range(len(grid))])

            @pl.when(first)
            def _():
                for e in range(len(exchanges)):
                    for cp in built(e)[0]:
                        cp.start()

        body(*ins, *outs, *scr)

        if exchanges:
            @pl.when(last)
            def _():
                for e in range(len(exchanges)):
                    for w in built(e)[1]:
                        w()

    all_in, all_out = list(in_specs) + [ANY] * nx_in, out_specs + tuple([ANY] * nx_out)
    all_scr = list(scratch_shapes) + x_scr
    cparams = _cparams(*(sem if not exchanges else ("arbitrary",) * len(grid)))
    if prefetch is None:
        res = pl.pallas_call(wrapped, out_shape=out_shape + tuple(x_outs), grid=grid, in_specs=all_in, out_specs=all_out,
                             scratch_shapes=all_scr, input_output_aliases=aliases, compiler_params=cparams,
                             name=name)(*args, *x_args)
    else:
        gs = pltpu.PrefetchScalarGridSpec(num_scalar_prefetch=1, grid=grid, in_specs=all_in, out_specs=all_out,
                                          scratch_shapes=all_scr)
        res = pl.pallas_call(wrapped, out_shape=out_shape + tuple(x_outs), grid_spec=gs, input_output_aliases=aliases,
                             compiler_params=cparams, name=name)(prefetch, *args, *x_args)
    xres = [tuple(res[n_out + o0:n_out + o0 + no]) for (_, _, o0, no) in spans]
    return tuple(res[:n_out]), xres


def _tables(S):
    f32 = np.float32
    pos = np.arange(S, dtype=f32)
    lane = np.arange(128)
    inv = (f32(ROPE_THETA) ** (-np.arange(0, 64, 2, dtype=f32) / f32(64))).astype(f32)
    ang = (pos[:, None] * inv[None, :]).astype(np.float64)
    idx = (lane % 64) % 32
    c, s = np.cos(ang)[:, idx], np.sin(ang)[:, idx]
    first = ((lane % 64) < 32)[None, :]
    rope = np.stack([c, np.where(first, 0.0, s), np.where(first, -s, 0.0)])
    base = (f32(1.0) / (f32(ROPE_THETA) ** np.linspace(0.0, 1.0, 64, dtype=f32))).astype(f32)
    ang2 = (pos[:, None] * base[None, :]).astype(np.float64)
    c2, s2 = np.cos(ang2)[:, lane // 2], np.sin(ang2)[:, lane // 2]
    even = (lane % 2 == 0)[None, :]
    th = np.stack([c2, np.where(even, 0.0, s2), np.where(even, -s2, 0.0)])
    return np.stack([rope, th, th * (128 ** -0.5)]).astype(f32)


def _rot(a, c, sa, sb, shift):
    return a * c + pltpu.roll(a, shift, 1) * sa + pltpu.roll(a, 128 - shift, 1) * sb


def _unrot(g, c, sa, sb, shift):
    return g * c + pltpu.roll(g * sa, 128 - shift, 1) + pltpu.roll(g * sb, shift, 1)


def _ret_consts():
    h = np.arange(RET_HEADS, dtype=np.float64)
    log_g = np.log1p(-(2.0 ** (-5.0 - h)))
    idx = np.arange(BLK, dtype=np.float64)
    diff = idx[:, None] - idx[None, :]
    dmask = np.where(diff[None] >= 0, np.exp(np.maximum(diff, 0.0)[None] * log_g[:, None, None]), 0.0)
    zeta = np.exp((BLK - 1 - idx)[None, :] * log_g[:, None])
    xi = np.exp((idx + 1.0)[None, :] * log_g[:, None])
    dec = np.exp(BLK * log_g)
    rep = lambda v: np.broadcast_to(v[:, :, None], (RET_HEADS, BLK, 128))
    return (jnp.asarray(dmask, F32), jnp.asarray(rep(zeta), F32), jnp.asarray(rep(xi), F32),
            jnp.asarray(np.broadcast_to(dec[:, None, None], (RET_HEADS, 8, 256)), F32))


def _rms_fwd(x, g):
    S = x.shape[0]
    tm = 512

    def body(x_ref, g_ref, h_ref, ht_ref):
        xv = x_ref[...]
        r = lax.rsqrt(jnp.mean(xv * xv, axis=-1, keepdims=True) + NORM_EPS)
        h = xv * r * g_ref[...]
        h_ref[...] = h.astype(BF16)
        ht_ref[...] = h.T.astype(BF16)

    return pl.pallas_call(
        body, out_shape=(SDS((S, D_MODEL), BF16), SDS((D_MODEL, S), BF16)), grid=(S // tm,),
        in_specs=[pl.BlockSpec((tm, D_MODEL), lambda i: (i, 0)), pl.BlockSpec((1, D_MODEL), lambda i: (0, 0))],
        out_specs=(pl.BlockSpec((tm, D_MODEL), lambda i: (i, 0)), pl.BlockSpec((D_MODEL, tm), lambda i: (0, i))),
        compiler_params=_cparams("parallel"), name="rms_fwd")(x, g)


def _in_proj(h, w_in, tab, exchanges=()):
    S = h.shape[0]
    tm = min(S, 2048)

    def body(h_ref, w_ref, t_ref, o_ref):
        j = pl.program_id(1)
        is_rope = j < 6
        is_theta = (j == QR_B) | (j == KR_B)
        sub = 512

        def rotated(shift):
            for i in range(tm // sub):
                rows = slice(i * sub, (i + 1) * sub)
                acc = _dot(h_ref[rows, :], w_ref[...])
                c, sa, sb = t_ref[0, 0, rows, :], t_ref[0, 1, rows, :], t_ref[0, 2, rows, :]
                for k in range(COLB // 128):
                    sl = slice(k * 128, (k + 1) * 128)
                    o_ref[rows, sl] = _rot(acc[:, sl], c, sa, sb, shift).astype(BF16)

        @pl.when(is_rope)
        def _():
            rotated(32)

        @pl.when(is_theta)
        def _():
            rotated(1)

        @pl.when(jnp.logical_not(is_rope | is_theta))
        def _():
            o_ref[...] = _dot(h_ref[...], w_ref[...]).astype(BF16)

    def tab_map(i, j):
        return (jnp.where(j == QR_B, 1, jnp.where(j == KR_B, 2, 0)), 0, i, 0)

    (proj,), xres = _carrier_call(
        body, (h, w_in, tab), out_shape=(SDS((S, PROJ_W), BF16),), grid=(S // tm, N_COLB),
        in_specs=[pl.BlockSpec((tm, D_MODEL), lambda i, j: (i, 0)),
                  pl.BlockSpec((D_MODEL, COLB), lambda i, j: (0, j)),
                  pl.BlockSpec((1, 3, tm, 128), tab_map)],
        out_specs=(pl.BlockSpec((tm, COLB), lambda i, j: (i, j)),),
        sem=("parallel", "arbitrary"), name="in_proj", exchanges=exchanges)
    return proj, xres


def _band_mask(n):
    qi = lax.broadcasted_iota(jnp.int32, (BLK, 2 * BLK), 0)
    kj = lax.broadcasted_iota(jnp.int32, (BLK, 2 * BLK), 1)
    dist = BLK + qi - kj
    return (dist >= 0) & (dist <= BLK) & ((kj >= BLK) | (n > 0))


def _qkv_col(d, gi):
    if d == 1:
        return lambda t, r: 3 * t + gi
    return lambda t, r: 3 * r + t


def _attn_fwd(qkv, d, gi, exchanges=()):
    L = qkv.shape[0]
    nb = L // BLK

    def body(q_ref, kc_ref, kp_ref, vc_ref, vp_ref, o_ref, lse_ref):
        n = pl.program_id(1)
        mask = _band_mask(n)
        mask2 = jnp.concatenate([mask, mask], axis=0)
        lane = lax.broadcasted_iota(jnp.int32, (BLK, 128), 1)
        lo = lane < 64
        lse_all = jnp.zeros((BLK, 128), F32)
        chunks = [slice(c * 128, (c + 1) * 128) for c in range(4)]
        scores, vals = [], []
        for sl in chunks:
            q = q_ref[:, sl]
            k = jnp.concatenate([kp_ref[:, sl], kc_ref[:, sl]], axis=0)
            vals.append(jnp.concatenate([vp_ref[:, sl], vc_ref[:, sl]], axis=0))
            q2 = jnp.concatenate([jnp.where(lo, q, jnp.zeros_like(q)), jnp.where(lo, jnp.zeros_like(q), q)], axis=0)
            scores.append(_dot_nt(q2, k))
        probs = []
        for c, s in enumerate(scores):
            s = jnp.where(mask2, s * 0.125, jnp.float32(-1e30))
            m = jnp.max(s, axis=-1, keepdims=True)
            p = jnp.exp(s - m)
            l = jnp.sum(p, axis=-1, keepdims=True)
            probs.append((p / l).astype(BF16))
            lse = m + jnp.log(l)
            lse_all = jnp.where(lane // 16 == 2 * c, lse[:BLK], jnp.where(lane // 16 == 2 * c + 1, lse[BLK:], lse_all))
        for sl, p, v in zip(chunks, probs, vals):
            o2 = _dot(p, v)
            o_ref[:, sl] = jnp.where(lo, o2[:BLK], o2[BLK:])
        lse_ref[...] = lse_all

    prev = lambda n: jnp.maximum(n - 1, 0)
    col = _qkv_col(d, gi)
    return _carrier_call(
        body, (qkv,) * 5, out_shape=(SDS((L, d * 512), F32), SDS((L, d * 128), F32)), grid=(d, nb),
        in_specs=[pl.BlockSpec((BLK, 512), lambda r, n: (n, col(0, r))),
                  pl.BlockSpec((BLK, 512), lambda r, n: (n, col(1, r))),
                  pl.BlockSpec((BLK, 512), lambda r, n: (prev(n), col(1, r))),
                  pl.BlockSpec((BLK, 512), lambda r, n: (n, col(2, r))),
                  pl.BlockSpec((BLK, 512), lambda r, n: (prev(n), col(2, r)))],
        out_specs=(pl.BlockSpec((BLK, 512), lambda r, n: (n, r)),
                   pl.BlockSpec((BLK, 128), lambda r, n: (n, r))),
        sem=("parallel", "arbitrary"), name=f"attn_fwd_g{gi}", exchanges=exchanges)


def _qkv_to_sub(proj, d, gi):
    S = proj.shape[0]
    tm = 512
    n = tm // d

    def body(q_ref, k_ref, v_ref, o_ref, scr):
        for t, ref in enumerate((q_ref, k_ref, v_ref)):
            for c in range(4):
                scr[c] = ref[:, c * 128:(c + 1) * 128].astype(F32)
            for r in range(d):
                for c in range(4):
                    col = (3 * r + t) * 512 + c * 128
                    o_ref[:, col:col + 128] = scr[c, pl.ds(r, n, stride=d), :].astype(BF16)

    return pl.pallas_call(
        body, out_shape=SDS((S // d, d * 1536), BF16), grid=(S // tm,),
        in_specs=[pl.BlockSpec((tm, 512), lambda i, b=b: (i, b + gi)) for b in (QA_B, KA_B, VA_B)],
        out_specs=pl.BlockSpec((n, d * 1536), lambda i: (i, 0)),
        scratch_shapes=[pltpu.VMEM((4, tm, 128), F32)],
        compiler_params=_cparams("parallel"), name=f"qkv_to_sub_g{gi}")(proj, proj, proj)


def _attn_merge(os_, lses):
    S = os_[0].shape[0]
    tm = 512

    def body(o0, o1, o2, l0, l1, l2, att_ref, lt_ref, so1, so2, sl1, sl2):
        lo = lax.broadcasted_iota(jnp.int32, (tm, 128), 1) < 64

        def natural(ref, d, scr, width):
            nch = width // 128
            if d == 1:
                return [ref[:, c * 128:(c + 1) * 128] for c in range(nch)]
            for r in range(d):
                for c in range(nch):
                    scr[c, pl.ds(r, tm // d, stride=d), :] = ref[:, r * width + c * 128:r * width + (c + 1) * 128]
            return [scr[c] for c in range(nch)]

        ls = [natural(l, d, s, 128)[0] for l, d, s in zip((l0, l1, l2), DILATIONS, (None, sl1, sl2))]
        m = jnp.maximum(jnp.maximum(ls[0], ls[1]), ls[2])
        es = [jnp.exp(v - m) for v in ls]
        z = es[0] + es[1] + es[2]
        lt_ref[...] = m + jnp.log(z)
        ws = [e / z for e in es]
        o_nat = [natural(o, d, s, 512) for o, d, s in zip((o0, o1, o2), DILATIONS, (None, so1, so2))]
        for c in range(4):
            acc = jnp.zeros((tm, 128), F32)
            for g in range(3):
                w_lo = jnp.broadcast_to(ws[g][:, 32 * c:32 * c + 1], (tm, 128))
                w_hi = jnp.broadcast_to(ws[g][:, 32 * c + 16:32 * c + 17], (tm, 128))
                acc = acc + jnp.where(lo, w_lo, w_hi) * o_nat[g][c]
            att_ref[:, c * 128:(c + 1) * 128] = acc.astype(BF16)

    sub = lambda w: [pl.BlockSpec((tm // d, d * w), lambda i: (i, 0)) for d in DILATIONS]
    return pl.pallas_call(
        body, out_shape=(SDS((S, 512), BF16), SDS((S, 128), F32)), grid=(S // tm,),
        in_specs=sub(512) + sub(128),
        out_specs=(pl.BlockSpec((tm, 512), lambda i: (i, 0)), pl.BlockSpec((tm, 128), lambda i: (i, 0))),
        scratch_shapes=[pltpu.VMEM((4, tm, 128), F32), pltpu.VMEM((4, tm, 128), F32),
                        pltpu.VMEM((1, tm, 128), F32), pltpu.VMEM((1, tm, 128), F32)],
        compiler_params=_cparams("parallel"), name="attn_merge")(*os_, *lses)


def _assemble_dproj(att_grads, dqr, dkr, dvr, dgr, dga, dgrr):
    S = dqr.shape[0]
    tm = 256

    def body(*refs):
        a = [refs[3 * t:3 * t + 3] for t in range(3)]
        dqr_ref, dkr_ref, dvr_ref, dgr_ref, dga_ref, dgrr_ref, o_ref, scr = refs[9:]
        for t in range(3):
            for g, d in enumerate(DILATIONS):
                base = (3 * t + g) * COLB
                if d == 1:
                    o_ref[:, base:base + COLB] = a[t][g][...]
                    continue
                for c in range(4):
                    for r in range(d):
                        scr[c, pl.ds(r, tm // d, stride=d), :] = a[t][g][:, r * 512 + c * 128:r * 512 + (c + 1) * 128].astype(F32)
                    o_ref[:, base + c * 128:base + (c + 1) * 128] = scr[c].astype(BF16)
        o_ref[:, 9 * COLB:10 * COLB] = dqr_ref[...]
        o_ref[:, 10 * COLB:11 * COLB] = dkr_ref[...]
        o_ref[:, 11 * COLB:13 * COLB] = dvr_ref[...]
        o_ref[:, 13 * COLB:15 * COLB] = dgr_ref[...]
        o_ref[:, 15 * COLB:17 * COLB] = dga_ref[...]
        o_ref[:, 17 * COLB:19 * COLB] = dgrr_ref[...]

    sub = [pl.BlockSpec((tm // d, d * 512), lambda i: (i, 0)) for d in DILATIONS]
    row = lambda w: pl.BlockSpec((tm, w), lambda i: (i, 0))
    flat = [att_grads[t][g] for t in range(3) for g in range(3)]
    return pl.pallas_call(
        body, out_shape=SDS((S, PROJ_W), BF16), grid=(S // tm,),
        in_specs=sub * 3 + [row(512), row(512), row(1024), row(1024), row(1024), row(1024)],
        out_specs=row(PROJ_W), scratch_shapes=[pltpu.VMEM((4, tm, 128), F32)],
        compiler_params=_cparams("parallel"), name="assemble_dproj")(*flat, dqr, dkr, dvr, dgr, dga, dgrr)


def _ret_fwd(proj, consts, exchanges=()):
    S = proj.shape[0]
    nc = S // BLK
    dmask, zeta, xi, dec = consts

    def body(q_ref, k_ref, v0_ref, v1_ref, g0_ref, g1_ref, dm_ref, z_ref, x_ref, dec_ref,
             y_ref, rn_ref, rs_ref, st_ref, R):
        @pl.when(pl.program_id(0) == 0)
        def _():
            R[...] = jnp.zeros_like(R)

        lane16 = lax.broadcasted_iota(jnp.int32, (BLK, 128), 1) // 16
        rs_all = jnp.zeros((BLK, 128), F32)
        first = []
        for h in range(RET_HEADS):
            hs = slice(h * 128, (h + 1) * 128)
            q, k = q_ref[:, hs], k_ref[:, hs]
            v = (v0_ref if h < 2 else v1_ref)[:, (h % 2) * 256:(h % 2 + 1) * 256]
            Rb = R[h].astype(BF16)
            st_ref[h] = Rb
            kz = (k.astype(F32) * z_ref[h]).astype(BF16)
            first.append((v, _dot_nt(q, k), _dot((q.astype(F32) * x_ref[h]).astype(BF16), Rb), _dot_tn(kz, v)))
        masked = [(s * dm_ref[h]).astype(BF16) for h, (_, s, _, _) in enumerate(first)]
        for h in range(RET_HEADS):
            vs = slice((h % 2) * 256, (h % 2 + 1) * 256)
            os_ = slice(h * 256, (h + 1) * 256)
            v, _, cross, kv = first[h]
            o = _dot(masked[h], v) + cross
            R[h] = R[h] * dec_ref[h, 0:1, :] + kv
            mu = jnp.mean(o, axis=-1, keepdims=True)
            oc = o - mu
            rstd = lax.rsqrt(jnp.mean(oc * oc, axis=-1, keepdims=True) + NORM_EPS)
            rn = oc * rstd
            gr = (g0_ref if h < 2 else g1_ref)[:, vs].astype(F32)
            y_ref[:, os_] = (rn * gr * _sigmoid(gr)).astype(BF16)
            rn_ref[:, os_] = rn.astype(BF16)
            rs_all = jnp.where(lane16 == h, rstd, rs_all)
        rs_ref[...] = rs_all

    cst = lambda shape: pl.BlockSpec(shape, lambda c: (0, 0, 0))
    blk = lambda j: pl.BlockSpec((BLK, 512), lambda c: (c, j))
    return _carrier_call(
        body, (proj, proj, proj, proj, proj, proj, dmask, zeta, xi, dec),
        out_shape=(SDS((S, 1024), BF16), SDS((S, 1024), BF16), SDS((S, 128), F32), SDS((RET_HEADS, nc, BLK, 256), BF16)),
        grid=(nc,),
        in_specs=[blk(QR_B), blk(KR_B), blk(11), blk(12), blk(13), blk(14),
                  cst((RET_HEADS, BLK, BLK)), cst((RET_HEADS, BLK, 128)), cst((RET_HEADS, BLK, 128)), cst((RET_HEADS, 8, 256))],
        out_specs=(pl.BlockSpec((BLK, 1024), lambda c: (c, 0)), pl.BlockSpec((BLK, 1024), lambda c: (c, 0)),
                   pl.BlockSpec((BLK, 128), lambda c: (c, 0)),
                   pl.BlockSpec((RET_HEADS, None, BLK, 256), lambda c: (0, c, 0, 0))),
        scratch_shapes=[pltpu.VMEM((RET_HEADS, BLK, 256), F32)],
        sem=("arbitrary",), name="ret_fwd", exchanges=exchanges)


def _branch_merge(att, yrin, proj, wa, wr):
    S = att.shape[0]
    tm = min(S, 2048)

    def body(a_ref, y_ref, ga_ref, gr_ref, wa_ref, wr_ref, m_ref, ya_ref, yr_ref):
        for rows in _row_pieces(tm):
            ya = _dot(a_ref[rows, :], wa_ref[...])
            yr = _dot(y_ref[rows, :], wr_ref[...])
            m_ref[rows, :] = (_sigmoid(ga_ref[rows, :].astype(F32)) * ya
                              + _sigmoid(gr_ref[rows, :].astype(F32)) * yr).astype(BF16)
            ya_ref[rows, :] = ya.astype(BF16)
            yr_ref[rows, :] = yr.astype(BF16)

    ospec = pl.BlockSpec((tm, 512), lambda i, j: (i, j))
    return pl.pallas_call(
        body, out_shape=(SDS((S, D_MODEL), BF16),) * 3, grid=(S // tm, 2),
        in_specs=[pl.BlockSpec((tm, 512), lambda i, j: (i, 0)), pl.BlockSpec((tm, 1024), lambda i, j: (i, 0)),
                  pl.BlockSpec((tm, 512), lambda i, j: (i, 15 + j)), pl.BlockSpec((tm, 512), lambda i, j: (i, 17 + j)),
                  pl.BlockSpec((512, 512), lambda i, j: (0, j)), pl.BlockSpec((1024, 512), lambda i, j: (0, j))],
        out_specs=(ospec, ospec, ospec),
        compiler_params=_cparams("parallel", "arbitrary"), name="branch_merge")(att, yrin, proj, proj, wa, wr)


def _out_proj(merged, wo, x, g2, exchanges=()):
    S = x.shape[0]
    tm = 1024

    def body(m_ref, w_ref, x_ref, g_ref, x1_ref, h2_ref):
        x1 = x_ref[...] + _dot(m_ref[...], w_ref[...])
        x1_ref[...] = x1
        r = lax.rsqrt(jnp.mean(x1 * x1, axis=-1, keepdims=True) + NORM_EPS)
        h2_ref[...] = (x1 * r * g_ref[...]).astype(BF16)

    row = pl.BlockSpec((tm, D_MODEL), lambda i: (i, 0))
    return _carrier_call(
        body, (merged, wo, x, g2), out_shape=(SDS((S, D_MODEL), F32), SDS((S, D_MODEL), BF16)), grid=(S // tm,),
        in_specs=[row, pl.BlockSpec((D_MODEL, D_MODEL), lambda i: (0, 0)), row, pl.BlockSpec((1, D_MODEL), lambda i: (0, 0))],
        out_specs=(row, row), sem=("parallel",), name="out_proj", exchanges=exchanges)


def _ffn_up(h2, wg, wu, exchanges=()):
    S = h2.shape[0]
    tm = min(S, 2048)

    def body(h_ref, wg_ref, wu_ref, g_ref, u_ref, a_ref):
        for rows in _row_pieces(tm):
            hv = h_ref[rows, :]
            g = _dot(hv, wg_ref[...])
            u = _dot(hv, wu_ref[...])
            g_ref[rows, :] = g.astype(BF16)
            u_ref[rows, :] = u.astype(BF16)
            a_ref[rows, :] = (g * _sigmoid(g) * u).astype(BF16)

    wspec = pl.BlockSpec((None, D_MODEL, HID_S), lambda i, s: (s, 0, 0))
    ospec = pl.BlockSpec((None, tm, HID_S), lambda i, s: (s, i, 0))
    return _carrier_call(
        body, (h2, wg, wu), out_shape=(SDS((N_SHARD, S, HID_S), BF16),) * 3, grid=(S // tm, N_SHARD),
        in_specs=[pl.BlockSpec((tm, D_MODEL), lambda i, s: (i, 0)), wspec, wspec],
        out_specs=(ospec, ospec, ospec),
        sem=("parallel", "arbitrary"), name="ffn_up", exchanges=exchanges)


def _ffn_down_loss(act, wd, x1, g3, tgt):
    S = x1.shape[0]
    tm = 512

    def body(a_ref, w_ref, x_ref, g_ref, t_ref, dx_ref, dxb_ref, dg_ref, ls_ref):
        @pl.when(pl.program_id(0) == 0)
        def _():
            dg_ref[...] = jnp.zeros_like(dg_ref)
            ls_ref[...] = jnp.zeros_like(ls_ref)

        g = g_ref[...]
        for rows in _row_pieces(tm, 256):
            y = _dot(a_ref[0, rows, :], w_ref[0])
            for s in range(1, N_SHARD):
                y = y + _dot(a_ref[s, rows, :], w_ref[s])
            x2 = x_ref[rows, :] + y
            r = lax.rsqrt(jnp.mean(x2 * x2, axis=-1, keepdims=True) + NORM_EPS)
            xh = x2 * r
            err = xh * g - t_ref[rows, :]
            ls_ref[...] += jnp.sum(jnp.sum(err * err, axis=-1, keepdims=True), axis=0, keepdims=True) * (0.5 / D_MODEL)
            dy = err * (1.0 / D_MODEL)
            dg_ref[...] += jnp.sum(dy * xh, axis=0, keepdims=True)
            dxh = dy * g
            dx = r * (dxh - xh * jnp.mean(dxh * xh, axis=-1, keepdims=True))
            dx_ref[rows, :] = dx
            dxb_ref[rows, :] = dx.astype(BF16)

    row = pl.BlockSpec((tm, D_MODEL), lambda i: (i, 0))
    vec = pl.BlockSpec((1, D_MODEL), lambda i: (0, 0))
    return pl.pallas_call(
        body, out_shape=(SDS((S, D_MODEL), F32), SDS((S, D_MODEL), BF16), SDS((1, D_MODEL), F32), SDS((8, 128), F32)),
        grid=(S // tm,),
        in_specs=[pl.BlockSpec((N_SHARD, tm, HID_S), lambda i: (0, i, 0)),
                  pl.BlockSpec((N_SHARD, HID_S, D_MODEL), lambda i: (0, 0, 0), pipeline_mode=pl.Buffered(1)),
                  row, vec, row],
        out_specs=(row, row, vec, pl.BlockSpec((8, 128), lambda i: (0, 0))),
        compiler_params=_cparams("arbitrary"), name="ffn_down_loss")(act, wd, x1, g3, tgt)


def _ffn_down_bwd(dx2b, wd, gte, up):
    S = dx2b.shape[0]
    tm = min(S, 2048)

    def body(d_ref, w_ref, g_ref, u_ref, dg_ref, du_ref):
        for rows in _row_pieces(tm, 256):
            da = _dot_nt(d_ref[rows, :], w_ref[...])
            g = g_ref[rows, :].astype(F32)
            sg = _sigmoid(g)
            dg_ref[rows, :] = (da * u_ref[rows, :].astype(F32) * sg * (1.0 + g * (1.0 - sg))).astype(BF16)
            du_ref[rows, :] = (da * g * sg).astype(BF16)

    aspec = pl.BlockSpec((None, tm, HID_S), lambda i, s: (s, i, 0))
    return pl.pallas_call(
        body, out_shape=(SDS((N_SHARD, S, HID_S), BF16),) * 2, grid=(S // tm, N_SHARD),
        in_specs=[pl.BlockSpec((tm, D_MODEL), lambda i, s: (i, 0)),
                  pl.BlockSpec((None, HID_S, D_MODEL), lambda i, s: (s, 0, 0)), aspec, aspec],
        out_specs=(aspec, aspec),
        compiler_params=_cparams("parallel", "arbitrary"), name="ffn_down_bwd")(dx2b, wd, gte, up)


def _wgrad(name, a, b, a_spec, b_spec, out_shape, out_spec, n_par, S):
    tk = 2048

    def body(a_ref, b_ref, o_ref):
        @pl.when(pl.program_id(1) == 0)
        def _():
            o_ref[...] = jnp.zeros_like(o_ref)

        o_ref[...] += _dot_tn(a_ref[...], b_ref[...])

    return pl.pallas_call(
        body, out_shape=SDS(out_shape, F32), grid=(n_par, S // tk),
        in_specs=[a_spec(tk), b_spec(tk)], out_specs=out_spec,
        compiler_params=_cparams("parallel", "arbitrary"), name=name)(a, b)


def _ffn_up_bwd(dgte, dup, wg, wu, x1, g2, dx2, exchanges=()):
    S = x1.shape[0]
    tm = 512

    def body(dg_ref, du_ref, wg_ref, wu_ref, x_ref, g_ref, dx2_ref, dx_ref, dxb_ref, dgn_ref):
        @pl.when(pl.program_id(0) == 0)
        def _():
            dgn_ref[...] = jnp.zeros_like(dgn_ref)

        for rows in _row_pieces(tm, 256):
            dh = _dot_nt(dg_ref[0, rows, :], wg_ref[0]) + _dot_nt(du_ref[0, rows, :], wu_ref[0])
            for s in range(1, N_SHARD):
                dh = dh + _dot_nt(dg_ref[s, rows, :], wg_ref[s]) + _dot_nt(du_ref[s, rows, :], wu_ref[s])
            xv = x_ref[rows, :]
            r = lax.rsqrt(jnp.mean(xv * xv, axis=-1, keepdims=True) + NORM_EPS)
            xh = xv * r
            dgn_ref[...] += jnp.sum(dh * xh, axis=0, keepdims=True)
            dxh = dh * g_ref[...]
            dx = dx2_ref[rows, :] + r * (dxh - xh * jnp.mean(dxh * xh, axis=-1, keepdims=True))
            dx_ref[rows, :] = dx
            dxb_ref[rows, :] = dx.astype(BF16)

    row = pl.BlockSpec((tm, D_MODEL), lambda i: (i, 0))
    vec = pl.BlockSpec((1, D_MODEL), lambda i: (0, 0))
    aspec = pl.BlockSpec((N_SHARD, tm, HID_S), lambda i: (0, i, 0))
    wspec = pl.BlockSpec((N_SHARD, D_MODEL, HID_S), lambda i: (0, 0, 0), pipeline_mode=pl.Buffered(1))
    return _carrier_call(
        body, (dgte, dup, wg, wu, x1, g2, dx2),
        out_shape=(SDS((S, D_MODEL), F32), SDS((S, D_MODEL), BF16), SDS((1, D_MODEL), F32)),
        grid=(S // tm,),
        in_specs=[aspec, aspec, wspec, wspec, row, vec, row], out_specs=(row, row, vec),
        sem=("arbitrary",), name="ffn_up_bwd", exchanges=exchanges)


def _out_proj_bwd(dx1b, wo, proj, ya, yr):
    S = dx1b.shape[0]
    tm = min(S, 2048)

    def body(d_ref, w_ref, ga_ref, gr_ref, ya_ref, yr_ref, dya_ref, dyr_ref, dga_ref, dgr_ref):
        for rows in _row_pieces(tm, 256):
            dm = _dot_nt(d_ref[rows, :], w_ref[...])
            sa = _sigmoid(ga_ref[rows, :].astype(F32))
            sr = _sigmoid(gr_ref[rows, :].astype(F32))
            dya_ref[rows, :] = (dm * sa).astype(BF16)
            dyr_ref[rows, :] = (dm * sr).astype(BF16)
            dga_ref[rows, :] = (dm * ya_ref[rows, :].astype(F32) * sa * (1.0 - sa)).astype(BF16)
            dgr_ref[rows, :] = (dm * yr_ref[rows, :].astype(F32) * sr * (1.0 - sr)).astype(BF16)

    blk = pl.BlockSpec((tm, 512), lambda i, j: (i, j))
    return pl.pallas_call(
        body, out_shape=(SDS((S, D_MODEL), BF16),) * 4, grid=(S // tm, 2),
        in_specs=[pl.BlockSpec((tm, D_MODEL), lambda i, j: (i, 0)), pl.BlockSpec((512, D_MODEL), lambda i, j: (j, 0)),
                  pl.BlockSpec((tm, 512), lambda i, j: (i, 15 + j)), pl.BlockSpec((tm, 512), lambda i, j: (i, 17 + j)),
                  blk, blk],
        out_specs=(blk,) * 4,
        compiler_params=_cparams("parallel", "arbitrary"), name="out_proj_bwd")(dx1b, wo, proj, proj, ya, yr)


def _branch_bwd(dya, dyr, wa, wr, att):
    S = dya.shape[0]
    tm = 1024

    def body(da_ref, dr_ref, wa_ref, wr_ref, att_ref, datt_ref, rho_ref, dyi_ref):
        datt = _dot_nt(da_ref[...], wa_ref[...])
        datt_ref[...] = datt.astype(BF16)
        dyi_ref[...] = _dot_nt(dr_ref[...], wr_ref[...]).astype(BF16)
        prod = datt * att_ref[...].astype(F32)
        lane = lax.broadcasted_iota(jnp.int32, (tm, 128), 1)
        lo = lane < 64
        rho = jnp.zeros((tm, 128), F32)
        for c in range(4):
            pc = prod[:, c * 128:(c + 1) * 128]
            tot = jnp.sum(pc, axis=-1, keepdims=True)
            low = jnp.sum(jnp.where(lo, pc, 0.0), axis=-1, keepdims=True)
            rho = jnp.where(lane // 16 == 2 * c, low, jnp.where(lane // 16 == 2 * c + 1, tot - low, rho))
        rho_ref[...] = rho

    row = lambda w: pl.BlockSpec((tm, w), lambda i: (i, 0))
    return pl.pallas_call(
        body, out_shape=(SDS((S, 512), BF16), SDS((S, 128), F32), SDS((S, 1024), BF16)), grid=(S // tm,),
        in_specs=[row(1024), row(1024), pl.BlockSpec((512, 1024), lambda i: (0, 0)),
                  pl.BlockSpec((1024, 1024), lambda i: (0, 0)), row(512)],
        out_specs=(row(512), row(128), row(1024)),
        compiler_params=_cparams("parallel"), name="branch_bwd")(dya, dyr, wa, wr, att)


def _attn_bwd(qkv, datt, lse, rho, rtab, d, gi, exchanges=()):
    L = qkv.shape[0]
    nb = L // BLK

    def body(q_ref, kc_ref, kp_ref, vc_ref, vp_ref, do_ref, lse_ref, rho_ref, tq_ref, tk_ref,
             dq_ref, dk_ref, dv_ref, ck, cv):
        n = pl.program_id(1)

        @pl.when(n == 0)
        def _():
            ck[...] = jnp.zeros_like(ck)
            cv[...] = jnp.zeros_like(cv)

        def store_rot(ref, val, t_ref, c):
            sl = slice(c * 128, (c + 1) * 128)
            ref[:, sl] = _unrot(val, t_ref[0], t_ref[1], t_ref[2], 32).astype(BF16)

        @pl.when(n < nb)
        def _():
            mask = _band_mask(n)
            mask2 = jnp.concatenate([mask, mask], axis=0)
            lo = lax.broadcasted_iota(jnp.int32, (BLK, 128), 1) < 64

            def stacked(a):
                return jnp.concatenate([jnp.where(lo, a, jnp.zeros_like(a)), jnp.where(lo, jnp.zeros_like(a), a)], axis=0)

            def head_cols(ref, c):
                return jnp.concatenate([jnp.broadcast_to(ref[:, 32 * c:32 * c + 1], (BLK, 2 * BLK)),
                                        jnp.broadcast_to(ref[:, 32 * c + 16:32 * c + 17], (BLK, 2 * BLK))], axis=0)

            ops, raw = [], []
            for c in range(4):
                sl = slice(c * 128, (c + 1) * 128)
                q2, do2 = stacked(q_ref[:, sl]), stacked(do_ref[:, sl])
                k = jnp.concatenate([kp_ref[:, sl], kc_ref[:, sl]], axis=0)
                v = jnp.concatenate([vp_ref[:, sl], vc_ref[:, sl]], axis=0)
                ops.append((q2, do2, k))
                raw.append((_dot_nt(q2, k), _dot_nt(do2, v)))
            grads = []
            for c, (s, dp) in enumerate(raw):
                p = jnp.where(mask2, jnp.exp(s * 0.125 - head_cols(lse_ref, c)), 0.0)
                grads.append(((p * (dp - head_cols(rho_ref, c)) * 0.125).astype(BF16), p.astype(BF16)))
            for c, ((q2, do2, k), (ds, pb)) in enumerate(zip(ops, grads)):
                sl = slice(c * 128, (c + 1) * 128)
                dq2 = _dot(ds, k)
                dq_c = jnp.where(lo, dq2[:BLK], dq2[BLK:])
                dk_c = _dot_tn(ds, q2)
                dv_c = _dot_tn(pb, do2)
                store_rot(dq_ref, dq_c, tq_ref, c)
                store_rot(dk_ref, ck[:, sl] + dk_c[:BLK], tk_ref, c)
                dv_ref[:, sl] = (cv[:, sl] + dv_c[:BLK]).astype(BF16)
                ck[:, sl] = dk_c[BLK:]
                cv[:, sl] = dv_c[BLK:]

        @pl.when(n == nb)
        def _():
            for c in range(4):
                sl = slice(c * 128, (c + 1) * 128)
                store_rot(dk_ref, ck[:, sl], tk_ref, c)
            dv_ref[...] = cv[...].astype(BF16)

    cur = lambda n: jnp.minimum(n, nb - 1)
    prev = lambda n: jnp.maximum(jnp.minimum(n, nb - 1) - 1, 0)
    fin = lambda n: jnp.maximum(n - 1, 0)
    col = _qkv_col(d, gi)
    return _carrier_call(
        body, (qkv, qkv, qkv, qkv, qkv, datt, lse, rho, rtab, rtab),
        out_shape=(SDS((L, d * 512), BF16),) * 3, grid=(d, nb + 1),
        in_specs=[pl.BlockSpec((BLK, 512), lambda r, n: (cur(n), col(0, r))),
                  pl.BlockSpec((BLK, 512), lambda r, n: (cur(n), col(1, r))),
                  pl.BlockSpec((BLK, 512), lambda r, n: (prev(n), col(1, r))),
                  pl.BlockSpec((BLK, 512), lambda r, n: (cur(n), col(2, r))),
                  pl.BlockSpec((BLK, 512), lambda r, n: (prev(n), col(2, r))),
                  pl.BlockSpec((BLK, 512), lambda r, n: (cur(n), r)),
                  pl.BlockSpec((BLK, 128), lambda r, n: (cur(n), r)),
                  pl.BlockSpec((BLK, 128), lambda r, n: (cur(n), r)),
                  pl.BlockSpec((3, BLK, 128), lambda r, n: (0, cur(n), r)),
                  pl.BlockSpec((3, BLK, 128), lambda r, n: (0, fin(n), r))],
        out_specs=(pl.BlockSpec((BLK, 512), lambda r, n: (cur(n), r)),
                   pl.BlockSpec((BLK, 512), lambda r, n: (fin(n), r)),
                   pl.BlockSpec((BLK, 512), lambda r, n: (fin(n), r))),
        scratch_shapes=[pltpu.VMEM((BLK, 512), F32), pltpu.VMEM((BLK, 512), F32)],
        sem=("parallel", "arbitrary"), name=f"attn_bwd_g{gi}", exchanges=exchanges)


def _ret_bwd(proj, rn, rstd, dyrin, states, tab, consts, exchanges=()):
    S = proj.shape[0]
    nc = S // BLK
    dmask, zeta, xi, dec = consts

    def body(q_ref, k_ref, v0_ref, v1_ref, g0_ref, g1_ref, rn_ref, rs_ref, dy_ref, st_ref, tq_ref, tk_ref,
             dm_ref, z_ref, x_ref, dec_ref, dq_ref, dk_ref, dv_ref, dgr_ref, dR):
        @pl.when(pl.program_id(0) == 0)
        def _():
            dR[...] = jnp.zeros_like(dR)

        dobs = []
        for h in range(RET_HEADS):
            vs = slice((h % 2) * 256, (h % 2 + 1) * 256)
            os_ = slice(h * 256, (h + 1) * 256)
            gr = (g0_ref if h < 2 else g1_ref)[:, vs].astype(F32)
            sg = _sigmoid(gr)
            rn_v = rn_ref[:, os_].astype(F32)
            dyi = dy_ref[:, os_].astype(F32)
            dgr_ref[:, os_] = (dyi * rn_v * sg * (1.0 + gr * (1.0 - sg))).astype(BF16)
            drn = dyi * gr * sg
            rstd = jnp.broadcast_to(rs_ref[:, 16 * h:16 * h + 1], (BLK, 256))
            do = rstd * (drn - jnp.mean(drn, axis=-1, keepdims=True) - rn_v * jnp.mean(drn * rn_v, axis=-1, keepdims=True))
            dobs.append(do.astype(BF16))
        first = []
        for h in range(RET_HEADS):
            hs = slice(h * 128, (h + 1) * 128)
            q, k = q_ref[:, hs], k_ref[:, hs]
            v = (v0_ref if h < 2 else v1_ref)[:, (h % 2) * 256:(h % 2 + 1) * 256]
            dob, dRb = dobs[h], dR[h].astype(BF16)
            kz = (k.astype(F32) * z_ref[h]).astype(BF16)
            qx = (q.astype(F32) * x_ref[h]).astype(BF16)
            first.append((q, k, _dot_nt(q, k), _dot_nt(dob, v), _dot(kz, dRb), _dot_nt(dob, st_ref[h]),
                          _dot_nt(v, dRb), _dot_tn(qx, dob)))
        masked = [((s * dm_ref[h]).astype(BF16), (dsr * dm_ref[h]).astype(BF16))
                  for h, (_, _, s, dsr, _, _, _, _) in enumerate(first)]
        for h in range(RET_HEADS):
            hs = slice(h * 128, (h + 1) * 128)
            os_ = slice(h * 256, (h + 1) * 256)
            q, k, _, _, dv_state, dq_state, dk_state, dr_new = first[h]
            sD, dS = masked[h]
            dv_ref[:, os_] = (_dot_tn(sD, dobs[h]) + dv_state).astype(BF16)
            dq = _dot(dS, k) + dq_state * x_ref[h]
            dk = _dot_tn(dS, q) + dk_state * z_ref[h]
            dR[h] = dR[h] * dec_ref[h, 0:1, :] + dr_new
            dq_ref[:, hs] = _unrot(dq, tq_ref[0], tq_ref[1], tq_ref[2], 1).astype(BF16)
            dk_ref[:, hs] = _unrot(dk, tk_ref[0], tk_ref[1], tk_ref[2], 1).astype(BF16)

    rc = lambda c: nc - 1 - c
    cst = lambda shape: pl.BlockSpec(shape, lambda c: (0, 0, 0))
    blk = lambda j: pl.BlockSpec((BLK, 512), lambda c: (rc(c), j))
    row = lambda w: pl.BlockSpec((BLK, w), lambda c: (rc(c), 0))
    return _carrier_call(
        body, (proj, proj, proj, proj, proj, proj, rn, rstd, dyrin, states, tab, tab, dmask, zeta, xi, dec),
        out_shape=(SDS((S, 512), BF16), SDS((S, 512), BF16), SDS((S, 1024), BF16), SDS((S, 1024), BF16)),
        grid=(nc,),
        in_specs=[blk(QR_B), blk(KR_B), blk(11), blk(12), blk(13), blk(14), row(1024), row(128), row(1024),
                  pl.BlockSpec((RET_HEADS, None, BLK, 256), lambda c: (0, rc(c), 0, 0)),
                  pl.BlockSpec((None, 3, BLK, 128), lambda c: (1, 0, rc(c), 0)),
                  pl.BlockSpec((None, 3, BLK, 128), lambda c: (2, 0, rc(c), 0)),
                  cst((RET_HEADS, BLK, BLK)), cst((RET_HEADS, BLK, 128)), cst((RET_HEADS, BLK, 128)), cst((RET_HEADS, 8, 256))],
        out_specs=(row(512), row(512), row(1024), row(1024)),
        scratch_shapes=[pltpu.VMEM((RET_HEADS, BLK, 256), F32)],
        sem=("arbitrary",), name="ret_bwd", exchanges=exchanges)


def _wgrad_in_half(ht, dproj, sidx, kept, exchanges=()):
    S = dproj.shape[0]
    tk = 2048
    half = (lambda sx: sx[4]) if kept else (lambda sx: 1 - sx[4])

    def body(a_ref, b_ref, o_ref):
        @pl.when(pl.program_id(1) == 0)
        def _():
            o_ref[...] = jnp.zeros_like(o_ref)

        o_ref[...] += _dot(a_ref[...], b_ref[...])

    (g,), xres = _carrier_call(
        body, (ht, dproj), out_shape=(SDS((D_MODEL // 2, PROJ_W), F32),), grid=(N_SHARD, S // tk),
        in_specs=[pl.BlockSpec((D_MODEL // 2, tk), lambda s, k, sx: (half(sx), k)),
                  pl.BlockSpec((tk, W_IN_S), lambda s, k, sx: (k, s))],
        out_specs=(pl.BlockSpec((D_MODEL // 2, W_IN_S), lambda s, k, sx: (0, s)),),
        sem=("parallel", "arbitrary"), name="wgrad_in_kept" if kept else "wgrad_in_sent", exchanges=exchanges,
        prefetch=sidx)
    return g, xres


def _in_proj_bwd(dproj, w_in, x, g1, dx1, exchanges=()):
    S = x.shape[0]
    tm = 512

    def body(d_ref, w_ref, x_ref, g_ref, dx1_ref, dx_ref, dgn_ref, acc):
        i, s = pl.program_id(0), pl.program_id(1)

        @pl.when(s == 0)
        def _():
            acc[...] = jnp.zeros_like(acc)

        @pl.when((i == 0) & (s == 0))
        def _():
            dgn_ref[...] = jnp.zeros_like(dgn_ref)

        acc[...] += _dot_nt(d_ref[...], w_ref[...])

        @pl.when(s == N_SHARD - 1)
        def _():
            xv = x_ref[...]
            r = lax.rsqrt(jnp.mean(xv * xv, axis=-1, keepdims=True) + NORM_EPS)
            xh = xv * r
            dh = acc[...]
            dgn_ref[...] += jnp.sum(dh * xh, axis=0, keepdims=True)
            dxh = dh * g_ref[...]
            dx_ref[...] = dx1_ref[...] + r * (dxh - xh * jnp.mean(dxh * xh, axis=-1, keepdims=True))

    row = pl.BlockSpec((tm, D_MODEL), lambda i, s: (i, 0))
    vec = pl.BlockSpec((1, D_MODEL), lambda i, s: (0, 0))
    (gx, dg), xres = _carrier_call(
        body, (dproj, w_in, x, g1, dx1),
        out_shape=(SDS((S, D_MODEL), F32), SDS((1, D_MODEL), F32)), grid=(S // tm, N_SHARD),
        in_specs=[pl.BlockSpec((tm, W_IN_S), lambda i, s: (i, s)),
                  pl.BlockSpec((D_MODEL, W_IN_S), lambda i, s: (0, s)), row, vec, row],
        out_specs=(row, vec), scratch_shapes=[pltpu.VMEM((tm, D_MODEL), F32)],
        sem=("arbitrary", "arbitrary"), name="in_proj_bwd", exchanges=exchanges)
    return gx, dg, xres


def _sub_view(a, d):
    S, W = a.shape
    return a.reshape(S // d, d * W)


def _step(x, tgt, g1, g2, g3, comm):
    S = x.shape[0]
    tab_np = _tables(S)
    tab = jnp.asarray(tab_np)
    consts = _ret_consts()

    h, ht = _rms_fwd(x, g1)
    w_in = comm.w_in()
    proj, xres = _in_proj(h, w_in, tab, comm.carry("in_proj"))
    comm.took("in_proj", xres)
    qkvs, o_parts, lse_parts = [], [], []
    for gi, d in enumerate(DILATIONS):
        qkv = proj if d == 1 else _qkv_to_sub(proj, d, gi)
        (o_g, lse_g), xres = _attn_fwd(qkv, d, gi, comm.carry(f"attn_fwd_g{gi}"))
        comm.took(f"attn_fwd_g{gi}", xres)
        qkvs.append(qkv)
        o_parts.append(o_g)
        lse_parts.append(lse_g)
    att, lse_tot = _attn_merge(o_parts, lse_parts)
    (yrin, rn, rstd, states), xres = _ret_fwd(proj, consts, comm.carry("ret_fwd"))
    comm.took("ret_fwd", xres)
    wa, wr, wo = comm.weight(1), comm.weight(2), comm.weight(3)
    merged, ya, yr = _branch_merge(att, yrin, proj, wa, wr)
    (x1, h2), xres = _out_proj(merged, wo, x, g2, comm.carry("out_proj"))
    comm.took("out_proj", xres)
    wg, wu = comm.weight(4), comm.weight(5)
    (gte, up, act), xres = _ffn_up(h2, wg, wu, comm.carry("ffn_up"))
    comm.took("ffn_up", xres)
    wd = comm.weight(6)
    dx2, dx2b, dg3, loss_p = _ffn_down_loss(act, wd, x1, g3, tgt)

    dgte, dup = _ffn_down_bwd(dx2b, wd, gte, up)
    tok3 = lambda w: (lambda tk: pl.BlockSpec((None, tk, w), lambda p, k: (p, k, 0)))
    tok2 = lambda w: (lambda tk: pl.BlockSpec((tk, w), lambda p, k: (k, 0)))
    g_d = _wgrad("wgrad_down", act, dx2b, tok3(HID_S), tok2(D_MODEL), (N_SHARD, HID_S, D_MODEL),
                 pl.BlockSpec((None, HID_S, D_MODEL), lambda p, k: (p, 0, 0)), N_SHARD, S)
    g_g = _wgrad("wgrad_gate", h2, dgte, tok2(D_MODEL), tok3(HID_S), (N_SHARD, D_MODEL, HID_S),
                 pl.BlockSpec((None, D_MODEL, HID_S), lambda p, k: (p, 0, 0)), N_SHARD, S)
    g_u = _wgrad("wgrad_up", h2, dup, tok2(D_MODEL), tok3(HID_S), (N_SHARD, D_MODEL, HID_S),
                 pl.BlockSpec((None, D_MODEL, HID_S), lambda p, k: (p, 0, 0)), N_SHARD, S)
    comm.grads({4: g_g, 5: g_u, 6: g_d})
    (dx1, dx1b, dg2), xres = _ffn_up_bwd(dgte, dup, wg, wu, x1, g2, dx2, comm.carry("ffn_up_bwd"))
    comm.took("ffn_up_bwd", xres)
    dya, dyr, dga, dgrr = _out_proj_bwd(dx1b, wo, proj, ya, yr)
    colblk = lambda w: (lambda tk: pl.BlockSpec((tk, w), lambda p, k: (k, p)))
    g_o = _wgrad("wgrad_out", merged, dx1b, colblk(256), tok2(D_MODEL), (D_MODEL, D_MODEL),
                 pl.BlockSpec((256, D_MODEL), lambda p, k: (p, 0)), 4, S)
    datt, rho, dyrin = _branch_bwd(dya, dyr, wa, wr, att)
    g_a = _wgrad("wgrad_attn", att, dya, tok2(512), colblk(512), (512, D_MODEL),
                 pl.BlockSpec((512, 512), lambda p, k: (0, p)), 2, S)
    g_r = _wgrad("wgrad_ret", yrin, dyr, colblk(256), tok2(D_MODEL), (D_MODEL, D_MODEL),
                 pl.BlockSpec((256, D_MODEL), lambda p, k: (p, 0)), 4, S)
    comm.grads({1: g_a, 2: g_r.reshape(N_SHARD, 256, D_MODEL), 3: g_o.reshape(N_SHARD, 256, D_MODEL)})
    (dqr, dkr, dvr, dgr), xres = _ret_bwd(proj, rn, rstd, dyrin, states, tab, consts, comm.carry("ret_bwd"))
    comm.took("ret_bwd", xres)
    dqs, dks, dvs = [], [], []
    for gi, d in enumerate(DILATIONS):
        rtab = jnp.asarray(tab_np[0].reshape(3, S // d, d * 128))
        (dq, dk, dv), xres = _attn_bwd(qkvs[gi], _sub_view(datt, d), _sub_view(lse_tot, d), _sub_view(rho, d), rtab, d, gi,
                                       comm.carry(f"attn_bwd_g{gi}"))
        comm.took(f"attn_bwd_g{gi}", xres)
        dqs.append(dq)
        dks.append(dk)
        dvs.append(dv)
    dproj = _assemble_dproj((dqs, dks, dvs), dqr, dkr, dvr, dgr, dga, dgrr)
    g_sent, xres = _wgrad_in_half(ht, dproj, comm.sidx, False, comm.carry("wgrad_in_sent"))
    comm.took("wgrad_in_sent", xres)
    comm.grads({"in_sent": g_sent})
    g_kept, xres = _wgrad_in_half(ht, dproj, comm.sidx, True, comm.carry("wgrad_in_kept"))
    comm.grads({"in_kept": g_kept})
    comm.took("wgrad_in_kept", xres)
    grad_x, dg1, xres = _in_proj_bwd(dproj, w_in, x, g1, dx1, comm.carry("in_proj_bwd"))
    comm.took("in_proj_bwd", xres)
    return loss_p, grad_x, (dg1, dg2, dg3)


W_KINDS = ("col", "col", "lead", "lead", "lead", "lead", "lead")
W_SHARD = ((1024, W_IN_S), (512, 256), (256, 1024), (256, 1024), (1024, HID_S), (1024, HID_S), (HID_S, 1024))
N_W = len(W_KINDS)


def _full_shape(wi):
    R, C = W_SHARD[wi]
    return (R, N_SHARD * C) if W_KINDS[wi] == "col" else (N_SHARD, R, C)


def _view(ref, wi, s, half):
    R, C = W_SHARD[wi]
    rows = pl.ds(half * (R // 2), R // 2)
    if W_KINDS[wi] == "col":
        return ref.at[rows, pl.ds(pl.multiple_of(s * C, 128), C)]
    return ref.at[s, rows, :]


def _mesh_pos():
    x, y, c = lax.axis_index("x"), lax.axis_index("y"), lax.axis_index("c")
    chips = [(1 - x, y), (x, 1 - y), (1 - x, 1 - y)]
    return x, y, c, chips


def _cast_bf16(a):
    R, C = a.shape
    tr = R // 2 if R % 32 == 0 else R

    def body(a_ref, o_ref):
        o_ref[...] = a_ref[...].astype(BF16)

    spec = pl.BlockSpec((tr, C), lambda i: (i, 0))
    return pl.pallas_call(body, out_shape=SDS((R, C), BF16), grid=(R // tr,), in_specs=[spec], out_specs=spec,
                          compiler_params=_cparams("parallel"), name=f"cast_{R}x{C}")(a)


def _remote(send, recv, k, src, dst, to):
    return pltpu.make_async_remote_copy(src_ref=src, dst_ref=dst, send_sem=send.at[k], recv_sem=recv.at[k],
                                        device_id=to, device_id_type=MESH)


def _gather_now(wis, shards):
    n = len(wis)

    def body(*refs):
        sh, full = refs[:n], refs[n:2 * n]
        send, recv, loc = refs[2 * n:]
        x, y, c, _ = _mesh_pos()
        s_me, sib = 2 * x + y, (x, y, 1 - c)
        xn, yn = (1 - x, y), (x, 1 - y)
        flip = lambda a, b: a + b - 2 * a * b
        via = (flip(x, 1 - c), flip(y, c))
        onto = (flip(x, c), flip(y, 1 - c))
        shard_of = lambda chip: 2 * chip[0] + chip[1]
        own, started = [], []
        for i, wi in enumerate(wis):
            Rh = W_SHARD[wi][0] // 2
            for hf in range(2):
                cp = pltpu.make_async_copy(sh[i].at[pl.ds(hf * Rh, Rh), :], _view(full[i], wi, s_me, hf), loc.at[2 * i + hf])
                cp.start()
                own.append(cp)
            for j, chip in enumerate((xn, yn)):
                cp = _remote(send, recv, 6 * i + j, sh[i].at[pl.ds(c * Rh, Rh), :], _view(full[i], wi, s_me, c), (*chip, c))
                cp.start()
                started.append(cp)

        def pass_to_sibling(i, wi, k, s):
            mine = _view(full[i], wi, s, c)
            fw = _remote(send, recv, 6 * i + k, mine, mine, sib)
            fw.start()
            started.append(fw)

        for i, wi in enumerate(wis):
            for j, chip in enumerate((xn, yn)):
                land = _view(full[i], wi, shard_of(chip), c)
                _remote(send, recv, 6 * i + j, land, land, (*chip, c)).wait_recv()
                pass_to_sibling(i, wi, 3 + j, shard_of(chip))
            relay = _view(full[i], wi, shard_of(via), c)
            fw = _remote(send, recv, 6 * i + 2, relay, relay, (*onto, c))
            fw.start()
            started.append(fw)
        s_diag = 2 * (1 - x) + (1 - y)
        for i, wi in enumerate(wis):
            land = _view(full[i], wi, s_diag, c)
            _remote(send, recv, 6 * i + 2, land, land, (*onto, c)).wait_recv()
            pass_to_sibling(i, wi, 5, s_diag)
        for i, wi in enumerate(wis):
            for k, s in ((3, shard_of(xn)), (4, shard_of(yn)), (5, s_diag)):
                land = _view(full[i], wi, s, 1 - c)
                _remote(send, recv, 6 * i + k, land, land, sib).wait_recv()
        for cp in started:
            cp.wait_send()
        for cp in own:
            cp.wait()

    return pl.pallas_call(
        body, out_shape=tuple(SDS(_full_shape(wi), BF16) for wi in wis),
        in_specs=[ANY] * n, out_specs=tuple([ANY] * n),
        scratch_shapes=[pltpu.SemaphoreType.DMA((6 * n,)), pltpu.SemaphoreType.DMA((6 * n,)),
                        pltpu.SemaphoreType.DMA((2 * n,))],
        name="gather_now")(*shards)


def _ex_gather_ici(wis, shards, then_d2d=False):
    n = len(wis)

    def build(ins, outs, send, recv, loc):
        x, y, c, chips = _mesh_pos()
        s_me, sib = 2 * x + y, (x, y, 1 - c)
        starts, waits, after = [], [], []
        for i, wi in enumerate(wis):
            Rh = W_SHARD[wi][0] // 2
            for hf in range(2):
                cp = pltpu.make_async_copy(ins[i].at[pl.ds(hf * Rh, Rh), :], _view(outs[i], wi, s_me, hf), loc.at[2 * i + hf])
                starts.append(cp)
                waits.append(cp.wait)
            for j, chip in enumerate(chips):
                cp = _remote(send, recv, 3 * i + j, ins[i].at[pl.ds(c * Rh, Rh), :], _view(outs[i], wi, s_me, c), (*chip, c))
                land = _view(outs[i], wi, 2 * chip[0] + chip[1], c)
                starts.append(cp)
                waits += [cp.wait_send, _remote(send, recv, 3 * i + j, land, land, (*chip, c)).wait_recv]
                if then_d2d:
                    theirs = _view(outs[i], wi, 2 * chip[0] + chip[1], 1 - c)
                    fw = _remote(send, recv, 3 * n + 3 * i + j, land, land, sib)
                    waits.append(fw.start)
                    after += [fw.wait_send, _remote(send, recv, 3 * n + 3 * i + j, theirs, theirs, sib).wait_recv]
        return starts, waits + after

    return _Exchange(shards, [SDS(_full_shape(wi), BF16) for wi in wis], {}, (6 if then_d2d else 3) * n, 2 * n, build)


def _ex_gather_d2d(wis, fulls):
    def build(ins, outs, send, recv, loc):
        x, y, c, chips = _mesh_pos()
        sib = (x, y, 1 - c)
        starts, waits = [], []
        for i, wi in enumerate(wis):
            for j, chip in enumerate(chips):
                mine = _view(outs[i], wi, 2 * chip[0] + chip[1], c)
                theirs = _view(outs[i], wi, 2 * chip[0] + chip[1], 1 - c)
                cp = _remote(send, recv, 3 * i + j, mine, mine, sib)
                starts.append(cp)
                waits += [cp.wait_send, _remote(send, recv, 3 * i + j, theirs, theirs, sib).wait_recv]
        return starts, waits

    return _Exchange(fulls, [SDS(f.shape, BF16) for f in fulls], {i: i for i in range(len(wis))}, 3 * len(wis), 0, build)


def _half_shape(wi):
    R, C = W_SHARD[wi]
    return (R // 2, N_SHARD * C) if W_KINDS[wi] == "col" else (N_SHARD, R // 2, C)


def _ex_pair(wis, grads):
    def build(ins, outs, send, recv, loc):
        x, y, c, _ = _mesh_pos()
        starts, waits = [], []
        for i, wi in enumerate(wis):
            Rh = W_SHARD[wi][0] // 2
            rows = pl.ds((1 - c) * Rh, Rh)
            if tuple(ins[i].shape) == _half_shape(wi):
                src = ins[i]
            else:
                src = ins[i].at[rows, :] if W_KINDS[wi] == "col" else ins[i].at[:, rows, :]
            cp = _remote(send, recv, i, src, outs[i], (x, y, 1 - c))
            starts.append(cp)
            waits.append(cp.wait)
        return starts, waits

    return _Exchange(grads, [SDS(_half_shape(wi), F32) for wi in wis], {}, len(wis), 0, build)


def _ex_chip(wis, pbs):
    def build(ins, outs, send, recv, loc):
        x, y, c, chips = _mesh_pos()
        starts, waits = [], []
        for i, wi in enumerate(wis):
            for j, chip in enumerate(chips):
                cp = _remote(send, recv, 3 * i + j, ins[i].at[j], outs[i].at[j], (*chip, c))
                starts.append(cp)
                waits.append(cp.wait)
        return starts, waits

    shapes = [SDS((3, W_SHARD[wi][0] // 2, W_SHARD[wi][1]), BF16) for wi in wis]
    return _Exchange(pbs, shapes, {}, 3 * len(wis), 0, build)


def _ex_share(wis, halves):
    def build(ins, outs, send, recv, loc):
        x, y, c, _ = _mesh_pos()
        sib = (x, y, 1 - c)
        starts, waits = [], []
        for i, wi in enumerate(wis):
            cp = _remote(send, recv, i, outs[i].at[c], outs[i].at[c], sib)
            starts.append(cp)
            waits += [cp.wait_send, _remote(send, recv, i, outs[i].at[1 - c], outs[i].at[1 - c], sib).wait_recv]
        return starts, waits

    return _Exchange(halves, [SDS(h.shape, F32) for h in halves], {i: i for i in range(len(wis))}, len(wis), 0, build)


def _row_tile(rh, C):
    best = 16
    for t in range(16, rh + 1, 16):
        if rh % t == 0 and t * C * 4 <= (3 << 19):
            best = t
    return best


def _pair_sum(wi, g, ra, sidx):
    R, C = W_SHARD[wi]
    Rh = R // 2
    tr = _row_tile(Rh, C)
    nt = Rh // tr
    off = 0 if tuple(g.shape) == _half_shape(wi) else nt
    col = W_KINDS[wi] == "col"

    def body(sidx_ref, *refs):
        gs, rs = refs[:4], refs[4:8]
        own_ref, pb_ref = refs[8:]
        own_ref[...] = gs[0][...] + rs[0][...]
        for j in range(3):
            pb_ref[j] = (gs[1 + j][...] + rs[1 + j][...]).astype(BF16)

    def gspec(slot):
        if col:
            return pl.BlockSpec((tr, C), lambda i, sx: (sx[4] * off + i, sx[slot]))
        return pl.BlockSpec((None, tr, C), lambda i, sx: (sx[slot], sx[4] * off + i, 0))

    def rspec(slot):
        if col:
            return pl.BlockSpec((tr, C), lambda i, sx: (i, sx[slot]))
        return pl.BlockSpec((None, tr, C), lambda i, sx: (sx[slot], i, 0))

    return pl.pallas_call(
        body, out_shape=(SDS((Rh, C), F32), SDS((3, Rh, C), BF16)),
        grid_spec=pltpu.PrefetchScalarGridSpec(
            num_scalar_prefetch=1, grid=(nt,),
            in_specs=[gspec(k) for k in range(4)] + [rspec(k) for k in range(4)],
            out_specs=(pl.BlockSpec((tr, C), lambda i, sx: (i, 0)), pl.BlockSpec((3, tr, C), lambda i, sx: (0, i, 0)))),
        compiler_params=_cparams("arbitrary"), name=f"pair_sum_w{wi}")(sidx, g, g, g, g, ra, ra, ra, ra)


def _chip_sum(wi, own, rb, sidx):
    R, C = W_SHARD[wi]
    Rh = R // 2
    tr = _row_tile(Rh, C)

    def body(sidx_ref, own_ref, rb_ref, o_ref):
        o_ref[...] = ((own_ref[...] + rb_ref[0].astype(F32)) + rb_ref[1].astype(F32)) + rb_ref[2].astype(F32)

    return pl.pallas_call(
        body, out_shape=SDS((2, Rh, C), F32),
        grid_spec=pltpu.PrefetchScalarGridSpec(
            num_scalar_prefetch=1, grid=(Rh // tr,),
            in_specs=[pl.BlockSpec((tr, C), lambda i, sx: (i, 0)), pl.BlockSpec((3, tr, C), lambda i, sx: (0, i, 0))],
            out_specs=pl.BlockSpec((None, tr, C), lambda i, sx: (sx[4], i, 0))),
        compiler_params=_cparams("arbitrary"), name=f"chip_sum_w{wi}")(sidx, own, rb)


def _gain_allgather(blk):
    m_per, n = blk.shape

    def body(x_ref, out_ref, send_sems, recv_sems, local_sem):
        x, y, c, chips = _mesh_pos()
        me, sibling = (x, y, c), (x, y, 1 - c)

        def rows(px, py, pc):
            return out_ref.at[pl.ds((4 * px + 2 * py + pc) * m_per, m_per), :]

        def copy(k, block, to, src=None):
            return pltpu.make_async_remote_copy(
                src_ref=rows(*block) if src is None else src, dst_ref=rows(*block),
                send_sem=send_sems.at[k], recv_sem=recv_sems.at[k], device_id=to, device_id_type=MESH)

        mine = pltpu.make_async_copy(x_ref, rows(*me), local_sem)
        mine.start()
        first = [copy(0, me, sibling, src=x_ref)]
        first += [copy(1 + j, me, (*chip, c), src=x_ref) for j, chip in enumerate(chips)]
        for cp in first:
            cp.start()
        passed = [copy(4 + j, (*chip, c), sibling) for j, chip in enumerate(chips)]
        for j, chip in enumerate(chips):
            copy(1 + j, (*chip, c), me).wait_recv()
            passed[j].start()
        copy(0, sibling, me).wait_recv()
        for j, chip in enumerate(chips):
            copy(4 + j, (*chip, 1 - c), me).wait_recv()
        for cp in first + passed:
            cp.wait_send()
        mine.wait()

    vm = pl.BlockSpec(memory_space=pltpu.VMEM)
    return pl.pallas_call(
        body, out_shape=SDS((8 * m_per, n), blk.dtype), in_specs=[vm], out_specs=vm,
        scratch_shapes=[pltpu.SemaphoreType.DMA((7,)), pltpu.SemaphoreType.DMA((7,)), pltpu.SemaphoreType.DMA],
        name="gain_allgather")(blk)


def _adam_math(w, g, m, v):
    mn = ADAM_B1 * m + (1.0 - ADAM_B1) * g
    vn = ADAM_B2 * v + (1.0 - ADAM_B2) * (g * g)
    mh = mn / (1.0 - ADAM_B1 ** ADAM_STEP)
    vh = vn / (1.0 - ADAM_B2 ** ADAM_STEP)
    return -ADAM_LR * (mh / (jnp.sqrt(vh) + ADAM_EPS) + ADAM_WD * w), mn, vn


def _adamw(wi, w, g, m, v):
    R, C = w.shape
    tr = _row_tile(R, C)

    def body(w_ref, g_ref, m_ref, v_ref, d_ref, mn_ref, vn_ref):
        d_ref[...], mn_ref[...], vn_ref[...] = _adam_math(w_ref[...], g_ref[...], m_ref[...], v_ref[...])

    spec = pl.BlockSpec((tr, C), lambda i: (i, 0))
    return pl.pallas_call(body, out_shape=(SDS((R, C), F32),) * 3, grid=(R // tr,), in_specs=[spec] * 4,
                          out_specs=(spec,) * 3, compiler_params=_cparams("parallel"), name=f"adamw_w{wi}")(w, g, m, v)


def _gain_update(gathered, w, m, v):
    def body(ga_ref, w_ref, m_ref, v_ref, g_ref, d_ref, mn_ref, vn_ref):
        g = ga_ref[0:8, :]
        for dev in range(1, 8):
            g = g + ga_ref[8 * dev:8 * dev + 8, :]
        g_ref[...] = g
        d_ref[...], mn_ref[...], vn_ref[...] = _adam_math(w_ref[...], g, m_ref[...], v_ref[...])

    return pl.pallas_call(body, out_shape=(SDS((8, 1024), F32),) * 4, name="gain_update")(gathered, w, m, v)


GROUP_FFN, GROUP_MIX, GROUP_IN = (4, 5, 6), (1, 2, 3), (0,)
REST = GROUP_MIX + GROUP_FFN


class _MeshComm:
    SCHEDULE = {
        "in_proj": [("ici", (1, 2, 3, 4))],
        "attn_fwd_g0": [("d2d", (1, 2, 3, 4))],
        "ret_fwd": [("ici", (5,))],
        "out_proj": [("d2d", (5,))],
        "ffn_up": [("both", (6,))],
        "ffn_up_bwd": [("pair", GROUP_FFN)],
        "ret_bwd": [("pair", GROUP_MIX), ("chip", (4,))],
        "attn_bwd_g0": [("chip", (5,))],
        "attn_bwd_g1": [("chip", (6,))],
        "attn_bwd_g2": [("chip", GROUP_MIX)],
        "wgrad_in_sent": [("share", GROUP_FFN + GROUP_MIX)],
        "wgrad_in_kept": [("pair", GROUP_IN)],
        "in_proj_bwd": [("chip", GROUP_IN)],
    }

    def __init__(self, shards):
        xi, yi, ci = lax.axis_index("x"), lax.axis_index("y"), lax.axis_index("c")
        self.sidx = jnp.stack([2 * xi + yi, 2 * (1 - xi) + yi, 2 * xi + (1 - yi), 2 * (1 - xi) + (1 - yi), ci]).astype(jnp.int32)
        self.shards, self.full = shards, {}
        self.g, self.own, self.pb, self.half, self.red = {}, {}, {}, {}, {}

    def w_in(self):
        return _gather_now(GROUP_IN, [self.shards[0]])[0]

    def weight(self, wi):
        return self.full[wi].reshape(D_MODEL, D_MODEL) if wi in (2, 3) else self.full[wi]

    def grads(self, by_wi):
        self.g.update(by_wi)

    def _exchange(self, stage, wis):
        pick = lambda table: [table[wi] for wi in wis]
        if stage == "ici":
            return _ex_gather_ici(wis, pick(self.shards))
        if stage == "both":
            return _ex_gather_ici(wis, pick(self.shards), then_d2d=True)
        if stage == "d2d":
            return _ex_gather_d2d(wis, pick(self.full))
        if stage == "pair":
            return _ex_pair(wis, [self.g["in_sent"] if wi == 0 else self.g[wi] for wi in wis])
        if stage == "chip":
            return _ex_chip(wis, pick(self.pb))
        return _ex_share(wis, pick(self.half))

    def _landed(self, stage, wis, res):
        for wi, r in zip(wis, res):
            if stage in ("ici", "d2d", "both"):
                self.full[wi] = r
            elif stage == "pair":
                self.own[wi], self.pb[wi] = _pair_sum(wi, self.g["in_kept"] if wi == 0 else self.g[wi], r, self.sidx)
            elif stage == "chip":
                self.half[wi] = _chip_sum(wi, self.own[wi], r, self.sidx)
            else:
                self.red[wi] = r

    def carry(self, point):
        return [self._exchange(stage, wis) for stage, wis in self.SCHEDULE.get(point, ())]

    def took(self, point, xres):
        for (stage, wis), res in zip(self.SCHEDULE.get(point, ()), xres):
            self._landed(stage, wis, res)

    def reduced(self):
        self._landed("share", GROUP_IN, _exchange_call(self._exchange("share", GROUP_IN), "share_w_in"))
        return [self.red[wi] for wi in range(N_W)]


def kernel(x, norm_mix_g, w_in, w_out_attn, w_out_ret, w_out, norm_ffn_g, w_ffn_gate, w_ffn_up, w_ffn_down, norm_final_g, loss_target, m_norm_mix_g, m_w_in, m_w_out_attn, m_w_out_ret, m_w_out, m_norm_ffn_g, m_w_ffn_gate, m_w_ffn_up, m_w_ffn_down, m_norm_final_g, v_norm_mix_g, v_w_in, v_w_out_attn, v_w_out_ret, v_w_out, v_norm_ffn_g, v_w_ffn_gate, v_w_ffn_up, v_w_ffn_down, v_norm_final_g):
    ws = (w_in, w_out_attn, w_out_ret, w_out, w_ffn_gate, w_ffn_up, w_ffn_down)
    ms = (m_w_in, m_w_out_attn, m_w_out_ret, m_w_out, m_w_ffn_gate, m_w_ffn_up, m_w_ffn_down)
    vs = (v_w_in, v_w_out_attn, v_w_out_ret, v_w_out, v_w_ffn_gate, v_w_ffn_up, v_w_ffn_down)
    shard2d = lambda a, wi: a.reshape(W_SHARD[wi])

    comm = _MeshComm([_cast_bf16(shard2d(w, wi)) for wi, w in enumerate(ws)])
    g3 = norm_final_g.reshape(1, D_MODEL)
    loss_p, grad_x, gain_g = _step(x[0], loss_target[0], norm_mix_g, norm_ffn_g, g3, comm)
    gred = comm.reduced()

    outs_g, outs_d, outs_m, outs_v = [], [], [], []
    for wi in range(N_W):
        g2d = gred[wi].reshape(W_SHARD[wi])
        dlt, mn, vn = _adamw(wi, shard2d(ws[wi], wi), g2d, shard2d(ms[wi], wi), shard2d(vs[wi], wi))
        for lst, a in ((outs_g, g2d), (outs_d, dlt), (outs_m, mn), (outs_v, vn)):
            lst.append(a.reshape(ws[wi].shape))

    pad8 = lambda rows: jnp.concatenate([r.reshape(1, D_MODEL) for r in rows]
                                        + [jnp.zeros((8 - len(rows), D_MODEL), F32)], axis=0)
    gathered = _gain_allgather(pad8((*gain_g, jnp.tile(loss_p[0:1], (1, D_MODEL // 128)))))
    gg, gd, gm, gv = _gain_update(gathered, pad8((norm_mix_g, norm_ffn_g, norm_final_g)),
                                  pad8((m_norm_mix_g, m_norm_ffn_g, m_norm_final_g)),
                                  pad8((v_norm_mix_g, v_norm_ffn_g, v_norm_final_g)))
    loss = gg[3, 0]

    def assemble(gain_rows, wlist):
        return (gain_rows[0:1], wlist[0], wlist[1], wlist[2], wlist[3], gain_rows[1:2],
                wlist[4], wlist[5], wlist[6], gain_rows[2])

    return (loss, grad_x[None], *assemble(gg, outs_g), *assemble(gd, outs_d), *assemble(gm, outs_m), *assemble(gv, outs_v))
```

```python
import functools
import math

import numpy as np
import jax
import jax.numpy as jnp
from jax import lax
from jax.experimental import pallas as pl
from jax.experimental.pallas import tpu as pltpu

F32, BF16 = jnp.float32, jnp.bfloat16
SDS = jax.ShapeDtypeStruct
MESH = pl.DeviceIdType.MESH

D_MODEL = 1024
PROJ_W = 9728
COLB = 512
N_COLB = PROJ_W // COLB
QA_B, KA_B, VA_B = 0, 3, 6
QR_B, KR_B = 9, 10
FFN_HID = 2816
N_SHARD = 4
HID_S = FFN_HID // N_SHARD
W_IN_S = PROJ_W // N_SHARD
DILATIONS = (1, 4, 16)
BLK = 128
RET_HEADS = 4
ROPE_THETA = 10000.0
NORM_EPS = 1e-6
ADAM_LR, ADAM_B1, ADAM_B2, ADAM_EPS, ADAM_WD, ADAM_STEP = 0.001, 0.9, 0.999, 1e-08, 0.01, 10
VMEM_LIMIT = 56 << 20


def _cparams(*sem):
    return pltpu.CompilerParams(dimension_semantics=sem or None, vmem_limit_bytes=VMEM_LIMIT)


def _dot(a, b):
    return jnp.dot(a, b, preferred_element_type=F32)


def _dot_nt(a, b):
    return lax.dot_general(a, b, (((1,), (1,)), ((), ())), preferred_element_type=F32)


def _dot_tn(a, b):
    return lax.dot_general(a, b, (((0,), (0,)), ((), ())), preferred_element_type=F32)


def _row_pieces(tm, sub=512):
    return [slice(i, i + sub) for i in range(0, tm, sub)]


def _sigmoid(z):
    return 0.5 * jnp.tanh(0.5 * z) + 0.5


ANY = pl.BlockSpec(memory_space=pl.ANY)


class _Exchange:
    def __init__(self, ins, out_shapes, aliases, n_sem, n_loc, build):
        self.ins, self.out_shapes, self.aliases = list(ins), list(out_shapes), dict(aliases)
        self.n_sem, self.n_loc, self.build = n_sem, n_loc, build

    def sems(self):
        return [pltpu.SemaphoreType.DMA((self.n_sem,)), pltpu.SemaphoreType.DMA((self.n_sem,)),
                pltpu.SemaphoreType.DMA((max(self.n_loc, 1),))]


def _exchange_call(ex, name):
    n_in, n_out = len(ex.ins), len(ex.out_shapes)

    def body(*refs):
        starts, waits = ex.build(refs[:n_in], refs[n_in:n_in + n_out], *refs[n_in + n_out:])
        for cp in starts:
            cp.start()
        for w in waits:
            w()

    return pl.pallas_call(body, out_shape=tuple(ex.out_shapes), in_specs=[ANY] * n_in, out_specs=tuple([ANY] * n_out),
                          input_output_aliases=ex.aliases, scratch_shapes=ex.sems(), name=name)(*ex.ins)


def _carrier_call(body, args, *, out_shape, grid, in_specs, out_specs, scratch_shapes=(), sem, name, exchanges=(),
                  prefetch=None, in_out_aliases=None):
    out_shape, out_specs = tuple(out_shape), tuple(out_specs)
    n_in, n_out, n_scr = len(args), len(out_shape), len(scratch_shapes)
    n_pre = 0 if prefetch is None else 1
    x_args, x_outs, x_scr, spans = [], [], [], []
    aliases = {n_pre + a: o for a, o in (in_out_aliases or {}).items()}
    for ex in exchanges:
        i0, o0 = len(x_args), len(x_outs)
        for a, o in ex.aliases.items():
            aliases[n_pre + n_in + i0 + a] = n_out + o0 + o
        x_args += ex.ins
        x_outs += ex.out_shapes
        x_scr += ex.sems()
        spans.append((i0, len(ex.ins), o0, len(ex.out_shapes)))
    nx_in, nx_out = len(x_args), len(x_outs)

    def wrapped(*refs):
        refs = refs[n_pre:]
        ins, xin = refs[:n_in], refs[n_in:n_in + nx_in]
        o_base = n_in + nx_in
        outs, xout = refs[o_base:o_base + n_out], refs[o_base + n_out:o_base + n_out + nx_out]
        s_base = o_base + n_out + nx_out
        scr, xs = refs[s_base:s_base + n_scr], refs[s_base + n_scr:]

        def built(e):
            i0, ni, o0, no = spans[e]
            return exchanges[e].build(xin[i0:i0 + ni], xout[o0:o0 + no], *xs[3 * e:3 * e + 3])

        if exchanges:
            first = functools.reduce(jnp.logical_and, [pl.program_id(k) == 0 for k in range(len(grid))])
            last = functools.reduce(jnp.logical_and, [pl.program_id(k) == grid[k] - 1 for k in range(len(grid))])

            @pl.when(first)
            def _():
                for e in range(len(exchanges)):
                    for cp in built(e)[0]:
                        cp.start()

        body(*ins, *outs, *scr)

        if exchanges:
            @pl.when(last)
            def _():
                for e in range(len(exchanges)):
                    for w in built(e)[1]:
                        w()

    all_in, all_out = list(in_specs) + [ANY] * nx_in, out_specs + tuple([ANY] * nx_out)
    all_scr = list(scratch_shapes) + x_scr
    cparams = _cparams(*(sem if not exchanges else ("arbitrary",) * len(grid)))
    if prefetch is None:
        res = pl.pallas_call(wrapped, out_shape=out_shape + tuple(x_outs), grid=grid, in_specs=all_in, out_specs=all_out,
                             scratch_shapes=all_scr, input_output_aliases=aliases, compiler_params=cparams,
                             name=name)(*args, *x_args)
    else:
        gs = pltpu.PrefetchScalarGridSpec(num_scalar_prefetch=1, grid=grid, in_specs=all_in, out_specs=all_out,
                                          scratch_shapes=all_scr)
        res = pl.pallas_call(wrapped, out_shape=out_shape + tuple(x_outs), grid_spec=gs, input_output_aliases=aliases,
                             compiler_params=cparams, name=name)(prefetch, *args, *x_args)
    xres = [tuple(res[n_out + o0:n_out + o0 + no]) for (_, _, o0, no) in spans]
    return tuple(res[:n_out]), xres


def _tables(S):
    f32 = np.float32
    pos = np.arange(S, dtype=f32)
    lane = np.arange(128)
    inv = (f32(ROPE_THETA) ** (-np.arange(0, 64, 2, dtype=f32) / f32(64))).astype(f32)
    ang = (pos[:, None] * inv[None, :]).astype(np.float64)
    idx = (lane % 64) % 32
    c, s = np.cos(ang)[:, idx], np.sin(ang)[:, idx]
    first = ((lane % 64) < 32)[None, :]
    rope = np.stack([c, np.where(first, 0.0, s), np.where(first, -s, 0.0)])
    base = (f32(1.0) / (f32(ROPE_THETA) ** np.linspace(0.0, 1.0, 64, dtype=f32))).astype(f32)
    ang2 = (pos[:, None] * base[None, :]).astype(np.float64)
    c2, s2 = np.cos(ang2)[:, lane // 2], np.sin(ang2)[:, lane // 2]
    even = (lane % 2 == 0)[None, :]
    th = np.stack([c2, np.where(even, 0.0, s2), np.where(even, -s2, 0.0)])
    return np.stack([rope, th, th * (128 ** -0.5)]).astype(f32)


def _rot(a, c, sa, sb, shift):
    return a * c + pltpu.roll(a, shift, 1) * sa + pltpu.roll(a, 128 - shift, 1) * sb


def _unrot(g, c, sa, sb, shift):
    return g * c + pltpu.roll(g * sa, 128 - shift, 1) + pltpu.roll(g * sb, shift, 1)


def _ret_consts():
    h = np.arange(RET_HEADS, dtype=np.float64)
    log_g = np.log1p(-(2.0 ** (-5.0 - h)))
    idx = np.arange(BLK, dtype=np.float64)
    diff = idx[:, None] - idx[None, :]
    dmask = np.where(diff[None] >= 0, np.exp(np.maximum(diff, 0.0)[None] * log_g[:, None, None]), 0.0)
    zeta = np.exp((BLK - 1 - idx)[None, :] * log_g[:, None])
    xi = np.exp((idx + 1.0)[None, :] * log_g[:, None])
    dec = np.exp(BLK * log_g)
    rep = lambda v: np.broadcast_to(v[:, :, None], (RET_HEADS, BLK, 128))
    return (jnp.asarray(dmask, F32), jnp.asarray(rep(zeta), F32), jnp.asarray(rep(xi), F32),
            jnp.asarray(np.broadcast_to(dec[:, None, None], (RET_HEADS, 8, 256)), F32))


def _rms_fwd(x, g):
    S = x.shape[0]
    tm = 512

    def body(x_ref, g_ref, h_ref, ht_ref):
        xv = x_ref[...]
        r = lax.rsqrt(jnp.mean(xv * xv, axis=-1, keepdims=True) + NORM_EPS)
        h = xv * r * g_ref[...]
        h_ref[...] = h.astype(BF16)
        ht_ref[...] = h.T.astype(BF16)

    return pl.pallas_call(
        body, out_shape=(SDS((S, D_MODEL), BF16), SDS((D_MODEL, S), BF16)), grid=(S // tm,),
        in_specs=[pl.BlockSpec((tm, D_MODEL), lambda i: (i, 0)), pl.BlockSpec((1, D_MODEL), lambda i: (0, 0))],
        out_specs=(pl.BlockSpec((tm, D_MODEL), lambda i: (i, 0)), pl.BlockSpec((D_MODEL, tm), lambda i: (0, i))),
        compiler_params=_cparams("parallel"), name="rms_fwd")(x, g)


def _in_proj(h, w_in, tab, exchanges=()):
    S = h.shape[0]
    tm = min(S, 2048)

    def body(h_ref, w_ref, t_ref, o_ref):
        j = pl.program_id(1)
        is_rope = j < 6
        is_theta = (j == QR_B) | (j == KR_B)
        sub = 512

        def rotated(shift):
            for i in range(tm // sub):
                rows = slice(i * sub, (i + 1) * sub)
                acc = _dot(h_ref[rows, :], w_ref[...])
                c, sa, sb = t_ref[0, 0, rows, :], t_ref[0, 1, rows, :], t_ref[0, 2, rows, :]
                for k in range(COLB // 128):
                    sl = slice(k * 128, (k + 1) * 128)
                    o_ref[rows, sl] = _rot(acc[:, sl], c, sa, sb, shift).astype(BF16)

        @pl.when(is_rope)
        def _():
            rotated(32)

        @pl.when(is_theta)
        def _():
            rotated(1)

        @pl.when(jnp.logical_not(is_rope | is_theta))
        def _():
            o_ref[...] = _dot(h_ref[...], w_ref[...]).astype(BF16)

    def tab_map(i, j):
        return (jnp.where(j == QR_B, 1, jnp.where(j == KR_B, 2, 0)), 0, i, 0)

    (proj,), xres = _carrier_call(
        body, (h, w_in, tab), out_shape=(SDS((S, PROJ_W), BF16),), grid=(S // tm, N_COLB),
        in_specs=[pl.BlockSpec((tm, D_MODEL), lambda i, j: (i, 0)),
                  pl.BlockSpec((D_MODEL, COLB), lambda i, j: (0, j)),
                  pl.BlockSpec((1, 3, tm, 128), tab_map)],
        out_specs=(pl.BlockSpec((tm, COLB), lambda i, j: (i, j)),),
        sem=("parallel", "arbitrary"), name="in_proj", exchanges=exchanges)
    return proj, xres


def _band_mask(n):
    qi = lax.broadcasted_iota(jnp.int32, (BLK, 2 * BLK), 0)
    kj = lax.broadcasted_iota(jnp.int32, (BLK, 2 * BLK), 1)
    dist = BLK + qi - kj
    return (dist >= 0) & (dist <= BLK) & ((kj >= BLK) | (n > 0))


def _qkv_col(d, gi):
    if d == 1:
        return lambda t, r: 3 * t + gi
    return lambda t, r: 3 * r + t


def _attn_fwd(qkv, d, gi, exchanges=()):
    L = qkv.shape[0]
    nb = L // BLK

    def body(q_ref, kc_ref, kp_ref, vc_ref, vp_ref, o_ref, lse_ref):
        n = pl.program_id(1)
        mask = _band_mask(n)
        mask2 = jnp.concatenate([mask, mask], axis=0)
        lane = lax.broadcasted_iota(jnp.int32, (BLK, 128), 1)
        lo = lane < 64
        lse_all = jnp.zeros((BLK, 128), F32)
        chunks = [slice(c * 128, (c + 1) * 128) for c in range(4)]
        scores, vals = [], []
        for sl in chunks:
            q = q_ref[:, sl]
            k = jnp.concatenate([kp_ref[:, sl], kc_ref[:, sl]], axis=0)
            vals.append(jnp.concatenate([vp_ref[:, sl], vc_ref[:, sl]], axis=0))
            q2 = jnp.concatenate([jnp.where(lo, q, jnp.zeros_like(q)), jnp.where(lo, jnp.zeros_like(q), q)], axis=0)
            scores.append(_dot_nt(q2, k))
        probs = []
        for c, s in enumerate(scores):
            s = jnp.where(mask2, s * 0.125, jnp.float32(-1e30))
            m = jnp.max(s, axis=-1, keepdims=True)
            p = jnp.exp(s - m)
            l = jnp.sum(p, axis=-1, keepdims=True)
            probs.append((p / l).astype(BF16))
            lse = m + jnp.log(l)
            lse_all = jnp.where(lane // 16 == 2 * c, lse[:BLK], jnp.where(lane // 16 == 2 * c + 1, lse[BLK:], lse_all))
        for sl, p, v in zip(chunks, probs, vals):
            o2 = _dot(p, v)
            o_ref[:, sl] = jnp.where(lo, o2[:BLK], o2[BLK:])
        lse_ref[...] = lse_all

    prev = lambda n: jnp.maximum(n - 1, 0)
    col = _qkv_col(d, gi)
    return _carrier_call(
        body, (qkv,) * 5, out_shape=(SDS((L, d * 512), F32), SDS((L, d * 128), F32)), grid=(d, nb),
        in_specs=[pl.BlockSpec((BLK, 512), lambda r, n: (n, col(0, r))),
                  pl.BlockSpec((BLK, 512), lambda r, n: (n, col(1, r))),
                  pl.BlockSpec((BLK, 512), lambda r, n: (prev(n), col(1, r))),
                  pl.BlockSpec((BLK, 512), lambda r, n: (n, col(2, r))),
                  pl.BlockSpec((BLK, 512), lambda r, n: (prev(n), col(2, r)))],
        out_specs=(pl.BlockSpec((BLK, 512), lambda r, n: (n, r)),
                   pl.BlockSpec((BLK, 128), lambda r, n: (n, r))),
        sem=("parallel", "arbitrary"), name=f"attn_fwd_g{gi}", exchanges=exchanges)


def _qkv_to_sub(proj, d, gi):
    S = proj.shape[0]
    tm = 512
    n = tm // d

    def body(q_ref, k_ref, v_ref, o_ref, scr):
        for t, ref in enumerate((q_ref, k_ref, v_ref)):
            for c in range(4):
                scr[c] = ref[:, c * 128:(c + 1) * 128].astype(F32)
            for r in range(d):
                for c in range(4):
                    col = (3 * r + t) * 512 + c * 128
                    o_ref[:, col:col + 128] = scr[c, pl.ds(r, n, stride=d), :].astype(BF16)

    return pl.pallas_call(
        body, out_shape=SDS((S // d, d * 1536), BF16), grid=(S // tm,),
        in_specs=[pl.BlockSpec((tm, 512), lambda i, b=b: (i, b + gi)) for b in (QA_B, KA_B, VA_B)],
        out_specs=pl.BlockSpec((n, d * 1536), lambda i: (i, 0)),
        scratch_shapes=[pltpu.VMEM((4, tm, 128), F32)],
        compiler_params=_cparams("parallel"), name=f"qkv_to_sub_g{gi}")(proj, proj, proj)


def _attn_merge(os_, lses):
    S = os_[0].shape[0]
    tm = 512

    def body(o0, o1, o2, l0, l1, l2, att_ref, lt_ref, so1, so2, sl1, sl2):
        lo = lax.broadcasted_iota(jnp.int32, (tm, 128), 1) < 64

        def natural(ref, d, scr, width):
            nch = width // 128
            if d == 1:
                return [ref[:, c * 128:(c + 1) * 128] for c in range(nch)]
            for r in range(d):
                for c in range(nch):
                    scr[c, pl.ds(r, tm // d, stride=d), :] = ref[:, r * width + c * 128:r * width + (c + 1) * 128]
            return [scr[c] for c in range(nch)]

        ls = [natural(l, d, s, 128)[0] for l, d, s in zip((l0, l1, l2), DILATIONS, (None, sl1, sl2))]
        m = jnp.maximum(jnp.maximum(ls[0], ls[1]), ls[2])
        es = [jnp.exp(v - m) for v in ls]
        z = es[0] + es[1] + es[2]
        lt_ref[...] = m + jnp.log(z)
        ws = [e / z for e in es]
        o_nat = [natural(o, d, s, 512) for o, d, s in zip((o0, o1, o2), DILATIONS, (None, so1, so2))]
        for c in range(4):
            acc = jnp.zeros((tm, 128), F32)
            for g in range(3):
                w_lo = jnp.broadcast_to(ws[g][:, 32 * c:32 * c + 1], (tm, 128))
                w_hi = jnp.broadcast_to(ws[g][:, 32 * c + 16:32 * c + 17], (tm, 128))
                acc = acc + jnp.where(lo, w_lo, w_hi) * o_nat[g][c]
            att_ref[:, c * 128:(c + 1) * 128] = acc.astype(BF16)

    sub = lambda w: [pl.BlockSpec((tm // d, d * w), lambda i: (i, 0)) for d in DILATIONS]
    return pl.pallas_call(
        body, out_shape=(SDS((S, 512), BF16), SDS((S, 128), F32)), grid=(S // tm,),
        in_specs=sub(512) + sub(128),
        out_specs=(pl.BlockSpec((tm, 512), lambda i: (i, 0)), pl.BlockSpec((tm, 128), lambda i: (i, 0))),
        scratch_shapes=[pltpu.VMEM((4, tm, 128), F32), pltpu.VMEM((4, tm, 128), F32),
                        pltpu.VMEM((1, tm, 128), F32), pltpu.VMEM((1, tm, 128), F32)],
        compiler_params=_cparams("parallel"), name="attn_merge")(*os_, *lses)


def _assemble_dproj(att_grads, dproj):
    S = dproj.shape[0]
    tm = 256

    def body(*refs):
        a = [refs[3 * t:3 * t + 3] for t in range(3)]
        dp_prev, o_ref, scr = refs[9:]
        for t in range(3):
            for g, d in enumerate(DILATIONS):
                base = (3 * t + g) * COLB
                if d == 1:
                    o_ref[:, base:base + COLB] = a[t][g][...]
                    continue
                for c in range(4):
                    for r in range(d):
                        scr[c, pl.ds(r, tm // d, stride=d), :] = a[t][g][:, r * 512 + c * 128:r * 512 + (c + 1) * 128].astype(F32)
                    o_ref[:, base + c * 128:base + (c + 1) * 128] = scr[c].astype(BF16)

    sub = [pl.BlockSpec((tm // d, d * 512), lambda i: (i, 0)) for d in DILATIONS]
    flat = [att_grads[t][g] for t in range(3) for g in range(3)]
    return pl.pallas_call(
        body, out_shape=SDS((S, PROJ_W), BF16), grid=(S // tm,),
        in_specs=sub * 3 + [ANY], out_specs=pl.BlockSpec((tm, 9 * COLB), lambda i: (i, 0)),
        scratch_shapes=[pltpu.VMEM((4, tm, 128), F32)], input_output_aliases={9: 0},
        compiler_params=_cparams("parallel"), name="assemble_dproj")(*flat, dproj)


def _ret_fwd(proj, consts, exchanges=()):
    S = proj.shape[0]
    nc = S // BLK
    dmask, zeta, xi, dec = consts

    def body(q_ref, k_ref, v0_ref, v1_ref, g0_ref, g1_ref, dm_ref, z_ref, x_ref, dec_ref,
             y_ref, rn_ref, rs_ref, st_ref, R):
        @pl.when(pl.program_id(0) == 0)
        def _():
            R[...] = jnp.zeros_like(R)

        lane16 = lax.broadcasted_iota(jnp.int32, (BLK, 128), 1) // 16
        rs_all = jnp.zeros((BLK, 128), F32)
        first = []
        for h in range(RET_HEADS):
            hs = slice(h * 128, (h + 1) * 128)
            q, k = q_ref[:, hs], k_ref[:, hs]
            v = (v0_ref if h < 2 else v1_ref)[:, (h % 2) * 256:(h % 2 + 1) * 256]
            Rb = R[h].astype(BF16)
            st_ref[h] = Rb
            kz = (k.astype(F32) * z_ref[h]).astype(BF16)
            first.append((v, _dot_nt(q, k), _dot((q.astype(F32) * x_ref[h]).astype(BF16), Rb), _dot_tn(kz, v)))
        masked = [(s * dm_ref[h]).astype(BF16) for h, (_, s, _, _) in enumerate(first)]
        for h in range(RET_HEADS):
            vs = slice((h % 2) * 256, (h % 2 + 1) * 256)
            os_ = slice(h * 256, (h + 1) * 256)
            v, _, cross, kv = first[h]
            o = _dot(masked[h], v) + cross
            R[h] = R[h] * dec_ref[h, 0:1, :] + kv
            mu = jnp.mean(o, axis=-1, keepdims=True)
            oc = o - mu
            rstd = lax.rsqrt(jnp.mean(oc * oc, axis=-1, keepdims=True) + NORM_EPS)
            rn = oc * rstd
            gr = (g0_ref if h < 2 else g1_ref)[:, vs].astype(F32)
            y_ref[:, os_] = (rn * gr * _sigmoid(gr)).astype(BF16)
            rn_ref[:, os_] = rn.astype(BF16)
            rs_all = jnp.where(lane16 == h, rstd, rs_all)
        rs_ref[...] = rs_all

    cst = lambda shape: pl.BlockSpec(shape, lambda c: (0, 0, 0))
    blk = lambda j: pl.BlockSpec((BLK, 512), lambda c: (c, j))
    return _carrier_call(
        body, (proj, proj, proj, proj, proj, proj, dmask, zeta, xi, dec),
        out_shape=(SDS((S, 1024), BF16), SDS((S, 1024), BF16), SDS((S, 128), F32), SDS((RET_HEADS, nc, BLK, 256), BF16)),
        grid=(nc,),
        in_specs=[blk(QR_B), blk(KR_B), blk(11), blk(12), blk(13), blk(14),
                  cst((RET_HEADS, BLK, BLK)), cst((RET_HEADS, BLK, 128)), cst((RET_HEADS, BLK, 128)), cst((RET_HEADS, 8, 256))],
        out_specs=(pl.BlockSpec((BLK, 1024), lambda c: (c, 0)), pl.BlockSpec((BLK, 1024), lambda c: (c, 0)),
                   pl.BlockSpec((BLK, 128), lambda c: (c, 0)),
                   pl.BlockSpec((RET_HEADS, None, BLK, 256), lambda c: (0, c, 0, 0))),
        scratch_shapes=[pltpu.VMEM((RET_HEADS, BLK, 256), F32)],
        sem=("arbitrary",), name="ret_fwd", exchanges=exchanges)


def _branch_merge(att, yrin, proj, wa, wr):
    S = att.shape[0]
    tm = min(S, 2048)

    def body(a_ref, y_ref, ga_ref, gr_ref, wa_ref, wr_ref, m_ref, ya_ref, yr_ref):
        for rows in _row_pieces(tm):
            ya = _dot(a_ref[rows, :], wa_ref[...])
            yr = _dot(y_ref[rows, :], wr_ref[...])
            m_ref[rows, :] = (_sigmoid(ga_ref[rows, :].astype(F32)) * ya
                              + _sigmoid(gr_ref[rows, :].astype(F32)) * yr).astype(BF16)
            ya_ref[rows, :] = ya.astype(BF16)
            yr_ref[rows, :] = yr.astype(BF16)

    ospec = pl.BlockSpec((tm, 512), lambda i, j: (i, j))
    return pl.pallas_call(
        body, out_shape=(SDS((S, D_MODEL), BF16),) * 3, grid=(S // tm, 2),
        in_specs=[pl.BlockSpec((tm, 512), lambda i, j: (i, 0)), pl.BlockSpec((tm, 1024), lambda i, j: (i, 0)),
                  pl.BlockSpec((tm, 512), lambda i, j: (i, 15 + j)), pl.BlockSpec((tm, 512), lambda i, j: (i, 17 + j)),
                  pl.BlockSpec((512, 512), lambda i, j: (0, j)), pl.BlockSpec((1024, 512), lambda i, j: (0, j))],
        out_specs=(ospec, ospec, ospec),
        compiler_params=_cparams("parallel", "arbitrary"), name="branch_merge")(att, yrin, proj, proj, wa, wr)


def _out_proj(merged, wo, x, g2, exchanges=()):
    S = x.shape[0]
    tm = 1024

    def body(m_ref, w_ref, x_ref, g_ref, x1_ref, h2_ref):
        x1 = x_ref[...] + _dot(m_ref[...], w_ref[...])
        x1_ref[...] = x1
        r = lax.rsqrt(jnp.mean(x1 * x1, axis=-1, keepdims=True) + NORM_EPS)
        h2_ref[...] = (x1 * r * g_ref[...]).astype(BF16)

    row = pl.BlockSpec((tm, D_MODEL), lambda i: (i, 0))
    return _carrier_call(
        body, (merged, wo, x, g2), out_shape=(SDS((S, D_MODEL), F32), SDS((S, D_MODEL), BF16)), grid=(S // tm,),
        in_specs=[row, pl.BlockSpec((D_MODEL, D_MODEL), lambda i: (0, 0)), row, pl.BlockSpec((1, D_MODEL), lambda i: (0, 0))],
        out_specs=(row, row), sem=("parallel",), name="out_proj", exchanges=exchanges)


def _ffn_up(h2, wg, wu, exchanges=()):
    S = h2.shape[0]
    tm = min(S, 2048)

    def body(h_ref, wg_ref, wu_ref, g_ref, u_ref, a_ref):
        for rows in _row_pieces(tm):
            hv = h_ref[rows, :]
            g = _dot(hv, wg_ref[...])
            u = _dot(hv, wu_ref[...])
            g_ref[rows, :] = g.astype(BF16)
            u_ref[rows, :] = u.astype(BF16)
            a_ref[rows, :] = (g * _sigmoid(g) * u).astype(BF16)

    wspec = pl.BlockSpec((None, D_MODEL, HID_S), lambda i, s: (s, 0, 0))
    ospec = pl.BlockSpec((None, tm, HID_S), lambda i, s: (s, i, 0))
    return _carrier_call(
        body, (h2, wg, wu), out_shape=(SDS((N_SHARD, S, HID_S), BF16),) * 3, grid=(S // tm, N_SHARD),
        in_specs=[pl.BlockSpec((tm, D_MODEL), lambda i, s: (i, 0)), wspec, wspec],
        out_specs=(ospec, ospec, ospec),
        sem=("parallel", "arbitrary"), name="ffn_up", exchanges=exchanges)


def _ffn_down_loss(act, wd, x1, g3, tgt):
    S = x1.shape[0]
    tm = 512

    def body(a_ref, w_ref, x_ref, g_ref, t_ref, dx_ref, dxb_ref, dg_ref, ls_ref):
        @pl.when(pl.program_id(0) == 0)
        def _():
            dg_ref[...] = jnp.zeros_like(dg_ref)
            ls_ref[...] = jnp.zeros_like(ls_ref)

        g = g_ref[...]
        for rows in _row_pieces(tm, 256):
            y = _dot(a_ref[0, rows, :], w_ref[0])
            for s in range(1, N_SHARD):
                y = y + _dot(a_ref[s, rows, :], w_ref[s])
            x2 = x_ref[rows, :] + y
            r = lax.rsqrt(jnp.mean(x2 * x2, axis=-1, keepdims=True) + NORM_EPS)
            xh = x2 * r
            err = xh * g - t_ref[rows, :]
            ls_ref[...] += jnp.sum(jnp.sum(err * err, axis=-1, keepdims=True), axis=0, keepdims=True) * (0.5 / D_MODEL)
            dy = err * (1.0 / D_MODEL)
            dg_ref[...] += jnp.sum(dy * xh, axis=0, keepdims=True)
            dxh = dy * g
            dx = r * (dxh - xh * jnp.mean(dxh * xh, axis=-1, keepdims=True))
            dx_ref[rows, :] = dx
            dxb_ref[rows, :] = dx.astype(BF16)

    row = pl.BlockSpec((tm, D_MODEL), lambda i: (i, 0))
    vec = pl.BlockSpec((1, D_MODEL), lambda i: (0, 0))
    return pl.pallas_call(
        body, out_shape=(SDS((S, D_MODEL), F32), SDS((S, D_MODEL), BF16), SDS((1, D_MODEL), F32), SDS((8, 128), F32)),
        grid=(S // tm,),
        in_specs=[pl.BlockSpec((N_SHARD, tm, HID_S), lambda i: (0, i, 0)),
                  pl.BlockSpec((N_SHARD, HID_S, D_MODEL), lambda i: (0, 0, 0), pipeline_mode=pl.Buffered(1)),
                  row, vec, row],
        out_specs=(row, row, vec, pl.BlockSpec((8, 128), lambda i: (0, 0))),
        compiler_params=_cparams("arbitrary"), name="ffn_down_loss")(act, wd, x1, g3, tgt)


def _ffn_down_bwd(dx2b, wd, gte, up):
    S = dx2b.shape[0]
    tm = min(S, 2048)

    def body(d_ref, w_ref, g_ref, u_ref, dg_ref, du_ref):
        for rows in _row_pieces(tm, 256):
            da = _dot_nt(d_ref[rows, :], w_ref[...])
            g = g_ref[rows, :].astype(F32)
            sg = _sigmoid(g)
            dg_ref[rows, :] = (da * u_ref[rows, :].astype(F32) * sg * (1.0 + g * (1.0 - sg))).astype(BF16)
            du_ref[rows, :] = (da * g * sg).astype(BF16)

    aspec = pl.BlockSpec((None, tm, HID_S), lambda i, s: (s, i, 0))
    return pl.pallas_call(
        body, out_shape=(SDS((N_SHARD, S, HID_S), BF16),) * 2, grid=(S // tm, N_SHARD),
        in_specs=[pl.BlockSpec((tm, D_MODEL), lambda i, s: (i, 0)),
                  pl.BlockSpec((None, HID_S, D_MODEL), lambda i, s: (s, 0, 0)), aspec, aspec],
        out_specs=(aspec, aspec),
        compiler_params=_cparams("parallel", "arbitrary"), name="ffn_down_bwd")(dx2b, wd, gte, up)


def _wgrad(name, a, b, a_spec, b_spec, out_shape, out_spec, n_par, S):
    tk = 2048

    def body(a_ref, b_ref, o_ref):
        @pl.when(pl.program_id(1) == 0)
        def _():
            o_ref[...] = jnp.zeros_like(o_ref)

        o_ref[...] += _dot_tn(a_ref[...], b_ref[...])

    return pl.pallas_call(
        body, out_shape=SDS(out_shape, F32), grid=(n_par, S // tk),
        in_specs=[a_spec(tk), b_spec(tk)], out_specs=out_spec,
        compiler_params=_cparams("parallel", "arbitrary"), name=name)(a, b)


def _ffn_up_bwd(dgte, dup, wg, wu, x1, g2, dx2, exchanges=()):
    S = x1.shape[0]
    tm = 512

    def body(dg_ref, du_ref, wg_ref, wu_ref, x_ref, g_ref, dx2_ref, dx_ref, dxb_ref, dgn_ref):
        @pl.when(pl.program_id(0) == 0)
        def _():
            dgn_ref[...] = jnp.zeros_like(dgn_ref)

        for rows in _row_pieces(tm, 256):
            dh = _dot_nt(dg_ref[0, rows, :], wg_ref[0]) + _dot_nt(du_ref[0, rows, :], wu_ref[0])
            for s in range(1, N_SHARD):
                dh = dh + _dot_nt(dg_ref[s, rows, :], wg_ref[s]) + _dot_nt(du_ref[s, rows, :], wu_ref[s])
            xv = x_ref[rows, :]
            r = lax.rsqrt(jnp.mean(xv * xv, axis=-1, keepdims=True) + NORM_EPS)
            xh = xv * r
            dgn_ref[...] += jnp.sum(dh * xh, axis=0, keepdims=True)
            dxh = dh * g_ref[...]
            dx = dx2_ref[rows, :] + r * (dxh - xh * jnp.mean(dxh * xh, axis=-1, keepdims=True))
            dx_ref[rows, :] = dx
            dxb_ref[rows, :] = dx.astype(BF16)

    row = pl.BlockSpec((tm, D_MODEL), lambda i: (i, 0))
    vec = pl.BlockSpec((1, D_MODEL), lambda i: (0, 0))
    aspec = pl.BlockSpec((N_SHARD, tm, HID_S), lambda i: (0, i, 0))
    wspec = pl.BlockSpec((N_SHARD, D_MODEL, HID_S), lambda i: (0, 0, 0), pipeline_mode=pl.Buffered(1))
    return _carrier_call(
        body, (dgte, dup, wg, wu, x1, g2, dx2),
        out_shape=(SDS((S, D_MODEL), F32), SDS((S, D_MODEL), BF16), SDS((1, D_MODEL), F32)),
        grid=(S // tm,),
        in_specs=[aspec, aspec, wspec, wspec, row, vec, row], out_specs=(row, row, vec),
        sem=("arbitrary",), name="ffn_up_bwd", exchanges=exchanges)


def _out_proj_bwd(dx1b, wo, proj, ya, yr):
    S = dx1b.shape[0]
    tm = 512
    gate0 = 15 * COLB

    def body(d_ref, w_ref, ga_ref, gr_ref, ya_ref, yr_ref, dya_ref, dyr_ref, dp_ref):
        for rows in _row_pieces(tm, 256):
            dm = _dot_nt(d_ref[rows, :], w_ref[...])
            sa = _sigmoid(ga_ref[rows, :].astype(F32))
            sr = _sigmoid(gr_ref[rows, :].astype(F32))
            dya_ref[rows, :] = (dm * sa).astype(BF16)
            dyr_ref[rows, :] = (dm * sr).astype(BF16)
            dp_ref[rows, 0:D_MODEL] = (dm * ya_ref[rows, :].astype(F32) * sa * (1.0 - sa)).astype(BF16)
            dp_ref[rows, D_MODEL:2 * D_MODEL] = (dm * yr_ref[rows, :].astype(F32) * sr * (1.0 - sr)).astype(BF16)

    row = pl.BlockSpec((tm, D_MODEL), lambda i: (i, 0))
    cols = lambda c0, w: pl.BlockSpec((pl.Element(tm), pl.Element(w)), lambda i: (i * tm, c0))
    return pl.pallas_call(
        body, out_shape=(SDS((S, D_MODEL), BF16), SDS((S, D_MODEL), BF16), SDS((S, PROJ_W), BF16)), grid=(S // tm,),
        in_specs=[row, pl.BlockSpec((D_MODEL, D_MODEL), lambda i: (0, 0), pipeline_mode=pl.Buffered(1)),
                  cols(gate0, D_MODEL), cols(gate0 + D_MODEL, D_MODEL), row, row],
        out_specs=(row, row, cols(gate0, 2 * D_MODEL)),
        compiler_params=_cparams("parallel"), name="out_proj_bwd")(dx1b, wo, proj, proj, ya, yr)


def _branch_bwd(dya, dyr, wa, wr, att):
    S = dya.shape[0]
    tm = 1024

    def body(da_ref, dr_ref, wa_ref, wr_ref, att_ref, datt_ref, rho_ref, dyi_ref):
        datt = _dot_nt(da_ref[...], wa_ref[...])
        datt_ref[...] = datt.astype(BF16)
        dyi_ref[...] = _dot_nt(dr_ref[...], wr_ref[...]).astype(BF16)
        prod = datt * att_ref[...].astype(F32)
        lane = lax.broadcasted_iota(jnp.int32, (tm, 128), 1)
        lo = lane < 64
        rho = jnp.zeros((tm, 128), F32)
        for c in range(4):
            pc = prod[:, c * 128:(c + 1) * 128]
            tot = jnp.sum(pc, axis=-1, keepdims=True)
            low = jnp.sum(jnp.where(lo, pc, 0.0), axis=-1, keepdims=True)
            rho = jnp.where(lane // 16 == 2 * c, low, jnp.where(lane // 16 == 2 * c + 1, tot - low, rho))
        rho_ref[...] = rho

    row = lambda w: pl.BlockSpec((tm, w), lambda i: (i, 0))
    return pl.pallas_call(
        body, out_shape=(SDS((S, 512), BF16), SDS((S, 128), F32), SDS((S, 1024), BF16)), grid=(S // tm,),
        in_specs=[row(1024), row(1024), pl.BlockSpec((512, 1024), lambda i: (0, 0)),
                  pl.BlockSpec((1024, 1024), lambda i: (0, 0)), row(512)],
        out_specs=(row(512), row(128), row(1024)),
        compiler_params=_cparams("parallel"), name="branch_bwd")(dya, dyr, wa, wr, att)


def _attn_bwd(qkv, datt, lse, rho, rtab, d, gi, exchanges=()):
    L = qkv.shape[0]
    nb = L // BLK

    def body(q_ref, kc_ref, kp_ref, vc_ref, vp_ref, do_ref, lse_ref, rho_ref, tq_ref, tk_ref,
             dq_ref, dk_ref, dv_ref, ck, cv):
        n = pl.program_id(1)

        @pl.when(n == 0)
        def _():
            ck[...] = jnp.zeros_like(ck)
            cv[...] = jnp.zeros_like(cv)

        def store_rot(ref, val, t_ref, c):
            sl = slice(c * 128, (c + 1) * 128)
            ref[:, sl] = _unrot(val, t_ref[0], t_ref[1], t_ref[2], 32).astype(BF16)

        @pl.when(n < nb)
        def _():
            mask = _band_mask(n)
            mask2 = jnp.concatenate([mask, mask], axis=0)
            lo = lax.broadcasted_iota(jnp.int32, (BLK, 128), 1) < 64

            def stacked(a):
                return jnp.concatenate([jnp.where(lo, a, jnp.zeros_like(a)), jnp.where(lo, jnp.zeros_like(a), a)], axis=0)

            def head_cols(ref, c):
                return jnp.concatenate([jnp.broadcast_to(ref[:, 32 * c:32 * c + 1], (BLK, 2 * BLK)),
                                        jnp.broadcast_to(ref[:, 32 * c + 16:32 * c + 17], (BLK, 2 * BLK))], axis=0)

            ops, raw = [], []
            for c in range(4):
                sl = slice(c * 128, (c + 1) * 128)
                q2, do2 = stacked(q_ref[:, sl]), stacked(do_ref[:, sl])
                k = jnp.concatenate([kp_ref[:, sl], kc_ref[:, sl]], axis=0)
                v = jnp.concatenate([vp_ref[:, sl], vc_ref[:, sl]], axis=0)
                ops.append((q2, do2, k))
                raw.append((_dot_nt(q2, k), _dot_nt(do2, v)))
            grads = []
            for c, (s, dp) in enumerate(raw):
                p = jnp.where(mask2, jnp.exp(s * 0.125 - head_cols(lse_ref, c)), 0.0)
                grads.append(((p * (dp - head_cols(rho_ref, c)) * 0.125).astype(BF16), p.astype(BF16)))
            for c, ((q2, do2, k), (ds, pb)) in enumerate(zip(ops, grads)):
                sl = slice(c * 128, (c + 1) * 128)
                dq2 = _dot(ds, k)
                dq_c = jnp.where(lo, dq2[:BLK], dq2[BLK:])
                dk_c = _dot_tn(ds, q2)
                dv_c = _dot_tn(pb, do2)
                store_rot(dq_ref, dq_c, tq_ref, c)
                store_rot(dk_ref, ck[:, sl] + dk_c[:BLK], tk_ref, c)
                dv_ref[:, sl] = (cv[:, sl] + dv_c[:BLK]).astype(BF16)
                ck[:, sl] = dk_c[BLK:]
                cv[:, sl] = dv_c[BLK:]

        @pl.when(n == nb)
        def _():
            for c in range(4):
                sl = slice(c * 128, (c + 1) * 128)
                store_rot(dk_ref, ck[:, sl], tk_ref, c)
            dv_ref[...] = cv[...].astype(BF16)

    cur = lambda n: jnp.minimum(n, nb - 1)
    prev = lambda n: jnp.maximum(jnp.minimum(n, nb - 1) - 1, 0)
    fin = lambda n: jnp.maximum(n - 1, 0)
    col = _qkv_col(d, gi)
    return _carrier_call(
        body, (qkv, qkv, qkv, qkv, qkv, datt, lse, rho, rtab, rtab),
        out_shape=(SDS((L, d * 512), BF16),) * 3, grid=(d, nb + 1),
        in_specs=[pl.BlockSpec((BLK, 512), lambda r, n: (cur(n), col(0, r))),
                  pl.BlockSpec((BLK, 512), lambda r, n: (cur(n), col(1, r))),
                  pl.BlockSpec((BLK, 512), lambda r, n: (prev(n), col(1, r))),
                  pl.BlockSpec((BLK, 512), lambda r, n: (cur(n), col(2, r))),
                  pl.BlockSpec((BLK, 512), lambda r, n: (prev(n), col(2, r))),
                  pl.BlockSpec((BLK, 512), lambda r, n: (cur(n), r)),
                  pl.BlockSpec((BLK, 128), lambda r, n: (cur(n), r)),
                  pl.BlockSpec((BLK, 128), lambda r, n: (cur(n), r)),
                  pl.BlockSpec((3, BLK, 128), lambda r, n: (0, cur(n), r)),
                  pl.BlockSpec((3, BLK, 128), lambda r, n: (0, fin(n), r))],
        out_specs=(pl.BlockSpec((BLK, 512), lambda r, n: (cur(n), r)),
                   pl.BlockSpec((BLK, 512), lambda r, n: (fin(n), r)),
                   pl.BlockSpec((BLK, 512), lambda r, n: (fin(n), r))),
        scratch_shapes=[pltpu.VMEM((BLK, 512), F32), pltpu.VMEM((BLK, 512), F32)],
        sem=("parallel", "arbitrary"), name=f"attn_bwd_g{gi}", exchanges=exchanges)


def _ret_bwd(proj, rn, rstd, dyrin, states, tab, consts, dproj, exchanges=()):
    S = proj.shape[0]
    nc = S // BLK
    dmask, zeta, xi, dec = consts

    def body(q_ref, k_ref, v0_ref, v1_ref, g0_ref, g1_ref, rn_ref, rs_ref, dy_ref, st_ref, tq_ref, tk_ref,
             dm_ref, z_ref, x_ref, dec_ref, dp_prev, dp_ref, dR):
        dq_ref, dk_ref = dp_ref.at[:, 0:512], dp_ref.at[:, 512:1024]
        dv_ref, dgr_ref = dp_ref.at[:, 1024:2048], dp_ref.at[:, 2048:3072]

        @pl.when(pl.program_id(0) == 0)
        def _():
            dR[...] = jnp.zeros_like(dR)

        dobs = []
        for h in range(RET_HEADS):
            vs = slice((h % 2) * 256, (h % 2 + 1) * 256)
            os_ = slice(h * 256, (h + 1) * 256)
            gr = (g0_ref if h < 2 else g1_ref)[:, vs].astype(F32)
            sg = _sigmoid(gr)
            rn_v = rn_ref[:, os_].astype(F32)
            dyi = dy_ref[:, os_].astype(F32)
            dgr_ref[:, os_] = (dyi * rn_v * sg * (1.0 + gr * (1.0 - sg))).astype(BF16)
            drn = dyi * gr * sg
            rstd = jnp.broadcast_to(rs_ref[:, 16 * h:16 * h + 1], (BLK, 256))
            do = rstd * (drn - jnp.mean(drn, axis=-1, keepdims=True) - rn_v * jnp.mean(drn * rn_v, axis=-1, keepdims=True))
            dobs.append(do.astype(BF16))
        first = []
        for h in range(RET_HEADS):
            hs = slice(h * 128, (h + 1) * 128)
            q, k = q_ref[:, hs], k_ref[:, hs]
            v = (v0_ref if h < 2 else v1_ref)[:, (h % 2) * 256:(h % 2 + 1) * 256]
            dob, dRb = dobs[h], dR[h].astype(BF16)
            kz = (k.astype(F32) * z_ref[h]).astype(BF16)
            qx = (q.astype(F32) * x_ref[h]).astype(BF16)
            first.append((q, k, _dot_nt(q, k), _dot_nt(dob, v), _dot(kz, dRb), _dot_nt(dob, st_ref[h]),
                          _dot_nt(v, dRb), _dot_tn(qx, dob)))
        masked = [((s * dm_ref[h]).astype(BF16), (dsr * dm_ref[h]).astype(BF16))
                  for h, (_, _, s, dsr, _, _, _, _) in enumerate(first)]
        for h in range(RET_HEADS):
            hs = slice(h * 128, (h + 1) * 128)
            os_ = slice(h * 256, (h + 1) * 256)
            q, k, _, _, dv_state, dq_state, dk_state, dr_new = first[h]
            sD, dS = masked[h]
            dv_ref[:, os_] = (_dot_tn(sD, dobs[h]) + dv_state).astype(BF16)
            dq = _dot(dS, k) + dq_state * x_ref[h]
            dk = _dot_tn(dS, q) + dk_state * z_ref[h]
            dR[h] = dR[h] * dec_ref[h, 0:1, :] + dr_new
            dq_ref[:, hs] = _unrot(dq, tq_ref[0], tq_ref[1], tq_ref[2], 1).astype(BF16)
            dk_ref[:, hs] = _unrot(dk, tk_ref[0], tk_ref[1], tk_ref[2], 1).astype(BF16)

    rc = lambda c: nc - 1 - c
    cst = lambda shape: pl.BlockSpec(shape, lambda c: (0, 0, 0))
    blk = lambda j: pl.BlockSpec((BLK, 512), lambda c: (rc(c), j))
    row = lambda w: pl.BlockSpec((BLK, w), lambda c: (rc(c), 0))
    (dproj,), xres = _carrier_call(
        body, (proj, proj, proj, proj, proj, proj, rn, rstd, dyrin, states, tab, tab, dmask, zeta, xi, dec, dproj),
        out_shape=(SDS((S, PROJ_W), BF16),), grid=(nc,),
        in_specs=[blk(QR_B), blk(KR_B), blk(11), blk(12), blk(13), blk(14), row(1024), row(128), row(1024),
                  pl.BlockSpec((RET_HEADS, None, BLK, 256), lambda c: (0, rc(c), 0, 0)),
                  pl.BlockSpec((None, 3, BLK, 128), lambda c: (1, 0, rc(c), 0)),
                  pl.BlockSpec((None, 3, BLK, 128), lambda c: (2, 0, rc(c), 0)),
                  cst((RET_HEADS, BLK, BLK)), cst((RET_HEADS, BLK, 128)), cst((RET_HEADS, BLK, 128)), cst((RET_HEADS, 8, 256)),
                  ANY],
        out_specs=(pl.BlockSpec((pl.Element(BLK), pl.Element(6 * COLB)), lambda c: (rc(c) * BLK, QR_B * COLB)),),
        scratch_shapes=[pltpu.VMEM((RET_HEADS, BLK, 256), F32)],
        sem=("arbitrary",), name="ret_bwd", exchanges=exchanges, in_out_aliases={16: 0})
    return dproj, xres


def _wgrad_in_half(ht, dproj, sidx, kept, exchanges=()):
    S = dproj.shape[0]
    tk = 2048
    half = (lambda sx: sx[4]) if kept else (lambda sx: 1 - sx[4])

    def body(a_ref, b_ref, o_ref):
        @pl.when(pl.program_id(1) == 0)
        def _():
            o_ref[...] = jnp.zeros_like(o_ref)

        o_ref[...] += _dot(a_ref[...], b_ref[...])

    (g,), xres = _carrier_call(
        body, (ht, dproj), out_shape=(SDS((D_MODEL // 2, PROJ_W), F32),), grid=(N_SHARD, S // tk),
        in_specs=[pl.BlockSpec((D_MODEL // 2, tk), lambda s, k, sx: (half(sx), k)),
                  pl.BlockSpec((tk, W_IN_S), lambda s, k, sx: (k, s))],
        out_specs=(pl.BlockSpec((D_MODEL // 2, W_IN_S), lambda s, k, sx: (0, s)),),
        sem=("parallel", "arbitrary"), name="wgrad_in_kept" if kept else "wgrad_in_sent", exchanges=exchanges,
        prefetch=sidx)
    return g, xres


def _in_proj_bwd(dproj, w_in, x, g1, dx1, exchanges=()):
    S = x.shape[0]
    tm = 512

    def body(d_ref, w_ref, x_ref, g_ref, dx1_ref, dx_ref, dgn_ref, acc):
        i, s = pl.program_id(0), pl.program_id(1)

        @pl.when(s == 0)
        def _():
            acc[...] = jnp.zeros_like(acc)

        @pl.when((i == 0) & (s == 0))
        def _():
            dgn_ref[...] = jnp.zeros_like(dgn_ref)

        acc[...] += _dot_nt(d_ref[...], w_ref[...])

        @pl.when(s == N_SHARD - 1)
        def _():
            xv = x_ref[...]
            r = lax.rsqrt(jnp.mean(xv * xv, axis=-1, keepdims=True) + NORM_EPS)
            xh = xv * r
            dh = acc[...]
            dgn_ref[...] += jnp.sum(dh * xh, axis=0, keepdims=True)
            dxh = dh * g_ref[...]
            dx_ref[...] = dx1_ref[...] + r * (dxh - xh * jnp.mean(dxh * xh, axis=-1, keepdims=True))

    row = pl.BlockSpec((tm, D_MODEL), lambda i, s: (i, 0))
    vec = pl.BlockSpec((1, D_MODEL), lambda i, s: (0, 0))
    (gx, dg), xres = _carrier_call(
        body, (dproj, w_in, x, g1, dx1),
        out_shape=(SDS((S, D_MODEL), F32), SDS((1, D_MODEL), F32)), grid=(S // tm, N_SHARD),
        in_specs=[pl.BlockSpec((tm, W_IN_S), lambda i, s: (i, s)),
                  pl.BlockSpec((D_MODEL, W_IN_S), lambda i, s: (0, s)), row, vec, row],
        out_specs=(row, vec), scratch_shapes=[pltpu.VMEM((tm, D_MODEL), F32)],
        sem=("arbitrary", "arbitrary"), name="in_proj_bwd", exchanges=exchanges)
    return gx, dg, xres


def _sub_view(a, d):
    S, W = a.shape
    return a.reshape(S // d, d * W)


def _step(x, tgt, g1, g2, g3, comm):
    S = x.shape[0]
    tab_np = _tables(S)
    tab = jnp.asarray(tab_np)
    consts = _ret_consts()

    h, ht = _rms_fwd(x, g1)
    w_in = comm.w_in()
    proj, xres = _in_proj(h, w_in, tab, comm.carry("in_proj"))
    comm.took("in_proj", xres)
    qkvs, o_parts, lse_parts = [], [], []
    for gi, d in enumerate(DILATIONS):
        qkv = proj if d == 1 else _qkv_to_sub(proj, d, gi)
        (o_g, lse_g), xres = _attn_fwd(qkv, d, gi, comm.carry(f"attn_fwd_g{gi}"))
        comm.took(f"attn_fwd_g{gi}", xres)
        qkvs.append(qkv)
        o_parts.append(o_g)
        lse_parts.append(lse_g)
    att, lse_tot = _attn_merge(o_parts, lse_parts)
    (yrin, rn, rstd, states), xres = _ret_fwd(proj, consts, comm.carry("ret_fwd"))
    comm.took("ret_fwd", xres)
    wa, wr, wo = comm.weight(1), comm.weight(2), comm.weight(3)
    merged, ya, yr = _branch_merge(att, yrin, proj, wa, wr)
    (x1, h2), xres = _out_proj(merged, wo, x, g2, comm.carry("out_proj"))
    comm.took("out_proj", xres)
    wg, wu = comm.weight(4), comm.weight(5)
    (gte, up, act), xres = _ffn_up(h2, wg, wu, comm.carry("ffn_up"))
    comm.took("ffn_up", xres)
    wd = comm.weight(6)
    dx2, dx2b, dg3, loss_p = _ffn_down_loss(act, wd, x1, g3, tgt)

    dgte, dup = _ffn_down_bwd(dx2b, wd, gte, up)
    tok3 = lambda w: (lambda tk: pl.BlockSpec((None, tk, w), lambda p, k: (p, k, 0)))
    tok2 = lambda w: (lambda tk: pl.BlockSpec((tk, w), lambda p, k: (k, 0)))
    g_d = _wgrad("wgrad_down", act, dx2b, tok3(HID_S), tok2(D_MODEL), (N_SHARD, HID_S, D_MODEL),
                 pl.BlockSpec((None, HID_S, D_MODEL), lambda p, k: (p, 0, 0)), N_SHARD, S)
    g_g = _wgrad("wgrad_gate", h2, dgte, tok2(D_MODEL), tok3(HID_S), (N_SHARD, D_MODEL, HID_S),
                 pl.BlockSpec((None, D_MODEL, HID_S), lambda p, k: (p, 0, 0)), N_SHARD, S)
    g_u = _wgrad("wgrad_up", h2, dup, tok2(D_MODEL), tok3(HID_S), (N_SHARD, D_MODEL, HID_S),
                 pl.BlockSpec((None, D_MODEL, HID_S), lambda p, k: (p, 0, 0)), N_SHARD, S)
    comm.grads({4: g_g, 5: g_u, 6: g_d})
    (dx1, dx1b, dg2), xres = _ffn_up_bwd(dgte, dup, wg, wu, x1, g2, dx2, comm.carry("ffn_up_bwd"))
    comm.took("ffn_up_bwd", xres)
    dya, dyr, dproj = _out_proj_bwd(dx1b, wo, proj, ya, yr)
    colblk = lambda w: (lambda tk: pl.BlockSpec((tk, w), lambda p, k: (k, p)))
    g_o = _wgrad("wgrad_out", merged, dx1b, colblk(256), tok2(D_MODEL), (D_MODEL, D_MODEL),
                 pl.BlockSpec((256, D_MODEL), lambda p, k: (p, 0)), 4, S)
    datt, rho, dyrin = _branch_bwd(dya, dyr, wa, wr, att)
    g_a = _wgrad("wgrad_attn", att, dya, tok2(512), colblk(512), (512, D_MODEL),
                 pl.BlockSpec((512, 512), lambda p, k: (0, p)), 2, S)
    g_r = _wgrad("wgrad_ret", yrin, dyr, colblk(256), tok2(D_MODEL), (D_MODEL, D_MODEL),
                 pl.BlockSpec((256, D_MODEL), lambda p, k: (p, 0)), 4, S)
    comm.grads({1: g_a, 2: g_r.reshape(N_SHARD, 256, D_MODEL), 3: g_o.reshape(N_SHARD, 256, D_MODEL)})
    dproj, xres = _ret_bwd(proj, rn, rstd, dyrin, states, tab, consts, dproj, comm.carry("ret_bwd"))
    comm.took("ret_bwd", xres)
    dqs, dks, dvs = [], [], []
    for gi, d in enumerate(DILATIONS):
        rtab = jnp.asarray(tab_np[0].reshape(3, S // d, d * 128))
        (dq, dk, dv), xres = _attn_bwd(qkvs[gi], _sub_view(datt, d), _sub_view(lse_tot, d), _sub_view(rho, d), rtab, d, gi,
                                       comm.carry(f"attn_bwd_g{gi}"))
        comm.took(f"attn_bwd_g{gi}", xres)
        dqs.append(dq)
        dks.append(dk)
        dvs.append(dv)
    dproj = _assemble_dproj((dqs, dks, dvs), dproj)
    g_sent, xres = _wgrad_in_half(ht, dproj, comm.sidx, False, comm.carry("wgrad_in_sent"))
    comm.took("wgrad_in_sent", xres)
    comm.grads({"in_sent": g_sent})
    g_kept, xres = _wgrad_in_half(ht, dproj, comm.sidx, True, comm.carry("wgrad_in_kept"))
    comm.grads({"in_kept": g_kept})
    comm.took("wgrad_in_kept", xres)
    grad_x, dg1, xres = _in_proj_bwd(dproj, w_in, x, g1, dx1, comm.carry("in_proj_bwd"))
    comm.took("in_proj_bwd", xres)
    return loss_p, grad_x, (dg1, dg2, dg3)


W_KINDS = ("col", "col", "lead", "lead", "lead", "lead", "lead")
W_SHARD = ((1024, W_IN_S), (512, 256), (256, 1024), (256, 1024), (1024, HID_S), (1024, HID_S), (HID_S, 1024))
N_W = len(W_KINDS)


def _full_shape(wi):
    R, C = W_SHARD[wi]
    return (R, N_SHARD * C) if W_KINDS[wi] == "col" else (N_SHARD, R, C)


def _view(ref, wi, s, half):
    R, C = W_SHARD[wi]
    rows = pl.ds(half * (R // 2), R // 2)
    if W_KINDS[wi] == "col":
        return ref.at[rows, pl.ds(pl.multiple_of(s * C, 128), C)]
    return ref.at[s, rows, :]


def _mesh_pos():
    x, y, c = lax.axis_index("x"), lax.axis_index("y"), lax.axis_index("c")
    chips = [(1 - x, y), (x, 1 - y), (1 - x, 1 - y)]
    return x, y, c, chips


def _cast_bf16(a):
    R, C = a.shape
    tr = R // 2 if R % 32 == 0 else R

    def body(a_ref, o_ref):
        o_ref[...] = a_ref[...].astype(BF16)

    spec = pl.BlockSpec((tr, C), lambda i: (i, 0))
    return pl.pallas_call(body, out_shape=SDS((R, C), BF16), grid=(R // tr,), in_specs=[spec], out_specs=spec,
                          compiler_params=_cparams("parallel"), name=f"cast_{R}x{C}")(a)


def _remote(send, recv, k, src, dst, to):
    return pltpu.make_async_remote_copy(src_ref=src, dst_ref=dst, send_sem=send.at[k], recv_sem=recv.at[k],
                                        device_id=to, device_id_type=MESH)


def _gather_now(wis, shards):
    n = len(wis)

    def body(*refs):
        sh, full = refs[:n], refs[n:2 * n]
        send, recv, loc = refs[2 * n:]
        x, y, c, _ = _mesh_pos()
        s_me, sib = 2 * x + y, (x, y, 1 - c)
        xn, yn = (1 - x, y), (x, 1 - y)
        flip = lambda a, b: a + b - 2 * a * b
        via = (flip(x, 1 - c), flip(y, c))
        onto = (flip(x, c), flip(y, 1 - c))
        shard_of = lambda chip: 2 * chip[0] + chip[1]
        own, started = [], []
        for i, wi in enumerate(wis):
            Rh = W_SHARD[wi][0] // 2
            for hf in range(2):
                cp = pltpu.make_async_copy(sh[i].at[pl.ds(hf * Rh, Rh), :], _view(full[i], wi, s_me, hf), loc.at[2 * i + hf])
                cp.start()
                own.append(cp)
            for j, chip in enumerate((xn, yn)):
                cp = _remote(send, recv, 6 * i + j, sh[i].at[pl.ds(c * Rh, Rh), :], _view(full[i], wi, s_me, c), (*chip, c))
                cp.start()
                started.append(cp)

        def pass_to_sibling(i, wi, k, s):
            mine = _view(full[i], wi, s, c)
            fw = _remote(send, recv, 6 * i + k, mine, mine, sib)
            fw.start()
            started.append(fw)

        for i, wi in enumerate(wis):
            for j, chip in enumerate((xn, yn)):
                land = _view(full[i], wi, shard_of(chip), c)
                _remote(send, recv, 6 * i + j, land, land, (*chip, c)).wait_recv()
                pass_to_sibling(i, wi, 3 + j, shard_of(chip))
            relay = _view(full[i], wi, shard_of(via), c)
            fw = _remote(send, recv, 6 * i + 2, relay, relay, (*onto, c))
            fw.start()
            started.append(fw)
        s_diag = 2 * (1 - x) + (1 - y)
        for i, wi in enumerate(wis):
            land = _view(full[i], wi, s_diag, c)
            _remote(send, recv, 6 * i + 2, land, land, (*onto, c)).wait_recv()
            pass_to_sibling(i, wi, 5, s_diag)
        for i, wi in enumerate(wis):
            for k, s in ((3, shard_of(xn)), (4, shard_of(yn)), (5, s_diag)):
                land = _view(full[i], wi, s, 1 - c)
                _remote(send, recv, 6 * i + k, land, land, sib).wait_recv()
        for cp in started:
            cp.wait_send()
        for cp in own:
            cp.wait()

    return pl.pallas_call(
        body, out_shape=tuple(SDS(_full_shape(wi), BF16) for wi in wis),
        in_specs=[ANY] * n, out_specs=tuple([ANY] * n),
        scratch_shapes=[pltpu.SemaphoreType.DMA((6 * n,)), pltpu.SemaphoreType.DMA((6 * n,)),
                        pltpu.SemaphoreType.DMA((2 * n,))],
        name="gather_now")(*shards)


def _ex_gather_ici(wis, shards, then_d2d=False):
    n = len(wis)

    def build(ins, outs, send, recv, loc):
        x, y, c, chips = _mesh_pos()
        s_me, sib = 2 * x + y, (x, y, 1 - c)
        starts, waits, after = [], [], []
        for i, wi in enumerate(wis):
            Rh = W_SHARD[wi][0] // 2
            for hf in range(2):
                cp = pltpu.make_async_copy(ins[i].at[pl.ds(hf * Rh, Rh), :], _view(outs[i], wi, s_me, hf), loc.at[2 * i + hf])
                starts.append(cp)
                waits.append(cp.wait)
            for j, chip in enumerate(chips):
                cp = _remote(send, recv, 3 * i + j, ins[i].at[pl.ds(c * Rh, Rh), :], _view(outs[i], wi, s_me, c), (*chip, c))
                land = _view(outs[i], wi, 2 * chip[0] + chip[1], c)
                starts.append(cp)
                waits += [cp.wait_send, _remote(send, recv, 3 * i + j, land, land, (*chip, c)).wait_recv]
                if then_d2d:
                    theirs = _view(outs[i], wi, 2 * chip[0] + chip[1], 1 - c)
                    fw = _remote(send, recv, 3 * n + 3 * i + j, land, land, sib)
                    waits.append(fw.start)
                    after += [fw.wait_send, _remote(send, recv, 3 * n + 3 * i + j, theirs, theirs, sib).wait_recv]
        return starts, waits + after

    return _Exchange(shards, [SDS(_full_shape(wi), BF16) for wi in wis], {}, (6 if then_d2d else 3) * n, 2 * n, build)


def _ex_gather_d2d(wis, fulls):
    def build(ins, outs, send, recv, loc):
        x, y, c, chips = _mesh_pos()
        sib = (x, y, 1 - c)
        starts, waits = [], []
        for i, wi in enumerate(wis):
            for j, chip in enumerate(chips):
                mine = _view(outs[i], wi, 2 * chip[0] + chip[1], c)
                theirs = _view(outs[i], wi, 2 * chip[0] + chip[1], 1 - c)
                cp = _remote(send, recv, 3 * i + j, mine, mine, sib)
                starts.append(cp)
                waits += [cp.wait_send, _remote(send, recv, 3 * i + j, theirs, theirs, sib).wait_recv]
        return starts, waits

    return _Exchange(fulls, [SDS(f.shape, BF16) for f in fulls], {i: i for i in range(len(wis))}, 3 * len(wis), 0, build)


def _half_shape(wi):
    R, C = W_SHARD[wi]
    return (R // 2, N_SHARD * C) if W_KINDS[wi] == "col" else (N_SHARD, R // 2, C)


def _ex_pair(wis, grads):
    def build(ins, outs, send, recv, loc):
        x, y, c, _ = _mesh_pos()
        starts, waits = [], []
        for i, wi in enumerate(wis):
            Rh = W_SHARD[wi][0] // 2
            rows = pl.ds((1 - c) * Rh, Rh)
            if tuple(ins[i].shape) == _half_shape(wi):
                src = ins[i]
            else:
                src = ins[i].at[rows, :] if W_KINDS[wi] == "col" else ins[i].at[:, rows, :]
            cp = _remote(send, recv, i, src, outs[i], (x, y, 1 - c))
            starts.append(cp)
            waits.append(cp.wait)
        return starts, waits

    return _Exchange(grads, [SDS(_half_shape(wi), F32) for wi in wis], {}, len(wis), 0, build)


def _ex_chip(wis, pbs):
    def build(ins, outs, send, recv, loc):
        x, y, c, chips = _mesh_pos()
        starts, waits = [], []
        for i, wi in enumerate(wis):
            for j, chip in enumerate(chips):
                cp = _remote(send, recv, 3 * i + j, ins[i].at[j], outs[i].at[j], (*chip, c))
                starts.append(cp)
                waits.append(cp.wait)
        return starts, waits

    shapes = [SDS((3, W_SHARD[wi][0] // 2, W_SHARD[wi][1]), BF16) for wi in wis]
    return _Exchange(pbs, shapes, {}, 3 * len(wis), 0, build)


def _ex_share(wis, halves):
    def build(ins, outs, send, recv, loc):
        x, y, c, _ = _mesh_pos()
        sib = (x, y, 1 - c)
        starts, waits = [], []
        for i, wi in enumerate(wis):
            cp = _remote(send, recv, i, outs[i].at[c], outs[i].at[c], sib)
            starts.append(cp)
            waits += [cp.wait_send, _remote(send, recv, i, outs[i].at[1 - c], outs[i].at[1 - c], sib).wait_recv]
        return starts, waits

    return _Exchange(halves, [SDS(h.shape, F32) for h in halves], {i: i for i in range(len(wis))}, len(wis), 0, build)


def _row_tile(rh, C):
    best = 16
    for t in range(16, rh + 1, 16):
        if rh % t == 0 and t * C * 4 <= (3 << 19):
            best = t
    return best


def _pair_sum(wi, g, ra, sidx):
    R, C = W_SHARD[wi]
    Rh = R // 2
    tr = _row_tile(Rh, C)
    nt = Rh // tr
    off = 0 if tuple(g.shape) == _half_shape(wi) else nt
    col = W_KINDS[wi] == "col"

    def body(sidx_ref, *refs):
        gs, rs = refs[:4], refs[4:8]
        own_ref, pb_ref = refs[8:]
        own_ref[...] = gs[0][...] + rs[0][...]
        for j in range(3):
            pb_ref[j] = (gs[1 + j][...] + rs[1 + j][...]).astype(BF16)

    def gspec(slot):
        if col:
            return pl.BlockSpec((tr, C), lambda i, sx: (sx[4] * off + i, sx[slot]))
        return pl.BlockSpec((None, tr, C), lambda i, sx: (sx[slot], sx[4] * off + i, 0))

    def rspec(slot):
        if col:
            return pl.BlockSpec((tr, C), lambda i, sx: (i, sx[slot]))
        return pl.BlockSpec((None, tr, C), lambda i, sx: (sx[slot], i, 0))

    return pl.pallas_call(
        body, out_shape=(SDS((Rh, C), F32), SDS((3, Rh, C), BF16)),
        grid_spec=pltpu.PrefetchScalarGridSpec(
            num_scalar_prefetch=1, grid=(nt,),
            in_specs=[gspec(k) for k in range(4)] + [rspec(k) for k in range(4)],
            out_specs=(pl.BlockSpec((tr, C), lambda i, sx: (i, 0)), pl.BlockSpec((3, tr, C), lambda i, sx: (0, i, 0)))),
        compiler_params=_cparams("arbitrary"), name=f"pair_sum_w{wi}")(sidx, g, g, g, g, ra, ra, ra, ra)


def _chip_sum(wi, own, rb, sidx):
    R, C = W_SHARD[wi]
    Rh = R // 2
    tr = _row_tile(Rh, C)

    def body(sidx_ref, own_ref, rb_ref, o_ref):
        o_ref[...] = ((own_ref[...] + rb_ref[0].astype(F32)) + rb_ref[1].astype(F32)) + rb_ref[2].astype(F32)

    return pl.pallas_call(
        body, out_shape=SDS((2, Rh, C), F32),
        grid_spec=pltpu.PrefetchScalarGridSpec(
            num_scalar_prefetch=1, grid=(Rh // tr,),
            in_specs=[pl.BlockSpec((tr, C), lambda i, sx: (i, 0)), pl.BlockSpec((3, tr, C), lambda i, sx: (0, i, 0))],
            out_specs=pl.BlockSpec((None, tr, C), lambda i, sx: (sx[4], i, 0))),
        compiler_params=_cparams("arbitrary"), name=f"chip_sum_w{wi}")(sidx, own, rb)


def _gain_allgather(blk):
    m_per, n = blk.shape

    def body(x_ref, out_ref, send_sems, recv_sems, local_sem):
        x, y, c, chips = _mesh_pos()
        me, sibling = (x, y, c), (x, y, 1 - c)

        def rows(px, py, pc):
            return out_ref.at[pl.ds((4 * px + 2 * py + pc) * m_per, m_per), :]

        def copy(k, block, to, src=None):
            return pltpu.make_async_remote_copy(
                src_ref=rows(*block) if src is None else src, dst_ref=rows(*block),
                send_sem=send_sems.at[k], recv_sem=recv_sems.at[k], device_id=to, device_id_type=MESH)

        mine = pltpu.make_async_copy(x_ref, rows(*me), local_sem)
        mine.start()
        first = [copy(0, me, sibling, src=x_ref)]
        first += [copy(1 + j, me, (*chip, c), src=x_ref) for j, chip in enumerate(chips)]
        for cp in first:
            cp.start()
        passed = [copy(4 + j, (*chip, c), sibling) for j, chip in enumerate(chips)]
        for j, chip in enumerate(chips):
            copy(1 + j, (*chip, c), me).wait_recv()
            passed[j].start()
        copy(0, sibling, me).wait_recv()
        for j, chip in enumerate(chips):
            copy(4 + j, (*chip, 1 - c), me).wait_recv()
        for cp in first + passed:
            cp.wait_send()
        mine.wait()

    vm = pl.BlockSpec(memory_space=pltpu.VMEM)
    return pl.pallas_call(
        body, out_shape=SDS((8 * m_per, n), blk.dtype), in_specs=[vm], out_specs=vm,
        scratch_shapes=[pltpu.SemaphoreType.DMA((7,)), pltpu.SemaphoreType.DMA((7,)), pltpu.SemaphoreType.DMA],
        name="gain_allgather")(blk)


def _adam_math(w, g, m, v):
    mn = ADAM_B1 * m + (1.0 - ADAM_B1) * g
    vn = ADAM_B2 * v + (1.0 - ADAM_B2) * (g * g)
    mh = mn / (1.0 - ADAM_B1 ** ADAM_STEP)
    vh = vn / (1.0 - ADAM_B2 ** ADAM_STEP)
    return -ADAM_LR * (mh / (jnp.sqrt(vh) + ADAM_EPS) + ADAM_WD * w), mn, vn


def _adamw(wi, w, g, m, v):
    R, C = w.shape
    tr = _row_tile(R, C)

    def body(w_ref, g_ref, m_ref, v_ref, go_ref, d_ref, mn_ref, vn_ref):
        g = g_ref[...]
        go_ref[...] = g
        d_ref[...], mn_ref[...], vn_ref[...] = _adam_math(w_ref[...], g, m_ref[...], v_ref[...])

    spec = pl.BlockSpec((tr, C), lambda i: (i, 0))
    return pl.pallas_call(body, out_shape=(SDS((R, C), F32),) * 4, grid=(R // tr,), in_specs=[spec] * 4,
                          out_specs=(spec,) * 4, compiler_params=_cparams("parallel"), name=f"adamw_w{wi}")(w, g, m, v)


def _gain_update(gathered, w, m, v):
    def body(ga_ref, w_ref, m_ref, v_ref, g_ref, d_ref, mn_ref, vn_ref):
        g = ga_ref[0:8, :]
        for dev in range(1, 8):
            g = g + ga_ref[8 * dev:8 * dev + 8, :]
        g_ref[...] = g
        d_ref[...], mn_ref[...], vn_ref[...] = _adam_math(w_ref[...], g, m_ref[...], v_ref[...])

    return pl.pallas_call(body, out_shape=(SDS((8, 1024), F32),) * 4, name="gain_update")(gathered, w, m, v)


GROUP_FFN, GROUP_MIX, GROUP_IN = (4, 5, 6), (1, 2, 3), (0,)
REST = GROUP_MIX + GROUP_FFN


class _MeshComm:
    SCHEDULE = {
        "in_proj": [("ici", (1, 2, 3, 4))],
        "attn_fwd_g0": [("d2d", (1, 2, 3, 4))],
        "ret_fwd": [("ici", (5,))],
        "out_proj": [("d2d", (5,))],
        "ffn_up": [("both", (6,))],
        "ffn_up_bwd": [("pair", GROUP_FFN)],
        "ret_bwd": [("pair", GROUP_MIX), ("chip", (4,))],
        "attn_bwd_g0": [("chip", (5,))],
        "attn_bwd_g1": [("chip", (6,))],
        "attn_bwd_g2": [("chip", GROUP_MIX)],
        "wgrad_in_sent": [("share", GROUP_FFN + GROUP_MIX)],
        "wgrad_in_kept": [("pair", GROUP_IN)],
        "in_proj_bwd": [("chip", GROUP_IN)],
    }

    def __init__(self, shards):
        xi, yi, ci = lax.axis_index("x"), lax.axis_index("y"), lax.axis_index("c")
        self.sidx = jnp.stack([2 * xi + yi, 2 * (1 - xi) + yi, 2 * xi + (1 - yi), 2 * (1 - xi) + (1 - yi), ci]).astype(jnp.int32)
        self.shards, self.full = shards, {}
        self.g, self.own, self.pb, self.half, self.red = {}, {}, {}, {}, {}

    def w_in(self):
        return _gather_now(GROUP_IN, [self.shards[0]])[0]

    def weight(self, wi):
        return self.full[wi].reshape(D_MODEL, D_MODEL) if wi in (2, 3) else self.full[wi]

    def grads(self, by_wi):
        self.g.update(by_wi)

    def _exchange(self, stage, wis):
        pick = lambda table: [table[wi] for wi in wis]
        if stage == "ici":
            return _ex_gather_ici(wis, pick(self.shards))
        if stage == "both":
            return _ex_gather_ici(wis, pick(self.shards), then_d2d=True)
        if stage == "d2d":
            return _ex_gather_d2d(wis, pick(self.full))
        if stage == "pair":
            return _ex_pair(wis, [self.g["in_sent"] if wi == 0 else self.g[wi] for wi in wis])
        if stage == "chip":
            return _ex_chip(wis, pick(self.pb))
        return _ex_share(wis, pick(self.half))

    def _landed(self, stage, wis, res):
        for wi, r in zip(wis, res):
            if stage in ("ici", "d2d", "both"):
                self.full[wi] = r
            elif stage == "pair":
                self.own[wi], self.pb[wi] = _pair_sum(wi, self.g["in_kept"] if wi == 0 else self.g[wi], r, self.sidx)
            elif stage == "chip":
                self.half[wi] = _chip_sum(wi, self.own[wi], r, self.sidx)
            else:
                self.red[wi] = r

    def carry(self, point):
        return [self._exchange(stage, wis) for stage, wis in self.SCHEDULE.get(point, ())]

    def took(self, point, xres):
        for (stage, wis), res in zip(self.SCHEDULE.get(point, ()), xres):
            self._landed(stage, wis, res)

    def reduced(self):
        self._landed("share", GROUP_IN, _exchange_call(self._exchange("share", GROUP_IN), "share_w_in"))
        return [self.red[wi] for wi in range(N_W)]


def kernel(x, norm_mix_g, w_in, w_out_attn, w_out_ret, w_out, norm_ffn_g, w_ffn_gate, w_ffn_up, w_ffn_down, norm_final_g, loss_target, m_norm_mix_g, m_w_in, m_w_out_attn, m_w_out_ret, m_w_out, m_norm_ffn_g, m_w_ffn_gate, m_w_ffn_up, m_w_ffn_down, m_norm_final_g, v_norm_mix_g, v_w_in, v_w_out_attn, v_w_out_ret, v_w_out, v_norm_ffn_g, v_w_ffn_gate, v_w_ffn_up, v_w_ffn_down, v_norm_final_g):
    ws = (w_in, w_out_attn, w_out_ret, w_out, w_ffn_gate, w_ffn_up, w_ffn_down)
    ms = (m_w_in, m_w_out_attn, m_w_out_ret, m_w_out, m_w_ffn_gate, m_w_ffn_up, m_w_ffn_down)
    vs = (v_w_in, v_w_out_attn, v_w_out_ret, v_w_out, v_w_ffn_gate, v_w_ffn_up, v_w_ffn_down)
    shard2d = lambda a, wi: a.reshape(W_SHARD[wi])

    comm = _MeshComm([_cast_bf16(shard2d(w, wi)) for wi, w in enumerate(ws)])
    g3 = norm_final_g.reshape(1, D_MODEL)
    loss_p, grad_x, gain_g = _step(x[0], loss_target[0], norm_mix_g, norm_ffn_g, g3, comm)
    gred = comm.reduced()

    outs_g, outs_d, outs_m, outs_v = [], [], [], []
    for wi in range(N_W):
        g2d = gred[wi].reshape(W_SHARD[wi])
        gout, dlt, mn, vn = _adamw(wi, shard2d(ws[wi], wi), g2d, shard2d(ms[wi], wi), shard2d(vs[wi], wi))
        for lst, a in ((outs_g, gout), (outs_d, dlt), (outs_m, mn), (outs_v, vn)):
            lst.append(a.reshape(ws[wi].shape))

    pad8 = lambda rows: jnp.concatenate([r.reshape(1, D_MODEL) for r in rows]
                                        + [jnp.zeros((8 - len(rows), D_MODEL), F32)], axis=0)
    gathered = _gain_allgather(pad8((*gain_g, jnp.tile(loss_p[0:1], (1, D_MODEL // 128)))))
    gg, gd, gm, gv = _gain_update(gathered, pad8((norm_mix_g, norm_ffn_g, norm_final_g)),
                                  pad8((m_norm_mix_g, m_norm_ffn_g, m_norm_final_g)),
                                  pad8((v_norm_mix_g, v_norm_ffn_g, v_norm_final_g)))
    loss = gg[3, 0]

    def assemble(gain_rows, wlist):
        return (gain_rows[0:1], wlist[0], wlist[1], wlist[2], wlist[3], gain_rows[1:2],
                wlist[4], wlist[5], wlist[6], gain_rows[2])

    return (loss, grad_x[None], *assemble(gg, outs_g), *assemble(gd, outs_d), *assemble(gm, outs_m), *assemble(gv, outs_v))
```

```python
import functools
import math

import numpy as np
import jax
import jax.numpy as jnp
from jax import lax
from jax.experimental import pallas as pl
from jax.experimental.pallas import tpu as pltpu

F32, BF16 = jnp.float32, jnp.bfloat16
SDS = jax.ShapeDtypeStruct
MESH = pl.DeviceIdType.MESH

D_MODEL = 1024
PROJ_W = 9728
COLB = 512
N_COLB = PROJ_W // COLB
QA_B, KA_B, VA_B = 0, 3, 6
QR_B, KR_B = 9, 10
FFN_HID = 2816
N_SHARD = 4
HID_S = FFN_HID // N_SHARD
W_IN_S = PROJ_W // N_SHARD
DILATIONS = (1, 4, 16)
BLK = 128
RET_HEADS = 4
ROPE_THETA = 10000.0
NORM_EPS = 1e-6
ADAM_LR, ADAM_B1, ADAM_B2, ADAM_EPS, ADAM_WD, ADAM_STEP = 0.001, 0.9, 0.999, 1e-08, 0.01, 10
VMEM_LIMIT = 56 << 20


def _cparams(*sem):
    return pltpu.CompilerParams(dimension_semantics=sem or None, vmem_limit_bytes=VMEM_LIMIT)


def _dot(a, b):
    return jnp.dot(a, b, preferred_element_type=F32)


def _dot_nt(a, b):
    return lax.dot_general(a, b, (((1,), (1,)), ((), ())), preferred_element_type=F32)


def _dot_tn(a, b):
    return lax.dot_general(a, b, (((0,), (0,)), ((), ())), preferred_element_type=F32)


def _row_pieces(tm, sub=512):
    return [slice(i, i + sub) for i in range(0, tm, sub)]


def _sigmoid(z):
    return 0.5 * jnp.tanh(0.5 * z) + 0.5


ANY = pl.BlockSpec(memory_space=pl.ANY)


class _Exchange:
    def __init__(self, ins, out_shapes, aliases, n_sem, n_loc, build):
        self.ins, self.out_shapes, self.aliases = list(ins), list(out_shapes), dict(aliases)
        self.n_sem, self.n_loc, self.build = n_sem, n_loc, build

    def sems(self):
        return [pltpu.SemaphoreType.DMA((self.n_sem,)), pltpu.SemaphoreType.DMA((self.n_sem,)),
                pltpu.SemaphoreType.DMA((max(self.n_loc, 1),))]


def _exchange_call(ex, name):
    n_in, n_out = len(ex.ins), len(ex.out_shapes)

    def body(*refs):
        starts, waits = ex.build(refs[:n_in], refs[n_in:n_in + n_out], *refs[n_in + n_out:])
        for cp in starts:
            cp.start()
        for w in waits:
            w()

    return pl.pallas_call(body, out_shape=tuple(ex.out_shapes), in_specs=[ANY] * n_in, out_specs=tuple([ANY] * n_out),
                          input_output_aliases=ex.aliases, scratch_shapes=ex.sems(), name=name)(*ex.ins)


def _carrier_call(body, args, *, out_shape, grid, in_specs, out_specs, scratch_shapes=(), sem, name, exchanges=(),
                  prefetch=None, in_out_aliases=None):
    out_shape, out_specs = tuple(out_shape), tuple(out_specs)
    n_in, n_out, n_scr = len(args), len(out_shape), len(scratch_shapes)
    n_pre = 0 if prefetch is None else 1
    x_args, x_outs, x_scr, spans = [], [], [], []
    aliases = {n_pre + a: o for a, o in (in_out_aliases or {}).items()}
    for ex in exchanges:
        i0, o0 = len(x_args), len(x_outs)
        for a, o in ex.aliases.items():
            aliases[n_pre + n_in + i0 + a] = n_out + o0 + o
        x_args += ex.ins
        x_outs += ex.out_shapes
        x_scr += ex.sems()
        spans.append((i0, len(ex.ins), o0, len(ex.out_shapes)))
    nx_in, nx_out = len(x_args), len(x_outs)

    def wrapped(*refs):
        refs = refs[n_pre:]
        ins, xin = refs[:n_in], refs[n_in:n_in + nx_in]
        o_base = n_in + nx_in
        outs, xout = refs[o_base:o_base + n_out], refs[o_base + n_out:o_base + n_out + nx_out]
        s_base = o_base + n_out + nx_out
        scr, xs = refs[s_base:s_base + n_scr], refs[s_base + n_scr:]

        def built(e):
            i0, ni, o0, no = spans[e]
            return exchanges[e].build(xin[i0:i0 + ni], xout[o0:o0 + no], *xs[3 * e:3 * e + 3])

        if exchanges:
            first = functools.reduce(jnp.logical_and, [pl.program_id(k) == 0 for k in range(len(grid))])
            last = functools.reduce(jnp.logical_and, [pl.program_id(k) == grid[k] - 1 for k in range(len(grid))])

            @pl.when(first)
            def _():
                for e in range(len(exchanges)):
                    for cp in built(e)[0]:
                        cp.start()

        body(*ins, *outs, *scr)

        if exchanges:
            @pl.when(last)
            def _():
                for e in range(len(exchanges)):
                    for w in built(e)[1]:
                        w()

    all_in, all_out = list(in_specs) + [ANY] * nx_in, out_specs + tuple([ANY] * nx_out)
    all_scr = list(scratch_shapes) + x_scr
    cparams = _cparams(*(sem if not exchanges else ("arbitrary",) * len(grid)))
    if prefetch is None:
        res = pl.pallas_call(wrapped, out_shape=out_shape + tuple(x_outs), grid=grid, in_specs=all_in, out_specs=all_out,
                             scratch_shapes=all_scr, input_output_aliases=aliases, compiler_params=cparams,
                             name=name)(*args, *x_args)
    else:
        gs = pltpu.PrefetchScalarGridSpec(num_scalar_prefetch=1, grid=grid, in_specs=all_in, out_specs=all_out,
                                          scratch_shapes=all_scr)
        res = pl.pallas_call(wrapped, out_shape=out_shape + tuple(x_outs), grid_spec=gs, input_output_aliases=aliases,
                             compiler_params=cparams, name=name)(prefetch, *args, *x_args)
    xres = [tuple(res[n_out + o0:n_out + o0 + no]) for (_, _, o0, no) in spans]
    return tuple(res[:n_out]), xres


def _tables(S):
    f32 = np.float32
    pos = np.arange(S, dtype=f32)
    lane = np.arange(128)
    inv = (f32(ROPE_THETA) ** (-np.arange(0, 64, 2, dtype=f32) / f32(64))).astype(f32)
    ang = (pos[:, None] * inv[None, :]).astype(np.float64)
    idx = (lane % 64) % 32
    c, s = np.cos(ang)[:, idx], np.sin(ang)[:, idx]
    first = ((lane % 64) < 32)[None, :]
    rope = np.stack([c, np.where(first, 0.0, s), np.where(first, -s, 0.0)])
    base = (f32(1.0) / (f32(ROPE_THETA) ** np.linspace(0.0, 1.0, 64, dtype=f32))).astype(f32)
    ang2 = (pos[:, None] * base[None, :]).astype(np.float64)
    c2, s2 = np.cos(ang2)[:, lane // 2], np.sin(ang2)[:, lane // 2]
    even = (lane % 2 == 0)[None, :]
    th = np.stack([c2, np.where(even, 0.0, s2), np.where(even, -s2, 0.0)])
    return np.stack([rope, th, th * (128 ** -0.5)]).astype(f32)


def _rot(a, c, sa, sb, shift):
    return a * c + pltpu.roll(a, shift, 1) * sa + pltpu.roll(a, 128 - shift, 1) * sb


def _unrot(g, c, sa, sb, shift):
    return g * c + pltpu.roll(g * sa, 128 - shift, 1) + pltpu.roll(g * sb, shift, 1)


def _ret_consts():
    h = np.arange(RET_HEADS, dtype=np.float64)
    log_g = np.log1p(-(2.0 ** (-5.0 - h)))
    idx = np.arange(BLK, dtype=np.float64)
    diff = idx[:, None] - idx[None, :]
    dmask = np.where(diff[None] >= 0, np.exp(np.maximum(diff, 0.0)[None] * log_g[:, None, None]), 0.0)
    zeta = np.exp((BLK - 1 - idx)[None, :] * log_g[:, None])
    xi = np.exp((idx + 1.0)[None, :] * log_g[:, None])
    dec = np.exp(BLK * log_g)
    rep = lambda v: np.broadcast_to(v[:, :, None], (RET_HEADS, BLK, 128))
    return (jnp.asarray(dmask, F32), jnp.asarray(rep(zeta), F32), jnp.asarray(rep(xi), F32),
            jnp.asarray(np.broadcast_to(dec[:, None, None], (RET_HEADS, 8, 256)), F32))


def _rms_fwd(x, g):
    S = x.shape[0]
    tm = 512

    def body(x_ref, g_ref, h_ref, ht_ref):
        xv = x_ref[...]
        r = lax.rsqrt(jnp.mean(xv * xv, axis=-1, keepdims=True) + NORM_EPS)
        h = xv * r * g_ref[...]
        h_ref[...] = h.astype(BF16)
        ht_ref[...] = h.T.astype(BF16)

    return pl.pallas_call(
        body, out_shape=(SDS((S, D_MODEL), BF16), SDS((D_MODEL, S), BF16)), grid=(S // tm,),
        in_specs=[pl.BlockSpec((tm, D_MODEL), lambda i: (i, 0)), pl.BlockSpec((1, D_MODEL), lambda i: (0, 0))],
        out_specs=(pl.BlockSpec((tm, D_MODEL), lambda i: (i, 0)), pl.BlockSpec((D_MODEL, tm), lambda i: (0, i))),
        compiler_params=_cparams("parallel"), name="rms_fwd")(x, g)


def _in_proj(h, w_in, tab, exchanges=()):
    S = h.shape[0]
    tm = min(S, 4096)

    def body(h_ref, w_ref, t_ref, o_ref):
        j = pl.program_id(1)
        is_rope = j < 6
        is_theta = (j == QR_B) | (j == KR_B)
        sub = 512

        def rotated(shift):
            for i in range(tm // sub):
                rows = slice(i * sub, (i + 1) * sub)
                acc = _dot(h_ref[rows, :], w_ref[...])
                c, sa, sb = t_ref[0, 0, rows, :], t_ref[0, 1, rows, :], t_ref[0, 2, rows, :]
                for k in range(COLB // 128):
                    sl = slice(k * 128, (k + 1) * 128)
                    o_ref[rows, sl] = _rot(acc[:, sl], c, sa, sb, shift).astype(BF16)

        @pl.when(is_rope)
        def _():
            rotated(32)

        @pl.when(is_theta)
        def _():
            rotated(1)

        @pl.when(jnp.logical_not(is_rope | is_theta))
        def _():
            o_ref[...] = _dot(h_ref[...], w_ref[...]).astype(BF16)

    def tab_map(i, j):
        return (jnp.where(j == QR_B, 1, jnp.where(j == KR_B, 2, 0)), 0, i, 0)

    (proj,), xres = _carrier_call(
        body, (h, w_in, tab), out_shape=(SDS((S, PROJ_W), BF16),), grid=(S // tm, N_COLB),
        in_specs=[pl.BlockSpec((tm, D_MODEL), lambda i, j: (i, 0)),
                  pl.BlockSpec((D_MODEL, COLB), lambda i, j: (0, j)),
                  pl.BlockSpec((1, 3, tm, 128), tab_map)],
        out_specs=(pl.BlockSpec((tm, COLB), lambda i, j: (i, j)),),
        sem=("parallel", "arbitrary"), name="in_proj", exchanges=exchanges)
    return proj, xres


def _band_mask(n):
    qi = lax.broadcasted_iota(jnp.int32, (BLK, 2 * BLK), 0)
    kj = lax.broadcasted_iota(jnp.int32, (BLK, 2 * BLK), 1)
    dist = BLK + qi - kj
    return (dist >= 0) & (dist <= BLK) & ((kj >= BLK) | (n > 0))


def _qkv_col(d, gi):
    if d == 1:
        return lambda t, r: 3 * t + gi
    return lambda t, r: 3 * r + t


def _attn_fwd(qkv, d, gi, exchanges=()):
    L = qkv.shape[0]
    nb = L // BLK

    def body(q_ref, kc_ref, kp_ref, vc_ref, vp_ref, o_ref, lse_ref):
        n = pl.program_id(1)
        mask = _band_mask(n)
        mask2 = jnp.concatenate([mask, mask], axis=0)
        lane = lax.broadcasted_iota(jnp.int32, (BLK, 128), 1)
        lo = lane < 64
        lse_all = jnp.zeros((BLK, 128), F32)
        chunks = [slice(c * 128, (c + 1) * 128) for c in range(4)]
        scores, vals = [], []
        for sl in chunks:
            q = q_ref[:, sl]
            k = jnp.concatenate([kp_ref[:, sl], kc_ref[:, sl]], axis=0)
            vals.append(jnp.concatenate([vp_ref[:, sl], vc_ref[:, sl]], axis=0))
            q2 = jnp.concatenate([jnp.where(lo, q, jnp.zeros_like(q)), jnp.where(lo, jnp.zeros_like(q), q)], axis=0)
            scores.append(_dot_nt(q2, k))
        probs = []
        for c, s in enumerate(scores):
            s = jnp.where(mask2, s * 0.125, jnp.float32(-1e30))
            m = jnp.max(s, axis=-1, keepdims=True)
            p = jnp.exp(s - m)
            l = jnp.sum(p, axis=-1, keepdims=True)
            probs.append((p / l).astype(BF16))
            lse = m + jnp.log(l)
            lse_all = jnp.where(lane // 16 == 2 * c, lse[:BLK], jnp.where(lane // 16 == 2 * c + 1, lse[BLK:], lse_all))
        for sl, p, v in zip(chunks, probs, vals):
            o2 = _dot(p, v)
            o_ref[:, sl] = jnp.where(lo, o2[:BLK], o2[BLK:])
        lse_ref[...] = lse_all

    prev = lambda n: jnp.maximum(n - 1, 0)
    col = _qkv_col(d, gi)
    return _carrier_call(
        body, (qkv,) * 5, out_shape=(SDS((L, d * 512), F32), SDS((L, d * 128), F32)), grid=(d, nb),
        in_specs=[pl.BlockSpec((BLK, 512), lambda r, n: (n, col(0, r))),
                  pl.BlockSpec((BLK, 512), lambda r, n: (n, col(1, r))),
                  pl.BlockSpec((BLK, 512), lambda r, n: (prev(n), col(1, r))),
                  pl.BlockSpec((BLK, 512), lambda r, n: (n, col(2, r))),
                  pl.BlockSpec((BLK, 512), lambda r, n: (prev(n), col(2, r)))],
        out_specs=(pl.BlockSpec((BLK, 512), lambda r, n: (n, r)),
                   pl.BlockSpec((BLK, 128), lambda r, n: (n, r))),
        sem=("parallel", "arbitrary"), name=f"attn_fwd_g{gi}", exchanges=exchanges)


def _qkv_to_sub(proj, d, gi):
    S = proj.shape[0]
    tm = 512
    n = tm // d

    def body(q_ref, k_ref, v_ref, o_ref, scr):
        for t, ref in enumerate((q_ref, k_ref, v_ref)):
            for c in range(4):
                scr[c] = ref[:, c * 128:(c + 1) * 128].astype(F32)
            for r in range(d):
                for c in range(4):
                    col = (3 * r + t) * 512 + c * 128
                    o_ref[:, col:col + 128] = scr[c, pl.ds(r, n, stride=d), :].astype(BF16)

    return pl.pallas_call(
        body, out_shape=SDS((S // d, d * 1536), BF16), grid=(S // tm,),
        in_specs=[pl.BlockSpec((tm, 512), lambda i, b=b: (i, b + gi)) for b in (QA_B, KA_B, VA_B)],
        out_specs=pl.BlockSpec((n, d * 1536), lambda i: (i, 0)),
        scratch_shapes=[pltpu.VMEM((4, tm, 128), F32)],
        compiler_params=_cparams("parallel"), name=f"qkv_to_sub_g{gi}")(proj, proj, proj)


def _attn_merge(os_, lses):
    S = os_[0].shape[0]
    tm = 512

    def body(o0, o1, o2, l0, l1, l2, att_ref, lt_ref, so1, so2, sl1, sl2):
        lo = lax.broadcasted_iota(jnp.int32, (tm, 128), 1) < 64

        def natural(ref, d, scr, width):
            nch = width // 128
            if d == 1:
                return [ref[:, c * 128:(c + 1) * 128] for c in range(nch)]
            for r in range(d):
                for c in range(nch):
                    scr[c, pl.ds(r, tm // d, stride=d), :] = ref[:, r * width + c * 128:r * width + (c + 1) * 128]
            return [scr[c] for c in range(nch)]

        ls = [natural(l, d, s, 128)[0] for l, d, s in zip((l0, l1, l2), DILATIONS, (None, sl1, sl2))]
        m = jnp.maximum(jnp.maximum(ls[0], ls[1]), ls[2])
        es = [jnp.exp(v - m) for v in ls]
        z = es[0] + es[1] + es[2]
        lt_ref[...] = m + jnp.log(z)
        ws = [e / z for e in es]
        o_nat = [natural(o, d, s, 512) for o, d, s in zip((o0, o1, o2), DILATIONS, (None, so1, so2))]
        for c in range(4):
            acc = jnp.zeros((tm, 128), F32)
            for g in range(3):
                w_lo = jnp.broadcast_to(ws[g][:, 32 * c:32 * c + 1], (tm, 128))
                w_hi = jnp.broadcast_to(ws[g][:, 32 * c + 16:32 * c + 17], (tm, 128))
                acc = acc + jnp.where(lo, w_lo, w_hi) * o_nat[g][c]
            att_ref[:, c * 128:(c + 1) * 128] = acc.astype(BF16)

    sub = lambda w: [pl.BlockSpec((tm // d, d * w), lambda i: (i, 0)) for d in DILATIONS]
    return pl.pallas_call(
        body, out_shape=(SDS((S, 512), BF16), SDS((S, 128), F32)), grid=(S // tm,),
        in_specs=sub(512) + sub(128),
        out_specs=(pl.BlockSpec((tm, 512), lambda i: (i, 0)), pl.BlockSpec((tm, 128), lambda i: (i, 0))),
        scratch_shapes=[pltpu.VMEM((4, tm, 128), F32), pltpu.VMEM((4, tm, 128), F32),
                        pltpu.VMEM((1, tm, 128), F32), pltpu.VMEM((1, tm, 128), F32)],
        compiler_params=_cparams("parallel"), name="attn_merge")(*os_, *lses)


def _assemble_dproj(att_grads, dproj):
    S = dproj.shape[0]
    tm = 256

    def body(*refs):
        a = [refs[3 * t:3 * t + 3] for t in range(3)]
        dp_prev, o_ref, scr = refs[9:]
        for t in range(3):
            for g, d in enumerate(DILATIONS):
                base = (3 * t + g) * COLB
                if d == 1:
                    o_ref[:, base:base + COLB] = a[t][g][...]
                    continue
                for c in range(4):
                    for r in range(d):
                        scr[c, pl.ds(r, tm // d, stride=d), :] = a[t][g][:, r * 512 + c * 128:r * 512 + (c + 1) * 128].astype(F32)
                    o_ref[:, base + c * 128:base + (c + 1) * 128] = scr[c].astype(BF16)

    sub = [pl.BlockSpec((tm // d, d * 512), lambda i: (i, 0)) for d in DILATIONS]
    flat = [att_grads[t][g] for t in range(3) for g in range(3)]
    return pl.pallas_call(
        body, out_shape=SDS((S, PROJ_W), BF16), grid=(S // tm,),
        in_specs=sub * 3 + [ANY], out_specs=pl.BlockSpec((tm, 9 * COLB), lambda i: (i, 0)),
        scratch_shapes=[pltpu.VMEM((4, tm, 128), F32)], input_output_aliases={9: 0},
        compiler_params=_cparams("parallel"), name="assemble_dproj")(*flat, dproj)


def _ret_fwd(proj, consts, exchanges=()):
    S = proj.shape[0]
    nc = S // BLK
    dmask, zeta, xi, dec = consts

    def body(q_ref, k_ref, v0_ref, v1_ref, g0_ref, g1_ref, dm_ref, z_ref, x_ref, dec_ref,
             y_ref, rn_ref, rs_ref, st_ref, R):
        @pl.when(pl.program_id(0) == 0)
        def _():
            R[...] = jnp.zeros_like(R)

        lane16 = lax.broadcasted_iota(jnp.int32, (BLK, 128), 1) // 16
        rs_all = jnp.zeros((BLK, 128), F32)
        first = []
        for h in range(RET_HEADS):
            hs = slice(h * 128, (h + 1) * 128)
            q, k = q_ref[:, hs], k_ref[:, hs]
            v = (v0_ref if h < 2 else v1_ref)[:, (h % 2) * 256:(h % 2 + 1) * 256]
            Rb = R[h].astype(BF16)
            st_ref[h] = Rb
            kz = (k.astype(F32) * z_ref[h]).astype(BF16)
            first.append((v, _dot_nt(q, k), _dot((q.astype(F32) * x_ref[h]).astype(BF16), Rb), _dot_tn(kz, v)))
        masked = [(s * dm_ref[h]).astype(BF16) for h, (_, s, _, _) in enumerate(first)]
        for h in range(RET_HEADS):
            vs = slice((h % 2) * 256, (h % 2 + 1) * 256)
            os_ = slice(h * 256, (h + 1) * 256)
            v, _, cross, kv = first[h]
            o = _dot(masked[h], v) + cross
            R[h] = R[h] * dec_ref[h, 0:1, :] + kv
            mu = jnp.mean(o, axis=-1, keepdims=True)
            oc = o - mu
            rstd = lax.rsqrt(jnp.mean(oc * oc, axis=-1, keepdims=True) + NORM_EPS)
            rn = oc * rstd
            gr = (g0_ref if h < 2 else g1_ref)[:, vs].astype(F32)
            y_ref[:, os_] = (rn * gr * _sigmoid(gr)).astype(BF16)
            rn_ref[:, os_] = rn.astype(BF16)
            rs_all = jnp.where(lane16 == h, rstd, rs_all)
        rs_ref[...] = rs_all

    cst = lambda shape: pl.BlockSpec(shape, lambda c: (0, 0, 0))
    blk = lambda j: pl.BlockSpec((BLK, 512), lambda c: (c, j))
    return _carrier_call(
        body, (proj, proj, proj, proj, proj, proj, dmask, zeta, xi, dec),
        out_shape=(SDS((S, 1024), BF16), SDS((S, 1024), BF16), SDS((S, 128), F32), SDS((RET_HEADS, nc, BLK, 256), BF16)),
        grid=(nc,),
        in_specs=[blk(QR_B), blk(KR_B), blk(11), blk(12), blk(13), blk(14),
                  cst((RET_HEADS, BLK, BLK)), cst((RET_HEADS, BLK, 128)), cst((RET_HEADS, BLK, 128)), cst((RET_HEADS, 8, 256))],
        out_specs=(pl.BlockSpec((BLK, 1024), lambda c: (c, 0)), pl.BlockSpec((BLK, 1024), lambda c: (c, 0)),
                   pl.BlockSpec((BLK, 128), lambda c: (c, 0)),
                   pl.BlockSpec((RET_HEADS, None, BLK, 256), lambda c: (0, c, 0, 0))),
        scratch_shapes=[pltpu.VMEM((RET_HEADS, BLK, 256), F32)],
        sem=("arbitrary",), name="ret_fwd", exchanges=exchanges)


def _branch_merge(att, yrin, proj, wa, wr):
    S = att.shape[0]
    tm = min(S, 2048)

    def body(a_ref, y_ref, ga_ref, gr_ref, wa_ref, wr_ref, m_ref, ya_ref, yr_ref):
        for rows in _row_pieces(tm):
            ya = _dot(a_ref[rows, :], wa_ref[...])
            yr = _dot(y_ref[rows, :], wr_ref[...])
            m_ref[rows, :] = (_sigmoid(ga_ref[rows, :].astype(F32)) * ya
                              + _sigmoid(gr_ref[rows, :].astype(F32)) * yr).astype(BF16)
            ya_ref[rows, :] = ya.astype(BF16)
            yr_ref[rows, :] = yr.astype(BF16)

    ospec = pl.BlockSpec((tm, 512), lambda i, j: (i, j))
    return pl.pallas_call(
        body, out_shape=(SDS((S, D_MODEL), BF16),) * 3, grid=(S // tm, 2),
        in_specs=[pl.BlockSpec((tm, 512), lambda i, j: (i, 0)), pl.BlockSpec((tm, 1024), lambda i, j: (i, 0)),
                  pl.BlockSpec((tm, 512), lambda i, j: (i, 15 + j)), pl.BlockSpec((tm, 512), lambda i, j: (i, 17 + j)),
                  pl.BlockSpec((512, 512), lambda i, j: (0, j)), pl.BlockSpec((1024, 512), lambda i, j: (0, j))],
        out_specs=(ospec, ospec, ospec),
        compiler_params=_cparams("parallel", "arbitrary"), name="branch_merge")(att, yrin, proj, proj, wa, wr)


def _out_proj(merged, wo, x, g2, exchanges=()):
    S = x.shape[0]
    tm = 1024

    def body(m_ref, w_ref, x_ref, g_ref, x1_ref, h2_ref):
        x1 = x_ref[...] + _dot(m_ref[...], w_ref[...])
        x1_ref[...] = x1
        r = lax.rsqrt(jnp.mean(x1 * x1, axis=-1, keepdims=True) + NORM_EPS)
        h2_ref[...] = (x1 * r * g_ref[...]).astype(BF16)

    row = pl.BlockSpec((tm, D_MODEL), lambda i: (i, 0))
    return _carrier_call(
        body, (merged, wo, x, g2), out_shape=(SDS((S, D_MODEL), F32), SDS((S, D_MODEL), BF16)), grid=(S // tm,),
        in_specs=[row, pl.BlockSpec((D_MODEL, D_MODEL), lambda i: (0, 0)), row, pl.BlockSpec((1, D_MODEL), lambda i: (0, 0))],
        out_specs=(row, row), sem=("parallel",), name="out_proj", exchanges=exchanges)


def _ffn_up(h2, wg, wu, exchanges=()):
    S = h2.shape[0]
    tm = min(S, 2048)

    def body(h_ref, wg_ref, wu_ref, g_ref, u_ref, a_ref):
        for rows in _row_pieces(tm):
            hv = h_ref[rows, :]
            g = _dot(hv, wg_ref[...])
            u = _dot(hv, wu_ref[...])
            g_ref[rows, :] = g.astype(BF16)
            u_ref[rows, :] = u.astype(BF16)
            a_ref[rows, :] = (g * _sigmoid(g) * u).astype(BF16)

    wspec = pl.BlockSpec((None, D_MODEL, HID_S), lambda i, s: (s, 0, 0))
    ospec = pl.BlockSpec((None, tm, HID_S), lambda i, s: (s, i, 0))
    return _carrier_call(
        body, (h2, wg, wu), out_shape=(SDS((N_SHARD, S, HID_S), BF16),) * 3, grid=(S // tm, N_SHARD),
        in_specs=[pl.BlockSpec((tm, D_MODEL), lambda i, s: (i, 0)), wspec, wspec],
        out_specs=(ospec, ospec, ospec),
        sem=("parallel", "arbitrary"), name="ffn_up", exchanges=exchanges)


def _ffn_down_loss(act, wd, x1, g3, tgt):
    S = x1.shape[0]
    tm = 512

    def body(a_ref, w_ref, x_ref, g_ref, t_ref, dx_ref, dxb_ref, dg_ref, ls_ref):
        @pl.when(pl.program_id(0) == 0)
        def _():
            dg_ref[...] = jnp.zeros_like(dg_ref)
            ls_ref[...] = jnp.zeros_like(ls_ref)

        g = g_ref[...]
        for rows in _row_pieces(tm, 256):
            y = _dot(a_ref[0, rows, :], w_ref[0])
            for s in range(1, N_SHARD):
                y = y + _dot(a_ref[s, rows, :], w_ref[s])
            x2 = x_ref[rows, :] + y
            r = lax.rsqrt(jnp.mean(x2 * x2, axis=-1, keepdims=True) + NORM_EPS)
            xh = x2 * r
            err = xh * g - t_ref[rows, :]
            ls_ref[...] += jnp.sum(jnp.sum(err * err, axis=-1, keepdims=True), axis=0, keepdims=True) * (0.5 / D_MODEL)
            dy = err * (1.0 / D_MODEL)
            dg_ref[...] += jnp.sum(dy * xh, axis=0, keepdims=True)
            dxh = dy * g
            dx = r * (dxh - xh * jnp.mean(dxh * xh, axis=-1, keepdims=True))
            dx_ref[rows, :] = dx
            dxb_ref[rows, :] = dx.astype(BF16)

    row = pl.BlockSpec((tm, D_MODEL), lambda i: (i, 0))
    vec = pl.BlockSpec((1, D_MODEL), lambda i: (0, 0))
    return pl.pallas_call(
        body, out_shape=(SDS((S, D_MODEL), F32), SDS((S, D_MODEL), BF16), SDS((1, D_MODEL), F32), SDS((8, 128), F32)),
        grid=(S // tm,),
        in_specs=[pl.BlockSpec((N_SHARD, tm, HID_S), lambda i: (0, i, 0)),
                  pl.BlockSpec((N_SHARD, HID_S, D_MODEL), lambda i: (0, 0, 0), pipeline_mode=pl.Buffered(1)),
                  row, vec, row],
        out_specs=(row, row, vec, pl.BlockSpec((8, 128), lambda i: (0, 0))),
        compiler_params=_cparams("arbitrary"), name="ffn_down_loss")(act, wd, x1, g3, tgt)


def _ffn_down_bwd(dx2b, wd, gte, up):
    S = dx2b.shape[0]
    tm = min(S, 2048)

    def body(d_ref, w_ref, g_ref, u_ref, dg_ref, du_ref):
        for rows in _row_pieces(tm, 256):
            da = _dot_nt(d_ref[rows, :], w_ref[...])
            g = g_ref[rows, :].astype(F32)
            sg = _sigmoid(g)
            dg_ref[rows, :] = (da * u_ref[rows, :].astype(F32) * sg * (1.0 + g * (1.0 - sg))).astype(BF16)
            du_ref[rows, :] = (da * g * sg).astype(BF16)

    aspec = pl.BlockSpec((None, tm, HID_S), lambda i, s: (s, i, 0))
    return pl.pallas_call(
        body, out_shape=(SDS((N_SHARD, S, HID_S), BF16),) * 2, grid=(S // tm, N_SHARD),
        in_specs=[pl.BlockSpec((tm, D_MODEL), lambda i, s: (i, 0)),
                  pl.BlockSpec((None, HID_S, D_MODEL), lambda i, s: (s, 0, 0)), aspec, aspec],
        out_specs=(aspec, aspec),
        compiler_params=_cparams("parallel", "arbitrary"), name="ffn_down_bwd")(dx2b, wd, gte, up)


def _wgrad(name, a, b, a_spec, b_spec, out_shape, out_spec, n_par, S):
    tk = 2048

    def body(a_ref, b_ref, o_ref):
        @pl.when(pl.program_id(1) == 0)
        def _():
            o_ref[...] = jnp.zeros_like(o_ref)

        o_ref[...] += _dot_tn(a_ref[...], b_ref[...])

    return pl.pallas_call(
        body, out_shape=SDS(out_shape, F32), grid=(n_par, S // tk),
        in_specs=[a_spec(tk), b_spec(tk)], out_specs=out_spec,
        compiler_params=_cparams("parallel", "arbitrary"), name=name)(a, b)


def _ffn_up_bwd(dgte, dup, wg, wu, x1, g2, dx2, exchanges=()):
    S = x1.shape[0]
    tm = 512

    def body(dg_ref, du_ref, wg_ref, wu_ref, x_ref, g_ref, dx2_ref, dx_ref, dxb_ref, dgn_ref):
        @pl.when(pl.program_id(0) == 0)
        def _():
            dgn_ref[...] = jnp.zeros_like(dgn_ref)

        for rows in _row_pieces(tm, 256):
            dh = _dot_nt(dg_ref[0, rows, :], wg_ref[0]) + _dot_nt(du_ref[0, rows, :], wu_ref[0])
            for s in range(1, N_SHARD):
                dh = dh + _dot_nt(dg_ref[s, rows, :], wg_ref[s]) + _dot_nt(du_ref[s, rows, :], wu_ref[s])
            xv = x_ref[rows, :]
            r = lax.rsqrt(jnp.mean(xv * xv, axis=-1, keepdims=True) + NORM_EPS)
            xh = xv * r
            dgn_ref[...] += jnp.sum(dh * xh, axis=0, keepdims=True)
            dxh = dh * g_ref[...]
            dx = dx2_ref[rows, :] + r * (dxh - xh * jnp.mean(dxh * xh, axis=-1, keepdims=True))
            dx_ref[rows, :] = dx
            dxb_ref[rows, :] = dx.astype(BF16)

    row = pl.BlockSpec((tm, D_MODEL), lambda i: (i, 0))
    vec = pl.BlockSpec((1, D_MODEL), lambda i: (0, 0))
    aspec = pl.BlockSpec((N_SHARD, tm, HID_S), lambda i: (0, i, 0))
    wspec = pl.BlockSpec((N_SHARD, D_MODEL, HID_S), lambda i: (0, 0, 0), pipeline_mode=pl.Buffered(1))
    return _carrier_call(
        body, (dgte, dup, wg, wu, x1, g2, dx2),
        out_shape=(SDS((S, D_MODEL), F32), SDS((S, D_MODEL), BF16), SDS((1, D_MODEL), F32)),
        grid=(S // tm,),
        in_specs=[aspec, aspec, wspec, wspec, row, vec, row], out_specs=(row, row, vec),
        sem=("arbitrary",), name="ffn_up_bwd", exchanges=exchanges)


def _out_proj_bwd(dx1b, wo, proj, ya, yr):
    S = dx1b.shape[0]
    tm = 512
    gate0 = 15 * COLB

    def body(d_ref, w_ref, ga_ref, gr_ref, ya_ref, yr_ref, dya_ref, dyr_ref, dp_ref):
        for rows in _row_pieces(tm, 256):
            dm = _dot_nt(d_ref[rows, :], w_ref[...])
            sa = _sigmoid(ga_ref[rows, :].astype(F32))
            sr = _sigmoid(gr_ref[rows, :].astype(F32))
            dya_ref[rows, :] = (dm * sa).astype(BF16)
            dyr_ref[rows, :] = (dm * sr).astype(BF16)
            dp_ref[rows, 0:D_MODEL] = (dm * ya_ref[rows, :].astype(F32) * sa * (1.0 - sa)).astype(BF16)
            dp_ref[rows, D_MODEL:2 * D_MODEL] = (dm * yr_ref[rows, :].astype(F32) * sr * (1.0 - sr)).astype(BF16)

    row = pl.BlockSpec((tm, D_MODEL), lambda i: (i, 0))
    cols = lambda c0, w: pl.BlockSpec((pl.Element(tm), pl.Element(w)), lambda i: (i * tm, c0))
    return pl.pallas_call(
        body, out_shape=(SDS((S, D_MODEL), BF16), SDS((S, D_MODEL), BF16), SDS((S, PROJ_W), BF16)), grid=(S // tm,),
        in_specs=[row, pl.BlockSpec((D_MODEL, D_MODEL), lambda i: (0, 0), pipeline_mode=pl.Buffered(1)),
                  cols(gate0, D_MODEL), cols(gate0 + D_MODEL, D_MODEL), row, row],
        out_specs=(row, row, cols(gate0, 2 * D_MODEL)),
        compiler_params=_cparams("parallel"), name="out_proj_bwd")(dx1b, wo, proj, proj, ya, yr)


def _branch_bwd(dya, dyr, wa, wr, att):
    S = dya.shape[0]
    tm = 1024

    def body(da_ref, dr_ref, wa_ref, wr_ref, att_ref, datt_ref, rho_ref, dyi_ref):
        datt = _dot_nt(da_ref[...], wa_ref[...])
        datt_ref[...] = datt.astype(BF16)
        dyi_ref[...] = _dot_nt(dr_ref[...], wr_ref[...]).astype(BF16)
        prod = datt * att_ref[...].astype(F32)
        lane = lax.broadcasted_iota(jnp.int32, (tm, 128), 1)
        lo = lane < 64
        rho = jnp.zeros((tm, 128), F32)
        for c in range(4):
            pc = prod[:, c * 128:(c + 1) * 128]
            tot = jnp.sum(pc, axis=-1, keepdims=True)
            low = jnp.sum(jnp.where(lo, pc, 0.0), axis=-1, keepdims=True)
            rho = jnp.where(lane // 16 == 2 * c, low, jnp.where(lane // 16 == 2 * c + 1, tot - low, rho))
        rho_ref[...] = rho

    row = lambda w: pl.BlockSpec((tm, w), lambda i: (i, 0))
    return pl.pallas_call(
        body, out_shape=(SDS((S, 512), BF16), SDS((S, 128), F32), SDS((S, 1024), BF16)), grid=(S // tm,),
        in_specs=[row(1024), row(1024), pl.BlockSpec((512, 1024), lambda i: (0, 0)),
                  pl.BlockSpec((1024, 1024), lambda i: (0, 0)), row(512)],
        out_specs=(row(512), row(128), row(1024)),
        compiler_params=_cparams("parallel"), name="branch_bwd")(dya, dyr, wa, wr, att)


def _attn_bwd(qkv, datt, lse, rho, rtab, d, gi, exchanges=()):
    L = qkv.shape[0]
    nb = L // BLK
    T = d * nb

    def body(q_ref, kc_ref, kp_ref, vc_ref, vp_ref, do_ref, lse_ref, rho_ref, tq_ref, tk_ref,
             dq_ref, dk_ref, dv_ref, ck, cv):
        t = pl.program_id(0)
        n = jnp.minimum(t, T - 1) % nb

        @pl.when(t == 0)
        def _():
            ck[...] = jnp.zeros_like(ck)
            cv[...] = jnp.zeros_like(cv)

        def store_rot(ref, val, t_ref, c):
            sl = slice(c * 128, (c + 1) * 128)
            ref[:, sl] = _unrot(val, t_ref[0], t_ref[1], t_ref[2], 32).astype(BF16)

        @pl.when(t < T)
        def _():
            mask = _band_mask(n)
            mask2 = jnp.concatenate([mask, mask], axis=0)
            lo = lax.broadcasted_iota(jnp.int32, (BLK, 128), 1) < 64

            def stacked(a):
                return jnp.concatenate([jnp.where(lo, a, jnp.zeros_like(a)), jnp.where(lo, jnp.zeros_like(a), a)], axis=0)

            def head_cols(ref, c):
                return jnp.concatenate([jnp.broadcast_to(ref[:, 32 * c:32 * c + 1], (BLK, 2 * BLK)),
                                        jnp.broadcast_to(ref[:, 32 * c + 16:32 * c + 17], (BLK, 2 * BLK))], axis=0)

            ops, raw = [], []
            for c in range(4):
                sl = slice(c * 128, (c + 1) * 128)
                q2, do2 = stacked(q_ref[:, sl]), stacked(do_ref[:, sl])
                k = jnp.concatenate([kp_ref[:, sl], kc_ref[:, sl]], axis=0)
                v = jnp.concatenate([vp_ref[:, sl], vc_ref[:, sl]], axis=0)
                ops.append((q2, do2, k))
                raw.append((_dot_nt(q2, k), _dot_nt(do2, v)))
            grads = []
            for c, (s, dp) in enumerate(raw):
                p = jnp.where(mask2, jnp.exp(s * 0.125 - head_cols(lse_ref, c)), 0.0)
                grads.append(((p * (dp - head_cols(rho_ref, c)) * 0.125).astype(BF16), p.astype(BF16)))
            for c, ((q2, do2, k), (ds, pb)) in enumerate(zip(ops, grads)):
                sl = slice(c * 128, (c + 1) * 128)
                dq2 = _dot(ds, k)
                dq_c = jnp.where(lo, dq2[:BLK], dq2[BLK:])
                dk_c = _dot_tn(ds, q2)
                dv_c = _dot_tn(pb, do2)
                store_rot(dq_ref, dq_c, tq_ref, c)
                store_rot(dk_ref, ck[:, sl] + dk_c[:BLK], tk_ref, c)
                dv_ref[:, sl] = (cv[:, sl] + dv_c[:BLK]).astype(BF16)
                ck[:, sl] = dk_c[BLK:]
                cv[:, sl] = dv_c[BLK:]

        @pl.when(t == T)
        def _():
            for c in range(4):
                sl = slice(c * 128, (c + 1) * 128)
                store_rot(dk_ref, ck[:, sl], tk_ref, c)
            dv_ref[...] = cv[...].astype(BF16)

    blk_of = lambda t: (jnp.minimum(t, T - 1) % nb, jnp.minimum(t, T - 1) // nb)
    cur = lambda t: blk_of(t)
    prev = lambda t: (jnp.maximum(blk_of(t)[0] - 1, 0), blk_of(t)[1])
    fin = lambda t: blk_of(jnp.maximum(t - 1, 0))
    col = _qkv_col(d, gi)
    qkv_spec = lambda kind, which: pl.BlockSpec((BLK, 512), lambda t: (which(t)[0], col(kind, which(t)[1])))
    row_spec = lambda w, which: pl.BlockSpec((BLK, w), lambda t: which(t))
    tab_spec = lambda which: pl.BlockSpec((3, BLK, 128), lambda t: (0, *which(t)))
    return _carrier_call(
        body, (qkv, qkv, qkv, qkv, qkv, datt, lse, rho, rtab, rtab),
        out_shape=(SDS((L, d * 512), BF16),) * 3, grid=(T + 1,),
        in_specs=[qkv_spec(0, cur), qkv_spec(1, cur), qkv_spec(1, prev), qkv_spec(2, cur), qkv_spec(2, prev),
                  row_spec(512, cur), row_spec(128, cur), row_spec(128, cur), tab_spec(cur), tab_spec(fin)],
        out_specs=(row_spec(512, cur), row_spec(512, fin), row_spec(512, fin)),
        scratch_shapes=[pltpu.VMEM((BLK, 512), F32), pltpu.VMEM((BLK, 512), F32)],
        sem=("arbitrary",), name=f"attn_bwd_g{gi}", exchanges=exchanges)


def _ret_bwd(proj, rn, rstd, dyrin, states, tab, consts, dproj, exchanges=()):
    S = proj.shape[0]
    nc = S // BLK
    dmask, zeta, xi, dec = consts

    def body(q_ref, k_ref, v0_ref, v1_ref, g0_ref, g1_ref, rn_ref, rs_ref, dy_ref, st_ref, tq_ref, tk_ref,
             dm_ref, z_ref, x_ref, dec_ref, dp_prev, dp_ref, dR):
        dq_ref, dk_ref = dp_ref.at[:, 0:512], dp_ref.at[:, 512:1024]
        dv_ref, dgr_ref = dp_ref.at[:, 1024:2048], dp_ref.at[:, 2048:3072]

        @pl.when(pl.program_id(0) == 0)
        def _():
            dR[...] = jnp.zeros_like(dR)

        dobs = []
        for h in range(RET_HEADS):
            vs = slice((h % 2) * 256, (h % 2 + 1) * 256)
            os_ = slice(h * 256, (h + 1) * 256)
            gr = (g0_ref if h < 2 else g1_ref)[:, vs].astype(F32)
            sg = _sigmoid(gr)
            rn_v = rn_ref[:, os_].astype(F32)
            dyi = dy_ref[:, os_].astype(F32)
            dgr_ref[:, os_] = (dyi * rn_v * sg * (1.0 + gr * (1.0 - sg))).astype(BF16)
            drn = dyi * gr * sg
            rstd = jnp.broadcast_to(rs_ref[:, 16 * h:16 * h + 1], (BLK, 256))
            do = rstd * (drn - jnp.mean(drn, axis=-1, keepdims=True) - rn_v * jnp.mean(drn * rn_v, axis=-1, keepdims=True))
            dobs.append(do.astype(BF16))
        first = []
        for h in range(RET_HEADS):
            hs = slice(h * 128, (h + 1) * 128)
            q, k = q_ref[:, hs], k_ref[:, hs]
            v = (v0_ref if h < 2 else v1_ref)[:, (h % 2) * 256:(h % 2 + 1) * 256]
            dob, dRb = dobs[h], dR[h].astype(BF16)
            kz = (k.astype(F32) * z_ref[h]).astype(BF16)
            qx = (q.astype(F32) * x_ref[h]).astype(BF16)
            first.append((q, k, _dot_nt(q, k), _dot_nt(dob, v), _dot(kz, dRb), _dot_nt(dob, st_ref[h]),
                          _dot_nt(v, dRb), _dot_tn(qx, dob)))
        masked = [((s * dm_ref[h]).astype(BF16), (dsr * dm_ref[h]).astype(BF16))
                  for h, (_, _, s, dsr, _, _, _, _) in enumerate(first)]
        for h in range(RET_HEADS):
            hs = slice(h * 128, (h + 1) * 128)
            os_ = slice(h * 256, (h + 1) * 256)
            q, k, _, _, dv_state, dq_state, dk_state, dr_new = first[h]
            sD, dS = masked[h]
            dv_ref[:, os_] = (_dot_tn(sD, dobs[h]) + dv_state).astype(BF16)
            dq = _dot(dS, k) + dq_state * x_ref[h]
            dk = _dot_tn(dS, q) + dk_state * z_ref[h]
            dR[h] = dR[h] * dec_ref[h, 0:1, :] + dr_new
            dq_ref[:, hs] = _unrot(dq, tq_ref[0], tq_ref[1], tq_ref[2], 1).astype(BF16)
            dk_ref[:, hs] = _unrot(dk, tk_ref[0], tk_ref[1], tk_ref[2], 1).astype(BF16)

    rc = lambda c: nc - 1 - c
    cst = lambda shape: pl.BlockSpec(shape, lambda c: (0, 0, 0))
    blk = lambda j: pl.BlockSpec((BLK, 512), lambda c: (rc(c), j))
    row = lambda w: pl.BlockSpec((BLK, w), lambda c: (rc(c), 0))
    (dproj,), xres = _carrier_call(
        body, (proj, proj, proj, proj, proj, proj, rn, rstd, dyrin, states, tab, tab, dmask, zeta, xi, dec, dproj),
        out_shape=(SDS((S, PROJ_W), BF16),), grid=(nc,),
        in_specs=[blk(QR_B), blk(KR_B), blk(11), blk(12), blk(13), blk(14), row(1024), row(128), row(1024),
                  pl.BlockSpec((RET_HEADS, None, BLK, 256), lambda c: (0, rc(c), 0, 0)),
                  pl.BlockSpec((None, 3, BLK, 128), lambda c: (1, 0, rc(c), 0)),
                  pl.BlockSpec((None, 3, BLK, 128), lambda c: (2, 0, rc(c), 0)),
                  cst((RET_HEADS, BLK, BLK)), cst((RET_HEADS, BLK, 128)), cst((RET_HEADS, BLK, 128)), cst((RET_HEADS, 8, 256)),
                  ANY],
        out_specs=(pl.BlockSpec((pl.Element(BLK), pl.Element(6 * COLB)), lambda c: (rc(c) * BLK, QR_B * COLB)),),
        scratch_shapes=[pltpu.VMEM((RET_HEADS, BLK, 256), F32)],
        sem=("arbitrary",), name="ret_bwd", exchanges=exchanges, in_out_aliases={16: 0})
    return dproj, xres


def _wgrad_in_half(ht, dproj, sidx, kept, exchanges=()):
    S = dproj.shape[0]
    tk = 2048
    half = (lambda sx: sx[4]) if kept else (lambda sx: 1 - sx[4])

    def body(a_ref, b_ref, o_ref):
        @pl.when(pl.program_id(1) == 0)
        def _():
            o_ref[...] = jnp.zeros_like(o_ref)

        o_ref[...] += _dot(a_ref[...], b_ref[...])

    (g,), xres = _carrier_call(
        body, (ht, dproj), out_shape=(SDS((D_MODEL // 2, PROJ_W), F32),), grid=(N_SHARD, S // tk),
        in_specs=[pl.BlockSpec((D_MODEL // 2, tk), lambda s, k, sx: (half(sx), k)),
                  pl.BlockSpec((tk, W_IN_S), lambda s, k, sx: (k, s))],
        out_specs=(pl.BlockSpec((D_MODEL // 2, W_IN_S), lambda s, k, sx: (0, s)),),
        sem=("parallel", "arbitrary"), name="wgrad_in_kept" if kept else "wgrad_in_sent", exchanges=exchanges,
        prefetch=sidx)
    return g, xres


def _in_proj_bwd(dproj, w_in, x, g1, dx1, exchanges=()):
    S = x.shape[0]
    tm = 1024

    def body(d_ref, w_ref, x_ref, g_ref, dx1_ref, dx_ref, dgn_ref, acc):
        i, s = pl.program_id(0), pl.program_id(1)

        @pl.when(s == 0)
        def _():
            acc[...] = jnp.zeros_like(acc)

        @pl.when((i == 0) & (s == 0))
        def _():
            dgn_ref[...] = jnp.zeros_like(dgn_ref)

        acc[...] += _dot_nt(d_ref[...], w_ref[...])

        @pl.when(s == N_SHARD - 1)
        def _():
            xv = x_ref[...]
            r = lax.rsqrt(jnp.mean(xv * xv, axis=-1, keepdims=True) + NORM_EPS)
            xh = xv * r
            dh = acc[...]
            dgn_ref[...] += jnp.sum(dh * xh, axis=0, keepdims=True)
            dxh = dh * g_ref[...]
            dx_ref[...] = dx1_ref[...] + r * (dxh - xh * jnp.mean(dxh * xh, axis=-1, keepdims=True))

    row = pl.BlockSpec((tm, D_MODEL), lambda i, s: (i, 0))
    vec = pl.BlockSpec((1, D_MODEL), lambda i, s: (0, 0))
    (gx, dg), xres = _carrier_call(
        body, (dproj, w_in, x, g1, dx1),
        out_shape=(SDS((S, D_MODEL), F32), SDS((1, D_MODEL), F32)), grid=(S // tm, N_SHARD),
        in_specs=[pl.BlockSpec((tm, W_IN_S), lambda i, s: (i, s)),
                  pl.BlockSpec((D_MODEL, W_IN_S), lambda i, s: (0, s)), row, vec, row],
        out_specs=(row, vec), scratch_shapes=[pltpu.VMEM((tm, D_MODEL), F32)],
        sem=("arbitrary", "arbitrary"), name="in_proj_bwd", exchanges=exchanges)
    return gx, dg, xres


def _sub_view(a, d):
    S, W = a.shape
    return a.reshape(S // d, d * W)


def _step(x, tgt, g1, g2, g3, comm):
    S = x.shape[0]
    tab_np = _tables(S)
    tab = jnp.asarray(tab_np)
    consts = _ret_consts()

    h, ht = _rms_fwd(x, g1)
    w_in = comm.w_in()
    proj, xres = _in_proj(h, w_in, tab, comm.carry("in_proj"))
    comm.took("in_proj", xres)
    qkvs, o_parts, lse_parts = [], [], []
    for gi, d in enumerate(DILATIONS):
        qkv = proj if d == 1 else _qkv_to_sub(proj, d, gi)
        (o_g, lse_g), xres = _attn_fwd(qkv, d, gi, comm.carry(f"attn_fwd_g{gi}"))
        comm.took(f"attn_fwd_g{gi}", xres)
        qkvs.append(qkv)
        o_parts.append(o_g)
        lse_parts.append(lse_g)
    att, lse_tot = _attn_merge(o_parts, lse_parts)
    (yrin, rn, rstd, states), xres = _ret_fwd(proj, consts, comm.carry("ret_fwd"))
    comm.took("ret_fwd", xres)
    wa, wr, wo = comm.weight(1), comm.weight(2), comm.weight(3)
    merged, ya, yr = _branch_merge(att, yrin, proj, wa, wr)
    (x1, h2), xres = _out_proj(merged, wo, x, g2, comm.carry("out_proj"))
    comm.took("out_proj", xres)
    wg, wu = comm.weight(4), comm.weight(5)
    (gte, up, act), xres = _ffn_up(h2, wg, wu, comm.carry("ffn_up"))
    comm.took("ffn_up", xres)
    wd = comm.weight(6)
    dx2, dx2b, dg3, loss_p = _ffn_down_loss(act, wd, x1, g3, tgt)

    dgte, dup = _ffn_down_bwd(dx2b, wd, gte, up)
    tok3 = lambda w: (lambda tk: pl.BlockSpec((None, tk, w), lambda p, k: (p, k, 0)))
    tok2 = lambda w: (lambda tk: pl.BlockSpec((tk, w), lambda p, k: (k, 0)))
    g_d = _wgrad("wgrad_down", act, dx2b, tok3(HID_S), tok2(D_MODEL), (N_SHARD, HID_S, D_MODEL),
                 pl.BlockSpec((None, HID_S, D_MODEL), lambda p, k: (p, 0, 0)), N_SHARD, S)
    g_g = _wgrad("wgrad_gate", h2, dgte, tok2(D_MODEL), tok3(HID_S), (N_SHARD, D_MODEL, HID_S),
                 pl.BlockSpec((None, D_MODEL, HID_S), lambda p, k: (p, 0, 0)), N_SHARD, S)
    g_u = _wgrad("wgrad_up", h2, dup, tok2(D_MODEL), tok3(HID_S), (N_SHARD, D_MODEL, HID_S),
                 pl.BlockSpec((None, D_MODEL, HID_S), lambda p, k: (p, 0, 0)), N_SHARD, S)
    comm.grads({4: g_g, 5: g_u, 6: g_d})
    (dx1, dx1b, dg2), xres = _ffn_up_bwd(dgte, dup, wg, wu, x1, g2, dx2, comm.carry("ffn_up_bwd"))
    comm.took("ffn_up_bwd", xres)
    dya, dyr, dproj = _out_proj_bwd(dx1b, wo, proj, ya, yr)
    colblk = lambda w: (lambda tk: pl.BlockSpec((tk, w), lambda p, k: (k, p)))
    g_o = _wgrad("wgrad_out", merged, dx1b, colblk(256), tok2(D_MODEL), (D_MODEL, D_MODEL),
                 pl.BlockSpec((256, D_MODEL), lambda p, k: (p, 0)), 4, S)
    datt, rho, dyrin = _branch_bwd(dya, dyr, wa, wr, att)
    g_a = _wgrad("wgrad_attn", att, dya, tok2(512), colblk(512), (512, D_MODEL),
                 pl.BlockSpec((512, 512), lambda p, k: (0, p)), 2, S)
    g_r = _wgrad("wgrad_ret", yrin, dyr, colblk(256), tok2(D_MODEL), (D_MODEL, D_MODEL),
                 pl.BlockSpec((256, D_MODEL), lambda p, k: (p, 0)), 4, S)
    comm.grads({1: g_a, 2: g_r.reshape(N_SHARD, 256, D_MODEL), 3: g_o.reshape(N_SHARD, 256, D_MODEL)})
    dproj, xres = _ret_bwd(proj, rn, rstd, dyrin, states, tab, consts, dproj, comm.carry("ret_bwd"))
    comm.took("ret_bwd", xres)
    dqs, dks, dvs = [], [], []
    for gi, d in enumerate(DILATIONS):
        rtab = jnp.asarray(tab_np[0].reshape(3, S // d, d * 128))
        (dq, dk, dv), xres = _attn_bwd(qkvs[gi], _sub_view(datt, d), _sub_view(lse_tot, d), _sub_view(rho, d), rtab, d, gi,
                                       comm.carry(f"attn_bwd_g{gi}"))
        comm.took(f"attn_bwd_g{gi}", xres)
        dqs.append(dq)
        dks.append(dk)
        dvs.append(dv)
    dproj = _assemble_dproj((dqs, dks, dvs), dproj)
    g_sent, xres = _wgrad_in_half(ht, dproj, comm.sidx, False, comm.carry("wgrad_in_sent"))
    comm.took("wgrad_in_sent", xres)
    comm.grads({"in_sent": g_sent})
    g_kept, xres = _wgrad_in_half(ht, dproj, comm.sidx, True, comm.carry("wgrad_in_kept"))
    comm.grads({"in_kept": g_kept})
    comm.took("wgrad_in_kept", xres)
    grad_x, dg1, xres = _in_proj_bwd(dproj, w_in, x, g1, dx1, comm.carry("in_proj_bwd"))
    comm.took("in_proj_bwd", xres)
    return loss_p, grad_x, (dg1, dg2, dg3)


W_KINDS = ("col", "col", "lead", "lead", "lead", "lead", "lead")
W_SHARD = ((1024, W_IN_S), (512, 256), (256, 1024), (256, 1024), (1024, HID_S), (1024, HID_S), (HID_S, 1024))
N_W = len(W_KINDS)


def _full_shape(wi):
    R, C = W_SHARD[wi]
    return (R, N_SHARD * C) if W_KINDS[wi] == "col" else (N_SHARD, R, C)


def _view(ref, wi, s, half):
    R, C = W_SHARD[wi]
    rows = pl.ds(half * (R // 2), R // 2)
    if W_KINDS[wi] == "col":
        return ref.at[rows, pl.ds(pl.multiple_of(s * C, 128), C)]
    return ref.at[s, rows, :]


def _mesh_pos():
    x, y, c = lax.axis_index("x"), lax.axis_index("y"), lax.axis_index("c")
    chips = [(1 - x, y), (x, 1 - y), (1 - x, 1 - y)]
    return x, y, c, chips


def _cast_bf16(a):
    R, C = a.shape
    tr = R // 2 if R % 32 == 0 else R

    def body(a_ref, o_ref):
        o_ref[...] = a_ref[...].astype(BF16)

    spec = pl.BlockSpec((tr, C), lambda i: (i, 0))
    return pl.pallas_call(body, out_shape=SDS((R, C), BF16), grid=(R // tr,), in_specs=[spec], out_specs=spec,
                          compiler_params=_cparams("parallel"), name=f"cast_{R}x{C}")(a)


def _remote(send, recv, k, src, dst, to):
    return pltpu.make_async_remote_copy(src_ref=src, dst_ref=dst, send_sem=send.at[k], recv_sem=recv.at[k],
                                        device_id=to, device_id_type=MESH)


def _gather_now(wis, shards):
    n = len(wis)

    def body(*refs):
        sh, full = refs[:n], refs[n:2 * n]
        send, recv, loc = refs[2 * n:]
        x, y, c, _ = _mesh_pos()
        s_me, sib = 2 * x + y, (x, y, 1 - c)
        xn, yn = (1 - x, y), (x, 1 - y)
        flip = lambda a, b: a + b - 2 * a * b
        via = (flip(x, 1 - c), flip(y, c))
        onto = (flip(x, c), flip(y, 1 - c))
        shard_of = lambda chip: 2 * chip[0] + chip[1]
        own, started = [], []
        for i, wi in enumerate(wis):
            Rh = W_SHARD[wi][0] // 2
            for hf in range(2):
                cp = pltpu.make_async_copy(sh[i].at[pl.ds(hf * Rh, Rh), :], _view(full[i], wi, s_me, hf), loc.at[2 * i + hf])
                cp.start()
                own.append(cp)
            for j, chip in enumerate((xn, yn)):
                cp = _remote(send, recv, 6 * i + j, sh[i].at[pl.ds(c * Rh, Rh), :], _view(full[i], wi, s_me, c), (*chip, c))
                cp.start()
                started.append(cp)

        def pass_to_sibling(i, wi, k, s):
            mine = _view(full[i], wi, s, c)
            fw = _remote(send, recv, 6 * i + k, mine, mine, sib)
            fw.start()
            started.append(fw)

        for i, wi in enumerate(wis):
            for j, chip in enumerate((xn, yn)):
                land = _view(full[i], wi, shard_of(chip), c)
                _remote(send, recv, 6 * i + j, land, land, (*chip, c)).wait_recv()
                pass_to_sibling(i, wi, 3 + j, shard_of(chip))
            relay = _view(full[i], wi, shard_of(via), c)
            fw = _remote(send, recv, 6 * i + 2, relay, relay, (*onto, c))
            fw.start()
            started.append(fw)
        s_diag = 2 * (1 - x) + (1 - y)
        for i, wi in enumerate(wis):
            land = _view(full[i], wi, s_diag, c)
            _remote(send, recv, 6 * i + 2, land, land, (*onto, c)).wait_recv()
            pass_to_sibling(i, wi, 5, s_diag)
        for i, wi in enumerate(wis):
            for k, s in ((3, shard_of(xn)), (4, shard_of(yn)), (5, s_diag)):
                land = _view(full[i], wi, s, 1 - c)
                _remote(send, recv, 6 * i + k, land, land, sib).wait_recv()
        for cp in started:
            cp.wait_send()
        for cp in own:
            cp.wait()

    return pl.pallas_call(
        body, out_shape=tuple(SDS(_full_shape(wi), BF16) for wi in wis),
        in_specs=[ANY] * n, out_specs=tuple([ANY] * n),
        scratch_shapes=[pltpu.SemaphoreType.DMA((6 * n,)), pltpu.SemaphoreType.DMA((6 * n,)),
                        pltpu.SemaphoreType.DMA((2 * n,))],
        name="gather_now")(*shards)


def _ex_gather_ici(wis, shards, then_d2d=False):
    n = len(wis)

    def build(ins, outs, send, recv, loc):
        x, y, c, chips = _mesh_pos()
        s_me, sib = 2 * x + y, (x, y, 1 - c)
        starts, waits, after = [], [], []
        for i, wi in enumerate(wis):
            Rh = W_SHARD[wi][0] // 2
            for hf in range(2):
                cp = pltpu.make_async_copy(ins[i].at[pl.ds(hf * Rh, Rh), :], _view(outs[i], wi, s_me, hf), loc.at[2 * i + hf])
                starts.append(cp)
                waits.append(cp.wait)
            for j, chip in enumerate(chips):
                cp = _remote(send, recv, 3 * i + j, ins[i].at[pl.ds(c * Rh, Rh), :], _view(outs[i], wi, s_me, c), (*chip, c))
                land = _view(outs[i], wi, 2 * chip[0] + chip[1], c)
                starts.append(cp)
                waits += [cp.wait_send, _remote(send, recv, 3 * i + j, land, land, (*chip, c)).wait_recv]
                if then_d2d:
                    theirs = _view(outs[i], wi, 2 * chip[0] + chip[1], 1 - c)
                    fw = _remote(send, recv, 3 * n + 3 * i + j, land, land, sib)
                    waits.append(fw.start)
                    after += [fw.wait_send, _remote(send, recv, 3 * n + 3 * i + j, theirs, theirs, sib).wait_recv]
        return starts, waits + after

    return _Exchange(shards, [SDS(_full_shape(wi), BF16) for wi in wis], {}, (6 if then_d2d else 3) * n, 2 * n, build)


def _ex_gather_d2d(wis, fulls):
    def build(ins, outs, send, recv, loc):
        x, y, c, chips = _mesh_pos()
        sib = (x, y, 1 - c)
        starts, waits = [], []
        for i, wi in enumerate(wis):
            for j, chip in enumerate(chips):
                mine = _view(outs[i], wi, 2 * chip[0] + chip[1], c)
                theirs = _view(outs[i], wi, 2 * chip[0] + chip[1], 1 - c)
                cp = _remote(send, recv, 3 * i + j, mine, mine, sib)
                starts.append(cp)
                waits += [cp.wait_send, _remote(send, recv, 3 * i + j, theirs, theirs, sib).wait_recv]
        return starts, waits

    return _Exchange(fulls, [SDS(f.shape, BF16) for f in fulls], {i: i for i in range(len(wis))}, 3 * len(wis), 0, build)


def _half_shape(wi):
    R, C = W_SHARD[wi]
    return (R // 2, N_SHARD * C) if W_KINDS[wi] == "col" else (N_SHARD, R // 2, C)


def _ex_pair(wis, grads):
    def build(ins, outs, send, recv, loc):
        x, y, c, _ = _mesh_pos()
        starts, waits = [], []
        for i, wi in enumerate(wis):
            Rh = W_SHARD[wi][0] // 2
            rows = pl.ds((1 - c) * Rh, Rh)
            if tuple(ins[i].shape) == _half_shape(wi):
                src = ins[i]
            else:
                src = ins[i].at[rows, :] if W_KINDS[wi] == "col" else ins[i].at[:, rows, :]
            cp = _remote(send, recv, i, src, outs[i], (x, y, 1 - c))
            starts.append(cp)
            waits.append(cp.wait)
        return starts, waits

    return _Exchange(grads, [SDS(_half_shape(wi), F32) for wi in wis], {}, len(wis), 0, build)


def _ex_chip(wis, pbs):
    def build(ins, outs, send, recv, loc):
        x, y, c, chips = _mesh_pos()
        starts, waits = [], []
        for i, wi in enumerate(wis):
            for j, chip in enumerate(chips):
                cp = _remote(send, recv, 3 * i + j, ins[i].at[j], outs[i].at[j], (*chip, c))
                starts.append(cp)
                waits.append(cp.wait)
        return starts, waits

    shapes = [SDS((3, W_SHARD[wi][0] // 2, W_SHARD[wi][1]), BF16) for wi in wis]
    return _Exchange(pbs, shapes, {}, 3 * len(wis), 0, build)


def _ex_share(wis, halves):
    def build(ins, outs, send, recv, loc):
        x, y, c, _ = _mesh_pos()
        sib = (x, y, 1 - c)
        starts, waits = [], []
        for i, wi in enumerate(wis):
            cp = _remote(send, recv, i, outs[i].at[c], outs[i].at[c], sib)
            starts.append(cp)
            waits += [cp.wait_send, _remote(send, recv, i, outs[i].at[1 - c], outs[i].at[1 - c], sib).wait_recv]
        return starts, waits

    return _Exchange(halves, [SDS(h.shape, F32) for h in halves], {i: i for i in range(len(wis))}, len(wis), 0, build)


def _row_tile(rh, C):
    best = 16
    for t in range(16, rh + 1, 16):
        if rh % t == 0 and t * C * 4 <= (3 << 19):
            best = t
    return best


def _pair_sum(wi, g, ra, sidx):
    R, C = W_SHARD[wi]
    Rh = R // 2
    tr = _row_tile(Rh, C)
    nt = Rh // tr
    off = 0 if tuple(g.shape) == _half_shape(wi) else nt
    col = W_KINDS[wi] == "col"

    def body(sidx_ref, *refs):
        gs, rs = refs[:4], refs[4:8]
        own_ref, pb_ref = refs[8:]
        own_ref[...] = gs[0][...] + rs[0][...]
        for j in range(3):
            pb_ref[j] = (gs[1 + j][...] + rs[1 + j][...]).astype(BF16)

    def gspec(slot):
        if col:
            return pl.BlockSpec((tr, C), lambda i, sx: (sx[4] * off + i, sx[slot]))
        return pl.BlockSpec((None, tr, C), lambda i, sx: (sx[slot], sx[4] * off + i, 0))

    def rspec(slot):
        if col:
            return pl.BlockSpec((tr, C), lambda i, sx: (i, sx[slot]))
        return pl.BlockSpec((None, tr, C), lambda i, sx: (sx[slot], i, 0))

    return pl.pallas_call(
        body, out_shape=(SDS((Rh, C), F32), SDS((3, Rh, C), BF16)),
        grid_spec=pltpu.PrefetchScalarGridSpec(
            num_scalar_prefetch=1, grid=(nt,),
            in_specs=[gspec(k) for k in range(4)] + [rspec(k) for k in range(4)],
            out_specs=(pl.BlockSpec((tr, C), lambda i, sx: (i, 0)), pl.BlockSpec((3, tr, C), lambda i, sx: (0, i, 0)))),
        compiler_params=_cparams("arbitrary"), name=f"pair_sum_w{wi}")(sidx, g, g, g, g, ra, ra, ra, ra)


def _chip_sum(wi, own, rb, sidx):
    R, C = W_SHARD[wi]
    Rh = R // 2
    tr = _row_tile(Rh, C)

    def body(sidx_ref, own_ref, rb_ref, o_ref):
        o_ref[...] = ((own_ref[...] + rb_ref[0].astype(F32)) + rb_ref[1].astype(F32)) + rb_ref[2].astype(F32)

    return pl.pallas_call(
        body, out_shape=SDS((2, Rh, C), F32),
        grid_spec=pltpu.PrefetchScalarGridSpec(
            num_scalar_prefetch=1, grid=(Rh // tr,),
            in_specs=[pl.BlockSpec((tr, C), lambda i, sx: (i, 0)), pl.BlockSpec((3, tr, C), lambda i, sx: (0, i, 0))],
            out_specs=pl.BlockSpec((None, tr, C), lambda i, sx: (sx[4], i, 0))),
        compiler_params=_cparams("arbitrary"), name=f"chip_sum_w{wi}")(sidx, own, rb)


def _gain_allgather(blk):
    m_per, n = blk.shape

    def body(x_ref, out_ref, send_sems, recv_sems, local_sem):
        x, y, c, chips = _mesh_pos()
        me, sibling = (x, y, c), (x, y, 1 - c)

        def rows(px, py, pc):
            return out_ref.at[pl.ds((4 * px + 2 * py + pc) * m_per, m_per), :]

        def copy(k, block, to, src=None):
            return pltpu.make_async_remote_copy(
                src_ref=rows(*block) if src is None else src, dst_ref=rows(*block),
                send_sem=send_sems.at[k], recv_sem=recv_sems.at[k], device_id=to, device_id_type=MESH)

        mine = pltpu.make_async_copy(x_ref, rows(*me), local_sem)
        mine.start()
        first = [copy(0, me, sibling, src=x_ref)]
        first += [copy(1 + j, me, (*chip, c), src=x_ref) for j, chip in enumerate(chips)]
        for cp in first:
            cp.start()
        passed = [copy(4 + j, (*chip, c), sibling) for j, chip in enumerate(chips)]
        for j, chip in enumerate(chips):
            copy(1 + j, (*chip, c), me).wait_recv()
            passed[j].start()
        copy(0, sibling, me).wait_recv()
        for j, chip in enumerate(chips):
            copy(4 + j, (*chip, 1 - c), me).wait_recv()
        for cp in first + passed:
            cp.wait_send()
        mine.wait()

    vm = pl.BlockSpec(memory_space=pltpu.VMEM)
    return pl.pallas_call(
        body, out_shape=SDS((8 * m_per, n), blk.dtype), in_specs=[vm], out_specs=vm,
        scratch_shapes=[pltpu.SemaphoreType.DMA((7,)), pltpu.SemaphoreType.DMA((7,)), pltpu.SemaphoreType.DMA],
        name="gain_allgather")(blk)


def _adam_math(w, g, m, v):
    mn = ADAM_B1 * m + (1.0 - ADAM_B1) * g
    vn = ADAM_B2 * v + (1.0 - ADAM_B2) * (g * g)
    mh = mn / (1.0 - ADAM_B1 ** ADAM_STEP)
    vh = vn / (1.0 - ADAM_B2 ** ADAM_STEP)
    return -ADAM_LR * (mh / (jnp.sqrt(vh) + ADAM_EPS) + ADAM_WD * w), mn, vn


def _adamw(wi, w, g, m, v):
    R, C = w.shape
    tr = _row_tile(R, C)

    def body(w_ref, g_ref, m_ref, v_ref, go_ref, d_ref, mn_ref, vn_ref):
        g = g_ref[...]
        go_ref[...] = g
        d_ref[...], mn_ref[...], vn_ref[...] = _adam_math(w_ref[...], g, m_ref[...], v_ref[...])

    spec = pl.BlockSpec((tr, C), lambda i: (i, 0))
    return pl.pallas_call(body, out_shape=(SDS((R, C), F32),) * 4, grid=(R // tr,), in_specs=[spec] * 4,
                          out_specs=(spec,) * 4, compiler_params=_cparams("parallel"), name=f"adamw_w{wi}")(w, g, m, v)


def _gain_update(gathered, w, m, v):
    def body(ga_ref, w_ref, m_ref, v_ref, g_ref, d_ref, mn_ref, vn_ref):
        g = ga_ref[0:8, :]
        for dev in range(1, 8):
            g = g + ga_ref[8 * dev:8 * dev + 8, :]
        g_ref[...] = g
        d_ref[...], mn_ref[...], vn_ref[...] = _adam_math(w_ref[...], g, m_ref[...], v_ref[...])

    return pl.pallas_call(body, out_shape=(SDS((8, 1024), F32),) * 4, name="gain_update")(gathered, w, m, v)


GROUP_FFN, GROUP_MIX, GROUP_IN = (4, 5, 6), (1, 2, 3), (0,)
REST = GROUP_MIX + GROUP_FFN


class _MeshComm:
    SCHEDULE = {
        "in_proj": [("ici", (1, 2, 3, 4))],
        "attn_fwd_g0": [("d2d", (1, 2, 3, 4))],
        "ret_fwd": [("ici", (5,))],
        "out_proj": [("d2d", (5,))],
        "ffn_up": [("both", (6,))],
        "ffn_up_bwd": [("pair", GROUP_FFN)],
        "ret_bwd": [("pair", GROUP_MIX), ("chip", (4,))],
        "attn_bwd_g0": [("chip", (5,))],
        "attn_bwd_g1": [("chip", (6,))],
        "attn_bwd_g2": [("chip", GROUP_MIX)],
        "wgrad_in_sent": [("share", GROUP_FFN + GROUP_MIX)],
        "wgrad_in_kept": [("pair", GROUP_IN)],
        "in_proj_bwd": [("chip", GROUP_IN)],
    }

    def __init__(self, shards):
        xi, yi, ci = lax.axis_index("x"), lax.axis_index("y"), lax.axis_index("c")
        self.sidx = jnp.stack([2 * xi + yi, 2 * (1 - xi) + yi, 2 * xi + (1 - yi), 2 * (1 - xi) + (1 - yi), ci]).astype(jnp.int32)
        self.shards, self.full = shards, {}
        self.g, self.own, self.pb, self.half, self.red = {}, {}, {}, {}, {}

    def w_in(self):
        return _gather_now(GROUP_IN, [self.shards[0]])[0]

    def weight(self, wi):
        return self.full[wi].reshape(D_MODEL, D_MODEL) if wi in (2, 3) else self.full[wi]

    def grads(self, by_wi):
        self.g.update(by_wi)

    def _exchange(self, stage, wis):
        pick = lambda table: [table[wi] for wi in wis]
        if stage == "ici":
            return _ex_gather_ici(wis, pick(self.shards))
        if stage == "both":
            return _ex_gather_ici(wis, pick(self.shards), then_d2d=True)
        if stage == "d2d":
            return _ex_gather_d2d(wis, pick(self.full))
        if stage == "pair":
            return _ex_pair(wis, [self.g["in_sent"] if wi == 0 else self.g[wi] for wi in wis])
        if stage == "chip":
            return _ex_chip(wis, pick(self.pb))
        return _ex_share(wis, pick(self.half))

    def _landed(self, stage, wis, res):
        for wi, r in zip(wis, res):
            if stage in ("ici", "d2d", "both"):
                self.full[wi] = r
            elif stage == "pair":
                self.own[wi], self.pb[wi] = _pair_sum(wi, self.g["in_kept"] if wi == 0 else self.g[wi], r, self.sidx)
            elif stage == "chip":
                self.half[wi] = _chip_sum(wi, self.own[wi], r, self.sidx)
            else:
                self.red[wi] = r

    def carry(self, point):
        return [self._exchange(stage, wis) for stage, wis in self.SCHEDULE.get(point, ())]

    def took(self, point, xres):
        for (stage, wis), res in zip(self.SCHEDULE.get(point, ()), xres):
            self._landed(stage, wis, res)

    def reduced(self):
        self._landed("share", GROUP_IN, _exchange_call(self._exchange("share", GROUP_IN), "share_w_in"))
        return [self.red[wi] for wi in range(N_W)]


def kernel(x, norm_mix_g, w_in, w_out_attn, w_out_ret, w_out, norm_ffn_g, w_ffn_gate, w_ffn_up, w_ffn_down, norm_final_g, loss_target, m_norm_mix_g, m_w_in, m_w_out_attn, m_w_out_ret, m_w_out, m_norm_ffn_g, m_w_ffn_gate, m_w_ffn_up, m_w_ffn_down, m_norm_final_g, v_norm_mix_g, v_w_in, v_w_out_attn, v_w_out_ret, v_w_out, v_norm_ffn_g, v_w_ffn_gate, v_w_ffn_up, v_w_ffn_down, v_norm_final_g):
    ws = (w_in, w_out_attn, w_out_ret, w_out, w_ffn_gate, w_ffn_up, w_ffn_down)
    ms = (m_w_in, m_w_out_attn, m_w_out_ret, m_w_out, m_w_ffn_gate, m_w_ffn_up, m_w_ffn_down)
    vs = (v_w_in, v_w_out_attn, v_w_out_ret, v_w_out, v_w_ffn_gate, v_w_ffn_up, v_w_ffn_down)
    shard2d = lambda a, wi: a.reshape(W_SHARD[wi])

    comm = _MeshComm([_cast_bf16(shard2d(w, wi)) for wi, w in enumerate(ws)])
    g3 = norm_final_g.reshape(1, D_MODEL)
    loss_p, grad_x, gain_g = _step(x[0], loss_target[0], norm_mix_g, norm_ffn_g, g3, comm)
    gred = comm.reduced()

    outs_g, outs_d, outs_m, outs_v = [], [], [], []
    for wi in range(N_W):
        g2d = gred[wi].reshape(W_SHARD[wi])
        gout, dlt, mn, vn = _adamw(wi, shard2d(ws[wi], wi), g2d, shard2d(ms[wi], wi), shard2d(vs[wi], wi))
        for lst, a in ((outs_g, gout), (outs_d, dlt), (outs_m, mn), (outs_v, vn)):
            lst.append(a.reshape(ws[wi].shape))

    pad8 = lambda rows: jnp.concatenate([r.reshape(1, D_MODEL) for r in rows]
                                        + [jnp.zeros((8 - len(rows), D_MODEL), F32)], axis=0)
    gathered = _gain_allgather(pad8((*gain_g, jnp.tile(loss_p[0:1], (1, D_MODEL // 128)))))
    gg, gd, gm, gv = _gain_update(gathered, pad8((norm_mix_g, norm_ffn_g, norm_final_g)),
                                  pad8((m_norm_mix_g, m_norm_ffn_g, m_norm_final_g)),
                                  pad8((v_norm_mix_g, v_norm_ffn_g, v_norm_final_g)))
    loss = gg[3, 0]

    def assemble(gain_rows, wlist):
        return (gain_rows[0:1], wlist[0], wlist[1], wlist[2], wlist[3], gain_rows[1:2],
                wlist[4], wlist[5], wlist[6], gain_rows[2])

    return (loss, grad_x[None], *assemble(gg, outs_g), *assemble(gd, outs_d), *assemble(gm, outs_m), *assemble(gv, outs_v))
```

```python
import functools
import math

import numpy as np
import jax
import jax.numpy as jnp
from jax import lax
from jax.experimental import pallas as pl
from jax.experimental.pallas import tpu as pltpu

F32, BF16 = jnp.float32, jnp.bfloat16
SDS = jax.ShapeDtypeStruct
MESH = pl.DeviceIdType.MESH

D_MODEL = 1024
PROJ_W = 9728
COLB = 512
N_COLB = PROJ_W // COLB
QA_B, KA_B, VA_B = 0, 3, 6
QR_B, KR_B = 9, 10
FFN_HID = 2816
N_SHARD = 4
HID_S = FFN_HID // N_SHARD
W_IN_S = PROJ_W // N_SHARD
DILATIONS = (1, 4, 16)
BLK = 128
RET_HEADS = 4
ROPE_THETA = 10000.0
NORM_EPS = 1e-6
ADAM_LR, ADAM_B1, ADAM_B2, ADAM_EPS, ADAM_WD, ADAM_STEP = 0.001, 0.9, 0.999, 1e-08, 0.01, 10
VMEM_LIMIT = 56 << 20


def _cparams(*sem):
    return pltpu.CompilerParams(dimension_semantics=sem or None, vmem_limit_bytes=VMEM_LIMIT)


def _dot(a, b):
    return jnp.dot(a, b, preferred_element_type=F32)


def _dot_nt(a, b):
    return lax.dot_general(a, b, (((1,), (1,)), ((), ())), preferred_element_type=F32)


def _dot_tn(a, b):
    return lax.dot_general(a, b, (((0,), (0,)), ((), ())), preferred_element_type=F32)


def _row_pieces(tm, sub=512):
    return [slice(i, i + sub) for i in range(0, tm, sub)]


def _sigmoid(z):
    return 0.5 * jnp.tanh(0.5 * z) + 0.5


ANY = pl.BlockSpec(memory_space=pl.ANY)


class _Exchange:
    def __init__(self, ins, out_shapes, aliases, n_sem, n_loc, build):
        self.ins, self.out_shapes, self.aliases = list(ins), list(out_shapes), dict(aliases)
        self.n_sem, self.n_loc, self.build = n_sem, n_loc, build

    def sems(self):
        return [pltpu.SemaphoreType.DMA((self.n_sem,)), pltpu.SemaphoreType.DMA((self.n_sem,)),
                pltpu.SemaphoreType.DMA((max(self.n_loc, 1),))]


def _exchange_call(ex, name):
    n_in, n_out = len(ex.ins), len(ex.out_shapes)

    def body(*refs):
        starts, waits = ex.build(refs[:n_in], refs[n_in:n_in + n_out], *refs[n_in + n_out:])
        for cp in starts:
            cp.start()
        for w in waits:
            w()

    return pl.pallas_call(body, out_shape=tuple(ex.out_shapes), in_specs=[ANY] * n_in, out_specs=tuple([ANY] * n_out),
                          input_output_aliases=ex.aliases, scratch_shapes=ex.sems(), name=name)(*ex.ins)


def _carrier_call(body, args, *, out_shape, grid, in_specs, out_specs, scratch_shapes=(), sem, name, exchanges=(),
                  prefetch=None, in_out_aliases=None):
    out_shape, out_specs = tuple(out_shape), tuple(out_specs)
    n_in, n_out, n_scr = len(args), len(out_shape), len(scratch_shapes)
    n_pre = 0 if prefetch is None else 1
    x_args, x_outs, x_scr, spans = [], [], [], []
    aliases = {n_pre + a: o for a, o in (in_out_aliases or {}).items()}
    for ex in exchanges:
        i0, o0 = len(x_args), len(x_outs)
        for a, o in ex.aliases.items():
            aliases[n_pre + n_in + i0 + a] = n_out + o0 + o
        x_args += ex.ins
        x_outs += ex.out_shapes
        x_scr += ex.sems()
        spans.append((i0, len(ex.ins), o0, len(ex.out_shapes)))
    nx_in, nx_out = len(x_args), len(x_outs)

    def wrapped(*refs):
        refs = refs[n_pre:]
        ins, xin = refs[:n_in], refs[n_in:n_in + nx_in]
        o_base = n_in + nx_in
        outs, xout = refs[o_base:o_base + n_out], refs[o_base + n_out:o_base + n_out + nx_out]
        s_base = o_base + n_out + nx_out
        scr, xs = refs[s_base:s_base + n_scr], refs[s_base + n_scr:]

        def built(e):
            i0, ni, o0, no = spans[e]
            return exchanges[e].build(xin[i0:i0 + ni], xout[o0:o0 + no], *xs[3 * e:3 * e + 3])

        if exchanges:
            first = functools.reduce(jnp.logical_and, [pl.program_id(k) == 0 for k in range(len(grid))])
            last = functools.reduce(jnp.logical_and, [pl.program_id(k) == grid[k] - 1 for k in range(len(grid))])

            @pl.when(first)
            def _():
                for e in range(len(exchanges)):
                    for cp in built(e)[0]:
                        cp.start()

        body(*ins, *outs, *scr)

        if exchanges:
            @pl.when(last)
            def _():
                for e in range(len(exchanges)):
                    for w in built(e)[1]:
                        w()

    all_in, all_out = list(in_specs) + [ANY] * nx_in, out_specs + tuple([ANY] * nx_out)
    all_scr = list(scratch_shapes) + x_scr
    cparams = _cparams(*(sem if not exchanges else ("arbitrary",) * len(grid)))
    if prefetch is None:
        res = pl.pallas_call(wrapped, out_shape=out_shape + tuple(x_outs), grid=grid, in_specs=all_in, out_specs=all_out,
                             scratch_shapes=all_scr, input_output_aliases=aliases, compiler_params=cparams,
                             name=name)(*args, *x_args)
    else:
        gs = pltpu.PrefetchScalarGridSpec(num_scalar_prefetch=1, grid=grid, in_specs=all_in, out_specs=all_out,
                                          scratch_shapes=all_scr)
        res = pl.pallas_call(wrapped, out_shape=out_shape + tuple(x_outs), grid_spec=gs, input_output_aliases=aliases,
                             compiler_params=cparams, name=name)(prefetch, *args, *x_args)
    xres = [tuple(res[n_out + o0:n_out + o0 + no]) for (_, _, o0, no) in spans]
    return tuple(res[:n_out]), xres


def _tables(S):
    f32 = np.float32
    pos = np.arange(S, dtype=f32)
    lane = np.arange(128)
    inv = (f32(ROPE_THETA) ** (-np.arange(0, 64, 2, dtype=f32) / f32(64))).astype(f32)
    ang = (pos[:, None] * inv[None, :]).astype(np.float64)
    idx = (lane % 64) % 32
    c, s = np.cos(ang)[:, idx], np.sin(ang)[:, idx]
    first = ((lane % 64) < 32)[None, :]
    rope = np.stack([c, np.where(first, 0.0, s), np.where(first, -s, 0.0)])
    base = (f32(1.0) / (f32(ROPE_THETA) ** np.linspace(0.0, 1.0, 64, dtype=f32))).astype(f32)
    ang2 = (pos[:, None] * base[None, :]).astype(np.float64)
    c2, s2 = np.cos(ang2)[:, lane // 2], np.sin(ang2)[:, lane // 2]
    even = (lane % 2 == 0)[None, :]
    th = np.stack([c2, np.where(even, 0.0, s2), np.where(even, -s2, 0.0)])
    return np.stack([rope, th, th * (128 ** -0.5)]).astype(f32)


def _rot(a, c, sa, sb, shift):
    return a * c + pltpu.roll(a, shift, 1) * sa + pltpu.roll(a, 128 - shift, 1) * sb


def _unrot(g, c, sa, sb, shift):
    return g * c + pltpu.roll(g * sa, 128 - shift, 1) + pltpu.roll(g * sb, shift, 1)


def _ret_consts():
    h = np.arange(RET_HEADS, dtype=np.float64)
    log_g = np.log1p(-(2.0 ** (-5.0 - h)))
    idx = np.arange(BLK, dtype=np.float64)
    diff = idx[:, None] - idx[None, :]
    dmask = np.where(diff[None] >= 0, np.exp(np.maximum(diff, 0.0)[None] * log_g[:, None, None]), 0.0)
    zeta = np.exp((BLK - 1 - idx)[None, :] * log_g[:, None])
    xi = np.exp((idx + 1.0)[None, :] * log_g[:, None])
    dec = np.exp(BLK * log_g)
    rep = lambda v: np.broadcast_to(v[:, :, None], (RET_HEADS, BLK, 128))
    return (jnp.asarray(dmask, F32), jnp.asarray(rep(zeta), F32), jnp.asarray(rep(xi), F32),
            jnp.asarray(np.broadcast_to(dec[:, None, None], (RET_HEADS, 8, 256)), F32))


def _rms_fwd(x, g):
    S = x.shape[0]
    tm = 512

    def body(x_ref, g_ref, h_ref, ht_ref):
        xv = x_ref[...]
        r = lax.rsqrt(jnp.mean(xv * xv, axis=-1, keepdims=True) + NORM_EPS)
        h = xv * r * g_ref[...]
        h_ref[...] = h.astype(BF16)
        ht_ref[...] = h.T.astype(BF16)

    return pl.pallas_call(
        body, out_shape=(SDS((S, D_MODEL), BF16), SDS((D_MODEL, S), BF16)), grid=(S // tm,),
        in_specs=[pl.BlockSpec((tm, D_MODEL), lambda i: (i, 0)), pl.BlockSpec((1, D_MODEL), lambda i: (0, 0))],
        out_specs=(pl.BlockSpec((tm, D_MODEL), lambda i: (i, 0)), pl.BlockSpec((D_MODEL, tm), lambda i: (0, i))),
        compiler_params=_cparams("parallel"), name="rms_fwd")(x, g)


def _in_proj(h, w_in, tab, exchanges=()):
    S = h.shape[0]
    tm = min(S, 4096)

    def body(h_ref, w_ref, t_ref, o_ref):
        j = pl.program_id(1)
        is_rope = j < 6
        is_theta = (j == QR_B) | (j == KR_B)
        sub = 512

        def rotated(shift):
            for i in range(tm // sub):
                rows = slice(i * sub, (i + 1) * sub)
                acc = _dot(h_ref[rows, :], w_ref[...])
                c, sa, sb = t_ref[0, 0, rows, :], t_ref[0, 1, rows, :], t_ref[0, 2, rows, :]
                for k in range(COLB // 128):
                    sl = slice(k * 128, (k + 1) * 128)
                    o_ref[rows, sl] = _rot(acc[:, sl], c, sa, sb, shift).astype(BF16)

        @pl.when(is_rope)
        def _():
            rotated(32)

        @pl.when(is_theta)
        def _():
            rotated(1)

        @pl.when(jnp.logical_not(is_rope | is_theta))
        def _():
            o_ref[...] = _dot(h_ref[...], w_ref[...]).astype(BF16)

    def tab_map(i, j):
        return (jnp.where(j == QR_B, 1, jnp.where(j == KR_B, 2, 0)), 0, i, 0)

    (proj,), xres = _carrier_call(
        body, (h, w_in, tab), out_shape=(SDS((S, PROJ_W), BF16),), grid=(S // tm, N_COLB),
        in_specs=[pl.BlockSpec((tm, D_MODEL), lambda i, j: (i, 0)),
                  pl.BlockSpec((D_MODEL, COLB), lambda i, j: (0, j)),
                  pl.BlockSpec((1, 3, tm, 128), tab_map)],
        out_specs=(pl.BlockSpec((tm, COLB), lambda i, j: (i, j)),),
        sem=("parallel", "arbitrary"), name="in_proj", exchanges=exchanges)
    return proj, xres


def _band_mask(n):
    qi = lax.broadcasted_iota(jnp.int32, (BLK, 2 * BLK), 0)
    kj = lax.broadcasted_iota(jnp.int32, (BLK, 2 * BLK), 1)
    dist = BLK + qi - kj
    return (dist >= 0) & (dist <= BLK) & ((kj >= BLK) | (n > 0))


def _qkv_col(d, gi):
    if d == 1:
        return lambda t, r: 3 * t + gi
    return lambda t, r: 3 * r + t


def _attn_fwd(qkv, d, gi, exchanges=()):
    L = qkv.shape[0]
    nb = L // BLK

    def body(q_ref, kc_ref, kp_ref, vc_ref, vp_ref, o_ref, lse_ref):
        n = pl.program_id(1)
        mask = _band_mask(n)
        mask2 = jnp.concatenate([mask, mask], axis=0)
        lane = lax.broadcasted_iota(jnp.int32, (BLK, 128), 1)
        lo = lane < 64
        lse_all = jnp.zeros((BLK, 128), F32)
        chunks = [slice(c * 128, (c + 1) * 128) for c in range(4)]
        scores, vals = [], []
        for sl in chunks:
            q = q_ref[:, sl]
            k = jnp.concatenate([kp_ref[:, sl], kc_ref[:, sl]], axis=0)
            vals.append(jnp.concatenate([vp_ref[:, sl], vc_ref[:, sl]], axis=0))
            q2 = jnp.concatenate([jnp.where(lo, q, jnp.zeros_like(q)), jnp.where(lo, jnp.zeros_like(q), q)], axis=0)
            scores.append(_dot_nt(q2, k))
        probs = []
        for c, s in enumerate(scores):
            s = jnp.where(mask2, s * 0.125, jnp.float32(-1e30))
            m = jnp.max(s, axis=-1, keepdims=True)
            p = jnp.exp(s - m)
            l = jnp.sum(p, axis=-1, keepdims=True)
            probs.append((p / l).astype(BF16))
            lse = m + jnp.log(l)
            lse_all = jnp.where(lane // 16 == 2 * c, lse[:BLK], jnp.where(lane // 16 == 2 * c + 1, lse[BLK:], lse_all))
        for sl, p, v in zip(chunks, probs, vals):
            o2 = _dot(p, v)
            o_ref[:, sl] = jnp.where(lo, o2[:BLK], o2[BLK:])
        lse_ref[...] = lse_all

    prev = lambda n: jnp.maximum(n - 1, 0)
    col = _qkv_col(d, gi)
    return _carrier_call(
        body, (qkv,) * 5, out_shape=(SDS((L, d * 512), F32), SDS((L, d * 128), F32)), grid=(d, nb),
        in_specs=[pl.BlockSpec((BLK, 512), lambda r, n: (n, col(0, r))),
                  pl.BlockSpec((BLK, 512), lambda r, n: (n, col(1, r))),
                  pl.BlockSpec((BLK, 512), lambda r, n: (prev(n), col(1, r))),
                  pl.BlockSpec((BLK, 512), lambda r, n: (n, col(2, r))),
                  pl.BlockSpec((BLK, 512), lambda r, n: (prev(n), col(2, r)))],
        out_specs=(pl.BlockSpec((BLK, 512), lambda r, n: (n, r)),
                   pl.BlockSpec((BLK, 128), lambda r, n: (n, r))),
        sem=("parallel", "arbitrary"), name=f"attn_fwd_g{gi}", exchanges=exchanges)


def _qkv_to_sub(proj, d, gi):
    S = proj.shape[0]
    tm = 512
    n = tm // d

    def body(q_ref, k_ref, v_ref, o_ref, scr):
        for t, ref in enumerate((q_ref, k_ref, v_ref)):
            for c in range(4):
                scr[c] = ref[:, c * 128:(c + 1) * 128].astype(F32)
            for r in range(d):
                for c in range(4):
                    col = (3 * r + t) * 512 + c * 128
                    o_ref[:, col:col + 128] = scr[c, pl.ds(r, n, stride=d), :].astype(BF16)

    return pl.pallas_call(
        body, out_shape=SDS((S // d, d * 1536), BF16), grid=(S // tm,),
        in_specs=[pl.BlockSpec((tm, 512), lambda i, b=b: (i, b + gi)) for b in (QA_B, KA_B, VA_B)],
        out_specs=pl.BlockSpec((n, d * 1536), lambda i: (i, 0)),
        scratch_shapes=[pltpu.VMEM((4, tm, 128), F32)],
        compiler_params=_cparams("parallel"), name=f"qkv_to_sub_g{gi}")(proj, proj, proj)


def _attn_merge(os_, lses):
    S = os_[0].shape[0]
    tm = 512

    def body(o0, o1, o2, l0, l1, l2, att_ref, lt_ref, so1, so2, sl1, sl2):
        lo = lax.broadcasted_iota(jnp.int32, (tm, 128), 1) < 64

        def natural(ref, d, scr, width):
            nch = width // 128
            if d == 1:
                return [ref[:, c * 128:(c + 1) * 128] for c in range(nch)]
            for r in range(d):
                for c in range(nch):
                    scr[c, pl.ds(r, tm // d, stride=d), :] = ref[:, r * width + c * 128:r * width + (c + 1) * 128]
            return [scr[c] for c in range(nch)]

        ls = [natural(l, d, s, 128)[0] for l, d, s in zip((l0, l1, l2), DILATIONS, (None, sl1, sl2))]
        m = jnp.maximum(jnp.maximum(ls[0], ls[1]), ls[2])
        es = [jnp.exp(v - m) for v in ls]
        z = es[0] + es[1] + es[2]
        lt_ref[...] = m + jnp.log(z)
        ws = [e / z for e in es]
        o_nat = [natural(o, d, s, 512) for o, d, s in zip((o0, o1, o2), DILATIONS, (None, so1, so2))]
        for c in range(4):
            acc = jnp.zeros((tm, 128), F32)
            for g in range(3):
                w_lo = jnp.broadcast_to(ws[g][:, 32 * c:32 * c + 1], (tm, 128))
                w_hi = jnp.broadcast_to(ws[g][:, 32 * c + 16:32 * c + 17], (tm, 128))
                acc = acc + jnp.where(lo, w_lo, w_hi) * o_nat[g][c]
            att_ref[:, c * 128:(c + 1) * 128] = acc.astype(BF16)

    sub = lambda w: [pl.BlockSpec((tm // d, d * w), lambda i: (i, 0)) for d in DILATIONS]
    return pl.pallas_call(
        body, out_shape=(SDS((S, 512), BF16), SDS((S, 128), F32)), grid=(S // tm,),
        in_specs=sub(512) + sub(128),
        out_specs=(pl.BlockSpec((tm, 512), lambda i: (i, 0)), pl.BlockSpec((tm, 128), lambda i: (i, 0))),
        scratch_shapes=[pltpu.VMEM((4, tm, 128), F32), pltpu.VMEM((4, tm, 128), F32),
                        pltpu.VMEM((1, tm, 128), F32), pltpu.VMEM((1, tm, 128), F32)],
        compiler_params=_cparams("parallel"), name="attn_merge")(*os_, *lses)


def _assemble_dproj(att_grads, dproj):
    S = dproj.shape[0]
    tm = 256

    def body(*refs):
        a = [refs[3 * t:3 * t + 3] for t in range(3)]
        dp_prev, o_ref, scr = refs[9:]
        for t in range(3):
            for g, d in enumerate(DILATIONS):
                base = (3 * t + g) * COLB
                if d == 1:
                    o_ref[:, base:base + COLB] = a[t][g][...]
                    continue
                for c in range(4):
                    for r in range(d):
                        scr[c, pl.ds(r, tm // d, stride=d), :] = a[t][g][:, r * 512 + c * 128:r * 512 + (c + 1) * 128].astype(F32)
                    o_ref[:, base + c * 128:base + (c + 1) * 128] = scr[c].astype(BF16)

    sub = [pl.BlockSpec((tm // d, d * 512), lambda i: (i, 0)) for d in DILATIONS]
    flat = [att_grads[t][g] for t in range(3) for g in range(3)]
    return pl.pallas_call(
        body, out_shape=SDS((S, PROJ_W), BF16), grid=(S // tm,),
        in_specs=sub * 3 + [ANY], out_specs=pl.BlockSpec((tm, 9 * COLB), lambda i: (i, 0)),
        scratch_shapes=[pltpu.VMEM((4, tm, 128), F32)], input_output_aliases={9: 0},
        compiler_params=_cparams("parallel"), name="assemble_dproj")(*flat, dproj)


def _ret_fwd(proj, consts, exchanges=()):
    S = proj.shape[0]
    nc = S // BLK
    dmask, zeta, xi, dec = consts

    def body(q_ref, k_ref, v0_ref, v1_ref, g0_ref, g1_ref, dm_ref, z_ref, x_ref, dec_ref,
             y_ref, rn_ref, rs_ref, st_ref, R):
        @pl.when(pl.program_id(0) == 0)
        def _():
            R[...] = jnp.zeros_like(R)

        lane16 = lax.broadcasted_iota(jnp.int32, (BLK, 128), 1) // 16
        rs_all = jnp.zeros((BLK, 128), F32)
        first = []
        for h in range(RET_HEADS):
            hs = slice(h * 128, (h + 1) * 128)
            q, k = q_ref[:, hs], k_ref[:, hs]
            v = (v0_ref if h < 2 else v1_ref)[:, (h % 2) * 256:(h % 2 + 1) * 256]
            Rb = R[h].astype(BF16)
            st_ref[h] = Rb
            kz = (k.astype(F32) * z_ref[h]).astype(BF16)
            first.append((v, _dot_nt(q, k), _dot((q.astype(F32) * x_ref[h]).astype(BF16), Rb), _dot_tn(kz, v)))
        masked = [(s * dm_ref[h]).astype(BF16) for h, (_, s, _, _) in enumerate(first)]
        for h in range(RET_HEADS):
            vs = slice((h % 2) * 256, (h % 2 + 1) * 256)
            os_ = slice(h * 256, (h + 1) * 256)
            v, _, cross, kv = first[h]
            o = _dot(masked[h], v) + cross
            R[h] = R[h] * dec_ref[h, 0:1, :] + kv
            mu = jnp.mean(o, axis=-1, keepdims=True)
            oc = o - mu
            rstd = lax.rsqrt(jnp.mean(oc * oc, axis=-1, keepdims=True) + NORM_EPS)
            rn = oc * rstd
            gr = (g0_ref if h < 2 else g1_ref)[:, vs].astype(F32)
            y_ref[:, os_] = (rn * gr * _sigmoid(gr)).astype(BF16)
            rn_ref[:, os_] = rn.astype(BF16)
            rs_all = jnp.where(lane16 == h, rstd, rs_all)
        rs_ref[...] = rs_all

    cst = lambda shape: pl.BlockSpec(shape, lambda c: (0, 0, 0))
    blk = lambda j: pl.BlockSpec((BLK, 512), lambda c: (c, j))
    return _carrier_call(
        body, (proj, proj, proj, proj, proj, proj, dmask, zeta, xi, dec),
        out_shape=(SDS((S, 1024), BF16), SDS((S, 1024), BF16), SDS((S, 128), F32), SDS((RET_HEADS, nc, BLK, 256), BF16)),
        grid=(nc,),
        in_specs=[blk(QR_B), blk(KR_B), blk(11), blk(12), blk(13), blk(14),
                  cst((RET_HEADS, BLK, BLK)), cst((RET_HEADS, BLK, 128)), cst((RET_HEADS, BLK, 128)), cst((RET_HEADS, 8, 256))],
        out_specs=(pl.BlockSpec((BLK, 1024), lambda c: (c, 0)), pl.BlockSpec((BLK, 1024), lambda c: (c, 0)),
                   pl.BlockSpec((BLK, 128), lambda c: (c, 0)),
                   pl.BlockSpec((RET_HEADS, None, BLK, 256), lambda c: (0, c, 0, 0))),
        scratch_shapes=[pltpu.VMEM((RET_HEADS, BLK, 256), F32)],
        sem=("arbitrary",), name="ret_fwd", exchanges=exchanges)


def _branch_merge(att, yrin, proj, wa, wr):
    S = att.shape[0]
    tm = min(S, 2048)

    def body(a_ref, y_ref, ga_ref, gr_ref, wa_ref, wr_ref, m_ref, ya_ref, yr_ref):
        for rows in _row_pieces(tm):
            ya = _dot(a_ref[rows, :], wa_ref[...])
            yr = _dot(y_ref[rows, :], wr_ref[...])
            m_ref[rows, :] = (_sigmoid(ga_ref[rows, :].astype(F32)) * ya
                              + _sigmoid(gr_ref[rows, :].astype(F32)) * yr).astype(BF16)
            ya_ref[rows, :] = ya.astype(BF16)
            yr_ref[rows, :] = yr.astype(BF16)

    ospec = pl.BlockSpec((tm, 512), lambda i, j: (i, j))
    return pl.pallas_call(
        body, out_shape=(SDS((S, D_MODEL), BF16),) * 3, grid=(S // tm, 2),
        in_specs=[pl.BlockSpec((tm, 512), lambda i, j: (i, 0)), pl.BlockSpec((tm, 1024), lambda i, j: (i, 0)),
                  pl.BlockSpec((tm, 512), lambda i, j: (i, 15 + j)), pl.BlockSpec((tm, 512), lambda i, j: (i, 17 + j)),
                  pl.BlockSpec((512, 512), lambda i, j: (0, j)), pl.BlockSpec((1024, 512), lambda i, j: (0, j))],
        out_specs=(ospec, ospec, ospec),
        compiler_params=_cparams("parallel", "arbitrary"), name="branch_merge")(att, yrin, proj, proj, wa, wr)


def _out_proj(merged, wo, x, g2, exchanges=()):
    S = x.shape[0]
    tm = 1024

    def body(m_ref, w_ref, x_ref, g_ref, x1_ref, h2_ref):
        x1 = x_ref[...] + _dot(m_ref[...], w_ref[...])
        x1_ref[...] = x1
        r = lax.rsqrt(jnp.mean(x1 * x1, axis=-1, keepdims=True) + NORM_EPS)
        h2_ref[...] = (x1 * r * g_ref[...]).astype(BF16)

    row = pl.BlockSpec((tm, D_MODEL), lambda i: (i, 0))
    return _carrier_call(
        body, (merged, wo, x, g2), out_shape=(SDS((S, D_MODEL), F32), SDS((S, D_MODEL), BF16)), grid=(S // tm,),
        in_specs=[row, pl.BlockSpec((D_MODEL, D_MODEL), lambda i: (0, 0)), row, pl.BlockSpec((1, D_MODEL), lambda i: (0, 0))],
        out_specs=(row, row), sem=("parallel",), name="out_proj", exchanges=exchanges)


def _ffn_up(h2, wg, wu, exchanges=()):
    S = h2.shape[0]
    tm = min(S, 2048)

    def body(h_ref, wg_ref, wu_ref, g_ref, u_ref, a_ref):
        for rows in _row_pieces(tm):
            hv = h_ref[rows, :]
            g = _dot(hv, wg_ref[...])
            u = _dot(hv, wu_ref[...])
            g_ref[rows, :] = g.astype(BF16)
            u_ref[rows, :] = u.astype(BF16)
            a_ref[rows, :] = (g * _sigmoid(g) * u).astype(BF16)

    wspec = pl.BlockSpec((None, D_MODEL, HID_S), lambda i, s: (s, 0, 0))
    ospec = pl.BlockSpec((None, tm, HID_S), lambda i, s: (s, i, 0))
    return _carrier_call(
        body, (h2, wg, wu), out_shape=(SDS((N_SHARD, S, HID_S), BF16),) * 3, grid=(S // tm, N_SHARD),
        in_specs=[pl.BlockSpec((tm, D_MODEL), lambda i, s: (i, 0)), wspec, wspec],
        out_specs=(ospec, ospec, ospec),
        sem=("parallel", "arbitrary"), name="ffn_up", exchanges=exchanges)


def _ffn_down_loss(act, wd, x1, g3, tgt):
    S = x1.shape[0]
    tm = 512

    def body(a_ref, w_ref, x_ref, g_ref, t_ref, dx_ref, dxb_ref, dg_ref, ls_ref):
        @pl.when(pl.program_id(0) == 0)
        def _():
            dg_ref[...] = jnp.zeros_like(dg_ref)
            ls_ref[...] = jnp.zeros_like(ls_ref)

        g = g_ref[...]
        for rows in _row_pieces(tm, 256):
            y = _dot(a_ref[0, rows, :], w_ref[0])
            for s in range(1, N_SHARD):
                y = y + _dot(a_ref[s, rows, :], w_ref[s])
            x2 = x_ref[rows, :] + y
            r = lax.rsqrt(jnp.mean(x2 * x2, axis=-1, keepdims=True) + NORM_EPS)
            xh = x2 * r
            err = xh * g - t_ref[rows, :]
            ls_ref[...] += jnp.sum(jnp.sum(err * err, axis=-1, keepdims=True), axis=0, keepdims=True) * (0.5 / D_MODEL)
            dy = err * (1.0 / D_MODEL)
            dg_ref[...] += jnp.sum(dy * xh, axis=0, keepdims=True)
            dxh = dy * g
            dx = r * (dxh - xh * jnp.mean(dxh * xh, axis=-1, keepdims=True))
            dx_ref[rows, :] = dx
            dxb_ref[rows, :] = dx.astype(BF16)

    row = pl.BlockSpec((tm, D_MODEL), lambda i: (i, 0))
    vec = pl.BlockSpec((1, D_MODEL), lambda i: (0, 0))
    return pl.pallas_call(
        body, out_shape=(SDS((S, D_MODEL), F32), SDS((S, D_MODEL), BF16), SDS((1, D_MODEL), F32), SDS((8, 128), F32)),
        grid=(S // tm,),
        in_specs=[pl.BlockSpec((N_SHARD, tm, HID_S), lambda i: (0, i, 0)),
                  pl.BlockSpec((N_SHARD, HID_S, D_MODEL), lambda i: (0, 0, 0), pipeline_mode=pl.Buffered(1)),
                  row, vec, row],
        out_specs=(row, row, vec, pl.BlockSpec((8, 128), lambda i: (0, 0))),
        compiler_params=_cparams("arbitrary"), name="ffn_down_loss")(act, wd, x1, g3, tgt)


def _ffn_down_bwd(dx2b, wd, gte, up):
    S = dx2b.shape[0]
    tm = min(S, 2048)

    def body(d_ref, w_ref, g_ref, u_ref, dg_ref, du_ref):
        for rows in _row_pieces(tm, 256):
            da = _dot_nt(d_ref[rows, :], w_ref[...])
            g = g_ref[rows, :].astype(F32)
            sg = _sigmoid(g)
            dg_ref[rows, :] = (da * u_ref[rows, :].astype(F32) * sg * (1.0 + g * (1.0 - sg))).astype(BF16)
            du_ref[rows, :] = (da * g * sg).astype(BF16)

    aspec = pl.BlockSpec((None, tm, HID_S), lambda i, s: (s, i, 0))
    return pl.pallas_call(
        body, out_shape=(SDS((N_SHARD, S, HID_S), BF16),) * 2, grid=(S // tm, N_SHARD),
        in_specs=[pl.BlockSpec((tm, D_MODEL), lambda i, s: (i, 0)),
                  pl.BlockSpec((None, HID_S, D_MODEL), lambda i, s: (s, 0, 0)), aspec, aspec],
        out_specs=(aspec, aspec),
        compiler_params=_cparams("parallel", "arbitrary"), name="ffn_down_bwd")(dx2b, wd, gte, up)


def _wgrad(name, a, b, a_spec, b_spec, out_shape, out_spec, n_par, S):
    tk = 2048

    def body(a_ref, b_ref, o_ref):
        @pl.when(pl.program_id(1) == 0)
        def _():
            o_ref[...] = jnp.zeros_like(o_ref)

        o_ref[...] += _dot_tn(a_ref[...], b_ref[...])

    return pl.pallas_call(
        body, out_shape=SDS(out_shape, F32), grid=(n_par, S // tk),
        in_specs=[a_spec(tk), b_spec(tk)], out_specs=out_spec,
        compiler_params=_cparams("parallel", "arbitrary"), name=name)(a, b)


def _ffn_up_bwd(dgte, dup, wg, wu, x1, g2, dx2, exchanges=()):
    S = x1.shape[0]
    tm = 512

    def body(dg_ref, du_ref, wg_ref, wu_ref, x_ref, g_ref, dx2_ref, dx_ref, dxb_ref, dgn_ref):
        @pl.when(pl.program_id(0) == 0)
        def _():
            dgn_ref[...] = jnp.zeros_like(dgn_ref)

        for rows in _row_pieces(tm, 256):
            dh = _dot_nt(dg_ref[0, rows, :], wg_ref[0]) + _dot_nt(du_ref[0, rows, :], wu_ref[0])
            for s in range(1, N_SHARD):
                dh = dh + _dot_nt(dg_ref[s, rows, :], wg_ref[s]) + _dot_nt(du_ref[s, rows, :], wu_ref[s])
            xv = x_ref[rows, :]
            r = lax.rsqrt(jnp.mean(xv * xv, axis=-1, keepdims=True) + NORM_EPS)
            xh = xv * r
            dgn_ref[...] += jnp.sum(dh * xh, axis=0, keepdims=True)
            dxh = dh * g_ref[...]
            dx = dx2_ref[rows, :] + r * (dxh - xh * jnp.mean(dxh * xh, axis=-1, keepdims=True))
            dx_ref[rows, :] = dx
            dxb_ref[rows, :] = dx.astype(BF16)

    row = pl.BlockSpec((tm, D_MODEL), lambda i: (i, 0))
    vec = pl.BlockSpec((1, D_MODEL), lambda i: (0, 0))
    aspec = pl.BlockSpec((N_SHARD, tm, HID_S), lambda i: (0, i, 0))
    wspec = pl.BlockSpec((N_SHARD, D_MODEL, HID_S), lambda i: (0, 0, 0), pipeline_mode=pl.Buffered(1))
    return _carrier_call(
        body, (dgte, dup, wg, wu, x1, g2, dx2),
        out_shape=(SDS((S, D_MODEL), F32), SDS((S, D_MODEL), BF16), SDS((1, D_MODEL), F32)),
        grid=(S // tm,),
        in_specs=[aspec, aspec, wspec, wspec, row, vec, row], out_specs=(row, row, vec),
        sem=("arbitrary",), name="ffn_up_bwd", exchanges=exchanges)


def _out_proj_bwd(dx1b, wo, proj, ya, yr):
    S = dx1b.shape[0]
    tm = 512
    gate0 = 15 * COLB

    def body(d_ref, w_ref, ga_ref, gr_ref, ya_ref, yr_ref, dya_ref, dyr_ref, dp_ref):
        for rows in _row_pieces(tm, 256):
            dm = _dot_nt(d_ref[rows, :], w_ref[...])
            sa = _sigmoid(ga_ref[rows, :].astype(F32))
            sr = _sigmoid(gr_ref[rows, :].astype(F32))
            dya_ref[rows, :] = (dm * sa).astype(BF16)
            dyr_ref[rows, :] = (dm * sr).astype(BF16)
            dp_ref[rows, 0:D_MODEL] = (dm * ya_ref[rows, :].astype(F32) * sa * (1.0 - sa)).astype(BF16)
            dp_ref[rows, D_MODEL:2 * D_MODEL] = (dm * yr_ref[rows, :].astype(F32) * sr * (1.0 - sr)).astype(BF16)

    row = pl.BlockSpec((tm, D_MODEL), lambda i: (i, 0))
    cols = lambda c0, w: pl.BlockSpec((pl.Element(tm), pl.Element(w)), lambda i: (i * tm, c0))
    return pl.pallas_call(
        body, out_shape=(SDS((S, D_MODEL), BF16), SDS((S, D_MODEL), BF16), SDS((S, PROJ_W), BF16)), grid=(S // tm,),
        in_specs=[row, pl.BlockSpec((D_MODEL, D_MODEL), lambda i: (0, 0), pipeline_mode=pl.Buffered(1)),
                  cols(gate0, D_MODEL), cols(gate0 + D_MODEL, D_MODEL), row, row],
        out_specs=(row, row, cols(gate0, 2 * D_MODEL)),
        compiler_params=_cparams("parallel"), name="out_proj_bwd")(dx1b, wo, proj, proj, ya, yr)


def _branch_bwd(dya, dyr, wa, wr, att):
    S = dya.shape[0]
    tm = 1024

    def body(da_ref, dr_ref, wa_ref, wr_ref, att_ref, datt_ref, rho_ref, dyi_ref):
        datt = _dot_nt(da_ref[...], wa_ref[...])
        datt_ref[...] = datt.astype(BF16)
        dyi_ref[...] = _dot_nt(dr_ref[...], wr_ref[...]).astype(BF16)
        prod = datt * att_ref[...].astype(F32)
        lane = lax.broadcasted_iota(jnp.int32, (tm, 128), 1)
        lo = lane < 64
        rho = jnp.zeros((tm, 128), F32)
        for c in range(4):
            pc = prod[:, c * 128:(c + 1) * 128]
            tot = jnp.sum(pc, axis=-1, keepdims=True)
            low = jnp.sum(jnp.where(lo, pc, 0.0), axis=-1, keepdims=True)
            rho = jnp.where(lane // 16 == 2 * c, low, jnp.where(lane // 16 == 2 * c + 1, tot - low, rho))
        rho_ref[...] = rho

    row = lambda w: pl.BlockSpec((tm, w), lambda i: (i, 0))
    return pl.pallas_call(
        body, out_shape=(SDS((S, 512), BF16), SDS((S, 128), F32), SDS((S, 1024), BF16)), grid=(S // tm,),
        in_specs=[row(1024), row(1024), pl.BlockSpec((512, 1024), lambda i: (0, 0)),
                  pl.BlockSpec((1024, 1024), lambda i: (0, 0)), row(512)],
        out_specs=(row(512), row(128), row(1024)),
        compiler_params=_cparams("parallel"), name="branch_bwd")(dya, dyr, wa, wr, att)


def _attn_bwd(qkv, datt, lse, rho, rtab, d, gi, exchanges=()):
    L = qkv.shape[0]
    nb = L // BLK
    T = d * nb

    def body(q_ref, kc_ref, kp_ref, vc_ref, vp_ref, do_ref, lse_ref, rho_ref, tq_ref, tk_ref,
             dq_ref, dk_ref, dv_ref, ck, cv):
        t = pl.program_id(0)
        n = jnp.minimum(t, T - 1) % nb

        @pl.when(t == 0)
        def _():
            ck[...] = jnp.zeros_like(ck)
            cv[...] = jnp.zeros_like(cv)

        def store_rot(ref, val, t_ref, c):
            sl = slice(c * 128, (c + 1) * 128)
            ref[:, sl] = _unrot(val, t_ref[0], t_ref[1], t_ref[2], 32).astype(BF16)

        @pl.when(t < T)
        def _():
            mask = _band_mask(n)
            mask2 = jnp.concatenate([mask, mask], axis=0)
            lo = lax.broadcasted_iota(jnp.int32, (BLK, 128), 1) < 64

            def stacked(a):
                return jnp.concatenate([jnp.where(lo, a, jnp.zeros_like(a)), jnp.where(lo, jnp.zeros_like(a), a)], axis=0)

            def head_cols(ref, c):
                return jnp.concatenate([jnp.broadcast_to(ref[:, 32 * c:32 * c + 1], (BLK, 2 * BLK)),
                                        jnp.broadcast_to(ref[:, 32 * c + 16:32 * c + 17], (BLK, 2 * BLK))], axis=0)

            ops, raw = [], []
            for c in range(4):
                sl = slice(c * 128, (c + 1) * 128)
                q2, do2 = stacked(q_ref[:, sl]), stacked(do_ref[:, sl])
                k = jnp.concatenate([kp_ref[:, sl], kc_ref[:, sl]], axis=0)
                v = jnp.concatenate([vp_ref[:, sl], vc_ref[:, sl]], axis=0)
                ops.append((q2, do2, k))
                raw.append((_dot_nt(q2, k), _dot_nt(do2, v)))
            grads = []
            for c, (s, dp) in enumerate(raw):
                p = jnp.where(mask2, jnp.exp(s * 0.125 - head_cols(lse_ref, c)), 0.0)
                grads.append(((p * (dp - head_cols(rho_ref, c)) * 0.125).astype(BF16), p.astype(BF16)))
            for c, ((q2, do2, k), (ds, pb)) in enumerate(zip(ops, grads)):
                sl = slice(c * 128, (c + 1) * 128)
                dq2 = _dot(ds, k)
                dq_c = jnp.where(lo, dq2[:BLK], dq2[BLK:])
                dk_c = _dot_tn(ds, q2)
                dv_c = _dot_tn(pb, do2)
                store_rot(dq_ref, dq_c, tq_ref, c)
                store_rot(dk_ref, ck[:, sl] + dk_c[:BLK], tk_ref, c)
                dv_ref[:, sl] = (cv[:, sl] + dv_c[:BLK]).astype(BF16)
                ck[:, sl] = dk_c[BLK:]
                cv[:, sl] = dv_c[BLK:]

        @pl.when(t == T)
        def _():
            for c in range(4):
                sl = slice(c * 128, (c + 1) * 128)
                store_rot(dk_ref, ck[:, sl], tk_ref, c)
            dv_ref[...] = cv[...].astype(BF16)

    blk_of = lambda t: (jnp.minimum(t, T - 1) % nb, jnp.minimum(t, T - 1) // nb)
    cur = lambda t: blk_of(t)
    prev = lambda t: (jnp.maximum(blk_of(t)[0] - 1, 0), blk_of(t)[1])
    fin = lambda t: blk_of(jnp.maximum(t - 1, 0))
    col = _qkv_col(d, gi)
    qkv_spec = lambda kind, which: pl.BlockSpec((BLK, 512), lambda t: (which(t)[0], col(kind, which(t)[1])))
    row_spec = lambda w, which: pl.BlockSpec((BLK, w), lambda t: which(t))
    tab_spec = lambda which: pl.BlockSpec((3, BLK, 128), lambda t: (0, *which(t)))
    return _carrier_call(
        body, (qkv, qkv, qkv, qkv, qkv, datt, lse, rho, rtab, rtab),
        out_shape=(SDS((L, d * 512), BF16),) * 3, grid=(T + 1,),
        in_specs=[qkv_spec(0, cur), qkv_spec(1, cur), qkv_spec(1, prev), qkv_spec(2, cur), qkv_spec(2, prev),
                  row_spec(512, cur), row_spec(128, cur), row_spec(128, cur), tab_spec(cur), tab_spec(fin)],
        out_specs=(row_spec(512, cur), row_spec(512, fin), row_spec(512, fin)),
        scratch_shapes=[pltpu.VMEM((BLK, 512), F32), pltpu.VMEM((BLK, 512), F32)],
        sem=("arbitrary",), name=f"attn_bwd_g{gi}", exchanges=exchanges)


def _ret_bwd(proj, rn, rstd, dyrin, states, tab, consts, dproj, exchanges=()):
    S = proj.shape[0]
    nc = S // BLK
    dmask, zeta, xi, dec = consts

    def body(q_ref, k_ref, v0_ref, v1_ref, g0_ref, g1_ref, rn_ref, rs_ref, dy_ref, st_ref, tq_ref, tk_ref,
             dm_ref, z_ref, x_ref, dec_ref, dp_prev, dp_ref, dR):
        dq_ref, dk_ref = dp_ref.at[:, 0:512], dp_ref.at[:, 512:1024]
        dv_ref, dgr_ref = dp_ref.at[:, 1024:2048], dp_ref.at[:, 2048:3072]

        @pl.when(pl.program_id(0) == 0)
        def _():
            dR[...] = jnp.zeros_like(dR)

        dobs = []
        for h in range(RET_HEADS):
            vs = slice((h % 2) * 256, (h % 2 + 1) * 256)
            os_ = slice(h * 256, (h + 1) * 256)
            gr = (g0_ref if h < 2 else g1_ref)[:, vs].astype(F32)
            sg = _sigmoid(gr)
            rn_v = rn_ref[:, os_].astype(F32)
            dyi = dy_ref[:, os_].astype(F32)
            dgr_ref[:, os_] = (dyi * rn_v * sg * (1.0 + gr * (1.0 - sg))).astype(BF16)
            drn = dyi * gr * sg
            rstd = jnp.broadcast_to(rs_ref[:, 16 * h:16 * h + 1], (BLK, 256))
            do = rstd * (drn - jnp.mean(drn, axis=-1, keepdims=True) - rn_v * jnp.mean(drn * rn_v, axis=-1, keepdims=True))
            dobs.append(do.astype(BF16))
        first = []
        for h in range(RET_HEADS):
            hs = slice(h * 128, (h + 1) * 128)
            q, k = q_ref[:, hs], k_ref[:, hs]
            v = (v0_ref if h < 2 else v1_ref)[:, (h % 2) * 256:(h % 2 + 1) * 256]
            dob, dRb = dobs[h], dR[h].astype(BF16)
            kz = (k.astype(F32) * z_ref[h]).astype(BF16)
            qx = (q.astype(F32) * x_ref[h]).astype(BF16)
            first.append((q, k, _dot_nt(q, k), _dot_nt(dob, v), _dot(kz, dRb), _dot_nt(dob, st_ref[h]),
                          _dot_nt(v, dRb), _dot_tn(qx, dob)))
        masked = [((s * dm_ref[h]).astype(BF16), (dsr * dm_ref[h]).astype(BF16))
                  for h, (_, _, s, dsr, _, _, _, _) in enumerate(first)]
        for h in range(RET_HEADS):
            hs = slice(h * 128, (h + 1) * 128)
            os_ = slice(h * 256, (h + 1) * 256)
            q, k, _, _, dv_state, dq_state, dk_state, dr_new = first[h]
            sD, dS = masked[h]
            dv_ref[:, os_] = (_dot_tn(sD, dobs[h]) + dv_state).astype(BF16)
            dq = _dot(dS, k) + dq_state * x_ref[h]
            dk = _dot_tn(dS, q) + dk_state * z_ref[h]
            dR[h] = dR[h] * dec_ref[h, 0:1, :] + dr_new
            dq_ref[:, hs] = _unrot(dq, tq_ref[0], tq_ref[1], tq_ref[2], 1).astype(BF16)
            dk_ref[:, hs] = _unrot(dk, tk_ref[0], tk_ref[1], tk_ref[2], 1).astype(BF16)

    rc = lambda c: nc - 1 - c
    cst = lambda shape: pl.BlockSpec(shape, lambda c: (0, 0, 0))
    blk = lambda j: pl.BlockSpec((BLK, 512), lambda c: (rc(c), j))
    row = lambda w: pl.BlockSpec((BLK, w), lambda c: (rc(c), 0))
    (dproj,), xres = _carrier_call(
        body, (proj, proj, proj, proj, proj, proj, rn, rstd, dyrin, states, tab, tab, dmask, zeta, xi, dec, dproj),
        out_shape=(SDS((S, PROJ_W), BF16),), grid=(nc,),
        in_specs=[blk(QR_B), blk(KR_B), blk(11), blk(12), blk(13), blk(14), row(1024), row(128), row(1024),
                  pl.BlockSpec((RET_HEADS, None, BLK, 256), lambda c: (0, rc(c), 0, 0)),
                  pl.BlockSpec((None, 3, BLK, 128), lambda c: (1, 0, rc(c), 0)),
                  pl.BlockSpec((None, 3, BLK, 128), lambda c: (2, 0, rc(c), 0)),
                  cst((RET_HEADS, BLK, BLK)), cst((RET_HEADS, BLK, 128)), cst((RET_HEADS, BLK, 128)), cst((RET_HEADS, 8, 256)),
                  ANY],
        out_specs=(pl.BlockSpec((pl.Element(BLK), pl.Element(6 * COLB)), lambda c: (rc(c) * BLK, QR_B * COLB)),),
        scratch_shapes=[pltpu.VMEM((RET_HEADS, BLK, 256), F32)],
        sem=("arbitrary",), name="ret_bwd", exchanges=exchanges, in_out_aliases={16: 0})
    return dproj, xres


def _wgrad_in_half(ht, dproj, sidx, kept, exchanges=()):
    S = dproj.shape[0]
    tk = 2048
    half = (lambda sx: sx[4]) if kept else (lambda sx: 1 - sx[4])

    def body(a_ref, b_ref, o_ref):
        @pl.when(pl.program_id(1) == 0)
        def _():
            o_ref[...] = jnp.zeros_like(o_ref)

        o_ref[...] += _dot(a_ref[...], b_ref[...])

    (g,), xres = _carrier_call(
        body, (ht, dproj), out_shape=(SDS((D_MODEL // 2, PROJ_W), F32),), grid=(N_SHARD, S // tk),
        in_specs=[pl.BlockSpec((D_MODEL // 2, tk), lambda s, k, sx: (half(sx), k)),
                  pl.BlockSpec((tk, W_IN_S), lambda s, k, sx: (k, s))],
        out_specs=(pl.BlockSpec((D_MODEL // 2, W_IN_S), lambda s, k, sx: (0, s)),),
        sem=("parallel", "arbitrary"), name="wgrad_in_kept" if kept else "wgrad_in_sent", exchanges=exchanges,
        prefetch=sidx)
    return g, xres


def _in_proj_bwd(dproj, w_in, x, g1, dx1, exchanges=()):
    S = x.shape[0]
    tm = 1024

    def body(d_ref, w_ref, x_ref, g_ref, dx1_ref, dx_ref, dgn_ref, acc):
        i, s = pl.program_id(0), pl.program_id(1)

        @pl.when(s == 0)
        def _():
            acc[...] = jnp.zeros_like(acc)

        @pl.when((i == 0) & (s == 0))
        def _():
            dgn_ref[...] = jnp.zeros_like(dgn_ref)

        acc[...] += _dot_nt(d_ref[...], w_ref[...])

        @pl.when(s == N_SHARD - 1)
        def _():
            xv = x_ref[...]
            r = lax.rsqrt(jnp.mean(xv * xv, axis=-1, keepdims=True) + NORM_EPS)
            xh = xv * r
            dh = acc[...]
            dgn_ref[...] += jnp.sum(dh * xh, axis=0, keepdims=True)
            dxh = dh * g_ref[...]
            dx_ref[...] = dx1_ref[...] + r * (dxh - xh * jnp.mean(dxh * xh, axis=-1, keepdims=True))

    row = pl.BlockSpec((tm, D_MODEL), lambda i, s: (i, 0))
    vec = pl.BlockSpec((1, D_MODEL), lambda i, s: (0, 0))
    (gx, dg), xres = _carrier_call(
        body, (dproj, w_in, x, g1, dx1),
        out_shape=(SDS((S, D_MODEL), F32), SDS((1, D_MODEL), F32)), grid=(S // tm, N_SHARD),
        in_specs=[pl.BlockSpec((tm, W_IN_S), lambda i, s: (i, s)),
                  pl.BlockSpec((D_MODEL, W_IN_S), lambda i, s: (0, s)), row, vec, row],
        out_specs=(row, vec), scratch_shapes=[pltpu.VMEM((tm, D_MODEL), F32)],
        sem=("arbitrary", "arbitrary"), name="in_proj_bwd", exchanges=exchanges)
    return gx, dg, xres


def _sub_view(a, d):
    S, W = a.shape
    return a.reshape(S // d, d * W)


def _step(x, tgt, g1, g2, g3, comm):
    S = x.shape[0]
    tab_np = _tables(S)
    tab = jnp.asarray(tab_np)
    consts = _ret_consts()

    h, ht = _rms_fwd(x, g1)
    w_in = comm.w_in()
    proj, xres = _in_proj(h, w_in, tab, comm.carry("in_proj"))
    comm.took("in_proj", xres)
    qkvs, o_parts, lse_parts = [], [], []
    for gi, d in enumerate(DILATIONS):
        qkv = proj if d == 1 else _qkv_to_sub(proj, d, gi)
        (o_g, lse_g), xres = _attn_fwd(qkv, d, gi, comm.carry(f"attn_fwd_g{gi}"))
        comm.took(f"attn_fwd_g{gi}", xres)
        qkvs.append(qkv)
        o_parts.append(o_g)
        lse_parts.append(lse_g)
    att, lse_tot = _attn_merge(o_parts, lse_parts)
    (yrin, rn, rstd, states), xres = _ret_fwd(proj, consts, comm.carry("ret_fwd"))
    comm.took("ret_fwd", xres)
    wa, wr, wo = comm.weight(1), comm.weight(2), comm.weight(3)
    merged, ya, yr = _branch_merge(att, yrin, proj, wa, wr)
    (x1, h2), xres = _out_proj(merged, wo, x, g2, comm.carry("out_proj"))
    comm.took("out_proj", xres)
    wg, wu = comm.weight(4), comm.weight(5)
    (gte, up, act), xres = _ffn_up(h2, wg, wu, comm.carry("ffn_up"))
    comm.took("ffn_up", xres)
    wd = comm.weight(6)
    dx2, dx2b, dg3, loss_p = _ffn_down_loss(act, wd, x1, g3, tgt)

    dgte, dup = _ffn_down_bwd(dx2b, wd, gte, up)
    tok3 = lambda w: (lambda tk: pl.BlockSpec((None, tk, w), lambda p, k: (p, k, 0)))
    tok2 = lambda w: (lambda tk: pl.BlockSpec((tk, w), lambda p, k: (k, 0)))
    g_d = _wgrad("wgrad_down", act, dx2b, tok3(HID_S), tok2(D_MODEL), (N_SHARD, HID_S, D_MODEL),
                 pl.BlockSpec((None, HID_S, D_MODEL), lambda p, k: (p, 0, 0)), N_SHARD, S)
    g_g = _wgrad("wgrad_gate", h2, dgte, tok2(D_MODEL), tok3(HID_S), (N_SHARD, D_MODEL, HID_S),
                 pl.BlockSpec((None, D_MODEL, HID_S), lambda p, k: (p, 0, 0)), N_SHARD, S)
    g_u = _wgrad("wgrad_up", h2, dup, tok2(D_MODEL), tok3(HID_S), (N_SHARD, D_MODEL, HID_S),
                 pl.BlockSpec((None, D_MODEL, HID_S), lambda p, k: (p, 0, 0)), N_SHARD, S)
    comm.grads({4: g_g, 5: g_u, 6: g_d})
    (dx1, dx1b, dg2), xres = _ffn_up_bwd(dgte, dup, wg, wu, x1, g2, dx2, comm.carry("ffn_up_bwd"))
    comm.took("ffn_up_bwd", xres)
    dya, dyr, dproj = _out_proj_bwd(dx1b, wo, proj, ya, yr)
    colblk = lambda w: (lambda tk: pl.BlockSpec((tk, w), lambda p, k: (k, p)))
    g_o = _wgrad("wgrad_out", merged, dx1b, colblk(256), tok2(D_MODEL), (D_MODEL, D_MODEL),
                 pl.BlockSpec((256, D_MODEL), lambda p, k: (p, 0)), 4, S)
    datt, rho, dyrin = _branch_bwd(dya, dyr, wa, wr, att)
    g_a = _wgrad("wgrad_attn", att, dya, tok2(512), colblk(512), (512, D_MODEL),
                 pl.BlockSpec((512, 512), lambda p, k: (0, p)), 2, S)
    g_r = _wgrad("wgrad_ret", yrin, dyr, colblk(256), tok2(D_MODEL), (D_MODEL, D_MODEL),
                 pl.BlockSpec((256, D_MODEL), lambda p, k: (p, 0)), 4, S)
    comm.grads({1: g_a, 2: g_r.reshape(N_SHARD, 256, D_MODEL), 3: g_o.reshape(N_SHARD, 256, D_MODEL)})
    dproj, xres = _ret_bwd(proj, rn, rstd, dyrin, states, tab, consts, dproj, comm.carry("ret_bwd"))
    comm.took("ret_bwd", xres)
    dqs, dks, dvs = [], [], []
    for gi, d in enumerate(DILATIONS):
        rtab = jnp.asarray(tab_np[0].reshape(3, S // d, d * 128))
        (dq, dk, dv), xres = _attn_bwd(qkvs[gi], _sub_view(datt, d), _sub_view(lse_tot, d), _sub_view(rho, d), rtab, d, gi,
                                       comm.carry(f"attn_bwd_g{gi}"))
        comm.took(f"attn_bwd_g{gi}", xres)
        dqs.append(dq)
        dks.append(dk)
        dvs.append(dv)
    dproj = _assemble_dproj((dqs, dks, dvs), dproj)
    g_sent, xres = _wgrad_in_half(ht, dproj, comm.sidx, False, comm.carry("wgrad_in_sent"))
    comm.took("wgrad_in_sent", xres)
    comm.grads({"in_sent": g_sent})
    g_kept, xres = _wgrad_in_half(ht, dproj, comm.sidx, True, comm.carry("wgrad_in_kept"))
    comm.grads({"in_kept": g_kept})
    comm.took("wgrad_in_kept", xres)
    grad_x, dg1, xres = _in_proj_bwd(dproj, w_in, x, g1, dx1, comm.carry("in_proj_bwd"))
    comm.took("in_proj_bwd", xres)
    return loss_p, grad_x, (dg1, dg2, dg3)


W_KINDS = ("col", "col", "lead", "lead", "lead", "lead", "lead")
W_SHARD = ((1024, W_IN_S), (512, 256), (256, 1024), (256, 1024), (1024, HID_S), (1024, HID_S), (HID_S, 1024))
N_W = len(W_KINDS)


def _full_shape(wi):
    R, C = W_SHARD[wi]
    return (R, N_SHARD * C) if W_KINDS[wi] == "col" else (N_SHARD, R, C)


def _view(ref, wi, s, half):
    R, C = W_SHARD[wi]
    rows = pl.ds(half * (R // 2), R // 2)
    if W_KINDS[wi] == "col":
        return ref.at[rows, pl.ds(pl.multiple_of(s * C, 128), C)]
    return ref.at[s, rows, :]


def _mesh_pos():
    x, y, c = lax.axis_index("x"), lax.axis_index("y"), lax.axis_index("c")
    chips = [(1 - x, y), (x, 1 - y), (1 - x, 1 - y)]
    return x, y, c, chips


def _cast_bf16(a):
    R, C = a.shape
    tr = R // 2 if R % 32 == 0 else R

    def body(a_ref, o_ref):
        o_ref[...] = a_ref[...].astype(BF16)

    spec = pl.BlockSpec((tr, C), lambda i: (i, 0))
    return pl.pallas_call(body, out_shape=SDS((R, C), BF16), grid=(R // tr,), in_specs=[spec], out_specs=spec,
                          compiler_params=_cparams("parallel"), name=f"cast_{R}x{C}")(a)


def _remote(send, recv, k, src, dst, to):
    return pltpu.make_async_remote_copy(src_ref=src, dst_ref=dst, send_sem=send.at[k], recv_sem=recv.at[k],
                                        device_id=to, device_id_type=MESH)


def _gather_now(wis, shards):
    n = len(wis)

    def body(*refs):
        sh, full = refs[:n], refs[n:2 * n]
        send, recv, loc = refs[2 * n:]
        x, y, c, _ = _mesh_pos()
        s_me, sib = 2 * x + y, (x, y, 1 - c)
        xn, yn = (1 - x, y), (x, 1 - y)
        flip = lambda a, b: a + b - 2 * a * b
        via = (flip(x, 1 - c), flip(y, c))
        onto = (flip(x, c), flip(y, 1 - c))
        shard_of = lambda chip: 2 * chip[0] + chip[1]
        own, started = [], []
        for i, wi in enumerate(wis):
            Rh = W_SHARD[wi][0] // 2
            for hf in range(2):
                cp = pltpu.make_async_copy(sh[i].at[pl.ds(hf * Rh, Rh), :], _view(full[i], wi, s_me, hf), loc.at[2 * i + hf])
                cp.start()
                own.append(cp)
            for j, chip in enumerate((xn, yn)):
                cp = _remote(send, recv, 6 * i + j, sh[i].at[pl.ds(c * Rh, Rh), :], _view(full[i], wi, s_me, c), (*chip, c))
                cp.start()
                started.append(cp)

        def pass_to_sibling(i, wi, k, s):
            mine = _view(full[i], wi, s, c)
            fw = _remote(send, recv, 6 * i + k, mine, mine, sib)
            fw.start()
            started.append(fw)

        for i, wi in enumerate(wis):
            for j, chip in enumerate((xn, yn)):
                land = _view(full[i], wi, shard_of(chip), c)
                _remote(send, recv, 6 * i + j, land, land, (*chip, c)).wait_recv()
                pass_to_sibling(i, wi, 3 + j, shard_of(chip))
            relay = _view(full[i], wi, shard_of(via), c)
            fw = _remote(send, recv, 6 * i + 2, relay, relay, (*onto, c))
            fw.start()
            started.append(fw)
        s_diag = 2 * (1 - x) + (1 - y)
        for i, wi in enumerate(wis):
            land = _view(full[i], wi, s_diag, c)
            _remote(send, recv, 6 * i + 2, land, land, (*onto, c)).wait_recv()
            pass_to_sibling(i, wi, 5, s_diag)
        for i, wi in enumerate(wis):
            for k, s in ((3, shard_of(xn)), (4, shard_of(yn)), (5, s_diag)):
                land = _view(full[i], wi, s, 1 - c)
                _remote(send, recv, 6 * i + k, land, land, sib).wait_recv()
        for cp in started:
            cp.wait_send()
        for cp in own:
            cp.wait()

    return pl.pallas_call(
        body, out_shape=tuple(SDS(_full_shape(wi), BF16) for wi in wis),
        in_specs=[ANY] * n, out_specs=tuple([ANY] * n),
        scratch_shapes=[pltpu.SemaphoreType.DMA((6 * n,)), pltpu.SemaphoreType.DMA((6 * n,)),
                        pltpu.SemaphoreType.DMA((2 * n,))],
        name="gather_now")(*shards)


def _ex_gather_ici(wis, shards, then_d2d=False, js=(0, 1, 2), fulls=None):
    n = len(wis)

    def build(ins, outs, send, recv, loc):
        x, y, c, chips = _mesh_pos()
        s_me, sib = 2 * x + y, (x, y, 1 - c)
        starts, waits, after = [], [], []
        for i, wi in enumerate(wis):
            Rh = W_SHARD[wi][0] // 2
            for hf in range(2 if fulls is None else 0):
                cp = pltpu.make_async_copy(ins[i].at[pl.ds(hf * Rh, Rh), :], _view(outs[i], wi, s_me, hf), loc.at[2 * i + hf])
                starts.append(cp)
                waits.append(cp.wait)
            for j, chip in enumerate(chips):
                if j not in js:
                    continue
                cp = _remote(send, recv, 3 * i + j, ins[i].at[pl.ds(c * Rh, Rh), :], _view(outs[i], wi, s_me, c), (*chip, c))
                land = _view(outs[i], wi, 2 * chip[0] + chip[1], c)
                starts.append(cp)
                waits += [cp.wait_send, _remote(send, recv, 3 * i + j, land, land, (*chip, c)).wait_recv]
                if then_d2d:
                    theirs = _view(outs[i], wi, 2 * chip[0] + chip[1], 1 - c)
                    fw = _remote(send, recv, 3 * n + 3 * i + j, land, land, sib)
                    waits.append(fw.start)
                    after += [fw.wait_send, _remote(send, recv, 3 * n + 3 * i + j, theirs, theirs, sib).wait_recv]
        return starts, waits + after

    aliases = {} if fulls is None else {n + i: i for i in range(n)}
    return _Exchange(list(shards) + list(fulls or ()), [SDS(_full_shape(wi), BF16) for wi in wis], aliases,
                     (6 if then_d2d else 3) * n, 2 * n, build)


def _ex_gather_d2d(wis, fulls):
    def build(ins, outs, send, recv, loc):
        x, y, c, chips = _mesh_pos()
        sib = (x, y, 1 - c)
        starts, waits = [], []
        for i, wi in enumerate(wis):
            for j, chip in enumerate(chips):
                mine = _view(outs[i], wi, 2 * chip[0] + chip[1], c)
                theirs = _view(outs[i], wi, 2 * chip[0] + chip[1], 1 - c)
                cp = _remote(send, recv, 3 * i + j, mine, mine, sib)
                starts.append(cp)
                waits += [cp.wait_send, _remote(send, recv, 3 * i + j, theirs, theirs, sib).wait_recv]
        return starts, waits

    return _Exchange(fulls, [SDS(f.shape, BF16) for f in fulls], {i: i for i in range(len(wis))}, 3 * len(wis), 0, build)


def _half_shape(wi):
    R, C = W_SHARD[wi]
    return (R // 2, N_SHARD * C) if W_KINDS[wi] == "col" else (N_SHARD, R // 2, C)


def _ex_pair(wis, grads):
    def build(ins, outs, send, recv, loc):
        x, y, c, _ = _mesh_pos()
        starts, waits = [], []
        for i, wi in enumerate(wis):
            Rh = W_SHARD[wi][0] // 2
            rows = pl.ds((1 - c) * Rh, Rh)
            if tuple(ins[i].shape) == _half_shape(wi):
                src = ins[i]
            else:
                src = ins[i].at[rows, :] if W_KINDS[wi] == "col" else ins[i].at[:, rows, :]
            cp = _remote(send, recv, i, src, outs[i], (x, y, 1 - c))
            starts.append(cp)
            waits.append(cp.wait)
        return starts, waits

    return _Exchange(grads, [SDS(_half_shape(wi), F32) for wi in wis], {}, len(wis), 0, build)


def _ex_chip(wis, pbs):
    def build(ins, outs, send, recv, loc):
        x, y, c, chips = _mesh_pos()
        starts, waits = [], []
        for i, wi in enumerate(wis):
            for j, chip in enumerate(chips):
                cp = _remote(send, recv, 3 * i + j, ins[i].at[j], outs[i].at[j], (*chip, c))
                starts.append(cp)
                waits.append(cp.wait)
        return starts, waits

    shapes = [SDS((3, W_SHARD[wi][0] // 2, W_SHARD[wi][1]), BF16) for wi in wis]
    return _Exchange(pbs, shapes, {}, 3 * len(wis), 0, build)


def _ex_share(wis, halves):
    def build(ins, outs, send, recv, loc):
        x, y, c, _ = _mesh_pos()
        sib = (x, y, 1 - c)
        starts, waits = [], []
        for i, wi in enumerate(wis):
            cp = _remote(send, recv, i, outs[i].at[c], outs[i].at[c], sib)
            starts.append(cp)
            waits += [cp.wait_send, _remote(send, recv, i, outs[i].at[1 - c], outs[i].at[1 - c], sib).wait_recv]
        return starts, waits

    return _Exchange(halves, [SDS(h.shape, F32) for h in halves], {i: i for i in range(len(wis))}, len(wis), 0, build)


def _row_tile(rh, C):
    best = 16
    for t in range(16, rh + 1, 16):
        if rh % t == 0 and t * C * 4 <= (3 << 19):
            best = t
    return best


def _pair_sum(wi, g, ra, sidx):
    R, C = W_SHARD[wi]
    Rh = R // 2
    tr = _row_tile(Rh, C)
    nt = Rh // tr
    off = 0 if tuple(g.shape) == _half_shape(wi) else nt
    col = W_KINDS[wi] == "col"

    def body(sidx_ref, *refs):
        gs, rs = refs[:4], refs[4:8]
        own_ref, pb_ref = refs[8:]
        own_ref[...] = gs[0][...] + rs[0][...]
        for j in range(3):
            pb_ref[j] = (gs[1 + j][...] + rs[1 + j][...]).astype(BF16)

    def gspec(slot):
        if col:
            return pl.BlockSpec((tr, C), lambda i, sx: (sx[4] * off + i, sx[slot]))
        return pl.BlockSpec((None, tr, C), lambda i, sx: (sx[slot], sx[4] * off + i, 0))

    def rspec(slot):
        if col:
            return pl.BlockSpec((tr, C), lambda i, sx: (i, sx[slot]))
        return pl.BlockSpec((None, tr, C), lambda i, sx: (sx[slot], i, 0))

    return pl.pallas_call(
        body, out_shape=(SDS((Rh, C), F32), SDS((3, Rh, C), BF16)),
        grid_spec=pltpu.PrefetchScalarGridSpec(
            num_scalar_prefetch=1, grid=(nt,),
            in_specs=[gspec(k) for k in range(4)] + [rspec(k) for k in range(4)],
            out_specs=(pl.BlockSpec((tr, C), lambda i, sx: (i, 0)), pl.BlockSpec((3, tr, C), lambda i, sx: (0, i, 0)))),
        compiler_params=_cparams("arbitrary"), name=f"pair_sum_w{wi}")(sidx, g, g, g, g, ra, ra, ra, ra)


def _chip_sum(wi, own, rb, sidx):
    R, C = W_SHARD[wi]
    Rh = R // 2
    tr = _row_tile(Rh, C)

    def body(sidx_ref, own_ref, rb_ref, o_ref):
        o_ref[...] = ((own_ref[...] + rb_ref[0].astype(F32)) + rb_ref[1].astype(F32)) + rb_ref[2].astype(F32)

    return pl.pallas_call(
        body, out_shape=SDS((2, Rh, C), F32),
        grid_spec=pltpu.PrefetchScalarGridSpec(
            num_scalar_prefetch=1, grid=(Rh // tr,),
            in_specs=[pl.BlockSpec((tr, C), lambda i, sx: (i, 0)), pl.BlockSpec((3, tr, C), lambda i, sx: (0, i, 0))],
            out_specs=pl.BlockSpec((None, tr, C), lambda i, sx: (sx[4], i, 0))),
        compiler_params=_cparams("arbitrary"), name=f"chip_sum_w{wi}")(sidx, own, rb)


def _gain_allgather(blk):
    m_per, n = blk.shape

    def body(x_ref, out_ref, send_sems, recv_sems, local_sem):
        x, y, c, chips = _mesh_pos()
        me, sibling = (x, y, c), (x, y, 1 - c)

        def rows(px, py, pc):
            return out_ref.at[pl.ds((4 * px + 2 * py + pc) * m_per, m_per), :]

        def copy(k, block, to, src=None):
            return pltpu.make_async_remote_copy(
                src_ref=rows(*block) if src is None else src, dst_ref=rows(*block),
                send_sem=send_sems.at[k], recv_sem=recv_sems.at[k], device_id=to, device_id_type=MESH)

        mine = pltpu.make_async_copy(x_ref, rows(*me), local_sem)
        mine.start()
        first = [copy(0, me, sibling, src=x_ref)]
        first += [copy(1 + j, me, (*chip, c), src=x_ref) for j, chip in enumerate(chips)]
        for cp in first:
            cp.start()
        passed = [copy(4 + j, (*chip, c), sibling) for j, chip in enumerate(chips)]
        for j, chip in enumerate(chips):
            copy(1 + j, (*chip, c), me).wait_recv()
            passed[j].start()
        copy(0, sibling, me).wait_recv()
        for j, chip in enumerate(chips):
            copy(4 + j, (*chip, 1 - c), me).wait_recv()
        for cp in first + passed:
            cp.wait_send()
        mine.wait()

    vm = pl.BlockSpec(memory_space=pltpu.VMEM)
    return pl.pallas_call(
        body, out_shape=SDS((8 * m_per, n), blk.dtype), in_specs=[vm], out_specs=vm,
        scratch_shapes=[pltpu.SemaphoreType.DMA((7,)), pltpu.SemaphoreType.DMA((7,)), pltpu.SemaphoreType.DMA],
        name="gain_allgather")(blk)


def _adam_math(w, g, m, v):
    mn = ADAM_B1 * m + (1.0 - ADAM_B1) * g
    vn = ADAM_B2 * v + (1.0 - ADAM_B2) * (g * g)
    mh = mn / (1.0 - ADAM_B1 ** ADAM_STEP)
    vh = vn / (1.0 - ADAM_B2 ** ADAM_STEP)
    return -ADAM_LR * (mh / (jnp.sqrt(vh) + ADAM_EPS) + ADAM_WD * w), mn, vn


def _adamw(wi, w, g, m, v):
    R, C = w.shape
    tr = _row_tile(R, C)

    def body(w_ref, g_ref, m_ref, v_ref, go_ref, d_ref, mn_ref, vn_ref):
        g = g_ref[...]
        go_ref[...] = g
        d_ref[...], mn_ref[...], vn_ref[...] = _adam_math(w_ref[...], g, m_ref[...], v_ref[...])

    spec = pl.BlockSpec((tr, C), lambda i: (i, 0))
    return pl.pallas_call(body, out_shape=(SDS((R, C), F32),) * 4, grid=(R // tr,), in_specs=[spec] * 4,
                          out_specs=(spec,) * 4, compiler_params=_cparams("parallel"), name=f"adamw_w{wi}")(w, g, m, v)


def _gain_update(gathered, w, m, v):
    def body(ga_ref, w_ref, m_ref, v_ref, g_ref, d_ref, mn_ref, vn_ref):
        g = ga_ref[0:8, :]
        for dev in range(1, 8):
            g = g + ga_ref[8 * dev:8 * dev + 8, :]
        g_ref[...] = g
        d_ref[...], mn_ref[...], vn_ref[...] = _adam_math(w_ref[...], g, m_ref[...], v_ref[...])

    return pl.pallas_call(body, out_shape=(SDS((8, 1024), F32),) * 4, name="gain_update")(gathered, w, m, v)


GROUP_FFN, GROUP_MIX, GROUP_IN = (4, 5, 6), (1, 2, 3), (0,)
REST = GROUP_MIX + GROUP_FFN


class _MeshComm:
    SCHEDULE = {
        "in_proj": [("ici", (1, 2, 3, 4))],
        "attn_fwd_g0": [("d2d", (1, 2, 3, 4))],
        "attn_fwd_g1": [("ici", (5,), (0, 1))],
        "attn_fwd_g2": [("ici", (5,), (2,)), ("ici", (6,), (0,))],
        "ret_fwd": [("ici", (6,), (1, 2))],
        "out_proj": [("d2d", (5,))],
        "ffn_up": [("d2d", (6,))],
        "ffn_up_bwd": [("pair", GROUP_FFN)],
        "ret_bwd": [("pair", GROUP_MIX), ("chip", (4,))],
        "attn_bwd_g0": [("chip", (5,))],
        "attn_bwd_g1": [("chip", (6,))],
        "attn_bwd_g2": [("chip", GROUP_MIX)],
        "wgrad_in_sent": [("share", GROUP_FFN + GROUP_MIX)],
        "wgrad_in_kept": [("pair", GROUP_IN)],
        "in_proj_bwd": [("chip", GROUP_IN)],
    }

    def __init__(self, shards):
        xi, yi, ci = lax.axis_index("x"), lax.axis_index("y"), lax.axis_index("c")
        self.sidx = jnp.stack([2 * xi + yi, 2 * (1 - xi) + yi, 2 * xi + (1 - yi), 2 * (1 - xi) + (1 - yi), ci]).astype(jnp.int32)
        self.shards, self.full = shards, {}
        self.g, self.own, self.pb, self.half, self.red = {}, {}, {}, {}, {}

    def w_in(self):
        return _gather_now(GROUP_IN, [self.shards[0]])[0]

    def weight(self, wi):
        return self.full[wi].reshape(D_MODEL, D_MODEL) if wi in (2, 3) else self.full[wi]

    def grads(self, by_wi):
        self.g.update(by_wi)

    def _exchange(self, stage, wis, js=(0, 1, 2)):
        pick = lambda table: [table[wi] for wi in wis]
        if stage == "ici":
            started = all(wi in self.full for wi in wis)
            return _ex_gather_ici(wis, pick(self.shards), js=js, fulls=pick(self.full) if started else None)
        if stage == "both":
            return _ex_gather_ici(wis, pick(self.shards), then_d2d=True)
        if stage == "d2d":
            return _ex_gather_d2d(wis, pick(self.full))
        if stage == "pair":
            return _ex_pair(wis, [self.g["in_sent"] if wi == 0 else self.g[wi] for wi in wis])
        if stage == "chip":
            return _ex_chip(wis, pick(self.pb))
        return _ex_share(wis, pick(self.half))

    def _landed(self, stage, wis, res):
        for wi, r in zip(wis, res):
            if stage in ("ici", "d2d", "both"):
                self.full[wi] = r
            elif stage == "pair":
                self.own[wi], self.pb[wi] = _pair_sum(wi, self.g["in_kept"] if wi == 0 else self.g[wi], r, self.sidx)
            elif stage == "chip":
                self.half[wi] = _chip_sum(wi, self.own[wi], r, self.sidx)
            else:
                self.red[wi] = r

    def carry(self, point):
        return [self._exchange(*entry) for entry in self.SCHEDULE.get(point, ())]

    def took(self, point, xres):
        for entry, res in zip(self.SCHEDULE.get(point, ()), xres):
            self._landed(entry[0], entry[1], res)

    def reduced(self):
        self._landed("share", GROUP_IN, _exchange_call(self._exchange("share", GROUP_IN), "share_w_in"))
        return [self.red[wi] for wi in range(N_W)]


def kernel(x, norm_mix_g, w_in, w_out_attn, w_out_ret, w_out, norm_ffn_g, w_ffn_gate, w_ffn_up, w_ffn_down, norm_final_g, loss_target, m_norm_mix_g, m_w_in, m_w_out_attn, m_w_out_ret, m_w_out, m_norm_ffn_g, m_w_ffn_gate, m_w_ffn_up, m_w_ffn_down, m_norm_final_g, v_norm_mix_g, v_w_in, v_w_out_attn, v_w_out_ret, v_w_out, v_norm_ffn_g, v_w_ffn_gate, v_w_ffn_up, v_w_ffn_down, v_norm_final_g):
    ws = (w_in, w_out_attn, w_out_ret, w_out, w_ffn_gate, w_ffn_up, w_ffn_down)
    ms = (m_w_in, m_w_out_attn, m_w_out_ret, m_w_out, m_w_ffn_gate, m_w_ffn_up, m_w_ffn_down)
    vs = (v_w_in, v_w_out_attn, v_w_out_ret, v_w_out, v_w_ffn_gate, v_w_ffn_up, v_w_ffn_down)
    shard2d = lambda a, wi: a.reshape(W_SHARD[wi])

    comm = _MeshComm([_cast_bf16(shard2d(w, wi)) for wi, w in enumerate(ws)])
    g3 = norm_final_g.reshape(1, D_MODEL)
    loss_p, grad_x, gain_g = _step(x[0], loss_target[0], norm_mix_g, norm_ffn_g, g3, comm)
    gred = comm.reduced()

    outs_g, outs_d, outs_m, outs_v = [], [], [], []
    for wi in range(N_W):
        g2d = gred[wi].reshape(W_SHARD[wi])
        gout, dlt, mn, vn = _adamw(wi, shard2d(ws[wi], wi), g2d, shard2d(ms[wi], wi), shard2d(vs[wi], wi))
        for lst, a in ((outs_g, gout), (outs_d, dlt), (outs_m, mn), (outs_v, vn)):
            lst.append(a.reshape(ws[wi].shape))

    pad8 = lambda rows: jnp.concatenate([r.reshape(1, D_MODEL) for r in rows]
                                        + [jnp.zeros((8 - len(rows), D_MODEL), F32)], axis=0)
    gathered = _gain_allgather(pad8((*gain_g, jnp.tile(loss_p[0:1], (1, D_MODEL // 128)))))
    gg, gd, gm, gv = _gain_update(gathered, pad8((norm_mix_g, norm_ffn_g, norm_final_g)),
                                  pad8((m_norm_mix_g, m_norm_ffn_g, m_norm_final_g)),
                                  pad8((v_norm_mix_g, v_norm_ffn_g, v_norm_final_g)))
    loss = gg[3, 0]

    def assemble(gain_rows, wlist):
        return (gain_rows[0:1], wlist[0], wlist[1], wlist[2], wlist[3], gain_rows[1:2],
                wlist[4], wlist[5], wlist[6], gain_rows[2])

    return (loss, grad_x[None], *assemble(gg, outs_g), *assemble(gd, outs_d), *assemble(gm, outs_m), *assemble(gv, outs_v))
```

```python
import functools
import math

import numpy as np
import jax
import jax.numpy as jnp
from jax import lax
from jax.experimental import pallas as pl
from jax.experimental.pallas import tpu as pltpu

F32, BF16 = jnp.float32, jnp.bfloat16
SDS = jax.ShapeDtypeStruct
MESH = pl.DeviceIdType.MESH

D_MODEL = 1024
PROJ_W = 9728
COLB = 512
N_COLB = PROJ_W // COLB
QA_B, KA_B, VA_B = 0, 3, 6
QR_B, KR_B = 9, 10
FFN_HID = 2816
N_SHARD = 4
HID_S = FFN_HID // N_SHARD
W_IN_S = PROJ_W // N_SHARD
DILATIONS = (1, 4, 16)
BLK = 128
RET_HEADS = 4
ROPE_THETA = 10000.0
NORM_EPS = 1e-6
ADAM_LR, ADAM_B1, ADAM_B2, ADAM_EPS, ADAM_WD, ADAM_STEP = 0.001, 0.9, 0.999, 1e-08, 0.01, 10
VMEM_LIMIT = 56 << 20


def _cparams(*sem):
    return pltpu.CompilerParams(dimension_semantics=sem or None, vmem_limit_bytes=VMEM_LIMIT)


def _dot(a, b):
    return jnp.dot(a, b, preferred_element_type=F32)


def _dot_nt(a, b):
    return lax.dot_general(a, b, (((1,), (1,)), ((), ())), preferred_element_type=F32)


def _dot_tn(a, b):
    return lax.dot_general(a, b, (((0,), (0,)), ((), ())), preferred_element_type=F32)


def _row_pieces(tm, sub=512):
    return [slice(i, i + sub) for i in range(0, tm, sub)]


def _sigmoid(z):
    return 0.5 * jnp.tanh(0.5 * z) + 0.5


ANY = pl.BlockSpec(memory_space=pl.ANY)


class _Exchange:
    def __init__(self, ins, out_shapes, aliases, n_sem, n_loc, build):
        self.ins, self.out_shapes, self.aliases = list(ins), list(out_shapes), dict(aliases)
        self.n_sem, self.n_loc, self.build = n_sem, n_loc, build

    def sems(self):
        return [pltpu.SemaphoreType.DMA((self.n_sem,)), pltpu.SemaphoreType.DMA((self.n_sem,)),
                pltpu.SemaphoreType.DMA((max(self.n_loc, 1),))]


def _carrier_call(body, args, *, out_shape, grid, in_specs, out_specs, scratch_shapes=(), sem, name, exchanges=(),
                  prefetch=None, in_out_aliases=None):
    out_shape, out_specs = tuple(out_shape), tuple(out_specs)
    n_in, n_out, n_scr = len(args), len(out_shape), len(scratch_shapes)
    n_pre = 0 if prefetch is None else 1
    x_args, x_outs, x_scr, spans = [], [], [], []
    aliases = {n_pre + a: o for a, o in (in_out_aliases or {}).items()}
    for ex in exchanges:
        i0, o0 = len(x_args), len(x_outs)
        for a, o in ex.aliases.items():
            aliases[n_pre + n_in + i0 + a] = n_out + o0 + o
        x_args += ex.ins
        x_outs += ex.out_shapes
        x_scr += ex.sems()
        spans.append((i0, len(ex.ins), o0, len(ex.out_shapes)))
    nx_in, nx_out = len(x_args), len(x_outs)

    def wrapped(*refs):
        refs = refs[n_pre:]
        ins, xin = refs[:n_in], refs[n_in:n_in + nx_in]
        o_base = n_in + nx_in
        outs, xout = refs[o_base:o_base + n_out], refs[o_base + n_out:o_base + n_out + nx_out]
        s_base = o_base + n_out + nx_out
        scr, xs = refs[s_base:s_base + n_scr], refs[s_base + n_scr:]

        def built(e):
            i0, ni, o0, no = spans[e]
            return exchanges[e].build(xin[i0:i0 + ni], xout[o0:o0 + no], *xs[3 * e:3 * e + 3])

        if exchanges:
            first = functools.reduce(jnp.logical_and, [pl.program_id(k) == 0 for k in range(len(grid))])
            last = functools.reduce(jnp.logical_and, [pl.program_id(k) == grid[k] - 1 for k in range(len(grid))])

            @pl.when(first)
            def _():
                for e in range(len(exchanges)):
                    for cp in built(e)[0]:
                        cp.start()

        body(*ins, *outs, *scr)

        if exchanges:
            @pl.when(last)
            def _():
                for e in range(len(exchanges)):
                    for w in built(e)[1]:
                        w()

    all_in, all_out = list(in_specs) + [ANY] * nx_in, out_specs + tuple([ANY] * nx_out)
    all_scr = list(scratch_shapes) + x_scr
    cparams = _cparams(*(sem if not exchanges else ("arbitrary",) * len(grid)))
    if prefetch is None:
        res = pl.pallas_call(wrapped, out_shape=out_shape + tuple(x_outs), grid=grid, in_specs=all_in, out_specs=all_out,
                             scratch_shapes=all_scr, input_output_aliases=aliases, compiler_params=cparams,
                             name=name)(*args, *x_args)
    else:
        gs = pltpu.PrefetchScalarGridSpec(num_scalar_prefetch=1, grid=grid, in_specs=all_in, out_specs=all_out,
                                          scratch_shapes=all_scr)
        res = pl.pallas_call(wrapped, out_shape=out_shape + tuple(x_outs), grid_spec=gs, input_output_aliases=aliases,
                             compiler_params=cparams, name=name)(prefetch, *args, *x_args)
    xres = [tuple(res[n_out + o0:n_out + o0 + no]) for (_, _, o0, no) in spans]
    return tuple(res[:n_out]), xres


def _tables(S):
    f32 = np.float32
    pos = np.arange(S, dtype=f32)
    lane = np.arange(128)
    inv = (f32(ROPE_THETA) ** (-np.arange(0, 64, 2, dtype=f32) / f32(64))).astype(f32)
    ang = (pos[:, None] * inv[None, :]).astype(np.float64)
    idx = (lane % 64) % 32
    c, s = np.cos(ang)[:, idx], np.sin(ang)[:, idx]
    first = ((lane % 64) < 32)[None, :]
    rope = np.stack([c, np.where(first, 0.0, s), np.where(first, -s, 0.0)])
    base = (f32(1.0) / (f32(ROPE_THETA) ** np.linspace(0.0, 1.0, 64, dtype=f32))).astype(f32)
    ang2 = (pos[:, None] * base[None, :]).astype(np.float64)
    c2, s2 = np.cos(ang2)[:, lane // 2], np.sin(ang2)[:, lane // 2]
    even = (lane % 2 == 0)[None, :]
    th = np.stack([c2, np.where(even, 0.0, s2), np.where(even, -s2, 0.0)])
    return np.stack([rope, th, th * (128 ** -0.5)]).astype(f32)


def _rot(a, c, sa, sb, shift):
    return a * c + pltpu.roll(a, shift, 1) * sa + pltpu.roll(a, 128 - shift, 1) * sb


def _unrot(g, c, sa, sb, shift):
    return g * c + pltpu.roll(g * sa, 128 - shift, 1) + pltpu.roll(g * sb, shift, 1)


def _ret_consts():
    h = np.arange(RET_HEADS, dtype=np.float64)
    log_g = np.log1p(-(2.0 ** (-5.0 - h)))
    idx = np.arange(BLK, dtype=np.float64)
    diff = idx[:, None] - idx[None, :]
    dmask = np.where(diff[None] >= 0, np.exp(np.maximum(diff, 0.0)[None] * log_g[:, None, None]), 0.0)
    zeta = np.exp((BLK - 1 - idx)[None, :] * log_g[:, None])
    xi = np.exp((idx + 1.0)[None, :] * log_g[:, None])
    dec = np.exp(BLK * log_g)
    rep = lambda v: np.broadcast_to(v[:, :, None], (RET_HEADS, BLK, 128))
    return (jnp.asarray(dmask, F32), jnp.asarray(rep(zeta), F32), jnp.asarray(rep(xi), F32),
            jnp.asarray(np.broadcast_to(dec[:, None, None], (RET_HEADS, 8, 256)), F32))


def _rms_fwd(x, g):
    S = x.shape[0]
    tm = 512

    def body(x_ref, g_ref, h_ref, ht_ref):
        xv = x_ref[...]
        r = lax.rsqrt(jnp.mean(xv * xv, axis=-1, keepdims=True) + NORM_EPS)
        h = xv * r * g_ref[...]
        h_ref[...] = h.astype(BF16)
        ht_ref[...] = h.T.astype(BF16)

    return pl.pallas_call(
        body, out_shape=(SDS((S, D_MODEL), BF16), SDS((D_MODEL, S), BF16)), grid=(S // tm,),
        in_specs=[pl.BlockSpec((tm, D_MODEL), lambda i: (i, 0)), pl.BlockSpec((1, D_MODEL), lambda i: (0, 0))],
        out_specs=(pl.BlockSpec((tm, D_MODEL), lambda i: (i, 0)), pl.BlockSpec((D_MODEL, tm), lambda i: (0, i))),
        compiler_params=_cparams("parallel"), name="rms_fwd")(x, g)


def _in_proj(h, w_in, tab, exchanges=()):
    S = h.shape[0]
    tm = min(S, 4096)

    def body(h_ref, w_ref, t_ref, o_ref):
        j = pl.program_id(1)
        is_rope = j < 6
        is_theta = (j == QR_B) | (j == KR_B)
        sub = 512

        def rotated(shift):
            for i in range(tm // sub):
                rows = slice(i * sub, (i + 1) * sub)
                acc = _dot(h_ref[rows, :], w_ref[...])
                c, sa, sb = t_ref[0, 0, rows, :], t_ref[0, 1, rows, :], t_ref[0, 2, rows, :]
                for k in range(COLB // 128):
                    sl = slice(k * 128, (k + 1) * 128)
                    o_ref[rows, sl] = _rot(acc[:, sl], c, sa, sb, shift).astype(BF16)

        @pl.when(is_rope)
        def _():
            rotated(32)

        @pl.when(is_theta)
        def _():
            rotated(1)

        @pl.when(jnp.logical_not(is_rope | is_theta))
        def _():
            o_ref[...] = _dot(h_ref[...], w_ref[...]).astype(BF16)

    def tab_map(i, j):
        return (jnp.where(j == QR_B, 1, jnp.where(j == KR_B, 2, 0)), 0, i, 0)

    (proj,), xres = _carrier_call(
        body, (h, w_in, tab), out_shape=(SDS((S, PROJ_W), BF16),), grid=(S // tm, N_COLB),
        in_specs=[pl.BlockSpec((tm, D_MODEL), lambda i, j: (i, 0)),
                  pl.BlockSpec((D_MODEL, COLB), lambda i, j: (0, j)),
                  pl.BlockSpec((1, 3, tm, 128), tab_map)],
        out_specs=(pl.BlockSpec((tm, COLB), lambda i, j: (i, j)),),
        sem=("parallel", "arbitrary"), name="in_proj", exchanges=exchanges)
    return proj, xres


def _band_mask(n):
    qi = lax.broadcasted_iota(jnp.int32, (BLK, 2 * BLK), 0)
    kj = lax.broadcasted_iota(jnp.int32, (BLK, 2 * BLK), 1)
    dist = BLK + qi - kj
    return (dist >= 0) & (dist <= BLK) & ((kj >= BLK) | (n > 0))


def _qkv_col(d, gi):
    if d == 1:
        return lambda t, r: 3 * t + gi
    return lambda t, r: 3 * r + t


def _attn_fwd(qkv, d, gi, exchanges=()):
    L = qkv.shape[0]
    nb = L // BLK

    def body(q_ref, kc_ref, kp_ref, vc_ref, vp_ref, o_ref, lse_ref):
        n = pl.program_id(1)
        mask = _band_mask(n)
        mask2 = jnp.concatenate([mask, mask], axis=0)
        lane = lax.broadcasted_iota(jnp.int32, (BLK, 128), 1)
        lo = lane < 64
        lse_all = jnp.zeros((BLK, 128), F32)
        chunks = [slice(c * 128, (c + 1) * 128) for c in range(4)]
        scores, vals = [], []
        for sl in chunks:
            q = q_ref[:, sl]
            k = jnp.concatenate([kp_ref[:, sl], kc_ref[:, sl]], axis=0)
            vals.append(jnp.concatenate([vp_ref[:, sl], vc_ref[:, sl]], axis=0))
            q2 = jnp.concatenate([jnp.where(lo, q, jnp.zeros_like(q)), jnp.where(lo, jnp.zeros_like(q), q)], axis=0)
            scores.append(_dot_nt(q2, k))
        probs = []
        for c, s in enumerate(scores):
            s = jnp.where(mask2, s * 0.125, jnp.float32(-1e30))
            m = jnp.max(s, axis=-1, keepdims=True)
            p = jnp.exp(s - m)
            l = jnp.sum(p, axis=-1, keepdims=True)
            probs.append((p / l).astype(BF16))
            lse = m + jnp.log(l)
            lse_all = jnp.where(lane // 16 == 2 * c, lse[:BLK], jnp.where(lane // 16 == 2 * c + 1, lse[BLK:], lse_all))
        for sl, p, v in zip(chunks, probs, vals):
            o2 = _dot(p, v)
            o_ref[:, sl] = jnp.where(lo, o2[:BLK], o2[BLK:])
        lse_ref[...] = lse_all

    prev = lambda n: jnp.maximum(n - 1, 0)
    col = _qkv_col(d, gi)
    return _carrier_call(
        body, (qkv,) * 5, out_shape=(SDS((L, d * 512), F32), SDS((L, d * 128), F32)), grid=(d, nb),
        in_specs=[pl.BlockSpec((BLK, 512), lambda r, n: (n, col(0, r))),
                  pl.BlockSpec((BLK, 512), lambda r, n: (n, col(1, r))),
                  pl.BlockSpec((BLK, 512), lambda r, n: (prev(n), col(1, r))),
                  pl.BlockSpec((BLK, 512), lambda r, n: (n, col(2, r))),
                  pl.BlockSpec((BLK, 512), lambda r, n: (prev(n), col(2, r)))],
        out_specs=(pl.BlockSpec((BLK, 512), lambda r, n: (n, r)),
                   pl.BlockSpec((BLK, 128), lambda r, n: (n, r))),
        sem=("parallel", "arbitrary"), name=f"attn_fwd_g{gi}", exchanges=exchanges)


def _qkv_to_sub(proj, d, gi):
    S = proj.shape[0]
    tm = 512
    n = tm // d

    def body(q_ref, k_ref, v_ref, o_ref, scr):
        for t, ref in enumerate((q_ref, k_ref, v_ref)):
            for c in range(4):
                scr[c] = ref[:, c * 128:(c + 1) * 128].astype(F32)
            for r in range(d):
                for c in range(4):
                    col = (3 * r + t) * 512 + c * 128
                    o_ref[:, col:col + 128] = scr[c, pl.ds(r, n, stride=d), :].astype(BF16)

    return pl.pallas_call(
        body, out_shape=SDS((S // d, d * 1536), BF16), grid=(S // tm,),
        in_specs=[pl.BlockSpec((tm, 512), lambda i, b=b: (i, b + gi)) for b in (QA_B, KA_B, VA_B)],
        out_specs=pl.BlockSpec((n, d * 1536), lambda i: (i, 0)),
        scratch_shapes=[pltpu.VMEM((4, tm, 128), F32)],
        compiler_params=_cparams("parallel"), name=f"qkv_to_sub_g{gi}")(proj, proj, proj)


def _attn_merge(os_, lses):
    S = os_[0].shape[0]
    tm = 512

    def body(o0, o1, o2, l0, l1, l2, att_ref, lt_ref, so1, so2, sl1, sl2):
        lo = lax.broadcasted_iota(jnp.int32, (tm, 128), 1) < 64

        def natural(ref, d, scr, width):
            nch = width // 128
            if d == 1:
                return [ref[:, c * 128:(c + 1) * 128] for c in range(nch)]
            for r in range(d):
                for c in range(nch):
                    scr[c, pl.ds(r, tm // d, stride=d), :] = ref[:, r * width + c * 128:r * width + (c + 1) * 128]
            return [scr[c] for c in range(nch)]

        ls = [natural(l, d, s, 128)[0] for l, d, s in zip((l0, l1, l2), DILATIONS, (None, sl1, sl2))]
        m = jnp.maximum(jnp.maximum(ls[0], ls[1]), ls[2])
        es = [jnp.exp(v - m) for v in ls]
        z = es[0] + es[1] + es[2]
        lt_ref[...] = m + jnp.log(z)
        ws = [e / z for e in es]
        o_nat = [natural(o, d, s, 512) for o, d, s in zip((o0, o1, o2), DILATIONS, (None, so1, so2))]
        for c in range(4):
            acc = jnp.zeros((tm, 128), F32)
            for g in range(3):
                w_lo = jnp.broadcast_to(ws[g][:, 32 * c:32 * c + 1], (tm, 128))
                w_hi = jnp.broadcast_to(ws[g][:, 32 * c + 16:32 * c + 17], (tm, 128))
                acc = acc + jnp.where(lo, w_lo, w_hi) * o_nat[g][c]
            att_ref[:, c * 128:(c + 1) * 128] = acc.astype(BF16)

    sub = lambda w: [pl.BlockSpec((tm // d, d * w), lambda i: (i, 0)) for d in DILATIONS]
    return pl.pallas_call(
        body, out_shape=(SDS((S, 512), BF16), SDS((S, 128), F32)), grid=(S // tm,),
        in_specs=sub(512) + sub(128),
        out_specs=(pl.BlockSpec((tm, 512), lambda i: (i, 0)), pl.BlockSpec((tm, 128), lambda i: (i, 0))),
        scratch_shapes=[pltpu.VMEM((4, tm, 128), F32), pltpu.VMEM((4, tm, 128), F32),
                        pltpu.VMEM((1, tm, 128), F32), pltpu.VMEM((1, tm, 128), F32)],
        compiler_params=_cparams("parallel"), name="attn_merge")(*os_, *lses)


def _assemble_dproj(att_grads, dproj):
    S = dproj.shape[0]
    tm = 256

    def body(*refs):
        a = [refs[3 * t:3 * t + 3] for t in range(3)]
        dp_prev, o_ref, scr = refs[9:]
        for t in range(3):
            for g, d in enumerate(DILATIONS):
                base = (3 * t + g) * COLB
                if d == 1:
                    o_ref[:, base:base + COLB] = a[t][g][...]
                    continue
                for c in range(4):
                    for r in range(d):
                        scr[c, pl.ds(r, tm // d, stride=d), :] = a[t][g][:, r * 512 + c * 128:r * 512 + (c + 1) * 128].astype(F32)
                    o_ref[:, base + c * 128:base + (c + 1) * 128] = scr[c].astype(BF16)

    sub = [pl.BlockSpec((tm // d, d * 512), lambda i: (i, 0)) for d in DILATIONS]
    flat = [att_grads[t][g] for t in range(3) for g in range(3)]
    return pl.pallas_call(
        body, out_shape=SDS((S, PROJ_W), BF16), grid=(S // tm,),
        in_specs=sub * 3 + [ANY], out_specs=pl.BlockSpec((tm, 9 * COLB), lambda i: (i, 0)),
        scratch_shapes=[pltpu.VMEM((4, tm, 128), F32)], input_output_aliases={9: 0},
        compiler_params=_cparams("parallel"), name="assemble_dproj")(*flat, dproj)


def _ret_fwd(proj, consts, exchanges=()):
    S = proj.shape[0]
    nc = S // BLK
    dmask, zeta, xi, dec = consts

    def body(q_ref, k_ref, v0_ref, v1_ref, g0_ref, g1_ref, dm_ref, z_ref, x_ref, dec_ref,
             y_ref, rn_ref, rs_ref, st_ref, R):
        @pl.when(pl.program_id(0) == 0)
        def _():
            R[...] = jnp.zeros_like(R)

        lane16 = lax.broadcasted_iota(jnp.int32, (BLK, 128), 1) // 16
        rs_all = jnp.zeros((BLK, 128), F32)
        first = []
        for h in range(RET_HEADS):
            hs = slice(h * 128, (h + 1) * 128)
            q, k = q_ref[:, hs], k_ref[:, hs]
            v = (v0_ref if h < 2 else v1_ref)[:, (h % 2) * 256:(h % 2 + 1) * 256]
            Rb = R[h].astype(BF16)
            st_ref[h] = Rb
            kz = (k.astype(F32) * z_ref[h]).astype(BF16)
            first.append((v, _dot_nt(q, k), _dot((q.astype(F32) * x_ref[h]).astype(BF16), Rb), _dot_tn(kz, v)))
        masked = [(s * dm_ref[h]).astype(BF16) for h, (_, s, _, _) in enumerate(first)]
        for h in range(RET_HEADS):
            vs = slice((h % 2) * 256, (h % 2 + 1) * 256)
            os_ = slice(h * 256, (h + 1) * 256)
            v, _, cross, kv = first[h]
            o = _dot(masked[h], v) + cross
            R[h] = R[h] * dec_ref[h, 0:1, :] + kv
            mu = jnp.mean(o, axis=-1, keepdims=True)
            oc = o - mu
            rstd = lax.rsqrt(jnp.mean(oc * oc, axis=-1, keepdims=True) + NORM_EPS)
            rn = oc * rstd
            gr = (g0_ref if h < 2 else g1_ref)[:, vs].astype(F32)
            y_ref[:, os_] = (rn * gr * _sigmoid(gr)).astype(BF16)
            rn_ref[:, os_] = rn.astype(BF16)
            rs_all = jnp.where(lane16 == h, rstd, rs_all)
        rs_ref[...] = rs_all

    cst = lambda shape: pl.BlockSpec(shape, lambda c: (0, 0, 0))
    blk = lambda j: pl.BlockSpec((BLK, 512), lambda c: (c, j))
    return _carrier_call(
        body, (proj, proj, proj, proj, proj, proj, dmask, zeta, xi, dec),
        out_shape=(SDS((S, 1024), BF16), SDS((S, 1024), BF16), SDS((S, 128), F32), SDS((RET_HEADS, nc, BLK, 256), BF16)),
        grid=(nc,),
        in_specs=[blk(QR_B), blk(KR_B), blk(11), blk(12), blk(13), blk(14),
                  cst((RET_HEADS, BLK, BLK)), cst((RET_HEADS, BLK, 128)), cst((RET_HEADS, BLK, 128)), cst((RET_HEADS, 8, 256))],
        out_specs=(pl.BlockSpec((BLK, 1024), lambda c: (c, 0)), pl.BlockSpec((BLK, 1024), lambda c: (c, 0)),
                   pl.BlockSpec((BLK, 128), lambda c: (c, 0)),
                   pl.BlockSpec((RET_HEADS, None, BLK, 256), lambda c: (0, c, 0, 0))),
        scratch_shapes=[pltpu.VMEM((RET_HEADS, BLK, 256), F32)],
        sem=("arbitrary",), name="ret_fwd", exchanges=exchanges)


def _branch_merge(att, yrin, proj, wa, wr):
    S = att.shape[0]
    tm = min(S, 2048)

    def body(a_ref, y_ref, ga_ref, gr_ref, wa_ref, wr_ref, m_ref, ya_ref, yr_ref):
        for rows in _row_pieces(tm):
            ya = _dot(a_ref[rows, :], wa_ref[...])
            yr = _dot(y_ref[rows, :], wr_ref[...])
            m_ref[rows, :] = (_sigmoid(ga_ref[rows, :].astype(F32)) * ya
                              + _sigmoid(gr_ref[rows, :].astype(F32)) * yr).astype(BF16)
            ya_ref[rows, :] = ya.astype(BF16)
            yr_ref[rows, :] = yr.astype(BF16)

    ospec = pl.BlockSpec((tm, 512), lambda i, j: (i, j))
    return pl.pallas_call(
        body, out_shape=(SDS((S, D_MODEL), BF16),) * 3, grid=(S // tm, 2),
        in_specs=[pl.BlockSpec((tm, 512), lambda i, j: (i, 0)), pl.BlockSpec((tm, 1024), lambda i, j: (i, 0)),
                  pl.BlockSpec((tm, 512), lambda i, j: (i, 15 + j)), pl.BlockSpec((tm, 512), lambda i, j: (i, 17 + j)),
                  pl.BlockSpec((512, 512), lambda i, j: (0, j)), pl.BlockSpec((1024, 512), lambda i, j: (0, j))],
        out_specs=(ospec, ospec, ospec),
        compiler_params=_cparams("parallel", "arbitrary"), name="branch_merge")(att, yrin, proj, proj, wa, wr)


def _out_proj(merged, wo, x, g2, exchanges=()):
    S = x.shape[0]
    tm = 1024

    def body(m_ref, w_ref, x_ref, g_ref, x1_ref, h2_ref):
        x1 = x_ref[...] + _dot(m_ref[...], w_ref[...])
        x1_ref[...] = x1
        r = lax.rsqrt(jnp.mean(x1 * x1, axis=-1, keepdims=True) + NORM_EPS)
        h2_ref[...] = (x1 * r * g_ref[...]).astype(BF16)

    row = pl.BlockSpec((tm, D_MODEL), lambda i: (i, 0))
    return _carrier_call(
        body, (merged, wo, x, g2), out_shape=(SDS((S, D_MODEL), F32), SDS((S, D_MODEL), BF16)), grid=(S // tm,),
        in_specs=[row, pl.BlockSpec((D_MODEL, D_MODEL), lambda i: (0, 0)), row, pl.BlockSpec((1, D_MODEL), lambda i: (0, 0))],
        out_specs=(row, row), sem=("parallel",), name="out_proj", exchanges=exchanges)


def _ffn_up(h2, wg, wu, exchanges=()):
    S = h2.shape[0]
    tm = min(S, 2048)

    def body(h_ref, wg_ref, wu_ref, g_ref, u_ref, a_ref):
        for rows in _row_pieces(tm):
            hv = h_ref[rows, :]
            g = _dot(hv, wg_ref[...])
            u = _dot(hv, wu_ref[...])
            g_ref[rows, :] = g.astype(BF16)
            u_ref[rows, :] = u.astype(BF16)
            a_ref[rows, :] = (g * _sigmoid(g) * u).astype(BF16)

    wspec = pl.BlockSpec((None, D_MODEL, HID_S), lambda i, s: (s, 0, 0))
    ospec = pl.BlockSpec((None, tm, HID_S), lambda i, s: (s, i, 0))
    return _carrier_call(
        body, (h2, wg, wu), out_shape=(SDS((N_SHARD, S, HID_S), BF16),) * 3, grid=(S // tm, N_SHARD),
        in_specs=[pl.BlockSpec((tm, D_MODEL), lambda i, s: (i, 0)), wspec, wspec],
        out_specs=(ospec, ospec, ospec),
        sem=("parallel", "arbitrary"), name="ffn_up", exchanges=exchanges)


def _ffn_down_loss(act, wd, x1, g3, tgt):
    S = x1.shape[0]
    tm = 512

    def body(a_ref, w_ref, x_ref, g_ref, t_ref, dx_ref, dxb_ref, dg_ref, ls_ref):
        @pl.when(pl.program_id(0) == 0)
        def _():
            dg_ref[...] = jnp.zeros_like(dg_ref)
            ls_ref[...] = jnp.zeros_like(ls_ref)

        g = g_ref[...]
        for rows in _row_pieces(tm, 256):
            y = _dot(a_ref[0, rows, :], w_ref[0])
            for s in range(1, N_SHARD):
                y = y + _dot(a_ref[s, rows, :], w_ref[s])
            x2 = x_ref[rows, :] + y
            r = lax.rsqrt(jnp.mean(x2 * x2, axis=-1, keepdims=True) + NORM_EPS)
            xh = x2 * r
            err = xh * g - t_ref[rows, :]
            ls_ref[...] += jnp.sum(jnp.sum(err * err, axis=-1, keepdims=True), axis=0, keepdims=True) * (0.5 / D_MODEL)
            dy = err * (1.0 / D_MODEL)
            dg_ref[...] += jnp.sum(dy * xh, axis=0, keepdims=True)
            dxh = dy * g
            dx = r * (dxh - xh * jnp.mean(dxh * xh, axis=-1, keepdims=True))
            dx_ref[rows, :] = dx
            dxb_ref[rows, :] = dx.astype(BF16)

    row = pl.BlockSpec((tm, D_MODEL), lambda i: (i, 0))
    vec = pl.BlockSpec((1, D_MODEL), lambda i: (0, 0))
    return pl.pallas_call(
        body, out_shape=(SDS((S, D_MODEL), F32), SDS((S, D_MODEL), BF16), SDS((1, D_MODEL), F32), SDS((8, 128), F32)),
        grid=(S // tm,),
        in_specs=[pl.BlockSpec((N_SHARD, tm, HID_S), lambda i: (0, i, 0)),
                  pl.BlockSpec((N_SHARD, HID_S, D_MODEL), lambda i: (0, 0, 0), pipeline_mode=pl.Buffered(1)),
                  row, vec, row],
        out_specs=(row, row, vec, pl.BlockSpec((8, 128), lambda i: (0, 0))),
        compiler_params=_cparams("arbitrary"), name="ffn_down_loss")(act, wd, x1, g3, tgt)


def _ffn_down_bwd(dx2b, wd, gte, up):
    S = dx2b.shape[0]
    tm = min(S, 2048)

    def body(d_ref, w_ref, g_ref, u_ref, dg_ref, du_ref):
        for rows in _row_pieces(tm, 256):
            da = _dot_nt(d_ref[rows, :], w_ref[...])
            g = g_ref[rows, :].astype(F32)
            sg = _sigmoid(g)
            dg_ref[rows, :] = (da * u_ref[rows, :].astype(F32) * sg * (1.0 + g * (1.0 - sg))).astype(BF16)
            du_ref[rows, :] = (da * g * sg).astype(BF16)

    aspec = pl.BlockSpec((None, tm, HID_S), lambda i, s: (s, i, 0))
    return pl.pallas_call(
        body, out_shape=(SDS((N_SHARD, S, HID_S), BF16),) * 2, grid=(S // tm, N_SHARD),
        in_specs=[pl.BlockSpec((tm, D_MODEL), lambda i, s: (i, 0)),
                  pl.BlockSpec((None, HID_S, D_MODEL), lambda i, s: (s, 0, 0)), aspec, aspec],
        out_specs=(aspec, aspec),
        compiler_params=_cparams("parallel", "arbitrary"), name="ffn_down_bwd")(dx2b, wd, gte, up)


def _wgrad(name, a, b, a_spec, b_spec, out_shape, out_spec, n_par, S):
    tk = 2048

    def body(a_ref, b_ref, o_ref):
        @pl.when(pl.program_id(1) == 0)
        def _():
            o_ref[...] = jnp.zeros_like(o_ref)

        o_ref[...] += _dot_tn(a_ref[...], b_ref[...])

    return pl.pallas_call(
        body, out_shape=SDS(out_shape, F32), grid=(n_par, S // tk),
        in_specs=[a_spec(tk), b_spec(tk)], out_specs=out_spec,
        compiler_params=_cparams("parallel", "arbitrary"), name=name)(a, b)


def _ffn_up_bwd(dgte, dup, wg, wu, x1, g2, dx2, exchanges=()):
    S = x1.shape[0]
    tm = 512

    def body(dg_ref, du_ref, wg_ref, wu_ref, x_ref, g_ref, dx2_ref, dx_ref, dxb_ref, dgn_ref):
        @pl.when(pl.program_id(0) == 0)
        def _():
            dgn_ref[...] = jnp.zeros_like(dgn_ref)

        for rows in _row_pieces(tm, 256):
            dh = _dot_nt(dg_ref[0, rows, :], wg_ref[0]) + _dot_nt(du_ref[0, rows, :], wu_ref[0])
            for s in range(1, N_SHARD):
                dh = dh + _dot_nt(dg_ref[s, rows, :], wg_ref[s]) + _dot_nt(du_ref[s, rows, :], wu_ref[s])
            xv = x_ref[rows, :]
            r = lax.rsqrt(jnp.mean(xv * xv, axis=-1, keepdims=True) + NORM_EPS)
            xh = xv * r
            dgn_ref[...] += jnp.sum(dh * xh, axis=0, keepdims=True)
            dxh = dh * g_ref[...]
            dx = dx2_ref[rows, :] + r * (dxh - xh * jnp.mean(dxh * xh, axis=-1, keepdims=True))
            dx_ref[rows, :] = dx
            dxb_ref[rows, :] = dx.astype(BF16)

    row = pl.BlockSpec((tm, D_MODEL), lambda i: (i, 0))
    vec = pl.BlockSpec((1, D_MODEL), lambda i: (0, 0))
    aspec = pl.BlockSpec((N_SHARD, tm, HID_S), lambda i: (0, i, 0))
    wspec = pl.BlockSpec((N_SHARD, D_MODEL, HID_S), lambda i: (0, 0, 0), pipeline_mode=pl.Buffered(1))
    return _carrier_call(
        body, (dgte, dup, wg, wu, x1, g2, dx2),
        out_shape=(SDS((S, D_MODEL), F32), SDS((S, D_MODEL), BF16), SDS((1, D_MODEL), F32)),
        grid=(S // tm,),
        in_specs=[aspec, aspec, wspec, wspec, row, vec, row], out_specs=(row, row, vec),
        sem=("arbitrary",), name="ffn_up_bwd", exchanges=exchanges)


def _out_proj_bwd(dx1b, wo, proj, ya, yr):
    S = dx1b.shape[0]
    tm = 512
    gate0 = 15 * COLB

    def body(d_ref, w_ref, ga_ref, gr_ref, ya_ref, yr_ref, dya_ref, dyr_ref, dp_ref):
        for rows in _row_pieces(tm, 256):
            dm = _dot_nt(d_ref[rows, :], w_ref[...])
            sa = _sigmoid(ga_ref[rows, :].astype(F32))
            sr = _sigmoid(gr_ref[rows, :].astype(F32))
            dya_ref[rows, :] = (dm * sa).astype(BF16)
            dyr_ref[rows, :] = (dm * sr).astype(BF16)
            dp_ref[rows, 0:D_MODEL] = (dm * ya_ref[rows, :].astype(F32) * sa * (1.0 - sa)).astype(BF16)
            dp_ref[rows, D_MODEL:2 * D_MODEL] = (dm * yr_ref[rows, :].astype(F32) * sr * (1.0 - sr)).astype(BF16)

    row = pl.BlockSpec((tm, D_MODEL), lambda i: (i, 0))
    cols = lambda c0, w: pl.BlockSpec((pl.Element(tm), pl.Element(w)), lambda i: (i * tm, c0))
    return pl.pallas_call(
        body, out_shape=(SDS((S, D_MODEL), BF16), SDS((S, D_MODEL), BF16), SDS((S, PROJ_W), BF16)), grid=(S // tm,),
        in_specs=[row, pl.BlockSpec((D_MODEL, D_MODEL), lambda i: (0, 0), pipeline_mode=pl.Buffered(1)),
                  cols(gate0, D_MODEL), cols(gate0 + D_MODEL, D_MODEL), row, row],
        out_specs=(row, row, cols(gate0, 2 * D_MODEL)),
        compiler_params=_cparams("parallel"), name="out_proj_bwd")(dx1b, wo, proj, proj, ya, yr)


def _branch_bwd(dya, dyr, wa, wr, att):
    S = dya.shape[0]
    tm = 1024

    def body(da_ref, dr_ref, wa_ref, wr_ref, att_ref, datt_ref, rho_ref, dyi_ref):
        datt = _dot_nt(da_ref[...], wa_ref[...])
        datt_ref[...] = datt.astype(BF16)
        dyi_ref[...] = _dot_nt(dr_ref[...], wr_ref[...]).astype(BF16)
        prod = datt * att_ref[...].astype(F32)
        lane = lax.broadcasted_iota(jnp.int32, (tm, 128), 1)
        lo = lane < 64
        rho = jnp.zeros((tm, 128), F32)
        for c in range(4):
            pc = prod[:, c * 128:(c + 1) * 128]
            tot = jnp.sum(pc, axis=-1, keepdims=True)
            low = jnp.sum(jnp.where(lo, pc, 0.0), axis=-1, keepdims=True)
            rho = jnp.where(lane // 16 == 2 * c, low, jnp.where(lane // 16 == 2 * c + 1, tot - low, rho))
        rho_ref[...] = rho

    row = lambda w: pl.BlockSpec((tm, w), lambda i: (i, 0))
    return pl.pallas_call(
        body, out_shape=(SDS((S, 512), BF16), SDS((S, 128), F32), SDS((S, 1024), BF16)), grid=(S // tm,),
        in_specs=[row(1024), row(1024), pl.BlockSpec((512, 1024), lambda i: (0, 0)),
                  pl.BlockSpec((1024, 1024), lambda i: (0, 0)), row(512)],
        out_specs=(row(512), row(128), row(1024)),
        compiler_params=_cparams("parallel"), name="branch_bwd")(dya, dyr, wa, wr, att)


def _attn_bwd(qkv, datt, lse, rho, rtab, d, gi, exchanges=()):
    L = qkv.shape[0]
    nb = L // BLK
    T = d * nb

    def body(q_ref, kc_ref, kp_ref, vc_ref, vp_ref, do_ref, lse_ref, rho_ref, tq_ref, tk_ref,
             dq_ref, dk_ref, dv_ref, ck, cv):
        t = pl.program_id(0)
        n = jnp.minimum(t, T - 1) % nb

        @pl.when(t == 0)
        def _():
            ck[...] = jnp.zeros_like(ck)
            cv[...] = jnp.zeros_like(cv)

        def store_rot(ref, val, t_ref, c):
            sl = slice(c * 128, (c + 1) * 128)
            ref[:, sl] = _unrot(val, t_ref[0], t_ref[1], t_ref[2], 32).astype(BF16)

        @pl.when(t < T)
        def _():
            mask = _band_mask(n)
            mask2 = jnp.concatenate([mask, mask], axis=0)
            lo = lax.broadcasted_iota(jnp.int32, (BLK, 128), 1) < 64

            def stacked(a):
                return jnp.concatenate([jnp.where(lo, a, jnp.zeros_like(a)), jnp.where(lo, jnp.zeros_like(a), a)], axis=0)

            def head_cols(ref, c):
                return jnp.concatenate([jnp.broadcast_to(ref[:, 32 * c:32 * c + 1], (BLK, 2 * BLK)),
                                        jnp.broadcast_to(ref[:, 32 * c + 16:32 * c + 17], (BLK, 2 * BLK))], axis=0)

            ops, raw = [], []
            for c in range(4):
                sl = slice(c * 128, (c + 1) * 128)
                q2, do2 = stacked(q_ref[:, sl]), stacked(do_ref[:, sl])
                k = jnp.concatenate([kp_ref[:, sl], kc_ref[:, sl]], axis=0)
                v = jnp.concatenate([vp_ref[:, sl], vc_ref[:, sl]], axis=0)
                ops.append((q2, do2, k))
                raw.append((_dot_nt(q2, k), _dot_nt(do2, v)))
            grads = []
            for c, (s, dp) in enumerate(raw):
                p = jnp.where(mask2, jnp.exp(s * 0.125 - head_cols(lse_ref, c)), 0.0)
                grads.append(((p * (dp - head_cols(rho_ref, c)) * 0.125).astype(BF16), p.astype(BF16)))
            for c, ((q2, do2, k), (ds, pb)) in enumerate(zip(ops, grads)):
                sl = slice(c * 128, (c + 1) * 128)
                dq2 = _dot(ds, k)
                dq_c = jnp.where(lo, dq2[:BLK], dq2[BLK:])
                dk_c = _dot_tn(ds, q2)
                dv_c = _dot_tn(pb, do2)
                store_rot(dq_ref, dq_c, tq_ref, c)
                store_rot(dk_ref, ck[:, sl] + dk_c[:BLK], tk_ref, c)
                dv_ref[:, sl] = (cv[:, sl] + dv_c[:BLK]).astype(BF16)
                ck[:, sl] = dk_c[BLK:]
                cv[:, sl] = dv_c[BLK:]

        @pl.when(t == T)
        def _():
            for c in range(4):
                sl = slice(c * 128, (c + 1) * 128)
                store_rot(dk_ref, ck[:, sl], tk_ref, c)
            dv_ref[...] = cv[...].astype(BF16)

    blk_of = lambda t: (jnp.minimum(t, T - 1) % nb, jnp.minimum(t, T - 1) // nb)
    cur = lambda t: blk_of(t)
    prev = lambda t: (jnp.maximum(blk_of(t)[0] - 1, 0), blk_of(t)[1])
    fin = lambda t: blk_of(jnp.maximum(t - 1, 0))
    col = _qkv_col(d, gi)
    qkv_spec = lambda kind, which: pl.BlockSpec((BLK, 512), lambda t: (which(t)[0], col(kind, which(t)[1])))
    row_spec = lambda w, which: pl.BlockSpec((BLK, w), lambda t: which(t))
    tab_spec = lambda which: pl.BlockSpec((3, BLK, 128), lambda t: (0, *which(t)))
    return _carrier_call(
        body, (qkv, qkv, qkv, qkv, qkv, datt, lse, rho, rtab, rtab),
        out_shape=(SDS((L, d * 512), BF16),) * 3, grid=(T + 1,),
        in_specs=[qkv_spec(0, cur), qkv_spec(1, cur), qkv_spec(1, prev), qkv_spec(2, cur), qkv_spec(2, prev),
                  row_spec(512, cur), row_spec(128, cur), row_spec(128, cur), tab_spec(cur), tab_spec(fin)],
        out_specs=(row_spec(512, cur), row_spec(512, fin), row_spec(512, fin)),
        scratch_shapes=[pltpu.VMEM((BLK, 512), F32), pltpu.VMEM((BLK, 512), F32)],
        sem=("arbitrary",), name=f"attn_bwd_g{gi}", exchanges=exchanges)


def _ret_bwd(proj, rn, rstd, dyrin, states, tab, consts, dproj, exchanges=()):
    S = proj.shape[0]
    nc = S // BLK
    dmask, zeta, xi, dec = consts

    def body(q_ref, k_ref, v0_ref, v1_ref, g0_ref, g1_ref, rn_ref, rs_ref, dy_ref, st_ref, tq_ref, tk_ref,
             dm_ref, z_ref, x_ref, dec_ref, dp_prev, dp_ref, dR):
        dq_ref, dk_ref = dp_ref.at[:, 0:512], dp_ref.at[:, 512:1024]
        dv_ref, dgr_ref = dp_ref.at[:, 1024:2048], dp_ref.at[:, 2048:3072]

        @pl.when(pl.program_id(0) == 0)
        def _():
            dR[...] = jnp.zeros_like(dR)

        dobs = []
        for h in range(RET_HEADS):
            vs = slice((h % 2) * 256, (h % 2 + 1) * 256)
            os_ = slice(h * 256, (h + 1) * 256)
            gr = (g0_ref if h < 2 else g1_ref)[:, vs].astype(F32)
            sg = _sigmoid(gr)
            rn_v = rn_ref[:, os_].astype(F32)
            dyi = dy_ref[:, os_].astype(F32)
            dgr_ref[:, os_] = (dyi * rn_v * sg * (1.0 + gr * (1.0 - sg))).astype(BF16)
            drn = dyi * gr * sg
            rstd = jnp.broadcast_to(rs_ref[:, 16 * h:16 * h + 1], (BLK, 256))
            do = rstd * (drn - jnp.mean(drn, axis=-1, keepdims=True) - rn_v * jnp.mean(drn * rn_v, axis=-1, keepdims=True))
            dobs.append(do.astype(BF16))
        first = []
        for h in range(RET_HEADS):
            hs = slice(h * 128, (h + 1) * 128)
            q, k = q_ref[:, hs], k_ref[:, hs]
            v = (v0_ref if h < 2 else v1_ref)[:, (h % 2) * 256:(h % 2 + 1) * 256]
            dob, dRb = dobs[h], dR[h].astype(BF16)
            kz = (k.astype(F32) * z_ref[h]).astype(BF16)
            qx = (q.astype(F32) * x_ref[h]).astype(BF16)
            first.append((q, k, _dot_nt(q, k), _dot_nt(dob, v), _dot(kz, dRb), _dot_nt(dob, st_ref[h]),
                          _dot_nt(v, dRb), _dot_tn(qx, dob)))
        masked = [((s * dm_ref[h]).astype(BF16), (dsr * dm_ref[h]).astype(BF16))
                  for h, (_, _, s, dsr, _, _, _, _) in enumerate(first)]
        for h in range(RET_HEADS):
            hs = slice(h * 128, (h + 1) * 128)
            os_ = slice(h * 256, (h + 1) * 256)
            q, k, _, _, dv_state, dq_state, dk_state, dr_new = first[h]
            sD, dS = masked[h]
            dv_ref[:, os_] = (_dot_tn(sD, dobs[h]) + dv_state).astype(BF16)
            dq = _dot(dS, k) + dq_state * x_ref[h]
            dk = _dot_tn(dS, q) + dk_state * z_ref[h]
            dR[h] = dR[h] * dec_ref[h, 0:1, :] + dr_new
            dq_ref[:, hs] = _unrot(dq, tq_ref[0], tq_ref[1], tq_ref[2], 1).astype(BF16)
            dk_ref[:, hs] = _unrot(dk, tk_ref[0], tk_ref[1], tk_ref[2], 1).astype(BF16)

    rc = lambda c: nc - 1 - c
    cst = lambda shape: pl.BlockSpec(shape, lambda c: (0, 0, 0))
    blk = lambda j: pl.BlockSpec((BLK, 512), lambda c: (rc(c), j))
    row = lambda w: pl.BlockSpec((BLK, w), lambda c: (rc(c), 0))
    (dproj,), xres = _carrier_call(
        body, (proj, proj, proj, proj, proj, proj, rn, rstd, dyrin, states, tab, tab, dmask, zeta, xi, dec, dproj),
        out_shape=(SDS((S, PROJ_W), BF16),), grid=(nc,),
        in_specs=[blk(QR_B), blk(KR_B), blk(11), blk(12), blk(13), blk(14), row(1024), row(128), row(1024),
                  pl.BlockSpec((RET_HEADS, None, BLK, 256), lambda c: (0, rc(c), 0, 0)),
                  pl.BlockSpec((None, 3, BLK, 128), lambda c: (1, 0, rc(c), 0)),
                  pl.BlockSpec((None, 3, BLK, 128), lambda c: (2, 0, rc(c), 0)),
                  cst((RET_HEADS, BLK, BLK)), cst((RET_HEADS, BLK, 128)), cst((RET_HEADS, BLK, 128)), cst((RET_HEADS, 8, 256)),
                  ANY],
        out_specs=(pl.BlockSpec((pl.Element(BLK), pl.Element(6 * COLB)), lambda c: (rc(c) * BLK, QR_B * COLB)),),
        scratch_shapes=[pltpu.VMEM((RET_HEADS, BLK, 256), F32)],
        sem=("arbitrary",), name="ret_bwd", exchanges=exchanges, in_out_aliases={16: 0})
    return dproj, xres


def _wgrad_in_half(ht, dproj, sidx, kept, exchanges=()):
    S = dproj.shape[0]
    tk = 2048
    half = (lambda sx: sx[4]) if kept else (lambda sx: 1 - sx[4])

    def body(a_ref, b_ref, o_ref):
        @pl.when(pl.program_id(1) == 0)
        def _():
            o_ref[...] = jnp.zeros_like(o_ref)

        o_ref[...] += _dot(a_ref[...], b_ref[...])

    (g,), xres = _carrier_call(
        body, (ht, dproj), out_shape=(SDS((D_MODEL // 2, PROJ_W), F32),), grid=(N_SHARD, S // tk),
        in_specs=[pl.BlockSpec((D_MODEL // 2, tk), lambda s, k, sx: (half(sx), k)),
                  pl.BlockSpec((tk, W_IN_S), lambda s, k, sx: (k, s))],
        out_specs=(pl.BlockSpec((D_MODEL // 2, W_IN_S), lambda s, k, sx: (0, s)),),
        sem=("parallel", "arbitrary"), name="wgrad_in_kept" if kept else "wgrad_in_sent", exchanges=exchanges,
        prefetch=sidx)
    return g, xres


def _in_proj_bwd(dproj, w_in, x, g1, dx1, exchanges=()):
    S = x.shape[0]
    tm = 1024

    def body(d_ref, w_ref, x_ref, g_ref, dx1_ref, dx_ref, dgn_ref, acc):
        i, s = pl.program_id(0), pl.program_id(1)

        @pl.when(s == 0)
        def _():
            acc[...] = jnp.zeros_like(acc)

        @pl.when((i == 0) & (s == 0))
        def _():
            dgn_ref[...] = jnp.zeros_like(dgn_ref)

        acc[...] += _dot_nt(d_ref[...], w_ref[...])

        @pl.when(s == N_SHARD - 1)
        def _():
            xv = x_ref[...]
            r = lax.rsqrt(jnp.mean(xv * xv, axis=-1, keepdims=True) + NORM_EPS)
            xh = xv * r
            dh = acc[...]
            dgn_ref[...] += jnp.sum(dh * xh, axis=0, keepdims=True)
            dxh = dh * g_ref[...]
            dx_ref[...] = dx1_ref[...] + r * (dxh - xh * jnp.mean(dxh * xh, axis=-1, keepdims=True))

    row = pl.BlockSpec((tm, D_MODEL), lambda i, s: (i, 0))
    vec = pl.BlockSpec((1, D_MODEL), lambda i, s: (0, 0))
    (gx, dg), xres = _carrier_call(
        body, (dproj, w_in, x, g1, dx1),
        out_shape=(SDS((S, D_MODEL), F32), SDS((1, D_MODEL), F32)), grid=(S // tm, N_SHARD),
        in_specs=[pl.BlockSpec((tm, W_IN_S), lambda i, s: (i, s)),
                  pl.BlockSpec((D_MODEL, W_IN_S), lambda i, s: (0, s)), row, vec, row],
        out_specs=(row, vec), scratch_shapes=[pltpu.VMEM((tm, D_MODEL), F32)],
        sem=("arbitrary", "arbitrary"), name="in_proj_bwd", exchanges=exchanges)
    return gx, dg, xres


def _sub_view(a, d):
    S, W = a.shape
    return a.reshape(S // d, d * W)


def _step(x, tgt, g1, g2, g3, comm):
    S = x.shape[0]
    tab_np = _tables(S)
    tab = jnp.asarray(tab_np)
    consts = _ret_consts()

    h, ht = _rms_fwd(x, g1)
    w_in = comm.w_in()
    proj, xres = _in_proj(h, w_in, tab, comm.carry("in_proj"))
    comm.took("in_proj", xres)
    qkvs, o_parts, lse_parts = [], [], []
    for gi, d in enumerate(DILATIONS):
        qkv = proj if d == 1 else _qkv_to_sub(proj, d, gi)
        (o_g, lse_g), xres = _attn_fwd(qkv, d, gi, comm.carry(f"attn_fwd_g{gi}"))
        comm.took(f"attn_fwd_g{gi}", xres)
        qkvs.append(qkv)
        o_parts.append(o_g)
        lse_parts.append(lse_g)
    att, lse_tot = _attn_merge(o_parts, lse_parts)
    (yrin, rn, rstd, states), xres = _ret_fwd(proj, consts, comm.carry("ret_fwd"))
    comm.took("ret_fwd", xres)
    wa, wr, wo = comm.weight(1), comm.weight(2), comm.weight(3)
    merged, ya, yr = _branch_merge(att, yrin, proj, wa, wr)
    (x1, h2), xres = _out_proj(merged, wo, x, g2, comm.carry("out_proj"))
    comm.took("out_proj", xres)
    wg, wu = comm.weight(4), comm.weight(5)
    (gte, up, act), xres = _ffn_up(h2, wg, wu, comm.carry("ffn_up"))
    comm.took("ffn_up", xres)
    wd = comm.weight(6)
    dx2, dx2b, dg3, loss_p = _ffn_down_loss(act, wd, x1, g3, tgt)

    dgte, dup = _ffn_down_bwd(dx2b, wd, gte, up)
    tok3 = lambda w: (lambda tk: pl.BlockSpec((None, tk, w), lambda p, k: (p, k, 0)))
    tok2 = lambda w: (lambda tk: pl.BlockSpec((tk, w), lambda p, k: (k, 0)))
    g_d = _wgrad("wgrad_down", act, dx2b, tok3(HID_S), tok2(D_MODEL), (N_SHARD, HID_S, D_MODEL),
                 pl.BlockSpec((None, HID_S, D_MODEL), lambda p, k: (p, 0, 0)), N_SHARD, S)
    g_g = _wgrad("wgrad_gate", h2, dgte, tok2(D_MODEL), tok3(HID_S), (N_SHARD, D_MODEL, HID_S),
                 pl.BlockSpec((None, D_MODEL, HID_S), lambda p, k: (p, 0, 0)), N_SHARD, S)
    g_u = _wgrad("wgrad_up", h2, dup, tok2(D_MODEL), tok3(HID_S), (N_SHARD, D_MODEL, HID_S),
                 pl.BlockSpec((None, D_MODEL, HID_S), lambda p, k: (p, 0, 0)), N_SHARD, S)
    comm.grads({4: g_g, 5: g_u, 6: g_d})
    (dx1, dx1b, dg2), xres = _ffn_up_bwd(dgte, dup, wg, wu, x1, g2, dx2, comm.carry("ffn_up_bwd"))
    comm.took("ffn_up_bwd", xres)
    dya, dyr, dproj = _out_proj_bwd(dx1b, wo, proj, ya, yr)
    colblk = lambda w: (lambda tk: pl.BlockSpec((tk, w), lambda p, k: (k, p)))
    g_o = _wgrad("wgrad_out", merged, dx1b, colblk(256), tok2(D_MODEL), (D_MODEL, D_MODEL),
                 pl.BlockSpec((256, D_MODEL), lambda p, k: (p, 0)), 4, S)
    datt, rho, dyrin = _branch_bwd(dya, dyr, wa, wr, att)
    g_a = _wgrad("wgrad_attn", att, dya, tok2(512), colblk(512), (512, D_MODEL),
                 pl.BlockSpec((512, 512), lambda p, k: (0, p)), 2, S)
    g_r = _wgrad("wgrad_ret", yrin, dyr, colblk(256), tok2(D_MODEL), (D_MODEL, D_MODEL),
                 pl.BlockSpec((256, D_MODEL), lambda p, k: (p, 0)), 4, S)
    comm.grads({1: g_a, 2: g_r.reshape(N_SHARD, 256, D_MODEL), 3: g_o.reshape(N_SHARD, 256, D_MODEL)})
    dproj, xres = _ret_bwd(proj, rn, rstd, dyrin, states, tab, consts, dproj, comm.carry("ret_bwd"))
    comm.took("ret_bwd", xres)
    dqs, dks, dvs = [], [], []
    for gi, d in enumerate(DILATIONS):
        rtab = jnp.asarray(tab_np[0].reshape(3, S // d, d * 128))
        (dq, dk, dv), xres = _attn_bwd(qkvs[gi], _sub_view(datt, d), _sub_view(lse_tot, d), _sub_view(rho, d), rtab, d, gi,
                                       comm.carry(f"attn_bwd_g{gi}"))
        comm.took(f"attn_bwd_g{gi}", xres)
        dqs.append(dq)
        dks.append(dk)
        dvs.append(dv)
    dproj = _assemble_dproj((dqs, dks, dvs), dproj)
    g_sent, xres = _wgrad_in_half(ht, dproj, comm.sidx, False, comm.carry("wgrad_in_sent"))
    comm.took("wgrad_in_sent", xres)
    comm.grads({"in_sent": g_sent})
    g_kept, xres = _wgrad_in_half(ht, dproj, comm.sidx, True, comm.carry("wgrad_in_kept"))
    comm.grads({"in_kept": g_kept})
    comm.took("wgrad_in_kept", xres)
    grad_x, dg1, xres = _in_proj_bwd(dproj, w_in, x, g1, dx1, comm.carry("in_proj_bwd"))
    comm.took("in_proj_bwd", xres)
    return loss_p, grad_x, (dg1, dg2, dg3)


W_KINDS = ("col", "col", "lead", "lead", "lead", "lead", "lead")
W_SHARD = ((1024, W_IN_S), (512, 256), (256, 1024), (256, 1024), (1024, HID_S), (1024, HID_S), (HID_S, 1024))
N_W = len(W_KINDS)


def _full_shape(wi):
    R, C = W_SHARD[wi]
    return (R, N_SHARD * C) if W_KINDS[wi] == "col" else (N_SHARD, R, C)


def _view(ref, wi, s, half):
    R, C = W_SHARD[wi]
    rows = pl.ds(half * (R // 2), R // 2)
    if W_KINDS[wi] == "col":
        return ref.at[rows, pl.ds(pl.multiple_of(s * C, 128), C)]
    return ref.at[s, rows, :]


def _mesh_pos():
    x, y, c = lax.axis_index("x"), lax.axis_index("y"), lax.axis_index("c")
    chips = [(1 - x, y), (x, 1 - y), (1 - x, 1 - y)]
    return x, y, c, chips


def _cast_bf16(a):
    R, C = a.shape
    tr = R // 2 if R % 32 == 0 else R

    def body(a_ref, o_ref):
        o_ref[...] = a_ref[...].astype(BF16)

    spec = pl.BlockSpec((tr, C), lambda i: (i, 0))
    return pl.pallas_call(body, out_shape=SDS((R, C), BF16), grid=(R // tr,), in_specs=[spec], out_specs=spec,
                          compiler_params=_cparams("parallel"), name=f"cast_{R}x{C}")(a)


def _remote(send, recv, k, src, dst, to):
    return pltpu.make_async_remote_copy(src_ref=src, dst_ref=dst, send_sem=send.at[k], recv_sem=recv.at[k],
                                        device_id=to, device_id_type=MESH)


def _gather_now(wis, shards):
    n = len(wis)

    def body(*refs):
        sh, full = refs[:n], refs[n:2 * n]
        send, recv, loc = refs[2 * n:]
        x, y, c, _ = _mesh_pos()
        s_me, sib = 2 * x + y, (x, y, 1 - c)
        xn, yn = (1 - x, y), (x, 1 - y)
        flip = lambda a, b: a + b - 2 * a * b
        via = (flip(x, 1 - c), flip(y, c))
        onto = (flip(x, c), flip(y, 1 - c))
        shard_of = lambda chip: 2 * chip[0] + chip[1]
        own, started = [], []
        for i, wi in enumerate(wis):
            Rh = W_SHARD[wi][0] // 2
            for hf in range(2):
                cp = pltpu.make_async_copy(sh[i].at[pl.ds(hf * Rh, Rh), :], _view(full[i], wi, s_me, hf), loc.at[2 * i + hf])
                cp.start()
                own.append(cp)
            for j, chip in enumerate((xn, yn)):
                cp = _remote(send, recv, 6 * i + j, sh[i].at[pl.ds(c * Rh, Rh), :], _view(full[i], wi, s_me, c), (*chip, c))
                cp.start()
                started.append(cp)

        def pass_to_sibling(i, wi, k, s):
            mine = _view(full[i], wi, s, c)
            fw = _remote(send, recv, 6 * i + k, mine, mine, sib)
            fw.start()
            started.append(fw)

        for i, wi in enumerate(wis):
            for j, chip in enumerate((xn, yn)):
                land = _view(full[i], wi, shard_of(chip), c)
                _remote(send, recv, 6 * i + j, land, land, (*chip, c)).wait_recv()
                pass_to_sibling(i, wi, 3 + j, shard_of(chip))
            relay = _view(full[i], wi, shard_of(via), c)
            fw = _remote(send, recv, 6 * i + 2, relay, relay, (*onto, c))
            fw.start()
            started.append(fw)
        s_diag = 2 * (1 - x) + (1 - y)
        for i, wi in enumerate(wis):
            land = _view(full[i], wi, s_diag, c)
            _remote(send, recv, 6 * i + 2, land, land, (*onto, c)).wait_recv()
            pass_to_sibling(i, wi, 5, s_diag)
        for i, wi in enumerate(wis):
            for k, s in ((3, shard_of(xn)), (4, shard_of(yn)), (5, s_diag)):
                land = _view(full[i], wi, s, 1 - c)
                _remote(send, recv, 6 * i + k, land, land, sib).wait_recv()
        for cp in started:
            cp.wait_send()
        for cp in own:
            cp.wait()

    return pl.pallas_call(
        body, out_shape=tuple(SDS(_full_shape(wi), BF16) for wi in wis),
        in_specs=[ANY] * n, out_specs=tuple([ANY] * n),
        scratch_shapes=[pltpu.SemaphoreType.DMA((6 * n,)), pltpu.SemaphoreType.DMA((6 * n,)),
                        pltpu.SemaphoreType.DMA((2 * n,))],
        name="gather_now")(*shards)


def _ex_gather_ici(wis, shards, then_d2d=False):
    n = len(wis)

    def build(ins, outs, send, recv, loc):
        x, y, c, chips = _mesh_pos()
        s_me, sib = 2 * x + y, (x, y, 1 - c)
        starts, waits, after = [], [], []
        for i, wi in enumerate(wis):
            Rh = W_SHARD[wi][0] // 2
            for hf in range(2):
                cp = pltpu.make_async_copy(ins[i].at[pl.ds(hf * Rh, Rh), :], _view(outs[i], wi, s_me, hf), loc.at[2 * i + hf])
                starts.append(cp)
                waits.append(cp.wait)
            for j, chip in enumerate(chips):
                cp = _remote(send, recv, 3 * i + j, ins[i].at[pl.ds(c * Rh, Rh), :], _view(outs[i], wi, s_me, c), (*chip, c))
                land = _view(outs[i], wi, 2 * chip[0] + chip[1], c)
                starts.append(cp)
                waits += [cp.wait_send, _remote(send, recv, 3 * i + j, land, land, (*chip, c)).wait_recv]
                if then_d2d:
                    theirs = _view(outs[i], wi, 2 * chip[0] + chip[1], 1 - c)
                    fw = _remote(send, recv, 3 * n + 3 * i + j, land, land, sib)
                    waits.append(fw.start)
                    after += [fw.wait_send, _remote(send, recv, 3 * n + 3 * i + j, theirs, theirs, sib).wait_recv]
        return starts, waits + after

    return _Exchange(shards, [SDS(_full_shape(wi), BF16) for wi in wis], {}, (6 if then_d2d else 3) * n, 2 * n, build)


def _ex_gather_d2d(wis, fulls):
    def build(ins, outs, send, recv, loc):
        x, y, c, chips = _mesh_pos()
        sib = (x, y, 1 - c)
        starts, waits = [], []
        for i, wi in enumerate(wis):
            for j, chip in enumerate(chips):
                mine = _view(outs[i], wi, 2 * chip[0] + chip[1], c)
                theirs = _view(outs[i], wi, 2 * chip[0] + chip[1], 1 - c)
                cp = _remote(send, recv, 3 * i + j, mine, mine, sib)
                starts.append(cp)
                waits += [cp.wait_send, _remote(send, recv, 3 * i + j, theirs, theirs, sib).wait_recv]
        return starts, waits

    return _Exchange(fulls, [SDS(f.shape, BF16) for f in fulls], {i: i for i in range(len(wis))}, 3 * len(wis), 0, build)


def _half_shape(wi):
    R, C = W_SHARD[wi]
    return (R // 2, N_SHARD * C) if W_KINDS[wi] == "col" else (N_SHARD, R // 2, C)


def _ex_pair(wis, grads):
    def build(ins, outs, send, recv, loc):
        x, y, c, _ = _mesh_pos()
        starts, waits = [], []
        for i, wi in enumerate(wis):
            Rh = W_SHARD[wi][0] // 2
            rows = pl.ds((1 - c) * Rh, Rh)
            if tuple(ins[i].shape) == _half_shape(wi):
                src = ins[i]
            else:
                src = ins[i].at[rows, :] if W_KINDS[wi] == "col" else ins[i].at[:, rows, :]
            cp = _remote(send, recv, i, src, outs[i], (x, y, 1 - c))
            starts.append(cp)
            waits.append(cp.wait)
        return starts, waits

    return _Exchange(grads, [SDS(_half_shape(wi), F32) for wi in wis], {}, len(wis), 0, build)


def _ex_chip(wis, pbs):
    def build(ins, outs, send, recv, loc):
        x, y, c, chips = _mesh_pos()
        starts, waits = [], []
        for i, wi in enumerate(wis):
            for j, chip in enumerate(chips):
                cp = _remote(send, recv, 3 * i + j, ins[i].at[j], outs[i].at[j], (*chip, c))
                starts.append(cp)
                waits.append(cp.wait)
        return starts, waits

    shapes = [SDS((3, W_SHARD[wi][0] // 2, W_SHARD[wi][1]), BF16) for wi in wis]
    return _Exchange(pbs, shapes, {}, 3 * len(wis), 0, build)


def _ex_share(wis, halves):
    def build(ins, outs, send, recv, loc):
        x, y, c, _ = _mesh_pos()
        sib = (x, y, 1 - c)
        starts, waits = [], []
        for i, wi in enumerate(wis):
            cp = _remote(send, recv, i, outs[i].at[c], outs[i].at[c], sib)
            starts.append(cp)
            waits += [cp.wait_send, _remote(send, recv, i, outs[i].at[1 - c], outs[i].at[1 - c], sib).wait_recv]
        return starts, waits

    return _Exchange(halves, [SDS(h.shape, F32) for h in halves], {i: i for i in range(len(wis))}, len(wis), 0, build)


def _row_tile(rh, C):
    best = 16
    for t in range(16, rh + 1, 16):
        if rh % t == 0 and t * C * 4 <= (3 << 19):
            best = t
    return best


def _pair_sum(wi, g, ra, sidx):
    R, C = W_SHARD[wi]
    Rh = R // 2
    tr = _row_tile(Rh, C)
    nt = Rh // tr
    off = 0 if tuple(g.shape) == _half_shape(wi) else nt
    col = W_KINDS[wi] == "col"

    def body(sidx_ref, *refs):
        gs, rs = refs[:4], refs[4:8]
        own_ref, pb_ref = refs[8:]
        own_ref[...] = gs[0][...] + rs[0][...]
        for j in range(3):
            pb_ref[j] = (gs[1 + j][...] + rs[1 + j][...]).astype(BF16)

    def gspec(slot):
        if col:
            return pl.BlockSpec((tr, C), lambda i, sx: (sx[4] * off + i, sx[slot]))
        return pl.BlockSpec((None, tr, C), lambda i, sx: (sx[slot], sx[4] * off + i, 0))

    def rspec(slot):
        if col:
            return pl.BlockSpec((tr, C), lambda i, sx: (i, sx[slot]))
        return pl.BlockSpec((None, tr, C), lambda i, sx: (sx[slot], i, 0))

    return pl.pallas_call(
        body, out_shape=(SDS((Rh, C), F32), SDS((3, Rh, C), BF16)),
        grid_spec=pltpu.PrefetchScalarGridSpec(
            num_scalar_prefetch=1, grid=(nt,),
            in_specs=[gspec(k) for k in range(4)] + [rspec(k) for k in range(4)],
            out_specs=(pl.BlockSpec((tr, C), lambda i, sx: (i, 0)), pl.BlockSpec((3, tr, C), lambda i, sx: (0, i, 0)))),
        compiler_params=_cparams("arbitrary"), name=f"pair_sum_w{wi}")(sidx, g, g, g, g, ra, ra, ra, ra)


def _chip_sum(wi, own, rb, sidx):
    R, C = W_SHARD[wi]
    Rh = R // 2
    tr = _row_tile(Rh, C)

    def body(sidx_ref, own_ref, rb_ref, o_ref):
        o_ref[...] = ((own_ref[...] + rb_ref[0].astype(F32)) + rb_ref[1].astype(F32)) + rb_ref[2].astype(F32)

    return pl.pallas_call(
        body, out_shape=SDS((2, Rh, C), F32),
        grid_spec=pltpu.PrefetchScalarGridSpec(
            num_scalar_prefetch=1, grid=(Rh // tr,),
            in_specs=[pl.BlockSpec((tr, C), lambda i, sx: (i, 0)), pl.BlockSpec((3, tr, C), lambda i, sx: (0, i, 0))],
            out_specs=pl.BlockSpec((None, tr, C), lambda i, sx: (sx[4], i, 0))),
        compiler_params=_cparams("arbitrary"), name=f"chip_sum_w{wi}")(sidx, own, rb)


def _gain_allgather(blk, ex):
    m_per, n = blk.shape
    n_in, n_out = len(ex.ins), len(ex.out_shapes)

    def body(x_ref, *rest):
        xin, out_ref, xout = rest[:n_in], rest[n_in], rest[n_in + 1:n_in + 1 + n_out]
        send_sems, recv_sems, local_sem = rest[n_in + 1 + n_out:n_in + 4 + n_out]
        ex_starts, ex_waits = ex.build(xin, xout, *rest[n_in + 4 + n_out:])
        for cp in ex_starts:
            cp.start()
        x, y, c, chips = _mesh_pos()
        me, sibling = (x, y, c), (x, y, 1 - c)

        def rows(px, py, pc):
            return out_ref.at[pl.ds((4 * px + 2 * py + pc) * m_per, m_per), :]

        def copy(k, block, to, src=None):
            return pltpu.make_async_remote_copy(
                src_ref=rows(*block) if src is None else src, dst_ref=rows(*block),
                send_sem=send_sems.at[k], recv_sem=recv_sems.at[k], device_id=to, device_id_type=MESH)

        mine = pltpu.make_async_copy(x_ref, rows(*me), local_sem)
        mine.start()
        first = [copy(0, me, sibling, src=x_ref)]
        first += [copy(1 + j, me, (*chip, c), src=x_ref) for j, chip in enumerate(chips)]
        for cp in first:
            cp.start()
        passed = [copy(4 + j, (*chip, c), sibling) for j, chip in enumerate(chips)]
        for j, chip in enumerate(chips):
            copy(1 + j, (*chip, c), me).wait_recv()
            passed[j].start()
        copy(0, sibling, me).wait_recv()
        for j, chip in enumerate(chips):
            copy(4 + j, (*chip, 1 - c), me).wait_recv()
        for cp in first + passed:
            cp.wait_send()
        mine.wait()
        for w in ex_waits:
            w()

    vm = pl.BlockSpec(memory_space=pltpu.VMEM)
    res = pl.pallas_call(
        body, out_shape=(SDS((8 * m_per, n), blk.dtype), *ex.out_shapes),
        in_specs=[vm] + [ANY] * n_in, out_specs=(vm, *[ANY] * n_out),
        input_output_aliases={1 + a: 1 + o for a, o in ex.aliases.items()},
        scratch_shapes=[pltpu.SemaphoreType.DMA((7,)), pltpu.SemaphoreType.DMA((7,)), pltpu.SemaphoreType.DMA] + ex.sems(),
        name="gain_allgather")(blk, *ex.ins)
    return res[0], tuple(res[1:])


def _adam_math(w, g, m, v):
    mn = ADAM_B1 * m + (1.0 - ADAM_B1) * g
    vn = ADAM_B2 * v + (1.0 - ADAM_B2) * (g * g)
    mh = mn / (1.0 - ADAM_B1 ** ADAM_STEP)
    vh = vn / (1.0 - ADAM_B2 ** ADAM_STEP)
    return -ADAM_LR * (mh / (jnp.sqrt(vh) + ADAM_EPS) + ADAM_WD * w), mn, vn


def _adamw(wi, w, g, m, v):
    R, C = w.shape
    tr = _row_tile(R, C)

    def body(w_ref, g_ref, m_ref, v_ref, go_ref, d_ref, mn_ref, vn_ref):
        g = g_ref[...]
        go_ref[...] = g
        d_ref[...], mn_ref[...], vn_ref[...] = _adam_math(w_ref[...], g, m_ref[...], v_ref[...])

    spec = pl.BlockSpec((tr, C), lambda i: (i, 0))
    return pl.pallas_call(body, out_shape=(SDS((R, C), F32),) * 4, grid=(R // tr,), in_specs=[spec] * 4,
                          out_specs=(spec,) * 4, compiler_params=_cparams("parallel"), name=f"adamw_w{wi}")(w, g, m, v)


def _gain_update(gathered, w, m, v):
    def body(ga_ref, w_ref, m_ref, v_ref, g_ref, d_ref, mn_ref, vn_ref):
        g = ga_ref[0:8, :]
        for dev in range(1, 8):
            g = g + ga_ref[8 * dev:8 * dev + 8, :]
        g_ref[...] = g
        d_ref[...], mn_ref[...], vn_ref[...] = _adam_math(w_ref[...], g, m_ref[...], v_ref[...])

    return pl.pallas_call(body, out_shape=(SDS((8, 1024), F32),) * 4, name="gain_update")(gathered, w, m, v)


GROUP_FFN, GROUP_MIX, GROUP_IN = (4, 5, 6), (1, 2, 3), (0,)
REST = GROUP_MIX + GROUP_FFN


class _MeshComm:
    SCHEDULE = {
        "in_proj": [("ici", (1, 2, 3, 4))],
        "ret_fwd": [("d2d", (1, 2, 3, 4)), ("ici", (5,))],
        "out_proj": [("d2d", (5,))],
        "ffn_up": [("both", (6,))],
        "ffn_up_bwd": [("pair", GROUP_FFN)],
        "ret_bwd": [("pair", GROUP_MIX), ("chip", (4,))],
        "attn_bwd_g0": [("chip", (5,))],
        "attn_bwd_g1": [("chip", (6,))],
        "attn_bwd_g2": [("chip", GROUP_MIX)],
        "wgrad_in_kept": [("pair", GROUP_IN), ("share", GROUP_FFN + GROUP_MIX)],
        "in_proj_bwd": [("chip", GROUP_IN)],
    }

    def __init__(self, shards):
        xi, yi, ci = lax.axis_index("x"), lax.axis_index("y"), lax.axis_index("c")
        self.sidx = jnp.stack([2 * xi + yi, 2 * (1 - xi) + yi, 2 * xi + (1 - yi), 2 * (1 - xi) + (1 - yi), ci]).astype(jnp.int32)
        self.shards, self.full = shards, {}
        self.g, self.own, self.pb, self.half, self.red = {}, {}, {}, {}, {}

    def w_in(self):
        return _gather_now(GROUP_IN, [self.shards[0]])[0]

    def weight(self, wi):
        return self.full[wi].reshape(D_MODEL, D_MODEL) if wi in (2, 3) else self.full[wi]

    def grads(self, by_wi):
        self.g.update(by_wi)

    def _exchange(self, stage, wis):
        pick = lambda table: [table[wi] for wi in wis]
        if stage == "ici":
            return _ex_gather_ici(wis, pick(self.shards))
        if stage == "both":
            return _ex_gather_ici(wis, pick(self.shards), then_d2d=True)
        if stage == "d2d":
            return _ex_gather_d2d(wis, pick(self.full))
        if stage == "pair":
            return _ex_pair(wis, [self.g["in_sent"] if wi == 0 else self.g[wi] for wi in wis])
        if stage == "chip":
            return _ex_chip(wis, pick(self.pb))
        return _ex_share(wis, pick(self.half))

    def _landed(self, stage, wis, res):
        for wi, r in zip(wis, res):
            if stage in ("ici", "d2d", "both"):
                self.full[wi] = r
            elif stage == "pair":
                self.own[wi], self.pb[wi] = _pair_sum(wi, self.g["in_kept"] if wi == 0 else self.g[wi], r, self.sidx)
            elif stage == "chip":
                self.half[wi] = _chip_sum(wi, self.own[wi], r, self.sidx)
            else:
                self.red[wi] = r

    def carry(self, point):
        return [self._exchange(stage, wis) for stage, wis in self.SCHEDULE.get(point, ())]

    def took(self, point, xres):
        for (stage, wis), res in zip(self.SCHEDULE.get(point, ()), xres):
            self._landed(stage, wis, res)

    def last_share(self):
        return self._exchange("share", GROUP_IN)

    def reduced(self, last_shared):
        self._landed("share", GROUP_IN, last_shared)
        return [self.red[wi] for wi in range(N_W)]


def kernel(x, norm_mix_g, w_in, w_out_attn, w_out_ret, w_out, norm_ffn_g, w_ffn_gate, w_ffn_up, w_ffn_down, norm_final_g, loss_target, m_norm_mix_g, m_w_in, m_w_out_attn, m_w_out_ret, m_w_out, m_norm_ffn_g, m_w_ffn_gate, m_w_ffn_up, m_w_ffn_down, m_norm_final_g, v_norm_mix_g, v_w_in, v_w_out_attn, v_w_out_ret, v_w_out, v_norm_ffn_g, v_w_ffn_gate, v_w_ffn_up, v_w_ffn_down, v_norm_final_g):
    ws = (w_in, w_out_attn, w_out_ret, w_out, w_ffn_gate, w_ffn_up, w_ffn_down)
    ms = (m_w_in, m_w_out_attn, m_w_out_ret, m_w_out, m_w_ffn_gate, m_w_ffn_up, m_w_ffn_down)
    vs = (v_w_in, v_w_out_attn, v_w_out_ret, v_w_out, v_w_ffn_gate, v_w_ffn_up, v_w_ffn_down)
    shard2d = lambda a, wi: a.reshape(W_SHARD[wi])

    comm = _MeshComm([_cast_bf16(shard2d(w, wi)) for wi, w in enumerate(ws)])
    g3 = norm_final_g.reshape(1, D_MODEL)
    loss_p, grad_x, gain_g = _step(x[0], loss_target[0], norm_mix_g, norm_ffn_g, g3, comm)

    pad8 = lambda rows: jnp.concatenate([r.reshape(1, D_MODEL) for r in rows]
                                        + [jnp.zeros((8 - len(rows), D_MODEL), F32)], axis=0)
    gathered, shared = _gain_allgather(pad8((*gain_g, jnp.tile(loss_p[0:1], (1, D_MODEL // 128)))), comm.last_share())
    gred = comm.reduced(shared)

    outs_g, outs_d, outs_m, outs_v = [], [], [], []
    for wi in range(N_W):
        g2d = gred[wi].reshape(W_SHARD[wi])
        gout, dlt, mn, vn = _adamw(wi, shard2d(ws[wi], wi), g2d, shard2d(ms[wi], wi), shard2d(vs[wi], wi))
        for lst, a in ((outs_g, gout), (outs_d, dlt), (outs_m, mn), (outs_v, vn)):
            lst.append(a.reshape(ws[wi].shape))

    gg, gd, gm, gv = _gain_update(gathered, pad8((norm_mix_g, norm_ffn_g, norm_final_g)),
                                  pad8((m_norm_mix_g, m_norm_ffn_g, m_norm_final_g)),
                                  pad8((v_norm_mix_g, v_norm_ffn_g, v_norm_final_g)))
    loss = gg[3, 0]

    def assemble(gain_rows, wlist):
        return (gain_rows[0:1], wlist[0], wlist[1], wlist[2], wlist[3], gain_rows[1:2],
                wlist[4], wlist[5], wlist[6], gain_rows[2])

    return (loss, grad_x[None], *assemble(gg, outs_g), *assemble(gd, outs_d), *assemble(gm, outs_m), *assemble(gv, outs_v))
```

```python
import functools
import math

import numpy as np
import jax
import jax.numpy as jnp
from jax import lax
from jax.experimental import pallas as pl
from jax.experimental.pallas import tpu as pltpu

F32, BF16 = jnp.float32, jnp.bfloat16
SDS = jax.ShapeDtypeStruct
MESH = pl.DeviceIdType.MESH

D_MODEL = 1024
PROJ_W = 9728
COLB = 512
N_COLB = PROJ_W // COLB
QA_B, KA_B, VA_B = 0, 3, 6
QR_B, KR_B = 9, 10
FFN_HID = 2816
N_SHARD = 4
HID_S = FFN_HID // N_SHARD
W_IN_S = PROJ_W // N_SHARD
DILATIONS = (1, 4, 16)
BLK = 128
RET_HEADS = 4
ROPE_THETA = 10000.0
NORM_EPS = 1e-6
ADAM_LR, ADAM_B1, ADAM_B2, ADAM_EPS, ADAM_WD, ADAM_STEP = 0.001, 0.9, 0.999, 1e-08, 0.01, 10
VMEM_LIMIT = 56 << 20


def _cparams(*sem):
    return pltpu.CompilerParams(dimension_semantics=sem or None, vmem_limit_bytes=VMEM_LIMIT)


def _dot(a, b):
    return jnp.dot(a, b, preferred_element_type=F32)


def _dot_nt(a, b):
    return lax.dot_general(a, b, (((1,), (1,)), ((), ())), preferred_element_type=F32)


def _dot_tn(a, b):
    return lax.dot_general(a, b, (((0,), (0,)), ((), ())), preferred_element_type=F32)


def _row_pieces(tm, sub=512):
    return [slice(i, i + sub) for i in range(0, tm, sub)]


def _sigmoid(z):
    return 0.5 * jnp.tanh(0.5 * z) + 0.5


ANY = pl.BlockSpec(memory_space=pl.ANY)


class _Exchange:
    def __init__(self, ins, out_shapes, aliases, n_sem, n_loc, build):
        self.ins, self.out_shapes, self.aliases = list(ins), list(out_shapes), dict(aliases)
        self.n_sem, self.n_loc, self.build = n_sem, n_loc, build

    def sems(self):
        return [pltpu.SemaphoreType.DMA((self.n_sem,)), pltpu.SemaphoreType.DMA((self.n_sem,)),
                pltpu.SemaphoreType.DMA((max(self.n_loc, 1),))]


def _carrier_call(body, args, *, out_shape, grid, in_specs, out_specs, scratch_shapes=(), sem, name, exchanges=(),
                  prefetch=None, in_out_aliases=None):
    out_shape, out_specs = tuple(out_shape), tuple(out_specs)
    n_in, n_out, n_scr = len(args), len(out_shape), len(scratch_shapes)
    n_pre = 0 if prefetch is None else 1
    x_args, x_outs, x_scr, spans = [], [], [], []
    aliases = {n_pre + a: o for a, o in (in_out_aliases or {}).items()}
    for ex in exchanges:
        i0, o0 = len(x_args), len(x_outs)
        for a, o in ex.aliases.items():
            aliases[n_pre + n_in + i0 + a] = n_out + o0 + o
        x_args += ex.ins
        x_outs += ex.out_shapes
        x_scr += ex.sems()
        spans.append((i0, len(ex.ins), o0, len(ex.out_shapes)))
    nx_in, nx_out = len(x_args), len(x_outs)

    def wrapped(*refs):
        refs = refs[n_pre:]
        ins, xin = refs[:n_in], refs[n_in:n_in + nx_in]
        o_base = n_in + nx_in
        outs, xout = refs[o_base:o_base + n_out], refs[o_base + n_out:o_base + n_out + nx_out]
        s_base = o_base + n_out + nx_out
        scr, xs = refs[s_base:s_base + n_scr], refs[s_base + n_scr:]

        def built(e):
            i0, ni, o0, no = spans[e]
            return exchanges[e].build(xin[i0:i0 + ni], xout[o0:o0 + no], *xs[3 * e:3 * e + 3])

        if exchanges:
            first = functools.reduce(jnp.logical_and, [pl.program_id(k) == 0 for k in range(len(grid))])
            last = functools.reduce(jnp.logical_and, [pl.program_id(k) == grid[k] - 1 for k in range(len(grid))])

            @pl.when(first)
            def _():
                for e in range(len(exchanges)):
                    for cp in built(e)[0]:
                        cp.start()

        body(*ins, *outs, *scr)

        if exchanges:
            @pl.when(last)
            def _():
                for e in range(len(exchanges)):
                    for w in built(e)[1]:
                        w()

    all_in, all_out = list(in_specs) + [ANY] * nx_in, out_specs + tuple([ANY] * nx_out)
    all_scr = list(scratch_shapes) + x_scr
    cparams = _cparams(*(sem if not exchanges else ("arbitrary",) * len(grid)))
    if prefetch is None:
        res = pl.pallas_call(wrapped, out_shape=out_shape + tuple(x_outs), grid=grid, in_specs=all_in, out_specs=all_out,
                             scratch_shapes=all_scr, input_output_aliases=aliases, compiler_params=cparams,
                             name=name)(*args, *x_args)
    else:
        gs = pltpu.PrefetchScalarGridSpec(num_scalar_prefetch=1, grid=grid, in_specs=all_in, out_specs=all_out,
                                          scratch_shapes=all_scr)
        res = pl.pallas_call(wrapped, out_shape=out_shape + tuple(x_outs), grid_spec=gs, input_output_aliases=aliases,
                             compiler_params=cparams, name=name)(prefetch, *args, *x_args)
    xres = [tuple(res[n_out + o0:n_out + o0 + no]) for (_, _, o0, no) in spans]
    return tuple(res[:n_out]), xres


def _tables(S):
    f32 = np.float32
    pos = np.arange(S, dtype=f32)
    lane = np.arange(128)
    inv = (f32(ROPE_THETA) ** (-np.arange(0, 64, 2, dtype=f32) / f32(64))).astype(f32)
    ang = (pos[:, None] * inv[None, :]).astype(np.float64)
    idx = (lane % 64) % 32
    c, s = np.cos(ang)[:, idx], np.sin(ang)[:, idx]
    first = ((lane % 64) < 32)[None, :]
    rope = np.stack([c, np.where(first, 0.0, s), np.where(first, -s, 0.0)])
    base = (f32(1.0) / (f32(ROPE_THETA) ** np.linspace(0.0, 1.0, 64, dtype=f32))).astype(f32)
    ang2 = (pos[:, None] * base[None, :]).astype(np.float64)
    c2, s2 = np.cos(ang2)[:, lane // 2], np.sin(ang2)[:, lane // 2]
    even = (lane % 2 == 0)[None, :]
    th = np.stack([c2, np.where(even, 0.0, s2), np.where(even, -s2, 0.0)])
    return np.stack([rope, th, th * (128 ** -0.5)]).astype(f32)


def _rot(a, c, sa, sb, shift):
    return a * c + pltpu.roll(a, shift, 1) * sa + pltpu.roll(a, 128 - shift, 1) * sb


def _unrot(g, c, sa, sb, shift):
    return g * c + pltpu.roll(g * sa, 128 - shift, 1) + pltpu.roll(g * sb, shift, 1)


def _ret_consts():
    h = np.arange(RET_HEADS, dtype=np.float64)
    log_g = np.log1p(-(2.0 ** (-5.0 - h)))
    idx = np.arange(BLK, dtype=np.float64)
    diff = idx[:, None] - idx[None, :]
    dmask = np.where(diff[None] >= 0, np.exp(np.maximum(diff, 0.0)[None] * log_g[:, None, None]), 0.0)
    zeta = np.exp((BLK - 1 - idx)[None, :] * log_g[:, None])
    xi = np.exp((idx + 1.0)[None, :] * log_g[:, None])
    dec = np.exp(BLK * log_g)
    rep = lambda v: np.broadcast_to(v[:, :, None], (RET_HEADS, BLK, 128))
    return (jnp.asarray(dmask, F32), jnp.asarray(rep(zeta), F32), jnp.asarray(rep(xi), F32),
            jnp.asarray(np.broadcast_to(dec[:, None, None], (RET_HEADS, 8, 256)), F32))


def _rms_fwd(x, g):
    S = x.shape[0]
    tm = 512

    def body(x_ref, g_ref, h_ref, ht_ref):
        xv = x_ref[...]
        r = lax.rsqrt(jnp.mean(xv * xv, axis=-1, keepdims=True) + NORM_EPS)
        h = xv * r * g_ref[...]
        h_ref[...] = h.astype(BF16)
        ht_ref[...] = h.T.astype(BF16)

    return pl.pallas_call(
        body, out_shape=(SDS((S, D_MODEL), BF16), SDS((D_MODEL, S), BF16)), grid=(S // tm,),
        in_specs=[pl.BlockSpec((tm, D_MODEL), lambda i: (i, 0)), pl.BlockSpec((1, D_MODEL), lambda i: (0, 0))],
        out_specs=(pl.BlockSpec((tm, D_MODEL), lambda i: (i, 0)), pl.BlockSpec((D_MODEL, tm), lambda i: (0, i))),
        compiler_params=_cparams("parallel"), name="rms_fwd")(x, g)


def _in_proj(h, w_in, tab, exchanges=()):
    S = h.shape[0]
    tm = min(S, 4096)

    def body(h_ref, w_ref, t_ref, o_ref):
        j = pl.program_id(1)
        is_rope = j < 6
        is_theta = (j == QR_B) | (j == KR_B)
        sub = 512

        def rotated(shift):
            for i in range(tm // sub):
                rows = slice(i * sub, (i + 1) * sub)
                acc = _dot(h_ref[rows, :], w_ref[...])
                c, sa, sb = t_ref[0, 0, rows, :], t_ref[0, 1, rows, :], t_ref[0, 2, rows, :]
                for k in range(COLB // 128):
                    sl = slice(k * 128, (k + 1) * 128)
                    o_ref[rows, sl] = _rot(acc[:, sl], c, sa, sb, shift).astype(BF16)

        @pl.when(is_rope)
        def _():
            rotated(32)

        @pl.when(is_theta)
        def _():
            rotated(1)

        @pl.when(jnp.logical_not(is_rope | is_theta))
        def _():
            o_ref[...] = _dot(h_ref[...], w_ref[...]).astype(BF16)

    def tab_map(i, j):
        return (jnp.where(j == QR_B, 1, jnp.where(j == KR_B, 2, 0)), 0, i, 0)

    (proj,), xres = _carrier_call(
        body, (h, w_in, tab), out_shape=(SDS((S, PROJ_W), BF16),), grid=(S // tm, N_COLB),
        in_specs=[pl.BlockSpec((tm, D_MODEL), lambda i, j: (i, 0)),
                  pl.BlockSpec((D_MODEL, COLB), lambda i, j: (0, j)),
                  pl.BlockSpec((1, 3, tm, 128), tab_map)],
        out_specs=(pl.BlockSpec((tm, COLB), lambda i, j: (i, j)),),
        sem=("parallel", "arbitrary"), name="in_proj", exchanges=exchanges)
    return proj, xres


def _band_mask(n):
    qi = lax.broadcasted_iota(jnp.int32, (BLK, 2 * BLK), 0)
    kj = lax.broadcasted_iota(jnp.int32, (BLK, 2 * BLK), 1)
    dist = BLK + qi - kj
    return (dist >= 0) & (dist <= BLK) & ((kj >= BLK) | (n > 0))


def _qkv_col(d, gi):
    if d == 1:
        return lambda t, r: 3 * t + gi
    return lambda t, r: 3 * r + t


def _attn_fwd(qkv, d, gi, exchanges=()):
    L = qkv.shape[0]
    nb = L // BLK

    def body(q_ref, kc_ref, kp_ref, vc_ref, vp_ref, o_ref, lse_ref):
        n = pl.program_id(1)
        mask = _band_mask(n)
        mask2 = jnp.concatenate([mask, mask], axis=0)
        lane = lax.broadcasted_iota(jnp.int32, (BLK, 128), 1)
        lo = lane < 64
        lse_all = jnp.zeros((BLK, 128), F32)
        chunks = [slice(c * 128, (c + 1) * 128) for c in range(4)]
        scores, vals = [], []
        for sl in chunks:
            q = q_ref[:, sl]
            k = jnp.concatenate([kp_ref[:, sl], kc_ref[:, sl]], axis=0)
            vals.append(jnp.concatenate([vp_ref[:, sl], vc_ref[:, sl]], axis=0))
            q2 = jnp.concatenate([jnp.where(lo, q, jnp.zeros_like(q)), jnp.where(lo, jnp.zeros_like(q), q)], axis=0)
            scores.append(_dot_nt(q2, k))
        probs = []
        for c, s in enumerate(scores):
            s = jnp.where(mask2, s * 0.125, jnp.float32(-1e30))
            m = jnp.max(s, axis=-1, keepdims=True)
            p = jnp.exp(s - m)
            l = jnp.sum(p, axis=-1, keepdims=True)
            probs.append((p / l).astype(BF16))
            lse = m + jnp.log(l)
            lse_all = jnp.where(lane // 16 == 2 * c, lse[:BLK], jnp.where(lane // 16 == 2 * c + 1, lse[BLK:], lse_all))
        for sl, p, v in zip(chunks, probs, vals):
            o2 = _dot(p, v)
            o_ref[:, sl] = jnp.where(lo, o2[:BLK], o2[BLK:])
        lse_ref[...] = lse_all

    prev = lambda n: jnp.maximum(n - 1, 0)
    col = _qkv_col(d, gi)
    return _carrier_call(
        body, (qkv,) * 5, out_shape=(SDS((L, d * 512), F32), SDS((L, d * 128), F32)), grid=(d, nb),
        in_specs=[pl.BlockSpec((BLK, 512), lambda r, n: (n, col(0, r))),
                  pl.BlockSpec((BLK, 512), lambda r, n: (n, col(1, r))),
                  pl.BlockSpec((BLK, 512), lambda r, n: (prev(n), col(1, r))),
                  pl.BlockSpec((BLK, 512), lambda r, n: (n, col(2, r))),
                  pl.BlockSpec((BLK, 512), lambda r, n: (prev(n), col(2, r)))],
        out_specs=(pl.BlockSpec((BLK, 512), lambda r, n: (n, r)),
                   pl.BlockSpec((BLK, 128), lambda r, n: (n, r))),
        sem=("parallel", "arbitrary"), name=f"attn_fwd_g{gi}", exchanges=exchanges)


def _qkv_to_sub(proj, d, gi):
    S = proj.shape[0]
    tm = 512
    n = tm // d

    def body(q_ref, k_ref, v_ref, o_ref, scr):
        for t, ref in enumerate((q_ref, k_ref, v_ref)):
            for c in range(4):
                scr[c] = ref[:, c * 128:(c + 1) * 128].astype(F32)
            for r in range(d):
                for c in range(4):
                    col = (3 * r + t) * 512 + c * 128
                    o_ref[:, col:col + 128] = scr[c, pl.ds(r, n, stride=d), :].astype(BF16)

    return pl.pallas_call(
        body, out_shape=SDS((S // d, d * 1536), BF16), grid=(S // tm,),
        in_specs=[pl.BlockSpec((tm, 512), lambda i, b=b: (i, b + gi)) for b in (QA_B, KA_B, VA_B)],
        out_specs=pl.BlockSpec((n, d * 1536), lambda i: (i, 0)),
        scratch_shapes=[pltpu.VMEM((4, tm, 128), F32)],
        compiler_params=_cparams("parallel"), name=f"qkv_to_sub_g{gi}")(proj, proj, proj)


def _attn_merge(os_, lses):
    S = os_[0].shape[0]
    tm = 512

    def body(o0, o1, o2, l0, l1, l2, att_ref, lt_ref, so1, so2, sl1, sl2):
        lo = lax.broadcasted_iota(jnp.int32, (tm, 128), 1) < 64

        def natural(ref, d, scr, width):
            nch = width // 128
            if d == 1:
                return [ref[:, c * 128:(c + 1) * 128] for c in range(nch)]
            for r in range(d):
                for c in range(nch):
                    scr[c, pl.ds(r, tm // d, stride=d), :] = ref[:, r * width + c * 128:r * width + (c + 1) * 128]
            return [scr[c] for c in range(nch)]

        ls = [natural(l, d, s, 128)[0] for l, d, s in zip((l0, l1, l2), DILATIONS, (None, sl1, sl2))]
        m = jnp.maximum(jnp.maximum(ls[0], ls[1]), ls[2])
        es = [jnp.exp(v - m) for v in ls]
        z = es[0] + es[1] + es[2]
        lt_ref[...] = m + jnp.log(z)
        ws = [e / z for e in es]
        o_nat = [natural(o, d, s, 512) for o, d, s in zip((o0, o1, o2), DILATIONS, (None, so1, so2))]
        for c in range(4):
            acc = jnp.zeros((tm, 128), F32)
            for g in range(3):
                w_lo = jnp.broadcast_to(ws[g][:, 32 * c:32 * c + 1], (tm, 128))
                w_hi = jnp.broadcast_to(ws[g][:, 32 * c + 16:32 * c + 17], (tm, 128))
                acc = acc + jnp.where(lo, w_lo, w_hi) * o_nat[g][c]
            att_ref[:, c * 128:(c + 1) * 128] = acc.astype(BF16)

    sub = lambda w: [pl.BlockSpec((tm // d, d * w), lambda i: (i, 0)) for d in DILATIONS]
    return pl.pallas_call(
        body, out_shape=(SDS((S, 512), BF16), SDS((S, 128), F32)), grid=(S // tm,),
        in_specs=sub(512) + sub(128),
        out_specs=(pl.BlockSpec((tm, 512), lambda i: (i, 0)), pl.BlockSpec((tm, 128), lambda i: (i, 0))),
        scratch_shapes=[pltpu.VMEM((4, tm, 128), F32), pltpu.VMEM((4, tm, 128), F32),
                        pltpu.VMEM((1, tm, 128), F32), pltpu.VMEM((1, tm, 128), F32)],
        compiler_params=_cparams("parallel"), name="attn_merge")(*os_, *lses)


def _assemble_dproj(att_grads, dproj):
    S = dproj.shape[0]
    tm = 256

    def body(*refs):
        a = [refs[3 * t:3 * t + 3] for t in range(3)]
        dp_prev, o_ref, scr = refs[9:]
        for t in range(3):
            for g, d in enumerate(DILATIONS):
                base = (3 * t + g) * COLB
                if d == 1:
                    o_ref[:, base:base + COLB] = a[t][g][...]
                    continue
                for c in range(4):
                    for r in range(d):
                        scr[c, pl.ds(r, tm // d, stride=d), :] = a[t][g][:, r * 512 + c * 128:r * 512 + (c + 1) * 128].astype(F32)
                    o_ref[:, base + c * 128:base + (c + 1) * 128] = scr[c].astype(BF16)

    sub = [pl.BlockSpec((tm // d, d * 512), lambda i: (i, 0)) for d in DILATIONS]
    flat = [att_grads[t][g] for t in range(3) for g in range(3)]
    return pl.pallas_call(
        body, out_shape=SDS((S, PROJ_W), BF16), grid=(S // tm,),
        in_specs=sub * 3 + [ANY], out_specs=pl.BlockSpec((tm, 9 * COLB), lambda i: (i, 0)),
        scratch_shapes=[pltpu.VMEM((4, tm, 128), F32)], input_output_aliases={9: 0},
        compiler_params=_cparams("parallel"), name="assemble_dproj")(*flat, dproj)


def _ret_fwd(proj, consts, exchanges=()):
    S = proj.shape[0]
    nc = S // BLK
    dmask, zeta, xi, dec = consts

    def body(q_ref, k_ref, v0_ref, v1_ref, g0_ref, g1_ref, dm_ref, z_ref, x_ref, dec_ref,
             y_ref, rn_ref, rs_ref, st_ref, R):
        @pl.when(pl.program_id(0) == 0)
        def _():
            R[...] = jnp.zeros_like(R)

        lane16 = lax.broadcasted_iota(jnp.int32, (BLK, 128), 1) // 16
        rs_all = jnp.zeros((BLK, 128), F32)
        first = []
        for h in range(RET_HEADS):
            hs = slice(h * 128, (h + 1) * 128)
            q, k = q_ref[:, hs], k_ref[:, hs]
            v = (v0_ref if h < 2 else v1_ref)[:, (h % 2) * 256:(h % 2 + 1) * 256]
            Rb = R[h].astype(BF16)
            st_ref[h] = Rb
            kz = (k.astype(F32) * z_ref[h]).astype(BF16)
            first.append((v, _dot_nt(q, k), _dot((q.astype(F32) * x_ref[h]).astype(BF16), Rb), _dot_tn(kz, v)))
        masked = [(s * dm_ref[h]).astype(BF16) for h, (_, s, _, _) in enumerate(first)]
        for h in range(RET_HEADS):
            vs = slice((h % 2) * 256, (h % 2 + 1) * 256)
            os_ = slice(h * 256, (h + 1) * 256)
            v, _, cross, kv = first[h]
            o = _dot(masked[h], v) + cross
            R[h] = R[h] * dec_ref[h, 0:1, :] + kv
            mu = jnp.mean(o, axis=-1, keepdims=True)
            oc = o - mu
            rstd = lax.rsqrt(jnp.mean(oc * oc, axis=-1, keepdims=True) + NORM_EPS)
            rn = oc * rstd
            gr = (g0_ref if h < 2 else g1_ref)[:, vs].astype(F32)
            y_ref[:, os_] = (rn * gr * _sigmoid(gr)).astype(BF16)
            rn_ref[:, os_] = rn.astype(BF16)
            rs_all = jnp.where(lane16 == h, rstd, rs_all)
        rs_ref[...] = rs_all

    cst = lambda shape: pl.BlockSpec(shape, lambda c: (0, 0, 0))
    blk = lambda j: pl.BlockSpec((BLK, 512), lambda c: (c, j))
    return _carrier_call(
        body, (proj, proj, proj, proj, proj, proj, dmask, zeta, xi, dec),
        out_shape=(SDS((S, 1024), BF16), SDS((S, 1024), BF16), SDS((S, 128), F32), SDS((RET_HEADS, nc, BLK, 256), BF16)),
        grid=(nc,),
        in_specs=[blk(QR_B), blk(KR_B), blk(11), blk(12), blk(13), blk(14),
                  cst((RET_HEADS, BLK, BLK)), cst((RET_HEADS, BLK, 128)), cst((RET_HEADS, BLK, 128)), cst((RET_HEADS, 8, 256))],
        out_specs=(pl.BlockSpec((BLK, 1024), lambda c: (c, 0)), pl.BlockSpec((BLK, 1024), lambda c: (c, 0)),
                   pl.BlockSpec((BLK, 128), lambda c: (c, 0)),
                   pl.BlockSpec((RET_HEADS, None, BLK, 256), lambda c: (0, c, 0, 0))),
        scratch_shapes=[pltpu.VMEM((RET_HEADS, BLK, 256), F32)],
        sem=("arbitrary",), name="ret_fwd", exchanges=exchanges)


def _branch_merge(att, yrin, proj, wa, wr):
    S = att.shape[0]
    tm = min(S, 2048)

    def body(a_ref, y_ref, ga_ref, gr_ref, wa_ref, wr_ref, m_ref, ya_ref, yr_ref):
        for rows in _row_pieces(tm):
            ya = _dot(a_ref[rows, :], wa_ref[...])
            yr = _dot(y_ref[rows, :], wr_ref[...])
            m_ref[rows, :] = (_sigmoid(ga_ref[rows, :].astype(F32)) * ya
                              + _sigmoid(gr_ref[rows, :].astype(F32)) * yr).astype(BF16)
            ya_ref[rows, :] = ya.astype(BF16)
            yr_ref[rows, :] = yr.astype(BF16)

    ospec = pl.BlockSpec((tm, 512), lambda i, j: (i, j))
    return pl.pallas_call(
        body, out_shape=(SDS((S, D_MODEL), BF16),) * 3, grid=(S // tm, 2),
        in_specs=[pl.BlockSpec((tm, 512), lambda i, j: (i, 0)), pl.BlockSpec((tm, 1024), lambda i, j: (i, 0)),
                  pl.BlockSpec((tm, 512), lambda i, j: (i, 15 + j)), pl.BlockSpec((tm, 512), lambda i, j: (i, 17 + j)),
                  pl.BlockSpec((512, 512), lambda i, j: (0, j)), pl.BlockSpec((1024, 512), lambda i, j: (0, j))],
        out_specs=(ospec, ospec, ospec),
        compiler_params=_cparams("parallel", "arbitrary"), name="branch_merge")(att, yrin, proj, proj, wa, wr)


def _out_proj(merged, wo, x, g2, exchanges=()):
    S = x.shape[0]
    tm = 1024

    def body(m_ref, w_ref, x_ref, g_ref, x1_ref, h2_ref):
        x1 = x_ref[...] + _dot(m_ref[...], w_ref[...])
        x1_ref[...] = x1
        r = lax.rsqrt(jnp.mean(x1 * x1, axis=-1, keepdims=True) + NORM_EPS)
        h2_ref[...] = (x1 * r * g_ref[...]).astype(BF16)

    row = pl.BlockSpec((tm, D_MODEL), lambda i: (i, 0))
    return _carrier_call(
        body, (merged, wo, x, g2), out_shape=(SDS((S, D_MODEL), F32), SDS((S, D_MODEL), BF16)), grid=(S // tm,),
        in_specs=[row, pl.BlockSpec((D_MODEL, D_MODEL), lambda i: (0, 0)), row, pl.BlockSpec((1, D_MODEL), lambda i: (0, 0))],
        out_specs=(row, row), sem=("parallel",), name="out_proj", exchanges=exchanges)


def _ffn_up(h2, wg, wu, exchanges=()):
    S = h2.shape[0]
    tm = min(S, 2048)

    def body(h_ref, wg_ref, wu_ref, g_ref, u_ref, a_ref):
        for rows in _row_pieces(tm):
            hv = h_ref[rows, :]
            g = _dot(hv, wg_ref[...])
            u = _dot(hv, wu_ref[...])
            g_ref[rows, :] = g.astype(BF16)
            u_ref[rows, :] = u.astype(BF16)
            a_ref[rows, :] = (g * _sigmoid(g) * u).astype(BF16)

    wspec = pl.BlockSpec((None, D_MODEL, HID_S), lambda i, s: (s, 0, 0))
    ospec = pl.BlockSpec((None, tm, HID_S), lambda i, s: (s, i, 0))
    return _carrier_call(
        body, (h2, wg, wu), out_shape=(SDS((N_SHARD, S, HID_S), BF16),) * 3, grid=(S // tm, N_SHARD),
        in_specs=[pl.BlockSpec((tm, D_MODEL), lambda i, s: (i, 0)), wspec, wspec],
        out_specs=(ospec, ospec, ospec),
        sem=("parallel", "arbitrary"), name="ffn_up", exchanges=exchanges)


def _ffn_down_loss(act, wd, x1, g3, tgt):
    S = x1.shape[0]
    tm = 512

    def body(a_ref, w_ref, x_ref, g_ref, t_ref, dx_ref, dxb_ref, dg_ref, ls_ref):
        @pl.when(pl.program_id(0) == 0)
        def _():
            dg_ref[...] = jnp.zeros_like(dg_ref)
            ls_ref[...] = jnp.zeros_like(ls_ref)

        g = g_ref[...]
        for rows in _row_pieces(tm, 256):
            y = _dot(a_ref[0, rows, :], w_ref[0])
            for s in range(1, N_SHARD):
                y = y + _dot(a_ref[s, rows, :], w_ref[s])
            x2 = x_ref[rows, :] + y
            r = lax.rsqrt(jnp.mean(x2 * x2, axis=-1, keepdims=True) + NORM_EPS)
            xh = x2 * r
            err = xh * g - t_ref[rows, :]
            ls_ref[...] += jnp.sum(jnp.sum(err * err, axis=-1, keepdims=True), axis=0, keepdims=True) * (0.5 / D_MODEL)
            dy = err * (1.0 / D_MODEL)
            dg_ref[...] += jnp.sum(dy * xh, axis=0, keepdims=True)
            dxh = dy * g
            dx = r * (dxh - xh * jnp.mean(dxh * xh, axis=-1, keepdims=True))
            dx_ref[rows, :] = dx
            dxb_ref[rows, :] = dx.astype(BF16)

    row = pl.BlockSpec((tm, D_MODEL), lambda i: (i, 0))
    vec = pl.BlockSpec((1, D_MODEL), lambda i: (0, 0))
    return pl.pallas_call(
        body, out_shape=(SDS((S, D_MODEL), F32), SDS((S, D_MODEL), BF16), SDS((1, D_MODEL), F32), SDS((8, 128), F32)),
        grid=(S // tm,),
        in_specs=[pl.BlockSpec((N_SHARD, tm, HID_S), lambda i: (0, i, 0)),
                  pl.BlockSpec((N_SHARD, HID_S, D_MODEL), lambda i: (0, 0, 0), pipeline_mode=pl.Buffered(1)),
                  row, vec, row],
        out_specs=(row, row, vec, pl.BlockSpec((8, 128), lambda i: (0, 0))),
        compiler_params=_cparams("arbitrary"), name="ffn_down_loss")(act, wd, x1, g3, tgt)


def _ffn_bwd(dx2b, dx2, wd, wg, wu, gte, up, x1, g2):
    S = x1.shape[0]
    tm = 256

    def body(d_ref, dx2_ref, wd_ref, wg_ref, wu_ref, g_ref, u_ref, x_ref, gn_ref,
             dg_ref, du_ref, dx_ref, dxb_ref, dgn_ref):
        @pl.when(pl.program_id(0) == 0)
        def _():
            dgn_ref[...] = jnp.zeros_like(dgn_ref)

        d = d_ref[...]
        dacts = [_dot_nt(d, wd_ref[s]) for s in range(N_SHARD)]
        dgs, dus = [], []
        for s, da in enumerate(dacts):
            g = g_ref[s].astype(F32)
            sg = _sigmoid(g)
            dgs.append((da * u_ref[s].astype(F32) * sg * (1.0 + g * (1.0 - sg))).astype(BF16))
            dus.append((da * g * sg).astype(BF16))
            dg_ref[s] = dgs[s]
            du_ref[s] = dus[s]
        dh = _dot_nt(dgs[0], wg_ref[0]) + _dot_nt(dus[0], wu_ref[0])
        for s in range(1, N_SHARD):
            dh = dh + _dot_nt(dgs[s], wg_ref[s]) + _dot_nt(dus[s], wu_ref[s])
        xv = x_ref[...]
        r = lax.rsqrt(jnp.mean(xv * xv, axis=-1, keepdims=True) + NORM_EPS)
        xh = xv * r
        dgn_ref[...] += jnp.sum(dh * xh, axis=0, keepdims=True)
        dxh = dh * gn_ref[...]
        dx = dx2_ref[...] + r * (dxh - xh * jnp.mean(dxh * xh, axis=-1, keepdims=True))
        dx_ref[...] = dx
        dxb_ref[...] = dx.astype(BF16)

    row = pl.BlockSpec((tm, D_MODEL), lambda i: (i, 0))
    vec = pl.BlockSpec((1, D_MODEL), lambda i: (0, 0))
    aspec = pl.BlockSpec((N_SHARD, tm, HID_S), lambda i: (0, i, 0))
    resident = lambda shape: pl.BlockSpec(shape, lambda i: (0, 0, 0), pipeline_mode=pl.Buffered(1))
    return pl.pallas_call(
        body,
        out_shape=(SDS((N_SHARD, S, HID_S), BF16), SDS((N_SHARD, S, HID_S), BF16),
                   SDS((S, D_MODEL), F32), SDS((S, D_MODEL), BF16), SDS((1, D_MODEL), F32)),
        grid=(S // tm,),
        in_specs=[row, row, resident((N_SHARD, HID_S, D_MODEL)), resident((N_SHARD, D_MODEL, HID_S)),
                  resident((N_SHARD, D_MODEL, HID_S)), aspec, aspec, row, vec],
        out_specs=(aspec, aspec, row, row, vec),
        compiler_params=_cparams("arbitrary"), name="ffn_bwd")(dx2b, dx2, wd, wg, wu, gte, up, x1, g2)


def _wgrad(name, a, b, a_spec, b_spec, out_shape, out_spec, n_par, S):
    tk = 2048

    def body(a_ref, b_ref, o_ref):
        @pl.when(pl.program_id(1) == 0)
        def _():
            o_ref[...] = jnp.zeros_like(o_ref)

        o_ref[...] += _dot_tn(a_ref[...], b_ref[...])

    return pl.pallas_call(
        body, out_shape=SDS(out_shape, F32), grid=(n_par, S // tk),
        in_specs=[a_spec(tk), b_spec(tk)], out_specs=out_spec,
        compiler_params=_cparams("parallel", "arbitrary"), name=name)(a, b)


def _out_proj_bwd(dx1b, wo, proj, ya, yr, exchanges=()):
    S = dx1b.shape[0]
    tm = 512
    gate0 = 15 * COLB

    def body(d_ref, w_ref, ga_ref, gr_ref, ya_ref, yr_ref, dya_ref, dyr_ref, dp_ref):
        for rows in _row_pieces(tm, 256):
            dm = _dot_nt(d_ref[rows, :], w_ref[...])
            sa = _sigmoid(ga_ref[rows, :].astype(F32))
            sr = _sigmoid(gr_ref[rows, :].astype(F32))
            dya_ref[rows, :] = (dm * sa).astype(BF16)
            dyr_ref[rows, :] = (dm * sr).astype(BF16)
            dp_ref[rows, 0:D_MODEL] = (dm * ya_ref[rows, :].astype(F32) * sa * (1.0 - sa)).astype(BF16)
            dp_ref[rows, D_MODEL:2 * D_MODEL] = (dm * yr_ref[rows, :].astype(F32) * sr * (1.0 - sr)).astype(BF16)

    row = pl.BlockSpec((tm, D_MODEL), lambda i: (i, 0))
    cols = lambda c0, w: pl.BlockSpec((pl.Element(tm), pl.Element(w)), lambda i: (i * tm, c0))
    return _carrier_call(
        body, (dx1b, wo, proj, proj, ya, yr),
        out_shape=(SDS((S, D_MODEL), BF16), SDS((S, D_MODEL), BF16), SDS((S, PROJ_W), BF16)), grid=(S // tm,),
        in_specs=[row, pl.BlockSpec((D_MODEL, D_MODEL), lambda i: (0, 0), pipeline_mode=pl.Buffered(1)),
                  cols(gate0, D_MODEL), cols(gate0 + D_MODEL, D_MODEL), row, row],
        out_specs=(row, row, cols(gate0, 2 * D_MODEL)),
        sem=("parallel",), name="out_proj_bwd", exchanges=exchanges)


def _branch_bwd(dya, dyr, wa, wr, att):
    S = dya.shape[0]
    tm = 1024

    def body(da_ref, dr_ref, wa_ref, wr_ref, att_ref, datt_ref, rho_ref, dyi_ref):
        datt = _dot_nt(da_ref[...], wa_ref[...])
        datt_ref[...] = datt.astype(BF16)
        dyi_ref[...] = _dot_nt(dr_ref[...], wr_ref[...]).astype(BF16)
        prod = datt * att_ref[...].astype(F32)
        lane = lax.broadcasted_iota(jnp.int32, (tm, 128), 1)
        lo = lane < 64
        rho = jnp.zeros((tm, 128), F32)
        for c in range(4):
            pc = prod[:, c * 128:(c + 1) * 128]
            tot = jnp.sum(pc, axis=-1, keepdims=True)
            low = jnp.sum(jnp.where(lo, pc, 0.0), axis=-1, keepdims=True)
            rho = jnp.where(lane // 16 == 2 * c, low, jnp.where(lane // 16 == 2 * c + 1, tot - low, rho))
        rho_ref[...] = rho

    row = lambda w: pl.BlockSpec((tm, w), lambda i: (i, 0))
    return pl.pallas_call(
        body, out_shape=(SDS((S, 512), BF16), SDS((S, 128), F32), SDS((S, 1024), BF16)), grid=(S // tm,),
        in_specs=[row(1024), row(1024), pl.BlockSpec((512, 1024), lambda i: (0, 0)),
                  pl.BlockSpec((1024, 1024), lambda i: (0, 0)), row(512)],
        out_specs=(row(512), row(128), row(1024)),
        compiler_params=_cparams("parallel"), name="branch_bwd")(dya, dyr, wa, wr, att)


def _attn_bwd(qkv, datt, lse, rho, rtab, d, gi, exchanges=()):
    L = qkv.shape[0]
    nb = L // BLK
    T = d * nb

    def body(q_ref, kc_ref, kp_ref, vc_ref, vp_ref, do_ref, lse_ref, rho_ref, tq_ref, tk_ref,
             dq_ref, dk_ref, dv_ref, ck, cv):
        t = pl.program_id(0)
        n = jnp.minimum(t, T - 1) % nb

        @pl.when(t == 0)
        def _():
            ck[...] = jnp.zeros_like(ck)
            cv[...] = jnp.zeros_like(cv)

        def store_rot(ref, val, t_ref, c):
            sl = slice(c * 128, (c + 1) * 128)
            ref[:, sl] = _unrot(val, t_ref[0], t_ref[1], t_ref[2], 32).astype(BF16)

        @pl.when(t < T)
        def _():
            mask = _band_mask(n)
            mask2 = jnp.concatenate([mask, mask], axis=0)
            lo = lax.broadcasted_iota(jnp.int32, (BLK, 128), 1) < 64

            def stacked(a):
                return jnp.concatenate([jnp.where(lo, a, jnp.zeros_like(a)), jnp.where(lo, jnp.zeros_like(a), a)], axis=0)

            def head_cols(ref, c):
                return jnp.concatenate([jnp.broadcast_to(ref[:, 32 * c:32 * c + 1], (BLK, 2 * BLK)),
                                        jnp.broadcast_to(ref[:, 32 * c + 16:32 * c + 17], (BLK, 2 * BLK))], axis=0)

            ops, raw = [], []
            for c in range(4):
                sl = slice(c * 128, (c + 1) * 128)
                q2, do2 = stacked(q_ref[:, sl]), stacked(do_ref[:, sl])
                k = jnp.concatenate([kp_ref[:, sl], kc_ref[:, sl]], axis=0)
                v = jnp.concatenate([vp_ref[:, sl], vc_ref[:, sl]], axis=0)
                ops.append((q2, do2, k))
                raw.append((_dot_nt(q2, k), _dot_nt(do2, v)))
            grads = []
            for c, (s, dp) in enumerate(raw):
                p = jnp.where(mask2, jnp.exp(s * 0.125 - head_cols(lse_ref, c)), 0.0)
                grads.append(((p * (dp - head_cols(rho_ref, c)) * 0.125).astype(BF16), p.astype(BF16)))
            for c, ((q2, do2, k), (ds, pb)) in enumerate(zip(ops, grads)):
                sl = slice(c * 128, (c + 1) * 128)
                dq2 = _dot(ds, k)
                dq_c = jnp.where(lo, dq2[:BLK], dq2[BLK:])
                dk_c = _dot_tn(ds, q2)
                dv_c = _dot_tn(pb, do2)
                store_rot(dq_ref, dq_c, tq_ref, c)
                store_rot(dk_ref, ck[:, sl] + dk_c[:BLK], tk_ref, c)
                dv_ref[:, sl] = (cv[:, sl] + dv_c[:BLK]).astype(BF16)
                ck[:, sl] = dk_c[BLK:]
                cv[:, sl] = dv_c[BLK:]

        @pl.when(t == T)
        def _():
            for c in range(4):
                sl = slice(c * 128, (c + 1) * 128)
                store_rot(dk_ref, ck[:, sl], tk_ref, c)
            dv_ref[...] = cv[...].astype(BF16)

    blk_of = lambda t: (jnp.minimum(t, T - 1) % nb, jnp.minimum(t, T - 1) // nb)
    cur = lambda t: blk_of(t)
    prev = lambda t: (jnp.maximum(blk_of(t)[0] - 1, 0), blk_of(t)[1])
    fin = lambda t: blk_of(jnp.maximum(t - 1, 0))
    col = _qkv_col(d, gi)
    qkv_spec = lambda kind, which: pl.BlockSpec((BLK, 512), lambda t: (which(t)[0], col(kind, which(t)[1])))
    row_spec = lambda w, which: pl.BlockSpec((BLK, w), lambda t: which(t))
    tab_spec = lambda which: pl.BlockSpec((3, BLK, 128), lambda t: (0, *which(t)))
    return _carrier_call(
        body, (qkv, qkv, qkv, qkv, qkv, datt, lse, rho, rtab, rtab),
        out_shape=(SDS((L, d * 512), BF16),) * 3, grid=(T + 1,),
        in_specs=[qkv_spec(0, cur), qkv_spec(1, cur), qkv_spec(1, prev), qkv_spec(2, cur), qkv_spec(2, prev),
                  row_spec(512, cur), row_spec(128, cur), row_spec(128, cur), tab_spec(cur), tab_spec(fin)],
        out_specs=(row_spec(512, cur), row_spec(512, fin), row_spec(512, fin)),
        scratch_shapes=[pltpu.VMEM((BLK, 512), F32), pltpu.VMEM((BLK, 512), F32)],
        sem=("arbitrary",), name=f"attn_bwd_g{gi}", exchanges=exchanges)


def _ret_bwd(proj, rn, rstd, dyrin, states, tab, consts, dproj, exchanges=()):
    S = proj.shape[0]
    nc = S // BLK
    dmask, zeta, xi, dec = consts

    def body(q_ref, k_ref, v0_ref, v1_ref, g0_ref, g1_ref, rn_ref, rs_ref, dy_ref, st_ref, tq_ref, tk_ref,
             dm_ref, z_ref, x_ref, dec_ref, dp_prev, dp_ref, dR):
        dq_ref, dk_ref = dp_ref.at[:, 0:512], dp_ref.at[:, 512:1024]
        dv_ref, dgr_ref = dp_ref.at[:, 1024:2048], dp_ref.at[:, 2048:3072]

        @pl.when(pl.program_id(0) == 0)
        def _():
            dR[...] = jnp.zeros_like(dR)

        dobs = []
        for h in range(RET_HEADS):
            vs = slice((h % 2) * 256, (h % 2 + 1) * 256)
            os_ = slice(h * 256, (h + 1) * 256)
            gr = (g0_ref if h < 2 else g1_ref)[:, vs].astype(F32)
            sg = _sigmoid(gr)
            rn_v = rn_ref[:, os_].astype(F32)
            dyi = dy_ref[:, os_].astype(F32)
            dgr_ref[:, os_] = (dyi * rn_v * sg * (1.0 + gr * (1.0 - sg))).astype(BF16)
            drn = dyi * gr * sg
            rstd = jnp.broadcast_to(rs_ref[:, 16 * h:16 * h + 1], (BLK, 256))
            do = rstd * (drn - jnp.mean(drn, axis=-1, keepdims=True) - rn_v * jnp.mean(drn * rn_v, axis=-1, keepdims=True))
            dobs.append(do.astype(BF16))
        first = []
        for h in range(RET_HEADS):
            hs = slice(h * 128, (h + 1) * 128)
            q, k = q_ref[:, hs], k_ref[:, hs]
            v = (v0_ref if h < 2 else v1_ref)[:, (h % 2) * 256:(h % 2 + 1) * 256]
            dob, dRb = dobs[h], dR[h].astype(BF16)
            kz = (k.astype(F32) * z_ref[h]).astype(BF16)
            qx = (q.astype(F32) * x_ref[h]).astype(BF16)
            first.append((q, k, _dot_nt(q, k), _dot_nt(dob, v), _dot(kz, dRb), _dot_nt(dob, st_ref[h]),
                          _dot_nt(v, dRb), _dot_tn(qx, dob)))
        masked = [((s * dm_ref[h]).astype(BF16), (dsr * dm_ref[h]).astype(BF16))
                  for h, (_, _, s, dsr, _, _, _, _) in enumerate(first)]
        for h in range(RET_HEADS):
            hs = slice(h * 128, (h + 1) * 128)
            os_ = slice(h * 256, (h + 1) * 256)
            q, k, _, _, dv_state, dq_state, dk_state, dr_new = first[h]
            sD, dS = masked[h]
            dv_ref[:, os_] = (_dot_tn(sD, dobs[h]) + dv_state).astype(BF16)
            dq = _dot(dS, k) + dq_state * x_ref[h]
            dk = _dot_tn(dS, q) + dk_state * z_ref[h]
            dR[h] = dR[h] * dec_ref[h, 0:1, :] + dr_new
            dq_ref[:, hs] = _unrot(dq, tq_ref[0], tq_ref[1], tq_ref[2], 1).astype(BF16)
            dk_ref[:, hs] = _unrot(dk, tk_ref[0], tk_ref[1], tk_ref[2], 1).astype(BF16)

    rc = lambda c: nc - 1 - c
    cst = lambda shape: pl.BlockSpec(shape, lambda c: (0, 0, 0))
    blk = lambda j: pl.BlockSpec((BLK, 512), lambda c: (rc(c), j))
    row = lambda w: pl.BlockSpec((BLK, w), lambda c: (rc(c), 0))
    (dproj,), xres = _carrier_call(
        body, (proj, proj, proj, proj, proj, proj, rn, rstd, dyrin, states, tab, tab, dmask, zeta, xi, dec, dproj),
        out_shape=(SDS((S, PROJ_W), BF16),), grid=(nc,),
        in_specs=[blk(QR_B), blk(KR_B), blk(11), blk(12), blk(13), blk(14), row(1024), row(128), row(1024),
                  pl.BlockSpec((RET_HEADS, None, BLK, 256), lambda c: (0, rc(c), 0, 0)),
                  pl.BlockSpec((None, 3, BLK, 128), lambda c: (1, 0, rc(c), 0)),
                  pl.BlockSpec((None, 3, BLK, 128), lambda c: (2, 0, rc(c), 0)),
                  cst((RET_HEADS, BLK, BLK)), cst((RET_HEADS, BLK, 128)), cst((RET_HEADS, BLK, 128)), cst((RET_HEADS, 8, 256)),
                  ANY],
        out_specs=(pl.BlockSpec((pl.Element(BLK), pl.Element(6 * COLB)), lambda c: (rc(c) * BLK, QR_B * COLB)),),
        scratch_shapes=[pltpu.VMEM((RET_HEADS, BLK, 256), F32)],
        sem=("arbitrary",), name="ret_bwd", exchanges=exchanges, in_out_aliases={16: 0})
    return dproj, xres


def _wgrad_in_half(ht, dproj, sidx, kept, exchanges=()):
    S = dproj.shape[0]
    tk = 2048
    half = (lambda sx: sx[4]) if kept else (lambda sx: 1 - sx[4])

    def body(a_ref, b_ref, o_ref):
        @pl.when(pl.program_id(1) == 0)
        def _():
            o_ref[...] = jnp.zeros_like(o_ref)

        o_ref[...] += _dot(a_ref[...], b_ref[...])

    (g,), xres = _carrier_call(
        body, (ht, dproj), out_shape=(SDS((D_MODEL // 2, PROJ_W), F32),), grid=(N_SHARD, S // tk),
        in_specs=[pl.BlockSpec((D_MODEL // 2, tk), lambda s, k, sx: (half(sx), k)),
                  pl.BlockSpec((tk, W_IN_S), lambda s, k, sx: (k, s))],
        out_specs=(pl.BlockSpec((D_MODEL // 2, W_IN_S), lambda s, k, sx: (0, s)),),
        sem=("parallel", "arbitrary"), name="wgrad_in_kept" if kept else "wgrad_in_sent", exchanges=exchanges,
        prefetch=sidx)
    return g, xres


def _in_proj_bwd(dproj, w_in, x, g1, dx1, exchanges=()):
    S = x.shape[0]
    tm = 1024

    def body(d_ref, w_ref, x_ref, g_ref, dx1_ref, dx_ref, dgn_ref, acc):
        i, s = pl.program_id(0), pl.program_id(1)

        @pl.when(s == 0)
        def _():
            acc[...] = jnp.zeros_like(acc)

        @pl.when((i == 0) & (s == 0))
        def _():
            dgn_ref[...] = jnp.zeros_like(dgn_ref)

        acc[...] += _dot_nt(d_ref[...], w_ref[...])

        @pl.when(s == N_SHARD - 1)
        def _():
            xv = x_ref[...]
            r = lax.rsqrt(jnp.mean(xv * xv, axis=-1, keepdims=True) + NORM_EPS)
            xh = xv * r
            dh = acc[...]
            dgn_ref[...] += jnp.sum(dh * xh, axis=0, keepdims=True)
            dxh = dh * g_ref[...]
            dx_ref[...] = dx1_ref[...] + r * (dxh - xh * jnp.mean(dxh * xh, axis=-1, keepdims=True))

    row = pl.BlockSpec((tm, D_MODEL), lambda i, s: (i, 0))
    vec = pl.BlockSpec((1, D_MODEL), lambda i, s: (0, 0))
    (gx, dg), xres = _carrier_call(
        body, (dproj, w_in, x, g1, dx1),
        out_shape=(SDS((S, D_MODEL), F32), SDS((1, D_MODEL), F32)), grid=(S // tm, N_SHARD),
        in_specs=[pl.BlockSpec((tm, W_IN_S), lambda i, s: (i, s)),
                  pl.BlockSpec((D_MODEL, W_IN_S), lambda i, s: (0, s)), row, vec, row],
        out_specs=(row, vec), scratch_shapes=[pltpu.VMEM((tm, D_MODEL), F32)],
        sem=("arbitrary", "arbitrary"), name="in_proj_bwd", exchanges=exchanges)
    return gx, dg, xres


def _sub_view(a, d):
    S, W = a.shape
    return a.reshape(S // d, d * W)


def _step(x, tgt, g1, g2, g3, comm):
    S = x.shape[0]
    tab_np = _tables(S)
    tab = jnp.asarray(tab_np)
    consts = _ret_consts()

    h, ht = _rms_fwd(x, g1)
    w_in = comm.w_in()
    proj, xres = _in_proj(h, w_in, tab, comm.carry("in_proj"))
    comm.took("in_proj", xres)
    qkvs, o_parts, lse_parts = [], [], []
    for gi, d in enumerate(DILATIONS):
        qkv = proj if d == 1 else _qkv_to_sub(proj, d, gi)
        (o_g, lse_g), xres = _attn_fwd(qkv, d, gi, comm.carry(f"attn_fwd_g{gi}"))
        comm.took(f"attn_fwd_g{gi}", xres)
        qkvs.append(qkv)
        o_parts.append(o_g)
        lse_parts.append(lse_g)
    att, lse_tot = _attn_merge(o_parts, lse_parts)
    (yrin, rn, rstd, states), xres = _ret_fwd(proj, consts, comm.carry("ret_fwd"))
    comm.took("ret_fwd", xres)
    wa, wr, wo = comm.weight(1), comm.weight(2), comm.weight(3)
    merged, ya, yr = _branch_merge(att, yrin, proj, wa, wr)
    (x1, h2), xres = _out_proj(merged, wo, x, g2, comm.carry("out_proj"))
    comm.took("out_proj", xres)
    wg, wu = comm.weight(4), comm.weight(5)
    (gte, up, act), xres = _ffn_up(h2, wg, wu, comm.carry("ffn_up"))
    comm.took("ffn_up", xres)
    wd = comm.weight(6)
    dx2, dx2b, dg3, loss_p = _ffn_down_loss(act, wd, x1, g3, tgt)

    dgte, dup, dx1, dx1b, dg2 = _ffn_bwd(dx2b, dx2, wd, wg, wu, gte, up, x1, g2)
    tok3 = lambda w: (lambda tk: pl.BlockSpec((None, tk, w), lambda p, k: (p, k, 0)))
    tok2 = lambda w: (lambda tk: pl.BlockSpec((tk, w), lambda p, k: (k, 0)))
    g_d = _wgrad("wgrad_down", act, dx2b, tok3(HID_S), tok2(D_MODEL), (N_SHARD, HID_S, D_MODEL),
                 pl.BlockSpec((None, HID_S, D_MODEL), lambda p, k: (p, 0, 0)), N_SHARD, S)
    g_g = _wgrad("wgrad_gate", h2, dgte, tok2(D_MODEL), tok3(HID_S), (N_SHARD, D_MODEL, HID_S),
                 pl.BlockSpec((None, D_MODEL, HID_S), lambda p, k: (p, 0, 0)), N_SHARD, S)
    g_u = _wgrad("wgrad_up", h2, dup, tok2(D_MODEL), tok3(HID_S), (N_SHARD, D_MODEL, HID_S),
                 pl.BlockSpec((None, D_MODEL, HID_S), lambda p, k: (p, 0, 0)), N_SHARD, S)
    comm.grads({4: g_g, 5: g_u, 6: g_d})
    (dya, dyr, dproj), xres = _out_proj_bwd(dx1b, wo, proj, ya, yr, comm.carry("out_proj_bwd"))
    comm.took("out_proj_bwd", xres)
    colblk = lambda w: (lambda tk: pl.BlockSpec((tk, w), lambda p, k: (k, p)))
    g_o = _wgrad("wgrad_out", merged, dx1b, colblk(256), tok2(D_MODEL), (D_MODEL, D_MODEL),
                 pl.BlockSpec((256, D_MODEL), lambda p, k: (p, 0)), 4, S)
    datt, rho, dyrin = _branch_bwd(dya, dyr, wa, wr, att)
    g_a = _wgrad("wgrad_attn", att, dya, tok2(512), colblk(512), (512, D_MODEL),
                 pl.BlockSpec((512, 512), lambda p, k: (0, p)), 2, S)
    g_r = _wgrad("wgrad_ret", yrin, dyr, colblk(256), tok2(D_MODEL), (D_MODEL, D_MODEL),
                 pl.BlockSpec((256, D_MODEL), lambda p, k: (p, 0)), 4, S)
    comm.grads({1: g_a, 2: g_r.reshape(N_SHARD, 256, D_MODEL), 3: g_o.reshape(N_SHARD, 256, D_MODEL)})
    dproj, xres = _ret_bwd(proj, rn, rstd, dyrin, states, tab, consts, dproj, comm.carry("ret_bwd"))
    comm.took("ret_bwd", xres)
    dqs, dks, dvs = [], [], []
    for gi, d in enumerate(DILATIONS):
        rtab = jnp.asarray(tab_np[0].reshape(3, S // d, d * 128))
        (dq, dk, dv), xres = _attn_bwd(qkvs[gi], _sub_view(datt, d), _sub_view(lse_tot, d), _sub_view(rho, d), rtab, d, gi,
                                       comm.carry(f"attn_bwd_g{gi}"))
        comm.took(f"attn_bwd_g{gi}", xres)
        dqs.append(dq)
        dks.append(dk)
        dvs.append(dv)
    dproj = _assemble_dproj((dqs, dks, dvs), dproj)
    g_sent, xres = _wgrad_in_half(ht, dproj, comm.sidx, False, comm.carry("wgrad_in_sent"))
    comm.took("wgrad_in_sent", xres)
    comm.grads({"in_sent": g_sent})
    g_kept, xres = _wgrad_in_half(ht, dproj, comm.sidx, True, comm.carry("wgrad_in_kept"))
    comm.grads({"in_kept": g_kept})
    comm.took("wgrad_in_kept", xres)
    grad_x, dg1, xres = _in_proj_bwd(dproj, w_in, x, g1, dx1, comm.carry("in_proj_bwd"))
    comm.took("in_proj_bwd", xres)
    return loss_p, grad_x, (dg1, dg2, dg3)


W_KINDS = ("col", "col", "lead", "lead", "lead", "lead", "lead")
W_SHARD = ((1024, W_IN_S), (512, 256), (256, 1024), (256, 1024), (1024, HID_S), (1024, HID_S), (HID_S, 1024))
N_W = len(W_KINDS)


def _full_shape(wi):
    R, C = W_SHARD[wi]
    return (R, N_SHARD * C) if W_KINDS[wi] == "col" else (N_SHARD, R, C)


def _view(ref, wi, s, half):
    R, C = W_SHARD[wi]
    rows = pl.ds(half * (R // 2), R // 2)
    if W_KINDS[wi] == "col":
        return ref.at[rows, pl.ds(pl.multiple_of(s * C, 128), C)]
    return ref.at[s, rows, :]


def _mesh_pos():
    x, y, c = lax.axis_index("x"), lax.axis_index("y"), lax.axis_index("c")
    chips = [(1 - x, y), (x, 1 - y), (1 - x, 1 - y)]
    return x, y, c, chips


def _cast_bf16(a):
    R, C = a.shape
    tr = R // 2 if R % 32 == 0 else R

    def body(a_ref, o_ref):
        o_ref[...] = a_ref[...].astype(BF16)

    spec = pl.BlockSpec((tr, C), lambda i: (i, 0))
    return pl.pallas_call(body, out_shape=SDS((R, C), BF16), grid=(R // tr,), in_specs=[spec], out_specs=spec,
                          compiler_params=_cparams("parallel"), name=f"cast_{R}x{C}")(a)


def _remote(send, recv, k, src, dst, to):
    return pltpu.make_async_remote_copy(src_ref=src, dst_ref=dst, send_sem=send.at[k], recv_sem=recv.at[k],
                                        device_id=to, device_id_type=MESH)


def _gather_now(wis, shards):
    n = len(wis)

    def body(*refs):
        sh, full = refs[:n], refs[n:2 * n]
        send, recv, loc = refs[2 * n:]
        x, y, c, _ = _mesh_pos()
        s_me, sib = 2 * x + y, (x, y, 1 - c)
        xn, yn = (1 - x, y), (x, 1 - y)
        flip = lambda a, b: a + b - 2 * a * b
        via = (flip(x, 1 - c), flip(y, c))
        onto = (flip(x, c), flip(y, 1 - c))
        shard_of = lambda chip: 2 * chip[0] + chip[1]
        own, started = [], []
        for i, wi in enumerate(wis):
            Rh = W_SHARD[wi][0] // 2
            for hf in range(2):
                cp = pltpu.make_async_copy(sh[i].at[pl.ds(hf * Rh, Rh), :], _view(full[i], wi, s_me, hf), loc.at[2 * i + hf])
                cp.start()
                own.append(cp)
            for j, chip in enumerate((xn, yn)):
                cp = _remote(send, recv, 6 * i + j, sh[i].at[pl.ds(c * Rh, Rh), :], _view(full[i], wi, s_me, c), (*chip, c))
                cp.start()
                started.append(cp)

        def pass_to_sibling(i, wi, k, s):
            mine = _view(full[i], wi, s, c)
            fw = _remote(send, recv, 6 * i + k, mine, mine, sib)
            fw.start()
            started.append(fw)

        for i, wi in enumerate(wis):
            for j, chip in enumerate((xn, yn)):
                land = _view(full[i], wi, shard_of(chip), c)
                _remote(send, recv, 6 * i + j, land, land, (*chip, c)).wait_recv()
                pass_to_sibling(i, wi, 3 + j, shard_of(chip))
            relay = _view(full[i], wi, shard_of(via), c)
            fw = _remote(send, recv, 6 * i + 2, relay, relay, (*onto, c))
            fw.start()
            started.append(fw)
        s_diag = 2 * (1 - x) + (1 - y)
        for i, wi in enumerate(wis):
            land = _view(full[i], wi, s_diag, c)
            _remote(send, recv, 6 * i + 2, land, land, (*onto, c)).wait_recv()
            pass_to_sibling(i, wi, 5, s_diag)
        for i, wi in enumerate(wis):
            for k, s in ((3, shard_of(xn)), (4, shard_of(yn)), (5, s_diag)):
                land = _view(full[i], wi, s, 1 - c)
                _remote(send, recv, 6 * i + k, land, land, sib).wait_recv()
        for cp in started:
            cp.wait_send()
        for cp in own:
            cp.wait()

    return pl.pallas_call(
        body, out_shape=tuple(SDS(_full_shape(wi), BF16) for wi in wis),
        in_specs=[ANY] * n, out_specs=tuple([ANY] * n),
        scratch_shapes=[pltpu.SemaphoreType.DMA((6 * n,)), pltpu.SemaphoreType.DMA((6 * n,)),
                        pltpu.SemaphoreType.DMA((2 * n,))],
        name="gather_now")(*shards)


def _ex_gather_ici(wis, shards, then_d2d=False):
    n = len(wis)

    def build(ins, outs, send, recv, loc):
        x, y, c, chips = _mesh_pos()
        s_me, sib = 2 * x + y, (x, y, 1 - c)
        starts, waits, after = [], [], []
        for i, wi in enumerate(wis):
            Rh = W_SHARD[wi][0] // 2
            for hf in range(2):
                cp = pltpu.make_async_copy(ins[i].at[pl.ds(hf * Rh, Rh), :], _view(outs[i], wi, s_me, hf), loc.at[2 * i + hf])
                starts.append(cp)
                waits.append(cp.wait)
            for j, chip in enumerate(chips):
                cp = _remote(send, recv, 3 * i + j, ins[i].at[pl.ds(c * Rh, Rh), :], _view(outs[i], wi, s_me, c), (*chip, c))
                land = _view(outs[i], wi, 2 * chip[0] + chip[1], c)
                starts.append(cp)
                waits += [cp.wait_send, _remote(send, recv, 3 * i + j, land, land, (*chip, c)).wait_recv]
                if then_d2d:
                    theirs = _view(outs[i], wi, 2 * chip[0] + chip[1], 1 - c)
                    fw = _remote(send, recv, 3 * n + 3 * i + j, land, land, sib)
                    waits.append(fw.start)
                    after += [fw.wait_send, _remote(send, recv, 3 * n + 3 * i + j, theirs, theirs, sib).wait_recv]
        return starts, waits + after

    return _Exchange(shards, [SDS(_full_shape(wi), BF16) for wi in wis], {}, (6 if then_d2d else 3) * n, 2 * n, build)


def _ex_gather_d2d(wis, fulls):
    def build(ins, outs, send, recv, loc):
        x, y, c, chips = _mesh_pos()
        sib = (x, y, 1 - c)
        starts, waits = [], []
        for i, wi in enumerate(wis):
            for j, chip in enumerate(chips):
                mine = _view(outs[i], wi, 2 * chip[0] + chip[1], c)
                theirs = _view(outs[i], wi, 2 * chip[0] + chip[1], 1 - c)
                cp = _remote(send, recv, 3 * i + j, mine, mine, sib)
                starts.append(cp)
                waits += [cp.wait_send, _remote(send, recv, 3 * i + j, theirs, theirs, sib).wait_recv]
        return starts, waits

    return _Exchange(fulls, [SDS(f.shape, BF16) for f in fulls], {i: i for i in range(len(wis))}, 3 * len(wis), 0, build)


def _half_shape(wi):
    R, C = W_SHARD[wi]
    return (R // 2, N_SHARD * C) if W_KINDS[wi] == "col" else (N_SHARD, R // 2, C)


def _ex_pair(wis, grads):
    def build(ins, outs, send, recv, loc):
        x, y, c, _ = _mesh_pos()
        starts, waits = [], []
        for i, wi in enumerate(wis):
            Rh = W_SHARD[wi][0] // 2
            rows = pl.ds((1 - c) * Rh, Rh)
            if tuple(ins[i].shape) == _half_shape(wi):
                src = ins[i]
            else:
                src = ins[i].at[rows, :] if W_KINDS[wi] == "col" else ins[i].at[:, rows, :]
            cp = _remote(send, recv, i, src, outs[i], (x, y, 1 - c))
            starts.append(cp)
            waits.append(cp.wait)
        return starts, waits

    return _Exchange(grads, [SDS(_half_shape(wi), F32) for wi in wis], {}, len(wis), 0, build)


def _ex_chip(wis, pbs):
    def build(ins, outs, send, recv, loc):
        x, y, c, chips = _mesh_pos()
        starts, waits = [], []
        for i, wi in enumerate(wis):
            for j, chip in enumerate(chips):
                cp = _remote(send, recv, 3 * i + j, ins[i].at[j], outs[i].at[j], (*chip, c))
                starts.append(cp)
                waits.append(cp.wait)
        return starts, waits

    shapes = [SDS((3, W_SHARD[wi][0] // 2, W_SHARD[wi][1]), BF16) for wi in wis]
    return _Exchange(pbs, shapes, {}, 3 * len(wis), 0, build)


def _ex_share(wis, halves):
    def build(ins, outs, send, recv, loc):
        x, y, c, _ = _mesh_pos()
        sib = (x, y, 1 - c)
        starts, waits = [], []
        for i, wi in enumerate(wis):
            cp = _remote(send, recv, i, outs[i].at[c], outs[i].at[c], sib)
            starts.append(cp)
            waits += [cp.wait_send, _remote(send, recv, i, outs[i].at[1 - c], outs[i].at[1 - c], sib).wait_recv]
        return starts, waits

    return _Exchange(halves, [SDS(h.shape, F32) for h in halves], {i: i for i in range(len(wis))}, len(wis), 0, build)


def _row_tile(rh, C):
    best = 16
    for t in range(16, rh + 1, 16):
        if rh % t == 0 and t * C * 4 <= (3 << 19):
            best = t
    return best


def _pair_sum(wi, g, ra, sidx):
    R, C = W_SHARD[wi]
    Rh = R // 2
    tr = _row_tile(Rh, C)
    nt = Rh // tr
    off = 0 if tuple(g.shape) == _half_shape(wi) else nt
    col = W_KINDS[wi] == "col"

    def body(sidx_ref, *refs):
        gs, rs = refs[:4], refs[4:8]
        own_ref, pb_ref = refs[8:]
        own_ref[...] = gs[0][...] + rs[0][...]
        for j in range(3):
            pb_ref[j] = (gs[1 + j][...] + rs[1 + j][...]).astype(BF16)

    def gspec(slot):
        if col:
            return pl.BlockSpec((tr, C), lambda i, sx: (sx[4] * off + i, sx[slot]))
        return pl.BlockSpec((None, tr, C), lambda i, sx: (sx[slot], sx[4] * off + i, 0))

    def rspec(slot):
        if col:
            return pl.BlockSpec((tr, C), lambda i, sx: (i, sx[slot]))
        return pl.BlockSpec((None, tr, C), lambda i, sx: (sx[slot], i, 0))

    return pl.pallas_call(
        body, out_shape=(SDS((Rh, C), F32), SDS((3, Rh, C), BF16)),
        grid_spec=pltpu.PrefetchScalarGridSpec(
            num_scalar_prefetch=1, grid=(nt,),
            in_specs=[gspec(k) for k in range(4)] + [rspec(k) for k in range(4)],
            out_specs=(pl.BlockSpec((tr, C), lambda i, sx: (i, 0)), pl.BlockSpec((3, tr, C), lambda i, sx: (0, i, 0)))),
        compiler_params=_cparams("arbitrary"), name=f"pair_sum_w{wi}")(sidx, g, g, g, g, ra, ra, ra, ra)


def _chip_sum(wi, own, rb, sidx):
    R, C = W_SHARD[wi]
    Rh = R // 2
    tr = _row_tile(Rh, C)

    def body(sidx_ref, own_ref, rb_ref, o_ref):
        o_ref[...] = ((own_ref[...] + rb_ref[0].astype(F32)) + rb_ref[1].astype(F32)) + rb_ref[2].astype(F32)

    return pl.pallas_call(
        body, out_shape=SDS((2, Rh, C), F32),
        grid_spec=pltpu.PrefetchScalarGridSpec(
            num_scalar_prefetch=1, grid=(Rh // tr,),
            in_specs=[pl.BlockSpec((tr, C), lambda i, sx: (i, 0)), pl.BlockSpec((3, tr, C), lambda i, sx: (0, i, 0))],
            out_specs=pl.BlockSpec((None, tr, C), lambda i, sx: (sx[4], i, 0))),
        compiler_params=_cparams("arbitrary"), name=f"chip_sum_w{wi}")(sidx, own, rb)


def _gain_allgather(blk, ex):
    m_per, n = blk.shape
    n_in, n_out = len(ex.ins), len(ex.out_shapes)

    def body(x_ref, *rest):
        xin, out_ref, xout = rest[:n_in], rest[n_in], rest[n_in + 1:n_in + 1 + n_out]
        send_sems, recv_sems, local_sem = rest[n_in + 1 + n_out:n_in + 4 + n_out]
        ex_starts, ex_waits = ex.build(xin, xout, *rest[n_in + 4 + n_out:])
        for cp in ex_starts:
            cp.start()
        x, y, c, chips = _mesh_pos()
        me, sibling = (x, y, c), (x, y, 1 - c)

        def rows(px, py, pc):
            return out_ref.at[pl.ds((4 * px + 2 * py + pc) * m_per, m_per), :]

        def copy(k, block, to, src=None):
            return pltpu.make_async_remote_copy(
                src_ref=rows(*block) if src is None else src, dst_ref=rows(*block),
                send_sem=send_sems.at[k], recv_sem=recv_sems.at[k], device_id=to, device_id_type=MESH)

        mine = pltpu.make_async_copy(x_ref, rows(*me), local_sem)
        mine.start()
        first = [copy(0, me, sibling, src=x_ref)]
        first += [copy(1 + j, me, (*chip, c), src=x_ref) for j, chip in enumerate(chips)]
        for cp in first:
            cp.start()
        passed = [copy(4 + j, (*chip, c), sibling) for j, chip in enumerate(chips)]
        for j, chip in enumerate(chips):
            copy(1 + j, (*chip, c), me).wait_recv()
            passed[j].start()
        copy(0, sibling, me).wait_recv()
        for j, chip in enumerate(chips):
            copy(4 + j, (*chip, 1 - c), me).wait_recv()
        for cp in first + passed:
            cp.wait_send()
        mine.wait()
        for w in ex_waits:
            w()

    vm = pl.BlockSpec(memory_space=pltpu.VMEM)
    res = pl.pallas_call(
        body, out_shape=(SDS((8 * m_per, n), blk.dtype), *ex.out_shapes),
        in_specs=[vm] + [ANY] * n_in, out_specs=(vm, *[ANY] * n_out),
        input_output_aliases={1 + a: 1 + o for a, o in ex.aliases.items()},
        scratch_shapes=[pltpu.SemaphoreType.DMA((7,)), pltpu.SemaphoreType.DMA((7,)), pltpu.SemaphoreType.DMA] + ex.sems(),
        name="gain_allgather")(blk, *ex.ins)
    return res[0], tuple(res[1:])


def _adam_math(w, g, m, v):
    mn = ADAM_B1 * m + (1.0 - ADAM_B1) * g
    vn = ADAM_B2 * v + (1.0 - ADAM_B2) * (g * g)
    mh = mn / (1.0 - ADAM_B1 ** ADAM_STEP)
    vh = vn / (1.0 - ADAM_B2 ** ADAM_STEP)
    return -ADAM_LR * (mh / (jnp.sqrt(vh) + ADAM_EPS) + ADAM_WD * w), mn, vn


def _adamw(wi, w, g, m, v):
    R, C = w.shape
    tr = _row_tile(R, C)

    def body(w_ref, g_ref, m_ref, v_ref, go_ref, d_ref, mn_ref, vn_ref):
        g = g_ref[...]
        go_ref[...] = g
        d_ref[...], mn_ref[...], vn_ref[...] = _adam_math(w_ref[...], g, m_ref[...], v_ref[...])

    spec = pl.BlockSpec((tr, C), lambda i: (i, 0))
    return pl.pallas_call(body, out_shape=(SDS((R, C), F32),) * 4, grid=(R // tr,), in_specs=[spec] * 4,
                          out_specs=(spec,) * 4, compiler_params=_cparams("parallel"), name=f"adamw_w{wi}")(w, g, m, v)


def _gain_update(gathered, w, m, v):
    def body(ga_ref, w_ref, m_ref, v_ref, g_ref, d_ref, mn_ref, vn_ref):
        g = ga_ref[0:8, :]
        for dev in range(1, 8):
            g = g + ga_ref[8 * dev:8 * dev + 8, :]
        g_ref[...] = g
        d_ref[...], mn_ref[...], vn_ref[...] = _adam_math(w_ref[...], g, m_ref[...], v_ref[...])

    return pl.pallas_call(body, out_shape=(SDS((8, 1024), F32),) * 4, name="gain_update")(gathered, w, m, v)


GROUP_FFN, GROUP_MIX, GROUP_IN = (4, 5, 6), (1, 2, 3), (0,)
REST = GROUP_MIX + GROUP_FFN


class _MeshComm:
    SCHEDULE = {
        "in_proj": [("ici", (1, 2, 3, 4))],
        "ret_fwd": [("d2d", (1, 2, 3, 4)), ("ici", (5,))],
        "out_proj": [("d2d", (5,))],
        "ffn_up": [("both", (6,))],
        "out_proj_bwd": [("pair", GROUP_FFN)],
        "ret_bwd": [("pair", GROUP_MIX), ("chip", (4,))],
        "attn_bwd_g0": [("chip", (5,))],
        "attn_bwd_g1": [("chip", (6,))],
        "attn_bwd_g2": [("chip", GROUP_MIX)],
        "wgrad_in_kept": [("pair", GROUP_IN), ("share", GROUP_FFN + GROUP_MIX)],
        "in_proj_bwd": [("chip", GROUP_IN)],
    }

    def __init__(self, shards):
        xi, yi, ci = lax.axis_index("x"), lax.axis_index("y"), lax.axis_index("c")
        self.sidx = jnp.stack([2 * xi + yi, 2 * (1 - xi) + yi, 2 * xi + (1 - yi), 2 * (1 - xi) + (1 - yi), ci]).astype(jnp.int32)
        self.shards, self.full = shards, {}
        self.g, self.own, self.pb, self.half, self.red = {}, {}, {}, {}, {}

    def w_in(self):
        return _gather_now(GROUP_IN, [self.shards[0]])[0]

    def weight(self, wi):
        return self.full[wi].reshape(D_MODEL, D_MODEL) if wi in (2, 3) else self.full[wi]

    def grads(self, by_wi):
        self.g.update(by_wi)

    def _exchange(self, stage, wis):
        pick = lambda table: [table[wi] for wi in wis]
        if stage == "ici":
            return _ex_gather_ici(wis, pick(self.shards))
        if stage == "both":
            return _ex_gather_ici(wis, pick(self.shards), then_d2d=True)
        if stage == "d2d":
            return _ex_gather_d2d(wis, pick(self.full))
        if stage == "pair":
            return _ex_pair(wis, [self.g["in_sent"] if wi == 0 else self.g[wi] for wi in wis])
        if stage == "chip":
            return _ex_chip(wis, pick(self.pb))
        return _ex_share(wis, pick(self.half))

    def _landed(self, stage, wis, res):
        for wi, r in zip(wis, res):
            if stage in ("ici", "d2d", "both"):
                self.full[wi] = r
            elif stage == "pair":
                self.own[wi], self.pb[wi] = _pair_sum(wi, self.g["in_kept"] if wi == 0 else self.g[wi], r, self.sidx)
            elif stage == "chip":
                self.half[wi] = _chip_sum(wi, self.own[wi], r, self.sidx)
            else:
                self.red[wi] = r

    def carry(self, point):
        return [self._exchange(stage, wis) for stage, wis in self.SCHEDULE.get(point, ())]

    def took(self, point, xres):
        for (stage, wis), res in zip(self.SCHEDULE.get(point, ()), xres):
            self._landed(stage, wis, res)

    def last_share(self):
        return self._exchange("share", GROUP_IN)

    def reduced(self, last_shared):
        self._landed("share", GROUP_IN, last_shared)
        return [self.red[wi] for wi in range(N_W)]


def kernel(x, norm_mix_g, w_in, w_out_attn, w_out_ret, w_out, norm_ffn_g, w_ffn_gate, w_ffn_up, w_ffn_down, norm_final_g, loss_target, m_norm_mix_g, m_w_in, m_w_out_attn, m_w_out_ret, m_w_out, m_norm_ffn_g, m_w_ffn_gate, m_w_ffn_up, m_w_ffn_down, m_norm_final_g, v_norm_mix_g, v_w_in, v_w_out_attn, v_w_out_ret, v_w_out, v_norm_ffn_g, v_w_ffn_gate, v_w_ffn_up, v_w_ffn_down, v_norm_final_g):
    ws = (w_in, w_out_attn, w_out_ret, w_out, w_ffn_gate, w_ffn_up, w_ffn_down)
    ms = (m_w_in, m_w_out_attn, m_w_out_ret, m_w_out, m_w_ffn_gate, m_w_ffn_up, m_w_ffn_down)
    vs = (v_w_in, v_w_out_attn, v_w_out_ret, v_w_out, v_w_ffn_gate, v_w_ffn_up, v_w_ffn_down)
    shard2d = lambda a, wi: a.reshape(W_SHARD[wi])

    comm = _MeshComm([_cast_bf16(shard2d(w, wi)) for wi, w in enumerate(ws)])
    g3 = norm_final_g.reshape(1, D_MODEL)
    loss_p, grad_x, gain_g = _step(x[0], loss_target[0], norm_mix_g, norm_ffn_g, g3, comm)

    pad8 = lambda rows: jnp.concatenate([r.reshape(1, D_MODEL) for r in rows]
                                        + [jnp.zeros((8 - len(rows), D_MODEL), F32)], axis=0)
    gathered, shared = _gain_allgather(pad8((*gain_g, jnp.tile(loss_p[0:1], (1, D_MODEL // 128)))), comm.last_share())
    gred = comm.reduced(shared)

    outs_g, outs_d, outs_m, outs_v = [], [], [], []
    for wi in range(N_W):
        g2d = gred[wi].reshape(W_SHARD[wi])
        gout, dlt, mn, vn = _adamw(wi, shard2d(ws[wi], wi), g2d, shard2d(ms[wi], wi), shard2d(vs[wi], wi))
        for lst, a in ((outs_g, gout), (outs_d, dlt), (outs_m, mn), (outs_v, vn)):
            lst.append(a.reshape(ws[wi].shape))

    gg, gd, gm, gv = _gain_update(gathered, pad8((norm_mix_g, norm_ffn_g, norm_final_g)),
                                  pad8((m_norm_mix_g, m_norm_ffn_g, m_norm_final_g)),
                                  pad8((v_norm_mix_g, v_norm_ffn_g, v_norm_final_g)))
    loss = gg[3, 0]

    def assemble(gain_rows, wlist):
        return (gain_rows[0:1], wlist[0], wlist[1], wlist[2], wlist[3], gain_rows[1:2],
                wlist[4], wlist[5], wlist[6], gain_rows[2])

    return (loss, grad_x[None], *assemble(gg, outs_g), *assemble(gd, outs_d), *assemble(gm, outs_m), *assemble(gv, outs_v))
```

```python
import functools
import math

import numpy as np
import jax
import jax.numpy as jnp
from jax import lax
from jax.experimental import pallas as pl
from jax.experimental.pallas import tpu as pltpu

F32, BF16 = jnp.float32, jnp.bfloat16
SDS = jax.ShapeDtypeStruct
MESH = pl.DeviceIdType.MESH

D_MODEL = 1024
PROJ_W = 9728
COLB = 512
N_COLB = PROJ_W // COLB
QA_B, KA_B, VA_B = 0, 3, 6
QR_B, KR_B = 9, 10
FFN_HID = 2816
N_SHARD = 4
HID_S = FFN_HID // N_SHARD
W_IN_S = PROJ_W // N_SHARD
DILATIONS = (1, 4, 16)
BLK = 128
RET_HEADS = 4
ROPE_THETA = 10000.0
NORM_EPS = 1e-6
ADAM_LR, ADAM_B1, ADAM_B2, ADAM_EPS, ADAM_WD, ADAM_STEP = 0.001, 0.9, 0.999, 1e-08, 0.01, 10
VMEM_LIMIT = 56 << 20


def _cparams(*sem):
    return pltpu.CompilerParams(dimension_semantics=sem or None, vmem_limit_bytes=VMEM_LIMIT)


def _dot(a, b):
    return jnp.dot(a, b, preferred_element_type=F32)


def _dot_nt(a, b):
    return lax.dot_general(a, b, (((1,), (1,)), ((), ())), preferred_element_type=F32)


def _dot_tn(a, b):
    return lax.dot_general(a, b, (((0,), (0,)), ((), ())), preferred_element_type=F32)


def _row_pieces(tm, sub=512):
    return [slice(i, i + sub) for i in range(0, tm, sub)]


def _sigmoid(z):
    return 0.5 * jnp.tanh(0.5 * z) + 0.5


ANY = pl.BlockSpec(memory_space=pl.ANY)


class _Exchange:
    def __init__(self, ins, out_shapes, aliases, n_sem, n_loc, build, stagger=False):
        self.ins, self.out_shapes, self.aliases = list(ins), list(out_shapes), dict(aliases)
        self.n_sem, self.n_loc, self.build, self.stagger = n_sem, n_loc, build, stagger

    def sems(self):
        return [pltpu.SemaphoreType.DMA((self.n_sem,)), pltpu.SemaphoreType.DMA((self.n_sem,)),
                pltpu.SemaphoreType.DMA((max(self.n_loc, 1),))]


def _carrier_call(body, args, *, out_shape, grid, in_specs, out_specs, scratch_shapes=(), sem, name, exchanges=(),
                  prefetch=None, in_out_aliases=None):
    out_shape, out_specs = tuple(out_shape), tuple(out_specs)
    n_in, n_out, n_scr = len(args), len(out_shape), len(scratch_shapes)
    n_pre = 0 if prefetch is None else 1
    x_args, x_outs, x_scr, spans = [], [], [], []
    aliases = {n_pre + a: o for a, o in (in_out_aliases or {}).items()}
    for ex in exchanges:
        i0, o0 = len(x_args), len(x_outs)
        for a, o in ex.aliases.items():
            aliases[n_pre + n_in + i0 + a] = n_out + o0 + o
        x_args += ex.ins
        x_outs += ex.out_shapes
        x_scr += ex.sems()
        spans.append((i0, len(ex.ins), o0, len(ex.out_shapes)))
    nx_in, nx_out = len(x_args), len(x_outs)

    def wrapped(*refs):
        refs = refs[n_pre:]
        ins, xin = refs[:n_in], refs[n_in:n_in + nx_in]
        o_base = n_in + nx_in
        outs, xout = refs[o_base:o_base + n_out], refs[o_base + n_out:o_base + n_out + nx_out]
        s_base = o_base + n_out + nx_out
        scr, xs = refs[s_base:s_base + n_scr], refs[s_base + n_scr:]

        def built(e):
            i0, ni, o0, no = spans[e]
            return exchanges[e].build(xin[i0:i0 + ni], xout[o0:o0 + no], *xs[3 * e:3 * e + 3])

        if exchanges:
            n_steps = math.prod(grid)
            step = functools.reduce(lambda acc, k: acc * grid[k] + pl.program_id(k), range(len(grid)), 0)
            last = step == n_steps - 1
            for e, ex in enumerate(exchanges):
                n_cp = len(built(e)[0])
                spread = max(n_steps // 2, 1) if ex.stagger else 1
                when = [(i * spread) // n_cp for i in range(n_cp)]
                for t in sorted(set(when)):
                    @pl.when(step == t)
                    def _(e=e, t=t, when=when):
                        for i, cp in enumerate(built(e)[0]):
                            if when[i] == t:
                                cp.start()

        body(*ins, *outs, *scr)

        if exchanges:
            @pl.when(last)
            def _():
                for e in range(len(exchanges)):
                    for w in built(e)[1]:
                        w()

    all_in, all_out = list(in_specs) + [ANY] * nx_in, out_specs + tuple([ANY] * nx_out)
    all_scr = list(scratch_shapes) + x_scr
    cparams = _cparams(*(sem if not exchanges else ("arbitrary",) * len(grid)))
    if prefetch is None:
        res = pl.pallas_call(wrapped, out_shape=out_shape + tuple(x_outs), grid=grid, in_specs=all_in, out_specs=all_out,
                             scratch_shapes=all_scr, input_output_aliases=aliases, compiler_params=cparams,
                             name=name)(*args, *x_args)
    else:
        gs = pltpu.PrefetchScalarGridSpec(num_scalar_prefetch=1, grid=grid, in_specs=all_in, out_specs=all_out,
                                          scratch_shapes=all_scr)
        res = pl.pallas_call(wrapped, out_shape=out_shape + tuple(x_outs), grid_spec=gs, input_output_aliases=aliases,
                             compiler_params=cparams, name=name)(prefetch, *args, *x_args)
    xres = [tuple(res[n_out + o0:n_out + o0 + no]) for (_, _, o0, no) in spans]
    return tuple(res[:n_out]), xres


def _tables(S):
    f32 = np.float32
    pos = np.arange(S, dtype=f32)
    lane = np.arange(128)
    inv = (f32(ROPE_THETA) ** (-np.arange(0, 64, 2, dtype=f32) / f32(64))).astype(f32)
    ang = (pos[:, None] * inv[None, :]).astype(np.float64)
    idx = (lane % 64) % 32
    c, s = np.cos(ang)[:, idx], np.sin(ang)[:, idx]
    first = ((lane % 64) < 32)[None, :]
    rope = np.stack([c, np.where(first, 0.0, s), np.where(first, -s, 0.0)])
    base = (f32(1.0) / (f32(ROPE_THETA) ** np.linspace(0.0, 1.0, 64, dtype=f32))).astype(f32)
    ang2 = (pos[:, None] * base[None, :]).astype(np.float64)
    c2, s2 = np.cos(ang2)[:, lane // 2], np.sin(ang2)[:, lane // 2]
    even = (lane % 2 == 0)[None, :]
    th = np.stack([c2, np.where(even, 0.0, s2), np.where(even, -s2, 0.0)])
    return np.stack([rope, th, th * (128 ** -0.5)]).astype(f32)


def _rot(a, c, sa, sb, shift):
    return a * c + pltpu.roll(a, shift, 1) * sa + pltpu.roll(a, 128 - shift, 1) * sb


def _unrot(g, c, sa, sb, shift):
    return g * c + pltpu.roll(g * sa, 128 - shift, 1) + pltpu.roll(g * sb, shift, 1)


def _ret_consts():
    h = np.arange(RET_HEADS, dtype=np.float64)
    log_g = np.log1p(-(2.0 ** (-5.0 - h)))
    idx = np.arange(BLK, dtype=np.float64)
    diff = idx[:, None] - idx[None, :]
    dmask = np.where(diff[None] >= 0, np.exp(np.maximum(diff, 0.0)[None] * log_g[:, None, None]), 0.0)
    zeta = np.exp((BLK - 1 - idx)[None, :] * log_g[:, None])
    xi = np.exp((idx + 1.0)[None, :] * log_g[:, None])
    dec = np.exp(BLK * log_g)
    rep = lambda v: np.broadcast_to(v[:, :, None], (RET_HEADS, BLK, 128))
    return (jnp.asarray(dmask, F32), jnp.asarray(rep(zeta), F32), jnp.asarray(rep(xi), F32),
            jnp.asarray(np.broadcast_to(dec[:, None, None], (RET_HEADS, 8, 256)), F32))


def _rms_fwd(x, g):
    S = x.shape[0]
    tm = 512

    def body(x_ref, g_ref, h_ref, ht_ref):
        xv = x_ref[...]
        r = lax.rsqrt(jnp.mean(xv * xv, axis=-1, keepdims=True) + NORM_EPS)
        h = xv * r * g_ref[...]
        h_ref[...] = h.astype(BF16)
        ht_ref[...] = h.T.astype(BF16)

    return pl.pallas_call(
        body, out_shape=(SDS((S, D_MODEL), BF16), SDS((D_MODEL, S), BF16)), grid=(S // tm,),
        in_specs=[pl.BlockSpec((tm, D_MODEL), lambda i: (i, 0)), pl.BlockSpec((1, D_MODEL), lambda i: (0, 0))],
        out_specs=(pl.BlockSpec((tm, D_MODEL), lambda i: (i, 0)), pl.BlockSpec((D_MODEL, tm), lambda i: (0, i))),
        compiler_params=_cparams("parallel"), name="rms_fwd")(x, g)


def _in_proj(h, w_in, tab, exchanges=()):
    S = h.shape[0]
    tm = min(S, 4096)

    def body(h_ref, w_ref, t_ref, o_ref):
        j = pl.program_id(1)
        is_rope = j < 6
        is_theta = (j == QR_B) | (j == KR_B)
        sub = 512

        def rotated(shift):
            for i in range(tm // sub):
                rows = slice(i * sub, (i + 1) * sub)
                acc = _dot(h_ref[rows, :], w_ref[...])
                c, sa, sb = t_ref[0, 0, rows, :], t_ref[0, 1, rows, :], t_ref[0, 2, rows, :]
                for k in range(COLB // 128):
                    sl = slice(k * 128, (k + 1) * 128)
                    o_ref[rows, sl] = _rot(acc[:, sl], c, sa, sb, shift).astype(BF16)

        @pl.when(is_rope)
        def _():
            rotated(32)

        @pl.when(is_theta)
        def _():
            rotated(1)

        @pl.when(jnp.logical_not(is_rope | is_theta))
        def _():
            o_ref[...] = _dot(h_ref[...], w_ref[...]).astype(BF16)

    def tab_map(i, j):
        return (jnp.where(j == QR_B, 1, jnp.where(j == KR_B, 2, 0)), 0, i, 0)

    (proj,), xres = _carrier_call(
        body, (h, w_in, tab), out_shape=(SDS((S, PROJ_W), BF16),), grid=(S // tm, N_COLB),
        in_specs=[pl.BlockSpec((tm, D_MODEL), lambda i, j: (i, 0)),
                  pl.BlockSpec((D_MODEL, COLB), lambda i, j: (0, j)),
                  pl.BlockSpec((1, 3, tm, 128), tab_map)],
        out_specs=(pl.BlockSpec((tm, COLB), lambda i, j: (i, j)),),
        sem=("parallel", "arbitrary"), name="in_proj", exchanges=exchanges)
    return proj, xres


def _band_mask(n):
    qi = lax.broadcasted_iota(jnp.int32, (BLK, 2 * BLK), 0)
    kj = lax.broadcasted_iota(jnp.int32, (BLK, 2 * BLK), 1)
    dist = BLK + qi - kj
    return (dist >= 0) & (dist <= BLK) & ((kj >= BLK) | (n > 0))


def _qkv_col(d, gi):
    if d == 1:
        return lambda t, r: 3 * t + gi
    return lambda t, r: 3 * r + t


def _attn_fwd(qkv, d, gi, exchanges=()):
    L = qkv.shape[0]
    nb = L // BLK

    def body(q_ref, kc_ref, kp_ref, vc_ref, vp_ref, o_ref, lse_ref):
        n = pl.program_id(1)
        mask = _band_mask(n)
        mask2 = jnp.concatenate([mask, mask], axis=0)
        lane = lax.broadcasted_iota(jnp.int32, (BLK, 128), 1)
        lo = lane < 64
        lse_all = jnp.zeros((BLK, 128), F32)
        chunks = [slice(c * 128, (c + 1) * 128) for c in range(4)]
        scores, vals = [], []
        for sl in chunks:
            q = q_ref[:, sl]
            k = jnp.concatenate([kp_ref[:, sl], kc_ref[:, sl]], axis=0)
            vals.append(jnp.concatenate([vp_ref[:, sl], vc_ref[:, sl]], axis=0))
            q2 = jnp.concatenate([jnp.where(lo, q, jnp.zeros_like(q)), jnp.where(lo, jnp.zeros_like(q), q)], axis=0)
            scores.append(_dot_nt(q2, k))
        probs = []
        for c, s in enumerate(scores):
            s = jnp.where(mask2, s * 0.125, jnp.float32(-1e30))
            m = jnp.max(s, axis=-1, keepdims=True)
            p = jnp.exp(s - m)
            l = jnp.sum(p, axis=-1, keepdims=True)
            probs.append((p / l).astype(BF16))
            lse = m + jnp.log(l)
            lse_all = jnp.where(lane // 16 == 2 * c, lse[:BLK], jnp.where(lane // 16 == 2 * c + 1, lse[BLK:], lse_all))
        for sl, p, v in zip(chunks, probs, vals):
            o2 = _dot(p, v)
            o_ref[:, sl] = jnp.where(lo, o2[:BLK], o2[BLK:])
        lse_ref[...] = lse_all

    prev = lambda n: jnp.maximum(n - 1, 0)
    col = _qkv_col(d, gi)
    return _carrier_call(
        body, (qkv,) * 5, out_shape=(SDS((L, d * 512), F32), SDS((L, d * 128), F32)), grid=(d, nb),
        in_specs=[pl.BlockSpec((BLK, 512), lambda r, n: (n, col(0, r))),
                  pl.BlockSpec((BLK, 512), lambda r, n: (n, col(1, r))),
                  pl.BlockSpec((BLK, 512), lambda r, n: (prev(n), col(1, r))),
                  pl.BlockSpec((BLK, 512), lambda r, n: (n, col(2, r))),
                  pl.BlockSpec((BLK, 512), lambda r, n: (prev(n), col(2, r)))],
        out_specs=(pl.BlockSpec((BLK, 512), lambda r, n: (n, r)),
                   pl.BlockSpec((BLK, 128), lambda r, n: (n, r))),
        sem=("parallel", "arbitrary"), name=f"attn_fwd_g{gi}", exchanges=exchanges)


def _qkv_to_sub(proj, d, gi):
    S = proj.shape[0]
    tm = 512
    n = tm // d

    def body(q_ref, k_ref, v_ref, o_ref, scr):
        for t, ref in enumerate((q_ref, k_ref, v_ref)):
            for c in range(4):
                scr[c] = ref[:, c * 128:(c + 1) * 128].astype(F32)
            for r in range(d):
                for c in range(4):
                    col = (3 * r + t) * 512 + c * 128
                    o_ref[:, col:col + 128] = scr[c, pl.ds(r, n, stride=d), :].astype(BF16)

    return pl.pallas_call(
        body, out_shape=SDS((S // d, d * 1536), BF16), grid=(S // tm,),
        in_specs=[pl.BlockSpec((tm, 512), lambda i, b=b: (i, b + gi)) for b in (QA_B, KA_B, VA_B)],
        out_specs=pl.BlockSpec((n, d * 1536), lambda i: (i, 0)),
        scratch_shapes=[pltpu.VMEM((4, tm, 128), F32)],
        compiler_params=_cparams("parallel"), name=f"qkv_to_sub_g{gi}")(proj, proj, proj)


def _attn_merge(os_, lses):
    S = os_[0].shape[0]
    tm = 512

    def body(o0, o1, o2, l0, l1, l2, att_ref, lt_ref, so1, so2, sl1, sl2):
        lo = lax.broadcasted_iota(jnp.int32, (tm, 128), 1) < 64

        def natural(ref, d, scr, width):
            nch = width // 128
            if d == 1:
                return [ref[:, c * 128:(c + 1) * 128] for c in range(nch)]
            for r in range(d):
                for c in range(nch):
                    scr[c, pl.ds(r, tm // d, stride=d), :] = ref[:, r * width + c * 128:r * width + (c + 1) * 128]
            return [scr[c] for c in range(nch)]

        ls = [natural(l, d, s, 128)[0] for l, d, s in zip((l0, l1, l2), DILATIONS, (None, sl1, sl2))]
        m = jnp.maximum(jnp.maximum(ls[0], ls[1]), ls[2])
        es = [jnp.exp(v - m) for v in ls]
        z = es[0] + es[1] + es[2]
        lt_ref[...] = m + jnp.log(z)
        ws = [e / z for e in es]
        o_nat = [natural(o, d, s, 512) for o, d, s in zip((o0, o1, o2), DILATIONS, (None, so1, so2))]
        for c in range(4):
            acc = jnp.zeros((tm, 128), F32)
            for g in range(3):
                w_lo = jnp.broadcast_to(ws[g][:, 32 * c:32 * c + 1], (tm, 128))
                w_hi = jnp.broadcast_to(ws[g][:, 32 * c + 16:32 * c + 17], (tm, 128))
                acc = acc + jnp.where(lo, w_lo, w_hi) * o_nat[g][c]
            att_ref[:, c * 128:(c + 1) * 128] = acc.astype(BF16)

    sub = lambda w: [pl.BlockSpec((tm // d, d * w), lambda i: (i, 0)) for d in DILATIONS]
    return pl.pallas_call(
        body, out_shape=(SDS((S, 512), BF16), SDS((S, 128), F32)), grid=(S // tm,),
        in_specs=sub(512) + sub(128),
        out_specs=(pl.BlockSpec((tm, 512), lambda i: (i, 0)), pl.BlockSpec((tm, 128), lambda i: (i, 0))),
        scratch_shapes=[pltpu.VMEM((4, tm, 128), F32), pltpu.VMEM((4, tm, 128), F32),
                        pltpu.VMEM((1, tm, 128), F32), pltpu.VMEM((1, tm, 128), F32)],
        compiler_params=_cparams("parallel"), name="attn_merge")(*os_, *lses)


def _assemble_dproj(att_grads, dproj):
    S = dproj.shape[0]
    tm = 256

    def body(*refs):
        a = [refs[3 * t:3 * t + 3] for t in range(3)]
        dp_prev, o_ref, scr = refs[9:]
        for t in range(3):
            for g, d in enumerate(DILATIONS):
                base = (3 * t + g) * COLB
                if d == 1:
                    o_ref[:, base:base + COLB] = a[t][g][...]
                    continue
                for c in range(4):
                    for r in range(d):
                        scr[c, pl.ds(r, tm // d, stride=d), :] = a[t][g][:, r * 512 + c * 128:r * 512 + (c + 1) * 128].astype(F32)
                    o_ref[:, base + c * 128:base + (c + 1) * 128] = scr[c].astype(BF16)

    sub = [pl.BlockSpec((tm // d, d * 512), lambda i: (i, 0)) for d in DILATIONS]
    flat = [att_grads[t][g] for t in range(3) for g in range(3)]
    return pl.pallas_call(
        body, out_shape=SDS((S, PROJ_W), BF16), grid=(S // tm,),
        in_specs=sub * 3 + [ANY], out_specs=pl.BlockSpec((tm, 9 * COLB), lambda i: (i, 0)),
        scratch_shapes=[pltpu.VMEM((4, tm, 128), F32)], input_output_aliases={9: 0},
        compiler_params=_cparams("parallel"), name="assemble_dproj")(*flat, dproj)


def _ret_fwd(proj, consts, exchanges=()):
    S = proj.shape[0]
    nc = S // BLK
    dmask, zeta, xi, dec = consts

    def body(q_ref, k_ref, v0_ref, v1_ref, g0_ref, g1_ref, dm_ref, z_ref, x_ref, dec_ref,
             y_ref, rn_ref, rs_ref, st_ref, R):
        @pl.when(pl.program_id(0) == 0)
        def _():
            R[...] = jnp.zeros_like(R)

        lane16 = lax.broadcasted_iota(jnp.int32, (BLK, 128), 1) // 16
        rs_all = jnp.zeros((BLK, 128), F32)
        first = []
        for h in range(RET_HEADS):
            hs = slice(h * 128, (h + 1) * 128)
            q, k = q_ref[:, hs], k_ref[:, hs]
            v = (v0_ref if h < 2 else v1_ref)[:, (h % 2) * 256:(h % 2 + 1) * 256]
            Rb = R[h].astype(BF16)
            st_ref[h] = Rb
            kz = (k.astype(F32) * z_ref[h]).astype(BF16)
            first.append((v, _dot_nt(q, k), _dot((q.astype(F32) * x_ref[h]).astype(BF16), Rb), _dot_tn(kz, v)))
        masked = [(s * dm_ref[h]).astype(BF16) for h, (_, s, _, _) in enumerate(first)]
        for h in range(RET_HEADS):
            vs = slice((h % 2) * 256, (h % 2 + 1) * 256)
            os_ = slice(h * 256, (h + 1) * 256)
            v, _, cross, kv = first[h]
            o = _dot(masked[h], v) + cross
            R[h] = R[h] * dec_ref[h, 0:1, :] + kv
            mu = jnp.mean(o, axis=-1, keepdims=True)
            oc = o - mu
            rstd = lax.rsqrt(jnp.mean(oc * oc, axis=-1, keepdims=True) + NORM_EPS)
            rn = oc * rstd
            gr = (g0_ref if h < 2 else g1_ref)[:, vs].astype(F32)
            y_ref[:, os_] = (rn * gr * _sigmoid(gr)).astype(BF16)
            rn_ref[:, os_] = rn.astype(BF16)
            rs_all = jnp.where(lane16 == h, rstd, rs_all)
        rs_ref[...] = rs_all

    cst = lambda shape: pl.BlockSpec(shape, lambda c: (0, 0, 0))
    blk = lambda j: pl.BlockSpec((BLK, 512), lambda c: (c, j))
    return _carrier_call(
        body, (proj, proj, proj, proj, proj, proj, dmask, zeta, xi, dec),
        out_shape=(SDS((S, 1024), BF16), SDS((S, 1024), BF16), SDS((S, 128), F32), SDS((RET_HEADS, nc, BLK, 256), BF16)),
        grid=(nc,),
        in_specs=[blk(QR_B), blk(KR_B), blk(11), blk(12), blk(13), blk(14),
                  cst((RET_HEADS, BLK, BLK)), cst((RET_HEADS, BLK, 128)), cst((RET_HEADS, BLK, 128)), cst((RET_HEADS, 8, 256))],
        out_specs=(pl.BlockSpec((BLK, 1024), lambda c: (c, 0)), pl.BlockSpec((BLK, 1024), lambda c: (c, 0)),
                   pl.BlockSpec((BLK, 128), lambda c: (c, 0)),
                   pl.BlockSpec((RET_HEADS, None, BLK, 256), lambda c: (0, c, 0, 0))),
        scratch_shapes=[pltpu.VMEM((RET_HEADS, BLK, 256), F32)],
        sem=("arbitrary",), name="ret_fwd", exchanges=exchanges)


def _branch_merge(att, yrin, proj, wa, wr):
    S = att.shape[0]
    tm = min(S, 2048)

    def body(a_ref, y_ref, ga_ref, gr_ref, wa_ref, wr_ref, m_ref, ya_ref, yr_ref):
        for rows in _row_pieces(tm):
            ya = _dot(a_ref[rows, :], wa_ref[...])
            yr = _dot(y_ref[rows, :], wr_ref[...])
            m_ref[rows, :] = (_sigmoid(ga_ref[rows, :].astype(F32)) * ya
                              + _sigmoid(gr_ref[rows, :].astype(F32)) * yr).astype(BF16)
            ya_ref[rows, :] = ya.astype(BF16)
            yr_ref[rows, :] = yr.astype(BF16)

    ospec = pl.BlockSpec((tm, 512), lambda i, j: (i, j))
    return pl.pallas_call(
        body, out_shape=(SDS((S, D_MODEL), BF16),) * 3, grid=(S // tm, 2),
        in_specs=[pl.BlockSpec((tm, 512), lambda i, j: (i, 0)), pl.BlockSpec((tm, 1024), lambda i, j: (i, 0)),
                  pl.BlockSpec((tm, 512), lambda i, j: (i, 15 + j)), pl.BlockSpec((tm, 512), lambda i, j: (i, 17 + j)),
                  pl.BlockSpec((512, 512), lambda i, j: (0, j)), pl.BlockSpec((1024, 512), lambda i, j: (0, j))],
        out_specs=(ospec, ospec, ospec),
        compiler_params=_cparams("parallel", "arbitrary"), name="branch_merge")(att, yrin, proj, proj, wa, wr)


def _out_proj(merged, wo, x, g2, exchanges=()):
    S = x.shape[0]
    tm = 1024

    def body(m_ref, w_ref, x_ref, g_ref, x1_ref, h2_ref):
        x1 = x_ref[...] + _dot(m_ref[...], w_ref[...])
        x1_ref[...] = x1
        r = lax.rsqrt(jnp.mean(x1 * x1, axis=-1, keepdims=True) + NORM_EPS)
        h2_ref[...] = (x1 * r * g_ref[...]).astype(BF16)

    row = pl.BlockSpec((tm, D_MODEL), lambda i: (i, 0))
    return _carrier_call(
        body, (merged, wo, x, g2), out_shape=(SDS((S, D_MODEL), F32), SDS((S, D_MODEL), BF16)), grid=(S // tm,),
        in_specs=[row, pl.BlockSpec((D_MODEL, D_MODEL), lambda i: (0, 0)), row, pl.BlockSpec((1, D_MODEL), lambda i: (0, 0))],
        out_specs=(row, row), sem=("parallel",), name="out_proj", exchanges=exchanges)


def _ffn_up(h2, wg, wu, exchanges=()):
    S = h2.shape[0]
    tm = min(S, 2048)

    def body(h_ref, wg_ref, wu_ref, g_ref, u_ref, a_ref):
        for rows in _row_pieces(tm):
            hv = h_ref[rows, :]
            g = _dot(hv, wg_ref[...])
            u = _dot(hv, wu_ref[...])
            g_ref[rows, :] = g.astype(BF16)
            u_ref[rows, :] = u.astype(BF16)
            a_ref[rows, :] = (g * _sigmoid(g) * u).astype(BF16)

    wspec = pl.BlockSpec((None, D_MODEL, HID_S), lambda i, s: (s, 0, 0))
    ospec = pl.BlockSpec((None, tm, HID_S), lambda i, s: (s, i, 0))
    return _carrier_call(
        body, (h2, wg, wu), out_shape=(SDS((N_SHARD, S, HID_S), BF16),) * 3, grid=(S // tm, N_SHARD),
        in_specs=[pl.BlockSpec((tm, D_MODEL), lambda i, s: (i, 0)), wspec, wspec],
        out_specs=(ospec, ospec, ospec),
        sem=("parallel", "arbitrary"), name="ffn_up", exchanges=exchanges)


def _ffn_down_loss(act, wd, x1, g3, tgt):
    S = x1.shape[0]
    tm = 512

    def body(a_ref, w_ref, x_ref, g_ref, t_ref, dx_ref, dxb_ref, dg_ref, ls_ref):
        @pl.when(pl.program_id(0) == 0)
        def _():
            dg_ref[...] = jnp.zeros_like(dg_ref)
            ls_ref[...] = jnp.zeros_like(ls_ref)

        g = g_ref[...]
        for rows in _row_pieces(tm, 256):
            y = _dot(a_ref[0, rows, :], w_ref[0])
            for s in range(1, N_SHARD):
                y = y + _dot(a_ref[s, rows, :], w_ref[s])
            x2 = x_ref[rows, :] + y
            r = lax.rsqrt(jnp.mean(x2 * x2, axis=-1, keepdims=True) + NORM_EPS)
            xh = x2 * r
            err = xh * g - t_ref[rows, :]
            ls_ref[...] += jnp.sum(jnp.sum(err * err, axis=-1, keepdims=True), axis=0, keepdims=True) * (0.5 / D_MODEL)
            dy = err * (1.0 / D_MODEL)
            dg_ref[...] += jnp.sum(dy * xh, axis=0, keepdims=True)
            dxh = dy * g
            dx = r * (dxh - xh * jnp.mean(dxh * xh, axis=-1, keepdims=True))
            dx_ref[rows, :] = dx
            dxb_ref[rows, :] = dx.astype(BF16)

    row = pl.BlockSpec((tm, D_MODEL), lambda i: (i, 0))
    vec = pl.BlockSpec((1, D_MODEL), lambda i: (0, 0))
    return pl.pallas_call(
        body, out_shape=(SDS((S, D_MODEL), F32), SDS((S, D_MODEL), BF16), SDS((1, D_MODEL), F32), SDS((8, 128), F32)),
        grid=(S // tm,),
        in_specs=[pl.BlockSpec((N_SHARD, tm, HID_S), lambda i: (0, i, 0)),
                  pl.BlockSpec((N_SHARD, HID_S, D_MODEL), lambda i: (0, 0, 0), pipeline_mode=pl.Buffered(1)),
                  row, vec, row],
        out_specs=(row, row, vec, pl.BlockSpec((8, 128), lambda i: (0, 0))),
        compiler_params=_cparams("arbitrary"), name="ffn_down_loss")(act, wd, x1, g3, tgt)


def _ffn_bwd(dx2b, dx2, wd, wg, wu, gte, up, x1, g2):
    S = x1.shape[0]
    tm = 256

    def body(d_ref, dx2_ref, wd_ref, wg_ref, wu_ref, g_ref, u_ref, x_ref, gn_ref,
             dg_ref, du_ref, dx_ref, dxb_ref, dgn_ref):
        @pl.when(pl.program_id(0) == 0)
        def _():
            dgn_ref[...] = jnp.zeros_like(dgn_ref)

        d = d_ref[...]
        dacts = [_dot_nt(d, wd_ref[s]) for s in range(N_SHARD)]
        dgs, dus = [], []
        for s, da in enumerate(dacts):
            g = g_ref[s].astype(F32)
            sg = _sigmoid(g)
            dgs.append((da * u_ref[s].astype(F32) * sg * (1.0 + g * (1.0 - sg))).astype(BF16))
            dus.append((da * g * sg).astype(BF16))
            dg_ref[s] = dgs[s]
            du_ref[s] = dus[s]
        dh = _dot_nt(dgs[0], wg_ref[0]) + _dot_nt(dus[0], wu_ref[0])
        for s in range(1, N_SHARD):
            dh = dh + _dot_nt(dgs[s], wg_ref[s]) + _dot_nt(dus[s], wu_ref[s])
        xv = x_ref[...]
        r = lax.rsqrt(jnp.mean(xv * xv, axis=-1, keepdims=True) + NORM_EPS)
        xh = xv * r
        dgn_ref[...] += jnp.sum(dh * xh, axis=0, keepdims=True)
        dxh = dh * gn_ref[...]
        dx = dx2_ref[...] + r * (dxh - xh * jnp.mean(dxh * xh, axis=-1, keepdims=True))
        dx_ref[...] = dx
        dxb_ref[...] = dx.astype(BF16)

    row = pl.BlockSpec((tm, D_MODEL), lambda i: (i, 0))
    vec = pl.BlockSpec((1, D_MODEL), lambda i: (0, 0))
    aspec = pl.BlockSpec((N_SHARD, tm, HID_S), lambda i: (0, i, 0))
    resident = lambda shape: pl.BlockSpec(shape, lambda i: (0, 0, 0), pipeline_mode=pl.Buffered(1))
    return pl.pallas_call(
        body,
        out_shape=(SDS((N_SHARD, S, HID_S), BF16), SDS((N_SHARD, S, HID_S), BF16),
                   SDS((S, D_MODEL), F32), SDS((S, D_MODEL), BF16), SDS((1, D_MODEL), F32)),
        grid=(S // tm,),
        in_specs=[row, row, resident((N_SHARD, HID_S, D_MODEL)), resident((N_SHARD, D_MODEL, HID_S)),
                  resident((N_SHARD, D_MODEL, HID_S)), aspec, aspec, row, vec],
        out_specs=(aspec, aspec, row, row, vec),
        compiler_params=_cparams("arbitrary"), name="ffn_bwd")(dx2b, dx2, wd, wg, wu, gte, up, x1, g2)


def _wgrad(name, a, b, a_spec, b_spec, out_shape, out_spec, n_par, S):
    tk = 2048

    def body(a_ref, b_ref, o_ref):
        @pl.when(pl.program_id(1) == 0)
        def _():
            o_ref[...] = jnp.zeros_like(o_ref)

        o_ref[...] += _dot_tn(a_ref[...], b_ref[...])

    return pl.pallas_call(
        body, out_shape=SDS(out_shape, F32), grid=(n_par, S // tk),
        in_specs=[a_spec(tk), b_spec(tk)], out_specs=out_spec,
        compiler_params=_cparams("parallel", "arbitrary"), name=name)(a, b)


def _out_proj_bwd(dx1b, wo, proj, ya, yr, exchanges=()):
    S = dx1b.shape[0]
    tm = 512
    gate0 = 15 * COLB

    def body(d_ref, w_ref, ga_ref, gr_ref, ya_ref, yr_ref, dya_ref, dyr_ref, dp_ref):
        for rows in _row_pieces(tm, 256):
            dm = _dot_nt(d_ref[rows, :], w_ref[...])
            sa = _sigmoid(ga_ref[rows, :].astype(F32))
            sr = _sigmoid(gr_ref[rows, :].astype(F32))
            dya_ref[rows, :] = (dm * sa).astype(BF16)
            dyr_ref[rows, :] = (dm * sr).astype(BF16)
            dp_ref[rows, 0:D_MODEL] = (dm * ya_ref[rows, :].astype(F32) * sa * (1.0 - sa)).astype(BF16)
            dp_ref[rows, D_MODEL:2 * D_MODEL] = (dm * yr_ref[rows, :].astype(F32) * sr * (1.0 - sr)).astype(BF16)

    row = pl.BlockSpec((tm, D_MODEL), lambda i: (i, 0))
    cols = lambda c0, w: pl.BlockSpec((pl.Element(tm), pl.Element(w)), lambda i: (i * tm, c0))
    return _carrier_call(
        body, (dx1b, wo, proj, proj, ya, yr),
        out_shape=(SDS((S, D_MODEL), BF16), SDS((S, D_MODEL), BF16), SDS((S, PROJ_W), BF16)), grid=(S // tm,),
        in_specs=[row, pl.BlockSpec((D_MODEL, D_MODEL), lambda i: (0, 0), pipeline_mode=pl.Buffered(1)),
                  cols(gate0, D_MODEL), cols(gate0 + D_MODEL, D_MODEL), row, row],
        out_specs=(row, row, cols(gate0, 2 * D_MODEL)),
        sem=("parallel",), name="out_proj_bwd", exchanges=exchanges)


def _branch_bwd(dya, dyr, wa, wr, att):
    S = dya.shape[0]
    tm = 1024

    def body(da_ref, dr_ref, wa_ref, wr_ref, att_ref, datt_ref, rho_ref, dyi_ref):
        datt = _dot_nt(da_ref[...], wa_ref[...])
        datt_ref[...] = datt.astype(BF16)
        dyi_ref[...] = _dot_nt(dr_ref[...], wr_ref[...]).astype(BF16)
        prod = datt * att_ref[...].astype(F32)
        lane = lax.broadcasted_iota(jnp.int32, (tm, 128), 1)
        lo = lane < 64
        rho = jnp.zeros((tm, 128), F32)
        for c in range(4):
            pc = prod[:, c * 128:(c + 1) * 128]
            tot = jnp.sum(pc, axis=-1, keepdims=True)
            low = jnp.sum(jnp.where(lo, pc, 0.0), axis=-1, keepdims=True)
            rho = jnp.where(lane // 16 == 2 * c, low, jnp.where(lane // 16 == 2 * c + 1, tot - low, rho))
        rho_ref[...] = rho

    row = lambda w: pl.BlockSpec((tm, w), lambda i: (i, 0))
    return pl.pallas_call(
        body, out_shape=(SDS((S, 512), BF16), SDS((S, 128), F32), SDS((S, 1024), BF16)), grid=(S // tm,),
        in_specs=[row(1024), row(1024), pl.BlockSpec((512, 1024), lambda i: (0, 0)),
                  pl.BlockSpec((1024, 1024), lambda i: (0, 0)), row(512)],
        out_specs=(row(512), row(128), row(1024)),
        compiler_params=_cparams("parallel"), name="branch_bwd")(dya, dyr, wa, wr, att)


def _attn_bwd(qkv, datt, lse, rho, rtab, d, gi, exchanges=()):
    L = qkv.shape[0]
    nb = L // BLK
    T = d * nb

    def body(q_ref, kc_ref, kp_ref, vc_ref, vp_ref, do_ref, lse_ref, rho_ref, tq_ref, tk_ref,
             dq_ref, dk_ref, dv_ref, ck, cv):
        t = pl.program_id(0)
        n = jnp.minimum(t, T - 1) % nb

        @pl.when(t == 0)
        def _():
            ck[...] = jnp.zeros_like(ck)
            cv[...] = jnp.zeros_like(cv)

        def store_rot(ref, val, t_ref, c):
            sl = slice(c * 128, (c + 1) * 128)
            ref[:, sl] = _unrot(val, t_ref[0], t_ref[1], t_ref[2], 32).astype(BF16)

        @pl.when(t < T)
        def _():
            mask = _band_mask(n)
            mask2 = jnp.concatenate([mask, mask], axis=0)
            lo = lax.broadcasted_iota(jnp.int32, (BLK, 128), 1) < 64

            def stacked(a):
                return jnp.concatenate([jnp.where(lo, a, jnp.zeros_like(a)), jnp.where(lo, jnp.zeros_like(a), a)], axis=0)

            def head_cols(ref, c):
                return jnp.concatenate([jnp.broadcast_to(ref[:, 32 * c:32 * c + 1], (BLK, 2 * BLK)),
                                        jnp.broadcast_to(ref[:, 32 * c + 16:32 * c + 17], (BLK, 2 * BLK))], axis=0)

            ops, raw = [], []
            for c in range(4):
                sl = slice(c * 128, (c + 1) * 128)
                q2, do2 = stacked(q_ref[:, sl]), stacked(do_ref[:, sl])
                k = jnp.concatenate([kp_ref[:, sl], kc_ref[:, sl]], axis=0)
                v = jnp.concatenate([vp_ref[:, sl], vc_ref[:, sl]], axis=0)
                ops.append((q2, do2, k))
                raw.append((_dot_nt(q2, k), _dot_nt(do2, v)))
            grads = []
            for c, (s, dp) in enumerate(raw):
                p = jnp.where(mask2, jnp.exp(s * 0.125 - head_cols(lse_ref, c)), 0.0)
                grads.append(((p * (dp - head_cols(rho_ref, c)) * 0.125).astype(BF16), p.astype(BF16)))
            for c, ((q2, do2, k), (ds, pb)) in enumerate(zip(ops, grads)):
                sl = slice(c * 128, (c + 1) * 128)
                dq2 = _dot(ds, k)
                dq_c = jnp.where(lo, dq2[:BLK], dq2[BLK:])
                dk_c = _dot_tn(ds, q2)
                dv_c = _dot_tn(pb, do2)
                store_rot(dq_ref, dq_c, tq_ref, c)
                store_rot(dk_ref, ck[:, sl] + dk_c[:BLK], tk_ref, c)
                dv_ref[:, sl] = (cv[:, sl] + dv_c[:BLK]).astype(BF16)
                ck[:, sl] = dk_c[BLK:]
                cv[:, sl] = dv_c[BLK:]

        @pl.when(t == T)
        def _():
            for c in range(4):
                sl = slice(c * 128, (c + 1) * 128)
                store_rot(dk_ref, ck[:, sl], tk_ref, c)
            dv_ref[...] = cv[...].astype(BF16)

    blk_of = lambda t: (jnp.minimum(t, T - 1) % nb, jnp.minimum(t, T - 1) // nb)
    cur = lambda t: blk_of(t)
    prev = lambda t: (jnp.maximum(blk_of(t)[0] - 1, 0), blk_of(t)[1])
    fin = lambda t: blk_of(jnp.maximum(t - 1, 0))
    col = _qkv_col(d, gi)
    qkv_spec = lambda kind, which: pl.BlockSpec((BLK, 512), lambda t: (which(t)[0], col(kind, which(t)[1])))
    row_spec = lambda w, which: pl.BlockSpec((BLK, w), lambda t: which(t))
    tab_spec = lambda which: pl.BlockSpec((3, BLK, 128), lambda t: (0, *which(t)))
    return _carrier_call(
        body, (qkv, qkv, qkv, qkv, qkv, datt, lse, rho, rtab, rtab),
        out_shape=(SDS((L, d * 512), BF16),) * 3, grid=(T + 1,),
        in_specs=[qkv_spec(0, cur), qkv_spec(1, cur), qkv_spec(1, prev), qkv_spec(2, cur), qkv_spec(2, prev),
                  row_spec(512, cur), row_spec(128, cur), row_spec(128, cur), tab_spec(cur), tab_spec(fin)],
        out_specs=(row_spec(512, cur), row_spec(512, fin), row_spec(512, fin)),
        scratch_shapes=[pltpu.VMEM((BLK, 512), F32), pltpu.VMEM((BLK, 512), F32)],
        sem=("arbitrary",), name=f"attn_bwd_g{gi}", exchanges=exchanges)


def _ret_bwd(proj, rn, rstd, dyrin, states, tab, consts, dproj, exchanges=()):
    S = proj.shape[0]
    nc = S // BLK
    dmask, zeta, xi, dec = consts

    def body(q_ref, k_ref, v0_ref, v1_ref, g0_ref, g1_ref, rn_ref, rs_ref, dy_ref, st_ref, tq_ref, tk_ref,
             dm_ref, z_ref, x_ref, dec_ref, dp_prev, dp_ref, dR):
        dq_ref, dk_ref = dp_ref.at[:, 0:512], dp_ref.at[:, 512:1024]
        dv_ref, dgr_ref = dp_ref.at[:, 1024:2048], dp_ref.at[:, 2048:3072]

        @pl.when(pl.program_id(0) == 0)
        def _():
            dR[...] = jnp.zeros_like(dR)

        dobs = []
        for h in range(RET_HEADS):
            vs = slice((h % 2) * 256, (h % 2 + 1) * 256)
            os_ = slice(h * 256, (h + 1) * 256)
            gr = (g0_ref if h < 2 else g1_ref)[:, vs].astype(F32)
            sg = _sigmoid(gr)
            rn_v = rn_ref[:, os_].astype(F32)
            dyi = dy_ref[:, os_].astype(F32)
            dgr_ref[:, os_] = (dyi * rn_v * sg * (1.0 + gr * (1.0 - sg))).astype(BF16)
            drn = dyi * gr * sg
            rstd = jnp.broadcast_to(rs_ref[:, 16 * h:16 * h + 1], (BLK, 256))
            do = rstd * (drn - jnp.mean(drn, axis=-1, keepdims=True) - rn_v * jnp.mean(drn * rn_v, axis=-1, keepdims=True))
            dobs.append(do.astype(BF16))
        first = []
        for h in range(RET_HEADS):
            hs = slice(h * 128, (h + 1) * 128)
            q, k = q_ref[:, hs], k_ref[:, hs]
            v = (v0_ref if h < 2 else v1_ref)[:, (h % 2) * 256:(h % 2 + 1) * 256]
            dob, dRb = dobs[h], dR[h].astype(BF16)
            kz = (k.astype(F32) * z_ref[h]).astype(BF16)
            qx = (q.astype(F32) * x_ref[h]).astype(BF16)
            first.append((q, k, _dot_nt(q, k), _dot_nt(dob, v), _dot(kz, dRb), _dot_nt(dob, st_ref[h]),
                          _dot_nt(v, dRb), _dot_tn(qx, dob)))
        masked = [((s * dm_ref[h]).astype(BF16), (dsr * dm_ref[h]).astype(BF16))
                  for h, (_, _, s, dsr, _, _, _, _) in enumerate(first)]
        for h in range(RET_HEADS):
            hs = slice(h * 128, (h + 1) * 128)
            os_ = slice(h * 256, (h + 1) * 256)
            q, k, _, _, dv_state, dq_state, dk_state, dr_new = first[h]
            sD, dS = masked[h]
            dv_ref[:, os_] = (_dot_tn(sD, dobs[h]) + dv_state).astype(BF16)
            dq = _dot(dS, k) + dq_state * x_ref[h]
            dk = _dot_tn(dS, q) + dk_state * z_ref[h]
            dR[h] = dR[h] * dec_ref[h, 0:1, :] + dr_new
            dq_ref[:, hs] = _unrot(dq, tq_ref[0], tq_ref[1], tq_ref[2], 1).astype(BF16)
            dk_ref[:, hs] = _unrot(dk, tk_ref[0], tk_ref[1], tk_ref[2], 1).astype(BF16)

    rc = lambda c: nc - 1 - c
    cst = lambda shape: pl.BlockSpec(shape, lambda c: (0, 0, 0))
    blk = lambda j: pl.BlockSpec((BLK, 512), lambda c: (rc(c), j))
    row = lambda w: pl.BlockSpec((BLK, w), lambda c: (rc(c), 0))
    (dproj,), xres = _carrier_call(
        body, (proj, proj, proj, proj, proj, proj, rn, rstd, dyrin, states, tab, tab, dmask, zeta, xi, dec, dproj),
        out_shape=(SDS((S, PROJ_W), BF16),), grid=(nc,),
        in_specs=[blk(QR_B), blk(KR_B), blk(11), blk(12), blk(13), blk(14), row(1024), row(128), row(1024),
                  pl.BlockSpec((RET_HEADS, None, BLK, 256), lambda c: (0, rc(c), 0, 0)),
                  pl.BlockSpec((None, 3, BLK, 128), lambda c: (1, 0, rc(c), 0)),
                  pl.BlockSpec((None, 3, BLK, 128), lambda c: (2, 0, rc(c), 0)),
                  cst((RET_HEADS, BLK, BLK)), cst((RET_HEADS, BLK, 128)), cst((RET_HEADS, BLK, 128)), cst((RET_HEADS, 8, 256)),
                  ANY],
        out_specs=(pl.BlockSpec((pl.Element(BLK), pl.Element(6 * COLB)), lambda c: (rc(c) * BLK, QR_B * COLB)),),
        scratch_shapes=[pltpu.VMEM((RET_HEADS, BLK, 256), F32)],
        sem=("arbitrary",), name="ret_bwd", exchanges=exchanges, in_out_aliases={16: 0})
    return dproj, xres


def _wgrad_in_half(ht, dproj, sidx, kept, exchanges=()):
    S = dproj.shape[0]
    tk = 2048
    half = (lambda sx: sx[4]) if kept else (lambda sx: 1 - sx[4])

    def body(a_ref, b_ref, o_ref):
        @pl.when(pl.program_id(1) == 0)
        def _():
            o_ref[...] = jnp.zeros_like(o_ref)

        o_ref[...] += _dot(a_ref[...], b_ref[...])

    (g,), xres = _carrier_call(
        body, (ht, dproj), out_shape=(SDS((D_MODEL // 2, PROJ_W), F32),), grid=(N_SHARD, S // tk),
        in_specs=[pl.BlockSpec((D_MODEL // 2, tk), lambda s, k, sx: (half(sx), k)),
                  pl.BlockSpec((tk, W_IN_S), lambda s, k, sx: (k, s))],
        out_specs=(pl.BlockSpec((D_MODEL // 2, W_IN_S), lambda s, k, sx: (0, s)),),
        sem=("parallel", "arbitrary"), name="wgrad_in_kept" if kept else "wgrad_in_sent", exchanges=exchanges,
        prefetch=sidx)
    return g, xres


def _in_proj_bwd(dproj, w_in, x, g1, dx1, exchanges=()):
    S = x.shape[0]
    tm = 1024

    def body(d_ref, w_ref, x_ref, g_ref, dx1_ref, dx_ref, dgn_ref, acc):
        i, s = pl.program_id(0), pl.program_id(1)

        @pl.when(s == 0)
        def _():
            acc[...] = jnp.zeros_like(acc)

        @pl.when((i == 0) & (s == 0))
        def _():
            dgn_ref[...] = jnp.zeros_like(dgn_ref)

        acc[...] += _dot_nt(d_ref[...], w_ref[...])

        @pl.when(s == N_SHARD - 1)
        def _():
            xv = x_ref[...]
            r = lax.rsqrt(jnp.mean(xv * xv, axis=-1, keepdims=True) + NORM_EPS)
            xh = xv * r
            dh = acc[...]
            dgn_ref[...] += jnp.sum(dh * xh, axis=0, keepdims=True)
            dxh = dh * g_ref[...]
            dx_ref[...] = dx1_ref[...] + r * (dxh - xh * jnp.mean(dxh * xh, axis=-1, keepdims=True))

    row = pl.BlockSpec((tm, D_MODEL), lambda i, s: (i, 0))
    vec = pl.BlockSpec((1, D_MODEL), lambda i, s: (0, 0))
    (gx, dg), xres = _carrier_call(
        body, (dproj, w_in, x, g1, dx1),
        out_shape=(SDS((S, D_MODEL), F32), SDS((1, D_MODEL), F32)), grid=(S // tm, N_SHARD),
        in_specs=[pl.BlockSpec((tm, W_IN_S), lambda i, s: (i, s)),
                  pl.BlockSpec((D_MODEL, W_IN_S), lambda i, s: (0, s)), row, vec, row],
        out_specs=(row, vec), scratch_shapes=[pltpu.VMEM((tm, D_MODEL), F32)],
        sem=("arbitrary", "arbitrary"), name="in_proj_bwd", exchanges=exchanges)
    return gx, dg, xres


def _sub_view(a, d):
    S, W = a.shape
    return a.reshape(S // d, d * W)


def _step(x, tgt, g1, g2, g3, comm):
    S = x.shape[0]
    tab_np = _tables(S)
    tab = jnp.asarray(tab_np)
    consts = _ret_consts()

    h, ht = _rms_fwd(x, g1)
    w_in = comm.w_in()
    proj, xres = _in_proj(h, w_in, tab, comm.carry("in_proj"))
    comm.took("in_proj", xres)
    qkvs, o_parts, lse_parts = [], [], []
    for gi, d in enumerate(DILATIONS):
        qkv = proj if d == 1 else _qkv_to_sub(proj, d, gi)
        (o_g, lse_g), xres = _attn_fwd(qkv, d, gi, comm.carry(f"attn_fwd_g{gi}"))
        comm.took(f"attn_fwd_g{gi}", xres)
        qkvs.append(qkv)
        o_parts.append(o_g)
        lse_parts.append(lse_g)
    att, lse_tot = _attn_merge(o_parts, lse_parts)
    (yrin, rn, rstd, states), xres = _ret_fwd(proj, consts, comm.carry("ret_fwd"))
    comm.took("ret_fwd", xres)
    wa, wr, wo = comm.weight(1), comm.weight(2), comm.weight(3)
    merged, ya, yr = _branch_merge(att, yrin, proj, wa, wr)
    (x1, h2), xres = _out_proj(merged, wo, x, g2, comm.carry("out_proj"))
    comm.took("out_proj", xres)
    wg, wu = comm.weight(4), comm.weight(5)
    (gte, up, act), xres = _ffn_up(h2, wg, wu, comm.carry("ffn_up"))
    comm.took("ffn_up", xres)
    wd = comm.weight(6)
    dx2, dx2b, dg3, loss_p = _ffn_down_loss(act, wd, x1, g3, tgt)

    dgte, dup, dx1, dx1b, dg2 = _ffn_bwd(dx2b, dx2, wd, wg, wu, gte, up, x1, g2)
    tok3 = lambda w: (lambda tk: pl.BlockSpec((None, tk, w), lambda p, k: (p, k, 0)))
    tok2 = lambda w: (lambda tk: pl.BlockSpec((tk, w), lambda p, k: (k, 0)))
    g_d = _wgrad("wgrad_down", act, dx2b, tok3(HID_S), tok2(D_MODEL), (N_SHARD, HID_S, D_MODEL),
                 pl.BlockSpec((None, HID_S, D_MODEL), lambda p, k: (p, 0, 0)), N_SHARD, S)
    g_g = _wgrad("wgrad_gate", h2, dgte, tok2(D_MODEL), tok3(HID_S), (N_SHARD, D_MODEL, HID_S),
                 pl.BlockSpec((None, D_MODEL, HID_S), lambda p, k: (p, 0, 0)), N_SHARD, S)
    g_u = _wgrad("wgrad_up", h2, dup, tok2(D_MODEL), tok3(HID_S), (N_SHARD, D_MODEL, HID_S),
                 pl.BlockSpec((None, D_MODEL, HID_S), lambda p, k: (p, 0, 0)), N_SHARD, S)
    comm.grads({4: g_g, 5: g_u, 6: g_d})
    (dya, dyr, dproj), xres = _out_proj_bwd(dx1b, wo, proj, ya, yr, comm.carry("out_proj_bwd"))
    comm.took("out_proj_bwd", xres)
    colblk = lambda w: (lambda tk: pl.BlockSpec((tk, w), lambda p, k: (k, p)))
    g_o = _wgrad("wgrad_out", merged, dx1b, colblk(256), tok2(D_MODEL), (D_MODEL, D_MODEL),
                 pl.BlockSpec((256, D_MODEL), lambda p, k: (p, 0)), 4, S)
    datt, rho, dyrin = _branch_bwd(dya, dyr, wa, wr, att)
    g_a = _wgrad("wgrad_attn", att, dya, tok2(512), colblk(512), (512, D_MODEL),
                 pl.BlockSpec((512, 512), lambda p, k: (0, p)), 2, S)
    g_r = _wgrad("wgrad_ret", yrin, dyr, colblk(256), tok2(D_MODEL), (D_MODEL, D_MODEL),
                 pl.BlockSpec((256, D_MODEL), lambda p, k: (p, 0)), 4, S)
    comm.grads({1: g_a, 2: g_r.reshape(N_SHARD, 256, D_MODEL), 3: g_o.reshape(N_SHARD, 256, D_MODEL)})
    dproj, xres = _ret_bwd(proj, rn, rstd, dyrin, states, tab, consts, dproj, comm.carry("ret_bwd"))
    comm.took("ret_bwd", xres)
    dqs, dks, dvs = [], [], []
    for gi, d in enumerate(DILATIONS):
        rtab = jnp.asarray(tab_np[0].reshape(3, S // d, d * 128))
        (dq, dk, dv), xres = _attn_bwd(qkvs[gi], _sub_view(datt, d), _sub_view(lse_tot, d), _sub_view(rho, d), rtab, d, gi,
                                       comm.carry(f"attn_bwd_g{gi}"))
        comm.took(f"attn_bwd_g{gi}", xres)
        dqs.append(dq)
        dks.append(dk)
        dvs.append(dv)
    dproj = _assemble_dproj((dqs, dks, dvs), dproj)
    g_sent, xres = _wgrad_in_half(ht, dproj, comm.sidx, False, comm.carry("wgrad_in_sent"))
    comm.took("wgrad_in_sent", xres)
    comm.grads({"in_sent": g_sent})
    g_kept, xres = _wgrad_in_half(ht, dproj, comm.sidx, True, comm.carry("wgrad_in_kept"))
    comm.grads({"in_kept": g_kept})
    comm.took("wgrad_in_kept", xres)
    grad_x, dg1, xres = _in_proj_bwd(dproj, w_in, x, g1, dx1, comm.carry("in_proj_bwd"))
    comm.took("in_proj_bwd", xres)
    return loss_p, grad_x, (dg1, dg2, dg3)


W_KINDS = ("col", "col", "lead", "lead", "lead", "lead", "lead")
W_SHARD = ((1024, W_IN_S), (512, 256), (256, 1024), (256, 1024), (1024, HID_S), (1024, HID_S), (HID_S, 1024))
N_W = len(W_KINDS)


def _full_shape(wi):
    R, C = W_SHARD[wi]
    return (R, N_SHARD * C) if W_KINDS[wi] == "col" else (N_SHARD, R, C)


def _view(ref, wi, s, half):
    R, C = W_SHARD[wi]
    rows = pl.ds(half * (R // 2), R // 2)
    if W_KINDS[wi] == "col":
        return ref.at[rows, pl.ds(pl.multiple_of(s * C, 128), C)]
    return ref.at[s, rows, :]


def _mesh_pos():
    x, y, c = lax.axis_index("x"), lax.axis_index("y"), lax.axis_index("c")
    chips = [(1 - x, y), (x, 1 - y), (1 - x, 1 - y)]
    return x, y, c, chips


def _cast_bf16(a):
    R, C = a.shape
    tr = R // 2 if R % 32 == 0 else R

    def body(a_ref, o_ref):
        o_ref[...] = a_ref[...].astype(BF16)

    spec = pl.BlockSpec((tr, C), lambda i: (i, 0))
    return pl.pallas_call(body, out_shape=SDS((R, C), BF16), grid=(R // tr,), in_specs=[spec], out_specs=spec,
                          compiler_params=_cparams("parallel"), name=f"cast_{R}x{C}")(a)


def _remote(send, recv, k, src, dst, to):
    return pltpu.make_async_remote_copy(src_ref=src, dst_ref=dst, send_sem=send.at[k], recv_sem=recv.at[k],
                                        device_id=to, device_id_type=MESH)


def _gather_now(wis, shards):
    n = len(wis)

    def body(*refs):
        sh, full = refs[:n], refs[n:2 * n]
        send, recv, loc = refs[2 * n:]
        x, y, c, _ = _mesh_pos()
        s_me, sib = 2 * x + y, (x, y, 1 - c)
        xn, yn = (1 - x, y), (x, 1 - y)
        flip = lambda a, b: a + b - 2 * a * b
        via = (flip(x, 1 - c), flip(y, c))
        onto = (flip(x, c), flip(y, 1 - c))
        shard_of = lambda chip: 2 * chip[0] + chip[1]
        own, started = [], []
        for i, wi in enumerate(wis):
            Rh = W_SHARD[wi][0] // 2
            for hf in range(2):
                cp = pltpu.make_async_copy(sh[i].at[pl.ds(hf * Rh, Rh), :], _view(full[i], wi, s_me, hf), loc.at[2 * i + hf])
                cp.start()
                own.append(cp)
            for j, chip in enumerate((xn, yn)):
                cp = _remote(send, recv, 6 * i + j, sh[i].at[pl.ds(c * Rh, Rh), :], _view(full[i], wi, s_me, c), (*chip, c))
                cp.start()
                started.append(cp)

        def pass_to_sibling(i, wi, k, s):
            mine = _view(full[i], wi, s, c)
            fw = _remote(send, recv, 6 * i + k, mine, mine, sib)
            fw.start()
            started.append(fw)

        for i, wi in enumerate(wis):
            for j, chip in enumerate((xn, yn)):
                land = _view(full[i], wi, shard_of(chip), c)
                _remote(send, recv, 6 * i + j, land, land, (*chip, c)).wait_recv()
                pass_to_sibling(i, wi, 3 + j, shard_of(chip))
            relay = _view(full[i], wi, shard_of(via), c)
            fw = _remote(send, recv, 6 * i + 2, relay, relay, (*onto, c))
            fw.start()
            started.append(fw)
        s_diag = 2 * (1 - x) + (1 - y)
        for i, wi in enumerate(wis):
            land = _view(full[i], wi, s_diag, c)
            _remote(send, recv, 6 * i + 2, land, land, (*onto, c)).wait_recv()
            pass_to_sibling(i, wi, 5, s_diag)
        for i, wi in enumerate(wis):
            for k, s in ((3, shard_of(xn)), (4, shard_of(yn)), (5, s_diag)):
                land = _view(full[i], wi, s, 1 - c)
                _remote(send, recv, 6 * i + k, land, land, sib).wait_recv()
        for cp in started:
            cp.wait_send()
        for cp in own:
            cp.wait()

    return pl.pallas_call(
        body, out_shape=tuple(SDS(_full_shape(wi), BF16) for wi in wis),
        in_specs=[ANY] * n, out_specs=tuple([ANY] * n),
        scratch_shapes=[pltpu.SemaphoreType.DMA((6 * n,)), pltpu.SemaphoreType.DMA((6 * n,)),
                        pltpu.SemaphoreType.DMA((2 * n,))],
        name="gather_now")(*shards)


def _ex_gather_ici(wis, shards, then_d2d=False):
    n = len(wis)

    def build(ins, outs, send, recv, loc):
        x, y, c, chips = _mesh_pos()
        s_me, sib = 2 * x + y, (x, y, 1 - c)
        starts, waits, after = [], [], []
        for i, wi in enumerate(wis):
            Rh = W_SHARD[wi][0] // 2
            for hf in range(2):
                cp = pltpu.make_async_copy(ins[i].at[pl.ds(hf * Rh, Rh), :], _view(outs[i], wi, s_me, hf), loc.at[2 * i + hf])
                starts.append(cp)
                waits.append(cp.wait)
            for j, chip in enumerate(chips):
                cp = _remote(send, recv, 3 * i + j, ins[i].at[pl.ds(c * Rh, Rh), :], _view(outs[i], wi, s_me, c), (*chip, c))
                land = _view(outs[i], wi, 2 * chip[0] + chip[1], c)
                starts.append(cp)
                waits += [cp.wait_send, _remote(send, recv, 3 * i + j, land, land, (*chip, c)).wait_recv]
                if then_d2d:
                    theirs = _view(outs[i], wi, 2 * chip[0] + chip[1], 1 - c)
                    fw = _remote(send, recv, 3 * n + 3 * i + j, land, land, sib)
                    waits.append(fw.start)
                    after += [fw.wait_send, _remote(send, recv, 3 * n + 3 * i + j, theirs, theirs, sib).wait_recv]
        return starts, waits + after

    return _Exchange(shards, [SDS(_full_shape(wi), BF16) for wi in wis], {}, (6 if then_d2d else 3) * n, 2 * n, build)


def _ex_gather_d2d(wis, fulls):
    def build(ins, outs, send, recv, loc):
        x, y, c, chips = _mesh_pos()
        sib = (x, y, 1 - c)
        starts, waits = [], []
        for i, wi in enumerate(wis):
            for j, chip in enumerate(chips):
                mine = _view(outs[i], wi, 2 * chip[0] + chip[1], c)
                theirs = _view(outs[i], wi, 2 * chip[0] + chip[1], 1 - c)
                cp = _remote(send, recv, 3 * i + j, mine, mine, sib)
                starts.append(cp)
                waits += [cp.wait_send, _remote(send, recv, 3 * i + j, theirs, theirs, sib).wait_recv]
        return starts, waits

    return _Exchange(fulls, [SDS(f.shape, BF16) for f in fulls], {i: i for i in range(len(wis))}, 3 * len(wis), 0, build,
                     stagger=True)


def _half_shape(wi):
    R, C = W_SHARD[wi]
    return (R // 2, N_SHARD * C) if W_KINDS[wi] == "col" else (N_SHARD, R // 2, C)


def _ex_pair(wis, grads):
    def build(ins, outs, send, recv, loc):
        x, y, c, _ = _mesh_pos()
        starts, waits = [], []
        for i, wi in enumerate(wis):
            Rh, C = W_SHARD[wi][0] // 2, W_SHARD[wi][1]
            rows = pl.ds((1 - c) * Rh, Rh)
            presplit = tuple(ins[i].shape) == _half_shape(wi)
            for s in range(N_SHARD):
                if W_KINDS[wi] == "col":
                    cols = pl.ds(s * C, C)
                    src, dst = ins[i].at[slice(None) if presplit else rows, cols], outs[i].at[:, cols]
                else:
                    src, dst = ins[i].at[s, rows, :], outs[i].at[s]
                cp = _remote(send, recv, N_SHARD * i + s, src, dst, (x, y, 1 - c))
                starts.append(cp)
                waits.append(cp.wait)
        return starts, waits

    return _Exchange(grads, [SDS(_half_shape(wi), F32) for wi in wis], {}, N_SHARD * len(wis), 0, build, stagger=True)


def _ex_chip(wis, pbs):
    def build(ins, outs, send, recv, loc):
        x, y, c, chips = _mesh_pos()
        starts, waits = [], []
        for i, wi in enumerate(wis):
            for j, chip in enumerate(chips):
                cp = _remote(send, recv, 3 * i + j, ins[i].at[j], outs[i].at[j], (*chip, c))
                starts.append(cp)
                waits.append(cp.wait)
        return starts, waits

    shapes = [SDS((3, W_SHARD[wi][0] // 2, W_SHARD[wi][1]), BF16) for wi in wis]
    return _Exchange(pbs, shapes, {}, 3 * len(wis), 0, build)


def _ex_share(wis, halves):
    def build(ins, outs, send, recv, loc):
        x, y, c, _ = _mesh_pos()
        sib = (x, y, 1 - c)
        starts, waits = [], []
        for i, wi in enumerate(wis):
            cp = _remote(send, recv, i, outs[i].at[c], outs[i].at[c], sib)
            starts.append(cp)
            waits += [cp.wait_send, _remote(send, recv, i, outs[i].at[1 - c], outs[i].at[1 - c], sib).wait_recv]
        return starts, waits

    return _Exchange(halves, [SDS(h.shape, F32) for h in halves], {i: i for i in range(len(wis))}, len(wis), 0, build,
                     stagger=True)


def _row_tile(rh, C):
    best = 16
    for t in range(16, rh + 1, 16):
        if rh % t == 0 and t * C * 4 <= (3 << 19):
            best = t
    return best


def _pair_sum(wi, g, ra, sidx):
    R, C = W_SHARD[wi]
    Rh = R // 2
    tr = _row_tile(Rh, C)
    nt = Rh // tr
    off = 0 if tuple(g.shape) == _half_shape(wi) else nt
    col = W_KINDS[wi] == "col"

    def body(sidx_ref, *refs):
        gs, rs = refs[:4], refs[4:8]
        own_ref, pb_ref = refs[8:]
        own_ref[...] = gs[0][...] + rs[0][...]
        for j in range(3):
            pb_ref[j] = (gs[1 + j][...] + rs[1 + j][...]).astype(BF16)

    def gspec(slot):
        if col:
            return pl.BlockSpec((tr, C), lambda i, sx: (sx[4] * off + i, sx[slot]))
        return pl.BlockSpec((None, tr, C), lambda i, sx: (sx[slot], sx[4] * off + i, 0))

    def rspec(slot):
        if col:
            return pl.BlockSpec((tr, C), lambda i, sx: (i, sx[slot]))
        return pl.BlockSpec((None, tr, C), lambda i, sx: (sx[slot], i, 0))

    return pl.pallas_call(
        body, out_shape=(SDS((Rh, C), F32), SDS((3, Rh, C), BF16)),
        grid_spec=pltpu.PrefetchScalarGridSpec(
            num_scalar_prefetch=1, grid=(nt,),
            in_specs=[gspec(k) for k in range(4)] + [rspec(k) for k in range(4)],
            out_specs=(pl.BlockSpec((tr, C), lambda i, sx: (i, 0)), pl.BlockSpec((3, tr, C), lambda i, sx: (0, i, 0)))),
        compiler_params=_cparams("arbitrary"), name=f"pair_sum_w{wi}")(sidx, g, g, g, g, ra, ra, ra, ra)


def _chip_sum(wi, own, rb, sidx):
    R, C = W_SHARD[wi]
    Rh = R // 2
    tr = _row_tile(Rh, C)

    def body(sidx_ref, own_ref, rb_ref, o_ref):
        o_ref[...] = ((own_ref[...] + rb_ref[0].astype(F32)) + rb_ref[1].astype(F32)) + rb_ref[2].astype(F32)

    return pl.pallas_call(
        body, out_shape=SDS((2, Rh, C), F32),
        grid_spec=pltpu.PrefetchScalarGridSpec(
            num_scalar_prefetch=1, grid=(Rh // tr,),
            in_specs=[pl.BlockSpec((tr, C), lambda i, sx: (i, 0)), pl.BlockSpec((3, tr, C), lambda i, sx: (0, i, 0))],
            out_specs=pl.BlockSpec((None, tr, C), lambda i, sx: (sx[4], i, 0))),
        compiler_params=_cparams("arbitrary"), name=f"chip_sum_w{wi}")(sidx, own, rb)


def _gain_allgather(blk, ex):
    m_per, n = blk.shape
    n_in, n_out = len(ex.ins), len(ex.out_shapes)

    def body(x_ref, *rest):
        xin, out_ref, xout = rest[:n_in], rest[n_in], rest[n_in + 1:n_in + 1 + n_out]
        send_sems, recv_sems, local_sem = rest[n_in + 1 + n_out:n_in + 4 + n_out]
        ex_starts, ex_waits = ex.build(xin, xout, *rest[n_in + 4 + n_out:])
        for cp in ex_starts:
            cp.start()
        x, y, c, chips = _mesh_pos()
        me, sibling = (x, y, c), (x, y, 1 - c)

        def rows(px, py, pc):
            return out_ref.at[pl.ds((4 * px + 2 * py + pc) * m_per, m_per), :]

        def copy(k, block, to, src=None):
            return pltpu.make_async_remote_copy(
                src_ref=rows(*block) if src is None else src, dst_ref=rows(*block),
                send_sem=send_sems.at[k], recv_sem=recv_sems.at[k], device_id=to, device_id_type=MESH)

        mine = pltpu.make_async_copy(x_ref, rows(*me), local_sem)
        mine.start()
        first = [copy(0, me, sibling, src=x_ref)]
        first += [copy(1 + j, me, (*chip, c), src=x_ref) for j, chip in enumerate(chips)]
        for cp in first:
            cp.start()
        passed = [copy(4 + j, (*chip, c), sibling) for j, chip in enumerate(chips)]
        for j, chip in enumerate(chips):
            copy(1 + j, (*chip, c), me).wait_recv()
            passed[j].start()
        copy(0, sibling, me).wait_recv()
        for j, chip in enumerate(chips):
            copy(4 + j, (*chip, 1 - c), me).wait_recv()
        for cp in first + passed:
            cp.wait_send()
        mine.wait()
        for w in ex_waits:
            w()

    vm = pl.BlockSpec(memory_space=pltpu.VMEM)
    res = pl.pallas_call(
        body, out_shape=(SDS((8 * m_per, n), blk.dtype), *ex.out_shapes),
        in_specs=[vm] + [ANY] * n_in, out_specs=(vm, *[ANY] * n_out),
        input_output_aliases={1 + a: 1 + o for a, o in ex.aliases.items()},
        scratch_shapes=[pltpu.SemaphoreType.DMA((7,)), pltpu.SemaphoreType.DMA((7,)), pltpu.SemaphoreType.DMA] + ex.sems(),
        name="gain_allgather")(blk, *ex.ins)
    return res[0], tuple(res[1:])


def _adam_math(w, g, m, v):
    mn = ADAM_B1 * m + (1.0 - ADAM_B1) * g
    vn = ADAM_B2 * v + (1.0 - ADAM_B2) * (g * g)
    mh = mn / (1.0 - ADAM_B1 ** ADAM_STEP)
    vh = vn / (1.0 - ADAM_B2 ** ADAM_STEP)
    return -ADAM_LR * (mh / (jnp.sqrt(vh) + ADAM_EPS) + ADAM_WD * w), mn, vn


def _adamw(wi, w, g, m, v):
    R, C = w.shape
    tr = _row_tile(R, C)

    def body(w_ref, g_ref, m_ref, v_ref, go_ref, d_ref, mn_ref, vn_ref):
        g = g_ref[...]
        go_ref[...] = g
        d_ref[...], mn_ref[...], vn_ref[...] = _adam_math(w_ref[...], g, m_ref[...], v_ref[...])

    spec = pl.BlockSpec((tr, C), lambda i: (i, 0))
    return pl.pallas_call(body, out_shape=(SDS((R, C), F32),) * 4, grid=(R // tr,), in_specs=[spec] * 4,
                          out_specs=(spec,) * 4, compiler_params=_cparams("parallel"), name=f"adamw_w{wi}")(w, g, m, v)


def _gain_update(gathered, w, m, v):
    def body(ga_ref, w_ref, m_ref, v_ref, g_ref, d_ref, mn_ref, vn_ref):
        g = ga_ref[0:8, :]
        for dev in range(1, 8):
            g = g + ga_ref[8 * dev:8 * dev + 8, :]
        g_ref[...] = g
        d_ref[...], mn_ref[...], vn_ref[...] = _adam_math(w_ref[...], g, m_ref[...], v_ref[...])

    return pl.pallas_call(body, out_shape=(SDS((8, 1024), F32),) * 4, name="gain_update")(gathered, w, m, v)


GROUP_FFN, GROUP_MIX, GROUP_IN = (4, 5, 6), (1, 2, 3), (0,)
REST = GROUP_MIX + GROUP_FFN


class _MeshComm:
    SCHEDULE = {
        "in_proj": [("ici", (1, 2, 3, 4))],
        "ret_fwd": [("d2d", (1, 2, 3, 4)), ("ici", (5,))],
        "out_proj": [("d2d", (5,))],
        "ffn_up": [("both", (6,))],
        "out_proj_bwd": [("pair", GROUP_FFN)],
        "ret_bwd": [("pair", GROUP_MIX), ("chip", (4,))],
        "attn_bwd_g0": [("chip", (5,))],
        "attn_bwd_g1": [("chip", (6,))],
        "attn_bwd_g2": [("chip", GROUP_MIX)],
        "wgrad_in_kept": [("pair", GROUP_IN), ("share", GROUP_FFN + GROUP_MIX)],
        "in_proj_bwd": [("chip", GROUP_IN)],
    }

    def __init__(self, shards):
        xi, yi, ci = lax.axis_index("x"), lax.axis_index("y"), lax.axis_index("c")
        self.sidx = jnp.stack([2 * xi + yi, 2 * (1 - xi) + yi, 2 * xi + (1 - yi), 2 * (1 - xi) + (1 - yi), ci]).astype(jnp.int32)
        self.shards, self.full = shards, {}
        self.g, self.own, self.pb, self.half, self.red = {}, {}, {}, {}, {}

    def w_in(self):
        return _gather_now(GROUP_IN, [self.shards[0]])[0]

    def weight(self, wi):
        return self.full[wi].reshape(D_MODEL, D_MODEL) if wi in (2, 3) else self.full[wi]

    def grads(self, by_wi):
        self.g.update(by_wi)

    def _exchange(self, stage, wis):
        pick = lambda table: [table[wi] for wi in wis]
        if stage == "ici":
            return _ex_gather_ici(wis, pick(self.shards))
        if stage == "both":
            return _ex_gather_ici(wis, pick(self.shards), then_d2d=True)
        if stage == "d2d":
            return _ex_gather_d2d(wis, pick(self.full))
        if stage == "pair":
            return _ex_pair(wis, [self.g["in_sent"] if wi == 0 else self.g[wi] for wi in wis])
        if stage == "chip":
            return _ex_chip(wis, pick(self.pb))
        return _ex_share(wis, pick(self.half))

    def _landed(self, stage, wis, res):
        for wi, r in zip(wis, res):
            if stage in ("ici", "d2d", "both"):
                self.full[wi] = r
            elif stage == "pair":
                self.own[wi], self.pb[wi] = _pair_sum(wi, self.g["in_kept"] if wi == 0 else self.g[wi], r, self.sidx)
            elif stage == "chip":
                self.half[wi] = _chip_sum(wi, self.own[wi], r, self.sidx)
            else:
                self.red[wi] = r

    def carry(self, point):
        return [self._exchange(stage, wis) for stage, wis in self.SCHEDULE.get(point, ())]

    def took(self, point, xres):
        for (stage, wis), res in zip(self.SCHEDULE.get(point, ()), xres):
            self._landed(stage, wis, res)

    def last_share(self):
        return self._exchange("share", GROUP_IN)

    def reduced(self, last_shared):
        self._landed("share", GROUP_IN, last_shared)
        return [self.red[wi] for wi in range(N_W)]


def kernel(x, norm_mix_g, w_in, w_out_attn, w_out_ret, w_out, norm_ffn_g, w_ffn_gate, w_ffn_up, w_ffn_down, norm_final_g, loss_target, m_norm_mix_g, m_w_in, m_w_out_attn, m_w_out_ret, m_w_out, m_norm_ffn_g, m_w_ffn_gate, m_w_ffn_up, m_w_ffn_down, m_norm_final_g, v_norm_mix_g, v_w_in, v_w_out_attn, v_w_out_ret, v_w_out, v_norm_ffn_g, v_w_ffn_gate, v_w_ffn_up, v_w_ffn_down, v_norm_final_g):
    ws = (w_in, w_out_attn, w_out_ret, w_out, w_ffn_gate, w_ffn_up, w_ffn_down)
    ms = (m_w_in, m_w_out_attn, m_w_out_ret, m_w_out, m_w_ffn_gate, m_w_ffn_up, m_w_ffn_down)
    vs = (v_w_in, v_w_out_attn, v_w_out_ret, v_w_out, v_w_ffn_gate, v_w_ffn_up, v_w_ffn_down)
    shard2d = lambda a, wi: a.reshape(W_SHARD[wi])

    comm = _MeshComm([_cast_bf16(shard2d(w, wi)) for wi, w in enumerate(ws)])
    g3 = norm_final_g.reshape(1, D_MODEL)
    loss_p, grad_x, gain_g = _step(x[0], loss_target[0], norm_mix_g, norm_ffn_g, g3, comm)

    pad8 = lambda rows: jnp.concatenate([r.reshape(1, D_MODEL) for r in rows]
                                        + [jnp.zeros((8 - len(rows), D_MODEL), F32)], axis=0)
    gathered, shared = _gain_allgather(pad8((*gain_g, jnp.tile(loss_p[0:1], (1, D_MODEL // 128)))), comm.last_share())
    gred = comm.reduced(shared)

    outs_g, outs_d, outs_m, outs_v = [], [], [], []
    for wi in range(N_W):
        g2d = gred[wi].reshape(W_SHARD[wi])
        gout, dlt, mn, vn = _adamw(wi, shard2d(ws[wi], wi), g2d, shard2d(ms[wi], wi), shard2d(vs[wi], wi))
        for lst, a in ((outs_g, gout), (outs_d, dlt), (outs_m, mn), (outs_v, vn)):
            lst.append(a.reshape(ws[wi].shape))

    gg, gd, gm, gv = _gain_update(gathered, pad8((norm_mix_g, norm_ffn_g, norm_final_g)),
                                  pad8((m_norm_mix_g, m_norm_ffn_g, m_norm_final_g)),
                                  pad8((v_norm_mix_g, v_norm_ffn_g, v_norm_final_g)))
    loss = gg[3, 0]

    def assemble(gain_rows, wlist):
        return (gain_rows[0:1], wlist[0], wlist[1], wlist[2], wlist[3], gain_rows[1:2],
                wlist[4], wlist[5], wlist[6], gain_rows[2])

    return (loss, grad_x[None], *assemble(gg, outs_g), *assemble(gd, outs_d), *assemble(gm, outs_m), *assemble(gv, outs_v))
```

```python
import functools
import math

import numpy as np
import jax
import jax.numpy as jnp
from jax import lax
from jax.experimental import pallas as pl
from jax.experimental.pallas import tpu as pltpu

F32, BF16 = jnp.float32, jnp.bfloat16
SDS = jax.ShapeDtypeStruct
MESH = pl.DeviceIdType.MESH

D_MODEL = 1024
PROJ_W = 9728
COLB = 512
N_COLB = PROJ_W // COLB
QA_B, KA_B, VA_B = 0, 3, 6
QR_B, KR_B = 9, 10
FFN_HID = 2816
N_SHARD = 4
HID_S = FFN_HID // N_SHARD
W_IN_S = PROJ_W // N_SHARD
DILATIONS = (1, 4, 16)
BLK = 128
RET_HEADS = 4
ROPE_THETA = 10000.0
NORM_EPS = 1e-6
ADAM_LR, ADAM_B1, ADAM_B2, ADAM_EPS, ADAM_WD, ADAM_STEP = 0.001, 0.9, 0.999, 1e-08, 0.01, 10
VMEM_LIMIT = 56 << 20


def _cparams(*sem):
    return pltpu.CompilerParams(dimension_semantics=sem or None, vmem_limit_bytes=VMEM_LIMIT)


def _dot(a, b):
    return jnp.dot(a, b, preferred_element_type=F32)


def _dot_nt(a, b):
    return lax.dot_general(a, b, (((1,), (1,)), ((), ())), preferred_element_type=F32)


def _dot_tn(a, b):
    return lax.dot_general(a, b, (((0,), (0,)), ((), ())), preferred_element_type=F32)


def _row_pieces(tm, sub=512):
    return [slice(i, i + sub) for i in range(0, tm, sub)]


def _sigmoid(z):
    return 0.5 * jnp.tanh(0.5 * z) + 0.5


ANY = pl.BlockSpec(memory_space=pl.ANY)


class _Exchange:
    def __init__(self, ins, out_shapes, aliases, n_sem, n_loc, build):
        self.ins, self.out_shapes, self.aliases = list(ins), list(out_shapes), dict(aliases)
        self.n_sem, self.n_loc, self.build = n_sem, n_loc, build

    def sems(self):
        return [pltpu.SemaphoreType.DMA((self.n_sem,)), pltpu.SemaphoreType.DMA((self.n_sem,)),
                pltpu.SemaphoreType.DMA((max(self.n_loc, 1),))]


def _carrier_call(body, args, *, out_shape, grid, in_specs, out_specs, scratch_shapes=(), sem, name, exchanges=(),
                  prefetch=None, in_out_aliases=None):
    out_shape, out_specs = tuple(out_shape), tuple(out_specs)
    n_in, n_out, n_scr = len(args), len(out_shape), len(scratch_shapes)
    n_pre = 0 if prefetch is None else 1
    x_args, x_outs, x_scr, spans = [], [], [], []
    aliases = {n_pre + a: o for a, o in (in_out_aliases or {}).items()}
    for ex in exchanges:
        i0, o0 = len(x_args), len(x_outs)
        for a, o in ex.aliases.items():
            aliases[n_pre + n_in + i0 + a] = n_out + o0 + o
        x_args += ex.ins
        x_outs += ex.out_shapes
        x_scr += ex.sems()
        spans.append((i0, len(ex.ins), o0, len(ex.out_shapes)))
    nx_in, nx_out = len(x_args), len(x_outs)

    def wrapped(*refs):
        refs = refs[n_pre:]
        ins, xin = refs[:n_in], refs[n_in:n_in + nx_in]
        o_base = n_in + nx_in
        outs, xout = refs[o_base:o_base + n_out], refs[o_base + n_out:o_base + n_out + nx_out]
        s_base = o_base + n_out + nx_out
        scr, xs = refs[s_base:s_base + n_scr], refs[s_base + n_scr:]

        def built(e):
            i0, ni, o0, no = spans[e]
            return exchanges[e].build(xin[i0:i0 + ni], xout[o0:o0 + no], *xs[3 * e:3 * e + 3])

        if exchanges:
            first = functools.reduce(jnp.logical_and, [pl.program_id(k) == 0 for k in range(len(grid))])
            last = functools.reduce(jnp.logical_and, [pl.program_id(k) == grid[k] - 1 for k in range(len(grid))])

            @pl.when(first)
            def _():
                for e in range(len(exchanges)):
                    for cp in built(e)[0]:
                        cp.start()

        body(*ins, *outs, *scr)

        if exchanges:
            @pl.when(last)
            def _():
                for e in range(len(exchanges)):
                    for w in built(e)[1]:
                        w()

    all_in, all_out = list(in_specs) + [ANY] * nx_in, out_specs + tuple([ANY] * nx_out)
    all_scr = list(scratch_shapes) + x_scr
    cparams = _cparams(*(sem if not exchanges else ("arbitrary",) * len(grid)))
    if prefetch is None:
        res = pl.pallas_call(wrapped, out_shape=out_shape + tuple(x_outs), grid=grid, in_specs=all_in, out_specs=all_out,
                             scratch_shapes=all_scr, input_output_aliases=aliases, compiler_params=cparams,
                             name=name)(*args, *x_args)
    else:
        gs = pltpu.PrefetchScalarGridSpec(num_scalar_prefetch=1, grid=grid, in_specs=all_in, out_specs=all_out,
                                          scratch_shapes=all_scr)
        res = pl.pallas_call(wrapped, out_shape=out_shape + tuple(x_outs), grid_spec=gs, input_output_aliases=aliases,
                             compiler_params=cparams, name=name)(prefetch, *args, *x_args)
    xres = [tuple(res[n_out + o0:n_out + o0 + no]) for (_, _, o0, no) in spans]
    return tuple(res[:n_out]), xres


def _tables(S):
    f32 = np.float32
    pos = np.arange(S, dtype=f32)
    lane = np.arange(128)
    inv = (f32(ROPE_THETA) ** (-np.arange(0, 64, 2, dtype=f32) / f32(64))).astype(f32)
    ang = (pos[:, None] * inv[None, :]).astype(np.float64)
    idx = (lane % 64) % 32
    c, s = np.cos(ang)[:, idx], np.sin(ang)[:, idx]
    first = ((lane % 64) < 32)[None, :]
    rope = np.stack([c, np.where(first, 0.0, s), np.where(first, -s, 0.0)])
    base = (f32(1.0) / (f32(ROPE_THETA) ** np.linspace(0.0, 1.0, 64, dtype=f32))).astype(f32)
    ang2 = (pos[:, None] * base[None, :]).astype(np.float64)
    c2, s2 = np.cos(ang2)[:, lane // 2], np.sin(ang2)[:, lane // 2]
    even = (lane % 2 == 0)[None, :]
    th = np.stack([c2, np.where(even, 0.0, s2), np.where(even, -s2, 0.0)])
    return np.stack([rope, th, th * (128 ** -0.5)]).astype(f32)


def _rot(a, c, sa, sb, shift):
    return a * c + pltpu.roll(a, shift, 1) * sa + pltpu.roll(a, 128 - shift, 1) * sb


def _unrot(g, c, sa, sb, shift):
    return g * c + pltpu.roll(g * sa, 128 - shift, 1) + pltpu.roll(g * sb, shift, 1)


def _ret_consts():
    h = np.arange(RET_HEADS, dtype=np.float64)
    log_g = np.log1p(-(2.0 ** (-5.0 - h)))
    idx = np.arange(BLK, dtype=np.float64)
    diff = idx[:, None] - idx[None, :]
    dmask = np.where(diff[None] >= 0, np.exp(np.maximum(diff, 0.0)[None] * log_g[:, None, None]), 0.0)
    zeta = np.exp((BLK - 1 - idx)[None, :] * log_g[:, None])
    xi = np.exp((idx + 1.0)[None, :] * log_g[:, None])
    dec = np.exp(BLK * log_g)
    rep = lambda v: np.broadcast_to(v[:, :, None], (RET_HEADS, BLK, 128))
    return (jnp.asarray(dmask, F32), jnp.asarray(rep(zeta), F32), jnp.asarray(rep(xi), F32),
            jnp.asarray(np.broadcast_to(dec[:, None, None], (RET_HEADS, 8, 256)), F32))


def _rms_fwd(x, g, exchanges=()):
    S = x.shape[0]
    tm = 512

    def body(x_ref, g_ref, h_ref, ht_ref):
        xv = x_ref[...]
        r = lax.rsqrt(jnp.mean(xv * xv, axis=-1, keepdims=True) + NORM_EPS)
        h = xv * r * g_ref[...]
        h_ref[...] = h.astype(BF16)
        ht_ref[...] = h.T.astype(BF16)

    return _carrier_call(
        body, (x, g), out_shape=(SDS((S, D_MODEL), BF16), SDS((D_MODEL, S), BF16)), grid=(S // tm,),
        in_specs=[pl.BlockSpec((tm, D_MODEL), lambda i: (i, 0)), pl.BlockSpec((1, D_MODEL), lambda i: (0, 0))],
        out_specs=(pl.BlockSpec((tm, D_MODEL), lambda i: (i, 0)), pl.BlockSpec((D_MODEL, tm), lambda i: (0, i))),
        sem=("parallel",), name="rms_fwd", exchanges=exchanges)


def _in_proj(h, w_in, tab, exchanges=()):
    S = h.shape[0]
    tm = min(S, 4096)

    def body(h_ref, w_ref, t_ref, o_ref):
        j = pl.program_id(1)
        is_rope = j < 6
        is_theta = (j == QR_B) | (j == KR_B)
        sub = 512

        def rotated(shift):
            for i in range(tm // sub):
                rows = slice(i * sub, (i + 1) * sub)
                acc = _dot(h_ref[rows, :], w_ref[...])
                c, sa, sb = t_ref[0, 0, rows, :], t_ref[0, 1, rows, :], t_ref[0, 2, rows, :]
                for k in range(COLB // 128):
                    sl = slice(k * 128, (k + 1) * 128)
                    o_ref[rows, sl] = _rot(acc[:, sl], c, sa, sb, shift).astype(BF16)

        @pl.when(is_rope)
        def _():
            rotated(32)

        @pl.when(is_theta)
        def _():
            rotated(1)

        @pl.when(jnp.logical_not(is_rope | is_theta))
        def _():
            o_ref[...] = _dot(h_ref[...], w_ref[...]).astype(BF16)

    def tab_map(i, j):
        return (jnp.where(j == QR_B, 1, jnp.where(j == KR_B, 2, 0)), 0, i, 0)

    (proj,), xres = _carrier_call(
        body, (h, w_in, tab), out_shape=(SDS((S, PROJ_W), BF16),), grid=(S // tm, N_COLB),
        in_specs=[pl.BlockSpec((tm, D_MODEL), lambda i, j: (i, 0)),
                  pl.BlockSpec((D_MODEL, COLB), lambda i, j: (0, j)),
                  pl.BlockSpec((1, 3, tm, 128), tab_map)],
        out_specs=(pl.BlockSpec((tm, COLB), lambda i, j: (i, j)),),
        sem=("parallel", "arbitrary"), name="in_proj", exchanges=exchanges)
    return proj, xres


def _band_mask(n):
    qi = lax.broadcasted_iota(jnp.int32, (BLK, 2 * BLK), 0)
    kj = lax.broadcasted_iota(jnp.int32, (BLK, 2 * BLK), 1)
    dist = BLK + qi - kj
    return (dist >= 0) & (dist <= BLK) & ((kj >= BLK) | (n > 0))


def _qkv_col(d, gi):
    if d == 1:
        return lambda t, r: 3 * t + gi
    return lambda t, r: 3 * r + t


def _attn_fwd(qkv, d, gi, exchanges=()):
    L = qkv.shape[0]
    nb = L // BLK

    def body(q_ref, kc_ref, kp_ref, vc_ref, vp_ref, o_ref, lse_ref):
        n = pl.program_id(1)
        mask = _band_mask(n)
        mask2 = jnp.concatenate([mask, mask], axis=0)
        lane = lax.broadcasted_iota(jnp.int32, (BLK, 128), 1)
        lo = lane < 64
        lse_all = jnp.zeros((BLK, 128), F32)
        chunks = [slice(c * 128, (c + 1) * 128) for c in range(4)]
        scores, vals = [], []
        for sl in chunks:
            q = q_ref[:, sl]
            k = jnp.concatenate([kp_ref[:, sl], kc_ref[:, sl]], axis=0)
            vals.append(jnp.concatenate([vp_ref[:, sl], vc_ref[:, sl]], axis=0))
            q2 = jnp.concatenate([jnp.where(lo, q, jnp.zeros_like(q)), jnp.where(lo, jnp.zeros_like(q), q)], axis=0)
            scores.append(_dot_nt(q2, k))
        probs = []
        for c, s in enumerate(scores):
            s = jnp.where(mask2, s * 0.125, jnp.float32(-1e30))
            m = jnp.max(s, axis=-1, keepdims=True)
            p = jnp.exp(s - m)
            l = jnp.sum(p, axis=-1, keepdims=True)
            probs.append((p / l).astype(BF16))
            lse = m + jnp.log(l)
            lse_all = jnp.where(lane // 16 == 2 * c, lse[:BLK], jnp.where(lane // 16 == 2 * c + 1, lse[BLK:], lse_all))
        for sl, p, v in zip(chunks, probs, vals):
            o2 = _dot(p, v)
            o_ref[:, sl] = jnp.where(lo, o2[:BLK], o2[BLK:])
        lse_ref[...] = lse_all

    prev = lambda n: jnp.maximum(n - 1, 0)
    col = _qkv_col(d, gi)
    return _carrier_call(
        body, (qkv,) * 5, out_shape=(SDS((L, d * 512), F32), SDS((L, d * 128), F32)), grid=(d, nb),
        in_specs=[pl.BlockSpec((BLK, 512), lambda r, n: (n, col(0, r))),
                  pl.BlockSpec((BLK, 512), lambda r, n: (n, col(1, r))),
                  pl.BlockSpec((BLK, 512), lambda r, n: (prev(n), col(1, r))),
                  pl.BlockSpec((BLK, 512), lambda r, n: (n, col(2, r))),
                  pl.BlockSpec((BLK, 512), lambda r, n: (prev(n), col(2, r)))],
        out_specs=(pl.BlockSpec((BLK, 512), lambda r, n: (n, r)),
                   pl.BlockSpec((BLK, 128), lambda r, n: (n, r))),
        sem=("parallel", "arbitrary"), name=f"attn_fwd_g{gi}", exchanges=exchanges)


def _qkv_to_sub(proj, d, gi):
    S = proj.shape[0]
    tm = 512
    n = tm // d

    def body(q_ref, k_ref, v_ref, o_ref, scr):
        for t, ref in enumerate((q_ref, k_ref, v_ref)):
            for c in range(4):
                scr[c] = ref[:, c * 128:(c + 1) * 128].astype(F32)
            for r in range(d):
                for c in range(4):
                    col = (3 * r + t) * 512 + c * 128
                    o_ref[:, col:col + 128] = scr[c, pl.ds(r, n, stride=d), :].astype(BF16)

    return pl.pallas_call(
        body, out_shape=SDS((S // d, d * 1536), BF16), grid=(S // tm,),
        in_specs=[pl.BlockSpec((tm, 512), lambda i, b=b: (i, b + gi)) for b in (QA_B, KA_B, VA_B)],
        out_specs=pl.BlockSpec((n, d * 1536), lambda i: (i, 0)),
        scratch_shapes=[pltpu.VMEM((4, tm, 128), F32)],
        compiler_params=_cparams("parallel"), name=f"qkv_to_sub_g{gi}")(proj, proj, proj)


def _attn_merge(os_, lses):
    S = os_[0].shape[0]
    tm = 512

    def body(o0, o1, o2, l0, l1, l2, att_ref, lt_ref, so1, so2, sl1, sl2):
        lo = lax.broadcasted_iota(jnp.int32, (tm, 128), 1) < 64

        def natural(ref, d, scr, width):
            nch = width // 128
            if d == 1:
                return [ref[:, c * 128:(c + 1) * 128] for c in range(nch)]
            for r in range(d):
                for c in range(nch):
                    scr[c, pl.ds(r, tm // d, stride=d), :] = ref[:, r * width + c * 128:r * width + (c + 1) * 128]
            return [scr[c] for c in range(nch)]

        ls = [natural(l, d, s, 128)[0] for l, d, s in zip((l0, l1, l2), DILATIONS, (None, sl1, sl2))]
        m = jnp.maximum(jnp.maximum(ls[0], ls[1]), ls[2])
        es = [jnp.exp(v - m) for v in ls]
        z = es[0] + es[1] + es[2]
        lt_ref[...] = m + jnp.log(z)
        ws = [e / z for e in es]
        o_nat = [natural(o, d, s, 512) for o, d, s in zip((o0, o1, o2), DILATIONS, (None, so1, so2))]
        for c in range(4):
            acc = jnp.zeros((tm, 128), F32)
            for g in range(3):
                w_lo = jnp.broadcast_to(ws[g][:, 32 * c:32 * c + 1], (tm, 128))
                w_hi = jnp.broadcast_to(ws[g][:, 32 * c + 16:32 * c + 17], (tm, 128))
                acc = acc + jnp.where(lo, w_lo, w_hi) * o_nat[g][c]
            att_ref[:, c * 128:(c + 1) * 128] = acc.astype(BF16)

    sub = lambda w: [pl.BlockSpec((tm // d, d * w), lambda i: (i, 0)) for d in DILATIONS]
    return pl.pallas_call(
        body, out_shape=(SDS((S, 512), BF16), SDS((S, 128), F32)), grid=(S // tm,),
        in_specs=sub(512) + sub(128),
        out_specs=(pl.BlockSpec((tm, 512), lambda i: (i, 0)), pl.BlockSpec((tm, 128), lambda i: (i, 0))),
        scratch_shapes=[pltpu.VMEM((4, tm, 128), F32), pltpu.VMEM((4, tm, 128), F32),
                        pltpu.VMEM((1, tm, 128), F32), pltpu.VMEM((1, tm, 128), F32)],
        compiler_params=_cparams("parallel"), name="attn_merge")(*os_, *lses)


def _assemble_dproj(att_grads, dproj):
    S = dproj.shape[0]
    tm = 256

    def body(*refs):
        a = [refs[3 * t:3 * t + 3] for t in range(3)]
        dp_prev, o_ref, scr = refs[9:]
        for t in range(3):
            for g, d in enumerate(DILATIONS):
                base = (3 * t + g) * COLB
                if d == 1:
                    o_ref[:, base:base + COLB] = a[t][g][...]
                    continue
                for c in range(4):
                    for r in range(d):
                        scr[c, pl.ds(r, tm // d, stride=d), :] = a[t][g][:, r * 512 + c * 128:r * 512 + (c + 1) * 128].astype(F32)
                    o_ref[:, base + c * 128:base + (c + 1) * 128] = scr[c].astype(BF16)

    sub = [pl.BlockSpec((tm // d, d * 512), lambda i: (i, 0)) for d in DILATIONS]
    flat = [att_grads[t][g] for t in range(3) for g in range(3)]
    return pl.pallas_call(
        body, out_shape=SDS((S, PROJ_W), BF16), grid=(S // tm,),
        in_specs=sub * 3 + [ANY], out_specs=pl.BlockSpec((tm, 9 * COLB), lambda i: (i, 0)),
        scratch_shapes=[pltpu.VMEM((4, tm, 128), F32)], input_output_aliases={9: 0},
        compiler_params=_cparams("parallel"), name="assemble_dproj")(*flat, dproj)


def _ret_fwd(proj, consts, exchanges=()):
    S = proj.shape[0]
    nc = S // BLK
    dmask, zeta, xi, dec = consts

    def body(q_ref, k_ref, v0_ref, v1_ref, g0_ref, g1_ref, dm_ref, z_ref, x_ref, dec_ref,
             y_ref, rn_ref, rs_ref, st_ref, R):
        @pl.when(pl.program_id(0) == 0)
        def _():
            R[...] = jnp.zeros_like(R)

        lane16 = lax.broadcasted_iota(jnp.int32, (BLK, 128), 1) // 16
        rs_all = jnp.zeros((BLK, 128), F32)
        first = []
        for h in range(RET_HEADS):
            hs = slice(h * 128, (h + 1) * 128)
            q, k = q_ref[:, hs], k_ref[:, hs]
            v = (v0_ref if h < 2 else v1_ref)[:, (h % 2) * 256:(h % 2 + 1) * 256]
            Rb = R[h].astype(BF16)
            st_ref[h] = Rb
            kz = (k.astype(F32) * z_ref[h]).astype(BF16)
            first.append((v, _dot_nt(q, k), _dot((q.astype(F32) * x_ref[h]).astype(BF16), Rb), _dot_tn(kz, v)))
        masked = [(s * dm_ref[h]).astype(BF16) for h, (_, s, _, _) in enumerate(first)]
        for h in range(RET_HEADS):
            vs = slice((h % 2) * 256, (h % 2 + 1) * 256)
            os_ = slice(h * 256, (h + 1) * 256)
            v, _, cross, kv = first[h]
            o = _dot(masked[h], v) + cross
            R[h] = R[h] * dec_ref[h, 0:1, :] + kv
            mu = jnp.mean(o, axis=-1, keepdims=True)
            oc = o - mu
            rstd = lax.rsqrt(jnp.mean(oc * oc, axis=-1, keepdims=True) + NORM_EPS)
            rn = oc * rstd
            gr = (g0_ref if h < 2 else g1_ref)[:, vs].astype(F32)
            y_ref[:, os_] = (rn * gr * _sigmoid(gr)).astype(BF16)
            rn_ref[:, os_] = rn.astype(BF16)
            rs_all = jnp.where(lane16 == h, rstd, rs_all)
        rs_ref[...] = rs_all

    cst = lambda shape: pl.BlockSpec(shape, lambda c: (0, 0, 0))
    blk = lambda j: pl.BlockSpec((BLK, 512), lambda c: (c, j))
    return _carrier_call(
        body, (proj, proj, proj, proj, proj, proj, dmask, zeta, xi, dec),
        out_shape=(SDS((S, 1024), BF16), SDS((S, 1024), BF16), SDS((S, 128), F32), SDS((RET_HEADS, nc, BLK, 256), BF16)),
        grid=(nc,),
        in_specs=[blk(QR_B), blk(KR_B), blk(11), blk(12), blk(13), blk(14),
                  cst((RET_HEADS, BLK, BLK)), cst((RET_HEADS, BLK, 128)), cst((RET_HEADS, BLK, 128)), cst((RET_HEADS, 8, 256))],
        out_specs=(pl.BlockSpec((BLK, 1024), lambda c: (c, 0)), pl.BlockSpec((BLK, 1024), lambda c: (c, 0)),
                   pl.BlockSpec((BLK, 128), lambda c: (c, 0)),
                   pl.BlockSpec((RET_HEADS, None, BLK, 256), lambda c: (0, c, 0, 0))),
        scratch_shapes=[pltpu.VMEM((RET_HEADS, BLK, 256), F32)],
        sem=("arbitrary",), name="ret_fwd", exchanges=exchanges)


def _branch_merge(att, yrin, proj, wa, wr):
    S = att.shape[0]
    tm = min(S, 2048)

    def body(a_ref, y_ref, ga_ref, gr_ref, wa_ref, wr_ref, m_ref, ya_ref, yr_ref):
        for rows in _row_pieces(tm):
            ya = _dot(a_ref[rows, :], wa_ref[...])
            yr = _dot(y_ref[rows, :], wr_ref[...])
            m_ref[rows, :] = (_sigmoid(ga_ref[rows, :].astype(F32)) * ya
                              + _sigmoid(gr_ref[rows, :].astype(F32)) * yr).astype(BF16)
            ya_ref[rows, :] = ya.astype(BF16)
            yr_ref[rows, :] = yr.astype(BF16)

    ospec = pl.BlockSpec((tm, 512), lambda i, j: (i, j))
    return pl.pallas_call(
        body, out_shape=(SDS((S, D_MODEL), BF16),) * 3, grid=(S // tm, 2),
        in_specs=[pl.BlockSpec((tm, 512), lambda i, j: (i, 0)), pl.BlockSpec((tm, 1024), lambda i, j: (i, 0)),
                  pl.BlockSpec((tm, 512), lambda i, j: (i, 15 + j)), pl.BlockSpec((tm, 512), lambda i, j: (i, 17 + j)),
                  pl.BlockSpec((512, 512), lambda i, j: (0, j)), pl.BlockSpec((1024, 512), lambda i, j: (0, j))],
        out_specs=(ospec, ospec, ospec),
        compiler_params=_cparams("parallel", "arbitrary"), name="branch_merge")(att, yrin, proj, proj, wa, wr)


def _out_proj(merged, wo, x, g2, exchanges=()):
    S = x.shape[0]
    tm = 1024

    def body(m_ref, w_ref, x_ref, g_ref, x1_ref, h2_ref):
        x1 = x_ref[...] + _dot(m_ref[...], w_ref[...])
        x1_ref[...] = x1
        r = lax.rsqrt(jnp.mean(x1 * x1, axis=-1, keepdims=True) + NORM_EPS)
        h2_ref[...] = (x1 * r * g_ref[...]).astype(BF16)

    row = pl.BlockSpec((tm, D_MODEL), lambda i: (i, 0))
    return _carrier_call(
        body, (merged, wo, x, g2), out_shape=(SDS((S, D_MODEL), F32), SDS((S, D_MODEL), BF16)), grid=(S // tm,),
        in_specs=[row, pl.BlockSpec((D_MODEL, D_MODEL), lambda i: (0, 0)), row, pl.BlockSpec((1, D_MODEL), lambda i: (0, 0))],
        out_specs=(row, row), sem=("parallel",), name="out_proj", exchanges=exchanges)


def _ffn_up(h2, wg, wu, exchanges=()):
    S = h2.shape[0]
    tm = min(S, 2048)

    def body(h_ref, wg_ref, wu_ref, g_ref, u_ref, a_ref):
        for rows in _row_pieces(tm):
            hv = h_ref[rows, :]
            g = _dot(hv, wg_ref[...])
            u = _dot(hv, wu_ref[...])
            g_ref[rows, :] = g.astype(BF16)
            u_ref[rows, :] = u.astype(BF16)
            a_ref[rows, :] = (g * _sigmoid(g) * u).astype(BF16)

    wspec = pl.BlockSpec((None, D_MODEL, HID_S), lambda i, s: (s, 0, 0))
    ospec = pl.BlockSpec((None, tm, HID_S), lambda i, s: (s, i, 0))
    return _carrier_call(
        body, (h2, wg, wu), out_shape=(SDS((N_SHARD, S, HID_S), BF16),) * 3, grid=(S // tm, N_SHARD),
        in_specs=[pl.BlockSpec((tm, D_MODEL), lambda i, s: (i, 0)), wspec, wspec],
        out_specs=(ospec, ospec, ospec),
        sem=("parallel", "arbitrary"), name="ffn_up", exchanges=exchanges)


def _ffn_down_loss(act, wd, x1, g3, tgt):
    S = x1.shape[0]
    tm = 512

    def body(a_ref, w_ref, x_ref, g_ref, t_ref, dx_ref, dxb_ref, dg_ref, ls_ref):
        @pl.when(pl.program_id(0) == 0)
        def _():
            dg_ref[...] = jnp.zeros_like(dg_ref)
            ls_ref[...] = jnp.zeros_like(ls_ref)

        g = g_ref[...]
        for rows in _row_pieces(tm, 256):
            y = _dot(a_ref[0, rows, :], w_ref[0])
            for s in range(1, N_SHARD):
                y = y + _dot(a_ref[s, rows, :], w_ref[s])
            x2 = x_ref[rows, :] + y
            r = lax.rsqrt(jnp.mean(x2 * x2, axis=-1, keepdims=True) + NORM_EPS)
            xh = x2 * r
            err = xh * g - t_ref[rows, :]
            ls_ref[...] += jnp.sum(jnp.sum(err * err, axis=-1, keepdims=True), axis=0, keepdims=True) * (0.5 / D_MODEL)
            dy = err * (1.0 / D_MODEL)
            dg_ref[...] += jnp.sum(dy * xh, axis=0, keepdims=True)
            dxh = dy * g
            dx = r * (dxh - xh * jnp.mean(dxh * xh, axis=-1, keepdims=True))
            dx_ref[rows, :] = dx
            dxb_ref[rows, :] = dx.astype(BF16)

    row = pl.BlockSpec((tm, D_MODEL), lambda i: (i, 0))
    vec = pl.BlockSpec((1, D_MODEL), lambda i: (0, 0))
    return pl.pallas_call(
        body, out_shape=(SDS((S, D_MODEL), F32), SDS((S, D_MODEL), BF16), SDS((1, D_MODEL), F32), SDS((8, 128), F32)),
        grid=(S // tm,),
        in_specs=[pl.BlockSpec((N_SHARD, tm, HID_S), lambda i: (0, i, 0)),
                  pl.BlockSpec((N_SHARD, HID_S, D_MODEL), lambda i: (0, 0, 0), pipeline_mode=pl.Buffered(1)),
                  row, vec, row],
        out_specs=(row, row, vec, pl.BlockSpec((8, 128), lambda i: (0, 0))),
        compiler_params=_cparams("arbitrary"), name="ffn_down_loss")(act, wd, x1, g3, tgt)


def _ffn_bwd(dx2b, dx2, wd, wg, wu, gte, up, x1, g2):
    S = x1.shape[0]
    tm = 256

    def body(d_ref, dx2_ref, wd_ref, wg_ref, wu_ref, g_ref, u_ref, x_ref, gn_ref,
             dg_ref, du_ref, dx_ref, dxb_ref, dgn_ref):
        @pl.when(pl.program_id(0) == 0)
        def _():
            dgn_ref[...] = jnp.zeros_like(dgn_ref)

        d = d_ref[...]
        dacts = [_dot_nt(d, wd_ref[s]) for s in range(N_SHARD)]
        dgs, dus = [], []
        for s, da in enumerate(dacts):
            g = g_ref[s].astype(F32)
            sg = _sigmoid(g)
            dgs.append((da * u_ref[s].astype(F32) * sg * (1.0 + g * (1.0 - sg))).astype(BF16))
            dus.append((da * g * sg).astype(BF16))
            dg_ref[s] = dgs[s]
            du_ref[s] = dus[s]
        dh = _dot_nt(dgs[0], wg_ref[0]) + _dot_nt(dus[0], wu_ref[0])
        for s in range(1, N_SHARD):
            dh = dh + _dot_nt(dgs[s], wg_ref[s]) + _dot_nt(dus[s], wu_ref[s])
        xv = x_ref[...]
        r = lax.rsqrt(jnp.mean(xv * xv, axis=-1, keepdims=True) + NORM_EPS)
        xh = xv * r
        dgn_ref[...] += jnp.sum(dh * xh, axis=0, keepdims=True)
        dxh = dh * gn_ref[...]
        dx = dx2_ref[...] + r * (dxh - xh * jnp.mean(dxh * xh, axis=-1, keepdims=True))
        dx_ref[...] = dx
        dxb_ref[...] = dx.astype(BF16)

    row = pl.BlockSpec((tm, D_MODEL), lambda i: (i, 0))
    vec = pl.BlockSpec((1, D_MODEL), lambda i: (0, 0))
    aspec = pl.BlockSpec((N_SHARD, tm, HID_S), lambda i: (0, i, 0))
    resident = lambda shape: pl.BlockSpec(shape, lambda i: (0, 0, 0), pipeline_mode=pl.Buffered(1))
    return pl.pallas_call(
        body,
        out_shape=(SDS((N_SHARD, S, HID_S), BF16), SDS((N_SHARD, S, HID_S), BF16),
                   SDS((S, D_MODEL), F32), SDS((S, D_MODEL), BF16), SDS((1, D_MODEL), F32)),
        grid=(S // tm,),
        in_specs=[row, row, resident((N_SHARD, HID_S, D_MODEL)), resident((N_SHARD, D_MODEL, HID_S)),
                  resident((N_SHARD, D_MODEL, HID_S)), aspec, aspec, row, vec],
        out_specs=(aspec, aspec, row, row, vec),
        compiler_params=_cparams("arbitrary"), name="ffn_bwd")(dx2b, dx2, wd, wg, wu, gte, up, x1, g2)


def _wgrad(name, a, b, a_spec, b_spec, out_shape, out_spec, n_par, S):
    tk = min(S, 4096)

    def body(a_ref, b_ref, o_ref):
        @pl.when(pl.program_id(1) == 0)
        def _():
            o_ref[...] = jnp.zeros_like(o_ref)

        o_ref[...] += _dot_tn(a_ref[...], b_ref[...])

    return pl.pallas_call(
        body, out_shape=SDS(out_shape, F32), grid=(n_par, S // tk),
        in_specs=[a_spec(tk), b_spec(tk)], out_specs=out_spec,
        compiler_params=_cparams("parallel", "arbitrary"), name=name)(a, b)


def _out_proj_bwd(dx1b, wo, proj, ya, yr, exchanges=()):
    S = dx1b.shape[0]
    tm = 512
    gate0 = 15 * COLB

    def body(d_ref, w_ref, ga_ref, gr_ref, ya_ref, yr_ref, dya_ref, dyr_ref, dp_ref):
        for rows in _row_pieces(tm, 256):
            dm = _dot_nt(d_ref[rows, :], w_ref[...])
            sa = _sigmoid(ga_ref[rows, :].astype(F32))
            sr = _sigmoid(gr_ref[rows, :].astype(F32))
            dya_ref[rows, :] = (dm * sa).astype(BF16)
            dyr_ref[rows, :] = (dm * sr).astype(BF16)
            dp_ref[rows, 0:D_MODEL] = (dm * ya_ref[rows, :].astype(F32) * sa * (1.0 - sa)).astype(BF16)
            dp_ref[rows, D_MODEL:2 * D_MODEL] = (dm * yr_ref[rows, :].astype(F32) * sr * (1.0 - sr)).astype(BF16)

    row = pl.BlockSpec((tm, D_MODEL), lambda i: (i, 0))
    cols = lambda c0, w: pl.BlockSpec((pl.Element(tm), pl.Element(w)), lambda i: (i * tm, c0))
    return _carrier_call(
        body, (dx1b, wo, proj, proj, ya, yr),
        out_shape=(SDS((S, D_MODEL), BF16), SDS((S, D_MODEL), BF16), SDS((S, PROJ_W), BF16)), grid=(S // tm,),
        in_specs=[row, pl.BlockSpec((D_MODEL, D_MODEL), lambda i: (0, 0), pipeline_mode=pl.Buffered(1)),
                  cols(gate0, D_MODEL), cols(gate0 + D_MODEL, D_MODEL), row, row],
        out_specs=(row, row, cols(gate0, 2 * D_MODEL)),
        sem=("parallel",), name="out_proj_bwd", exchanges=exchanges)


def _branch_bwd(dya, dyr, wa, wr, att):
    S = dya.shape[0]
    tm = 1024

    def body(da_ref, dr_ref, wa_ref, wr_ref, att_ref, datt_ref, rho_ref, dyi_ref):
        datt = _dot_nt(da_ref[...], wa_ref[...])
        datt_ref[...] = datt.astype(BF16)
        dyi_ref[...] = _dot_nt(dr_ref[...], wr_ref[...]).astype(BF16)
        prod = datt * att_ref[...].astype(F32)
        lane = lax.broadcasted_iota(jnp.int32, (tm, 128), 1)
        lo = lane < 64
        rho = jnp.zeros((tm, 128), F32)
        for c in range(4):
            pc = prod[:, c * 128:(c + 1) * 128]
            tot = jnp.sum(pc, axis=-1, keepdims=True)
            low = jnp.sum(jnp.where(lo, pc, 0.0), axis=-1, keepdims=True)
            rho = jnp.where(lane // 16 == 2 * c, low, jnp.where(lane // 16 == 2 * c + 1, tot - low, rho))
        rho_ref[...] = rho

    row = lambda w: pl.BlockSpec((tm, w), lambda i: (i, 0))
    return pl.pallas_call(
        body, out_shape=(SDS((S, 512), BF16), SDS((S, 128), F32), SDS((S, 1024), BF16)), grid=(S // tm,),
        in_specs=[row(1024), row(1024), pl.BlockSpec((512, 1024), lambda i: (0, 0)),
                  pl.BlockSpec((1024, 1024), lambda i: (0, 0)), row(512)],
        out_specs=(row(512), row(128), row(1024)),
        compiler_params=_cparams("parallel"), name="branch_bwd")(dya, dyr, wa, wr, att)


def _attn_bwd(qkv, datt, lse, rho, rtab, d, gi, exchanges=()):
    L = qkv.shape[0]
    nb = L // BLK
    T = d * nb

    def body(q_ref, kc_ref, kp_ref, vc_ref, vp_ref, do_ref, lse_ref, rho_ref, tq_ref, tk_ref,
             dq_ref, dk_ref, dv_ref, ck, cv):
        t = pl.program_id(0)
        n = jnp.minimum(t, T - 1) % nb

        @pl.when(t == 0)
        def _():
            ck[...] = jnp.zeros_like(ck)
            cv[...] = jnp.zeros_like(cv)

        def store_rot(ref, val, t_ref, c):
            sl = slice(c * 128, (c + 1) * 128)
            ref[:, sl] = _unrot(val, t_ref[0], t_ref[1], t_ref[2], 32).astype(BF16)

        @pl.when(t < T)
        def _():
            mask = _band_mask(n)
            mask2 = jnp.concatenate([mask, mask], axis=0)
            lo = lax.broadcasted_iota(jnp.int32, (BLK, 128), 1) < 64

            def stacked(a):
                return jnp.concatenate([jnp.where(lo, a, jnp.zeros_like(a)), jnp.where(lo, jnp.zeros_like(a), a)], axis=0)

            def head_cols(ref, c):
                return jnp.concatenate([jnp.broadcast_to(ref[:, 32 * c:32 * c + 1], (BLK, 2 * BLK)),
                                        jnp.broadcast_to(ref[:, 32 * c + 16:32 * c + 17], (BLK, 2 * BLK))], axis=0)

            ops, raw = [], []
            for c in range(4):
                sl = slice(c * 128, (c + 1) * 128)
                q2, do2 = stacked(q_ref[:, sl]), stacked(do_ref[:, sl])
                k = jnp.concatenate([kp_ref[:, sl], kc_ref[:, sl]], axis=0)
                v = jnp.concatenate([vp_ref[:, sl], vc_ref[:, sl]], axis=0)
                ops.append((q2, do2, k))
                raw.append((_dot_nt(q2, k), _dot_nt(do2, v)))
            grads = []
            for c, (s, dp) in enumerate(raw):
                p = jnp.where(mask2, jnp.exp(s * 0.125 - head_cols(lse_ref, c)), 0.0)
                grads.append(((p * (dp - head_cols(rho_ref, c)) * 0.125).astype(BF16), p.astype(BF16)))
            for c, ((q2, do2, k), (ds, pb)) in enumerate(zip(ops, grads)):
                sl = slice(c * 128, (c + 1) * 128)
                dq2 = _dot(ds, k)
                dq_c = jnp.where(lo, dq2[:BLK], dq2[BLK:])
                dk_c = _dot_tn(ds, q2)
                dv_c = _dot_tn(pb, do2)
                store_rot(dq_ref, dq_c, tq_ref, c)
                store_rot(dk_ref, ck[:, sl] + dk_c[:BLK], tk_ref, c)
                dv_ref[:, sl] = (cv[:, sl] + dv_c[:BLK]).astype(BF16)
                ck[:, sl] = dk_c[BLK:]
                cv[:, sl] = dv_c[BLK:]

        @pl.when(t == T)
        def _():
            for c in range(4):
                sl = slice(c * 128, (c + 1) * 128)
                store_rot(dk_ref, ck[:, sl], tk_ref, c)
            dv_ref[...] = cv[...].astype(BF16)

    blk_of = lambda t: (jnp.minimum(t, T - 1) % nb, jnp.minimum(t, T - 1) // nb)
    cur = lambda t: blk_of(t)
    prev = lambda t: (jnp.maximum(blk_of(t)[0] - 1, 0), blk_of(t)[1])
    fin = lambda t: blk_of(jnp.maximum(t - 1, 0))
    col = _qkv_col(d, gi)
    qkv_spec = lambda kind, which: pl.BlockSpec((BLK, 512), lambda t: (which(t)[0], col(kind, which(t)[1])))
    row_spec = lambda w, which: pl.BlockSpec((BLK, w), lambda t: which(t))
    tab_spec = lambda which: pl.BlockSpec((3, BLK, 128), lambda t: (0, *which(t)))
    return _carrier_call(
        body, (qkv, qkv, qkv, qkv, qkv, datt, lse, rho, rtab, rtab),
        out_shape=(SDS((L, d * 512), BF16),) * 3, grid=(T + 1,),
        in_specs=[qkv_spec(0, cur), qkv_spec(1, cur), qkv_spec(1, prev), qkv_spec(2, cur), qkv_spec(2, prev),
                  row_spec(512, cur), row_spec(128, cur), row_spec(128, cur), tab_spec(cur), tab_spec(fin)],
        out_specs=(row_spec(512, cur), row_spec(512, fin), row_spec(512, fin)),
        scratch_shapes=[pltpu.VMEM((BLK, 512), F32), pltpu.VMEM((BLK, 512), F32)],
        sem=("arbitrary",), name=f"attn_bwd_g{gi}", exchanges=exchanges)


def _ret_bwd(proj, rn, rstd, dyrin, states, tab, consts, dproj, exchanges=()):
    S = proj.shape[0]
    nc = S // BLK
    dmask, zeta, xi, dec = consts

    def body(q_ref, k_ref, v0_ref, v1_ref, g0_ref, g1_ref, rn_ref, rs_ref, dy_ref, st_ref, tq_ref, tk_ref,
             dm_ref, z_ref, x_ref, dec_ref, dp_prev, dp_ref, dR):
        dq_ref, dk_ref = dp_ref.at[:, 0:512], dp_ref.at[:, 512:1024]
        dv_ref, dgr_ref = dp_ref.at[:, 1024:2048], dp_ref.at[:, 2048:3072]

        @pl.when(pl.program_id(0) == 0)
        def _():
            dR[...] = jnp.zeros_like(dR)

        dobs = []
        for h in range(RET_HEADS):
            vs = slice((h % 2) * 256, (h % 2 + 1) * 256)
            os_ = slice(h * 256, (h + 1) * 256)
            gr = (g0_ref if h < 2 else g1_ref)[:, vs].astype(F32)
            sg = _sigmoid(gr)
            rn_v = rn_ref[:, os_].astype(F32)
            dyi = dy_ref[:, os_].astype(F32)
            dgr_ref[:, os_] = (dyi * rn_v * sg * (1.0 + gr * (1.0 - sg))).astype(BF16)
            drn = dyi * gr * sg
            rstd = jnp.broadcast_to(rs_ref[:, 16 * h:16 * h + 1], (BLK, 256))
            do = rstd * (drn - jnp.mean(drn, axis=-1, keepdims=True) - rn_v * jnp.mean(drn * rn_v, axis=-1, keepdims=True))
            dobs.append(do.astype(BF16))
        first = []
        for h in range(RET_HEADS):
            hs = slice(h * 128, (h + 1) * 128)
            q, k = q_ref[:, hs], k_ref[:, hs]
            v = (v0_ref if h < 2 else v1_ref)[:, (h % 2) * 256:(h % 2 + 1) * 256]
            dob, dRb = dobs[h], dR[h].astype(BF16)
            kz = (k.astype(F32) * z_ref[h]).astype(BF16)
            qx = (q.astype(F32) * x_ref[h]).astype(BF16)
            first.append((q, k, _dot_nt(q, k), _dot_nt(dob, v), _dot(kz, dRb), _dot_nt(dob, st_ref[h]),
                          _dot_nt(v, dRb), _dot_tn(qx, dob)))
        masked = [((s * dm_ref[h]).astype(BF16), (dsr * dm_ref[h]).astype(BF16))
                  for h, (_, _, s, dsr, _, _, _, _) in enumerate(first)]
        for h in range(RET_HEADS):
            hs = slice(h * 128, (h + 1) * 128)
            os_ = slice(h * 256, (h + 1) * 256)
            q, k, _, _, dv_state, dq_state, dk_state, dr_new = first[h]
            sD, dS = masked[h]
            dv_ref[:, os_] = (_dot_tn(sD, dobs[h]) + dv_state).astype(BF16)
            dq = _dot(dS, k) + dq_state * x_ref[h]
            dk = _dot_tn(dS, q) + dk_state * z_ref[h]
            dR[h] = dR[h] * dec_ref[h, 0:1, :] + dr_new
            dq_ref[:, hs] = _unrot(dq, tq_ref[0], tq_ref[1], tq_ref[2], 1).astype(BF16)
            dk_ref[:, hs] = _unrot(dk, tk_ref[0], tk_ref[1], tk_ref[2], 1).astype(BF16)

    rc = lambda c: nc - 1 - c
    cst = lambda shape: pl.BlockSpec(shape, lambda c: (0, 0, 0))
    blk = lambda j: pl.BlockSpec((BLK, 512), lambda c: (rc(c), j))
    row = lambda w: pl.BlockSpec((BLK, w), lambda c: (rc(c), 0))
    (dproj,), xres = _carrier_call(
        body, (proj, proj, proj, proj, proj, proj, rn, rstd, dyrin, states, tab, tab, dmask, zeta, xi, dec, dproj),
        out_shape=(SDS((S, PROJ_W), BF16),), grid=(nc,),
        in_specs=[blk(QR_B), blk(KR_B), blk(11), blk(12), blk(13), blk(14), row(1024), row(128), row(1024),
                  pl.BlockSpec((RET_HEADS, None, BLK, 256), lambda c: (0, rc(c), 0, 0)),
                  pl.BlockSpec((None, 3, BLK, 128), lambda c: (1, 0, rc(c), 0)),
                  pl.BlockSpec((None, 3, BLK, 128), lambda c: (2, 0, rc(c), 0)),
                  cst((RET_HEADS, BLK, BLK)), cst((RET_HEADS, BLK, 128)), cst((RET_HEADS, BLK, 128)), cst((RET_HEADS, 8, 256)),
                  ANY],
        out_specs=(pl.BlockSpec((pl.Element(BLK), pl.Element(6 * COLB)), lambda c: (rc(c) * BLK, QR_B * COLB)),),
        scratch_shapes=[pltpu.VMEM((RET_HEADS, BLK, 256), F32)],
        sem=("arbitrary",), name="ret_bwd", exchanges=exchanges, in_out_aliases={16: 0})
    return dproj, xres


def _wgrad_in_half(ht, dproj, sidx, kept, exchanges=()):
    S = dproj.shape[0]
    tk = 2048
    half = (lambda sx: sx[4]) if kept else (lambda sx: 1 - sx[4])

    def body(a_ref, b_ref, o_ref):
        @pl.when(pl.program_id(1) == 0)
        def _():
            o_ref[...] = jnp.zeros_like(o_ref)

        o_ref[...] += _dot(a_ref[...], b_ref[...])

    (g,), xres = _carrier_call(
        body, (ht, dproj), out_shape=(SDS((D_MODEL // 2, PROJ_W), F32),), grid=(N_SHARD, S // tk),
        in_specs=[pl.BlockSpec((D_MODEL // 2, tk), lambda s, k, sx: (half(sx), k)),
                  pl.BlockSpec((tk, W_IN_S), lambda s, k, sx: (k, s))],
        out_specs=(pl.BlockSpec((D_MODEL // 2, W_IN_S), lambda s, k, sx: (0, s)),),
        sem=("parallel", "arbitrary"), name="wgrad_in_kept" if kept else "wgrad_in_sent", exchanges=exchanges,
        prefetch=sidx)
    return g, xres


def _in_proj_bwd(dproj, w_in, x, g1, dx1, exchanges=()):
    S = x.shape[0]
    tm = 1024

    def body(d_ref, w_ref, x_ref, g_ref, dx1_ref, dx_ref, dgn_ref, acc):
        i, s = pl.program_id(0), pl.program_id(1)

        @pl.when(s == 0)
        def _():
            acc[...] = jnp.zeros_like(acc)

        @pl.when((i == 0) & (s == 0))
        def _():
            dgn_ref[...] = jnp.zeros_like(dgn_ref)

        acc[...] += _dot_nt(d_ref[...], w_ref[...])

        @pl.when(s == N_SHARD - 1)
        def _():
            xv = x_ref[...]
            r = lax.rsqrt(jnp.mean(xv * xv, axis=-1, keepdims=True) + NORM_EPS)
            xh = xv * r
            dh = acc[...]
            dgn_ref[...] += jnp.sum(dh * xh, axis=0, keepdims=True)
            dxh = dh * g_ref[...]
            dx_ref[...] = dx1_ref[...] + r * (dxh - xh * jnp.mean(dxh * xh, axis=-1, keepdims=True))

    row = pl.BlockSpec((tm, D_MODEL), lambda i, s: (i, 0))
    vec = pl.BlockSpec((1, D_MODEL), lambda i, s: (0, 0))
    (gx, dg), xres = _carrier_call(
        body, (dproj, w_in, x, g1, dx1),
        out_shape=(SDS((S, D_MODEL), F32), SDS((1, D_MODEL), F32)), grid=(S // tm, N_SHARD),
        in_specs=[pl.BlockSpec((tm, W_IN_S), lambda i, s: (i, s)),
                  pl.BlockSpec((D_MODEL, W_IN_S), lambda i, s: (0, s)), row, vec, row],
        out_specs=(row, vec), scratch_shapes=[pltpu.VMEM((tm, D_MODEL), F32)],
        sem=("arbitrary", "arbitrary"), name="in_proj_bwd", exchanges=exchanges)
    return gx, dg, xres


def _sub_view(a, d):
    S, W = a.shape
    return a.reshape(S // d, d * W)


def _step(x, tgt, g1, g2, g3, comm):
    S = x.shape[0]
    tab_np = _tables(S)
    tab = jnp.asarray(tab_np)
    consts = _ret_consts()

    (h, ht), xres = _rms_fwd(x, g1, comm.carry("rms_fwd"))
    comm.took("rms_fwd", xres)
    w_in = comm.weight(0)
    proj, xres = _in_proj(h, w_in, tab, comm.carry("in_proj"))
    comm.took("in_proj", xres)
    qkvs, o_parts, lse_parts = [], [], []
    for gi, d in enumerate(DILATIONS):
        qkv = proj if d == 1 else _qkv_to_sub(proj, d, gi)
        (o_g, lse_g), xres = _attn_fwd(qkv, d, gi, comm.carry(f"attn_fwd_g{gi}"))
        comm.took(f"attn_fwd_g{gi}", xres)
        qkvs.append(qkv)
        o_parts.append(o_g)
        lse_parts.append(lse_g)
    att, lse_tot = _attn_merge(o_parts, lse_parts)
    (yrin, rn, rstd, states), xres = _ret_fwd(proj, consts, comm.carry("ret_fwd"))
    comm.took("ret_fwd", xres)
    wa, wr, wo = comm.weight(1), comm.weight(2), comm.weight(3)
    merged, ya, yr = _branch_merge(att, yrin, proj, wa, wr)
    (x1, h2), xres = _out_proj(merged, wo, x, g2, comm.carry("out_proj"))
    comm.took("out_proj", xres)
    wg, wu = comm.weight(4), comm.weight(5)
    (gte, up, act), xres = _ffn_up(h2, wg, wu, comm.carry("ffn_up"))
    comm.took("ffn_up", xres)
    wd = comm.weight(6)
    dx2, dx2b, dg3, loss_p = _ffn_down_loss(act, wd, x1, g3, tgt)

    dgte, dup, dx1, dx1b, dg2 = _ffn_bwd(dx2b, dx2, wd, wg, wu, gte, up, x1, g2)
    tok3 = lambda w: (lambda tk: pl.BlockSpec((None, tk, w), lambda p, k: (p, k, 0)))
    tok2 = lambda w: (lambda tk: pl.BlockSpec((tk, w), lambda p, k: (k, 0)))
    g_d = _wgrad("wgrad_down", act, dx2b, tok3(HID_S), tok2(D_MODEL), (N_SHARD, HID_S, D_MODEL),
                 pl.BlockSpec((None, HID_S, D_MODEL), lambda p, k: (p, 0, 0)), N_SHARD, S)
    g_g = _wgrad("wgrad_gate", h2, dgte, tok2(D_MODEL), tok3(HID_S), (N_SHARD, D_MODEL, HID_S),
                 pl.BlockSpec((None, D_MODEL, HID_S), lambda p, k: (p, 0, 0)), N_SHARD, S)
    g_u = _wgrad("wgrad_up", h2, dup, tok2(D_MODEL), tok3(HID_S), (N_SHARD, D_MODEL, HID_S),
                 pl.BlockSpec((None, D_MODEL, HID_S), lambda p, k: (p, 0, 0)), N_SHARD, S)
    comm.grads({4: g_g, 5: g_u, 6: g_d})
    (dya, dyr, dproj), xres = _out_proj_bwd(dx1b, wo, proj, ya, yr, comm.carry("out_proj_bwd"))
    comm.took("out_proj_bwd", xres)
    colblk = lambda w: (lambda tk: pl.BlockSpec((tk, w), lambda p, k: (k, p)))
    g_o = _wgrad("wgrad_out", merged, dx1b, colblk(256), tok2(D_MODEL), (D_MODEL, D_MODEL),
                 pl.BlockSpec((256, D_MODEL), lambda p, k: (p, 0)), 4, S)
    datt, rho, dyrin = _branch_bwd(dya, dyr, wa, wr, att)
    g_a = _wgrad("wgrad_attn", att, dya, tok2(512), colblk(512), (512, D_MODEL),
                 pl.BlockSpec((512, 512), lambda p, k: (0, p)), 2, S)
    g_r = _wgrad("wgrad_ret", yrin, dyr, colblk(256), tok2(D_MODEL), (D_MODEL, D_MODEL),
                 pl.BlockSpec((256, D_MODEL), lambda p, k: (p, 0)), 4, S)
    comm.grads({1: g_a, 2: g_r.reshape(N_SHARD, 256, D_MODEL), 3: g_o.reshape(N_SHARD, 256, D_MODEL)})
    dproj, xres = _ret_bwd(proj, rn, rstd, dyrin, states, tab, consts, dproj, comm.carry("ret_bwd"))
    comm.took("ret_bwd", xres)
    dqs, dks, dvs = [], [], []
    for gi, d in enumerate(DILATIONS):
        rtab = jnp.asarray(tab_np[0].reshape(3, S // d, d * 128))
        (dq, dk, dv), xres = _attn_bwd(qkvs[gi], _sub_view(datt, d), _sub_view(lse_tot, d), _sub_view(rho, d), rtab, d, gi,
                                       comm.carry(f"attn_bwd_g{gi}"))
        comm.took(f"attn_bwd_g{gi}", xres)
        dqs.append(dq)
        dks.append(dk)
        dvs.append(dv)
    dproj = _assemble_dproj((dqs, dks, dvs), dproj)
    g_sent, xres = _wgrad_in_half(ht, dproj, comm.sidx, False, comm.carry("wgrad_in_sent"))
    comm.took("wgrad_in_sent", xres)
    comm.grads({"in_sent": g_sent})
    g_kept, xres = _wgrad_in_half(ht, dproj, comm.sidx, True, comm.carry("wgrad_in_kept"))
    comm.grads({"in_kept": g_kept})
    comm.took("wgrad_in_kept", xres)
    grad_x, dg1, xres = _in_proj_bwd(dproj, w_in, x, g1, dx1, comm.carry("in_proj_bwd"))
    comm.took("in_proj_bwd", xres)
    return loss_p, grad_x, (dg1, dg2, dg3)


W_KINDS = ("col", "col", "lead", "lead", "lead", "lead", "lead")
W_SHARD = ((1024, W_IN_S), (512, 256), (256, 1024), (256, 1024), (1024, HID_S), (1024, HID_S), (HID_S, 1024))
N_W = len(W_KINDS)


def _full_shape(wi):
    R, C = W_SHARD[wi]
    return (R, N_SHARD * C) if W_KINDS[wi] == "col" else (N_SHARD, R, C)


def _view(ref, wi, s, half):
    R, C = W_SHARD[wi]
    rows = pl.ds(half * (R // 2), R // 2)
    if W_KINDS[wi] == "col":
        return ref.at[rows, pl.ds(pl.multiple_of(s * C, 128), C)]
    return ref.at[s, rows, :]


def _mesh_pos():
    x, y, c = lax.axis_index("x"), lax.axis_index("y"), lax.axis_index("c")
    chips = [(1 - x, y), (x, 1 - y), (1 - x, 1 - y)]
    return x, y, c, chips


def _cast_bf16(a):
    R, C = a.shape
    tr = R // 2 if R % 32 == 0 else R

    def body(a_ref, o_ref):
        o_ref[...] = a_ref[...].astype(BF16)

    spec = pl.BlockSpec((tr, C), lambda i: (i, 0))
    return pl.pallas_call(body, out_shape=SDS((R, C), BF16), grid=(R // tr,), in_specs=[spec], out_specs=spec,
                          compiler_params=_cparams("parallel"), name=f"cast_{R}x{C}")(a)


def _remote(send, recv, k, src, dst, to):
    return pltpu.make_async_remote_copy(src_ref=src, dst_ref=dst, send_sem=send.at[k], recv_sem=recv.at[k],
                                        device_id=to, device_id_type=MESH)


def _ex_gather_ring(wis, shards):
    n = len(wis)

    def build(sh, full, send, recv, loc):
        x, y, c, _ = _mesh_pos()
        s_me, sib = 2 * x + y, (x, y, 1 - c)
        xn, yn = (1 - x, y), (x, 1 - y)
        flip = lambda a, b: a + b - 2 * a * b
        via = (flip(x, 1 - c), flip(y, c))
        onto = (flip(x, c), flip(y, 1 - c))
        shard_of = lambda chip: 2 * chip[0] + chip[1]
        starts, waits, sent = [], [], []
        for i, wi in enumerate(wis):
            Rh = W_SHARD[wi][0] // 2
            for hf in range(2):
                cp = pltpu.make_async_copy(sh[i].at[pl.ds(hf * Rh, Rh), :], _view(full[i], wi, s_me, hf), loc.at[2 * i + hf])
                starts.append(cp)
                sent.append(cp.wait)
            for j, chip in enumerate((xn, yn)):
                cp = _remote(send, recv, 6 * i + j, sh[i].at[pl.ds(c * Rh, Rh), :], _view(full[i], wi, s_me, c), (*chip, c))
                starts.append(cp)
                sent.append(cp.wait_send)

        def pass_to_sibling(i, wi, k, s):
            mine = _view(full[i], wi, s, c)
            fw = _remote(send, recv, 6 * i + k, mine, mine, sib)
            waits.append(fw.start)
            sent.append(fw.wait_send)

        for i, wi in enumerate(wis):
            for j, chip in enumerate((xn, yn)):
                land = _view(full[i], wi, shard_of(chip), c)
                waits.append(_remote(send, recv, 6 * i + j, land, land, (*chip, c)).wait_recv)
                pass_to_sibling(i, wi, 3 + j, shard_of(chip))
            relay = _view(full[i], wi, shard_of(via), c)
            fw = _remote(send, recv, 6 * i + 2, relay, relay, (*onto, c))
            waits.append(fw.start)
            sent.append(fw.wait_send)
        s_diag = 2 * (1 - x) + (1 - y)
        for i, wi in enumerate(wis):
            land = _view(full[i], wi, s_diag, c)
            waits.append(_remote(send, recv, 6 * i + 2, land, land, (*onto, c)).wait_recv)
            pass_to_sibling(i, wi, 5, s_diag)
        for i, wi in enumerate(wis):
            for k, s in ((3, shard_of(xn)), (4, shard_of(yn)), (5, s_diag)):
                land = _view(full[i], wi, s, 1 - c)
                waits.append(_remote(send, recv, 6 * i + k, land, land, sib).wait_recv)
        return starts, waits + sent

    return _Exchange(shards, [SDS(_full_shape(wi), BF16) for wi in wis], {}, 6 * n, 2 * n, build)


def _ex_gather_ici(wis, shards, then_d2d=False):
    n = len(wis)

    def build(ins, outs, send, recv, loc):
        x, y, c, chips = _mesh_pos()
        s_me, sib = 2 * x + y, (x, y, 1 - c)
        starts, waits, after = [], [], []
        for i, wi in enumerate(wis):
            Rh = W_SHARD[wi][0] // 2
            for hf in range(2):
                cp = pltpu.make_async_copy(ins[i].at[pl.ds(hf * Rh, Rh), :], _view(outs[i], wi, s_me, hf), loc.at[2 * i + hf])
                starts.append(cp)
                waits.append(cp.wait)
            for j, chip in enumerate(chips):
                cp = _remote(send, recv, 3 * i + j, ins[i].at[pl.ds(c * Rh, Rh), :], _view(outs[i], wi, s_me, c), (*chip, c))
                land = _view(outs[i], wi, 2 * chip[0] + chip[1], c)
                starts.append(cp)
                waits += [cp.wait_send, _remote(send, recv, 3 * i + j, land, land, (*chip, c)).wait_recv]
                if then_d2d:
                    theirs = _view(outs[i], wi, 2 * chip[0] + chip[1], 1 - c)
                    fw = _remote(send, recv, 3 * n + 3 * i + j, land, land, sib)
                    waits.append(fw.start)
                    after += [fw.wait_send, _remote(send, recv, 3 * n + 3 * i + j, theirs, theirs, sib).wait_recv]
        return starts, waits + after

    return _Exchange(shards, [SDS(_full_shape(wi), BF16) for wi in wis], {}, (6 if then_d2d else 3) * n, 2 * n, build)


def _ex_gather_d2d(wis, fulls):
    def build(ins, outs, send, recv, loc):
        x, y, c, chips = _mesh_pos()
        sib = (x, y, 1 - c)
        starts, waits = [], []
        for i, wi in enumerate(wis):
            for j, chip in enumerate(chips):
                mine = _view(outs[i], wi, 2 * chip[0] + chip[1], c)
                theirs = _view(outs[i], wi, 2 * chip[0] + chip[1], 1 - c)
                cp = _remote(send, recv, 3 * i + j, mine, mine, sib)
                starts.append(cp)
                waits += [cp.wait_send, _remote(send, recv, 3 * i + j, theirs, theirs, sib).wait_recv]
        return starts, waits

    return _Exchange(fulls, [SDS(f.shape, BF16) for f in fulls], {i: i for i in range(len(wis))}, 3 * len(wis), 0, build)


def _half_shape(wi):
    R, C = W_SHARD[wi]
    return (R // 2, N_SHARD * C) if W_KINDS[wi] == "col" else (N_SHARD, R // 2, C)


def _ex_pair(wis, grads):
    def build(ins, outs, send, recv, loc):
        x, y, c, _ = _mesh_pos()
        starts, waits = [], []
        for i, wi in enumerate(wis):
            Rh = W_SHARD[wi][0] // 2
            rows = pl.ds((1 - c) * Rh, Rh)
            if tuple(ins[i].shape) == _half_shape(wi):
                src = ins[i]
            else:
                src = ins[i].at[rows, :] if W_KINDS[wi] == "col" else ins[i].at[:, rows, :]
            cp = _remote(send, recv, i, src, outs[i], (x, y, 1 - c))
            starts.append(cp)
            waits.append(cp.wait)
        return starts, waits

    return _Exchange(grads, [SDS(_half_shape(wi), F32) for wi in wis], {}, len(wis), 0, build)


def _ex_chip(wis, pbs):
    def build(ins, outs, send, recv, loc):
        x, y, c, chips = _mesh_pos()
        starts, waits = [], []
        for i, wi in enumerate(wis):
            for j, chip in enumerate(chips):
                cp = _remote(send, recv, 3 * i + j, ins[i].at[j], outs[i].at[j], (*chip, c))
                starts.append(cp)
                waits.append(cp.wait)
        return starts, waits

    shapes = [SDS((3, W_SHARD[wi][0] // 2, W_SHARD[wi][1]), BF16) for wi in wis]
    return _Exchange(pbs, shapes, {}, 3 * len(wis), 0, build)


def _ex_share(wis, halves):
    def build(ins, outs, send, recv, loc):
        x, y, c, _ = _mesh_pos()
        sib = (x, y, 1 - c)
        starts, waits = [], []
        for i, wi in enumerate(wis):
            cp = _remote(send, recv, i, outs[i].at[c], outs[i].at[c], sib)
            starts.append(cp)
            waits += [cp.wait_send, _remote(send, recv, i, outs[i].at[1 - c], outs[i].at[1 - c], sib).wait_recv]
        return starts, waits

    return _Exchange(halves, [SDS(h.shape, F32) for h in halves], {i: i for i in range(len(wis))}, len(wis), 0, build)


def _row_tile(rh, C):
    best = 16
    for t in range(16, rh + 1, 16):
        if rh % t == 0 and t * C * 4 <= (3 << 19):
            best = t
    return best


def _pair_sum(wi, g, ra, sidx):
    R, C = W_SHARD[wi]
    Rh = R // 2
    tr = _row_tile(Rh, C)
    nt = Rh // tr
    off = 0 if tuple(g.shape) == _half_shape(wi) else nt
    col = W_KINDS[wi] == "col"

    def body(sidx_ref, *refs):
        gs, rs = refs[:4], refs[4:8]
        own_ref, pb_ref = refs[8:]
        own_ref[...] = gs[0][...] + rs[0][...]
        for j in range(3):
            pb_ref[j] = (gs[1 + j][...] + rs[1 + j][...]).astype(BF16)

    def gspec(slot):
        if col:
            return pl.BlockSpec((tr, C), lambda i, sx: (sx[4] * off + i, sx[slot]))
        return pl.BlockSpec((None, tr, C), lambda i, sx: (sx[slot], sx[4] * off + i, 0))

    def rspec(slot):
        if col:
            return pl.BlockSpec((tr, C), lambda i, sx: (i, sx[slot]))
        return pl.BlockSpec((None, tr, C), lambda i, sx: (sx[slot], i, 0))

    return pl.pallas_call(
        body, out_shape=(SDS((Rh, C), F32), SDS((3, Rh, C), BF16)),
        grid_spec=pltpu.PrefetchScalarGridSpec(
            num_scalar_prefetch=1, grid=(nt,),
            in_specs=[gspec(k) for k in range(4)] + [rspec(k) for k in range(4)],
            out_specs=(pl.BlockSpec((tr, C), lambda i, sx: (i, 0)), pl.BlockSpec((3, tr, C), lambda i, sx: (0, i, 0)))),
        compiler_params=_cparams("arbitrary"), name=f"pair_sum_w{wi}")(sidx, g, g, g, g, ra, ra, ra, ra)


def _chip_sum(wi, own, rb, sidx):
    R, C = W_SHARD[wi]
    Rh = R // 2
    tr = _row_tile(Rh, C)

    def body(sidx_ref, own_ref, rb_ref, o_ref):
        o_ref[...] = ((own_ref[...] + rb_ref[0].astype(F32)) + rb_ref[1].astype(F32)) + rb_ref[2].astype(F32)

    return pl.pallas_call(
        body, out_shape=SDS((2, Rh, C), F32),
        grid_spec=pltpu.PrefetchScalarGridSpec(
            num_scalar_prefetch=1, grid=(Rh // tr,),
            in_specs=[pl.BlockSpec((tr, C), lambda i, sx: (i, 0)), pl.BlockSpec((3, tr, C), lambda i, sx: (0, i, 0))],
            out_specs=pl.BlockSpec((None, tr, C), lambda i, sx: (sx[4], i, 0))),
        compiler_params=_cparams("arbitrary"), name=f"chip_sum_w{wi}")(sidx, own, rb)


def _gain_allgather(blk, ex):
    m_per, n = blk.shape
    n_in, n_out = len(ex.ins), len(ex.out_shapes)

    def body(x_ref, *rest):
        xin, out_ref, xout = rest[:n_in], rest[n_in], rest[n_in + 1:n_in + 1 + n_out]
        send_sems, recv_sems, local_sem = rest[n_in + 1 + n_out:n_in + 4 + n_out]
        ex_starts, ex_waits = ex.build(xin, xout, *rest[n_in + 4 + n_out:])
        for cp in ex_starts:
            cp.start()
        x, y, c, chips = _mesh_pos()
        me, sibling = (x, y, c), (x, y, 1 - c)

        def rows(px, py, pc):
            return out_ref.at[pl.ds((4 * px + 2 * py + pc) * m_per, m_per), :]

        def copy(k, block, to, src=None):
            return pltpu.make_async_remote_copy(
                src_ref=rows(*block) if src is None else src, dst_ref=rows(*block),
                send_sem=send_sems.at[k], recv_sem=recv_sems.at[k], device_id=to, device_id_type=MESH)

        mine = pltpu.make_async_copy(x_ref, rows(*me), local_sem)
        mine.start()
        first = [copy(0, me, sibling, src=x_ref)]
        first += [copy(1 + j, me, (*chip, c), src=x_ref) for j, chip in enumerate(chips)]
        for cp in first:
            cp.start()
        passed = [copy(4 + j, (*chip, c), sibling) for j, chip in enumerate(chips)]
        for j, chip in enumerate(chips):
            copy(1 + j, (*chip, c), me).wait_recv()
            passed[j].start()
        copy(0, sibling, me).wait_recv()
        for j, chip in enumerate(chips):
            copy(4 + j, (*chip, 1 - c), me).wait_recv()
        for cp in first + passed:
            cp.wait_send()
        mine.wait()
        for w in ex_waits:
            w()

    vm = pl.BlockSpec(memory_space=pltpu.VMEM)
    res = pl.pallas_call(
        body, out_shape=(SDS((8 * m_per, n), blk.dtype), *ex.out_shapes),
        in_specs=[vm] + [ANY] * n_in, out_specs=(vm, *[ANY] * n_out),
        input_output_aliases={1 + a: 1 + o for a, o in ex.aliases.items()},
        scratch_shapes=[pltpu.SemaphoreType.DMA((7,)), pltpu.SemaphoreType.DMA((7,)), pltpu.SemaphoreType.DMA] + ex.sems(),
        name="gain_allgather")(blk, *ex.ins)
    return res[0], tuple(res[1:])


def _adam_math(w, g, m, v):
    mn = ADAM_B1 * m + (1.0 - ADAM_B1) * g
    vn = ADAM_B2 * v + (1.0 - ADAM_B2) * (g * g)
    mh = mn / (1.0 - ADAM_B1 ** ADAM_STEP)
    vh = vn / (1.0 - ADAM_B2 ** ADAM_STEP)
    return -ADAM_LR * (mh / (jnp.sqrt(vh) + ADAM_EPS) + ADAM_WD * w), mn, vn


def _adamw(wi, w, g, m, v):
    R, C = w.shape
    tr = _row_tile(R, C)

    def body(w_ref, g_ref, m_ref, v_ref, go_ref, d_ref, mn_ref, vn_ref):
        g = g_ref[...]
        go_ref[...] = g
        d_ref[...], mn_ref[...], vn_ref[...] = _adam_math(w_ref[...], g, m_ref[...], v_ref[...])

    spec = pl.BlockSpec((tr, C), lambda i: (i, 0))
    return pl.pallas_call(body, out_shape=(SDS((R, C), F32),) * 4, grid=(R // tr,), in_specs=[spec] * 4,
                          out_specs=(spec,) * 4, compiler_params=_cparams("parallel"), name=f"adamw_w{wi}")(w, g, m, v)


def _gain_update(gathered, w, m, v):
    def body(ga_ref, w_ref, m_ref, v_ref, g_ref, d_ref, mn_ref, vn_ref):
        g = ga_ref[0:8, :]
        for dev in range(1, 8):
            g = g + ga_ref[8 * dev:8 * dev + 8, :]
        g_ref[...] = g
        d_ref[...], mn_ref[...], vn_ref[...] = _adam_math(w_ref[...], g, m_ref[...], v_ref[...])

    return pl.pallas_call(body, out_shape=(SDS((8, 1024), F32),) * 4, name="gain_update")(gathered, w, m, v)


GROUP_FFN, GROUP_MIX, GROUP_IN = (4, 5, 6), (1, 2, 3), (0,)
REST = GROUP_MIX + GROUP_FFN


class _MeshComm:
    SCHEDULE = {
        "rms_fwd": [("ring", GROUP_IN)],
        "in_proj": [("ici", (1, 2, 3, 4))],
        "ret_fwd": [("d2d", (1, 2, 3, 4)), ("ici", (5,))],
        "out_proj": [("d2d", (5,))],
        "ffn_up": [("both", (6,))],
        "out_proj_bwd": [("pair", GROUP_FFN)],
        "ret_bwd": [("pair", GROUP_MIX), ("chip", (4,))],
        "attn_bwd_g0": [("chip", (5,))],
        "attn_bwd_g1": [("chip", (6,))],
        "attn_bwd_g2": [("chip", GROUP_MIX)],
        "wgrad_in_kept": [("pair", GROUP_IN), ("share", GROUP_FFN + GROUP_MIX)],
        "in_proj_bwd": [("chip", GROUP_IN)],
    }

    def __init__(self, shards):
        xi, yi, ci = lax.axis_index("x"), lax.axis_index("y"), lax.axis_index("c")
        self.sidx = jnp.stack([2 * xi + yi, 2 * (1 - xi) + yi, 2 * xi + (1 - yi), 2 * (1 - xi) + (1 - yi), ci]).astype(jnp.int32)
        self.shards, self.full = shards, {}
        self.g, self.own, self.pb, self.half, self.red = {}, {}, {}, {}, {}

    def weight(self, wi):
        return self.full[wi].reshape(D_MODEL, D_MODEL) if wi in (2, 3) else self.full[wi]

    def grads(self, by_wi):
        self.g.update(by_wi)

    def _exchange(self, stage, wis):
        pick = lambda table: [table[wi] for wi in wis]
        if stage == "ring":
            return _ex_gather_ring(wis, pick(self.shards))
        if stage == "ici":
            return _ex_gather_ici(wis, pick(self.shards))
        if stage == "both":
            return _ex_gather_ici(wis, pick(self.shards), then_d2d=True)
        if stage == "d2d":
            return _ex_gather_d2d(wis, pick(self.full))
        if stage == "pair":
            return _ex_pair(wis, [self.g["in_sent"] if wi == 0 else self.g[wi] for wi in wis])
        if stage == "chip":
            return _ex_chip(wis, pick(self.pb))
        return _ex_share(wis, pick(self.half))

    def _landed(self, stage, wis, res):
        for wi, r in zip(wis, res):
            if stage in ("ring", "ici", "d2d", "both"):
                self.full[wi] = r
            elif stage == "pair":
                self.own[wi], self.pb[wi] = _pair_sum(wi, self.g["in_kept"] if wi == 0 else self.g[wi], r, self.sidx)
            elif stage == "chip":
                self.half[wi] = _chip_sum(wi, self.own[wi], r, self.sidx)
            else:
                self.red[wi] = r

    def carry(self, point):
        return [self._exchange(stage, wis) for stage, wis in self.SCHEDULE.get(point, ())]

    def took(self, point, xres):
        for (stage, wis), res in zip(self.SCHEDULE.get(point, ()), xres):
            self._landed(stage, wis, res)

    def last_share(self):
        return self._exchange("share", GROUP_IN)

    def reduced(self, last_shared):
        self._landed("share", GROUP_IN, last_shared)
        return [self.red[wi] for wi in range(N_W)]


def kernel(x, norm_mix_g, w_in, w_out_attn, w_out_ret, w_out, norm_ffn_g, w_ffn_gate, w_ffn_up, w_ffn_down, norm_final_g, loss_target, m_norm_mix_g, m_w_in, m_w_out_attn, m_w_out_ret, m_w_out, m_norm_ffn_g, m_w_ffn_gate, m_w_ffn_up, m_w_ffn_down, m_norm_final_g, v_norm_mix_g, v_w_in, v_w_out_attn, v_w_out_ret, v_w_out, v_norm_ffn_g, v_w_ffn_gate, v_w_ffn_up, v_w_ffn_down, v_norm_final_g):
    ws = (w_in, w_out_attn, w_out_ret, w_out, w_ffn_gate, w_ffn_up, w_ffn_down)
    ms = (m_w_in, m_w_out_attn, m_w_out_ret, m_w_out, m_w_ffn_gate, m_w_ffn_up, m_w_ffn_down)
    vs = (v_w_in, v_w_out_attn, v_w_out_ret, v_w_out, v_w_ffn_gate, v_w_ffn_up, v_w_ffn_down)
    shard2d = lambda a, wi: a.reshape(W_SHARD[wi])

    comm = _MeshComm([_cast_bf16(shard2d(w, wi)) for wi, w in enumerate(ws)])
    g3 = norm_final_g.reshape(1, D_MODEL)
    loss_p, grad_x, gain_g = _step(x[0], loss_target[0], norm_mix_g, norm_ffn_g, g3, comm)

    pad8 = lambda rows: jnp.concatenate([r.reshape(1, D_MODEL) for r in rows]
                                        + [jnp.zeros((8 - len(rows), D_MODEL), F32)], axis=0)
    gathered, shared = _gain_allgather(pad8((*gain_g, jnp.tile(loss_p[0:1], (1, D_MODEL // 128)))), comm.last_share())
    gred = comm.reduced(shared)

    outs_g, outs_d, outs_m, outs_v = [], [], [], []
    for wi in range(N_W):
        g2d = gred[wi].reshape(W_SHARD[wi])
        gout, dlt, mn, vn = _adamw(wi, shard2d(ws[wi], wi), g2d, shard2d(ms[wi], wi), shard2d(vs[wi], wi))
        for lst, a in ((outs_g, gout), (outs_d, dlt), (outs_m, mn), (outs_v, vn)):
            lst.append(a.reshape(ws[wi].shape))

    gg, gd, gm, gv = _gain_update(gathered, pad8((norm_mix_g, norm_ffn_g, norm_final_g)),
                                  pad8((m_norm_mix_g, m_norm_ffn_g, m_norm_final_g)),
                                  pad8((v_norm_mix_g, v_norm_ffn_g, v_norm_final_g)))
    loss = gg[3, 0]

    def assemble(gain_rows, wlist):
        return (gain_rows[0:1], wlist[0], wlist[1], wlist[2], wlist[3], gain_rows[1:2],
                wlist[4], wlist[5], wlist[6], gain_rows[2])

    return (loss, grad_x[None], *assemble(gg, outs_g), *assemble(gd, outs_d), *assemble(gm, outs_m), *assemble(gv, outs_v))
```

```python
import functools
import math

import numpy as np
import jax
import jax.numpy as jnp
from jax import lax
from jax.experimental import pallas as pl
from jax.experimental.pallas import tpu as pltpu

F32, BF16 = jnp.float32, jnp.bfloat16
SDS = jax.ShapeDtypeStruct
MESH = pl.DeviceIdType.MESH

D_MODEL = 1024
PROJ_W = 9728
COLB = 512
N_COLB = PROJ_W // COLB
QA_B, KA_B, VA_B = 0, 3, 6
QR_B, KR_B = 9, 10
FFN_HID = 2816
N_SHARD = 4
HID_S = FFN_HID // N_SHARD
W_IN_S = PROJ_W // N_SHARD
DILATIONS = (1, 4, 16)
BLK = 128
RET_HEADS = 4
ROPE_THETA = 10000.0
NORM_EPS = 1e-6
ADAM_LR, ADAM_B1, ADAM_B2, ADAM_EPS, ADAM_WD, ADAM_STEP = 0.001, 0.9, 0.999, 1e-08, 0.01, 10
VMEM_LIMIT = 56 << 20


def _cparams(*sem):
    return pltpu.CompilerParams(dimension_semantics=sem or None, vmem_limit_bytes=VMEM_LIMIT)


def _dot(a, b):
    return jnp.dot(a, b, preferred_element_type=F32)


def _dot_nt(a, b):
    return lax.dot_general(a, b, (((1,), (1,)), ((), ())), preferred_element_type=F32)


def _dot_tn(a, b):
    return lax.dot_general(a, b, (((0,), (0,)), ((), ())), preferred_element_type=F32)


def _row_pieces(tm, sub=512):
    return [slice(i, i + sub) for i in range(0, tm, sub)]


def _sigmoid(z):
    return 0.5 * jnp.tanh(0.5 * z) + 0.5


ANY = pl.BlockSpec(memory_space=pl.ANY)


class _Exchange:
    def __init__(self, ins, out_shapes, aliases, n_sem, n_loc, build):
        self.ins, self.out_shapes, self.aliases = list(ins), list(out_shapes), dict(aliases)
        self.n_sem, self.n_loc, self.build = n_sem, n_loc, build

    def sems(self):
        return [pltpu.SemaphoreType.DMA((self.n_sem,)), pltpu.SemaphoreType.DMA((self.n_sem,)),
                pltpu.SemaphoreType.DMA((max(self.n_loc, 1),))]


def _carrier_call(body, args, *, out_shape, grid, in_specs, out_specs, scratch_shapes=(), sem, name, exchanges=(),
                  prefetch=None, in_out_aliases=None):
    out_shape, out_specs = tuple(out_shape), tuple(out_specs)
    n_in, n_out, n_scr = len(args), len(out_shape), len(scratch_shapes)
    n_pre = 0 if prefetch is None else 1
    x_args, x_outs, x_scr, spans = [], [], [], []
    aliases = {n_pre + a: o for a, o in (in_out_aliases or {}).items()}
    for ex in exchanges:
        i0, o0 = len(x_args), len(x_outs)
        for a, o in ex.aliases.items():
            aliases[n_pre + n_in + i0 + a] = n_out + o0 + o
        x_args += ex.ins
        x_outs += ex.out_shapes
        x_scr += ex.sems()
        spans.append((i0, len(ex.ins), o0, len(ex.out_shapes)))
    nx_in, nx_out = len(x_args), len(x_outs)

    def wrapped(*refs):
        refs = refs[n_pre:]
        ins, xin = refs[:n_in], refs[n_in:n_in + nx_in]
        o_base = n_in + nx_in
        outs, xout = refs[o_base:o_base + n_out], refs[o_base + n_out:o_base + n_out + nx_out]
        s_base = o_base + n_out + nx_out
        scr, xs = refs[s_base:s_base + n_scr], refs[s_base + n_scr:]

        def built(e):
            i0, ni, o0, no = spans[e]
            return exchanges[e].build(xin[i0:i0 + ni], xout[o0:o0 + no], *xs[3 * e:3 * e + 3])

        if exchanges:
            first = functools.reduce(jnp.logical_and, [pl.program_id(k) == 0 for k in range(len(grid))])
            last = functools.reduce(jnp.logical_and, [pl.program_id(k) == grid[k] - 1 for k in range(len(grid))])

            @pl.when(first)
            def _():
                for e in range(len(exchanges)):
                    for cp in built(e)[0]:
                        cp.start()

        body(*ins, *outs, *scr)

        if exchanges:
            @pl.when(last)
            def _():
                for e in range(len(exchanges)):
                    for w in built(e)[1]:
                        w()

    all_in, all_out = list(in_specs) + [ANY] * nx_in, out_specs + tuple([ANY] * nx_out)
    all_scr = list(scratch_shapes) + x_scr
    cparams = _cparams(*(sem if not exchanges else ("arbitrary",) * len(grid)))
    if prefetch is None:
        res = pl.pallas_call(wrapped, out_shape=out_shape + tuple(x_outs), grid=grid, in_specs=all_in, out_specs=all_out,
                             scratch_shapes=all_scr, input_output_aliases=aliases, compiler_params=cparams,
                             name=name)(*args, *x_args)
    else:
        gs = pltpu.PrefetchScalarGridSpec(num_scalar_prefetch=1, grid=grid, in_specs=all_in, out_specs=all_out,
                                          scratch_shapes=all_scr)
        res = pl.pallas_call(wrapped, out_shape=out_shape + tuple(x_outs), grid_spec=gs, input_output_aliases=aliases,
                             compiler_params=cparams, name=name)(prefetch, *args, *x_args)
    xres = [tuple(res[n_out + o0:n_out + o0 + no]) for (_, _, o0, no) in spans]
    return tuple(res[:n_out]), xres


def _tables(S):
    f32 = np.float32
    pos = np.arange(S, dtype=f32)
    lane = np.arange(128)
    inv = (f32(ROPE_THETA) ** (-np.arange(0, 64, 2, dtype=f32) / f32(64))).astype(f32)
    ang = (pos[:, None] * inv[None, :]).astype(np.float64)
    idx = (lane % 64) % 32
    c, s = np.cos(ang)[:, idx], np.sin(ang)[:, idx]
    first = ((lane % 64) < 32)[None, :]
    rope = np.stack([c, np.where(first, 0.0, s), np.where(first, -s, 0.0)])
    base = (f32(1.0) / (f32(ROPE_THETA) ** np.linspace(0.0, 1.0, 64, dtype=f32))).astype(f32)
    ang2 = (pos[:, None] * base[None, :]).astype(np.float64)
    c2, s2 = np.cos(ang2)[:, lane // 2], np.sin(ang2)[:, lane // 2]
    even = (lane % 2 == 0)[None, :]
    th = np.stack([c2, np.where(even, 0.0, s2), np.where(even, -s2, 0.0)])
    return np.stack([rope, th, th * (128 ** -0.5)]).astype(f32)


def _rot(a, c, sa, sb, shift):
    return a * c + pltpu.roll(a, shift, 1) * sa + pltpu.roll(a, 128 - shift, 1) * sb


def _unrot(g, c, sa, sb, shift):
    return g * c + pltpu.roll(g * sa, 128 - shift, 1) + pltpu.roll(g * sb, shift, 1)


def _ret_consts():
    h = np.arange(RET_HEADS, dtype=np.float64)
    log_g = np.log1p(-(2.0 ** (-5.0 - h)))
    idx = np.arange(BLK, dtype=np.float64)
    diff = idx[:, None] - idx[None, :]
    dmask = np.where(diff[None] >= 0, np.exp(np.maximum(diff, 0.0)[None] * log_g[:, None, None]), 0.0)
    zeta = np.exp((BLK - 1 - idx)[None, :] * log_g[:, None])
    xi = np.exp((idx + 1.0)[None, :] * log_g[:, None])
    dec = np.exp(BLK * log_g)
    rep = lambda v: np.broadcast_to(v[:, :, None], (RET_HEADS, BLK, 128))
    return (jnp.asarray(dmask, F32), jnp.asarray(rep(zeta), F32), jnp.asarray(rep(xi), F32),
            jnp.asarray(np.broadcast_to(dec[:, None, None], (RET_HEADS, 8, 256)), F32))


def _rms_fwd(x, g, to_cast=(), exchanges=()):
    S = x.shape[0]
    steps = 4
    tm = S // steps
    n_c = len(to_cast)

    def body(x_ref, g_ref, *refs):
        c_in, (h_ref, ht_ref), c_out = refs[:n_c], refs[n_c:n_c + 2], refs[n_c + 2:]
        for rows in _row_pieces(tm, 512):
            xv = x_ref[rows, :]
            r = lax.rsqrt(jnp.mean(xv * xv, axis=-1, keepdims=True) + NORM_EPS)
            h = xv * r * g_ref[...]
            h_ref[rows, :] = h.astype(BF16)
            ht_ref[:, rows] = h.T.astype(BF16)
        for a_ref, o_ref in zip(c_in, c_out):
            o_ref[...] = a_ref[...].astype(BF16)

    slab = lambda a: pl.BlockSpec((a.shape[0] // steps, a.shape[1]), lambda i: (i, 0))
    return _carrier_call(
        body, (x, g, *to_cast),
        out_shape=(SDS((S, D_MODEL), BF16), SDS((D_MODEL, S), BF16), *[SDS(a.shape, BF16) for a in to_cast]),
        grid=(steps,),
        in_specs=[pl.BlockSpec((tm, D_MODEL), lambda i: (i, 0)), pl.BlockSpec((1, D_MODEL), lambda i: (0, 0))]
        + [slab(a) for a in to_cast],
        out_specs=(pl.BlockSpec((tm, D_MODEL), lambda i: (i, 0)), pl.BlockSpec((D_MODEL, tm), lambda i: (0, i)),
                   *[slab(a) for a in to_cast]),
        sem=("parallel",), name="rms_fwd", exchanges=exchanges)


def _in_proj(h, w_in, tab, exchanges=()):
    S = h.shape[0]
    tm = min(S, 4096)

    def body(h_ref, w_ref, t_ref, o_ref):
        j = pl.program_id(1)
        is_rope = j < 6
        is_theta = (j == QR_B) | (j == KR_B)
        sub = 512

        def rotated(shift):
            for i in range(tm // sub):
                rows = slice(i * sub, (i + 1) * sub)
                acc = _dot(h_ref[rows, :], w_ref[...])
                c, sa, sb = t_ref[0, 0, rows, :], t_ref[0, 1, rows, :], t_ref[0, 2, rows, :]
                for k in range(COLB // 128):
                    sl = slice(k * 128, (k + 1) * 128)
                    o_ref[rows, sl] = _rot(acc[:, sl], c, sa, sb, shift).astype(BF16)

        @pl.when(is_rope)
        def _():
            rotated(32)

        @pl.when(is_theta)
        def _():
            rotated(1)

        @pl.when(jnp.logical_not(is_rope | is_theta))
        def _():
            o_ref[...] = _dot(h_ref[...], w_ref[...]).astype(BF16)

    def tab_map(i, j):
        return (jnp.where(j == QR_B, 1, jnp.where(j == KR_B, 2, 0)), 0, i, 0)

    (proj,), xres = _carrier_call(
        body, (h, w_in, tab), out_shape=(SDS((S, PROJ_W), BF16),), grid=(S // tm, N_COLB),
        in_specs=[pl.BlockSpec((tm, D_MODEL), lambda i, j: (i, 0)),
                  pl.BlockSpec((D_MODEL, COLB), lambda i, j: (0, j)),
                  pl.BlockSpec((1, 3, tm, 128), tab_map)],
        out_specs=(pl.BlockSpec((tm, COLB), lambda i, j: (i, j)),),
        sem=("parallel", "arbitrary"), name="in_proj", exchanges=exchanges)
    return proj, xres


def _band_mask(n):
    qi = lax.broadcasted_iota(jnp.int32, (BLK, 2 * BLK), 0)
    kj = lax.broadcasted_iota(jnp.int32, (BLK, 2 * BLK), 1)
    dist = BLK + qi - kj
    return (dist >= 0) & (dist <= BLK) & ((kj >= BLK) | (n > 0))


def _qkv_col(d, gi):
    if d == 1:
        return lambda t, r: 3 * t + gi
    return lambda t, r: 3 * r + t


def _attn_fwd(qkv, d, gi, exchanges=()):
    L = qkv.shape[0]
    nb = L // BLK

    def body(q_ref, kc_ref, kp_ref, vc_ref, vp_ref, o_ref, lse_ref):
        n = pl.program_id(1)
        mask = _band_mask(n)
        mask2 = jnp.concatenate([mask, mask], axis=0)
        lane = lax.broadcasted_iota(jnp.int32, (BLK, 128), 1)
        lo = lane < 64
        lse_all = jnp.zeros((BLK, 128), F32)
        chunks = [slice(c * 128, (c + 1) * 128) for c in range(4)]
        scores, vals = [], []
        for sl in chunks:
            q = q_ref[:, sl]
            k = jnp.concatenate([kp_ref[:, sl], kc_ref[:, sl]], axis=0)
            vals.append(jnp.concatenate([vp_ref[:, sl], vc_ref[:, sl]], axis=0))
            q2 = jnp.concatenate([jnp.where(lo, q, jnp.zeros_like(q)), jnp.where(lo, jnp.zeros_like(q), q)], axis=0)
            scores.append(_dot_nt(q2, k))
        probs = []
        for c, s in enumerate(scores):
            s = jnp.where(mask2, s * 0.125, jnp.float32(-1e30))
            m = jnp.max(s, axis=-1, keepdims=True)
            p = jnp.exp(s - m)
            l = jnp.sum(p, axis=-1, keepdims=True)
            probs.append((p / l).astype(BF16))
            lse = m + jnp.log(l)
            lse_all = jnp.where(lane // 16 == 2 * c, lse[:BLK], jnp.where(lane // 16 == 2 * c + 1, lse[BLK:], lse_all))
        for sl, p, v in zip(chunks, probs, vals):
            o2 = _dot(p, v)
            o_ref[:, sl] = jnp.where(lo, o2[:BLK], o2[BLK:])
        lse_ref[...] = lse_all

    prev = lambda n: jnp.maximum(n - 1, 0)
    col = _qkv_col(d, gi)
    return _carrier_call(
        body, (qkv,) * 5, out_shape=(SDS((L, d * 512), F32), SDS((L, d * 128), F32)), grid=(d, nb),
        in_specs=[pl.BlockSpec((BLK, 512), lambda r, n: (n, col(0, r))),
                  pl.BlockSpec((BLK, 512), lambda r, n: (n, col(1, r))),
                  pl.BlockSpec((BLK, 512), lambda r, n: (prev(n), col(1, r))),
                  pl.BlockSpec((BLK, 512), lambda r, n: (n, col(2, r))),
                  pl.BlockSpec((BLK, 512), lambda r, n: (prev(n), col(2, r)))],
        out_specs=(pl.BlockSpec((BLK, 512), lambda r, n: (n, r)),
                   pl.BlockSpec((BLK, 128), lambda r, n: (n, r))),
        sem=("parallel", "arbitrary"), name=f"attn_fwd_g{gi}", exchanges=exchanges)


def _qkv_to_sub(proj, d, gi):
    S = proj.shape[0]
    tm = 512
    n = tm // d

    def body(q_ref, k_ref, v_ref, o_ref, scr):
        for t, ref in enumerate((q_ref, k_ref, v_ref)):
            for c in range(4):
                scr[c] = ref[:, c * 128:(c + 1) * 128].astype(F32)
            for r in range(d):
                for c in range(4):
                    col = (3 * r + t) * 512 + c * 128
                    o_ref[:, col:col + 128] = scr[c, pl.ds(r, n, stride=d), :].astype(BF16)

    return pl.pallas_call(
        body, out_shape=SDS((S // d, d * 1536), BF16), grid=(S // tm,),
        in_specs=[pl.BlockSpec((tm, 512), lambda i, b=b: (i, b + gi)) for b in (QA_B, KA_B, VA_B)],
        out_specs=pl.BlockSpec((n, d * 1536), lambda i: (i, 0)),
        scratch_shapes=[pltpu.VMEM((4, tm, 128), F32)],
        compiler_params=_cparams("parallel"), name=f"qkv_to_sub_g{gi}")(proj, proj, proj)


def _attn_merge(os_, lses):
    S = os_[0].shape[0]
    tm = 512

    def body(o0, o1, o2, l0, l1, l2, att_ref, lt_ref, so1, so2, sl1, sl2):
        lo = lax.broadcasted_iota(jnp.int32, (tm, 128), 1) < 64

        def natural(ref, d, scr, width):
            nch = width // 128
            if d == 1:
                return [ref[:, c * 128:(c + 1) * 128] for c in range(nch)]
            for r in range(d):
                for c in range(nch):
                    scr[c, pl.ds(r, tm // d, stride=d), :] = ref[:, r * width + c * 128:r * width + (c + 1) * 128]
            return [scr[c] for c in range(nch)]

        ls = [natural(l, d, s, 128)[0] for l, d, s in zip((l0, l1, l2), DILATIONS, (None, sl1, sl2))]
        m = jnp.maximum(jnp.maximum(ls[0], ls[1]), ls[2])
        es = [jnp.exp(v - m) for v in ls]
        z = es[0] + es[1] + es[2]
        lt_ref[...] = m + jnp.log(z)
        ws = [e / z for e in es]
        o_nat = [natural(o, d, s, 512) for o, d, s in zip((o0, o1, o2), DILATIONS, (None, so1, so2))]
        for c in range(4):
            acc = jnp.zeros((tm, 128), F32)
            for g in range(3):
                w_lo = jnp.broadcast_to(ws[g][:, 32 * c:32 * c + 1], (tm, 128))
                w_hi = jnp.broadcast_to(ws[g][:, 32 * c + 16:32 * c + 17], (tm, 128))
                acc = acc + jnp.where(lo, w_lo, w_hi) * o_nat[g][c]
            att_ref[:, c * 128:(c + 1) * 128] = acc.astype(BF16)

    sub = lambda w: [pl.BlockSpec((tm // d, d * w), lambda i: (i, 0)) for d in DILATIONS]
    return pl.pallas_call(
        body, out_shape=(SDS((S, 512), BF16), SDS((S, 128), F32)), grid=(S // tm,),
        in_specs=sub(512) + sub(128),
        out_specs=(pl.BlockSpec((tm, 512), lambda i: (i, 0)), pl.BlockSpec((tm, 128), lambda i: (i, 0))),
        scratch_shapes=[pltpu.VMEM((4, tm, 128), F32), pltpu.VMEM((4, tm, 128), F32),
                        pltpu.VMEM((1, tm, 128), F32), pltpu.VMEM((1, tm, 128), F32)],
        compiler_params=_cparams("parallel"), name="attn_merge")(*os_, *lses)


def _assemble_dproj(att_grads, dproj):
    S = dproj.shape[0]
    tm = 256

    def body(*refs):
        a = [refs[3 * t:3 * t + 3] for t in range(3)]
        dp_prev, o_ref, scr = refs[9:]
        for t in range(3):
            for g, d in enumerate(DILATIONS):
                base = (3 * t + g) * COLB
                if d == 1:
                    o_ref[:, base:base + COLB] = a[t][g][...]
                    continue
                for c in range(4):
                    for r in range(d):
                        scr[c, pl.ds(r, tm // d, stride=d), :] = a[t][g][:, r * 512 + c * 128:r * 512 + (c + 1) * 128].astype(F32)
                    o_ref[:, base + c * 128:base + (c + 1) * 128] = scr[c].astype(BF16)

    sub = [pl.BlockSpec((tm // d, d * 512), lambda i: (i, 0)) for d in DILATIONS]
    flat = [att_grads[t][g] for t in range(3) for g in range(3)]
    return pl.pallas_call(
        body, out_shape=SDS((S, PROJ_W), BF16), grid=(S // tm,),
        in_specs=sub * 3 + [ANY], out_specs=pl.BlockSpec((tm, 9 * COLB), lambda i: (i, 0)),
        scratch_shapes=[pltpu.VMEM((4, tm, 128), F32)], input_output_aliases={9: 0},
        compiler_params=_cparams("parallel"), name="assemble_dproj")(*flat, dproj)


def _ret_fwd(proj, consts, exchanges=()):
    S = proj.shape[0]
    nc = S // BLK
    dmask, zeta, xi, dec = consts

    def body(q_ref, k_ref, v0_ref, v1_ref, g0_ref, g1_ref, dm_ref, z_ref, x_ref, dec_ref,
             y_ref, rn_ref, rs_ref, st_ref, R):
        @pl.when(pl.program_id(0) == 0)
        def _():
            R[...] = jnp.zeros_like(R)

        lane16 = lax.broadcasted_iota(jnp.int32, (BLK, 128), 1) // 16
        rs_all = jnp.zeros((BLK, 128), F32)
        first = []
        for h in range(RET_HEADS):
            hs = slice(h * 128, (h + 1) * 128)
            q, k = q_ref[:, hs], k_ref[:, hs]
            v = (v0_ref if h < 2 else v1_ref)[:, (h % 2) * 256:(h % 2 + 1) * 256]
            Rb = R[h].astype(BF16)
            st_ref[h] = Rb
            kz = (k.astype(F32) * z_ref[h]).astype(BF16)
            first.append((v, _dot_nt(q, k), _dot((q.astype(F32) * x_ref[h]).astype(BF16), Rb), _dot_tn(kz, v)))
        masked = [(s * dm_ref[h]).astype(BF16) for h, (_, s, _, _) in enumerate(first)]
        for h in range(RET_HEADS):
            vs = slice((h % 2) * 256, (h % 2 + 1) * 256)
            os_ = slice(h * 256, (h + 1) * 256)
            v, _, cross, kv = first[h]
            o = _dot(masked[h], v) + cross
            R[h] = R[h] * dec_ref[h, 0:1, :] + kv
            mu = jnp.mean(o, axis=-1, keepdims=True)
            oc = o - mu
            rstd = lax.rsqrt(jnp.mean(oc * oc, axis=-1, keepdims=True) + NORM_EPS)
            rn = oc * rstd
            gr = (g0_ref if h < 2 else g1_ref)[:, vs].astype(F32)
            y_ref[:, os_] = (rn * gr * _sigmoid(gr)).astype(BF16)
            rn_ref[:, os_] = rn.astype(BF16)
            rs_all = jnp.where(lane16 == h, rstd, rs_all)
        rs_ref[...] = rs_all

    cst = lambda shape: pl.BlockSpec(shape, lambda c: (0, 0, 0))
    blk = lambda j: pl.BlockSpec((BLK, 512), lambda c: (c, j))
    return _carrier_call(
        body, (proj, proj, proj, proj, proj, proj, dmask, zeta, xi, dec),
        out_shape=(SDS((S, 1024), BF16), SDS((S, 1024), BF16), SDS((S, 128), F32), SDS((RET_HEADS, nc, BLK, 256), BF16)),
        grid=(nc,),
        in_specs=[blk(QR_B), blk(KR_B), blk(11), blk(12), blk(13), blk(14),
                  cst((RET_HEADS, BLK, BLK)), cst((RET_HEADS, BLK, 128)), cst((RET_HEADS, BLK, 128)), cst((RET_HEADS, 8, 256))],
        out_specs=(pl.BlockSpec((BLK, 1024), lambda c: (c, 0)), pl.BlockSpec((BLK, 1024), lambda c: (c, 0)),
                   pl.BlockSpec((BLK, 128), lambda c: (c, 0)),
                   pl.BlockSpec((RET_HEADS, None, BLK, 256), lambda c: (0, c, 0, 0))),
        scratch_shapes=[pltpu.VMEM((RET_HEADS, BLK, 256), F32)],
        sem=("arbitrary",), name="ret_fwd", exchanges=exchanges)


def _branch_merge(att, yrin, proj, wa, wr):
    S = att.shape[0]
    tm = min(S, 2048)

    def body(a_ref, y_ref, ga_ref, gr_ref, wa_ref, wr_ref, m_ref, ya_ref, yr_ref):
        for rows in _row_pieces(tm):
            ya = _dot(a_ref[rows, :], wa_ref[...])
            yr = _dot(y_ref[rows, :], wr_ref[...])
            m_ref[rows, :] = (_sigmoid(ga_ref[rows, :].astype(F32)) * ya
                              + _sigmoid(gr_ref[rows, :].astype(F32)) * yr).astype(BF16)
            ya_ref[rows, :] = ya.astype(BF16)
            yr_ref[rows, :] = yr.astype(BF16)

    ospec = pl.BlockSpec((tm, 512), lambda i, j: (i, j))
    return pl.pallas_call(
        body, out_shape=(SDS((S, D_MODEL), BF16),) * 3, grid=(S // tm, 2),
        in_specs=[pl.BlockSpec((tm, 512), lambda i, j: (i, 0)), pl.BlockSpec((tm, 1024), lambda i, j: (i, 0)),
                  pl.BlockSpec((tm, 512), lambda i, j: (i, 15 + j)), pl.BlockSpec((tm, 512), lambda i, j: (i, 17 + j)),
                  pl.BlockSpec((512, 512), lambda i, j: (0, j)), pl.BlockSpec((1024, 512), lambda i, j: (0, j))],
        out_specs=(ospec, ospec, ospec),
        compiler_params=_cparams("parallel", "arbitrary"), name="branch_merge")(att, yrin, proj, proj, wa, wr)


def _out_proj(merged, wo, x, g2, exchanges=()):
    S = x.shape[0]
    tm = 1024

    def body(m_ref, w_ref, x_ref, g_ref, x1_ref, h2_ref):
        x1 = x_ref[...] + _dot(m_ref[...], w_ref[...])
        x1_ref[...] = x1
        r = lax.rsqrt(jnp.mean(x1 * x1, axis=-1, keepdims=True) + NORM_EPS)
        h2_ref[...] = (x1 * r * g_ref[...]).astype(BF16)

    row = pl.BlockSpec((tm, D_MODEL), lambda i: (i, 0))
    return _carrier_call(
        body, (merged, wo, x, g2), out_shape=(SDS((S, D_MODEL), F32), SDS((S, D_MODEL), BF16)), grid=(S // tm,),
        in_specs=[row, pl.BlockSpec((D_MODEL, D_MODEL), lambda i: (0, 0)), row, pl.BlockSpec((1, D_MODEL), lambda i: (0, 0))],
        out_specs=(row, row), sem=("parallel",), name="out_proj", exchanges=exchanges)


def _ffn_up(h2, wg, wu, exchanges=()):
    S = h2.shape[0]
    tm = min(S, 2048)

    def body(h_ref, wg_ref, wu_ref, g_ref, u_ref, a_ref):
        for rows in _row_pieces(tm):
            hv = h_ref[rows, :]
            g = _dot(hv, wg_ref[...])
            u = _dot(hv, wu_ref[...])
            g_ref[rows, :] = g.astype(BF16)
            u_ref[rows, :] = u.astype(BF16)
            a_ref[rows, :] = (g * _sigmoid(g) * u).astype(BF16)

    wspec = pl.BlockSpec((None, D_MODEL, HID_S), lambda i, s: (s, 0, 0))
    ospec = pl.BlockSpec((None, tm, HID_S), lambda i, s: (s, i, 0))
    return _carrier_call(
        body, (h2, wg, wu), out_shape=(SDS((N_SHARD, S, HID_S), BF16),) * 3, grid=(S // tm, N_SHARD),
        in_specs=[pl.BlockSpec((tm, D_MODEL), lambda i, s: (i, 0)), wspec, wspec],
        out_specs=(ospec, ospec, ospec),
        sem=("parallel", "arbitrary"), name="ffn_up", exchanges=exchanges)


def _ffn_down_loss(act, wd, x1, g3, tgt):
    S = x1.shape[0]
    tm = 512

    def body(a_ref, w_ref, x_ref, g_ref, t_ref, dx_ref, dxb_ref, dg_ref, ls_ref):
        @pl.when(pl.program_id(0) == 0)
        def _():
            dg_ref[...] = jnp.zeros_like(dg_ref)
            ls_ref[...] = jnp.zeros_like(ls_ref)

        g = g_ref[...]
        for rows in _row_pieces(tm, 256):
            y = _dot(a_ref[0, rows, :], w_ref[0])
            for s in range(1, N_SHARD):
                y = y + _dot(a_ref[s, rows, :], w_ref[s])
            x2 = x_ref[rows, :] + y
            r = lax.rsqrt(jnp.mean(x2 * x2, axis=-1, keepdims=True) + NORM_EPS)
            xh = x2 * r
            err = xh * g - t_ref[rows, :]
            ls_ref[...] += jnp.sum(jnp.sum(err * err, axis=-1, keepdims=True), axis=0, keepdims=True) * (0.5 / D_MODEL)
            dy = err * (1.0 / D_MODEL)
            dg_ref[...] += jnp.sum(dy * xh, axis=0, keepdims=True)
            dxh = dy * g
            dx = r * (dxh - xh * jnp.mean(dxh * xh, axis=-1, keepdims=True))
            dx_ref[rows, :] = dx
            dxb_ref[rows, :] = dx.astype(BF16)

    row = pl.BlockSpec((tm, D_MODEL), lambda i: (i, 0))
    vec = pl.BlockSpec((1, D_MODEL), lambda i: (0, 0))
    return pl.pallas_call(
        body, out_shape=(SDS((S, D_MODEL), F32), SDS((S, D_MODEL), BF16), SDS((1, D_MODEL), F32), SDS((8, 128), F32)),
        grid=(S // tm,),
        in_specs=[pl.BlockSpec((N_SHARD, tm, HID_S), lambda i: (0, i, 0)),
                  pl.BlockSpec((N_SHARD, HID_S, D_MODEL), lambda i: (0, 0, 0), pipeline_mode=pl.Buffered(1)),
                  row, vec, row],
        out_specs=(row, row, vec, pl.BlockSpec((8, 128), lambda i: (0, 0))),
        compiler_params=_cparams("arbitrary"), name="ffn_down_loss")(act, wd, x1, g3, tgt)


def _ffn_bwd(dx2b, dx2, wd, wg, wu, gte, up, x1, g2):
    S = x1.shape[0]
    tm = 256

    def body(d_ref, dx2_ref, wd_ref, wg_ref, wu_ref, g_ref, u_ref, x_ref, gn_ref,
             dg_ref, du_ref, dx_ref, dxb_ref, dgn_ref):
        @pl.when(pl.program_id(0) == 0)
        def _():
            dgn_ref[...] = jnp.zeros_like(dgn_ref)

        d = d_ref[...]
        dacts = [_dot_nt(d, wd_ref[s]) for s in range(N_SHARD)]
        dgs, dus = [], []
        for s, da in enumerate(dacts):
            g = g_ref[s].astype(F32)
            sg = _sigmoid(g)
            dgs.append((da * u_ref[s].astype(F32) * sg * (1.0 + g * (1.0 - sg))).astype(BF16))
            dus.append((da * g * sg).astype(BF16))
            dg_ref[s] = dgs[s]
            du_ref[s] = dus[s]
        dh = _dot_nt(dgs[0], wg_ref[0]) + _dot_nt(dus[0], wu_ref[0])
        for s in range(1, N_SHARD):
            dh = dh + _dot_nt(dgs[s], wg_ref[s]) + _dot_nt(dus[s], wu_ref[s])
        xv = x_ref[...]
        r = lax.rsqrt(jnp.mean(xv * xv, axis=-1, keepdims=True) + NORM_EPS)
        xh = xv * r
        dgn_ref[...] += jnp.sum(dh * xh, axis=0, keepdims=True)
        dxh = dh * gn_ref[...]
        dx = dx2_ref[...] + r * (dxh - xh * jnp.mean(dxh * xh, axis=-1, keepdims=True))
        dx_ref[...] = dx
        dxb_ref[...] = dx.astype(BF16)

    row = pl.BlockSpec((tm, D_MODEL), lambda i: (i, 0))
    vec = pl.BlockSpec((1, D_MODEL), lambda i: (0, 0))
    aspec = pl.BlockSpec((N_SHARD, tm, HID_S), lambda i: (0, i, 0))
    resident = lambda shape: pl.BlockSpec(shape, lambda i: (0, 0, 0), pipeline_mode=pl.Buffered(1))
    return pl.pallas_call(
        body,
        out_shape=(SDS((N_SHARD, S, HID_S), BF16), SDS((N_SHARD, S, HID_S), BF16),
                   SDS((S, D_MODEL), F32), SDS((S, D_MODEL), BF16), SDS((1, D_MODEL), F32)),
        grid=(S // tm,),
        in_specs=[row, row, resident((N_SHARD, HID_S, D_MODEL)), resident((N_SHARD, D_MODEL, HID_S)),
                  resident((N_SHARD, D_MODEL, HID_S)), aspec, aspec, row, vec],
        out_specs=(aspec, aspec, row, row, vec),
        compiler_params=_cparams("arbitrary"), name="ffn_bwd")(dx2b, dx2, wd, wg, wu, gte, up, x1, g2)


def _wgrad(name, a, b, a_spec, b_spec, out_shape, out_spec, n_par, S):
    tk = min(S, 4096)

    def body(a_ref, b_ref, o_ref):
        @pl.when(pl.program_id(1) == 0)
        def _():
            o_ref[...] = jnp.zeros_like(o_ref)

        o_ref[...] += _dot_tn(a_ref[...], b_ref[...])

    return pl.pallas_call(
        body, out_shape=SDS(out_shape, F32), grid=(n_par, S // tk),
        in_specs=[a_spec(tk), b_spec(tk)], out_specs=out_spec,
        compiler_params=_cparams("parallel", "arbitrary"), name=name)(a, b)


def _out_proj_bwd(dx1b, wo, proj, ya, yr, exchanges=()):
    S = dx1b.shape[0]
    tm = 512
    gate0 = 15 * COLB

    def body(d_ref, w_ref, ga_ref, gr_ref, ya_ref, yr_ref, dya_ref, dyr_ref, dp_ref):
        for rows in _row_pieces(tm, 256):
            dm = _dot_nt(d_ref[rows, :], w_ref[...])
            sa = _sigmoid(ga_ref[rows, :].astype(F32))
            sr = _sigmoid(gr_ref[rows, :].astype(F32))
            dya_ref[rows, :] = (dm * sa).astype(BF16)
            dyr_ref[rows, :] = (dm * sr).astype(BF16)
            dp_ref[rows, 0:D_MODEL] = (dm * ya_ref[rows, :].astype(F32) * sa * (1.0 - sa)).astype(BF16)
            dp_ref[rows, D_MODEL:2 * D_MODEL] = (dm * yr_ref[rows, :].astype(F32) * sr * (1.0 - sr)).astype(BF16)

    row = pl.BlockSpec((tm, D_MODEL), lambda i: (i, 0))
    cols = lambda c0, w: pl.BlockSpec((pl.Element(tm), pl.Element(w)), lambda i: (i * tm, c0))
    return _carrier_call(
        body, (dx1b, wo, proj, proj, ya, yr),
        out_shape=(SDS((S, D_MODEL), BF16), SDS((S, D_MODEL), BF16), SDS((S, PROJ_W), BF16)), grid=(S // tm,),
        in_specs=[row, pl.BlockSpec((D_MODEL, D_MODEL), lambda i: (0, 0), pipeline_mode=pl.Buffered(1)),
                  cols(gate0, D_MODEL), cols(gate0 + D_MODEL, D_MODEL), row, row],
        out_specs=(row, row, cols(gate0, 2 * D_MODEL)),
        sem=("parallel",), name="out_proj_bwd", exchanges=exchanges)


def _branch_bwd(dya, dyr, wa, wr, att):
    S = dya.shape[0]
    tm = 1024

    def body(da_ref, dr_ref, wa_ref, wr_ref, att_ref, datt_ref, rho_ref, dyi_ref):
        datt = _dot_nt(da_ref[...], wa_ref[...])
        datt_ref[...] = datt.astype(BF16)
        dyi_ref[...] = _dot_nt(dr_ref[...], wr_ref[...]).astype(BF16)
        prod = datt * att_ref[...].astype(F32)
        lane = lax.broadcasted_iota(jnp.int32, (tm, 128), 1)
        lo = lane < 64
        rho = jnp.zeros((tm, 128), F32)
        for c in range(4):
            pc = prod[:, c * 128:(c + 1) * 128]
            tot = jnp.sum(pc, axis=-1, keepdims=True)
            low = jnp.sum(jnp.where(lo, pc, 0.0), axis=-1, keepdims=True)
            rho = jnp.where(lane // 16 == 2 * c, low, jnp.where(lane // 16 == 2 * c + 1, tot - low, rho))
        rho_ref[...] = rho

    row = lambda w: pl.BlockSpec((tm, w), lambda i: (i, 0))
    return pl.pallas_call(
        body, out_shape=(SDS((S, 512), BF16), SDS((S, 128), F32), SDS((S, 1024), BF16)), grid=(S // tm,),
        in_specs=[row(1024), row(1024), pl.BlockSpec((512, 1024), lambda i: (0, 0)),
                  pl.BlockSpec((1024, 1024), lambda i: (0, 0)), row(512)],
        out_specs=(row(512), row(128), row(1024)),
        compiler_params=_cparams("parallel"), name="branch_bwd")(dya, dyr, wa, wr, att)


def _attn_bwd(qkv, datt, lse, rho, rtab, d, gi, exchanges=()):
    L = qkv.shape[0]
    nb = L // BLK
    T = d * nb

    def body(q_ref, kc_ref, kp_ref, vc_ref, vp_ref, do_ref, lse_ref, rho_ref, tq_ref, tk_ref,
             dq_ref, dk_ref, dv_ref, ck, cv):
        t = pl.program_id(0)
        n = jnp.minimum(t, T - 1) % nb

        @pl.when(t == 0)
        def _():
            ck[...] = jnp.zeros_like(ck)
            cv[...] = jnp.zeros_like(cv)

        def store_rot(ref, val, t_ref, c):
            sl = slice(c * 128, (c + 1) * 128)
            ref[:, sl] = _unrot(val, t_ref[0], t_ref[1], t_ref[2], 32).astype(BF16)

        @pl.when(t < T)
        def _():
            mask = _band_mask(n)
            mask2 = jnp.concatenate([mask, mask], axis=0)
            lo = lax.broadcasted_iota(jnp.int32, (BLK, 128), 1) < 64

            def stacked(a):
                return jnp.concatenate([jnp.where(lo, a, jnp.zeros_like(a)), jnp.where(lo, jnp.zeros_like(a), a)], axis=0)

            def head_cols(ref, c):
                return jnp.concatenate([jnp.broadcast_to(ref[:, 32 * c:32 * c + 1], (BLK, 2 * BLK)),
                                        jnp.broadcast_to(ref[:, 32 * c + 16:32 * c + 17], (BLK, 2 * BLK))], axis=0)

            ops, raw = [], []
            for c in range(4):
                sl = slice(c * 128, (c + 1) * 128)
                q2, do2 = stacked(q_ref[:, sl]), stacked(do_ref[:, sl])
                k = jnp.concatenate([kp_ref[:, sl], kc_ref[:, sl]], axis=0)
                v = jnp.concatenate([vp_ref[:, sl], vc_ref[:, sl]], axis=0)
                ops.append((q2, do2, k))
                raw.append((_dot_nt(q2, k), _dot_nt(do2, v)))
            grads = []
            for c, (s, dp) in enumerate(raw):
                p = jnp.where(mask2, jnp.exp(s * 0.125 - head_cols(lse_ref, c)), 0.0)
                grads.append(((p * (dp - head_cols(rho_ref, c)) * 0.125).astype(BF16), p.astype(BF16)))
            for c, ((q2, do2, k), (ds, pb)) in enumerate(zip(ops, grads)):
                sl = slice(c * 128, (c + 1) * 128)
                dq2 = _dot(ds, k)
                dq_c = jnp.where(lo, dq2[:BLK], dq2[BLK:])
                dk_c = _dot_tn(ds, q2)
                dv_c = _dot_tn(pb, do2)
                store_rot(dq_ref, dq_c, tq_ref, c)
                store_rot(dk_ref, ck[:, sl] + dk_c[:BLK], tk_ref, c)
                dv_ref[:, sl] = (cv[:, sl] + dv_c[:BLK]).astype(BF16)
                ck[:, sl] = dk_c[BLK:]
                cv[:, sl] = dv_c[BLK:]

        @pl.when(t == T)
        def _():
            for c in range(4):
                sl = slice(c * 128, (c + 1) * 128)
                store_rot(dk_ref, ck[:, sl], tk_ref, c)
            dv_ref[...] = cv[...].astype(BF16)

    blk_of = lambda t: (jnp.minimum(t, T - 1) % nb, jnp.minimum(t, T - 1) // nb)
    cur = lambda t: blk_of(t)
    prev = lambda t: (jnp.maximum(blk_of(t)[0] - 1, 0), blk_of(t)[1])
    fin = lambda t: blk_of(jnp.maximum(t - 1, 0))
    col = _qkv_col(d, gi)
    qkv_spec = lambda kind, which: pl.BlockSpec((BLK, 512), lambda t: (which(t)[0], col(kind, which(t)[1])))
    row_spec = lambda w, which: pl.BlockSpec((BLK, w), lambda t: which(t))
    tab_spec = lambda which: pl.BlockSpec((3, BLK, 128), lambda t: (0, *which(t)))
    return _carrier_call(
        body, (qkv, qkv, qkv, qkv, qkv, datt, lse, rho, rtab, rtab),
        out_shape=(SDS((L, d * 512), BF16),) * 3, grid=(T + 1,),
        in_specs=[qkv_spec(0, cur), qkv_spec(1, cur), qkv_spec(1, prev), qkv_spec(2, cur), qkv_spec(2, prev),
                  row_spec(512, cur), row_spec(128, cur), row_spec(128, cur), tab_spec(cur), tab_spec(fin)],
        out_specs=(row_spec(512, cur), row_spec(512, fin), row_spec(512, fin)),
        scratch_shapes=[pltpu.VMEM((BLK, 512), F32), pltpu.VMEM((BLK, 512), F32)],
        sem=("arbitrary",), name=f"attn_bwd_g{gi}", exchanges=exchanges)


def _ret_bwd(proj, rn, rstd, dyrin, states, tab, consts, dproj, exchanges=()):
    S = proj.shape[0]
    nc = S // BLK
    dmask, zeta, xi, dec = consts

    def body(q_ref, k_ref, v0_ref, v1_ref, g0_ref, g1_ref, rn_ref, rs_ref, dy_ref, st_ref, tq_ref, tk_ref,
             dm_ref, z_ref, x_ref, dec_ref, dp_prev, dp_ref, dR):
        dq_ref, dk_ref = dp_ref.at[:, 0:512], dp_ref.at[:, 512:1024]
        dv_ref, dgr_ref = dp_ref.at[:, 1024:2048], dp_ref.at[:, 2048:3072]

        @pl.when(pl.program_id(0) == 0)
        def _():
            dR[...] = jnp.zeros_like(dR)

        dobs = []
        for h in range(RET_HEADS):
            vs = slice((h % 2) * 256, (h % 2 + 1) * 256)
            os_ = slice(h * 256, (h + 1) * 256)
            gr = (g0_ref if h < 2 else g1_ref)[:, vs].astype(F32)
            sg = _sigmoid(gr)
            rn_v = rn_ref[:, os_].astype(F32)
            dyi = dy_ref[:, os_].astype(F32)
            dgr_ref[:, os_] = (dyi * rn_v * sg * (1.0 + gr * (1.0 - sg))).astype(BF16)
            drn = dyi * gr * sg
            rstd = jnp.broadcast_to(rs_ref[:, 16 * h:16 * h + 1], (BLK, 256))
            do = rstd * (drn - jnp.mean(drn, axis=-1, keepdims=True) - rn_v * jnp.mean(drn * rn_v, axis=-1, keepdims=True))
            dobs.append(do.astype(BF16))
        first = []
        for h in range(RET_HEADS):
            hs = slice(h * 128, (h + 1) * 128)
            q, k = q_ref[:, hs], k_ref[:, hs]
            v = (v0_ref if h < 2 else v1_ref)[:, (h % 2) * 256:(h % 2 + 1) * 256]
            dob, dRb = dobs[h], dR[h].astype(BF16)
            kz = (k.astype(F32) * z_ref[h]).astype(BF16)
            qx = (q.astype(F32) * x_ref[h]).astype(BF16)
            first.append((q, k, _dot_nt(q, k), _dot_nt(dob, v), _dot(kz, dRb), _dot_nt(dob, st_ref[h]),
                          _dot_nt(v, dRb), _dot_tn(qx, dob)))
        masked = [((s * dm_ref[h]).astype(BF16), (dsr * dm_ref[h]).astype(BF16))
                  for h, (_, _, s, dsr, _, _, _, _) in enumerate(first)]
        for h in range(RET_HEADS):
            hs = slice(h * 128, (h + 1) * 128)
            os_ = slice(h * 256, (h + 1) * 256)
            q, k, _, _, dv_state, dq_state, dk_state, dr_new = first[h]
            sD, dS = masked[h]
            dv_ref[:, os_] = (_dot_tn(sD, dobs[h]) + dv_state).astype(BF16)
            dq = _dot(dS, k) + dq_state * x_ref[h]
            dk = _dot_tn(dS, q) + dk_state * z_ref[h]
            dR[h] = dR[h] * dec_ref[h, 0:1, :] + dr_new
            dq_ref[:, hs] = _unrot(dq, tq_ref[0], tq_ref[1], tq_ref[2], 1).astype(BF16)
            dk_ref[:, hs] = _unrot(dk, tk_ref[0], tk_ref[1], tk_ref[2], 1).astype(BF16)

    rc = lambda c: nc - 1 - c
    cst = lambda shape: pl.BlockSpec(shape, lambda c: (0, 0, 0))
    blk = lambda j: pl.BlockSpec((BLK, 512), lambda c: (rc(c), j))
    row = lambda w: pl.BlockSpec((BLK, w), lambda c: (rc(c), 0))
    (dproj,), xres = _carrier_call(
        body, (proj, proj, proj, proj, proj, proj, rn, rstd, dyrin, states, tab, tab, dmask, zeta, xi, dec, dproj),
        out_shape=(SDS((S, PROJ_W), BF16),), grid=(nc,),
        in_specs=[blk(QR_B), blk(KR_B), blk(11), blk(12), blk(13), blk(14), row(1024), row(128), row(1024),
                  pl.BlockSpec((RET_HEADS, None, BLK, 256), lambda c: (0, rc(c), 0, 0)),
                  pl.BlockSpec((None, 3, BLK, 128), lambda c: (1, 0, rc(c), 0)),
                  pl.BlockSpec((None, 3, BLK, 128), lambda c: (2, 0, rc(c), 0)),
                  cst((RET_HEADS, BLK, BLK)), cst((RET_HEADS, BLK, 128)), cst((RET_HEADS, BLK, 128)), cst((RET_HEADS, 8, 256)),
                  ANY],
        out_specs=(pl.BlockSpec((pl.Element(BLK), pl.Element(6 * COLB)), lambda c: (rc(c) * BLK, QR_B * COLB)),),
        scratch_shapes=[pltpu.VMEM((RET_HEADS, BLK, 256), F32)],
        sem=("arbitrary",), name="ret_bwd", exchanges=exchanges, in_out_aliases={16: 0})
    return dproj, xres


def _wgrad_in_half(ht, dproj, sidx, kept, exchanges=()):
    S = dproj.shape[0]
    tk = 2048
    half = (lambda sx: sx[4]) if kept else (lambda sx: 1 - sx[4])

    def body(a_ref, b_ref, o_ref):
        @pl.when(pl.program_id(1) == 0)
        def _():
            o_ref[...] = jnp.zeros_like(o_ref)

        o_ref[...] += _dot(a_ref[...], b_ref[...])

    (g,), xres = _carrier_call(
        body, (ht, dproj), out_shape=(SDS((D_MODEL // 2, PROJ_W), F32),), grid=(N_SHARD, S // tk),
        in_specs=[pl.BlockSpec((D_MODEL // 2, tk), lambda s, k, sx: (half(sx), k)),
                  pl.BlockSpec((tk, W_IN_S), lambda s, k, sx: (k, s))],
        out_specs=(pl.BlockSpec((D_MODEL // 2, W_IN_S), lambda s, k, sx: (0, s)),),
        sem=("parallel", "arbitrary"), name="wgrad_in_kept" if kept else "wgrad_in_sent", exchanges=exchanges,
        prefetch=sidx)
    return g, xres


def _in_proj_bwd(dproj, w_in, x, g1, dx1, exchanges=()):
    S = x.shape[0]
    tm = 1024

    def body(d_ref, w_ref, x_ref, g_ref, dx1_ref, dx_ref, dgn_ref, acc):
        i, s = pl.program_id(0), pl.program_id(1)

        @pl.when(s == 0)
        def _():
            acc[...] = jnp.zeros_like(acc)

        @pl.when((i == 0) & (s == 0))
        def _():
            dgn_ref[...] = jnp.zeros_like(dgn_ref)

        acc[...] += _dot_nt(d_ref[...], w_ref[...])

        @pl.when(s == N_SHARD - 1)
        def _():
            xv = x_ref[...]
            r = lax.rsqrt(jnp.mean(xv * xv, axis=-1, keepdims=True) + NORM_EPS)
            xh = xv * r
            dh = acc[...]
            dgn_ref[...] += jnp.sum(dh * xh, axis=0, keepdims=True)
            dxh = dh * g_ref[...]
            dx_ref[...] = dx1_ref[...] + r * (dxh - xh * jnp.mean(dxh * xh, axis=-1, keepdims=True))

    row = pl.BlockSpec((tm, D_MODEL), lambda i, s: (i, 0))
    vec = pl.BlockSpec((1, D_MODEL), lambda i, s: (0, 0))
    (gx, dg), xres = _carrier_call(
        body, (dproj, w_in, x, g1, dx1),
        out_shape=(SDS((S, D_MODEL), F32), SDS((1, D_MODEL), F32)), grid=(S // tm, N_SHARD),
        in_specs=[pl.BlockSpec((tm, W_IN_S), lambda i, s: (i, s)),
                  pl.BlockSpec((D_MODEL, W_IN_S), lambda i, s: (0, s)), row, vec, row],
        out_specs=(row, vec), scratch_shapes=[pltpu.VMEM((tm, D_MODEL), F32)],
        sem=("arbitrary", "arbitrary"), name="in_proj_bwd", exchanges=exchanges)
    return gx, dg, xres


def _sub_view(a, d):
    S, W = a.shape
    return a.reshape(S // d, d * W)


def _step(x, tgt, g1, g2, g3, comm):
    S = x.shape[0]
    tab_np = _tables(S)
    tab = jnp.asarray(tab_np)
    consts = _ret_consts()

    (h, ht, *casts), xres = _rms_fwd(x, g1, comm.to_cast(), comm.carry("rms_fwd"))
    comm.cast_done(casts)
    comm.took("rms_fwd", xres)
    w_in = comm.weight(0)
    proj, xres = _in_proj(h, w_in, tab, comm.carry("in_proj"))
    comm.took("in_proj", xres)
    qkvs, o_parts, lse_parts = [], [], []
    for gi, d in enumerate(DILATIONS):
        qkv = proj if d == 1 else _qkv_to_sub(proj, d, gi)
        (o_g, lse_g), xres = _attn_fwd(qkv, d, gi, comm.carry(f"attn_fwd_g{gi}"))
        comm.took(f"attn_fwd_g{gi}", xres)
        qkvs.append(qkv)
        o_parts.append(o_g)
        lse_parts.append(lse_g)
    att, lse_tot = _attn_merge(o_parts, lse_parts)
    (yrin, rn, rstd, states), xres = _ret_fwd(proj, consts, comm.carry("ret_fwd"))
    comm.took("ret_fwd", xres)
    wa, wr, wo = comm.weight(1), comm.weight(2), comm.weight(3)
    merged, ya, yr = _branch_merge(att, yrin, proj, wa, wr)
    (x1, h2), xres = _out_proj(merged, wo, x, g2, comm.carry("out_proj"))
    comm.took("out_proj", xres)
    wg, wu = comm.weight(4), comm.weight(5)
    (gte, up, act), xres = _ffn_up(h2, wg, wu, comm.carry("ffn_up"))
    comm.took("ffn_up", xres)
    wd = comm.weight(6)
    dx2, dx2b, dg3, loss_p = _ffn_down_loss(act, wd, x1, g3, tgt)

    dgte, dup, dx1, dx1b, dg2 = _ffn_bwd(dx2b, dx2, wd, wg, wu, gte, up, x1, g2)
    tok3 = lambda w: (lambda tk: pl.BlockSpec((None, tk, w), lambda p, k: (p, k, 0)))
    tok2 = lambda w: (lambda tk: pl.BlockSpec((tk, w), lambda p, k: (k, 0)))
    g_d = _wgrad("wgrad_down", act, dx2b, tok3(HID_S), tok2(D_MODEL), (N_SHARD, HID_S, D_MODEL),
                 pl.BlockSpec((None, HID_S, D_MODEL), lambda p, k: (p, 0, 0)), N_SHARD, S)
    g_g = _wgrad("wgrad_gate", h2, dgte, tok2(D_MODEL), tok3(HID_S), (N_SHARD, D_MODEL, HID_S),
                 pl.BlockSpec((None, D_MODEL, HID_S), lambda p, k: (p, 0, 0)), N_SHARD, S)
    g_u = _wgrad("wgrad_up", h2, dup, tok2(D_MODEL), tok3(HID_S), (N_SHARD, D_MODEL, HID_S),
                 pl.BlockSpec((None, D_MODEL, HID_S), lambda p, k: (p, 0, 0)), N_SHARD, S)
    comm.grads({4: g_g, 5: g_u, 6: g_d})
    (dya, dyr, dproj), xres = _out_proj_bwd(dx1b, wo, proj, ya, yr, comm.carry("out_proj_bwd"))
    comm.took("out_proj_bwd", xres)
    colblk = lambda w: (lambda tk: pl.BlockSpec((tk, w), lambda p, k: (k, p)))
    g_o = _wgrad("wgrad_out", merged, dx1b, colblk(256), tok2(D_MODEL), (D_MODEL, D_MODEL),
                 pl.BlockSpec((256, D_MODEL), lambda p, k: (p, 0)), 4, S)
    datt, rho, dyrin = _branch_bwd(dya, dyr, wa, wr, att)
    g_a = _wgrad("wgrad_attn", att, dya, tok2(512), colblk(512), (512, D_MODEL),
                 pl.BlockSpec((512, 512), lambda p, k: (0, p)), 2, S)
    g_r = _wgrad("wgrad_ret", yrin, dyr, colblk(256), tok2(D_MODEL), (D_MODEL, D_MODEL),
                 pl.BlockSpec((256, D_MODEL), lambda p, k: (p, 0)), 4, S)
    comm.grads({1: g_a, 2: g_r.reshape(N_SHARD, 256, D_MODEL), 3: g_o.reshape(N_SHARD, 256, D_MODEL)})
    dproj, xres = _ret_bwd(proj, rn, rstd, dyrin, states, tab, consts, dproj, comm.carry("ret_bwd"))
    comm.took("ret_bwd", xres)
    dqs, dks, dvs = [], [], []
    for gi, d in enumerate(DILATIONS):
        rtab = jnp.asarray(tab_np[0].reshape(3, S // d, d * 128))
        (dq, dk, dv), xres = _attn_bwd(qkvs[gi], _sub_view(datt, d), _sub_view(lse_tot, d), _sub_view(rho, d), rtab, d, gi,
                                       comm.carry(f"attn_bwd_g{gi}"))
        comm.took(f"attn_bwd_g{gi}", xres)
        dqs.append(dq)
        dks.append(dk)
        dvs.append(dv)
    dproj = _assemble_dproj((dqs, dks, dvs), dproj)
    g_sent, xres = _wgrad_in_half(ht, dproj, comm.sidx, False, comm.carry("wgrad_in_sent"))
    comm.took("wgrad_in_sent", xres)
    comm.grads({"in_sent": g_sent})
    g_kept, xres = _wgrad_in_half(ht, dproj, comm.sidx, True, comm.carry("wgrad_in_kept"))
    comm.grads({"in_kept": g_kept})
    comm.took("wgrad_in_kept", xres)
    grad_x, dg1, xres = _in_proj_bwd(dproj, w_in, x, g1, dx1, comm.carry("in_proj_bwd"))
    comm.took("in_proj_bwd", xres)
    return loss_p, grad_x, (dg1, dg2, dg3)


W_KINDS = ("col", "col", "lead", "lead", "lead", "lead", "lead")
W_SHARD = ((1024, W_IN_S), (512, 256), (256, 1024), (256, 1024), (1024, HID_S), (1024, HID_S), (HID_S, 1024))
N_W = len(W_KINDS)


def _full_shape(wi):
    R, C = W_SHARD[wi]
    return (R, N_SHARD * C) if W_KINDS[wi] == "col" else (N_SHARD, R, C)


def _view(ref, wi, s, half):
    R, C = W_SHARD[wi]
    rows = pl.ds(half * (R // 2), R // 2)
    if W_KINDS[wi] == "col":
        return ref.at[rows, pl.ds(pl.multiple_of(s * C, 128), C)]
    return ref.at[s, rows, :]


def _mesh_pos():
    x, y, c = lax.axis_index("x"), lax.axis_index("y"), lax.axis_index("c")
    chips = [(1 - x, y), (x, 1 - y), (1 - x, 1 - y)]
    return x, y, c, chips


def _cast_bf16(a):
    R, C = a.shape
    tr = R // 2 if R % 32 == 0 else R

    def body(a_ref, o_ref):
        o_ref[...] = a_ref[...].astype(BF16)

    spec = pl.BlockSpec((tr, C), lambda i: (i, 0))
    return pl.pallas_call(body, out_shape=SDS((R, C), BF16), grid=(R // tr,), in_specs=[spec], out_specs=spec,
                          compiler_params=_cparams("parallel"), name=f"cast_{R}x{C}")(a)


def _remote(send, recv, k, src, dst, to):
    return pltpu.make_async_remote_copy(src_ref=src, dst_ref=dst, send_sem=send.at[k], recv_sem=recv.at[k],
                                        device_id=to, device_id_type=MESH)


def _ex_gather_ring(wis, shards):
    n = len(wis)

    def build(sh, full, send, recv, loc):
        x, y, c, _ = _mesh_pos()
        s_me, sib = 2 * x + y, (x, y, 1 - c)
        xn, yn = (1 - x, y), (x, 1 - y)
        flip = lambda a, b: a + b - 2 * a * b
        via = (flip(x, 1 - c), flip(y, c))
        onto = (flip(x, c), flip(y, 1 - c))
        shard_of = lambda chip: 2 * chip[0] + chip[1]
        starts, waits, sent = [], [], []
        for i, wi in enumerate(wis):
            Rh = W_SHARD[wi][0] // 2
            for hf in range(2):
                cp = pltpu.make_async_copy(sh[i].at[pl.ds(hf * Rh, Rh), :], _view(full[i], wi, s_me, hf), loc.at[2 * i + hf])
                starts.append(cp)
                sent.append(cp.wait)
            for j, chip in enumerate((xn, yn)):
                cp = _remote(send, recv, 6 * i + j, sh[i].at[pl.ds(c * Rh, Rh), :], _view(full[i], wi, s_me, c), (*chip, c))
                starts.append(cp)
                sent.append(cp.wait_send)

        def pass_to_sibling(i, wi, k, s):
            mine = _view(full[i], wi, s, c)
            fw = _remote(send, recv, 6 * i + k, mine, mine, sib)
            waits.append(fw.start)
            sent.append(fw.wait_send)

        for i, wi in enumerate(wis):
            for j, chip in enumerate((xn, yn)):
                land = _view(full[i], wi, shard_of(chip), c)
                waits.append(_remote(send, recv, 6 * i + j, land, land, (*chip, c)).wait_recv)
                pass_to_sibling(i, wi, 3 + j, shard_of(chip))
            relay = _view(full[i], wi, shard_of(via), c)
            fw = _remote(send, recv, 6 * i + 2, relay, relay, (*onto, c))
            waits.append(fw.start)
            sent.append(fw.wait_send)
        s_diag = 2 * (1 - x) + (1 - y)
        for i, wi in enumerate(wis):
            land = _view(full[i], wi, s_diag, c)
            waits.append(_remote(send, recv, 6 * i + 2, land, land, (*onto, c)).wait_recv)
            pass_to_sibling(i, wi, 5, s_diag)
        for i, wi in enumerate(wis):
            for k, s in ((3, shard_of(xn)), (4, shard_of(yn)), (5, s_diag)):
                land = _view(full[i], wi, s, 1 - c)
                waits.append(_remote(send, recv, 6 * i + k, land, land, sib).wait_recv)
        return starts, waits + sent

    return _Exchange(shards, [SDS(_full_shape(wi), BF16) for wi in wis], {}, 6 * n, 2 * n, build)


def _ex_gather_ici(wis, shards, then_d2d=False):
    n = len(wis)

    def build(ins, outs, send, recv, loc):
        x, y, c, chips = _mesh_pos()
        s_me, sib = 2 * x + y, (x, y, 1 - c)
        starts, waits, after = [], [], []
        for i, wi in enumerate(wis):
            Rh = W_SHARD[wi][0] // 2
            for hf in range(2):
                cp = pltpu.make_async_copy(ins[i].at[pl.ds(hf * Rh, Rh), :], _view(outs[i], wi, s_me, hf), loc.at[2 * i + hf])
                starts.append(cp)
                waits.append(cp.wait)
            for j, chip in enumerate(chips):
                cp = _remote(send, recv, 3 * i + j, ins[i].at[pl.ds(c * Rh, Rh), :], _view(outs[i], wi, s_me, c), (*chip, c))
                land = _view(outs[i], wi, 2 * chip[0] + chip[1], c)
                starts.append(cp)
                waits += [cp.wait_send, _remote(send, recv, 3 * i + j, land, land, (*chip, c)).wait_recv]
                if then_d2d:
                    theirs = _view(outs[i], wi, 2 * chip[0] + chip[1], 1 - c)
                    fw = _remote(send, recv, 3 * n + 3 * i + j, land, land, sib)
                    waits.append(fw.start)
                    after += [fw.wait_send, _remote(send, recv, 3 * n + 3 * i + j, theirs, theirs, sib).wait_recv]
        return starts, waits + after

    return _Exchange(shards, [SDS(_full_shape(wi), BF16) for wi in wis], {}, (6 if then_d2d else 3) * n, 2 * n, build)


def _ex_gather_d2d(wis, fulls):
    def build(ins, outs, send, recv, loc):
        x, y, c, chips = _mesh_pos()
        sib = (x, y, 1 - c)
        starts, waits = [], []
        for i, wi in enumerate(wis):
            for j, chip in enumerate(chips):
                mine = _view(outs[i], wi, 2 * chip[0] + chip[1], c)
                theirs = _view(outs[i], wi, 2 * chip[0] + chip[1], 1 - c)
                cp = _remote(send, recv, 3 * i + j, mine, mine, sib)
                starts.append(cp)
                waits += [cp.wait_send, _remote(send, recv, 3 * i + j, theirs, theirs, sib).wait_recv]
        return starts, waits

    return _Exchange(fulls, [SDS(f.shape, BF16) for f in fulls], {i: i for i in range(len(wis))}, 3 * len(wis), 0, build)


def _half_shape(wi):
    R, C = W_SHARD[wi]
    return (R // 2, N_SHARD * C) if W_KINDS[wi] == "col" else (N_SHARD, R // 2, C)


def _ex_pair(wis, grads):
    def build(ins, outs, send, recv, loc):
        x, y, c, _ = _mesh_pos()
        starts, waits = [], []
        for i, wi in enumerate(wis):
            Rh = W_SHARD[wi][0] // 2
            rows = pl.ds((1 - c) * Rh, Rh)
            if tuple(ins[i].shape) == _half_shape(wi):
                src = ins[i]
            else:
                src = ins[i].at[rows, :] if W_KINDS[wi] == "col" else ins[i].at[:, rows, :]
            cp = _remote(send, recv, i, src, outs[i], (x, y, 1 - c))
            starts.append(cp)
            waits.append(cp.wait)
        return starts, waits

    return _Exchange(grads, [SDS(_half_shape(wi), F32) for wi in wis], {}, len(wis), 0, build)


def _ex_chip(wis, pbs):
    def build(ins, outs, send, recv, loc):
        x, y, c, chips = _mesh_pos()
        starts, waits = [], []
        for i, wi in enumerate(wis):
            for j, chip in enumerate(chips):
                cp = _remote(send, recv, 3 * i + j, ins[i].at[j], outs[i].at[j], (*chip, c))
                starts.append(cp)
                waits.append(cp.wait)
        return starts, waits

    shapes = [SDS((3, W_SHARD[wi][0] // 2, W_SHARD[wi][1]), BF16) for wi in wis]
    return _Exchange(pbs, shapes, {}, 3 * len(wis), 0, build)


def _ex_share(wis, halves):
    def build(ins, outs, send, recv, loc):
        x, y, c, _ = _mesh_pos()
        sib = (x, y, 1 - c)
        starts, waits = [], []
        for i, wi in enumerate(wis):
            cp = _remote(send, recv, i, outs[i].at[c], outs[i].at[c], sib)
            starts.append(cp)
            waits += [cp.wait_send, _remote(send, recv, i, outs[i].at[1 - c], outs[i].at[1 - c], sib).wait_recv]
        return starts, waits

    return _Exchange(halves, [SDS(h.shape, F32) for h in halves], {i: i for i in range(len(wis))}, len(wis), 0, build)


def _row_tile(rh, C):
    best = 16
    for t in range(16, rh + 1, 16):
        if rh % t == 0 and t * C * 4 <= (3 << 19):
            best = t
    return best


def _pair_sum(wi, g, ra, sidx):
    R, C = W_SHARD[wi]
    Rh = R // 2
    tr = _row_tile(Rh, C)
    nt = Rh // tr
    off = 0 if tuple(g.shape) == _half_shape(wi) else nt
    col = W_KINDS[wi] == "col"

    def body(sidx_ref, *refs):
        gs, rs = refs[:4], refs[4:8]
        own_ref, pb_ref = refs[8:]
        own_ref[...] = gs[0][...] + rs[0][...]
        for j in range(3):
            pb_ref[j] = (gs[1 + j][...] + rs[1 + j][...]).astype(BF16)

    def gspec(slot):
        if col:
            return pl.BlockSpec((tr, C), lambda i, sx: (sx[4] * off + i, sx[slot]))
        return pl.BlockSpec((None, tr, C), lambda i, sx: (sx[slot], sx[4] * off + i, 0))

    def rspec(slot):
        if col:
            return pl.BlockSpec((tr, C), lambda i, sx: (i, sx[slot]))
        return pl.BlockSpec((None, tr, C), lambda i, sx: (sx[slot], i, 0))

    return pl.pallas_call(
        body, out_shape=(SDS((Rh, C), F32), SDS((3, Rh, C), BF16)),
        grid_spec=pltpu.PrefetchScalarGridSpec(
            num_scalar_prefetch=1, grid=(nt,),
            in_specs=[gspec(k) for k in range(4)] + [rspec(k) for k in range(4)],
            out_specs=(pl.BlockSpec((tr, C), lambda i, sx: (i, 0)), pl.BlockSpec((3, tr, C), lambda i, sx: (0, i, 0)))),
        compiler_params=_cparams("arbitrary"), name=f"pair_sum_w{wi}")(sidx, g, g, g, g, ra, ra, ra, ra)


def _chip_sum(wi, own, rb, sidx):
    R, C = W_SHARD[wi]
    Rh = R // 2
    tr = _row_tile(Rh, C)

    def body(sidx_ref, own_ref, rb_ref, o_ref):
        o_ref[...] = ((own_ref[...] + rb_ref[0].astype(F32)) + rb_ref[1].astype(F32)) + rb_ref[2].astype(F32)

    return pl.pallas_call(
        body, out_shape=SDS((2, Rh, C), F32),
        grid_spec=pltpu.PrefetchScalarGridSpec(
            num_scalar_prefetch=1, grid=(Rh // tr,),
            in_specs=[pl.BlockSpec((tr, C), lambda i, sx: (i, 0)), pl.BlockSpec((3, tr, C), lambda i, sx: (0, i, 0))],
            out_specs=pl.BlockSpec((None, tr, C), lambda i, sx: (sx[4], i, 0))),
        compiler_params=_cparams("arbitrary"), name=f"chip_sum_w{wi}")(sidx, own, rb)


def _gain_allgather(blk, ex):
    m_per, n = blk.shape
    n_in, n_out = len(ex.ins), len(ex.out_shapes)

    def body(x_ref, *rest):
        xin, out_ref, xout = rest[:n_in], rest[n_in], rest[n_in + 1:n_in + 1 + n_out]
        send_sems, recv_sems, local_sem = rest[n_in + 1 + n_out:n_in + 4 + n_out]
        ex_starts, ex_waits = ex.build(xin, xout, *rest[n_in + 4 + n_out:])
        for cp in ex_starts:
            cp.start()
        x, y, c, chips = _mesh_pos()
        me, sibling = (x, y, c), (x, y, 1 - c)

        def rows(px, py, pc):
            return out_ref.at[pl.ds((4 * px + 2 * py + pc) * m_per, m_per), :]

        def copy(k, block, to, src=None):
            return pltpu.make_async_remote_copy(
                src_ref=rows(*block) if src is None else src, dst_ref=rows(*block),
                send_sem=send_sems.at[k], recv_sem=recv_sems.at[k], device_id=to, device_id_type=MESH)

        mine = pltpu.make_async_copy(x_ref, rows(*me), local_sem)
        mine.start()
        first = [copy(0, me, sibling, src=x_ref)]
        first += [copy(1 + j, me, (*chip, c), src=x_ref) for j, chip in enumerate(chips)]
        for cp in first:
            cp.start()
        passed = [copy(4 + j, (*chip, c), sibling) for j, chip in enumerate(chips)]
        for j, chip in enumerate(chips):
            copy(1 + j, (*chip, c), me).wait_recv()
            passed[j].start()
        copy(0, sibling, me).wait_recv()
        for j, chip in enumerate(chips):
            copy(4 + j, (*chip, 1 - c), me).wait_recv()
        for cp in first + passed:
            cp.wait_send()
        mine.wait()
        for w in ex_waits:
            w()

    vm = pl.BlockSpec(memory_space=pltpu.VMEM)
    res = pl.pallas_call(
        body, out_shape=(SDS((8 * m_per, n), blk.dtype), *ex.out_shapes),
        in_specs=[vm] + [ANY] * n_in, out_specs=(vm, *[ANY] * n_out),
        input_output_aliases={1 + a: 1 + o for a, o in ex.aliases.items()},
        scratch_shapes=[pltpu.SemaphoreType.DMA((7,)), pltpu.SemaphoreType.DMA((7,)), pltpu.SemaphoreType.DMA] + ex.sems(),
        name="gain_allgather")(blk, *ex.ins)
    return res[0], tuple(res[1:])


def _adam_math(w, g, m, v):
    mn = ADAM_B1 * m + (1.0 - ADAM_B1) * g
    vn = ADAM_B2 * v + (1.0 - ADAM_B2) * (g * g)
    mh = mn / (1.0 - ADAM_B1 ** ADAM_STEP)
    vh = vn / (1.0 - ADAM_B2 ** ADAM_STEP)
    return -ADAM_LR * (mh / (jnp.sqrt(vh) + ADAM_EPS) + ADAM_WD * w), mn, vn


def _adamw(wi, w, g, m, v):
    R, C = w.shape
    tr = _row_tile(R, C)

    def body(w_ref, g_ref, m_ref, v_ref, go_ref, d_ref, mn_ref, vn_ref):
        g = g_ref[...]
        go_ref[...] = g
        d_ref[...], mn_ref[...], vn_ref[...] = _adam_math(w_ref[...], g, m_ref[...], v_ref[...])

    spec = pl.BlockSpec((tr, C), lambda i: (i, 0))
    return pl.pallas_call(body, out_shape=(SDS((R, C), F32),) * 4, grid=(R // tr,), in_specs=[spec] * 4,
                          out_specs=(spec,) * 4, compiler_params=_cparams("parallel"), name=f"adamw_w{wi}")(w, g, m, v)


def _gain_update(gathered, w, m, v):
    def body(ga_ref, w_ref, m_ref, v_ref, g_ref, d_ref, mn_ref, vn_ref):
        g = ga_ref[0:8, :]
        for dev in range(1, 8):
            g = g + ga_ref[8 * dev:8 * dev + 8, :]
        g_ref[...] = g
        d_ref[...], mn_ref[...], vn_ref[...] = _adam_math(w_ref[...], g, m_ref[...], v_ref[...])

    return pl.pallas_call(body, out_shape=(SDS((8, 1024), F32),) * 4, name="gain_update")(gathered, w, m, v)


GROUP_FFN, GROUP_MIX, GROUP_IN = (4, 5, 6), (1, 2, 3), (0,)
REST = GROUP_MIX + GROUP_FFN


class _MeshComm:
    SCHEDULE = {
        "rms_fwd": [("ring", GROUP_IN)],
        "in_proj": [("ici", (1, 2, 3, 4))],
        "ret_fwd": [("d2d", (1, 2, 3, 4)), ("ici", (5,))],
        "out_proj": [("d2d", (5,))],
        "ffn_up": [("both", (6,))],
        "out_proj_bwd": [("pair", GROUP_FFN)],
        "ret_bwd": [("pair", GROUP_MIX), ("chip", (4,))],
        "attn_bwd_g0": [("chip", (5,))],
        "attn_bwd_g1": [("chip", (6,))],
        "attn_bwd_g2": [("chip", GROUP_MIX)],
        "wgrad_in_kept": [("pair", GROUP_IN), ("share", GROUP_FFN + GROUP_MIX)],
        "in_proj_bwd": [("chip", GROUP_IN)],
    }

    def __init__(self, w_in_shard, rest_f32):
        xi, yi, ci = lax.axis_index("x"), lax.axis_index("y"), lax.axis_index("c")
        self.sidx = jnp.stack([2 * xi + yi, 2 * (1 - xi) + yi, 2 * xi + (1 - yi), 2 * (1 - xi) + (1 - yi), ci]).astype(jnp.int32)
        self.shards, self.rest_f32, self.full = {0: w_in_shard}, list(rest_f32), {}
        self.g, self.own, self.pb, self.half, self.red = {}, {}, {}, {}, {}

    def to_cast(self):
        return self.rest_f32

    def cast_done(self, casts):
        self.shards.update(zip(REST, casts))

    def weight(self, wi):
        return self.full[wi].reshape(D_MODEL, D_MODEL) if wi in (2, 3) else self.full[wi]

    def grads(self, by_wi):
        self.g.update(by_wi)

    def _exchange(self, stage, wis):
        pick = lambda table: [table[wi] for wi in wis]
        if stage == "ring":
            return _ex_gather_ring(wis, pick(self.shards))
        if stage == "ici":
            return _ex_gather_ici(wis, pick(self.shards))
        if stage == "both":
            return _ex_gather_ici(wis, pick(self.shards), then_d2d=True)
        if stage == "d2d":
            return _ex_gather_d2d(wis, pick(self.full))
        if stage == "pair":
            return _ex_pair(wis, [self.g["in_sent"] if wi == 0 else self.g[wi] for wi in wis])
        if stage == "chip":
            return _ex_chip(wis, pick(self.pb))
        return _ex_share(wis, pick(self.half))

    def _landed(self, stage, wis, res):
        for wi, r in zip(wis, res):
            if stage in ("ring", "ici", "d2d", "both"):
                self.full[wi] = r
            elif stage == "pair":
                self.own[wi], self.pb[wi] = _pair_sum(wi, self.g["in_kept"] if wi == 0 else self.g[wi], r, self.sidx)
            elif stage == "chip":
                self.half[wi] = _chip_sum(wi, self.own[wi], r, self.sidx)
            else:
                self.red[wi] = r

    def carry(self, point):
        return [self._exchange(stage, wis) for stage, wis in self.SCHEDULE.get(point, ())]

    def took(self, point, xres):
        for (stage, wis), res in zip(self.SCHEDULE.get(point, ()), xres):
            self._landed(stage, wis, res)

    def last_share(self):
        return self._exchange("share", GROUP_IN)

    def reduced(self, last_shared):
        self._landed("share", GROUP_IN, last_shared)
        return [self.red[wi] for wi in range(N_W)]


def kernel(x, norm_mix_g, w_in, w_out_attn, w_out_ret, w_out, norm_ffn_g, w_ffn_gate, w_ffn_up, w_ffn_down, norm_final_g, loss_target, m_norm_mix_g, m_w_in, m_w_out_attn, m_w_out_ret, m_w_out, m_norm_ffn_g, m_w_ffn_gate, m_w_ffn_up, m_w_ffn_down, m_norm_final_g, v_norm_mix_g, v_w_in, v_w_out_attn, v_w_out_ret, v_w_out, v_norm_ffn_g, v_w_ffn_gate, v_w_ffn_up, v_w_ffn_down, v_norm_final_g):
    ws = (w_in, w_out_attn, w_out_ret, w_out, w_ffn_gate, w_ffn_up, w_ffn_down)
    ms = (m_w_in, m_w_out_attn, m_w_out_ret, m_w_out, m_w_ffn_gate, m_w_ffn_up, m_w_ffn_down)
    vs = (v_w_in, v_w_out_attn, v_w_out_ret, v_w_out, v_w_ffn_gate, v_w_ffn_up, v_w_ffn_down)
    shard2d = lambda a, wi: a.reshape(W_SHARD[wi])

    comm = _MeshComm(_cast_bf16(shard2d(ws[0], 0)), [shard2d(ws[wi], wi) for wi in REST])
    g3 = norm_final_g.reshape(1, D_MODEL)
    loss_p, grad_x, gain_g = _step(x[0], loss_target[0], norm_mix_g, norm_ffn_g, g3, comm)

    pad8 = lambda rows: jnp.concatenate([r.reshape(1, D_MODEL) for r in rows]
                                        + [jnp.zeros((8 - len(rows), D_MODEL), F32)], axis=0)
    gathered, shared = _gain_allgather(pad8((*gain_g, jnp.tile(loss_p[0:1], (1, D_MODEL // 128)))), comm.last_share())
    gred = comm.reduced(shared)

    outs_g, outs_d, outs_m, outs_v = [], [], [], []
    for wi in range(N_W):
        g2d = gred[wi].reshape(W_SHARD[wi])
        gout, dlt, mn, vn = _adamw(wi, shard2d(ws[wi], wi), g2d, shard2d(ms[wi], wi), shard2d(vs[wi], wi))
        for lst, a in ((outs_g, gout), (outs_d, dlt), (outs_m, mn), (outs_v, vn)):
            lst.append(a.reshape(ws[wi].shape))

    gg, gd, gm, gv = _gain_update(gathered, pad8((norm_mix_g, norm_ffn_g, norm_final_g)),
                                  pad8((m_norm_mix_g, m_norm_ffn_g, m_norm_final_g)),
                                  pad8((v_norm_mix_g, v_norm_ffn_g, v_norm_final_g)))
    loss = gg[3, 0]

    def assemble(gain_rows, wlist):
        return (gain_rows[0:1], wlist[0], wlist[1], wlist[2], wlist[3], gain_rows[1:2],
                wlist[4], wlist[5], wlist[6], gain_rows[2])

    return (loss, grad_x[None], *assemble(gg, outs_g), *assemble(gd, outs_d), *assemble(gm, outs_m), *assemble(gv, outs_v))
```

```python
import functools
import math

import numpy as np
import jax
import jax.numpy as jnp
from jax import lax
from jax.experimental import pallas as pl
from jax.experimental.pallas import tpu as pltpu

F32, BF16 = jnp.float32, jnp.bfloat16
SDS = jax.ShapeDtypeStruct
MESH = pl.DeviceIdType.MESH

D_MODEL = 1024
PROJ_W = 9728
COLB = 512
N_COLB = PROJ_W // COLB
QA_B, KA_B, VA_B = 0, 3, 6
QR_B, KR_B = 9, 10
FFN_HID = 2816
N_SHARD = 4
HID_S = FFN_HID // N_SHARD
W_IN_S = PROJ_W // N_SHARD
DILATIONS = (1, 4, 16)
BLK = 128
RET_HEADS = 4
ROPE_THETA = 10000.0
NORM_EPS = 1e-6
ADAM_LR, ADAM_B1, ADAM_B2, ADAM_EPS, ADAM_WD, ADAM_STEP = 0.001, 0.9, 0.999, 1e-08, 0.01, 10
VMEM_LIMIT = 56 << 20


def _cparams(*sem):
    return pltpu.CompilerParams(dimension_semantics=sem or None, vmem_limit_bytes=VMEM_LIMIT)


def _dot(a, b):
    return jnp.dot(a, b, preferred_element_type=F32)


def _dot_nt(a, b):
    return lax.dot_general(a, b, (((1,), (1,)), ((), ())), preferred_element_type=F32)


def _dot_tn(a, b):
    return lax.dot_general(a, b, (((0,), (0,)), ((), ())), preferred_element_type=F32)


def _row_pieces(tm, sub=512):
    return [slice(i, i + sub) for i in range(0, tm, sub)]


def _sigmoid(z):
    return 0.5 * jnp.tanh(0.5 * z) + 0.5


ANY = pl.BlockSpec(memory_space=pl.ANY)


class _Exchange:
    def __init__(self, ins, out_shapes, aliases, n_sem, n_loc, build):
        self.ins, self.out_shapes, self.aliases = list(ins), list(out_shapes), dict(aliases)
        self.n_sem, self.n_loc, self.build = n_sem, n_loc, build

    def sems(self):
        return [pltpu.SemaphoreType.DMA((self.n_sem,)), pltpu.SemaphoreType.DMA((self.n_sem,)),
                pltpu.SemaphoreType.DMA((max(self.n_loc, 1),))]


def _carrier_call(body, args, *, out_shape, grid, in_specs, out_specs, scratch_shapes=(), sem, name, exchanges=(),
                  prefetch=None, in_out_aliases=None):
    out_shape, out_specs = tuple(out_shape), tuple(out_specs)
    n_in, n_out, n_scr = len(args), len(out_shape), len(scratch_shapes)
    n_pre = 0 if prefetch is None else 1
    x_args, x_outs, x_scr, spans = [], [], [], []
    aliases = {n_pre + a: o for a, o in (in_out_aliases or {}).items()}
    for ex in exchanges:
        i0, o0 = len(x_args), len(x_outs)
        for a, o in ex.aliases.items():
            aliases[n_pre + n_in + i0 + a] = n_out + o0 + o
        x_args += ex.ins
        x_outs += ex.out_shapes
        x_scr += ex.sems()
        spans.append((i0, len(ex.ins), o0, len(ex.out_shapes)))
    nx_in, nx_out = len(x_args), len(x_outs)

    def wrapped(*refs):
        refs = refs[n_pre:]
        ins, xin = refs[:n_in], refs[n_in:n_in + nx_in]
        o_base = n_in + nx_in
        outs, xout = refs[o_base:o_base + n_out], refs[o_base + n_out:o_base + n_out + nx_out]
        s_base = o_base + n_out + nx_out
        scr, xs = refs[s_base:s_base + n_scr], refs[s_base + n_scr:]

        def built(e):
            i0, ni, o0, no = spans[e]
            return exchanges[e].build(xin[i0:i0 + ni], xout[o0:o0 + no], *xs[3 * e:3 * e + 3])

        if exchanges:
            first = functools.reduce(jnp.logical_and, [pl.program_id(k) == 0 for k in range(len(grid))])
            last = functools.reduce(jnp.logical_and, [pl.program_id(k) == grid[k] - 1 for k in range(len(grid))])

            @pl.when(first)
            def _():
                for e in range(len(exchanges)):
                    for cp in built(e)[0]:
                        cp.start()

        body(*ins, *outs, *scr)

        if exchanges:
            @pl.when(last)
            def _():
                for e in range(len(exchanges)):
                    for w in built(e)[1]:
                        w()

    all_in, all_out = list(in_specs) + [ANY] * nx_in, out_specs + tuple([ANY] * nx_out)
    all_scr = list(scratch_shapes) + x_scr
    cparams = _cparams(*(sem if not exchanges else ("arbitrary",) * len(grid)))
    if prefetch is None:
        res = pl.pallas_call(wrapped, out_shape=out_shape + tuple(x_outs), grid=grid, in_specs=all_in, out_specs=all_out,
                             scratch_shapes=all_scr, input_output_aliases=aliases, compiler_params=cparams,
                             name=name)(*args, *x_args)
    else:
        gs = pltpu.PrefetchScalarGridSpec(num_scalar_prefetch=1, grid=grid, in_specs=all_in, out_specs=all_out,
                                          scratch_shapes=all_scr)
        res = pl.pallas_call(wrapped, out_shape=out_shape + tuple(x_outs), grid_spec=gs, input_output_aliases=aliases,
                             compiler_params=cparams, name=name)(prefetch, *args, *x_args)
    xres = [tuple(res[n_out + o0:n_out + o0 + no]) for (_, _, o0, no) in spans]
    return tuple(res[:n_out]), xres


def _tables(S):
    f32 = np.float32
    pos = np.arange(S, dtype=f32)
    lane = np.arange(128)
    inv = (f32(ROPE_THETA) ** (-np.arange(0, 64, 2, dtype=f32) / f32(64))).astype(f32)
    ang = (pos[:, None] * inv[None, :]).astype(np.float64)
    idx = (lane % 64) % 32
    c, s = np.cos(ang)[:, idx], np.sin(ang)[:, idx]
    first = ((lane % 64) < 32)[None, :]
    rope = np.stack([c, np.where(first, 0.0, s), np.where(first, -s, 0.0)])
    base = (f32(1.0) / (f32(ROPE_THETA) ** np.linspace(0.0, 1.0, 64, dtype=f32))).astype(f32)
    ang2 = (pos[:, None] * base[None, :]).astype(np.float64)
    c2, s2 = np.cos(ang2)[:, lane // 2], np.sin(ang2)[:, lane // 2]
    even = (lane % 2 == 0)[None, :]
    th = np.stack([c2, np.where(even, 0.0, s2), np.where(even, -s2, 0.0)])
    return np.stack([rope, th, th * (128 ** -0.5)]).astype(f32)


def _rot(a, c, sa, sb, shift):
    return a * c + pltpu.roll(a, shift, 1) * sa + pltpu.roll(a, 128 - shift, 1) * sb


def _unrot(g, c, sa, sb, shift):
    return g * c + pltpu.roll(g * sa, 128 - shift, 1) + pltpu.roll(g * sb, shift, 1)


def _ret_consts():
    h = np.arange(RET_HEADS, dtype=np.float64)
    log_g = np.log1p(-(2.0 ** (-5.0 - h)))
    idx = np.arange(BLK, dtype=np.float64)
    diff = idx[:, None] - idx[None, :]
    dmask = np.where(diff[None] >= 0, np.exp(np.maximum(diff, 0.0)[None] * log_g[:, None, None]), 0.0)
    zeta = np.exp((BLK - 1 - idx)[None, :] * log_g[:, None])
    xi = np.exp((idx + 1.0)[None, :] * log_g[:, None])
    dec = np.exp(BLK * log_g)
    rep = lambda v: np.broadcast_to(v[:, :, None], (RET_HEADS, BLK, 128))
    return (jnp.asarray(dmask, F32), jnp.asarray(rep(zeta), F32), jnp.asarray(rep(xi), F32),
            jnp.asarray(np.broadcast_to(dec[:, None, None], (RET_HEADS, 8, 256)), F32))


def _rms_fwd(x, g, to_cast=(), exchanges=()):
    S = x.shape[0]
    steps = 4
    tm = S // steps
    n_c = len(to_cast)

    def body(x_ref, g_ref, *refs):
        c_in, (h_ref, ht_ref), c_out = refs[:n_c], refs[n_c:n_c + 2], refs[n_c + 2:]
        for rows in _row_pieces(tm, 512):
            xv = x_ref[rows, :]
            r = lax.rsqrt(jnp.mean(xv * xv, axis=-1, keepdims=True) + NORM_EPS)
            h = xv * r * g_ref[...]
            h_ref[rows, :] = h.astype(BF16)
            ht_ref[:, rows] = h.T.astype(BF16)
        for a_ref, o_ref in zip(c_in, c_out):
            o_ref[...] = a_ref[...].astype(BF16)

    slab = lambda a: pl.BlockSpec((a.shape[0] // steps, a.shape[1]), lambda i: (i, 0))
    return _carrier_call(
        body, (x, g, *to_cast),
        out_shape=(SDS((S, D_MODEL), BF16), SDS((D_MODEL, S), BF16), *[SDS(a.shape, BF16) for a in to_cast]),
        grid=(steps,),
        in_specs=[pl.BlockSpec((tm, D_MODEL), lambda i: (i, 0)), pl.BlockSpec((1, D_MODEL), lambda i: (0, 0))]
        + [slab(a) for a in to_cast],
        out_specs=(pl.BlockSpec((tm, D_MODEL), lambda i: (i, 0)), pl.BlockSpec((D_MODEL, tm), lambda i: (0, i)),
                   *[slab(a) for a in to_cast]),
        sem=("parallel",), name="rms_fwd", exchanges=exchanges)


def _in_proj(h, w_in, tab, exchanges=()):
    S = h.shape[0]
    tm = min(S, 4096)

    def body(h_ref, w_ref, t_ref, o_ref):
        j = pl.program_id(1)
        is_rope = j < 6
        is_theta = (j == QR_B) | (j == KR_B)
        sub = 512

        def rotated(shift):
            for i in range(tm // sub):
                rows = slice(i * sub, (i + 1) * sub)
                acc = _dot(h_ref[rows, :], w_ref[...])
                c, sa, sb = t_ref[0, 0, rows, :], t_ref[0, 1, rows, :], t_ref[0, 2, rows, :]
                for k in range(COLB // 128):
                    sl = slice(k * 128, (k + 1) * 128)
                    o_ref[rows, sl] = _rot(acc[:, sl], c, sa, sb, shift).astype(BF16)

        @pl.when(is_rope)
        def _():
            rotated(32)

        @pl.when(is_theta)
        def _():
            rotated(1)

        @pl.when(jnp.logical_not(is_rope | is_theta))
        def _():
            o_ref[...] = _dot(h_ref[...], w_ref[...]).astype(BF16)

    def tab_map(i, j):
        return (jnp.where(j == QR_B, 1, jnp.where(j == KR_B, 2, 0)), 0, i, 0)

    (proj,), xres = _carrier_call(
        body, (h, w_in, tab), out_shape=(SDS((S, PROJ_W), BF16),), grid=(S // tm, N_COLB),
        in_specs=[pl.BlockSpec((tm, D_MODEL), lambda i, j: (i, 0)),
                  pl.BlockSpec((D_MODEL, COLB), lambda i, j: (0, j)),
                  pl.BlockSpec((1, 3, tm, 128), tab_map)],
        out_specs=(pl.BlockSpec((tm, COLB), lambda i, j: (i, j)),),
        sem=("parallel", "arbitrary"), name="in_proj", exchanges=exchanges)
    return proj, xres


def _band_mask(n):
    qi = lax.broadcasted_iota(jnp.int32, (BLK, 2 * BLK), 0)
    kj = lax.broadcasted_iota(jnp.int32, (BLK, 2 * BLK), 1)
    dist = BLK + qi - kj
    return (dist >= 0) & (dist <= BLK) & ((kj >= BLK) | (n > 0))


def _qkv_col(d, gi):
    if d == 1:
        return lambda t, r: 3 * t + gi
    return lambda t, r: 3 * r + t


def _attn_fwd(qkv, d, gi, exchanges=()):
    L = qkv.shape[0]
    nb = L // BLK

    def body(q_ref, kc_ref, kp_ref, vc_ref, vp_ref, o_ref, lse_ref):
        n = pl.program_id(1)
        mask = _band_mask(n)
        mask2 = jnp.concatenate([mask, mask], axis=0)
        lane = lax.broadcasted_iota(jnp.int32, (BLK, 128), 1)
        lo = lane < 64
        lse_all = jnp.zeros((BLK, 128), F32)
        chunks = [slice(c * 128, (c + 1) * 128) for c in range(4)]
        scores, vals = [], []
        for sl in chunks:
            q = q_ref[:, sl]
            k = jnp.concatenate([kp_ref[:, sl], kc_ref[:, sl]], axis=0)
            vals.append(jnp.concatenate([vp_ref[:, sl], vc_ref[:, sl]], axis=0))
            q2 = jnp.concatenate([jnp.where(lo, q, jnp.zeros_like(q)), jnp.where(lo, jnp.zeros_like(q), q)], axis=0)
            scores.append(_dot_nt(q2, k))
        probs = []
        for c, s in enumerate(scores):
            s = jnp.where(mask2, s * 0.125, jnp.float32(-1e30))
            m = jnp.max(s, axis=-1, keepdims=True)
            p = jnp.exp(s - m)
            l = jnp.sum(p, axis=-1, keepdims=True)
            probs.append((p / l).astype(BF16))
            lse = m + jnp.log(l)
            lse_all = jnp.where(lane // 16 == 2 * c, lse[:BLK], jnp.where(lane // 16 == 2 * c + 1, lse[BLK:], lse_all))
        for sl, p, v in zip(chunks, probs, vals):
            o2 = _dot(p, v)
            o_ref[:, sl] = jnp.where(lo, o2[:BLK], o2[BLK:])
        lse_ref[...] = lse_all

    prev = lambda n: jnp.maximum(n - 1, 0)
    col = _qkv_col(d, gi)
    return _carrier_call(
        body, (qkv,) * 5, out_shape=(SDS((L, d * 512), F32), SDS((L, d * 128), F32)), grid=(d, nb),
        in_specs=[pl.BlockSpec((BLK, 512), lambda r, n: (n, col(0, r))),
                  pl.BlockSpec((BLK, 512), lambda r, n: (n, col(1, r))),
                  pl.BlockSpec((BLK, 512), lambda r, n: (prev(n), col(1, r))),
                  pl.BlockSpec((BLK, 512), lambda r, n: (n, col(2, r))),
                  pl.BlockSpec((BLK, 512), lambda r, n: (prev(n), col(2, r)))],
        out_specs=(pl.BlockSpec((BLK, 512), lambda r, n: (n, r)),
                   pl.BlockSpec((BLK, 128), lambda r, n: (n, r))),
        sem=("parallel", "arbitrary"), name=f"attn_fwd_g{gi}", exchanges=exchanges)


def _qkv_to_sub(proj, d, gi):
    S = proj.shape[0]
    tm = 512
    n = tm // d

    def body(q_ref, k_ref, v_ref, o_ref, scr):
        for t, ref in enumerate((q_ref, k_ref, v_ref)):
            for c in range(4):
                scr[c] = ref[:, c * 128:(c + 1) * 128].astype(F32)
            for r in range(d):
                for c in range(4):
                    col = (3 * r + t) * 512 + c * 128
                    o_ref[:, col:col + 128] = scr[c, pl.ds(r, n, stride=d), :].astype(BF16)

    return pl.pallas_call(
        body, out_shape=SDS((S // d, d * 1536), BF16), grid=(S // tm,),
        in_specs=[pl.BlockSpec((tm, 512), lambda i, b=b: (i, b + gi)) for b in (QA_B, KA_B, VA_B)],
        out_specs=pl.BlockSpec((n, d * 1536), lambda i: (i, 0)),
        scratch_shapes=[pltpu.VMEM((4, tm, 128), F32)],
        compiler_params=_cparams("parallel"), name=f"qkv_to_sub_g{gi}")(proj, proj, proj)


def _attn_merge(os_, lses):
    S = os_[0].shape[0]
    tm = 512

    def body(o0, o1, o2, l0, l1, l2, att_ref, lt_ref, so1, so2, sl1, sl2):
        lo = lax.broadcasted_iota(jnp.int32, (tm, 128), 1) < 64

        def natural(ref, d, scr, width):
            nch = width // 128
            if d == 1:
                return [ref[:, c * 128:(c + 1) * 128] for c in range(nch)]
            for r in range(d):
                for c in range(nch):
                    scr[c, pl.ds(r, tm // d, stride=d), :] = ref[:, r * width + c * 128:r * width + (c + 1) * 128]
            return [scr[c] for c in range(nch)]

        ls = [natural(l, d, s, 128)[0] for l, d, s in zip((l0, l1, l2), DILATIONS, (None, sl1, sl2))]
        m = jnp.maximum(jnp.maximum(ls[0], ls[1]), ls[2])
        es = [jnp.exp(v - m) for v in ls]
        z = es[0] + es[1] + es[2]
        lt_ref[...] = m + jnp.log(z)
        ws = [e / z for e in es]
        o_nat = [natural(o, d, s, 512) for o, d, s in zip((o0, o1, o2), DILATIONS, (None, so1, so2))]
        for c in range(4):
            acc = jnp.zeros((tm, 128), F32)
            for g in range(3):
                w_lo = jnp.broadcast_to(ws[g][:, 32 * c:32 * c + 1], (tm, 128))
                w_hi = jnp.broadcast_to(ws[g][:, 32 * c + 16:32 * c + 17], (tm, 128))
                acc = acc + jnp.where(lo, w_lo, w_hi) * o_nat[g][c]
            att_ref[:, c * 128:(c + 1) * 128] = acc.astype(BF16)

    sub = lambda w: [pl.BlockSpec((tm // d, d * w), lambda i: (i, 0)) for d in DILATIONS]
    return pl.pallas_call(
        body, out_shape=(SDS((S, 512), BF16), SDS((S, 128), F32)), grid=(S // tm,),
        in_specs=sub(512) + sub(128),
        out_specs=(pl.BlockSpec((tm, 512), lambda i: (i, 0)), pl.BlockSpec((tm, 128), lambda i: (i, 0))),
        scratch_shapes=[pltpu.VMEM((4, tm, 128), F32), pltpu.VMEM((4, tm, 128), F32),
                        pltpu.VMEM((1, tm, 128), F32), pltpu.VMEM((1, tm, 128), F32)],
        compiler_params=_cparams("parallel"), name="attn_merge")(*os_, *lses)


def _assemble_dproj(att_grads, dproj):
    S = dproj.shape[0]
    tm = 256

    def body(*refs):
        a = [refs[3 * t:3 * t + 3] for t in range(3)]
        dp_prev, o_ref, scr = refs[9:]
        for t in range(3):
            for g, d in enumerate(DILATIONS):
                base = (3 * t + g) * COLB
                if d == 1:
                    o_ref[:, base:base + COLB] = a[t][g][...]
                    continue
                for c in range(4):
                    for r in range(d):
                        scr[c, pl.ds(r, tm // d, stride=d), :] = a[t][g][:, r * 512 + c * 128:r * 512 + (c + 1) * 128].astype(F32)
                    o_ref[:, base + c * 128:base + (c + 1) * 128] = scr[c].astype(BF16)

    sub = [pl.BlockSpec((tm // d, d * 512), lambda i: (i, 0)) for d in DILATIONS]
    flat = [att_grads[t][g] for t in range(3) for g in range(3)]
    return pl.pallas_call(
        body, out_shape=SDS((S, PROJ_W), BF16), grid=(S // tm,),
        in_specs=sub * 3 + [ANY], out_specs=pl.BlockSpec((tm, 9 * COLB), lambda i: (i, 0)),
        scratch_shapes=[pltpu.VMEM((4, tm, 128), F32)], input_output_aliases={9: 0},
        compiler_params=_cparams("parallel"), name="assemble_dproj")(*flat, dproj)


def _ret_fwd(proj, consts, exchanges=()):
    S = proj.shape[0]
    nc = S // BLK
    dmask, zeta, xi, dec = consts

    def body(q_ref, k_ref, v0_ref, v1_ref, g0_ref, g1_ref, dm_ref, z_ref, x_ref, dec_ref,
             y_ref, rn_ref, rs_ref, st_ref, R):
        @pl.when(pl.program_id(0) == 0)
        def _():
            R[...] = jnp.zeros_like(R)

        lane16 = lax.broadcasted_iota(jnp.int32, (BLK, 128), 1) // 16
        rs_all = jnp.zeros((BLK, 128), F32)
        first = []
        for h in range(RET_HEADS):
            hs = slice(h * 128, (h + 1) * 128)
            q, k = q_ref[:, hs], k_ref[:, hs]
            v = (v0_ref if h < 2 else v1_ref)[:, (h % 2) * 256:(h % 2 + 1) * 256]
            Rb = R[h].astype(BF16)
            st_ref[h] = Rb
            kz = (k.astype(F32) * z_ref[h]).astype(BF16)
            first.append((v, _dot_nt(q, k), _dot((q.astype(F32) * x_ref[h]).astype(BF16), Rb), _dot_tn(kz, v)))
        masked = [(s * dm_ref[h]).astype(BF16) for h, (_, s, _, _) in enumerate(first)]
        for h in range(RET_HEADS):
            vs = slice((h % 2) * 256, (h % 2 + 1) * 256)
            os_ = slice(h * 256, (h + 1) * 256)
            v, _, cross, kv = first[h]
            o = _dot(masked[h], v) + cross
            R[h] = R[h] * dec_ref[h, 0:1, :] + kv
            mu = jnp.mean(o, axis=-1, keepdims=True)
            oc = o - mu
            rstd = lax.rsqrt(jnp.mean(oc * oc, axis=-1, keepdims=True) + NORM_EPS)
            rn = oc * rstd
            gr = (g0_ref if h < 2 else g1_ref)[:, vs].astype(F32)
            y_ref[:, os_] = (rn * gr * _sigmoid(gr)).astype(BF16)
            rn_ref[:, os_] = rn.astype(BF16)
            rs_all = jnp.where(lane16 == h, rstd, rs_all)
        rs_ref[...] = rs_all

    cst = lambda shape: pl.BlockSpec(shape, lambda c: (0, 0, 0))
    blk = lambda j: pl.BlockSpec((BLK, 512), lambda c: (c, j))
    return _carrier_call(
        body, (proj, proj, proj, proj, proj, proj, dmask, zeta, xi, dec),
        out_shape=(SDS((S, 1024), BF16), SDS((S, 1024), BF16), SDS((S, 128), F32), SDS((RET_HEADS, nc, BLK, 256), BF16)),
        grid=(nc,),
        in_specs=[blk(QR_B), blk(KR_B), blk(11), blk(12), blk(13), blk(14),
                  cst((RET_HEADS, BLK, BLK)), cst((RET_HEADS, BLK, 128)), cst((RET_HEADS, BLK, 128)), cst((RET_HEADS, 8, 256))],
        out_specs=(pl.BlockSpec((BLK, 1024), lambda c: (c, 0)), pl.BlockSpec((BLK, 1024), lambda c: (c, 0)),
                   pl.BlockSpec((BLK, 128), lambda c: (c, 0)),
                   pl.BlockSpec((RET_HEADS, None, BLK, 256), lambda c: (0, c, 0, 0))),
        scratch_shapes=[pltpu.VMEM((RET_HEADS, BLK, 256), F32)],
        sem=("arbitrary",), name="ret_fwd", exchanges=exchanges)


def _mix_out(att, yrin, proj, wa, wr, wo, x, g2, exchanges=()):
    S = x.shape[0]
    tm = 512
    gate0 = 15 * COLB

    def body(a_ref, y_ref, ga_ref, gr_ref, wa_ref, wr_ref, wo_ref, x_ref, g_ref, m_ref, ya_ref, yr_ref, x1_ref, h2_ref):
        pieces = _row_pieces(tm, 256)
        branches = [(_dot(a_ref[rows, :], wa_ref[...]), _dot(y_ref[rows, :], wr_ref[...])) for rows in pieces]
        merged = []
        for rows, (ya, yr) in zip(pieces, branches):
            m = (_sigmoid(ga_ref[rows, :].astype(F32)) * ya + _sigmoid(gr_ref[rows, :].astype(F32)) * yr).astype(BF16)
            m_ref[rows, :] = m
            ya_ref[rows, :] = ya.astype(BF16)
            yr_ref[rows, :] = yr.astype(BF16)
            merged.append(m)
        for rows, m in zip(pieces, merged):
            x1 = x_ref[rows, :] + _dot(m, wo_ref[...])
            x1_ref[rows, :] = x1
            r = lax.rsqrt(jnp.mean(x1 * x1, axis=-1, keepdims=True) + NORM_EPS)
            h2_ref[rows, :] = (x1 * r * g_ref[...]).astype(BF16)

    row = lambda w: pl.BlockSpec((tm, w), lambda i: (i, 0))
    cols = lambda c0: pl.BlockSpec((pl.Element(tm), pl.Element(D_MODEL)), lambda i: (i * tm, c0))
    resident = lambda r, c: pl.BlockSpec((r, c), lambda i: (0, 0), pipeline_mode=pl.Buffered(1))
    return _carrier_call(
        body, (att, yrin, proj, proj, wa, wr, wo, x, g2),
        out_shape=(SDS((S, D_MODEL), BF16),) * 3 + (SDS((S, D_MODEL), F32), SDS((S, D_MODEL), BF16)), grid=(S // tm,),
        in_specs=[row(512), row(D_MODEL), cols(gate0), cols(gate0 + D_MODEL), resident(512, D_MODEL),
                  resident(D_MODEL, D_MODEL), resident(D_MODEL, D_MODEL), row(D_MODEL),
                  pl.BlockSpec((1, D_MODEL), lambda i: (0, 0))],
        out_specs=(row(D_MODEL),) * 5, sem=("parallel",), name="mix_out", exchanges=exchanges)


def _ffn_up(h2, wg, wu, exchanges=()):
    S = h2.shape[0]
    tm = min(S, 2048)

    def body(h_ref, wg_ref, wu_ref, g_ref, u_ref, a_ref):
        for rows in _row_pieces(tm):
            hv = h_ref[rows, :]
            g = _dot(hv, wg_ref[...])
            u = _dot(hv, wu_ref[...])
            g_ref[rows, :] = g.astype(BF16)
            u_ref[rows, :] = u.astype(BF16)
            a_ref[rows, :] = (g * _sigmoid(g) * u).astype(BF16)

    wspec = pl.BlockSpec((None, D_MODEL, HID_S), lambda i, s: (s, 0, 0))
    ospec = pl.BlockSpec((None, tm, HID_S), lambda i, s: (s, i, 0))
    return _carrier_call(
        body, (h2, wg, wu), out_shape=(SDS((N_SHARD, S, HID_S), BF16),) * 3, grid=(S // tm, N_SHARD),
        in_specs=[pl.BlockSpec((tm, D_MODEL), lambda i, s: (i, 0)), wspec, wspec],
        out_specs=(ospec, ospec, ospec),
        sem=("parallel", "arbitrary"), name="ffn_up", exchanges=exchanges)


def _ffn_down_loss(act, wd, x1, g3, tgt):
    S = x1.shape[0]
    tm = 512

    def body(a_ref, w_ref, x_ref, g_ref, t_ref, dx_ref, dxb_ref, dg_ref, ls_ref):
        @pl.when(pl.program_id(0) == 0)
        def _():
            dg_ref[...] = jnp.zeros_like(dg_ref)
            ls_ref[...] = jnp.zeros_like(ls_ref)

        g = g_ref[...]
        for rows in _row_pieces(tm, 256):
            y = _dot(a_ref[0, rows, :], w_ref[0])
            for s in range(1, N_SHARD):
                y = y + _dot(a_ref[s, rows, :], w_ref[s])
            x2 = x_ref[rows, :] + y
            r = lax.rsqrt(jnp.mean(x2 * x2, axis=-1, keepdims=True) + NORM_EPS)
            xh = x2 * r
            err = xh * g - t_ref[rows, :]
            ls_ref[...] += jnp.sum(jnp.sum(err * err, axis=-1, keepdims=True), axis=0, keepdims=True) * (0.5 / D_MODEL)
            dy = err * (1.0 / D_MODEL)
            dg_ref[...] += jnp.sum(dy * xh, axis=0, keepdims=True)
            dxh = dy * g
            dx = r * (dxh - xh * jnp.mean(dxh * xh, axis=-1, keepdims=True))
            dx_ref[rows, :] = dx
            dxb_ref[rows, :] = dx.astype(BF16)

    row = pl.BlockSpec((tm, D_MODEL), lambda i: (i, 0))
    vec = pl.BlockSpec((1, D_MODEL), lambda i: (0, 0))
    return pl.pallas_call(
        body, out_shape=(SDS((S, D_MODEL), F32), SDS((S, D_MODEL), BF16), SDS((1, D_MODEL), F32), SDS((8, 128), F32)),
        grid=(S // tm,),
        in_specs=[pl.BlockSpec((N_SHARD, tm, HID_S), lambda i: (0, i, 0)),
                  pl.BlockSpec((N_SHARD, HID_S, D_MODEL), lambda i: (0, 0, 0), pipeline_mode=pl.Buffered(1)),
                  row, vec, row],
        out_specs=(row, row, vec, pl.BlockSpec((8, 128), lambda i: (0, 0))),
        compiler_params=_cparams("arbitrary"), name="ffn_down_loss")(act, wd, x1, g3, tgt)


def _ffn_bwd(dx2b, dx2, wd, wg, wu, gte, up, x1, g2):
    S = x1.shape[0]
    tm = 256

    def body(d_ref, dx2_ref, wd_ref, wg_ref, wu_ref, g_ref, u_ref, x_ref, gn_ref,
             dg_ref, du_ref, dx_ref, dxb_ref, dgn_ref):
        @pl.when(pl.program_id(0) == 0)
        def _():
            dgn_ref[...] = jnp.zeros_like(dgn_ref)

        d = d_ref[...]
        dacts = [_dot_nt(d, wd_ref[s]) for s in range(N_SHARD)]
        dgs, dus = [], []
        for s, da in enumerate(dacts):
            g = g_ref[s].astype(F32)
            sg = _sigmoid(g)
            dgs.append((da * u_ref[s].astype(F32) * sg * (1.0 + g * (1.0 - sg))).astype(BF16))
            dus.append((da * g * sg).astype(BF16))
            dg_ref[s] = dgs[s]
            du_ref[s] = dus[s]
        dh = _dot_nt(dgs[0], wg_ref[0]) + _dot_nt(dus[0], wu_ref[0])
        for s in range(1, N_SHARD):
            dh = dh + _dot_nt(dgs[s], wg_ref[s]) + _dot_nt(dus[s], wu_ref[s])
        xv = x_ref[...]
        r = lax.rsqrt(jnp.mean(xv * xv, axis=-1, keepdims=True) + NORM_EPS)
        xh = xv * r
        dgn_ref[...] += jnp.sum(dh * xh, axis=0, keepdims=True)
        dxh = dh * gn_ref[...]
        dx = dx2_ref[...] + r * (dxh - xh * jnp.mean(dxh * xh, axis=-1, keepdims=True))
        dx_ref[...] = dx
        dxb_ref[...] = dx.astype(BF16)

    row = pl.BlockSpec((tm, D_MODEL), lambda i: (i, 0))
    vec = pl.BlockSpec((1, D_MODEL), lambda i: (0, 0))
    aspec = pl.BlockSpec((N_SHARD, tm, HID_S), lambda i: (0, i, 0))
    resident = lambda shape: pl.BlockSpec(shape, lambda i: (0, 0, 0), pipeline_mode=pl.Buffered(1))
    return pl.pallas_call(
        body,
        out_shape=(SDS((N_SHARD, S, HID_S), BF16), SDS((N_SHARD, S, HID_S), BF16),
                   SDS((S, D_MODEL), F32), SDS((S, D_MODEL), BF16), SDS((1, D_MODEL), F32)),
        grid=(S // tm,),
        in_specs=[row, row, resident((N_SHARD, HID_S, D_MODEL)), resident((N_SHARD, D_MODEL, HID_S)),
                  resident((N_SHARD, D_MODEL, HID_S)), aspec, aspec, row, vec],
        out_specs=(aspec, aspec, row, row, vec),
        compiler_params=_cparams("arbitrary"), name="ffn_bwd")(dx2b, dx2, wd, wg, wu, gte, up, x1, g2)


def _wgrad(name, a, b, a_spec, b_spec, out_shape, out_spec, n_par, S):
    tk = min(S, 4096)

    def body(a_ref, b_ref, o_ref):
        @pl.when(pl.program_id(1) == 0)
        def _():
            o_ref[...] = jnp.zeros_like(o_ref)

        o_ref[...] += _dot_tn(a_ref[...], b_ref[...])

    return pl.pallas_call(
        body, out_shape=SDS(out_shape, F32), grid=(n_par, S // tk),
        in_specs=[a_spec(tk), b_spec(tk)], out_specs=out_spec,
        compiler_params=_cparams("parallel", "arbitrary"), name=name)(a, b)


def _mix_bwd(dx1b, wo, proj, ya, yr, wa, wr, att, exchanges=()):
    S = dx1b.shape[0]
    tm = 512
    gate0 = 15 * COLB

    def body(d_ref, wo_ref, ga_ref, gr_ref, ya_ref, yr_ref, wa_ref, wr_ref, att_ref,
             dya_ref, dyr_ref, dp_ref, datt_ref, rho_ref, dyi_ref):
        pieces = _row_pieces(tm, 256)
        dms = [_dot_nt(d_ref[rows, :], wo_ref[...]) for rows in pieces]
        branch = []
        for rows, dm in zip(pieces, dms):
            sa = _sigmoid(ga_ref[rows, :].astype(F32))
            sr = _sigmoid(gr_ref[rows, :].astype(F32))
            dya, dyr = (dm * sa).astype(BF16), (dm * sr).astype(BF16)
            dya_ref[rows, :] = dya
            dyr_ref[rows, :] = dyr
            dp_ref[rows, 0:D_MODEL] = (dm * ya_ref[rows, :].astype(F32) * sa * (1.0 - sa)).astype(BF16)
            dp_ref[rows, D_MODEL:2 * D_MODEL] = (dm * yr_ref[rows, :].astype(F32) * sr * (1.0 - sr)).astype(BF16)
            branch.append((dya, dyr))
        lane = lax.broadcasted_iota(jnp.int32, (256, 128), 1)
        lo = lane < 64
        for rows, (dya, dyr) in zip(pieces, branch):
            datt = _dot_nt(dya, wa_ref[...])
            datt_ref[rows, :] = datt.astype(BF16)
            dyi_ref[rows, :] = _dot_nt(dyr, wr_ref[...]).astype(BF16)
            prod = datt * att_ref[rows, :].astype(F32)
            rho = jnp.zeros((256, 128), F32)
            for c in range(4):
                pc = prod[:, c * 128:(c + 1) * 128]
                tot = jnp.sum(pc, axis=-1, keepdims=True)
                low = jnp.sum(jnp.where(lo, pc, 0.0), axis=-1, keepdims=True)
                rho = jnp.where(lane // 16 == 2 * c, low, jnp.where(lane // 16 == 2 * c + 1, tot - low, rho))
            rho_ref[rows, :] = rho

    row = lambda w: pl.BlockSpec((tm, w), lambda i: (i, 0))
    cols = lambda c0, w: pl.BlockSpec((pl.Element(tm), pl.Element(w)), lambda i: (i * tm, c0))
    resident = lambda r, c: pl.BlockSpec((r, c), lambda i: (0, 0), pipeline_mode=pl.Buffered(1))
    return _carrier_call(
        body, (dx1b, wo, proj, proj, ya, yr, wa, wr, att),
        out_shape=(SDS((S, D_MODEL), BF16), SDS((S, D_MODEL), BF16), SDS((S, PROJ_W), BF16),
                   SDS((S, 512), BF16), SDS((S, 128), F32), SDS((S, D_MODEL), BF16)),
        grid=(S // tm,),
        in_specs=[row(D_MODEL), resident(D_MODEL, D_MODEL), cols(gate0, D_MODEL), cols(gate0 + D_MODEL, D_MODEL),
                  row(D_MODEL), row(D_MODEL), resident(512, D_MODEL), resident(D_MODEL, D_MODEL), row(512)],
        out_specs=(row(D_MODEL), row(D_MODEL), cols(gate0, 2 * D_MODEL), row(512), row(128), row(D_MODEL)),
        sem=("parallel",), name="mix_bwd", exchanges=exchanges)


def _attn_bwd(qkv, datt, lse, rho, rtab, d, gi, exchanges=()):
    L = qkv.shape[0]
    nb = L // BLK
    T = d * nb

    def body(q_ref, kc_ref, kp_ref, vc_ref, vp_ref, do_ref, lse_ref, rho_ref, tq_ref, tk_ref,
             dq_ref, dk_ref, dv_ref, ck, cv):
        t = pl.program_id(0)
        n = jnp.minimum(t, T - 1) % nb

        @pl.when(t == 0)
        def _():
            ck[...] = jnp.zeros_like(ck)
            cv[...] = jnp.zeros_like(cv)

        def store_rot(ref, val, t_ref, c):
            sl = slice(c * 128, (c + 1) * 128)
            ref[:, sl] = _unrot(val, t_ref[0], t_ref[1], t_ref[2], 32).astype(BF16)

        @pl.when(t < T)
        def _():
            mask = _band_mask(n)
            mask2 = jnp.concatenate([mask, mask], axis=0)
            lo = lax.broadcasted_iota(jnp.int32, (BLK, 128), 1) < 64

            def stacked(a):
                return jnp.concatenate([jnp.where(lo, a, jnp.zeros_like(a)), jnp.where(lo, jnp.zeros_like(a), a)], axis=0)

            def head_cols(ref, c):
                return jnp.concatenate([jnp.broadcast_to(ref[:, 32 * c:32 * c + 1], (BLK, 2 * BLK)),
                                        jnp.broadcast_to(ref[:, 32 * c + 16:32 * c + 17], (BLK, 2 * BLK))], axis=0)

            ops, raw = [], []
            for c in range(4):
                sl = slice(c * 128, (c + 1) * 128)
                q2, do2 = stacked(q_ref[:, sl]), stacked(do_ref[:, sl])
                k = jnp.concatenate([kp_ref[:, sl], kc_ref[:, sl]], axis=0)
                v = jnp.concatenate([vp_ref[:, sl], vc_ref[:, sl]], axis=0)
                ops.append((q2, do2, k))
                raw.append((_dot_nt(q2, k), _dot_nt(do2, v)))
            grads = []
            for c, (s, dp) in enumerate(raw):
                p = jnp.where(mask2, jnp.exp(s * 0.125 - head_cols(lse_ref, c)), 0.0)
                grads.append(((p * (dp - head_cols(rho_ref, c)) * 0.125).astype(BF16), p.astype(BF16)))
            for c, ((q2, do2, k), (ds, pb)) in enumerate(zip(ops, grads)):
                sl = slice(c * 128, (c + 1) * 128)
                dq2 = _dot(ds, k)
                dq_c = jnp.where(lo, dq2[:BLK], dq2[BLK:])
                dk_c = _dot_tn(ds, q2)
                dv_c = _dot_tn(pb, do2)
                store_rot(dq_ref, dq_c, tq_ref, c)
                store_rot(dk_ref, ck[:, sl] + dk_c[:BLK], tk_ref, c)
                dv_ref[:, sl] = (cv[:, sl] + dv_c[:BLK]).astype(BF16)
                ck[:, sl] = dk_c[BLK:]
                cv[:, sl] = dv_c[BLK:]

        @pl.when(t == T)
        def _():
            for c in range(4):
                sl = slice(c * 128, (c + 1) * 128)
                store_rot(dk_ref, ck[:, sl], tk_ref, c)
            dv_ref[...] = cv[...].astype(BF16)

    blk_of = lambda t: (jnp.minimum(t, T - 1) % nb, jnp.minimum(t, T - 1) // nb)
    cur = lambda t: blk_of(t)
    prev = lambda t: (jnp.maximum(blk_of(t)[0] - 1, 0), blk_of(t)[1])
    fin = lambda t: blk_of(jnp.maximum(t - 1, 0))
    col = _qkv_col(d, gi)
    qkv_spec = lambda kind, which: pl.BlockSpec((BLK, 512), lambda t: (which(t)[0], col(kind, which(t)[1])))
    row_spec = lambda w, which: pl.BlockSpec((BLK, w), lambda t: which(t))
    tab_spec = lambda which: pl.BlockSpec((3, BLK, 128), lambda t: (0, *which(t)))
    return _carrier_call(
        body, (qkv, qkv, qkv, qkv, qkv, datt, lse, rho, rtab, rtab),
        out_shape=(SDS((L, d * 512), BF16),) * 3, grid=(T + 1,),
        in_specs=[qkv_spec(0, cur), qkv_spec(1, cur), qkv_spec(1, prev), qkv_spec(2, cur), qkv_spec(2, prev),
                  row_spec(512, cur), row_spec(128, cur), row_spec(128, cur), tab_spec(cur), tab_spec(fin)],
        out_specs=(row_spec(512, cur), row_spec(512, fin), row_spec(512, fin)),
        scratch_shapes=[pltpu.VMEM((BLK, 512), F32), pltpu.VMEM((BLK, 512), F32)],
        sem=("arbitrary",), name=f"attn_bwd_g{gi}", exchanges=exchanges)


def _ret_bwd(proj, rn, rstd, dyrin, states, tab, consts, dproj, exchanges=()):
    S = proj.shape[0]
    nc = S // BLK
    dmask, zeta, xi, dec = consts

    def body(q_ref, k_ref, v0_ref, v1_ref, g0_ref, g1_ref, rn_ref, rs_ref, dy_ref, st_ref, tq_ref, tk_ref,
             dm_ref, z_ref, x_ref, dec_ref, dp_prev, dp_ref, dR):
        dq_ref, dk_ref = dp_ref.at[:, 0:512], dp_ref.at[:, 512:1024]
        dv_ref, dgr_ref = dp_ref.at[:, 1024:2048], dp_ref.at[:, 2048:3072]

        @pl.when(pl.program_id(0) == 0)
        def _():
            dR[...] = jnp.zeros_like(dR)

        dobs = []
        for h in range(RET_HEADS):
            vs = slice((h % 2) * 256, (h % 2 + 1) * 256)
            os_ = slice(h * 256, (h + 1) * 256)
            gr = (g0_ref if h < 2 else g1_ref)[:, vs].astype(F32)
            sg = _sigmoid(gr)
            rn_v = rn_ref[:, os_].astype(F32)
            dyi = dy_ref[:, os_].astype(F32)
            dgr_ref[:, os_] = (dyi * rn_v * sg * (1.0 + gr * (1.0 - sg))).astype(BF16)
            drn = dyi * gr * sg
            rstd = jnp.broadcast_to(rs_ref[:, 16 * h:16 * h + 1], (BLK, 256))
            do = rstd * (drn - jnp.mean(drn, axis=-1, keepdims=True) - rn_v * jnp.mean(drn * rn_v, axis=-1, keepdims=True))
            dobs.append(do.astype(BF16))
        first = []
        for h in range(RET_HEADS):
            hs = slice(h * 128, (h + 1) * 128)
            q, k = q_ref[:, hs], k_ref[:, hs]
            v = (v0_ref if h < 2 else v1_ref)[:, (h % 2) * 256:(h % 2 + 1) * 256]
            dob, dRb = dobs[h], dR[h].astype(BF16)
            kz = (k.astype(F32) * z_ref[h]).astype(BF16)
            qx = (q.astype(F32) * x_ref[h]).astype(BF16)
            first.append((q, k, _dot_nt(q, k), _dot_nt(dob, v), _dot(kz, dRb), _dot_nt(dob, st_ref[h]),
                          _dot_nt(v, dRb), _dot_tn(qx, dob)))
        masked = [((s * dm_ref[h]).astype(BF16), (dsr * dm_ref[h]).astype(BF16))
                  for h, (_, _, s, dsr, _, _, _, _) in enumerate(first)]
        for h in range(RET_HEADS):
            hs = slice(h * 128, (h + 1) * 128)
            os_ = slice(h * 256, (h + 1) * 256)
            q, k, _, _, dv_state, dq_state, dk_state, dr_new = first[h]
            sD, dS = masked[h]
            dv_ref[:, os_] = (_dot_tn(sD, dobs[h]) + dv_state).astype(BF16)
            dq = _dot(dS, k) + dq_state * x_ref[h]
            dk = _dot_tn(dS, q) + dk_state * z_ref[h]
            dR[h] = dR[h] * dec_ref[h, 0:1, :] + dr_new
            dq_ref[:, hs] = _unrot(dq, tq_ref[0], tq_ref[1], tq_ref[2], 1).astype(BF16)
            dk_ref[:, hs] = _unrot(dk, tk_ref[0], tk_ref[1], tk_ref[2], 1).astype(BF16)

    rc = lambda c: nc - 1 - c
    cst = lambda shape: pl.BlockSpec(shape, lambda c: (0, 0, 0))
    blk = lambda j: pl.BlockSpec((BLK, 512), lambda c: (rc(c), j))
    row = lambda w: pl.BlockSpec((BLK, w), lambda c: (rc(c), 0))
    (dproj,), xres = _carrier_call(
        body, (proj, proj, proj, proj, proj, proj, rn, rstd, dyrin, states, tab, tab, dmask, zeta, xi, dec, dproj),
        out_shape=(SDS((S, PROJ_W), BF16),), grid=(nc,),
        in_specs=[blk(QR_B), blk(KR_B), blk(11), blk(12), blk(13), blk(14), row(1024), row(128), row(1024),
                  pl.BlockSpec((RET_HEADS, None, BLK, 256), lambda c: (0, rc(c), 0, 0)),
                  pl.BlockSpec((None, 3, BLK, 128), lambda c: (1, 0, rc(c), 0)),
                  pl.BlockSpec((None, 3, BLK, 128), lambda c: (2, 0, rc(c), 0)),
                  cst((RET_HEADS, BLK, BLK)), cst((RET_HEADS, BLK, 128)), cst((RET_HEADS, BLK, 128)), cst((RET_HEADS, 8, 256)),
                  ANY],
        out_specs=(pl.BlockSpec((pl.Element(BLK), pl.Element(6 * COLB)), lambda c: (rc(c) * BLK, QR_B * COLB)),),
        scratch_shapes=[pltpu.VMEM((RET_HEADS, BLK, 256), F32)],
        sem=("arbitrary",), name="ret_bwd", exchanges=exchanges, in_out_aliases={16: 0})
    return dproj, xres


def _wgrad_in_half(ht, dproj, sidx, kept, exchanges=()):
    S = dproj.shape[0]
    tk = 2048
    half = (lambda sx: sx[4]) if kept else (lambda sx: 1 - sx[4])

    def body(a_ref, b_ref, o_ref):
        @pl.when(pl.program_id(1) == 0)
        def _():
            o_ref[...] = jnp.zeros_like(o_ref)

        o_ref[...] += _dot(a_ref[...], b_ref[...])

    (g,), xres = _carrier_call(
        body, (ht, dproj), out_shape=(SDS((D_MODEL // 2, PROJ_W), F32),), grid=(N_SHARD, S // tk),
        in_specs=[pl.BlockSpec((D_MODEL // 2, tk), lambda s, k, sx: (half(sx), k)),
                  pl.BlockSpec((tk, W_IN_S), lambda s, k, sx: (k, s))],
        out_specs=(pl.BlockSpec((D_MODEL // 2, W_IN_S), lambda s, k, sx: (0, s)),),
        sem=("parallel", "arbitrary"), name="wgrad_in_kept" if kept else "wgrad_in_sent", exchanges=exchanges,
        prefetch=sidx)
    return g, xres


def _in_proj_bwd(dproj, w_in, x, g1, dx1, exchanges=()):
    S = x.shape[0]
    tm = 1024

    def body(d_ref, w_ref, x_ref, g_ref, dx1_ref, dx_ref, dgn_ref, acc):
        i, s = pl.program_id(0), pl.program_id(1)

        @pl.when(s == 0)
        def _():
            acc[...] = jnp.zeros_like(acc)

        @pl.when((i == 0) & (s == 0))
        def _():
            dgn_ref[...] = jnp.zeros_like(dgn_ref)

        acc[...] += _dot_nt(d_ref[...], w_ref[...])

        @pl.when(s == N_SHARD - 1)
        def _():
            xv = x_ref[...]
            r = lax.rsqrt(jnp.mean(xv * xv, axis=-1, keepdims=True) + NORM_EPS)
            xh = xv * r
            dh = acc[...]
            dgn_ref[...] += jnp.sum(dh * xh, axis=0, keepdims=True)
            dxh = dh * g_ref[...]
            dx_ref[...] = dx1_ref[...] + r * (dxh - xh * jnp.mean(dxh * xh, axis=-1, keepdims=True))

    row = pl.BlockSpec((tm, D_MODEL), lambda i, s: (i, 0))
    vec = pl.BlockSpec((1, D_MODEL), lambda i, s: (0, 0))
    (gx, dg), xres = _carrier_call(
        body, (dproj, w_in, x, g1, dx1),
        out_shape=(SDS((S, D_MODEL), F32), SDS((1, D_MODEL), F32)), grid=(S // tm, N_SHARD),
        in_specs=[pl.BlockSpec((tm, W_IN_S), lambda i, s: (i, s)),
                  pl.BlockSpec((D_MODEL, W_IN_S), lambda i, s: (0, s)), row, vec, row],
        out_specs=(row, vec), scratch_shapes=[pltpu.VMEM((tm, D_MODEL), F32)],
        sem=("arbitrary", "arbitrary"), name="in_proj_bwd", exchanges=exchanges)
    return gx, dg, xres


def _sub_view(a, d):
    S, W = a.shape
    return a.reshape(S // d, d * W)


def _step(x, tgt, g1, g2, g3, comm):
    S = x.shape[0]
    tab_np = _tables(S)
    tab = jnp.asarray(tab_np)
    consts = _ret_consts()

    (h, ht, *casts), xres = _rms_fwd(x, g1, comm.to_cast(), comm.carry("rms_fwd"))
    comm.cast_done(casts)
    comm.took("rms_fwd", xres)
    w_in = comm.weight(0)
    proj, xres = _in_proj(h, w_in, tab, comm.carry("in_proj"))
    comm.took("in_proj", xres)
    qkvs, o_parts, lse_parts = [], [], []
    for gi, d in enumerate(DILATIONS):
        qkv = proj if d == 1 else _qkv_to_sub(proj, d, gi)
        (o_g, lse_g), xres = _attn_fwd(qkv, d, gi, comm.carry(f"attn_fwd_g{gi}"))
        comm.took(f"attn_fwd_g{gi}", xres)
        qkvs.append(qkv)
        o_parts.append(o_g)
        lse_parts.append(lse_g)
    att, lse_tot = _attn_merge(o_parts, lse_parts)
    (yrin, rn, rstd, states), xres = _ret_fwd(proj, consts, comm.carry("ret_fwd"))
    comm.took("ret_fwd", xres)
    wa, wr, wo = comm.weight(1), comm.weight(2), comm.weight(3)
    (merged, ya, yr, x1, h2), xres = _mix_out(att, yrin, proj, wa, wr, wo, x, g2, comm.carry("mix_out"))
    comm.took("mix_out", xres)
    wg, wu = comm.weight(4), comm.weight(5)
    (gte, up, act), xres = _ffn_up(h2, wg, wu, comm.carry("ffn_up"))
    comm.took("ffn_up", xres)
    wd = comm.weight(6)
    dx2, dx2b, dg3, loss_p = _ffn_down_loss(act, wd, x1, g3, tgt)

    dgte, dup, dx1, dx1b, dg2 = _ffn_bwd(dx2b, dx2, wd, wg, wu, gte, up, x1, g2)
    tok3 = lambda w: (lambda tk: pl.BlockSpec((None, tk, w), lambda p, k: (p, k, 0)))
    tok2 = lambda w: (lambda tk: pl.BlockSpec((tk, w), lambda p, k: (k, 0)))
    g_d = _wgrad("wgrad_down", act, dx2b, tok3(HID_S), tok2(D_MODEL), (N_SHARD, HID_S, D_MODEL),
                 pl.BlockSpec((None, HID_S, D_MODEL), lambda p, k: (p, 0, 0)), N_SHARD, S)
    g_g = _wgrad("wgrad_gate", h2, dgte, tok2(D_MODEL), tok3(HID_S), (N_SHARD, D_MODEL, HID_S),
                 pl.BlockSpec((None, D_MODEL, HID_S), lambda p, k: (p, 0, 0)), N_SHARD, S)
    g_u = _wgrad("wgrad_up", h2, dup, tok2(D_MODEL), tok3(HID_S), (N_SHARD, D_MODEL, HID_S),
                 pl.BlockSpec((None, D_MODEL, HID_S), lambda p, k: (p, 0, 0)), N_SHARD, S)
    comm.grads({4: g_g, 5: g_u, 6: g_d})
    (dya, dyr, dproj, datt, rho, dyrin), xres = _mix_bwd(dx1b, wo, proj, ya, yr, wa, wr, att, comm.carry("mix_bwd"))
    comm.took("mix_bwd", xres)
    colblk = lambda w: (lambda tk: pl.BlockSpec((tk, w), lambda p, k: (k, p)))
    g_o = _wgrad("wgrad_out", merged, dx1b, colblk(256), tok2(D_MODEL), (D_MODEL, D_MODEL),
                 pl.BlockSpec((256, D_MODEL), lambda p, k: (p, 0)), 4, S)
    g_a = _wgrad("wgrad_attn", att, dya, tok2(512), colblk(512), (512, D_MODEL),
                 pl.BlockSpec((512, 512), lambda p, k: (0, p)), 2, S)
    g_r = _wgrad("wgrad_ret", yrin, dyr, colblk(256), tok2(D_MODEL), (D_MODEL, D_MODEL),
                 pl.BlockSpec((256, D_MODEL), lambda p, k: (p, 0)), 4, S)
    comm.grads({1: g_a, 2: g_r.reshape(N_SHARD, 256, D_MODEL), 3: g_o.reshape(N_SHARD, 256, D_MODEL)})
    dproj, xres = _ret_bwd(proj, rn, rstd, dyrin, states, tab, consts, dproj, comm.carry("ret_bwd"))
    comm.took("ret_bwd", xres)
    dqs, dks, dvs = [], [], []
    for gi, d in enumerate(DILATIONS):
        rtab = jnp.asarray(tab_np[0].reshape(3, S // d, d * 128))
        (dq, dk, dv), xres = _attn_bwd(qkvs[gi], _sub_view(datt, d), _sub_view(lse_tot, d), _sub_view(rho, d), rtab, d, gi,
                                       comm.carry(f"attn_bwd_g{gi}"))
        comm.took(f"attn_bwd_g{gi}", xres)
        dqs.append(dq)
        dks.append(dk)
        dvs.append(dv)
    dproj = _assemble_dproj((dqs, dks, dvs), dproj)
    g_sent, xres = _wgrad_in_half(ht, dproj, comm.sidx, False, comm.carry("wgrad_in_sent"))
    comm.took("wgrad_in_sent", xres)
    comm.grads({"in_sent": g_sent})
    g_kept, xres = _wgrad_in_half(ht, dproj, comm.sidx, True, comm.carry("wgrad_in_kept"))
    comm.grads({"in_kept": g_kept})
    comm.took("wgrad_in_kept", xres)
    grad_x, dg1, xres = _in_proj_bwd(dproj, w_in, x, g1, dx1, comm.carry("in_proj_bwd"))
    comm.took("in_proj_bwd", xres)
    return loss_p, grad_x, (dg1, dg2, dg3)


W_KINDS = ("col", "col", "lead", "lead", "lead", "lead", "lead")
W_SHARD = ((1024, W_IN_S), (512, 256), (256, 1024), (256, 1024), (1024, HID_S), (1024, HID_S), (HID_S, 1024))
N_W = len(W_KINDS)


def _full_shape(wi):
    R, C = W_SHARD[wi]
    return (R, N_SHARD * C) if W_KINDS[wi] == "col" else (N_SHARD, R, C)


def _view(ref, wi, s, half):
    R, C = W_SHARD[wi]
    rows = pl.ds(half * (R // 2), R // 2)
    if W_KINDS[wi] == "col":
        return ref.at[rows, pl.ds(pl.multiple_of(s * C, 128), C)]
    return ref.at[s, rows, :]


def _mesh_pos():
    x, y, c = lax.axis_index("x"), lax.axis_index("y"), lax.axis_index("c")
    chips = [(1 - x, y), (x, 1 - y), (1 - x, 1 - y)]
    return x, y, c, chips


def _cast_bf16(a):
    R, C = a.shape
    tr = R // 2 if R % 32 == 0 else R

    def body(a_ref, o_ref):
        o_ref[...] = a_ref[...].astype(BF16)

    spec = pl.BlockSpec((tr, C), lambda i: (i, 0))
    return pl.pallas_call(body, out_shape=SDS((R, C), BF16), grid=(R // tr,), in_specs=[spec], out_specs=spec,
                          compiler_params=_cparams("parallel"), name=f"cast_{R}x{C}")(a)


def _remote(send, recv, k, src, dst, to):
    return pltpu.make_async_remote_copy(src_ref=src, dst_ref=dst, send_sem=send.at[k], recv_sem=recv.at[k],
                                        device_id=to, device_id_type=MESH)


def _ex_gather_ring(wis, shards):
    n = len(wis)

    def build(sh, full, send, recv, loc):
        x, y, c, _ = _mesh_pos()
        s_me, sib = 2 * x + y, (x, y, 1 - c)
        xn, yn = (1 - x, y), (x, 1 - y)
        flip = lambda a, b: a + b - 2 * a * b
        via = (flip(x, 1 - c), flip(y, c))
        onto = (flip(x, c), flip(y, 1 - c))
        shard_of = lambda chip: 2 * chip[0] + chip[1]
        starts, waits, sent = [], [], []
        for i, wi in enumerate(wis):
            Rh = W_SHARD[wi][0] // 2
            for hf in range(2):
                cp = pltpu.make_async_copy(sh[i].at[pl.ds(hf * Rh, Rh), :], _view(full[i], wi, s_me, hf), loc.at[2 * i + hf])
                starts.append(cp)
                sent.append(cp.wait)
            for j, chip in enumerate((xn, yn)):
                cp = _remote(send, recv, 6 * i + j, sh[i].at[pl.ds(c * Rh, Rh), :], _view(full[i], wi, s_me, c), (*chip, c))
                starts.append(cp)
                sent.append(cp.wait_send)

        def pass_to_sibling(i, wi, k, s):
            mine = _view(full[i], wi, s, c)
            fw = _remote(send, recv, 6 * i + k, mine, mine, sib)
            waits.append(fw.start)
            sent.append(fw.wait_send)

        for i, wi in enumerate(wis):
            for j, chip in enumerate((xn, yn)):
                land = _view(full[i], wi, shard_of(chip), c)
                waits.append(_remote(send, recv, 6 * i + j, land, land, (*chip, c)).wait_recv)
                pass_to_sibling(i, wi, 3 + j, shard_of(chip))
            relay = _view(full[i], wi, shard_of(via), c)
            fw = _remote(send, recv, 6 * i + 2, relay, relay, (*onto, c))
            waits.append(fw.start)
            sent.append(fw.wait_send)
        s_diag = 2 * (1 - x) + (1 - y)
        for i, wi in enumerate(wis):
            land = _view(full[i], wi, s_diag, c)
            waits.append(_remote(send, recv, 6 * i + 2, land, land, (*onto, c)).wait_recv)
            pass_to_sibling(i, wi, 5, s_diag)
        for i, wi in enumerate(wis):
            for k, s in ((3, shard_of(xn)), (4, shard_of(yn)), (5, s_diag)):
                land = _view(full[i], wi, s, 1 - c)
                waits.append(_remote(send, recv, 6 * i + k, land, land, sib).wait_recv)
        return starts, waits + sent

    return _Exchange(shards, [SDS(_full_shape(wi), BF16) for wi in wis], {}, 6 * n, 2 * n, build)


def _ex_gather_ici(wis, shards, then_d2d=False):
    n = len(wis)

    def build(ins, outs, send, recv, loc):
        x, y, c, chips = _mesh_pos()
        s_me, sib = 2 * x + y, (x, y, 1 - c)
        starts, waits, after = [], [], []
        for i, wi in enumerate(wis):
            Rh = W_SHARD[wi][0] // 2
            for hf in range(2):
                cp = pltpu.make_async_copy(ins[i].at[pl.ds(hf * Rh, Rh), :], _view(outs[i], wi, s_me, hf), loc.at[2 * i + hf])
                starts.append(cp)
                waits.append(cp.wait)
            for j, chip in enumerate(chips):
                cp = _remote(send, recv, 3 * i + j, ins[i].at[pl.ds(c * Rh, Rh), :], _view(outs[i], wi, s_me, c), (*chip, c))
                land = _view(outs[i], wi, 2 * chip[0] + chip[1], c)
                starts.append(cp)
                waits += [cp.wait_send, _remote(send, recv, 3 * i + j, land, land, (*chip, c)).wait_recv]
                if then_d2d:
                    theirs = _view(outs[i], wi, 2 * chip[0] + chip[1], 1 - c)
                    fw = _remote(send, recv, 3 * n + 3 * i + j, land, land, sib)
                    waits.append(fw.start)
                    after += [fw.wait_send, _remote(send, recv, 3 * n + 3 * i + j, theirs, theirs, sib).wait_recv]
        return starts, waits + after

    return _Exchange(shards, [SDS(_full_shape(wi), BF16) for wi in wis], {}, (6 if then_d2d else 3) * n, 2 * n, build)


def _ex_gather_d2d(wis, fulls):
    def build(ins, outs, send, recv, loc):
        x, y, c, chips = _mesh_pos()
        sib = (x, y, 1 - c)
        starts, waits = [], []
        for i, wi in enumerate(wis):
            for j, chip in enumerate(chips):
                mine = _view(outs[i], wi, 2 * chip[0] + chip[1], c)
                theirs = _view(outs[i], wi, 2 * chip[0] + chip[1], 1 - c)
                cp = _remote(send, recv, 3 * i + j, mine, mine, sib)
                starts.append(cp)
                waits += [cp.wait_send, _remote(send, recv, 3 * i + j, theirs, theirs, sib).wait_recv]
        return starts, waits

    return _Exchange(fulls, [SDS(f.shape, BF16) for f in fulls], {i: i for i in range(len(wis))}, 3 * len(wis), 0, build)


def _half_shape(wi):
    R, C = W_SHARD[wi]
    return (R // 2, N_SHARD * C) if W_KINDS[wi] == "col" else (N_SHARD, R // 2, C)


def _ex_pair(wis, grads):
    def build(ins, outs, send, recv, loc):
        x, y, c, _ = _mesh_pos()
        starts, waits = [], []
        for i, wi in enumerate(wis):
            Rh = W_SHARD[wi][0] // 2
            rows = pl.ds((1 - c) * Rh, Rh)
            if tuple(ins[i].shape) == _half_shape(wi):
                src = ins[i]
            else:
                src = ins[i].at[rows, :] if W_KINDS[wi] == "col" else ins[i].at[:, rows, :]
            cp = _remote(send, recv, i, src, outs[i], (x, y, 1 - c))
            starts.append(cp)
            waits.append(cp.wait)
        return starts, waits

    return _Exchange(grads, [SDS(_half_shape(wi), F32) for wi in wis], {}, len(wis), 0, build)


def _ex_chip(wis, pbs):
    def build(ins, outs, send, recv, loc):
        x, y, c, chips = _mesh_pos()
        starts, waits = [], []
        for i, wi in enumerate(wis):
            for j, chip in enumerate(chips):
                cp = _remote(send, recv, 3 * i + j, ins[i].at[j], outs[i].at[j], (*chip, c))
                starts.append(cp)
                waits.append(cp.wait)
        return starts, waits

    shapes = [SDS((3, W_SHARD[wi][0] // 2, W_SHARD[wi][1]), BF16) for wi in wis]
    return _Exchange(pbs, shapes, {}, 3 * len(wis), 0, build)


def _ex_share(wis, halves):
    def build(ins, outs, send, recv, loc):
        x, y, c, _ = _mesh_pos()
        sib = (x, y, 1 - c)
        starts, waits = [], []
        for i, wi in enumerate(wis):
            cp = _remote(send, recv, i, outs[i].at[c], outs[i].at[c], sib)
            starts.append(cp)
            waits += [cp.wait_send, _remote(send, recv, i, outs[i].at[1 - c], outs[i].at[1 - c], sib).wait_recv]
        return starts, waits

    return _Exchange(halves, [SDS(h.shape, F32) for h in halves], {i: i for i in range(len(wis))}, len(wis), 0, build)


def _row_tile(rh, C):
    best = 16
    for t in range(16, rh + 1, 16):
        if rh % t == 0 and t * C * 4 <= (3 << 19):
            best = t
    return best


def _pair_sum(wi, g, ra, sidx):
    R, C = W_SHARD[wi]
    Rh = R // 2
    tr = _row_tile(Rh, C)
    nt = Rh // tr
    off = 0 if tuple(g.shape) == _half_shape(wi) else nt
    col = W_KINDS[wi] == "col"

    def body(sidx_ref, *refs):
        gs, rs = refs[:4], refs[4:8]
        own_ref, pb_ref = refs[8:]
        own_ref[...] = gs[0][...] + rs[0][...]
        for j in range(3):
            pb_ref[j] = (gs[1 + j][...] + rs[1 + j][...]).astype(BF16)

    def gspec(slot):
        if col:
            return pl.BlockSpec((tr, C), lambda i, sx: (sx[4] * off + i, sx[slot]))
        return pl.BlockSpec((None, tr, C), lambda i, sx: (sx[slot], sx[4] * off + i, 0))

    def rspec(slot):
        if col:
            return pl.BlockSpec((tr, C), lambda i, sx: (i, sx[slot]))
        return pl.BlockSpec((None, tr, C), lambda i, sx: (sx[slot], i, 0))

    return pl.pallas_call(
        body, out_shape=(SDS((Rh, C), F32), SDS((3, Rh, C), BF16)),
        grid_spec=pltpu.PrefetchScalarGridSpec(
            num_scalar_prefetch=1, grid=(nt,),
            in_specs=[gspec(k) for k in range(4)] + [rspec(k) for k in range(4)],
            out_specs=(pl.BlockSpec((tr, C), lambda i, sx: (i, 0)), pl.BlockSpec((3, tr, C), lambda i, sx: (0, i, 0)))),
        compiler_params=_cparams("arbitrary"), name=f"pair_sum_w{wi}")(sidx, g, g, g, g, ra, ra, ra, ra)


def _chip_sum(wi, own, rb, sidx):
    R, C = W_SHARD[wi]
    Rh = R // 2
    tr = _row_tile(Rh, C)

    def body(sidx_ref, own_ref, rb_ref, o_ref):
        o_ref[...] = ((own_ref[...] + rb_ref[0].astype(F32)) + rb_ref[1].astype(F32)) + rb_ref[2].astype(F32)

    return pl.pallas_call(
        body, out_shape=SDS((2, Rh, C), F32),
        grid_spec=pltpu.PrefetchScalarGridSpec(
            num_scalar_prefetch=1, grid=(Rh // tr,),
            in_specs=[pl.BlockSpec((tr, C), lambda i, sx: (i, 0)), pl.BlockSpec((3, tr, C), lambda i, sx: (0, i, 0))],
            out_specs=pl.BlockSpec((None, tr, C), lambda i, sx: (sx[4], i, 0))),
        compiler_params=_cparams("arbitrary"), name=f"chip_sum_w{wi}")(sidx, own, rb)


def _gain_allgather(blk, ex):
    m_per, n = blk.shape
    n_in, n_out = len(ex.ins), len(ex.out_shapes)

    def body(x_ref, *rest):
        xin, out_ref, xout = rest[:n_in], rest[n_in], rest[n_in + 1:n_in + 1 + n_out]
        send_sems, recv_sems, local_sem = rest[n_in + 1 + n_out:n_in + 4 + n_out]
        ex_starts, ex_waits = ex.build(xin, xout, *rest[n_in + 4 + n_out:])
        for cp in ex_starts:
            cp.start()
        x, y, c, chips = _mesh_pos()
        me, sibling = (x, y, c), (x, y, 1 - c)

        def rows(px, py, pc):
            return out_ref.at[pl.ds((4 * px + 2 * py + pc) * m_per, m_per), :]

        def copy(k, block, to, src=None):
            return pltpu.make_async_remote_copy(
                src_ref=rows(*block) if src is None else src, dst_ref=rows(*block),
                send_sem=send_sems.at[k], recv_sem=recv_sems.at[k], device_id=to, device_id_type=MESH)

        mine = pltpu.make_async_copy(x_ref, rows(*me), local_sem)
        mine.start()
        first = [copy(0, me, sibling, src=x_ref)]
        first += [copy(1 + j, me, (*chip, c), src=x_ref) for j, chip in enumerate(chips)]
        for cp in first:
            cp.start()
        passed = [copy(4 + j, (*chip, c), sibling) for j, chip in enumerate(chips)]
        for j, chip in enumerate(chips):
            copy(1 + j, (*chip, c), me).wait_recv()
            passed[j].start()
        copy(0, sibling, me).wait_recv()
        for j, chip in enumerate(chips):
            copy(4 + j, (*chip, 1 - c), me).wait_recv()
        for cp in first + passed:
            cp.wait_send()
        mine.wait()
        for w in ex_waits:
            w()

    vm = pl.BlockSpec(memory_space=pltpu.VMEM)
    res = pl.pallas_call(
        body, out_shape=(SDS((8 * m_per, n), blk.dtype), *ex.out_shapes),
        in_specs=[vm] + [ANY] * n_in, out_specs=(vm, *[ANY] * n_out),
        input_output_aliases={1 + a: 1 + o for a, o in ex.aliases.items()},
        scratch_shapes=[pltpu.SemaphoreType.DMA((7,)), pltpu.SemaphoreType.DMA((7,)), pltpu.SemaphoreType.DMA] + ex.sems(),
        name="gain_allgather")(blk, *ex.ins)
    return res[0], tuple(res[1:])


def _adam_math(w, g, m, v):
    mn = ADAM_B1 * m + (1.0 - ADAM_B1) * g
    vn = ADAM_B2 * v + (1.0 - ADAM_B2) * (g * g)
    mh = mn / (1.0 - ADAM_B1 ** ADAM_STEP)
    vh = vn / (1.0 - ADAM_B2 ** ADAM_STEP)
    return -ADAM_LR * (mh / (jnp.sqrt(vh) + ADAM_EPS) + ADAM_WD * w), mn, vn


def _adamw(wi, w, g, m, v):
    R, C = w.shape
    tr = _row_tile(R, C)

    def body(w_ref, g_ref, m_ref, v_ref, go_ref, d_ref, mn_ref, vn_ref):
        g = g_ref[...]
        go_ref[...] = g
        d_ref[...], mn_ref[...], vn_ref[...] = _adam_math(w_ref[...], g, m_ref[...], v_ref[...])

    spec = pl.BlockSpec((tr, C), lambda i: (i, 0))
    return pl.pallas_call(body, out_shape=(SDS((R, C), F32),) * 4, grid=(R // tr,), in_specs=[spec] * 4,
                          out_specs=(spec,) * 4, compiler_params=_cparams("parallel"), name=f"adamw_w{wi}")(w, g, m, v)


def _gain_update(gathered, w, m, v):
    def body(ga_ref, w_ref, m_ref, v_ref, g_ref, d_ref, mn_ref, vn_ref):
        g = ga_ref[0:8, :]
        for dev in range(1, 8):
            g = g + ga_ref[8 * dev:8 * dev + 8, :]
        g_ref[...] = g
        d_ref[...], mn_ref[...], vn_ref[...] = _adam_math(w_ref[...], g, m_ref[...], v_ref[...])

    return pl.pallas_call(body, out_shape=(SDS((8, 1024), F32),) * 4, name="gain_update")(gathered, w, m, v)


GROUP_FFN, GROUP_MIX, GROUP_IN = (4, 5, 6), (1, 2, 3), (0,)
REST = GROUP_MIX + GROUP_FFN


class _MeshComm:
    SCHEDULE = {
        "rms_fwd": [("ring", GROUP_IN)],
        "in_proj": [("ici", (1, 2, 3, 4))],
        "ret_fwd": [("d2d", (1, 2, 3, 4)), ("ici", (5,))],
        "mix_out": [("d2d", (5,))],
        "ffn_up": [("both", (6,))],
        "mix_bwd": [("pair", GROUP_FFN)],
        "ret_bwd": [("pair", GROUP_MIX), ("chip", (4,))],
        "attn_bwd_g0": [("chip", (5,))],
        "attn_bwd_g1": [("chip", (6,))],
        "attn_bwd_g2": [("chip", GROUP_MIX)],
        "wgrad_in_kept": [("pair", GROUP_IN), ("share", GROUP_FFN + GROUP_MIX)],
        "in_proj_bwd": [("chip", GROUP_IN)],
    }

    def __init__(self, w_in_shard, rest_f32):
        xi, yi, ci = lax.axis_index("x"), lax.axis_index("y"), lax.axis_index("c")
        self.sidx = jnp.stack([2 * xi + yi, 2 * (1 - xi) + yi, 2 * xi + (1 - yi), 2 * (1 - xi) + (1 - yi), ci]).astype(jnp.int32)
        self.shards, self.rest_f32, self.full = {0: w_in_shard}, list(rest_f32), {}
        self.g, self.own, self.pb, self.half, self.red = {}, {}, {}, {}, {}

    def to_cast(self):
        return self.rest_f32

    def cast_done(self, casts):
        self.shards.update(zip(REST, casts))

    def weight(self, wi):
        return self.full[wi].reshape(D_MODEL, D_MODEL) if wi in (2, 3) else self.full[wi]

    def grads(self, by_wi):
        self.g.update(by_wi)

    def _exchange(self, stage, wis):
        pick = lambda table: [table[wi] for wi in wis]
        if stage == "ring":
            return _ex_gather_ring(wis, pick(self.shards))
        if stage == "ici":
            return _ex_gather_ici(wis, pick(self.shards))
        if stage == "both":
            return _ex_gather_ici(wis, pick(self.shards), then_d2d=True)
        if stage == "d2d":
            return _ex_gather_d2d(wis, pick(self.full))
        if stage == "pair":
            return _ex_pair(wis, [self.g["in_sent"] if wi == 0 else self.g[wi] for wi in wis])
        if stage == "chip":
            return _ex_chip(wis, pick(self.pb))
        return _ex_share(wis, pick(self.half))

    def _landed(self, stage, wis, res):
        for wi, r in zip(wis, res):
            if stage in ("ring", "ici", "d2d", "both"):
                self.full[wi] = r
            elif stage == "pair":
                self.own[wi], self.pb[wi] = _pair_sum(wi, self.g["in_kept"] if wi == 0 else self.g[wi], r, self.sidx)
            elif stage == "chip":
                self.half[wi] = _chip_sum(wi, self.own[wi], r, self.sidx)
            else:
                self.red[wi] = r

    def carry(self, point):
        return [self._exchange(stage, wis) for stage, wis in self.SCHEDULE.get(point, ())]

    def took(self, point, xres):
        for (stage, wis), res in zip(self.SCHEDULE.get(point, ()), xres):
            self._landed(stage, wis, res)

    def last_share(self):
        return self._exchange("share", GROUP_IN)

    def reduced(self, last_shared):
        self._landed("share", GROUP_IN, last_shared)
        return [self.red[wi] for wi in range(N_W)]


def kernel(x, norm_mix_g, w_in, w_out_attn, w_out_ret, w_out, norm_ffn_g, w_ffn_gate, w_ffn_up, w_ffn_down, norm_final_g, loss_target, m_norm_mix_g, m_w_in, m_w_out_attn, m_w_out_ret, m_w_out, m_norm_ffn_g, m_w_ffn_gate, m_w_ffn_up, m_w_ffn_down, m_norm_final_g, v_norm_mix_g, v_w_in, v_w_out_attn, v_w_out_ret, v_w_out, v_norm_ffn_g, v_w_ffn_gate, v_w_ffn_up, v_w_ffn_down, v_norm_final_g):
    ws = (w_in, w_out_attn, w_out_ret, w_out, w_ffn_gate, w_ffn_up, w_ffn_down)
    ms = (m_w_in, m_w_out_attn, m_w_out_ret, m_w_out, m_w_ffn_gate, m_w_ffn_up, m_w_ffn_down)
    vs = (v_w_in, v_w_out_attn, v_w_out_ret, v_w_out, v_w_ffn_gate, v_w_ffn_up, v_w_ffn_down)
    shard2d = lambda a, wi: a.reshape(W_SHARD[wi])

    comm = _MeshComm(_cast_bf16(shard2d(ws[0], 0)), [shard2d(ws[wi], wi) for wi in REST])
    g3 = norm_final_g.reshape(1, D_MODEL)
    loss_p, grad_x, gain_g = _step(x[0], loss_target[0], norm_mix_g, norm_ffn_g, g3, comm)

    pad8 = lambda rows: jnp.concatenate([r.reshape(1, D_MODEL) for r in rows]
                                        + [jnp.zeros((8 - len(rows), D_MODEL), F32)], axis=0)
    gathered, shared = _gain_allgather(pad8((*gain_g, jnp.tile(loss_p[0:1], (1, D_MODEL // 128)))), comm.last_share())
    gred = comm.reduced(shared)

    outs_g, outs_d, outs_m, outs_v = [], [], [], []
    for wi in range(N_W):
        g2d = gred[wi].reshape(W_SHARD[wi])
        gout, dlt, mn, vn = _adamw(wi, shard2d(ws[wi], wi), g2d, shard2d(ms[wi], wi), shard2d(vs[wi], wi))
        for lst, a in ((outs_g, gout), (outs_d, dlt), (outs_m, mn), (outs_v, vn)):
            lst.append(a.reshape(ws[wi].shape))

    gg, gd, gm, gv = _gain_update(gathered, pad8((norm_mix_g, norm_ffn_g, norm_final_g)),
                                  pad8((m_norm_mix_g, m_norm_ffn_g, m_norm_final_g)),
                                  pad8((v_norm_mix_g, v_norm_ffn_g, v_norm_final_g)))
    loss = gg[3, 0]

    def assemble(gain_rows, wlist):
        return (gain_rows[0:1], wlist[0], wlist[1], wlist[2], wlist[3], gain_rows[1:2],
                wlist[4], wlist[5], wlist[6], gain_rows[2])

    return (loss, grad_x[None], *assemble(gg, outs_g), *assemble(gd, outs_d), *assemble(gm, outs_m), *assemble(gv, outs_v))
```

```python
import functools
import math

import numpy as np
import jax
import jax.numpy as jnp
from jax import lax
from jax.experimental import pallas as pl
from jax.experimental.pallas import tpu as pltpu

F32, BF16 = jnp.float32, jnp.bfloat16
SDS = jax.ShapeDtypeStruct
MESH = pl.DeviceIdType.MESH

D_MODEL = 1024
PROJ_W = 9728
COLB = 512
N_COLB = PROJ_W // COLB
QA_B, KA_B, VA_B = 0, 3, 6
QR_B, KR_B = 9, 10
FFN_HID = 2816
N_SHARD = 4
HID_S = FFN_HID // N_SHARD
W_IN_S = PROJ_W // N_SHARD
DILATIONS = (1, 4, 16)
BLK = 128
RET_HEADS = 4
ROPE_THETA = 10000.0
NORM_EPS = 1e-6
ADAM_LR, ADAM_B1, ADAM_B2, ADAM_EPS, ADAM_WD, ADAM_STEP = 0.001, 0.9, 0.999, 1e-08, 0.01, 10
VMEM_LIMIT = 56 << 20


def _cparams(*sem):
    return pltpu.CompilerParams(dimension_semantics=sem or None, vmem_limit_bytes=VMEM_LIMIT)


def _dot(a, b):
    return jnp.dot(a, b, preferred_element_type=F32)


def _dot_nt(a, b):
    return lax.dot_general(a, b, (((1,), (1,)), ((), ())), preferred_element_type=F32)


def _dot_tn(a, b):
    return lax.dot_general(a, b, (((0,), (0,)), ((), ())), preferred_element_type=F32)


def _row_pieces(tm, sub=512):
    return [slice(i, i + sub) for i in range(0, tm, sub)]


def _sigmoid(z):
    return 0.5 * jnp.tanh(0.5 * z) + 0.5


ANY = pl.BlockSpec(memory_space=pl.ANY)


class _Exchange:
    def __init__(self, ins, out_shapes, aliases, n_sem, n_loc, build):
        self.ins, self.out_shapes, self.aliases = list(ins), list(out_shapes), dict(aliases)
        self.n_sem, self.n_loc, self.build = n_sem, n_loc, build

    def sems(self):
        return [pltpu.SemaphoreType.DMA((self.n_sem,)), pltpu.SemaphoreType.DMA((self.n_sem,)),
                pltpu.SemaphoreType.DMA((max(self.n_loc, 1),))]


def _carrier_call(body, args, *, out_shape, grid, in_specs, out_specs, scratch_shapes=(), sem, name, exchanges=(),
                  prefetch=None, in_out_aliases=None):
    out_shape, out_specs = tuple(out_shape), tuple(out_specs)
    n_in, n_out, n_scr = len(args), len(out_shape), len(scratch_shapes)
    n_pre = 0 if prefetch is None else 1
    x_args, x_outs, x_scr, spans = [], [], [], []
    aliases = {n_pre + a: o for a, o in (in_out_aliases or {}).items()}
    for ex in exchanges:
        i0, o0 = len(x_args), len(x_outs)
        for a, o in ex.aliases.items():
            aliases[n_pre + n_in + i0 + a] = n_out + o0 + o
        x_args += ex.ins
        x_outs += ex.out_shapes
        x_scr += ex.sems()
        spans.append((i0, len(ex.ins), o0, len(ex.out_shapes)))
    nx_in, nx_out = len(x_args), len(x_outs)

    def wrapped(*refs):
        refs = refs[n_pre:]
        ins, xin = refs[:n_in], refs[n_in:n_in + nx_in]
        o_base = n_in + nx_in
        outs, xout = refs[o_base:o_base + n_out], refs[o_base + n_out:o_base + n_out + nx_out]
        s_base = o_base + n_out + nx_out
        scr, xs = refs[s_base:s_base + n_scr], refs[s_base + n_scr:]

        def built(e):
            i0, ni, o0, no = spans[e]
            return exchanges[e].build(xin[i0:i0 + ni], xout[o0:o0 + no], *xs[3 * e:3 * e + 3])

        if exchanges:
            first = functools.reduce(jnp.logical_and, [pl.program_id(k) == 0 for k in range(len(grid))])
            last = functools.reduce(jnp.logical_and, [pl.program_id(k) == grid[k] - 1 for k in range(len(grid))])

            @pl.when(first)
            def _():
                for e in range(len(exchanges)):
                    for cp in built(e)[0]:
                        cp.start()

        body(*ins, *outs, *scr)

        if exchanges:
            @pl.when(last)
            def _():
                for e in range(len(exchanges)):
                    for w in built(e)[1]:
                        w()

    all_in, all_out = list(in_specs) + [ANY] * nx_in, out_specs + tuple([ANY] * nx_out)
    all_scr = list(scratch_shapes) + x_scr
    cparams = _cparams(*(sem if not exchanges else ("arbitrary",) * len(grid)))
    if prefetch is None:
        res = pl.pallas_call(wrapped, out_shape=out_shape + tuple(x_outs), grid=grid, in_specs=all_in, out_specs=all_out,
                             scratch_shapes=all_scr, input_output_aliases=aliases, compiler_params=cparams,
                             name=name)(*args, *x_args)
    else:
        gs = pltpu.PrefetchScalarGridSpec(num_scalar_prefetch=1, grid=grid, in_specs=all_in, out_specs=all_out,
                                          scratch_shapes=all_scr)
        res = pl.pallas_call(wrapped, out_shape=out_shape + tuple(x_outs), grid_spec=gs, input_output_aliases=aliases,
                             compiler_params=cparams, name=name)(prefetch, *args, *x_args)
    xres = [tuple(res[n_out + o0:n_out + o0 + no]) for (_, _, o0, no) in spans]
    return tuple(res[:n_out]), xres


def _tables(S):
    f32 = np.float32
    pos = np.arange(S, dtype=f32)
    lane = np.arange(128)
    inv = (f32(ROPE_THETA) ** (-np.arange(0, 64, 2, dtype=f32) / f32(64))).astype(f32)
    ang = (pos[:, None] * inv[None, :]).astype(np.float64)
    idx = (lane % 64) % 32
    c, s = np.cos(ang)[:, idx], np.sin(ang)[:, idx]
    first = ((lane % 64) < 32)[None, :]
    rope = np.stack([c, np.where(first, 0.0, s), np.where(first, -s, 0.0)])
    base = (f32(1.0) / (f32(ROPE_THETA) ** np.linspace(0.0, 1.0, 64, dtype=f32))).astype(f32)
    ang2 = (pos[:, None] * base[None, :]).astype(np.float64)
    c2, s2 = np.cos(ang2)[:, lane // 2], np.sin(ang2)[:, lane // 2]
    even = (lane % 2 == 0)[None, :]
    th = np.stack([c2, np.where(even, 0.0, s2), np.where(even, -s2, 0.0)])
    return np.stack([rope, th, th * (128 ** -0.5)]).astype(f32)


def _rot(a, c, sa, sb, shift):
    return a * c + pltpu.roll(a, shift, 1) * sa + pltpu.roll(a, 128 - shift, 1) * sb


def _unrot(g, c, sa, sb, shift):
    return g * c + pltpu.roll(g * sa, 128 - shift, 1) + pltpu.roll(g * sb, shift, 1)


def _ret_consts():
    h = np.arange(RET_HEADS, dtype=np.float64)
    log_g = np.log1p(-(2.0 ** (-5.0 - h)))
    idx = np.arange(BLK, dtype=np.float64)
    diff = idx[:, None] - idx[None, :]
    dmask = np.where(diff[None] >= 0, np.exp(np.maximum(diff, 0.0)[None] * log_g[:, None, None]), 0.0)
    zeta = np.exp((BLK - 1 - idx)[None, :] * log_g[:, None])
    xi = np.exp((idx + 1.0)[None, :] * log_g[:, None])
    dec = np.exp(BLK * log_g)
    rep = lambda v: np.broadcast_to(v[:, :, None], (RET_HEADS, BLK, 128))
    return (jnp.asarray(dmask, F32), jnp.asarray(rep(zeta), F32), jnp.asarray(rep(xi), F32),
            jnp.asarray(np.broadcast_to(dec[:, None, None], (RET_HEADS, 8, 256)), F32))


def _rms_fwd(x, g, to_cast=(), exchanges=()):
    S = x.shape[0]
    steps = 4
    tm = S // steps
    n_c = len(to_cast)

    def body(x_ref, g_ref, *refs):
        c_in, (h_ref, ht_ref), c_out = refs[:n_c], refs[n_c:n_c + 2], refs[n_c + 2:]
        for rows in _row_pieces(tm, 512):
            xv = x_ref[rows, :]
            r = lax.rsqrt(jnp.mean(xv * xv, axis=-1, keepdims=True) + NORM_EPS)
            h = xv * r * g_ref[...]
            h_ref[rows, :] = h.astype(BF16)
            ht_ref[:, rows] = h.T.astype(BF16)
        for a_ref, o_ref in zip(c_in, c_out):
            o_ref[...] = a_ref[...].astype(BF16)

    slab = lambda a: pl.BlockSpec((a.shape[0] // steps, a.shape[1]), lambda i: (i, 0))
    return _carrier_call(
        body, (x, g, *to_cast),
        out_shape=(SDS((S, D_MODEL), BF16), SDS((D_MODEL, S), BF16), *[SDS(a.shape, BF16) for a in to_cast]),
        grid=(steps,),
        in_specs=[pl.BlockSpec((tm, D_MODEL), lambda i: (i, 0)), pl.BlockSpec((1, D_MODEL), lambda i: (0, 0))]
        + [slab(a) for a in to_cast],
        out_specs=(pl.BlockSpec((tm, D_MODEL), lambda i: (i, 0)), pl.BlockSpec((D_MODEL, tm), lambda i: (0, i)),
                   *[slab(a) for a in to_cast]),
        sem=("parallel",), name="rms_fwd", exchanges=exchanges)


def _in_proj(h, w_in, tab, exchanges=()):
    S = h.shape[0]
    tm = min(S, 4096)

    def body(h_ref, w_ref, t_ref, o_ref):
        j = pl.program_id(1)
        is_rope = j < 6
        is_theta = (j == QR_B) | (j == KR_B)
        sub = 512

        def rotated(shift):
            for i in range(tm // sub):
                rows = slice(i * sub, (i + 1) * sub)
                acc = _dot(h_ref[rows, :], w_ref[...])
                c, sa, sb = t_ref[0, 0, rows, :], t_ref[0, 1, rows, :], t_ref[0, 2, rows, :]
                for k in range(COLB // 128):
                    sl = slice(k * 128, (k + 1) * 128)
                    o_ref[rows, sl] = _rot(acc[:, sl], c, sa, sb, shift).astype(BF16)

        @pl.when(is_rope)
        def _():
            rotated(32)

        @pl.when(is_theta)
        def _():
            rotated(1)

        @pl.when(jnp.logical_not(is_rope | is_theta))
        def _():
            o_ref[...] = _dot(h_ref[...], w_ref[...]).astype(BF16)

    def tab_map(i, j):
        return (jnp.where(j == QR_B, 1, jnp.where(j == KR_B, 2, 0)), 0, i, 0)

    (proj,), xres = _carrier_call(
        body, (h, w_in, tab), out_shape=(SDS((S, PROJ_W), BF16),), grid=(S // tm, N_COLB),
        in_specs=[pl.BlockSpec((tm, D_MODEL), lambda i, j: (i, 0)),
                  pl.BlockSpec((D_MODEL, COLB), lambda i, j: (0, j)),
                  pl.BlockSpec((1, 3, tm, 128), tab_map)],
        out_specs=(pl.BlockSpec((tm, COLB), lambda i, j: (i, j)),),
        sem=("parallel", "arbitrary"), name="in_proj", exchanges=exchanges)
    return proj, xres


def _band_mask(n):
    qi = lax.broadcasted_iota(jnp.int32, (BLK, 2 * BLK), 0)
    kj = lax.broadcasted_iota(jnp.int32, (BLK, 2 * BLK), 1)
    dist = BLK + qi - kj
    return (dist >= 0) & (dist <= BLK) & ((kj >= BLK) | (n > 0))


def _qkv_col(d, gi):
    if d == 1:
        return lambda t, r: 3 * t + gi
    return lambda t, r: 3 * r + t


def _attn_fwd(qkv, d, gi, exchanges=()):
    L = qkv.shape[0]
    nb = L // BLK

    def body(q_ref, kc_ref, kp_ref, vc_ref, vp_ref, o_ref, lse_ref):
        n = pl.program_id(1)
        mask = _band_mask(n)
        mask2 = jnp.concatenate([mask, mask], axis=0)
        lane = lax.broadcasted_iota(jnp.int32, (BLK, 128), 1)
        lo = lane < 64
        lse_all = jnp.zeros((BLK, 128), F32)
        chunks = [slice(c * 128, (c + 1) * 128) for c in range(4)]
        scores, vals = [], []
        for sl in chunks:
            q = q_ref[:, sl]
            k = jnp.concatenate([kp_ref[:, sl], kc_ref[:, sl]], axis=0)
            vals.append(jnp.concatenate([vp_ref[:, sl], vc_ref[:, sl]], axis=0))
            q2 = jnp.concatenate([jnp.where(lo, q, jnp.zeros_like(q)), jnp.where(lo, jnp.zeros_like(q), q)], axis=0)
            scores.append(_dot_nt(q2, k))
        probs = []
        for c, s in enumerate(scores):
            s = jnp.where(mask2, s * 0.125, jnp.float32(-1e30))
            m = jnp.max(s, axis=-1, keepdims=True)
            p = jnp.exp(s - m)
            l = jnp.sum(p, axis=-1, keepdims=True)
            probs.append((p / l).astype(BF16))
            lse = m + jnp.log(l)
            lse_all = jnp.where(lane // 16 == 2 * c, lse[:BLK], jnp.where(lane // 16 == 2 * c + 1, lse[BLK:], lse_all))
        for sl, p, v in zip(chunks, probs, vals):
            o2 = _dot(p, v)
            o_ref[:, sl] = jnp.where(lo, o2[:BLK], o2[BLK:])
        lse_ref[...] = lse_all

    prev = lambda n: jnp.maximum(n - 1, 0)
    col = _qkv_col(d, gi)
    return _carrier_call(
        body, (qkv,) * 5, out_shape=(SDS((L, d * 512), F32), SDS((L, d * 128), F32)), grid=(d, nb),
        in_specs=[pl.BlockSpec((BLK, 512), lambda r, n: (n, col(0, r))),
                  pl.BlockSpec((BLK, 512), lambda r, n: (n, col(1, r))),
                  pl.BlockSpec((BLK, 512), lambda r, n: (prev(n), col(1, r))),
                  pl.BlockSpec((BLK, 512), lambda r, n: (n, col(2, r))),
                  pl.BlockSpec((BLK, 512), lambda r, n: (prev(n), col(2, r)))],
        out_specs=(pl.BlockSpec((BLK, 512), lambda r, n: (n, r)),
                   pl.BlockSpec((BLK, 128), lambda r, n: (n, r))),
        sem=("parallel", "arbitrary"), name=f"attn_fwd_g{gi}", exchanges=exchanges)


def _qkv_to_sub(proj, d, gi):
    S = proj.shape[0]
    tm = 512
    n = tm // d

    def body(q_ref, k_ref, v_ref, o_ref, scr):
        for t, ref in enumerate((q_ref, k_ref, v_ref)):
            for c in range(4):
                scr[c] = ref[:, c * 128:(c + 1) * 128].astype(F32)
            for r in range(d):
                for c in range(4):
                    col = (3 * r + t) * 512 + c * 128
                    o_ref[:, col:col + 128] = scr[c, pl.ds(r, n, stride=d), :].astype(BF16)

    return pl.pallas_call(
        body, out_shape=SDS((S // d, d * 1536), BF16), grid=(S // tm,),
        in_specs=[pl.BlockSpec((tm, 512), lambda i, b=b: (i, b + gi)) for b in (QA_B, KA_B, VA_B)],
        out_specs=pl.BlockSpec((n, d * 1536), lambda i: (i, 0)),
        scratch_shapes=[pltpu.VMEM((4, tm, 128), F32)],
        compiler_params=_cparams("parallel"), name=f"qkv_to_sub_g{gi}")(proj, proj, proj)


def _attn_merge(os_, lses):
    S = os_[0].shape[0]
    tm = 512

    def body(o0, o1, o2, l0, l1, l2, att_ref, lt_ref, so1, so2, sl1, sl2):
        lo = lax.broadcasted_iota(jnp.int32, (tm, 128), 1) < 64

        def natural(ref, d, scr, width):
            nch = width // 128
            if d == 1:
                return [ref[:, c * 128:(c + 1) * 128] for c in range(nch)]
            for r in range(d):
                for c in range(nch):
                    scr[c, pl.ds(r, tm // d, stride=d), :] = ref[:, r * width + c * 128:r * width + (c + 1) * 128]
            return [scr[c] for c in range(nch)]

        ls = [natural(l, d, s, 128)[0] for l, d, s in zip((l0, l1, l2), DILATIONS, (None, sl1, sl2))]
        m = jnp.maximum(jnp.maximum(ls[0], ls[1]), ls[2])
        es = [jnp.exp(v - m) for v in ls]
        z = es[0] + es[1] + es[2]
        lt_ref[...] = m + jnp.log(z)
        ws = [e / z for e in es]
        o_nat = [natural(o, d, s, 512) for o, d, s in zip((o0, o1, o2), DILATIONS, (None, so1, so2))]
        for c in range(4):
            acc = jnp.zeros((tm, 128), F32)
            for g in range(3):
                w_lo = jnp.broadcast_to(ws[g][:, 32 * c:32 * c + 1], (tm, 128))
                w_hi = jnp.broadcast_to(ws[g][:, 32 * c + 16:32 * c + 17], (tm, 128))
                acc = acc + jnp.where(lo, w_lo, w_hi) * o_nat[g][c]
            att_ref[:, c * 128:(c + 1) * 128] = acc.astype(BF16)

    sub = lambda w: [pl.BlockSpec((tm // d, d * w), lambda i: (i, 0)) for d in DILATIONS]
    return pl.pallas_call(
        body, out_shape=(SDS((S, 512), BF16), SDS((S, 128), F32)), grid=(S // tm,),
        in_specs=sub(512) + sub(128),
        out_specs=(pl.BlockSpec((tm, 512), lambda i: (i, 0)), pl.BlockSpec((tm, 128), lambda i: (i, 0))),
        scratch_shapes=[pltpu.VMEM((4, tm, 128), F32), pltpu.VMEM((4, tm, 128), F32),
                        pltpu.VMEM((1, tm, 128), F32), pltpu.VMEM((1, tm, 128), F32)],
        compiler_params=_cparams("parallel"), name="attn_merge")(*os_, *lses)


def _assemble_dproj(att_grads, dproj):
    S = dproj.shape[0]
    tm = 256

    def body(*refs):
        a = [refs[3 * t:3 * t + 3] for t in range(3)]
        dp_prev, o_ref, scr = refs[9:]
        for t in range(3):
            for g, d in enumerate(DILATIONS):
                base = (3 * t + g) * COLB
                if d == 1:
                    o_ref[:, base:base + COLB] = a[t][g][...]
                    continue
                for c in range(4):
                    for r in range(d):
                        scr[c, pl.ds(r, tm // d, stride=d), :] = a[t][g][:, r * 512 + c * 128:r * 512 + (c + 1) * 128].astype(F32)
                    o_ref[:, base + c * 128:base + (c + 1) * 128] = scr[c].astype(BF16)

    sub = [pl.BlockSpec((tm // d, d * 512), lambda i: (i, 0)) for d in DILATIONS]
    flat = [att_grads[t][g] for t in range(3) for g in range(3)]
    return pl.pallas_call(
        body, out_shape=SDS((S, PROJ_W), BF16), grid=(S // tm,),
        in_specs=sub * 3 + [ANY], out_specs=pl.BlockSpec((tm, 9 * COLB), lambda i: (i, 0)),
        scratch_shapes=[pltpu.VMEM((4, tm, 128), F32)], input_output_aliases={9: 0},
        compiler_params=_cparams("parallel"), name="assemble_dproj")(*flat, dproj)


def _ret_fwd(proj, consts, exchanges=()):
    S = proj.shape[0]
    nc = S // BLK
    dmask, zeta, xi, dec = consts

    def body(q_ref, k_ref, v0_ref, v1_ref, g0_ref, g1_ref, dm_ref, z_ref, x_ref, dec_ref,
             y_ref, rn_ref, rs_ref, st_ref, R):
        @pl.when(pl.program_id(0) == 0)
        def _():
            R[...] = jnp.zeros_like(R)

        lane16 = lax.broadcasted_iota(jnp.int32, (BLK, 128), 1) // 16
        rs_all = jnp.zeros((BLK, 128), F32)
        first = []
        for h in range(RET_HEADS):
            hs = slice(h * 128, (h + 1) * 128)
            q, k = q_ref[:, hs], k_ref[:, hs]
            v = (v0_ref if h < 2 else v1_ref)[:, (h % 2) * 256:(h % 2 + 1) * 256]
            Rb = R[h].astype(BF16)
            st_ref[h] = Rb
            kz = (k.astype(F32) * z_ref[h]).astype(BF16)
            first.append((v, _dot_nt(q, k), _dot((q.astype(F32) * x_ref[h]).astype(BF16), Rb), _dot_tn(kz, v)))
        masked = [(s * dm_ref[h]).astype(BF16) for h, (_, s, _, _) in enumerate(first)]
        for h in range(RET_HEADS):
            vs = slice((h % 2) * 256, (h % 2 + 1) * 256)
            os_ = slice(h * 256, (h + 1) * 256)
            v, _, cross, kv = first[h]
            o = _dot(masked[h], v) + cross
            R[h] = R[h] * dec_ref[h, 0:1, :] + kv
            mu = jnp.mean(o, axis=-1, keepdims=True)
            oc = o - mu
            rstd = lax.rsqrt(jnp.mean(oc * oc, axis=-1, keepdims=True) + NORM_EPS)
            rn = oc * rstd
            gr = (g0_ref if h < 2 else g1_ref)[:, vs].astype(F32)
            y_ref[:, os_] = (rn * gr * _sigmoid(gr)).astype(BF16)
            rn_ref[:, os_] = rn.astype(BF16)
            rs_all = jnp.where(lane16 == h, rstd, rs_all)
        rs_ref[...] = rs_all

    cst = lambda shape: pl.BlockSpec(shape, lambda c: (0, 0, 0))
    blk = lambda j: pl.BlockSpec((BLK, 512), lambda c: (c, j))
    return _carrier_call(
        body, (proj, proj, proj, proj, proj, proj, dmask, zeta, xi, dec),
        out_shape=(SDS((S, 1024), BF16), SDS((S, 1024), BF16), SDS((S, 128), F32), SDS((RET_HEADS, nc, BLK, 256), BF16)),
        grid=(nc,),
        in_specs=[blk(QR_B), blk(KR_B), blk(11), blk(12), blk(13), blk(14),
                  cst((RET_HEADS, BLK, BLK)), cst((RET_HEADS, BLK, 128)), cst((RET_HEADS, BLK, 128)), cst((RET_HEADS, 8, 256))],
        out_specs=(pl.BlockSpec((BLK, 1024), lambda c: (c, 0)), pl.BlockSpec((BLK, 1024), lambda c: (c, 0)),
                   pl.BlockSpec((BLK, 128), lambda c: (c, 0)),
                   pl.BlockSpec((RET_HEADS, None, BLK, 256), lambda c: (0, c, 0, 0))),
        scratch_shapes=[pltpu.VMEM((RET_HEADS, BLK, 256), F32)],
        sem=("arbitrary",), name="ret_fwd", exchanges=exchanges)


def _mix_out(att, yrin, proj, wa, wr, wo, x, g2, exchanges=()):
    S = x.shape[0]
    tm = 512
    gate0 = 15 * COLB

    def body(a_ref, y_ref, ga_ref, gr_ref, wa_ref, wr_ref, wo_ref, x_ref, g_ref, m_ref, ya_ref, yr_ref, x1_ref, h2_ref):
        pieces = _row_pieces(tm, 256)
        branches = [(_dot(a_ref[rows, :], wa_ref[...]), _dot(y_ref[rows, :], wr_ref[...])) for rows in pieces]
        merged = []
        for rows, (ya, yr) in zip(pieces, branches):
            m = (_sigmoid(ga_ref[rows, :].astype(F32)) * ya + _sigmoid(gr_ref[rows, :].astype(F32)) * yr).astype(BF16)
            m_ref[rows, :] = m
            ya_ref[rows, :] = ya.astype(BF16)
            yr_ref[rows, :] = yr.astype(BF16)
            merged.append(m)
        for rows, m in zip(pieces, merged):
            x1 = x_ref[rows, :] + _dot(m, wo_ref[...])
            x1_ref[rows, :] = x1
            r = lax.rsqrt(jnp.mean(x1 * x1, axis=-1, keepdims=True) + NORM_EPS)
            h2_ref[rows, :] = (x1 * r * g_ref[...]).astype(BF16)

    row = lambda w: pl.BlockSpec((tm, w), lambda i: (i, 0))
    cols = lambda c0: pl.BlockSpec((pl.Element(tm), pl.Element(D_MODEL)), lambda i: (i * tm, c0))
    resident = lambda r, c: pl.BlockSpec((r, c), lambda i: (0, 0), pipeline_mode=pl.Buffered(1))
    return _carrier_call(
        body, (att, yrin, proj, proj, wa, wr, wo, x, g2),
        out_shape=(SDS((S, D_MODEL), BF16),) * 3 + (SDS((S, D_MODEL), F32), SDS((S, D_MODEL), BF16)), grid=(S // tm,),
        in_specs=[row(512), row(D_MODEL), cols(gate0), cols(gate0 + D_MODEL), resident(512, D_MODEL),
                  resident(D_MODEL, D_MODEL), resident(D_MODEL, D_MODEL), row(D_MODEL),
                  pl.BlockSpec((1, D_MODEL), lambda i: (0, 0))],
        out_specs=(row(D_MODEL),) * 5, sem=("parallel",), name="mix_out", exchanges=exchanges)


def _ffn_fwd(h2, wg, wu, wd, x1, g3, tgt):
    S = x1.shape[0]
    tm = 256

    def body(h_ref, wg_ref, wu_ref, wd_ref, x_ref, g_ref, t_ref,
             gt_ref, up_ref, a_ref, dx_ref, dxb_ref, dg_ref, ls_ref):
        @pl.when(pl.program_id(0) == 0)
        def _():
            dg_ref[...] = jnp.zeros_like(dg_ref)
            ls_ref[...] = jnp.zeros_like(ls_ref)

        hv = h_ref[...]
        pre = [(_dot(hv, wg_ref[s]), _dot(hv, wu_ref[s])) for s in range(N_SHARD)]
        acts = []
        for s, (g, u) in enumerate(pre):
            a = (g * _sigmoid(g) * u).astype(BF16)
            gt_ref[s] = g.astype(BF16)
            up_ref[s] = u.astype(BF16)
            a_ref[s] = a
            acts.append(a)
        y = _dot(acts[0], wd_ref[0])
        for s in range(1, N_SHARD):
            y = y + _dot(acts[s], wd_ref[s])
        gn = g_ref[...]
        x2 = x_ref[...] + y
        r = lax.rsqrt(jnp.mean(x2 * x2, axis=-1, keepdims=True) + NORM_EPS)
        xh = x2 * r
        err = xh * gn - t_ref[...]
        ls_ref[...] += jnp.sum(jnp.sum(err * err, axis=-1, keepdims=True), axis=0, keepdims=True) * (0.5 / D_MODEL)
        dy = err * (1.0 / D_MODEL)
        dg_ref[...] += jnp.sum(dy * xh, axis=0, keepdims=True)
        dxh = dy * gn
        dx = r * (dxh - xh * jnp.mean(dxh * xh, axis=-1, keepdims=True))
        dx_ref[...] = dx
        dxb_ref[...] = dx.astype(BF16)

    row = pl.BlockSpec((tm, D_MODEL), lambda i: (i, 0))
    vec = pl.BlockSpec((1, D_MODEL), lambda i: (0, 0))
    aspec = pl.BlockSpec((N_SHARD, tm, HID_S), lambda i: (0, i, 0))
    resident = lambda shape: pl.BlockSpec(shape, lambda i: (0, 0, 0), pipeline_mode=pl.Buffered(1))
    return pl.pallas_call(
        body,
        out_shape=(SDS((N_SHARD, S, HID_S), BF16),) * 3
        + (SDS((S, D_MODEL), F32), SDS((S, D_MODEL), BF16), SDS((1, D_MODEL), F32), SDS((8, 128), F32)),
        grid=(S // tm,),
        in_specs=[row, resident((N_SHARD, D_MODEL, HID_S)), resident((N_SHARD, D_MODEL, HID_S)),
                  resident((N_SHARD, HID_S, D_MODEL)), row, vec, row],
        out_specs=(aspec, aspec, aspec, row, row, vec, pl.BlockSpec((8, 128), lambda i: (0, 0))),
        compiler_params=_cparams("arbitrary"), name="ffn_fwd")(h2, wg, wu, wd, x1, g3, tgt)


def _ffn_bwd(dx2b, dx2, wd, wg, wu, gte, up, x1, g2):
    S = x1.shape[0]
    tm = 256

    def body(d_ref, dx2_ref, wd_ref, wg_ref, wu_ref, g_ref, u_ref, x_ref, gn_ref,
             dg_ref, du_ref, dx_ref, dxb_ref, dgn_ref):
        @pl.when(pl.program_id(0) == 0)
        def _():
            dgn_ref[...] = jnp.zeros_like(dgn_ref)

        d = d_ref[...]
        dacts = [_dot_nt(d, wd_ref[s]) for s in range(N_SHARD)]
        dgs, dus = [], []
        for s, da in enumerate(dacts):
            g = g_ref[s].astype(F32)
            sg = _sigmoid(g)
            dgs.append((da * u_ref[s].astype(F32) * sg * (1.0 + g * (1.0 - sg))).astype(BF16))
            dus.append((da * g * sg).astype(BF16))
            dg_ref[s] = dgs[s]
            du_ref[s] = dus[s]
        dh = _dot_nt(dgs[0], wg_ref[0]) + _dot_nt(dus[0], wu_ref[0])
        for s in range(1, N_SHARD):
            dh = dh + _dot_nt(dgs[s], wg_ref[s]) + _dot_nt(dus[s], wu_ref[s])
        xv = x_ref[...]
        r = lax.rsqrt(jnp.mean(xv * xv, axis=-1, keepdims=True) + NORM_EPS)
        xh = xv * r
        dgn_ref[...] += jnp.sum(dh * xh, axis=0, keepdims=True)
        dxh = dh * gn_ref[...]
        dx = dx2_ref[...] + r * (dxh - xh * jnp.mean(dxh * xh, axis=-1, keepdims=True))
        dx_ref[...] = dx
        dxb_ref[...] = dx.astype(BF16)

    row = pl.BlockSpec((tm, D_MODEL), lambda i: (i, 0))
    vec = pl.BlockSpec((1, D_MODEL), lambda i: (0, 0))
    aspec = pl.BlockSpec((N_SHARD, tm, HID_S), lambda i: (0, i, 0))
    resident = lambda shape: pl.BlockSpec(shape, lambda i: (0, 0, 0), pipeline_mode=pl.Buffered(1))
    return pl.pallas_call(
        body,
        out_shape=(SDS((N_SHARD, S, HID_S), BF16), SDS((N_SHARD, S, HID_S), BF16),
                   SDS((S, D_MODEL), F32), SDS((S, D_MODEL), BF16), SDS((1, D_MODEL), F32)),
        grid=(S // tm,),
        in_specs=[row, row, resident((N_SHARD, HID_S, D_MODEL)), resident((N_SHARD, D_MODEL, HID_S)),
                  resident((N_SHARD, D_MODEL, HID_S)), aspec, aspec, row, vec],
        out_specs=(aspec, aspec, row, row, vec),
        compiler_params=_cparams("arbitrary"), name="ffn_bwd")(dx2b, dx2, wd, wg, wu, gte, up, x1, g2)


def _wgrad(name, a, b, a_spec, b_spec, out_shape, out_spec, n_par, S):
    tk = min(S, 4096)

    def body(a_ref, b_ref, o_ref):
        @pl.when(pl.program_id(1) == 0)
        def _():
            o_ref[...] = jnp.zeros_like(o_ref)

        o_ref[...] += _dot_tn(a_ref[...], b_ref[...])

    return pl.pallas_call(
        body, out_shape=SDS(out_shape, F32), grid=(n_par, S // tk),
        in_specs=[a_spec(tk), b_spec(tk)], out_specs=out_spec,
        compiler_params=_cparams("parallel", "arbitrary"), name=name)(a, b)


def _mix_bwd(dx1b, wo, proj, ya, yr, wa, wr, att, exchanges=()):
    S = dx1b.shape[0]
    tm = 512
    gate0 = 15 * COLB

    def body(d_ref, wo_ref, ga_ref, gr_ref, ya_ref, yr_ref, wa_ref, wr_ref, att_ref,
             dya_ref, dyr_ref, dp_ref, datt_ref, rho_ref, dyi_ref):
        pieces = _row_pieces(tm, 256)
        dms = [_dot_nt(d_ref[rows, :], wo_ref[...]) for rows in pieces]
        branch = []
        for rows, dm in zip(pieces, dms):
            sa = _sigmoid(ga_ref[rows, :].astype(F32))
            sr = _sigmoid(gr_ref[rows, :].astype(F32))
            dya, dyr = (dm * sa).astype(BF16), (dm * sr).astype(BF16)
            dya_ref[rows, :] = dya
            dyr_ref[rows, :] = dyr
            dp_ref[rows, 0:D_MODEL] = (dm * ya_ref[rows, :].astype(F32) * sa * (1.0 - sa)).astype(BF16)
            dp_ref[rows, D_MODEL:2 * D_MODEL] = (dm * yr_ref[rows, :].astype(F32) * sr * (1.0 - sr)).astype(BF16)
            branch.append((dya, dyr))
        lane = lax.broadcasted_iota(jnp.int32, (256, 128), 1)
        lo = lane < 64
        for rows, (dya, dyr) in zip(pieces, branch):
            datt = _dot_nt(dya, wa_ref[...])
            datt_ref[rows, :] = datt.astype(BF16)
            dyi_ref[rows, :] = _dot_nt(dyr, wr_ref[...]).astype(BF16)
            prod = datt * att_ref[rows, :].astype(F32)
            rho = jnp.zeros((256, 128), F32)
            for c in range(4):
                pc = prod[:, c * 128:(c + 1) * 128]
                tot = jnp.sum(pc, axis=-1, keepdims=True)
                low = jnp.sum(jnp.where(lo, pc, 0.0), axis=-1, keepdims=True)
                rho = jnp.where(lane // 16 == 2 * c, low, jnp.where(lane // 16 == 2 * c + 1, tot - low, rho))
            rho_ref[rows, :] = rho

    row = lambda w: pl.BlockSpec((tm, w), lambda i: (i, 0))
    cols = lambda c0, w: pl.BlockSpec((pl.Element(tm), pl.Element(w)), lambda i: (i * tm, c0))
    resident = lambda r, c: pl.BlockSpec((r, c), lambda i: (0, 0), pipeline_mode=pl.Buffered(1))
    return _carrier_call(
        body, (dx1b, wo, proj, proj, ya, yr, wa, wr, att),
        out_shape=(SDS((S, D_MODEL), BF16), SDS((S, D_MODEL), BF16), SDS((S, PROJ_W), BF16),
                   SDS((S, 512), BF16), SDS((S, 128), F32), SDS((S, D_MODEL), BF16)),
        grid=(S // tm,),
        in_specs=[row(D_MODEL), resident(D_MODEL, D_MODEL), cols(gate0, D_MODEL), cols(gate0 + D_MODEL, D_MODEL),
                  row(D_MODEL), row(D_MODEL), resident(512, D_MODEL), resident(D_MODEL, D_MODEL), row(512)],
        out_specs=(row(D_MODEL), row(D_MODEL), cols(gate0, 2 * D_MODEL), row(512), row(128), row(D_MODEL)),
        sem=("parallel",), name="mix_bwd", exchanges=exchanges)


def _attn_bwd(qkv, datt, lse, rho, rtab, d, gi, exchanges=()):
    L = qkv.shape[0]
    nb = L // BLK
    T = d * nb

    def body(q_ref, kc_ref, kp_ref, vc_ref, vp_ref, do_ref, lse_ref, rho_ref, tq_ref, tk_ref,
             dq_ref, dk_ref, dv_ref, ck, cv):
        t = pl.program_id(0)
        n = jnp.minimum(t, T - 1) % nb

        @pl.when(t == 0)
        def _():
            ck[...] = jnp.zeros_like(ck)
            cv[...] = jnp.zeros_like(cv)

        def store_rot(ref, val, t_ref, c):
            sl = slice(c * 128, (c + 1) * 128)
            ref[:, sl] = _unrot(val, t_ref[0], t_ref[1], t_ref[2], 32).astype(BF16)

        @pl.when(t < T)
        def _():
            mask = _band_mask(n)
            mask2 = jnp.concatenate([mask, mask], axis=0)
            lo = lax.broadcasted_iota(jnp.int32, (BLK, 128), 1) < 64

            def stacked(a):
                return jnp.concatenate([jnp.where(lo, a, jnp.zeros_like(a)), jnp.where(lo, jnp.zeros_like(a), a)], axis=0)

            def head_cols(ref, c):
                return jnp.concatenate([jnp.broadcast_to(ref[:, 32 * c:32 * c + 1], (BLK, 2 * BLK)),
                                        jnp.broadcast_to(ref[:, 32 * c + 16:32 * c + 17], (BLK, 2 * BLK))], axis=0)

            ops, raw = [], []
            for c in range(4):
                sl = slice(c * 128, (c + 1) * 128)
                q2, do2 = stacked(q_ref[:, sl]), stacked(do_ref[:, sl])
                k = jnp.concatenate([kp_ref[:, sl], kc_ref[:, sl]], axis=0)
                v = jnp.concatenate([vp_ref[:, sl], vc_ref[:, sl]], axis=0)
                ops.append((q2, do2, k))
                raw.append((_dot_nt(q2, k), _dot_nt(do2, v)))
            grads = []
            for c, (s, dp) in enumerate(raw):
                p = jnp.where(mask2, jnp.exp(s * 0.125 - head_cols(lse_ref, c)), 0.0)
                grads.append(((p * (dp - head_cols(rho_ref, c)) * 0.125).astype(BF16), p.astype(BF16)))
            for c, ((q2, do2, k), (ds, pb)) in enumerate(zip(ops, grads)):
                sl = slice(c * 128, (c + 1) * 128)
                dq2 = _dot(ds, k)
                dq_c = jnp.where(lo, dq2[:BLK], dq2[BLK:])
                dk_c = _dot_tn(ds, q2)
                dv_c = _dot_tn(pb, do2)
                store_rot(dq_ref, dq_c, tq_ref, c)
                store_rot(dk_ref, ck[:, sl] + dk_c[:BLK], tk_ref, c)
                dv_ref[:, sl] = (cv[:, sl] + dv_c[:BLK]).astype(BF16)
                ck[:, sl] = dk_c[BLK:]
                cv[:, sl] = dv_c[BLK:]

        @pl.when(t == T)
        def _():
            for c in range(4):
                sl = slice(c * 128, (c + 1) * 128)
                store_rot(dk_ref, ck[:, sl], tk_ref, c)
            dv_ref[...] = cv[...].astype(BF16)

    blk_of = lambda t: (jnp.minimum(t, T - 1) % nb, jnp.minimum(t, T - 1) // nb)
    cur = lambda t: blk_of(t)
    prev = lambda t: (jnp.maximum(blk_of(t)[0] - 1, 0), blk_of(t)[1])
    fin = lambda t: blk_of(jnp.maximum(t - 1, 0))
    col = _qkv_col(d, gi)
    qkv_spec = lambda kind, which: pl.BlockSpec((BLK, 512), lambda t: (which(t)[0], col(kind, which(t)[1])))
    row_spec = lambda w, which: pl.BlockSpec((BLK, w), lambda t: which(t))
    tab_spec = lambda which: pl.BlockSpec((3, BLK, 128), lambda t: (0, *which(t)))
    return _carrier_call(
        body, (qkv, qkv, qkv, qkv, qkv, datt, lse, rho, rtab, rtab),
        out_shape=(SDS((L, d * 512), BF16),) * 3, grid=(T + 1,),
        in_specs=[qkv_spec(0, cur), qkv_spec(1, cur), qkv_spec(1, prev), qkv_spec(2, cur), qkv_spec(2, prev),
                  row_spec(512, cur), row_spec(128, cur), row_spec(128, cur), tab_spec(cur), tab_spec(fin)],
        out_specs=(row_spec(512, cur), row_spec(512, fin), row_spec(512, fin)),
        scratch_shapes=[pltpu.VMEM((BLK, 512), F32), pltpu.VMEM((BLK, 512), F32)],
        sem=("arbitrary",), name=f"attn_bwd_g{gi}", exchanges=exchanges)


def _ret_bwd(proj, rn, rstd, dyrin, states, tab, consts, dproj, exchanges=()):
    S = proj.shape[0]
    nc = S // BLK
    dmask, zeta, xi, dec = consts

    def body(q_ref, k_ref, v0_ref, v1_ref, g0_ref, g1_ref, rn_ref, rs_ref, dy_ref, st_ref, tq_ref, tk_ref,
             dm_ref, z_ref, x_ref, dec_ref, dp_prev, dp_ref, dR):
        dq_ref, dk_ref = dp_ref.at[:, 0:512], dp_ref.at[:, 512:1024]
        dv_ref, dgr_ref = dp_ref.at[:, 1024:2048], dp_ref.at[:, 2048:3072]

        @pl.when(pl.program_id(0) == 0)
        def _():
            dR[...] = jnp.zeros_like(dR)

        dobs = []
        for h in range(RET_HEADS):
            vs = slice((h % 2) * 256, (h % 2 + 1) * 256)
            os_ = slice(h * 256, (h + 1) * 256)
            gr = (g0_ref if h < 2 else g1_ref)[:, vs].astype(F32)
            sg = _sigmoid(gr)
            rn_v = rn_ref[:, os_].astype(F32)
            dyi = dy_ref[:, os_].astype(F32)
            dgr_ref[:, os_] = (dyi * rn_v * sg * (1.0 + gr * (1.0 - sg))).astype(BF16)
            drn = dyi * gr * sg
            rstd = jnp.broadcast_to(rs_ref[:, 16 * h:16 * h + 1], (BLK, 256))
            do = rstd * (drn - jnp.mean(drn, axis=-1, keepdims=True) - rn_v * jnp.mean(drn * rn_v, axis=-1, keepdims=True))
            dobs.append(do.astype(BF16))
        first = []
        for h in range(RET_HEADS):
            hs = slice(h * 128, (h + 1) * 128)
            q, k = q_ref[:, hs], k_ref[:, hs]
            v = (v0_ref if h < 2 else v1_ref)[:, (h % 2) * 256:(h % 2 + 1) * 256]
            dob, dRb = dobs[h], dR[h].astype(BF16)
            kz = (k.astype(F32) * z_ref[h]).astype(BF16)
            qx = (q.astype(F32) * x_ref[h]).astype(BF16)
            first.append((q, k, _dot_nt(q, k), _dot_nt(dob, v), _dot(kz, dRb), _dot_nt(dob, st_ref[h]),
                          _dot_nt(v, dRb), _dot_tn(qx, dob)))
        masked = [((s * dm_ref[h]).astype(BF16), (dsr * dm_ref[h]).astype(BF16))
                  for h, (_, _, s, dsr, _, _, _, _) in enumerate(first)]
        for h in range(RET_HEADS):
            hs = slice(h * 128, (h + 1) * 128)
            os_ = slice(h * 256, (h + 1) * 256)
            q, k, _, _, dv_state, dq_state, dk_state, dr_new = first[h]
            sD, dS = masked[h]
            dv_ref[:, os_] = (_dot_tn(sD, dobs[h]) + dv_state).astype(BF16)
            dq = _dot(dS, k) + dq_state * x_ref[h]
            dk = _dot_tn(dS, q) + dk_state * z_ref[h]
            dR[h] = dR[h] * dec_ref[h, 0:1, :] + dr_new
            dq_ref[:, hs] = _unrot(dq, tq_ref[0], tq_ref[1], tq_ref[2], 1).astype(BF16)
            dk_ref[:, hs] = _unrot(dk, tk_ref[0], tk_ref[1], tk_ref[2], 1).astype(BF16)

    rc = lambda c: nc - 1 - c
    cst = lambda shape: pl.BlockSpec(shape, lambda c: (0, 0, 0))
    blk = lambda j: pl.BlockSpec((BLK, 512), lambda c: (rc(c), j))
    row = lambda w: pl.BlockSpec((BLK, w), lambda c: (rc(c), 0))
    (dproj,), xres = _carrier_call(
        body, (proj, proj, proj, proj, proj, proj, rn, rstd, dyrin, states, tab, tab, dmask, zeta, xi, dec, dproj),
        out_shape=(SDS((S, PROJ_W), BF16),), grid=(nc,),
        in_specs=[blk(QR_B), blk(KR_B), blk(11), blk(12), blk(13), blk(14), row(1024), row(128), row(1024),
                  pl.BlockSpec((RET_HEADS, None, BLK, 256), lambda c: (0, rc(c), 0, 0)),
                  pl.BlockSpec((None, 3, BLK, 128), lambda c: (1, 0, rc(c), 0)),
                  pl.BlockSpec((None, 3, BLK, 128), lambda c: (2, 0, rc(c), 0)),
                  cst((RET_HEADS, BLK, BLK)), cst((RET_HEADS, BLK, 128)), cst((RET_HEADS, BLK, 128)), cst((RET_HEADS, 8, 256)),
                  ANY],
        out_specs=(pl.BlockSpec((pl.Element(BLK), pl.Element(6 * COLB)), lambda c: (rc(c) * BLK, QR_B * COLB)),),
        scratch_shapes=[pltpu.VMEM((RET_HEADS, BLK, 256), F32)],
        sem=("arbitrary",), name="ret_bwd", exchanges=exchanges, in_out_aliases={16: 0})
    return dproj, xres


def _wgrad_in_half(ht, dproj, sidx, kept, exchanges=()):
    S = dproj.shape[0]
    tk = 2048
    half = (lambda sx: sx[4]) if kept else (lambda sx: 1 - sx[4])

    def body(a_ref, b_ref, o_ref):
        @pl.when(pl.program_id(1) == 0)
        def _():
            o_ref[...] = jnp.zeros_like(o_ref)

        o_ref[...] += _dot(a_ref[...], b_ref[...])

    (g,), xres = _carrier_call(
        body, (ht, dproj), out_shape=(SDS((D_MODEL // 2, PROJ_W), F32),), grid=(N_SHARD, S // tk),
        in_specs=[pl.BlockSpec((D_MODEL // 2, tk), lambda s, k, sx: (half(sx), k)),
                  pl.BlockSpec((tk, W_IN_S), lambda s, k, sx: (k, s))],
        out_specs=(pl.BlockSpec((D_MODEL // 2, W_IN_S), lambda s, k, sx: (0, s)),),
        sem=("parallel", "arbitrary"), name="wgrad_in_kept" if kept else "wgrad_in_sent", exchanges=exchanges,
        prefetch=sidx)
    return g, xres


def _in_proj_bwd(dproj, w_in, x, g1, dx1, exchanges=()):
    S = x.shape[0]
    tm = 1024

    def body(d_ref, w_ref, x_ref, g_ref, dx1_ref, dx_ref, dgn_ref, acc):
        i, s = pl.program_id(0), pl.program_id(1)

        @pl.when(s == 0)
        def _():
            acc[...] = jnp.zeros_like(acc)

        @pl.when((i == 0) & (s == 0))
        def _():
            dgn_ref[...] = jnp.zeros_like(dgn_ref)

        acc[...] += _dot_nt(d_ref[...], w_ref[...])

        @pl.when(s == N_SHARD - 1)
        def _():
            xv = x_ref[...]
            r = lax.rsqrt(jnp.mean(xv * xv, axis=-1, keepdims=True) + NORM_EPS)
            xh = xv * r
            dh = acc[...]
            dgn_ref[...] += jnp.sum(dh * xh, axis=0, keepdims=True)
            dxh = dh * g_ref[...]
            dx_ref[...] = dx1_ref[...] + r * (dxh - xh * jnp.mean(dxh * xh, axis=-1, keepdims=True))

    row = pl.BlockSpec((tm, D_MODEL), lambda i, s: (i, 0))
    vec = pl.BlockSpec((1, D_MODEL), lambda i, s: (0, 0))
    (gx, dg), xres = _carrier_call(
        body, (dproj, w_in, x, g1, dx1),
        out_shape=(SDS((S, D_MODEL), F32), SDS((1, D_MODEL), F32)), grid=(S // tm, N_SHARD),
        in_specs=[pl.BlockSpec((tm, W_IN_S), lambda i, s: (i, s)),
                  pl.BlockSpec((D_MODEL, W_IN_S), lambda i, s: (0, s)), row, vec, row],
        out_specs=(row, vec), scratch_shapes=[pltpu.VMEM((tm, D_MODEL), F32)],
        sem=("arbitrary", "arbitrary"), name="in_proj_bwd", exchanges=exchanges)
    return gx, dg, xres


def _sub_view(a, d):
    S, W = a.shape
    return a.reshape(S // d, d * W)


def _step(x, tgt, g1, g2, g3, comm):
    S = x.shape[0]
    tab_np = _tables(S)
    tab = jnp.asarray(tab_np)
    consts = _ret_consts()

    (h, ht, *casts), xres = _rms_fwd(x, g1, comm.to_cast(), comm.carry("rms_fwd"))
    comm.cast_done(casts)
    comm.took("rms_fwd", xres)
    w_in = comm.weight(0)
    proj, xres = _in_proj(h, w_in, tab, comm.carry("in_proj"))
    comm.took("in_proj", xres)
    qkvs, o_parts, lse_parts = [], [], []
    for gi, d in enumerate(DILATIONS):
        qkv = proj if d == 1 else _qkv_to_sub(proj, d, gi)
        (o_g, lse_g), xres = _attn_fwd(qkv, d, gi, comm.carry(f"attn_fwd_g{gi}"))
        comm.took(f"attn_fwd_g{gi}", xres)
        qkvs.append(qkv)
        o_parts.append(o_g)
        lse_parts.append(lse_g)
    att, lse_tot = _attn_merge(o_parts, lse_parts)
    (yrin, rn, rstd, states), xres = _ret_fwd(proj, consts, comm.carry("ret_fwd"))
    comm.took("ret_fwd", xres)
    wa, wr, wo = comm.weight(1), comm.weight(2), comm.weight(3)
    (merged, ya, yr, x1, h2), xres = _mix_out(att, yrin, proj, wa, wr, wo, x, g2, comm.carry("mix_out"))
    comm.took("mix_out", xres)
    wg, wu, wd = comm.weight(4), comm.weight(5), comm.weight(6)
    gte, up, act, dx2, dx2b, dg3, loss_p = _ffn_fwd(h2, wg, wu, wd, x1, g3, tgt)

    dgte, dup, dx1, dx1b, dg2 = _ffn_bwd(dx2b, dx2, wd, wg, wu, gte, up, x1, g2)
    tok3 = lambda w: (lambda tk: pl.BlockSpec((None, tk, w), lambda p, k: (p, k, 0)))
    tok2 = lambda w: (lambda tk: pl.BlockSpec((tk, w), lambda p, k: (k, 0)))
    g_d = _wgrad("wgrad_down", act, dx2b, tok3(HID_S), tok2(D_MODEL), (N_SHARD, HID_S, D_MODEL),
                 pl.BlockSpec((None, HID_S, D_MODEL), lambda p, k: (p, 0, 0)), N_SHARD, S)
    g_g = _wgrad("wgrad_gate", h2, dgte, tok2(D_MODEL), tok3(HID_S), (N_SHARD, D_MODEL, HID_S),
                 pl.BlockSpec((None, D_MODEL, HID_S), lambda p, k: (p, 0, 0)), N_SHARD, S)
    g_u = _wgrad("wgrad_up", h2, dup, tok2(D_MODEL), tok3(HID_S), (N_SHARD, D_MODEL, HID_S),
                 pl.BlockSpec((None, D_MODEL, HID_S), lambda p, k: (p, 0, 0)), N_SHARD, S)
    comm.grads({4: g_g, 5: g_u, 6: g_d})
    (dya, dyr, dproj, datt, rho, dyrin), xres = _mix_bwd(dx1b, wo, proj, ya, yr, wa, wr, att, comm.carry("mix_bwd"))
    comm.took("mix_bwd", xres)
    colblk = lambda w: (lambda tk: pl.BlockSpec((tk, w), lambda p, k: (k, p)))
    g_o = _wgrad("wgrad_out", merged, dx1b, colblk(256), tok2(D_MODEL), (D_MODEL, D_MODEL),
                 pl.BlockSpec((256, D_MODEL), lambda p, k: (p, 0)), 4, S)
    g_a = _wgrad("wgrad_attn", att, dya, tok2(512), colblk(512), (512, D_MODEL),
                 pl.BlockSpec((512, 512), lambda p, k: (0, p)), 2, S)
    g_r = _wgrad("wgrad_ret", yrin, dyr, colblk(256), tok2(D_MODEL), (D_MODEL, D_MODEL),
                 pl.BlockSpec((256, D_MODEL), lambda p, k: (p, 0)), 4, S)
    comm.grads({1: g_a, 2: g_r.reshape(N_SHARD, 256, D_MODEL), 3: g_o.reshape(N_SHARD, 256, D_MODEL)})
    dproj, xres = _ret_bwd(proj, rn, rstd, dyrin, states, tab, consts, dproj, comm.carry("ret_bwd"))
    comm.took("ret_bwd", xres)
    dqs, dks, dvs = [], [], []
    for gi, d in enumerate(DILATIONS):
        rtab = jnp.asarray(tab_np[0].reshape(3, S // d, d * 128))
        (dq, dk, dv), xres = _attn_bwd(qkvs[gi], _sub_view(datt, d), _sub_view(lse_tot, d), _sub_view(rho, d), rtab, d, gi,
                                       comm.carry(f"attn_bwd_g{gi}"))
        comm.took(f"attn_bwd_g{gi}", xres)
        dqs.append(dq)
        dks.append(dk)
        dvs.append(dv)
    dproj = _assemble_dproj((dqs, dks, dvs), dproj)
    g_sent, xres = _wgrad_in_half(ht, dproj, comm.sidx, False, comm.carry("wgrad_in_sent"))
    comm.took("wgrad_in_sent", xres)
    comm.grads({"in_sent": g_sent})
    g_kept, xres = _wgrad_in_half(ht, dproj, comm.sidx, True, comm.carry("wgrad_in_kept"))
    comm.grads({"in_kept": g_kept})
    comm.took("wgrad_in_kept", xres)
    grad_x, dg1, xres = _in_proj_bwd(dproj, w_in, x, g1, dx1, comm.carry("in_proj_bwd"))
    comm.took("in_proj_bwd", xres)
    return loss_p, grad_x, (dg1, dg2, dg3)


W_KINDS = ("col", "col", "lead", "lead", "lead", "lead", "lead")
W_SHARD = ((1024, W_IN_S), (512, 256), (256, 1024), (256, 1024), (1024, HID_S), (1024, HID_S), (HID_S, 1024))
N_W = len(W_KINDS)


def _full_shape(wi):
    R, C = W_SHARD[wi]
    return (R, N_SHARD * C) if W_KINDS[wi] == "col" else (N_SHARD, R, C)


def _view(ref, wi, s, half):
    R, C = W_SHARD[wi]
    rows = pl.ds(half * (R // 2), R // 2)
    if W_KINDS[wi] == "col":
        return ref.at[rows, pl.ds(pl.multiple_of(s * C, 128), C)]
    return ref.at[s, rows, :]


def _mesh_pos():
    x, y, c = lax.axis_index("x"), lax.axis_index("y"), lax.axis_index("c")
    chips = [(1 - x, y), (x, 1 - y), (1 - x, 1 - y)]
    return x, y, c, chips


def _cast_bf16(a):
    R, C = a.shape
    tr = R // 2 if R % 32 == 0 else R

    def body(a_ref, o_ref):
        o_ref[...] = a_ref[...].astype(BF16)

    spec = pl.BlockSpec((tr, C), lambda i: (i, 0))
    return pl.pallas_call(body, out_shape=SDS((R, C), BF16), grid=(R // tr,), in_specs=[spec], out_specs=spec,
                          compiler_params=_cparams("parallel"), name=f"cast_{R}x{C}")(a)


def _remote(send, recv, k, src, dst, to):
    return pltpu.make_async_remote_copy(src_ref=src, dst_ref=dst, send_sem=send.at[k], recv_sem=recv.at[k],
                                        device_id=to, device_id_type=MESH)


def _ex_gather_ring(wis, shards):
    n = len(wis)

    def build(sh, full, send, recv, loc):
        x, y, c, _ = _mesh_pos()
        s_me, sib = 2 * x + y, (x, y, 1 - c)
        xn, yn = (1 - x, y), (x, 1 - y)
        flip = lambda a, b: a + b - 2 * a * b
        via = (flip(x, 1 - c), flip(y, c))
        onto = (flip(x, c), flip(y, 1 - c))
        shard_of = lambda chip: 2 * chip[0] + chip[1]
        starts, waits, sent = [], [], []
        for i, wi in enumerate(wis):
            Rh = W_SHARD[wi][0] // 2
            for hf in range(2):
                cp = pltpu.make_async_copy(sh[i].at[pl.ds(hf * Rh, Rh), :], _view(full[i], wi, s_me, hf), loc.at[2 * i + hf])
                starts.append(cp)
                sent.append(cp.wait)
            for j, chip in enumerate((xn, yn)):
                cp = _remote(send, recv, 6 * i + j, sh[i].at[pl.ds(c * Rh, Rh), :], _view(full[i], wi, s_me, c), (*chip, c))
                starts.append(cp)
                sent.append(cp.wait_send)

        def pass_to_sibling(i, wi, k, s):
            mine = _view(full[i], wi, s, c)
            fw = _remote(send, recv, 6 * i + k, mine, mine, sib)
            waits.append(fw.start)
            sent.append(fw.wait_send)

        for i, wi in enumerate(wis):
            for j, chip in enumerate((xn, yn)):
                land = _view(full[i], wi, shard_of(chip), c)
                waits.append(_remote(send, recv, 6 * i + j, land, land, (*chip, c)).wait_recv)
                pass_to_sibling(i, wi, 3 + j, shard_of(chip))
            relay = _view(full[i], wi, shard_of(via), c)
            fw = _remote(send, recv, 6 * i + 2, relay, relay, (*onto, c))
            waits.append(fw.start)
            sent.append(fw.wait_send)
        s_diag = 2 * (1 - x) + (1 - y)
        for i, wi in enumerate(wis):
            land = _view(full[i], wi, s_diag, c)
            waits.append(_remote(send, recv, 6 * i + 2, land, land, (*onto, c)).wait_recv)
            pass_to_sibling(i, wi, 5, s_diag)
        for i, wi in enumerate(wis):
            for k, s in ((3, shard_of(xn)), (4, shard_of(yn)), (5, s_diag)):
                land = _view(full[i], wi, s, 1 - c)
                waits.append(_remote(send, recv, 6 * i + k, land, land, sib).wait_recv)
        return starts, waits + sent

    return _Exchange(shards, [SDS(_full_shape(wi), BF16) for wi in wis], {}, 6 * n, 2 * n, build)


def _ex_gather_ici(wis, shards, then_d2d=False):
    n = len(wis)

    def build(ins, outs, send, recv, loc):
        x, y, c, chips = _mesh_pos()
        s_me, sib = 2 * x + y, (x, y, 1 - c)
        starts, waits, after = [], [], []
        for i, wi in enumerate(wis):
            Rh = W_SHARD[wi][0] // 2
            for hf in range(2):
                cp = pltpu.make_async_copy(ins[i].at[pl.ds(hf * Rh, Rh), :], _view(outs[i], wi, s_me, hf), loc.at[2 * i + hf])
                starts.append(cp)
                waits.append(cp.wait)
            for j, chip in enumerate(chips):
                cp = _remote(send, recv, 3 * i + j, ins[i].at[pl.ds(c * Rh, Rh), :], _view(outs[i], wi, s_me, c), (*chip, c))
                land = _view(outs[i], wi, 2 * chip[0] + chip[1], c)
                starts.append(cp)
                waits += [cp.wait_send, _remote(send, recv, 3 * i + j, land, land, (*chip, c)).wait_recv]
                if then_d2d:
                    theirs = _view(outs[i], wi, 2 * chip[0] + chip[1], 1 - c)
                    fw = _remote(send, recv, 3 * n + 3 * i + j, land, land, sib)
                    waits.append(fw.start)
                    after += [fw.wait_send, _remote(send, recv, 3 * n + 3 * i + j, theirs, theirs, sib).wait_recv]
        return starts, waits + after

    return _Exchange(shards, [SDS(_full_shape(wi), BF16) for wi in wis], {}, (6 if then_d2d else 3) * n, 2 * n, build)


def _ex_gather_d2d(wis, fulls):
    def build(ins, outs, send, recv, loc):
        x, y, c, chips = _mesh_pos()
        sib = (x, y, 1 - c)
        starts, waits = [], []
        for i, wi in enumerate(wis):
            for j, chip in enumerate(chips):
                mine = _view(outs[i], wi, 2 * chip[0] + chip[1], c)
                theirs = _view(outs[i], wi, 2 * chip[0] + chip[1], 1 - c)
                cp = _remote(send, recv, 3 * i + j, mine, mine, sib)
                starts.append(cp)
                waits += [cp.wait_send, _remote(send, recv, 3 * i + j, theirs, theirs, sib).wait_recv]
        return starts, waits

    return _Exchange(fulls, [SDS(f.shape, BF16) for f in fulls], {i: i for i in range(len(wis))}, 3 * len(wis), 0, build)


def _half_shape(wi):
    R, C = W_SHARD[wi]
    return (R // 2, N_SHARD * C) if W_KINDS[wi] == "col" else (N_SHARD, R // 2, C)


def _ex_pair(wis, grads):
    def build(ins, outs, send, recv, loc):
        x, y, c, _ = _mesh_pos()
        starts, waits = [], []
        for i, wi in enumerate(wis):
            Rh = W_SHARD[wi][0] // 2
            rows = pl.ds((1 - c) * Rh, Rh)
            if tuple(ins[i].shape) == _half_shape(wi):
                src = ins[i]
            else:
                src = ins[i].at[rows, :] if W_KINDS[wi] == "col" else ins[i].at[:, rows, :]
            cp = _remote(send, recv, i, src, outs[i], (x, y, 1 - c))
            starts.append(cp)
            waits.append(cp.wait)
        return starts, waits

    return _Exchange(grads, [SDS(_half_shape(wi), F32) for wi in wis], {}, len(wis), 0, build)


def _ex_chip(wis, pbs):
    def build(ins, outs, send, recv, loc):
        x, y, c, chips = _mesh_pos()
        starts, waits = [], []
        for i, wi in enumerate(wis):
            for j, chip in enumerate(chips):
                cp = _remote(send, recv, 3 * i + j, ins[i].at[j], outs[i].at[j], (*chip, c))
                starts.append(cp)
                waits.append(cp.wait)
        return starts, waits

    shapes = [SDS((3, W_SHARD[wi][0] // 2, W_SHARD[wi][1]), BF16) for wi in wis]
    return _Exchange(pbs, shapes, {}, 3 * len(wis), 0, build)


def _ex_share(wis, halves):
    def build(ins, outs, send, recv, loc):
        x, y, c, _ = _mesh_pos()
        sib = (x, y, 1 - c)
        starts, waits = [], []
        for i, wi in enumerate(wis):
            cp = _remote(send, recv, i, outs[i].at[c], outs[i].at[c], sib)
            starts.append(cp)
            waits += [cp.wait_send, _remote(send, recv, i, outs[i].at[1 - c], outs[i].at[1 - c], sib).wait_recv]
        return starts, waits

    return _Exchange(halves, [SDS(h.shape, F32) for h in halves], {i: i for i in range(len(wis))}, len(wis), 0, build)


def _row_tile(rh, C):
    best = 16
    for t in range(16, rh + 1, 16):
        if rh % t == 0 and t * C * 4 <= (3 << 19):
            best = t
    return best


def _pair_sum(wi, g, ra, sidx):
    R, C = W_SHARD[wi]
    Rh = R // 2
    tr = _row_tile(Rh, C)
    nt = Rh // tr
    off = 0 if tuple(g.shape) == _half_shape(wi) else nt
    col = W_KINDS[wi] == "col"

    def body(sidx_ref, *refs):
        gs, rs = refs[:4], refs[4:8]
        own_ref, pb_ref = refs[8:]
        own_ref[...] = gs[0][...] + rs[0][...]
        for j in range(3):
            pb_ref[j] = (gs[1 + j][...] + rs[1 + j][...]).astype(BF16)

    def gspec(slot):
        if col:
            return pl.BlockSpec((tr, C), lambda i, sx: (sx[4] * off + i, sx[slot]))
        return pl.BlockSpec((None, tr, C), lambda i, sx: (sx[slot], sx[4] * off + i, 0))

    def rspec(slot):
        if col:
            return pl.BlockSpec((tr, C), lambda i, sx: (i, sx[slot]))
        return pl.BlockSpec((None, tr, C), lambda i, sx: (sx[slot], i, 0))

    return pl.pallas_call(
        body, out_shape=(SDS((Rh, C), F32), SDS((3, Rh, C), BF16)),
        grid_spec=pltpu.PrefetchScalarGridSpec(
            num_scalar_prefetch=1, grid=(nt,),
            in_specs=[gspec(k) for k in range(4)] + [rspec(k) for k in range(4)],
            out_specs=(pl.BlockSpec((tr, C), lambda i, sx: (i, 0)), pl.BlockSpec((3, tr, C), lambda i, sx: (0, i, 0)))),
        compiler_params=_cparams("arbitrary"), name=f"pair_sum_w{wi}")(sidx, g, g, g, g, ra, ra, ra, ra)


def _chip_sum(wi, own, rb, sidx):
    R, C = W_SHARD[wi]
    Rh = R // 2
    tr = _row_tile(Rh, C)

    def body(sidx_ref, own_ref, rb_ref, o_ref):
        o_ref[...] = ((own_ref[...] + rb_ref[0].astype(F32)) + rb_ref[1].astype(F32)) + rb_ref[2].astype(F32)

    return pl.pallas_call(
        body, out_shape=SDS((2, Rh, C), F32),
        grid_spec=pltpu.PrefetchScalarGridSpec(
            num_scalar_prefetch=1, grid=(Rh // tr,),
            in_specs=[pl.BlockSpec((tr, C), lambda i, sx: (i, 0)), pl.BlockSpec((3, tr, C), lambda i, sx: (0, i, 0))],
            out_specs=pl.BlockSpec((None, tr, C), lambda i, sx: (sx[4], i, 0))),
        compiler_params=_cparams("arbitrary"), name=f"chip_sum_w{wi}")(sidx, own, rb)


def _gain_allgather(blk, ex):
    m_per, n = blk.shape
    n_in, n_out = len(ex.ins), len(ex.out_shapes)

    def body(x_ref, *rest):
        xin, out_ref, xout = rest[:n_in], rest[n_in], rest[n_in + 1:n_in + 1 + n_out]
        send_sems, recv_sems, local_sem = rest[n_in + 1 + n_out:n_in + 4 + n_out]
        ex_starts, ex_waits = ex.build(xin, xout, *rest[n_in + 4 + n_out:])
        for cp in ex_starts:
            cp.start()
        x, y, c, chips = _mesh_pos()
        me, sibling = (x, y, c), (x, y, 1 - c)

        def rows(px, py, pc):
            return out_ref.at[pl.ds((4 * px + 2 * py + pc) * m_per, m_per), :]

        def copy(k, block, to, src=None):
            return pltpu.make_async_remote_copy(
                src_ref=rows(*block) if src is None else src, dst_ref=rows(*block),
                send_sem=send_sems.at[k], recv_sem=recv_sems.at[k], device_id=to, device_id_type=MESH)

        mine = pltpu.make_async_copy(x_ref, rows(*me), local_sem)
        mine.start()
        first = [copy(0, me, sibling, src=x_ref)]
        first += [copy(1 + j, me, (*chip, c), src=x_ref) for j, chip in enumerate(chips)]
        for cp in first:
            cp.start()
        passed = [copy(4 + j, (*chip, c), sibling) for j, chip in enumerate(chips)]
        for j, chip in enumerate(chips):
            copy(1 + j, (*chip, c), me).wait_recv()
            passed[j].start()
        copy(0, sibling, me).wait_recv()
        for j, chip in enumerate(chips):
            copy(4 + j, (*chip, 1 - c), me).wait_recv()
        for cp in first + passed:
            cp.wait_send()
        mine.wait()
        for w in ex_waits:
            w()

    vm = pl.BlockSpec(memory_space=pltpu.VMEM)
    res = pl.pallas_call(
        body, out_shape=(SDS((8 * m_per, n), blk.dtype), *ex.out_shapes),
        in_specs=[vm] + [ANY] * n_in, out_specs=(vm, *[ANY] * n_out),
        input_output_aliases={1 + a: 1 + o for a, o in ex.aliases.items()},
        scratch_shapes=[pltpu.SemaphoreType.DMA((7,)), pltpu.SemaphoreType.DMA((7,)), pltpu.SemaphoreType.DMA] + ex.sems(),
        name="gain_allgather")(blk, *ex.ins)
    return res[0], tuple(res[1:])


def _adam_math(w, g, m, v):
    mn = ADAM_B1 * m + (1.0 - ADAM_B1) * g
    vn = ADAM_B2 * v + (1.0 - ADAM_B2) * (g * g)
    mh = mn / (1.0 - ADAM_B1 ** ADAM_STEP)
    vh = vn / (1.0 - ADAM_B2 ** ADAM_STEP)
    return -ADAM_LR * (mh / (jnp.sqrt(vh) + ADAM_EPS) + ADAM_WD * w), mn, vn


def _adamw(wi, w, g, m, v):
    R, C = w.shape
    tr = _row_tile(R, C)

    def body(w_ref, g_ref, m_ref, v_ref, go_ref, d_ref, mn_ref, vn_ref):
        g = g_ref[...]
        go_ref[...] = g
        d_ref[...], mn_ref[...], vn_ref[...] = _adam_math(w_ref[...], g, m_ref[...], v_ref[...])

    spec = pl.BlockSpec((tr, C), lambda i: (i, 0))
    return pl.pallas_call(body, out_shape=(SDS((R, C), F32),) * 4, grid=(R // tr,), in_specs=[spec] * 4,
                          out_specs=(spec,) * 4, compiler_params=_cparams("parallel"), name=f"adamw_w{wi}")(w, g, m, v)


def _gain_update(gathered, w, m, v):
    def body(ga_ref, w_ref, m_ref, v_ref, g_ref, d_ref, mn_ref, vn_ref):
        g = ga_ref[0:8, :]
        for dev in range(1, 8):
            g = g + ga_ref[8 * dev:8 * dev + 8, :]
        g_ref[...] = g
        d_ref[...], mn_ref[...], vn_ref[...] = _adam_math(w_ref[...], g, m_ref[...], v_ref[...])

    return pl.pallas_call(body, out_shape=(SDS((8, 1024), F32),) * 4, name="gain_update")(gathered, w, m, v)


GROUP_FFN, GROUP_MIX, GROUP_IN = (4, 5, 6), (1, 2, 3), (0,)
REST = GROUP_MIX + GROUP_FFN


class _MeshComm:
    SCHEDULE = {
        "rms_fwd": [("ring", GROUP_IN)],
        "in_proj": [("ici", (1, 2, 3, 4))],
        "ret_fwd": [("d2d", (1, 2, 3, 4)), ("ici", (5,))],
        "mix_out": [("d2d", (5,)), ("both", (6,))],
        "mix_bwd": [("pair", GROUP_FFN)],
        "ret_bwd": [("pair", GROUP_MIX), ("chip", (4,))],
        "attn_bwd_g0": [("chip", (5,))],
        "attn_bwd_g1": [("chip", (6,))],
        "attn_bwd_g2": [("chip", GROUP_MIX)],
        "wgrad_in_kept": [("pair", GROUP_IN), ("share", GROUP_FFN + GROUP_MIX)],
        "in_proj_bwd": [("chip", GROUP_IN)],
    }

    def __init__(self, w_in_shard, rest_f32):
        xi, yi, ci = lax.axis_index("x"), lax.axis_index("y"), lax.axis_index("c")
        self.sidx = jnp.stack([2 * xi + yi, 2 * (1 - xi) + yi, 2 * xi + (1 - yi), 2 * (1 - xi) + (1 - yi), ci]).astype(jnp.int32)
        self.shards, self.rest_f32, self.full = {0: w_in_shard}, list(rest_f32), {}
        self.g, self.own, self.pb, self.half, self.red = {}, {}, {}, {}, {}

    def to_cast(self):
        return self.rest_f32

    def cast_done(self, casts):
        self.shards.update(zip(REST, casts))

    def weight(self, wi):
        return self.full[wi].reshape(D_MODEL, D_MODEL) if wi in (2, 3) else self.full[wi]

    def grads(self, by_wi):
        self.g.update(by_wi)

    def _exchange(self, stage, wis):
        pick = lambda table: [table[wi] for wi in wis]
        if stage == "ring":
            return _ex_gather_ring(wis, pick(self.shards))
        if stage == "ici":
            return _ex_gather_ici(wis, pick(self.shards))
        if stage == "both":
            return _ex_gather_ici(wis, pick(self.shards), then_d2d=True)
        if stage == "d2d":
            return _ex_gather_d2d(wis, pick(self.full))
        if stage == "pair":
            return _ex_pair(wis, [self.g["in_sent"] if wi == 0 else self.g[wi] for wi in wis])
        if stage == "chip":
            return _ex_chip(wis, pick(self.pb))
        return _ex_share(wis, pick(self.half))

    def _landed(self, stage, wis, res):
        for wi, r in zip(wis, res):
            if stage in ("ring", "ici", "d2d", "both"):
                self.full[wi] = r
            elif stage == "pair":
                self.own[wi], self.pb[wi] = _pair_sum(wi, self.g["in_kept"] if wi == 0 else self.g[wi], r, self.sidx)
            elif stage == "chip":
                self.half[wi] = _chip_sum(wi, self.own[wi], r, self.sidx)
            else:
                self.red[wi] = r

    def carry(self, point):
        return [self._exchange(stage, wis) for stage, wis in self.SCHEDULE.get(point, ())]

    def took(self, point, xres):
        for (stage, wis), res in zip(self.SCHEDULE.get(point, ()), xres):
            self._landed(stage, wis, res)

    def last_share(self):
        return self._exchange("share", GROUP_IN)

    def reduced(self, last_shared):
        self._landed("share", GROUP_IN, last_shared)
        return [self.red[wi] for wi in range(N_W)]


def kernel(x, norm_mix_g, w_in, w_out_attn, w_out_ret, w_out, norm_ffn_g, w_ffn_gate, w_ffn_up, w_ffn_down, norm_final_g, loss_target, m_norm_mix_g, m_w_in, m_w_out_attn, m_w_out_ret, m_w_out, m_norm_ffn_g, m_w_ffn_gate, m_w_ffn_up, m_w_ffn_down, m_norm_final_g, v_norm_mix_g, v_w_in, v_w_out_attn, v_w_out_ret, v_w_out, v_norm_ffn_g, v_w_ffn_gate, v_w_ffn_up, v_w_ffn_down, v_norm_final_g):
    ws = (w_in, w_out_attn, w_out_ret, w_out, w_ffn_gate, w_ffn_up, w_ffn_down)
    ms = (m_w_in, m_w_out_attn, m_w_out_ret, m_w_out, m_w_ffn_gate, m_w_ffn_up, m_w_ffn_down)
    vs = (v_w_in, v_w_out_attn, v_w_out_ret, v_w_out, v_w_ffn_gate, v_w_ffn_up, v_w_ffn_down)
    shard2d = lambda a, wi: a.reshape(W_SHARD[wi])

    comm = _MeshComm(_cast_bf16(shard2d(ws[0], 0)), [shard2d(ws[wi], wi) for wi in REST])
    g3 = norm_final_g.reshape(1, D_MODEL)
    loss_p, grad_x, gain_g = _step(x[0], loss_target[0], norm_mix_g, norm_ffn_g, g3, comm)

    pad8 = lambda rows: jnp.concatenate([r.reshape(1, D_MODEL) for r in rows]
                                        + [jnp.zeros((8 - len(rows), D_MODEL), F32)], axis=0)
    gathered, shared = _gain_allgather(pad8((*gain_g, jnp.tile(loss_p[0:1], (1, D_MODEL // 128)))), comm.last_share())
    gred = comm.reduced(shared)

    outs_g, outs_d, outs_m, outs_v = [], [], [], []
    for wi in range(N_W):
        g2d = gred[wi].reshape(W_SHARD[wi])
        gout, dlt, mn, vn = _adamw(wi, shard2d(ws[wi], wi), g2d, shard2d(ms[wi], wi), shard2d(vs[wi], wi))
        for lst, a in ((outs_g, gout), (outs_d, dlt), (outs_m, mn), (outs_v, vn)):
            lst.append(a.reshape(ws[wi].shape))

    gg, gd, gm, gv = _gain_update(gathered, pad8((norm_mix_g, norm_ffn_g, norm_final_g)),
                                  pad8((m_norm_mix_g, m_norm_ffn_g, m_norm_final_g)),
                                  pad8((v_norm_mix_g, v_norm_ffn_g, v_norm_final_g)))
    loss = gg[3, 0]

    def assemble(gain_rows, wlist):
        return (gain_rows[0:1], wlist[0], wlist[1], wlist[2], wlist[3], gain_rows[1:2],
                wlist[4], wlist[5], wlist[6], gain_rows[2])

    return (loss, grad_x[None], *assemble(gg, outs_g), *assemble(gd, outs_d), *assemble(gm, outs_m), *assemble(gv, outs_v))
```

```python
import functools
import math

import numpy as np
import jax
import jax.numpy as jnp
from jax import lax
from jax.experimental import pallas as pl
from jax.experimental.pallas import tpu as pltpu

F32, BF16 = jnp.float32, jnp.bfloat16
SDS = jax.ShapeDtypeStruct
MESH = pl.DeviceIdType.MESH

D_MODEL = 1024
PROJ_W = 9728
COLB = 512
N_COLB = PROJ_W // COLB
QA_B, KA_B, VA_B = 0, 3, 6
QR_B, KR_B = 9, 10
FFN_HID = 2816
N_SHARD = 4
HID_S = FFN_HID // N_SHARD
W_IN_S = PROJ_W // N_SHARD
DILATIONS = (1, 4, 16)
BLK = 128
RET_HEADS = 4
ROPE_THETA = 10000.0
NORM_EPS = 1e-6
ADAM_LR, ADAM_B1, ADAM_B2, ADAM_EPS, ADAM_WD, ADAM_STEP = 0.001, 0.9, 0.999, 1e-08, 0.01, 10
VMEM_LIMIT = 56 << 20


def _cparams(*sem):
    return pltpu.CompilerParams(dimension_semantics=sem or None, vmem_limit_bytes=VMEM_LIMIT)


def _dot(a, b):
    return jnp.dot(a, b, preferred_element_type=F32)


def _dot_nt(a, b):
    return lax.dot_general(a, b, (((1,), (1,)), ((), ())), preferred_element_type=F32)


def _dot_tn(a, b):
    return lax.dot_general(a, b, (((0,), (0,)), ((), ())), preferred_element_type=F32)


def _row_pieces(tm, sub=512):
    return [slice(i, i + sub) for i in range(0, tm, sub)]


def _sigmoid(z):
    return 0.5 * jnp.tanh(0.5 * z) + 0.5


ANY = pl.BlockSpec(memory_space=pl.ANY)


class _Exchange:
    def __init__(self, ins, out_shapes, aliases, n_sem, n_loc, build):
        self.ins, self.out_shapes, self.aliases = list(ins), list(out_shapes), dict(aliases)
        self.n_sem, self.n_loc, self.build = n_sem, n_loc, build

    def sems(self):
        return [pltpu.SemaphoreType.DMA((self.n_sem,)), pltpu.SemaphoreType.DMA((self.n_sem,)),
                pltpu.SemaphoreType.DMA((max(self.n_loc, 1),))]


def _carrier_call(body, args, *, out_shape, grid, in_specs, out_specs, scratch_shapes=(), sem, name, exchanges=(),
                  prefetch=None, in_out_aliases=None):
    out_shape, out_specs = tuple(out_shape), tuple(out_specs)
    n_in, n_out, n_scr = len(args), len(out_shape), len(scratch_shapes)
    n_pre = 0 if prefetch is None else 1
    x_args, x_outs, x_scr, spans = [], [], [], []
    aliases = {n_pre + a: o for a, o in (in_out_aliases or {}).items()}
    for ex in exchanges:
        i0, o0 = len(x_args), len(x_outs)
        for a, o in ex.aliases.items():
            aliases[n_pre + n_in + i0 + a] = n_out + o0 + o
        x_args += ex.ins
        x_outs += ex.out_shapes
        x_scr += ex.sems()
        spans.append((i0, len(ex.ins), o0, len(ex.out_shapes)))
    nx_in, nx_out = len(x_args), len(x_outs)

    def wrapped(*refs):
        refs = refs[n_pre:]
        ins, xin = refs[:n_in], refs[n_in:n_in + nx_in]
        o_base = n_in + nx_in
        outs, xout = refs[o_base:o_base + n_out], refs[o_base + n_out:o_base + n_out + nx_out]
        s_base = o_base + n_out + nx_out
        scr, xs = refs[s_base:s_base + n_scr], refs[s_base + n_scr:]

        def built(e):
            i0, ni, o0, no = spans[e]
            return exchanges[e].build(xin[i0:i0 + ni], xout[o0:o0 + no], *xs[3 * e:3 * e + 3])

        if exchanges:
            first = functools.reduce(jnp.logical_and, [pl.program_id(k) == 0 for k in range(len(grid))])
            last = functools.reduce(jnp.logical_and, [pl.program_id(k) == grid[k] - 1 for k in range(len(grid))])

            @pl.when(first)
            def _():
                for e in range(len(exchanges)):
                    for cp in built(e)[0]:
                        cp.start()

        body(*ins, *outs, *scr)

        if exchanges:
            @pl.when(last)
            def _():
                for e in range(len(exchanges)):
                    for w in built(e)[1]:
                        w()

    all_in, all_out = list(in_specs) + [ANY] * nx_in, out_specs + tuple([ANY] * nx_out)
    all_scr = list(scratch_shapes) + x_scr
    cparams = _cparams(*(sem if not exchanges else ("arbitrary",) * len(grid)))
    if prefetch is None:
        res = pl.pallas_call(wrapped, out_shape=out_shape + tuple(x_outs), grid=grid, in_specs=all_in, out_specs=all_out,
                             scratch_shapes=all_scr, input_output_aliases=aliases, compiler_params=cparams,
                             name=name)(*args, *x_args)
    else:
        gs = pltpu.PrefetchScalarGridSpec(num_scalar_prefetch=1, grid=grid, in_specs=all_in, out_specs=all_out,
                                          scratch_shapes=all_scr)
        res = pl.pallas_call(wrapped, out_shape=out_shape + tuple(x_outs), grid_spec=gs, input_output_aliases=aliases,
                             compiler_params=cparams, name=name)(prefetch, *args, *x_args)
    xres = [tuple(res[n_out + o0:n_out + o0 + no]) for (_, _, o0, no) in spans]
    return tuple(res[:n_out]), xres


def _tables(S):
    f32 = np.float32
    pos = np.arange(S, dtype=f32)
    lane = np.arange(128)
    inv = (f32(ROPE_THETA) ** (-np.arange(0, 64, 2, dtype=f32) / f32(64))).astype(f32)
    ang = (pos[:, None] * inv[None, :]).astype(np.float64)
    idx = (lane % 64) % 32
    c, s = np.cos(ang)[:, idx], np.sin(ang)[:, idx]
    first = ((lane % 64) < 32)[None, :]
    rope = np.stack([c, np.where(first, 0.0, s), np.where(first, -s, 0.0)])
    base = (f32(1.0) / (f32(ROPE_THETA) ** np.linspace(0.0, 1.0, 64, dtype=f32))).astype(f32)
    ang2 = (pos[:, None] * base[None, :]).astype(np.float64)
    c2, s2 = np.cos(ang2)[:, lane // 2], np.sin(ang2)[:, lane // 2]
    even = (lane % 2 == 0)[None, :]
    th = np.stack([c2, np.where(even, 0.0, s2), np.where(even, -s2, 0.0)])
    return np.stack([rope, th, th * (128 ** -0.5)]).astype(f32)


def _rot(a, c, sa, sb, shift):
    return a * c + pltpu.roll(a, shift, 1) * sa + pltpu.roll(a, 128 - shift, 1) * sb


def _unrot(g, c, sa, sb, shift):
    return g * c + pltpu.roll(g * sa, 128 - shift, 1) + pltpu.roll(g * sb, shift, 1)


def _ret_consts():
    h = np.arange(RET_HEADS, dtype=np.float64)
    log_g = np.log1p(-(2.0 ** (-5.0 - h)))
    idx = np.arange(BLK, dtype=np.float64)
    diff = idx[:, None] - idx[None, :]
    dmask = np.where(diff[None] >= 0, np.exp(np.maximum(diff, 0.0)[None] * log_g[:, None, None]), 0.0)
    zeta = np.exp((BLK - 1 - idx)[None, :] * log_g[:, None])
    xi = np.exp((idx + 1.0)[None, :] * log_g[:, None])
    dec = np.exp(BLK * log_g)
    rep = lambda v: np.broadcast_to(v[:, :, None], (RET_HEADS, BLK, 128))
    return (jnp.asarray(dmask, F32), jnp.asarray(rep(zeta), F32), jnp.asarray(rep(xi), F32),
            jnp.asarray(np.broadcast_to(dec[:, None, None], (RET_HEADS, 8, 256)), F32))


def _rms_fwd(x, g, to_cast=(), exchanges=()):
    S = x.shape[0]
    steps = 4
    tm = S // steps
    n_c = len(to_cast)

    def body(x_ref, g_ref, *refs):
        c_in, (h_ref, ht_ref), c_out = refs[:n_c], refs[n_c:n_c + 2], refs[n_c + 2:]
        for rows in _row_pieces(tm, 512):
            xv = x_ref[rows, :]
            r = lax.rsqrt(jnp.mean(xv * xv, axis=-1, keepdims=True) + NORM_EPS)
            h = xv * r * g_ref[...]
            h_ref[rows, :] = h.astype(BF16)
            ht_ref[:, rows] = h.T.astype(BF16)
        for a_ref, o_ref in zip(c_in, c_out):
            o_ref[...] = a_ref[...].astype(BF16)

    slab = lambda a: pl.BlockSpec((a.shape[0] // steps, a.shape[1]), lambda i: (i, 0))
    return _carrier_call(
        body, (x, g, *to_cast),
        out_shape=(SDS((S, D_MODEL), BF16), SDS((D_MODEL, S), BF16), *[SDS(a.shape, BF16) for a in to_cast]),
        grid=(steps,),
        in_specs=[pl.BlockSpec((tm, D_MODEL), lambda i: (i, 0)), pl.BlockSpec((1, D_MODEL), lambda i: (0, 0))]
        + [slab(a) for a in to_cast],
        out_specs=(pl.BlockSpec((tm, D_MODEL), lambda i: (i, 0)), pl.BlockSpec((D_MODEL, tm), lambda i: (0, i)),
                   *[slab(a) for a in to_cast]),
        sem=("parallel",), name="rms_fwd", exchanges=exchanges)


def _in_proj(h, w_in, tab, exchanges=()):
    S = h.shape[0]
    tm = min(S, 4096)

    def body(h_ref, w_ref, t_ref, o_ref):
        j = pl.program_id(1)
        is_rope = j < 6
        is_theta = (j == QR_B) | (j == KR_B)
        sub = 512

        def rotated(shift):
            for i in range(tm // sub):
                rows = slice(i * sub, (i + 1) * sub)
                acc = _dot(h_ref[rows, :], w_ref[...])
                c, sa, sb = t_ref[0, 0, rows, :], t_ref[0, 1, rows, :], t_ref[0, 2, rows, :]
                for k in range(COLB // 128):
                    sl = slice(k * 128, (k + 1) * 128)
                    o_ref[rows, sl] = _rot(acc[:, sl], c, sa, sb, shift).astype(BF16)

        @pl.when(is_rope)
        def _():
            rotated(32)

        @pl.when(is_theta)
        def _():
            rotated(1)

        @pl.when(jnp.logical_not(is_rope | is_theta))
        def _():
            o_ref[...] = _dot(h_ref[...], w_ref[...]).astype(BF16)

    def tab_map(i, j):
        return (jnp.where(j == QR_B, 1, jnp.where(j == KR_B, 2, 0)), 0, i, 0)

    (proj,), xres = _carrier_call(
        body, (h, w_in, tab), out_shape=(SDS((S, PROJ_W), BF16),), grid=(S // tm, N_COLB),
        in_specs=[pl.BlockSpec((tm, D_MODEL), lambda i, j: (i, 0)),
                  pl.BlockSpec((D_MODEL, COLB), lambda i, j: (0, j)),
                  pl.BlockSpec((1, 3, tm, 128), tab_map)],
        out_specs=(pl.BlockSpec((tm, COLB), lambda i, j: (i, j)),),
        sem=("parallel", "arbitrary"), name="in_proj", exchanges=exchanges)
    return proj, xres


def _band_mask(n):
    qi = lax.broadcasted_iota(jnp.int32, (BLK, 2 * BLK), 0)
    kj = lax.broadcasted_iota(jnp.int32, (BLK, 2 * BLK), 1)
    dist = BLK + qi - kj
    return (dist >= 0) & (dist <= BLK) & ((kj >= BLK) | (n > 0))


def _qkv_col(d, gi):
    if d == 1:
        return lambda t, r: 3 * t + gi
    return lambda t, r: 3 * r + t


def _attn_fwd(qkv, d, gi, exchanges=()):
    L = qkv.shape[0]
    nb = L // BLK

    def body(q_ref, kc_ref, kp_ref, vc_ref, vp_ref, o_ref, lse_ref):
        n = pl.program_id(1)
        mask = _band_mask(n)
        mask2 = jnp.concatenate([mask, mask], axis=0)
        lane = lax.broadcasted_iota(jnp.int32, (BLK, 128), 1)
        lo = lane < 64
        lse_all = jnp.zeros((BLK, 128), F32)
        chunks = [slice(c * 128, (c + 1) * 128) for c in range(4)]
        scores, vals = [], []
        for sl in chunks:
            q = q_ref[:, sl]
            k = jnp.concatenate([kp_ref[:, sl], kc_ref[:, sl]], axis=0)
            vals.append(jnp.concatenate([vp_ref[:, sl], vc_ref[:, sl]], axis=0))
            q2 = jnp.concatenate([jnp.where(lo, q, jnp.zeros_like(q)), jnp.where(lo, jnp.zeros_like(q), q)], axis=0)
            scores.append(_dot_nt(q2, k))
        probs = []
        for c, s in enumerate(scores):
            s = jnp.where(mask2, s * 0.125, jnp.float32(-1e30))
            m = jnp.max(s, axis=-1, keepdims=True)
            p = jnp.exp(s - m)
            l = jnp.sum(p, axis=-1, keepdims=True)
            probs.append((p / l).astype(BF16))
            lse = m + jnp.log(l)
            lse_all = jnp.where(lane // 16 == 2 * c, lse[:BLK], jnp.where(lane // 16 == 2 * c + 1, lse[BLK:], lse_all))
        for sl, p, v in zip(chunks, probs, vals):
            o2 = _dot(p, v)
            o_ref[:, sl] = jnp.where(lo, o2[:BLK], o2[BLK:])
        lse_ref[...] = lse_all

    prev = lambda n: jnp.maximum(n - 1, 0)
    col = _qkv_col(d, gi)
    return _carrier_call(
        body, (qkv,) * 5, out_shape=(SDS((L, d * 512), F32), SDS((L, d * 128), F32)), grid=(d, nb),
        in_specs=[pl.BlockSpec((BLK, 512), lambda r, n: (n, col(0, r))),
                  pl.BlockSpec((BLK, 512), lambda r, n: (n, col(1, r))),
                  pl.BlockSpec((BLK, 512), lambda r, n: (prev(n), col(1, r))),
                  pl.BlockSpec((BLK, 512), lambda r, n: (n, col(2, r))),
                  pl.BlockSpec((BLK, 512), lambda r, n: (prev(n), col(2, r)))],
        out_specs=(pl.BlockSpec((BLK, 512), lambda r, n: (n, r)),
                   pl.BlockSpec((BLK, 128), lambda r, n: (n, r))),
        sem=("parallel", "arbitrary"), name=f"attn_fwd_g{gi}", exchanges=exchanges)


def _qkv_to_sub(proj, d, gi):
    S = proj.shape[0]
    tm = 512
    n = tm // d

    def body(q_ref, k_ref, v_ref, o_ref, scr):
        for t, ref in enumerate((q_ref, k_ref, v_ref)):
            for c in range(4):
                scr[c] = ref[:, c * 128:(c + 1) * 128].astype(F32)
            for r in range(d):
                for c in range(4):
                    col = (3 * r + t) * 512 + c * 128
                    o_ref[:, col:col + 128] = scr[c, pl.ds(r, n, stride=d), :].astype(BF16)

    return pl.pallas_call(
        body, out_shape=SDS((S // d, d * 1536), BF16), grid=(S // tm,),
        in_specs=[pl.BlockSpec((tm, 512), lambda i, b=b: (i, b + gi)) for b in (QA_B, KA_B, VA_B)],
        out_specs=pl.BlockSpec((n, d * 1536), lambda i: (i, 0)),
        scratch_shapes=[pltpu.VMEM((4, tm, 128), F32)],
        compiler_params=_cparams("parallel"), name=f"qkv_to_sub_g{gi}")(proj, proj, proj)


def _attn_merge(os_, lses):
    S = os_[0].shape[0]
    tm = 512

    def body(o0, o1, o2, l0, l1, l2, att_ref, lt_ref, so1, so2, sl1, sl2):
        lo = lax.broadcasted_iota(jnp.int32, (tm, 128), 1) < 64

        def natural(ref, d, scr, width):
            nch = width // 128
            if d == 1:
                return [ref[:, c * 128:(c + 1) * 128] for c in range(nch)]
            for r in range(d):
                for c in range(nch):
                    scr[c, pl.ds(r, tm // d, stride=d), :] = ref[:, r * width + c * 128:r * width + (c + 1) * 128]
            return [scr[c] for c in range(nch)]

        ls = [natural(l, d, s, 128)[0] for l, d, s in zip((l0, l1, l2), DILATIONS, (None, sl1, sl2))]
        m = jnp.maximum(jnp.maximum(ls[0], ls[1]), ls[2])
        es = [jnp.exp(v - m) for v in ls]
        z = es[0] + es[1] + es[2]
        lt_ref[...] = m + jnp.log(z)
        ws = [e / z for e in es]
        o_nat = [natural(o, d, s, 512) for o, d, s in zip((o0, o1, o2), DILATIONS, (None, so1, so2))]
        for c in range(4):
            acc = jnp.zeros((tm, 128), F32)
            for g in range(3):
                w_lo = jnp.broadcast_to(ws[g][:, 32 * c:32 * c + 1], (tm, 128))
                w_hi = jnp.broadcast_to(ws[g][:, 32 * c + 16:32 * c + 17], (tm, 128))
                acc = acc + jnp.where(lo, w_lo, w_hi) * o_nat[g][c]
            att_ref[:, c * 128:(c + 1) * 128] = acc.astype(BF16)

    sub = lambda w: [pl.BlockSpec((tm // d, d * w), lambda i: (i, 0)) for d in DILATIONS]
    return pl.pallas_call(
        body, out_shape=(SDS((S, 512), BF16), SDS((S, 128), F32)), grid=(S // tm,),
        in_specs=sub(512) + sub(128),
        out_specs=(pl.BlockSpec((tm, 512), lambda i: (i, 0)), pl.BlockSpec((tm, 128), lambda i: (i, 0))),
        scratch_shapes=[pltpu.VMEM((4, tm, 128), F32), pltpu.VMEM((4, tm, 128), F32),
                        pltpu.VMEM((1, tm, 128), F32), pltpu.VMEM((1, tm, 128), F32)],
        compiler_params=_cparams("parallel"), name="attn_merge")(*os_, *lses)


def _assemble_dproj(att_grads, dproj):
    S = dproj.shape[0]
    tm = 256

    def body(*refs):
        a = [refs[3 * t:3 * t + 3] for t in range(3)]
        dp_prev, o_ref, scr = refs[9:]
        for t in range(3):
            for g, d in enumerate(DILATIONS):
                base = (3 * t + g) * COLB
                if d == 1:
                    o_ref[:, base:base + COLB] = a[t][g][...]
                    continue
                for c in range(4):
                    for r in range(d):
                        scr[c, pl.ds(r, tm // d, stride=d), :] = a[t][g][:, r * 512 + c * 128:r * 512 + (c + 1) * 128].astype(F32)
                    o_ref[:, base + c * 128:base + (c + 1) * 128] = scr[c].astype(BF16)

    sub = [pl.BlockSpec((tm // d, d * 512), lambda i: (i, 0)) for d in DILATIONS]
    flat = [att_grads[t][g] for t in range(3) for g in range(3)]
    return pl.pallas_call(
        body, out_shape=SDS((S, PROJ_W), BF16), grid=(S // tm,),
        in_specs=sub * 3 + [ANY], out_specs=pl.BlockSpec((tm, 9 * COLB), lambda i: (i, 0)),
        scratch_shapes=[pltpu.VMEM((4, tm, 128), F32)], input_output_aliases={9: 0},
        compiler_params=_cparams("parallel"), name="assemble_dproj")(*flat, dproj)


def _ret_fwd(proj, consts, exchanges=()):
    S = proj.shape[0]
    nc = S // BLK
    dmask, zeta, xi, dec = consts

    def body(q_ref, k_ref, v0_ref, v1_ref, g0_ref, g1_ref, dm_ref, z_ref, x_ref, dec_ref,
             y_ref, rn_ref, rs_ref, st_ref, R):
        @pl.when(pl.program_id(0) == 0)
        def _():
            R[...] = jnp.zeros_like(R)

        lane16 = lax.broadcasted_iota(jnp.int32, (BLK, 128), 1) // 16
        rs_all = jnp.zeros((BLK, 128), F32)
        first = []
        for h in range(RET_HEADS):
            hs = slice(h * 128, (h + 1) * 128)
            q, k = q_ref[:, hs], k_ref[:, hs]
            v = (v0_ref if h < 2 else v1_ref)[:, (h % 2) * 256:(h % 2 + 1) * 256]
            Rb = R[h].astype(BF16)
            st_ref[h] = Rb
            kz = (k.astype(F32) * z_ref[h]).astype(BF16)
            first.append((v, _dot_nt(q, k), _dot((q.astype(F32) * x_ref[h]).astype(BF16), Rb), _dot_tn(kz, v)))
        masked = [(s * dm_ref[h]).astype(BF16) for h, (_, s, _, _) in enumerate(first)]
        for h in range(RET_HEADS):
            vs = slice((h % 2) * 256, (h % 2 + 1) * 256)
            os_ = slice(h * 256, (h + 1) * 256)
            v, _, cross, kv = first[h]
            o = _dot(masked[h], v) + cross
            R[h] = R[h] * dec_ref[h, 0:1, :] + kv
            mu = jnp.mean(o, axis=-1, keepdims=True)
            oc = o - mu
            rstd = lax.rsqrt(jnp.mean(oc * oc, axis=-1, keepdims=True) + NORM_EPS)
            rn = oc * rstd
            gr = (g0_ref if h < 2 else g1_ref)[:, vs].astype(F32)
            y_ref[:, os_] = (rn * gr * _sigmoid(gr)).astype(BF16)
            rn_ref[:, os_] = rn.astype(BF16)
            rs_all = jnp.where(lane16 == h, rstd, rs_all)
        rs_ref[...] = rs_all

    cst = lambda shape: pl.BlockSpec(shape, lambda c: (0, 0, 0))
    blk = lambda j: pl.BlockSpec((BLK, 512), lambda c: (c, j))
    return _carrier_call(
        body, (proj, proj, proj, proj, proj, proj, dmask, zeta, xi, dec),
        out_shape=(SDS((S, 1024), BF16), SDS((S, 1024), BF16), SDS((S, 128), F32), SDS((RET_HEADS, nc, BLK, 256), BF16)),
        grid=(nc,),
        in_specs=[blk(QR_B), blk(KR_B), blk(11), blk(12), blk(13), blk(14),
                  cst((RET_HEADS, BLK, BLK)), cst((RET_HEADS, BLK, 128)), cst((RET_HEADS, BLK, 128)), cst((RET_HEADS, 8, 256))],
        out_specs=(pl.BlockSpec((BLK, 1024), lambda c: (c, 0)), pl.BlockSpec((BLK, 1024), lambda c: (c, 0)),
                   pl.BlockSpec((BLK, 128), lambda c: (c, 0)),
                   pl.BlockSpec((RET_HEADS, None, BLK, 256), lambda c: (0, c, 0, 0))),
        scratch_shapes=[pltpu.VMEM((RET_HEADS, BLK, 256), F32)],
        sem=("arbitrary",), name="ret_fwd", exchanges=exchanges)


def _mix_out(att, yrin, proj, wa, wr, wo, x, g2, exchanges=()):
    S = x.shape[0]
    tm = 512
    gate0 = 15 * COLB

    def body(a_ref, y_ref, ga_ref, gr_ref, wa_ref, wr_ref, wo_ref, x_ref, g_ref, m_ref, ya_ref, yr_ref, x1_ref, h2_ref):
        pieces = _row_pieces(tm, 256)
        branches = [(_dot(a_ref[rows, :], wa_ref[...]), _dot(y_ref[rows, :], wr_ref[...])) for rows in pieces]
        merged = []
        for rows, (ya, yr) in zip(pieces, branches):
            m = (_sigmoid(ga_ref[rows, :].astype(F32)) * ya + _sigmoid(gr_ref[rows, :].astype(F32)) * yr).astype(BF16)
            m_ref[rows, :] = m
            ya_ref[rows, :] = ya.astype(BF16)
            yr_ref[rows, :] = yr.astype(BF16)
            merged.append(m)
        for rows, m in zip(pieces, merged):
            x1 = x_ref[rows, :] + _dot(m, wo_ref[...])
            x1_ref[rows, :] = x1
            r = lax.rsqrt(jnp.mean(x1 * x1, axis=-1, keepdims=True) + NORM_EPS)
            h2_ref[rows, :] = (x1 * r * g_ref[...]).astype(BF16)

    row = lambda w: pl.BlockSpec((tm, w), lambda i: (i, 0))
    cols = lambda c0: pl.BlockSpec((pl.Element(tm), pl.Element(D_MODEL)), lambda i: (i * tm, c0))
    resident = lambda r, c: pl.BlockSpec((r, c), lambda i: (0, 0), pipeline_mode=pl.Buffered(1))
    return _carrier_call(
        body, (att, yrin, proj, proj, wa, wr, wo, x, g2),
        out_shape=(SDS((S, D_MODEL), BF16),) * 3 + (SDS((S, D_MODEL), F32), SDS((S, D_MODEL), BF16)), grid=(S // tm,),
        in_specs=[row(512), row(D_MODEL), cols(gate0), cols(gate0 + D_MODEL), resident(512, D_MODEL),
                  resident(D_MODEL, D_MODEL), resident(D_MODEL, D_MODEL), row(D_MODEL),
                  pl.BlockSpec((1, D_MODEL), lambda i: (0, 0))],
        out_specs=(row(D_MODEL),) * 5, sem=("parallel",), name="mix_out", exchanges=exchanges)


def _ffn_up(h2, wg, wu, exchanges=()):
    S = h2.shape[0]
    tm = min(S, 2048)

    def body(h_ref, wg_ref, wu_ref, g_ref, u_ref, a_ref):
        for rows in _row_pieces(tm):
            hv = h_ref[rows, :]
            g = _dot(hv, wg_ref[...])
            u = _dot(hv, wu_ref[...])
            g_ref[rows, :] = g.astype(BF16)
            u_ref[rows, :] = u.astype(BF16)
            a_ref[rows, :] = (g * _sigmoid(g) * u).astype(BF16)

    wspec = pl.BlockSpec((None, D_MODEL, HID_S), lambda i, s: (s, 0, 0))
    ospec = pl.BlockSpec((None, tm, HID_S), lambda i, s: (s, i, 0))
    return _carrier_call(
        body, (h2, wg, wu), out_shape=(SDS((N_SHARD, S, HID_S), BF16),) * 3, grid=(S // tm, N_SHARD),
        in_specs=[pl.BlockSpec((tm, D_MODEL), lambda i, s: (i, 0)), wspec, wspec],
        out_specs=(ospec, ospec, ospec),
        sem=("parallel", "arbitrary"), name="ffn_up", exchanges=exchanges)


def _ffn_down_loss(act, wd, x1, g3, tgt):
    S = x1.shape[0]
    tm = 512

    def body(a_ref, w_ref, x_ref, g_ref, t_ref, dx_ref, dxb_ref, dg_ref, ls_ref):
        @pl.when(pl.program_id(0) == 0)
        def _():
            dg_ref[...] = jnp.zeros_like(dg_ref)
            ls_ref[...] = jnp.zeros_like(ls_ref)

        g = g_ref[...]
        for rows in _row_pieces(tm, 256):
            y = _dot(a_ref[0, rows, :], w_ref[0])
            for s in range(1, N_SHARD):
                y = y + _dot(a_ref[s, rows, :], w_ref[s])
            x2 = x_ref[rows, :] + y
            r = lax.rsqrt(jnp.mean(x2 * x2, axis=-1, keepdims=True) + NORM_EPS)
            xh = x2 * r
            err = xh * g - t_ref[rows, :]
            ls_ref[...] += jnp.sum(jnp.sum(err * err, axis=-1, keepdims=True), axis=0, keepdims=True) * (0.5 / D_MODEL)
            dy = err * (1.0 / D_MODEL)
            dg_ref[...] += jnp.sum(dy * xh, axis=0, keepdims=True)
            dxh = dy * g
            dx = r * (dxh - xh * jnp.mean(dxh * xh, axis=-1, keepdims=True))
            dx_ref[rows, :] = dx
            dxb_ref[rows, :] = dx.astype(BF16)

    row = pl.BlockSpec((tm, D_MODEL), lambda i: (i, 0))
    vec = pl.BlockSpec((1, D_MODEL), lambda i: (0, 0))
    return pl.pallas_call(
        body, out_shape=(SDS((S, D_MODEL), F32), SDS((S, D_MODEL), BF16), SDS((1, D_MODEL), F32), SDS((8, 128), F32)),
        grid=(S // tm,),
        in_specs=[pl.BlockSpec((N_SHARD, tm, HID_S), lambda i: (0, i, 0)),
                  pl.BlockSpec((N_SHARD, HID_S, D_MODEL), lambda i: (0, 0, 0), pipeline_mode=pl.Buffered(1)),
                  row, vec, row],
        out_specs=(row, row, vec, pl.BlockSpec((8, 128), lambda i: (0, 0))),
        compiler_params=_cparams("arbitrary"), name="ffn_down_loss")(act, wd, x1, g3, tgt)


def _ffn_bwd(dx2b, dx2, wd, wg, wu, gte, up, x1, g2):
    S = x1.shape[0]
    tm = 256

    def body(d_ref, dx2_ref, wd_ref, wg_ref, wu_ref, g_ref, u_ref, x_ref, gn_ref,
             dg_ref, du_ref, dx_ref, dxb_ref, dgn_ref):
        @pl.when(pl.program_id(0) == 0)
        def _():
            dgn_ref[...] = jnp.zeros_like(dgn_ref)

        d = d_ref[...]
        dacts = [_dot_nt(d, wd_ref[s]) for s in range(N_SHARD)]
        dgs, dus = [], []
        for s, da in enumerate(dacts):
            g = g_ref[s].astype(F32)
            sg = _sigmoid(g)
            dgs.append((da * u_ref[s].astype(F32) * sg * (1.0 + g * (1.0 - sg))).astype(BF16))
            dus.append((da * g * sg).astype(BF16))
            dg_ref[s] = dgs[s]
            du_ref[s] = dus[s]
        dh = _dot_nt(dgs[0], wg_ref[0]) + _dot_nt(dus[0], wu_ref[0])
        for s in range(1, N_SHARD):
            dh = dh + _dot_nt(dgs[s], wg_ref[s]) + _dot_nt(dus[s], wu_ref[s])
        xv = x_ref[...]
        r = lax.rsqrt(jnp.mean(xv * xv, axis=-1, keepdims=True) + NORM_EPS)
        xh = xv * r
        dgn_ref[...] += jnp.sum(dh * xh, axis=0, keepdims=True)
        dxh = dh * gn_ref[...]
        dx = dx2_ref[...] + r * (dxh - xh * jnp.mean(dxh * xh, axis=-1, keepdims=True))
        dx_ref[...] = dx
        dxb_ref[...] = dx.astype(BF16)

    row = pl.BlockSpec((tm, D_MODEL), lambda i: (i, 0))
    vec = pl.BlockSpec((1, D_MODEL), lambda i: (0, 0))
    aspec = pl.BlockSpec((N_SHARD, tm, HID_S), lambda i: (0, i, 0))
    resident = lambda shape: pl.BlockSpec(shape, lambda i: (0, 0, 0), pipeline_mode=pl.Buffered(1))
    return pl.pallas_call(
        body,
        out_shape=(SDS((N_SHARD, S, HID_S), BF16), SDS((N_SHARD, S, HID_S), BF16),
                   SDS((S, D_MODEL), F32), SDS((S, D_MODEL), BF16), SDS((1, D_MODEL), F32)),
        grid=(S // tm,),
        in_specs=[row, row, resident((N_SHARD, HID_S, D_MODEL)), resident((N_SHARD, D_MODEL, HID_S)),
                  resident((N_SHARD, D_MODEL, HID_S)), aspec, aspec, row, vec],
        out_specs=(aspec, aspec, row, row, vec),
        compiler_params=_cparams("arbitrary"), name="ffn_bwd")(dx2b, dx2, wd, wg, wu, gte, up, x1, g2)


def _wgrad(name, a, b, a_spec, b_spec, out_shape, out_spec, n_par, S):
    tk = min(S, 4096)

    def body(a_ref, b_ref, o_ref):
        @pl.when(pl.program_id(1) == 0)
        def _():
            o_ref[...] = jnp.zeros_like(o_ref)

        o_ref[...] += _dot_tn(a_ref[...], b_ref[...])

    return pl.pallas_call(
        body, out_shape=SDS(out_shape, F32), grid=(n_par, S // tk),
        in_specs=[a_spec(tk), b_spec(tk)], out_specs=out_spec,
        compiler_params=_cparams("parallel", "arbitrary"), name=name)(a, b)


def _mix_bwd(dx1b, wo, proj, ya, yr, wa, wr, att, exchanges=()):
    S = dx1b.shape[0]
    tm = 512
    gate0 = 15 * COLB

    def body(d_ref, wo_ref, ga_ref, gr_ref, ya_ref, yr_ref, wa_ref, wr_ref, att_ref,
             dya_ref, dyr_ref, dp_ref, datt_ref, rho_ref, dyi_ref):
        pieces = _row_pieces(tm, 256)
        dms = [_dot_nt(d_ref[rows, :], wo_ref[...]) for rows in pieces]
        branch = []
        for rows, dm in zip(pieces, dms):
            sa = _sigmoid(ga_ref[rows, :].astype(F32))
            sr = _sigmoid(gr_ref[rows, :].astype(F32))
            dya, dyr = (dm * sa).astype(BF16), (dm * sr).astype(BF16)
            dya_ref[rows, :] = dya
            dyr_ref[rows, :] = dyr
            dp_ref[rows, 0:D_MODEL] = (dm * ya_ref[rows, :].astype(F32) * sa * (1.0 - sa)).astype(BF16)
            dp_ref[rows, D_MODEL:2 * D_MODEL] = (dm * yr_ref[rows, :].astype(F32) * sr * (1.0 - sr)).astype(BF16)
            branch.append((dya, dyr))
        lane = lax.broadcasted_iota(jnp.int32, (256, 128), 1)
        lo = lane < 64
        for rows, (dya, dyr) in zip(pieces, branch):
            datt = _dot_nt(dya, wa_ref[...])
            datt_ref[rows, :] = datt.astype(BF16)
            dyi_ref[rows, :] = _dot_nt(dyr, wr_ref[...]).astype(BF16)
            prod = datt * att_ref[rows, :].astype(F32)
            rho = jnp.zeros((256, 128), F32)
            for c in range(4):
                pc = prod[:, c * 128:(c + 1) * 128]
                tot = jnp.sum(pc, axis=-1, keepdims=True)
                low = jnp.sum(jnp.where(lo, pc, 0.0), axis=-1, keepdims=True)
                rho = jnp.where(lane // 16 == 2 * c, low, jnp.where(lane // 16 == 2 * c + 1, tot - low, rho))
            rho_ref[rows, :] = rho

    row = lambda w: pl.BlockSpec((tm, w), lambda i: (i, 0))
    cols = lambda c0, w: pl.BlockSpec((pl.Element(tm), pl.Element(w)), lambda i: (i * tm, c0))
    resident = lambda r, c: pl.BlockSpec((r, c), lambda i: (0, 0), pipeline_mode=pl.Buffered(1))
    return _carrier_call(
        body, (dx1b, wo, proj, proj, ya, yr, wa, wr, att),
        out_shape=(SDS((S, D_MODEL), BF16), SDS((S, D_MODEL), BF16), SDS((S, PROJ_W), BF16),
                   SDS((S, 512), BF16), SDS((S, 128), F32), SDS((S, D_MODEL), BF16)),
        grid=(S // tm,),
        in_specs=[row(D_MODEL), resident(D_MODEL, D_MODEL), cols(gate0, D_MODEL), cols(gate0 + D_MODEL, D_MODEL),
                  row(D_MODEL), row(D_MODEL), resident(512, D_MODEL), resident(D_MODEL, D_MODEL), row(512)],
        out_specs=(row(D_MODEL), row(D_MODEL), cols(gate0, 2 * D_MODEL), row(512), row(128), row(D_MODEL)),
        sem=("parallel",), name="mix_bwd", exchanges=exchanges)


def _attn_bwd(qkv, datt, lse, rho, rtab, d, gi, exchanges=()):
    L = qkv.shape[0]
    nb = L // BLK
    T = d * nb

    def body(q_ref, kc_ref, kp_ref, vc_ref, vp_ref, do_ref, lse_ref, rho_ref, tq_ref, tk_ref,
             dq_ref, dk_ref, dv_ref, ck, cv):
        t = pl.program_id(0)
        n = jnp.minimum(t, T - 1) % nb

        @pl.when(t == 0)
        def _():
            ck[...] = jnp.zeros_like(ck)
            cv[...] = jnp.zeros_like(cv)

        def store_rot(ref, val, t_ref, c):
            sl = slice(c * 128, (c + 1) * 128)
            ref[:, sl] = _unrot(val, t_ref[0], t_ref[1], t_ref[2], 32).astype(BF16)

        @pl.when(t < T)
        def _():
            mask = _band_mask(n)
            mask2 = jnp.concatenate([mask, mask], axis=0)
            lo = lax.broadcasted_iota(jnp.int32, (BLK, 128), 1) < 64

            def stacked(a):
                return jnp.concatenate([jnp.where(lo, a, jnp.zeros_like(a)), jnp.where(lo, jnp.zeros_like(a), a)], axis=0)

            def head_cols(ref, c):
                return jnp.concatenate([jnp.broadcast_to(ref[:, 32 * c:32 * c + 1], (BLK, 2 * BLK)),
                                        jnp.broadcast_to(ref[:, 32 * c + 16:32 * c + 17], (BLK, 2 * BLK))], axis=0)

            ops, raw = [], []
            for c in range(4):
                sl = slice(c * 128, (c + 1) * 128)
                q2, do2 = stacked(q_ref[:, sl]), stacked(do_ref[:, sl])
                k = jnp.concatenate([kp_ref[:, sl], kc_ref[:, sl]], axis=0)
                v = jnp.concatenate([vp_ref[:, sl], vc_ref[:, sl]], axis=0)
                ops.append((q2, do2, k))
                raw.append((_dot_nt(q2, k), _dot_nt(do2, v)))
            grads = []
            for c, (s, dp) in enumerate(raw):
                p = jnp.where(mask2, jnp.exp(s * 0.125 - head_cols(lse_ref, c)), 0.0)
                grads.append(((p * (dp - head_cols(rho_ref, c)) * 0.125).astype(BF16), p.astype(BF16)))
            for c, ((q2, do2, k), (ds, pb)) in enumerate(zip(ops, grads)):
                sl = slice(c * 128, (c + 1) * 128)
                dq2 = _dot(ds, k)
                dq_c = jnp.where(lo, dq2[:BLK], dq2[BLK:])
                dk_c = _dot_tn(ds, q2)
                dv_c = _dot_tn(pb, do2)
                store_rot(dq_ref, dq_c, tq_ref, c)
                store_rot(dk_ref, ck[:, sl] + dk_c[:BLK], tk_ref, c)
                dv_ref[:, sl] = (cv[:, sl] + dv_c[:BLK]).astype(BF16)
                ck[:, sl] = dk_c[BLK:]
                cv[:, sl] = dv_c[BLK:]

        @pl.when(t == T)
        def _():
            for c in range(4):
                sl = slice(c * 128, (c + 1) * 128)
                store_rot(dk_ref, ck[:, sl], tk_ref, c)
            dv_ref[...] = cv[...].astype(BF16)

    blk_of = lambda t: (jnp.minimum(t, T - 1) % nb, jnp.minimum(t, T - 1) // nb)
    cur = lambda t: blk_of(t)
    prev = lambda t: (jnp.maximum(blk_of(t)[0] - 1, 0), blk_of(t)[1])
    fin = lambda t: blk_of(jnp.maximum(t - 1, 0))
    col = _qkv_col(d, gi)
    qkv_spec = lambda kind, which: pl.BlockSpec((BLK, 512), lambda t: (which(t)[0], col(kind, which(t)[1])))
    row_spec = lambda w, which: pl.BlockSpec((BLK, w), lambda t: which(t))
    tab_spec = lambda which: pl.BlockSpec((3, BLK, 128), lambda t: (0, *which(t)))
    return _carrier_call(
        body, (qkv, qkv, qkv, qkv, qkv, datt, lse, rho, rtab, rtab),
        out_shape=(SDS((L, d * 512), BF16),) * 3, grid=(T + 1,),
        in_specs=[qkv_spec(0, cur), qkv_spec(1, cur), qkv_spec(1, prev), qkv_spec(2, cur), qkv_spec(2, prev),
                  row_spec(512, cur), row_spec(128, cur), row_spec(128, cur), tab_spec(cur), tab_spec(fin)],
        out_specs=(row_spec(512, cur), row_spec(512, fin), row_spec(512, fin)),
        scratch_shapes=[pltpu.VMEM((BLK, 512), F32), pltpu.VMEM((BLK, 512), F32)],
        sem=("arbitrary",), name=f"attn_bwd_g{gi}", exchanges=exchanges)


def _ret_bwd(proj, rn, rstd, dyrin, states, tab, consts, dproj, exchanges=()):
    S = proj.shape[0]
    nc = S // BLK
    dmask, zeta, xi, dec = consts

    def body(q_ref, k_ref, v0_ref, v1_ref, g0_ref, g1_ref, rn_ref, rs_ref, dy_ref, st_ref, tq_ref, tk_ref,
             dm_ref, z_ref, x_ref, dec_ref, dp_prev, dp_ref, dR):
        dq_ref, dk_ref = dp_ref.at[:, 0:512], dp_ref.at[:, 512:1024]
        dv_ref, dgr_ref = dp_ref.at[:, 1024:2048], dp_ref.at[:, 2048:3072]

        @pl.when(pl.program_id(0) == 0)
        def _():
            dR[...] = jnp.zeros_like(dR)

        dobs = []
        for h in range(RET_HEADS):
            vs = slice((h % 2) * 256, (h % 2 + 1) * 256)
            os_ = slice(h * 256, (h + 1) * 256)
            gr = (g0_ref if h < 2 else g1_ref)[:, vs].astype(F32)
            sg = _sigmoid(gr)
            rn_v = rn_ref[:, os_].astype(F32)
            dyi = dy_ref[:, os_].astype(F32)
            dgr_ref[:, os_] = (dyi * rn_v * sg * (1.0 + gr * (1.0 - sg))).astype(BF16)
            drn = dyi * gr * sg
            rstd = jnp.broadcast_to(rs_ref[:, 16 * h:16 * h + 1], (BLK, 256))
            do = rstd * (drn - jnp.mean(drn, axis=-1, keepdims=True) - rn_v * jnp.mean(drn * rn_v, axis=-1, keepdims=True))
            dobs.append(do.astype(BF16))
        first = []
        for h in range(RET_HEADS):
            hs = slice(h * 128, (h + 1) * 128)
            q, k = q_ref[:, hs], k_ref[:, hs]
            v = (v0_ref if h < 2 else v1_ref)[:, (h % 2) * 256:(h % 2 + 1) * 256]
            dob, dRb = dobs[h], dR[h].astype(BF16)
            kz = (k.astype(F32) * z_ref[h]).astype(BF16)
            qx = (q.astype(F32) * x_ref[h]).astype(BF16)
            first.append((q, k, _dot_nt(q, k), _dot_nt(dob, v), _dot(kz, dRb), _dot_nt(dob, st_ref[h]),
                          _dot_nt(v, dRb), _dot_tn(qx, dob)))
        masked = [((s * dm_ref[h]).astype(BF16), (dsr * dm_ref[h]).astype(BF16))
                  for h, (_, _, s, dsr, _, _, _, _) in enumerate(first)]
        for h in range(RET_HEADS):
            hs = slice(h * 128, (h + 1) * 128)
            os_ = slice(h * 256, (h + 1) * 256)
            q, k, _, _, dv_state, dq_state, dk_state, dr_new = first[h]
            sD, dS = masked[h]
            dv_ref[:, os_] = (_dot_tn(sD, dobs[h]) + dv_state).astype(BF16)
            dq = _dot(dS, k) + dq_state * x_ref[h]
            dk = _dot_tn(dS, q) + dk_state * z_ref[h]
            dR[h] = dR[h] * dec_ref[h, 0:1, :] + dr_new
            dq_ref[:, hs] = _unrot(dq, tq_ref[0], tq_ref[1], tq_ref[2], 1).astype(BF16)
            dk_ref[:, hs] = _unrot(dk, tk_ref[0], tk_ref[1], tk_ref[2], 1).astype(BF16)

    rc = lambda c: nc - 1 - c
    cst = lambda shape: pl.BlockSpec(shape, lambda c: (0, 0, 0))
    blk = lambda j: pl.BlockSpec((BLK, 512), lambda c: (rc(c), j))
    row = lambda w: pl.BlockSpec((BLK, w), lambda c: (rc(c), 0))
    (dproj,), xres = _carrier_call(
        body, (proj, proj, proj, proj, proj, proj, rn, rstd, dyrin, states, tab, tab, dmask, zeta, xi, dec, dproj),
        out_shape=(SDS((S, PROJ_W), BF16),), grid=(nc,),
        in_specs=[blk(QR_B), blk(KR_B), blk(11), blk(12), blk(13), blk(14), row(1024), row(128), row(1024),
                  pl.BlockSpec((RET_HEADS, None, BLK, 256), lambda c: (0, rc(c), 0, 0)),
                  pl.BlockSpec((None, 3, BLK, 128), lambda c: (1, 0, rc(c), 0)),
                  pl.BlockSpec((None, 3, BLK, 128), lambda c: (2, 0, rc(c), 0)),
                  cst((RET_HEADS, BLK, BLK)), cst((RET_HEADS, BLK, 128)), cst((RET_HEADS, BLK, 128)), cst((RET_HEADS, 8, 256)),
                  ANY],
        out_specs=(pl.BlockSpec((pl.Element(BLK), pl.Element(6 * COLB)), lambda c: (rc(c) * BLK, QR_B * COLB)),),
        scratch_shapes=[pltpu.VMEM((RET_HEADS, BLK, 256), F32)],
        sem=("arbitrary",), name="ret_bwd", exchanges=exchanges, in_out_aliases={16: 0})
    return dproj, xres


def _wgrad_in_half(ht, dproj, sidx, kept, exchanges=()):
    S = dproj.shape[0]
    tk = 2048
    half = (lambda sx: sx[4]) if kept else (lambda sx: 1 - sx[4])

    def body(a_ref, b_ref, o_ref):
        @pl.when(pl.program_id(1) == 0)
        def _():
            o_ref[...] = jnp.zeros_like(o_ref)

        o_ref[...] += _dot(a_ref[...], b_ref[...])

    (g,), xres = _carrier_call(
        body, (ht, dproj), out_shape=(SDS((D_MODEL // 2, PROJ_W), F32),), grid=(N_SHARD, S // tk),
        in_specs=[pl.BlockSpec((D_MODEL // 2, tk), lambda s, k, sx: (half(sx), k)),
                  pl.BlockSpec((tk, W_IN_S), lambda s, k, sx: (k, s))],
        out_specs=(pl.BlockSpec((D_MODEL // 2, W_IN_S), lambda s, k, sx: (0, s)),),
        sem=("parallel", "arbitrary"), name="wgrad_in_kept" if kept else "wgrad_in_sent", exchanges=exchanges,
        prefetch=sidx)
    return g, xres


def _in_proj_bwd(dproj, w_in, x, g1, dx1, exchanges=()):
    S = x.shape[0]
    tm = 1024

    def body(d_ref, w_ref, x_ref, g_ref, dx1_ref, dx_ref, dgn_ref, acc):
        i, s = pl.program_id(0), pl.program_id(1)

        @pl.when(s == 0)
        def _():
            acc[...] = jnp.zeros_like(acc)

        @pl.when((i == 0) & (s == 0))
        def _():
            dgn_ref[...] = jnp.zeros_like(dgn_ref)

        acc[...] += _dot_nt(d_ref[...], w_ref[...])

        @pl.when(s == N_SHARD - 1)
        def _():
            xv = x_ref[...]
            r = lax.rsqrt(jnp.mean(xv * xv, axis=-1, keepdims=True) + NORM_EPS)
            xh = xv * r
            dh = acc[...]
            dgn_ref[...] += jnp.sum(dh * xh, axis=0, keepdims=True)
            dxh = dh * g_ref[...]
            dx_ref[...] = dx1_ref[...] + r * (dxh - xh * jnp.mean(dxh * xh, axis=-1, keepdims=True))

    row = pl.BlockSpec((tm, D_MODEL), lambda i, s: (i, 0))
    vec = pl.BlockSpec((1, D_MODEL), lambda i, s: (0, 0))
    (gx, dg), xres = _carrier_call(
        body, (dproj, w_in, x, g1, dx1),
        out_shape=(SDS((S, D_MODEL), F32), SDS((1, D_MODEL), F32)), grid=(S // tm, N_SHARD),
        in_specs=[pl.BlockSpec((tm, W_IN_S), lambda i, s: (i, s)),
                  pl.BlockSpec((D_MODEL, W_IN_S), lambda i, s: (0, s)), row, vec, row],
        out_specs=(row, vec), scratch_shapes=[pltpu.VMEM((tm, D_MODEL), F32)],
        sem=("arbitrary", "arbitrary"), name="in_proj_bwd", exchanges=exchanges)
    return gx, dg, xres


def _sub_view(a, d):
    S, W = a.shape
    return a.reshape(S // d, d * W)


def _step(x, tgt, g1, g2, g3, comm):
    S = x.shape[0]
    tab_np = _tables(S)
    tab = jnp.asarray(tab_np)
    consts = _ret_consts()

    (h, ht, *casts), xres = _rms_fwd(x, g1, comm.to_cast(), comm.carry("rms_fwd"))
    comm.cast_done(casts)
    comm.took("rms_fwd", xres)
    w_in = comm.weight(0)
    proj, xres = _in_proj(h, w_in, tab, comm.carry("in_proj"))
    comm.took("in_proj", xres)
    qkvs, o_parts, lse_parts = [], [], []
    for gi, d in enumerate(DILATIONS):
        qkv = proj if d == 1 else _qkv_to_sub(proj, d, gi)
        (o_g, lse_g), xres = _attn_fwd(qkv, d, gi, comm.carry(f"attn_fwd_g{gi}"))
        comm.took(f"attn_fwd_g{gi}", xres)
        qkvs.append(qkv)
        o_parts.append(o_g)
        lse_parts.append(lse_g)
    att, lse_tot = _attn_merge(o_parts, lse_parts)
    (yrin, rn, rstd, states), xres = _ret_fwd(proj, consts, comm.carry("ret_fwd"))
    comm.took("ret_fwd", xres)
    wa, wr, wo = comm.weight(1), comm.weight(2), comm.weight(3)
    (merged, ya, yr, x1, h2), xres = _mix_out(att, yrin, proj, wa, wr, wo, x, g2, comm.carry("mix_out"))
    comm.took("mix_out", xres)
    wg, wu = comm.weight(4), comm.weight(5)
    (gte, up, act), xres = _ffn_up(h2, wg, wu, comm.carry("ffn_up"))
    comm.took("ffn_up", xres)
    wd = comm.weight(6)
    dx2, dx2b, dg3, loss_p = _ffn_down_loss(act, wd, x1, g3, tgt)

    dgte, dup, dx1, dx1b, dg2 = _ffn_bwd(dx2b, dx2, wd, wg, wu, gte, up, x1, g2)
    tok3 = lambda w: (lambda tk: pl.BlockSpec((None, tk, w), lambda p, k: (p, k, 0)))
    tok2 = lambda w: (lambda tk: pl.BlockSpec((tk, w), lambda p, k: (k, 0)))
    g_d = _wgrad("wgrad_down", act, dx2b, tok3(HID_S), tok2(D_MODEL), (N_SHARD, HID_S, D_MODEL),
                 pl.BlockSpec((None, HID_S, D_MODEL), lambda p, k: (p, 0, 0)), N_SHARD, S)
    g_g = _wgrad("wgrad_gate", h2, dgte, tok2(D_MODEL), tok3(HID_S), (N_SHARD, D_MODEL, HID_S),
                 pl.BlockSpec((None, D_MODEL, HID_S), lambda p, k: (p, 0, 0)), N_SHARD, S)
    g_u = _wgrad("wgrad_up", h2, dup, tok2(D_MODEL), tok3(HID_S), (N_SHARD, D_MODEL, HID_S),
                 pl.BlockSpec((None, D_MODEL, HID_S), lambda p, k: (p, 0, 0)), N_SHARD, S)
    comm.grads({4: g_g, 5: g_u, 6: g_d})
    (dya, dyr, dproj, datt, rho, dyrin), xres = _mix_bwd(dx1b, wo, proj, ya, yr, wa, wr, att, comm.carry("mix_bwd"))
    comm.took("mix_bwd", xres)
    colblk = lambda w: (lambda tk: pl.BlockSpec((tk, w), lambda p, k: (k, p)))
    g_o = _wgrad("wgrad_out", merged, dx1b, colblk(256), tok2(D_MODEL), (D_MODEL, D_MODEL),
                 pl.BlockSpec((256, D_MODEL), lambda p, k: (p, 0)), 4, S)
    g_a = _wgrad("wgrad_attn", att, dya, tok2(512), colblk(512), (512, D_MODEL),
                 pl.BlockSpec((512, 512), lambda p, k: (0, p)), 2, S)
    g_r = _wgrad("wgrad_ret", yrin, dyr, colblk(256), tok2(D_MODEL), (D_MODEL, D_MODEL),
                 pl.BlockSpec((256, D_MODEL), lambda p, k: (p, 0)), 4, S)
    comm.grads({1: g_a, 2: g_r.reshape(N_SHARD, 256, D_MODEL), 3: g_o.reshape(N_SHARD, 256, D_MODEL)})
    dproj, xres = _ret_bwd(proj, rn, rstd, dyrin, states, tab, consts, dproj, comm.carry("ret_bwd"))
    comm.took("ret_bwd", xres)
    dqs, dks, dvs = [], [], []
    for gi, d in enumerate(DILATIONS):
        rtab = jnp.asarray(tab_np[0].reshape(3, S // d, d * 128))
        (dq, dk, dv), xres = _attn_bwd(qkvs[gi], _sub_view(datt, d), _sub_view(lse_tot, d), _sub_view(rho, d), rtab, d, gi,
                                       comm.carry(f"attn_bwd_g{gi}"))
        comm.took(f"attn_bwd_g{gi}", xres)
        dqs.append(dq)
        dks.append(dk)
        dvs.append(dv)
    dproj = _assemble_dproj((dqs, dks, dvs), dproj)
    g_sent, xres = _wgrad_in_half(ht, dproj, comm.sidx, False, comm.carry("wgrad_in_sent"))
    comm.took("wgrad_in_sent", xres)
    comm.grads({"in_sent": g_sent})
    g_kept, xres = _wgrad_in_half(ht, dproj, comm.sidx, True, comm.carry("wgrad_in_kept"))
    comm.grads({"in_kept": g_kept})
    comm.took("wgrad_in_kept", xres)
    grad_x, dg1, xres = _in_proj_bwd(dproj, w_in, x, g1, dx1, comm.carry("in_proj_bwd"))
    comm.took("in_proj_bwd", xres)
    return loss_p, grad_x, (dg1, dg2, dg3)


W_KINDS = ("col", "col", "lead", "lead", "lead", "lead", "lead")
W_SHARD = ((1024, W_IN_S), (512, 256), (256, 1024), (256, 1024), (1024, HID_S), (1024, HID_S), (HID_S, 1024))
N_W = len(W_KINDS)


def _full_shape(wi):
    R, C = W_SHARD[wi]
    return (R, N_SHARD * C) if W_KINDS[wi] == "col" else (N_SHARD, R, C)


def _view(ref, wi, s, half):
    R, C = W_SHARD[wi]
    rows = pl.ds(half * (R // 2), R // 2)
    if W_KINDS[wi] == "col":
        return ref.at[rows, pl.ds(pl.multiple_of(s * C, 128), C)]
    return ref.at[s, rows, :]


def _mesh_pos():
    x, y, c = lax.axis_index("x"), lax.axis_index("y"), lax.axis_index("c")
    chips = [(1 - x, y), (x, 1 - y), (1 - x, 1 - y)]
    return x, y, c, chips


def _cast_bf16(a):
    R, C = a.shape
    tr = R // 2 if R % 32 == 0 else R

    def body(a_ref, o_ref):
        o_ref[...] = a_ref[...].astype(BF16)

    spec = pl.BlockSpec((tr, C), lambda i: (i, 0))
    return pl.pallas_call(body, out_shape=SDS((R, C), BF16), grid=(R // tr,), in_specs=[spec], out_specs=spec,
                          compiler_params=_cparams("parallel"), name=f"cast_{R}x{C}")(a)


def _remote(send, recv, k, src, dst, to):
    return pltpu.make_async_remote_copy(src_ref=src, dst_ref=dst, send_sem=send.at[k], recv_sem=recv.at[k],
                                        device_id=to, device_id_type=MESH)


def _ex_gather_ring(wis, shards):
    n = len(wis)

    def build(sh, full, send, recv, loc):
        x, y, c, _ = _mesh_pos()
        s_me, sib = 2 * x + y, (x, y, 1 - c)
        xn, yn = (1 - x, y), (x, 1 - y)
        flip = lambda a, b: a + b - 2 * a * b
        via = (flip(x, 1 - c), flip(y, c))
        onto = (flip(x, c), flip(y, 1 - c))
        shard_of = lambda chip: 2 * chip[0] + chip[1]
        starts, waits, sent = [], [], []
        for i, wi in enumerate(wis):
            Rh = W_SHARD[wi][0] // 2
            for hf in range(2):
                cp = pltpu.make_async_copy(sh[i].at[pl.ds(hf * Rh, Rh), :], _view(full[i], wi, s_me, hf), loc.at[2 * i + hf])
                starts.append(cp)
                sent.append(cp.wait)
            for j, chip in enumerate((xn, yn)):
                cp = _remote(send, recv, 6 * i + j, sh[i].at[pl.ds(c * Rh, Rh), :], _view(full[i], wi, s_me, c), (*chip, c))
                starts.append(cp)
                sent.append(cp.wait_send)

        def pass_to_sibling(i, wi, k, s):
            mine = _view(full[i], wi, s, c)
            fw = _remote(send, recv, 6 * i + k, mine, mine, sib)
            waits.append(fw.start)
            sent.append(fw.wait_send)

        for i, wi in enumerate(wis):
            for j, chip in enumerate((xn, yn)):
                land = _view(full[i], wi, shard_of(chip), c)
                waits.append(_remote(send, recv, 6 * i + j, land, land, (*chip, c)).wait_recv)
                pass_to_sibling(i, wi, 3 + j, shard_of(chip))
            relay = _view(full[i], wi, shard_of(via), c)
            fw = _remote(send, recv, 6 * i + 2, relay, relay, (*onto, c))
            waits.append(fw.start)
            sent.append(fw.wait_send)
        s_diag = 2 * (1 - x) + (1 - y)
        for i, wi in enumerate(wis):
            land = _view(full[i], wi, s_diag, c)
            waits.append(_remote(send, recv, 6 * i + 2, land, land, (*onto, c)).wait_recv)
            pass_to_sibling(i, wi, 5, s_diag)
        for i, wi in enumerate(wis):
            for k, s in ((3, shard_of(xn)), (4, shard_of(yn)), (5, s_diag)):
                land = _view(full[i], wi, s, 1 - c)
                waits.append(_remote(send, recv, 6 * i + k, land, land, sib).wait_recv)
        return starts, waits + sent

    return _Exchange(shards, [SDS(_full_shape(wi), BF16) for wi in wis], {}, 6 * n, 2 * n, build)


def _ex_gather_ici(wis, shards, then_d2d=False):
    n = len(wis)

    def build(ins, outs, send, recv, loc):
        x, y, c, chips = _mesh_pos()
        s_me, sib = 2 * x + y, (x, y, 1 - c)
        starts, waits, after = [], [], []
        for i, wi in enumerate(wis):
            Rh = W_SHARD[wi][0] // 2
            for hf in range(2):
                cp = pltpu.make_async_copy(ins[i].at[pl.ds(hf * Rh, Rh), :], _view(outs[i], wi, s_me, hf), loc.at[2 * i + hf])
                starts.append(cp)
                waits.append(cp.wait)
            for j, chip in enumerate(chips):
                cp = _remote(send, recv, 3 * i + j, ins[i].at[pl.ds(c * Rh, Rh), :], _view(outs[i], wi, s_me, c), (*chip, c))
                land = _view(outs[i], wi, 2 * chip[0] + chip[1], c)
                starts.append(cp)
                waits += [cp.wait_send, _remote(send, recv, 3 * i + j, land, land, (*chip, c)).wait_recv]
                if then_d2d:
                    theirs = _view(outs[i], wi, 2 * chip[0] + chip[1], 1 - c)
                    fw = _remote(send, recv, 3 * n + 3 * i + j, land, land, sib)
                    waits.append(fw.start)
                    after += [fw.wait_send, _remote(send, recv, 3 * n + 3 * i + j, theirs, theirs, sib).wait_recv]
        return starts, waits + after

    return _Exchange(shards, [SDS(_full_shape(wi), BF16) for wi in wis], {}, (6 if then_d2d else 3) * n, 2 * n, build)


def _ex_gather_d2d(wis, fulls):
    def build(ins, outs, send, recv, loc):
        x, y, c, chips = _mesh_pos()
        sib = (x, y, 1 - c)
        starts, waits = [], []
        for i, wi in enumerate(wis):
            for j, chip in enumerate(chips):
                mine = _view(outs[i], wi, 2 * chip[0] + chip[1], c)
                theirs = _view(outs[i], wi, 2 * chip[0] + chip[1], 1 - c)
                cp = _remote(send, recv, 3 * i + j, mine, mine, sib)
                starts.append(cp)
                waits += [cp.wait_send, _remote(send, recv, 3 * i + j, theirs, theirs, sib).wait_recv]
        return starts, waits

    return _Exchange(fulls, [SDS(f.shape, BF16) for f in fulls], {i: i for i in range(len(wis))}, 3 * len(wis), 0, build)


def _half_shape(wi):
    R, C = W_SHARD[wi]
    return (R // 2, N_SHARD * C) if W_KINDS[wi] == "col" else (N_SHARD, R // 2, C)


def _ex_pair(wis, grads):
    def build(ins, outs, send, recv, loc):
        x, y, c, _ = _mesh_pos()
        starts, waits = [], []
        for i, wi in enumerate(wis):
            Rh = W_SHARD[wi][0] // 2
            rows = pl.ds((1 - c) * Rh, Rh)
            if tuple(ins[i].shape) == _half_shape(wi):
                src = ins[i]
            else:
                src = ins[i].at[rows, :] if W_KINDS[wi] == "col" else ins[i].at[:, rows, :]
            cp = _remote(send, recv, i, src, outs[i], (x, y, 1 - c))
            starts.append(cp)
            waits.append(cp.wait)
        return starts, waits

    return _Exchange(grads, [SDS(_half_shape(wi), F32) for wi in wis], {}, len(wis), 0, build)


def _ex_chip(wis, pbs):
    def build(ins, outs, send, recv, loc):
        x, y, c, chips = _mesh_pos()
        starts, waits = [], []
        for i, wi in enumerate(wis):
            for j, chip in enumerate(chips):
                cp = _remote(send, recv, 3 * i + j, ins[i].at[j], outs[i].at[j], (*chip, c))
                starts.append(cp)
                waits.append(cp.wait)
        return starts, waits

    shapes = [SDS((3, W_SHARD[wi][0] // 2, W_SHARD[wi][1]), BF16) for wi in wis]
    return _Exchange(pbs, shapes, {}, 3 * len(wis), 0, build)


def _ex_share(wis, halves):
    def build(ins, outs, send, recv, loc):
        x, y, c, _ = _mesh_pos()
        sib = (x, y, 1 - c)
        starts, waits = [], []
        for i, wi in enumerate(wis):
            cp = _remote(send, recv, i, outs[i].at[c], outs[i].at[c], sib)
            starts.append(cp)
            waits += [cp.wait_send, _remote(send, recv, i, outs[i].at[1 - c], outs[i].at[1 - c], sib).wait_recv]
        return starts, waits

    return _Exchange(halves, [SDS(h.shape, F32) for h in halves], {i: i for i in range(len(wis))}, len(wis), 0, build)


def _row_tile(rh, C):
    best = 16
    for t in range(16, rh + 1, 16):
        if rh % t == 0 and t * C * 4 <= (3 << 19):
            best = t
    return best


def _pair_sum(wi, g, ra, sidx):
    R, C = W_SHARD[wi]
    Rh = R // 2
    tr = _row_tile(Rh, C)
    nt = Rh // tr
    off = 0 if tuple(g.shape) == _half_shape(wi) else nt
    col = W_KINDS[wi] == "col"

    def body(sidx_ref, *refs):
        gs, rs = refs[:4], refs[4:8]
        own_ref, pb_ref = refs[8:]
        own_ref[...] = gs[0][...] + rs[0][...]
        for j in range(3):
            pb_ref[j] = (gs[1 + j][...] + rs[1 + j][...]).astype(BF16)

    def gspec(slot):
        if col:
            return pl.BlockSpec((tr, C), lambda i, sx: (sx[4] * off + i, sx[slot]))
        return pl.BlockSpec((None, tr, C), lambda i, sx: (sx[slot], sx[4] * off + i, 0))

    def rspec(slot):
        if col:
            return pl.BlockSpec((tr, C), lambda i, sx: (i, sx[slot]))
        return pl.BlockSpec((None, tr, C), lambda i, sx: (sx[slot], i, 0))

    return pl.pallas_call(
        body, out_shape=(SDS((Rh, C), F32), SDS((3, Rh, C), BF16)),
        grid_spec=pltpu.PrefetchScalarGridSpec(
            num_scalar_prefetch=1, grid=(nt,),
            in_specs=[gspec(k) for k in range(4)] + [rspec(k) for k in range(4)],
            out_specs=(pl.BlockSpec((tr, C), lambda i, sx: (i, 0)), pl.BlockSpec((3, tr, C), lambda i, sx: (0, i, 0)))),
        compiler_params=_cparams("arbitrary"), name=f"pair_sum_w{wi}")(sidx, g, g, g, g, ra, ra, ra, ra)


def _chip_sum(wi, own, rb, sidx):
    R, C = W_SHARD[wi]
    Rh = R // 2
    tr = _row_tile(Rh, C)

    def body(sidx_ref, own_ref, rb_ref, o_ref):
        o_ref[...] = ((own_ref[...] + rb_ref[0].astype(F32)) + rb_ref[1].astype(F32)) + rb_ref[2].astype(F32)

    return pl.pallas_call(
        body, out_shape=SDS((2, Rh, C), F32),
        grid_spec=pltpu.PrefetchScalarGridSpec(
            num_scalar_prefetch=1, grid=(Rh // tr,),
            in_specs=[pl.BlockSpec((tr, C), lambda i, sx: (i, 0)), pl.BlockSpec((3, tr, C), lambda i, sx: (0, i, 0))],
            out_specs=pl.BlockSpec((None, tr, C), lambda i, sx: (sx[4], i, 0))),
        compiler_params=_cparams("arbitrary"), name=f"chip_sum_w{wi}")(sidx, own, rb)


def _gain_allgather(blk, ex):
    m_per, n = blk.shape
    n_in, n_out = len(ex.ins), len(ex.out_shapes)

    def body(x_ref, *rest):
        xin, out_ref, xout = rest[:n_in], rest[n_in], rest[n_in + 1:n_in + 1 + n_out]
        send_sems, recv_sems, local_sem = rest[n_in + 1 + n_out:n_in + 4 + n_out]
        ex_starts, ex_waits = ex.build(xin, xout, *rest[n_in + 4 + n_out:])
        for cp in ex_starts:
            cp.start()
        x, y, c, chips = _mesh_pos()
        me, sibling = (x, y, c), (x, y, 1 - c)

        def rows(px, py, pc):
            return out_ref.at[pl.ds((4 * px + 2 * py + pc) * m_per, m_per), :]

        def copy(k, block, to, src=None):
            return pltpu.make_async_remote_copy(
                src_ref=rows(*block) if src is None else src, dst_ref=rows(*block),
                send_sem=send_sems.at[k], recv_sem=recv_sems.at[k], device_id=to, device_id_type=MESH)

        mine = pltpu.make_async_copy(x_ref, rows(*me), local_sem)
        mine.start()
        first = [copy(0, me, sibling, src=x_ref)]
        first += [copy(1 + j, me, (*chip, c), src=x_ref) for j, chip in enumerate(chips)]
        for cp in first:
            cp.start()
        passed = [copy(4 + j, (*chip, c), sibling) for j, chip in enumerate(chips)]
        for j, chip in enumerate(chips):
            copy(1 + j, (*chip, c), me).wait_recv()
            passed[j].start()
        copy(0, sibling, me).wait_recv()
        for j, chip in enumerate(chips):
            copy(4 + j, (*chip, 1 - c), me).wait_recv()
        for cp in first + passed:
            cp.wait_send()
        mine.wait()
        for w in ex_waits:
            w()

    vm = pl.BlockSpec(memory_space=pltpu.VMEM)
    res = pl.pallas_call(
        body, out_shape=(SDS((8 * m_per, n), blk.dtype), *ex.out_shapes),
        in_specs=[vm] + [ANY] * n_in, out_specs=(vm, *[ANY] * n_out),
        input_output_aliases={1 + a: 1 + o for a, o in ex.aliases.items()},
        scratch_shapes=[pltpu.SemaphoreType.DMA((7,)), pltpu.SemaphoreType.DMA((7,)), pltpu.SemaphoreType.DMA] + ex.sems(),
        name="gain_allgather")(blk, *ex.ins)
    return res[0], tuple(res[1:])


def _adam_math(w, g, m, v):
    mn = ADAM_B1 * m + (1.0 - ADAM_B1) * g
    vn = ADAM_B2 * v + (1.0 - ADAM_B2) * (g * g)
    mh = mn / (1.0 - ADAM_B1 ** ADAM_STEP)
    vh = vn / (1.0 - ADAM_B2 ** ADAM_STEP)
    return -ADAM_LR * (mh / (jnp.sqrt(vh) + ADAM_EPS) + ADAM_WD * w), mn, vn


def _adamw(wi, w, g, m, v):
    R, C = w.shape
    tr = _row_tile(R, C)

    def body(w_ref, g_ref, m_ref, v_ref, go_ref, d_ref, mn_ref, vn_ref):
        g = g_ref[...]
        go_ref[...] = g
        d_ref[...], mn_ref[...], vn_ref[...] = _adam_math(w_ref[...], g, m_ref[...], v_ref[...])

    spec = pl.BlockSpec((tr, C), lambda i: (i, 0))
    return pl.pallas_call(body, out_shape=(SDS((R, C), F32),) * 4, grid=(R // tr,), in_specs=[spec] * 4,
                          out_specs=(spec,) * 4, compiler_params=_cparams("parallel"), name=f"adamw_w{wi}")(w, g, m, v)


def _gain_update(gathered, w, m, v):
    def body(ga_ref, w_ref, m_ref, v_ref, g_ref, d_ref, mn_ref, vn_ref):
        g = ga_ref[0:8, :]
        for dev in range(1, 8):
            g = g + ga_ref[8 * dev:8 * dev + 8, :]
        g_ref[...] = g
        d_ref[...], mn_ref[...], vn_ref[...] = _adam_math(w_ref[...], g, m_ref[...], v_ref[...])

    return pl.pallas_call(body, out_shape=(SDS((8, 1024), F32),) * 4, name="gain_update")(gathered, w, m, v)


GROUP_FFN, GROUP_MIX, GROUP_IN = (4, 5, 6), (1, 2, 3), (0,)
REST = GROUP_MIX + GROUP_FFN


class _MeshComm:
    SCHEDULE = {
        "rms_fwd": [("ring", GROUP_IN)],
        "in_proj": [("ici", (1, 2, 3, 4))],
        "ret_fwd": [("d2d", (1, 2, 3, 4)), ("ici", (5,))],
        "mix_out": [("d2d", (5,))],
        "ffn_up": [("both", (6,))],
        "mix_bwd": [("pair", GROUP_FFN)],
        "ret_bwd": [("pair", GROUP_MIX)],
        "attn_bwd_g0": [("chip", (5,))],
        "attn_bwd_g1": [("chip", (6,))],
        "attn_bwd_g2": [("chip", GROUP_MIX)],
        "wgrad_in_sent": [("chip", (4,))],
        "wgrad_in_kept": [("pair", GROUP_IN), ("share", GROUP_FFN + GROUP_MIX)],
        "in_proj_bwd": [("chip", GROUP_IN)],
    }

    def __init__(self, w_in_shard, rest_f32):
        xi, yi, ci = lax.axis_index("x"), lax.axis_index("y"), lax.axis_index("c")
        self.sidx = jnp.stack([2 * xi + yi, 2 * (1 - xi) + yi, 2 * xi + (1 - yi), 2 * (1 - xi) + (1 - yi), ci]).astype(jnp.int32)
        self.shards, self.rest_f32, self.full = {0: w_in_shard}, list(rest_f32), {}
        self.g, self.own, self.pb, self.half, self.red = {}, {}, {}, {}, {}

    def to_cast(self):
        return self.rest_f32

    def cast_done(self, casts):
        self.shards.update(zip(REST, casts))

    def weight(self, wi):
        return self.full[wi].reshape(D_MODEL, D_MODEL) if wi in (2, 3) else self.full[wi]

    def grads(self, by_wi):
        self.g.update(by_wi)

    def _exchange(self, stage, wis):
        pick = lambda table: [table[wi] for wi in wis]
        if stage == "ring":
            return _ex_gather_ring(wis, pick(self.shards))
        if stage == "ici":
            return _ex_gather_ici(wis, pick(self.shards))
        if stage == "both":
            return _ex_gather_ici(wis, pick(self.shards), then_d2d=True)
        if stage == "d2d":
            return _ex_gather_d2d(wis, pick(self.full))
        if stage == "pair":
            return _ex_pair(wis, [self.g["in_sent"] if wi == 0 else self.g[wi] for wi in wis])
        if stage == "chip":
            return _ex_chip(wis, pick(self.pb))
        return _ex_share(wis, pick(self.half))

    def _landed(self, stage, wis, res):
        for wi, r in zip(wis, res):
            if stage in ("ring", "ici", "d2d", "both"):
                self.full[wi] = r
            elif stage == "pair":
                self.own[wi], self.pb[wi] = _pair_sum(wi, self.g["in_kept"] if wi == 0 else self.g[wi], r, self.sidx)
            elif stage == "chip":
                self.half[wi] = _chip_sum(wi, self.own[wi], r, self.sidx)
            else:
                self.red[wi] = r

    def carry(self, point):
        return [self._exchange(stage, wis) for stage, wis in self.SCHEDULE.get(point, ())]

    def took(self, point, xres):
        for (stage, wis), res in zip(self.SCHEDULE.get(point, ()), xres):
            self._landed(stage, wis, res)

    def last_share(self):
        return self._exchange("share", GROUP_IN)

    def reduced(self, last_shared):
        self._landed("share", GROUP_IN, last_shared)
        return [self.red[wi] for wi in range(N_W)]


def kernel(x, norm_mix_g, w_in, w_out_attn, w_out_ret, w_out, norm_ffn_g, w_ffn_gate, w_ffn_up, w_ffn_down, norm_final_g, loss_target, m_norm_mix_g, m_w_in, m_w_out_attn, m_w_out_ret, m_w_out, m_norm_ffn_g, m_w_ffn_gate, m_w_ffn_up, m_w_ffn_down, m_norm_final_g, v_norm_mix_g, v_w_in, v_w_out_attn, v_w_out_ret, v_w_out, v_norm_ffn_g, v_w_ffn_gate, v_w_ffn_up, v_w_ffn_down, v_norm_final_g):
    ws = (w_in, w_out_attn, w_out_ret, w_out, w_ffn_gate, w_ffn_up, w_ffn_down)
    ms = (m_w_in, m_w_out_attn, m_w_out_ret, m_w_out, m_w_ffn_gate, m_w_ffn_up, m_w_ffn_down)
    vs = (v_w_in, v_w_out_attn, v_w_out_ret, v_w_out, v_w_ffn_gate, v_w_ffn_up, v_w_ffn_down)
    shard2d = lambda a, wi: a.reshape(W_SHARD[wi])

    comm = _MeshComm(_cast_bf16(shard2d(ws[0], 0)), [shard2d(ws[wi], wi) for wi in REST])
    g3 = norm_final_g.reshape(1, D_MODEL)
    loss_p, grad_x, gain_g = _step(x[0], loss_target[0], norm_mix_g, norm_ffn_g, g3, comm)

    pad8 = lambda rows: jnp.concatenate([r.reshape(1, D_MODEL) for r in rows]
                                        + [jnp.zeros((8 - len(rows), D_MODEL), F32)], axis=0)
    gathered, shared = _gain_allgather(pad8((*gain_g, jnp.tile(loss_p[0:1], (1, D_MODEL // 128)))), comm.last_share())
    gred = comm.reduced(shared)

    outs_g, outs_d, outs_m, outs_v = [], [], [], []
    for wi in range(N_W):
        g2d = gred[wi].reshape(W_SHARD[wi])
        gout, dlt, mn, vn = _adamw(wi, shard2d(ws[wi], wi), g2d, shard2d(ms[wi], wi), shard2d(vs[wi], wi))
        for lst, a in ((outs_g, gout), (outs_d, dlt), (outs_m, mn), (outs_v, vn)):
            lst.append(a.reshape(ws[wi].shape))

    gg, gd, gm, gv = _gain_update(gathered, pad8((norm_mix_g, norm_ffn_g, norm_final_g)),
                                  pad8((m_norm_mix_g, m_norm_ffn_g, m_norm_final_g)),
                                  pad8((v_norm_mix_g, v_norm_ffn_g, v_norm_final_g)))
    loss = gg[3, 0]

    def assemble(gain_rows, wlist):
        return (gain_rows[0:1], wlist[0], wlist[1], wlist[2], wlist[3], gain_rows[1:2],
                wlist[4], wlist[5], wlist[6], gain_rows[2])

    return (loss, grad_x[None], *assemble(gg, outs_g), *assemble(gd, outs_d), *assemble(gm, outs_m), *assemble(gv, outs_v))
```

```python
import functools
import math

import numpy as np
import jax
import jax.numpy as jnp
from jax import lax
from jax.experimental import pallas as pl
from jax.experimental.pallas import tpu as pltpu

F32, BF16 = jnp.float32, jnp.bfloat16
SDS = jax.ShapeDtypeStruct
MESH = pl.DeviceIdType.MESH

D_MODEL = 1024
PROJ_W = 9728
COLB = 512
N_COLB = PROJ_W // COLB
QA_B, KA_B, VA_B = 0, 3, 6
QR_B, KR_B = 9, 10
FFN_HID = 2816
N_SHARD = 4
HID_S = FFN_HID // N_SHARD
W_IN_S = PROJ_W // N_SHARD
DILATIONS = (1, 4, 16)
BLK = 128
RET_HEADS = 4
ROPE_THETA = 10000.0
NORM_EPS = 1e-6
ADAM_LR, ADAM_B1, ADAM_B2, ADAM_EPS, ADAM_WD, ADAM_STEP = 0.001, 0.9, 0.999, 1e-08, 0.01, 10
VMEM_LIMIT = 56 << 20


def _cparams(*sem):
    return pltpu.CompilerParams(dimension_semantics=sem or None, vmem_limit_bytes=VMEM_LIMIT)


def _dot(a, b):
    return jnp.dot(a, b, preferred_element_type=F32)


def _dot_nt(a, b):
    return lax.dot_general(a, b, (((1,), (1,)), ((), ())), preferred_element_type=F32)


def _dot_tn(a, b):
    return lax.dot_general(a, b, (((0,), (0,)), ((), ())), preferred_element_type=F32)


def _row_pieces(tm, sub=512):
    return [slice(i, i + sub) for i in range(0, tm, sub)]


def _sigmoid(z):
    return 0.5 * jnp.tanh(0.5 * z) + 0.5


ANY = pl.BlockSpec(memory_space=pl.ANY)


class _Exchange:
    def __init__(self, ins, out_shapes, aliases, n_sem, n_loc, build):
        self.ins, self.out_shapes, self.aliases = list(ins), list(out_shapes), dict(aliases)
        self.n_sem, self.n_loc, self.build = n_sem, n_loc, build

    def sems(self):
        return [pltpu.SemaphoreType.DMA((self.n_sem,)), pltpu.SemaphoreType.DMA((self.n_sem,)),
                pltpu.SemaphoreType.DMA((max(self.n_loc, 1),))]


def _carrier_call(body, args, *, out_shape, grid, in_specs, out_specs, scratch_shapes=(), sem, name, exchanges=(),
                  prefetch=None, in_out_aliases=None):
    out_shape, out_specs = tuple(out_shape), tuple(out_specs)
    n_in, n_out, n_scr = len(args), len(out_shape), len(scratch_shapes)
    n_pre = 0 if prefetch is None else 1
    x_args, x_outs, x_scr, spans = [], [], [], []
    aliases = {n_pre + a: o for a, o in (in_out_aliases or {}).items()}
    for ex in exchanges:
        i0, o0 = len(x_args), len(x_outs)
        for a, o in ex.aliases.items():
            aliases[n_pre + n_in + i0 + a] = n_out + o0 + o
        x_args += ex.ins
        x_outs += ex.out_shapes
        x_scr += ex.sems()
        spans.append((i0, len(ex.ins), o0, len(ex.out_shapes)))
    nx_in, nx_out = len(x_args), len(x_outs)

    def wrapped(*refs):
        refs = refs[n_pre:]
        ins, xin = refs[:n_in], refs[n_in:n_in + nx_in]
        o_base = n_in + nx_in
        outs, xout = refs[o_base:o_base + n_out], refs[o_base + n_out:o_base + n_out + nx_out]
        s_base = o_base + n_out + nx_out
        scr, xs = refs[s_base:s_base + n_scr], refs[s_base + n_scr:]

        def built(e):
            i0, ni, o0, no = spans[e]
            return exchanges[e].build(xin[i0:i0 + ni], xout[o0:o0 + no], *xs[3 * e:3 * e + 3])

        if exchanges:
            first = functools.reduce(jnp.logical_and, [pl.program_id(k) == 0 for k in range(len(grid))])
            last = functools.reduce(jnp.logical_and, [pl.program_id(k) == grid[k] - 1 for k in range(len(grid))])

            @pl.when(first)
            def _():
                for e in range(len(exchanges)):
                    for cp in built(e)[0]:
                        cp.start()

        body(*ins, *outs, *scr)

        if exchanges:
            @pl.when(last)
            def _():
                for e in range(len(exchanges)):
                    for w in built(e)[1]:
                        w()

    all_in, all_out = list(in_specs) + [ANY] * nx_in, out_specs + tuple([ANY] * nx_out)
    all_scr = list(scratch_shapes) + x_scr
    cparams = _cparams(*(sem if not exchanges else ("arbitrary",) * len(grid)))
    if prefetch is None:
        res = pl.pallas_call(wrapped, out_shape=out_shape + tuple(x_outs), grid=grid, in_specs=all_in, out_specs=all_out,
                             scratch_shapes=all_scr, input_output_aliases=aliases, compiler_params=cparams,
                             name=name)(*args, *x_args)
    else:
        gs = pltpu.PrefetchScalarGridSpec(num_scalar_prefetch=1, grid=grid, in_specs=all_in, out_specs=all_out,
                                          scratch_shapes=all_scr)
        res = pl.pallas_call(wrapped, out_shape=out_shape + tuple(x_outs), grid_spec=gs, input_output_aliases=aliases,
                             compiler_params=cparams, name=name)(prefetch, *args, *x_args)
    xres = [tuple(res[n_out + o0:n_out + o0 + no]) for (_, _, o0, no) in spans]
    return tuple(res[:n_out]), xres


def _tables(S):
    f32 = np.float32
    pos = np.arange(S, dtype=f32)
    lane = np.arange(128)
    inv = (f32(ROPE_THETA) ** (-np.arange(0, 64, 2, dtype=f32) / f32(64))).astype(f32)
    ang = (pos[:, None] * inv[None, :]).astype(np.float64)
    idx = (lane % 64) % 32
    c, s = np.cos(ang)[:, idx], np.sin(ang)[:, idx]
    first = ((lane % 64) < 32)[None, :]
    rope = np.stack([c, np.where(first, 0.0, s), np.where(first, -s, 0.0)])
    base = (f32(1.0) / (f32(ROPE_THETA) ** np.linspace(0.0, 1.0, 64, dtype=f32))).astype(f32)
    ang2 = (pos[:, None] * base[None, :]).astype(np.float64)
    c2, s2 = np.cos(ang2)[:, lane // 2], np.sin(ang2)[:, lane // 2]
    even = (lane % 2 == 0)[None, :]
    th = np.stack([c2, np.where(even, 0.0, s2), np.where(even, -s2, 0.0)])
    return np.stack([rope, th, th * (128 ** -0.5)]).astype(f32)


def _rot(a, c, sa, sb, shift):
    return a * c + pltpu.roll(a, shift, 1) * sa + pltpu.roll(a, 128 - shift, 1) * sb


def _unrot(g, c, sa, sb, shift):
    return g * c + pltpu.roll(g * sa, 128 - shift, 1) + pltpu.roll(g * sb, shift, 1)


def _ret_consts():
    h = np.arange(RET_HEADS, dtype=np.float64)
    log_g = np.log1p(-(2.0 ** (-5.0 - h)))
    idx = np.arange(BLK, dtype=np.float64)
    diff = idx[:, None] - idx[None, :]
    dmask = np.where(diff[None] >= 0, np.exp(np.maximum(diff, 0.0)[None] * log_g[:, None, None]), 0.0)
    zeta = np.exp((BLK - 1 - idx)[None, :] * log_g[:, None])
    xi = np.exp((idx + 1.0)[None, :] * log_g[:, None])
    dec = np.exp(BLK * log_g)
    rep = lambda v: np.broadcast_to(v[:, :, None], (RET_HEADS, BLK, 128))
    return (jnp.asarray(dmask, F32), jnp.asarray(rep(zeta), F32), jnp.asarray(rep(xi), F32),
            jnp.asarray(np.broadcast_to(dec[:, None, None], (RET_HEADS, 8, 256)), F32))


def _rms_fwd(x, g, to_cast=(), exchanges=()):
    S = x.shape[0]
    steps = 4
    tm = S // steps
    n_c = len(to_cast)

    def body(x_ref, g_ref, *refs):
        c_in, (h_ref, ht_ref), c_out = refs[:n_c], refs[n_c:n_c + 2], refs[n_c + 2:]
        for rows in _row_pieces(tm, 512):
            xv = x_ref[rows, :]
            r = lax.rsqrt(jnp.mean(xv * xv, axis=-1, keepdims=True) + NORM_EPS)
            h = xv * r * g_ref[...]
            h_ref[rows, :] = h.astype(BF16)
            ht_ref[:, rows] = h.T.astype(BF16)
        for a_ref, o_ref in zip(c_in, c_out):
            o_ref[...] = a_ref[...].astype(BF16)

    slab = lambda a: pl.BlockSpec((a.shape[0] // steps, a.shape[1]), lambda i: (i, 0))
    return _carrier_call(
        body, (x, g, *to_cast),
        out_shape=(SDS((S, D_MODEL), BF16), SDS((D_MODEL, S), BF16), *[SDS(a.shape, BF16) for a in to_cast]),
        grid=(steps,),
        in_specs=[pl.BlockSpec((tm, D_MODEL), lambda i: (i, 0)), pl.BlockSpec((1, D_MODEL), lambda i: (0, 0))]
        + [slab(a) for a in to_cast],
        out_specs=(pl.BlockSpec((tm, D_MODEL), lambda i: (i, 0)), pl.BlockSpec((D_MODEL, tm), lambda i: (0, i)),
                   *[slab(a) for a in to_cast]),
        sem=("parallel",), name="rms_fwd", exchanges=exchanges)


def _in_proj(h, w_in, tab, exchanges=()):
    S = h.shape[0]
    tm = min(S, 4096)

    def body(h_ref, w_ref, t_ref, o_ref):
        j = pl.program_id(1)
        is_rope = j < 6
        is_theta = (j == QR_B) | (j == KR_B)
        sub = 512

        def rotated(shift):
            for i in range(tm // sub):
                rows = slice(i * sub, (i + 1) * sub)
                acc = _dot(h_ref[rows, :], w_ref[...])
                c, sa, sb = t_ref[0, 0, rows, :], t_ref[0, 1, rows, :], t_ref[0, 2, rows, :]
                for k in range(COLB // 128):
                    sl = slice(k * 128, (k + 1) * 128)
                    o_ref[rows, sl] = _rot(acc[:, sl], c, sa, sb, shift).astype(BF16)

        @pl.when(is_rope)
        def _():
            rotated(32)

        @pl.when(is_theta)
        def _():
            rotated(1)

        @pl.when(jnp.logical_not(is_rope | is_theta))
        def _():
            o_ref[...] = _dot(h_ref[...], w_ref[...]).astype(BF16)

    def tab_map(i, j):
        return (jnp.where(j == QR_B, 1, jnp.where(j == KR_B, 2, 0)), 0, i, 0)

    (proj,), xres = _carrier_call(
        body, (h, w_in, tab), out_shape=(SDS((S, PROJ_W), BF16),), grid=(S // tm, N_COLB),
        in_specs=[pl.BlockSpec((tm, D_MODEL), lambda i, j: (i, 0)),
                  pl.BlockSpec((D_MODEL, COLB), lambda i, j: (0, j)),
                  pl.BlockSpec((1, 3, tm, 128), tab_map)],
        out_specs=(pl.BlockSpec((tm, COLB), lambda i, j: (i, j)),),
        sem=("parallel", "arbitrary"), name="in_proj", exchanges=exchanges)
    return proj, xres


def _band_mask(n):
    qi = lax.broadcasted_iota(jnp.int32, (BLK, 2 * BLK), 0)
    kj = lax.broadcasted_iota(jnp.int32, (BLK, 2 * BLK), 1)
    dist = BLK + qi - kj
    return (dist >= 0) & (dist <= BLK) & ((kj >= BLK) | (n > 0))


def _qkv_col(d, gi):
    if d == 1:
        return lambda t, r: 3 * t + gi
    return lambda t, r: 3 * r + t


def _attn_fwd(qkv, d, gi, exchanges=()):
    L = qkv.shape[0]
    nb = L // BLK

    def body(q_ref, kc_ref, kp_ref, vc_ref, vp_ref, o_ref, lse_ref):
        n = pl.program_id(1)
        mask = _band_mask(n)
        mask2 = jnp.concatenate([mask, mask], axis=0)
        lane = lax.broadcasted_iota(jnp.int32, (BLK, 128), 1)
        lo = lane < 64
        lse_all = jnp.zeros((BLK, 128), F32)
        chunks = [slice(c * 128, (c + 1) * 128) for c in range(4)]
        scores, vals = [], []
        for sl in chunks:
            q = q_ref[:, sl]
            k = jnp.concatenate([kp_ref[:, sl], kc_ref[:, sl]], axis=0)
            vals.append(jnp.concatenate([vp_ref[:, sl], vc_ref[:, sl]], axis=0))
            q2 = jnp.concatenate([jnp.where(lo, q, jnp.zeros_like(q)), jnp.where(lo, jnp.zeros_like(q), q)], axis=0)
            scores.append(_dot_nt(q2, k))
        probs = []
        for c, s in enumerate(scores):
            s = jnp.where(mask2, s * 0.125, jnp.float32(-1e30))
            m = jnp.max(s, axis=-1, keepdims=True)
            p = jnp.exp(s - m)
            l = jnp.sum(p, axis=-1, keepdims=True)
            probs.append((p / l).astype(BF16))
            lse = m + jnp.log(l)
            lse_all = jnp.where(lane // 16 == 2 * c, lse[:BLK], jnp.where(lane // 16 == 2 * c + 1, lse[BLK:], lse_all))
        for sl, p, v in zip(chunks, probs, vals):
            o2 = _dot(p, v)
            o_ref[:, sl] = jnp.where(lo, o2[:BLK], o2[BLK:])
        lse_ref[...] = lse_all

    prev = lambda n: jnp.maximum(n - 1, 0)
    col = _qkv_col(d, gi)
    return _carrier_call(
        body, (qkv,) * 5, out_shape=(SDS((L, d * 512), F32), SDS((L, d * 128), F32)), grid=(d, nb),
        in_specs=[pl.BlockSpec((BLK, 512), lambda r, n: (n, col(0, r))),
                  pl.BlockSpec((BLK, 512), lambda r, n: (n, col(1, r))),
                  pl.BlockSpec((BLK, 512), lambda r, n: (prev(n), col(1, r))),
                  pl.BlockSpec((BLK, 512), lambda r, n: (n, col(2, r))),
                  pl.BlockSpec((BLK, 512), lambda r, n: (prev(n), col(2, r)))],
        out_specs=(pl.BlockSpec((BLK, 512), lambda r, n: (n, r)),
                   pl.BlockSpec((BLK, 128), lambda r, n: (n, r))),
        sem=("parallel", "arbitrary"), name=f"attn_fwd_g{gi}", exchanges=exchanges)


def _qkv_to_sub(proj, d, gi):
    S = proj.shape[0]
    tm = 512
    n = tm // d

    def body(q_ref, k_ref, v_ref, o_ref, scr):
        for t, ref in enumerate((q_ref, k_ref, v_ref)):
            for c in range(4):
                scr[c] = ref[:, c * 128:(c + 1) * 128].astype(F32)
            for r in range(d):
                for c in range(4):
                    col = (3 * r + t) * 512 + c * 128
                    o_ref[:, col:col + 128] = scr[c, pl.ds(r, n, stride=d), :].astype(BF16)

    return pl.pallas_call(
        body, out_shape=SDS((S // d, d * 1536), BF16), grid=(S // tm,),
        in_specs=[pl.BlockSpec((tm, 512), lambda i, b=b: (i, b + gi)) for b in (QA_B, KA_B, VA_B)],
        out_specs=pl.BlockSpec((n, d * 1536), lambda i: (i, 0)),
        scratch_shapes=[pltpu.VMEM((4, tm, 128), F32)],
        compiler_params=_cparams("parallel"), name=f"qkv_to_sub_g{gi}")(proj, proj, proj)


def _attn_merge(os_, lses):
    S = os_[0].shape[0]
    tm = 512

    def body(o0, o1, o2, l0, l1, l2, att_ref, lt_ref, so1, so2, sl1, sl2):
        lo = lax.broadcasted_iota(jnp.int32, (tm, 128), 1) < 64

        def natural(ref, d, scr, width):
            nch = width // 128
            if d == 1:
                return [ref[:, c * 128:(c + 1) * 128] for c in range(nch)]
            for r in range(d):
                for c in range(nch):
                    scr[c, pl.ds(r, tm // d, stride=d), :] = ref[:, r * width + c * 128:r * width + (c + 1) * 128]
            return [scr[c] for c in range(nch)]

        ls = [natural(l, d, s, 128)[0] for l, d, s in zip((l0, l1, l2), DILATIONS, (None, sl1, sl2))]
        m = jnp.maximum(jnp.maximum(ls[0], ls[1]), ls[2])
        es = [jnp.exp(v - m) for v in ls]
        z = es[0] + es[1] + es[2]
        lt_ref[...] = m + jnp.log(z)
        ws = [e / z for e in es]
        o_nat = [natural(o, d, s, 512) for o, d, s in zip((o0, o1, o2), DILATIONS, (None, so1, so2))]
        for c in range(4):
            acc = jnp.zeros((tm, 128), F32)
            for g in range(3):
                w_lo = jnp.broadcast_to(ws[g][:, 32 * c:32 * c + 1], (tm, 128))
                w_hi = jnp.broadcast_to(ws[g][:, 32 * c + 16:32 * c + 17], (tm, 128))
                acc = acc + jnp.where(lo, w_lo, w_hi) * o_nat[g][c]
            att_ref[:, c * 128:(c + 1) * 128] = acc.astype(BF16)

    sub = lambda w: [pl.BlockSpec((tm // d, d * w), lambda i: (i, 0)) for d in DILATIONS]
    return pl.pallas_call(
        body, out_shape=(SDS((S, 512), BF16), SDS((S, 128), F32)), grid=(S // tm,),
        in_specs=sub(512) + sub(128),
        out_specs=(pl.BlockSpec((tm, 512), lambda i: (i, 0)), pl.BlockSpec((tm, 128), lambda i: (i, 0))),
        scratch_shapes=[pltpu.VMEM((4, tm, 128), F32), pltpu.VMEM((4, tm, 128), F32),
                        pltpu.VMEM((1, tm, 128), F32), pltpu.VMEM((1, tm, 128), F32)],
        compiler_params=_cparams("parallel"), name="attn_merge")(*os_, *lses)


def _assemble_dproj(att_grads, dproj):
    S = dproj.shape[0]
    tm = 256

    def body(*refs):
        a = [refs[3 * t:3 * t + 3] for t in range(3)]
        dp_prev, o_ref, scr = refs[9:]
        for t in range(3):
            for g, d in enumerate(DILATIONS):
                base = (3 * t + g) * COLB
                if d == 1:
                    o_ref[:, base:base + COLB] = a[t][g][...]
                    continue
                for c in range(4):
                    for r in range(d):
                        scr[c, pl.ds(r, tm // d, stride=d), :] = a[t][g][:, r * 512 + c * 128:r * 512 + (c + 1) * 128].astype(F32)
                    o_ref[:, base + c * 128:base + (c + 1) * 128] = scr[c].astype(BF16)

    sub = [pl.BlockSpec((tm // d, d * 512), lambda i: (i, 0)) for d in DILATIONS]
    flat = [att_grads[t][g] for t in range(3) for g in range(3)]
    return pl.pallas_call(
        body, out_shape=SDS((S, PROJ_W), BF16), grid=(S // tm,),
        in_specs=sub * 3 + [ANY], out_specs=pl.BlockSpec((tm, 9 * COLB), lambda i: (i, 0)),
        scratch_shapes=[pltpu.VMEM((4, tm, 128), F32)], input_output_aliases={9: 0},
        compiler_params=_cparams("parallel"), name="assemble_dproj")(*flat, dproj)


def _ret_fwd(proj, consts, exchanges=()):
    S = proj.shape[0]
    nc = S // BLK
    dmask, zeta, xi, dec = consts

    def body(q_ref, k_ref, v0_ref, v1_ref, g0_ref, g1_ref, dm_ref, z_ref, x_ref, dec_ref,
             y_ref, rn_ref, rs_ref, st_ref, R):
        @pl.when(pl.program_id(0) == 0)
        def _():
            R[...] = jnp.zeros_like(R)

        lane16 = lax.broadcasted_iota(jnp.int32, (BLK, 128), 1) // 16
        rs_all = jnp.zeros((BLK, 128), F32)
        first = []
        for h in range(RET_HEADS):
            hs = slice(h * 128, (h + 1) * 128)
            q, k = q_ref[:, hs], k_ref[:, hs]
            v = (v0_ref if h < 2 else v1_ref)[:, (h % 2) * 256:(h % 2 + 1) * 256]
            Rb = R[h].astype(BF16)
            st_ref[h] = Rb
            kz = (k.astype(F32) * z_ref[h]).astype(BF16)
            first.append((v, _dot_nt(q, k), _dot((q.astype(F32) * x_ref[h]).astype(BF16), Rb), _dot_tn(kz, v)))
        masked = [(s * dm_ref[h]).astype(BF16) for h, (_, s, _, _) in enumerate(first)]
        for h in range(RET_HEADS):
            vs = slice((h % 2) * 256, (h % 2 + 1) * 256)
            os_ = slice(h * 256, (h + 1) * 256)
            v, _, cross, kv = first[h]
            o = _dot(masked[h], v) + cross
            R[h] = R[h] * dec_ref[h, 0:1, :] + kv
            mu = jnp.mean(o, axis=-1, keepdims=True)
            oc = o - mu
            rstd = lax.rsqrt(jnp.mean(oc * oc, axis=-1, keepdims=True) + NORM_EPS)
            rn = oc * rstd
            gr = (g0_ref if h < 2 else g1_ref)[:, vs].astype(F32)
            y_ref[:, os_] = (rn * gr * _sigmoid(gr)).astype(BF16)
            rn_ref[:, os_] = rn.astype(BF16)
            rs_all = jnp.where(lane16 == h, rstd, rs_all)
        rs_ref[...] = rs_all

    cst = lambda shape: pl.BlockSpec(shape, lambda c: (0, 0, 0))
    blk = lambda j: pl.BlockSpec((BLK, 512), lambda c: (c, j))
    return _carrier_call(
        body, (proj, proj, proj, proj, proj, proj, dmask, zeta, xi, dec),
        out_shape=(SDS((S, 1024), BF16), SDS((S, 1024), BF16), SDS((S, 128), F32), SDS((RET_HEADS, nc, BLK, 256), BF16)),
        grid=(nc,),
        in_specs=[blk(QR_B), blk(KR_B), blk(11), blk(12), blk(13), blk(14),
                  cst((RET_HEADS, BLK, BLK)), cst((RET_HEADS, BLK, 128)), cst((RET_HEADS, BLK, 128)), cst((RET_HEADS, 8, 256))],
        out_specs=(pl.BlockSpec((BLK, 1024), lambda c: (c, 0)), pl.BlockSpec((BLK, 1024), lambda c: (c, 0)),
                   pl.BlockSpec((BLK, 128), lambda c: (c, 0)),
                   pl.BlockSpec((RET_HEADS, None, BLK, 256), lambda c: (0, c, 0, 0))),
        scratch_shapes=[pltpu.VMEM((RET_HEADS, BLK, 256), F32)],
        sem=("arbitrary",), name="ret_fwd", exchanges=exchanges)


def _mix_out(att, yrin, proj, wa, wr, wo, x, g2, exchanges=()):
    S = x.shape[0]
    tm = 512
    gate0 = 15 * COLB

    def body(a_ref, y_ref, ga_ref, gr_ref, wa_ref, wr_ref, wo_ref, x_ref, g_ref, m_ref, ya_ref, yr_ref, x1_ref, h2_ref):
        pieces = _row_pieces(tm, 256)
        branches = [(_dot(a_ref[rows, :], wa_ref[...]), _dot(y_ref[rows, :], wr_ref[...])) for rows in pieces]
        merged = []
        for rows, (ya, yr) in zip(pieces, branches):
            m = (_sigmoid(ga_ref[rows, :].astype(F32)) * ya + _sigmoid(gr_ref[rows, :].astype(F32)) * yr).astype(BF16)
            m_ref[rows, :] = m
            ya_ref[rows, :] = ya.astype(BF16)
            yr_ref[rows, :] = yr.astype(BF16)
            merged.append(m)
        for rows, m in zip(pieces, merged):
            x1 = x_ref[rows, :] + _dot(m, wo_ref[...])
            x1_ref[rows, :] = x1
            r = lax.rsqrt(jnp.mean(x1 * x1, axis=-1, keepdims=True) + NORM_EPS)
            h2_ref[rows, :] = (x1 * r * g_ref[...]).astype(BF16)

    row = lambda w: pl.BlockSpec((tm, w), lambda i: (i, 0))
    cols = lambda c0: pl.BlockSpec((pl.Element(tm), pl.Element(D_MODEL)), lambda i: (i * tm, c0))
    resident = lambda r, c: pl.BlockSpec((r, c), lambda i: (0, 0), pipeline_mode=pl.Buffered(1))
    return _carrier_call(
        body, (att, yrin, proj, proj, wa, wr, wo, x, g2),
        out_shape=(SDS((S, D_MODEL), BF16),) * 3 + (SDS((S, D_MODEL), F32), SDS((S, D_MODEL), BF16)), grid=(S // tm,),
        in_specs=[row(512), row(D_MODEL), cols(gate0), cols(gate0 + D_MODEL), resident(512, D_MODEL),
                  resident(D_MODEL, D_MODEL), resident(D_MODEL, D_MODEL), row(D_MODEL),
                  pl.BlockSpec((1, D_MODEL), lambda i: (0, 0))],
        out_specs=(row(D_MODEL),) * 5, sem=("parallel",), name="mix_out", exchanges=exchanges)


def _ffn_up(h2, wg, wu, exchanges=()):
    S = h2.shape[0]
    tm = min(S, 2048)

    def body(h_ref, wg_ref, wu_ref, g_ref, u_ref, a_ref):
        for rows in _row_pieces(tm):
            hv = h_ref[rows, :]
            g = _dot(hv, wg_ref[...])
            u = _dot(hv, wu_ref[...])
            g_ref[rows, :] = g.astype(BF16)
            u_ref[rows, :] = u.astype(BF16)
            a_ref[rows, :] = (g * _sigmoid(g) * u).astype(BF16)

    wspec = pl.BlockSpec((None, D_MODEL, HID_S), lambda i, s: (s, 0, 0))
    ospec = pl.BlockSpec((None, tm, HID_S), lambda i, s: (s, i, 0))
    return _carrier_call(
        body, (h2, wg, wu), out_shape=(SDS((N_SHARD, S, HID_S), BF16),) * 3, grid=(S // tm, N_SHARD),
        in_specs=[pl.BlockSpec((tm, D_MODEL), lambda i, s: (i, 0)), wspec, wspec],
        out_specs=(ospec, ospec, ospec),
        sem=("parallel", "arbitrary"), name="ffn_up", exchanges=exchanges)


def _ffn_down_loss(act, wd, x1, g3, tgt):
    S = x1.shape[0]
    tm = 512

    def body(a_ref, w_ref, x_ref, g_ref, t_ref, dx_ref, dxb_ref, dg_ref, ls_ref):
        @pl.when(pl.program_id(0) == 0)
        def _():
            dg_ref[...] = jnp.zeros_like(dg_ref)
            ls_ref[...] = jnp.zeros_like(ls_ref)

        g = g_ref[...]
        for rows in _row_pieces(tm, 256):
            y = _dot(a_ref[0, rows, :], w_ref[0])
            for s in range(1, N_SHARD):
                y = y + _dot(a_ref[s, rows, :], w_ref[s])
            x2 = x_ref[rows, :] + y
            r = lax.rsqrt(jnp.mean(x2 * x2, axis=-1, keepdims=True) + NORM_EPS)
            xh = x2 * r
            err = xh * g - t_ref[rows, :]
            ls_ref[...] += jnp.sum(jnp.sum(err * err, axis=-1, keepdims=True), axis=0, keepdims=True) * (0.5 / D_MODEL)
            dy = err * (1.0 / D_MODEL)
            dg_ref[...] += jnp.sum(dy * xh, axis=0, keepdims=True)
            dxh = dy * g
            dx = r * (dxh - xh * jnp.mean(dxh * xh, axis=-1, keepdims=True))
            dx_ref[rows, :] = dx
            dxb_ref[rows, :] = dx.astype(BF16)

    row = pl.BlockSpec((tm, D_MODEL), lambda i: (i, 0))
    vec = pl.BlockSpec((1, D_MODEL), lambda i: (0, 0))
    return pl.pallas_call(
        body, out_shape=(SDS((S, D_MODEL), F32), SDS((S, D_MODEL), BF16), SDS((1, D_MODEL), F32), SDS((8, 128), F32)),
        grid=(S // tm,),
        in_specs=[pl.BlockSpec((N_SHARD, tm, HID_S), lambda i: (0, i, 0)),
                  pl.BlockSpec((N_SHARD, HID_S, D_MODEL), lambda i: (0, 0, 0), pipeline_mode=pl.Buffered(1)),
                  row, vec, row],
        out_specs=(row, row, vec, pl.BlockSpec((8, 128), lambda i: (0, 0))),
        compiler_params=_cparams("arbitrary"), name="ffn_down_loss")(act, wd, x1, g3, tgt)


def _ffn_bwd(dx2b, dx2, wd, wg, wu, gte, up, x1, g2):
    S = x1.shape[0]
    tm = 256

    def body(d_ref, dx2_ref, wd_ref, wg_ref, wu_ref, g_ref, u_ref, x_ref, gn_ref,
             dg_ref, du_ref, dx_ref, dxb_ref, dgn_ref):
        @pl.when(pl.program_id(0) == 0)
        def _():
            dgn_ref[...] = jnp.zeros_like(dgn_ref)

        d = d_ref[...]
        dacts = [_dot_nt(d, wd_ref[s]) for s in range(N_SHARD)]
        dgs, dus = [], []
        for s, da in enumerate(dacts):
            g = g_ref[s].astype(F32)
            sg = _sigmoid(g)
            dgs.append((da * u_ref[s].astype(F32) * sg * (1.0 + g * (1.0 - sg))).astype(BF16))
            dus.append((da * g * sg).astype(BF16))
            dg_ref[s] = dgs[s]
            du_ref[s] = dus[s]
        dh = _dot_nt(dgs[0], wg_ref[0]) + _dot_nt(dus[0], wu_ref[0])
        for s in range(1, N_SHARD):
            dh = dh + _dot_nt(dgs[s], wg_ref[s]) + _dot_nt(dus[s], wu_ref[s])
        xv = x_ref[...]
        r = lax.rsqrt(jnp.mean(xv * xv, axis=-1, keepdims=True) + NORM_EPS)
        xh = xv * r
        dgn_ref[...] += jnp.sum(dh * xh, axis=0, keepdims=True)
        dxh = dh * gn_ref[...]
        dx = dx2_ref[...] + r * (dxh - xh * jnp.mean(dxh * xh, axis=-1, keepdims=True))
        dx_ref[...] = dx
        dxb_ref[...] = dx.astype(BF16)

    row = pl.BlockSpec((tm, D_MODEL), lambda i: (i, 0))
    vec = pl.BlockSpec((1, D_MODEL), lambda i: (0, 0))
    aspec = pl.BlockSpec((N_SHARD, tm, HID_S), lambda i: (0, i, 0))
    resident = lambda shape: pl.BlockSpec(shape, lambda i: (0, 0, 0), pipeline_mode=pl.Buffered(1))
    return pl.pallas_call(
        body,
        out_shape=(SDS((N_SHARD, S, HID_S), BF16), SDS((N_SHARD, S, HID_S), BF16),
                   SDS((S, D_MODEL), F32), SDS((S, D_MODEL), BF16), SDS((1, D_MODEL), F32)),
        grid=(S // tm,),
        in_specs=[row, row, resident((N_SHARD, HID_S, D_MODEL)), resident((N_SHARD, D_MODEL, HID_S)),
                  resident((N_SHARD, D_MODEL, HID_S)), aspec, aspec, row, vec],
        out_specs=(aspec, aspec, row, row, vec),
        compiler_params=_cparams("arbitrary"), name="ffn_bwd")(dx2b, dx2, wd, wg, wu, gte, up, x1, g2)


def _wgrad(name, a, b, a_spec, b_spec, out_shape, out_spec, n_par, S):
    tk = min(S, 4096)

    def body(a_ref, b_ref, o_ref):
        @pl.when(pl.program_id(1) == 0)
        def _():
            o_ref[...] = jnp.zeros_like(o_ref)

        o_ref[...] += _dot_tn(a_ref[...], b_ref[...])

    return pl.pallas_call(
        body, out_shape=SDS(out_shape, F32), grid=(n_par, S // tk),
        in_specs=[a_spec(tk), b_spec(tk)], out_specs=out_spec,
        compiler_params=_cparams("parallel", "arbitrary"), name=name)(a, b)


def _mix_bwd(dx1b, wo, proj, ya, yr, wa, wr, att, exchanges=()):
    S = dx1b.shape[0]
    tm = 512
    gate0 = 15 * COLB

    def body(d_ref, wo_ref, ga_ref, gr_ref, ya_ref, yr_ref, wa_ref, wr_ref, att_ref,
             dya_ref, dyr_ref, dp_ref, datt_ref, rho_ref, dyi_ref):
        pieces = _row_pieces(tm, 256)
        dms = [_dot_nt(d_ref[rows, :], wo_ref[...]) for rows in pieces]
        branch = []
        for rows, dm in zip(pieces, dms):
            sa = _sigmoid(ga_ref[rows, :].astype(F32))
            sr = _sigmoid(gr_ref[rows, :].astype(F32))
            dya, dyr = (dm * sa).astype(BF16), (dm * sr).astype(BF16)
            dya_ref[rows, :] = dya
            dyr_ref[rows, :] = dyr
            dp_ref[rows, 0:D_MODEL] = (dm * ya_ref[rows, :].astype(F32) * sa * (1.0 - sa)).astype(BF16)
            dp_ref[rows, D_MODEL:2 * D_MODEL] = (dm * yr_ref[rows, :].astype(F32) * sr * (1.0 - sr)).astype(BF16)
            branch.append((dya, dyr))
        lane = lax.broadcasted_iota(jnp.int32, (256, 128), 1)
        lo = lane < 64
        for rows, (dya, dyr) in zip(pieces, branch):
            datt = _dot_nt(dya, wa_ref[...])
            datt_ref[rows, :] = datt.astype(BF16)
            dyi_ref[rows, :] = _dot_nt(dyr, wr_ref[...]).astype(BF16)
            prod = datt * att_ref[rows, :].astype(F32)
            rho = jnp.zeros((256, 128), F32)
            for c in range(4):
                pc = prod[:, c * 128:(c + 1) * 128]
                tot = jnp.sum(pc, axis=-1, keepdims=True)
                low = jnp.sum(jnp.where(lo, pc, 0.0), axis=-1, keepdims=True)
                rho = jnp.where(lane // 16 == 2 * c, low, jnp.where(lane // 16 == 2 * c + 1, tot - low, rho))
            rho_ref[rows, :] = rho

    row = lambda w: pl.BlockSpec((tm, w), lambda i: (i, 0))
    cols = lambda c0, w: pl.BlockSpec((pl.Element(tm), pl.Element(w)), lambda i: (i * tm, c0))
    resident = lambda r, c: pl.BlockSpec((r, c), lambda i: (0, 0), pipeline_mode=pl.Buffered(1))
    return _carrier_call(
        body, (dx1b, wo, proj, proj, ya, yr, wa, wr, att),
        out_shape=(SDS((S, D_MODEL), BF16), SDS((S, D_MODEL), BF16), SDS((S, PROJ_W), BF16),
                   SDS((S, 512), BF16), SDS((S, 128), F32), SDS((S, D_MODEL), BF16)),
        grid=(S // tm,),
        in_specs=[row(D_MODEL), resident(D_MODEL, D_MODEL), cols(gate0, D_MODEL), cols(gate0 + D_MODEL, D_MODEL),
                  row(D_MODEL), row(D_MODEL), resident(512, D_MODEL), resident(D_MODEL, D_MODEL), row(512)],
        out_specs=(row(D_MODEL), row(D_MODEL), cols(gate0, 2 * D_MODEL), row(512), row(128), row(D_MODEL)),
        sem=("parallel",), name="mix_bwd", exchanges=exchanges)


def _attn_bwd(qkv, datt, lse, rho, rtab, d, gi, exchanges=()):
    L = qkv.shape[0]
    nb = L // BLK
    T = d * nb

    def body(q_ref, kc_ref, kp_ref, vc_ref, vp_ref, do_ref, lse_ref, rho_ref, tq_ref, tk_ref,
             dq_ref, dk_ref, dv_ref, ck, cv):
        t = pl.program_id(0)
        n = jnp.minimum(t, T - 1) % nb

        @pl.when(t == 0)
        def _():
            ck[...] = jnp.zeros_like(ck)
            cv[...] = jnp.zeros_like(cv)

        def store_rot(ref, val, t_ref, c):
            sl = slice(c * 128, (c + 1) * 128)
            ref[:, sl] = _unrot(val, t_ref[0], t_ref[1], t_ref[2], 32).astype(BF16)

        @pl.when(t < T)
        def _():
            mask = _band_mask(n)
            mask2 = jnp.concatenate([mask, mask], axis=0)
            lo = lax.broadcasted_iota(jnp.int32, (BLK, 128), 1) < 64

            def stacked(a):
                return jnp.concatenate([jnp.where(lo, a, jnp.zeros_like(a)), jnp.where(lo, jnp.zeros_like(a), a)], axis=0)

            def head_cols(ref, c):
                return jnp.concatenate([jnp.broadcast_to(ref[:, 32 * c:32 * c + 1], (BLK, 2 * BLK)),
                                        jnp.broadcast_to(ref[:, 32 * c + 16:32 * c + 17], (BLK, 2 * BLK))], axis=0)

            ops, raw = [], []
            for c in range(4):
                sl = slice(c * 128, (c + 1) * 128)
                q2, do2 = stacked(q_ref[:, sl]), stacked(do_ref[:, sl])
                k = jnp.concatenate([kp_ref[:, sl], kc_ref[:, sl]], axis=0)
                v = jnp.concatenate([vp_ref[:, sl], vc_ref[:, sl]], axis=0)
                ops.append((q2, do2, k))
                raw.append((_dot_nt(q2, k), _dot_nt(do2, v)))
            grads = []
            for c, (s, dp) in enumerate(raw):
                p = jnp.where(mask2, jnp.exp(s * 0.125 - head_cols(lse_ref, c)), 0.0)
                grads.append(((p * (dp - head_cols(rho_ref, c)) * 0.125).astype(BF16), p.astype(BF16)))
            for c, ((q2, do2, k), (ds, pb)) in enumerate(zip(ops, grads)):
                sl = slice(c * 128, (c + 1) * 128)
                dq2 = _dot(ds, k)
                dq_c = jnp.where(lo, dq2[:BLK], dq2[BLK:])
                dk_c = _dot_tn(ds, q2)
                dv_c = _dot_tn(pb, do2)
                store_rot(dq_ref, dq_c, tq_ref, c)
                store_rot(dk_ref, ck[:, sl] + dk_c[:BLK], tk_ref, c)
                dv_ref[:, sl] = (cv[:, sl] + dv_c[:BLK]).astype(BF16)
                ck[:, sl] = dk_c[BLK:]
                cv[:, sl] = dv_c[BLK:]

        @pl.when(t == T)
        def _():
            for c in range(4):
                sl = slice(c * 128, (c + 1) * 128)
                store_rot(dk_ref, ck[:, sl], tk_ref, c)
            dv_ref[...] = cv[...].astype(BF16)

    blk_of = lambda t: (jnp.minimum(t, T - 1) % nb, jnp.minimum(t, T - 1) // nb)
    cur = lambda t: blk_of(t)
    prev = lambda t: (jnp.maximum(blk_of(t)[0] - 1, 0), blk_of(t)[1])
    fin = lambda t: blk_of(jnp.maximum(t - 1, 0))
    col = _qkv_col(d, gi)
    qkv_spec = lambda kind, which: pl.BlockSpec((BLK, 512), lambda t: (which(t)[0], col(kind, which(t)[1])))
    row_spec = lambda w, which: pl.BlockSpec((BLK, w), lambda t: which(t))
    tab_spec = lambda which: pl.BlockSpec((3, BLK, 128), lambda t: (0, *which(t)))
    return _carrier_call(
        body, (qkv, qkv, qkv, qkv, qkv, datt, lse, rho, rtab, rtab),
        out_shape=(SDS((L, d * 512), BF16),) * 3, grid=(T + 1,),
        in_specs=[qkv_spec(0, cur), qkv_spec(1, cur), qkv_spec(1, prev), qkv_spec(2, cur), qkv_spec(2, prev),
                  row_spec(512, cur), row_spec(128, cur), row_spec(128, cur), tab_spec(cur), tab_spec(fin)],
        out_specs=(row_spec(512, cur), row_spec(512, fin), row_spec(512, fin)),
        scratch_shapes=[pltpu.VMEM((BLK, 512), F32), pltpu.VMEM((BLK, 512), F32)],
        sem=("arbitrary",), name=f"attn_bwd_g{gi}", exchanges=exchanges)


def _ret_bwd(proj, rn, rstd, dyrin, states, tab, consts, dproj, exchanges=()):
    S = proj.shape[0]
    nc = S // BLK
    dmask, zeta, xi, dec = consts

    def body(q_ref, k_ref, v0_ref, v1_ref, g0_ref, g1_ref, rn_ref, rs_ref, dy_ref, st_ref, tq_ref, tk_ref,
             dm_ref, z_ref, x_ref, dec_ref, dp_prev, dp_ref, dR):
        dq_ref, dk_ref = dp_ref.at[:, 0:512], dp_ref.at[:, 512:1024]
        dv_ref, dgr_ref = dp_ref.at[:, 1024:2048], dp_ref.at[:, 2048:3072]

        @pl.when(pl.program_id(0) == 0)
        def _():
            dR[...] = jnp.zeros_like(dR)

        dobs = []
        for h in range(RET_HEADS):
            vs = slice((h % 2) * 256, (h % 2 + 1) * 256)
            os_ = slice(h * 256, (h + 1) * 256)
            gr = (g0_ref if h < 2 else g1_ref)[:, vs].astype(F32)
            sg = _sigmoid(gr)
            rn_v = rn_ref[:, os_].astype(F32)
            dyi = dy_ref[:, os_].astype(F32)
            dgr_ref[:, os_] = (dyi * rn_v * sg * (1.0 + gr * (1.0 - sg))).astype(BF16)
            drn = dyi * gr * sg
            rstd = jnp.broadcast_to(rs_ref[:, 16 * h:16 * h + 1], (BLK, 256))
            do = rstd * (drn - jnp.mean(drn, axis=-1, keepdims=True) - rn_v * jnp.mean(drn * rn_v, axis=-1, keepdims=True))
            dobs.append(do.astype(BF16))
        first = []
        for h in range(RET_HEADS):
            hs = slice(h * 128, (h + 1) * 128)
            q, k = q_ref[:, hs], k_ref[:, hs]
            v = (v0_ref if h < 2 else v1_ref)[:, (h % 2) * 256:(h % 2 + 1) * 256]
            dob, dRb = dobs[h], dR[h].astype(BF16)
            kz = (k.astype(F32) * z_ref[h]).astype(BF16)
            qx = (q.astype(F32) * x_ref[h]).astype(BF16)
            first.append((q, k, _dot_nt(q, k), _dot_nt(dob, v), _dot(kz, dRb), _dot_nt(dob, st_ref[h]),
                          _dot_nt(v, dRb), _dot_tn(qx, dob)))
        masked = [((s * dm_ref[h]).astype(BF16), (dsr * dm_ref[h]).astype(BF16))
                  for h, (_, _, s, dsr, _, _, _, _) in enumerate(first)]
        for h in range(RET_HEADS):
            hs = slice(h * 128, (h + 1) * 128)
            os_ = slice(h * 256, (h + 1) * 256)
            q, k, _, _, dv_state, dq_state, dk_state, dr_new = first[h]
            sD, dS = masked[h]
            dv_ref[:, os_] = (_dot_tn(sD, dobs[h]) + dv_state).astype(BF16)
            dq = _dot(dS, k) + dq_state * x_ref[h]
            dk = _dot_tn(dS, q) + dk_state * z_ref[h]
            dR[h] = dR[h] * dec_ref[h, 0:1, :] + dr_new
            dq_ref[:, hs] = _unrot(dq, tq_ref[0], tq_ref[1], tq_ref[2], 1).astype(BF16)
            dk_ref[:, hs] = _unrot(dk, tk_ref[0], tk_ref[1], tk_ref[2], 1).astype(BF16)

    rc = lambda c: nc - 1 - c
    cst = lambda shape: pl.BlockSpec(shape, lambda c: (0, 0, 0))
    blk = lambda j: pl.BlockSpec((BLK, 512), lambda c: (rc(c), j))
    row = lambda w: pl.BlockSpec((BLK, w), lambda c: (rc(c), 0))
    (dproj,), xres = _carrier_call(
        body, (proj, proj, proj, proj, proj, proj, rn, rstd, dyrin, states, tab, tab, dmask, zeta, xi, dec, dproj),
        out_shape=(SDS((S, PROJ_W), BF16),), grid=(nc,),
        in_specs=[blk(QR_B), blk(KR_B), blk(11), blk(12), blk(13), blk(14), row(1024), row(128), row(1024),
                  pl.BlockSpec((RET_HEADS, None, BLK, 256), lambda c: (0, rc(c), 0, 0)),
                  pl.BlockSpec((None, 3, BLK, 128), lambda c: (1, 0, rc(c), 0)),
                  pl.BlockSpec((None, 3, BLK, 128), lambda c: (2, 0, rc(c), 0)),
                  cst((RET_HEADS, BLK, BLK)), cst((RET_HEADS, BLK, 128)), cst((RET_HEADS, BLK, 128)), cst((RET_HEADS, 8, 256)),
                  ANY],
        out_specs=(pl.BlockSpec((pl.Element(BLK), pl.Element(6 * COLB)), lambda c: (rc(c) * BLK, QR_B * COLB)),),
        scratch_shapes=[pltpu.VMEM((RET_HEADS, BLK, 256), F32)],
        sem=("arbitrary",), name="ret_bwd", exchanges=exchanges, in_out_aliases={16: 0})
    return dproj, xres


def _wgrad_in_half(ht, dproj, sidx, kept, exchanges=()):
    S = dproj.shape[0]
    tk = 2048
    half = (lambda sx: sx[4]) if kept else (lambda sx: 1 - sx[4])

    def body(a_ref, b_ref, o_ref):
        @pl.when(pl.program_id(1) == 0)
        def _():
            o_ref[...] = jnp.zeros_like(o_ref)

        o_ref[...] += _dot(a_ref[...], b_ref[...])

    (g,), xres = _carrier_call(
        body, (ht, dproj), out_shape=(SDS((D_MODEL // 2, PROJ_W), F32),), grid=(N_SHARD, S // tk),
        in_specs=[pl.BlockSpec((D_MODEL // 2, tk), lambda s, k, sx: (half(sx), k)),
                  pl.BlockSpec((tk, W_IN_S), lambda s, k, sx: (k, s))],
        out_specs=(pl.BlockSpec((D_MODEL // 2, W_IN_S), lambda s, k, sx: (0, s)),),
        sem=("parallel", "arbitrary"), name="wgrad_in_kept" if kept else "wgrad_in_sent", exchanges=exchanges,
        prefetch=sidx)
    return g, xres


def _in_proj_bwd(dproj, w_in, x, g1, dx1, exchanges=()):
    S = x.shape[0]
    tm = 1024

    def body(d_ref, w_ref, x_ref, g_ref, dx1_ref, dx_ref, dgn_ref, acc):
        i, s = pl.program_id(0), pl.program_id(1)

        @pl.when(s == 0)
        def _():
            acc[...] = jnp.zeros_like(acc)

        @pl.when((i == 0) & (s == 0))
        def _():
            dgn_ref[...] = jnp.zeros_like(dgn_ref)

        acc[...] += _dot_nt(d_ref[...], w_ref[...])

        @pl.when(s == N_SHARD - 1)
        def _():
            xv = x_ref[...]
            r = lax.rsqrt(jnp.mean(xv * xv, axis=-1, keepdims=True) + NORM_EPS)
            xh = xv * r
            dh = acc[...]
            dgn_ref[...] += jnp.sum(dh * xh, axis=0, keepdims=True)
            dxh = dh * g_ref[...]
            dx_ref[...] = dx1_ref[...] + r * (dxh - xh * jnp.mean(dxh * xh, axis=-1, keepdims=True))

    row = pl.BlockSpec((tm, D_MODEL), lambda i, s: (i, 0))
    vec = pl.BlockSpec((1, D_MODEL), lambda i, s: (0, 0))
    (gx, dg), xres = _carrier_call(
        body, (dproj, w_in, x, g1, dx1),
        out_shape=(SDS((S, D_MODEL), F32), SDS((1, D_MODEL), F32)), grid=(S // tm, N_SHARD),
        in_specs=[pl.BlockSpec((tm, W_IN_S), lambda i, s: (i, s)),
                  pl.BlockSpec((D_MODEL, W_IN_S), lambda i, s: (0, s)), row, vec, row],
        out_specs=(row, vec), scratch_shapes=[pltpu.VMEM((tm, D_MODEL), F32)],
        sem=("arbitrary", "arbitrary"), name="in_proj_bwd", exchanges=exchanges)
    return gx, dg, xres


def _sub_view(a, d):
    S, W = a.shape
    return a.reshape(S // d, d * W)


def _step(x, tgt, g1, g2, g3, comm):
    S = x.shape[0]
    tab_np = _tables(S)
    tab = jnp.asarray(tab_np)
    consts = _ret_consts()

    (h, ht, *casts), xres = _rms_fwd(x, g1, comm.to_cast(), comm.carry("rms_fwd"))
    comm.cast_done(casts)
    comm.took("rms_fwd", xres)
    w_in = comm.weight(0)
    proj, xres = _in_proj(h, w_in, tab, comm.carry("in_proj"))
    comm.took("in_proj", xres)
    qkvs, o_parts, lse_parts = [], [], []
    for gi, d in enumerate(DILATIONS):
        qkv = proj if d == 1 else _qkv_to_sub(proj, d, gi)
        (o_g, lse_g), xres = _attn_fwd(qkv, d, gi, comm.carry(f"attn_fwd_g{gi}"))
        comm.took(f"attn_fwd_g{gi}", xres)
        qkvs.append(qkv)
        o_parts.append(o_g)
        lse_parts.append(lse_g)
    att, lse_tot = _attn_merge(o_parts, lse_parts)
    (yrin, rn, rstd, states), xres = _ret_fwd(proj, consts, comm.carry("ret_fwd"))
    comm.took("ret_fwd", xres)
    wa, wr, wo = comm.weight(1), comm.weight(2), comm.weight(3)
    (merged, ya, yr, x1, h2), xres = _mix_out(att, yrin, proj, wa, wr, wo, x, g2, comm.carry("mix_out"))
    comm.took("mix_out", xres)
    wg, wu = comm.weight(4), comm.weight(5)
    (gte, up, act), xres = _ffn_up(h2, wg, wu, comm.carry("ffn_up"))
    comm.took("ffn_up", xres)
    wd = comm.weight(6)
    dx2, dx2b, dg3, loss_p = _ffn_down_loss(act, wd, x1, g3, tgt)

    dgte, dup, dx1, dx1b, dg2 = _ffn_bwd(dx2b, dx2, wd, wg, wu, gte, up, x1, g2)
    tok3 = lambda w: (lambda tk: pl.BlockSpec((None, tk, w), lambda p, k: (p, k, 0)))
    tok2 = lambda w: (lambda tk: pl.BlockSpec((tk, w), lambda p, k: (k, 0)))
    g_d = _wgrad("wgrad_down", act, dx2b, tok3(HID_S), tok2(D_MODEL), (N_SHARD, HID_S, D_MODEL),
                 pl.BlockSpec((None, HID_S, D_MODEL), lambda p, k: (p, 0, 0)), N_SHARD, S)
    g_g = _wgrad("wgrad_gate", h2, dgte, tok2(D_MODEL), tok3(HID_S), (N_SHARD, D_MODEL, HID_S),
                 pl.BlockSpec((None, D_MODEL, HID_S), lambda p, k: (p, 0, 0)), N_SHARD, S)
    g_u = _wgrad("wgrad_up", h2, dup, tok2(D_MODEL), tok3(HID_S), (N_SHARD, D_MODEL, HID_S),
                 pl.BlockSpec((None, D_MODEL, HID_S), lambda p, k: (p, 0, 0)), N_SHARD, S)
    comm.grads({4: g_g, 5: g_u, 6: g_d})
    (dya, dyr, dproj, datt, rho, dyrin), xres = _mix_bwd(dx1b, wo, proj, ya, yr, wa, wr, att, comm.carry("mix_bwd"))
    comm.took("mix_bwd", xres)
    colblk = lambda w: (lambda tk: pl.BlockSpec((tk, w), lambda p, k: (k, p)))
    g_o = _wgrad("wgrad_out", merged, dx1b, colblk(256), tok2(D_MODEL), (D_MODEL, D_MODEL),
                 pl.BlockSpec((256, D_MODEL), lambda p, k: (p, 0)), 4, S)
    g_a = _wgrad("wgrad_attn", att, dya, tok2(512), colblk(512), (512, D_MODEL),
                 pl.BlockSpec((512, 512), lambda p, k: (0, p)), 2, S)
    g_r = _wgrad("wgrad_ret", yrin, dyr, colblk(256), tok2(D_MODEL), (D_MODEL, D_MODEL),
                 pl.BlockSpec((256, D_MODEL), lambda p, k: (p, 0)), 4, S)
    comm.grads({1: g_a, 2: g_r.reshape(N_SHARD, 256, D_MODEL), 3: g_o.reshape(N_SHARD, 256, D_MODEL)})
    dproj, xres = _ret_bwd(proj, rn, rstd, dyrin, states, tab, consts, dproj, comm.carry("ret_bwd"))
    comm.took("ret_bwd", xres)
    dqs, dks, dvs = [], [], []
    for gi, d in enumerate(DILATIONS):
        rtab = jnp.asarray(tab_np[0].reshape(3, S // d, d * 128))
        (dq, dk, dv), xres = _attn_bwd(qkvs[gi], _sub_view(datt, d), _sub_view(lse_tot, d), _sub_view(rho, d), rtab, d, gi,
                                       comm.carry(f"attn_bwd_g{gi}"))
        comm.took(f"attn_bwd_g{gi}", xres)
        dqs.append(dq)
        dks.append(dk)
        dvs.append(dv)
    dproj = _assemble_dproj((dqs, dks, dvs), dproj)
    g_sent, xres = _wgrad_in_half(ht, dproj, comm.sidx, False, comm.carry("wgrad_in_sent"))
    comm.took("wgrad_in_sent", xres)
    comm.grads({"in_sent": g_sent})
    g_kept, xres = _wgrad_in_half(ht, dproj, comm.sidx, True, comm.carry("wgrad_in_kept"))
    comm.grads({"in_kept": g_kept})
    comm.took("wgrad_in_kept", xres)
    grad_x, dg1, xres = _in_proj_bwd(dproj, w_in, x, g1, dx1, comm.carry("in_proj_bwd"))
    comm.took("in_proj_bwd", xres)
    return loss_p, grad_x, (dg1, dg2, dg3)


W_KINDS = ("col", "col", "lead", "lead", "lead", "lead", "lead")
W_SHARD = ((1024, W_IN_S), (512, 256), (256, 1024), (256, 1024), (1024, HID_S), (1024, HID_S), (HID_S, 1024))
N_W = len(W_KINDS)


def _full_shape(wi):
    R, C = W_SHARD[wi]
    return (R, N_SHARD * C) if W_KINDS[wi] == "col" else (N_SHARD, R, C)


def _view(ref, wi, s, half):
    R, C = W_SHARD[wi]
    rows = pl.ds(half * (R // 2), R // 2)
    if W_KINDS[wi] == "col":
        return ref.at[rows, pl.ds(pl.multiple_of(s * C, 128), C)]
    return ref.at[s, rows, :]


def _mesh_pos():
    x, y, c = lax.axis_index("x"), lax.axis_index("y"), lax.axis_index("c")
    chips = [(1 - x, y), (x, 1 - y), (1 - x, 1 - y)]
    return x, y, c, chips


def _cast_bf16(a):
    R, C = a.shape
    tr = R // 2 if R % 32 == 0 else R

    def body(a_ref, o_ref):
        o_ref[...] = a_ref[...].astype(BF16)

    spec = pl.BlockSpec((tr, C), lambda i: (i, 0))
    return pl.pallas_call(body, out_shape=SDS((R, C), BF16), grid=(R // tr,), in_specs=[spec], out_specs=spec,
                          compiler_params=_cparams("parallel"), name=f"cast_{R}x{C}")(a)


def _remote(send, recv, k, src, dst, to):
    return pltpu.make_async_remote_copy(src_ref=src, dst_ref=dst, send_sem=send.at[k], recv_sem=recv.at[k],
                                        device_id=to, device_id_type=MESH)


def _ex_gather_ring(wis, shards):
    n = len(wis)

    def build(sh, full, send, recv, loc):
        x, y, c, _ = _mesh_pos()
        s_me, sib = 2 * x + y, (x, y, 1 - c)
        xn, yn = (1 - x, y), (x, 1 - y)
        flip = lambda a, b: a + b - 2 * a * b
        via = (flip(x, 1 - c), flip(y, c))
        onto = (flip(x, c), flip(y, 1 - c))
        shard_of = lambda chip: 2 * chip[0] + chip[1]
        starts, waits, sent = [], [], []
        for i, wi in enumerate(wis):
            Rh = W_SHARD[wi][0] // 2
            for hf in range(2):
                cp = pltpu.make_async_copy(sh[i].at[pl.ds(hf * Rh, Rh), :], _view(full[i], wi, s_me, hf), loc.at[2 * i + hf])
                starts.append(cp)
                sent.append(cp.wait)
            for j, chip in enumerate((xn, yn)):
                cp = _remote(send, recv, 6 * i + j, sh[i].at[pl.ds(c * Rh, Rh), :], _view(full[i], wi, s_me, c), (*chip, c))
                starts.append(cp)
                sent.append(cp.wait_send)

        def pass_to_sibling(i, wi, k, s):
            mine = _view(full[i], wi, s, c)
            fw = _remote(send, recv, 6 * i + k, mine, mine, sib)
            waits.append(fw.start)
            sent.append(fw.wait_send)

        for i, wi in enumerate(wis):
            for j, chip in enumerate((xn, yn)):
                land = _view(full[i], wi, shard_of(chip), c)
                waits.append(_remote(send, recv, 6 * i + j, land, land, (*chip, c)).wait_recv)
                pass_to_sibling(i, wi, 3 + j, shard_of(chip))
            relay = _view(full[i], wi, shard_of(via), c)
            fw = _remote(send, recv, 6 * i + 2, relay, relay, (*onto, c))
            waits.append(fw.start)
            sent.append(fw.wait_send)
        s_diag = 2 * (1 - x) + (1 - y)
        for i, wi in enumerate(wis):
            land = _view(full[i], wi, s_diag, c)
            waits.append(_remote(send, recv, 6 * i + 2, land, land, (*onto, c)).wait_recv)
            pass_to_sibling(i, wi, 5, s_diag)
        for i, wi in enumerate(wis):
            for k, s in ((3, shard_of(xn)), (4, shard_of(yn)), (5, s_diag)):
                land = _view(full[i], wi, s, 1 - c)
                waits.append(_remote(send, recv, 6 * i + k, land, land, sib).wait_recv)
        return starts, waits + sent

    return _Exchange(shards, [SDS(_full_shape(wi), BF16) for wi in wis], {}, 6 * n, 2 * n, build)


def _ex_gather_ici(wis, shards, then_d2d=False):
    n = len(wis)

    def build(ins, outs, send, recv, loc):
        x, y, c, chips = _mesh_pos()
        s_me, sib = 2 * x + y, (x, y, 1 - c)
        starts, waits, after = [], [], []
        for i, wi in enumerate(wis):
            Rh = W_SHARD[wi][0] // 2
            for hf in range(2):
                cp = pltpu.make_async_copy(ins[i].at[pl.ds(hf * Rh, Rh), :], _view(outs[i], wi, s_me, hf), loc.at[2 * i + hf])
                starts.append(cp)
                waits.append(cp.wait)
            for j, chip in enumerate(chips):
                cp = _remote(send, recv, 3 * i + j, ins[i].at[pl.ds(c * Rh, Rh), :], _view(outs[i], wi, s_me, c), (*chip, c))
                land = _view(outs[i], wi, 2 * chip[0] + chip[1], c)
                starts.append(cp)
                waits += [cp.wait_send, _remote(send, recv, 3 * i + j, land, land, (*chip, c)).wait_recv]
                if then_d2d:
                    theirs = _view(outs[i], wi, 2 * chip[0] + chip[1], 1 - c)
                    fw = _remote(send, recv, 3 * n + 3 * i + j, land, land, sib)
                    waits.append(fw.start)
                    after += [fw.wait_send, _remote(send, recv, 3 * n + 3 * i + j, theirs, theirs, sib).wait_recv]
        return starts, waits + after

    return _Exchange(shards, [SDS(_full_shape(wi), BF16) for wi in wis], {}, (6 if then_d2d else 3) * n, 2 * n, build)


def _ex_gather_d2d(wis, fulls):
    def build(ins, outs, send, recv, loc):
        x, y, c, chips = _mesh_pos()
        sib = (x, y, 1 - c)
        starts, waits = [], []
        for i, wi in enumerate(wis):
            for j, chip in enumerate(chips):
                mine = _view(outs[i], wi, 2 * chip[0] + chip[1], c)
                theirs = _view(outs[i], wi, 2 * chip[0] + chip[1], 1 - c)
                cp = _remote(send, recv, 3 * i + j, mine, mine, sib)
                starts.append(cp)
                waits += [cp.wait_send, _remote(send, recv, 3 * i + j, theirs, theirs, sib).wait_recv]
        return starts, waits

    return _Exchange(fulls, [SDS(f.shape, BF16) for f in fulls], {i: i for i in range(len(wis))}, 3 * len(wis), 0, build)


def _half_shape(wi):
    R, C = W_SHARD[wi]
    return (R // 2, N_SHARD * C) if W_KINDS[wi] == "col" else (N_SHARD, R // 2, C)


def _ex_pair(wis, grads):
    def build(ins, outs, send, recv, loc):
        x, y, c, _ = _mesh_pos()
        starts, waits = [], []
        for i, wi in enumerate(wis):
            Rh = W_SHARD[wi][0] // 2
            rows = pl.ds((1 - c) * Rh, Rh)
            if tuple(ins[i].shape) == _half_shape(wi):
                src = ins[i]
            else:
                src = ins[i].at[rows, :] if W_KINDS[wi] == "col" else ins[i].at[:, rows, :]
            cp = _remote(send, recv, i, src, outs[i], (x, y, 1 - c))
            starts.append(cp)
            waits.append(cp.wait)
        return starts, waits

    return _Exchange(grads, [SDS(_half_shape(wi), F32) for wi in wis], {}, len(wis), 0, build)


def _ex_chip(wis, pbs):
    def build(ins, outs, send, recv, loc):
        x, y, c, chips = _mesh_pos()
        starts, waits = [], []
        for i, wi in enumerate(wis):
            for j, chip in enumerate(chips):
                cp = _remote(send, recv, 3 * i + j, ins[i].at[j], outs[i].at[j], (*chip, c))
                starts.append(cp)
                waits.append(cp.wait)
        return starts, waits

    shapes = [SDS((3, W_SHARD[wi][0] // 2, W_SHARD[wi][1]), BF16) for wi in wis]
    return _Exchange(pbs, shapes, {}, 3 * len(wis), 0, build)


def _ex_share(wis, halves):
    def build(ins, outs, send, recv, loc):
        x, y, c, _ = _mesh_pos()
        sib = (x, y, 1 - c)
        starts, waits = [], []
        for i, wi in enumerate(wis):
            cp = _remote(send, recv, i, outs[i].at[c], outs[i].at[c], sib)
            starts.append(cp)
            waits += [cp.wait_send, _remote(send, recv, i, outs[i].at[1 - c], outs[i].at[1 - c], sib).wait_recv]
        return starts, waits

    return _Exchange(halves, [SDS(h.shape, F32) for h in halves], {i: i for i in range(len(wis))}, len(wis), 0, build)


def _row_tile(rh, C):
    best = 16
    for t in range(16, rh + 1, 16):
        if rh % t == 0 and t * C * 4 <= (3 << 19):
            best = t
    return best


def _pair_sum(wi, g, ra, sidx):
    R, C = W_SHARD[wi]
    Rh = R // 2
    tr = _row_tile(Rh, C)
    nt = Rh // tr
    off = 0 if tuple(g.shape) == _half_shape(wi) else nt
    col = W_KINDS[wi] == "col"

    def body(sidx_ref, *refs):
        gs, rs = refs[:4], refs[4:8]
        own_ref, pb_ref = refs[8:]
        own_ref[...] = gs[0][...] + rs[0][...]
        for j in range(3):
            pb_ref[j] = (gs[1 + j][...] + rs[1 + j][...]).astype(BF16)

    def gspec(slot):
        if col:
            return pl.BlockSpec((tr, C), lambda i, sx: (sx[4] * off + i, sx[slot]))
        return pl.BlockSpec((None, tr, C), lambda i, sx: (sx[slot], sx[4] * off + i, 0))

    def rspec(slot):
        if col:
            return pl.BlockSpec((tr, C), lambda i, sx: (i, sx[slot]))
        return pl.BlockSpec((None, tr, C), lambda i, sx: (sx[slot], i, 0))

    return pl.pallas_call(
        body, out_shape=(SDS((Rh, C), F32), SDS((3, Rh, C), BF16)),
        grid_spec=pltpu.PrefetchScalarGridSpec(
            num_scalar_prefetch=1, grid=(nt,),
            in_specs=[gspec(k) for k in range(4)] + [rspec(k) for k in range(4)],
            out_specs=(pl.BlockSpec((tr, C), lambda i, sx: (i, 0)), pl.BlockSpec((3, tr, C), lambda i, sx: (0, i, 0)))),
        compiler_params=_cparams("arbitrary"), name=f"pair_sum_w{wi}")(sidx, g, g, g, g, ra, ra, ra, ra)


def _chip_sum(wi, own, rb, sidx):
    R, C = W_SHARD[wi]
    Rh = R // 2
    tr = _row_tile(Rh, C)

    def body(sidx_ref, own_ref, rb_ref, o_ref):
        o_ref[...] = ((own_ref[...] + rb_ref[0].astype(F32)) + rb_ref[1].astype(F32)) + rb_ref[2].astype(F32)

    return pl.pallas_call(
        body, out_shape=SDS((2, Rh, C), F32),
        grid_spec=pltpu.PrefetchScalarGridSpec(
            num_scalar_prefetch=1, grid=(Rh // tr,),
            in_specs=[pl.BlockSpec((tr, C), lambda i, sx: (i, 0)), pl.BlockSpec((3, tr, C), lambda i, sx: (0, i, 0))],
            out_specs=pl.BlockSpec((None, tr, C), lambda i, sx: (sx[4], i, 0))),
        compiler_params=_cparams("arbitrary"), name=f"chip_sum_w{wi}")(sidx, own, rb)


def _gain_allgather(blk, ex):
    m_per, n = blk.shape
    n_in, n_out = len(ex.ins), len(ex.out_shapes)

    def body(x_ref, *rest):
        xin, out_ref, xout = rest[:n_in], rest[n_in], rest[n_in + 1:n_in + 1 + n_out]
        send_sems, recv_sems, local_sem = rest[n_in + 1 + n_out:n_in + 4 + n_out]
        ex_starts, ex_waits = ex.build(xin, xout, *rest[n_in + 4 + n_out:])
        for cp in ex_starts:
            cp.start()
        x, y, c, chips = _mesh_pos()
        me, sibling = (x, y, c), (x, y, 1 - c)

        def rows(px, py, pc):
            return out_ref.at[pl.ds((4 * px + 2 * py + pc) * m_per, m_per), :]

        def copy(k, block, to, src=None):
            return pltpu.make_async_remote_copy(
                src_ref=rows(*block) if src is None else src, dst_ref=rows(*block),
                send_sem=send_sems.at[k], recv_sem=recv_sems.at[k], device_id=to, device_id_type=MESH)

        mine = pltpu.make_async_copy(x_ref, rows(*me), local_sem)
        mine.start()
        first = [copy(0, me, sibling, src=x_ref)]
        first += [copy(1 + j, me, (*chip, c), src=x_ref) for j, chip in enumerate(chips)]
        for cp in first:
            cp.start()
        passed = [copy(4 + j, (*chip, c), sibling) for j, chip in enumerate(chips)]
        for j, chip in enumerate(chips):
            copy(1 + j, (*chip, c), me).wait_recv()
            passed[j].start()
        copy(0, sibling, me).wait_recv()
        for j, chip in enumerate(chips):
            copy(4 + j, (*chip, 1 - c), me).wait_recv()
        for cp in first + passed:
            cp.wait_send()
        mine.wait()
        for w in ex_waits:
            w()

    vm = pl.BlockSpec(memory_space=pltpu.VMEM)
    res = pl.pallas_call(
        body, out_shape=(SDS((8 * m_per, n), blk.dtype), *ex.out_shapes),
        in_specs=[vm] + [ANY] * n_in, out_specs=(vm, *[ANY] * n_out),
        input_output_aliases={1 + a: 1 + o for a, o in ex.aliases.items()},
        scratch_shapes=[pltpu.SemaphoreType.DMA((7,)), pltpu.SemaphoreType.DMA((7,)), pltpu.SemaphoreType.DMA] + ex.sems(),
        name="gain_allgather")(blk, *ex.ins)
    return res[0], tuple(res[1:])


def _adam_math(w, g, m, v):
    mn = ADAM_B1 * m + (1.0 - ADAM_B1) * g
    vn = ADAM_B2 * v + (1.0 - ADAM_B2) * (g * g)
    mh = mn / (1.0 - ADAM_B1 ** ADAM_STEP)
    vh = vn / (1.0 - ADAM_B2 ** ADAM_STEP)
    return -ADAM_LR * (mh / (jnp.sqrt(vh) + ADAM_EPS) + ADAM_WD * w), mn, vn


def _adamw(name, ws, gs, ms, vs):
    n, steps = len(ws), 8

    def body(*refs):
        for k in range(n):
            w_ref, g_ref, m_ref, v_ref = refs[4 * k:4 * k + 4]
            go_ref, d_ref, mn_ref, vn_ref = refs[4 * n + 4 * k:4 * n + 4 * k + 4]
            g = g_ref[...]
            go_ref[...] = g
            d_ref[...], mn_ref[...], vn_ref[...] = _adam_math(w_ref[...], g, m_ref[...], v_ref[...])

    specs = [pl.BlockSpec((w.shape[0] // steps, w.shape[1]), lambda i: (i, 0)) for w in ws for _ in range(4)]
    res = pl.pallas_call(
        body, out_shape=tuple(SDS(w.shape, F32) for w in ws for _ in range(4)), grid=(steps,),
        in_specs=specs, out_specs=tuple(specs), compiler_params=_cparams("parallel"),
        name=name)(*[a for k in range(n) for a in (ws[k], gs[k], ms[k], vs[k])])
    return [tuple(res[4 * k:4 * k + 4]) for k in range(n)]


def _gain_update(gathered, w, m, v):
    def body(ga_ref, w_ref, m_ref, v_ref, g_ref, d_ref, mn_ref, vn_ref):
        g = ga_ref[0:8, :]
        for dev in range(1, 8):
            g = g + ga_ref[8 * dev:8 * dev + 8, :]
        g_ref[...] = g
        d_ref[...], mn_ref[...], vn_ref[...] = _adam_math(w_ref[...], g, m_ref[...], v_ref[...])

    return pl.pallas_call(body, out_shape=(SDS((8, 1024), F32),) * 4, name="gain_update")(gathered, w, m, v)


GROUP_FFN, GROUP_MIX, GROUP_IN = (4, 5, 6), (1, 2, 3), (0,)
REST = GROUP_MIX + GROUP_FFN


class _MeshComm:
    SCHEDULE = {
        "rms_fwd": [("ring", GROUP_IN)],
        "in_proj": [("ici", (1, 2, 3, 4))],
        "ret_fwd": [("d2d", (1, 2, 3, 4)), ("ici", (5,))],
        "mix_out": [("d2d", (5,))],
        "ffn_up": [("both", (6,))],
        "mix_bwd": [("pair", GROUP_FFN)],
        "ret_bwd": [("pair", GROUP_MIX), ("chip", (4,))],
        "attn_bwd_g0": [("chip", (5,))],
        "attn_bwd_g1": [("chip", (6,))],
        "attn_bwd_g2": [("chip", GROUP_MIX)],
        "wgrad_in_kept": [("pair", GROUP_IN), ("share", GROUP_FFN + GROUP_MIX)],
        "in_proj_bwd": [("chip", GROUP_IN)],
    }

    def __init__(self, w_in_shard, rest_f32):
        xi, yi, ci = lax.axis_index("x"), lax.axis_index("y"), lax.axis_index("c")
        self.sidx = jnp.stack([2 * xi + yi, 2 * (1 - xi) + yi, 2 * xi + (1 - yi), 2 * (1 - xi) + (1 - yi), ci]).astype(jnp.int32)
        self.shards, self.rest_f32, self.full = {0: w_in_shard}, list(rest_f32), {}
        self.g, self.own, self.pb, self.half, self.red = {}, {}, {}, {}, {}

    def to_cast(self):
        return self.rest_f32

    def cast_done(self, casts):
        self.shards.update(zip(REST, casts))

    def weight(self, wi):
        return self.full[wi].reshape(D_MODEL, D_MODEL) if wi in (2, 3) else self.full[wi]

    def grads(self, by_wi):
        self.g.update(by_wi)

    def _exchange(self, stage, wis):
        pick = lambda table: [table[wi] for wi in wis]
        if stage == "ring":
            return _ex_gather_ring(wis, pick(self.shards))
        if stage == "ici":
            return _ex_gather_ici(wis, pick(self.shards))
        if stage == "both":
            return _ex_gather_ici(wis, pick(self.shards), then_d2d=True)
        if stage == "d2d":
            return _ex_gather_d2d(wis, pick(self.full))
        if stage == "pair":
            return _ex_pair(wis, [self.g["in_sent"] if wi == 0 else self.g[wi] for wi in wis])
        if stage == "chip":
            return _ex_chip(wis, pick(self.pb))
        return _ex_share(wis, pick(self.half))

    def _landed(self, stage, wis, res):
        for wi, r in zip(wis, res):
            if stage in ("ring", "ici", "d2d", "both"):
                self.full[wi] = r
            elif stage == "pair":
                self.own[wi], self.pb[wi] = _pair_sum(wi, self.g["in_kept"] if wi == 0 else self.g[wi], r, self.sidx)
            elif stage == "chip":
                self.half[wi] = _chip_sum(wi, self.own[wi], r, self.sidx)
            else:
                self.red[wi] = r

    def carry(self, point):
        return [self._exchange(stage, wis) for stage, wis in self.SCHEDULE.get(point, ())]

    def took(self, point, xres):
        for (stage, wis), res in zip(self.SCHEDULE.get(point, ()), xres):
            self._landed(stage, wis, res)

    def last_share(self):
        return self._exchange("share", GROUP_IN)

    def reduced(self, last_shared):
        self._landed("share", GROUP_IN, last_shared)
        return [self.red[wi] for wi in range(N_W)]


def kernel(x, norm_mix_g, w_in, w_out_attn, w_out_ret, w_out, norm_ffn_g, w_ffn_gate, w_ffn_up, w_ffn_down, norm_final_g, loss_target, m_norm_mix_g, m_w_in, m_w_out_attn, m_w_out_ret, m_w_out, m_norm_ffn_g, m_w_ffn_gate, m_w_ffn_up, m_w_ffn_down, m_norm_final_g, v_norm_mix_g, v_w_in, v_w_out_attn, v_w_out_ret, v_w_out, v_norm_ffn_g, v_w_ffn_gate, v_w_ffn_up, v_w_ffn_down, v_norm_final_g):
    ws = (w_in, w_out_attn, w_out_ret, w_out, w_ffn_gate, w_ffn_up, w_ffn_down)
    ms = (m_w_in, m_w_out_attn, m_w_out_ret, m_w_out, m_w_ffn_gate, m_w_ffn_up, m_w_ffn_down)
    vs = (v_w_in, v_w_out_attn, v_w_out_ret, v_w_out, v_w_ffn_gate, v_w_ffn_up, v_w_ffn_down)
    shard2d = lambda a, wi: a.reshape(W_SHARD[wi])

    comm = _MeshComm(_cast_bf16(shard2d(ws[0], 0)), [shard2d(ws[wi], wi) for wi in REST])
    g3 = norm_final_g.reshape(1, D_MODEL)
    loss_p, grad_x, gain_g = _step(x[0], loss_target[0], norm_mix_g, norm_ffn_g, g3, comm)

    pad8 = lambda rows: jnp.concatenate([r.reshape(1, D_MODEL) for r in rows]
                                        + [jnp.zeros((8 - len(rows), D_MODEL), F32)], axis=0)
    gathered, shared = _gain_allgather(pad8((*gain_g, jnp.tile(loss_p[0:1], (1, D_MODEL // 128)))), comm.last_share())
    gred = comm.reduced(shared)

    def adam(name, wis):
        two_d = lambda arrs: [shard2d(arrs[wi], wi) for wi in wis]
        return _adamw(name, two_d(ws), [gred[wi].reshape(W_SHARD[wi]) for wi in wis], two_d(ms), two_d(vs))

    updates = dict(zip(REST + GROUP_IN, adam("adamw_rest", REST) + adam("adamw_w_in", GROUP_IN)))
    outs_g, outs_d, outs_m, outs_v = ([updates[wi][k].reshape(ws[wi].shape) for wi in range(N_W)] for k in range(4))

    gg, gd, gm, gv = _gain_update(gathered, pad8((norm_mix_g, norm_ffn_g, norm_final_g)),
                                  pad8((m_norm_mix_g, m_norm_ffn_g, m_norm_final_g)),
                                  pad8((v_norm_mix_g, v_norm_ffn_g, v_norm_final_g)))
    loss = gg[3, 0]

    def assemble(gain_rows, wlist):
        return (gain_rows[0:1], wlist[0], wlist[1], wlist[2], wlist[3], gain_rows[1:2],
                wlist[4], wlist[5], wlist[6], gain_rows[2])

    return (loss, grad_x[None], *assemble(gg, outs_g), *assemble(gd, outs_d), *assemble(gm, outs_m), *assemble(gv, outs_v))
```

```python
import functools
import math

import numpy as np
import jax
import jax.numpy as jnp
from jax import lax
from jax.experimental import pallas as pl
from jax.experimental.pallas import tpu as pltpu

F32, BF16 = jnp.float32, jnp.bfloat16
SDS = jax.ShapeDtypeStruct
MESH = pl.DeviceIdType.MESH

D_MODEL = 1024
PROJ_W = 9728
COLB = 512
N_COLB = PROJ_W // COLB
QA_B, KA_B, VA_B = 0, 3, 6
QR_B, KR_B = 9, 10
FFN_HID = 2816
N_SHARD = 4
HID_S = FFN_HID // N_SHARD
W_IN_S = PROJ_W // N_SHARD
DILATIONS = (1, 4, 16)
BLK = 128
RET_HEADS = 4
ROPE_THETA = 10000.0
NORM_EPS = 1e-6
ADAM_LR, ADAM_B1, ADAM_B2, ADAM_EPS, ADAM_WD, ADAM_STEP = 0.001, 0.9, 0.999, 1e-08, 0.01, 10
VMEM_LIMIT = 56 << 20


def _cparams(*sem):
    return pltpu.CompilerParams(dimension_semantics=sem or None, vmem_limit_bytes=VMEM_LIMIT)


def _dot(a, b):
    return jnp.dot(a, b, preferred_element_type=F32)


def _dot_nt(a, b):
    return lax.dot_general(a, b, (((1,), (1,)), ((), ())), preferred_element_type=F32)


def _dot_tn(a, b):
    return lax.dot_general(a, b, (((0,), (0,)), ((), ())), preferred_element_type=F32)


def _row_pieces(tm, sub=512):
    return [slice(i, i + sub) for i in range(0, tm, sub)]


def _sigmoid(z):
    return 0.5 * jnp.tanh(0.5 * z) + 0.5


ANY = pl.BlockSpec(memory_space=pl.ANY)


class _Exchange:
    def __init__(self, ins, out_shapes, aliases, n_sem, n_loc, build):
        self.ins, self.out_shapes, self.aliases = list(ins), list(out_shapes), dict(aliases)
        self.n_sem, self.n_loc, self.build = n_sem, n_loc, build

    def sems(self):
        return [pltpu.SemaphoreType.DMA((self.n_sem,)), pltpu.SemaphoreType.DMA((self.n_sem,)),
                pltpu.SemaphoreType.DMA((max(self.n_loc, 1),))]


def _carrier_call(body, args, *, out_shape, grid, in_specs, out_specs, scratch_shapes=(), sem, name, exchanges=(),
                  prefetch=None, in_out_aliases=None):
    out_shape, out_specs = tuple(out_shape), tuple(out_specs)
    n_in, n_out, n_scr = len(args), len(out_shape), len(scratch_shapes)
    n_pre = 0 if prefetch is None else 1
    x_args, x_outs, x_scr, spans = [], [], [], []
    aliases = {n_pre + a: o for a, o in (in_out_aliases or {}).items()}
    for ex in exchanges:
        i0, o0 = len(x_args), len(x_outs)
        for a, o in ex.aliases.items():
            aliases[n_pre + n_in + i0 + a] = n_out + o0 + o
        x_args += ex.ins
        x_outs += ex.out_shapes
        x_scr += ex.sems()
        spans.append((i0, len(ex.ins), o0, len(ex.out_shapes)))
    nx_in, nx_out = len(x_args), len(x_outs)

    def wrapped(*refs):
        refs = refs[n_pre:]
        ins, xin = refs[:n_in], refs[n_in:n_in + nx_in]
        o_base = n_in + nx_in
        outs, xout = refs[o_base:o_base + n_out], refs[o_base + n_out:o_base + n_out + nx_out]
        s_base = o_base + n_out + nx_out
        scr, xs = refs[s_base:s_base + n_scr], refs[s_base + n_scr:]

        def built(e):
            i0, ni, o0, no = spans[e]
            return exchanges[e].build(xin[i0:i0 + ni], xout[o0:o0 + no], *xs[3 * e:3 * e + 3])

        if exchanges:
            first = functools.reduce(jnp.logical_and, [pl.program_id(k) == 0 for k in range(len(grid))])
            last = functools.reduce(jnp.logical_and, [pl.program_id(k) == grid[k] - 1 for k in range(len(grid))])

            @pl.when(first)
            def _():
                for e in range(len(exchanges)):
                    for cp in built(e)[0]:
                        cp.start()

        body(*ins, *outs, *scr)

        if exchanges:
            @pl.when(last)
            def _():
                for e in range(len(exchanges)):
                    for w in built(e)[1]:
                        w()

    all_in, all_out = list(in_specs) + [ANY] * nx_in, out_specs + tuple([ANY] * nx_out)
    all_scr = list(scratch_shapes) + x_scr
    cparams = _cparams(*(sem if not exchanges else ("arbitrary",) * len(grid)))
    if prefetch is None:
        res = pl.pallas_call(wrapped, out_shape=out_shape + tuple(x_outs), grid=grid, in_specs=all_in, out_specs=all_out,
                             scratch_shapes=all_scr, input_output_aliases=aliases, compiler_params=cparams,
                             name=name)(*args, *x_args)
    else:
        gs = pltpu.PrefetchScalarGridSpec(num_scalar_prefetch=1, grid=grid, in_specs=all_in, out_specs=all_out,
                                          scratch_shapes=all_scr)
        res = pl.pallas_call(wrapped, out_shape=out_shape + tuple(x_outs), grid_spec=gs, input_output_aliases=aliases,
                             compiler_params=cparams, name=name)(prefetch, *args, *x_args)
    xres = [tuple(res[n_out + o0:n_out + o0 + no]) for (_, _, o0, no) in spans]
    return tuple(res[:n_out]), xres


def _tables(S):
    f32 = np.float32
    pos = np.arange(S, dtype=f32)
    lane = np.arange(128)
    inv = (f32(ROPE_THETA) ** (-np.arange(0, 64, 2, dtype=f32) / f32(64))).astype(f32)
    ang = (pos[:, None] * inv[None, :]).astype(np.float64)
    idx = (lane % 64) % 32
    c, s = np.cos(ang)[:, idx], np.sin(ang)[:, idx]
    first = ((lane % 64) < 32)[None, :]
    rope = np.stack([c, np.where(first, 0.0, s), np.where(first, -s, 0.0)])
    base = (f32(1.0) / (f32(ROPE_THETA) ** np.linspace(0.0, 1.0, 64, dtype=f32))).astype(f32)
    ang2 = (pos[:, None] * base[None, :]).astype(np.float64)
    c2, s2 = np.cos(ang2)[:, lane // 2], np.sin(ang2)[:, lane // 2]
    even = (lane % 2 == 0)[None, :]
    th = np.stack([c2, np.where(even, 0.0, s2), np.where(even, -s2, 0.0)])
    return np.stack([rope, th, th * (128 ** -0.5)]).astype(f32)


def _rot(a, c, sa, sb, shift):
    return a * c + pltpu.roll(a, shift, 1) * sa + pltpu.roll(a, 128 - shift, 1) * sb


def _unrot(g, c, sa, sb, shift):
    return g * c + pltpu.roll(g * sa, 128 - shift, 1) + pltpu.roll(g * sb, shift, 1)


def _ret_consts():
    h = np.arange(RET_HEADS, dtype=np.float64)
    log_g = np.log1p(-(2.0 ** (-5.0 - h)))
    idx = np.arange(BLK, dtype=np.float64)
    diff = idx[:, None] - idx[None, :]
    dmask = np.where(diff[None] >= 0, np.exp(np.maximum(diff, 0.0)[None] * log_g[:, None, None]), 0.0)
    zeta = np.exp((BLK - 1 - idx)[None, :] * log_g[:, None])
    xi = np.exp((idx + 1.0)[None, :] * log_g[:, None])
    dec = np.exp(BLK * log_g)
    rep = lambda v: np.broadcast_to(v[:, :, None], (RET_HEADS, BLK, 128))
    return (jnp.asarray(dmask, F32), jnp.asarray(rep(zeta), F32), jnp.asarray(rep(xi), F32),
            jnp.asarray(np.broadcast_to(dec[:, None, None], (RET_HEADS, 8, 256)), F32))


def _rms_fwd(x, g, to_cast=(), exchanges=()):
    S = x.shape[0]
    steps = 4
    tm = S // steps
    n_c = len(to_cast)

    def body(x_ref, g_ref, *refs):
        c_in, (h_ref, ht_ref), c_out = refs[:n_c], refs[n_c:n_c + 2], refs[n_c + 2:]
        for rows in _row_pieces(tm, 512):
            xv = x_ref[rows, :]
            r = lax.rsqrt(jnp.mean(xv * xv, axis=-1, keepdims=True) + NORM_EPS)
            h = xv * r * g_ref[...]
            h_ref[rows, :] = h.astype(BF16)
            ht_ref[:, rows] = h.T.astype(BF16)
        for a_ref, o_ref in zip(c_in, c_out):
            o_ref[...] = a_ref[...].astype(BF16)

    slab = lambda a: pl.BlockSpec((a.shape[0] // steps, a.shape[1]), lambda i: (i, 0))
    return _carrier_call(
        body, (x, g, *to_cast),
        out_shape=(SDS((S, D_MODEL), BF16), SDS((D_MODEL, S), BF16), *[SDS(a.shape, BF16) for a in to_cast]),
        grid=(steps,),
        in_specs=[pl.BlockSpec((tm, D_MODEL), lambda i: (i, 0)), pl.BlockSpec((1, D_MODEL), lambda i: (0, 0))]
        + [slab(a) for a in to_cast],
        out_specs=(pl.BlockSpec((tm, D_MODEL), lambda i: (i, 0)), pl.BlockSpec((D_MODEL, tm), lambda i: (0, i)),
                   *[slab(a) for a in to_cast]),
        sem=("parallel",), name="rms_fwd", exchanges=exchanges)


def _in_proj(h, w_in, tab, exchanges=()):
    S = h.shape[0]
    tm = min(S, 4096)

    def body(h_ref, w_ref, t_ref, o_ref):
        j = pl.program_id(1)
        is_rope = j < 6
        is_theta = (j == QR_B) | (j == KR_B)
        sub = 512

        def rotated(shift):
            for i in range(tm // sub):
                rows = slice(i * sub, (i + 1) * sub)
                acc = _dot(h_ref[rows, :], w_ref[...])
                c, sa, sb = t_ref[0, 0, rows, :], t_ref[0, 1, rows, :], t_ref[0, 2, rows, :]
                for k in range(COLB // 128):
                    sl = slice(k * 128, (k + 1) * 128)
                    o_ref[rows, sl] = _rot(acc[:, sl], c, sa, sb, shift).astype(BF16)

        @pl.when(is_rope)
        def _():
            rotated(32)

        @pl.when(is_theta)
        def _():
            rotated(1)

        @pl.when(jnp.logical_not(is_rope | is_theta))
        def _():
            o_ref[...] = _dot(h_ref[...], w_ref[...]).astype(BF16)

    def tab_map(i, j):
        return (jnp.where(j == QR_B, 1, jnp.where(j == KR_B, 2, 0)), 0, i, 0)

    (proj,), xres = _carrier_call(
        body, (h, w_in, tab), out_shape=(SDS((S, PROJ_W), BF16),), grid=(S // tm, N_COLB),
        in_specs=[pl.BlockSpec((tm, D_MODEL), lambda i, j: (i, 0)),
                  pl.BlockSpec((D_MODEL, COLB), lambda i, j: (0, j)),
                  pl.BlockSpec((1, 3, tm, 128), tab_map)],
        out_specs=(pl.BlockSpec((tm, COLB), lambda i, j: (i, j)),),
        sem=("parallel", "arbitrary"), name="in_proj", exchanges=exchanges)
    return proj, xres


def _band_mask(n):
    qi = lax.broadcasted_iota(jnp.int32, (BLK, 2 * BLK), 0)
    kj = lax.broadcasted_iota(jnp.int32, (BLK, 2 * BLK), 1)
    dist = BLK + qi - kj
    return (dist >= 0) & (dist <= BLK) & ((kj >= BLK) | (n > 0))


def _qkv_col(d, gi):
    if d == 1:
        return lambda t, r: 3 * t + gi
    return lambda t, r: 3 * r + t


def _attn_fwd(qkv, d, gi, exchanges=()):
    L = qkv.shape[0]
    nb = L // BLK

    def body(q_ref, kc_ref, kp_ref, vc_ref, vp_ref, o_ref, lse_ref):
        n = pl.program_id(1)
        mask = _band_mask(n)
        mask2 = jnp.concatenate([mask, mask], axis=0)
        lane = lax.broadcasted_iota(jnp.int32, (BLK, 128), 1)
        lo = lane < 64
        lse_all = jnp.zeros((BLK, 128), F32)
        chunks = [slice(c * 128, (c + 1) * 128) for c in range(4)]
        scores, vals = [], []
        for sl in chunks:
            q = q_ref[:, sl]
            k = jnp.concatenate([kp_ref[:, sl], kc_ref[:, sl]], axis=0)
            vals.append(jnp.concatenate([vp_ref[:, sl], vc_ref[:, sl]], axis=0))
            q2 = jnp.concatenate([jnp.where(lo, q, jnp.zeros_like(q)), jnp.where(lo, jnp.zeros_like(q), q)], axis=0)
            scores.append(_dot_nt(q2, k))
        probs = []
        for c, s in enumerate(scores):
            s = jnp.where(mask2, s * 0.125, jnp.float32(-1e30))
            m = jnp.max(s, axis=-1, keepdims=True)
            p = jnp.exp(s - m)
            l = jnp.sum(p, axis=-1, keepdims=True)
            probs.append((p / l).astype(BF16))
            lse = m + jnp.log(l)
            lse_all = jnp.where(lane // 16 == 2 * c, lse[:BLK], jnp.where(lane // 16 == 2 * c + 1, lse[BLK:], lse_all))
        for sl, p, v in zip(chunks, probs, vals):
            o2 = _dot(p, v)
            o_ref[:, sl] = jnp.where(lo, o2[:BLK], o2[BLK:])
        lse_ref[...] = lse_all

    prev = lambda n: jnp.maximum(n - 1, 0)
    col = _qkv_col(d, gi)
    return _carrier_call(
        body, (qkv,) * 5, out_shape=(SDS((L, d * 512), F32), SDS((L, d * 128), F32)), grid=(d, nb),
        in_specs=[pl.BlockSpec((BLK, 512), lambda r, n: (n, col(0, r))),
                  pl.BlockSpec((BLK, 512), lambda r, n: (n, col(1, r))),
                  pl.BlockSpec((BLK, 512), lambda r, n: (prev(n), col(1, r))),
                  pl.BlockSpec((BLK, 512), lambda r, n: (n, col(2, r))),
                  pl.BlockSpec((BLK, 512), lambda r, n: (prev(n), col(2, r)))],
        out_specs=(pl.BlockSpec((BLK, 512), lambda r, n: (n, r)),
                   pl.BlockSpec((BLK, 128), lambda r, n: (n, r))),
        sem=("parallel", "arbitrary"), name=f"attn_fwd_g{gi}", exchanges=exchanges)


def _qkv_to_sub(proj, d, gi):
    S = proj.shape[0]
    tm = 512
    n = tm // d

    def body(q_ref, k_ref, v_ref, o_ref, scr):
        for t, ref in enumerate((q_ref, k_ref, v_ref)):
            for c in range(4):
                scr[c] = ref[:, c * 128:(c + 1) * 128].astype(F32)
            for r in range(d):
                for c in range(4):
                    col = (3 * r + t) * 512 + c * 128
                    o_ref[:, col:col + 128] = scr[c, pl.ds(r, n, stride=d), :].astype(BF16)

    return pl.pallas_call(
        body, out_shape=SDS((S // d, d * 1536), BF16), grid=(S // tm,),
        in_specs=[pl.BlockSpec((tm, 512), lambda i, b=b: (i, b + gi)) for b in (QA_B, KA_B, VA_B)],
        out_specs=pl.BlockSpec((n, d * 1536), lambda i: (i, 0)),
        scratch_shapes=[pltpu.VMEM((4, tm, 128), F32)],
        compiler_params=_cparams("parallel"), name=f"qkv_to_sub_g{gi}")(proj, proj, proj)


def _attn_merge(os_, lses):
    S = os_[0].shape[0]
    tm = 512

    def body(o0, o1, o2, l0, l1, l2, att_ref, lt_ref, so1, so2, sl1, sl2):
        lo = lax.broadcasted_iota(jnp.int32, (tm, 128), 1) < 64

        def natural(ref, d, scr, width):
            nch = width // 128
            if d == 1:
                return [ref[:, c * 128:(c + 1) * 128] for c in range(nch)]
            for r in range(d):
                for c in range(nch):
                    scr[c, pl.ds(r, tm // d, stride=d), :] = ref[:, r * width + c * 128:r * width + (c + 1) * 128]
            return [scr[c] for c in range(nch)]

        ls = [natural(l, d, s, 128)[0] for l, d, s in zip((l0, l1, l2), DILATIONS, (None, sl1, sl2))]
        m = jnp.maximum(jnp.maximum(ls[0], ls[1]), ls[2])
        es = [jnp.exp(v - m) for v in ls]
        z = es[0] + es[1] + es[2]
        lt_ref[...] = m + jnp.log(z)
        ws = [e / z for e in es]
        o_nat = [natural(o, d, s, 512) for o, d, s in zip((o0, o1, o2), DILATIONS, (None, so1, so2))]
        for c in range(4):
            acc = jnp.zeros((tm, 128), F32)
            for g in range(3):
                w_lo = jnp.broadcast_to(ws[g][:, 32 * c:32 * c + 1], (tm, 128))
                w_hi = jnp.broadcast_to(ws[g][:, 32 * c + 16:32 * c + 17], (tm, 128))
                acc = acc + jnp.where(lo, w_lo, w_hi) * o_nat[g][c]
            att_ref[:, c * 128:(c + 1) * 128] = acc.astype(BF16)

    sub = lambda w: [pl.BlockSpec((tm // d, d * w), lambda i: (i, 0)) for d in DILATIONS]
    return pl.pallas_call(
        body, out_shape=(SDS((S, 512), BF16), SDS((S, 128), F32)), grid=(S // tm,),
        in_specs=sub(512) + sub(128),
        out_specs=(pl.BlockSpec((tm, 512), lambda i: (i, 0)), pl.BlockSpec((tm, 128), lambda i: (i, 0))),
        scratch_shapes=[pltpu.VMEM((4, tm, 128), F32), pltpu.VMEM((4, tm, 128), F32),
                        pltpu.VMEM((1, tm, 128), F32), pltpu.VMEM((1, tm, 128), F32)],
        compiler_params=_cparams("parallel"), name="attn_merge")(*os_, *lses)


def _assemble_dproj(att_grads, dproj):
    S = dproj.shape[0]
    tm = 256

    def body(*refs):
        a = [refs[3 * t:3 * t + 3] for t in range(3)]
        dp_prev, o_ref, scr = refs[9:]
        for t in range(3):
            for g, d in enumerate(DILATIONS):
                base = (3 * t + g) * COLB
                if d == 1:
                    o_ref[:, base:base + COLB] = a[t][g][...]
                    continue
                for c in range(4):
                    for r in range(d):
                        scr[c, pl.ds(r, tm // d, stride=d), :] = a[t][g][:, r * 512 + c * 128:r * 512 + (c + 1) * 128].astype(F32)
                    o_ref[:, base + c * 128:base + (c + 1) * 128] = scr[c].astype(BF16)

    sub = [pl.BlockSpec((tm // d, d * 512), lambda i: (i, 0)) for d in DILATIONS]
    flat = [att_grads[t][g] for t in range(3) for g in range(3)]
    return pl.pallas_call(
        body, out_shape=SDS((S, PROJ_W), BF16), grid=(S // tm,),
        in_specs=sub * 3 + [ANY], out_specs=pl.BlockSpec((tm, 9 * COLB), lambda i: (i, 0)),
        scratch_shapes=[pltpu.VMEM((4, tm, 128), F32)], input_output_aliases={9: 0},
        compiler_params=_cparams("parallel"), name="assemble_dproj")(*flat, dproj)


def _ret_fwd(proj, consts, exchanges=()):
    S = proj.shape[0]
    nc = S // BLK
    dmask, zeta, xi, dec = consts

    def body(q_ref, k_ref, v0_ref, v1_ref, g0_ref, g1_ref, dm_ref, z_ref, x_ref, dec_ref,
             y_ref, rn_ref, rs_ref, st_ref, R):
        @pl.when(pl.program_id(0) == 0)
        def _():
            R[...] = jnp.zeros_like(R)

        lane16 = lax.broadcasted_iota(jnp.int32, (BLK, 128), 1) // 16
        rs_all = jnp.zeros((BLK, 128), F32)
        first = []
        for h in range(RET_HEADS):
            hs = slice(h * 128, (h + 1) * 128)
            q, k = q_ref[:, hs], k_ref[:, hs]
            v = (v0_ref if h < 2 else v1_ref)[:, (h % 2) * 256:(h % 2 + 1) * 256]
            Rb = R[h].astype(BF16)
            st_ref[h] = Rb
            kz = (k.astype(F32) * z_ref[h]).astype(BF16)
            first.append((v, _dot_nt(q, k), _dot((q.astype(F32) * x_ref[h]).astype(BF16), Rb), _dot_tn(kz, v)))
        masked = [(s * dm_ref[h]).astype(BF16) for h, (_, s, _, _) in enumerate(first)]
        for h in range(RET_HEADS):
            vs = slice((h % 2) * 256, (h % 2 + 1) * 256)
            os_ = slice(h * 256, (h + 1) * 256)
            v, _, cross, kv = first[h]
            o = _dot(masked[h], v) + cross
            R[h] = R[h] * dec_ref[h, 0:1, :] + kv
            mu = jnp.mean(o, axis=-1, keepdims=True)
            oc = o - mu
            rstd = lax.rsqrt(jnp.mean(oc * oc, axis=-1, keepdims=True) + NORM_EPS)
            rn = oc * rstd
            gr = (g0_ref if h < 2 else g1_ref)[:, vs].astype(F32)
            y_ref[:, os_] = (rn * gr * _sigmoid(gr)).astype(BF16)
            rn_ref[:, os_] = rn.astype(BF16)
            rs_all = jnp.where(lane16 == h, rstd, rs_all)
        rs_ref[...] = rs_all

    cst = lambda shape: pl.BlockSpec(shape, lambda c: (0, 0, 0))
    blk = lambda j: pl.BlockSpec((BLK, 512), lambda c: (c, j))
    return _carrier_call(
        body, (proj, proj, proj, proj, proj, proj, dmask, zeta, xi, dec),
        out_shape=(SDS((S, 1024), BF16), SDS((S, 1024), BF16), SDS((S, 128), F32), SDS((RET_HEADS, nc, BLK, 256), BF16)),
        grid=(nc,),
        in_specs=[blk(QR_B), blk(KR_B), blk(11), blk(12), blk(13), blk(14),
                  cst((RET_HEADS, BLK, BLK)), cst((RET_HEADS, BLK, 128)), cst((RET_HEADS, BLK, 128)), cst((RET_HEADS, 8, 256))],
        out_specs=(pl.BlockSpec((BLK, 1024), lambda c: (c, 0)), pl.BlockSpec((BLK, 1024), lambda c: (c, 0)),
                   pl.BlockSpec((BLK, 128), lambda c: (c, 0)),
                   pl.BlockSpec((RET_HEADS, None, BLK, 256), lambda c: (0, c, 0, 0))),
        scratch_shapes=[pltpu.VMEM((RET_HEADS, BLK, 256), F32)],
        sem=("arbitrary",), name="ret_fwd", exchanges=exchanges)


def _mix_out(att, yrin, proj, wa, wr, wo, x, g2, exchanges=()):
    S = x.shape[0]
    tm = 512
    gate0 = 15 * COLB

    def body(a_ref, y_ref, ga_ref, gr_ref, wa_ref, wr_ref, wo_ref, x_ref, g_ref, m_ref, ya_ref, yr_ref, x1_ref, h2_ref):
        pieces = _row_pieces(tm, 256)
        branches = [(_dot(a_ref[rows, :], wa_ref[...]), _dot(y_ref[rows, :], wr_ref[...])) for rows in pieces]
        merged = []
        for rows, (ya, yr) in zip(pieces, branches):
            m = (_sigmoid(ga_ref[rows, :].astype(F32)) * ya + _sigmoid(gr_ref[rows, :].astype(F32)) * yr).astype(BF16)
            m_ref[rows, :] = m
            ya_ref[rows, :] = ya.astype(BF16)
            yr_ref[rows, :] = yr.astype(BF16)
            merged.append(m)
        for rows, m in zip(pieces, merged):
            x1 = x_ref[rows, :] + _dot(m, wo_ref[...])
            x1_ref[rows, :] = x1
            r = lax.rsqrt(jnp.mean(x1 * x1, axis=-1, keepdims=True) + NORM_EPS)
            h2_ref[rows, :] = (x1 * r * g_ref[...]).astype(BF16)

    row = lambda w: pl.BlockSpec((tm, w), lambda i: (i, 0))
    cols = lambda c0: pl.BlockSpec((pl.Element(tm), pl.Element(D_MODEL)), lambda i: (i * tm, c0))
    resident = lambda r, c: pl.BlockSpec((r, c), lambda i: (0, 0), pipeline_mode=pl.Buffered(1))
    return _carrier_call(
        body, (att, yrin, proj, proj, wa, wr, wo, x, g2),
        out_shape=(SDS((S, D_MODEL), BF16),) * 3 + (SDS((S, D_MODEL), F32), SDS((S, D_MODEL), BF16)), grid=(S // tm,),
        in_specs=[row(512), row(D_MODEL), cols(gate0), cols(gate0 + D_MODEL), resident(512, D_MODEL),
                  resident(D_MODEL, D_MODEL), resident(D_MODEL, D_MODEL), row(D_MODEL),
                  pl.BlockSpec((1, D_MODEL), lambda i: (0, 0))],
        out_specs=(row(D_MODEL),) * 5, sem=("parallel",), name="mix_out", exchanges=exchanges)


def _ffn_up(h2, wg, wu, exchanges=()):
    S = h2.shape[0]
    tm = min(S, 2048)

    def body(h_ref, wg_ref, wu_ref, g_ref, u_ref, a_ref):
        for rows in _row_pieces(tm):
            hv = h_ref[rows, :]
            g = _dot_nt(hv, wg_ref[...])
            u = _dot_nt(hv, wu_ref[...])
            g_ref[rows, :] = g.astype(BF16)
            u_ref[rows, :] = u.astype(BF16)
            a_ref[rows, :] = (g * _sigmoid(g) * u).astype(BF16)

    wspec = pl.BlockSpec((None, HID_S, D_MODEL), lambda i, s: (s, 0, 0))
    ospec = pl.BlockSpec((None, tm, HID_S), lambda i, s: (s, i, 0))
    return _carrier_call(
        body, (h2, wg, wu), out_shape=(SDS((N_SHARD, S, HID_S), BF16),) * 3, grid=(S // tm, N_SHARD),
        in_specs=[pl.BlockSpec((tm, D_MODEL), lambda i, s: (i, 0)), wspec, wspec],
        out_specs=(ospec, ospec, ospec),
        sem=("parallel", "arbitrary"), name="ffn_up", exchanges=exchanges)


def _ffn_down_loss(act, wd, x1, g3, tgt):
    S = x1.shape[0]
    tm = 512

    def body(a_ref, w_ref, x_ref, g_ref, t_ref, dx_ref, dxb_ref, dg_ref, ls_ref):
        @pl.when(pl.program_id(0) == 0)
        def _():
            dg_ref[...] = jnp.zeros_like(dg_ref)
            ls_ref[...] = jnp.zeros_like(ls_ref)

        g = g_ref[...]
        for rows in _row_pieces(tm, 256):
            y = _dot(a_ref[0, rows, :], w_ref[0])
            for s in range(1, N_SHARD):
                y = y + _dot(a_ref[s, rows, :], w_ref[s])
            x2 = x_ref[rows, :] + y
            r = lax.rsqrt(jnp.mean(x2 * x2, axis=-1, keepdims=True) + NORM_EPS)
            xh = x2 * r
            err = xh * g - t_ref[rows, :]
            ls_ref[...] += jnp.sum(jnp.sum(err * err, axis=-1, keepdims=True), axis=0, keepdims=True) * (0.5 / D_MODEL)
            dy = err * (1.0 / D_MODEL)
            dg_ref[...] += jnp.sum(dy * xh, axis=0, keepdims=True)
            dxh = dy * g
            dx = r * (dxh - xh * jnp.mean(dxh * xh, axis=-1, keepdims=True))
            dx_ref[rows, :] = dx
            dxb_ref[rows, :] = dx.astype(BF16)

    row = pl.BlockSpec((tm, D_MODEL), lambda i: (i, 0))
    vec = pl.BlockSpec((1, D_MODEL), lambda i: (0, 0))
    return pl.pallas_call(
        body, out_shape=(SDS((S, D_MODEL), F32), SDS((S, D_MODEL), BF16), SDS((1, D_MODEL), F32), SDS((8, 128), F32)),
        grid=(S // tm,),
        in_specs=[pl.BlockSpec((N_SHARD, tm, HID_S), lambda i: (0, i, 0)),
                  pl.BlockSpec((N_SHARD, HID_S, D_MODEL), lambda i: (0, 0, 0), pipeline_mode=pl.Buffered(1)),
                  row, vec, row],
        out_specs=(row, row, vec, pl.BlockSpec((8, 128), lambda i: (0, 0))),
        compiler_params=_cparams("arbitrary"), name="ffn_down_loss")(act, wd, x1, g3, tgt)


def _ffn_bwd(dx2b, dx2, wd, wg, wu, gte, up, x1, g2):
    S = x1.shape[0]
    tm = 256

    def body(d_ref, dx2_ref, wd_ref, wg_ref, wu_ref, g_ref, u_ref, x_ref, gn_ref,
             dg_ref, du_ref, dx_ref, dxb_ref, dgn_ref):
        @pl.when(pl.program_id(0) == 0)
        def _():
            dgn_ref[...] = jnp.zeros_like(dgn_ref)

        d = d_ref[...]
        dacts = [_dot_nt(d, wd_ref[s]) for s in range(N_SHARD)]
        dgs, dus = [], []
        for s, da in enumerate(dacts):
            g = g_ref[s].astype(F32)
            sg = _sigmoid(g)
            dgs.append((da * u_ref[s].astype(F32) * sg * (1.0 + g * (1.0 - sg))).astype(BF16))
            dus.append((da * g * sg).astype(BF16))
            dg_ref[s] = dgs[s]
            du_ref[s] = dus[s]
        dh = _dot(dgs[0], wg_ref[0]) + _dot(dus[0], wu_ref[0])
        for s in range(1, N_SHARD):
            dh = dh + _dot(dgs[s], wg_ref[s]) + _dot(dus[s], wu_ref[s])
        xv = x_ref[...]
        r = lax.rsqrt(jnp.mean(xv * xv, axis=-1, keepdims=True) + NORM_EPS)
        xh = xv * r
        dgn_ref[...] += jnp.sum(dh * xh, axis=0, keepdims=True)
        dxh = dh * gn_ref[...]
        dx = dx2_ref[...] + r * (dxh - xh * jnp.mean(dxh * xh, axis=-1, keepdims=True))
        dx_ref[...] = dx
        dxb_ref[...] = dx.astype(BF16)

    row = pl.BlockSpec((tm, D_MODEL), lambda i: (i, 0))
    vec = pl.BlockSpec((1, D_MODEL), lambda i: (0, 0))
    aspec = pl.BlockSpec((N_SHARD, tm, HID_S), lambda i: (0, i, 0))
    resident = lambda shape: pl.BlockSpec(shape, lambda i: (0, 0, 0), pipeline_mode=pl.Buffered(1))
    return pl.pallas_call(
        body,
        out_shape=(SDS((N_SHARD, S, HID_S), BF16), SDS((N_SHARD, S, HID_S), BF16),
                   SDS((S, D_MODEL), F32), SDS((S, D_MODEL), BF16), SDS((1, D_MODEL), F32)),
        grid=(S // tm,),
        in_specs=[row, row, resident((N_SHARD, HID_S, D_MODEL)), resident((N_SHARD, HID_S, D_MODEL)),
                  resident((N_SHARD, HID_S, D_MODEL)), aspec, aspec, row, vec],
        out_specs=(aspec, aspec, row, row, vec),
        compiler_params=_cparams("arbitrary"), name="ffn_bwd")(dx2b, dx2, wd, wg, wu, gte, up, x1, g2)


def _wgrad(name, a, b, a_spec, b_spec, out_shape, out_spec, n_par, S):
    tk = min(S, 4096)

    def body(a_ref, b_ref, o_ref):
        @pl.when(pl.program_id(1) == 0)
        def _():
            o_ref[...] = jnp.zeros_like(o_ref)

        o_ref[...] += _dot_tn(a_ref[...], b_ref[...])

    return pl.pallas_call(
        body, out_shape=SDS(out_shape, F32), grid=(n_par, S // tk),
        in_specs=[a_spec(tk), b_spec(tk)], out_specs=out_spec,
        compiler_params=_cparams("parallel", "arbitrary"), name=name)(a, b)


def _mix_bwd(dx1b, wo, proj, ya, yr, wa, wr, att, exchanges=()):
    S = dx1b.shape[0]
    tm = 512
    gate0 = 15 * COLB

    def body(d_ref, wo_ref, ga_ref, gr_ref, ya_ref, yr_ref, wa_ref, wr_ref, att_ref,
             dya_ref, dyr_ref, dp_ref, datt_ref, rho_ref, dyi_ref):
        pieces = _row_pieces(tm, 256)
        dms = [_dot_nt(d_ref[rows, :], wo_ref[...]) for rows in pieces]
        branch = []
        for rows, dm in zip(pieces, dms):
            sa = _sigmoid(ga_ref[rows, :].astype(F32))
            sr = _sigmoid(gr_ref[rows, :].astype(F32))
            dya, dyr = (dm * sa).astype(BF16), (dm * sr).astype(BF16)
            dya_ref[rows, :] = dya
            dyr_ref[rows, :] = dyr
            dp_ref[rows, 0:D_MODEL] = (dm * ya_ref[rows, :].astype(F32) * sa * (1.0 - sa)).astype(BF16)
            dp_ref[rows, D_MODEL:2 * D_MODEL] = (dm * yr_ref[rows, :].astype(F32) * sr * (1.0 - sr)).astype(BF16)
            branch.append((dya, dyr))
        lane = lax.broadcasted_iota(jnp.int32, (256, 128), 1)
        lo = lane < 64
        for rows, (dya, dyr) in zip(pieces, branch):
            datt = _dot_nt(dya, wa_ref[...])
            datt_ref[rows, :] = datt.astype(BF16)
            dyi_ref[rows, :] = _dot_nt(dyr, wr_ref[...]).astype(BF16)
            prod = datt * att_ref[rows, :].astype(F32)
            rho = jnp.zeros((256, 128), F32)
            for c in range(4):
                pc = prod[:, c * 128:(c + 1) * 128]
                tot = jnp.sum(pc, axis=-1, keepdims=True)
                low = jnp.sum(jnp.where(lo, pc, 0.0), axis=-1, keepdims=True)
                rho = jnp.where(lane // 16 == 2 * c, low, jnp.where(lane // 16 == 2 * c + 1, tot - low, rho))
            rho_ref[rows, :] = rho

    row = lambda w: pl.BlockSpec((tm, w), lambda i: (i, 0))
    cols = lambda c0, w: pl.BlockSpec((pl.Element(tm), pl.Element(w)), lambda i: (i * tm, c0))
    resident = lambda r, c: pl.BlockSpec((r, c), lambda i: (0, 0), pipeline_mode=pl.Buffered(1))
    return _carrier_call(
        body, (dx1b, wo, proj, proj, ya, yr, wa, wr, att),
        out_shape=(SDS((S, D_MODEL), BF16), SDS((S, D_MODEL), BF16), SDS((S, PROJ_W), BF16),
                   SDS((S, 512), BF16), SDS((S, 128), F32), SDS((S, D_MODEL), BF16)),
        grid=(S // tm,),
        in_specs=[row(D_MODEL), resident(D_MODEL, D_MODEL), cols(gate0, D_MODEL), cols(gate0 + D_MODEL, D_MODEL),
                  row(D_MODEL), row(D_MODEL), resident(512, D_MODEL), resident(D_MODEL, D_MODEL), row(512)],
        out_specs=(row(D_MODEL), row(D_MODEL), cols(gate0, 2 * D_MODEL), row(512), row(128), row(D_MODEL)),
        sem=("parallel",), name="mix_bwd", exchanges=exchanges)


def _attn_bwd(qkv, datt, lse, rho, rtab, d, gi, exchanges=()):
    L = qkv.shape[0]
    nb = L // BLK
    T = d * nb

    def body(q_ref, kc_ref, kp_ref, vc_ref, vp_ref, do_ref, lse_ref, rho_ref, tq_ref, tk_ref,
             dq_ref, dk_ref, dv_ref, ck, cv):
        t = pl.program_id(0)
        n = jnp.minimum(t, T - 1) % nb

        @pl.when(t == 0)
        def _():
            ck[...] = jnp.zeros_like(ck)
            cv[...] = jnp.zeros_like(cv)

        def store_rot(ref, val, t_ref, c):
            sl = slice(c * 128, (c + 1) * 128)
            ref[:, sl] = _unrot(val, t_ref[0], t_ref[1], t_ref[2], 32).astype(BF16)

        @pl.when(t < T)
        def _():
            mask = _band_mask(n)
            mask2 = jnp.concatenate([mask, mask], axis=0)
            lo = lax.broadcasted_iota(jnp.int32, (BLK, 128), 1) < 64

            def stacked(a):
                return jnp.concatenate([jnp.where(lo, a, jnp.zeros_like(a)), jnp.where(lo, jnp.zeros_like(a), a)], axis=0)

            def head_cols(ref, c):
                return jnp.concatenate([jnp.broadcast_to(ref[:, 32 * c:32 * c + 1], (BLK, 2 * BLK)),
                                        jnp.broadcast_to(ref[:, 32 * c + 16:32 * c + 17], (BLK, 2 * BLK))], axis=0)

            ops, raw = [], []
            for c in range(4):
                sl = slice(c * 128, (c + 1) * 128)
                q2, do2 = stacked(q_ref[:, sl]), stacked(do_ref[:, sl])
                k = jnp.concatenate([kp_ref[:, sl], kc_ref[:, sl]], axis=0)
                v = jnp.concatenate([vp_ref[:, sl], vc_ref[:, sl]], axis=0)
                ops.append((q2, do2, k))
                raw.append((_dot_nt(q2, k), _dot_nt(do2, v)))
            grads = []
            for c, (s, dp) in enumerate(raw):
                p = jnp.where(mask2, jnp.exp(s * 0.125 - head_cols(lse_ref, c)), 0.0)
                grads.append(((p * (dp - head_cols(rho_ref, c)) * 0.125).astype(BF16), p.astype(BF16)))
            for c, ((q2, do2, k), (ds, pb)) in enumerate(zip(ops, grads)):
                sl = slice(c * 128, (c + 1) * 128)
                dq2 = _dot(ds, k)
                dq_c = jnp.where(lo, dq2[:BLK], dq2[BLK:])
                dk_c = _dot_tn(ds, q2)
                dv_c = _dot_tn(pb, do2)
                store_rot(dq_ref, dq_c, tq_ref, c)
                store_rot(dk_ref, ck[:, sl] + dk_c[:BLK], tk_ref, c)
                dv_ref[:, sl] = (cv[:, sl] + dv_c[:BLK]).astype(BF16)
                ck[:, sl] = dk_c[BLK:]
                cv[:, sl] = dv_c[BLK:]

        @pl.when(t == T)
        def _():
            for c in range(4):
                sl = slice(c * 128, (c + 1) * 128)
                store_rot(dk_ref, ck[:, sl], tk_ref, c)
            dv_ref[...] = cv[...].astype(BF16)

    blk_of = lambda t: (jnp.minimum(t, T - 1) % nb, jnp.minimum(t, T - 1) // nb)
    cur = lambda t: blk_of(t)
    prev = lambda t: (jnp.maximum(blk_of(t)[0] - 1, 0), blk_of(t)[1])
    fin = lambda t: blk_of(jnp.maximum(t - 1, 0))
    col = _qkv_col(d, gi)
    qkv_spec = lambda kind, which: pl.BlockSpec((BLK, 512), lambda t: (which(t)[0], col(kind, which(t)[1])))
    row_spec = lambda w, which: pl.BlockSpec((BLK, w), lambda t: which(t))
    tab_spec = lambda which: pl.BlockSpec((3, BLK, 128), lambda t: (0, *which(t)))
    return _carrier_call(
        body, (qkv, qkv, qkv, qkv, qkv, datt, lse, rho, rtab, rtab),
        out_shape=(SDS((L, d * 512), BF16),) * 3, grid=(T + 1,),
        in_specs=[qkv_spec(0, cur), qkv_spec(1, cur), qkv_spec(1, prev), qkv_spec(2, cur), qkv_spec(2, prev),
                  row_spec(512, cur), row_spec(128, cur), row_spec(128, cur), tab_spec(cur), tab_spec(fin)],
        out_specs=(row_spec(512, cur), row_spec(512, fin), row_spec(512, fin)),
        scratch_shapes=[pltpu.VMEM((BLK, 512), F32), pltpu.VMEM((BLK, 512), F32)],
        sem=("arbitrary",), name=f"attn_bwd_g{gi}", exchanges=exchanges)


def _ret_bwd(proj, rn, rstd, dyrin, states, tab, consts, dproj, exchanges=()):
    S = proj.shape[0]
    nc = S // BLK
    dmask, zeta, xi, dec = consts

    def body(q_ref, k_ref, v0_ref, v1_ref, g0_ref, g1_ref, rn_ref, rs_ref, dy_ref, st_ref, tq_ref, tk_ref,
             dm_ref, z_ref, x_ref, dec_ref, dp_prev, dp_ref, dR):
        dq_ref, dk_ref = dp_ref.at[:, 0:512], dp_ref.at[:, 512:1024]
        dv_ref, dgr_ref = dp_ref.at[:, 1024:2048], dp_ref.at[:, 2048:3072]

        @pl.when(pl.program_id(0) == 0)
        def _():
            dR[...] = jnp.zeros_like(dR)

        dobs = []
        for h in range(RET_HEADS):
            vs = slice((h % 2) * 256, (h % 2 + 1) * 256)
            os_ = slice(h * 256, (h + 1) * 256)
            gr = (g0_ref if h < 2 else g1_ref)[:, vs].astype(F32)
            sg = _sigmoid(gr)
            rn_v = rn_ref[:, os_].astype(F32)
            dyi = dy_ref[:, os_].astype(F32)
            dgr_ref[:, os_] = (dyi * rn_v * sg * (1.0 + gr * (1.0 - sg))).astype(BF16)
            drn = dyi * gr * sg
            rstd = jnp.broadcast_to(rs_ref[:, 16 * h:16 * h + 1], (BLK, 256))
            do = rstd * (drn - jnp.mean(drn, axis=-1, keepdims=True) - rn_v * jnp.mean(drn * rn_v, axis=-1, keepdims=True))
            dobs.append(do.astype(BF16))
        first = []
        for h in range(RET_HEADS):
            hs = slice(h * 128, (h + 1) * 128)
            q, k = q_ref[:, hs], k_ref[:, hs]
            v = (v0_ref if h < 2 else v1_ref)[:, (h % 2) * 256:(h % 2 + 1) * 256]
            dob, dRb = dobs[h], dR[h].astype(BF16)
            kz = (k.astype(F32) * z_ref[h]).astype(BF16)
            qx = (q.astype(F32) * x_ref[h]).astype(BF16)
            first.append((q, k, _dot_nt(q, k), _dot_nt(dob, v), _dot(kz, dRb), _dot_nt(dob, st_ref[h]),
                          _dot_nt(v, dRb), _dot_tn(qx, dob)))
        masked = [((s * dm_ref[h]).astype(BF16), (dsr * dm_ref[h]).astype(BF16))
                  for h, (_, _, s, dsr, _, _, _, _) in enumerate(first)]
        for h in range(RET_HEADS):
            hs = slice(h * 128, (h + 1) * 128)
            os_ = slice(h * 256, (h + 1) * 256)
            q, k, _, _, dv_state, dq_state, dk_state, dr_new = first[h]
            sD, dS = masked[h]
            dv_ref[:, os_] = (_dot_tn(sD, dobs[h]) + dv_state).astype(BF16)
            dq = _dot(dS, k) + dq_state * x_ref[h]
            dk = _dot_tn(dS, q) + dk_state * z_ref[h]
            dR[h] = dR[h] * dec_ref[h, 0:1, :] + dr_new
            dq_ref[:, hs] = _unrot(dq, tq_ref[0], tq_ref[1], tq_ref[2], 1).astype(BF16)
            dk_ref[:, hs] = _unrot(dk, tk_ref[0], tk_ref[1], tk_ref[2], 1).astype(BF16)

    rc = lambda c: nc - 1 - c
    cst = lambda shape: pl.BlockSpec(shape, lambda c: (0, 0, 0))
    blk = lambda j: pl.BlockSpec((BLK, 512), lambda c: (rc(c), j))
    row = lambda w: pl.BlockSpec((BLK, w), lambda c: (rc(c), 0))
    (dproj,), xres = _carrier_call(
        body, (proj, proj, proj, proj, proj, proj, rn, rstd, dyrin, states, tab, tab, dmask, zeta, xi, dec, dproj),
        out_shape=(SDS((S, PROJ_W), BF16),), grid=(nc,),
        in_specs=[blk(QR_B), blk(KR_B), blk(11), blk(12), blk(13), blk(14), row(1024), row(128), row(1024),
                  pl.BlockSpec((RET_HEADS, None, BLK, 256), lambda c: (0, rc(c), 0, 0)),
                  pl.BlockSpec((None, 3, BLK, 128), lambda c: (1, 0, rc(c), 0)),
                  pl.BlockSpec((None, 3, BLK, 128), lambda c: (2, 0, rc(c), 0)),
                  cst((RET_HEADS, BLK, BLK)), cst((RET_HEADS, BLK, 128)), cst((RET_HEADS, BLK, 128)), cst((RET_HEADS, 8, 256)),
                  ANY],
        out_specs=(pl.BlockSpec((pl.Element(BLK), pl.Element(6 * COLB)), lambda c: (rc(c) * BLK, QR_B * COLB)),),
        scratch_shapes=[pltpu.VMEM((RET_HEADS, BLK, 256), F32)],
        sem=("arbitrary",), name="ret_bwd", exchanges=exchanges, in_out_aliases={16: 0})
    return dproj, xres


def _wgrad_in_half(ht, dproj, sidx, kept, exchanges=()):
    S = dproj.shape[0]
    tk = 2048
    half = (lambda sx: sx[4]) if kept else (lambda sx: 1 - sx[4])

    def body(a_ref, b_ref, o_ref):
        @pl.when(pl.program_id(1) == 0)
        def _():
            o_ref[...] = jnp.zeros_like(o_ref)

        o_ref[...] += _dot(a_ref[...], b_ref[...])

    (g,), xres = _carrier_call(
        body, (ht, dproj), out_shape=(SDS((D_MODEL // 2, PROJ_W), F32),), grid=(N_SHARD, S // tk),
        in_specs=[pl.BlockSpec((D_MODEL // 2, tk), lambda s, k, sx: (half(sx), k)),
                  pl.BlockSpec((tk, W_IN_S), lambda s, k, sx: (k, s))],
        out_specs=(pl.BlockSpec((D_MODEL // 2, W_IN_S), lambda s, k, sx: (0, s)),),
        sem=("parallel", "arbitrary"), name="wgrad_in_kept" if kept else "wgrad_in_sent", exchanges=exchanges,
        prefetch=sidx)
    return g, xres


def _in_proj_bwd(dproj, w_in, x, g1, dx1, exchanges=()):
    S = x.shape[0]
    tm = 1024

    def body(d_ref, w_ref, x_ref, g_ref, dx1_ref, dx_ref, dgn_ref, acc):
        i, s = pl.program_id(0), pl.program_id(1)

        @pl.when(s == 0)
        def _():
            acc[...] = jnp.zeros_like(acc)

        @pl.when((i == 0) & (s == 0))
        def _():
            dgn_ref[...] = jnp.zeros_like(dgn_ref)

        acc[...] += _dot_nt(d_ref[...], w_ref[...])

        @pl.when(s == N_SHARD - 1)
        def _():
            xv = x_ref[...]
            r = lax.rsqrt(jnp.mean(xv * xv, axis=-1, keepdims=True) + NORM_EPS)
            xh = xv * r
            dh = acc[...]
            dgn_ref[...] += jnp.sum(dh * xh, axis=0, keepdims=True)
            dxh = dh * g_ref[...]
            dx_ref[...] = dx1_ref[...] + r * (dxh - xh * jnp.mean(dxh * xh, axis=-1, keepdims=True))

    row = pl.BlockSpec((tm, D_MODEL), lambda i, s: (i, 0))
    vec = pl.BlockSpec((1, D_MODEL), lambda i, s: (0, 0))
    (gx, dg), xres = _carrier_call(
        body, (dproj, w_in, x, g1, dx1),
        out_shape=(SDS((S, D_MODEL), F32), SDS((1, D_MODEL), F32)), grid=(S // tm, N_SHARD),
        in_specs=[pl.BlockSpec((tm, W_IN_S), lambda i, s: (i, s)),
                  pl.BlockSpec((D_MODEL, W_IN_S), lambda i, s: (0, s)), row, vec, row],
        out_specs=(row, vec), scratch_shapes=[pltpu.VMEM((tm, D_MODEL), F32)],
        sem=("arbitrary", "arbitrary"), name="in_proj_bwd", exchanges=exchanges)
    return gx, dg, xres


def _sub_view(a, d):
    S, W = a.shape
    return a.reshape(S // d, d * W)


def _step(x, tgt, g1, g2, g3, comm):
    S = x.shape[0]
    tab_np = _tables(S)
    tab = jnp.asarray(tab_np)
    consts = _ret_consts()

    (h, ht, *casts), xres = _rms_fwd(x, g1, comm.to_cast(), comm.carry("rms_fwd"))
    comm.cast_done(casts)
    comm.took("rms_fwd", xres)
    w_in = comm.weight(0)
    proj, xres = _in_proj(h, w_in, tab, comm.carry("in_proj"))
    comm.took("in_proj", xres)
    qkvs, o_parts, lse_parts = [], [], []
    for gi, d in enumerate(DILATIONS):
        qkv = proj if d == 1 else _qkv_to_sub(proj, d, gi)
        (o_g, lse_g), xres = _attn_fwd(qkv, d, gi, comm.carry(f"attn_fwd_g{gi}"))
        comm.took(f"attn_fwd_g{gi}", xres)
        qkvs.append(qkv)
        o_parts.append(o_g)
        lse_parts.append(lse_g)
    att, lse_tot = _attn_merge(o_parts, lse_parts)
    (yrin, rn, rstd, states), xres = _ret_fwd(proj, consts, comm.carry("ret_fwd"))
    comm.took("ret_fwd", xres)
    wa, wr, wo = comm.weight(1), comm.weight(2), comm.weight(3)
    (merged, ya, yr, x1, h2), xres = _mix_out(att, yrin, proj, wa, wr, wo, x, g2, comm.carry("mix_out"))
    comm.took("mix_out", xres)
    wg, wu = comm.weight(4), comm.weight(5)
    (gte, up, act), xres = _ffn_up(h2, wg, wu, comm.carry("ffn_up"))
    comm.took("ffn_up", xres)
    wd = comm.weight(6)
    dx2, dx2b, dg3, loss_p = _ffn_down_loss(act, wd, x1, g3, tgt)

    dgte, dup, dx1, dx1b, dg2 = _ffn_bwd(dx2b, dx2, wd, wg, wu, gte, up, x1, g2)
    tok3 = lambda w: (lambda tk: pl.BlockSpec((None, tk, w), lambda p, k: (p, k, 0)))
    tok2 = lambda w: (lambda tk: pl.BlockSpec((tk, w), lambda p, k: (k, 0)))
    g_d = _wgrad("wgrad_down", act, dx2b, tok3(HID_S), tok2(D_MODEL), (N_SHARD, HID_S, D_MODEL),
                 pl.BlockSpec((None, HID_S, D_MODEL), lambda p, k: (p, 0, 0)), N_SHARD, S)
    g_g = _wgrad("wgrad_gate", dgte, h2, tok3(HID_S), tok2(D_MODEL), (N_SHARD, HID_S, D_MODEL),
                 pl.BlockSpec((None, HID_S, D_MODEL), lambda p, k: (p, 0, 0)), N_SHARD, S)
    g_u = _wgrad("wgrad_up", dup, h2, tok3(HID_S), tok2(D_MODEL), (N_SHARD, HID_S, D_MODEL),
                 pl.BlockSpec((None, HID_S, D_MODEL), lambda p, k: (p, 0, 0)), N_SHARD, S)
    comm.grads({4: g_g, 5: g_u, 6: g_d})
    (dya, dyr, dproj, datt, rho, dyrin), xres = _mix_bwd(dx1b, wo, proj, ya, yr, wa, wr, att, comm.carry("mix_bwd"))
    comm.took("mix_bwd", xres)
    colblk = lambda w: (lambda tk: pl.BlockSpec((tk, w), lambda p, k: (k, p)))
    g_o = _wgrad("wgrad_out", merged, dx1b, colblk(256), tok2(D_MODEL), (D_MODEL, D_MODEL),
                 pl.BlockSpec((256, D_MODEL), lambda p, k: (p, 0)), 4, S)
    g_a = _wgrad("wgrad_attn", att, dya, tok2(512), colblk(512), (512, D_MODEL),
                 pl.BlockSpec((512, 512), lambda p, k: (0, p)), 2, S)
    g_r = _wgrad("wgrad_ret", yrin, dyr, colblk(256), tok2(D_MODEL), (D_MODEL, D_MODEL),
                 pl.BlockSpec((256, D_MODEL), lambda p, k: (p, 0)), 4, S)
    comm.grads({1: g_a, 2: g_r.reshape(N_SHARD, 256, D_MODEL), 3: g_o.reshape(N_SHARD, 256, D_MODEL)})
    dproj, xres = _ret_bwd(proj, rn, rstd, dyrin, states, tab, consts, dproj, comm.carry("ret_bwd"))
    comm.took("ret_bwd", xres)
    dqs, dks, dvs = [], [], []
    for gi, d in enumerate(DILATIONS):
        rtab = jnp.asarray(tab_np[0].reshape(3, S // d, d * 128))
        (dq, dk, dv), xres = _attn_bwd(qkvs[gi], _sub_view(datt, d), _sub_view(lse_tot, d), _sub_view(rho, d), rtab, d, gi,
                                       comm.carry(f"attn_bwd_g{gi}"))
        comm.took(f"attn_bwd_g{gi}", xres)
        dqs.append(dq)
        dks.append(dk)
        dvs.append(dv)
    dproj = _assemble_dproj((dqs, dks, dvs), dproj)
    g_sent, xres = _wgrad_in_half(ht, dproj, comm.sidx, False, comm.carry("wgrad_in_sent"))
    comm.took("wgrad_in_sent", xres)
    comm.grads({"in_sent": g_sent})
    g_kept, xres = _wgrad_in_half(ht, dproj, comm.sidx, True, comm.carry("wgrad_in_kept"))
    comm.grads({"in_kept": g_kept})
    comm.took("wgrad_in_kept", xres)
    grad_x, dg1, xres = _in_proj_bwd(dproj, w_in, x, g1, dx1, comm.carry("in_proj_bwd"))
    comm.took("in_proj_bwd", xres)
    return loss_p, grad_x, (dg1, dg2, dg3)


W_KINDS = ("col", "col", "lead", "lead", "lead", "lead", "lead")
W_SHARD = ((1024, W_IN_S), (512, 256), (256, 1024), (256, 1024), (HID_S, 1024), (HID_S, 1024), (HID_S, 1024))
W_TRANSPOSED = (4, 5)
N_W = len(W_KINDS)


def _full_shape(wi):
    R, C = W_SHARD[wi]
    return (R, N_SHARD * C) if W_KINDS[wi] == "col" else (N_SHARD, R, C)


def _view(ref, wi, s, half):
    R, C = W_SHARD[wi]
    rows = pl.ds(half * (R // 2), R // 2)
    if W_KINDS[wi] == "col":
        return ref.at[rows, pl.ds(pl.multiple_of(s * C, 128), C)]
    return ref.at[s, rows, :]


def _mesh_pos():
    x, y, c = lax.axis_index("x"), lax.axis_index("y"), lax.axis_index("c")
    chips = [(1 - x, y), (x, 1 - y), (1 - x, 1 - y)]
    return x, y, c, chips


def _cast_bf16(a):
    R, C = a.shape
    tr = R // 2 if R % 32 == 0 else R

    def body(a_ref, o_ref):
        o_ref[...] = a_ref[...].astype(BF16)

    spec = pl.BlockSpec((tr, C), lambda i: (i, 0))
    return pl.pallas_call(body, out_shape=SDS((R, C), BF16), grid=(R // tr,), in_specs=[spec], out_specs=spec,
                          compiler_params=_cparams("parallel"), name=f"cast_{R}x{C}")(a)


def _remote(send, recv, k, src, dst, to):
    return pltpu.make_async_remote_copy(src_ref=src, dst_ref=dst, send_sem=send.at[k], recv_sem=recv.at[k],
                                        device_id=to, device_id_type=MESH)


def _ex_gather_ring(wis, shards):
    n = len(wis)

    def build(sh, full, send, recv, loc):
        x, y, c, _ = _mesh_pos()
        s_me, sib = 2 * x + y, (x, y, 1 - c)
        xn, yn = (1 - x, y), (x, 1 - y)
        flip = lambda a, b: a + b - 2 * a * b
        via = (flip(x, 1 - c), flip(y, c))
        onto = (flip(x, c), flip(y, 1 - c))
        shard_of = lambda chip: 2 * chip[0] + chip[1]
        starts, waits, sent = [], [], []
        for i, wi in enumerate(wis):
            Rh = W_SHARD[wi][0] // 2
            for hf in range(2):
                cp = pltpu.make_async_copy(sh[i].at[pl.ds(hf * Rh, Rh), :], _view(full[i], wi, s_me, hf), loc.at[2 * i + hf])
                starts.append(cp)
                sent.append(cp.wait)
            for j, chip in enumerate((xn, yn)):
                cp = _remote(send, recv, 6 * i + j, sh[i].at[pl.ds(c * Rh, Rh), :], _view(full[i], wi, s_me, c), (*chip, c))
                starts.append(cp)
                sent.append(cp.wait_send)

        def pass_to_sibling(i, wi, k, s):
            mine = _view(full[i], wi, s, c)
            fw = _remote(send, recv, 6 * i + k, mine, mine, sib)
            waits.append(fw.start)
            sent.append(fw.wait_send)

        for i, wi in enumerate(wis):
            for j, chip in enumerate((xn, yn)):
                land = _view(full[i], wi, shard_of(chip), c)
                waits.append(_remote(send, recv, 6 * i + j, land, land, (*chip, c)).wait_recv)
                pass_to_sibling(i, wi, 3 + j, shard_of(chip))
            relay = _view(full[i], wi, shard_of(via), c)
            fw = _remote(send, recv, 6 * i + 2, relay, relay, (*onto, c))
            waits.append(fw.start)
            sent.append(fw.wait_send)
        s_diag = 2 * (1 - x) + (1 - y)
        for i, wi in enumerate(wis):
            land = _view(full[i], wi, s_diag, c)
            waits.append(_remote(send, recv, 6 * i + 2, land, land, (*onto, c)).wait_recv)
            pass_to_sibling(i, wi, 5, s_diag)
        for i, wi in enumerate(wis):
            for k, s in ((3, shard_of(xn)), (4, shard_of(yn)), (5, s_diag)):
                land = _view(full[i], wi, s, 1 - c)
                waits.append(_remote(send, recv, 6 * i + k, land, land, sib).wait_recv)
        return starts, waits + sent

    return _Exchange(shards, [SDS(_full_shape(wi), BF16) for wi in wis], {}, 6 * n, 2 * n, build)


def _ex_gather_ici(wis, shards, then_d2d=False):
    n = len(wis)

    def build(ins, outs, send, recv, loc):
        x, y, c, chips = _mesh_pos()
        s_me, sib = 2 * x + y, (x, y, 1 - c)
        starts, waits, after = [], [], []
        for i, wi in enumerate(wis):
            Rh = W_SHARD[wi][0] // 2
            for hf in range(2):
                cp = pltpu.make_async_copy(ins[i].at[pl.ds(hf * Rh, Rh), :], _view(outs[i], wi, s_me, hf), loc.at[2 * i + hf])
                starts.append(cp)
                waits.append(cp.wait)
            for j, chip in enumerate(chips):
                cp = _remote(send, recv, 3 * i + j, ins[i].at[pl.ds(c * Rh, Rh), :], _view(outs[i], wi, s_me, c), (*chip, c))
                land = _view(outs[i], wi, 2 * chip[0] + chip[1], c)
                starts.append(cp)
                waits += [cp.wait_send, _remote(send, recv, 3 * i + j, land, land, (*chip, c)).wait_recv]
                if then_d2d:
                    theirs = _view(outs[i], wi, 2 * chip[0] + chip[1], 1 - c)
                    fw = _remote(send, recv, 3 * n + 3 * i + j, land, land, sib)
                    waits.append(fw.start)
                    after += [fw.wait_send, _remote(send, recv, 3 * n + 3 * i + j, theirs, theirs, sib).wait_recv]
        return starts, waits + after

    return _Exchange(shards, [SDS(_full_shape(wi), BF16) for wi in wis], {}, (6 if then_d2d else 3) * n, 2 * n, build)


def _ex_gather_d2d(wis, fulls):
    def build(ins, outs, send, recv, loc):
        x, y, c, chips = _mesh_pos()
        sib = (x, y, 1 - c)
        starts, waits = [], []
        for i, wi in enumerate(wis):
            for j, chip in enumerate(chips):
                mine = _view(outs[i], wi, 2 * chip[0] + chip[1], c)
                theirs = _view(outs[i], wi, 2 * chip[0] + chip[1], 1 - c)
                cp = _remote(send, recv, 3 * i + j, mine, mine, sib)
                starts.append(cp)
                waits += [cp.wait_send, _remote(send, recv, 3 * i + j, theirs, theirs, sib).wait_recv]
        return starts, waits

    return _Exchange(fulls, [SDS(f.shape, BF16) for f in fulls], {i: i for i in range(len(wis))}, 3 * len(wis), 0, build)


def _half_shape(wi):
    R, C = W_SHARD[wi]
    return (R // 2, N_SHARD * C) if W_KINDS[wi] == "col" else (N_SHARD, R // 2, C)


def _ex_pair(wis, grads):
    def build(ins, outs, send, recv, loc):
        x, y, c, _ = _mesh_pos()
        starts, waits = [], []
        for i, wi in enumerate(wis):
            Rh = W_SHARD[wi][0] // 2
            rows = pl.ds((1 - c) * Rh, Rh)
            if tuple(ins[i].shape) == _half_shape(wi):
                src = ins[i]
            else:
                src = ins[i].at[rows, :] if W_KINDS[wi] == "col" else ins[i].at[:, rows, :]
            cp = _remote(send, recv, i, src, outs[i], (x, y, 1 - c))
            starts.append(cp)
            waits.append(cp.wait)
        return starts, waits

    return _Exchange(grads, [SDS(_half_shape(wi), F32) for wi in wis], {}, len(wis), 0, build)


def _ex_chip(wis, pbs):
    def build(ins, outs, send, recv, loc):
        x, y, c, chips = _mesh_pos()
        starts, waits = [], []
        for i, wi in enumerate(wis):
            for j, chip in enumerate(chips):
                cp = _remote(send, recv, 3 * i + j, ins[i].at[j], outs[i].at[j], (*chip, c))
                starts.append(cp)
                waits.append(cp.wait)
        return starts, waits

    shapes = [SDS((3, W_SHARD[wi][0] // 2, W_SHARD[wi][1]), BF16) for wi in wis]
    return _Exchange(pbs, shapes, {}, 3 * len(wis), 0, build)


def _ex_share(wis, halves):
    def build(ins, outs, send, recv, loc):
        x, y, c, _ = _mesh_pos()
        sib = (x, y, 1 - c)
        starts, waits = [], []
        for i, wi in enumerate(wis):
            cp = _remote(send, recv, i, outs[i].at[c], outs[i].at[c], sib)
            starts.append(cp)
            waits += [cp.wait_send, _remote(send, recv, i, outs[i].at[1 - c], outs[i].at[1 - c], sib).wait_recv]
        return starts, waits

    return _Exchange(halves, [SDS(h.shape, F32) for h in halves], {i: i for i in range(len(wis))}, len(wis), 0, build)


def _row_tile(rh, C):
    best = 16
    for t in range(16, rh + 1, 16):
        if rh % t == 0 and t * C * 4 <= (3 << 19):
            best = t
    return best


def _pair_sum(wi, g, ra, sidx):
    R, C = W_SHARD[wi]
    Rh = R // 2
    tr = _row_tile(Rh, C)
    nt = Rh // tr
    off = 0 if tuple(g.shape) == _half_shape(wi) else nt
    col = W_KINDS[wi] == "col"

    def body(sidx_ref, *refs):
        gs, rs = refs[:4], refs[4:8]
        own_ref, pb_ref = refs[8:]
        own_ref[...] = gs[0][...] + rs[0][...]
        for j in range(3):
            pb_ref[j] = (gs[1 + j][...] + rs[1 + j][...]).astype(BF16)

    def gspec(slot):
        if col:
            return pl.BlockSpec((tr, C), lambda i, sx: (sx[4] * off + i, sx[slot]))
        return pl.BlockSpec((None, tr, C), lambda i, sx: (sx[slot], sx[4] * off + i, 0))

    def rspec(slot):
        if col:
            return pl.BlockSpec((tr, C), lambda i, sx: (i, sx[slot]))
        return pl.BlockSpec((None, tr, C), lambda i, sx: (sx[slot], i, 0))

    return pl.pallas_call(
        body, out_shape=(SDS((Rh, C), F32), SDS((3, Rh, C), BF16)),
        grid_spec=pltpu.PrefetchScalarGridSpec(
            num_scalar_prefetch=1, grid=(nt,),
            in_specs=[gspec(k) for k in range(4)] + [rspec(k) for k in range(4)],
            out_specs=(pl.BlockSpec((tr, C), lambda i, sx: (i, 0)), pl.BlockSpec((3, tr, C), lambda i, sx: (0, i, 0)))),
        compiler_params=_cparams("arbitrary"), name=f"pair_sum_w{wi}")(sidx, g, g, g, g, ra, ra, ra, ra)


def _chip_sum(wi, own, rb, sidx):
    R, C = W_SHARD[wi]
    Rh = R // 2
    tr = _row_tile(Rh, C)

    def body(sidx_ref, own_ref, rb_ref, o_ref):
        o_ref[...] = ((own_ref[...] + rb_ref[0].astype(F32)) + rb_ref[1].astype(F32)) + rb_ref[2].astype(F32)

    return pl.pallas_call(
        body, out_shape=SDS((2, Rh, C), F32),
        grid_spec=pltpu.PrefetchScalarGridSpec(
            num_scalar_prefetch=1, grid=(Rh // tr,),
            in_specs=[pl.BlockSpec((tr, C), lambda i, sx: (i, 0)), pl.BlockSpec((3, tr, C), lambda i, sx: (0, i, 0))],
            out_specs=pl.BlockSpec((None, tr, C), lambda i, sx: (sx[4], i, 0))),
        compiler_params=_cparams("arbitrary"), name=f"chip_sum_w{wi}")(sidx, own, rb)


def _gain_allgather(blk, ex):
    m_per, n = blk.shape
    n_in, n_out = len(ex.ins), len(ex.out_shapes)

    def body(x_ref, *rest):
        xin, out_ref, xout = rest[:n_in], rest[n_in], rest[n_in + 1:n_in + 1 + n_out]
        send_sems, recv_sems, local_sem = rest[n_in + 1 + n_out:n_in + 4 + n_out]
        ex_starts, ex_waits = ex.build(xin, xout, *rest[n_in + 4 + n_out:])
        for cp in ex_starts:
            cp.start()
        x, y, c, chips = _mesh_pos()
        me, sibling = (x, y, c), (x, y, 1 - c)

        def rows(px, py, pc):
            return out_ref.at[pl.ds((4 * px + 2 * py + pc) * m_per, m_per), :]

        def copy(k, block, to, src=None):
            return pltpu.make_async_remote_copy(
                src_ref=rows(*block) if src is None else src, dst_ref=rows(*block),
                send_sem=send_sems.at[k], recv_sem=recv_sems.at[k], device_id=to, device_id_type=MESH)

        mine = pltpu.make_async_copy(x_ref, rows(*me), local_sem)
        mine.start()
        first = [copy(0, me, sibling, src=x_ref)]
        first += [copy(1 + j, me, (*chip, c), src=x_ref) for j, chip in enumerate(chips)]
        for cp in first:
            cp.start()
        passed = [copy(4 + j, (*chip, c), sibling) for j, chip in enumerate(chips)]
        for j, chip in enumerate(chips):
            copy(1 + j, (*chip, c), me).wait_recv()
            passed[j].start()
        copy(0, sibling, me).wait_recv()
        for j, chip in enumerate(chips):
            copy(4 + j, (*chip, 1 - c), me).wait_recv()
        for cp in first + passed:
            cp.wait_send()
        mine.wait()
        for w in ex_waits:
            w()

    vm = pl.BlockSpec(memory_space=pltpu.VMEM)
    res = pl.pallas_call(
        body, out_shape=(SDS((8 * m_per, n), blk.dtype), *ex.out_shapes),
        in_specs=[vm] + [ANY] * n_in, out_specs=(vm, *[ANY] * n_out),
        input_output_aliases={1 + a: 1 + o for a, o in ex.aliases.items()},
        scratch_shapes=[pltpu.SemaphoreType.DMA((7,)), pltpu.SemaphoreType.DMA((7,)), pltpu.SemaphoreType.DMA] + ex.sems(),
        name="gain_allgather")(blk, *ex.ins)
    return res[0], tuple(res[1:])


def _adam_math(w, g, m, v):
    mn = ADAM_B1 * m + (1.0 - ADAM_B1) * g
    vn = ADAM_B2 * v + (1.0 - ADAM_B2) * (g * g)
    mh = mn / (1.0 - ADAM_B1 ** ADAM_STEP)
    vh = vn / (1.0 - ADAM_B2 ** ADAM_STEP)
    return -ADAM_LR * (mh / (jnp.sqrt(vh) + ADAM_EPS) + ADAM_WD * w), mn, vn


def _adamw(name, ws, gs, ms, vs):
    n, steps = len(ws), 8

    def body(*refs):
        for k in range(n):
            w_ref, g_ref, m_ref, v_ref = refs[4 * k:4 * k + 4]
            go_ref, d_ref, mn_ref, vn_ref = refs[4 * n + 4 * k:4 * n + 4 * k + 4]
            g = g_ref[...]
            go_ref[...] = g
            d_ref[...], mn_ref[...], vn_ref[...] = _adam_math(w_ref[...], g, m_ref[...], v_ref[...])

    specs = [pl.BlockSpec((w.shape[0] // steps, w.shape[1]), lambda i: (i, 0)) for w in ws for _ in range(4)]
    res = pl.pallas_call(
        body, out_shape=tuple(SDS(w.shape, F32) for w in ws for _ in range(4)), grid=(steps,),
        in_specs=specs, out_specs=tuple(specs), compiler_params=_cparams("parallel"),
        name=name)(*[a for k in range(n) for a in (ws[k], gs[k], ms[k], vs[k])])
    return [tuple(res[4 * k:4 * k + 4]) for k in range(n)]


def _gain_update(gathered, w, m, v):
    def body(ga_ref, w_ref, m_ref, v_ref, g_ref, d_ref, mn_ref, vn_ref):
        g = ga_ref[0:8, :]
        for dev in range(1, 8):
            g = g + ga_ref[8 * dev:8 * dev + 8, :]
        g_ref[...] = g
        d_ref[...], mn_ref[...], vn_ref[...] = _adam_math(w_ref[...], g, m_ref[...], v_ref[...])

    return pl.pallas_call(body, out_shape=(SDS((8, 1024), F32),) * 4, name="gain_update")(gathered, w, m, v)


GROUP_FFN, GROUP_MIX, GROUP_IN = (4, 5, 6), (1, 2, 3), (0,)
REST = GROUP_MIX + GROUP_FFN


class _MeshComm:
    SCHEDULE = {
        "rms_fwd": [("ring", GROUP_IN)],
        "in_proj": [("ici", (1, 2, 3, 4))],
        "ret_fwd": [("d2d", (1, 2, 3, 4)), ("ici", (5,))],
        "mix_out": [("d2d", (5,))],
        "ffn_up": [("both", (6,))],
        "mix_bwd": [("pair", GROUP_FFN)],
        "ret_bwd": [("pair", GROUP_MIX), ("chip", (4,))],
        "attn_bwd_g0": [("chip", (5,))],
        "attn_bwd_g1": [("chip", (6,))],
        "attn_bwd_g2": [("chip", GROUP_MIX)],
        "wgrad_in_kept": [("pair", GROUP_IN), ("share", GROUP_FFN + GROUP_MIX)],
        "in_proj_bwd": [("chip", GROUP_IN)],
    }

    def __init__(self, w_in_shard, rest_f32):
        xi, yi, ci = lax.axis_index("x"), lax.axis_index("y"), lax.axis_index("c")
        self.sidx = jnp.stack([2 * xi + yi, 2 * (1 - xi) + yi, 2 * xi + (1 - yi), 2 * (1 - xi) + (1 - yi), ci]).astype(jnp.int32)
        self.shards, self.rest_f32, self.full = {0: w_in_shard}, list(rest_f32), {}
        self.g, self.own, self.pb, self.half, self.red = {}, {}, {}, {}, {}

    def to_cast(self):
        return self.rest_f32

    def cast_done(self, casts):
        self.shards.update(zip(REST, casts))

    def weight(self, wi):
        return self.full[wi].reshape(D_MODEL, D_MODEL) if wi in (2, 3) else self.full[wi]

    def grads(self, by_wi):
        self.g.update(by_wi)

    def _exchange(self, stage, wis):
        pick = lambda table: [table[wi] for wi in wis]
        if stage == "ring":
            return _ex_gather_ring(wis, pick(self.shards))
        if stage == "ici":
            return _ex_gather_ici(wis, pick(self.shards))
        if stage == "both":
            return _ex_gather_ici(wis, pick(self.shards), then_d2d=True)
        if stage == "d2d":
            return _ex_gather_d2d(wis, pick(self.full))
        if stage == "pair":
            return _ex_pair(wis, [self.g["in_sent"] if wi == 0 else self.g[wi] for wi in wis])
        if stage == "chip":
            return _ex_chip(wis, pick(self.pb))
        return _ex_share(wis, pick(self.half))

    def _landed(self, stage, wis, res):
        for wi, r in zip(wis, res):
            if stage in ("ring", "ici", "d2d", "both"):
                self.full[wi] = r
            elif stage == "pair":
                self.own[wi], self.pb[wi] = _pair_sum(wi, self.g["in_kept"] if wi == 0 else self.g[wi], r, self.sidx)
            elif stage == "chip":
                self.half[wi] = _chip_sum(wi, self.own[wi], r, self.sidx)
            else:
                self.red[wi] = r

    def carry(self, point):
        return [self._exchange(stage, wis) for stage, wis in self.SCHEDULE.get(point, ())]

    def took(self, point, xres):
        for (stage, wis), res in zip(self.SCHEDULE.get(point, ()), xres):
            self._landed(stage, wis, res)

    def last_share(self):
        return self._exchange("share", GROUP_IN)

    def reduced(self, last_shared):
        self._landed("share", GROUP_IN, last_shared)
        return [self.red[wi] for wi in range(N_W)]


def kernel(x, norm_mix_g, w_in, w_out_attn, w_out_ret, w_out, norm_ffn_g, w_ffn_gate, w_ffn_up, w_ffn_down, norm_final_g, loss_target, m_norm_mix_g, m_w_in, m_w_out_attn, m_w_out_ret, m_w_out, m_norm_ffn_g, m_w_ffn_gate, m_w_ffn_up, m_w_ffn_down, m_norm_final_g, v_norm_mix_g, v_w_in, v_w_out_attn, v_w_out_ret, v_w_out, v_norm_ffn_g, v_w_ffn_gate, v_w_ffn_up, v_w_ffn_down, v_norm_final_g):
    ws = (w_in, w_out_attn, w_out_ret, w_out, w_ffn_gate, w_ffn_up, w_ffn_down)
    ms = (m_w_in, m_w_out_attn, m_w_out_ret, m_w_out, m_w_ffn_gate, m_w_ffn_up, m_w_ffn_down)
    vs = (v_w_in, v_w_out_attn, v_w_out_ret, v_w_out, v_w_ffn_gate, v_w_ffn_up, v_w_ffn_down)

    def shard2d(a, wi):
        return jnp.swapaxes(a[0], 0, 1) if wi in W_TRANSPOSED else a.reshape(W_SHARD[wi])

    def as_given(a2d, wi):
        return jnp.swapaxes(a2d, 0, 1)[None] if wi in W_TRANSPOSED else a2d.reshape(ws[wi].shape)

    comm = _MeshComm(_cast_bf16(shard2d(ws[0], 0)), [shard2d(ws[wi], wi) for wi in REST])
    g3 = norm_final_g.reshape(1, D_MODEL)
    loss_p, grad_x, gain_g = _step(x[0], loss_target[0], norm_mix_g, norm_ffn_g, g3, comm)

    pad8 = lambda rows: jnp.concatenate([r.reshape(1, D_MODEL) for r in rows]
                                        + [jnp.zeros((8 - len(rows), D_MODEL), F32)], axis=0)
    gathered, shared = _gain_allgather(pad8((*gain_g, jnp.tile(loss_p[0:1], (1, D_MODEL // 128)))), comm.last_share())
    gred = comm.reduced(shared)

    def adam(name, wis):
        two_d = lambda arrs: [shard2d(arrs[wi], wi) for wi in wis]
        return _adamw(name, two_d(ws), [gred[wi].reshape(W_SHARD[wi]) for wi in wis], two_d(ms), two_d(vs))

    updates = dict(zip(REST + GROUP_IN, adam("adamw_rest", REST) + adam("adamw_w_in", GROUP_IN)))
    outs_g, outs_d, outs_m, outs_v = ([as_given(updates[wi][k], wi) for wi in range(N_W)] for k in range(4))

    gg, gd, gm, gv = _gain_update(gathered, pad8((norm_mix_g, norm_ffn_g, norm_final_g)),
                                  pad8((m_norm_mix_g, m_norm_ffn_g, m_norm_final_g)),
                                  pad8((v_norm_mix_g, v_norm_ffn_g, v_norm_final_g)))
    loss = gg[3, 0]

    def assemble(gain_rows, wlist):
        return (gain_rows[0:1], wlist[0], wlist[1], wlist[2], wlist[3], gain_rows[1:2],
                wlist[4], wlist[5], wlist[6], gain_rows[2])

    return (loss, grad_x[None], *assemble(gg, outs_g), *assemble(gd, outs_d), *assemble(gm, outs_m), *assemble(gv, outs_v))
```

```python
import functools
import math

import numpy as np
import jax
import jax.numpy as jnp
from jax import lax
from jax.experimental import pallas as pl
from jax.experimental.pallas import tpu as pltpu

F32, BF16 = jnp.float32, jnp.bfloat16
SDS = jax.ShapeDtypeStruct
MESH = pl.DeviceIdType.MESH

D_MODEL = 1024
PROJ_W = 9728
COLB = 512
N_COLB = PROJ_W // COLB
QA_B, KA_B, VA_B = 0, 3, 6
QR_B, KR_B = 9, 10
FFN_HID = 2816
N_SHARD = 4
HID_S = FFN_HID // N_SHARD
W_IN_S = PROJ_W // N_SHARD
DILATIONS = (1, 4, 16)
BLK = 128
RET_HEADS = 4
ROPE_THETA = 10000.0
NORM_EPS = 1e-6
ADAM_LR, ADAM_B1, ADAM_B2, ADAM_EPS, ADAM_WD, ADAM_STEP = 0.001, 0.9, 0.999, 1e-08, 0.01, 10
VMEM_LIMIT = 56 << 20


def _cparams(*sem):
    return pltpu.CompilerParams(dimension_semantics=sem or None, vmem_limit_bytes=VMEM_LIMIT)


def _dot(a, b):
    return jnp.dot(a, b, preferred_element_type=F32)


def _dot_nt(a, b):
    return lax.dot_general(a, b, (((1,), (1,)), ((), ())), preferred_element_type=F32)


def _dot_tn(a, b):
    return lax.dot_general(a, b, (((0,), (0,)), ((), ())), preferred_element_type=F32)


def _row_pieces(tm, sub=512):
    return [slice(i, i + sub) for i in range(0, tm, sub)]


def _sigmoid(z):
    return 0.5 * jnp.tanh(0.5 * z) + 0.5


ANY = pl.BlockSpec(memory_space=pl.ANY)


class _Exchange:
    def __init__(self, ins, out_shapes, aliases, n_sem, n_loc, build):
        self.ins, self.out_shapes, self.aliases = list(ins), list(out_shapes), dict(aliases)
        self.n_sem, self.n_loc, self.build = n_sem, n_loc, build

    def sems(self):
        return [pltpu.SemaphoreType.DMA((self.n_sem,)), pltpu.SemaphoreType.DMA((self.n_sem,)),
                pltpu.SemaphoreType.DMA((max(self.n_loc, 1),))]


def _carrier_call(body, args, *, out_shape, grid, in_specs, out_specs, scratch_shapes=(), sem, name, exchanges=(),
                  prefetch=None, in_out_aliases=None):
    out_shape, out_specs = tuple(out_shape), tuple(out_specs)
    n_in, n_out, n_scr = len(args), len(out_shape), len(scratch_shapes)
    n_pre = 0 if prefetch is None else 1
    x_args, x_outs, x_scr, spans = [], [], [], []
    aliases = {n_pre + a: o for a, o in (in_out_aliases or {}).items()}
    for ex in exchanges:
        i0, o0 = len(x_args), len(x_outs)
        for a, o in ex.aliases.items():
            aliases[n_pre + n_in + i0 + a] = n_out + o0 + o
        x_args += ex.ins
        x_outs += ex.out_shapes
        x_scr += ex.sems()
        spans.append((i0, len(ex.ins), o0, len(ex.out_shapes)))
    nx_in, nx_out = len(x_args), len(x_outs)

    def wrapped(*refs):
        refs = refs[n_pre:]
        ins, xin = refs[:n_in], refs[n_in:n_in + nx_in]
        o_base = n_in + nx_in
        outs, xout = refs[o_base:o_base + n_out], refs[o_base + n_out:o_base + n_out + nx_out]
        s_base = o_base + n_out + nx_out
        scr, xs = refs[s_base:s_base + n_scr], refs[s_base + n_scr:]

        def built(e):
            i0, ni, o0, no = spans[e]
            return exchanges[e].build(xin[i0:i0 + ni], xout[o0:o0 + no], *xs[3 * e:3 * e + 3])

        if exchanges:
            first = functools.reduce(jnp.logical_and, [pl.program_id(k) == 0 for k in range(len(grid))])
            last = functools.reduce(jnp.logical_and, [pl.program_id(k) == grid[k] - 1 for k in range(len(grid))])

            @pl.when(first)
            def _():
                for e in range(len(exchanges)):
                    for cp in built(e)[0]:
                        cp.start()

        body(*ins, *outs, *scr)

        if exchanges:
            @pl.when(last)
            def _():
                for e in range(len(exchanges)):
                    for w in built(e)[1]:
                        w()

    all_in, all_out = list(in_specs) + [ANY] * nx_in, out_specs + tuple([ANY] * nx_out)
    all_scr = list(scratch_shapes) + x_scr
    cparams = _cparams(*(sem if not exchanges else ("arbitrary",) * len(grid)))
    if prefetch is None:
        res = pl.pallas_call(wrapped, out_shape=out_shape + tuple(x_outs), grid=grid, in_specs=all_in, out_specs=all_out,
                             scratch_shapes=all_scr, input_output_aliases=aliases, compiler_params=cparams,
                             name=name)(*args, *x_args)
    else:
        gs = pltpu.PrefetchScalarGridSpec(num_scalar_prefetch=1, grid=grid, in_specs=all_in, out_specs=all_out,
                                          scratch_shapes=all_scr)
        res = pl.pallas_call(wrapped, out_shape=out_shape + tuple(x_outs), grid_spec=gs, input_output_aliases=aliases,
                             compiler_params=cparams, name=name)(prefetch, *args, *x_args)
    xres = [tuple(res[n_out + o0:n_out + o0 + no]) for (_, _, o0, no) in spans]
    return tuple(res[:n_out]), xres


def _tables(S):
    f32 = np.float32
    pos = np.arange(S, dtype=f32)
    lane = np.arange(128)
    inv = (f32(ROPE_THETA) ** (-np.arange(0, 64, 2, dtype=f32) / f32(64))).astype(f32)
    ang = (pos[:, None] * inv[None, :]).astype(np.float64)
    idx = (lane % 64) % 32
    c, s = np.cos(ang)[:, idx], np.sin(ang)[:, idx]
    first = ((lane % 64) < 32)[None, :]
    rope = np.stack([c, np.where(first, 0.0, s), np.where(first, -s, 0.0)])
    base = (f32(1.0) / (f32(ROPE_THETA) ** np.linspace(0.0, 1.0, 64, dtype=f32))).astype(f32)
    ang2 = (pos[:, None] * base[None, :]).astype(np.float64)
    c2, s2 = np.cos(ang2)[:, lane // 2], np.sin(ang2)[:, lane // 2]
    even = (lane % 2 == 0)[None, :]
    th = np.stack([c2, np.where(even, 0.0, s2), np.where(even, -s2, 0.0)])
    return np.stack([rope, th, th * (128 ** -0.5)]).astype(f32)


def _rot(a, c, sa, sb, shift):
    return a * c + pltpu.roll(a, shift, 1) * sa + pltpu.roll(a, 128 - shift, 1) * sb


def _unrot(g, c, sa, sb, shift):
    return g * c + pltpu.roll(g * sa, 128 - shift, 1) + pltpu.roll(g * sb, shift, 1)


def _ret_consts():
    h = np.arange(RET_HEADS, dtype=np.float64)
    log_g = np.log1p(-(2.0 ** (-5.0 - h)))
    idx = np.arange(BLK, dtype=np.float64)
    diff = idx[:, None] - idx[None, :]
    dmask = np.where(diff[None] >= 0, np.exp(np.maximum(diff, 0.0)[None] * log_g[:, None, None]), 0.0)
    zeta = np.exp((BLK - 1 - idx)[None, :] * log_g[:, None])
    xi = np.exp((idx + 1.0)[None, :] * log_g[:, None])
    dec = np.exp(BLK * log_g)
    rep = lambda v: np.broadcast_to(v[:, :, None], (RET_HEADS, BLK, 128))
    return (jnp.asarray(dmask, F32), jnp.asarray(rep(zeta), F32), jnp.asarray(rep(xi), F32),
            jnp.asarray(np.broadcast_to(dec[:, None, None], (RET_HEADS, 8, 256)), F32))


def _rms_fwd(x, g, to_cast=(), exchanges=()):
    S = x.shape[0]
    steps = 4
    tm = S // steps
    n_c = len(to_cast)

    def body(x_ref, g_ref, *refs):
        c_in, (h_ref, ht_ref), c_out = refs[:n_c], refs[n_c:n_c + 2], refs[n_c + 2:]
        for rows in _row_pieces(tm, 512):
            xv = x_ref[rows, :]
            r = lax.rsqrt(jnp.mean(xv * xv, axis=-1, keepdims=True) + NORM_EPS)
            h = xv * r * g_ref[...]
            h_ref[rows, :] = h.astype(BF16)
            ht_ref[:, rows] = h.T.astype(BF16)
        for a_ref, o_ref in zip(c_in, c_out):
            o_ref[...] = a_ref[...].astype(BF16)

    slab = lambda a: pl.BlockSpec((a.shape[0] // steps, a.shape[1]), lambda i: (i, 0))
    return _carrier_call(
        body, (x, g, *to_cast),
        out_shape=(SDS((S, D_MODEL), BF16), SDS((D_MODEL, S), BF16), *[SDS(a.shape, BF16) for a in to_cast]),
        grid=(steps,),
        in_specs=[pl.BlockSpec((tm, D_MODEL), lambda i: (i, 0)), pl.BlockSpec((1, D_MODEL), lambda i: (0, 0))]
        + [slab(a) for a in to_cast],
        out_specs=(pl.BlockSpec((tm, D_MODEL), lambda i: (i, 0)), pl.BlockSpec((D_MODEL, tm), lambda i: (0, i)),
                   *[slab(a) for a in to_cast]),
        sem=("parallel",), name="rms_fwd", exchanges=exchanges)


def _in_proj(h, w_in, tab, exchanges=()):
    S = h.shape[0]
    tm = min(S, 4096)

    def body(h_ref, w_ref, t_ref, o_ref):
        j = pl.program_id(1)
        is_rope = j < 6
        is_theta = (j == QR_B) | (j == KR_B)
        sub = 512

        def rotated(shift):
            for i in range(tm // sub):
                rows = slice(i * sub, (i + 1) * sub)
                acc = _dot(h_ref[rows, :], w_ref[...])
                c, sa, sb = t_ref[0, 0, rows, :], t_ref[0, 1, rows, :], t_ref[0, 2, rows, :]
                for k in range(COLB // 128):
                    sl = slice(k * 128, (k + 1) * 128)
                    o_ref[rows, sl] = _rot(acc[:, sl], c, sa, sb, shift).astype(BF16)

        @pl.when(is_rope)
        def _():
            rotated(32)

        @pl.when(is_theta)
        def _():
            rotated(1)

        @pl.when(jnp.logical_not(is_rope | is_theta))
        def _():
            o_ref[...] = _dot(h_ref[...], w_ref[...]).astype(BF16)

    def tab_map(i, j):
        return (jnp.where(j == QR_B, 1, jnp.where(j == KR_B, 2, 0)), 0, i, 0)

    (proj,), xres = _carrier_call(
        body, (h, w_in, tab), out_shape=(SDS((S, PROJ_W), BF16),), grid=(S // tm, N_COLB),
        in_specs=[pl.BlockSpec((tm, D_MODEL), lambda i, j: (i, 0)),
                  pl.BlockSpec((D_MODEL, COLB), lambda i, j: (0, j)),
                  pl.BlockSpec((1, 3, tm, 128), tab_map)],
        out_specs=(pl.BlockSpec((tm, COLB), lambda i, j: (i, j)),),
        sem=("parallel", "arbitrary"), name="in_proj", exchanges=exchanges)
    return proj, xres


def _band_mask(n):
    qi = lax.broadcasted_iota(jnp.int32, (BLK, 2 * BLK), 0)
    kj = lax.broadcasted_iota(jnp.int32, (BLK, 2 * BLK), 1)
    dist = BLK + qi - kj
    return (dist >= 0) & (dist <= BLK) & ((kj >= BLK) | (n > 0))


def _qkv_col(d, gi):
    if d == 1:
        return lambda t, r: 3 * t + gi
    return lambda t, r: 3 * r + t


def _attn_fwd(qkv, d, gi, exchanges=()):
    L = qkv.shape[0]
    nb = L // BLK

    def body(q_ref, kc_ref, kp_ref, vc_ref, vp_ref, o_ref, lse_ref):
        n = pl.program_id(1)
        mask = _band_mask(n)
        mask2 = jnp.concatenate([mask, mask], axis=0)
        lane = lax.broadcasted_iota(jnp.int32, (BLK, 128), 1)
        lo = lane < 64
        lse_all = jnp.zeros((BLK, 128), F32)
        chunks = [slice(c * 128, (c + 1) * 128) for c in range(4)]
        scores, vals = [], []
        for sl in chunks:
            q = q_ref[:, sl]
            k = jnp.concatenate([kp_ref[:, sl], kc_ref[:, sl]], axis=0)
            vals.append(jnp.concatenate([vp_ref[:, sl], vc_ref[:, sl]], axis=0))
            q2 = jnp.concatenate([jnp.where(lo, q, jnp.zeros_like(q)), jnp.where(lo, jnp.zeros_like(q), q)], axis=0)
            scores.append(_dot_nt(q2, k))
        probs = []
        for c, s in enumerate(scores):
            s = jnp.where(mask2, s * 0.125, jnp.float32(-1e30))
            m = jnp.max(s, axis=-1, keepdims=True)
            p = jnp.exp(s - m)
            l = jnp.sum(p, axis=-1, keepdims=True)
            probs.append((p / l).astype(BF16))
            lse = m + jnp.log(l)
            lse_all = jnp.where(lane // 16 == 2 * c, lse[:BLK], jnp.where(lane // 16 == 2 * c + 1, lse[BLK:], lse_all))
        for sl, p, v in zip(chunks, probs, vals):
            o2 = _dot(p, v)
            o_ref[:, sl] = jnp.where(lo, o2[:BLK], o2[BLK:])
        lse_ref[...] = lse_all

    prev = lambda n: jnp.maximum(n - 1, 0)
    col = _qkv_col(d, gi)
    return _carrier_call(
        body, (qkv,) * 5, out_shape=(SDS((L, d * 512), F32), SDS((L, d * 128), F32)), grid=(d, nb),
        in_specs=[pl.BlockSpec((BLK, 512), lambda r, n: (n, col(0, r))),
                  pl.BlockSpec((BLK, 512), lambda r, n: (n, col(1, r))),
                  pl.BlockSpec((BLK, 512), lambda r, n: (prev(n), col(1, r))),
                  pl.BlockSpec((BLK, 512), lambda r, n: (n, col(2, r))),
                  pl.BlockSpec((BLK, 512), lambda r, n: (prev(n), col(2, r)))],
        out_specs=(pl.BlockSpec((BLK, 512), lambda r, n: (n, r)),
                   pl.BlockSpec((BLK, 128), lambda r, n: (n, r))),
        sem=("parallel", "arbitrary"), name=f"attn_fwd_g{gi}", exchanges=exchanges)


def _qkv_to_sub(proj, d, gi):
    S = proj.shape[0]
    tm = 512
    n = tm // d

    def body(q_ref, k_ref, v_ref, o_ref, scr):
        for t, ref in enumerate((q_ref, k_ref, v_ref)):
            for c in range(4):
                scr[c] = ref[:, c * 128:(c + 1) * 128].astype(F32)
            for r in range(d):
                for c in range(4):
                    col = (3 * r + t) * 512 + c * 128
                    o_ref[:, col:col + 128] = scr[c, pl.ds(r, n, stride=d), :].astype(BF16)

    return pl.pallas_call(
        body, out_shape=SDS((S // d, d * 1536), BF16), grid=(S // tm,),
        in_specs=[pl.BlockSpec((tm, 512), lambda i, b=b: (i, b + gi)) for b in (QA_B, KA_B, VA_B)],
        out_specs=pl.BlockSpec((n, d * 1536), lambda i: (i, 0)),
        scratch_shapes=[pltpu.VMEM((4, tm, 128), F32)],
        compiler_params=_cparams("parallel"), name=f"qkv_to_sub_g{gi}")(proj, proj, proj)


def _attn_merge(os_, lses):
    S = os_[0].shape[0]
    tm = 512

    def body(o0, o1, o2, l0, l1, l2, att_ref, lt_ref, lt1_ref, lt2_ref, so1, so2, sl1, sl2):
        lo = lax.broadcasted_iota(jnp.int32, (tm, 128), 1) < 64

        def natural(ref, d, scr, width):
            nch = width // 128
            if d == 1:
                return [ref[:, c * 128:(c + 1) * 128] for c in range(nch)]
            for r in range(d):
                for c in range(nch):
                    scr[c, pl.ds(r, tm // d, stride=d), :] = ref[:, r * width + c * 128:r * width + (c + 1) * 128]
            return [scr[c] for c in range(nch)]

        ls = [natural(l, d, s, 128)[0] for l, d, s in zip((l0, l1, l2), DILATIONS, (None, sl1, sl2))]
        m = jnp.maximum(jnp.maximum(ls[0], ls[1]), ls[2])
        es = [jnp.exp(v - m) for v in ls]
        z = es[0] + es[1] + es[2]
        lt = m + jnp.log(z)
        lt_ref[...] = lt
        sl1[0] = lt
        for ref, d in ((lt1_ref, DILATIONS[1]), (lt2_ref, DILATIONS[2])):
            for r in range(d):
                ref[:, r * 128:(r + 1) * 128] = sl1[0, pl.ds(r, tm // d, stride=d), :]
        ws = [e / z for e in es]
        o_nat = [natural(o, d, s, 512) for o, d, s in zip((o0, o1, o2), DILATIONS, (None, so1, so2))]
        for c in range(4):
            acc = jnp.zeros((tm, 128), F32)
            for g in range(3):
                w_lo = jnp.broadcast_to(ws[g][:, 32 * c:32 * c + 1], (tm, 128))
                w_hi = jnp.broadcast_to(ws[g][:, 32 * c + 16:32 * c + 17], (tm, 128))
                acc = acc + jnp.where(lo, w_lo, w_hi) * o_nat[g][c]
            att_ref[:, c * 128:(c + 1) * 128] = acc.astype(BF16)

    sub = lambda w: [pl.BlockSpec((tm // d, d * w), lambda i: (i, 0)) for d in DILATIONS]
    att, *lts = pl.pallas_call(
        body, out_shape=(SDS((S, 512), BF16), *[SDS((S // d, d * 128), F32) for d in DILATIONS]), grid=(S // tm,),
        in_specs=sub(512) + sub(128),
        out_specs=(pl.BlockSpec((tm, 512), lambda i: (i, 0)), *sub(128)),
        scratch_shapes=[pltpu.VMEM((4, tm, 128), F32), pltpu.VMEM((4, tm, 128), F32),
                        pltpu.VMEM((1, tm, 128), F32), pltpu.VMEM((1, tm, 128), F32)],
        compiler_params=_cparams("parallel"), name="attn_merge")(*os_, *lses)
    return att, lts


def _assemble_dproj(att_grads, dproj):
    S = dproj.shape[0]
    tm = 256

    def body(*refs):
        a = [refs[3 * t:3 * t + 3] for t in range(3)]
        dp_prev, o_ref, scr = refs[9:]
        for t in range(3):
            for g, d in enumerate(DILATIONS):
                base = (3 * t + g) * COLB
                if d == 1:
                    o_ref[:, base:base + COLB] = a[t][g][...]
                    continue
                for c in range(4):
                    for r in range(d):
                        scr[c, pl.ds(r, tm // d, stride=d), :] = a[t][g][:, r * 512 + c * 128:r * 512 + (c + 1) * 128].astype(F32)
                    o_ref[:, base + c * 128:base + (c + 1) * 128] = scr[c].astype(BF16)

    sub = [pl.BlockSpec((tm // d, d * 512), lambda i: (i, 0)) for d in DILATIONS]
    flat = [att_grads[t][g] for t in range(3) for g in range(3)]
    return pl.pallas_call(
        body, out_shape=SDS((S, PROJ_W), BF16), grid=(S // tm,),
        in_specs=sub * 3 + [ANY], out_specs=pl.BlockSpec((tm, 9 * COLB), lambda i: (i, 0)),
        scratch_shapes=[pltpu.VMEM((4, tm, 128), F32)], input_output_aliases={9: 0},
        compiler_params=_cparams("parallel"), name="assemble_dproj")(*flat, dproj)


def _ret_fwd(proj, consts, exchanges=()):
    S = proj.shape[0]
    nc = S // BLK
    dmask, zeta, xi, dec = consts

    def body(q_ref, k_ref, v0_ref, v1_ref, g0_ref, g1_ref, dm_ref, z_ref, x_ref, dec_ref,
             y_ref, rn_ref, rs_ref, st_ref, R):
        @pl.when(pl.program_id(0) == 0)
        def _():
            R[...] = jnp.zeros_like(R)

        lane16 = lax.broadcasted_iota(jnp.int32, (BLK, 128), 1) // 16
        rs_all = jnp.zeros((BLK, 128), F32)
        first = []
        for h in range(RET_HEADS):
            hs = slice(h * 128, (h + 1) * 128)
            q, k = q_ref[:, hs], k_ref[:, hs]
            v = (v0_ref if h < 2 else v1_ref)[:, (h % 2) * 256:(h % 2 + 1) * 256]
            Rb = R[h].astype(BF16)
            st_ref[h] = Rb
            kz = (k.astype(F32) * z_ref[h]).astype(BF16)
            first.append((v, _dot_nt(q, k), _dot((q.astype(F32) * x_ref[h]).astype(BF16), Rb), _dot_tn(kz, v)))
        masked = [(s * dm_ref[h]).astype(BF16) for h, (_, s, _, _) in enumerate(first)]
        for h in range(RET_HEADS):
            vs = slice((h % 2) * 256, (h % 2 + 1) * 256)
            os_ = slice(h * 256, (h + 1) * 256)
            v, _, cross, kv = first[h]
            o = _dot(masked[h], v) + cross
            R[h] = R[h] * dec_ref[h, 0:1, :] + kv
            mu = jnp.mean(o, axis=-1, keepdims=True)
            oc = o - mu
            rstd = lax.rsqrt(jnp.mean(oc * oc, axis=-1, keepdims=True) + NORM_EPS)
            rn = oc * rstd
            gr = (g0_ref if h < 2 else g1_ref)[:, vs].astype(F32)
            y_ref[:, os_] = (rn * gr * _sigmoid(gr)).astype(BF16)
            rn_ref[:, os_] = rn.astype(BF16)
            rs_all = jnp.where(lane16 == h, rstd, rs_all)
        rs_ref[...] = rs_all

    cst = lambda shape: pl.BlockSpec(shape, lambda c: (0, 0, 0))
    blk = lambda j: pl.BlockSpec((BLK, 512), lambda c: (c, j))
    return _carrier_call(
        body, (proj, proj, proj, proj, proj, proj, dmask, zeta, xi, dec),
        out_shape=(SDS((S, 1024), BF16), SDS((S, 1024), BF16), SDS((S, 128), F32), SDS((RET_HEADS, nc, BLK, 256), BF16)),
        grid=(nc,),
        in_specs=[blk(QR_B), blk(KR_B), blk(11), blk(12), blk(13), blk(14),
                  cst((RET_HEADS, BLK, BLK)), cst((RET_HEADS, BLK, 128)), cst((RET_HEADS, BLK, 128)), cst((RET_HEADS, 8, 256))],
        out_specs=(pl.BlockSpec((BLK, 1024), lambda c: (c, 0)), pl.BlockSpec((BLK, 1024), lambda c: (c, 0)),
                   pl.BlockSpec((BLK, 128), lambda c: (c, 0)),
                   pl.BlockSpec((RET_HEADS, None, BLK, 256), lambda c: (0, c, 0, 0))),
        scratch_shapes=[pltpu.VMEM((RET_HEADS, BLK, 256), F32)],
        sem=("arbitrary",), name="ret_fwd", exchanges=exchanges)


def _mix_out(att, yrin, proj, wa, wr, wo, x, g2, exchanges=()):
    S = x.shape[0]
    tm = 512
    gate0 = 15 * COLB

    def body(a_ref, y_ref, ga_ref, gr_ref, wa_ref, wr_ref, wo_ref, x_ref, g_ref, m_ref, ya_ref, yr_ref, x1_ref, h2_ref):
        pieces = _row_pieces(tm, 256)
        branches = [(_dot(a_ref[rows, :], wa_ref[...]), _dot(y_ref[rows, :], wr_ref[...])) for rows in pieces]
        merged = []
        for rows, (ya, yr) in zip(pieces, branches):
            m = (_sigmoid(ga_ref[rows, :].astype(F32)) * ya + _sigmoid(gr_ref[rows, :].astype(F32)) * yr).astype(BF16)
            m_ref[rows, :] = m
            ya_ref[rows, :] = ya.astype(BF16)
            yr_ref[rows, :] = yr.astype(BF16)
            merged.append(m)
        for rows, m in zip(pieces, merged):
            x1 = x_ref[rows, :] + _dot(m, wo_ref[...])
            x1_ref[rows, :] = x1
            r = lax.rsqrt(jnp.mean(x1 * x1, axis=-1, keepdims=True) + NORM_EPS)
            h2_ref[rows, :] = (x1 * r * g_ref[...]).astype(BF16)

    row = lambda w: pl.BlockSpec((tm, w), lambda i: (i, 0))
    cols = lambda c0: pl.BlockSpec((pl.Element(tm), pl.Element(D_MODEL)), lambda i: (i * tm, c0))
    resident = lambda r, c: pl.BlockSpec((r, c), lambda i: (0, 0), pipeline_mode=pl.Buffered(1))
    return _carrier_call(
        body, (att, yrin, proj, proj, wa, wr, wo, x, g2),
        out_shape=(SDS((S, D_MODEL), BF16),) * 3 + (SDS((S, D_MODEL), F32), SDS((S, D_MODEL), BF16)), grid=(S // tm,),
        in_specs=[row(512), row(D_MODEL), cols(gate0), cols(gate0 + D_MODEL), resident(512, D_MODEL),
                  resident(D_MODEL, D_MODEL), resident(D_MODEL, D_MODEL), row(D_MODEL),
                  pl.BlockSpec((1, D_MODEL), lambda i: (0, 0))],
        out_specs=(row(D_MODEL),) * 5, sem=("parallel",), name="mix_out", exchanges=exchanges)


def _ffn_up(h2, wg, wu, exchanges=()):
    S = h2.shape[0]
    tm = min(S, 2048)

    def body(h_ref, wg_ref, wu_ref, g_ref, u_ref, a_ref):
        for rows in _row_pieces(tm):
            hv = h_ref[rows, :]
            g = _dot_nt(hv, wg_ref[...])
            u = _dot_nt(hv, wu_ref[...])
            g_ref[rows, :] = g.astype(BF16)
            u_ref[rows, :] = u.astype(BF16)
            a_ref[rows, :] = (g * _sigmoid(g) * u).astype(BF16)

    wspec = pl.BlockSpec((None, HID_S, D_MODEL), lambda i, s: (s, 0, 0))
    ospec = pl.BlockSpec((None, tm, HID_S), lambda i, s: (s, i, 0))
    return _carrier_call(
        body, (h2, wg, wu), out_shape=(SDS((N_SHARD, S, HID_S), BF16),) * 3, grid=(S // tm, N_SHARD),
        in_specs=[pl.BlockSpec((tm, D_MODEL), lambda i, s: (i, 0)), wspec, wspec],
        out_specs=(ospec, ospec, ospec),
        sem=("parallel", "arbitrary"), name="ffn_up", exchanges=exchanges)


def _ffn_down_loss(act, wd, x1, g3, tgt):
    S = x1.shape[0]
    tm = 512

    def body(a_ref, w_ref, x_ref, g_ref, t_ref, dx_ref, dxb_ref, dg_ref, ls_ref):
        @pl.when(pl.program_id(0) == 0)
        def _():
            dg_ref[...] = jnp.zeros_like(dg_ref)
            ls_ref[...] = jnp.zeros_like(ls_ref)

        g = g_ref[...]
        for rows in _row_pieces(tm, 256):
            y = _dot(a_ref[0, rows, :], w_ref[0])
            for s in range(1, N_SHARD):
                y = y + _dot(a_ref[s, rows, :], w_ref[s])
            x2 = x_ref[rows, :] + y
            r = lax.rsqrt(jnp.mean(x2 * x2, axis=-1, keepdims=True) + NORM_EPS)
            xh = x2 * r
            err = xh * g - t_ref[rows, :]
            ls_ref[...] += jnp.sum(jnp.sum(err * err, axis=-1, keepdims=True), axis=0, keepdims=True) * (0.5 / D_MODEL)
            dy = err * (1.0 / D_MODEL)
            dg_ref[...] += jnp.sum(dy * xh, axis=0, keepdims=True)
            dxh = dy * g
            dx = r * (dxh - xh * jnp.mean(dxh * xh, axis=-1, keepdims=True))
            dx_ref[rows, :] = dx
            dxb_ref[rows, :] = dx.astype(BF16)

    row = pl.BlockSpec((tm, D_MODEL), lambda i: (i, 0))
    vec = pl.BlockSpec((1, D_MODEL), lambda i: (0, 0))
    return pl.pallas_call(
        body, out_shape=(SDS((S, D_MODEL), F32), SDS((S, D_MODEL), BF16), SDS((1, D_MODEL), F32), SDS((8, 128), F32)),
        grid=(S // tm,),
        in_specs=[pl.BlockSpec((N_SHARD, tm, HID_S), lambda i: (0, i, 0)),
                  pl.BlockSpec((N_SHARD, HID_S, D_MODEL), lambda i: (0, 0, 0), pipeline_mode=pl.Buffered(1)),
                  row, vec, row],
        out_specs=(row, row, vec, pl.BlockSpec((8, 128), lambda i: (0, 0))),
        compiler_params=_cparams("arbitrary"), name="ffn_down_loss")(act, wd, x1, g3, tgt)


def _ffn_bwd(dx2b, dx2, wd, wg, wu, gte, up, x1, g2):
    S = x1.shape[0]
    tm = 256

    def body(d_ref, dx2_ref, wd_ref, wg_ref, wu_ref, g_ref, u_ref, x_ref, gn_ref,
             dg_ref, du_ref, dx_ref, dxb_ref, dgn_ref):
        @pl.when(pl.program_id(0) == 0)
        def _():
            dgn_ref[...] = jnp.zeros_like(dgn_ref)

        d = d_ref[...]
        dacts = [_dot_nt(d, wd_ref[s]) for s in range(N_SHARD)]
        dgs, dus = [], []
        for s, da in enumerate(dacts):
            g = g_ref[s].astype(F32)
            sg = _sigmoid(g)
            dgs.append((da * u_ref[s].astype(F32) * sg * (1.0 + g * (1.0 - sg))).astype(BF16))
            dus.append((da * g * sg).astype(BF16))
            dg_ref[s] = dgs[s]
            du_ref[s] = dus[s]
        dh = _dot(dgs[0], wg_ref[0]) + _dot(dus[0], wu_ref[0])
        for s in range(1, N_SHARD):
            dh = dh + _dot(dgs[s], wg_ref[s]) + _dot(dus[s], wu_ref[s])
        xv = x_ref[...]
        r = lax.rsqrt(jnp.mean(xv * xv, axis=-1, keepdims=True) + NORM_EPS)
        xh = xv * r
        dgn_ref[...] += jnp.sum(dh * xh, axis=0, keepdims=True)
        dxh = dh * gn_ref[...]
        dx = dx2_ref[...] + r * (dxh - xh * jnp.mean(dxh * xh, axis=-1, keepdims=True))
        dx_ref[...] = dx
        dxb_ref[...] = dx.astype(BF16)

    row = pl.BlockSpec((tm, D_MODEL), lambda i: (i, 0))
    vec = pl.BlockSpec((1, D_MODEL), lambda i: (0, 0))
    aspec = pl.BlockSpec((N_SHARD, tm, HID_S), lambda i: (0, i, 0))
    resident = lambda shape: pl.BlockSpec(shape, lambda i: (0, 0, 0), pipeline_mode=pl.Buffered(1))
    return pl.pallas_call(
        body,
        out_shape=(SDS((N_SHARD, S, HID_S), BF16), SDS((N_SHARD, S, HID_S), BF16),
                   SDS((S, D_MODEL), F32), SDS((S, D_MODEL), BF16), SDS((1, D_MODEL), F32)),
        grid=(S // tm,),
        in_specs=[row, row, resident((N_SHARD, HID_S, D_MODEL)), resident((N_SHARD, HID_S, D_MODEL)),
                  resident((N_SHARD, HID_S, D_MODEL)), aspec, aspec, row, vec],
        out_specs=(aspec, aspec, row, row, vec),
        compiler_params=_cparams("arbitrary"), name="ffn_bwd")(dx2b, dx2, wd, wg, wu, gte, up, x1, g2)


def _wgrad(name, a, b, a_spec, b_spec, out_shape, out_spec, n_par, S):
    tk = min(S, 4096)

    def body(a_ref, b_ref, o_ref):
        @pl.when(pl.program_id(1) == 0)
        def _():
            o_ref[...] = jnp.zeros_like(o_ref)

        o_ref[...] += _dot_tn(a_ref[...], b_ref[...])

    return pl.pallas_call(
        body, out_shape=SDS(out_shape, F32), grid=(n_par, S // tk),
        in_specs=[a_spec(tk), b_spec(tk)], out_specs=out_spec,
        compiler_params=_cparams("parallel", "arbitrary"), name=name)(a, b)


def _mix_bwd(dx1b, wo, proj, ya, yr, wa, wr, att, exchanges=()):
    S = dx1b.shape[0]
    tm = 512
    gate0 = 15 * COLB

    def body(d_ref, wo_ref, ga_ref, gr_ref, ya_ref, yr_ref, wa_ref, wr_ref, att_ref,
             dya_ref, dyr_ref, dp_ref, dyi_ref, datt_ref, datt1_ref, datt2_ref, rho_ref, rho1_ref, rho2_ref,
             datt_scr, rho_scr):
        pieces = _row_pieces(tm, 256)
        dms = [_dot_nt(d_ref[rows, :], wo_ref[...]) for rows in pieces]
        branch = []
        for rows, dm in zip(pieces, dms):
            sa = _sigmoid(ga_ref[rows, :].astype(F32))
            sr = _sigmoid(gr_ref[rows, :].astype(F32))
            dya, dyr = (dm * sa).astype(BF16), (dm * sr).astype(BF16)
            dya_ref[rows, :] = dya
            dyr_ref[rows, :] = dyr
            dp_ref[rows, 0:D_MODEL] = (dm * ya_ref[rows, :].astype(F32) * sa * (1.0 - sa)).astype(BF16)
            dp_ref[rows, D_MODEL:2 * D_MODEL] = (dm * yr_ref[rows, :].astype(F32) * sr * (1.0 - sr)).astype(BF16)
            branch.append((dya, dyr))
        lane = lax.broadcasted_iota(jnp.int32, (256, 128), 1)
        lo = lane < 64
        for rows, (dya, dyr) in zip(pieces, branch):
            datt = _dot_nt(dya, wa_ref[...])
            datt_ref[rows, :] = datt.astype(BF16)
            dyi_ref[rows, :] = _dot_nt(dyr, wr_ref[...]).astype(BF16)
            prod = datt * att_ref[rows, :].astype(F32)
            rho = jnp.zeros((256, 128), F32)
            for c in range(4):
                pc = prod[:, c * 128:(c + 1) * 128]
                tot = jnp.sum(pc, axis=-1, keepdims=True)
                low = jnp.sum(jnp.where(lo, pc, 0.0), axis=-1, keepdims=True)
                rho = jnp.where(lane // 16 == 2 * c, low, jnp.where(lane // 16 == 2 * c + 1, tot - low, rho))
            rho_ref[rows, :] = rho
            rho_scr[0] = rho
            for c in range(4):
                datt_scr[c] = datt[:, c * 128:(c + 1) * 128]
            for d, dv_ref, rv_ref in ((DILATIONS[1], datt1_ref, rho1_ref), (DILATIONS[2], datt2_ref, rho2_ref)):
                n = 256 // d
                sub_rows = slice(rows.start // d, rows.start // d + n)
                for r in range(d):
                    rv_ref[sub_rows, r * 128:(r + 1) * 128] = rho_scr[0, pl.ds(r, n, stride=d), :]
                    for c in range(4):
                        col = r * 512 + c * 128
                        dv_ref[sub_rows, col:col + 128] = datt_scr[c, pl.ds(r, n, stride=d), :].astype(BF16)

    row = lambda w: pl.BlockSpec((tm, w), lambda i: (i, 0))
    sub = lambda w: [pl.BlockSpec((tm // d, d * w), lambda i: (i, 0)) for d in DILATIONS]
    cols = lambda c0, w: pl.BlockSpec((pl.Element(tm), pl.Element(w)), lambda i: (i * tm, c0))
    resident = lambda r, c: pl.BlockSpec((r, c), lambda i: (0, 0), pipeline_mode=pl.Buffered(1))
    (dya, dyr, dproj, dyrin, *views), xres = _carrier_call(
        body, (dx1b, wo, proj, proj, ya, yr, wa, wr, att),
        out_shape=(SDS((S, D_MODEL), BF16), SDS((S, D_MODEL), BF16), SDS((S, PROJ_W), BF16), SDS((S, D_MODEL), BF16),
                   *[SDS((S // d, d * 512), BF16) for d in DILATIONS], *[SDS((S // d, d * 128), F32) for d in DILATIONS]),
        grid=(S // tm,),
        in_specs=[row(D_MODEL), resident(D_MODEL, D_MODEL), cols(gate0, D_MODEL), cols(gate0 + D_MODEL, D_MODEL),
                  row(D_MODEL), row(D_MODEL), resident(512, D_MODEL), resident(D_MODEL, D_MODEL), row(512)],
        out_specs=(row(D_MODEL), row(D_MODEL), cols(gate0, 2 * D_MODEL), row(D_MODEL), *sub(512), *sub(128)),
        scratch_shapes=[pltpu.VMEM((4, 256, 128), F32), pltpu.VMEM((1, 256, 128), F32)],
        sem=("parallel",), name="mix_bwd", exchanges=exchanges)
    return (dya, dyr, dproj, dyrin, views[:3], views[3:]), xres


def _attn_bwd(qkv, datt, lse, rho, rtab, d, gi, exchanges=()):
    L = qkv.shape[0]
    nb = L // BLK
    T = d * nb

    def body(q_ref, kc_ref, kp_ref, vc_ref, vp_ref, do_ref, lse_ref, rho_ref, tq_ref, tk_ref,
             dq_ref, dk_ref, dv_ref, ck, cv):
        t = pl.program_id(0)
        n = jnp.minimum(t, T - 1) % nb

        @pl.when(t == 0)
        def _():
            ck[...] = jnp.zeros_like(ck)
            cv[...] = jnp.zeros_like(cv)

        def store_rot(ref, val, t_ref, c):
            sl = slice(c * 128, (c + 1) * 128)
            ref[:, sl] = _unrot(val, t_ref[0], t_ref[1], t_ref[2], 32).astype(BF16)

        @pl.when(t < T)
        def _():
            mask = _band_mask(n)
            mask2 = jnp.concatenate([mask, mask], axis=0)
            lo = lax.broadcasted_iota(jnp.int32, (BLK, 128), 1) < 64

            def stacked(a):
                return jnp.concatenate([jnp.where(lo, a, jnp.zeros_like(a)), jnp.where(lo, jnp.zeros_like(a), a)], axis=0)

            def head_cols(ref, c):
                return jnp.concatenate([jnp.broadcast_to(ref[:, 32 * c:32 * c + 1], (BLK, 2 * BLK)),
                                        jnp.broadcast_to(ref[:, 32 * c + 16:32 * c + 17], (BLK, 2 * BLK))], axis=0)

            ops, raw = [], []
            for c in range(4):
                sl = slice(c * 128, (c + 1) * 128)
                q2, do2 = stacked(q_ref[:, sl]), stacked(do_ref[:, sl])
                k = jnp.concatenate([kp_ref[:, sl], kc_ref[:, sl]], axis=0)
                v = jnp.concatenate([vp_ref[:, sl], vc_ref[:, sl]], axis=0)
                ops.append((q2, do2, k))
                raw.append((_dot_nt(q2, k), _dot_nt(do2, v)))
            grads = []
            for c, (s, dp) in enumerate(raw):
                p = jnp.where(mask2, jnp.exp(s * 0.125 - head_cols(lse_ref, c)), 0.0)
                grads.append(((p * (dp - head_cols(rho_ref, c)) * 0.125).astype(BF16), p.astype(BF16)))
            for c, ((q2, do2, k), (ds, pb)) in enumerate(zip(ops, grads)):
                sl = slice(c * 128, (c + 1) * 128)
                dq2 = _dot(ds, k)
                dq_c = jnp.where(lo, dq2[:BLK], dq2[BLK:])
                dk_c = _dot_tn(ds, q2)
                dv_c = _dot_tn(pb, do2)
                store_rot(dq_ref, dq_c, tq_ref, c)
                store_rot(dk_ref, ck[:, sl] + dk_c[:BLK], tk_ref, c)
                dv_ref[:, sl] = (cv[:, sl] + dv_c[:BLK]).astype(BF16)
                ck[:, sl] = dk_c[BLK:]
                cv[:, sl] = dv_c[BLK:]

        @pl.when(t == T)
        def _():
            for c in range(4):
                sl = slice(c * 128, (c + 1) * 128)
                store_rot(dk_ref, ck[:, sl], tk_ref, c)
            dv_ref[...] = cv[...].astype(BF16)

    blk_of = lambda t: (jnp.minimum(t, T - 1) % nb, jnp.minimum(t, T - 1) // nb)
    cur = lambda t: blk_of(t)
    prev = lambda t: (jnp.maximum(blk_of(t)[0] - 1, 0), blk_of(t)[1])
    fin = lambda t: blk_of(jnp.maximum(t - 1, 0))
    col = _qkv_col(d, gi)
    qkv_spec = lambda kind, which: pl.BlockSpec((BLK, 512), lambda t: (which(t)[0], col(kind, which(t)[1])))
    row_spec = lambda w, which: pl.BlockSpec((BLK, w), lambda t: which(t))
    tab_spec = lambda which: pl.BlockSpec((3, BLK, 128), lambda t: (0, *which(t)))
    return _carrier_call(
        body, (qkv, qkv, qkv, qkv, qkv, datt, lse, rho, rtab, rtab),
        out_shape=(SDS((L, d * 512), BF16),) * 3, grid=(T + 1,),
        in_specs=[qkv_spec(0, cur), qkv_spec(1, cur), qkv_spec(1, prev), qkv_spec(2, cur), qkv_spec(2, prev),
                  row_spec(512, cur), row_spec(128, cur), row_spec(128, cur), tab_spec(cur), tab_spec(fin)],
        out_specs=(row_spec(512, cur), row_spec(512, fin), row_spec(512, fin)),
        scratch_shapes=[pltpu.VMEM((BLK, 512), F32), pltpu.VMEM((BLK, 512), F32)],
        sem=("arbitrary",), name=f"attn_bwd_g{gi}", exchanges=exchanges)


def _ret_bwd(proj, rn, rstd, dyrin, states, tab, consts, dproj, exchanges=()):
    S = proj.shape[0]
    nc = S // BLK
    dmask, zeta, xi, dec = consts

    def body(q_ref, k_ref, v0_ref, v1_ref, g0_ref, g1_ref, rn_ref, rs_ref, dy_ref, st_ref, tq_ref, tk_ref,
             dm_ref, z_ref, x_ref, dec_ref, dp_prev, dp_ref, dR):
        dq_ref, dk_ref = dp_ref.at[:, 0:512], dp_ref.at[:, 512:1024]
        dv_ref, dgr_ref = dp_ref.at[:, 1024:2048], dp_ref.at[:, 2048:3072]

        @pl.when(pl.program_id(0) == 0)
        def _():
            dR[...] = jnp.zeros_like(dR)

        dobs = []
        for h in range(RET_HEADS):
            vs = slice((h % 2) * 256, (h % 2 + 1) * 256)
            os_ = slice(h * 256, (h + 1) * 256)
            gr = (g0_ref if h < 2 else g1_ref)[:, vs].astype(F32)
            sg = _sigmoid(gr)
            rn_v = rn_ref[:, os_].astype(F32)
            dyi = dy_ref[:, os_].astype(F32)
            dgr_ref[:, os_] = (dyi * rn_v * sg * (1.0 + gr * (1.0 - sg))).astype(BF16)
            drn = dyi * gr * sg
            rstd = jnp.broadcast_to(rs_ref[:, 16 * h:16 * h + 1], (BLK, 256))
            do = rstd * (drn - jnp.mean(drn, axis=-1, keepdims=True) - rn_v * jnp.mean(drn * rn_v, axis=-1, keepdims=True))
            dobs.append(do.astype(BF16))
        first = []
        for h in range(RET_HEADS):
            hs = slice(h * 128, (h + 1) * 128)
            q, k = q_ref[:, hs], k_ref[:, hs]
            v = (v0_ref if h < 2 else v1_ref)[:, (h % 2) * 256:(h % 2 + 1) * 256]
            dob, dRb = dobs[h], dR[h].astype(BF16)
            kz = (k.astype(F32) * z_ref[h]).astype(BF16)
            qx = (q.astype(F32) * x_ref[h]).astype(BF16)
            first.append((q, k, _dot_nt(q, k), _dot_nt(dob, v), _dot(kz, dRb), _dot_nt(dob, st_ref[h]),
                          _dot_nt(v, dRb), _dot_tn(qx, dob)))
        masked = [((s * dm_ref[h]).astype(BF16), (dsr * dm_ref[h]).astype(BF16))
                  for h, (_, _, s, dsr, _, _, _, _) in enumerate(first)]
        for h in range(RET_HEADS):
            hs = slice(h * 128, (h + 1) * 128)
            os_ = slice(h * 256, (h + 1) * 256)
            q, k, _, _, dv_state, dq_state, dk_state, dr_new = first[h]
            sD, dS = masked[h]
            dv_ref[:, os_] = (_dot_tn(sD, dobs[h]) + dv_state).astype(BF16)
            dq = _dot(dS, k) + dq_state * x_ref[h]
            dk = _dot_tn(dS, q) + dk_state * z_ref[h]
            dR[h] = dR[h] * dec_ref[h, 0:1, :] + dr_new
            dq_ref[:, hs] = _unrot(dq, tq_ref[0], tq_ref[1], tq_ref[2], 1).astype(BF16)
            dk_ref[:, hs] = _unrot(dk, tk_ref[0], tk_ref[1], tk_ref[2], 1).astype(BF16)

    rc = lambda c: nc - 1 - c
    cst = lambda shape: pl.BlockSpec(shape, lambda c: (0, 0, 0))
    blk = lambda j: pl.BlockSpec((BLK, 512), lambda c: (rc(c), j))
    row = lambda w: pl.BlockSpec((BLK, w), lambda c: (rc(c), 0))
    (dproj,), xres = _carrier_call(
        body, (proj, proj, proj, proj, proj, proj, rn, rstd, dyrin, states, tab, tab, dmask, zeta, xi, dec, dproj),
        out_shape=(SDS((S, PROJ_W), BF16),), grid=(nc,),
        in_specs=[blk(QR_B), blk(KR_B), blk(11), blk(12), blk(13), blk(14), row(1024), row(128), row(1024),
                  pl.BlockSpec((RET_HEADS, None, BLK, 256), lambda c: (0, rc(c), 0, 0)),
                  pl.BlockSpec((None, 3, BLK, 128), lambda c: (1, 0, rc(c), 0)),
                  pl.BlockSpec((None, 3, BLK, 128), lambda c: (2, 0, rc(c), 0)),
                  cst((RET_HEADS, BLK, BLK)), cst((RET_HEADS, BLK, 128)), cst((RET_HEADS, BLK, 128)), cst((RET_HEADS, 8, 256)),
                  ANY],
        out_specs=(pl.BlockSpec((pl.Element(BLK), pl.Element(6 * COLB)), lambda c: (rc(c) * BLK, QR_B * COLB)),),
        scratch_shapes=[pltpu.VMEM((RET_HEADS, BLK, 256), F32)],
        sem=("arbitrary",), name="ret_bwd", exchanges=exchanges, in_out_aliases={16: 0})
    return dproj, xres


def _wgrad_in_half(ht, dproj, sidx, kept, exchanges=()):
    S = dproj.shape[0]
    tk = 2048
    half = (lambda sx: sx[4]) if kept else (lambda sx: 1 - sx[4])

    def body(a_ref, b_ref, o_ref):
        @pl.when(pl.program_id(1) == 0)
        def _():
            o_ref[...] = jnp.zeros_like(o_ref)

        o_ref[...] += _dot(a_ref[...], b_ref[...])

    (g,), xres = _carrier_call(
        body, (ht, dproj), out_shape=(SDS((D_MODEL // 2, PROJ_W), F32),), grid=(N_SHARD, S // tk),
        in_specs=[pl.BlockSpec((D_MODEL // 2, tk), lambda s, k, sx: (half(sx), k)),
                  pl.BlockSpec((tk, W_IN_S), lambda s, k, sx: (k, s))],
        out_specs=(pl.BlockSpec((D_MODEL // 2, W_IN_S), lambda s, k, sx: (0, s)),),
        sem=("parallel", "arbitrary"), name="wgrad_in_kept" if kept else "wgrad_in_sent", exchanges=exchanges,
        prefetch=sidx)
    return g, xres


def _in_proj_bwd(dproj, w_in, x, g1, dx1, exchanges=()):
    S = x.shape[0]
    tm = 1024

    def body(d_ref, w_ref, x_ref, g_ref, dx1_ref, dx_ref, dgn_ref, acc):
        i, s = pl.program_id(0), pl.program_id(1)

        @pl.when(s == 0)
        def _():
            acc[...] = jnp.zeros_like(acc)

        @pl.when((i == 0) & (s == 0))
        def _():
            dgn_ref[...] = jnp.zeros_like(dgn_ref)

        acc[...] += _dot_nt(d_ref[...], w_ref[...])

        @pl.when(s == N_SHARD - 1)
        def _():
            xv = x_ref[...]
            r = lax.rsqrt(jnp.mean(xv * xv, axis=-1, keepdims=True) + NORM_EPS)
            xh = xv * r
            dh = acc[...]
            dgn_ref[...] += jnp.sum(dh * xh, axis=0, keepdims=True)
            dxh = dh * g_ref[...]
            dx_ref[...] = dx1_ref[...] + r * (dxh - xh * jnp.mean(dxh * xh, axis=-1, keepdims=True))

    row = pl.BlockSpec((tm, D_MODEL), lambda i, s: (i, 0))
    vec = pl.BlockSpec((1, D_MODEL), lambda i, s: (0, 0))
    (gx, dg), xres = _carrier_call(
        body, (dproj, w_in, x, g1, dx1),
        out_shape=(SDS((S, D_MODEL), F32), SDS((1, D_MODEL), F32)), grid=(S // tm, N_SHARD),
        in_specs=[pl.BlockSpec((tm, W_IN_S), lambda i, s: (i, s)),
                  pl.BlockSpec((D_MODEL, W_IN_S), lambda i, s: (0, s)), row, vec, row],
        out_specs=(row, vec), scratch_shapes=[pltpu.VMEM((tm, D_MODEL), F32)],
        sem=("arbitrary", "arbitrary"), name="in_proj_bwd", exchanges=exchanges)
    return gx, dg, xres


def _step(x, tgt, g1, g2, g3, comm):
    S = x.shape[0]
    tab_np = _tables(S)
    tab = jnp.asarray(tab_np)
    consts = _ret_consts()

    (h, ht, *casts), xres = _rms_fwd(x, g1, comm.to_cast(), comm.carry("rms_fwd"))
    comm.cast_done(casts)
    comm.took("rms_fwd", xres)
    w_in = comm.weight(0)
    proj, xres = _in_proj(h, w_in, tab, comm.carry("in_proj"))
    comm.took("in_proj", xres)
    qkvs, o_parts, lse_parts = [], [], []
    for gi, d in enumerate(DILATIONS):
        qkv = proj if d == 1 else _qkv_to_sub(proj, d, gi)
        (o_g, lse_g), xres = _attn_fwd(qkv, d, gi, comm.carry(f"attn_fwd_g{gi}"))
        comm.took(f"attn_fwd_g{gi}", xres)
        qkvs.append(qkv)
        o_parts.append(o_g)
        lse_parts.append(lse_g)
    att, lse_views = _attn_merge(o_parts, lse_parts)
    (yrin, rn, rstd, states), xres = _ret_fwd(proj, consts, comm.carry("ret_fwd"))
    comm.took("ret_fwd", xres)
    wa, wr, wo = comm.weight(1), comm.weight(2), comm.weight(3)
    (merged, ya, yr, x1, h2), xres = _mix_out(att, yrin, proj, wa, wr, wo, x, g2, comm.carry("mix_out"))
    comm.took("mix_out", xres)
    wg, wu = comm.weight(4), comm.weight(5)
    (gte, up, act), xres = _ffn_up(h2, wg, wu, comm.carry("ffn_up"))
    comm.took("ffn_up", xres)
    wd = comm.weight(6)
    dx2, dx2b, dg3, loss_p = _ffn_down_loss(act, wd, x1, g3, tgt)

    dgte, dup, dx1, dx1b, dg2 = _ffn_bwd(dx2b, dx2, wd, wg, wu, gte, up, x1, g2)
    tok3 = lambda w: (lambda tk: pl.BlockSpec((None, tk, w), lambda p, k: (p, k, 0)))
    tok2 = lambda w: (lambda tk: pl.BlockSpec((tk, w), lambda p, k: (k, 0)))
    g_d = _wgrad("wgrad_down", act, dx2b, tok3(HID_S), tok2(D_MODEL), (N_SHARD, HID_S, D_MODEL),
                 pl.BlockSpec((None, HID_S, D_MODEL), lambda p, k: (p, 0, 0)), N_SHARD, S)
    g_g = _wgrad("wgrad_gate", dgte, h2, tok3(HID_S), tok2(D_MODEL), (N_SHARD, HID_S, D_MODEL),
                 pl.BlockSpec((None, HID_S, D_MODEL), lambda p, k: (p, 0, 0)), N_SHARD, S)
    g_u = _wgrad("wgrad_up", dup, h2, tok3(HID_S), tok2(D_MODEL), (N_SHARD, HID_S, D_MODEL),
                 pl.BlockSpec((None, HID_S, D_MODEL), lambda p, k: (p, 0, 0)), N_SHARD, S)
    comm.grads({4: g_g, 5: g_u, 6: g_d})
    (dya, dyr, dproj, dyrin, datt_views, rho_views), xres = _mix_bwd(dx1b, wo, proj, ya, yr, wa, wr, att,
                                                                       comm.carry("mix_bwd"))
    comm.took("mix_bwd", xres)
    colblk = lambda w: (lambda tk: pl.BlockSpec((tk, w), lambda p, k: (k, p)))
    g_o = _wgrad("wgrad_out", merged, dx1b, colblk(256), tok2(D_MODEL), (D_MODEL, D_MODEL),
                 pl.BlockSpec((256, D_MODEL), lambda p, k: (p, 0)), 4, S)
    g_a = _wgrad("wgrad_attn", att, dya, tok2(512), colblk(512), (512, D_MODEL),
                 pl.BlockSpec((512, 512), lambda p, k: (0, p)), 2, S)
    g_r = _wgrad("wgrad_ret", yrin, dyr, colblk(256), tok2(D_MODEL), (D_MODEL, D_MODEL),
                 pl.BlockSpec((256, D_MODEL), lambda p, k: (p, 0)), 4, S)
    comm.grads({1: g_a, 2: g_r.reshape(N_SHARD, 256, D_MODEL), 3: g_o.reshape(N_SHARD, 256, D_MODEL)})
    dproj, xres = _ret_bwd(proj, rn, rstd, dyrin, states, tab, consts, dproj, comm.carry("ret_bwd"))
    comm.took("ret_bwd", xres)
    dqs, dks, dvs = [], [], []
    for gi, d in enumerate(DILATIONS):
        rtab = jnp.asarray(tab_np[0].reshape(3, S // d, d * 128))
        (dq, dk, dv), xres = _attn_bwd(qkvs[gi], datt_views[gi], lse_views[gi], rho_views[gi], rtab, d, gi,
                                       comm.carry(f"attn_bwd_g{gi}"))
        comm.took(f"attn_bwd_g{gi}", xres)
        dqs.append(dq)
        dks.append(dk)
        dvs.append(dv)
    dproj = _assemble_dproj((dqs, dks, dvs), dproj)
    g_sent, xres = _wgrad_in_half(ht, dproj, comm.sidx, False, comm.carry("wgrad_in_sent"))
    comm.took("wgrad_in_sent", xres)
    comm.grads({"in_sent": g_sent})
    g_kept, xres = _wgrad_in_half(ht, dproj, comm.sidx, True, comm.carry("wgrad_in_kept"))
    comm.grads({"in_kept": g_kept})
    comm.took("wgrad_in_kept", xres)
    grad_x, dg1, xres = _in_proj_bwd(dproj, w_in, x, g1, dx1, comm.carry("in_proj_bwd"))
    comm.took("in_proj_bwd", xres)
    return loss_p, grad_x, (dg1, dg2, dg3)


W_KINDS = ("col", "col", "lead", "lead", "lead", "lead", "lead")
W_SHARD = ((1024, W_IN_S), (512, 256), (256, 1024), (256, 1024), (HID_S, 1024), (HID_S, 1024), (HID_S, 1024))
W_TRANSPOSED = (4, 5)
N_W = len(W_KINDS)


def _full_shape(wi):
    R, C = W_SHARD[wi]
    return (R, N_SHARD * C) if W_KINDS[wi] == "col" else (N_SHARD, R, C)


def _view(ref, wi, s, half):
    R, C = W_SHARD[wi]
    rows = pl.ds(half * (R // 2), R // 2)
    if W_KINDS[wi] == "col":
        return ref.at[rows, pl.ds(pl.multiple_of(s * C, 128), C)]
    return ref.at[s, rows, :]


def _mesh_pos():
    x, y, c = lax.axis_index("x"), lax.axis_index("y"), lax.axis_index("c")
    chips = [(1 - x, y), (x, 1 - y), (1 - x, 1 - y)]
    return x, y, c, chips


def _cast_bf16(a):
    R, C = a.shape
    tr = R // 2 if R % 32 == 0 else R

    def body(a_ref, o_ref):
        o_ref[...] = a_ref[...].astype(BF16)

    spec = pl.BlockSpec((tr, C), lambda i: (i, 0))
    return pl.pallas_call(body, out_shape=SDS((R, C), BF16), grid=(R // tr,), in_specs=[spec], out_specs=spec,
                          compiler_params=_cparams("parallel"), name=f"cast_{R}x{C}")(a)


def _remote(send, recv, k, src, dst, to):
    return pltpu.make_async_remote_copy(src_ref=src, dst_ref=dst, send_sem=send.at[k], recv_sem=recv.at[k],
                                        device_id=to, device_id_type=MESH)


def _ex_gather_ring(wis, shards):
    n = len(wis)

    def build(sh, full, send, recv, loc):
        x, y, c, _ = _mesh_pos()
        s_me, sib = 2 * x + y, (x, y, 1 - c)
        xn, yn = (1 - x, y), (x, 1 - y)
        flip = lambda a, b: a + b - 2 * a * b
        via = (flip(x, 1 - c), flip(y, c))
        onto = (flip(x, c), flip(y, 1 - c))
        shard_of = lambda chip: 2 * chip[0] + chip[1]
        starts, waits, sent = [], [], []
        for i, wi in enumerate(wis):
            Rh = W_SHARD[wi][0] // 2
            for hf in range(2):
                cp = pltpu.make_async_copy(sh[i].at[pl.ds(hf * Rh, Rh), :], _view(full[i], wi, s_me, hf), loc.at[2 * i + hf])
                starts.append(cp)
                sent.append(cp.wait)
            for j, chip in enumerate((xn, yn)):
                cp = _remote(send, recv, 6 * i + j, sh[i].at[pl.ds(c * Rh, Rh), :], _view(full[i], wi, s_me, c), (*chip, c))
                starts.append(cp)
                sent.append(cp.wait_send)

        def pass_to_sibling(i, wi, k, s):
            mine = _view(full[i], wi, s, c)
            fw = _remote(send, recv, 6 * i + k, mine, mine, sib)
            waits.append(fw.start)
            sent.append(fw.wait_send)

        for i, wi in enumerate(wis):
            for j, chip in enumerate((xn, yn)):
                land = _view(full[i], wi, shard_of(chip), c)
                waits.append(_remote(send, recv, 6 * i + j, land, land, (*chip, c)).wait_recv)
                pass_to_sibling(i, wi, 3 + j, shard_of(chip))
            relay = _view(full[i], wi, shard_of(via), c)
            fw = _remote(send, recv, 6 * i + 2, relay, relay, (*onto, c))
            waits.append(fw.start)
            sent.append(fw.wait_send)
        s_diag = 2 * (1 - x) + (1 - y)
        for i, wi in enumerate(wis):
            land = _view(full[i], wi, s_diag, c)
            waits.append(_remote(send, recv, 6 * i + 2, land, land, (*onto, c)).wait_recv)
            pass_to_sibling(i, wi, 5, s_diag)
        for i, wi in enumerate(wis):
            for k, s in ((3, shard_of(xn)), (4, shard_of(yn)), (5, s_diag)):
                land = _view(full[i], wi, s, 1 - c)
                waits.append(_remote(send, recv, 6 * i + k, land, land, sib).wait_recv)
        return starts, waits + sent

    return _Exchange(shards, [SDS(_full_shape(wi), BF16) for wi in wis], {}, 6 * n, 2 * n, build)


def _ex_gather_ici(wis, shards, then_d2d=False):
    n = len(wis)

    def build(ins, outs, send, recv, loc):
        x, y, c, chips = _mesh_pos()
        s_me, sib = 2 * x + y, (x, y, 1 - c)
        starts, waits, after = [], [], []
        for i, wi in enumerate(wis):
            Rh = W_SHARD[wi][0] // 2
            for hf in range(2):
                cp = pltpu.make_async_copy(ins[i].at[pl.ds(hf * Rh, Rh), :], _view(outs[i], wi, s_me, hf), loc.at[2 * i + hf])
                starts.append(cp)
                waits.append(cp.wait)
            for j, chip in enumerate(chips):
                cp = _remote(send, recv, 3 * i + j, ins[i].at[pl.ds(c * Rh, Rh), :], _view(outs[i], wi, s_me, c), (*chip, c))
                land = _view(outs[i], wi, 2 * chip[0] + chip[1], c)
                starts.append(cp)
                waits += [cp.wait_send, _remote(send, recv, 3 * i + j, land, land, (*chip, c)).wait_recv]
                if then_d2d:
                    theirs = _view(outs[i], wi, 2 * chip[0] + chip[1], 1 - c)
                    fw = _remote(send, recv, 3 * n + 3 * i + j, land, land, sib)
                    waits.append(fw.start)
                    after += [fw.wait_send, _remote(send, recv, 3 * n + 3 * i + j, theirs, theirs, sib).wait_recv]
        return starts, waits + after

    return _Exchange(shards, [SDS(_full_shape(wi), BF16) for wi in wis], {}, (6 if then_d2d else 3) * n, 2 * n, build)


def _ex_gather_d2d(wis, fulls):
    def build(ins, outs, send, recv, loc):
        x, y, c, chips = _mesh_pos()
        sib = (x, y, 1 - c)
        starts, waits = [], []
        for i, wi in enumerate(wis):
            for j, chip in enumerate(chips):
                mine = _view(outs[i], wi, 2 * chip[0] + chip[1], c)
                theirs = _view(outs[i], wi, 2 * chip[0] + chip[1], 1 - c)
                cp = _remote(send, recv, 3 * i + j, mine, mine, sib)
                starts.append(cp)
                waits += [cp.wait_send, _remote(send, recv, 3 * i + j, theirs, theirs, sib).wait_recv]
        return starts, waits

    return _Exchange(fulls, [SDS(f.shape, BF16) for f in fulls], {i: i for i in range(len(wis))}, 3 * len(wis), 0, build)


def _half_shape(wi):
    R, C = W_SHARD[wi]
    return (R // 2, N_SHARD * C) if W_KINDS[wi] == "col" else (N_SHARD, R // 2, C)


def _ex_pair(wis, grads):
    def build(ins, outs, send, recv, loc):
        x, y, c, _ = _mesh_pos()
        starts, waits = [], []
        for i, wi in enumerate(wis):
            Rh = W_SHARD[wi][0] // 2
            rows = pl.ds((1 - c) * Rh, Rh)
            if tuple(ins[i].shape) == _half_shape(wi):
                src = ins[i]
            else:
                src = ins[i].at[rows, :] if W_KINDS[wi] == "col" else ins[i].at[:, rows, :]
            cp = _remote(send, recv, i, src, outs[i], (x, y, 1 - c))
            starts.append(cp)
            waits.append(cp.wait)
        return starts, waits

    return _Exchange(grads, [SDS(_half_shape(wi), F32) for wi in wis], {}, len(wis), 0, build)


def _ex_chip(wis, pbs):
    def build(ins, outs, send, recv, loc):
        x, y, c, chips = _mesh_pos()
        starts, waits = [], []
        for i, wi in enumerate(wis):
            for j, chip in enumerate(chips):
                cp = _remote(send, recv, 3 * i + j, ins[i].at[j], outs[i].at[j], (*chip, c))
                starts.append(cp)
                waits.append(cp.wait)
        return starts, waits

    shapes = [SDS((3, W_SHARD[wi][0] // 2, W_SHARD[wi][1]), BF16) for wi in wis]
    return _Exchange(pbs, shapes, {}, 3 * len(wis), 0, build)


def _ex_share(wis, halves):
    def build(ins, outs, send, recv, loc):
        x, y, c, _ = _mesh_pos()
        sib = (x, y, 1 - c)
        starts, waits = [], []
        for i, wi in enumerate(wis):
            cp = _remote(send, recv, i, outs[i].at[c], outs[i].at[c], sib)
            starts.append(cp)
            waits += [cp.wait_send, _remote(send, recv, i, outs[i].at[1 - c], outs[i].at[1 - c], sib).wait_recv]
        return starts, waits

    return _Exchange(halves, [SDS(h.shape, F32) for h in halves], {i: i for i in range(len(wis))}, len(wis), 0, build)


def _row_tile(rh, C):
    best = 16
    for t in range(16, rh + 1, 16):
        if rh % t == 0 and t * C * 4 <= (3 << 19):
            best = t
    return best


def _pair_sum(wi, g, ra, sidx):
    R, C = W_SHARD[wi]
    Rh = R // 2
    tr = _row_tile(Rh, C)
    nt = Rh // tr
    off = 0 if tuple(g.shape) == _half_shape(wi) else nt
    col = W_KINDS[wi] == "col"

    def body(sidx_ref, *refs):
        gs, rs = refs[:4], refs[4:8]
        own_ref, pb_ref = refs[8:]
        own_ref[...] = gs[0][...] + rs[0][...]
        for j in range(3):
            pb_ref[j] = (gs[1 + j][...] + rs[1 + j][...]).astype(BF16)

    def gspec(slot):
        if col:
            return pl.BlockSpec((tr, C), lambda i, sx: (sx[4] * off + i, sx[slot]))
        return pl.BlockSpec((None, tr, C), lambda i, sx: (sx[slot], sx[4] * off + i, 0))

    def rspec(slot):
        if col:
            return pl.BlockSpec((tr, C), lambda i, sx: (i, sx[slot]))
        return pl.BlockSpec((None, tr, C), lambda i, sx: (sx[slot], i, 0))

    return pl.pallas_call(
        body, out_shape=(SDS((Rh, C), F32), SDS((3, Rh, C), BF16)),
        grid_spec=pltpu.PrefetchScalarGridSpec(
            num_scalar_prefetch=1, grid=(nt,),
            in_specs=[gspec(k) for k in range(4)] + [rspec(k) for k in range(4)],
            out_specs=(pl.BlockSpec((tr, C), lambda i, sx: (i, 0)), pl.BlockSpec((3, tr, C), lambda i, sx: (0, i, 0)))),
        compiler_params=_cparams("arbitrary"), name=f"pair_sum_w{wi}")(sidx, g, g, g, g, ra, ra, ra, ra)


def _chip_sum(wi, own, rb, sidx):
    R, C = W_SHARD[wi]
    Rh = R // 2
    tr = _row_tile(Rh, C)

    def body(sidx_ref, own_ref, rb_ref, o_ref):
        o_ref[...] = ((own_ref[...] + rb_ref[0].astype(F32)) + rb_ref[1].astype(F32)) + rb_ref[2].astype(F32)

    return pl.pallas_call(
        body, out_shape=SDS((2, Rh, C), F32),
        grid_spec=pltpu.PrefetchScalarGridSpec(
            num_scalar_prefetch=1, grid=(Rh // tr,),
            in_specs=[pl.BlockSpec((tr, C), lambda i, sx: (i, 0)), pl.BlockSpec((3, tr, C), lambda i, sx: (0, i, 0))],
            out_specs=pl.BlockSpec((None, tr, C), lambda i, sx: (sx[4], i, 0))),
        compiler_params=_cparams("arbitrary"), name=f"chip_sum_w{wi}")(sidx, own, rb)


def _gain_allgather(blk, ex):
    m_per, n = blk.shape
    n_in, n_out = len(ex.ins), len(ex.out_shapes)

    def body(x_ref, *rest):
        xin, out_ref, xout = rest[:n_in], rest[n_in], rest[n_in + 1:n_in + 1 + n_out]
        send_sems, recv_sems, local_sem = rest[n_in + 1 + n_out:n_in + 4 + n_out]
        ex_starts, ex_waits = ex.build(xin, xout, *rest[n_in + 4 + n_out:])
        for cp in ex_starts:
            cp.start()
        x, y, c, chips = _mesh_pos()
        me, sibling = (x, y, c), (x, y, 1 - c)

        def rows(px, py, pc):
            return out_ref.at[pl.ds((4 * px + 2 * py + pc) * m_per, m_per), :]

        def copy(k, block, to, src=None):
            return pltpu.make_async_remote_copy(
                src_ref=rows(*block) if src is None else src, dst_ref=rows(*block),
                send_sem=send_sems.at[k], recv_sem=recv_sems.at[k], device_id=to, device_id_type=MESH)

        mine = pltpu.make_async_copy(x_ref, rows(*me), local_sem)
        mine.start()
        first = [copy(0, me, sibling, src=x_ref)]
        first += [copy(1 + j, me, (*chip, c), src=x_ref) for j, chip in enumerate(chips)]
        for cp in first:
            cp.start()
        passed = [copy(4 + j, (*chip, c), sibling) for j, chip in enumerate(chips)]
        for j, chip in enumerate(chips):
            copy(1 + j, (*chip, c), me).wait_recv()
            passed[j].start()
        copy(0, sibling, me).wait_recv()
        for j, chip in enumerate(chips):
            copy(4 + j, (*chip, 1 - c), me).wait_recv()
        for cp in first + passed:
            cp.wait_send()
        mine.wait()
        for w in ex_waits:
            w()

    vm = pl.BlockSpec(memory_space=pltpu.VMEM)
    res = pl.pallas_call(
        body, out_shape=(SDS((8 * m_per, n), blk.dtype), *ex.out_shapes),
        in_specs=[vm] + [ANY] * n_in, out_specs=(vm, *[ANY] * n_out),
        input_output_aliases={1 + a: 1 + o for a, o in ex.aliases.items()},
        scratch_shapes=[pltpu.SemaphoreType.DMA((7,)), pltpu.SemaphoreType.DMA((7,)), pltpu.SemaphoreType.DMA] + ex.sems(),
        name="gain_allgather")(blk, *ex.ins)
    return res[0], tuple(res[1:])


def _adam_math(w, g, m, v):
    mn = ADAM_B1 * m + (1.0 - ADAM_B1) * g
    vn = ADAM_B2 * v + (1.0 - ADAM_B2) * (g * g)
    mh = mn / (1.0 - ADAM_B1 ** ADAM_STEP)
    vh = vn / (1.0 - ADAM_B2 ** ADAM_STEP)
    return -ADAM_LR * (mh / (jnp.sqrt(vh) + ADAM_EPS) + ADAM_WD * w), mn, vn


def _adamw(name, ws, gs, ms, vs):
    n, steps = len(ws), 8

    def body(*refs):
        for k in range(n):
            w_ref, g_ref, m_ref, v_ref = refs[4 * k:4 * k + 4]
            go_ref, d_ref, mn_ref, vn_ref = refs[4 * n + 4 * k:4 * n + 4 * k + 4]
            g = g_ref[...]
            go_ref[...] = g
            d_ref[...], mn_ref[...], vn_ref[...] = _adam_math(w_ref[...], g, m_ref[...], v_ref[...])

    specs = [pl.BlockSpec((w.shape[0] // steps, w.shape[1]), lambda i: (i, 0)) for w in ws for _ in range(4)]
    res = pl.pallas_call(
        body, out_shape=tuple(SDS(w.shape, F32) for w in ws for _ in range(4)), grid=(steps,),
        in_specs=specs, out_specs=tuple(specs), compiler_params=_cparams("parallel"),
        name=name)(*[a for k in range(n) for a in (ws[k], gs[k], ms[k], vs[k])])
    return [tuple(res[4 * k:4 * k + 4]) for k in range(n)]


def _gain_update(gathered, w, m, v):
    def body(ga_ref, w_ref, m_ref, v_ref, g_ref, d_ref, mn_ref, vn_ref):
        g = ga_ref[0:8, :]
        for dev in range(1, 8):
            g = g + ga_ref[8 * dev:8 * dev + 8, :]
        g_ref[...] = g
        d_ref[...], mn_ref[...], vn_ref[...] = _adam_math(w_ref[...], g, m_ref[...], v_ref[...])

    return pl.pallas_call(body, out_shape=(SDS((8, 1024), F32),) * 4, name="gain_update")(gathered, w, m, v)


GROUP_FFN, GROUP_MIX, GROUP_IN = (4, 5, 6), (1, 2, 3), (0,)
REST = GROUP_MIX + GROUP_FFN


class _MeshComm:
    SCHEDULE = {
        "rms_fwd": [("ring", GROUP_IN)],
        "in_proj": [("ici", (1, 2, 3, 4))],
        "ret_fwd": [("d2d", (1, 2, 3, 4)), ("ici", (5,))],
        "mix_out": [("d2d", (5,))],
        "ffn_up": [("both", (6,))],
        "mix_bwd": [("pair", GROUP_FFN)],
        "ret_bwd": [("pair", GROUP_MIX), ("chip", (4,))],
        "attn_bwd_g0": [("chip", (5,))],
        "attn_bwd_g1": [("chip", (6,))],
        "attn_bwd_g2": [("chip", GROUP_MIX)],
        "wgrad_in_kept": [("pair", GROUP_IN), ("share", GROUP_FFN + GROUP_MIX)],
        "in_proj_bwd": [("chip", GROUP_IN)],
    }

    def __init__(self, w_in_shard, rest_f32):
        xi, yi, ci = lax.axis_index("x"), lax.axis_index("y"), lax.axis_index("c")
        self.sidx = jnp.stack([2 * xi + yi, 2 * (1 - xi) + yi, 2 * xi + (1 - yi), 2 * (1 - xi) + (1 - yi), ci]).astype(jnp.int32)
        self.shards, self.rest_f32, self.full = {0: w_in_shard}, list(rest_f32), {}
        self.g, self.own, self.pb, self.half, self.red = {}, {}, {}, {}, {}

    def to_cast(self):
        return self.rest_f32

    def cast_done(self, casts):
        self.shards.update(zip(REST, casts))

    def weight(self, wi):
        return self.full[wi].reshape(D_MODEL, D_MODEL) if wi in (2, 3) else self.full[wi]

    def grads(self, by_wi):
        self.g.update(by_wi)

    def _exchange(self, stage, wis):
        pick = lambda table: [table[wi] for wi in wis]
        if stage == "ring":
            return _ex_gather_ring(wis, pick(self.shards))
        if stage == "ici":
            return _ex_gather_ici(wis, pick(self.shards))
        if stage == "both":
            return _ex_gather_ici(wis, pick(self.shards), then_d2d=True)
        if stage == "d2d":
            return _ex_gather_d2d(wis, pick(self.full))
        if stage == "pair":
            return _ex_pair(wis, [self.g["in_sent"] if wi == 0 else self.g[wi] for wi in wis])
        if stage == "chip":
            return _ex_chip(wis, pick(self.pb))
        return _ex_share(wis, pick(self.half))

    def _landed(self, stage, wis, res):
        for wi, r in zip(wis, res):
            if stage in ("ring", "ici", "d2d", "both"):
                self.full[wi] = r
            elif stage == "pair":
                self.own[wi], self.pb[wi] = _pair_sum(wi, self.g["in_kept"] if wi == 0 else self.g[wi], r, self.sidx)
            elif stage == "chip":
                self.half[wi] = _chip_sum(wi, self.own[wi], r, self.sidx)
            else:
                self.red[wi] = r

    def carry(self, point):
        return [self._exchange(stage, wis) for stage, wis in self.SCHEDULE.get(point, ())]

    def took(self, point, xres):
        for (stage, wis), res in zip(self.SCHEDULE.get(point, ()), xres):
            self._landed(stage, wis, res)

    def last_share(self):
        return self._exchange("share", GROUP_IN)

    def reduced(self, last_shared):
        self._landed("share", GROUP_IN, last_shared)
        return [self.red[wi] for wi in range(N_W)]


def kernel(x, norm_mix_g, w_in, w_out_attn, w_out_ret, w_out, norm_ffn_g, w_ffn_gate, w_ffn_up, w_ffn_down, norm_final_g, loss_target, m_norm_mix_g, m_w_in, m_w_out_attn, m_w_out_ret, m_w_out, m_norm_ffn_g, m_w_ffn_gate, m_w_ffn_up, m_w_ffn_down, m_norm_final_g, v_norm_mix_g, v_w_in, v_w_out_attn, v_w_out_ret, v_w_out, v_norm_ffn_g, v_w_ffn_gate, v_w_ffn_up, v_w_ffn_down, v_norm_final_g):
    ws = (w_in, w_out_attn, w_out_ret, w_out, w_ffn_gate, w_ffn_up, w_ffn_down)
    ms = (m_w_in, m_w_out_attn, m_w_out_ret, m_w_out, m_w_ffn_gate, m_w_ffn_up, m_w_ffn_down)
    vs = (v_w_in, v_w_out_attn, v_w_out_ret, v_w_out, v_w_ffn_gate, v_w_ffn_up, v_w_ffn_down)

    def shard2d(a, wi):
        return jnp.swapaxes(a[0], 0, 1) if wi in W_TRANSPOSED else a.reshape(W_SHARD[wi])

    def as_given(a2d, wi):
        return jnp.swapaxes(a2d, 0, 1)[None] if wi in W_TRANSPOSED else a2d.reshape(ws[wi].shape)

    comm = _MeshComm(_cast_bf16(shard2d(ws[0], 0)), [shard2d(ws[wi], wi) for wi in REST])
    g3 = norm_final_g.reshape(1, D_MODEL)
    loss_p, grad_x, gain_g = _step(x[0], loss_target[0], norm_mix_g, norm_ffn_g, g3, comm)

    pad8 = lambda rows: jnp.concatenate([r.reshape(1, D_MODEL) for r in rows]
                                        + [jnp.zeros((8 - len(rows), D_MODEL), F32)], axis=0)
    gathered, shared = _gain_allgather(pad8((*gain_g, jnp.tile(loss_p[0:1], (1, D_MODEL // 128)))), comm.last_share())
    gred = comm.reduced(shared)

    def adam(name, wis):
        two_d = lambda arrs: [shard2d(arrs[wi], wi) for wi in wis]
        return _adamw(name, two_d(ws), [gred[wi].reshape(W_SHARD[wi]) for wi in wis], two_d(ms), two_d(vs))

    updates = dict(zip(REST + GROUP_IN, adam("adamw_rest", REST) + adam("adamw_w_in", GROUP_IN)))
    outs_g, outs_d, outs_m, outs_v = ([as_given(updates[wi][k], wi) for wi in range(N_W)] for k in range(4))

    gg, gd, gm, gv = _gain_update(gathered, pad8((norm_mix_g, norm_ffn_g, norm_final_g)),
                                  pad8((m_norm_mix_g, m_norm_ffn_g, m_norm_final_g)),
                                  pad8((v_norm_mix_g, v_norm_ffn_g, v_norm_final_g)))
    loss = gg[3, 0]

    def assemble(gain_rows, wlist):
        return (gain_rows[0:1], wlist[0], wlist[1], wlist[2], wlist[3], gain_rows[1:2],
                wlist[4], wlist[5], wlist[6], gain_rows[2])

    return (loss, grad_x[None], *assemble(gg, outs_g), *assemble(gd, outs_d), *assemble(gm, outs_m), *assemble(gv, outs_v))
```

```python
import functools

import numpy as np
import jax
import jax.numpy as jnp
from jax import lax
from jax.experimental import pallas as pl
from jax.experimental.pallas import tpu as pltpu

F32, BF16 = jnp.float32, jnp.bfloat16
SDS = jax.ShapeDtypeStruct
MESH = pl.DeviceIdType.MESH

D_MODEL = 1024
PROJ_W = 9728
COLB = 512
N_COLB = PROJ_W // COLB
QA_B, KA_B, VA_B = 0, 3, 6
QR_B, KR_B = 9, 10
FFN_HID = 2816
N_SHARD = 4
HID_S = FFN_HID // N_SHARD
W_IN_S = PROJ_W // N_SHARD
DILATIONS = (1, 4, 16)
BLK = 128
RET_HEADS = 4
ROPE_THETA = 10000.0
NORM_EPS = 1e-6
ADAM_LR, ADAM_B1, ADAM_B2, ADAM_EPS, ADAM_WD, ADAM_STEP = 0.001, 0.9, 0.999, 1e-08, 0.01, 10
VMEM_LIMIT = 56 << 20


def _cparams(*sem):
    return pltpu.CompilerParams(dimension_semantics=sem or None, vmem_limit_bytes=VMEM_LIMIT)


def _dot(a, b):
    return jnp.dot(a, b, preferred_element_type=F32)


def _dot_nt(a, b):
    return lax.dot_general(a, b, (((1,), (1,)), ((), ())), preferred_element_type=F32)


def _dot_tn(a, b):
    return lax.dot_general(a, b, (((0,), (0,)), ((), ())), preferred_element_type=F32)


def _row_pieces(tm, sub=512):
    return [slice(i, i + sub) for i in range(0, tm, sub)]


def _sigmoid(z):
    return 0.5 * jnp.tanh(0.5 * z) + 0.5


ANY = pl.BlockSpec(memory_space=pl.ANY)


class _Exchange:
    def __init__(self, ins, out_shapes, aliases, n_sem, n_loc, build):
        self.ins, self.out_shapes, self.aliases = list(ins), list(out_shapes), dict(aliases)
        self.n_sem, self.n_loc, self.build = n_sem, n_loc, build

    def sems(self):
        return [pltpu.SemaphoreType.DMA((self.n_sem,)), pltpu.SemaphoreType.DMA((self.n_sem,)),
                pltpu.SemaphoreType.DMA((max(self.n_loc, 1),))]


def _carrier_call(body, args, *, out_shape, grid, in_specs, out_specs, scratch_shapes=(), sem, name, exchanges=(),
                  prefetch=None, in_out_aliases=None):
    out_shape, out_specs = tuple(out_shape), tuple(out_specs)
    n_in, n_out, n_scr = len(args), len(out_shape), len(scratch_shapes)
    n_pre = 0 if prefetch is None else 1
    x_args, x_outs, x_scr, spans = [], [], [], []
    aliases = {n_pre + a: o for a, o in (in_out_aliases or {}).items()}
    for ex in exchanges:
        i0, o0 = len(x_args), len(x_outs)
        for a, o in ex.aliases.items():
            aliases[n_pre + n_in + i0 + a] = n_out + o0 + o
        x_args += ex.ins
        x_outs += ex.out_shapes
        x_scr += ex.sems()
        spans.append((i0, len(ex.ins), o0, len(ex.out_shapes)))
    nx_in, nx_out = len(x_args), len(x_outs)

    def wrapped(*refs):
        refs = refs[n_pre:]
        ins, xin = refs[:n_in], refs[n_in:n_in + nx_in]
        o_base = n_in + nx_in
        outs, xout = refs[o_base:o_base + n_out], refs[o_base + n_out:o_base + n_out + nx_out]
        s_base = o_base + n_out + nx_out
        scr, xs = refs[s_base:s_base + n_scr], refs[s_base + n_scr:]

        def built(e):
            i0, ni, o0, no = spans[e]
            return exchanges[e].build(xin[i0:i0 + ni], xout[o0:o0 + no], *xs[3 * e:3 * e + 3])

        if exchanges:
            first = functools.reduce(jnp.logical_and, [pl.program_id(k) == 0 for k in range(len(grid))])
            last = functools.reduce(jnp.logical_and, [pl.program_id(k) == grid[k] - 1 for k in range(len(grid))])

            @pl.when(first)
            def _():
                for e in range(len(exchanges)):
                    for cp in built(e)[0]:
                        cp.start()

        body(*ins, *outs, *scr)

        if exchanges:
            @pl.when(last)
            def _():
                for e in range(len(exchanges)):
                    for w in built(e)[1]:
                        w()

    all_in, all_out = list(in_specs) + [ANY] * nx_in, out_specs + tuple([ANY] * nx_out)
    all_scr = list(scratch_shapes) + x_scr
    cparams = _cparams(*(sem if not exchanges else ("arbitrary",) * len(grid)))
    if prefetch is None:
        res = pl.pallas_call(wrapped, out_shape=out_shape + tuple(x_outs), grid=grid, in_specs=all_in, out_specs=all_out,
                             scratch_shapes=all_scr, input_output_aliases=aliases, compiler_params=cparams,
                             name=name)(*args, *x_args)
    else:
        gs = pltpu.PrefetchScalarGridSpec(num_scalar_prefetch=1, grid=grid, in_specs=all_in, out_specs=all_out,
                                          scratch_shapes=all_scr)
        res = pl.pallas_call(wrapped, out_shape=out_shape + tuple(x_outs), grid_spec=gs, input_output_aliases=aliases,
                             compiler_params=cparams, name=name)(prefetch, *args, *x_args)
    xres = [tuple(res[n_out + o0:n_out + o0 + no]) for (_, _, o0, no) in spans]
    return tuple(res[:n_out]), xres


def _tables(S):
    f32 = np.float32
    pos = np.arange(S, dtype=f32)
    lane = np.arange(128)
    inv = (f32(ROPE_THETA) ** (-np.arange(0, 64, 2, dtype=f32) / f32(64))).astype(f32)
    ang = (pos[:, None] * inv[None, :]).astype(np.float64)
    idx = (lane % 64) % 32
    c, s = np.cos(ang)[:, idx], np.sin(ang)[:, idx]
    first = ((lane % 64) < 32)[None, :]
    rope = np.stack([c, np.where(first, 0.0, s), np.where(first, -s, 0.0)])
    base = (f32(1.0) / (f32(ROPE_THETA) ** np.linspace(0.0, 1.0, 64, dtype=f32))).astype(f32)
    ang2 = (pos[:, None] * base[None, :]).astype(np.float64)
    c2, s2 = np.cos(ang2)[:, lane // 2], np.sin(ang2)[:, lane // 2]
    even = (lane % 2 == 0)[None, :]
    th = np.stack([c2, np.where(even, 0.0, s2), np.where(even, -s2, 0.0)])
    return np.stack([rope, th, th * (128 ** -0.5)]).astype(f32)


def _rot(a, c, sa, sb, shift):
    return a * c + pltpu.roll(a, shift, 1) * sa + pltpu.roll(a, 128 - shift, 1) * sb


def _unrot(g, c, sa, sb, shift):
    return g * c + pltpu.roll(g * sa, 128 - shift, 1) + pltpu.roll(g * sb, shift, 1)


def _ret_consts():
    h = np.arange(RET_HEADS, dtype=np.float64)
    log_g = np.log1p(-(2.0 ** (-5.0 - h)))
    idx = np.arange(BLK, dtype=np.float64)
    diff = idx[:, None] - idx[None, :]
    dmask = np.where(diff[None] >= 0, np.exp(np.maximum(diff, 0.0)[None] * log_g[:, None, None]), 0.0)
    zeta = np.exp((BLK - 1 - idx)[None, :] * log_g[:, None])
    xi = np.exp((idx + 1.0)[None, :] * log_g[:, None])
    dec = np.exp(BLK * log_g)
    rep = lambda v: np.broadcast_to(v[:, :, None], (RET_HEADS, BLK, 128))
    return (jnp.asarray(dmask, F32), jnp.asarray(rep(zeta), F32), jnp.asarray(rep(xi), F32),
            jnp.asarray(np.broadcast_to(dec[:, None, None], (RET_HEADS, 8, 256)), F32))


def _rms_fwd(x, g, to_cast=(), exchanges=()):
    S = x.shape[0]
    steps = 4
    tm = S // steps
    n_c = len(to_cast)

    def body(x_ref, g_ref, *refs):
        c_in, (h_ref, ht_ref), c_out = refs[:n_c], refs[n_c:n_c + 2], refs[n_c + 2:]
        for rows in _row_pieces(tm, 512):
            xv = x_ref[rows, :]
            r = lax.rsqrt(jnp.mean(xv * xv, axis=-1, keepdims=True) + NORM_EPS)
            h = xv * r * g_ref[...]
            h_ref[rows, :] = h.astype(BF16)
            ht_ref[:, rows] = h.T.astype(BF16)
        for a_ref, o_ref in zip(c_in, c_out):
            o_ref[...] = a_ref[...].astype(BF16)

    slab = lambda a: pl.BlockSpec((a.shape[0] // steps, a.shape[1]), lambda i: (i, 0))
    return _carrier_call(
        body, (x, g, *to_cast),
        out_shape=(SDS((S, D_MODEL), BF16), SDS((D_MODEL, S), BF16), *[SDS(a.shape, BF16) for a in to_cast]),
        grid=(steps,),
        in_specs=[pl.BlockSpec((tm, D_MODEL), lambda i: (i, 0)), pl.BlockSpec((1, D_MODEL), lambda i: (0, 0))]
        + [slab(a) for a in to_cast],
        out_specs=(pl.BlockSpec((tm, D_MODEL), lambda i: (i, 0)), pl.BlockSpec((D_MODEL, tm), lambda i: (0, i)),
                   *[slab(a) for a in to_cast]),
        sem=("parallel",), name="rms_fwd", exchanges=exchanges)


def _in_proj(h, w_in, tab, exchanges=()):
    S = h.shape[0]
    tm = min(S, 4096)

    def body(h_ref, w_ref, t_ref, o_ref):
        j = pl.program_id(1)
        is_rope = j < 6
        is_theta = (j == QR_B) | (j == KR_B)
        sub = 512

        def rotated(shift):
            for i in range(tm // sub):
                rows = slice(i * sub, (i + 1) * sub)
                acc = _dot(h_ref[rows, :], w_ref[...])
                c, sa, sb = t_ref[0, 0, rows, :], t_ref[0, 1, rows, :], t_ref[0, 2, rows, :]
                for k in range(COLB // 128):
                    sl = slice(k * 128, (k + 1) * 128)
                    o_ref[rows, sl] = _rot(acc[:, sl], c, sa, sb, shift).astype(BF16)

        @pl.when(is_rope)
        def _():
            rotated(32)

        @pl.when(is_theta)
        def _():
            rotated(1)

        @pl.when(jnp.logical_not(is_rope | is_theta))
        def _():
            o_ref[...] = _dot(h_ref[...], w_ref[...]).astype(BF16)

    def tab_map(i, j):
        return (jnp.where(j == QR_B, 1, jnp.where(j == KR_B, 2, 0)), 0, i, 0)

    (proj,), xres = _carrier_call(
        body, (h, w_in, tab), out_shape=(SDS((S, PROJ_W), BF16),), grid=(S // tm, N_COLB),
        in_specs=[pl.BlockSpec((tm, D_MODEL), lambda i, j: (i, 0)),
                  pl.BlockSpec((D_MODEL, COLB), lambda i, j: (0, j)),
                  pl.BlockSpec((1, 3, tm, 128), tab_map)],
        out_specs=(pl.BlockSpec((tm, COLB), lambda i, j: (i, j)),),
        sem=("parallel", "arbitrary"), name="in_proj", exchanges=exchanges)
    return proj, xres


def _band_mask(n):
    qi = lax.broadcasted_iota(jnp.int32, (BLK, 2 * BLK), 0)
    kj = lax.broadcasted_iota(jnp.int32, (BLK, 2 * BLK), 1)
    dist = BLK + qi - kj
    return (dist >= 0) & (dist <= BLK) & ((kj >= BLK) | (n > 0))


def _qkv_col(d, gi):
    if d == 1:
        return lambda t, r: 3 * t + gi
    return lambda t, r: 3 * r + t


def _attn_fwd(qkv, d, gi, exchanges=()):
    L = qkv.shape[0]
    nb = L // BLK

    def body(q_ref, kc_ref, kp_ref, vc_ref, vp_ref, o_ref, lse_ref):
        n = pl.program_id(1)
        mask = _band_mask(n)
        mask2 = jnp.concatenate([mask, mask], axis=0)
        lane = lax.broadcasted_iota(jnp.int32, (BLK, 128), 1)
        lo = lane < 64
        lse_all = jnp.zeros((BLK, 128), F32)
        chunks = [slice(c * 128, (c + 1) * 128) for c in range(4)]
        scores, vals = [], []
        for sl in chunks:
            q = q_ref[:, sl]
            k = jnp.concatenate([kp_ref[:, sl], kc_ref[:, sl]], axis=0)
            vals.append(jnp.concatenate([vp_ref[:, sl], vc_ref[:, sl]], axis=0))
            q2 = jnp.concatenate([jnp.where(lo, q, jnp.zeros_like(q)), jnp.where(lo, jnp.zeros_like(q), q)], axis=0)
            scores.append(_dot_nt(q2, k))
        probs = []
        for c, s in enumerate(scores):
            s = jnp.where(mask2, s * 0.125, jnp.float32(-1e30))
            m = jnp.max(s, axis=-1, keepdims=True)
            p = jnp.exp(s - m)
            l = jnp.sum(p, axis=-1, keepdims=True)
            probs.append((p * (1.0 / l)).astype(BF16))
            lse = m + jnp.log(l)
            lse_all = jnp.where(lane // 16 == 2 * c, lse[:BLK], jnp.where(lane // 16 == 2 * c + 1, lse[BLK:], lse_all))
        for sl, p, v in zip(chunks, probs, vals):
            o2 = _dot(p, v)
            o_ref[:, sl] = jnp.where(lo, o2[:BLK], o2[BLK:])
        lse_ref[...] = lse_all

    prev = lambda n: jnp.maximum(n - 1, 0)
    col = _qkv_col(d, gi)
    return _carrier_call(
        body, (qkv,) * 5, out_shape=(SDS((L, d * 512), F32), SDS((L, d * 128), F32)), grid=(d, nb),
        in_specs=[pl.BlockSpec((BLK, 512), lambda r, n: (n, col(0, r))),
                  pl.BlockSpec((BLK, 512), lambda r, n: (n, col(1, r))),
                  pl.BlockSpec((BLK, 512), lambda r, n: (prev(n), col(1, r))),
                  pl.BlockSpec((BLK, 512), lambda r, n: (n, col(2, r))),
                  pl.BlockSpec((BLK, 512), lambda r, n: (prev(n), col(2, r)))],
        out_specs=(pl.BlockSpec((BLK, 512), lambda r, n: (n, r)),
                   pl.BlockSpec((BLK, 128), lambda r, n: (n, r))),
        sem=("parallel", "arbitrary"), name=f"attn_fwd_g{gi}", exchanges=exchanges)


def _qkv_to_sub(proj, d, gi):
    S = proj.shape[0]
    tm = 512
    n = tm // d

    def body(q_ref, k_ref, v_ref, o_ref, scr):
        for t, ref in enumerate((q_ref, k_ref, v_ref)):
            for c in range(4):
                scr[c] = ref[:, c * 128:(c + 1) * 128].astype(F32)
            for r in range(d):
                for c in range(4):
                    col = (3 * r + t) * 512 + c * 128
                    o_ref[:, col:col + 128] = scr[c, pl.ds(r, n, stride=d), :].astype(BF16)

    return pl.pallas_call(
        body, out_shape=SDS((S // d, d * 1536), BF16), grid=(S // tm,),
        in_specs=[pl.BlockSpec((tm, 512), lambda i, b=b: (i, b + gi)) for b in (QA_B, KA_B, VA_B)],
        out_specs=pl.BlockSpec((n, d * 1536), lambda i: (i, 0)),
        scratch_shapes=[pltpu.VMEM((4, tm, 128), F32)],
        compiler_params=_cparams("parallel"), name=f"qkv_to_sub_g{gi}")(proj, proj, proj)


def _attn_merge(os_, lses):
    S = os_[0].shape[0]
    tm = 512

    def body(o0, o1, o2, l0, l1, l2, att_ref, lt_ref, lt1_ref, lt2_ref, so1, so2, sl1, sl2):
        lo = lax.broadcasted_iota(jnp.int32, (tm, 128), 1) < 64

        def natural(ref, d, scr, width):
            nch = width // 128
            if d == 1:
                return [ref[:, c * 128:(c + 1) * 128] for c in range(nch)]
            for r in range(d):
                for c in range(nch):
                    scr[c, pl.ds(r, tm // d, stride=d), :] = ref[:, r * width + c * 128:r * width + (c + 1) * 128]
            return [scr[c] for c in range(nch)]

        ls = [natural(l, d, s, 128)[0] for l, d, s in zip((l0, l1, l2), DILATIONS, (None, sl1, sl2))]
        m = jnp.maximum(jnp.maximum(ls[0], ls[1]), ls[2])
        es = [jnp.exp(v - m) for v in ls]
        z = es[0] + es[1] + es[2]
        lt = m + jnp.log(z)
        lt_ref[...] = lt
        sl1[0] = lt
        for ref, d in ((lt1_ref, DILATIONS[1]), (lt2_ref, DILATIONS[2])):
            for r in range(d):
                ref[:, r * 128:(r + 1) * 128] = sl1[0, pl.ds(r, tm // d, stride=d), :]
        ws = [e / z for e in es]
        o_nat = [natural(o, d, s, 512) for o, d, s in zip((o0, o1, o2), DILATIONS, (None, so1, so2))]
        for c in range(4):
            acc = jnp.zeros((tm, 128), F32)
            for g in range(3):
                w_lo = jnp.broadcast_to(ws[g][:, 32 * c:32 * c + 1], (tm, 128))
                w_hi = jnp.broadcast_to(ws[g][:, 32 * c + 16:32 * c + 17], (tm, 128))
                acc = acc + jnp.where(lo, w_lo, w_hi) * o_nat[g][c]
            att_ref[:, c * 128:(c + 1) * 128] = acc.astype(BF16)

    sub = lambda w: [pl.BlockSpec((tm // d, d * w), lambda i: (i, 0)) for d in DILATIONS]
    att, *lts = pl.pallas_call(
        body, out_shape=(SDS((S, 512), BF16), *[SDS((S // d, d * 128), F32) for d in DILATIONS]), grid=(S // tm,),
        in_specs=sub(512) + sub(128),
        out_specs=(pl.BlockSpec((tm, 512), lambda i: (i, 0)), *sub(128)),
        scratch_shapes=[pltpu.VMEM((4, tm, 128), F32), pltpu.VMEM((4, tm, 128), F32),
                        pltpu.VMEM((1, tm, 128), F32), pltpu.VMEM((1, tm, 128), F32)],
        compiler_params=_cparams("parallel"), name="attn_merge")(*os_, *lses)
    return att, lts


def _assemble_dproj(att_grads, dproj):
    S = dproj.shape[0]
    tm = 256

    def body(*refs):
        a = [refs[3 * t:3 * t + 3] for t in range(3)]
        dp_prev, o_ref, scr = refs[9:]
        for t in range(3):
            for g, d in enumerate(DILATIONS):
                base = (3 * t + g) * COLB
                if d == 1:
                    o_ref[:, base:base + COLB] = a[t][g][...]
                    continue
                for c in range(4):
                    for r in range(d):
                        scr[c, pl.ds(r, tm // d, stride=d), :] = a[t][g][:, r * 512 + c * 128:r * 512 + (c + 1) * 128].astype(F32)
                    o_ref[:, base + c * 128:base + (c + 1) * 128] = scr[c].astype(BF16)

    sub = [pl.BlockSpec((tm // d, d * 512), lambda i: (i, 0)) for d in DILATIONS]
    flat = [att_grads[t][g] for t in range(3) for g in range(3)]
    return pl.pallas_call(
        body, out_shape=SDS((S, PROJ_W), BF16), grid=(S // tm,),
        in_specs=sub * 3 + [ANY], out_specs=pl.BlockSpec((tm, 9 * COLB), lambda i: (i, 0)),
        scratch_shapes=[pltpu.VMEM((4, tm, 128), F32)], input_output_aliases={9: 0},
        compiler_params=_cparams("parallel"), name="assemble_dproj")(*flat, dproj)


def _ret_fwd(proj, consts, exchanges=()):
    S = proj.shape[0]
    nc = S // BLK
    dmask, zeta, xi, dec = consts

    def body(q_ref, k_ref, v0_ref, v1_ref, g0_ref, g1_ref, dm_ref, z_ref, x_ref, dec_ref,
             y_ref, rn_ref, rs_ref, st_ref, R):
        @pl.when(pl.program_id(0) == 0)
        def _():
            R[...] = jnp.zeros_like(R)

        lane16 = lax.broadcasted_iota(jnp.int32, (BLK, 128), 1) // 16
        rs_all = jnp.zeros((BLK, 128), F32)
        first = []
        for h in range(RET_HEADS):
            hs = slice(h * 128, (h + 1) * 128)
            q, k = q_ref[:, hs], k_ref[:, hs]
            v = (v0_ref if h < 2 else v1_ref)[:, (h % 2) * 256:(h % 2 + 1) * 256]
            Rb = R[h].astype(BF16)
            st_ref[h] = Rb
            kz = (k.astype(F32) * z_ref[h]).astype(BF16)
            first.append((v, _dot_nt(q, k), _dot((q.astype(F32) * x_ref[h]).astype(BF16), Rb), _dot_tn(kz, v)))
        masked = [(s * dm_ref[h]).astype(BF16) for h, (_, s, _, _) in enumerate(first)]
        for h in range(RET_HEADS):
            vs = slice((h % 2) * 256, (h % 2 + 1) * 256)
            os_ = slice(h * 256, (h + 1) * 256)
            v, _, cross, kv = first[h]
            o = _dot(masked[h], v) + cross
            R[h] = R[h] * dec_ref[h, 0:1, :] + kv
            mu = jnp.mean(o, axis=-1, keepdims=True)
            oc = o - mu
            rstd = lax.rsqrt(jnp.mean(oc * oc, axis=-1, keepdims=True) + NORM_EPS)
            rn = oc * rstd
            gr = (g0_ref if h < 2 else g1_ref)[:, vs].astype(F32)
            y_ref[:, os_] = (rn * gr * _sigmoid(gr)).astype(BF16)
            rn_ref[:, os_] = rn.astype(BF16)
            rs_all = jnp.where(lane16 == h, rstd, rs_all)
        rs_ref[...] = rs_all

    cst = lambda shape: pl.BlockSpec(shape, lambda c: (0, 0, 0))
    blk = lambda j: pl.BlockSpec((BLK, 512), lambda c: (c, j))
    return _carrier_call(
        body, (proj, proj, proj, proj, proj, proj, dmask, zeta, xi, dec),
        out_shape=(SDS((S, 1024), BF16), SDS((S, 1024), BF16), SDS((S, 128), F32), SDS((RET_HEADS, nc, BLK, 256), BF16)),
        grid=(nc,),
        in_specs=[blk(QR_B), blk(KR_B), blk(11), blk(12), blk(13), blk(14),
                  cst((RET_HEADS, BLK, BLK)), cst((RET_HEADS, BLK, 128)), cst((RET_HEADS, BLK, 128)), cst((RET_HEADS, 8, 256))],
        out_specs=(pl.BlockSpec((BLK, 1024), lambda c: (c, 0)), pl.BlockSpec((BLK, 1024), lambda c: (c, 0)),
                   pl.BlockSpec((BLK, 128), lambda c: (c, 0)),
                   pl.BlockSpec((RET_HEADS, None, BLK, 256), lambda c: (0, c, 0, 0))),
        scratch_shapes=[pltpu.VMEM((RET_HEADS, BLK, 256), F32)],
        sem=("arbitrary",), name="ret_fwd", exchanges=exchanges)


def _mix_out(att, yrin, proj, wa, wr, wo, x, g2, exchanges=()):
    S = x.shape[0]
    tm = 512
    gate0 = 15 * COLB

    def body(a_ref, y_ref, ga_ref, gr_ref, wa_ref, wr_ref, wo_ref, x_ref, g_ref, m_ref, ya_ref, yr_ref, x1_ref, h2_ref):
        pieces = _row_pieces(tm, 256)
        branches = [(_dot(a_ref[rows, :], wa_ref[...]), _dot(y_ref[rows, :], wr_ref[...])) for rows in pieces]
        merged = []
        for rows, (ya, yr) in zip(pieces, branches):
            m = (_sigmoid(ga_ref[rows, :].astype(F32)) * ya + _sigmoid(gr_ref[rows, :].astype(F32)) * yr).astype(BF16)
            m_ref[rows, :] = m
            ya_ref[rows, :] = ya.astype(BF16)
            yr_ref[rows, :] = yr.astype(BF16)
            merged.append(m)
        for rows, m in zip(pieces, merged):
            x1 = x_ref[rows, :] + _dot(m, wo_ref[...])
            x1_ref[rows, :] = x1
            r = lax.rsqrt(jnp.mean(x1 * x1, axis=-1, keepdims=True) + NORM_EPS)
            h2_ref[rows, :] = (x1 * r * g_ref[...]).astype(BF16)

    row = lambda w: pl.BlockSpec((tm, w), lambda i: (i, 0))
    cols = lambda c0: pl.BlockSpec((pl.Element(tm), pl.Element(D_MODEL)), lambda i: (i * tm, c0))
    resident = lambda r, c: pl.BlockSpec((r, c), lambda i: (0, 0), pipeline_mode=pl.Buffered(1))
    return _carrier_call(
        body, (att, yrin, proj, proj, wa, wr, wo, x, g2),
        out_shape=(SDS((S, D_MODEL), BF16),) * 3 + (SDS((S, D_MODEL), F32), SDS((S, D_MODEL), BF16)), grid=(S // tm,),
        in_specs=[row(512), row(D_MODEL), cols(gate0), cols(gate0 + D_MODEL), resident(512, D_MODEL),
                  resident(D_MODEL, D_MODEL), resident(D_MODEL, D_MODEL), row(D_MODEL),
                  pl.BlockSpec((1, D_MODEL), lambda i: (0, 0))],
        out_specs=(row(D_MODEL),) * 5, sem=("parallel",), name="mix_out", exchanges=exchanges)


def _ffn_up(h2, wg, wu, exchanges=()):
    S = h2.shape[0]
    tm = min(S, 2048)

    def body(h_ref, wg_ref, wu_ref, g_ref, u_ref, a_ref):
        for rows in _row_pieces(tm):
            hv = h_ref[rows, :]
            g = _dot_nt(hv, wg_ref[...])
            u = _dot_nt(hv, wu_ref[...])
            g_ref[rows, :] = g.astype(BF16)
            u_ref[rows, :] = u.astype(BF16)
            a_ref[rows, :] = (g * _sigmoid(g) * u).astype(BF16)

    wspec = pl.BlockSpec((None, HID_S, D_MODEL), lambda i, s: (s, 0, 0))
    ospec = pl.BlockSpec((None, tm, HID_S), lambda i, s: (s, i, 0))
    return _carrier_call(
        body, (h2, wg, wu), out_shape=(SDS((N_SHARD, S, HID_S), BF16),) * 3, grid=(S // tm, N_SHARD),
        in_specs=[pl.BlockSpec((tm, D_MODEL), lambda i, s: (i, 0)), wspec, wspec],
        out_specs=(ospec, ospec, ospec),
        sem=("parallel", "arbitrary"), name="ffn_up", exchanges=exchanges)


def _ffn_down_loss(act, wd, x1, g3, tgt):
    S = x1.shape[0]
    tm = 512

    def body(a_ref, w_ref, x_ref, g_ref, t_ref, dx_ref, dxb_ref, dg_ref, ls_ref):
        @pl.when(pl.program_id(0) == 0)
        def _():
            dg_ref[...] = jnp.zeros_like(dg_ref)
            ls_ref[...] = jnp.zeros_like(ls_ref)

        g = g_ref[...]
        for rows in _row_pieces(tm, 256):
            y = _dot(a_ref[0, rows, :], w_ref[0])
            for s in range(1, N_SHARD):
                y = y + _dot(a_ref[s, rows, :], w_ref[s])
            x2 = x_ref[rows, :] + y
            r = lax.rsqrt(jnp.mean(x2 * x2, axis=-1, keepdims=True) + NORM_EPS)
            xh = x2 * r
            err = xh * g - t_ref[rows, :]
            ls_ref[...] += jnp.sum(jnp.sum(err * err, axis=-1, keepdims=True), axis=0, keepdims=True) * (0.5 / D_MODEL)
            dy = err * (1.0 / D_MODEL)
            dg_ref[...] += jnp.sum(dy * xh, axis=0, keepdims=True)
            dxh = dy * g
            dx = r * (dxh - xh * jnp.mean(dxh * xh, axis=-1, keepdims=True))
            dx_ref[rows, :] = dx
            dxb_ref[rows, :] = dx.astype(BF16)

    row = pl.BlockSpec((tm, D_MODEL), lambda i: (i, 0))
    vec = pl.BlockSpec((1, D_MODEL), lambda i: (0, 0))
    return pl.pallas_call(
        body, out_shape=(SDS((S, D_MODEL), F32), SDS((S, D_MODEL), BF16), SDS((1, D_MODEL), F32), SDS((8, 128), F32)),
        grid=(S // tm,),
        in_specs=[pl.BlockSpec((N_SHARD, tm, HID_S), lambda i: (0, i, 0)),
                  pl.BlockSpec((N_SHARD, HID_S, D_MODEL), lambda i: (0, 0, 0), pipeline_mode=pl.Buffered(1)),
                  row, vec, row],
        out_specs=(row, row, vec, pl.BlockSpec((8, 128), lambda i: (0, 0))),
        compiler_params=_cparams("arbitrary"), name="ffn_down_loss")(act, wd, x1, g3, tgt)


def _ffn_bwd(dx2b, dx2, wd, wg, wu, gte, up, x1, g2):
    S = x1.shape[0]
    tm = 256

    def body(d_ref, dx2_ref, wd_ref, wg_ref, wu_ref, g_ref, u_ref, x_ref, gn_ref,
             dg_ref, du_ref, dx_ref, dxb_ref, dgn_ref):
        @pl.when(pl.program_id(0) == 0)
        def _():
            dgn_ref[...] = jnp.zeros_like(dgn_ref)

        d = d_ref[...]
        dacts = [_dot_nt(d, wd_ref[s]) for s in range(N_SHARD)]
        dgs, dus = [], []
        for s, da in enumerate(dacts):
            g = g_ref[s].astype(F32)
            sg = _sigmoid(g)
            dgs.append((da * u_ref[s].astype(F32) * sg * (1.0 + g * (1.0 - sg))).astype(BF16))
            dus.append((da * g * sg).astype(BF16))
            dg_ref[s] = dgs[s]
            du_ref[s] = dus[s]
        dh = _dot(dgs[0], wg_ref[0]) + _dot(dus[0], wu_ref[0])
        for s in range(1, N_SHARD):
            dh = dh + _dot(dgs[s], wg_ref[s]) + _dot(dus[s], wu_ref[s])
        xv = x_ref[...]
        r = lax.rsqrt(jnp.mean(xv * xv, axis=-1, keepdims=True) + NORM_EPS)
        xh = xv * r
        dgn_ref[...] += jnp.sum(dh * xh, axis=0, keepdims=True)
        dxh = dh * gn_ref[...]
        dx = dx2_ref[...] + r * (dxh - xh * jnp.mean(dxh * xh, axis=-1, keepdims=True))
        dx_ref[...] = dx
        dxb_ref[...] = dx.astype(BF16)

    row = pl.BlockSpec((tm, D_MODEL), lambda i: (i, 0))
    vec = pl.BlockSpec((1, D_MODEL), lambda i: (0, 0))
    aspec = pl.BlockSpec((N_SHARD, tm, HID_S), lambda i: (0, i, 0))
    resident = lambda shape: pl.BlockSpec(shape, lambda i: (0, 0, 0), pipeline_mode=pl.Buffered(1))
    return pl.pallas_call(
        body,
        out_shape=(SDS((N_SHARD, S, HID_S), BF16), SDS((N_SHARD, S, HID_S), BF16),
                   SDS((S, D_MODEL), F32), SDS((S, D_MODEL), BF16), SDS((1, D_MODEL), F32)),
        grid=(S // tm,),
        in_specs=[row, row, resident((N_SHARD, HID_S, D_MODEL)), resident((N_SHARD, HID_S, D_MODEL)),
                  resident((N_SHARD, HID_S, D_MODEL)), aspec, aspec, row, vec],
        out_specs=(aspec, aspec, row, row, vec),
        compiler_params=_cparams("arbitrary"), name="ffn_bwd")(dx2b, dx2, wd, wg, wu, gte, up, x1, g2)


def _wgrad(name, a, b, a_spec, b_spec, out_shape, out_spec, n_par, S):
    tk = min(S, 4096)

    def body(a_ref, b_ref, o_ref):
        @pl.when(pl.program_id(1) == 0)
        def _():
            o_ref[...] = jnp.zeros_like(o_ref)

        o_ref[...] += _dot_tn(a_ref[...], b_ref[...])

    return pl.pallas_call(
        body, out_shape=SDS(out_shape, F32), grid=(n_par, S // tk),
        in_specs=[a_spec(tk), b_spec(tk)], out_specs=out_spec,
        compiler_params=_cparams("parallel", "arbitrary"), name=name)(a, b)


def _mix_bwd(dx1b, wo, proj, ya, yr, wa, wr, att, exchanges=()):
    S = dx1b.shape[0]
    tm = 512
    gate0 = 15 * COLB

    def body(d_ref, wo_ref, ga_ref, gr_ref, ya_ref, yr_ref, wa_ref, wr_ref, att_ref,
             dya_ref, dyr_ref, dp_ref, dyi_ref, datt_ref, datt1_ref, datt2_ref, rho_ref, rho1_ref, rho2_ref,
             datt_scr, rho_scr):
        pieces = _row_pieces(tm, 256)
        dms = [_dot_nt(d_ref[rows, :], wo_ref[...]) for rows in pieces]
        branch = []
        for rows, dm in zip(pieces, dms):
            sa = _sigmoid(ga_ref[rows, :].astype(F32))
            sr = _sigmoid(gr_ref[rows, :].astype(F32))
            dya, dyr = (dm * sa).astype(BF16), (dm * sr).astype(BF16)
            dya_ref[rows, :] = dya
            dyr_ref[rows, :] = dyr
            dp_ref[rows, 0:D_MODEL] = (dm * ya_ref[rows, :].astype(F32) * sa * (1.0 - sa)).astype(BF16)
            dp_ref[rows, D_MODEL:2 * D_MODEL] = (dm * yr_ref[rows, :].astype(F32) * sr * (1.0 - sr)).astype(BF16)
            branch.append((dya, dyr))
        lane = lax.broadcasted_iota(jnp.int32, (256, 128), 1)
        lo = lane < 64
        for rows, (dya, dyr) in zip(pieces, branch):
            datt = _dot_nt(dya, wa_ref[...])
            datt_ref[rows, :] = datt.astype(BF16)
            dyi_ref[rows, :] = _dot_nt(dyr, wr_ref[...]).astype(BF16)
            prod = datt * att_ref[rows, :].astype(F32)
            rho = jnp.zeros((256, 128), F32)
            for c in range(4):
                pc = prod[:, c * 128:(c + 1) * 128]
                tot = jnp.sum(pc, axis=-1, keepdims=True)
                low = jnp.sum(jnp.where(lo, pc, 0.0), axis=-1, keepdims=True)
                rho = jnp.where(lane // 16 == 2 * c, low, jnp.where(lane // 16 == 2 * c + 1, tot - low, rho))
            rho_ref[rows, :] = rho
            rho_scr[0] = rho
            for c in range(4):
                datt_scr[c] = datt[:, c * 128:(c + 1) * 128]
            for d, dv_ref, rv_ref in ((DILATIONS[1], datt1_ref, rho1_ref), (DILATIONS[2], datt2_ref, rho2_ref)):
                n = 256 // d
                sub_rows = slice(rows.start // d, rows.start // d + n)
                for r in range(d):
                    rv_ref[sub_rows, r * 128:(r + 1) * 128] = rho_scr[0, pl.ds(r, n, stride=d), :]
                    for c in range(4):
                        col = r * 512 + c * 128
                        dv_ref[sub_rows, col:col + 128] = datt_scr[c, pl.ds(r, n, stride=d), :].astype(BF16)

    row = lambda w: pl.BlockSpec((tm, w), lambda i: (i, 0))
    sub = lambda w: [pl.BlockSpec((tm // d, d * w), lambda i: (i, 0)) for d in DILATIONS]
    cols = lambda c0, w: pl.BlockSpec((pl.Element(tm), pl.Element(w)), lambda i: (i * tm, c0))
    resident = lambda r, c: pl.BlockSpec((r, c), lambda i: (0, 0), pipeline_mode=pl.Buffered(1))
    (dya, dyr, dproj, dyrin, *views), xres = _carrier_call(
        body, (dx1b, wo, proj, proj, ya, yr, wa, wr, att),
        out_shape=(SDS((S, D_MODEL), BF16), SDS((S, D_MODEL), BF16), SDS((S, PROJ_W), BF16), SDS((S, D_MODEL), BF16),
                   *[SDS((S // d, d * 512), BF16) for d in DILATIONS], *[SDS((S // d, d * 128), F32) for d in DILATIONS]),
        grid=(S // tm,),
        in_specs=[row(D_MODEL), resident(D_MODEL, D_MODEL), cols(gate0, D_MODEL), cols(gate0 + D_MODEL, D_MODEL),
                  row(D_MODEL), row(D_MODEL), resident(512, D_MODEL), resident(D_MODEL, D_MODEL), row(512)],
        out_specs=(row(D_MODEL), row(D_MODEL), cols(gate0, 2 * D_MODEL), row(D_MODEL), *sub(512), *sub(128)),
        scratch_shapes=[pltpu.VMEM((4, 256, 128), F32), pltpu.VMEM((1, 256, 128), F32)],
        sem=("parallel",), name="mix_bwd", exchanges=exchanges)
    return (dya, dyr, dproj, dyrin, views[:3], views[3:]), xres


def _attn_bwd(qkv, datt, lse, rho, rtab, d, gi, exchanges=()):
    L = qkv.shape[0]
    nb = L // BLK
    T = d * nb

    def body(q_ref, kc_ref, kp_ref, vc_ref, vp_ref, do_ref, lse_ref, rho_ref, tq_ref, tk_ref,
             dq_ref, dk_ref, dv_ref, ck, cv):
        t = pl.program_id(0)
        n = jnp.minimum(t, T - 1) % nb

        @pl.when(t == 0)
        def _():
            ck[...] = jnp.zeros_like(ck)
            cv[...] = jnp.zeros_like(cv)

        def store_rot(ref, val, t_ref, c):
            sl = slice(c * 128, (c + 1) * 128)
            ref[:, sl] = _unrot(val, t_ref[0], t_ref[1], t_ref[2], 32).astype(BF16)

        @pl.when(t < T)
        def _():
            mask = _band_mask(n)
            mask2 = jnp.concatenate([mask, mask], axis=0)
            lo = lax.broadcasted_iota(jnp.int32, (BLK, 128), 1) < 64

            def stacked(a):
                return jnp.concatenate([jnp.where(lo, a, jnp.zeros_like(a)), jnp.where(lo, jnp.zeros_like(a), a)], axis=0)

            def head_cols(ref, c):
                return jnp.concatenate([jnp.broadcast_to(ref[:, 32 * c:32 * c + 1], (BLK, 2 * BLK)),
                                        jnp.broadcast_to(ref[:, 32 * c + 16:32 * c + 17], (BLK, 2 * BLK))], axis=0)

            ops, raw = [], []
            for c in range(4):
                sl = slice(c * 128, (c + 1) * 128)
                q2, do2 = stacked(q_ref[:, sl]), stacked(do_ref[:, sl])
                k = jnp.concatenate([kp_ref[:, sl], kc_ref[:, sl]], axis=0)
                v = jnp.concatenate([vp_ref[:, sl], vc_ref[:, sl]], axis=0)
                ops.append((q2, do2, k))
                raw.append((_dot_nt(q2, k), _dot_nt(do2, v)))
            grads = []
            for c, (s, dp) in enumerate(raw):
                p = jnp.where(mask2, jnp.exp(s * 0.125 - head_cols(lse_ref, c)), 0.0)
                grads.append(((p * (dp - head_cols(rho_ref, c)) * 0.125).astype(BF16), p.astype(BF16)))
            for c, ((q2, do2, k), (ds, pb)) in enumerate(zip(ops, grads)):
                sl = slice(c * 128, (c + 1) * 128)
                dq2 = _dot(ds, k)
                dq_c = jnp.where(lo, dq2[:BLK], dq2[BLK:])
                dk_c = _dot_tn(ds, q2)
                dv_c = _dot_tn(pb, do2)
                store_rot(dq_ref, dq_c, tq_ref, c)
                store_rot(dk_ref, ck[:, sl] + dk_c[:BLK], tk_ref, c)
                dv_ref[:, sl] = (cv[:, sl] + dv_c[:BLK]).astype(BF16)
                ck[:, sl] = dk_c[BLK:]
                cv[:, sl] = dv_c[BLK:]

        @pl.when(t == T)
        def _():
            for c in range(4):
                sl = slice(c * 128, (c + 1) * 128)
                store_rot(dk_ref, ck[:, sl], tk_ref, c)
            dv_ref[...] = cv[...].astype(BF16)

    blk_of = lambda t: (jnp.minimum(t, T - 1) % nb, jnp.minimum(t, T - 1) // nb)
    cur = lambda t: blk_of(t)
    prev = lambda t: (jnp.maximum(blk_of(t)[0] - 1, 0), blk_of(t)[1])
    fin = lambda t: blk_of(jnp.maximum(t - 1, 0))
    col = _qkv_col(d, gi)
    qkv_spec = lambda kind, which: pl.BlockSpec((BLK, 512), lambda t: (which(t)[0], col(kind, which(t)[1])))
    row_spec = lambda w, which: pl.BlockSpec((BLK, w), lambda t: which(t))
    tab_spec = lambda which: pl.BlockSpec((3, BLK, 128), lambda t: (0, *which(t)))
    return _carrier_call(
        body, (qkv, qkv, qkv, qkv, qkv, datt, lse, rho, rtab, rtab),
        out_shape=(SDS((L, d * 512), BF16),) * 3, grid=(T + 1,),
        in_specs=[qkv_spec(0, cur), qkv_spec(1, cur), qkv_spec(1, prev), qkv_spec(2, cur), qkv_spec(2, prev),
                  row_spec(512, cur), row_spec(128, cur), row_spec(128, cur), tab_spec(cur), tab_spec(fin)],
        out_specs=(row_spec(512, cur), row_spec(512, fin), row_spec(512, fin)),
        scratch_shapes=[pltpu.VMEM((BLK, 512), F32), pltpu.VMEM((BLK, 512), F32)],
        sem=("arbitrary",), name=f"attn_bwd_g{gi}", exchanges=exchanges)


def _ret_bwd(proj, rn, rstd, dyrin, states, tab, consts, dproj, exchanges=()):
    S = proj.shape[0]
    nc = S // BLK
    dmask, zeta, xi, dec = consts

    def body(q_ref, k_ref, v0_ref, v1_ref, g0_ref, g1_ref, rn_ref, rs_ref, dy_ref, st_ref, tq_ref, tk_ref,
             dm_ref, z_ref, x_ref, dec_ref, dp_prev, dp_ref, dR):
        dq_ref, dk_ref = dp_ref.at[:, 0:512], dp_ref.at[:, 512:1024]
        dv_ref, dgr_ref = dp_ref.at[:, 1024:2048], dp_ref.at[:, 2048:3072]

        @pl.when(pl.program_id(0) == 0)
        def _():
            dR[...] = jnp.zeros_like(dR)

        dobs = []
        for h in range(RET_HEADS):
            vs = slice((h % 2) * 256, (h % 2 + 1) * 256)
            os_ = slice(h * 256, (h + 1) * 256)
            gr = (g0_ref if h < 2 else g1_ref)[:, vs].astype(F32)
            sg = _sigmoid(gr)
            rn_v = rn_ref[:, os_].astype(F32)
            dyi = dy_ref[:, os_].astype(F32)
            dgr_ref[:, os_] = (dyi * rn_v * sg * (1.0 + gr * (1.0 - sg))).astype(BF16)
            drn = dyi * gr * sg
            rstd = jnp.broadcast_to(rs_ref[:, 16 * h:16 * h + 1], (BLK, 256))
            do = rstd * (drn - jnp.mean(drn, axis=-1, keepdims=True) - rn_v * jnp.mean(drn * rn_v, axis=-1, keepdims=True))
            dobs.append(do.astype(BF16))
        first = []
        for h in range(RET_HEADS):
            hs = slice(h * 128, (h + 1) * 128)
            q, k = q_ref[:, hs], k_ref[:, hs]
            v = (v0_ref if h < 2 else v1_ref)[:, (h % 2) * 256:(h % 2 + 1) * 256]
            dob, dRb = dobs[h], dR[h].astype(BF16)
            kz = (k.astype(F32) * z_ref[h]).astype(BF16)
            qx = (q.astype(F32) * x_ref[h]).astype(BF16)
            first.append((q, k, _dot_nt(q, k), _dot_nt(dob, v), _dot(kz, dRb), _dot_nt(dob, st_ref[h]),
                          _dot_nt(v, dRb), _dot_tn(qx, dob)))
        masked = [((s * dm_ref[h]).astype(BF16), (dsr * dm_ref[h]).astype(BF16))
                  for h, (_, _, s, dsr, _, _, _, _) in enumerate(first)]
        for h in range(RET_HEADS):
            hs = slice(h * 128, (h + 1) * 128)
            os_ = slice(h * 256, (h + 1) * 256)
            q, k, _, _, dv_state, dq_state, dk_state, dr_new = first[h]
            sD, dS = masked[h]
            dv_ref[:, os_] = (_dot_tn(sD, dobs[h]) + dv_state).astype(BF16)
            dq = _dot(dS, k) + dq_state * x_ref[h]
            dk = _dot_tn(dS, q) + dk_state * z_ref[h]
            dR[h] = dR[h] * dec_ref[h, 0:1, :] + dr_new
            dq_ref[:, hs] = _unrot(dq, tq_ref[0], tq_ref[1], tq_ref[2], 1).astype(BF16)
            dk_ref[:, hs] = _unrot(dk, tk_ref[0], tk_ref[1], tk_ref[2], 1).astype(BF16)

    rc = lambda c: nc - 1 - c
    cst = lambda shape: pl.BlockSpec(shape, lambda c: (0, 0, 0))
    blk = lambda j: pl.BlockSpec((BLK, 512), lambda c: (rc(c), j))
    row = lambda w: pl.BlockSpec((BLK, w), lambda c: (rc(c), 0))
    (dproj,), xres = _carrier_call(
        body, (proj, proj, proj, proj, proj, proj, rn, rstd, dyrin, states, tab, tab, dmask, zeta, xi, dec, dproj),
        out_shape=(SDS((S, PROJ_W), BF16),), grid=(nc,),
        in_specs=[blk(QR_B), blk(KR_B), blk(11), blk(12), blk(13), blk(14), row(1024), row(128), row(1024),
                  pl.BlockSpec((RET_HEADS, None, BLK, 256), lambda c: (0, rc(c), 0, 0)),
                  pl.BlockSpec((None, 3, BLK, 128), lambda c: (1, 0, rc(c), 0)),
                  pl.BlockSpec((None, 3, BLK, 128), lambda c: (2, 0, rc(c), 0)),
                  cst((RET_HEADS, BLK, BLK)), cst((RET_HEADS, BLK, 128)), cst((RET_HEADS, BLK, 128)), cst((RET_HEADS, 8, 256)),
                  ANY],
        out_specs=(pl.BlockSpec((pl.Element(BLK), pl.Element(6 * COLB)), lambda c: (rc(c) * BLK, QR_B * COLB)),),
        scratch_shapes=[pltpu.VMEM((RET_HEADS, BLK, 256), F32)],
        sem=("arbitrary",), name="ret_bwd", exchanges=exchanges, in_out_aliases={16: 0})
    return dproj, xres


def _wgrad_in_half(ht, dproj, sidx, kept, exchanges=()):
    S = dproj.shape[0]
    tk = 2048
    half = (lambda sx: sx[4]) if kept else (lambda sx: 1 - sx[4])

    def body(a_ref, b_ref, o_ref):
        @pl.when(pl.program_id(1) == 0)
        def _():
            o_ref[...] = jnp.zeros_like(o_ref)

        o_ref[...] += _dot(a_ref[...], b_ref[...])

    (g,), xres = _carrier_call(
        body, (ht, dproj), out_shape=(SDS((D_MODEL // 2, PROJ_W), F32),), grid=(N_SHARD, S // tk),
        in_specs=[pl.BlockSpec((D_MODEL // 2, tk), lambda s, k, sx: (half(sx), k)),
                  pl.BlockSpec((tk, W_IN_S), lambda s, k, sx: (k, s))],
        out_specs=(pl.BlockSpec((D_MODEL // 2, W_IN_S), lambda s, k, sx: (0, s)),),
        sem=("parallel", "arbitrary"), name="wgrad_in_kept" if kept else "wgrad_in_sent", exchanges=exchanges,
        prefetch=sidx)
    return g, xres


def _in_proj_bwd(dproj, w_in, x, g1, dx1, exchanges=()):
    S = x.shape[0]
    tm = 1024

    def body(d_ref, w_ref, x_ref, g_ref, dx1_ref, dx_ref, dgn_ref, acc):
        i, s = pl.program_id(0), pl.program_id(1)

        @pl.when(s == 0)
        def _():
            acc[...] = jnp.zeros_like(acc)

        @pl.when((i == 0) & (s == 0))
        def _():
            dgn_ref[...] = jnp.zeros_like(dgn_ref)

        acc[...] += _dot_nt(d_ref[...], w_ref[...])

        @pl.when(s == N_SHARD - 1)
        def _():
            xv = x_ref[...]
            r = lax.rsqrt(jnp.mean(xv * xv, axis=-1, keepdims=True) + NORM_EPS)
            xh = xv * r
            dh = acc[...]
            dgn_ref[...] += jnp.sum(dh * xh, axis=0, keepdims=True)
            dxh = dh * g_ref[...]
            dx_ref[...] = dx1_ref[...] + r * (dxh - xh * jnp.mean(dxh * xh, axis=-1, keepdims=True))

    row = pl.BlockSpec((tm, D_MODEL), lambda i, s: (i, 0))
    vec = pl.BlockSpec((1, D_MODEL), lambda i, s: (0, 0))
    (gx, dg), xres = _carrier_call(
        body, (dproj, w_in, x, g1, dx1),
        out_shape=(SDS((S, D_MODEL), F32), SDS((1, D_MODEL), F32)), grid=(S // tm, N_SHARD),
        in_specs=[pl.BlockSpec((tm, W_IN_S), lambda i, s: (i, s)),
                  pl.BlockSpec((D_MODEL, W_IN_S), lambda i, s: (0, s)), row, vec, row],
        out_specs=(row, vec), scratch_shapes=[pltpu.VMEM((tm, D_MODEL), F32)],
        sem=("arbitrary", "arbitrary"), name="in_proj_bwd", exchanges=exchanges)
    return gx, dg, xres


def _step(x, tgt, g1, g2, g3, comm):
    S = x.shape[0]
    tab_np = _tables(S)
    tab = jnp.asarray(tab_np)
    consts = _ret_consts()

    (h, ht, *casts), xres = _rms_fwd(x, g1, comm.to_cast(), comm.carry("rms_fwd"))
    comm.cast_done(casts)
    comm.took("rms_fwd", xres)
    w_in = comm.weight(0)
    proj, xres = _in_proj(h, w_in, tab, comm.carry("in_proj"))
    comm.took("in_proj", xres)
    qkvs, o_parts, lse_parts = [], [], []
    for gi, d in enumerate(DILATIONS):
        qkv = proj if d == 1 else _qkv_to_sub(proj, d, gi)
        (o_g, lse_g), xres = _attn_fwd(qkv, d, gi, comm.carry(f"attn_fwd_g{gi}"))
        comm.took(f"attn_fwd_g{gi}", xres)
        qkvs.append(qkv)
        o_parts.append(o_g)
        lse_parts.append(lse_g)
    att, lse_views = _attn_merge(o_parts, lse_parts)
    (yrin, rn, rstd, states), xres = _ret_fwd(proj, consts, comm.carry("ret_fwd"))
    comm.took("ret_fwd", xres)
    wa, wr, wo = comm.weight(1), comm.weight(2), comm.weight(3)
    (merged, ya, yr, x1, h2), xres = _mix_out(att, yrin, proj, wa, wr, wo, x, g2, comm.carry("mix_out"))
    comm.took("mix_out", xres)
    wg, wu = comm.weight(4), comm.weight(5)
    (gte, up, act), xres = _ffn_up(h2, wg, wu, comm.carry("ffn_up"))
    comm.took("ffn_up", xres)
    wd = comm.weight(6)
    dx2, dx2b, dg3, loss_p = _ffn_down_loss(act, wd, x1, g3, tgt)

    dgte, dup, dx1, dx1b, dg2 = _ffn_bwd(dx2b, dx2, wd, wg, wu, gte, up, x1, g2)
    tok3 = lambda w: (lambda tk: pl.BlockSpec((None, tk, w), lambda p, k: (p, k, 0)))
    tok2 = lambda w: (lambda tk: pl.BlockSpec((tk, w), lambda p, k: (k, 0)))
    g_d = _wgrad("wgrad_down", act, dx2b, tok3(HID_S), tok2(D_MODEL), (N_SHARD, HID_S, D_MODEL),
                 pl.BlockSpec((None, HID_S, D_MODEL), lambda p, k: (p, 0, 0)), N_SHARD, S)
    g_g = _wgrad("wgrad_gate", dgte, h2, tok3(HID_S), tok2(D_MODEL), (N_SHARD, HID_S, D_MODEL),
                 pl.BlockSpec((None, HID_S, D_MODEL), lambda p, k: (p, 0, 0)), N_SHARD, S)
    g_u = _wgrad("wgrad_up", dup, h2, tok3(HID_S), tok2(D_MODEL), (N_SHARD, HID_S, D_MODEL),
                 pl.BlockSpec((None, HID_S, D_MODEL), lambda p, k: (p, 0, 0)), N_SHARD, S)
    comm.grads({4: g_g, 5: g_u, 6: g_d})
    (dya, dyr, dproj, dyrin, datt_views, rho_views), xres = _mix_bwd(dx1b, wo, proj, ya, yr, wa, wr, att,
                                                                       comm.carry("mix_bwd"))
    comm.took("mix_bwd", xres)
    colblk = lambda w: (lambda tk: pl.BlockSpec((tk, w), lambda p, k: (k, p)))
    g_o = _wgrad("wgrad_out", merged, dx1b, colblk(256), tok2(D_MODEL), (D_MODEL, D_MODEL),
                 pl.BlockSpec((256, D_MODEL), lambda p, k: (p, 0)), 4, S)
    g_a = _wgrad("wgrad_attn", att, dya, tok2(512), colblk(512), (512, D_MODEL),
                 pl.BlockSpec((512, 512), lambda p, k: (0, p)), 2, S)
    g_r = _wgrad("wgrad_ret", yrin, dyr, colblk(256), tok2(D_MODEL), (D_MODEL, D_MODEL),
                 pl.BlockSpec((256, D_MODEL), lambda p, k: (p, 0)), 4, S)
    comm.grads({1: g_a, 2: g_r.reshape(N_SHARD, 256, D_MODEL), 3: g_o.reshape(N_SHARD, 256, D_MODEL)})
    dproj, xres = _ret_bwd(proj, rn, rstd, dyrin, states, tab, consts, dproj, comm.carry("ret_bwd"))
    comm.took("ret_bwd", xres)
    dqs, dks, dvs = [], [], []
    for gi, d in enumerate(DILATIONS):
        rtab = jnp.asarray(tab_np[0].reshape(3, S // d, d * 128))
        (dq, dk, dv), xres = _attn_bwd(qkvs[gi], datt_views[gi], lse_views[gi], rho_views[gi], rtab, d, gi,
                                       comm.carry(f"attn_bwd_g{gi}"))
        comm.took(f"attn_bwd_g{gi}", xres)
        dqs.append(dq)
        dks.append(dk)
        dvs.append(dv)
    dproj = _assemble_dproj((dqs, dks, dvs), dproj)
    g_sent, xres = _wgrad_in_half(ht, dproj, comm.sidx, False, comm.carry("wgrad_in_sent"))
    comm.took("wgrad_in_sent", xres)
    comm.grads({"in_sent": g_sent})
    g_kept, xres = _wgrad_in_half(ht, dproj, comm.sidx, True, comm.carry("wgrad_in_kept"))
    comm.grads({"in_kept": g_kept})
    comm.took("wgrad_in_kept", xres)
    grad_x, dg1, xres = _in_proj_bwd(dproj, w_in, x, g1, dx1, comm.carry("in_proj_bwd"))
    comm.took("in_proj_bwd", xres)
    return loss_p, grad_x, (dg1, dg2, dg3)


W_KINDS = ("col", "col", "lead", "lead", "lead", "lead", "lead")
W_SHARD = ((1024, W_IN_S), (512, 256), (256, 1024), (256, 1024), (HID_S, 1024), (HID_S, 1024), (HID_S, 1024))
W_TRANSPOSED = (4, 5)
N_W = len(W_KINDS)


def _full_shape(wi):
    R, C = W_SHARD[wi]
    return (R, N_SHARD * C) if W_KINDS[wi] == "col" else (N_SHARD, R, C)


def _view(ref, wi, s, half):
    R, C = W_SHARD[wi]
    rows = pl.ds(half * (R // 2), R // 2)
    if W_KINDS[wi] == "col":
        return ref.at[rows, pl.ds(pl.multiple_of(s * C, 128), C)]
    return ref.at[s, rows, :]


def _mesh_pos():
    x, y, c = lax.axis_index("x"), lax.axis_index("y"), lax.axis_index("c")
    chips = [(1 - x, y), (x, 1 - y), (1 - x, 1 - y)]
    return x, y, c, chips


def _cast_bf16(a):
    R, C = a.shape
    tr = R // 2 if R % 32 == 0 else R

    def body(a_ref, o_ref):
        o_ref[...] = a_ref[...].astype(BF16)

    spec = pl.BlockSpec((tr, C), lambda i: (i, 0))
    return pl.pallas_call(body, out_shape=SDS((R, C), BF16), grid=(R // tr,), in_specs=[spec], out_specs=spec,
                          compiler_params=_cparams("parallel"), name=f"cast_{R}x{C}")(a)


def _remote(send, recv, k, src, dst, to):
    return pltpu.make_async_remote_copy(src_ref=src, dst_ref=dst, send_sem=send.at[k], recv_sem=recv.at[k],
                                        device_id=to, device_id_type=MESH)


def _ex_gather_ring(wis, shards):
    n = len(wis)

    def build(sh, full, send, recv, loc):
        x, y, c, _ = _mesh_pos()
        s_me, sib = 2 * x + y, (x, y, 1 - c)
        xn, yn = (1 - x, y), (x, 1 - y)
        flip = lambda a, b: a + b - 2 * a * b
        via = (flip(x, 1 - c), flip(y, c))
        onto = (flip(x, c), flip(y, 1 - c))
        shard_of = lambda chip: 2 * chip[0] + chip[1]
        starts, waits, sent = [], [], []
        for i, wi in enumerate(wis):
            Rh = W_SHARD[wi][0] // 2
            for hf in range(2):
                cp = pltpu.make_async_copy(sh[i].at[pl.ds(hf * Rh, Rh), :], _view(full[i], wi, s_me, hf), loc.at[2 * i + hf])
                starts.append(cp)
                sent.append(cp.wait)
            for j, chip in enumerate((xn, yn)):
                cp = _remote(send, recv, 6 * i + j, sh[i].at[pl.ds(c * Rh, Rh), :], _view(full[i], wi, s_me, c), (*chip, c))
                starts.append(cp)
                sent.append(cp.wait_send)

        def pass_to_sibling(i, wi, k, s):
            mine = _view(full[i], wi, s, c)
            fw = _remote(send, recv, 6 * i + k, mine, mine, sib)
            waits.append(fw.start)
            sent.append(fw.wait_send)

        for i, wi in enumerate(wis):
            for j, chip in enumerate((xn, yn)):
                land = _view(full[i], wi, shard_of(chip), c)
                waits.append(_remote(send, recv, 6 * i + j, land, land, (*chip, c)).wait_recv)
                pass_to_sibling(i, wi, 3 + j, shard_of(chip))
            relay = _view(full[i], wi, shard_of(via), c)
            fw = _remote(send, recv, 6 * i + 2, relay, relay, (*onto, c))
            waits.append(fw.start)
            sent.append(fw.wait_send)
        s_diag = 2 * (1 - x) + (1 - y)
        for i, wi in enumerate(wis):
            land = _view(full[i], wi, s_diag, c)
            waits.append(_remote(send, recv, 6 * i + 2, land, land, (*onto, c)).wait_recv)
            pass_to_sibling(i, wi, 5, s_diag)
        for i, wi in enumerate(wis):
            for k, s in ((3, shard_of(xn)), (4, shard_of(yn)), (5, s_diag)):
                land = _view(full[i], wi, s, 1 - c)
                waits.append(_remote(send, recv, 6 * i + k, land, land, sib).wait_recv)
        return starts, waits + sent

    return _Exchange(shards, [SDS(_full_shape(wi), BF16) for wi in wis], {}, 6 * n, 2 * n, build)


def _ex_gather_ici(wis, shards, then_d2d=False):
    n = len(wis)

    def build(ins, outs, send, recv, loc):
        x, y, c, chips = _mesh_pos()
        s_me, sib = 2 * x + y, (x, y, 1 - c)
        starts, waits, after = [], [], []
        for i, wi in enumerate(wis):
            Rh = W_SHARD[wi][0] // 2
            for hf in range(2):
                cp = pltpu.make_async_copy(ins[i].at[pl.ds(hf * Rh, Rh), :], _view(outs[i], wi, s_me, hf), loc.at[2 * i + hf])
                starts.append(cp)
                waits.append(cp.wait)
            for j, chip in enumerate(chips):
                cp = _remote(send, recv, 3 * i + j, ins[i].at[pl.ds(c * Rh, Rh), :], _view(outs[i], wi, s_me, c), (*chip, c))
                land = _view(outs[i], wi, 2 * chip[0] + chip[1], c)
                starts.append(cp)
                waits += [cp.wait_send, _remote(send, recv, 3 * i + j, land, land, (*chip, c)).wait_recv]
                if then_d2d:
                    theirs = _view(outs[i], wi, 2 * chip[0] + chip[1], 1 - c)
                    fw = _remote(send, recv, 3 * n + 3 * i + j, land, land, sib)
                    waits.append(fw.start)
                    after += [fw.wait_send, _remote(send, recv, 3 * n + 3 * i + j, theirs, theirs, sib).wait_recv]
        return starts, waits + after

    return _Exchange(shards, [SDS(_full_shape(wi), BF16) for wi in wis], {}, (6 if then_d2d else 3) * n, 2 * n, build)


def _ex_gather_d2d(wis, fulls):
    def build(ins, outs, send, recv, loc):
        x, y, c, chips = _mesh_pos()
        sib = (x, y, 1 - c)
        starts, waits = [], []
        for i, wi in enumerate(wis):
            for j, chip in enumerate(chips):
                mine = _view(outs[i], wi, 2 * chip[0] + chip[1], c)
                theirs = _view(outs[i], wi, 2 * chip[0] + chip[1], 1 - c)
                cp = _remote(send, recv, 3 * i + j, mine, mine, sib)
                starts.append(cp)
                waits += [cp.wait_send, _remote(send, recv, 3 * i + j, theirs, theirs, sib).wait_recv]
        return starts, waits

    return _Exchange(fulls, [SDS(f.shape, BF16) for f in fulls], {i: i for i in range(len(wis))}, 3 * len(wis), 0, build)


def _half_shape(wi):
    R, C = W_SHARD[wi]
    return (R // 2, N_SHARD * C) if W_KINDS[wi] == "col" else (N_SHARD, R // 2, C)


def _ex_pair(wis, grads):
    def build(ins, outs, send, recv, loc):
        x, y, c, _ = _mesh_pos()
        starts, waits = [], []
        for i, wi in enumerate(wis):
            Rh = W_SHARD[wi][0] // 2
            rows = pl.ds((1 - c) * Rh, Rh)
            if tuple(ins[i].shape) == _half_shape(wi):
                src = ins[i]
            else:
                src = ins[i].at[rows, :] if W_KINDS[wi] == "col" else ins[i].at[:, rows, :]
            cp = _remote(send, recv, i, src, outs[i], (x, y, 1 - c))
            starts.append(cp)
            waits.append(cp.wait)
        return starts, waits

    return _Exchange(grads, [SDS(_half_shape(wi), F32) for wi in wis], {}, len(wis), 0, build)


def _ex_chip(wis, pbs):
    def build(ins, outs, send, recv, loc):
        x, y, c, chips = _mesh_pos()
        starts, waits = [], []
        for i, wi in enumerate(wis):
            for j, chip in enumerate(chips):
                cp = _remote(send, recv, 3 * i + j, ins[i].at[j], outs[i].at[j], (*chip, c))
                starts.append(cp)
                waits.append(cp.wait)
        return starts, waits

    shapes = [SDS((3, W_SHARD[wi][0] // 2, W_SHARD[wi][1]), BF16) for wi in wis]
    return _Exchange(pbs, shapes, {}, 3 * len(wis), 0, build)


def _ex_share(wis, halves):
    def build(ins, outs, send, recv, loc):
        x, y, c, _ = _mesh_pos()
        sib = (x, y, 1 - c)
        starts, waits = [], []
        for i, wi in enumerate(wis):
            cp = _remote(send, recv, i, outs[i].at[c], outs[i].at[c], sib)
            starts.append(cp)
            waits += [cp.wait_send, _remote(send, recv, i, outs[i].at[1 - c], outs[i].at[1 - c], sib).wait_recv]
        return starts, waits

    return _Exchange(halves, [SDS(h.shape, F32) for h in halves], {i: i for i in range(len(wis))}, len(wis), 0, build)


def _row_tile(rh, C):
    best = 16
    for t in range(16, rh + 1, 16):
        if rh % t == 0 and t * C * 4 <= (3 << 19):
            best = t
    return best


def _pair_sum(wi, g, ra, sidx):
    R, C = W_SHARD[wi]
    Rh = R // 2
    tr = _row_tile(Rh, C)
    nt = Rh // tr
    off = 0 if tuple(g.shape) == _half_shape(wi) else nt
    col = W_KINDS[wi] == "col"

    def body(sidx_ref, *refs):
        gs, rs = refs[:4], refs[4:8]
        own_ref, pb_ref = refs[8:]
        own_ref[...] = gs[0][...] + rs[0][...]
        for j in range(3):
            pb_ref[j] = (gs[1 + j][...] + rs[1 + j][...]).astype(BF16)

    def gspec(slot):
        if col:
            return pl.BlockSpec((tr, C), lambda i, sx: (sx[4] * off + i, sx[slot]))
        return pl.BlockSpec((None, tr, C), lambda i, sx: (sx[slot], sx[4] * off + i, 0))

    def rspec(slot):
        if col:
            return pl.BlockSpec((tr, C), lambda i, sx: (i, sx[slot]))
        return pl.BlockSpec((None, tr, C), lambda i, sx: (sx[slot], i, 0))

    return pl.pallas_call(
        body, out_shape=(SDS((Rh, C), F32), SDS((3, Rh, C), BF16)),
        grid_spec=pltpu.PrefetchScalarGridSpec(
            num_scalar_prefetch=1, grid=(nt,),
            in_specs=[gspec(k) for k in range(4)] + [rspec(k) for k in range(4)],
            out_specs=(pl.BlockSpec((tr, C), lambda i, sx: (i, 0)), pl.BlockSpec((3, tr, C), lambda i, sx: (0, i, 0)))),
        compiler_params=_cparams("arbitrary"), name=f"pair_sum_w{wi}")(sidx, g, g, g, g, ra, ra, ra, ra)


def _chip_sum(wi, own, rb, sidx):
    R, C = W_SHARD[wi]
    Rh = R // 2
    tr = _row_tile(Rh, C)

    def body(sidx_ref, own_ref, rb_ref, o_ref):
        o_ref[...] = ((own_ref[...] + rb_ref[0].astype(F32)) + rb_ref[1].astype(F32)) + rb_ref[2].astype(F32)

    return pl.pallas_call(
        body, out_shape=SDS((2, Rh, C), F32),
        grid_spec=pltpu.PrefetchScalarGridSpec(
            num_scalar_prefetch=1, grid=(Rh // tr,),
            in_specs=[pl.BlockSpec((tr, C), lambda i, sx: (i, 0)), pl.BlockSpec((3, tr, C), lambda i, sx: (0, i, 0))],
            out_specs=pl.BlockSpec((None, tr, C), lambda i, sx: (sx[4], i, 0))),
        compiler_params=_cparams("arbitrary"), name=f"chip_sum_w{wi}")(sidx, own, rb)


def _gain_allgather(blk, ex):
    m_per, n = blk.shape
    n_in, n_out = len(ex.ins), len(ex.out_shapes)

    def body(x_ref, *rest):
        xin, out_ref, xout = rest[:n_in], rest[n_in], rest[n_in + 1:n_in + 1 + n_out]
        send_sems, recv_sems, local_sem = rest[n_in + 1 + n_out:n_in + 4 + n_out]
        ex_starts, ex_waits = ex.build(xin, xout, *rest[n_in + 4 + n_out:])
        for cp in ex_starts:
            cp.start()
        x, y, c, chips = _mesh_pos()
        me, sibling = (x, y, c), (x, y, 1 - c)

        def rows(px, py, pc):
            return out_ref.at[pl.ds((4 * px + 2 * py + pc) * m_per, m_per), :]

        def copy(k, block, to, src=None):
            return pltpu.make_async_remote_copy(
                src_ref=rows(*block) if src is None else src, dst_ref=rows(*block),
                send_sem=send_sems.at[k], recv_sem=recv_sems.at[k], device_id=to, device_id_type=MESH)

        mine = pltpu.make_async_copy(x_ref, rows(*me), local_sem)
        mine.start()
        first = [copy(0, me, sibling, src=x_ref)]
        first += [copy(1 + j, me, (*chip, c), src=x_ref) for j, chip in enumerate(chips)]
        for cp in first:
            cp.start()
        passed = [copy(4 + j, (*chip, c), sibling) for j, chip in enumerate(chips)]
        for j, chip in enumerate(chips):
            copy(1 + j, (*chip, c), me).wait_recv()
            passed[j].start()
        copy(0, sibling, me).wait_recv()
        for j, chip in enumerate(chips):
            copy(4 + j, (*chip, 1 - c), me).wait_recv()
        for cp in first + passed:
            cp.wait_send()
        mine.wait()
        for w in ex_waits:
            w()

    vm = pl.BlockSpec(memory_space=pltpu.VMEM)
    res = pl.pallas_call(
        body, out_shape=(SDS((8 * m_per, n), blk.dtype), *ex.out_shapes),
        in_specs=[vm] + [ANY] * n_in, out_specs=(vm, *[ANY] * n_out),
        input_output_aliases={1 + a: 1 + o for a, o in ex.aliases.items()},
        scratch_shapes=[pltpu.SemaphoreType.DMA((7,)), pltpu.SemaphoreType.DMA((7,)), pltpu.SemaphoreType.DMA] + ex.sems(),
        name="gain_allgather")(blk, *ex.ins)
    return res[0], tuple(res[1:])


def _adam_math(w, g, m, v):
    mn = ADAM_B1 * m + (1.0 - ADAM_B1) * g
    vn = ADAM_B2 * v + (1.0 - ADAM_B2) * (g * g)
    mh = mn / (1.0 - ADAM_B1 ** ADAM_STEP)
    vh = vn / (1.0 - ADAM_B2 ** ADAM_STEP)
    return -ADAM_LR * (mh / (jnp.sqrt(vh) + ADAM_EPS) + ADAM_WD * w), mn, vn


def _adamw(name, ws, gs, ms, vs):
    n, steps = len(ws), 8

    def body(*refs):
        for k in range(n):
            w_ref, g_ref, m_ref, v_ref = refs[4 * k:4 * k + 4]
            go_ref, d_ref, mn_ref, vn_ref = refs[4 * n + 4 * k:4 * n + 4 * k + 4]
            g = g_ref[...]
            go_ref[...] = g
            d_ref[...], mn_ref[...], vn_ref[...] = _adam_math(w_ref[...], g, m_ref[...], v_ref[...])

    specs = [pl.BlockSpec((w.shape[0] // steps, w.shape[1]), lambda i: (i, 0)) for w in ws for _ in range(4)]
    res = pl.pallas_call(
        body, out_shape=tuple(SDS(w.shape, F32) for w in ws for _ in range(4)), grid=(steps,),
        in_specs=specs, out_specs=tuple(specs), compiler_params=_cparams("parallel"),
        name=name)(*[a for k in range(n) for a in (ws[k], gs[k], ms[k], vs[k])])
    return [tuple(res[4 * k:4 * k + 4]) for k in range(n)]


def _gain_update(gathered, w, m, v):
    def body(ga_ref, w_ref, m_ref, v_ref, g_ref, d_ref, mn_ref, vn_ref):
        g = ga_ref[0:8, :]
        for dev in range(1, 8):
            g = g + ga_ref[8 * dev:8 * dev + 8, :]
        g_ref[...] = g
        d_ref[...], mn_ref[...], vn_ref[...] = _adam_math(w_ref[...], g, m_ref[...], v_ref[...])

    return pl.pallas_call(body, out_shape=(SDS((8, 1024), F32),) * 4, name="gain_update")(gathered, w, m, v)


GROUP_FFN, GROUP_MIX, GROUP_IN = (4, 5, 6), (1, 2, 3), (0,)
REST = GROUP_MIX + GROUP_FFN


class _MeshComm:
    SCHEDULE = {
        "rms_fwd": [("ring", GROUP_IN)],
        "in_proj": [("ici", (1, 2, 3, 4))],
        "ret_fwd": [("d2d", (1, 2, 3, 4)), ("ici", (5,))],
        "mix_out": [("d2d", (5,))],
        "ffn_up": [("both", (6,))],
        "mix_bwd": [("pair", GROUP_FFN)],
        "ret_bwd": [("pair", GROUP_MIX), ("chip", (4,))],
        "attn_bwd_g0": [("chip", (5,))],
        "attn_bwd_g1": [("chip", (6,))],
        "attn_bwd_g2": [("chip", GROUP_MIX)],
        "wgrad_in_kept": [("pair", GROUP_IN), ("share", GROUP_FFN + GROUP_MIX)],
        "in_proj_bwd": [("chip", GROUP_IN)],
    }

    def __init__(self, w_in_shard, rest_f32):
        xi, yi, ci = lax.axis_index("x"), lax.axis_index("y"), lax.axis_index("c")
        self.sidx = jnp.stack([2 * xi + yi, 2 * (1 - xi) + yi, 2 * xi + (1 - yi), 2 * (1 - xi) + (1 - yi), ci]).astype(jnp.int32)
        self.shards, self.rest_f32, self.full = {0: w_in_shard}, list(rest_f32), {}
        self.g, self.own, self.pb, self.half, self.red = {}, {}, {}, {}, {}

    def to_cast(self):
        return self.rest_f32

    def cast_done(self, casts):
        self.shards.update(zip(REST, casts))

    def weight(self, wi):
        return self.full[wi].reshape(D_MODEL, D_MODEL) if wi in (2, 3) else self.full[wi]

    def grads(self, by_wi):
        self.g.update(by_wi)

    def _exchange(self, stage, wis):
        pick = lambda table: [table[wi] for wi in wis]
        if stage == "ring":
            return _ex_gather_ring(wis, pick(self.shards))
        if stage == "ici":
            return _ex_gather_ici(wis, pick(self.shards))
        if stage == "both":
            return _ex_gather_ici(wis, pick(self.shards), then_d2d=True)
        if stage == "d2d":
            return _ex_gather_d2d(wis, pick(self.full))
        if stage == "pair":
            return _ex_pair(wis, [self.g["in_sent"] if wi == 0 else self.g[wi] for wi in wis])
        if stage == "chip":
            return _ex_chip(wis, pick(self.pb))
        return _ex_share(wis, pick(self.half))

    def _landed(self, stage, wis, res):
        for wi, r in zip(wis, res):
            if stage in ("ring", "ici", "d2d", "both"):
                self.full[wi] = r
            elif stage == "pair":
                self.own[wi], self.pb[wi] = _pair_sum(wi, self.g["in_kept"] if wi == 0 else self.g[wi], r, self.sidx)
            elif stage == "chip":
                self.half[wi] = _chip_sum(wi, self.own[wi], r, self.sidx)
            else:
                self.red[wi] = r

    def carry(self, point):
        return [self._exchange(stage, wis) for stage, wis in self.SCHEDULE.get(point, ())]

    def took(self, point, xres):
        for (stage, wis), res in zip(self.SCHEDULE.get(point, ()), xres):
            self._landed(stage, wis, res)

    def last_share(self):
        return self._exchange("share", GROUP_IN)

    def reduced(self, last_shared):
        self._landed("share", GROUP_IN, last_shared)
        return [self.red[wi] for wi in range(N_W)]


def kernel(x, norm_mix_g, w_in, w_out_attn, w_out_ret, w_out, norm_ffn_g, w_ffn_gate, w_ffn_up, w_ffn_down, norm_final_g, loss_target, m_norm_mix_g, m_w_in, m_w_out_attn, m_w_out_ret, m_w_out, m_norm_ffn_g, m_w_ffn_gate, m_w_ffn_up, m_w_ffn_down, m_norm_final_g, v_norm_mix_g, v_w_in, v_w_out_attn, v_w_out_ret, v_w_out, v_norm_ffn_g, v_w_ffn_gate, v_w_ffn_up, v_w_ffn_down, v_norm_final_g):
    ws = (w_in, w_out_attn, w_out_ret, w_out, w_ffn_gate, w_ffn_up, w_ffn_down)
    ms = (m_w_in, m_w_out_attn, m_w_out_ret, m_w_out, m_w_ffn_gate, m_w_ffn_up, m_w_ffn_down)
    vs = (v_w_in, v_w_out_attn, v_w_out_ret, v_w_out, v_w_ffn_gate, v_w_ffn_up, v_w_ffn_down)

    def shard2d(a, wi):
        return jnp.swapaxes(a[0], 0, 1) if wi in W_TRANSPOSED else a.reshape(W_SHARD[wi])

    def as_given(a2d, wi):
        return jnp.swapaxes(a2d, 0, 1)[None] if wi in W_TRANSPOSED else a2d.reshape(ws[wi].shape)

    comm = _MeshComm(_cast_bf16(shard2d(ws[0], 0)), [shard2d(ws[wi], wi) for wi in REST])
    g3 = norm_final_g.reshape(1, D_MODEL)
    loss_p, grad_x, gain_g = _step(x[0], loss_target[0], norm_mix_g, norm_ffn_g, g3, comm)

    pad8 = lambda rows: jnp.concatenate([r.reshape(1, D_MODEL) for r in rows]
                                        + [jnp.zeros((8 - len(rows), D_MODEL), F32)], axis=0)
    gathered, shared = _gain_allgather(pad8((*gain_g, jnp.tile(loss_p[0:1], (1, D_MODEL // 128)))), comm.last_share())
    gred = comm.reduced(shared)

    def adam(name, wis):
        two_d = lambda arrs: [shard2d(arrs[wi], wi) for wi in wis]
        return _adamw(name, two_d(ws), [gred[wi].reshape(W_SHARD[wi]) for wi in wis], two_d(ms), two_d(vs))

    updates = dict(zip(REST + GROUP_IN, adam("adamw_rest", REST) + adam("adamw_w_in", GROUP_IN)))
    outs_g, outs_d, outs_m, outs_v = ([as_given(updates[wi][k], wi) for wi in range(N_W)] for k in range(4))

    gg, gd, gm, gv = _gain_update(gathered, pad8((norm_mix_g, norm_ffn_g, norm_final_g)),
                                  pad8((m_norm_mix_g, m_norm_ffn_g, m_norm_final_g)),
                                  pad8((v_norm_mix_g, v_norm_ffn_g, v_norm_final_g)))
    loss = gg[3, 0]

    def assemble(gain_rows, wlist):
        return (gain_rows[0:1], wlist[0], wlist[1], wlist[2], wlist[3], gain_rows[1:2],
                wlist[4], wlist[5], wlist[6], gain_rows[2])

    return (loss, grad_x[None], *assemble(gg, outs_g), *assemble(gd, outs_d), *assemble(gm, outs_m), *assemble(gv, outs_v))
```

```python
import functools

import numpy as np
import jax
import jax.numpy as jnp
from jax import lax
from jax.experimental import pallas as pl
from jax.experimental.pallas import tpu as pltpu

F32, BF16 = jnp.float32, jnp.bfloat16
SDS = jax.ShapeDtypeStruct
MESH = pl.DeviceIdType.MESH

D_MODEL = 1024
PROJ_W = 9728
COLB = 512
N_COLB = PROJ_W // COLB
QA_B, KA_B, VA_B = 0, 3, 6
QR_B, KR_B = 9, 10
FFN_HID = 2816
N_SHARD = 4
HID_S = FFN_HID // N_SHARD
W_IN_S = PROJ_W // N_SHARD
DILATIONS = (1, 4, 16)
BLK = 128
RET_HEADS = 4
ROPE_THETA = 10000.0
NORM_EPS = 1e-6
ADAM_LR, ADAM_B1, ADAM_B2, ADAM_EPS, ADAM_WD, ADAM_STEP = 0.001, 0.9, 0.999, 1e-08, 0.01, 10
VMEM_LIMIT = 56 << 20


def _cparams(*sem):
    return pltpu.CompilerParams(dimension_semantics=sem or None, vmem_limit_bytes=VMEM_LIMIT)


def _dot(a, b):
    return jnp.dot(a, b, preferred_element_type=F32)


def _dot_nt(a, b):
    return lax.dot_general(a, b, (((1,), (1,)), ((), ())), preferred_element_type=F32)


def _dot_tn(a, b):
    return lax.dot_general(a, b, (((0,), (0,)), ((), ())), preferred_element_type=F32)


def _row_pieces(tm, sub=512):
    return [slice(i, i + sub) for i in range(0, tm, sub)]


def _sigmoid(z):
    return 0.5 * jnp.tanh(0.5 * z) + 0.5


ANY = pl.BlockSpec(memory_space=pl.ANY)


class _Exchange:
    def __init__(self, ins, out_shapes, aliases, n_sem, n_loc, build):
        self.ins, self.out_shapes, self.aliases = list(ins), list(out_shapes), dict(aliases)
        self.n_sem, self.n_loc, self.build = n_sem, n_loc, build

    def sems(self):
        return [pltpu.SemaphoreType.DMA((self.n_sem,)), pltpu.SemaphoreType.DMA((self.n_sem,)),
                pltpu.SemaphoreType.DMA((max(self.n_loc, 1),))]


def _carrier_call(body, args, *, out_shape, grid, in_specs, out_specs, scratch_shapes=(), sem, name, exchanges=(),
                  prefetch=None, in_out_aliases=None):
    out_shape, out_specs = tuple(out_shape), tuple(out_specs)
    n_in, n_out, n_scr = len(args), len(out_shape), len(scratch_shapes)
    n_pre = 0 if prefetch is None else 1
    x_args, x_outs, x_scr, spans = [], [], [], []
    aliases = {n_pre + a: o for a, o in (in_out_aliases or {}).items()}
    for ex in exchanges:
        i0, o0 = len(x_args), len(x_outs)
        for a, o in ex.aliases.items():
            aliases[n_pre + n_in + i0 + a] = n_out + o0 + o
        x_args += ex.ins
        x_outs += ex.out_shapes
        x_scr += ex.sems()
        spans.append((i0, len(ex.ins), o0, len(ex.out_shapes)))
    nx_in, nx_out = len(x_args), len(x_outs)

    def wrapped(*refs):
        refs = refs[n_pre:]
        ins, xin = refs[:n_in], refs[n_in:n_in + nx_in]
        o_base = n_in + nx_in
        outs, xout = refs[o_base:o_base + n_out], refs[o_base + n_out:o_base + n_out + nx_out]
        s_base = o_base + n_out + nx_out
        scr, xs = refs[s_base:s_base + n_scr], refs[s_base + n_scr:]

        def built(e):
            i0, ni, o0, no = spans[e]
            return exchanges[e].build(xin[i0:i0 + ni], xout[o0:o0 + no], *xs[3 * e:3 * e + 3])

        if exchanges:
            first = functools.reduce(jnp.logical_and, [pl.program_id(k) == 0 for k in range(len(grid))])
            last = functools.reduce(jnp.logical_and, [pl.program_id(k) == grid[k] - 1 for k in range(len(grid))])

            @pl.when(first)
            def _():
                for e in range(len(exchanges)):
                    for cp in built(e)[0]:
                        cp.start()

        body(*ins, *outs, *scr)

        if exchanges:
            @pl.when(last)
            def _():
                for e in range(len(exchanges)):
                    for w in built(e)[1]:
                        w()

    all_in, all_out = list(in_specs) + [ANY] * nx_in, out_specs + tuple([ANY] * nx_out)
    all_scr = list(scratch_shapes) + x_scr
    cparams = _cparams(*(sem if not exchanges else ("arbitrary",) * len(grid)))
    if prefetch is None:
        res = pl.pallas_call(wrapped, out_shape=out_shape + tuple(x_outs), grid=grid, in_specs=all_in, out_specs=all_out,
                             scratch_shapes=all_scr, input_output_aliases=aliases, compiler_params=cparams,
                             name=name)(*args, *x_args)
    else:
        gs = pltpu.PrefetchScalarGridSpec(num_scalar_prefetch=1, grid=grid, in_specs=all_in, out_specs=all_out,
                                          scratch_shapes=all_scr)
        res = pl.pallas_call(wrapped, out_shape=out_shape + tuple(x_outs), grid_spec=gs, input_output_aliases=aliases,
                             compiler_params=cparams, name=name)(prefetch, *args, *x_args)
    xres = [tuple(res[n_out + o0:n_out + o0 + no]) for (_, _, o0, no) in spans]
    return tuple(res[:n_out]), xres


def _tables(S):
    f32 = np.float32
    pos = np.arange(S, dtype=f32)
    lane = np.arange(128)
    inv = (f32(ROPE_THETA) ** (-np.arange(0, 64, 2, dtype=f32) / f32(64))).astype(f32)
    ang = (pos[:, None] * inv[None, :]).astype(np.float64)
    idx = (lane % 64) % 32
    c, s = np.cos(ang)[:, idx], np.sin(ang)[:, idx]
    first = ((lane % 64) < 32)[None, :]
    rope = np.stack([c, np.where(first, 0.0, s), np.where(first, -s, 0.0)])
    base = (f32(1.0) / (f32(ROPE_THETA) ** np.linspace(0.0, 1.0, 64, dtype=f32))).astype(f32)
    ang2 = (pos[:, None] * base[None, :]).astype(np.float64)
    c2, s2 = np.cos(ang2)[:, lane // 2], np.sin(ang2)[:, lane // 2]
    even = (lane % 2 == 0)[None, :]
    th = np.stack([c2, np.where(even, 0.0, s2), np.where(even, -s2, 0.0)])
    return np.stack([rope, th, th * (128 ** -0.5)]).astype(f32)


def _rot(a, c, sa, sb, shift):
    return a * c + pltpu.roll(a, shift, 1) * sa + pltpu.roll(a, 128 - shift, 1) * sb


def _unrot(g, c, sa, sb, shift):
    return g * c + pltpu.roll(g * sa, 128 - shift, 1) + pltpu.roll(g * sb, shift, 1)


def _ret_consts():
    h = np.arange(RET_HEADS, dtype=np.float64)
    log_g = np.log1p(-(2.0 ** (-5.0 - h)))
    idx = np.arange(BLK, dtype=np.float64)
    diff = idx[:, None] - idx[None, :]
    dmask = np.where(diff[None] >= 0, np.exp(np.maximum(diff, 0.0)[None] * log_g[:, None, None]), 0.0)
    zeta = np.exp((BLK - 1 - idx)[None, :] * log_g[:, None])
    xi = np.exp((idx + 1.0)[None, :] * log_g[:, None])
    dec = np.exp(BLK * log_g)
    rep = lambda v: np.broadcast_to(v[:, :, None], (RET_HEADS, BLK, 128))
    return (jnp.asarray(dmask, F32), jnp.asarray(rep(zeta), F32), jnp.asarray(rep(xi), F32),
            jnp.asarray(np.broadcast_to(dec[:, None, None], (RET_HEADS, 8, 256)), F32))


def _rms_fwd(x, g, to_cast=(), exchanges=()):
    S = x.shape[0]
    steps = 4
    tm = S // steps
    n_c = len(to_cast)

    def body(x_ref, g_ref, *refs):
        c_in, (h_ref, ht_ref), c_out = refs[:n_c], refs[n_c:n_c + 2], refs[n_c + 2:]
        for rows in _row_pieces(tm, 512):
            xv = x_ref[rows, :]
            r = lax.rsqrt(jnp.mean(xv * xv, axis=-1, keepdims=True) + NORM_EPS)
            h = xv * r * g_ref[...]
            h_ref[rows, :] = h.astype(BF16)
            ht_ref[:, rows] = h.T.astype(BF16)
        for a_ref, o_ref in zip(c_in, c_out):
            o_ref[...] = a_ref[...].astype(BF16)

    slab = lambda a: pl.BlockSpec((a.shape[0] // steps, a.shape[1]), lambda i: (i, 0))
    return _carrier_call(
        body, (x, g, *to_cast),
        out_shape=(SDS((S, D_MODEL), BF16), SDS((D_MODEL, S), BF16), *[SDS(a.shape, BF16) for a in to_cast]),
        grid=(steps,),
        in_specs=[pl.BlockSpec((tm, D_MODEL), lambda i: (i, 0)), pl.BlockSpec((1, D_MODEL), lambda i: (0, 0))]
        + [slab(a) for a in to_cast],
        out_specs=(pl.BlockSpec((tm, D_MODEL), lambda i: (i, 0)), pl.BlockSpec((D_MODEL, tm), lambda i: (0, i)),
                   *[slab(a) for a in to_cast]),
        sem=("parallel",), name="rms_fwd", exchanges=exchanges)


def _in_proj(h, w_in, tab, exchanges=()):
    S = h.shape[0]
    tm = min(S, 4096)

    def body(h_ref, w_ref, t_ref, o_ref):
        j = pl.program_id(1)
        is_rope = j < 6
        is_theta = (j == QR_B) | (j == KR_B)
        sub = 512

        def rotated(shift):
            for i in range(tm // sub):
                rows = slice(i * sub, (i + 1) * sub)
                acc = _dot(h_ref[rows, :], w_ref[...])
                c, sa, sb = t_ref[0, 0, rows, :], t_ref[0, 1, rows, :], t_ref[0, 2, rows, :]
                for k in range(COLB // 128):
                    sl = slice(k * 128, (k + 1) * 128)
                    o_ref[rows, sl] = _rot(acc[:, sl], c, sa, sb, shift).astype(BF16)

        @pl.when(is_rope)
        def _():
            rotated(32)

        @pl.when(is_theta)
        def _():
            rotated(1)

        @pl.when(jnp.logical_not(is_rope | is_theta))
        def _():
            o_ref[...] = _dot(h_ref[...], w_ref[...]).astype(BF16)

    def tab_map(i, j):
        return (jnp.where(j == QR_B, 1, jnp.where(j == KR_B, 2, 0)), 0, i, 0)

    (proj,), xres = _carrier_call(
        body, (h, w_in, tab), out_shape=(SDS((S, PROJ_W), BF16),), grid=(S // tm, N_COLB),
        in_specs=[pl.BlockSpec((tm, D_MODEL), lambda i, j: (i, 0)),
                  pl.BlockSpec((D_MODEL, COLB), lambda i, j: (0, j)),
                  pl.BlockSpec((1, 3, tm, 128), tab_map)],
        out_specs=(pl.BlockSpec((tm, COLB), lambda i, j: (i, j)),),
        sem=("parallel", "arbitrary"), name="in_proj", exchanges=exchanges)
    return proj, xres


def _band_mask(n):
    qi = lax.broadcasted_iota(jnp.int32, (BLK, 2 * BLK), 0)
    kj = lax.broadcasted_iota(jnp.int32, (BLK, 2 * BLK), 1)
    dist = BLK + qi - kj
    return (dist >= 0) & (dist <= BLK) & ((kj >= BLK) | (n > 0))


def _qkv_col(d, gi):
    if d == 1:
        return lambda t, r: 3 * t + gi
    return lambda t, r: 3 * r + t


def _attn_fwd(qkv, d, gi, exchanges=()):
    L = qkv.shape[0]
    nb = L // BLK
    qb = 2 if nb % 2 == 0 else 1

    def body(q_ref, kc_ref, kp_ref, vc_ref, vp_ref, o_ref, lse_ref):
        i = pl.program_id(1)
        lane = lax.broadcasted_iota(jnp.int32, (BLK, 128), 1)
        lo = lane < 64
        chunks = [slice(c * 128, (c + 1) * 128) for c in range(4)]
        scores, vals, masks = [], [], []
        for b in range(qb):
            rows = slice(b * BLK, (b + 1) * BLK)
            before = slice((b - 1) * BLK, b * BLK)
            mask = _band_mask(qb * i + b)
            masks.append(jnp.concatenate([mask, mask], axis=0))
            for sl in chunks:
                q = q_ref[rows, sl]
                k = jnp.concatenate([kp_ref[:, sl] if b == 0 else kc_ref[before, sl], kc_ref[rows, sl]], axis=0)
                vals.append(jnp.concatenate([vp_ref[:, sl] if b == 0 else vc_ref[before, sl], vc_ref[rows, sl]], axis=0))
                q2 = jnp.concatenate([jnp.where(lo, q, jnp.zeros_like(q)), jnp.where(lo, jnp.zeros_like(q), q)], axis=0)
                scores.append(_dot_nt(q2, k))
        probs, lses = [], []
        for j, s in enumerate(scores):
            b, c = divmod(j, 4)
            s = jnp.where(masks[b], s * 0.125, jnp.float32(-1e30))
            m = jnp.max(s, axis=-1, keepdims=True)
            p = jnp.exp(s - m)
            l = jnp.sum(p, axis=-1, keepdims=True)
            probs.append((p * (1.0 / l)).astype(BF16))
            lses.append(m + jnp.log(l))
        for j, (p, v) in enumerate(zip(probs, vals)):
            b, c = divmod(j, 4)
            o2 = _dot(p, v)
            o_ref[b * BLK:(b + 1) * BLK, chunks[c]] = jnp.where(lo, o2[:BLK], o2[BLK:])
        for b in range(qb):
            lse_all = jnp.zeros((BLK, 128), F32)
            for c in range(4):
                lse = lses[4 * b + c]
                lse_all = jnp.where(lane // 16 == 2 * c, lse[:BLK], jnp.where(lane // 16 == 2 * c + 1, lse[BLK:], lse_all))
            lse_ref[b * BLK:(b + 1) * BLK, :] = lse_all

    prev = lambda i: jnp.maximum(qb * i - 1, 0)
    col = _qkv_col(d, gi)
    return _carrier_call(
        body, (qkv,) * 5, out_shape=(SDS((L, d * 512), F32), SDS((L, d * 128), F32)), grid=(d, nb // qb),
        in_specs=[pl.BlockSpec((qb * BLK, 512), lambda r, i: (i, col(0, r))),
                  pl.BlockSpec((qb * BLK, 512), lambda r, i: (i, col(1, r))),
                  pl.BlockSpec((BLK, 512), lambda r, i: (prev(i), col(1, r))),
                  pl.BlockSpec((qb * BLK, 512), lambda r, i: (i, col(2, r))),
                  pl.BlockSpec((BLK, 512), lambda r, i: (prev(i), col(2, r)))],
        out_specs=(pl.BlockSpec((qb * BLK, 512), lambda r, i: (i, r)),
                   pl.BlockSpec((qb * BLK, 128), lambda r, i: (i, r))),
        sem=("parallel", "arbitrary"), name=f"attn_fwd_g{gi}", exchanges=exchanges)


def _qkv_to_sub(proj, d, gi):
    S = proj.shape[0]
    tm = 512
    n = tm // d

    def body(q_ref, k_ref, v_ref, o_ref, scr):
        for t, ref in enumerate((q_ref, k_ref, v_ref)):
            for c in range(4):
                scr[c] = ref[:, c * 128:(c + 1) * 128].astype(F32)
            for r in range(d):
                for c in range(4):
                    col = (3 * r + t) * 512 + c * 128
                    o_ref[:, col:col + 128] = scr[c, pl.ds(r, n, stride=d), :].astype(BF16)

    return pl.pallas_call(
        body, out_shape=SDS((S // d, d * 1536), BF16), grid=(S // tm,),
        in_specs=[pl.BlockSpec((tm, 512), lambda i, b=b: (i, b + gi)) for b in (QA_B, KA_B, VA_B)],
        out_specs=pl.BlockSpec((n, d * 1536), lambda i: (i, 0)),
        scratch_shapes=[pltpu.VMEM((4, tm, 128), F32)],
        compiler_params=_cparams("parallel"), name=f"qkv_to_sub_g{gi}")(proj, proj, proj)


def _attn_merge(os_, lses):
    S = os_[0].shape[0]
    tm = 512

    def body(o0, o1, o2, l0, l1, l2, att_ref, lt_ref, lt1_ref, lt2_ref, so1, so2, sl1, sl2):
        lo = lax.broadcasted_iota(jnp.int32, (tm, 128), 1) < 64

        def natural(ref, d, scr, width):
            nch = width // 128
            if d == 1:
                return [ref[:, c * 128:(c + 1) * 128] for c in range(nch)]
            for r in range(d):
                for c in range(nch):
                    scr[c, pl.ds(r, tm // d, stride=d), :] = ref[:, r * width + c * 128:r * width + (c + 1) * 128]
            return [scr[c] for c in range(nch)]

        ls = [natural(l, d, s, 128)[0] for l, d, s in zip((l0, l1, l2), DILATIONS, (None, sl1, sl2))]
        m = jnp.maximum(jnp.maximum(ls[0], ls[1]), ls[2])
        es = [jnp.exp(v - m) for v in ls]
        z = es[0] + es[1] + es[2]
        lt = m + jnp.log(z)
        lt_ref[...] = lt
        sl1[0] = lt
        for ref, d in ((lt1_ref, DILATIONS[1]), (lt2_ref, DILATIONS[2])):
            for r in range(d):
                ref[:, r * 128:(r + 1) * 128] = sl1[0, pl.ds(r, tm // d, stride=d), :]
        ws = [e / z for e in es]
        o_nat = [natural(o, d, s, 512) for o, d, s in zip((o0, o1, o2), DILATIONS, (None, so1, so2))]
        for c in range(4):
            acc = jnp.zeros((tm, 128), F32)
            for g in range(3):
                w_lo = jnp.broadcast_to(ws[g][:, 32 * c:32 * c + 1], (tm, 128))
                w_hi = jnp.broadcast_to(ws[g][:, 32 * c + 16:32 * c + 17], (tm, 128))
                acc = acc + jnp.where(lo, w_lo, w_hi) * o_nat[g][c]
            att_ref[:, c * 128:(c + 1) * 128] = acc.astype(BF16)

    sub = lambda w: [pl.BlockSpec((tm // d, d * w), lambda i: (i, 0)) for d in DILATIONS]
    att, *lts = pl.pallas_call(
        body, out_shape=(SDS((S, 512), BF16), *[SDS((S // d, d * 128), F32) for d in DILATIONS]), grid=(S // tm,),
        in_specs=sub(512) + sub(128),
        out_specs=(pl.BlockSpec((tm, 512), lambda i: (i, 0)), *sub(128)),
        scratch_shapes=[pltpu.VMEM((4, tm, 128), F32), pltpu.VMEM((4, tm, 128), F32),
                        pltpu.VMEM((1, tm, 128), F32), pltpu.VMEM((1, tm, 128), F32)],
        compiler_params=_cparams("parallel"), name="attn_merge")(*os_, *lses)
    return att, lts


def _assemble_dproj(att_grads, dproj):
    S = dproj.shape[0]
    tm = 256

    def body(*refs):
        a = [refs[3 * t:3 * t + 3] for t in range(3)]
        dp_prev, o_ref, scr = refs[9:]
        for t in range(3):
            for g, d in enumerate(DILATIONS):
                base = (3 * t + g) * COLB
                if d == 1:
                    o_ref[:, base:base + COLB] = a[t][g][...]
                    continue
                for c in range(4):
                    for r in range(d):
                        scr[c, pl.ds(r, tm // d, stride=d), :] = a[t][g][:, r * 512 + c * 128:r * 512 + (c + 1) * 128].astype(F32)
                    o_ref[:, base + c * 128:base + (c + 1) * 128] = scr[c].astype(BF16)

    sub = [pl.BlockSpec((tm // d, d * 512), lambda i: (i, 0)) for d in DILATIONS]
    flat = [att_grads[t][g] for t in range(3) for g in range(3)]
    return pl.pallas_call(
        body, out_shape=SDS((S, PROJ_W), BF16), grid=(S // tm,),
        in_specs=sub * 3 + [ANY], out_specs=pl.BlockSpec((tm, 9 * COLB), lambda i: (i, 0)),
        scratch_shapes=[pltpu.VMEM((4, tm, 128), F32)], input_output_aliases={9: 0},
        compiler_params=_cparams("parallel"), name="assemble_dproj")(*flat, dproj)


def _ret_fwd(proj, consts, exchanges=()):
    S = proj.shape[0]
    nc = S // BLK
    dmask, zeta, xi, dec = consts

    def body(q_ref, k_ref, v0_ref, v1_ref, g0_ref, g1_ref, dm_ref, z_ref, x_ref, dec_ref,
             y_ref, rn_ref, rs_ref, st_ref, R):
        @pl.when(pl.program_id(0) == 0)
        def _():
            R[...] = jnp.zeros_like(R)

        lane16 = lax.broadcasted_iota(jnp.int32, (BLK, 128), 1) // 16
        rs_all = jnp.zeros((BLK, 128), F32)
        first = []
        for h in range(RET_HEADS):
            hs = slice(h * 128, (h + 1) * 128)
            q, k = q_ref[:, hs], k_ref[:, hs]
            v = (v0_ref if h < 2 else v1_ref)[:, (h % 2) * 256:(h % 2 + 1) * 256]
            Rb = R[h].astype(BF16)
            st_ref[h] = Rb
            kz = (k.astype(F32) * z_ref[h]).astype(BF16)
            first.append((v, _dot_nt(q, k), _dot((q.astype(F32) * x_ref[h]).astype(BF16), Rb), _dot_tn(kz, v)))
        masked = [(s * dm_ref[h]).astype(BF16) for h, (_, s, _, _) in enumerate(first)]
        for h in range(RET_HEADS):
            vs = slice((h % 2) * 256, (h % 2 + 1) * 256)
            os_ = slice(h * 256, (h + 1) * 256)
            v, _, cross, kv = first[h]
            o = _dot(masked[h], v) + cross
            R[h] = R[h] * dec_ref[h, 0:1, :] + kv
            mu = jnp.mean(o, axis=-1, keepdims=True)
            oc = o - mu
            rstd = lax.rsqrt(jnp.mean(oc * oc, axis=-1, keepdims=True) + NORM_EPS)
            rn = oc * rstd
            gr = (g0_ref if h < 2 else g1_ref)[:, vs].astype(F32)
            y_ref[:, os_] = (rn * gr * _sigmoid(gr)).astype(BF16)
            rn_ref[:, os_] = rn.astype(BF16)
            rs_all = jnp.where(lane16 == h, rstd, rs_all)
        rs_ref[...] = rs_all

    cst = lambda shape: pl.BlockSpec(shape, lambda c: (0, 0, 0))
    blk = lambda j: pl.BlockSpec((BLK, 512), lambda c: (c, j))
    return _carrier_call(
        body, (proj, proj, proj, proj, proj, proj, dmask, zeta, xi, dec),
        out_shape=(SDS((S, 1024), BF16), SDS((S, 1024), BF16), SDS((S, 128), F32), SDS((RET_HEADS, nc, BLK, 256), BF16)),
        grid=(nc,),
        in_specs=[blk(QR_B), blk(KR_B), blk(11), blk(12), blk(13), blk(14),
                  cst((RET_HEADS, BLK, BLK)), cst((RET_HEADS, BLK, 128)), cst((RET_HEADS, BLK, 128)), cst((RET_HEADS, 8, 256))],
        out_specs=(pl.BlockSpec((BLK, 1024), lambda c: (c, 0)), pl.BlockSpec((BLK, 1024), lambda c: (c, 0)),
                   pl.BlockSpec((BLK, 128), lambda c: (c, 0)),
                   pl.BlockSpec((RET_HEADS, None, BLK, 256), lambda c: (0, c, 0, 0))),
        scratch_shapes=[pltpu.VMEM((RET_HEADS, BLK, 256), F32)],
        sem=("arbitrary",), name="ret_fwd", exchanges=exchanges)


def _mix_out(att, yrin, proj, wa, wr, wo, x, g2, exchanges=()):
    S = x.shape[0]
    tm = 512
    gate0 = 15 * COLB

    def body(a_ref, y_ref, ga_ref, gr_ref, wa_ref, wr_ref, wo_ref, x_ref, g_ref, m_ref, ya_ref, yr_ref, x1_ref, h2_ref):
        pieces = _row_pieces(tm, 256)
        branches = [(_dot(a_ref[rows, :], wa_ref[...]), _dot(y_ref[rows, :], wr_ref[...])) for rows in pieces]
        merged = []
        for rows, (ya, yr) in zip(pieces, branches):
            m = (_sigmoid(ga_ref[rows, :].astype(F32)) * ya + _sigmoid(gr_ref[rows, :].astype(F32)) * yr).astype(BF16)
            m_ref[rows, :] = m
            ya_ref[rows, :] = ya.astype(BF16)
            yr_ref[rows, :] = yr.astype(BF16)
            merged.append(m)
        for rows, m in zip(pieces, merged):
            x1 = x_ref[rows, :] + _dot(m, wo_ref[...])
            x1_ref[rows, :] = x1
            r = lax.rsqrt(jnp.mean(x1 * x1, axis=-1, keepdims=True) + NORM_EPS)
            h2_ref[rows, :] = (x1 * r * g_ref[...]).astype(BF16)

    row = lambda w: pl.BlockSpec((tm, w), lambda i: (i, 0))
    cols = lambda c0: pl.BlockSpec((pl.Element(tm), pl.Element(D_MODEL)), lambda i: (i * tm, c0))
    resident = lambda r, c: pl.BlockSpec((r, c), lambda i: (0, 0), pipeline_mode=pl.Buffered(1))
    return _carrier_call(
        body, (att, yrin, proj, proj, wa, wr, wo, x, g2),
        out_shape=(SDS((S, D_MODEL), BF16),) * 3 + (SDS((S, D_MODEL), F32), SDS((S, D_MODEL), BF16)), grid=(S // tm,),
        in_specs=[row(512), row(D_MODEL), cols(gate0), cols(gate0 + D_MODEL), resident(512, D_MODEL),
                  resident(D_MODEL, D_MODEL), resident(D_MODEL, D_MODEL), row(D_MODEL),
                  pl.BlockSpec((1, D_MODEL), lambda i: (0, 0))],
        out_specs=(row(D_MODEL),) * 5, sem=("parallel",), name="mix_out", exchanges=exchanges)


def _ffn_up(h2, wg, wu, exchanges=()):
    S = h2.shape[0]
    tm = min(S, 2048)

    def body(h_ref, wg_ref, wu_ref, g_ref, u_ref, a_ref):
        for rows in _row_pieces(tm):
            hv = h_ref[rows, :]
            g = _dot_nt(hv, wg_ref[...])
            u = _dot_nt(hv, wu_ref[...])
            g_ref[rows, :] = g.astype(BF16)
            u_ref[rows, :] = u.astype(BF16)
            a_ref[rows, :] = (g * _sigmoid(g) * u).astype(BF16)

    wspec = pl.BlockSpec((None, HID_S, D_MODEL), lambda i, s: (s, 0, 0))
    ospec = pl.BlockSpec((None, tm, HID_S), lambda i, s: (s, i, 0))
    return _carrier_call(
        body, (h2, wg, wu), out_shape=(SDS((N_SHARD, S, HID_S), BF16),) * 3, grid=(S // tm, N_SHARD),
        in_specs=[pl.BlockSpec((tm, D_MODEL), lambda i, s: (i, 0)), wspec, wspec],
        out_specs=(ospec, ospec, ospec),
        sem=("parallel", "arbitrary"), name="ffn_up", exchanges=exchanges)


def _ffn_down_loss(act, wd, x1, g3, tgt):
    S = x1.shape[0]
    tm = 512

    def body(a_ref, w_ref, x_ref, g_ref, t_ref, dx_ref, dxb_ref, dg_ref, ls_ref):
        @pl.when(pl.program_id(0) == 0)
        def _():
            dg_ref[...] = jnp.zeros_like(dg_ref)
            ls_ref[...] = jnp.zeros_like(ls_ref)

        g = g_ref[...]
        for rows in _row_pieces(tm, 256):
            y = _dot(a_ref[0, rows, :], w_ref[0])
            for s in range(1, N_SHARD):
                y = y + _dot(a_ref[s, rows, :], w_ref[s])
            x2 = x_ref[rows, :] + y
            r = lax.rsqrt(jnp.mean(x2 * x2, axis=-1, keepdims=True) + NORM_EPS)
            xh = x2 * r
            err = xh * g - t_ref[rows, :]
            ls_ref[...] += jnp.sum(jnp.sum(err * err, axis=-1, keepdims=True), axis=0, keepdims=True) * (0.5 / D_MODEL)
            dy = err * (1.0 / D_MODEL)
            dg_ref[...] += jnp.sum(dy * xh, axis=0, keepdims=True)
            dxh = dy * g
            dx = r * (dxh - xh * jnp.mean(dxh * xh, axis=-1, keepdims=True))
            dx_ref[rows, :] = dx
            dxb_ref[rows, :] = dx.astype(BF16)

    row = pl.BlockSpec((tm, D_MODEL), lambda i: (i, 0))
    vec = pl.BlockSpec((1, D_MODEL), lambda i: (0, 0))
    return pl.pallas_call(
        body, out_shape=(SDS((S, D_MODEL), F32), SDS((S, D_MODEL), BF16), SDS((1, D_MODEL), F32), SDS((8, 128), F32)),
        grid=(S // tm,),
        in_specs=[pl.BlockSpec((N_SHARD, tm, HID_S), lambda i: (0, i, 0)),
                  pl.BlockSpec((N_SHARD, HID_S, D_MODEL), lambda i: (0, 0, 0), pipeline_mode=pl.Buffered(1)),
                  row, vec, row],
        out_specs=(row, row, vec, pl.BlockSpec((8, 128), lambda i: (0, 0))),
        compiler_params=_cparams("arbitrary"), name="ffn_down_loss")(act, wd, x1, g3, tgt)


def _ffn_bwd(dx2b, dx2, wd, wg, wu, gte, up, x1, g2):
    S = x1.shape[0]
    tm = 256

    def body(d_ref, dx2_ref, wd_ref, wg_ref, wu_ref, g_ref, u_ref, x_ref, gn_ref,
             dg_ref, du_ref, dx_ref, dxb_ref, dgn_ref):
        @pl.when(pl.program_id(0) == 0)
        def _():
            dgn_ref[...] = jnp.zeros_like(dgn_ref)

        d = d_ref[...]
        dacts = [_dot_nt(d, wd_ref[s]) for s in range(N_SHARD)]
        dgs, dus = [], []
        for s, da in enumerate(dacts):
            g = g_ref[s].astype(F32)
            sg = _sigmoid(g)
            dgs.append((da * u_ref[s].astype(F32) * sg * (1.0 + g * (1.0 - sg))).astype(BF16))
            dus.append((da * g * sg).astype(BF16))
            dg_ref[s] = dgs[s]
            du_ref[s] = dus[s]
        dh = _dot(dgs[0], wg_ref[0]) + _dot(dus[0], wu_ref[0])
        for s in range(1, N_SHARD):
            dh = dh + _dot(dgs[s], wg_ref[s]) + _dot(dus[s], wu_ref[s])
        xv = x_ref[...]
        r = lax.rsqrt(jnp.mean(xv * xv, axis=-1, keepdims=True) + NORM_EPS)
        xh = xv * r
        dgn_ref[...] += jnp.sum(dh * xh, axis=0, keepdims=True)
        dxh = dh * gn_ref[...]
        dx = dx2_ref[...] + r * (dxh - xh * jnp.mean(dxh * xh, axis=-1, keepdims=True))
        dx_ref[...] = dx
        dxb_ref[...] = dx.astype(BF16)

    row = pl.BlockSpec((tm, D_MODEL), lambda i: (i, 0))
    vec = pl.BlockSpec((1, D_MODEL), lambda i: (0, 0))
    aspec = pl.BlockSpec((N_SHARD, tm, HID_S), lambda i: (0, i, 0))
    resident = lambda shape: pl.BlockSpec(shape, lambda i: (0, 0, 0), pipeline_mode=pl.Buffered(1))
    return pl.pallas_call(
        body,
        out_shape=(SDS((N_SHARD, S, HID_S), BF16), SDS((N_SHARD, S, HID_S), BF16),
                   SDS((S, D_MODEL), F32), SDS((S, D_MODEL), BF16), SDS((1, D_MODEL), F32)),
        grid=(S // tm,),
        in_specs=[row, row, resident((N_SHARD, HID_S, D_MODEL)), resident((N_SHARD, HID_S, D_MODEL)),
                  resident((N_SHARD, HID_S, D_MODEL)), aspec, aspec, row, vec],
        out_specs=(aspec, aspec, row, row, vec),
        compiler_params=_cparams("arbitrary"), name="ffn_bwd")(dx2b, dx2, wd, wg, wu, gte, up, x1, g2)


def _wgrad(name, a, b, a_spec, b_spec, out_shape, out_spec, n_par, S):
    tk = min(S, 4096)

    def body(a_ref, b_ref, o_ref):
        @pl.when(pl.program_id(1) == 0)
        def _():
            o_ref[...] = jnp.zeros_like(o_ref)

        o_ref[...] += _dot_tn(a_ref[...], b_ref[...])

    return pl.pallas_call(
        body, out_shape=SDS(out_shape, F32), grid=(n_par, S // tk),
        in_specs=[a_spec(tk), b_spec(tk)], out_specs=out_spec,
        compiler_params=_cparams("parallel", "arbitrary"), name=name)(a, b)


def _mix_bwd(dx1b, wo, proj, ya, yr, wa, wr, att, exchanges=()):
    S = dx1b.shape[0]
    tm = 512
    gate0 = 15 * COLB

    def body(d_ref, wo_ref, ga_ref, gr_ref, ya_ref, yr_ref, wa_ref, wr_ref, att_ref,
             dya_ref, dyr_ref, dp_ref, dyi_ref, datt_ref, datt1_ref, datt2_ref, rho_ref, rho1_ref, rho2_ref,
             datt_scr, rho_scr):
        pieces = _row_pieces(tm, 256)
        dms = [_dot_nt(d_ref[rows, :], wo_ref[...]) for rows in pieces]
        branch = []
        for rows, dm in zip(pieces, dms):
            sa = _sigmoid(ga_ref[rows, :].astype(F32))
            sr = _sigmoid(gr_ref[rows, :].astype(F32))
            dya, dyr = (dm * sa).astype(BF16), (dm * sr).astype(BF16)
            dya_ref[rows, :] = dya
            dyr_ref[rows, :] = dyr
            dp_ref[rows, 0:D_MODEL] = (dm * ya_ref[rows, :].astype(F32) * sa * (1.0 - sa)).astype(BF16)
            dp_ref[rows, D_MODEL:2 * D_MODEL] = (dm * yr_ref[rows, :].astype(F32) * sr * (1.0 - sr)).astype(BF16)
            branch.append((dya, dyr))
        lane = lax.broadcasted_iota(jnp.int32, (256, 128), 1)
        lo = lane < 64
        for rows, (dya, dyr) in zip(pieces, branch):
            datt = _dot_nt(dya, wa_ref[...])
            datt_ref[rows, :] = datt.astype(BF16)
            dyi_ref[rows, :] = _dot_nt(dyr, wr_ref[...]).astype(BF16)
            prod = datt * att_ref[rows, :].astype(F32)
            rho = jnp.zeros((256, 128), F32)
            for c in range(4):
                pc = prod[:, c * 128:(c + 1) * 128]
                tot = jnp.sum(pc, axis=-1, keepdims=True)
                low = jnp.sum(jnp.where(lo, pc, 0.0), axis=-1, keepdims=True)
                rho = jnp.where(lane // 16 == 2 * c, low, jnp.where(lane // 16 == 2 * c + 1, tot - low, rho))
            rho_ref[rows, :] = rho
            rho_scr[0] = rho
            for c in range(4):
                datt_scr[c] = datt[:, c * 128:(c + 1) * 128]
            for d, dv_ref, rv_ref in ((DILATIONS[1], datt1_ref, rho1_ref), (DILATIONS[2], datt2_ref, rho2_ref)):
                n = 256 // d
                sub_rows = slice(rows.start // d, rows.start // d + n)
                for r in range(d):
                    rv_ref[sub_rows, r * 128:(r + 1) * 128] = rho_scr[0, pl.ds(r, n, stride=d), :]
                    for c in range(4):
                        col = r * 512 + c * 128
                        dv_ref[sub_rows, col:col + 128] = datt_scr[c, pl.ds(r, n, stride=d), :].astype(BF16)

    row = lambda w: pl.BlockSpec((tm, w), lambda i: (i, 0))
    sub = lambda w: [pl.BlockSpec((tm // d, d * w), lambda i: (i, 0)) for d in DILATIONS]
    cols = lambda c0, w: pl.BlockSpec((pl.Element(tm), pl.Element(w)), lambda i: (i * tm, c0))
    resident = lambda r, c: pl.BlockSpec((r, c), lambda i: (0, 0), pipeline_mode=pl.Buffered(1))
    (dya, dyr, dproj, dyrin, *views), xres = _carrier_call(
        body, (dx1b, wo, proj, proj, ya, yr, wa, wr, att),
        out_shape=(SDS((S, D_MODEL), BF16), SDS((S, D_MODEL), BF16), SDS((S, PROJ_W), BF16), SDS((S, D_MODEL), BF16),
                   *[SDS((S // d, d * 512), BF16) for d in DILATIONS], *[SDS((S // d, d * 128), F32) for d in DILATIONS]),
        grid=(S // tm,),
        in_specs=[row(D_MODEL), resident(D_MODEL, D_MODEL), cols(gate0, D_MODEL), cols(gate0 + D_MODEL, D_MODEL),
                  row(D_MODEL), row(D_MODEL), resident(512, D_MODEL), resident(D_MODEL, D_MODEL), row(512)],
        out_specs=(row(D_MODEL), row(D_MODEL), cols(gate0, 2 * D_MODEL), row(D_MODEL), *sub(512), *sub(128)),
        scratch_shapes=[pltpu.VMEM((4, 256, 128), F32), pltpu.VMEM((1, 256, 128), F32)],
        sem=("parallel",), name="mix_bwd", exchanges=exchanges)
    return (dya, dyr, dproj, dyrin, views[:3], views[3:]), xres


def _attn_bwd(qkv, datt, lse, rho, rtab, d, gi, exchanges=()):
    L = qkv.shape[0]
    nb = L // BLK
    T = d * nb

    def body(q_ref, kc_ref, kp_ref, vc_ref, vp_ref, do_ref, lse_ref, rho_ref, tq_ref, tk_ref,
             dq_ref, dk_ref, dv_ref, ck, cv):
        t = pl.program_id(0)
        n = jnp.minimum(t, T - 1) % nb

        @pl.when(t == 0)
        def _():
            ck[...] = jnp.zeros_like(ck)
            cv[...] = jnp.zeros_like(cv)

        def store_rot(ref, val, t_ref, c):
            sl = slice(c * 128, (c + 1) * 128)
            ref[:, sl] = _unrot(val, t_ref[0], t_ref[1], t_ref[2], 32).astype(BF16)

        @pl.when(t < T)
        def _():
            mask = _band_mask(n)
            mask2 = jnp.concatenate([mask, mask], axis=0)
            lo = lax.broadcasted_iota(jnp.int32, (BLK, 128), 1) < 64

            def stacked(a):
                return jnp.concatenate([jnp.where(lo, a, jnp.zeros_like(a)), jnp.where(lo, jnp.zeros_like(a), a)], axis=0)

            def head_cols(ref, c):
                return jnp.concatenate([jnp.broadcast_to(ref[:, 32 * c:32 * c + 1], (BLK, 2 * BLK)),
                                        jnp.broadcast_to(ref[:, 32 * c + 16:32 * c + 17], (BLK, 2 * BLK))], axis=0)

            ops, raw = [], []
            for c in range(4):
                sl = slice(c * 128, (c + 1) * 128)
                q2, do2 = stacked(q_ref[:, sl]), stacked(do_ref[:, sl])
                k = jnp.concatenate([kp_ref[:, sl], kc_ref[:, sl]], axis=0)
                v = jnp.concatenate([vp_ref[:, sl], vc_ref[:, sl]], axis=0)
                ops.append((q2, do2, k))
                raw.append((_dot_nt(q2, k), _dot_nt(do2, v)))
            grads = []
            for c, (s, dp) in enumerate(raw):
                p = jnp.where(mask2, jnp.exp(s * 0.125 - head_cols(lse_ref, c)), 0.0)
                grads.append(((p * (dp - head_cols(rho_ref, c)) * 0.125).astype(BF16), p.astype(BF16)))
            for c, ((q2, do2, k), (ds, pb)) in enumerate(zip(ops, grads)):
                sl = slice(c * 128, (c + 1) * 128)
                dq2 = _dot(ds, k)
                dq_c = jnp.where(lo, dq2[:BLK], dq2[BLK:])
                dk_c = _dot_tn(ds, q2)
                dv_c = _dot_tn(pb, do2)
                store_rot(dq_ref, dq_c, tq_ref, c)
                store_rot(dk_ref, ck[:, sl] + dk_c[:BLK], tk_ref, c)
                dv_ref[:, sl] = (cv[:, sl] + dv_c[:BLK]).astype(BF16)
                ck[:, sl] = dk_c[BLK:]
                cv[:, sl] = dv_c[BLK:]

        @pl.when(t == T)
        def _():
            for c in range(4):
                sl = slice(c * 128, (c + 1) * 128)
                store_rot(dk_ref, ck[:, sl], tk_ref, c)
            dv_ref[...] = cv[...].astype(BF16)

    blk_of = lambda t: (jnp.minimum(t, T - 1) % nb, jnp.minimum(t, T - 1) // nb)
    cur = lambda t: blk_of(t)
    prev = lambda t: (jnp.maximum(blk_of(t)[0] - 1, 0), blk_of(t)[1])
    fin = lambda t: blk_of(jnp.maximum(t - 1, 0))
    col = _qkv_col(d, gi)
    qkv_spec = lambda kind, which: pl.BlockSpec((BLK, 512), lambda t: (which(t)[0], col(kind, which(t)[1])))
    row_spec = lambda w, which: pl.BlockSpec((BLK, w), lambda t: which(t))
    tab_spec = lambda which: pl.BlockSpec((3, BLK, 128), lambda t: (0, *which(t)))
    return _carrier_call(
        body, (qkv, qkv, qkv, qkv, qkv, datt, lse, rho, rtab, rtab),
        out_shape=(SDS((L, d * 512), BF16),) * 3, grid=(T + 1,),
        in_specs=[qkv_spec(0, cur), qkv_spec(1, cur), qkv_spec(1, prev), qkv_spec(2, cur), qkv_spec(2, prev),
                  row_spec(512, cur), row_spec(128, cur), row_spec(128, cur), tab_spec(cur), tab_spec(fin)],
        out_specs=(row_spec(512, cur), row_spec(512, fin), row_spec(512, fin)),
        scratch_shapes=[pltpu.VMEM((BLK, 512), F32), pltpu.VMEM((BLK, 512), F32)],
        sem=("arbitrary",), name=f"attn_bwd_g{gi}", exchanges=exchanges)


def _ret_bwd(proj, rn, rstd, dyrin, states, tab, consts, dproj, exchanges=()):
    S = proj.shape[0]
    nc = S // BLK
    dmask, zeta, xi, dec = consts

    def body(q_ref, k_ref, v0_ref, v1_ref, g0_ref, g1_ref, rn_ref, rs_ref, dy_ref, st_ref, tq_ref, tk_ref,
             dm_ref, z_ref, x_ref, dec_ref, dp_prev, dp_ref, dR):
        dq_ref, dk_ref = dp_ref.at[:, 0:512], dp_ref.at[:, 512:1024]
        dv_ref, dgr_ref = dp_ref.at[:, 1024:2048], dp_ref.at[:, 2048:3072]

        @pl.when(pl.program_id(0) == 0)
        def _():
            dR[...] = jnp.zeros_like(dR)

        dobs = []
        for h in range(RET_HEADS):
            vs = slice((h % 2) * 256, (h % 2 + 1) * 256)
            os_ = slice(h * 256, (h + 1) * 256)
            gr = (g0_ref if h < 2 else g1_ref)[:, vs].astype(F32)
            sg = _sigmoid(gr)
            rn_v = rn_ref[:, os_].astype(F32)
            dyi = dy_ref[:, os_].astype(F32)
            dgr_ref[:, os_] = (dyi * rn_v * sg * (1.0 + gr * (1.0 - sg))).astype(BF16)
            drn = dyi * gr * sg
            rstd = jnp.broadcast_to(rs_ref[:, 16 * h:16 * h + 1], (BLK, 256))
            do = rstd * (drn - jnp.mean(drn, axis=-1, keepdims=True) - rn_v * jnp.mean(drn * rn_v, axis=-1, keepdims=True))
            dobs.append(do.astype(BF16))
        first = []
        for h in range(RET_HEADS):
            hs = slice(h * 128, (h + 1) * 128)
            q, k = q_ref[:, hs], k_ref[:, hs]
            v = (v0_ref if h < 2 else v1_ref)[:, (h % 2) * 256:(h % 2 + 1) * 256]
            dob, dRb = dobs[h], dR[h].astype(BF16)
            kz = (k.astype(F32) * z_ref[h]).astype(BF16)
            qx = (q.astype(F32) * x_ref[h]).astype(BF16)
            first.append((q, k, _dot_nt(q, k), _dot_nt(dob, v), _dot(kz, dRb), _dot_nt(dob, st_ref[h]),
                          _dot_nt(v, dRb), _dot_tn(qx, dob)))
        masked = [((s * dm_ref[h]).astype(BF16), (dsr * dm_ref[h]).astype(BF16))
                  for h, (_, _, s, dsr, _, _, _, _) in enumerate(first)]
        for h in range(RET_HEADS):
            hs = slice(h * 128, (h + 1) * 128)
            os_ = slice(h * 256, (h + 1) * 256)
            q, k, _, _, dv_state, dq_state, dk_state, dr_new = first[h]
            sD, dS = masked[h]
            dv_ref[:, os_] = (_dot_tn(sD, dobs[h]) + dv_state).astype(BF16)
            dq = _dot(dS, k) + dq_state * x_ref[h]
            dk = _dot_tn(dS, q) + dk_state * z_ref[h]
            dR[h] = dR[h] * dec_ref[h, 0:1, :] + dr_new
            dq_ref[:, hs] = _unrot(dq, tq_ref[0], tq_ref[1], tq_ref[2], 1).astype(BF16)
            dk_ref[:, hs] = _unrot(dk, tk_ref[0], tk_ref[1], tk_ref[2], 1).astype(BF16)

    rc = lambda c: nc - 1 - c
    cst = lambda shape: pl.BlockSpec(shape, lambda c: (0, 0, 0))
    blk = lambda j: pl.BlockSpec((BLK, 512), lambda c: (rc(c), j))
    row = lambda w: pl.BlockSpec((BLK, w), lambda c: (rc(c), 0))
    (dproj,), xres = _carrier_call(
        body, (proj, proj, proj, proj, proj, proj, rn, rstd, dyrin, states, tab, tab, dmask, zeta, xi, dec, dproj),
        out_shape=(SDS((S, PROJ_W), BF16),), grid=(nc,),
        in_specs=[blk(QR_B), blk(KR_B), blk(11), blk(12), blk(13), blk(14), row(1024), row(128), row(1024),
                  pl.BlockSpec((RET_HEADS, None, BLK, 256), lambda c: (0, rc(c), 0, 0)),
                  pl.BlockSpec((None, 3, BLK, 128), lambda c: (1, 0, rc(c), 0)),
                  pl.BlockSpec((None, 3, BLK, 128), lambda c: (2, 0, rc(c), 0)),
                  cst((RET_HEADS, BLK, BLK)), cst((RET_HEADS, BLK, 128)), cst((RET_HEADS, BLK, 128)), cst((RET_HEADS, 8, 256)),
                  ANY],
        out_specs=(pl.BlockSpec((pl.Element(BLK), pl.Element(6 * COLB)), lambda c: (rc(c) * BLK, QR_B * COLB)),),
        scratch_shapes=[pltpu.VMEM((RET_HEADS, BLK, 256), F32)],
        sem=("arbitrary",), name="ret_bwd", exchanges=exchanges, in_out_aliases={16: 0})
    return dproj, xres


def _wgrad_in_half(ht, dproj, sidx, kept, exchanges=()):
    S = dproj.shape[0]
    tk = 2048
    half = (lambda sx: sx[4]) if kept else (lambda sx: 1 - sx[4])

    def body(a_ref, b_ref, o_ref):
        @pl.when(pl.program_id(1) == 0)
        def _():
            o_ref[...] = jnp.zeros_like(o_ref)

        o_ref[...] += _dot(a_ref[...], b_ref[...])

    (g,), xres = _carrier_call(
        body, (ht, dproj), out_shape=(SDS((D_MODEL // 2, PROJ_W), F32),), grid=(N_SHARD, S // tk),
        in_specs=[pl.BlockSpec((D_MODEL // 2, tk), lambda s, k, sx: (half(sx), k)),
                  pl.BlockSpec((tk, W_IN_S), lambda s, k, sx: (k, s))],
        out_specs=(pl.BlockSpec((D_MODEL // 2, W_IN_S), lambda s, k, sx: (0, s)),),
        sem=("parallel", "arbitrary"), name="wgrad_in_kept" if kept else "wgrad_in_sent", exchanges=exchanges,
        prefetch=sidx)
    return g, xres


def _in_proj_bwd(dproj, w_in, x, g1, dx1, exchanges=()):
    S = x.shape[0]
    tm = 1024

    def body(d_ref, w_ref, x_ref, g_ref, dx1_ref, dx_ref, dgn_ref, acc):
        i, s = pl.program_id(0), pl.program_id(1)

        @pl.when(s == 0)
        def _():
            acc[...] = jnp.zeros_like(acc)

        @pl.when((i == 0) & (s == 0))
        def _():
            dgn_ref[...] = jnp.zeros_like(dgn_ref)

        acc[...] += _dot_nt(d_ref[...], w_ref[...])

        @pl.when(s == N_SHARD - 1)
        def _():
            xv = x_ref[...]
            r = lax.rsqrt(jnp.mean(xv * xv, axis=-1, keepdims=True) + NORM_EPS)
            xh = xv * r
            dh = acc[...]
            dgn_ref[...] += jnp.sum(dh * xh, axis=0, keepdims=True)
            dxh = dh * g_ref[...]
            dx_ref[...] = dx1_ref[...] + r * (dxh - xh * jnp.mean(dxh * xh, axis=-1, keepdims=True))

    row = pl.BlockSpec((tm, D_MODEL), lambda i, s: (i, 0))
    vec = pl.BlockSpec((1, D_MODEL), lambda i, s: (0, 0))
    (gx, dg), xres = _carrier_call(
        body, (dproj, w_in, x, g1, dx1),
        out_shape=(SDS((S, D_MODEL), F32), SDS((1, D_MODEL), F32)), grid=(S // tm, N_SHARD),
        in_specs=[pl.BlockSpec((tm, W_IN_S), lambda i, s: (i, s)),
                  pl.BlockSpec((D_MODEL, W_IN_S), lambda i, s: (0, s)), row, vec, row],
        out_specs=(row, vec), scratch_shapes=[pltpu.VMEM((tm, D_MODEL), F32)],
        sem=("arbitrary", "arbitrary"), name="in_proj_bwd", exchanges=exchanges)
    return gx, dg, xres


def _step(x, tgt, g1, g2, g3, comm):
    S = x.shape[0]
    tab_np = _tables(S)
    tab = jnp.asarray(tab_np)
    consts = _ret_consts()

    (h, ht, *casts), xres = _rms_fwd(x, g1, comm.to_cast(), comm.carry("rms_fwd"))
    comm.cast_done(casts)
    comm.took("rms_fwd", xres)
    w_in = comm.weight(0)
    proj, xres = _in_proj(h, w_in, tab, comm.carry("in_proj"))
    comm.took("in_proj", xres)
    qkvs, o_parts, lse_parts = [], [], []
    for gi, d in enumerate(DILATIONS):
        qkv = proj if d == 1 else _qkv_to_sub(proj, d, gi)
        (o_g, lse_g), xres = _attn_fwd(qkv, d, gi, comm.carry(f"attn_fwd_g{gi}"))
        comm.took(f"attn_fwd_g{gi}", xres)
        qkvs.append(qkv)
        o_parts.append(o_g)
        lse_parts.append(lse_g)
    att, lse_views = _attn_merge(o_parts, lse_parts)
    (yrin, rn, rstd, states), xres = _ret_fwd(proj, consts, comm.carry("ret_fwd"))
    comm.took("ret_fwd", xres)
    wa, wr, wo = comm.weight(1), comm.weight(2), comm.weight(3)
    (merged, ya, yr, x1, h2), xres = _mix_out(att, yrin, proj, wa, wr, wo, x, g2, comm.carry("mix_out"))
    comm.took("mix_out", xres)
    wg, wu = comm.weight(4), comm.weight(5)
    (gte, up, act), xres = _ffn_up(h2, wg, wu, comm.carry("ffn_up"))
    comm.took("ffn_up", xres)
    wd = comm.weight(6)
    dx2, dx2b, dg3, loss_p = _ffn_down_loss(act, wd, x1, g3, tgt)

    dgte, dup, dx1, dx1b, dg2 = _ffn_bwd(dx2b, dx2, wd, wg, wu, gte, up, x1, g2)
    tok3 = lambda w: (lambda tk: pl.BlockSpec((None, tk, w), lambda p, k: (p, k, 0)))
    tok2 = lambda w: (lambda tk: pl.BlockSpec((tk, w), lambda p, k: (k, 0)))
    g_d = _wgrad("wgrad_down", act, dx2b, tok3(HID_S), tok2(D_MODEL), (N_SHARD, HID_S, D_MODEL),
                 pl.BlockSpec((None, HID_S, D_MODEL), lambda p, k: (p, 0, 0)), N_SHARD, S)
    g_g = _wgrad("wgrad_gate", dgte, h2, tok3(HID_S), tok2(D_MODEL), (N_SHARD, HID_S, D_MODEL),
                 pl.BlockSpec((None, HID_S, D_MODEL), lambda p, k: (p, 0, 0)), N_SHARD, S)
    g_u = _wgrad("wgrad_up", dup, h2, tok3(HID_S), tok2(D_MODEL), (N_SHARD, HID_S, D_MODEL),
                 pl.BlockSpec((None, HID_S, D_MODEL), lambda p, k: (p, 0, 0)), N_SHARD, S)
    comm.grads({4: g_g, 5: g_u, 6: g_d})
    (dya, dyr, dproj, dyrin, datt_views, rho_views), xres = _mix_bwd(dx1b, wo, proj, ya, yr, wa, wr, att,
                                                                       comm.carry("mix_bwd"))
    comm.took("mix_bwd", xres)
    colblk = lambda w: (lambda tk: pl.BlockSpec((tk, w), lambda p, k: (k, p)))
    g_o = _wgrad("wgrad_out", merged, dx1b, colblk(256), tok2(D_MODEL), (D_MODEL, D_MODEL),
                 pl.BlockSpec((256, D_MODEL), lambda p, k: (p, 0)), 4, S)
    g_a = _wgrad("wgrad_attn", att, dya, tok2(512), colblk(512), (512, D_MODEL),
                 pl.BlockSpec((512, 512), lambda p, k: (0, p)), 2, S)
    g_r = _wgrad("wgrad_ret", yrin, dyr, colblk(256), tok2(D_MODEL), (D_MODEL, D_MODEL),
                 pl.BlockSpec((256, D_MODEL), lambda p, k: (p, 0)), 4, S)
    comm.grads({1: g_a, 2: g_r.reshape(N_SHARD, 256, D_MODEL), 3: g_o.reshape(N_SHARD, 256, D_MODEL)})
    dproj, xres = _ret_bwd(proj, rn, rstd, dyrin, states, tab, consts, dproj, comm.carry("ret_bwd"))
    comm.took("ret_bwd", xres)
    dqs, dks, dvs = [], [], []
    for gi, d in enumerate(DILATIONS):
        rtab = jnp.asarray(tab_np[0].reshape(3, S // d, d * 128))
        (dq, dk, dv), xres = _attn_bwd(qkvs[gi], datt_views[gi], lse_views[gi], rho_views[gi], rtab, d, gi,
                                       comm.carry(f"attn_bwd_g{gi}"))
        comm.took(f"attn_bwd_g{gi}", xres)
        dqs.append(dq)
        dks.append(dk)
        dvs.append(dv)
    dproj = _assemble_dproj((dqs, dks, dvs), dproj)
    g_sent, xres = _wgrad_in_half(ht, dproj, comm.sidx, False, comm.carry("wgrad_in_sent"))
    comm.took("wgrad_in_sent", xres)
    comm.grads({"in_sent": g_sent})
    g_kept, xres = _wgrad_in_half(ht, dproj, comm.sidx, True, comm.carry("wgrad_in_kept"))
    comm.grads({"in_kept": g_kept})
    comm.took("wgrad_in_kept", xres)
    grad_x, dg1, xres = _in_proj_bwd(dproj, w_in, x, g1, dx1, comm.carry("in_proj_bwd"))
    comm.took("in_proj_bwd", xres)
    return loss_p, grad_x, (dg1, dg2, dg3)


W_KINDS = ("col", "col", "lead", "lead", "lead", "lead", "lead")
W_SHARD = ((1024, W_IN_S), (512, 256), (256, 1024), (256, 1024), (HID_S, 1024), (HID_S, 1024), (HID_S, 1024))
W_TRANSPOSED = (4, 5)
N_W = len(W_KINDS)


def _full_shape(wi):
    R, C = W_SHARD[wi]
    return (R, N_SHARD * C) if W_KINDS[wi] == "col" else (N_SHARD, R, C)


def _view(ref, wi, s, half):
    R, C = W_SHARD[wi]
    rows = pl.ds(half * (R // 2), R // 2)
    if W_KINDS[wi] == "col":
        return ref.at[rows, pl.ds(pl.multiple_of(s * C, 128), C)]
    return ref.at[s, rows, :]


def _mesh_pos():
    x, y, c = lax.axis_index("x"), lax.axis_index("y"), lax.axis_index("c")
    chips = [(1 - x, y), (x, 1 - y), (1 - x, 1 - y)]
    return x, y, c, chips


def _cast_bf16(a):
    R, C = a.shape
    tr = R // 2 if R % 32 == 0 else R

    def body(a_ref, o_ref):
        o_ref[...] = a_ref[...].astype(BF16)

    spec = pl.BlockSpec((tr, C), lambda i: (i, 0))
    return pl.pallas_call(body, out_shape=SDS((R, C), BF16), grid=(R // tr,), in_specs=[spec], out_specs=spec,
                          compiler_params=_cparams("parallel"), name=f"cast_{R}x{C}")(a)


def _remote(send, recv, k, src, dst, to):
    return pltpu.make_async_remote_copy(src_ref=src, dst_ref=dst, send_sem=send.at[k], recv_sem=recv.at[k],
                                        device_id=to, device_id_type=MESH)


def _ex_gather_ring(wis, shards):
    n = len(wis)

    def build(sh, full, send, recv, loc):
        x, y, c, _ = _mesh_pos()
        s_me, sib = 2 * x + y, (x, y, 1 - c)
        xn, yn = (1 - x, y), (x, 1 - y)
        flip = lambda a, b: a + b - 2 * a * b
        via = (flip(x, 1 - c), flip(y, c))
        onto = (flip(x, c), flip(y, 1 - c))
        shard_of = lambda chip: 2 * chip[0] + chip[1]
        starts, waits, sent = [], [], []
        for i, wi in enumerate(wis):
            Rh = W_SHARD[wi][0] // 2
            for hf in range(2):
                cp = pltpu.make_async_copy(sh[i].at[pl.ds(hf * Rh, Rh), :], _view(full[i], wi, s_me, hf), loc.at[2 * i + hf])
                starts.append(cp)
                sent.append(cp.wait)
            for j, chip in enumerate((xn, yn)):
                cp = _remote(send, recv, 6 * i + j, sh[i].at[pl.ds(c * Rh, Rh), :], _view(full[i], wi, s_me, c), (*chip, c))
                starts.append(cp)
                sent.append(cp.wait_send)

        def pass_to_sibling(i, wi, k, s):
            mine = _view(full[i], wi, s, c)
            fw = _remote(send, recv, 6 * i + k, mine, mine, sib)
            waits.append(fw.start)
            sent.append(fw.wait_send)

        for i, wi in enumerate(wis):
            for j, chip in enumerate((xn, yn)):
                land = _view(full[i], wi, shard_of(chip), c)
                waits.append(_remote(send, recv, 6 * i + j, land, land, (*chip, c)).wait_recv)
                pass_to_sibling(i, wi, 3 + j, shard_of(chip))
            relay = _view(full[i], wi, shard_of(via), c)
            fw = _remote(send, recv, 6 * i + 2, relay, relay, (*onto, c))
            waits.append(fw.start)
            sent.append(fw.wait_send)
        s_diag = 2 * (1 - x) + (1 - y)
        for i, wi in enumerate(wis):
            land = _view(full[i], wi, s_diag, c)
            waits.append(_remote(send, recv, 6 * i + 2, land, land, (*onto, c)).wait_recv)
            pass_to_sibling(i, wi, 5, s_diag)
        for i, wi in enumerate(wis):
            for k, s in ((3, shard_of(xn)), (4, shard_of(yn)), (5, s_diag)):
                land = _view(full[i], wi, s, 1 - c)
                waits.append(_remote(send, recv, 6 * i + k, land, land, sib).wait_recv)
        return starts, waits + sent

    return _Exchange(shards, [SDS(_full_shape(wi), BF16) for wi in wis], {}, 6 * n, 2 * n, build)


def _ex_gather_ici(wis, shards, then_d2d=False):
    n = len(wis)

    def build(ins, outs, send, recv, loc):
        x, y, c, chips = _mesh_pos()
        s_me, sib = 2 * x + y, (x, y, 1 - c)
        starts, waits, after = [], [], []
        for i, wi in enumerate(wis):
            Rh = W_SHARD[wi][0] // 2
            for hf in range(2):
                cp = pltpu.make_async_copy(ins[i].at[pl.ds(hf * Rh, Rh), :], _view(outs[i], wi, s_me, hf), loc.at[2 * i + hf])
                starts.append(cp)
                waits.append(cp.wait)
            for j, chip in enumerate(chips):
                cp = _remote(send, recv, 3 * i + j, ins[i].at[pl.ds(c * Rh, Rh), :], _view(outs[i], wi, s_me, c), (*chip, c))
                land = _view(outs[i], wi, 2 * chip[0] + chip[1], c)
                starts.append(cp)
                waits += [cp.wait_send, _remote(send, recv, 3 * i + j, land, land, (*chip, c)).wait_recv]
                if then_d2d:
                    theirs = _view(outs[i], wi, 2 * chip[0] + chip[1], 1 - c)
                    fw = _remote(send, recv, 3 * n + 3 * i + j, land, land, sib)
                    waits.append(fw.start)
                    after += [fw.wait_send, _remote(send, recv, 3 * n + 3 * i + j, theirs, theirs, sib).wait_recv]
        return starts, waits + after

    return _Exchange(shards, [SDS(_full_shape(wi), BF16) for wi in wis], {}, (6 if then_d2d else 3) * n, 2 * n, build)


def _ex_gather_d2d(wis, fulls):
    def build(ins, outs, send, recv, loc):
        x, y, c, chips = _mesh_pos()
        sib = (x, y, 1 - c)
        starts, waits = [], []
        for i, wi in enumerate(wis):
            for j, chip in enumerate(chips):
                mine = _view(outs[i], wi, 2 * chip[0] + chip[1], c)
                theirs = _view(outs[i], wi, 2 * chip[0] + chip[1], 1 - c)
                cp = _remote(send, recv, 3 * i + j, mine, mine, sib)
                starts.append(cp)
                waits += [cp.wait_send, _remote(send, recv, 3 * i + j, theirs, theirs, sib).wait_recv]
        return starts, waits

    return _Exchange(fulls, [SDS(f.shape, BF16) for f in fulls], {i: i for i in range(len(wis))}, 3 * len(wis), 0, build)


def _half_shape(wi):
    R, C = W_SHARD[wi]
    return (R // 2, N_SHARD * C) if W_KINDS[wi] == "col" else (N_SHARD, R // 2, C)


def _ex_pair(wis, grads):
    def build(ins, outs, send, recv, loc):
        x, y, c, _ = _mesh_pos()
        starts, waits = [], []
        for i, wi in enumerate(wis):
            Rh = W_SHARD[wi][0] // 2
            rows = pl.ds((1 - c) * Rh, Rh)
            if tuple(ins[i].shape) == _half_shape(wi):
                src = ins[i]
            else:
                src = ins[i].at[rows, :] if W_KINDS[wi] == "col" else ins[i].at[:, rows, :]
            cp = _remote(send, recv, i, src, outs[i], (x, y, 1 - c))
            starts.append(cp)
            waits.append(cp.wait)
        return starts, waits

    return _Exchange(grads, [SDS(_half_shape(wi), F32) for wi in wis], {}, len(wis), 0, build)


def _ex_chip(wis, pbs):
    def build(ins, outs, send, recv, loc):
        x, y, c, chips = _mesh_pos()
        starts, waits = [], []
        for i, wi in enumerate(wis):
            for j, chip in enumerate(chips):
                cp = _remote(send, recv, 3 * i + j, ins[i].at[j], outs[i].at[j], (*chip, c))
                starts.append(cp)
                waits.append(cp.wait)
        return starts, waits

    shapes = [SDS((3, W_SHARD[wi][0] // 2, W_SHARD[wi][1]), BF16) for wi in wis]
    return _Exchange(pbs, shapes, {}, 3 * len(wis), 0, build)


def _ex_share(wis, halves):
    def build(ins, outs, send, recv, loc):
        x, y, c, _ = _mesh_pos()
        sib = (x, y, 1 - c)
        starts, waits = [], []
        for i, wi in enumerate(wis):
            cp = _remote(send, recv, i, outs[i].at[c], outs[i].at[c], sib)
            starts.append(cp)
            waits += [cp.wait_send, _remote(send, recv, i, outs[i].at[1 - c], outs[i].at[1 - c], sib).wait_recv]
        return starts, waits

    return _Exchange(halves, [SDS(h.shape, F32) for h in halves], {i: i for i in range(len(wis))}, len(wis), 0, build)


def _row_tile(rh, C):
    best = 16
    for t in range(16, rh + 1, 16):
        if rh % t == 0 and t * C * 4 <= (3 << 19):
            best = t
    return best


def _pair_sum(wi, g, ra, sidx):
    R, C = W_SHARD[wi]
    Rh = R // 2
    tr = _row_tile(Rh, C)
    nt = Rh // tr
    off = 0 if tuple(g.shape) == _half_shape(wi) else nt
    col = W_KINDS[wi] == "col"

    def body(sidx_ref, *refs):
        gs, rs = refs[:4], refs[4:8]
        own_ref, pb_ref = refs[8:]
        own_ref[...] = gs[0][...] + rs[0][...]
        for j in range(3):
            pb_ref[j] = (gs[1 + j][...] + rs[1 + j][...]).astype(BF16)

    def gspec(slot):
        if col:
            return pl.BlockSpec((tr, C), lambda i, sx: (sx[4] * off + i, sx[slot]))
        return pl.BlockSpec((None, tr, C), lambda i, sx: (sx[slot], sx[4] * off + i, 0))

    def rspec(slot):
        if col:
            return pl.BlockSpec((tr, C), lambda i, sx: (i, sx[slot]))
        return pl.BlockSpec((None, tr, C), lambda i, sx: (sx[slot], i, 0))

    return pl.pallas_call(
        body, out_shape=(SDS((Rh, C), F32), SDS((3, Rh, C), BF16)),
        grid_spec=pltpu.PrefetchScalarGridSpec(
            num_scalar_prefetch=1, grid=(nt,),
            in_specs=[gspec(k) for k in range(4)] + [rspec(k) for k in range(4)],
            out_specs=(pl.BlockSpec((tr, C), lambda i, sx: (i, 0)), pl.BlockSpec((3, tr, C), lambda i, sx: (0, i, 0)))),
        compiler_params=_cparams("arbitrary"), name=f"pair_sum_w{wi}")(sidx, g, g, g, g, ra, ra, ra, ra)


def _chip_sum(wi, own, rb, sidx):
    R, C = W_SHARD[wi]
    Rh = R // 2
    tr = _row_tile(Rh, C)

    def body(sidx_ref, own_ref, rb_ref, o_ref):
        o_ref[...] = ((own_ref[...] + rb_ref[0].astype(F32)) + rb_ref[1].astype(F32)) + rb_ref[2].astype(F32)

    return pl.pallas_call(
        body, out_shape=SDS((2, Rh, C), F32),
        grid_spec=pltpu.PrefetchScalarGridSpec(
            num_scalar_prefetch=1, grid=(Rh // tr,),
            in_specs=[pl.BlockSpec((tr, C), lambda i, sx: (i, 0)), pl.BlockSpec((3, tr, C), lambda i, sx: (0, i, 0))],
            out_specs=pl.BlockSpec((None, tr, C), lambda i, sx: (sx[4], i, 0))),
        compiler_params=_cparams("arbitrary"), name=f"chip_sum_w{wi}")(sidx, own, rb)


def _gain_allgather(blk, ex):
    m_per, n = blk.shape
    n_in, n_out = len(ex.ins), len(ex.out_shapes)

    def body(x_ref, *rest):
        xin, out_ref, xout = rest[:n_in], rest[n_in], rest[n_in + 1:n_in + 1 + n_out]
        send_sems, recv_sems, local_sem = rest[n_in + 1 + n_out:n_in + 4 + n_out]
        ex_starts, ex_waits = ex.build(xin, xout, *rest[n_in + 4 + n_out:])
        for cp in ex_starts:
            cp.start()
        x, y, c, chips = _mesh_pos()
        me, sibling = (x, y, c), (x, y, 1 - c)

        def rows(px, py, pc):
            return out_ref.at[pl.ds((4 * px + 2 * py + pc) * m_per, m_per), :]

        def copy(k, block, to, src=None):
            return pltpu.make_async_remote_copy(
                src_ref=rows(*block) if src is None else src, dst_ref=rows(*block),
                send_sem=send_sems.at[k], recv_sem=recv_sems.at[k], device_id=to, device_id_type=MESH)

        mine = pltpu.make_async_copy(x_ref, rows(*me), local_sem)
        mine.start()
        first = [copy(0, me, sibling, src=x_ref)]
        first += [copy(1 + j, me, (*chip, c), src=x_ref) for j, chip in enumerate(chips)]
        for cp in first:
            cp.start()
        passed = [copy(4 + j, (*chip, c), sibling) for j, chip in enumerate(chips)]
        for j, chip in enumerate(chips):
            copy(1 + j, (*chip, c), me).wait_recv()
            passed[j].start()
        copy(0, sibling, me).wait_recv()
        for j, chip in enumerate(chips):
            copy(4 + j, (*chip, 1 - c), me).wait_recv()
        for cp in first + passed:
            cp.wait_send()
        mine.wait()
        for w in ex_waits:
            w()

    vm = pl.BlockSpec(memory_space=pltpu.VMEM)
    res = pl.pallas_call(
        body, out_shape=(SDS((8 * m_per, n), blk.dtype), *ex.out_shapes),
        in_specs=[vm] + [ANY] * n_in, out_specs=(vm, *[ANY] * n_out),
        input_output_aliases={1 + a: 1 + o for a, o in ex.aliases.items()},
        scratch_shapes=[pltpu.SemaphoreType.DMA((7,)), pltpu.SemaphoreType.DMA((7,)), pltpu.SemaphoreType.DMA] + ex.sems(),
        name="gain_allgather")(blk, *ex.ins)
    return res[0], tuple(res[1:])


def _adam_math(w, g, m, v):
    mn = ADAM_B1 * m + (1.0 - ADAM_B1) * g
    vn = ADAM_B2 * v + (1.0 - ADAM_B2) * (g * g)
    mh = mn / (1.0 - ADAM_B1 ** ADAM_STEP)
    vh = vn / (1.0 - ADAM_B2 ** ADAM_STEP)
    return -ADAM_LR * (mh / (jnp.sqrt(vh) + ADAM_EPS) + ADAM_WD * w), mn, vn


def _adamw(name, ws, gs, ms, vs):
    n, steps = len(ws), 8

    def body(*refs):
        for k in range(n):
            w_ref, g_ref, m_ref, v_ref = refs[4 * k:4 * k + 4]
            go_ref, d_ref, mn_ref, vn_ref = refs[4 * n + 4 * k:4 * n + 4 * k + 4]
            g = g_ref[...]
            go_ref[...] = g
            d_ref[...], mn_ref[...], vn_ref[...] = _adam_math(w_ref[...], g, m_ref[...], v_ref[...])

    specs = [pl.BlockSpec((w.shape[0] // steps, w.shape[1]), lambda i: (i, 0)) for w in ws for _ in range(4)]
    res = pl.pallas_call(
        body, out_shape=tuple(SDS(w.shape, F32) for w in ws for _ in range(4)), grid=(steps,),
        in_specs=specs, out_specs=tuple(specs), compiler_params=_cparams("parallel"),
        name=name)(*[a for k in range(n) for a in (ws[k], gs[k], ms[k], vs[k])])
    return [tuple(res[4 * k:4 * k + 4]) for k in range(n)]


def _gain_update(gathered, w, m, v):
    def body(ga_ref, w_ref, m_ref, v_ref, g_ref, d_ref, mn_ref, vn_ref):
        g = ga_ref[0:8, :]
        for dev in range(1, 8):
            g = g + ga_ref[8 * dev:8 * dev + 8, :]
        g_ref[...] = g
        d_ref[...], mn_ref[...], vn_ref[...] = _adam_math(w_ref[...], g, m_ref[...], v_ref[...])

    return pl.pallas_call(body, out_shape=(SDS((8, 1024), F32),) * 4, name="gain_update")(gathered, w, m, v)


GROUP_FFN, GROUP_MIX, GROUP_IN = (4, 5, 6), (1, 2, 3), (0,)
REST = GROUP_MIX + GROUP_FFN


class _MeshComm:
    SCHEDULE = {
        "rms_fwd": [("ring", GROUP_IN)],
        "in_proj": [("ici", (1, 2, 3, 4))],
        "ret_fwd": [("d2d", (1, 2, 3, 4)), ("ici", (5,))],
        "mix_out": [("d2d", (5,))],
        "ffn_up": [("both", (6,))],
        "mix_bwd": [("pair", GROUP_FFN)],
        "ret_bwd": [("pair", GROUP_MIX), ("chip", (4,))],
        "attn_bwd_g0": [("chip", (5,))],
        "attn_bwd_g1": [("chip", (6,))],
        "attn_bwd_g2": [("chip", GROUP_MIX)],
        "wgrad_in_kept": [("pair", GROUP_IN), ("share", GROUP_FFN + GROUP_MIX)],
        "in_proj_bwd": [("chip", GROUP_IN)],
    }

    def __init__(self, w_in_shard, rest_f32):
        xi, yi, ci = lax.axis_index("x"), lax.axis_index("y"), lax.axis_index("c")
        self.sidx = jnp.stack([2 * xi + yi, 2 * (1 - xi) + yi, 2 * xi + (1 - yi), 2 * (1 - xi) + (1 - yi), ci]).astype(jnp.int32)
        self.shards, self.rest_f32, self.full = {0: w_in_shard}, list(rest_f32), {}
        self.g, self.own, self.pb, self.half, self.red = {}, {}, {}, {}, {}

    def to_cast(self):
        return self.rest_f32

    def cast_done(self, casts):
        self.shards.update(zip(REST, casts))

    def weight(self, wi):
        return self.full[wi].reshape(D_MODEL, D_MODEL) if wi in (2, 3) else self.full[wi]

    def grads(self, by_wi):
        self.g.update(by_wi)

    def _exchange(self, stage, wis):
        pick = lambda table: [table[wi] for wi in wis]
        if stage == "ring":
            return _ex_gather_ring(wis, pick(self.shards))
        if stage == "ici":
            return _ex_gather_ici(wis, pick(self.shards))
        if stage == "both":
            return _ex_gather_ici(wis, pick(self.shards), then_d2d=True)
        if stage == "d2d":
            return _ex_gather_d2d(wis, pick(self.full))
        if stage == "pair":
            return _ex_pair(wis, [self.g["in_sent"] if wi == 0 else self.g[wi] for wi in wis])
        if stage == "chip":
            return _ex_chip(wis, pick(self.pb))
        return _ex_share(wis, pick(self.half))

    def _landed(self, stage, wis, res):
        for wi, r in zip(wis, res):
            if stage in ("ring", "ici", "d2d", "both"):
                self.full[wi] = r
            elif stage == "pair":
                self.own[wi], self.pb[wi] = _pair_sum(wi, self.g["in_kept"] if wi == 0 else self.g[wi], r, self.sidx)
            elif stage == "chip":
                self.half[wi] = _chip_sum(wi, self.own[wi], r, self.sidx)
            else:
                self.red[wi] = r

    def carry(self, point):
        return [self._exchange(stage, wis) for stage, wis in self.SCHEDULE.get(point, ())]

    def took(self, point, xres):
        for (stage, wis), res in zip(self.SCHEDULE.get(point, ()), xres):
            self._landed(stage, wis, res)

    def last_share(self):
        return self._exchange("share", GROUP_IN)

    def reduced(self, last_shared):
        self._landed("share", GROUP_IN, last_shared)
        return [self.red[wi] for wi in range(N_W)]


def kernel(x, norm_mix_g, w_in, w_out_attn, w_out_ret, w_out, norm_ffn_g, w_ffn_gate, w_ffn_up, w_ffn_down, norm_final_g, loss_target, m_norm_mix_g, m_w_in, m_w_out_attn, m_w_out_ret, m_w_out, m_norm_ffn_g, m_w_ffn_gate, m_w_ffn_up, m_w_ffn_down, m_norm_final_g, v_norm_mix_g, v_w_in, v_w_out_attn, v_w_out_ret, v_w_out, v_norm_ffn_g, v_w_ffn_gate, v_w_ffn_up, v_w_ffn_down, v_norm_final_g):
    ws = (w_in, w_out_attn, w_out_ret, w_out, w_ffn_gate, w_ffn_up, w_ffn_down)
    ms = (m_w_in, m_w_out_attn, m_w_out_ret, m_w_out, m_w_ffn_gate, m_w_ffn_up, m_w_ffn_down)
    vs = (v_w_in, v_w_out_attn, v_w_out_ret, v_w_out, v_w_ffn_gate, v_w_ffn_up, v_w_ffn_down)

    def shard2d(a, wi):
        return jnp.swapaxes(a[0], 0, 1) if wi in W_TRANSPOSED else a.reshape(W_SHARD[wi])

    def as_given(a2d, wi):
        return jnp.swapaxes(a2d, 0, 1)[None] if wi in W_TRANSPOSED else a2d.reshape(ws[wi].shape)

    comm = _MeshComm(_cast_bf16(shard2d(ws[0], 0)), [shard2d(ws[wi], wi) for wi in REST])
    g3 = norm_final_g.reshape(1, D_MODEL)
    loss_p, grad_x, gain_g = _step(x[0], loss_target[0], norm_mix_g, norm_ffn_g, g3, comm)

    pad8 = lambda rows: jnp.concatenate([r.reshape(1, D_MODEL) for r in rows]
                                        + [jnp.zeros((8 - len(rows), D_MODEL), F32)], axis=0)
    gathered, shared = _gain_allgather(pad8((*gain_g, jnp.tile(loss_p[0:1], (1, D_MODEL // 128)))), comm.last_share())
    gred = comm.reduced(shared)

    def adam(name, wis):
        two_d = lambda arrs: [shard2d(arrs[wi], wi) for wi in wis]
        return _adamw(name, two_d(ws), [gred[wi].reshape(W_SHARD[wi]) for wi in wis], two_d(ms), two_d(vs))

    updates = dict(zip(REST + GROUP_IN, adam("adamw_rest", REST) + adam("adamw_w_in", GROUP_IN)))
    outs_g, outs_d, outs_m, outs_v = ([as_given(updates[wi][k], wi) for wi in range(N_W)] for k in range(4))

    gg, gd, gm, gv = _gain_update(gathered, pad8((norm_mix_g, norm_ffn_g, norm_final_g)),
                                  pad8((m_norm_mix_g, m_norm_ffn_g, m_norm_final_g)),
                                  pad8((v_norm_mix_g, v_norm_ffn_g, v_norm_final_g)))
    loss = gg[3, 0]

    def assemble(gain_rows, wlist):
        return (gain_rows[0:1], wlist[0], wlist[1], wlist[2], wlist[3], gain_rows[1:2],
                wlist[4], wlist[5], wlist[6], gain_rows[2])

    return (loss, grad_x[None], *assemble(gg, outs_g), *assemble(gd, outs_d), *assemble(gm, outs_m), *assemble(gv, outs_v))
```

```python
import functools

import numpy as np
import jax
import jax.numpy as jnp
from jax import lax
from jax.experimental import pallas as pl
from jax.experimental.pallas import tpu as pltpu

F32, BF16 = jnp.float32, jnp.bfloat16
SDS = jax.ShapeDtypeStruct
MESH = pl.DeviceIdType.MESH

D_MODEL = 1024
PROJ_W = 9728
COLB = 512
N_COLB = PROJ_W // COLB
QA_B, KA_B, VA_B = 0, 3, 6
QR_B, KR_B = 9, 10
FFN_HID = 2816
N_SHARD = 4
HID_S = FFN_HID // N_SHARD
W_IN_S = PROJ_W // N_SHARD
DILATIONS = (1, 4, 16)
BLK = 128
RET_HEADS = 4
ROPE_THETA = 10000.0
NORM_EPS = 1e-6
ADAM_LR, ADAM_B1, ADAM_B2, ADAM_EPS, ADAM_WD, ADAM_STEP = 0.001, 0.9, 0.999, 1e-08, 0.01, 10
VMEM_LIMIT = 56 << 20


def _cparams(*sem):
    return pltpu.CompilerParams(dimension_semantics=sem or None, vmem_limit_bytes=VMEM_LIMIT)


def _dot(a, b):
    return jnp.dot(a, b, preferred_element_type=F32)


def _dot_nt(a, b):
    return lax.dot_general(a, b, (((1,), (1,)), ((), ())), preferred_element_type=F32)


def _dot_tn(a, b):
    return lax.dot_general(a, b, (((0,), (0,)), ((), ())), preferred_element_type=F32)


def _row_pieces(tm, sub=512):
    return [slice(i, i + sub) for i in range(0, tm, sub)]


def _sigmoid(z):
    return 0.5 * jnp.tanh(0.5 * z) + 0.5


ANY = pl.BlockSpec(memory_space=pl.ANY)


class _Exchange:
    def __init__(self, ins, out_shapes, aliases, n_sem, n_loc, build):
        self.ins, self.out_shapes, self.aliases = list(ins), list(out_shapes), dict(aliases)
        self.n_sem, self.n_loc, self.build = n_sem, n_loc, build

    def sems(self):
        return [pltpu.SemaphoreType.DMA((self.n_sem,)), pltpu.SemaphoreType.DMA((self.n_sem,)),
                pltpu.SemaphoreType.DMA((max(self.n_loc, 1),))]


def _carrier_call(body, args, *, out_shape, grid, in_specs, out_specs, scratch_shapes=(), sem, name, exchanges=(),
                  prefetch=None, in_out_aliases=None):
    out_shape, out_specs = tuple(out_shape), tuple(out_specs)
    n_in, n_out, n_scr = len(args), len(out_shape), len(scratch_shapes)
    n_pre = 0 if prefetch is None else 1
    x_args, x_outs, x_scr, spans = [], [], [], []
    aliases = {n_pre + a: o for a, o in (in_out_aliases or {}).items()}
    for ex in exchanges:
        i0, o0 = len(x_args), len(x_outs)
        for a, o in ex.aliases.items():
            aliases[n_pre + n_in + i0 + a] = n_out + o0 + o
        x_args += ex.ins
        x_outs += ex.out_shapes
        x_scr += ex.sems()
        spans.append((i0, len(ex.ins), o0, len(ex.out_shapes)))
    nx_in, nx_out = len(x_args), len(x_outs)

    def wrapped(*refs):
        refs = refs[n_pre:]
        ins, xin = refs[:n_in], refs[n_in:n_in + nx_in]
        o_base = n_in + nx_in
        outs, xout = refs[o_base:o_base + n_out], refs[o_base + n_out:o_base + n_out + nx_out]
        s_base = o_base + n_out + nx_out
        scr, xs = refs[s_base:s_base + n_scr], refs[s_base + n_scr:]

        def built(e):
            i0, ni, o0, no = spans[e]
            return exchanges[e].build(xin[i0:i0 + ni], xout[o0:o0 + no], *xs[3 * e:3 * e + 3])

        if exchanges:
            first = functools.reduce(jnp.logical_and, [pl.program_id(k) == 0 for k in range(len(grid))])
            last = functools.reduce(jnp.logical_and, [pl.program_id(k) == grid[k] - 1 for k in range(len(grid))])

            @pl.when(first)
            def _():
                for e in range(len(exchanges)):
                    for cp in built(e)[0]:
                        cp.start()

        body(*ins, *outs, *scr)

        if exchanges:
            @pl.when(last)
            def _():
                for e in range(len(exchanges)):
                    for w in built(e)[1]:
                        w()

    all_in, all_out = list(in_specs) + [ANY] * nx_in, out_specs + tuple([ANY] * nx_out)
    all_scr = list(scratch_shapes) + x_scr
    cparams = _cparams(*(sem if not exchanges else ("arbitrary",) * len(grid)))
    if prefetch is None:
        res = pl.pallas_call(wrapped, out_shape=out_shape + tuple(x_outs), grid=grid, in_specs=all_in, out_specs=all_out,
                             scratch_shapes=all_scr, input_output_aliases=aliases, compiler_params=cparams,
                             name=name)(*args, *x_args)
    else:
        gs = pltpu.PrefetchScalarGridSpec(num_scalar_prefetch=1, grid=grid, in_specs=all_in, out_specs=all_out,
                                          scratch_shapes=all_scr)
        res = pl.pallas_call(wrapped, out_shape=out_shape + tuple(x_outs), grid_spec=gs, input_output_aliases=aliases,
                             compiler_params=cparams, name=name)(prefetch, *args, *x_args)
    xres = [tuple(res[n_out + o0:n_out + o0 + no]) for (_, _, o0, no) in spans]
    return tuple(res[:n_out]), xres


def _tables(S):
    f32 = np.float32
    pos = np.arange(S, dtype=f32)
    lane = np.arange(128)
    inv = (f32(ROPE_THETA) ** (-np.arange(0, 64, 2, dtype=f32) / f32(64))).astype(f32)
    ang = (pos[:, None] * inv[None, :]).astype(np.float64)
    idx = (lane % 64) % 32
    c, s = np.cos(ang)[:, idx], np.sin(ang)[:, idx]
    first = ((lane % 64) < 32)[None, :]
    rope = np.stack([c, np.where(first, 0.0, s), np.where(first, -s, 0.0)])
    base = (f32(1.0) / (f32(ROPE_THETA) ** np.linspace(0.0, 1.0, 64, dtype=f32))).astype(f32)
    ang2 = (pos[:, None] * base[None, :]).astype(np.float64)
    c2, s2 = np.cos(ang2)[:, lane // 2], np.sin(ang2)[:, lane // 2]
    even = (lane % 2 == 0)[None, :]
    th = np.stack([c2, np.where(even, 0.0, s2), np.where(even, -s2, 0.0)])
    return np.stack([rope, th, th * (128 ** -0.5)]).astype(f32)


def _rot(a, c, sa, sb, shift):
    return a * c + pltpu.roll(a, shift, 1) * sa + pltpu.roll(a, 128 - shift, 1) * sb


def _unrot(g, c, sa, sb, shift):
    return g * c + pltpu.roll(g * sa, 128 - shift, 1) + pltpu.roll(g * sb, shift, 1)


def _ret_consts():
    h = np.arange(RET_HEADS, dtype=np.float64)
    log_g = np.log1p(-(2.0 ** (-5.0 - h)))
    idx = np.arange(BLK, dtype=np.float64)
    diff = idx[:, None] - idx[None, :]
    dmask = np.where(diff[None] >= 0, np.exp(np.maximum(diff, 0.0)[None] * log_g[:, None, None]), 0.0)
    zeta = np.exp((BLK - 1 - idx)[None, :] * log_g[:, None])
    xi = np.exp((idx + 1.0)[None, :] * log_g[:, None])
    dec = np.exp(BLK * log_g)
    rep = lambda v: np.broadcast_to(v[:, :, None], (RET_HEADS, BLK, 128))
    return (jnp.asarray(dmask, F32), jnp.asarray(rep(zeta), F32), jnp.asarray(rep(xi), F32),
            jnp.asarray(np.broadcast_to(dec[:, None, None], (RET_HEADS, 8, 256)), F32))


def _rms_fwd(x, g, to_cast=(), exchanges=()):
    S = x.shape[0]
    steps = 4
    tm = S // steps
    n_c = len(to_cast)

    def body(x_ref, g_ref, *refs):
        c_in, (h_ref, ht_ref), c_out = refs[:n_c], refs[n_c:n_c + 2], refs[n_c + 2:]
        for rows in _row_pieces(tm, 512):
            xv = x_ref[rows, :]
            r = lax.rsqrt(jnp.mean(xv * xv, axis=-1, keepdims=True) + NORM_EPS)
            h = xv * r * g_ref[...]
            h_ref[rows, :] = h.astype(BF16)
            ht_ref[:, rows] = h.T.astype(BF16)
        for a_ref, o_ref in zip(c_in, c_out):
            o_ref[...] = a_ref[...].astype(BF16)

    slab = lambda a: pl.BlockSpec((a.shape[0] // steps, a.shape[1]), lambda i: (i, 0))
    return _carrier_call(
        body, (x, g, *to_cast),
        out_shape=(SDS((S, D_MODEL), BF16), SDS((D_MODEL, S), BF16), *[SDS(a.shape, BF16) for a in to_cast]),
        grid=(steps,),
        in_specs=[pl.BlockSpec((tm, D_MODEL), lambda i: (i, 0)), pl.BlockSpec((1, D_MODEL), lambda i: (0, 0))]
        + [slab(a) for a in to_cast],
        out_specs=(pl.BlockSpec((tm, D_MODEL), lambda i: (i, 0)), pl.BlockSpec((D_MODEL, tm), lambda i: (0, i)),
                   *[slab(a) for a in to_cast]),
        sem=("parallel",), name="rms_fwd", exchanges=exchanges)


def _in_proj(h, w_in, tab, exchanges=()):
    S = h.shape[0]
    tm = min(S, 4096)

    def body(h_ref, w_ref, t_ref, o_ref):
        j = pl.program_id(1)
        is_rope = j < 6
        is_theta = (j == QR_B) | (j == KR_B)
        sub = 512

        def rotated(shift):
            for i in range(tm // sub):
                rows = slice(i * sub, (i + 1) * sub)
                acc = _dot(h_ref[rows, :], w_ref[...])
                c, sa, sb = t_ref[0, 0, rows, :], t_ref[0, 1, rows, :], t_ref[0, 2, rows, :]
                for k in range(COLB // 128):
                    sl = slice(k * 128, (k + 1) * 128)
                    o_ref[rows, sl] = _rot(acc[:, sl], c, sa, sb, shift).astype(BF16)

        @pl.when(is_rope)
        def _():
            rotated(32)

        @pl.when(is_theta)
        def _():
            rotated(1)

        @pl.when(jnp.logical_not(is_rope | is_theta))
        def _():
            o_ref[...] = _dot(h_ref[...], w_ref[...]).astype(BF16)

    def tab_map(i, j):
        return (jnp.where(j == QR_B, 1, jnp.where(j == KR_B, 2, 0)), 0, i, 0)

    (proj,), xres = _carrier_call(
        body, (h, w_in, tab), out_shape=(SDS((S, PROJ_W), BF16),), grid=(S // tm, N_COLB),
        in_specs=[pl.BlockSpec((tm, D_MODEL), lambda i, j: (i, 0)),
                  pl.BlockSpec((D_MODEL, COLB), lambda i, j: (0, j)),
                  pl.BlockSpec((1, 3, tm, 128), tab_map)],
        out_specs=(pl.BlockSpec((tm, COLB), lambda i, j: (i, j)),),
        sem=("parallel", "arbitrary"), name="in_proj", exchanges=exchanges)
    return proj, xres


def _band_mask(n):
    qi = lax.broadcasted_iota(jnp.int32, (BLK, 2 * BLK), 0)
    kj = lax.broadcasted_iota(jnp.int32, (BLK, 2 * BLK), 1)
    dist = BLK + qi - kj
    return (dist >= 0) & (dist <= BLK) & ((kj >= BLK) | (n > 0))


def _qkv_col(d, gi):
    if d == 1:
        return lambda t, r: 3 * t + gi
    return lambda t, r: 3 * r + t


def _attn_fwd(qkv, d, gi, exchanges=()):
    L = qkv.shape[0]
    nb = L // BLK
    qb = next(n for n in (4, 2, 1) if nb % n == 0)

    def body(q_ref, kc_ref, kp_ref, vc_ref, vp_ref, o_ref, lse_ref):
        i = pl.program_id(1)
        lane = lax.broadcasted_iota(jnp.int32, (BLK, 128), 1)
        lo = lane < 64
        chunks = [slice(c * 128, (c + 1) * 128) for c in range(4)]
        scores, vals, masks = [], [], []
        for b in range(qb):
            rows = slice(b * BLK, (b + 1) * BLK)
            before = slice((b - 1) * BLK, b * BLK)
            mask = _band_mask(qb * i + b)
            masks.append(jnp.concatenate([mask, mask], axis=0))
            for sl in chunks:
                q = q_ref[rows, sl]
                k = jnp.concatenate([kp_ref[:, sl] if b == 0 else kc_ref[before, sl], kc_ref[rows, sl]], axis=0)
                vals.append(jnp.concatenate([vp_ref[:, sl] if b == 0 else vc_ref[before, sl], vc_ref[rows, sl]], axis=0))
                q2 = jnp.concatenate([jnp.where(lo, q, jnp.zeros_like(q)), jnp.where(lo, jnp.zeros_like(q), q)], axis=0)
                scores.append(_dot_nt(q2, k))
        probs, lses = [], []
        for j, s in enumerate(scores):
            b, c = divmod(j, 4)
            s = jnp.where(masks[b], s * 0.125, jnp.float32(-1e30))
            m = jnp.max(s, axis=-1, keepdims=True)
            p = jnp.exp(s - m)
            l = jnp.sum(p, axis=-1, keepdims=True)
            probs.append((p * (1.0 / l)).astype(BF16))
            lses.append(m + jnp.log(l))
        for j, (p, v) in enumerate(zip(probs, vals)):
            b, c = divmod(j, 4)
            o2 = _dot(p, v)
            o_ref[b * BLK:(b + 1) * BLK, chunks[c]] = jnp.where(lo, o2[:BLK], o2[BLK:])
        for b in range(qb):
            lse_all = jnp.zeros((BLK, 128), F32)
            for c in range(4):
                lse = lses[4 * b + c]
                lse_all = jnp.where(lane // 16 == 2 * c, lse[:BLK], jnp.where(lane // 16 == 2 * c + 1, lse[BLK:], lse_all))
            lse_ref[b * BLK:(b + 1) * BLK, :] = lse_all

    prev = lambda i: jnp.maximum(qb * i - 1, 0)
    col = _qkv_col(d, gi)
    return _carrier_call(
        body, (qkv,) * 5, out_shape=(SDS((L, d * 512), F32), SDS((L, d * 128), F32)), grid=(d, nb // qb),
        in_specs=[pl.BlockSpec((qb * BLK, 512), lambda r, i: (i, col(0, r))),
                  pl.BlockSpec((qb * BLK, 512), lambda r, i: (i, col(1, r))),
                  pl.BlockSpec((BLK, 512), lambda r, i: (prev(i), col(1, r))),
                  pl.BlockSpec((qb * BLK, 512), lambda r, i: (i, col(2, r))),
                  pl.BlockSpec((BLK, 512), lambda r, i: (prev(i), col(2, r)))],
        out_specs=(pl.BlockSpec((qb * BLK, 512), lambda r, i: (i, r)),
                   pl.BlockSpec((qb * BLK, 128), lambda r, i: (i, r))),
        sem=("parallel", "arbitrary"), name=f"attn_fwd_g{gi}", exchanges=exchanges)


def _qkv_to_sub(proj, d, gi):
    S = proj.shape[0]
    tm = 512
    n = tm // d

    def body(q_ref, k_ref, v_ref, o_ref, scr):
        for t, ref in enumerate((q_ref, k_ref, v_ref)):
            for c in range(4):
                scr[c] = ref[:, c * 128:(c + 1) * 128].astype(F32)
            for r in range(d):
                for c in range(4):
                    col = (3 * r + t) * 512 + c * 128
                    o_ref[:, col:col + 128] = scr[c, pl.ds(r, n, stride=d), :].astype(BF16)

    return pl.pallas_call(
        body, out_shape=SDS((S // d, d * 1536), BF16), grid=(S // tm,),
        in_specs=[pl.BlockSpec((tm, 512), lambda i, b=b: (i, b + gi)) for b in (QA_B, KA_B, VA_B)],
        out_specs=pl.BlockSpec((n, d * 1536), lambda i: (i, 0)),
        scratch_shapes=[pltpu.VMEM((4, tm, 128), F32)],
        compiler_params=_cparams("parallel"), name=f"qkv_to_sub_g{gi}")(proj, proj, proj)


def _attn_merge(os_, lses):
    S = os_[0].shape[0]
    tm = 512

    def body(o0, o1, o2, l0, l1, l2, att_ref, lt_ref, lt1_ref, lt2_ref, so1, so2, sl1, sl2):
        lo = lax.broadcasted_iota(jnp.int32, (tm, 128), 1) < 64

        def natural(ref, d, scr, width):
            nch = width // 128
            if d == 1:
                return [ref[:, c * 128:(c + 1) * 128] for c in range(nch)]
            for r in range(d):
                for c in range(nch):
                    scr[c, pl.ds(r, tm // d, stride=d), :] = ref[:, r * width + c * 128:r * width + (c + 1) * 128]
            return [scr[c] for c in range(nch)]

        ls = [natural(l, d, s, 128)[0] for l, d, s in zip((l0, l1, l2), DILATIONS, (None, sl1, sl2))]
        m = jnp.maximum(jnp.maximum(ls[0], ls[1]), ls[2])
        es = [jnp.exp(v - m) for v in ls]
        z = es[0] + es[1] + es[2]
        lt = m + jnp.log(z)
        lt_ref[...] = lt
        sl1[0] = lt
        for ref, d in ((lt1_ref, DILATIONS[1]), (lt2_ref, DILATIONS[2])):
            for r in range(d):
                ref[:, r * 128:(r + 1) * 128] = sl1[0, pl.ds(r, tm // d, stride=d), :]
        ws = [e / z for e in es]
        o_nat = [natural(o, d, s, 512) for o, d, s in zip((o0, o1, o2), DILATIONS, (None, so1, so2))]
        for c in range(4):
            acc = jnp.zeros((tm, 128), F32)
            for g in range(3):
                w_lo = jnp.broadcast_to(ws[g][:, 32 * c:32 * c + 1], (tm, 128))
                w_hi = jnp.broadcast_to(ws[g][:, 32 * c + 16:32 * c + 17], (tm, 128))
                acc = acc + jnp.where(lo, w_lo, w_hi) * o_nat[g][c]
            att_ref[:, c * 128:(c + 1) * 128] = acc.astype(BF16)

    sub = lambda w: [pl.BlockSpec((tm // d, d * w), lambda i: (i, 0)) for d in DILATIONS]
    att, *lts = pl.pallas_call(
        body, out_shape=(SDS((S, 512), BF16), *[SDS((S // d, d * 128), F32) for d in DILATIONS]), grid=(S // tm,),
        in_specs=sub(512) + sub(128),
        out_specs=(pl.BlockSpec((tm, 512), lambda i: (i, 0)), *sub(128)),
        scratch_shapes=[pltpu.VMEM((4, tm, 128), F32), pltpu.VMEM((4, tm, 128), F32),
                        pltpu.VMEM((1, tm, 128), F32), pltpu.VMEM((1, tm, 128), F32)],
        compiler_params=_cparams("parallel"), name="attn_merge")(*os_, *lses)
    return att, lts


def _assemble_dproj(att_grads, dproj):
    S = dproj.shape[0]
    tm = 256

    def body(*refs):
        a = [refs[3 * t:3 * t + 3] for t in range(3)]
        dp_prev, o_ref, scr = refs[9:]
        for t in range(3):
            for g, d in enumerate(DILATIONS):
                base = (3 * t + g) * COLB
                if d == 1:
                    o_ref[:, base:base + COLB] = a[t][g][...]
                    continue
                for c in range(4):
                    for r in range(d):
                        scr[c, pl.ds(r, tm // d, stride=d), :] = a[t][g][:, r * 512 + c * 128:r * 512 + (c + 1) * 128].astype(F32)
                    o_ref[:, base + c * 128:base + (c + 1) * 128] = scr[c].astype(BF16)

    sub = [pl.BlockSpec((tm // d, d * 512), lambda i: (i, 0)) for d in DILATIONS]
    flat = [att_grads[t][g] for t in range(3) for g in range(3)]
    return pl.pallas_call(
        body, out_shape=SDS((S, PROJ_W), BF16), grid=(S // tm,),
        in_specs=sub * 3 + [ANY], out_specs=pl.BlockSpec((tm, 9 * COLB), lambda i: (i, 0)),
        scratch_shapes=[pltpu.VMEM((4, tm, 128), F32)], input_output_aliases={9: 0},
        compiler_params=_cparams("parallel"), name="assemble_dproj")(*flat, dproj)


def _ret_fwd(proj, consts, exchanges=()):
    S = proj.shape[0]
    nc = S // BLK
    dmask, zeta, xi, dec = consts

    def body(q_ref, k_ref, v0_ref, v1_ref, g0_ref, g1_ref, dm_ref, z_ref, x_ref, dec_ref,
             y_ref, rn_ref, rs_ref, st_ref, R):
        @pl.when(pl.program_id(0) == 0)
        def _():
            R[...] = jnp.zeros_like(R)

        lane16 = lax.broadcasted_iota(jnp.int32, (BLK, 128), 1) // 16
        rs_all = jnp.zeros((BLK, 128), F32)
        first = []
        for h in range(RET_HEADS):
            hs = slice(h * 128, (h + 1) * 128)
            q, k = q_ref[:, hs], k_ref[:, hs]
            v = (v0_ref if h < 2 else v1_ref)[:, (h % 2) * 256:(h % 2 + 1) * 256]
            Rb = R[h].astype(BF16)
            st_ref[h] = Rb
            kz = (k.astype(F32) * z_ref[h]).astype(BF16)
            first.append((v, _dot_nt(q, k), _dot((q.astype(F32) * x_ref[h]).astype(BF16), Rb), _dot_tn(kz, v)))
        masked = [(s * dm_ref[h]).astype(BF16) for h, (_, s, _, _) in enumerate(first)]
        for h in range(RET_HEADS):
            vs = slice((h % 2) * 256, (h % 2 + 1) * 256)
            os_ = slice(h * 256, (h + 1) * 256)
            v, _, cross, kv = first[h]
            o = _dot(masked[h], v) + cross
            R[h] = R[h] * dec_ref[h, 0:1, :] + kv
            mu = jnp.mean(o, axis=-1, keepdims=True)
            oc = o - mu
            rstd = lax.rsqrt(jnp.mean(oc * oc, axis=-1, keepdims=True) + NORM_EPS)
            rn = oc * rstd
            gr = (g0_ref if h < 2 else g1_ref)[:, vs].astype(F32)
            y_ref[:, os_] = (rn * gr * _sigmoid(gr)).astype(BF16)
            rn_ref[:, os_] = rn.astype(BF16)
            rs_all = jnp.where(lane16 == h, rstd, rs_all)
        rs_ref[...] = rs_all

    cst = lambda shape: pl.BlockSpec(shape, lambda c: (0, 0, 0))
    blk = lambda j: pl.BlockSpec((BLK, 512), lambda c: (c, j))
    return _carrier_call(
        body, (proj, proj, proj, proj, proj, proj, dmask, zeta, xi, dec),
        out_shape=(SDS((S, 1024), BF16), SDS((S, 1024), BF16), SDS((S, 128), F32), SDS((RET_HEADS, nc, BLK, 256), BF16)),
        grid=(nc,),
        in_specs=[blk(QR_B), blk(KR_B), blk(11), blk(12), blk(13), blk(14),
                  cst((RET_HEADS, BLK, BLK)), cst((RET_HEADS, BLK, 128)), cst((RET_HEADS, BLK, 128)), cst((RET_HEADS, 8, 256))],
        out_specs=(pl.BlockSpec((BLK, 1024), lambda c: (c, 0)), pl.BlockSpec((BLK, 1024), lambda c: (c, 0)),
                   pl.BlockSpec((BLK, 128), lambda c: (c, 0)),
                   pl.BlockSpec((RET_HEADS, None, BLK, 256), lambda c: (0, c, 0, 0))),
        scratch_shapes=[pltpu.VMEM((RET_HEADS, BLK, 256), F32)],
        sem=("arbitrary",), name="ret_fwd", exchanges=exchanges)


def _mix_out(att, yrin, proj, wa, wr, wo, x, g2, exchanges=()):
    S = x.shape[0]
    tm = 512
    gate0 = 15 * COLB

    def body(a_ref, y_ref, ga_ref, gr_ref, wa_ref, wr_ref, wo_ref, x_ref, g_ref, m_ref, ya_ref, yr_ref, x1_ref, h2_ref):
        pieces = _row_pieces(tm, 256)
        branches = [(_dot(a_ref[rows, :], wa_ref[...]), _dot(y_ref[rows, :], wr_ref[...])) for rows in pieces]
        merged = []
        for rows, (ya, yr) in zip(pieces, branches):
            m = (_sigmoid(ga_ref[rows, :].astype(F32)) * ya + _sigmoid(gr_ref[rows, :].astype(F32)) * yr).astype(BF16)
            m_ref[rows, :] = m
            ya_ref[rows, :] = ya.astype(BF16)
            yr_ref[rows, :] = yr.astype(BF16)
            merged.append(m)
        for rows, m in zip(pieces, merged):
            x1 = x_ref[rows, :] + _dot(m, wo_ref[...])
            x1_ref[rows, :] = x1
            r = lax.rsqrt(jnp.mean(x1 * x1, axis=-1, keepdims=True) + NORM_EPS)
            h2_ref[rows, :] = (x1 * r * g_ref[...]).astype(BF16)

    row = lambda w: pl.BlockSpec((tm, w), lambda i: (i, 0))
    cols = lambda c0: pl.BlockSpec((pl.Element(tm), pl.Element(D_MODEL)), lambda i: (i * tm, c0))
    resident = lambda r, c: pl.BlockSpec((r, c), lambda i: (0, 0), pipeline_mode=pl.Buffered(1))
    return _carrier_call(
        body, (att, yrin, proj, proj, wa, wr, wo, x, g2),
        out_shape=(SDS((S, D_MODEL), BF16),) * 3 + (SDS((S, D_MODEL), F32), SDS((S, D_MODEL), BF16)), grid=(S // tm,),
        in_specs=[row(512), row(D_MODEL), cols(gate0), cols(gate0 + D_MODEL), resident(512, D_MODEL),
                  resident(D_MODEL, D_MODEL), resident(D_MODEL, D_MODEL), row(D_MODEL),
                  pl.BlockSpec((1, D_MODEL), lambda i: (0, 0))],
        out_specs=(row(D_MODEL),) * 5, sem=("parallel",), name="mix_out", exchanges=exchanges)


def _ffn_up(h2, wg, wu, exchanges=()):
    S = h2.shape[0]
    tm = min(S, 2048)

    def body(h_ref, wg_ref, wu_ref, g_ref, u_ref, a_ref):
        for rows in _row_pieces(tm):
            hv = h_ref[rows, :]
            g = _dot_nt(hv, wg_ref[...])
            u = _dot_nt(hv, wu_ref[...])
            g_ref[rows, :] = g.astype(BF16)
            u_ref[rows, :] = u.astype(BF16)
            a_ref[rows, :] = (g * _sigmoid(g) * u).astype(BF16)

    wspec = pl.BlockSpec((None, HID_S, D_MODEL), lambda i, s: (s, 0, 0))
    ospec = pl.BlockSpec((None, tm, HID_S), lambda i, s: (s, i, 0))
    return _carrier_call(
        body, (h2, wg, wu), out_shape=(SDS((N_SHARD, S, HID_S), BF16),) * 3, grid=(S // tm, N_SHARD),
        in_specs=[pl.BlockSpec((tm, D_MODEL), lambda i, s: (i, 0)), wspec, wspec],
        out_specs=(ospec, ospec, ospec),
        sem=("parallel", "arbitrary"), name="ffn_up", exchanges=exchanges)


def _ffn_down_loss(act, wd, x1, g3, tgt):
    S = x1.shape[0]
    tm = 512

    def body(a_ref, w_ref, x_ref, g_ref, t_ref, dx_ref, dxb_ref, dg_ref, ls_ref):
        @pl.when(pl.program_id(0) == 0)
        def _():
            dg_ref[...] = jnp.zeros_like(dg_ref)
            ls_ref[...] = jnp.zeros_like(ls_ref)

        g = g_ref[...]
        for rows in _row_pieces(tm, 256):
            y = _dot(a_ref[0, rows, :], w_ref[0])
            for s in range(1, N_SHARD):
                y = y + _dot(a_ref[s, rows, :], w_ref[s])
            x2 = x_ref[rows, :] + y
            r = lax.rsqrt(jnp.mean(x2 * x2, axis=-1, keepdims=True) + NORM_EPS)
            xh = x2 * r
            err = xh * g - t_ref[rows, :]
            ls_ref[...] += jnp.sum(jnp.sum(err * err, axis=-1, keepdims=True), axis=0, keepdims=True) * (0.5 / D_MODEL)
            dy = err * (1.0 / D_MODEL)
            dg_ref[...] += jnp.sum(dy * xh, axis=0, keepdims=True)
            dxh = dy * g
            dx = r * (dxh - xh * jnp.mean(dxh * xh, axis=-1, keepdims=True))
            dx_ref[rows, :] = dx
            dxb_ref[rows, :] = dx.astype(BF16)

    row = pl.BlockSpec((tm, D_MODEL), lambda i: (i, 0))
    vec = pl.BlockSpec((1, D_MODEL), lambda i: (0, 0))
    return pl.pallas_call(
        body, out_shape=(SDS((S, D_MODEL), F32), SDS((S, D_MODEL), BF16), SDS((1, D_MODEL), F32), SDS((8, 128), F32)),
        grid=(S // tm,),
        in_specs=[pl.BlockSpec((N_SHARD, tm, HID_S), lambda i: (0, i, 0)),
                  pl.BlockSpec((N_SHARD, HID_S, D_MODEL), lambda i: (0, 0, 0), pipeline_mode=pl.Buffered(1)),
                  row, vec, row],
        out_specs=(row, row, vec, pl.BlockSpec((8, 128), lambda i: (0, 0))),
        compiler_params=_cparams("arbitrary"), name="ffn_down_loss")(act, wd, x1, g3, tgt)


def _ffn_bwd(dx2b, dx2, wd, wg, wu, gte, up, x1, g2):
    S = x1.shape[0]
    tm = 256

    def body(d_ref, dx2_ref, wd_ref, wg_ref, wu_ref, g_ref, u_ref, x_ref, gn_ref,
             dg_ref, du_ref, dx_ref, dxb_ref, dgn_ref):
        @pl.when(pl.program_id(0) == 0)
        def _():
            dgn_ref[...] = jnp.zeros_like(dgn_ref)

        d = d_ref[...]
        dacts = [_dot_nt(d, wd_ref[s]) for s in range(N_SHARD)]
        dgs, dus = [], []
        for s, da in enumerate(dacts):
            g = g_ref[s].astype(F32)
            sg = _sigmoid(g)
            dgs.append((da * u_ref[s].astype(F32) * sg * (1.0 + g * (1.0 - sg))).astype(BF16))
            dus.append((da * g * sg).astype(BF16))
            dg_ref[s] = dgs[s]
            du_ref[s] = dus[s]
        dh = _dot(dgs[0], wg_ref[0]) + _dot(dus[0], wu_ref[0])
        for s in range(1, N_SHARD):
            dh = dh + _dot(dgs[s], wg_ref[s]) + _dot(dus[s], wu_ref[s])
        xv = x_ref[...]
        r = lax.rsqrt(jnp.mean(xv * xv, axis=-1, keepdims=True) + NORM_EPS)
        xh = xv * r
        dgn_ref[...] += jnp.sum(dh * xh, axis=0, keepdims=True)
        dxh = dh * gn_ref[...]
        dx = dx2_ref[...] + r * (dxh - xh * jnp.mean(dxh * xh, axis=-1, keepdims=True))
        dx_ref[...] = dx
        dxb_ref[...] = dx.astype(BF16)

    row = pl.BlockSpec((tm, D_MODEL), lambda i: (i, 0))
    vec = pl.BlockSpec((1, D_MODEL), lambda i: (0, 0))
    aspec = pl.BlockSpec((N_SHARD, tm, HID_S), lambda i: (0, i, 0))
    resident = lambda shape: pl.BlockSpec(shape, lambda i: (0, 0, 0), pipeline_mode=pl.Buffered(1))
    return pl.pallas_call(
        body,
        out_shape=(SDS((N_SHARD, S, HID_S), BF16), SDS((N_SHARD, S, HID_S), BF16),
                   SDS((S, D_MODEL), F32), SDS((S, D_MODEL), BF16), SDS((1, D_MODEL), F32)),
        grid=(S // tm,),
        in_specs=[row, row, resident((N_SHARD, HID_S, D_MODEL)), resident((N_SHARD, HID_S, D_MODEL)),
                  resident((N_SHARD, HID_S, D_MODEL)), aspec, aspec, row, vec],
        out_specs=(aspec, aspec, row, row, vec),
        compiler_params=_cparams("arbitrary"), name="ffn_bwd")(dx2b, dx2, wd, wg, wu, gte, up, x1, g2)


def _wgrad(name, a, b, a_spec, b_spec, out_shape, out_spec, n_par, S):
    tk = min(S, 4096)

    def body(a_ref, b_ref, o_ref):
        @pl.when(pl.program_id(1) == 0)
        def _():
            o_ref[...] = jnp.zeros_like(o_ref)

        o_ref[...] += _dot_tn(a_ref[...], b_ref[...])

    return pl.pallas_call(
        body, out_shape=SDS(out_shape, F32), grid=(n_par, S // tk),
        in_specs=[a_spec(tk), b_spec(tk)], out_specs=out_spec,
        compiler_params=_cparams("parallel", "arbitrary"), name=name)(a, b)


def _mix_bwd(dx1b, wo, proj, ya, yr, wa, wr, att, exchanges=()):
    S = dx1b.shape[0]
    tm = 512
    gate0 = 15 * COLB

    def body(d_ref, wo_ref, ga_ref, gr_ref, ya_ref, yr_ref, wa_ref, wr_ref, att_ref,
             dya_ref, dyr_ref, dp_ref, dyi_ref, datt_ref, datt1_ref, datt2_ref, rho_ref, rho1_ref, rho2_ref,
             datt_scr, rho_scr):
        pieces = _row_pieces(tm, 256)
        dms = [_dot_nt(d_ref[rows, :], wo_ref[...]) for rows in pieces]
        branch = []
        for rows, dm in zip(pieces, dms):
            sa = _sigmoid(ga_ref[rows, :].astype(F32))
            sr = _sigmoid(gr_ref[rows, :].astype(F32))
            dya, dyr = (dm * sa).astype(BF16), (dm * sr).astype(BF16)
            dya_ref[rows, :] = dya
            dyr_ref[rows, :] = dyr
            dp_ref[rows, 0:D_MODEL] = (dm * ya_ref[rows, :].astype(F32) * sa * (1.0 - sa)).astype(BF16)
            dp_ref[rows, D_MODEL:2 * D_MODEL] = (dm * yr_ref[rows, :].astype(F32) * sr * (1.0 - sr)).astype(BF16)
            branch.append((dya, dyr))
        lane = lax.broadcasted_iota(jnp.int32, (256, 128), 1)
        lo = lane < 64
        for rows, (dya, dyr) in zip(pieces, branch):
            datt = _dot_nt(dya, wa_ref[...])
            datt_ref[rows, :] = datt.astype(BF16)
            dyi_ref[rows, :] = _dot_nt(dyr, wr_ref[...]).astype(BF16)
            prod = datt * att_ref[rows, :].astype(F32)
            rho = jnp.zeros((256, 128), F32)
            for c in range(4):
                pc = prod[:, c * 128:(c + 1) * 128]
                tot = jnp.sum(pc, axis=-1, keepdims=True)
                low = jnp.sum(jnp.where(lo, pc, 0.0), axis=-1, keepdims=True)
                rho = jnp.where(lane // 16 == 2 * c, low, jnp.where(lane // 16 == 2 * c + 1, tot - low, rho))
            rho_ref[rows, :] = rho
            rho_scr[0] = rho
            for c in range(4):
                datt_scr[c] = datt[:, c * 128:(c + 1) * 128]
            for d, dv_ref, rv_ref in ((DILATIONS[1], datt1_ref, rho1_ref), (DILATIONS[2], datt2_ref, rho2_ref)):
                n = 256 // d
                sub_rows = slice(rows.start // d, rows.start // d + n)
                for r in range(d):
                    rv_ref[sub_rows, r * 128:(r + 1) * 128] = rho_scr[0, pl.ds(r, n, stride=d), :]
                    for c in range(4):
                        col = r * 512 + c * 128
                        dv_ref[sub_rows, col:col + 128] = datt_scr[c, pl.ds(r, n, stride=d), :].astype(BF16)

    row = lambda w: pl.BlockSpec((tm, w), lambda i: (i, 0))
    sub = lambda w: [pl.BlockSpec((tm // d, d * w), lambda i: (i, 0)) for d in DILATIONS]
    cols = lambda c0, w: pl.BlockSpec((pl.Element(tm), pl.Element(w)), lambda i: (i * tm, c0))
    resident = lambda r, c: pl.BlockSpec((r, c), lambda i: (0, 0), pipeline_mode=pl.Buffered(1))
    (dya, dyr, dproj, dyrin, *views), xres = _carrier_call(
        body, (dx1b, wo, proj, proj, ya, yr, wa, wr, att),
        out_shape=(SDS((S, D_MODEL), BF16), SDS((S, D_MODEL), BF16), SDS((S, PROJ_W), BF16), SDS((S, D_MODEL), BF16),
                   *[SDS((S // d, d * 512), BF16) for d in DILATIONS], *[SDS((S // d, d * 128), F32) for d in DILATIONS]),
        grid=(S // tm,),
        in_specs=[row(D_MODEL), resident(D_MODEL, D_MODEL), cols(gate0, D_MODEL), cols(gate0 + D_MODEL, D_MODEL),
                  row(D_MODEL), row(D_MODEL), resident(512, D_MODEL), resident(D_MODEL, D_MODEL), row(512)],
        out_specs=(row(D_MODEL), row(D_MODEL), cols(gate0, 2 * D_MODEL), row(D_MODEL), *sub(512), *sub(128)),
        scratch_shapes=[pltpu.VMEM((4, 256, 128), F32), pltpu.VMEM((1, 256, 128), F32)],
        sem=("parallel",), name="mix_bwd", exchanges=exchanges)
    return (dya, dyr, dproj, dyrin, views[:3], views[3:]), xres


def _attn_bwd(qkv, datt, lse, rho, rtab, d, gi, exchanges=()):
    L = qkv.shape[0]
    nb = L // BLK
    T = d * nb

    def body(q_ref, kc_ref, kp_ref, vc_ref, vp_ref, do_ref, lse_ref, rho_ref, tq_ref, tk_ref,
             dq_ref, dk_ref, dv_ref, ck, cv):
        t = pl.program_id(0)
        n = jnp.minimum(t, T - 1) % nb

        @pl.when(t == 0)
        def _():
            ck[...] = jnp.zeros_like(ck)
            cv[...] = jnp.zeros_like(cv)

        def store_rot(ref, val, t_ref, c):
            sl = slice(c * 128, (c + 1) * 128)
            ref[:, sl] = _unrot(val, t_ref[0], t_ref[1], t_ref[2], 32).astype(BF16)

        @pl.when(t < T)
        def _():
            mask = _band_mask(n)
            mask2 = jnp.concatenate([mask, mask], axis=0)
            lo = lax.broadcasted_iota(jnp.int32, (BLK, 128), 1) < 64

            def stacked(a):
                return jnp.concatenate([jnp.where(lo, a, jnp.zeros_like(a)), jnp.where(lo, jnp.zeros_like(a), a)], axis=0)

            def head_cols(ref, c):
                return jnp.concatenate([jnp.broadcast_to(ref[:, 32 * c:32 * c + 1], (BLK, 2 * BLK)),
                                        jnp.broadcast_to(ref[:, 32 * c + 16:32 * c + 17], (BLK, 2 * BLK))], axis=0)

            ops, raw = [], []
            for c in range(4):
                sl = slice(c * 128, (c + 1) * 128)
                q2, do2 = stacked(q_ref[:, sl]), stacked(do_ref[:, sl])
                k = jnp.concatenate([kp_ref[:, sl], kc_ref[:, sl]], axis=0)
                v = jnp.concatenate([vp_ref[:, sl], vc_ref[:, sl]], axis=0)
                ops.append((q2, do2, k))
                raw.append((_dot_nt(q2, k), _dot_nt(do2, v)))
            grads = []
            for c, (s, dp) in enumerate(raw):
                p = jnp.where(mask2, jnp.exp(s * 0.125 - head_cols(lse_ref, c)), 0.0)
                grads.append(((p * (dp - head_cols(rho_ref, c)) * 0.125).astype(BF16), p.astype(BF16)))
            for c, ((q2, do2, k), (ds, pb)) in enumerate(zip(ops, grads)):
                sl = slice(c * 128, (c + 1) * 128)
                dq2 = _dot(ds, k)
                dq_c = jnp.where(lo, dq2[:BLK], dq2[BLK:])
                dk_c = _dot_tn(ds, q2)
                dv_c = _dot_tn(pb, do2)
                store_rot(dq_ref, dq_c, tq_ref, c)
                store_rot(dk_ref, ck[:, sl] + dk_c[:BLK], tk_ref, c)
                dv_ref[:, sl] = (cv[:, sl] + dv_c[:BLK]).astype(BF16)
                ck[:, sl] = dk_c[BLK:]
                cv[:, sl] = dv_c[BLK:]

        @pl.when(t == T)
        def _():
            for c in range(4):
                sl = slice(c * 128, (c + 1) * 128)
                store_rot(dk_ref, ck[:, sl], tk_ref, c)
            dv_ref[...] = cv[...].astype(BF16)

    blk_of = lambda t: (jnp.minimum(t, T - 1) % nb, jnp.minimum(t, T - 1) // nb)
    cur = lambda t: blk_of(t)
    prev = lambda t: (jnp.maximum(blk_of(t)[0] - 1, 0), blk_of(t)[1])
    fin = lambda t: blk_of(jnp.maximum(t - 1, 0))
    col = _qkv_col(d, gi)
    qkv_spec = lambda kind, which: pl.BlockSpec((BLK, 512), lambda t: (which(t)[0], col(kind, which(t)[1])))
    row_spec = lambda w, which: pl.BlockSpec((BLK, w), lambda t: which(t))
    tab_spec = lambda which: pl.BlockSpec((3, BLK, 128), lambda t: (0, *which(t)))
    return _carrier_call(
        body, (qkv, qkv, qkv, qkv, qkv, datt, lse, rho, rtab, rtab),
        out_shape=(SDS((L, d * 512), BF16),) * 3, grid=(T + 1,),
        in_specs=[qkv_spec(0, cur), qkv_spec(1, cur), qkv_spec(1, prev), qkv_spec(2, cur), qkv_spec(2, prev),
                  row_spec(512, cur), row_spec(128, cur), row_spec(128, cur), tab_spec(cur), tab_spec(fin)],
        out_specs=(row_spec(512, cur), row_spec(512, fin), row_spec(512, fin)),
        scratch_shapes=[pltpu.VMEM((BLK, 512), F32), pltpu.VMEM((BLK, 512), F32)],
        sem=("arbitrary",), name=f"attn_bwd_g{gi}", exchanges=exchanges)


def _ret_bwd(proj, rn, rstd, dyrin, states, tab, consts, dproj, exchanges=()):
    S = proj.shape[0]
    nc = S // BLK
    dmask, zeta, xi, dec = consts

    def body(q_ref, k_ref, v0_ref, v1_ref, g0_ref, g1_ref, rn_ref, rs_ref, dy_ref, st_ref, tq_ref, tk_ref,
             dm_ref, z_ref, x_ref, dec_ref, dp_prev, dp_ref, dR):
        dq_ref, dk_ref = dp_ref.at[:, 0:512], dp_ref.at[:, 512:1024]
        dv_ref, dgr_ref = dp_ref.at[:, 1024:2048], dp_ref.at[:, 2048:3072]

        @pl.when(pl.program_id(0) == 0)
        def _():
            dR[...] = jnp.zeros_like(dR)

        dobs = []
        for h in range(RET_HEADS):
            vs = slice((h % 2) * 256, (h % 2 + 1) * 256)
            os_ = slice(h * 256, (h + 1) * 256)
            gr = (g0_ref if h < 2 else g1_ref)[:, vs].astype(F32)
            sg = _sigmoid(gr)
            rn_v = rn_ref[:, os_].astype(F32)
            dyi = dy_ref[:, os_].astype(F32)
            dgr_ref[:, os_] = (dyi * rn_v * sg * (1.0 + gr * (1.0 - sg))).astype(BF16)
            drn = dyi * gr * sg
            rstd = jnp.broadcast_to(rs_ref[:, 16 * h:16 * h + 1], (BLK, 256))
            do = rstd * (drn - jnp.mean(drn, axis=-1, keepdims=True) - rn_v * jnp.mean(drn * rn_v, axis=-1, keepdims=True))
            dobs.append(do.astype(BF16))
        first = []
        for h in range(RET_HEADS):
            hs = slice(h * 128, (h + 1) * 128)
            q, k = q_ref[:, hs], k_ref[:, hs]
            v = (v0_ref if h < 2 else v1_ref)[:, (h % 2) * 256:(h % 2 + 1) * 256]
            dob, dRb = dobs[h], dR[h].astype(BF16)
            kz = (k.astype(F32) * z_ref[h]).astype(BF16)
            qx = (q.astype(F32) * x_ref[h]).astype(BF16)
            first.append((q, k, _dot_nt(q, k), _dot_nt(dob, v), _dot(kz, dRb), _dot_nt(dob, st_ref[h]),
                          _dot_nt(v, dRb), _dot_tn(qx, dob)))
        masked = [((s * dm_ref[h]).astype(BF16), (dsr * dm_ref[h]).astype(BF16))
                  for h, (_, _, s, dsr, _, _, _, _) in enumerate(first)]
        for h in range(RET_HEADS):
            hs = slice(h * 128, (h + 1) * 128)
            os_ = slice(h * 256, (h + 1) * 256)
            q, k, _, _, dv_state, dq_state, dk_state, dr_new = first[h]
            sD, dS = masked[h]
            dv_ref[:, os_] = (_dot_tn(sD, dobs[h]) + dv_state).astype(BF16)
            dq = _dot(dS, k) + dq_state * x_ref[h]
            dk = _dot_tn(dS, q) + dk_state * z_ref[h]
            dR[h] = dR[h] * dec_ref[h, 0:1, :] + dr_new
            dq_ref[:, hs] = _unrot(dq, tq_ref[0], tq_ref[1], tq_ref[2], 1).astype(BF16)
            dk_ref[:, hs] = _unrot(dk, tk_ref[0], tk_ref[1], tk_ref[2], 1).astype(BF16)

    rc = lambda c: nc - 1 - c
    cst = lambda shape: pl.BlockSpec(shape, lambda c: (0, 0, 0))
    blk = lambda j: pl.BlockSpec((BLK, 512), lambda c: (rc(c), j))
    row = lambda w: pl.BlockSpec((BLK, w), lambda c: (rc(c), 0))
    (dproj,), xres = _carrier_call(
        body, (proj, proj, proj, proj, proj, proj, rn, rstd, dyrin, states, tab, tab, dmask, zeta, xi, dec, dproj),
        out_shape=(SDS((S, PROJ_W), BF16),), grid=(nc,),
        in_specs=[blk(QR_B), blk(KR_B), blk(11), blk(12), blk(13), blk(14), row(1024), row(128), row(1024),
                  pl.BlockSpec((RET_HEADS, None, BLK, 256), lambda c: (0, rc(c), 0, 0)),
                  pl.BlockSpec((None, 3, BLK, 128), lambda c: (1, 0, rc(c), 0)),
                  pl.BlockSpec((None, 3, BLK, 128), lambda c: (2, 0, rc(c), 0)),
                  cst((RET_HEADS, BLK, BLK)), cst((RET_HEADS, BLK, 128)), cst((RET_HEADS, BLK, 128)), cst((RET_HEADS, 8, 256)),
                  ANY],
        out_specs=(pl.BlockSpec((pl.Element(BLK), pl.Element(6 * COLB)), lambda c: (rc(c) * BLK, QR_B * COLB)),),
        scratch_shapes=[pltpu.VMEM((RET_HEADS, BLK, 256), F32)],
        sem=("arbitrary",), name="ret_bwd", exchanges=exchanges, in_out_aliases={16: 0})
    return dproj, xres


def _wgrad_in_half(ht, dproj, sidx, kept, exchanges=()):
    S = dproj.shape[0]
    tk = 2048
    half = (lambda sx: sx[4]) if kept else (lambda sx: 1 - sx[4])

    def body(a_ref, b_ref, o_ref):
        @pl.when(pl.program_id(1) == 0)
        def _():
            o_ref[...] = jnp.zeros_like(o_ref)

        o_ref[...] += _dot(a_ref[...], b_ref[...])

    (g,), xres = _carrier_call(
        body, (ht, dproj), out_shape=(SDS((D_MODEL // 2, PROJ_W), F32),), grid=(N_SHARD, S // tk),
        in_specs=[pl.BlockSpec((D_MODEL // 2, tk), lambda s, k, sx: (half(sx), k)),
                  pl.BlockSpec((tk, W_IN_S), lambda s, k, sx: (k, s))],
        out_specs=(pl.BlockSpec((D_MODEL // 2, W_IN_S), lambda s, k, sx: (0, s)),),
        sem=("parallel", "arbitrary"), name="wgrad_in_kept" if kept else "wgrad_in_sent", exchanges=exchanges,
        prefetch=sidx)
    return g, xres


def _in_proj_bwd(dproj, w_in, x, g1, dx1, exchanges=()):
    S = x.shape[0]
    tm = 1024

    def body(d_ref, w_ref, x_ref, g_ref, dx1_ref, dx_ref, dgn_ref, acc):
        i, s = pl.program_id(0), pl.program_id(1)

        @pl.when(s == 0)
        def _():
            acc[...] = jnp.zeros_like(acc)

        @pl.when((i == 0) & (s == 0))
        def _():
            dgn_ref[...] = jnp.zeros_like(dgn_ref)

        acc[...] += _dot_nt(d_ref[...], w_ref[...])

        @pl.when(s == N_SHARD - 1)
        def _():
            xv = x_ref[...]
            r = lax.rsqrt(jnp.mean(xv * xv, axis=-1, keepdims=True) + NORM_EPS)
            xh = xv * r
            dh = acc[...]
            dgn_ref[...] += jnp.sum(dh * xh, axis=0, keepdims=True)
            dxh = dh * g_ref[...]
            dx_ref[...] = dx1_ref[...] + r * (dxh - xh * jnp.mean(dxh * xh, axis=-1, keepdims=True))

    row = pl.BlockSpec((tm, D_MODEL), lambda i, s: (i, 0))
    vec = pl.BlockSpec((1, D_MODEL), lambda i, s: (0, 0))
    (gx, dg), xres = _carrier_call(
        body, (dproj, w_in, x, g1, dx1),
        out_shape=(SDS((S, D_MODEL), F32), SDS((1, D_MODEL), F32)), grid=(S // tm, N_SHARD),
        in_specs=[pl.BlockSpec((tm, W_IN_S), lambda i, s: (i, s)),
                  pl.BlockSpec((D_MODEL, W_IN_S), lambda i, s: (0, s)), row, vec, row],
        out_specs=(row, vec), scratch_shapes=[pltpu.VMEM((tm, D_MODEL), F32)],
        sem=("arbitrary", "arbitrary"), name="in_proj_bwd", exchanges=exchanges)
    return gx, dg, xres


def _step(x, tgt, g1, g2, g3, comm):
    S = x.shape[0]
    tab_np = _tables(S)
    tab = jnp.asarray(tab_np)
    consts = _ret_consts()

    (h, ht, *casts), xres = _rms_fwd(x, g1, comm.to_cast(), comm.carry("rms_fwd"))
    comm.cast_done(casts)
    comm.took("rms_fwd", xres)
    w_in = comm.weight(0)
    proj, xres = _in_proj(h, w_in, tab, comm.carry("in_proj"))
    comm.took("in_proj", xres)
    qkvs, o_parts, lse_parts = [], [], []
    for gi, d in enumerate(DILATIONS):
        qkv = proj if d == 1 else _qkv_to_sub(proj, d, gi)
        (o_g, lse_g), xres = _attn_fwd(qkv, d, gi, comm.carry(f"attn_fwd_g{gi}"))
        comm.took(f"attn_fwd_g{gi}", xres)
        qkvs.append(qkv)
        o_parts.append(o_g)
        lse_parts.append(lse_g)
    att, lse_views = _attn_merge(o_parts, lse_parts)
    (yrin, rn, rstd, states), xres = _ret_fwd(proj, consts, comm.carry("ret_fwd"))
    comm.took("ret_fwd", xres)
    wa, wr, wo = comm.weight(1), comm.weight(2), comm.weight(3)
    (merged, ya, yr, x1, h2), xres = _mix_out(att, yrin, proj, wa, wr, wo, x, g2, comm.carry("mix_out"))
    comm.took("mix_out", xres)
    wg, wu = comm.weight(4), comm.weight(5)
    (gte, up, act), xres = _ffn_up(h2, wg, wu, comm.carry("ffn_up"))
    comm.took("ffn_up", xres)
    wd = comm.weight(6)
    dx2, dx2b, dg3, loss_p = _ffn_down_loss(act, wd, x1, g3, tgt)

    dgte, dup, dx1, dx1b, dg2 = _ffn_bwd(dx2b, dx2, wd, wg, wu, gte, up, x1, g2)
    tok3 = lambda w: (lambda tk: pl.BlockSpec((None, tk, w), lambda p, k: (p, k, 0)))
    tok2 = lambda w: (lambda tk: pl.BlockSpec((tk, w), lambda p, k: (k, 0)))
    g_d = _wgrad("wgrad_down", act, dx2b, tok3(HID_S), tok2(D_MODEL), (N_SHARD, HID_S, D_MODEL),
                 pl.BlockSpec((None, HID_S, D_MODEL), lambda p, k: (p, 0, 0)), N_SHARD, S)
    g_g = _wgrad("wgrad_gate", dgte, h2, tok3(HID_S), tok2(D_MODEL), (N_SHARD, HID_S, D_MODEL),
                 pl.BlockSpec((None, HID_S, D_MODEL), lambda p, k: (p, 0, 0)), N_SHARD, S)
    g_u = _wgrad("wgrad_up", dup, h2, tok3(HID_S), tok2(D_MODEL), (N_SHARD, HID_S, D_MODEL),
                 pl.BlockSpec((None, HID_S, D_MODEL), lambda p, k: (p, 0, 0)), N_SHARD, S)
    comm.grads({4: g_g, 5: g_u, 6: g_d})
    (dya, dyr, dproj, dyrin, datt_views, rho_views), xres = _mix_bwd(dx1b, wo, proj, ya, yr, wa, wr, att,
                                                                       comm.carry("mix_bwd"))
    comm.took("mix_bwd", xres)
    colblk = lambda w: (lambda tk: pl.BlockSpec((tk, w), lambda p, k: (k, p)))
    g_o = _wgrad("wgrad_out", merged, dx1b, colblk(256), tok2(D_MODEL), (D_MODEL, D_MODEL),
                 pl.BlockSpec((256, D_MODEL), lambda p, k: (p, 0)), 4, S)
    g_a = _wgrad("wgrad_attn", att, dya, tok2(512), colblk(512), (512, D_MODEL),
                 pl.BlockSpec((512, 512), lambda p, k: (0, p)), 2, S)
    g_r = _wgrad("wgrad_ret", yrin, dyr, colblk(256), tok2(D_MODEL), (D_MODEL, D_MODEL),
                 pl.BlockSpec((256, D_MODEL), lambda p, k: (p, 0)), 4, S)
    comm.grads({1: g_a, 2: g_r.reshape(N_SHARD, 256, D_MODEL), 3: g_o.reshape(N_SHARD, 256, D_MODEL)})
    dproj, xres = _ret_bwd(proj, rn, rstd, dyrin, states, tab, consts, dproj, comm.carry("ret_bwd"))
    comm.took("ret_bwd", xres)
    dqs, dks, dvs = [], [], []
    for gi, d in enumerate(DILATIONS):
        rtab = jnp.asarray(tab_np[0].reshape(3, S // d, d * 128))
        (dq, dk, dv), xres = _attn_bwd(qkvs[gi], datt_views[gi], lse_views[gi], rho_views[gi], rtab, d, gi,
                                       comm.carry(f"attn_bwd_g{gi}"))
        comm.took(f"attn_bwd_g{gi}", xres)
        dqs.append(dq)
        dks.append(dk)
        dvs.append(dv)
    dproj = _assemble_dproj((dqs, dks, dvs), dproj)
    g_sent, xres = _wgrad_in_half(ht, dproj, comm.sidx, False, comm.carry("wgrad_in_sent"))
    comm.took("wgrad_in_sent", xres)
    comm.grads({"in_sent": g_sent})
    g_kept, xres = _wgrad_in_half(ht, dproj, comm.sidx, True, comm.carry("wgrad_in_kept"))
    comm.grads({"in_kept": g_kept})
    comm.took("wgrad_in_kept", xres)
    grad_x, dg1, xres = _in_proj_bwd(dproj, w_in, x, g1, dx1, comm.carry("in_proj_bwd"))
    comm.took("in_proj_bwd", xres)
    return loss_p, grad_x, (dg1, dg2, dg3)


W_KINDS = ("col", "col", "lead", "lead", "lead", "lead", "lead")
W_SHARD = ((1024, W_IN_S), (512, 256), (256, 1024), (256, 1024), (HID_S, 1024), (HID_S, 1024), (HID_S, 1024))
W_TRANSPOSED = (4, 5)
N_W = len(W_KINDS)


def _full_shape(wi):
    R, C = W_SHARD[wi]
    return (R, N_SHARD * C) if W_KINDS[wi] == "col" else (N_SHARD, R, C)


def _view(ref, wi, s, half):
    R, C = W_SHARD[wi]
    rows = pl.ds(half * (R // 2), R // 2)
    if W_KINDS[wi] == "col":
        return ref.at[rows, pl.ds(pl.multiple_of(s * C, 128), C)]
    return ref.at[s, rows, :]


def _mesh_pos():
    x, y, c = lax.axis_index("x"), lax.axis_index("y"), lax.axis_index("c")
    chips = [(1 - x, y), (x, 1 - y), (1 - x, 1 - y)]
    return x, y, c, chips


def _cast_bf16(a):
    R, C = a.shape
    tr = R // 2 if R % 32 == 0 else R

    def body(a_ref, o_ref):
        o_ref[...] = a_ref[...].astype(BF16)

    spec = pl.BlockSpec((tr, C), lambda i: (i, 0))
    return pl.pallas_call(body, out_shape=SDS((R, C), BF16), grid=(R // tr,), in_specs=[spec], out_specs=spec,
                          compiler_params=_cparams("parallel"), name=f"cast_{R}x{C}")(a)


def _remote(send, recv, k, src, dst, to):
    return pltpu.make_async_remote_copy(src_ref=src, dst_ref=dst, send_sem=send.at[k], recv_sem=recv.at[k],
                                        device_id=to, device_id_type=MESH)


def _ex_gather_ring(wis, shards):
    n = len(wis)

    def build(sh, full, send, recv, loc):
        x, y, c, _ = _mesh_pos()
        s_me, sib = 2 * x + y, (x, y, 1 - c)
        xn, yn = (1 - x, y), (x, 1 - y)
        flip = lambda a, b: a + b - 2 * a * b
        via = (flip(x, 1 - c), flip(y, c))
        onto = (flip(x, c), flip(y, 1 - c))
        shard_of = lambda chip: 2 * chip[0] + chip[1]
        starts, waits, sent = [], [], []
        for i, wi in enumerate(wis):
            Rh = W_SHARD[wi][0] // 2
            for hf in range(2):
                cp = pltpu.make_async_copy(sh[i].at[pl.ds(hf * Rh, Rh), :], _view(full[i], wi, s_me, hf), loc.at[2 * i + hf])
                starts.append(cp)
                sent.append(cp.wait)
            for j, chip in enumerate((xn, yn)):
                cp = _remote(send, recv, 6 * i + j, sh[i].at[pl.ds(c * Rh, Rh), :], _view(full[i], wi, s_me, c), (*chip, c))
                starts.append(cp)
                sent.append(cp.wait_send)

        def pass_to_sibling(i, wi, k, s):
            mine = _view(full[i], wi, s, c)
            fw = _remote(send, recv, 6 * i + k, mine, mine, sib)
            waits.append(fw.start)
            sent.append(fw.wait_send)

        for i, wi in enumerate(wis):
            for j, chip in enumerate((xn, yn)):
                land = _view(full[i], wi, shard_of(chip), c)
                waits.append(_remote(send, recv, 6 * i + j, land, land, (*chip, c)).wait_recv)
                pass_to_sibling(i, wi, 3 + j, shard_of(chip))
            relay = _view(full[i], wi, shard_of(via), c)
            fw = _remote(send, recv, 6 * i + 2, relay, relay, (*onto, c))
            waits.append(fw.start)
            sent.append(fw.wait_send)
        s_diag = 2 * (1 - x) + (1 - y)
        for i, wi in enumerate(wis):
            land = _view(full[i], wi, s_diag, c)
            waits.append(_remote(send, recv, 6 * i + 2, land, land, (*onto, c)).wait_recv)
            pass_to_sibling(i, wi, 5, s_diag)
        for i, wi in enumerate(wis):
            for k, s in ((3, shard_of(xn)), (4, shard_of(yn)), (5, s_diag)):
                land = _view(full[i], wi, s, 1 - c)
                waits.append(_remote(send, recv, 6 * i + k, land, land, sib).wait_recv)
        return starts, waits + sent

    return _Exchange(shards, [SDS(_full_shape(wi), BF16) for wi in wis], {}, 6 * n, 2 * n, build)


def _ex_gather_ici(wis, shards, then_d2d=False):
    n = len(wis)

    def build(ins, outs, send, recv, loc):
        x, y, c, chips = _mesh_pos()
        s_me, sib = 2 * x + y, (x, y, 1 - c)
        starts, waits, after = [], [], []
        for i, wi in enumerate(wis):
            Rh = W_SHARD[wi][0] // 2
            for hf in range(2):
                cp = pltpu.make_async_copy(ins[i].at[pl.ds(hf * Rh, Rh), :], _view(outs[i], wi, s_me, hf), loc.at[2 * i + hf])
                starts.append(cp)
                waits.append(cp.wait)
            for j, chip in enumerate(chips):
                cp = _remote(send, recv, 3 * i + j, ins[i].at[pl.ds(c * Rh, Rh), :], _view(outs[i], wi, s_me, c), (*chip, c))
                land = _view(outs[i], wi, 2 * chip[0] + chip[1], c)
                starts.append(cp)
                waits += [cp.wait_send, _remote(send, recv, 3 * i + j, land, land, (*chip, c)).wait_recv]
                if then_d2d:
                    theirs = _view(outs[i], wi, 2 * chip[0] + chip[1], 1 - c)
                    fw = _remote(send, recv, 3 * n + 3 * i + j, land, land, sib)
                    waits.append(fw.start)
                    after += [fw.wait_send, _remote(send, recv, 3 * n + 3 * i + j, theirs, theirs, sib).wait_recv]
        return starts, waits + after

    return _Exchange(shards, [SDS(_full_shape(wi), BF16) for wi in wis], {}, (6 if then_d2d else 3) * n, 2 * n, build)


def _ex_gather_d2d(wis, fulls):
    def build(ins, outs, send, recv, loc):
        x, y, c, chips = _mesh_pos()
        sib = (x, y, 1 - c)
        starts, waits = [], []
        for i, wi in enumerate(wis):
            for j, chip in enumerate(chips):
                mine = _view(outs[i], wi, 2 * chip[0] + chip[1], c)
                theirs = _view(outs[i], wi, 2 * chip[0] + chip[1], 1 - c)
                cp = _remote(send, recv, 3 * i + j, mine, mine, sib)
                starts.append(cp)
                waits += [cp.wait_send, _remote(send, recv, 3 * i + j, theirs, theirs, sib).wait_recv]
        return starts, waits

    return _Exchange(fulls, [SDS(f.shape, BF16) for f in fulls], {i: i for i in range(len(wis))}, 3 * len(wis), 0, build)


def _half_shape(wi):
    R, C = W_SHARD[wi]
    return (R // 2, N_SHARD * C) if W_KINDS[wi] == "col" else (N_SHARD, R // 2, C)


def _ex_pair(wis, grads):
    def build(ins, outs, send, recv, loc):
        x, y, c, _ = _mesh_pos()
        starts, waits = [], []
        for i, wi in enumerate(wis):
            Rh = W_SHARD[wi][0] // 2
            rows = pl.ds((1 - c) * Rh, Rh)
            if tuple(ins[i].shape) == _half_shape(wi):
                src = ins[i]
            else:
                src = ins[i].at[rows, :] if W_KINDS[wi] == "col" else ins[i].at[:, rows, :]
            cp = _remote(send, recv, i, src, outs[i], (x, y, 1 - c))
            starts.append(cp)
            waits.append(cp.wait)
        return starts, waits

    return _Exchange(grads, [SDS(_half_shape(wi), F32) for wi in wis], {}, len(wis), 0, build)


def _ex_chip(wis, pbs):
    def build(ins, outs, send, recv, loc):
        x, y, c, chips = _mesh_pos()
        starts, waits = [], []
        for i, wi in enumerate(wis):
            for j, chip in enumerate(chips):
                cp = _remote(send, recv, 3 * i + j, ins[i].at[j], outs[i].at[j], (*chip, c))
                starts.append(cp)
                waits.append(cp.wait)
        return starts, waits

    shapes = [SDS((3, W_SHARD[wi][0] // 2, W_SHARD[wi][1]), BF16) for wi in wis]
    return _Exchange(pbs, shapes, {}, 3 * len(wis), 0, build)


def _ex_share(wis, halves):
    def build(ins, outs, send, recv, loc):
        x, y, c, _ = _mesh_pos()
        sib = (x, y, 1 - c)
        starts, waits = [], []
        for i, wi in enumerate(wis):
            cp = _remote(send, recv, i, outs[i].at[c], outs[i].at[c], sib)
            starts.append(cp)
            waits += [cp.wait_send, _remote(send, recv, i, outs[i].at[1 - c], outs[i].at[1 - c], sib).wait_recv]
        return starts, waits

    return _Exchange(halves, [SDS(h.shape, F32) for h in halves], {i: i for i in range(len(wis))}, len(wis), 0, build)


def _row_tile(rh, C):
    best = 16
    for t in range(16, rh + 1, 16):
        if rh % t == 0 and t * C * 4 <= (3 << 19):
            best = t
    return best


def _pair_sum(wi, g, ra, sidx):
    R, C = W_SHARD[wi]
    Rh = R // 2
    tr = _row_tile(Rh, C)
    nt = Rh // tr
    off = 0 if tuple(g.shape) == _half_shape(wi) else nt
    col = W_KINDS[wi] == "col"

    def body(sidx_ref, *refs):
        gs, rs = refs[:4], refs[4:8]
        own_ref, pb_ref = refs[8:]
        own_ref[...] = gs[0][...] + rs[0][...]
        for j in range(3):
            pb_ref[j] = (gs[1 + j][...] + rs[1 + j][...]).astype(BF16)

    def gspec(slot):
        if col:
            return pl.BlockSpec((tr, C), lambda i, sx: (sx[4] * off + i, sx[slot]))
        return pl.BlockSpec((None, tr, C), lambda i, sx: (sx[slot], sx[4] * off + i, 0))

    def rspec(slot):
        if col:
            return pl.BlockSpec((tr, C), lambda i, sx: (i, sx[slot]))
        return pl.BlockSpec((None, tr, C), lambda i, sx: (sx[slot], i, 0))

    return pl.pallas_call(
        body, out_shape=(SDS((Rh, C), F32), SDS((3, Rh, C), BF16)),
        grid_spec=pltpu.PrefetchScalarGridSpec(
            num_scalar_prefetch=1, grid=(nt,),
            in_specs=[gspec(k) for k in range(4)] + [rspec(k) for k in range(4)],
            out_specs=(pl.BlockSpec((tr, C), lambda i, sx: (i, 0)), pl.BlockSpec((3, tr, C), lambda i, sx: (0, i, 0)))),
        compiler_params=_cparams("arbitrary"), name=f"pair_sum_w{wi}")(sidx, g, g, g, g, ra, ra, ra, ra)


def _chip_sum(wi, own, rb, sidx):
    R, C = W_SHARD[wi]
    Rh = R // 2
    tr = _row_tile(Rh, C)

    def body(sidx_ref, own_ref, rb_ref, o_ref):
        o_ref[...] = ((own_ref[...] + rb_ref[0].astype(F32)) + rb_ref[1].astype(F32)) + rb_ref[2].astype(F32)

    return pl.pallas_call(
        body, out_shape=SDS((2, Rh, C), F32),
        grid_spec=pltpu.PrefetchScalarGridSpec(
            num_scalar_prefetch=1, grid=(Rh // tr,),
            in_specs=[pl.BlockSpec((tr, C), lambda i, sx: (i, 0)), pl.BlockSpec((3, tr, C), lambda i, sx: (0, i, 0))],
            out_specs=pl.BlockSpec((None, tr, C), lambda i, sx: (sx[4], i, 0))),
        compiler_params=_cparams("arbitrary"), name=f"chip_sum_w{wi}")(sidx, own, rb)


def _gain_allgather(blk, ex):
    m_per, n = blk.shape
    n_in, n_out = len(ex.ins), len(ex.out_shapes)

    def body(x_ref, *rest):
        xin, out_ref, xout = rest[:n_in], rest[n_in], rest[n_in + 1:n_in + 1 + n_out]
        send_sems, recv_sems, local_sem = rest[n_in + 1 + n_out:n_in + 4 + n_out]
        ex_starts, ex_waits = ex.build(xin, xout, *rest[n_in + 4 + n_out:])
        for cp in ex_starts:
            cp.start()
        x, y, c, chips = _mesh_pos()
        me, sibling = (x, y, c), (x, y, 1 - c)

        def rows(px, py, pc):
            return out_ref.at[pl.ds((4 * px + 2 * py + pc) * m_per, m_per), :]

        def copy(k, block, to, src=None):
            return pltpu.make_async_remote_copy(
                src_ref=rows(*block) if src is None else src, dst_ref=rows(*block),
                send_sem=send_sems.at[k], recv_sem=recv_sems.at[k], device_id=to, device_id_type=MESH)

        mine = pltpu.make_async_copy(x_ref, rows(*me), local_sem)
        mine.start()
        first = [copy(0, me, sibling, src=x_ref)]
        first += [copy(1 + j, me, (*chip, c), src=x_ref) for j, chip in enumerate(chips)]
        for cp in first:
            cp.start()
        passed = [copy(4 + j, (*chip, c), sibling) for j, chip in enumerate(chips)]
        for j, chip in enumerate(chips):
            copy(1 + j, (*chip, c), me).wait_recv()
            passed[j].start()
        copy(0, sibling, me).wait_recv()
        for j, chip in enumerate(chips):
            copy(4 + j, (*chip, 1 - c), me).wait_recv()
        for cp in first + passed:
            cp.wait_send()
        mine.wait()
        for w in ex_waits:
            w()

    vm = pl.BlockSpec(memory_space=pltpu.VMEM)
    res = pl.pallas_call(
        body, out_shape=(SDS((8 * m_per, n), blk.dtype), *ex.out_shapes),
        in_specs=[vm] + [ANY] * n_in, out_specs=(vm, *[ANY] * n_out),
        input_output_aliases={1 + a: 1 + o for a, o in ex.aliases.items()},
        scratch_shapes=[pltpu.SemaphoreType.DMA((7,)), pltpu.SemaphoreType.DMA((7,)), pltpu.SemaphoreType.DMA] + ex.sems(),
        name="gain_allgather")(blk, *ex.ins)
    return res[0], tuple(res[1:])


def _adam_math(w, g, m, v):
    mn = ADAM_B1 * m + (1.0 - ADAM_B1) * g
    vn = ADAM_B2 * v + (1.0 - ADAM_B2) * (g * g)
    mh = mn / (1.0 - ADAM_B1 ** ADAM_STEP)
    vh = vn / (1.0 - ADAM_B2 ** ADAM_STEP)
    return -ADAM_LR * (mh / (jnp.sqrt(vh) + ADAM_EPS) + ADAM_WD * w), mn, vn


def _adamw(name, ws, gs, ms, vs):
    n, steps = len(ws), 8

    def body(*refs):
        for k in range(n):
            w_ref, g_ref, m_ref, v_ref = refs[4 * k:4 * k + 4]
            go_ref, d_ref, mn_ref, vn_ref = refs[4 * n + 4 * k:4 * n + 4 * k + 4]
            g = g_ref[...]
            go_ref[...] = g
            d_ref[...], mn_ref[...], vn_ref[...] = _adam_math(w_ref[...], g, m_ref[...], v_ref[...])

    specs = [pl.BlockSpec((w.shape[0] // steps, w.shape[1]), lambda i: (i, 0)) for w in ws for _ in range(4)]
    res = pl.pallas_call(
        body, out_shape=tuple(SDS(w.shape, F32) for w in ws for _ in range(4)), grid=(steps,),
        in_specs=specs, out_specs=tuple(specs), compiler_params=_cparams("parallel"),
        name=name)(*[a for k in range(n) for a in (ws[k], gs[k], ms[k], vs[k])])
    return [tuple(res[4 * k:4 * k + 4]) for k in range(n)]


def _gain_update(gathered, w, m, v):
    def body(ga_ref, w_ref, m_ref, v_ref, g_ref, d_ref, mn_ref, vn_ref):
        g = ga_ref[0:8, :]
        for dev in range(1, 8):
            g = g + ga_ref[8 * dev:8 * dev + 8, :]
        g_ref[...] = g
        d_ref[...], mn_ref[...], vn_ref[...] = _adam_math(w_ref[...], g, m_ref[...], v_ref[...])

    return pl.pallas_call(body, out_shape=(SDS((8, 1024), F32),) * 4, name="gain_update")(gathered, w, m, v)


GROUP_FFN, GROUP_MIX, GROUP_IN = (4, 5, 6), (1, 2, 3), (0,)
REST = GROUP_MIX + GROUP_FFN


class _MeshComm:
    SCHEDULE = {
        "rms_fwd": [("ring", GROUP_IN)],
        "in_proj": [("ici", (1, 2, 3, 4))],
        "ret_fwd": [("d2d", (1, 2, 3, 4)), ("ici", (5,))],
        "mix_out": [("d2d", (5,))],
        "ffn_up": [("both", (6,))],
        "mix_bwd": [("pair", GROUP_FFN)],
        "ret_bwd": [("pair", GROUP_MIX), ("chip", (4,))],
        "attn_bwd_g0": [("chip", (5,))],
        "attn_bwd_g1": [("chip", (6,))],
        "attn_bwd_g2": [("chip", GROUP_MIX)],
        "wgrad_in_kept": [("pair", GROUP_IN), ("share", GROUP_FFN + GROUP_MIX)],
        "in_proj_bwd": [("chip", GROUP_IN)],
    }

    def __init__(self, w_in_shard, rest_f32):
        xi, yi, ci = lax.axis_index("x"), lax.axis_index("y"), lax.axis_index("c")
        self.sidx = jnp.stack([2 * xi + yi, 2 * (1 - xi) + yi, 2 * xi + (1 - yi), 2 * (1 - xi) + (1 - yi), ci]).astype(jnp.int32)
        self.shards, self.rest_f32, self.full = {0: w_in_shard}, list(rest_f32), {}
        self.g, self.own, self.pb, self.half, self.red = {}, {}, {}, {}, {}

    def to_cast(self):
        return self.rest_f32

    def cast_done(self, casts):
        self.shards.update(zip(REST, casts))

    def weight(self, wi):
        return self.full[wi].reshape(D_MODEL, D_MODEL) if wi in (2, 3) else self.full[wi]

    def grads(self, by_wi):
        self.g.update(by_wi)

    def _exchange(self, stage, wis):
        pick = lambda table: [table[wi] for wi in wis]
        if stage == "ring":
            return _ex_gather_ring(wis, pick(self.shards))
        if stage == "ici":
            return _ex_gather_ici(wis, pick(self.shards))
        if stage == "both":
            return _ex_gather_ici(wis, pick(self.shards), then_d2d=True)
        if stage == "d2d":
            return _ex_gather_d2d(wis, pick(self.full))
        if stage == "pair":
            return _ex_pair(wis, [self.g["in_sent"] if wi == 0 else self.g[wi] for wi in wis])
        if stage == "chip":
            return _ex_chip(wis, pick(self.pb))
        return _ex_share(wis, pick(self.half))

    def _landed(self, stage, wis, res):
        for wi, r in zip(wis, res):
            if stage in ("ring", "ici", "d2d", "both"):
                self.full[wi] = r
            elif stage == "pair":
                self.own[wi], self.pb[wi] = _pair_sum(wi, self.g["in_kept"] if wi == 0 else self.g[wi], r, self.sidx)
            elif stage == "chip":
                self.half[wi] = _chip_sum(wi, self.own[wi], r, self.sidx)
            else:
                self.red[wi] = r

    def carry(self, point):
        return [self._exchange(stage, wis) for stage, wis in self.SCHEDULE.get(point, ())]

    def took(self, point, xres):
        for (stage, wis), res in zip(self.SCHEDULE.get(point, ()), xres):
            self._landed(stage, wis, res)

    def last_share(self):
        return self._exchange("share", GROUP_IN)

    def reduced(self, last_shared):
        self._landed("share", GROUP_IN, last_shared)
        return [self.red[wi] for wi in range(N_W)]


def kernel(x, norm_mix_g, w_in, w_out_attn, w_out_ret, w_out, norm_ffn_g, w_ffn_gate, w_ffn_up, w_ffn_down, norm_final_g, loss_target, m_norm_mix_g, m_w_in, m_w_out_attn, m_w_out_ret, m_w_out, m_norm_ffn_g, m_w_ffn_gate, m_w_ffn_up, m_w_ffn_down, m_norm_final_g, v_norm_mix_g, v_w_in, v_w_out_attn, v_w_out_ret, v_w_out, v_norm_ffn_g, v_w_ffn_gate, v_w_ffn_up, v_w_ffn_down, v_norm_final_g):
    ws = (w_in, w_out_attn, w_out_ret, w_out, w_ffn_gate, w_ffn_up, w_ffn_down)
    ms = (m_w_in, m_w_out_attn, m_w_out_ret, m_w_out, m_w_ffn_gate, m_w_ffn_up, m_w_ffn_down)
    vs = (v_w_in, v_w_out_attn, v_w_out_ret, v_w_out, v_w_ffn_gate, v_w_ffn_up, v_w_ffn_down)

    def shard2d(a, wi):
        return jnp.swapaxes(a[0], 0, 1) if wi in W_TRANSPOSED else a.reshape(W_SHARD[wi])

    def as_given(a2d, wi):
        return jnp.swapaxes(a2d, 0, 1)[None] if wi in W_TRANSPOSED else a2d.reshape(ws[wi].shape)

    comm = _MeshComm(_cast_bf16(shard2d(ws[0], 0)), [shard2d(ws[wi], wi) for wi in REST])
    g3 = norm_final_g.reshape(1, D_MODEL)
    loss_p, grad_x, gain_g = _step(x[0], loss_target[0], norm_mix_g, norm_ffn_g, g3, comm)

    pad8 = lambda rows: jnp.concatenate([r.reshape(1, D_MODEL) for r in rows]
                                        + [jnp.zeros((8 - len(rows), D_MODEL), F32)], axis=0)
    gathered, shared = _gain_allgather(pad8((*gain_g, jnp.tile(loss_p[0:1], (1, D_MODEL // 128)))), comm.last_share())
    gred = comm.reduced(shared)

    def adam(name, wis):
        two_d = lambda arrs: [shard2d(arrs[wi], wi) for wi in wis]
        return _adamw(name, two_d(ws), [gred[wi].reshape(W_SHARD[wi]) for wi in wis], two_d(ms), two_d(vs))

    updates = dict(zip(REST + GROUP_IN, adam("adamw_rest", REST) + adam("adamw_w_in", GROUP_IN)))
    outs_g, outs_d, outs_m, outs_v = ([as_given(updates[wi][k], wi) for wi in range(N_W)] for k in range(4))

    gg, gd, gm, gv = _gain_update(gathered, pad8((norm_mix_g, norm_ffn_g, norm_final_g)),
                                  pad8((m_norm_mix_g, m_norm_ffn_g, m_norm_final_g)),
                                  pad8((v_norm_mix_g, v_norm_ffn_g, v_norm_final_g)))
    loss = gg[3, 0]

    def assemble(gain_rows, wlist):
        return (gain_rows[0:1], wlist[0], wlist[1], wlist[2], wlist[3], gain_rows[1:2],
                wlist[4], wlist[5], wlist[6], gain_rows[2])

    return (loss, grad_x[None], *assemble(gg, outs_g), *assemble(gd, outs_d), *assemble(gm, outs_m), *assemble(gv, outs_v))
```

```python
import functools

import numpy as np
import jax
import jax.numpy as jnp
from jax import lax
from jax.experimental import pallas as pl
from jax.experimental.pallas import tpu as pltpu

F32, BF16 = jnp.float32, jnp.bfloat16
SDS = jax.ShapeDtypeStruct
MESH = pl.DeviceIdType.MESH

D_MODEL = 1024
PROJ_W = 9728
COLB = 512
N_COLB = PROJ_W // COLB
QA_B, KA_B, VA_B = 0, 3, 6
QR_B, KR_B = 9, 10
FFN_HID = 2816
N_SHARD = 4
HID_S = FFN_HID // N_SHARD
W_IN_S = PROJ_W // N_SHARD
DILATIONS = (1, 4, 16)
BLK = 128
RET_HEADS = 4
ROPE_THETA = 10000.0
NORM_EPS = 1e-6
ADAM_LR, ADAM_B1, ADAM_B2, ADAM_EPS, ADAM_WD, ADAM_STEP = 0.001, 0.9, 0.999, 1e-08, 0.01, 10
VMEM_LIMIT = 56 << 20


def _cparams(*sem):
    return pltpu.CompilerParams(dimension_semantics=sem or None, vmem_limit_bytes=VMEM_LIMIT)


def _dot(a, b):
    return jnp.dot(a, b, preferred_element_type=F32)


def _dot_nt(a, b):
    return lax.dot_general(a, b, (((1,), (1,)), ((), ())), preferred_element_type=F32)


def _dot_tn(a, b):
    return lax.dot_general(a, b, (((0,), (0,)), ((), ())), preferred_element_type=F32)


def _row_pieces(tm, sub=512):
    return [slice(i, i + sub) for i in range(0, tm, sub)]


def _sigmoid(z):
    return 0.5 * jnp.tanh(0.5 * z) + 0.5


ANY = pl.BlockSpec(memory_space=pl.ANY)


class _Exchange:
    def __init__(self, ins, out_shapes, aliases, n_sem, n_loc, build):
        self.ins, self.out_shapes, self.aliases = list(ins), list(out_shapes), dict(aliases)
        self.n_sem, self.n_loc, self.build = n_sem, n_loc, build

    def sems(self):
        return [pltpu.SemaphoreType.DMA((self.n_sem,)), pltpu.SemaphoreType.DMA((self.n_sem,)),
                pltpu.SemaphoreType.DMA((max(self.n_loc, 1),))]


def _carrier_call(body, args, *, out_shape, grid, in_specs, out_specs, scratch_shapes=(), sem, name, exchanges=(),
                  prefetch=None, in_out_aliases=None):
    out_shape, out_specs = tuple(out_shape), tuple(out_specs)
    n_in, n_out, n_scr = len(args), len(out_shape), len(scratch_shapes)
    n_pre = 0 if prefetch is None else 1
    x_args, x_outs, x_scr, spans = [], [], [], []
    aliases = {n_pre + a: o for a, o in (in_out_aliases or {}).items()}
    for ex in exchanges:
        i0, o0 = len(x_args), len(x_outs)
        for a, o in ex.aliases.items():
            aliases[n_pre + n_in + i0 + a] = n_out + o0 + o
        x_args += ex.ins
        x_outs += ex.out_shapes
        x_scr += ex.sems()
        spans.append((i0, len(ex.ins), o0, len(ex.out_shapes)))
    nx_in, nx_out = len(x_args), len(x_outs)

    def wrapped(*refs):
        refs = refs[n_pre:]
        ins, xin = refs[:n_in], refs[n_in:n_in + nx_in]
        o_base = n_in + nx_in
        outs, xout = refs[o_base:o_base + n_out], refs[o_base + n_out:o_base + n_out + nx_out]
        s_base = o_base + n_out + nx_out
        scr, xs = refs[s_base:s_base + n_scr], refs[s_base + n_scr:]

        def built(e):
            i0, ni, o0, no = spans[e]
            return exchanges[e].build(xin[i0:i0 + ni], xout[o0:o0 + no], *xs[3 * e:3 * e + 3])

        if exchanges:
            first = functools.reduce(jnp.logical_and, [pl.program_id(k) == 0 for k in range(len(grid))])
            last = functools.reduce(jnp.logical_and, [pl.program_id(k) == grid[k] - 1 for k in range(len(grid))])

            @pl.when(first)
            def _():
                for e in range(len(exchanges)):
                    for cp in built(e)[0]:
                        cp.start()

        body(*ins, *outs, *scr)

        if exchanges:
            @pl.when(last)
            def _():
                for e in range(len(exchanges)):
                    for w in built(e)[1]:
                        w()

    all_in, all_out = list(in_specs) + [ANY] * nx_in, out_specs + tuple([ANY] * nx_out)
    all_scr = list(scratch_shapes) + x_scr
    cparams = _cparams(*(sem if not exchanges else ("arbitrary",) * len(grid)))
    if prefetch is None:
        res = pl.pallas_call(wrapped, out_shape=out_shape + tuple(x_outs), grid=grid, in_specs=all_in, out_specs=all_out,
                             scratch_shapes=all_scr, input_output_aliases=aliases, compiler_params=cparams,
                             name=name)(*args, *x_args)
    else:
        gs = pltpu.PrefetchScalarGridSpec(num_scalar_prefetch=1, grid=grid, in_specs=all_in, out_specs=all_out,
                                          scratch_shapes=all_scr)
        res = pl.pallas_call(wrapped, out_shape=out_shape + tuple(x_outs), grid_spec=gs, input_output_aliases=aliases,
                             compiler_params=cparams, name=name)(prefetch, *args, *x_args)
    xres = [tuple(res[n_out + o0:n_out + o0 + no]) for (_, _, o0, no) in spans]
    return tuple(res[:n_out]), xres


def _tables(S):
    f32 = np.float32
    pos = np.arange(S, dtype=f32)
    lane = np.arange(128)
    inv = (f32(ROPE_THETA) ** (-np.arange(0, 64, 2, dtype=f32) / f32(64))).astype(f32)
    ang = (pos[:, None] * inv[None, :]).astype(np.float64)
    idx = (lane % 64) % 32
    c, s = np.cos(ang)[:, idx], np.sin(ang)[:, idx]
    first = ((lane % 64) < 32)[None, :]
    rope = np.stack([c, np.where(first, 0.0, s), np.where(first, -s, 0.0)])
    base = (f32(1.0) / (f32(ROPE_THETA) ** np.linspace(0.0, 1.0, 64, dtype=f32))).astype(f32)
    ang2 = (pos[:, None] * base[None, :]).astype(np.float64)
    c2, s2 = np.cos(ang2)[:, lane // 2], np.sin(ang2)[:, lane // 2]
    even = (lane % 2 == 0)[None, :]
    th = np.stack([c2, np.where(even, 0.0, s2), np.where(even, -s2, 0.0)])
    return np.stack([rope, th, th * (128 ** -0.5)]).astype(f32)


def _rot(a, c, sa, sb, shift):
    return a * c + pltpu.roll(a, shift, 1) * sa + pltpu.roll(a, 128 - shift, 1) * sb


def _unrot(g, c, sa, sb, shift):
    return g * c + pltpu.roll(g * sa, 128 - shift, 1) + pltpu.roll(g * sb, shift, 1)


def _ret_consts():
    h = np.arange(RET_HEADS, dtype=np.float64)
    log_g = np.log1p(-(2.0 ** (-5.0 - h)))
    idx = np.arange(BLK, dtype=np.float64)
    diff = idx[:, None] - idx[None, :]
    dmask = np.where(diff[None] >= 0, np.exp(np.maximum(diff, 0.0)[None] * log_g[:, None, None]), 0.0)
    zeta = np.exp((BLK - 1 - idx)[None, :] * log_g[:, None])
    xi = np.exp((idx + 1.0)[None, :] * log_g[:, None])
    dec = np.exp(BLK * log_g)
    rep = lambda v: np.broadcast_to(v[:, :, None], (RET_HEADS, BLK, 128))
    return (jnp.asarray(dmask, F32), jnp.asarray(rep(zeta), F32), jnp.asarray(rep(xi), F32),
            jnp.asarray(np.broadcast_to(dec[:, None, None], (RET_HEADS, 8, 256)), F32))


def _rms_fwd(x, g, to_cast=(), exchanges=()):
    S = x.shape[0]
    steps = 4
    tm = S // steps
    n_c = len(to_cast)

    def body(x_ref, g_ref, *refs):
        c_in, (h_ref, ht_ref), c_out = refs[:n_c], refs[n_c:n_c + 2], refs[n_c + 2:]
        for rows in _row_pieces(tm, 512):
            xv = x_ref[rows, :]
            r = lax.rsqrt(jnp.mean(xv * xv, axis=-1, keepdims=True) + NORM_EPS)
            h = xv * r * g_ref[...]
            h_ref[rows, :] = h.astype(BF16)
            ht_ref[:, rows] = h.T.astype(BF16)
        for a_ref, o_ref in zip(c_in, c_out):
            o_ref[...] = a_ref[...].astype(BF16)

    slab = lambda a: pl.BlockSpec((a.shape[0] // steps, a.shape[1]), lambda i: (i, 0))
    return _carrier_call(
        body, (x, g, *to_cast),
        out_shape=(SDS((S, D_MODEL), BF16), SDS((D_MODEL, S), BF16), *[SDS(a.shape, BF16) for a in to_cast]),
        grid=(steps,),
        in_specs=[pl.BlockSpec((tm, D_MODEL), lambda i: (i, 0)), pl.BlockSpec((1, D_MODEL), lambda i: (0, 0))]
        + [slab(a) for a in to_cast],
        out_specs=(pl.BlockSpec((tm, D_MODEL), lambda i: (i, 0)), pl.BlockSpec((D_MODEL, tm), lambda i: (0, i)),
                   *[slab(a) for a in to_cast]),
        sem=("parallel",), name="rms_fwd", exchanges=exchanges)


def _in_proj(h, w_in, tab, exchanges=()):
    S = h.shape[0]
    tm = min(S, 4096)

    def body(h_ref, w_ref, t_ref, o_ref):
        j = pl.program_id(1)
        is_rope = j < 6
        is_theta = (j == QR_B) | (j == KR_B)
        sub = 512

        def rotated(shift):
            for i in range(tm // sub):
                rows = slice(i * sub, (i + 1) * sub)
                acc = _dot(h_ref[rows, :], w_ref[...])
                c, sa, sb = t_ref[0, 0, rows, :], t_ref[0, 1, rows, :], t_ref[0, 2, rows, :]
                for k in range(COLB // 128):
                    sl = slice(k * 128, (k + 1) * 128)
                    o_ref[rows, sl] = _rot(acc[:, sl], c, sa, sb, shift).astype(BF16)

        @pl.when(is_rope)
        def _():
            rotated(32)

        @pl.when(is_theta)
        def _():
            rotated(1)

        @pl.when(jnp.logical_not(is_rope | is_theta))
        def _():
            o_ref[...] = _dot(h_ref[...], w_ref[...]).astype(BF16)

    def tab_map(i, j):
        return (jnp.where(j == QR_B, 1, jnp.where(j == KR_B, 2, 0)), 0, i, 0)

    (proj,), xres = _carrier_call(
        body, (h, w_in, tab), out_shape=(SDS((S, PROJ_W), BF16),), grid=(S // tm, N_COLB),
        in_specs=[pl.BlockSpec((tm, D_MODEL), lambda i, j: (i, 0)),
                  pl.BlockSpec((D_MODEL, COLB), lambda i, j: (0, j)),
                  pl.BlockSpec((1, 3, tm, 128), tab_map)],
        out_specs=(pl.BlockSpec((tm, COLB), lambda i, j: (i, j)),),
        sem=("parallel", "arbitrary"), name="in_proj", exchanges=exchanges)
    return proj, xres


def _band_mask(n):
    qi = lax.broadcasted_iota(jnp.int32, (BLK, 2 * BLK), 0)
    kj = lax.broadcasted_iota(jnp.int32, (BLK, 2 * BLK), 1)
    dist = BLK + qi - kj
    return (dist >= 0) & (dist <= BLK) & ((kj >= BLK) | (n > 0))


def _qkv_col(d, gi):
    if d == 1:
        return lambda t, r: 3 * t + gi
    return lambda t, r: 3 * r + t


def _attn_fwd(qkv, d, gi, exchanges=()):
    L = qkv.shape[0]
    nb = L // BLK
    qb = next(n for n in (4, 2, 1) if nb % n == 0)

    def body(q_ref, kc_ref, kp_ref, vc_ref, vp_ref, o_ref, lse_ref):
        i = pl.program_id(1)
        lane = lax.broadcasted_iota(jnp.int32, (BLK, 128), 1)
        lo = lane < 64
        chunks = [slice(c * 128, (c + 1) * 128) for c in range(4)]
        scores, vals, masks = [], [], []
        for b in range(qb):
            rows = slice(b * BLK, (b + 1) * BLK)
            before = slice((b - 1) * BLK, b * BLK)
            mask = _band_mask(qb * i + b)
            masks.append(jnp.concatenate([mask, mask], axis=0))
            for sl in chunks:
                q = q_ref[rows, sl]
                k = jnp.concatenate([kp_ref[:, sl] if b == 0 else kc_ref[before, sl], kc_ref[rows, sl]], axis=0)
                vals.append(jnp.concatenate([vp_ref[:, sl] if b == 0 else vc_ref[before, sl], vc_ref[rows, sl]], axis=0))
                q2 = jnp.concatenate([jnp.where(lo, q, jnp.zeros_like(q)), jnp.where(lo, jnp.zeros_like(q), q)], axis=0)
                scores.append(_dot_nt(q2, k))
        probs, lses = [], []
        for j, s in enumerate(scores):
            b, c = divmod(j, 4)
            s = jnp.where(masks[b], s * 0.125, jnp.float32(-1e30))
            m = jnp.max(s, axis=-1, keepdims=True)
            p = jnp.exp(s - m)
            l = jnp.sum(p, axis=-1, keepdims=True)
            probs.append((p * (1.0 / l)).astype(BF16))
            lses.append(m + jnp.log(l))
        for j, (p, v) in enumerate(zip(probs, vals)):
            b, c = divmod(j, 4)
            o2 = _dot(p, v)
            o_ref[b * BLK:(b + 1) * BLK, chunks[c]] = jnp.where(lo, o2[:BLK], o2[BLK:])
        for b in range(qb):
            lse_all = jnp.zeros((BLK, 128), F32)
            for c in range(4):
                lse = lses[4 * b + c]
                lse_all = jnp.where(lane // 16 == 2 * c, lse[:BLK], jnp.where(lane // 16 == 2 * c + 1, lse[BLK:], lse_all))
            lse_ref[b * BLK:(b + 1) * BLK, :] = lse_all

    prev = lambda i: jnp.maximum(qb * i - 1, 0)
    col = _qkv_col(d, gi)
    return _carrier_call(
        body, (qkv,) * 5, out_shape=(SDS((L, d * 512), F32), SDS((L, d * 128), F32)), grid=(d, nb // qb),
        in_specs=[pl.BlockSpec((qb * BLK, 512), lambda r, i: (i, col(0, r))),
                  pl.BlockSpec((qb * BLK, 512), lambda r, i: (i, col(1, r))),
                  pl.BlockSpec((BLK, 512), lambda r, i: (prev(i), col(1, r))),
                  pl.BlockSpec((qb * BLK, 512), lambda r, i: (i, col(2, r))),
                  pl.BlockSpec((BLK, 512), lambda r, i: (prev(i), col(2, r)))],
        out_specs=(pl.BlockSpec((qb * BLK, 512), lambda r, i: (i, r)),
                   pl.BlockSpec((qb * BLK, 128), lambda r, i: (i, r))),
        sem=("parallel", "arbitrary"), name=f"attn_fwd_g{gi}", exchanges=exchanges)


def _qkv_to_sub(proj, d, gi):
    S = proj.shape[0]
    tm = 1024
    n = tm // d

    def body(q_ref, k_ref, v_ref, o_ref, scr):
        for t, ref in enumerate((q_ref, k_ref, v_ref)):
            for c in range(4):
                scr[c] = ref[:, c * 128:(c + 1) * 128].astype(F32)
            for r in range(d):
                for c in range(4):
                    col = (3 * r + t) * 512 + c * 128
                    o_ref[:, col:col + 128] = scr[c, pl.ds(r, n, stride=d), :].astype(BF16)

    return pl.pallas_call(
        body, out_shape=SDS((S // d, d * 1536), BF16), grid=(S // tm,),
        in_specs=[pl.BlockSpec((tm, 512), lambda i, b=b: (i, b + gi)) for b in (QA_B, KA_B, VA_B)],
        out_specs=pl.BlockSpec((n, d * 1536), lambda i: (i, 0)),
        scratch_shapes=[pltpu.VMEM((4, tm, 128), F32)],
        compiler_params=_cparams("parallel"), name=f"qkv_to_sub_g{gi}")(proj, proj, proj)


def _attn_merge(os_, lses):
    S = os_[0].shape[0]
    tm = 512

    def body(o0, o1, o2, l0, l1, l2, att_ref, lt_ref, lt1_ref, lt2_ref, so1, so2, sl1, sl2):
        lo = lax.broadcasted_iota(jnp.int32, (tm, 128), 1) < 64

        def natural(ref, d, scr, width):
            nch = width // 128
            if d == 1:
                return [ref[:, c * 128:(c + 1) * 128] for c in range(nch)]
            for r in range(d):
                for c in range(nch):
                    scr[c, pl.ds(r, tm // d, stride=d), :] = ref[:, r * width + c * 128:r * width + (c + 1) * 128]
            return [scr[c] for c in range(nch)]

        ls = [natural(l, d, s, 128)[0] for l, d, s in zip((l0, l1, l2), DILATIONS, (None, sl1, sl2))]
        m = jnp.maximum(jnp.maximum(ls[0], ls[1]), ls[2])
        es = [jnp.exp(v - m) for v in ls]
        z = es[0] + es[1] + es[2]
        lt = m + jnp.log(z)
        lt_ref[...] = lt
        sl1[0] = lt
        for ref, d in ((lt1_ref, DILATIONS[1]), (lt2_ref, DILATIONS[2])):
            for r in range(d):
                ref[:, r * 128:(r + 1) * 128] = sl1[0, pl.ds(r, tm // d, stride=d), :]
        ws = [e / z for e in es]
        o_nat = [natural(o, d, s, 512) for o, d, s in zip((o0, o1, o2), DILATIONS, (None, so1, so2))]
        for c in range(4):
            acc = jnp.zeros((tm, 128), F32)
            for g in range(3):
                w_lo = jnp.broadcast_to(ws[g][:, 32 * c:32 * c + 1], (tm, 128))
                w_hi = jnp.broadcast_to(ws[g][:, 32 * c + 16:32 * c + 17], (tm, 128))
                acc = acc + jnp.where(lo, w_lo, w_hi) * o_nat[g][c]
            att_ref[:, c * 128:(c + 1) * 128] = acc.astype(BF16)

    sub = lambda w: [pl.BlockSpec((tm // d, d * w), lambda i: (i, 0)) for d in DILATIONS]
    att, *lts = pl.pallas_call(
        body, out_shape=(SDS((S, 512), BF16), *[SDS((S // d, d * 128), F32) for d in DILATIONS]), grid=(S // tm,),
        in_specs=sub(512) + sub(128),
        out_specs=(pl.BlockSpec((tm, 512), lambda i: (i, 0)), *sub(128)),
        scratch_shapes=[pltpu.VMEM((4, tm, 128), F32), pltpu.VMEM((4, tm, 128), F32),
                        pltpu.VMEM((1, tm, 128), F32), pltpu.VMEM((1, tm, 128), F32)],
        compiler_params=_cparams("parallel"), name="attn_merge")(*os_, *lses)
    return att, lts


def _assemble_dproj(att_grads, dproj):
    S = dproj.shape[0]
    tm = 512

    def body(*refs):
        a = [refs[3 * t:3 * t + 3] for t in range(3)]
        dp_prev, o_ref, scr = refs[9:]
        for t in range(3):
            for g, d in enumerate(DILATIONS):
                base = (3 * t + g) * COLB
                if d == 1:
                    o_ref[:, base:base + COLB] = a[t][g][...]
                    continue
                for c in range(4):
                    for r in range(d):
                        scr[c, pl.ds(r, tm // d, stride=d), :] = a[t][g][:, r * 512 + c * 128:r * 512 + (c + 1) * 128].astype(F32)
                    o_ref[:, base + c * 128:base + (c + 1) * 128] = scr[c].astype(BF16)

    sub = [pl.BlockSpec((tm // d, d * 512), lambda i: (i, 0)) for d in DILATIONS]
    flat = [att_grads[t][g] for t in range(3) for g in range(3)]
    return pl.pallas_call(
        body, out_shape=SDS((S, PROJ_W), BF16), grid=(S // tm,),
        in_specs=sub * 3 + [ANY], out_specs=pl.BlockSpec((tm, 9 * COLB), lambda i: (i, 0)),
        scratch_shapes=[pltpu.VMEM((4, tm, 128), F32)], input_output_aliases={9: 0},
        compiler_params=_cparams("parallel"), name="assemble_dproj")(*flat, dproj)


def _ret_fwd(proj, consts, exchanges=()):
    S = proj.shape[0]
    nc = S // BLK
    dmask, zeta, xi, dec = consts

    def body(q_ref, k_ref, v0_ref, v1_ref, g0_ref, g1_ref, dm_ref, z_ref, x_ref, dec_ref,
             y_ref, rn_ref, rs_ref, st_ref, R):
        @pl.when(pl.program_id(0) == 0)
        def _():
            R[...] = jnp.zeros_like(R)

        lane16 = lax.broadcasted_iota(jnp.int32, (BLK, 128), 1) // 16
        rs_all = jnp.zeros((BLK, 128), F32)
        first = []
        for h in range(RET_HEADS):
            hs = slice(h * 128, (h + 1) * 128)
            q, k = q_ref[:, hs], k_ref[:, hs]
            v = (v0_ref if h < 2 else v1_ref)[:, (h % 2) * 256:(h % 2 + 1) * 256]
            Rb = R[h].astype(BF16)
            st_ref[h] = Rb
            kz = (k.astype(F32) * z_ref[h]).astype(BF16)
            first.append((v, _dot_nt(q, k), _dot((q.astype(F32) * x_ref[h]).astype(BF16), Rb), _dot_tn(kz, v)))
        masked = [(s * dm_ref[h]).astype(BF16) for h, (_, s, _, _) in enumerate(first)]
        for h in range(RET_HEADS):
            vs = slice((h % 2) * 256, (h % 2 + 1) * 256)
            os_ = slice(h * 256, (h + 1) * 256)
            v, _, cross, kv = first[h]
            o = _dot(masked[h], v) + cross
            R[h] = R[h] * dec_ref[h, 0:1, :] + kv
            mu = jnp.mean(o, axis=-1, keepdims=True)
            oc = o - mu
            rstd = lax.rsqrt(jnp.mean(oc * oc, axis=-1, keepdims=True) + NORM_EPS)
            rn = oc * rstd
            gr = (g0_ref if h < 2 else g1_ref)[:, vs].astype(F32)
            y_ref[:, os_] = (rn * gr * _sigmoid(gr)).astype(BF16)
            rn_ref[:, os_] = rn.astype(BF16)
            rs_all = jnp.where(lane16 == h, rstd, rs_all)
        rs_ref[...] = rs_all

    cst = lambda shape: pl.BlockSpec(shape, lambda c: (0, 0, 0))
    blk = lambda j: pl.BlockSpec((BLK, 512), lambda c: (c, j))
    return _carrier_call(
        body, (proj, proj, proj, proj, proj, proj, dmask, zeta, xi, dec),
        out_shape=(SDS((S, 1024), BF16), SDS((S, 1024), BF16), SDS((S, 128), F32), SDS((RET_HEADS, nc, BLK, 256), BF16)),
        grid=(nc,),
        in_specs=[blk(QR_B), blk(KR_B), blk(11), blk(12), blk(13), blk(14),
                  cst((RET_HEADS, BLK, BLK)), cst((RET_HEADS, BLK, 128)), cst((RET_HEADS, BLK, 128)), cst((RET_HEADS, 8, 256))],
        out_specs=(pl.BlockSpec((BLK, 1024), lambda c: (c, 0)), pl.BlockSpec((BLK, 1024), lambda c: (c, 0)),
                   pl.BlockSpec((BLK, 128), lambda c: (c, 0)),
                   pl.BlockSpec((RET_HEADS, None, BLK, 256), lambda c: (0, c, 0, 0))),
        scratch_shapes=[pltpu.VMEM((RET_HEADS, BLK, 256), F32)],
        sem=("arbitrary",), name="ret_fwd", exchanges=exchanges)


def _mix_out(att, yrin, proj, wa, wr, wo, x, g2, exchanges=()):
    S = x.shape[0]
    tm = 512
    gate0 = 15 * COLB

    def body(a_ref, y_ref, ga_ref, gr_ref, wa_ref, wr_ref, wo_ref, x_ref, g_ref, m_ref, ya_ref, yr_ref, x1_ref, h2_ref):
        pieces = _row_pieces(tm, 256)
        branches = [(_dot(a_ref[rows, :], wa_ref[...]), _dot(y_ref[rows, :], wr_ref[...])) for rows in pieces]
        merged = []
        for rows, (ya, yr) in zip(pieces, branches):
            m = (_sigmoid(ga_ref[rows, :].astype(F32)) * ya + _sigmoid(gr_ref[rows, :].astype(F32)) * yr).astype(BF16)
            m_ref[rows, :] = m
            ya_ref[rows, :] = ya.astype(BF16)
            yr_ref[rows, :] = yr.astype(BF16)
            merged.append(m)
        for rows, m in zip(pieces, merged):
            x1 = x_ref[rows, :] + _dot(m, wo_ref[...])
            x1_ref[rows, :] = x1
            r = lax.rsqrt(jnp.mean(x1 * x1, axis=-1, keepdims=True) + NORM_EPS)
            h2_ref[rows, :] = (x1 * r * g_ref[...]).astype(BF16)

    row = lambda w: pl.BlockSpec((tm, w), lambda i: (i, 0))
    cols = lambda c0: pl.BlockSpec((pl.Element(tm), pl.Element(D_MODEL)), lambda i: (i * tm, c0))
    resident = lambda r, c: pl.BlockSpec((r, c), lambda i: (0, 0), pipeline_mode=pl.Buffered(1))
    return _carrier_call(
        body, (att, yrin, proj, proj, wa, wr, wo, x, g2),
        out_shape=(SDS((S, D_MODEL), BF16),) * 3 + (SDS((S, D_MODEL), F32), SDS((S, D_MODEL), BF16)), grid=(S // tm,),
        in_specs=[row(512), row(D_MODEL), cols(gate0), cols(gate0 + D_MODEL), resident(512, D_MODEL),
                  resident(D_MODEL, D_MODEL), resident(D_MODEL, D_MODEL), row(D_MODEL),
                  pl.BlockSpec((1, D_MODEL), lambda i: (0, 0))],
        out_specs=(row(D_MODEL),) * 5, sem=("parallel",), name="mix_out", exchanges=exchanges)


def _ffn_up(h2, wg, wu, exchanges=()):
    S = h2.shape[0]
    tm = min(S, 2048)

    def body(h_ref, wg_ref, wu_ref, g_ref, u_ref, a_ref):
        for rows in _row_pieces(tm):
            hv = h_ref[rows, :]
            g = _dot_nt(hv, wg_ref[...])
            u = _dot_nt(hv, wu_ref[...])
            g_ref[rows, :] = g.astype(BF16)
            u_ref[rows, :] = u.astype(BF16)
            a_ref[rows, :] = (g * _sigmoid(g) * u).astype(BF16)

    wspec = pl.BlockSpec((None, HID_S, D_MODEL), lambda i, s: (s, 0, 0))
    ospec = pl.BlockSpec((None, tm, HID_S), lambda i, s: (s, i, 0))
    return _carrier_call(
        body, (h2, wg, wu), out_shape=(SDS((N_SHARD, S, HID_S), BF16),) * 3, grid=(S // tm, N_SHARD),
        in_specs=[pl.BlockSpec((tm, D_MODEL), lambda i, s: (i, 0)), wspec, wspec],
        out_specs=(ospec, ospec, ospec),
        sem=("parallel", "arbitrary"), name="ffn_up", exchanges=exchanges)


def _ffn_down_loss(act, wd, x1, g3, tgt):
    S = x1.shape[0]
    tm = 512

    def body(a_ref, w_ref, x_ref, g_ref, t_ref, dx_ref, dxb_ref, dg_ref, ls_ref):
        @pl.when(pl.program_id(0) == 0)
        def _():
            dg_ref[...] = jnp.zeros_like(dg_ref)
            ls_ref[...] = jnp.zeros_like(ls_ref)

        g = g_ref[...]
        for rows in _row_pieces(tm, 256):
            y = _dot(a_ref[0, rows, :], w_ref[0])
            for s in range(1, N_SHARD):
                y = y + _dot(a_ref[s, rows, :], w_ref[s])
            x2 = x_ref[rows, :] + y
            r = lax.rsqrt(jnp.mean(x2 * x2, axis=-1, keepdims=True) + NORM_EPS)
            xh = x2 * r
            err = xh * g - t_ref[rows, :]
            ls_ref[...] += jnp.sum(jnp.sum(err * err, axis=-1, keepdims=True), axis=0, keepdims=True) * (0.5 / D_MODEL)
            dy = err * (1.0 / D_MODEL)
            dg_ref[...] += jnp.sum(dy * xh, axis=0, keepdims=True)
            dxh = dy * g
            dx = r * (dxh - xh * jnp.mean(dxh * xh, axis=-1, keepdims=True))
            dx_ref[rows, :] = dx
            dxb_ref[rows, :] = dx.astype(BF16)

    row = pl.BlockSpec((tm, D_MODEL), lambda i: (i, 0))
    vec = pl.BlockSpec((1, D_MODEL), lambda i: (0, 0))
    return pl.pallas_call(
        body, out_shape=(SDS((S, D_MODEL), F32), SDS((S, D_MODEL), BF16), SDS((1, D_MODEL), F32), SDS((8, 128), F32)),
        grid=(S // tm,),
        in_specs=[pl.BlockSpec((N_SHARD, tm, HID_S), lambda i: (0, i, 0)),
                  pl.BlockSpec((N_SHARD, HID_S, D_MODEL), lambda i: (0, 0, 0), pipeline_mode=pl.Buffered(1)),
                  row, vec, row],
        out_specs=(row, row, vec, pl.BlockSpec((8, 128), lambda i: (0, 0))),
        compiler_params=_cparams("arbitrary"), name="ffn_down_loss")(act, wd, x1, g3, tgt)


def _ffn_bwd(dx2b, dx2, wd, wg, wu, gte, up, x1, g2):
    S = x1.shape[0]
    tm = 256

    def body(d_ref, dx2_ref, wd_ref, wg_ref, wu_ref, g_ref, u_ref, x_ref, gn_ref,
             dg_ref, du_ref, dx_ref, dxb_ref, dgn_ref):
        @pl.when(pl.program_id(0) == 0)
        def _():
            dgn_ref[...] = jnp.zeros_like(dgn_ref)

        d = d_ref[...]
        dacts = [_dot_nt(d, wd_ref[s]) for s in range(N_SHARD)]
        dgs, dus = [], []
        for s, da in enumerate(dacts):
            g = g_ref[s].astype(F32)
            sg = _sigmoid(g)
            dgs.append((da * u_ref[s].astype(F32) * sg * (1.0 + g * (1.0 - sg))).astype(BF16))
            dus.append((da * g * sg).astype(BF16))
            dg_ref[s] = dgs[s]
            du_ref[s] = dus[s]
        dh = _dot(dgs[0], wg_ref[0]) + _dot(dus[0], wu_ref[0])
        for s in range(1, N_SHARD):
            dh = dh + _dot(dgs[s], wg_ref[s]) + _dot(dus[s], wu_ref[s])
        xv = x_ref[...]
        r = lax.rsqrt(jnp.mean(xv * xv, axis=-1, keepdims=True) + NORM_EPS)
        xh = xv * r
        dgn_ref[...] += jnp.sum(dh * xh, axis=0, keepdims=True)
        dxh = dh * gn_ref[...]
        dx = dx2_ref[...] + r * (dxh - xh * jnp.mean(dxh * xh, axis=-1, keepdims=True))
        dx_ref[...] = dx
        dxb_ref[...] = dx.astype(BF16)

    row = pl.BlockSpec((tm, D_MODEL), lambda i: (i, 0))
    vec = pl.BlockSpec((1, D_MODEL), lambda i: (0, 0))
    aspec = pl.BlockSpec((N_SHARD, tm, HID_S), lambda i: (0, i, 0))
    resident = lambda shape: pl.BlockSpec(shape, lambda i: (0, 0, 0), pipeline_mode=pl.Buffered(1))
    return pl.pallas_call(
        body,
        out_shape=(SDS((N_SHARD, S, HID_S), BF16), SDS((N_SHARD, S, HID_S), BF16),
                   SDS((S, D_MODEL), F32), SDS((S, D_MODEL), BF16), SDS((1, D_MODEL), F32)),
        grid=(S // tm,),
        in_specs=[row, row, resident((N_SHARD, HID_S, D_MODEL)), resident((N_SHARD, HID_S, D_MODEL)),
                  resident((N_SHARD, HID_S, D_MODEL)), aspec, aspec, row, vec],
        out_specs=(aspec, aspec, row, row, vec),
        compiler_params=_cparams("arbitrary"), name="ffn_bwd")(dx2b, dx2, wd, wg, wu, gte, up, x1, g2)


def _wgrad(name, a, b, a_spec, b_spec, out_shape, out_spec, n_par, S):
    tk = min(S, 4096)

    def body(a_ref, b_ref, o_ref):
        @pl.when(pl.program_id(1) == 0)
        def _():
            o_ref[...] = jnp.zeros_like(o_ref)

        o_ref[...] += _dot_tn(a_ref[...], b_ref[...])

    return pl.pallas_call(
        body, out_shape=SDS(out_shape, F32), grid=(n_par, S // tk),
        in_specs=[a_spec(tk), b_spec(tk)], out_specs=out_spec,
        compiler_params=_cparams("parallel", "arbitrary"), name=name)(a, b)


def _mix_bwd(dx1b, wo, proj, ya, yr, wa, wr, att, exchanges=()):
    S = dx1b.shape[0]
    tm = 512
    gate0 = 15 * COLB

    def body(d_ref, wo_ref, ga_ref, gr_ref, ya_ref, yr_ref, wa_ref, wr_ref, att_ref,
             dya_ref, dyr_ref, dp_ref, dyi_ref, datt_ref, datt1_ref, datt2_ref, rho_ref, rho1_ref, rho2_ref,
             datt_scr, rho_scr):
        pieces = _row_pieces(tm, 256)
        dms = [_dot_nt(d_ref[rows, :], wo_ref[...]) for rows in pieces]
        branch = []
        for rows, dm in zip(pieces, dms):
            sa = _sigmoid(ga_ref[rows, :].astype(F32))
            sr = _sigmoid(gr_ref[rows, :].astype(F32))
            dya, dyr = (dm * sa).astype(BF16), (dm * sr).astype(BF16)
            dya_ref[rows, :] = dya
            dyr_ref[rows, :] = dyr
            dp_ref[rows, 0:D_MODEL] = (dm * ya_ref[rows, :].astype(F32) * sa * (1.0 - sa)).astype(BF16)
            dp_ref[rows, D_MODEL:2 * D_MODEL] = (dm * yr_ref[rows, :].astype(F32) * sr * (1.0 - sr)).astype(BF16)
            branch.append((dya, dyr))
        lane = lax.broadcasted_iota(jnp.int32, (256, 128), 1)
        lo = lane < 64
        for rows, (dya, dyr) in zip(pieces, branch):
            datt = _dot_nt(dya, wa_ref[...])
            datt_ref[rows, :] = datt.astype(BF16)
            dyi_ref[rows, :] = _dot_nt(dyr, wr_ref[...]).astype(BF16)
            prod = datt * att_ref[rows, :].astype(F32)
            rho = jnp.zeros((256, 128), F32)
            for c in range(4):
                pc = prod[:, c * 128:(c + 1) * 128]
                tot = jnp.sum(pc, axis=-1, keepdims=True)
                low = jnp.sum(jnp.where(lo, pc, 0.0), axis=-1, keepdims=True)
                rho = jnp.where(lane // 16 == 2 * c, low, jnp.where(lane // 16 == 2 * c + 1, tot - low, rho))
            rho_ref[rows, :] = rho
            rho_scr[0] = rho
            for c in range(4):
                datt_scr[c] = datt[:, c * 128:(c + 1) * 128]
            for d, dv_ref, rv_ref in ((DILATIONS[1], datt1_ref, rho1_ref), (DILATIONS[2], datt2_ref, rho2_ref)):
                n = 256 // d
                sub_rows = slice(rows.start // d, rows.start // d + n)
                for r in range(d):
                    rv_ref[sub_rows, r * 128:(r + 1) * 128] = rho_scr[0, pl.ds(r, n, stride=d), :]
                    for c in range(4):
                        col = r * 512 + c * 128
                        dv_ref[sub_rows, col:col + 128] = datt_scr[c, pl.ds(r, n, stride=d), :].astype(BF16)

    row = lambda w: pl.BlockSpec((tm, w), lambda i: (i, 0))
    sub = lambda w: [pl.BlockSpec((tm // d, d * w), lambda i: (i, 0)) for d in DILATIONS]
    cols = lambda c0, w: pl.BlockSpec((pl.Element(tm), pl.Element(w)), lambda i: (i * tm, c0))
    resident = lambda r, c: pl.BlockSpec((r, c), lambda i: (0, 0), pipeline_mode=pl.Buffered(1))
    (dya, dyr, dproj, dyrin, *views), xres = _carrier_call(
        body, (dx1b, wo, proj, proj, ya, yr, wa, wr, att),
        out_shape=(SDS((S, D_MODEL), BF16), SDS((S, D_MODEL), BF16), SDS((S, PROJ_W), BF16), SDS((S, D_MODEL), BF16),
                   *[SDS((S // d, d * 512), BF16) for d in DILATIONS], *[SDS((S // d, d * 128), F32) for d in DILATIONS]),
        grid=(S // tm,),
        in_specs=[row(D_MODEL), resident(D_MODEL, D_MODEL), cols(gate0, D_MODEL), cols(gate0 + D_MODEL, D_MODEL),
                  row(D_MODEL), row(D_MODEL), resident(512, D_MODEL), resident(D_MODEL, D_MODEL), row(512)],
        out_specs=(row(D_MODEL), row(D_MODEL), cols(gate0, 2 * D_MODEL), row(D_MODEL), *sub(512), *sub(128)),
        scratch_shapes=[pltpu.VMEM((4, 256, 128), F32), pltpu.VMEM((1, 256, 128), F32)],
        sem=("parallel",), name="mix_bwd", exchanges=exchanges)
    return (dya, dyr, dproj, dyrin, views[:3], views[3:]), xres


def _attn_bwd(qkv, datt, lse, rho, rtab, d, gi, exchanges=()):
    L = qkv.shape[0]
    nb = L // BLK
    T = d * nb

    def body(q_ref, kc_ref, kp_ref, vc_ref, vp_ref, do_ref, lse_ref, rho_ref, tq_ref, tk_ref,
             dq_ref, dk_ref, dv_ref, ck, cv):
        t = pl.program_id(0)
        n = jnp.minimum(t, T - 1) % nb

        @pl.when(t == 0)
        def _():
            ck[...] = jnp.zeros_like(ck)
            cv[...] = jnp.zeros_like(cv)

        def store_rot(ref, val, t_ref, c):
            sl = slice(c * 128, (c + 1) * 128)
            ref[:, sl] = _unrot(val, t_ref[0], t_ref[1], t_ref[2], 32).astype(BF16)

        @pl.when(t < T)
        def _():
            mask = _band_mask(n)
            mask2 = jnp.concatenate([mask, mask], axis=0)
            lo = lax.broadcasted_iota(jnp.int32, (BLK, 128), 1) < 64

            def stacked(a):
                return jnp.concatenate([jnp.where(lo, a, jnp.zeros_like(a)), jnp.where(lo, jnp.zeros_like(a), a)], axis=0)

            def head_cols(ref, c):
                return jnp.concatenate([jnp.broadcast_to(ref[:, 32 * c:32 * c + 1], (BLK, 2 * BLK)),
                                        jnp.broadcast_to(ref[:, 32 * c + 16:32 * c + 17], (BLK, 2 * BLK))], axis=0)

            ops, raw = [], []
            for c in range(4):
                sl = slice(c * 128, (c + 1) * 128)
                q2, do2 = stacked(q_ref[:, sl]), stacked(do_ref[:, sl])
                k = jnp.concatenate([kp_ref[:, sl], kc_ref[:, sl]], axis=0)
                v = jnp.concatenate([vp_ref[:, sl], vc_ref[:, sl]], axis=0)
                ops.append((q2, do2, k))
                raw.append((_dot_nt(q2, k), _dot_nt(do2, v)))
            grads = []
            for c, (s, dp) in enumerate(raw):
                p = jnp.where(mask2, jnp.exp(s * 0.125 - head_cols(lse_ref, c)), 0.0)
                grads.append(((p * (dp - head_cols(rho_ref, c)) * 0.125).astype(BF16), p.astype(BF16)))
            for c, ((q2, do2, k), (ds, pb)) in enumerate(zip(ops, grads)):
                sl = slice(c * 128, (c + 1) * 128)
                dq2 = _dot(ds, k)
                dq_c = jnp.where(lo, dq2[:BLK], dq2[BLK:])
                dk_c = _dot_tn(ds, q2)
                dv_c = _dot_tn(pb, do2)
                store_rot(dq_ref, dq_c, tq_ref, c)
                store_rot(dk_ref, ck[:, sl] + dk_c[:BLK], tk_ref, c)
                dv_ref[:, sl] = (cv[:, sl] + dv_c[:BLK]).astype(BF16)
                ck[:, sl] = dk_c[BLK:]
                cv[:, sl] = dv_c[BLK:]

        @pl.when(t == T)
        def _():
            for c in range(4):
                sl = slice(c * 128, (c + 1) * 128)
                store_rot(dk_ref, ck[:, sl], tk_ref, c)
            dv_ref[...] = cv[...].astype(BF16)

    blk_of = lambda t: (jnp.minimum(t, T - 1) % nb, jnp.minimum(t, T - 1) // nb)
    cur = lambda t: blk_of(t)
    prev = lambda t: (jnp.maximum(blk_of(t)[0] - 1, 0), blk_of(t)[1])
    fin = lambda t: blk_of(jnp.maximum(t - 1, 0))
    col = _qkv_col(d, gi)
    qkv_spec = lambda kind, which: pl.BlockSpec((BLK, 512), lambda t: (which(t)[0], col(kind, which(t)[1])))
    row_spec = lambda w, which: pl.BlockSpec((BLK, w), lambda t: which(t))
    tab_spec = lambda which: pl.BlockSpec((3, BLK, 128), lambda t: (0, *which(t)))
    return _carrier_call(
        body, (qkv, qkv, qkv, qkv, qkv, datt, lse, rho, rtab, rtab),
        out_shape=(SDS((L, d * 512), BF16),) * 3, grid=(T + 1,),
        in_specs=[qkv_spec(0, cur), qkv_spec(1, cur), qkv_spec(1, prev), qkv_spec(2, cur), qkv_spec(2, prev),
                  row_spec(512, cur), row_spec(128, cur), row_spec(128, cur), tab_spec(cur), tab_spec(fin)],
        out_specs=(row_spec(512, cur), row_spec(512, fin), row_spec(512, fin)),
        scratch_shapes=[pltpu.VMEM((BLK, 512), F32), pltpu.VMEM((BLK, 512), F32)],
        sem=("arbitrary",), name=f"attn_bwd_g{gi}", exchanges=exchanges)


def _ret_bwd(proj, rn, rstd, dyrin, states, tab, consts, dproj, exchanges=()):
    S = proj.shape[0]
    nc = S // BLK
    dmask, zeta, xi, dec = consts

    def body(q_ref, k_ref, v0_ref, v1_ref, g0_ref, g1_ref, rn_ref, rs_ref, dy_ref, st_ref, tq_ref, tk_ref,
             dm_ref, z_ref, x_ref, dec_ref, dp_prev, dp_ref, dR):
        dq_ref, dk_ref = dp_ref.at[:, 0:512], dp_ref.at[:, 512:1024]
        dv_ref, dgr_ref = dp_ref.at[:, 1024:2048], dp_ref.at[:, 2048:3072]

        @pl.when(pl.program_id(0) == 0)
        def _():
            dR[...] = jnp.zeros_like(dR)

        dobs = []
        for h in range(RET_HEADS):
            vs = slice((h % 2) * 256, (h % 2 + 1) * 256)
            os_ = slice(h * 256, (h + 1) * 256)
            gr = (g0_ref if h < 2 else g1_ref)[:, vs].astype(F32)
            sg = _sigmoid(gr)
            rn_v = rn_ref[:, os_].astype(F32)
            dyi = dy_ref[:, os_].astype(F32)
            dgr_ref[:, os_] = (dyi * rn_v * sg * (1.0 + gr * (1.0 - sg))).astype(BF16)
            drn = dyi * gr * sg
            rstd = jnp.broadcast_to(rs_ref[:, 16 * h:16 * h + 1], (BLK, 256))
            do = rstd * (drn - jnp.mean(drn, axis=-1, keepdims=True) - rn_v * jnp.mean(drn * rn_v, axis=-1, keepdims=True))
            dobs.append(do.astype(BF16))
        first = []
        for h in range(RET_HEADS):
            hs = slice(h * 128, (h + 1) * 128)
            q, k = q_ref[:, hs], k_ref[:, hs]
            v = (v0_ref if h < 2 else v1_ref)[:, (h % 2) * 256:(h % 2 + 1) * 256]
            dob, dRb = dobs[h], dR[h].astype(BF16)
            kz = (k.astype(F32) * z_ref[h]).astype(BF16)
            qx = (q.astype(F32) * x_ref[h]).astype(BF16)
            first.append((q, k, _dot_nt(q, k), _dot_nt(dob, v), _dot(kz, dRb), _dot_nt(dob, st_ref[h]),
                          _dot_nt(v, dRb), _dot_tn(qx, dob)))
        masked = [((s * dm_ref[h]).astype(BF16), (dsr * dm_ref[h]).astype(BF16))
                  for h, (_, _, s, dsr, _, _, _, _) in enumerate(first)]
        for h in range(RET_HEADS):
            hs = slice(h * 128, (h + 1) * 128)
            os_ = slice(h * 256, (h + 1) * 256)
            q, k, _, _, dv_state, dq_state, dk_state, dr_new = first[h]
            sD, dS = masked[h]
            dv_ref[:, os_] = (_dot_tn(sD, dobs[h]) + dv_state).astype(BF16)
            dq = _dot(dS, k) + dq_state * x_ref[h]
            dk = _dot_tn(dS, q) + dk_state * z_ref[h]
            dR[h] = dR[h] * dec_ref[h, 0:1, :] + dr_new
            dq_ref[:, hs] = _unrot(dq, tq_ref[0], tq_ref[1], tq_ref[2], 1).astype(BF16)
            dk_ref[:, hs] = _unrot(dk, tk_ref[0], tk_ref[1], tk_ref[2], 1).astype(BF16)

    rc = lambda c: nc - 1 - c
    cst = lambda shape: pl.BlockSpec(shape, lambda c: (0, 0, 0))
    blk = lambda j: pl.BlockSpec((BLK, 512), lambda c: (rc(c), j))
    row = lambda w: pl.BlockSpec((BLK, w), lambda c: (rc(c), 0))
    (dproj,), xres = _carrier_call(
        body, (proj, proj, proj, proj, proj, proj, rn, rstd, dyrin, states, tab, tab, dmask, zeta, xi, dec, dproj),
        out_shape=(SDS((S, PROJ_W), BF16),), grid=(nc,),
        in_specs=[blk(QR_B), blk(KR_B), blk(11), blk(12), blk(13), blk(14), row(1024), row(128), row(1024),
                  pl.BlockSpec((RET_HEADS, None, BLK, 256), lambda c: (0, rc(c), 0, 0)),
                  pl.BlockSpec((None, 3, BLK, 128), lambda c: (1, 0, rc(c), 0)),
                  pl.BlockSpec((None, 3, BLK, 128), lambda c: (2, 0, rc(c), 0)),
                  cst((RET_HEADS, BLK, BLK)), cst((RET_HEADS, BLK, 128)), cst((RET_HEADS, BLK, 128)), cst((RET_HEADS, 8, 256)),
                  ANY],
        out_specs=(pl.BlockSpec((pl.Element(BLK), pl.Element(6 * COLB)), lambda c: (rc(c) * BLK, QR_B * COLB)),),
        scratch_shapes=[pltpu.VMEM((RET_HEADS, BLK, 256), F32)],
        sem=("arbitrary",), name="ret_bwd", exchanges=exchanges, in_out_aliases={16: 0})
    return dproj, xres


def _wgrad_in_half(ht, dproj, sidx, kept, exchanges=()):
    S = dproj.shape[0]
    tk = 2048
    half = (lambda sx: sx[4]) if kept else (lambda sx: 1 - sx[4])

    def body(a_ref, b_ref, o_ref):
        @pl.when(pl.program_id(1) == 0)
        def _():
            o_ref[...] = jnp.zeros_like(o_ref)

        o_ref[...] += _dot(a_ref[...], b_ref[...])

    (g,), xres = _carrier_call(
        body, (ht, dproj), out_shape=(SDS((D_MODEL // 2, PROJ_W), F32),), grid=(N_SHARD, S // tk),
        in_specs=[pl.BlockSpec((D_MODEL // 2, tk), lambda s, k, sx: (half(sx), k)),
                  pl.BlockSpec((tk, W_IN_S), lambda s, k, sx: (k, s))],
        out_specs=(pl.BlockSpec((D_MODEL // 2, W_IN_S), lambda s, k, sx: (0, s)),),
        sem=("parallel", "arbitrary"), name="wgrad_in_kept" if kept else "wgrad_in_sent", exchanges=exchanges,
        prefetch=sidx)
    return g, xres


def _in_proj_bwd(dproj, w_in, x, g1, dx1, exchanges=()):
    S = x.shape[0]
    tm = 1024

    def body(d_ref, w_ref, x_ref, g_ref, dx1_ref, dx_ref, dgn_ref, acc):
        i, s = pl.program_id(0), pl.program_id(1)

        @pl.when(s == 0)
        def _():
            acc[...] = jnp.zeros_like(acc)

        @pl.when((i == 0) & (s == 0))
        def _():
            dgn_ref[...] = jnp.zeros_like(dgn_ref)

        acc[...] += _dot_nt(d_ref[...], w_ref[...])

        @pl.when(s == N_SHARD - 1)
        def _():
            xv = x_ref[...]
            r = lax.rsqrt(jnp.mean(xv * xv, axis=-1, keepdims=True) + NORM_EPS)
            xh = xv * r
            dh = acc[...]
            dgn_ref[...] += jnp.sum(dh * xh, axis=0, keepdims=True)
            dxh = dh * g_ref[...]
            dx_ref[...] = dx1_ref[...] + r * (dxh - xh * jnp.mean(dxh * xh, axis=-1, keepdims=True))

    row = pl.BlockSpec((tm, D_MODEL), lambda i, s: (i, 0))
    vec = pl.BlockSpec((1, D_MODEL), lambda i, s: (0, 0))
    (gx, dg), xres = _carrier_call(
        body, (dproj, w_in, x, g1, dx1),
        out_shape=(SDS((S, D_MODEL), F32), SDS((1, D_MODEL), F32)), grid=(S // tm, N_SHARD),
        in_specs=[pl.BlockSpec((tm, W_IN_S), lambda i, s: (i, s)),
                  pl.BlockSpec((D_MODEL, W_IN_S), lambda i, s: (0, s)), row, vec, row],
        out_specs=(row, vec), scratch_shapes=[pltpu.VMEM((tm, D_MODEL), F32)],
        sem=("arbitrary", "arbitrary"), name="in_proj_bwd", exchanges=exchanges)
    return gx, dg, xres


def _step(x, tgt, g1, g2, g3, comm):
    S = x.shape[0]
    tab_np = _tables(S)
    tab = jnp.asarray(tab_np)
    consts = _ret_consts()

    (h, ht, *casts), xres = _rms_fwd(x, g1, comm.to_cast(), comm.carry("rms_fwd"))
    comm.cast_done(casts)
    comm.took("rms_fwd", xres)
    w_in = comm.weight(0)
    proj, xres = _in_proj(h, w_in, tab, comm.carry("in_proj"))
    comm.took("in_proj", xres)
    qkvs, o_parts, lse_parts = [], [], []
    for gi, d in enumerate(DILATIONS):
        qkv = proj if d == 1 else _qkv_to_sub(proj, d, gi)
        (o_g, lse_g), xres = _attn_fwd(qkv, d, gi, comm.carry(f"attn_fwd_g{gi}"))
        comm.took(f"attn_fwd_g{gi}", xres)
        qkvs.append(qkv)
        o_parts.append(o_g)
        lse_parts.append(lse_g)
    att, lse_views = _attn_merge(o_parts, lse_parts)
    (yrin, rn, rstd, states), xres = _ret_fwd(proj, consts, comm.carry("ret_fwd"))
    comm.took("ret_fwd", xres)
    wa, wr, wo = comm.weight(1), comm.weight(2), comm.weight(3)
    (merged, ya, yr, x1, h2), xres = _mix_out(att, yrin, proj, wa, wr, wo, x, g2, comm.carry("mix_out"))
    comm.took("mix_out", xres)
    wg, wu = comm.weight(4), comm.weight(5)
    (gte, up, act), xres = _ffn_up(h2, wg, wu, comm.carry("ffn_up"))
    comm.took("ffn_up", xres)
    wd = comm.weight(6)
    dx2, dx2b, dg3, loss_p = _ffn_down_loss(act, wd, x1, g3, tgt)

    dgte, dup, dx1, dx1b, dg2 = _ffn_bwd(dx2b, dx2, wd, wg, wu, gte, up, x1, g2)
    tok3 = lambda w: (lambda tk: pl.BlockSpec((None, tk, w), lambda p, k: (p, k, 0)))
    tok2 = lambda w: (lambda tk: pl.BlockSpec((tk, w), lambda p, k: (k, 0)))
    g_d = _wgrad("wgrad_down", act, dx2b, tok3(HID_S), tok2(D_MODEL), (N_SHARD, HID_S, D_MODEL),
                 pl.BlockSpec((None, HID_S, D_MODEL), lambda p, k: (p, 0, 0)), N_SHARD, S)
    g_g = _wgrad("wgrad_gate", dgte, h2, tok3(HID_S), tok2(D_MODEL), (N_SHARD, HID_S, D_MODEL),
                 pl.BlockSpec((None, HID_S, D_MODEL), lambda p, k: (p, 0, 0)), N_SHARD, S)
    g_u = _wgrad("wgrad_up", dup, h2, tok3(HID_S), tok2(D_MODEL), (N_SHARD, HID_S, D_MODEL),
                 pl.BlockSpec((None, HID_S, D_MODEL), lambda p, k: (p, 0, 0)), N_SHARD, S)
    comm.grads({4: g_g, 5: g_u, 6: g_d})
    (dya, dyr, dproj, dyrin, datt_views, rho_views), xres = _mix_bwd(dx1b, wo, proj, ya, yr, wa, wr, att,
                                                                       comm.carry("mix_bwd"))
    comm.took("mix_bwd", xres)
    colblk = lambda w: (lambda tk: pl.BlockSpec((tk, w), lambda p, k: (k, p)))
    g_o = _wgrad("wgrad_out", merged, dx1b, colblk(256), tok2(D_MODEL), (D_MODEL, D_MODEL),
                 pl.BlockSpec((256, D_MODEL), lambda p, k: (p, 0)), 4, S)
    g_a = _wgrad("wgrad_attn", att, dya, tok2(512), colblk(512), (512, D_MODEL),
                 pl.BlockSpec((512, 512), lambda p, k: (0, p)), 2, S)
    g_r = _wgrad("wgrad_ret", yrin, dyr, colblk(256), tok2(D_MODEL), (D_MODEL, D_MODEL),
                 pl.BlockSpec((256, D_MODEL), lambda p, k: (p, 0)), 4, S)
    comm.grads({1: g_a, 2: g_r.reshape(N_SHARD, 256, D_MODEL), 3: g_o.reshape(N_SHARD, 256, D_MODEL)})
    dproj, xres = _ret_bwd(proj, rn, rstd, dyrin, states, tab, consts, dproj, comm.carry("ret_bwd"))
    comm.took("ret_bwd", xres)
    dqs, dks, dvs = [], [], []
    for gi, d in enumerate(DILATIONS):
        rtab = jnp.asarray(tab_np[0].reshape(3, S // d, d * 128))
        (dq, dk, dv), xres = _attn_bwd(qkvs[gi], datt_views[gi], lse_views[gi], rho_views[gi], rtab, d, gi,
                                       comm.carry(f"attn_bwd_g{gi}"))
        comm.took(f"attn_bwd_g{gi}", xres)
        dqs.append(dq)
        dks.append(dk)
        dvs.append(dv)
    dproj = _assemble_dproj((dqs, dks, dvs), dproj)
    g_sent, xres = _wgrad_in_half(ht, dproj, comm.sidx, False, comm.carry("wgrad_in_sent"))
    comm.took("wgrad_in_sent", xres)
    comm.grads({"in_sent": g_sent})
    g_kept, xres = _wgrad_in_half(ht, dproj, comm.sidx, True, comm.carry("wgrad_in_kept"))
    comm.grads({"in_kept": g_kept})
    comm.took("wgrad_in_kept", xres)
    grad_x, dg1, xres = _in_proj_bwd(dproj, w_in, x, g1, dx1, comm.carry("in_proj_bwd"))
    comm.took("in_proj_bwd", xres)
    return loss_p, grad_x, (dg1, dg2, dg3)


W_KINDS = ("col", "col", "lead", "lead", "lead", "lead", "lead")
W_SHARD = ((1024, W_IN_S), (512, 256), (256, 1024), (256, 1024), (HID_S, 1024), (HID_S, 1024), (HID_S, 1024))
W_TRANSPOSED = (4, 5)
N_W = len(W_KINDS)


def _full_shape(wi):
    R, C = W_SHARD[wi]
    return (R, N_SHARD * C) if W_KINDS[wi] == "col" else (N_SHARD, R, C)


def _view(ref, wi, s, half):
    R, C = W_SHARD[wi]
    rows = pl.ds(half * (R // 2), R // 2)
    if W_KINDS[wi] == "col":
        return ref.at[rows, pl.ds(pl.multiple_of(s * C, 128), C)]
    return ref.at[s, rows, :]


def _mesh_pos():
    x, y, c = lax.axis_index("x"), lax.axis_index("y"), lax.axis_index("c")
    chips = [(1 - x, y), (x, 1 - y), (1 - x, 1 - y)]
    return x, y, c, chips


def _cast_bf16(a):
    R, C = a.shape
    tr = R // 2 if R % 32 == 0 else R

    def body(a_ref, o_ref):
        o_ref[...] = a_ref[...].astype(BF16)

    spec = pl.BlockSpec((tr, C), lambda i: (i, 0))
    return pl.pallas_call(body, out_shape=SDS((R, C), BF16), grid=(R // tr,), in_specs=[spec], out_specs=spec,
                          compiler_params=_cparams("parallel"), name=f"cast_{R}x{C}")(a)


def _remote(send, recv, k, src, dst, to):
    return pltpu.make_async_remote_copy(src_ref=src, dst_ref=dst, send_sem=send.at[k], recv_sem=recv.at[k],
                                        device_id=to, device_id_type=MESH)


def _ex_gather_ring(wis, shards):
    n = len(wis)

    def build(sh, full, send, recv, loc):
        x, y, c, _ = _mesh_pos()
        s_me, sib = 2 * x + y, (x, y, 1 - c)
        xn, yn = (1 - x, y), (x, 1 - y)
        flip = lambda a, b: a + b - 2 * a * b
        via = (flip(x, 1 - c), flip(y, c))
        onto = (flip(x, c), flip(y, 1 - c))
        shard_of = lambda chip: 2 * chip[0] + chip[1]
        starts, waits, sent = [], [], []
        for i, wi in enumerate(wis):
            Rh = W_SHARD[wi][0] // 2
            for hf in range(2):
                cp = pltpu.make_async_copy(sh[i].at[pl.ds(hf * Rh, Rh), :], _view(full[i], wi, s_me, hf), loc.at[2 * i + hf])
                starts.append(cp)
                sent.append(cp.wait)
            for j, chip in enumerate((xn, yn)):
                cp = _remote(send, recv, 6 * i + j, sh[i].at[pl.ds(c * Rh, Rh), :], _view(full[i], wi, s_me, c), (*chip, c))
                starts.append(cp)
                sent.append(cp.wait_send)

        def pass_to_sibling(i, wi, k, s):
            mine = _view(full[i], wi, s, c)
            fw = _remote(send, recv, 6 * i + k, mine, mine, sib)
            waits.append(fw.start)
            sent.append(fw.wait_send)

        for i, wi in enumerate(wis):
            for j, chip in enumerate((xn, yn)):
                land = _view(full[i], wi, shard_of(chip), c)
                waits.append(_remote(send, recv, 6 * i + j, land, land, (*chip, c)).wait_recv)
                pass_to_sibling(i, wi, 3 + j, shard_of(chip))
            relay = _view(full[i], wi, shard_of(via), c)
            fw = _remote(send, recv, 6 * i + 2, relay, relay, (*onto, c))
            waits.append(fw.start)
            sent.append(fw.wait_send)
        s_diag = 2 * (1 - x) + (1 - y)
        for i, wi in enumerate(wis):
            land = _view(full[i], wi, s_diag, c)
            waits.append(_remote(send, recv, 6 * i + 2, land, land, (*onto, c)).wait_recv)
            pass_to_sibling(i, wi, 5, s_diag)
        for i, wi in enumerate(wis):
            for k, s in ((3, shard_of(xn)), (4, shard_of(yn)), (5, s_diag)):
                land = _view(full[i], wi, s, 1 - c)
                waits.append(_remote(send, recv, 6 * i + k, land, land, sib).wait_recv)
        return starts, waits + sent

    return _Exchange(shards, [SDS(_full_shape(wi), BF16) for wi in wis], {}, 6 * n, 2 * n, build)


def _ex_gather_ici(wis, shards, then_d2d=False):
    n = len(wis)

    def build(ins, outs, send, recv, loc):
        x, y, c, chips = _mesh_pos()
        s_me, sib = 2 * x + y, (x, y, 1 - c)
        starts, waits, after = [], [], []
        for i, wi in enumerate(wis):
            Rh = W_SHARD[wi][0] // 2
            for hf in range(2):
                cp = pltpu.make_async_copy(ins[i].at[pl.ds(hf * Rh, Rh), :], _view(outs[i], wi, s_me, hf), loc.at[2 * i + hf])
                starts.append(cp)
                waits.append(cp.wait)
            for j, chip in enumerate(chips):
                cp = _remote(send, recv, 3 * i + j, ins[i].at[pl.ds(c * Rh, Rh), :], _view(outs[i], wi, s_me, c), (*chip, c))
                land = _view(outs[i], wi, 2 * chip[0] + chip[1], c)
                starts.append(cp)
                waits += [cp.wait_send, _remote(send, recv, 3 * i + j, land, land, (*chip, c)).wait_recv]
                if then_d2d:
                    theirs = _view(outs[i], wi, 2 * chip[0] + chip[1], 1 - c)
                    fw = _remote(send, recv, 3 * n + 3 * i + j, land, land, sib)
                    waits.append(fw.start)
                    after += [fw.wait_send, _remote(send, recv, 3 * n + 3 * i + j, theirs, theirs, sib).wait_recv]
        return starts, waits + after

    return _Exchange(shards, [SDS(_full_shape(wi), BF16) for wi in wis], {}, (6 if then_d2d else 3) * n, 2 * n, build)


def _ex_gather_d2d(wis, fulls):
    def build(ins, outs, send, recv, loc):
        x, y, c, chips = _mesh_pos()
        sib = (x, y, 1 - c)
        starts, waits = [], []
        for i, wi in enumerate(wis):
            for j, chip in enumerate(chips):
                mine = _view(outs[i], wi, 2 * chip[0] + chip[1], c)
                theirs = _view(outs[i], wi, 2 * chip[0] + chip[1], 1 - c)
                cp = _remote(send, recv, 3 * i + j, mine, mine, sib)
                starts.append(cp)
                waits += [cp.wait_send, _remote(send, recv, 3 * i + j, theirs, theirs, sib).wait_recv]
        return starts, waits

    return _Exchange(fulls, [SDS(f.shape, BF16) for f in fulls], {i: i for i in range(len(wis))}, 3 * len(wis), 0, build)


def _half_shape(wi):
    R, C = W_SHARD[wi]
    return (R // 2, N_SHARD * C) if W_KINDS[wi] == "col" else (N_SHARD, R // 2, C)


def _ex_pair(wis, grads):
    def build(ins, outs, send, recv, loc):
        x, y, c, _ = _mesh_pos()
        starts, waits = [], []
        for i, wi in enumerate(wis):
            Rh = W_SHARD[wi][0] // 2
            rows = pl.ds((1 - c) * Rh, Rh)
            if tuple(ins[i].shape) == _half_shape(wi):
                src = ins[i]
            else:
                src = ins[i].at[rows, :] if W_KINDS[wi] == "col" else ins[i].at[:, rows, :]
            cp = _remote(send, recv, i, src, outs[i], (x, y, 1 - c))
            starts.append(cp)
            waits.append(cp.wait)
        return starts, waits

    return _Exchange(grads, [SDS(_half_shape(wi), F32) for wi in wis], {}, len(wis), 0, build)


def _ex_chip(wis, pbs):
    def build(ins, outs, send, recv, loc):
        x, y, c, chips = _mesh_pos()
        starts, waits = [], []
        for i, wi in enumerate(wis):
            for j, chip in enumerate(chips):
                cp = _remote(send, recv, 3 * i + j, ins[i].at[j], outs[i].at[j], (*chip, c))
                starts.append(cp)
                waits.append(cp.wait)
        return starts, waits

    shapes = [SDS((3, W_SHARD[wi][0] // 2, W_SHARD[wi][1]), BF16) for wi in wis]
    return _Exchange(pbs, shapes, {}, 3 * len(wis), 0, build)


def _ex_share(wis, halves):
    def build(ins, outs, send, recv, loc):
        x, y, c, _ = _mesh_pos()
        sib = (x, y, 1 - c)
        starts, waits = [], []
        for i, wi in enumerate(wis):
            cp = _remote(send, recv, i, outs[i].at[c], outs[i].at[c], sib)
            starts.append(cp)
            waits += [cp.wait_send, _remote(send, recv, i, outs[i].at[1 - c], outs[i].at[1 - c], sib).wait_recv]
        return starts, waits

    return _Exchange(halves, [SDS(h.shape, F32) for h in halves], {i: i for i in range(len(wis))}, len(wis), 0, build)


def _row_tile(rh, C):
    best = 16
    for t in range(16, rh + 1, 16):
        if rh % t == 0 and t * C * 4 <= (3 << 19):
            best = t
    return best


def _pair_sum(wi, g, ra, sidx):
    R, C = W_SHARD[wi]
    Rh = R // 2
    tr = _row_tile(Rh, C)
    nt = Rh // tr
    off = 0 if tuple(g.shape) == _half_shape(wi) else nt
    col = W_KINDS[wi] == "col"

    def body(sidx_ref, *refs):
        gs, rs = refs[:4], refs[4:8]
        own_ref, pb_ref = refs[8:]
        own_ref[...] = gs[0][...] + rs[0][...]
        for j in range(3):
            pb_ref[j] = (gs[1 + j][...] + rs[1 + j][...]).astype(BF16)

    def gspec(slot):
        if col:
            return pl.BlockSpec((tr, C), lambda i, sx: (sx[4] * off + i, sx[slot]))
        return pl.BlockSpec((None, tr, C), lambda i, sx: (sx[slot], sx[4] * off + i, 0))

    def rspec(slot):
        if col:
            return pl.BlockSpec((tr, C), lambda i, sx: (i, sx[slot]))
        return pl.BlockSpec((None, tr, C), lambda i, sx: (sx[slot], i, 0))

    return pl.pallas_call(
        body, out_shape=(SDS((Rh, C), F32), SDS((3, Rh, C), BF16)),
        grid_spec=pltpu.PrefetchScalarGridSpec(
            num_scalar_prefetch=1, grid=(nt,),
            in_specs=[gspec(k) for k in range(4)] + [rspec(k) for k in range(4)],
            out_specs=(pl.BlockSpec((tr, C), lambda i, sx: (i, 0)), pl.BlockSpec((3, tr, C), lambda i, sx: (0, i, 0)))),
        compiler_params=_cparams("arbitrary"), name=f"pair_sum_w{wi}")(sidx, g, g, g, g, ra, ra, ra, ra)


def _chip_sum(wi, own, rb, sidx):
    R, C = W_SHARD[wi]
    Rh = R // 2
    tr = _row_tile(Rh, C)

    def body(sidx_ref, own_ref, rb_ref, o_ref):
        o_ref[...] = ((own_ref[...] + rb_ref[0].astype(F32)) + rb_ref[1].astype(F32)) + rb_ref[2].astype(F32)

    return pl.pallas_call(
        body, out_shape=SDS((2, Rh, C), F32),
        grid_spec=pltpu.PrefetchScalarGridSpec(
            num_scalar_prefetch=1, grid=(Rh // tr,),
            in_specs=[pl.BlockSpec((tr, C), lambda i, sx: (i, 0)), pl.BlockSpec((3, tr, C), lambda i, sx: (0, i, 0))],
            out_specs=pl.BlockSpec((None, tr, C), lambda i, sx: (sx[4], i, 0))),
        compiler_params=_cparams("arbitrary"), name=f"chip_sum_w{wi}")(sidx, own, rb)


def _gain_allgather(blk, ex):
    m_per, n = blk.shape
    n_in, n_out = len(ex.ins), len(ex.out_shapes)

    def body(x_ref, *rest):
        xin, out_ref, xout = rest[:n_in], rest[n_in], rest[n_in + 1:n_in + 1 + n_out]
        send_sems, recv_sems, local_sem = rest[n_in + 1 + n_out:n_in + 4 + n_out]
        ex_starts, ex_waits = ex.build(xin, xout, *rest[n_in + 4 + n_out:])
        for cp in ex_starts:
            cp.start()
        x, y, c, chips = _mesh_pos()
        me, sibling = (x, y, c), (x, y, 1 - c)

        def rows(px, py, pc):
            return out_ref.at[pl.ds((4 * px + 2 * py + pc) * m_per, m_per), :]

        def copy(k, block, to, src=None):
            return pltpu.make_async_remote_copy(
                src_ref=rows(*block) if src is None else src, dst_ref=rows(*block),
                send_sem=send_sems.at[k], recv_sem=recv_sems.at[k], device_id=to, device_id_type=MESH)

        mine = pltpu.make_async_copy(x_ref, rows(*me), local_sem)
        mine.start()
        first = [copy(0, me, sibling, src=x_ref)]
        first += [copy(1 + j, me, (*chip, c), src=x_ref) for j, chip in enumerate(chips)]
        for cp in first:
            cp.start()
        passed = [copy(4 + j, (*chip, c), sibling) for j, chip in enumerate(chips)]
        for j, chip in enumerate(chips):
            copy(1 + j, (*chip, c), me).wait_recv()
            passed[j].start()
        copy(0, sibling, me).wait_recv()
        for j, chip in enumerate(chips):
            copy(4 + j, (*chip, 1 - c), me).wait_recv()
        for cp in first + passed:
            cp.wait_send()
        mine.wait()
        for w in ex_waits:
            w()

    vm = pl.BlockSpec(memory_space=pltpu.VMEM)
    res = pl.pallas_call(
        body, out_shape=(SDS((8 * m_per, n), blk.dtype), *ex.out_shapes),
        in_specs=[vm] + [ANY] * n_in, out_specs=(vm, *[ANY] * n_out),
        input_output_aliases={1 + a: 1 + o for a, o in ex.aliases.items()},
        scratch_shapes=[pltpu.SemaphoreType.DMA((7,)), pltpu.SemaphoreType.DMA((7,)), pltpu.SemaphoreType.DMA] + ex.sems(),
        name="gain_allgather")(blk, *ex.ins)
    return res[0], tuple(res[1:])


def _adam_math(w, g, m, v):
    mn = ADAM_B1 * m + (1.0 - ADAM_B1) * g
    vn = ADAM_B2 * v + (1.0 - ADAM_B2) * (g * g)
    mh = mn / (1.0 - ADAM_B1 ** ADAM_STEP)
    vh = vn / (1.0 - ADAM_B2 ** ADAM_STEP)
    return -ADAM_LR * (mh / (jnp.sqrt(vh) + ADAM_EPS) + ADAM_WD * w), mn, vn


def _adamw(name, ws, gs, ms, vs):
    n, steps = len(ws), 8

    def body(*refs):
        for k in range(n):
            w_ref, g_ref, m_ref, v_ref = refs[4 * k:4 * k + 4]
            go_ref, d_ref, mn_ref, vn_ref = refs[4 * n + 4 * k:4 * n + 4 * k + 4]
            g = g_ref[...]
            go_ref[...] = g
            d_ref[...], mn_ref[...], vn_ref[...] = _adam_math(w_ref[...], g, m_ref[...], v_ref[...])

    specs = [pl.BlockSpec((w.shape[0] // steps, w.shape[1]), lambda i: (i, 0)) for w in ws for _ in range(4)]
    res = pl.pallas_call(
        body, out_shape=tuple(SDS(w.shape, F32) for w in ws for _ in range(4)), grid=(steps,),
        in_specs=specs, out_specs=tuple(specs), compiler_params=_cparams("parallel"),
        name=name)(*[a for k in range(n) for a in (ws[k], gs[k], ms[k], vs[k])])
    return [tuple(res[4 * k:4 * k + 4]) for k in range(n)]


def _gain_update(gathered, w, m, v):
    def body(ga_ref, w_ref, m_ref, v_ref, g_ref, d_ref, mn_ref, vn_ref):
        g = ga_ref[0:8, :]
        for dev in range(1, 8):
            g = g + ga_ref[8 * dev:8 * dev + 8, :]
        g_ref[...] = g
        d_ref[...], mn_ref[...], vn_ref[...] = _adam_math(w_ref[...], g, m_ref[...], v_ref[...])

    return pl.pallas_call(body, out_shape=(SDS((8, 1024), F32),) * 4, name="gain_update")(gathered, w, m, v)


GROUP_FFN, GROUP_MIX, GROUP_IN = (4, 5, 6), (1, 2, 3), (0,)
REST = GROUP_MIX + GROUP_FFN


class _MeshComm:
    SCHEDULE = {
        "rms_fwd": [("ring", GROUP_IN)],
        "in_proj": [("ici", (1, 2, 3, 4))],
        "ret_fwd": [("d2d", (1, 2, 3, 4)), ("ici", (5,))],
        "mix_out": [("d2d", (5,))],
        "ffn_up": [("both", (6,))],
        "mix_bwd": [("pair", GROUP_FFN)],
        "ret_bwd": [("pair", GROUP_MIX), ("chip", (4,))],
        "attn_bwd_g0": [("chip", (5,))],
        "attn_bwd_g1": [("chip", (6,))],
        "attn_bwd_g2": [("chip", GROUP_MIX)],
        "wgrad_in_kept": [("pair", GROUP_IN), ("share", GROUP_FFN + GROUP_MIX)],
        "in_proj_bwd": [("chip", GROUP_IN)],
    }

    def __init__(self, w_in_shard, rest_f32):
        xi, yi, ci = lax.axis_index("x"), lax.axis_index("y"), lax.axis_index("c")
        self.sidx = jnp.stack([2 * xi + yi, 2 * (1 - xi) + yi, 2 * xi + (1 - yi), 2 * (1 - xi) + (1 - yi), ci]).astype(jnp.int32)
        self.shards, self.rest_f32, self.full = {0: w_in_shard}, list(rest_f32), {}
        self.g, self.own, self.pb, self.half, self.red = {}, {}, {}, {}, {}

    def to_cast(self):
        return self.rest_f32

    def cast_done(self, casts):
        self.shards.update(zip(REST, casts))

    def weight(self, wi):
        return self.full[wi].reshape(D_MODEL, D_MODEL) if wi in (2, 3) else self.full[wi]

    def grads(self, by_wi):
        self.g.update(by_wi)

    def _exchange(self, stage, wis):
        pick = lambda table: [table[wi] for wi in wis]
        if stage == "ring":
            return _ex_gather_ring(wis, pick(self.shards))
        if stage == "ici":
            return _ex_gather_ici(wis, pick(self.shards))
        if stage == "both":
            return _ex_gather_ici(wis, pick(self.shards), then_d2d=True)
        if stage == "d2d":
            return _ex_gather_d2d(wis, pick(self.full))
        if stage == "pair":
            return _ex_pair(wis, [self.g["in_sent"] if wi == 0 else self.g[wi] for wi in wis])
        if stage == "chip":
            return _ex_chip(wis, pick(self.pb))
        return _ex_share(wis, pick(self.half))

    def _landed(self, stage, wis, res):
        for wi, r in zip(wis, res):
            if stage in ("ring", "ici", "d2d", "both"):
                self.full[wi] = r
            elif stage == "pair":
                self.own[wi], self.pb[wi] = _pair_sum(wi, self.g["in_kept"] if wi == 0 else self.g[wi], r, self.sidx)
            elif stage == "chip":
                self.half[wi] = _chip_sum(wi, self.own[wi], r, self.sidx)
            else:
                self.red[wi] = r

    def carry(self, point):
        return [self._exchange(stage, wis) for stage, wis in self.SCHEDULE.get(point, ())]

    def took(self, point, xres):
        for (stage, wis), res in zip(self.SCHEDULE.get(point, ()), xres):
            self._landed(stage, wis, res)

    def last_share(self):
        return self._exchange("share", GROUP_IN)

    def reduced(self, last_shared):
        self._landed("share", GROUP_IN, last_shared)
        return [self.red[wi] for wi in range(N_W)]


def kernel(x, norm_mix_g, w_in, w_out_attn, w_out_ret, w_out, norm_ffn_g, w_ffn_gate, w_ffn_up, w_ffn_down, norm_final_g, loss_target, m_norm_mix_g, m_w_in, m_w_out_attn, m_w_out_ret, m_w_out, m_norm_ffn_g, m_w_ffn_gate, m_w_ffn_up, m_w_ffn_down, m_norm_final_g, v_norm_mix_g, v_w_in, v_w_out_attn, v_w_out_ret, v_w_out, v_norm_ffn_g, v_w_ffn_gate, v_w_ffn_up, v_w_ffn_down, v_norm_final_g):
    ws = (w_in, w_out_attn, w_out_ret, w_out, w_ffn_gate, w_ffn_up, w_ffn_down)
    ms = (m_w_in, m_w_out_attn, m_w_out_ret, m_w_out, m_w_ffn_gate, m_w_ffn_up, m_w_ffn_down)
    vs = (v_w_in, v_w_out_attn, v_w_out_ret, v_w_out, v_w_ffn_gate, v_w_ffn_up, v_w_ffn_down)

    def shard2d(a, wi):
        return jnp.swapaxes(a[0], 0, 1) if wi in W_TRANSPOSED else a.reshape(W_SHARD[wi])

    def as_given(a2d, wi):
        return jnp.swapaxes(a2d, 0, 1)[None] if wi in W_TRANSPOSED else a2d.reshape(ws[wi].shape)

    comm = _MeshComm(_cast_bf16(shard2d(ws[0], 0)), [shard2d(ws[wi], wi) for wi in REST])
    g3 = norm_final_g.reshape(1, D_MODEL)
    loss_p, grad_x, gain_g = _step(x[0], loss_target[0], norm_mix_g, norm_ffn_g, g3, comm)

    pad8 = lambda rows: jnp.concatenate([r.reshape(1, D_MODEL) for r in rows]
                                        + [jnp.zeros((8 - len(rows), D_MODEL), F32)], axis=0)
    gathered, shared = _gain_allgather(pad8((*gain_g, jnp.tile(loss_p[0:1], (1, D_MODEL // 128)))), comm.last_share())
    gred = comm.reduced(shared)

    def adam(name, wis):
        two_d = lambda arrs: [shard2d(arrs[wi], wi) for wi in wis]
        return _adamw(name, two_d(ws), [gred[wi].reshape(W_SHARD[wi]) for wi in wis], two_d(ms), two_d(vs))

    updates = dict(zip(REST + GROUP_IN, adam("adamw_rest", REST) + adam("adamw_w_in", GROUP_IN)))
    outs_g, outs_d, outs_m, outs_v = ([as_given(updates[wi][k], wi) for wi in range(N_W)] for k in range(4))

    gg, gd, gm, gv = _gain_update(gathered, pad8((norm_mix_g, norm_ffn_g, norm_final_g)),
                                  pad8((m_norm_mix_g, m_norm_ffn_g, m_norm_final_g)),
                                  pad8((v_norm_mix_g, v_norm_ffn_g, v_norm_final_g)))
    loss = gg[3, 0]

    def assemble(gain_rows, wlist):
        return (gain_rows[0:1], wlist[0], wlist[1], wlist[2], wlist[3], gain_rows[1:2],
                wlist[4], wlist[5], wlist[6], gain_rows[2])

    return (loss, grad_x[None], *assemble(gg, outs_g), *assemble(gd, outs_d), *assemble(gm, outs_m), *assemble(gv, outs_v))
```

```python
import functools

import numpy as np
import jax
import jax.numpy as jnp
from jax import lax
from jax.experimental import pallas as pl
from jax.experimental.pallas import tpu as pltpu

F32, BF16 = jnp.float32, jnp.bfloat16
SDS = jax.ShapeDtypeStruct
MESH = pl.DeviceIdType.MESH

D_MODEL = 1024
PROJ_W = 9728
COLB = 512
N_COLB = PROJ_W // COLB
QA_B, KA_B, VA_B = 0, 3, 6
QR_B, KR_B = 9, 10
FFN_HID = 2816
N_SHARD = 4
HID_S = FFN_HID // N_SHARD
W_IN_S = PROJ_W // N_SHARD
DILATIONS = (1, 4, 16)
BLK = 128
RET_HEADS = 4
ROPE_THETA = 10000.0
NORM_EPS = 1e-6
ADAM_LR, ADAM_B1, ADAM_B2, ADAM_EPS, ADAM_WD, ADAM_STEP = 0.001, 0.9, 0.999, 1e-08, 0.01, 10
VMEM_LIMIT = 56 << 20


def _cparams(*sem):
    return pltpu.CompilerParams(dimension_semantics=sem or None, vmem_limit_bytes=VMEM_LIMIT)


def _dot(a, b):
    return jnp.dot(a, b, preferred_element_type=F32)


def _dot_nt(a, b):
    return lax.dot_general(a, b, (((1,), (1,)), ((), ())), preferred_element_type=F32)


def _dot_tn(a, b):
    return lax.dot_general(a, b, (((0,), (0,)), ((), ())), preferred_element_type=F32)


def _row_pieces(tm, sub=512):
    return [slice(i, i + sub) for i in range(0, tm, sub)]


def _sigmoid(z):
    return 0.5 * jnp.tanh(0.5 * z) + 0.5


ANY = pl.BlockSpec(memory_space=pl.ANY)


class _Exchange:
    def __init__(self, ins, out_shapes, aliases, n_sem, n_loc, build):
        self.ins, self.out_shapes, self.aliases = list(ins), list(out_shapes), dict(aliases)
        self.n_sem, self.n_loc, self.build = n_sem, n_loc, build

    def sems(self):
        return [pltpu.SemaphoreType.DMA((self.n_sem,)), pltpu.SemaphoreType.DMA((self.n_sem,)),
                pltpu.SemaphoreType.DMA((max(self.n_loc, 1),))]


def _carrier_call(body, args, *, out_shape, grid, in_specs, out_specs, scratch_shapes=(), sem, name, exchanges=(),
                  prefetch=None, in_out_aliases=None):
    out_shape, out_specs = tuple(out_shape), tuple(out_specs)
    n_in, n_out, n_scr = len(args), len(out_shape), len(scratch_shapes)
    n_pre = 0 if prefetch is None else 1
    x_args, x_outs, x_scr, spans = [], [], [], []
    aliases = {n_pre + a: o for a, o in (in_out_aliases or {}).items()}
    for ex in exchanges:
        i0, o0 = len(x_args), len(x_outs)
        for a, o in ex.aliases.items():
            aliases[n_pre + n_in + i0 + a] = n_out + o0 + o
        x_args += ex.ins
        x_outs += ex.out_shapes
        x_scr += ex.sems()
        spans.append((i0, len(ex.ins), o0, len(ex.out_shapes)))
    nx_in, nx_out = len(x_args), len(x_outs)

    def wrapped(*refs):
        refs = refs[n_pre:]
        ins, xin = refs[:n_in], refs[n_in:n_in + nx_in]
        o_base = n_in + nx_in
        outs, xout = refs[o_base:o_base + n_out], refs[o_base + n_out:o_base + n_out + nx_out]
        s_base = o_base + n_out + nx_out
        scr, xs = refs[s_base:s_base + n_scr], refs[s_base + n_scr:]

        def built(e):
            i0, ni, o0, no = spans[e]
            return exchanges[e].build(xin[i0:i0 + ni], xout[o0:o0 + no], *xs[3 * e:3 * e + 3])

        if exchanges:
            first = functools.reduce(jnp.logical_and, [pl.program_id(k) == 0 for k in range(len(grid))])
            last = functools.reduce(jnp.logical_and, [pl.program_id(k) == grid[k] - 1 for k in range(len(grid))])

            @pl.when(first)
            def _():
                for e in range(len(exchanges)):
                    for cp in built(e)[0]:
                        cp.start()

        body(*ins, *outs, *scr)

        if exchanges:
            @pl.when(last)
            def _():
                for e in range(len(exchanges)):
                    for w in built(e)[1]:
                        w()

    all_in, all_out = list(in_specs) + [ANY] * nx_in, out_specs + tuple([ANY] * nx_out)
    all_scr = list(scratch_shapes) + x_scr
    cparams = _cparams(*(sem if not exchanges else ("arbitrary",) * len(grid)))
    if prefetch is None:
        res = pl.pallas_call(wrapped, out_shape=out_shape + tuple(x_outs), grid=grid, in_specs=all_in, out_specs=all_out,
                             scratch_shapes=all_scr, input_output_aliases=aliases, compiler_params=cparams,
                             name=name)(*args, *x_args)
    else:
        gs = pltpu.PrefetchScalarGridSpec(num_scalar_prefetch=1, grid=grid, in_specs=all_in, out_specs=all_out,
                                          scratch_shapes=all_scr)
        res = pl.pallas_call(wrapped, out_shape=out_shape + tuple(x_outs), grid_spec=gs, input_output_aliases=aliases,
                             compiler_params=cparams, name=name)(prefetch, *args, *x_args)
    xres = [tuple(res[n_out + o0:n_out + o0 + no]) for (_, _, o0, no) in spans]
    return tuple(res[:n_out]), xres


def _tables(S):
    f32 = np.float32
    pos = np.arange(S, dtype=f32)
    lane = np.arange(128)
    inv = (f32(ROPE_THETA) ** (-np.arange(0, 64, 2, dtype=f32) / f32(64))).astype(f32)
    ang = (pos[:, None] * inv[None, :]).astype(np.float64)
    idx = (lane % 64) % 32
    c, s = np.cos(ang)[:, idx], np.sin(ang)[:, idx]
    first = ((lane % 64) < 32)[None, :]
    rope = np.stack([c, np.where(first, 0.0, s), np.where(first, -s, 0.0)])
    base = (f32(1.0) / (f32(ROPE_THETA) ** np.linspace(0.0, 1.0, 64, dtype=f32))).astype(f32)
    ang2 = (pos[:, None] * base[None, :]).astype(np.float64)
    c2, s2 = np.cos(ang2)[:, lane // 2], np.sin(ang2)[:, lane // 2]
    even = (lane % 2 == 0)[None, :]
    th = np.stack([c2, np.where(even, 0.0, s2), np.where(even, -s2, 0.0)])
    return np.stack([rope, th, th * (128 ** -0.5)]).astype(f32)


def _rot(a, c, sa, sb, shift):
    return a * c + pltpu.roll(a, shift, 1) * sa + pltpu.roll(a, 128 - shift, 1) * sb


def _unrot(g, c, sa, sb, shift):
    return g * c + pltpu.roll(g * sa, 128 - shift, 1) + pltpu.roll(g * sb, shift, 1)


def _ret_consts():
    h = np.arange(RET_HEADS, dtype=np.float64)
    log_g = np.log1p(-(2.0 ** (-5.0 - h)))
    idx = np.arange(BLK, dtype=np.float64)
    diff = idx[:, None] - idx[None, :]
    dmask = np.where(diff[None] >= 0, np.exp(np.maximum(diff, 0.0)[None] * log_g[:, None, None]), 0.0)
    zeta = np.exp((BLK - 1 - idx)[None, :] * log_g[:, None])
    xi = np.exp((idx + 1.0)[None, :] * log_g[:, None])
    dec = np.exp(BLK * log_g)
    rep = lambda v: np.broadcast_to(v[:, :, None], (RET_HEADS, BLK, 128))
    return (jnp.asarray(dmask, F32), jnp.asarray(rep(zeta), F32), jnp.asarray(rep(xi), F32),
            jnp.asarray(np.broadcast_to(dec[:, None, None], (RET_HEADS, 8, 256)), F32))


def _rms_fwd(x, g, to_cast=(), exchanges=()):
    S = x.shape[0]
    steps = 4
    tm = S // steps
    n_c = len(to_cast)

    def body(x_ref, g_ref, *refs):
        c_in, (h_ref, ht_ref), c_out = refs[:n_c], refs[n_c:n_c + 2], refs[n_c + 2:]
        for rows in _row_pieces(tm, 512):
            xv = x_ref[rows, :]
            r = lax.rsqrt(jnp.mean(xv * xv, axis=-1, keepdims=True) + NORM_EPS)
            h = xv * r * g_ref[...]
            h_ref[rows, :] = h.astype(BF16)
            ht_ref[:, rows] = h.T.astype(BF16)
        for a_ref, o_ref in zip(c_in, c_out):
            o_ref[...] = a_ref[...].astype(BF16)

    slab = lambda a: pl.BlockSpec((a.shape[0] // steps, a.shape[1]), lambda i: (i, 0))
    return _carrier_call(
        body, (x, g, *to_cast),
        out_shape=(SDS((S, D_MODEL), BF16), SDS((D_MODEL, S), BF16), *[SDS(a.shape, BF16) for a in to_cast]),
        grid=(steps,),
        in_specs=[pl.BlockSpec((tm, D_MODEL), lambda i: (i, 0)), pl.BlockSpec((1, D_MODEL), lambda i: (0, 0))]
        + [slab(a) for a in to_cast],
        out_specs=(pl.BlockSpec((tm, D_MODEL), lambda i: (i, 0)), pl.BlockSpec((D_MODEL, tm), lambda i: (0, i)),
                   *[slab(a) for a in to_cast]),
        sem=("parallel",), name="rms_fwd", exchanges=exchanges)


def _in_proj(h, w_in, tab, exchanges=()):
    S = h.shape[0]
    tm = min(S, 4096)

    def body(h_ref, w_ref, t_ref, o_ref):
        j = pl.program_id(1)
        is_rope = j < 6
        is_theta = (j == QR_B) | (j == KR_B)
        sub = 512

        def rotated(shift):
            for i in range(tm // sub):
                rows = slice(i * sub, (i + 1) * sub)
                acc = _dot(h_ref[rows, :], w_ref[...])
                c, sa, sb = t_ref[0, 0, rows, :], t_ref[0, 1, rows, :], t_ref[0, 2, rows, :]
                for k in range(COLB // 128):
                    sl = slice(k * 128, (k + 1) * 128)
                    o_ref[rows, sl] = _rot(acc[:, sl], c, sa, sb, shift).astype(BF16)

        @pl.when(is_rope)
        def _():
            rotated(32)

        @pl.when(is_theta)
        def _():
            rotated(1)

        @pl.when(jnp.logical_not(is_rope | is_theta))
        def _():
            o_ref[...] = _dot(h_ref[...], w_ref[...]).astype(BF16)

    def tab_map(i, j):
        return (jnp.where(j == QR_B, 1, jnp.where(j == KR_B, 2, 0)), 0, i, 0)

    (proj,), xres = _carrier_call(
        body, (h, w_in, tab), out_shape=(SDS((S, PROJ_W), BF16),), grid=(S // tm, N_COLB),
        in_specs=[pl.BlockSpec((tm, D_MODEL), lambda i, j: (i, 0)),
                  pl.BlockSpec((D_MODEL, COLB), lambda i, j: (0, j)),
                  pl.BlockSpec((1, 3, tm, 128), tab_map)],
        out_specs=(pl.BlockSpec((tm, COLB), lambda i, j: (i, j)),),
        sem=("parallel", "arbitrary"), name="in_proj", exchanges=exchanges)
    return proj, xres


def _band_mask(n):
    qi = lax.broadcasted_iota(jnp.int32, (BLK, 2 * BLK), 0)
    kj = lax.broadcasted_iota(jnp.int32, (BLK, 2 * BLK), 1)
    dist = BLK + qi - kj
    return (dist >= 0) & (dist <= BLK) & ((kj >= BLK) | (n > 0))


def _qkv_col(d, gi):
    if d == 1:
        return lambda t, r: 3 * t + gi
    return lambda t, r: 3 * r + t


def _attn_fwd(qkv, d, gi, exchanges=()):
    L = qkv.shape[0]
    nb = L // BLK
    qb = next(n for n in (4, 2, 1) if nb % n == 0)

    def body(q_ref, kc_ref, kp_ref, vc_ref, vp_ref, o_ref, lse_ref):
        i = pl.program_id(1)
        lane = lax.broadcasted_iota(jnp.int32, (BLK, 128), 1)
        lo = lane < 64
        chunks = [slice(c * 128, (c + 1) * 128) for c in range(4)]
        scores, vals, masks = [], [], []
        for b in range(qb):
            rows = slice(b * BLK, (b + 1) * BLK)
            before = slice((b - 1) * BLK, b * BLK)
            mask = _band_mask(qb * i + b)
            masks.append(jnp.concatenate([mask, mask], axis=0))
            for sl in chunks:
                q = q_ref[rows, sl]
                k = jnp.concatenate([kp_ref[:, sl] if b == 0 else kc_ref[before, sl], kc_ref[rows, sl]], axis=0)
                vals.append(jnp.concatenate([vp_ref[:, sl] if b == 0 else vc_ref[before, sl], vc_ref[rows, sl]], axis=0))
                q2 = jnp.concatenate([jnp.where(lo, q, jnp.zeros_like(q)), jnp.where(lo, jnp.zeros_like(q), q)], axis=0)
                scores.append(_dot_nt(q2, k))
        probs, lses = [], []
        for j, s in enumerate(scores):
            b, c = divmod(j, 4)
            s = jnp.where(masks[b], s * 0.125, jnp.float32(-1e30))
            m = jnp.max(s, axis=-1, keepdims=True)
            p = jnp.exp(s - m)
            l = jnp.sum(p, axis=-1, keepdims=True)
            probs.append((p * (1.0 / l)).astype(BF16))
            lses.append(m + jnp.log(l))
        for j, (p, v) in enumerate(zip(probs, vals)):
            b, c = divmod(j, 4)
            o2 = _dot(p, v)
            o_ref[b * BLK:(b + 1) * BLK, chunks[c]] = jnp.where(lo, o2[:BLK], o2[BLK:])
        for b in range(qb):
            lse_all = jnp.zeros((BLK, 128), F32)
            for c in range(4):
                lse = lses[4 * b + c]
                lse_all = jnp.where(lane // 16 == 2 * c, lse[:BLK], jnp.where(lane // 16 == 2 * c + 1, lse[BLK:], lse_all))
            lse_ref[b * BLK:(b + 1) * BLK, :] = lse_all

    prev = lambda i: jnp.maximum(qb * i - 1, 0)
    col = _qkv_col(d, gi)
    return _carrier_call(
        body, (qkv,) * 5, out_shape=(SDS((L, d * 512), F32), SDS((L, d * 128), F32)), grid=(d, nb // qb),
        in_specs=[pl.BlockSpec((qb * BLK, 512), lambda r, i: (i, col(0, r))),
                  pl.BlockSpec((qb * BLK, 512), lambda r, i: (i, col(1, r))),
                  pl.BlockSpec((BLK, 512), lambda r, i: (prev(i), col(1, r))),
                  pl.BlockSpec((qb * BLK, 512), lambda r, i: (i, col(2, r))),
                  pl.BlockSpec((BLK, 512), lambda r, i: (prev(i), col(2, r)))],
        out_specs=(pl.BlockSpec((qb * BLK, 512), lambda r, i: (i, r)),
                   pl.BlockSpec((qb * BLK, 128), lambda r, i: (i, r))),
        sem=("parallel", "arbitrary"), name=f"attn_fwd_g{gi}", exchanges=exchanges)


def _qkv_to_sub(proj, d, gi):
    S = proj.shape[0]
    tm = 512
    n = tm // d

    def body(q_ref, k_ref, v_ref, o_ref, scr):
        for t, ref in enumerate((q_ref, k_ref, v_ref)):
            for c in range(4):
                scr[c] = ref[:, c * 128:(c + 1) * 128].astype(F32)
            for r in range(d):
                for c in range(4):
                    col = (3 * r + t) * 512 + c * 128
                    o_ref[:, col:col + 128] = scr[c, pl.ds(r, n, stride=d), :].astype(BF16)

    return pl.pallas_call(
        body, out_shape=SDS((S // d, d * 1536), BF16), grid=(S // tm,),
        in_specs=[pl.BlockSpec((tm, 512), lambda i, b=b: (i, b + gi)) for b in (QA_B, KA_B, VA_B)],
        out_specs=pl.BlockSpec((n, d * 1536), lambda i: (i, 0)),
        scratch_shapes=[pltpu.VMEM((4, tm, 128), F32)],
        compiler_params=_cparams("parallel"), name=f"qkv_to_sub_g{gi}")(proj, proj, proj)


def _attn_merge(os_, lses):
    S = os_[0].shape[0]
    tm = 512

    def body(o0, o1, o2, l0, l1, l2, att_ref, lt_ref, lt1_ref, lt2_ref, so1, so2, sl1, sl2):
        lo = lax.broadcasted_iota(jnp.int32, (tm, 128), 1) < 64

        def natural(ref, d, scr, width):
            nch = width // 128
            if d == 1:
                return [ref[:, c * 128:(c + 1) * 128] for c in range(nch)]
            for r in range(d):
                for c in range(nch):
                    scr[c, pl.ds(r, tm // d, stride=d), :] = ref[:, r * width + c * 128:r * width + (c + 1) * 128]
            return [scr[c] for c in range(nch)]

        ls = [natural(l, d, s, 128)[0] for l, d, s in zip((l0, l1, l2), DILATIONS, (None, sl1, sl2))]
        m = jnp.maximum(jnp.maximum(ls[0], ls[1]), ls[2])
        es = [jnp.exp(v - m) for v in ls]
        z = es[0] + es[1] + es[2]
        lt = m + jnp.log(z)
        lt_ref[...] = lt
        sl1[0] = lt
        for ref, d in ((lt1_ref, DILATIONS[1]), (lt2_ref, DILATIONS[2])):
            for r in range(d):
                ref[:, r * 128:(r + 1) * 128] = sl1[0, pl.ds(r, tm // d, stride=d), :]
        ws = [e / z for e in es]
        o_nat = [natural(o, d, s, 512) for o, d, s in zip((o0, o1, o2), DILATIONS, (None, so1, so2))]
        for c in range(4):
            acc = jnp.zeros((tm, 128), F32)
            for g in range(3):
                w_lo = jnp.broadcast_to(ws[g][:, 32 * c:32 * c + 1], (tm, 128))
                w_hi = jnp.broadcast_to(ws[g][:, 32 * c + 16:32 * c + 17], (tm, 128))
                acc = acc + jnp.where(lo, w_lo, w_hi) * o_nat[g][c]
            att_ref[:, c * 128:(c + 1) * 128] = acc.astype(BF16)

    sub = lambda w: [pl.BlockSpec((tm // d, d * w), lambda i: (i, 0)) for d in DILATIONS]
    att, *lts = pl.pallas_call(
        body, out_shape=(SDS((S, 512), BF16), *[SDS((S // d, d * 128), F32) for d in DILATIONS]), grid=(S // tm,),
        in_specs=sub(512) + sub(128),
        out_specs=(pl.BlockSpec((tm, 512), lambda i: (i, 0)), *sub(128)),
        scratch_shapes=[pltpu.VMEM((4, tm, 128), F32), pltpu.VMEM((4, tm, 128), F32),
                        pltpu.VMEM((1, tm, 128), F32), pltpu.VMEM((1, tm, 128), F32)],
        compiler_params=_cparams("parallel"), name="attn_merge")(*os_, *lses)
    return att, lts


def _assemble_dproj(att_grads, dproj):
    S = dproj.shape[0]
    tm = 256

    def body(*refs):
        a = [refs[3 * t:3 * t + 3] for t in range(3)]
        dp_prev, o_ref, scr = refs[9:]
        for t in range(3):
            for g, d in enumerate(DILATIONS):
                base = (3 * t + g) * COLB
                if d == 1:
                    o_ref[:, base:base + COLB] = a[t][g][...]
                    continue
                for c in range(4):
                    for r in range(d):
                        scr[c, pl.ds(r, tm // d, stride=d), :] = a[t][g][:, r * 512 + c * 128:r * 512 + (c + 1) * 128].astype(F32)
                    o_ref[:, base + c * 128:base + (c + 1) * 128] = scr[c].astype(BF16)

    sub = [pl.BlockSpec((tm // d, d * 512), lambda i: (i, 0)) for d in DILATIONS]
    flat = [att_grads[t][g] for t in range(3) for g in range(3)]
    return pl.pallas_call(
        body, out_shape=SDS((S, PROJ_W), BF16), grid=(S // tm,),
        in_specs=sub * 3 + [ANY], out_specs=pl.BlockSpec((tm, 9 * COLB), lambda i: (i, 0)),
        scratch_shapes=[pltpu.VMEM((4, tm, 128), F32)], input_output_aliases={9: 0},
        compiler_params=_cparams("parallel"), name="assemble_dproj")(*flat, dproj)


def _ret_fwd(proj, consts, exchanges=()):
    S = proj.shape[0]
    nc = S // BLK
    dmask, zeta, xi, dec = consts

    def body(q_ref, k_ref, v0_ref, v1_ref, g0_ref, g1_ref, dm_ref, z_ref, x_ref, dec_ref,
             y_ref, rn_ref, rs_ref, st_ref, R):
        @pl.when(pl.program_id(0) == 0)
        def _():
            R[...] = jnp.zeros_like(R)

        lane16 = lax.broadcasted_iota(jnp.int32, (BLK, 128), 1) // 16
        rs_all = jnp.zeros((BLK, 128), F32)
        first = []
        for h in range(RET_HEADS):
            hs = slice(h * 128, (h + 1) * 128)
            q, k = q_ref[:, hs], k_ref[:, hs]
            v = (v0_ref if h < 2 else v1_ref)[:, (h % 2) * 256:(h % 2 + 1) * 256]
            Rb = R[h].astype(BF16)
            st_ref[h] = Rb
            kz = (k.astype(F32) * z_ref[h]).astype(BF16)
            first.append((v, _dot_nt(q, k), _dot((q.astype(F32) * x_ref[h]).astype(BF16), Rb), _dot_tn(kz, v)))
        masked = [(s * dm_ref[h]).astype(BF16) for h, (_, s, _, _) in enumerate(first)]
        for h in range(RET_HEADS):
            vs = slice((h % 2) * 256, (h % 2 + 1) * 256)
            os_ = slice(h * 256, (h + 1) * 256)
            v, _, cross, kv = first[h]
            o = _dot(masked[h], v) + cross
            R[h] = R[h] * dec_ref[h, 0:1, :] + kv
            mu = jnp.mean(o, axis=-1, keepdims=True)
            oc = o - mu
            rstd = lax.rsqrt(jnp.mean(oc * oc, axis=-1, keepdims=True) + NORM_EPS)
            rn = oc * rstd
            gr = (g0_ref if h < 2 else g1_ref)[:, vs].astype(F32)
            y_ref[:, os_] = (rn * gr * _sigmoid(gr)).astype(BF16)
            rn_ref[:, os_] = rn.astype(BF16)
            rs_all = jnp.where(lane16 == h, rstd, rs_all)
        rs_ref[...] = rs_all

    cst = lambda shape: pl.BlockSpec(shape, lambda c: (0, 0, 0))
    blk = lambda j: pl.BlockSpec((BLK, 512), lambda c: (c, j))
    return _carrier_call(
        body, (proj, proj, proj, proj, proj, proj, dmask, zeta, xi, dec),
        out_shape=(SDS((S, 1024), BF16), SDS((S, 1024), BF16), SDS((S, 128), F32), SDS((RET_HEADS, nc, BLK, 256), BF16)),
        grid=(nc,),
        in_specs=[blk(QR_B), blk(KR_B), blk(11), blk(12), blk(13), blk(14),
                  cst((RET_HEADS, BLK, BLK)), cst((RET_HEADS, BLK, 128)), cst((RET_HEADS, BLK, 128)), cst((RET_HEADS, 8, 256))],
        out_specs=(pl.BlockSpec((BLK, 1024), lambda c: (c, 0)), pl.BlockSpec((BLK, 1024), lambda c: (c, 0)),
                   pl.BlockSpec((BLK, 128), lambda c: (c, 0)),
                   pl.BlockSpec((RET_HEADS, None, BLK, 256), lambda c: (0, c, 0, 0))),
        scratch_shapes=[pltpu.VMEM((RET_HEADS, BLK, 256), F32)],
        sem=("arbitrary",), name="ret_fwd", exchanges=exchanges)


def _mix_out(att, yrin, proj, wa, wr, wo, x, g2, exchanges=()):
    S = x.shape[0]
    tm = 512
    gate0 = 15 * COLB

    def body(a_ref, y_ref, ga_ref, gr_ref, wa_ref, wr_ref, wo_ref, x_ref, g_ref, m_ref, ya_ref, yr_ref, x1_ref, h2_ref):
        pieces = _row_pieces(tm, 256)
        branches = [(_dot(a_ref[rows, :], wa_ref[...]), _dot(y_ref[rows, :], wr_ref[...])) for rows in pieces]
        merged = []
        for rows, (ya, yr) in zip(pieces, branches):
            m = (_sigmoid(ga_ref[rows, :].astype(F32)) * ya + _sigmoid(gr_ref[rows, :].astype(F32)) * yr).astype(BF16)
            m_ref[rows, :] = m
            ya_ref[rows, :] = ya.astype(BF16)
            yr_ref[rows, :] = yr.astype(BF16)
            merged.append(m)
        for rows, m in zip(pieces, merged):
            x1 = x_ref[rows, :] + _dot(m, wo_ref[...])
            x1_ref[rows, :] = x1
            r = lax.rsqrt(jnp.mean(x1 * x1, axis=-1, keepdims=True) + NORM_EPS)
            h2_ref[rows, :] = (x1 * r * g_ref[...]).astype(BF16)

    row = lambda w: pl.BlockSpec((tm, w), lambda i: (i, 0))
    cols = lambda c0: pl.BlockSpec((pl.Element(tm), pl.Element(D_MODEL)), lambda i: (i * tm, c0))
    resident = lambda r, c: pl.BlockSpec((r, c), lambda i: (0, 0), pipeline_mode=pl.Buffered(1))
    return _carrier_call(
        body, (att, yrin, proj, proj, wa, wr, wo, x, g2),
        out_shape=(SDS((S, D_MODEL), BF16),) * 3 + (SDS((S, D_MODEL), F32), SDS((S, D_MODEL), BF16)), grid=(S // tm,),
        in_specs=[row(512), row(D_MODEL), cols(gate0), cols(gate0 + D_MODEL), resident(512, D_MODEL),
                  resident(D_MODEL, D_MODEL), resident(D_MODEL, D_MODEL), row(D_MODEL),
                  pl.BlockSpec((1, D_MODEL), lambda i: (0, 0))],
        out_specs=(row(D_MODEL),) * 5, sem=("parallel",), name="mix_out", exchanges=exchanges)


def _ffn_up(h2, wg, wu, exchanges=()):
    S = h2.shape[0]
    tm = min(S, 2048)

    def body(h_ref, wg_ref, wu_ref, g_ref, u_ref, a_ref):
        for rows in _row_pieces(tm):
            hv = h_ref[rows, :]
            g = _dot_nt(hv, wg_ref[...])
            u = _dot_nt(hv, wu_ref[...])
            g_ref[rows, :] = g.astype(BF16)
            u_ref[rows, :] = u.astype(BF16)
            a_ref[rows, :] = (g * _sigmoid(g) * u).astype(BF16)

    wspec = pl.BlockSpec((None, HID_S, D_MODEL), lambda i, s: (s, 0, 0))
    ospec = pl.BlockSpec((None, tm, HID_S), lambda i, s: (s, i, 0))
    return _carrier_call(
        body, (h2, wg, wu), out_shape=(SDS((N_SHARD, S, HID_S), BF16),) * 3, grid=(S // tm, N_SHARD),
        in_specs=[pl.BlockSpec((tm, D_MODEL), lambda i, s: (i, 0)), wspec, wspec],
        out_specs=(ospec, ospec, ospec),
        sem=("parallel", "arbitrary"), name="ffn_up", exchanges=exchanges)


def _ffn_down_loss(act, wd, x1, g3, tgt):
    S = x1.shape[0]
    tm = 512

    def body(a_ref, w_ref, x_ref, g_ref, t_ref, dx_ref, dxb_ref, dg_ref, ls_ref):
        @pl.when(pl.program_id(0) == 0)
        def _():
            dg_ref[...] = jnp.zeros_like(dg_ref)
            ls_ref[...] = jnp.zeros_like(ls_ref)

        g = g_ref[...]
        for rows in _row_pieces(tm, 256):
            y = _dot(a_ref[0, rows, :], w_ref[0])
            for s in range(1, N_SHARD):
                y = y + _dot(a_ref[s, rows, :], w_ref[s])
            x2 = x_ref[rows, :] + y
            r = lax.rsqrt(jnp.mean(x2 * x2, axis=-1, keepdims=True) + NORM_EPS)
            xh = x2 * r
            err = xh * g - t_ref[rows, :]
            ls_ref[...] += jnp.sum(jnp.sum(err * err, axis=-1, keepdims=True), axis=0, keepdims=True) * (0.5 / D_MODEL)
            dy = err * (1.0 / D_MODEL)
            dg_ref[...] += jnp.sum(dy * xh, axis=0, keepdims=True)
            dxh = dy * g
            dx = r * (dxh - xh * jnp.mean(dxh * xh, axis=-1, keepdims=True))
            dx_ref[rows, :] = dx
            dxb_ref[rows, :] = dx.astype(BF16)

    row = pl.BlockSpec((tm, D_MODEL), lambda i: (i, 0))
    vec = pl.BlockSpec((1, D_MODEL), lambda i: (0, 0))
    return pl.pallas_call(
        body, out_shape=(SDS((S, D_MODEL), F32), SDS((S, D_MODEL), BF16), SDS((1, D_MODEL), F32), SDS((8, 128), F32)),
        grid=(S // tm,),
        in_specs=[pl.BlockSpec((N_SHARD, tm, HID_S), lambda i: (0, i, 0)),
                  pl.BlockSpec((N_SHARD, HID_S, D_MODEL), lambda i: (0, 0, 0), pipeline_mode=pl.Buffered(1)),
                  row, vec, row],
        out_specs=(row, row, vec, pl.BlockSpec((8, 128), lambda i: (0, 0))),
        compiler_params=_cparams("arbitrary"), name="ffn_down_loss")(act, wd, x1, g3, tgt)


def _ffn_bwd(dx2b, dx2, wd, wg, wu, gte, up, x1, g2):
    S = x1.shape[0]
    tm = 256

    def body(d_ref, dx2_ref, wd_ref, wg_ref, wu_ref, g_ref, u_ref, x_ref, gn_ref,
             dg_ref, du_ref, dx_ref, dxb_ref, dgn_ref):
        @pl.when(pl.program_id(0) == 0)
        def _():
            dgn_ref[...] = jnp.zeros_like(dgn_ref)

        d = d_ref[...]
        dacts = [_dot_nt(d, wd_ref[s]) for s in range(N_SHARD)]
        dgs, dus = [], []
        for s, da in enumerate(dacts):
            g = g_ref[s].astype(F32)
            sg = _sigmoid(g)
            dgs.append((da * u_ref[s].astype(F32) * sg * (1.0 + g * (1.0 - sg))).astype(BF16))
            dus.append((da * g * sg).astype(BF16))
            dg_ref[s] = dgs[s]
            du_ref[s] = dus[s]
        dh = _dot(dgs[0], wg_ref[0]) + _dot(dus[0], wu_ref[0])
        for s in range(1, N_SHARD):
            dh = dh + _dot(dgs[s], wg_ref[s]) + _dot(dus[s], wu_ref[s])
        xv = x_ref[...]
        r = lax.rsqrt(jnp.mean(xv * xv, axis=-1, keepdims=True) + NORM_EPS)
        xh = xv * r
        dgn_ref[...] += jnp.sum(dh * xh, axis=0, keepdims=True)
        dxh = dh * gn_ref[...]
        dx = dx2_ref[...] + r * (dxh - xh * jnp.mean(dxh * xh, axis=-1, keepdims=True))
        dx_ref[...] = dx
        dxb_ref[...] = dx.astype(BF16)

    row = pl.BlockSpec((tm, D_MODEL), lambda i: (i, 0))
    vec = pl.BlockSpec((1, D_MODEL), lambda i: (0, 0))
    aspec = pl.BlockSpec((N_SHARD, tm, HID_S), lambda i: (0, i, 0))
    resident = lambda shape: pl.BlockSpec(shape, lambda i: (0, 0, 0), pipeline_mode=pl.Buffered(1))
    return pl.pallas_call(
        body,
        out_shape=(SDS((N_SHARD, S, HID_S), BF16), SDS((N_SHARD, S, HID_S), BF16),
                   SDS((S, D_MODEL), F32), SDS((S, D_MODEL), BF16), SDS((1, D_MODEL), F32)),
        grid=(S // tm,),
        in_specs=[row, row, resident((N_SHARD, HID_S, D_MODEL)), resident((N_SHARD, HID_S, D_MODEL)),
                  resident((N_SHARD, HID_S, D_MODEL)), aspec, aspec, row, vec],
        out_specs=(aspec, aspec, row, row, vec),
        compiler_params=_cparams("arbitrary"), name="ffn_bwd")(dx2b, dx2, wd, wg, wu, gte, up, x1, g2)


def _wgrad(name, a, b, a_spec, b_spec, out_shape, out_spec, n_par, S):
    tk = min(S, 4096)

    def body(a_ref, b_ref, o_ref):
        @pl.when(pl.program_id(1) == 0)
        def _():
            o_ref[...] = jnp.zeros_like(o_ref)

        o_ref[...] += _dot_tn(a_ref[...], b_ref[...])

    return pl.pallas_call(
        body, out_shape=SDS(out_shape, F32), grid=(n_par, S // tk),
        in_specs=[a_spec(tk), b_spec(tk)], out_specs=out_spec,
        compiler_params=_cparams("parallel", "arbitrary"), name=name)(a, b)


def _mix_bwd(dx1b, wo, proj, ya, yr, wa, wr, att, exchanges=()):
    S = dx1b.shape[0]
    tm = 512
    gate0 = 15 * COLB

    def body(d_ref, wo_ref, ga_ref, gr_ref, ya_ref, yr_ref, wa_ref, wr_ref, att_ref,
             dya_ref, dyr_ref, dp_ref, dyi_ref, datt_ref, datt1_ref, datt2_ref, rho_ref, rho1_ref, rho2_ref,
             datt_scr, rho_scr):
        pieces = _row_pieces(tm, 256)
        dms = [_dot_nt(d_ref[rows, :], wo_ref[...]) for rows in pieces]
        branch = []
        for rows, dm in zip(pieces, dms):
            sa = _sigmoid(ga_ref[rows, :].astype(F32))
            sr = _sigmoid(gr_ref[rows, :].astype(F32))
            dya, dyr = (dm * sa).astype(BF16), (dm * sr).astype(BF16)
            dya_ref[rows, :] = dya
            dyr_ref[rows, :] = dyr
            dp_ref[rows, 0:D_MODEL] = (dm * ya_ref[rows, :].astype(F32) * sa * (1.0 - sa)).astype(BF16)
            dp_ref[rows, D_MODEL:2 * D_MODEL] = (dm * yr_ref[rows, :].astype(F32) * sr * (1.0 - sr)).astype(BF16)
            branch.append((dya, dyr))
        lane = lax.broadcasted_iota(jnp.int32, (256, 128), 1)
        lo = lane < 64
        for rows, (dya, dyr) in zip(pieces, branch):
            datt = _dot_nt(dya, wa_ref[...])
            datt_ref[rows, :] = datt.astype(BF16)
            dyi_ref[rows, :] = _dot_nt(dyr, wr_ref[...]).astype(BF16)
            prod = datt * att_ref[rows, :].astype(F32)
            rho = jnp.zeros((256, 128), F32)
            for c in range(4):
                pc = prod[:, c * 128:(c + 1) * 128]
                tot = jnp.sum(pc, axis=-1, keepdims=True)
                low = jnp.sum(jnp.where(lo, pc, 0.0), axis=-1, keepdims=True)
                rho = jnp.where(lane // 16 == 2 * c, low, jnp.where(lane // 16 == 2 * c + 1, tot - low, rho))
            rho_ref[rows, :] = rho
            rho_scr[0] = rho
            for c in range(4):
                datt_scr[c] = datt[:, c * 128:(c + 1) * 128]
            for d, dv_ref, rv_ref in ((DILATIONS[1], datt1_ref, rho1_ref), (DILATIONS[2], datt2_ref, rho2_ref)):
                n = 256 // d
                sub_rows = slice(rows.start // d, rows.start // d + n)
                for r in range(d):
                    rv_ref[sub_rows, r * 128:(r + 1) * 128] = rho_scr[0, pl.ds(r, n, stride=d), :]
                    for c in range(4):
                        col = r * 512 + c * 128
                        dv_ref[sub_rows, col:col + 128] = datt_scr[c, pl.ds(r, n, stride=d), :].astype(BF16)

    row = lambda w: pl.BlockSpec((tm, w), lambda i: (i, 0))
    sub = lambda w: [pl.BlockSpec((tm // d, d * w), lambda i: (i, 0)) for d in DILATIONS]
    cols = lambda c0, w: pl.BlockSpec((pl.Element(tm), pl.Element(w)), lambda i: (i * tm, c0))
    resident = lambda r, c: pl.BlockSpec((r, c), lambda i: (0, 0), pipeline_mode=pl.Buffered(1))
    (dya, dyr, dproj, dyrin, *views), xres = _carrier_call(
        body, (dx1b, wo, proj, proj, ya, yr, wa, wr, att),
        out_shape=(SDS((S, D_MODEL), BF16), SDS((S, D_MODEL), BF16), SDS((S, PROJ_W), BF16), SDS((S, D_MODEL), BF16),
                   *[SDS((S // d, d * 512), BF16) for d in DILATIONS], *[SDS((S // d, d * 128), F32) for d in DILATIONS]),
        grid=(S // tm,),
        in_specs=[row(D_MODEL), resident(D_MODEL, D_MODEL), cols(gate0, D_MODEL), cols(gate0 + D_MODEL, D_MODEL),
                  row(D_MODEL), row(D_MODEL), resident(512, D_MODEL), resident(D_MODEL, D_MODEL), row(512)],
        out_specs=(row(D_MODEL), row(D_MODEL), cols(gate0, 2 * D_MODEL), row(D_MODEL), *sub(512), *sub(128)),
        scratch_shapes=[pltpu.VMEM((4, 256, 128), F32), pltpu.VMEM((1, 256, 128), F32)],
        sem=("parallel",), name="mix_bwd", exchanges=exchanges)
    return (dya, dyr, dproj, dyrin, views[:3], views[3:]), xres


def _attn_bwd(qkv, datt, lse, rho, rtab, d, gi, exchanges=()):
    L = qkv.shape[0]
    nb = L // BLK
    T = d * nb

    def body(q_ref, kc_ref, kp_ref, vc_ref, vp_ref, do_ref, lse_ref, rho_ref, tq_ref, tk_ref,
             dq_ref, dk_ref, dv_ref, ck, cv):
        t = pl.program_id(0)
        n = jnp.minimum(t, T - 1) % nb

        @pl.when(t == 0)
        def _():
            ck[...] = jnp.zeros_like(ck)
            cv[...] = jnp.zeros_like(cv)

        def store_rot(ref, val, t_ref, c):
            sl = slice(c * 128, (c + 1) * 128)
            ref[:, sl] = _unrot(val, t_ref[0], t_ref[1], t_ref[2], 32).astype(BF16)

        @pl.when(t < T)
        def _():
            mask = _band_mask(n)
            mask2 = jnp.concatenate([mask, mask], axis=0)
            lo = lax.broadcasted_iota(jnp.int32, (BLK, 128), 1) < 64

            def stacked(a):
                return jnp.concatenate([jnp.where(lo, a, jnp.zeros_like(a)), jnp.where(lo, jnp.zeros_like(a), a)], axis=0)

            def head_cols(ref, c):
                return jnp.concatenate([jnp.broadcast_to(ref[:, 32 * c:32 * c + 1], (BLK, 2 * BLK)),
                                        jnp.broadcast_to(ref[:, 32 * c + 16:32 * c + 17], (BLK, 2 * BLK))], axis=0)

            ops, raw = [], []
            for c in range(4):
                sl = slice(c * 128, (c + 1) * 128)
                q2, do2 = stacked(q_ref[:, sl]), stacked(do_ref[:, sl])
                k = jnp.concatenate([kp_ref[:, sl], kc_ref[:, sl]], axis=0)
                v = jnp.concatenate([vp_ref[:, sl], vc_ref[:, sl]], axis=0)
                ops.append((q2, do2, k))
                raw.append((_dot_nt(q2, k), _dot_nt(do2, v)))
            grads = []
            for c, (s, dp) in enumerate(raw):
                p = jnp.where(mask2, jnp.exp(s * 0.125 - head_cols(lse_ref, c)), 0.0)
                grads.append(((p * (dp - head_cols(rho_ref, c)) * 0.125).astype(BF16), p.astype(BF16)))
            for c, ((q2, do2, k), (ds, pb)) in enumerate(zip(ops, grads)):
                sl = slice(c * 128, (c + 1) * 128)
                dq2 = _dot(ds, k)
                dq_c = jnp.where(lo, dq2[:BLK], dq2[BLK:])
                dk_c = _dot_tn(ds, q2)
                dv_c = _dot_tn(pb, do2)
                store_rot(dq_ref, dq_c, tq_ref, c)
                store_rot(dk_ref, ck[:, sl] + dk_c[:BLK], tk_ref, c)
                dv_ref[:, sl] = (cv[:, sl] + dv_c[:BLK]).astype(BF16)
                ck[:, sl] = dk_c[BLK:]
                cv[:, sl] = dv_c[BLK:]

        @pl.when(t == T)
        def _():
            for c in range(4):
                sl = slice(c * 128, (c + 1) * 128)
                store_rot(dk_ref, ck[:, sl], tk_ref, c)
            dv_ref[...] = cv[...].astype(BF16)

    blk_of = lambda t: (jnp.minimum(t, T - 1) % nb, jnp.minimum(t, T - 1) // nb)
    cur = lambda t: blk_of(t)
    prev = lambda t: (jnp.maximum(blk_of(t)[0] - 1, 0), blk_of(t)[1])
    fin = lambda t: blk_of(jnp.maximum(t - 1, 0))
    col = _qkv_col(d, gi)
    qkv_spec = lambda kind, which: pl.BlockSpec((BLK, 512), lambda t: (which(t)[0], col(kind, which(t)[1])))
    row_spec = lambda w, which: pl.BlockSpec((BLK, w), lambda t: which(t))
    tab_spec = lambda which: pl.BlockSpec((3, BLK, 128), lambda t: (0, *which(t)))
    return _carrier_call(
        body, (qkv, qkv, qkv, qkv, qkv, datt, lse, rho, rtab, rtab),
        out_shape=(SDS((L, d * 512), BF16),) * 3, grid=(T + 1,),
        in_specs=[qkv_spec(0, cur), qkv_spec(1, cur), qkv_spec(1, prev), qkv_spec(2, cur), qkv_spec(2, prev),
                  row_spec(512, cur), row_spec(128, cur), row_spec(128, cur), tab_spec(cur), tab_spec(fin)],
        out_specs=(row_spec(512, cur), row_spec(512, fin), row_spec(512, fin)),
        scratch_shapes=[pltpu.VMEM((BLK, 512), F32), pltpu.VMEM((BLK, 512), F32)],
        sem=("arbitrary",), name=f"attn_bwd_g{gi}", exchanges=exchanges)


def _ret_bwd(proj, rn, rstd, dyrin, states, tab, consts, dproj, exchanges=()):
    S = proj.shape[0]
    nc = S // BLK
    dmask, zeta, xi, dec = consts

    def body(q_ref, k_ref, v0_ref, v1_ref, g0_ref, g1_ref, rn_ref, rs_ref, dy_ref, st_ref, tq_ref, tk_ref,
             dm_ref, z_ref, x_ref, dec_ref, dp_prev, dp_ref, dR):
        dq_ref, dk_ref = dp_ref.at[:, 0:512], dp_ref.at[:, 512:1024]
        dv_ref, dgr_ref = dp_ref.at[:, 1024:2048], dp_ref.at[:, 2048:3072]

        @pl.when(pl.program_id(0) == 0)
        def _():
            dR[...] = jnp.zeros_like(dR)

        dobs = []
        for h in range(RET_HEADS):
            vs = slice((h % 2) * 256, (h % 2 + 1) * 256)
            os_ = slice(h * 256, (h + 1) * 256)
            gr = (g0_ref if h < 2 else g1_ref)[:, vs].astype(F32)
            sg = _sigmoid(gr)
            rn_v = rn_ref[:, os_].astype(F32)
            dyi = dy_ref[:, os_].astype(F32)
            dgr_ref[:, os_] = (dyi * rn_v * sg * (1.0 + gr * (1.0 - sg))).astype(BF16)
            drn = dyi * gr * sg
            rstd = jnp.broadcast_to(rs_ref[:, 16 * h:16 * h + 1], (BLK, 256))
            do = rstd * (drn - jnp.mean(drn, axis=-1, keepdims=True) - rn_v * jnp.mean(drn * rn_v, axis=-1, keepdims=True))
            dobs.append(do.astype(BF16))
        first = []
        for h in range(RET_HEADS):
            hs = slice(h * 128, (h + 1) * 128)
            q, k = q_ref[:, hs], k_ref[:, hs]
            v = (v0_ref if h < 2 else v1_ref)[:, (h % 2) * 256:(h % 2 + 1) * 256]
            dob, dRb = dobs[h], dR[h].astype(BF16)
            kz = (k.astype(F32) * z_ref[h]).astype(BF16)
            qx = (q.astype(F32) * x_ref[h]).astype(BF16)
            first.append((q, k, _dot_nt(q, k), _dot_nt(dob, v), _dot(kz, dRb), _dot_nt(dob, st_ref[h]),
                          _dot_nt(v, dRb), _dot_tn(qx, dob)))
        masked = [((s * dm_ref[h]).astype(BF16), (dsr * dm_ref[h]).astype(BF16))
                  for h, (_, _, s, dsr, _, _, _, _) in enumerate(first)]
        for h in range(RET_HEADS):
            hs = slice(h * 128, (h + 1) * 128)
            os_ = slice(h * 256, (h + 1) * 256)
            q, k, _, _, dv_state, dq_state, dk_state, dr_new = first[h]
            sD, dS = masked[h]
            dv_ref[:, os_] = (_dot_tn(sD, dobs[h]) + dv_state).astype(BF16)
            dq = _dot(dS, k) + dq_state * x_ref[h]
            dk = _dot_tn(dS, q) + dk_state * z_ref[h]
            dR[h] = dR[h] * dec_ref[h, 0:1, :] + dr_new
            dq_ref[:, hs] = _unrot(dq, tq_ref[0], tq_ref[1], tq_ref[2], 1).astype(BF16)
            dk_ref[:, hs] = _unrot(dk, tk_ref[0], tk_ref[1], tk_ref[2], 1).astype(BF16)

    rc = lambda c: nc - 1 - c
    cst = lambda shape: pl.BlockSpec(shape, lambda c: (0, 0, 0))
    blk = lambda j: pl.BlockSpec((BLK, 512), lambda c: (rc(c), j))
    row = lambda w: pl.BlockSpec((BLK, w), lambda c: (rc(c), 0))
    (dproj,), xres = _carrier_call(
        body, (proj, proj, proj, proj, proj, proj, rn, rstd, dyrin, states, tab, tab, dmask, zeta, xi, dec, dproj),
        out_shape=(SDS((S, PROJ_W), BF16),), grid=(nc,),
        in_specs=[blk(QR_B), blk(KR_B), blk(11), blk(12), blk(13), blk(14), row(1024), row(128), row(1024),
                  pl.BlockSpec((RET_HEADS, None, BLK, 256), lambda c: (0, rc(c), 0, 0)),
                  pl.BlockSpec((None, 3, BLK, 128), lambda c: (1, 0, rc(c), 0)),
                  pl.BlockSpec((None, 3, BLK, 128), lambda c: (2, 0, rc(c), 0)),
                  cst((RET_HEADS, BLK, BLK)), cst((RET_HEADS, BLK, 128)), cst((RET_HEADS, BLK, 128)), cst((RET_HEADS, 8, 256)),
                  ANY],
        out_specs=(pl.BlockSpec((pl.Element(BLK), pl.Element(6 * COLB)), lambda c: (rc(c) * BLK, QR_B * COLB)),),
        scratch_shapes=[pltpu.VMEM((RET_HEADS, BLK, 256), F32)],
        sem=("arbitrary",), name="ret_bwd", exchanges=exchanges, in_out_aliases={16: 0})
    return dproj, xres


def _wgrad_in_half(ht, dproj, sidx, kept, exchanges=()):
    S = dproj.shape[0]
    tk = 2048
    half = (lambda sx: sx[4]) if kept else (lambda sx: 1 - sx[4])

    def body(a_ref, b_ref, o_ref):
        @pl.when(pl.program_id(1) == 0)
        def _():
            o_ref[...] = jnp.zeros_like(o_ref)

        o_ref[...] += _dot(a_ref[...], b_ref[...])

    (g,), xres = _carrier_call(
        body, (ht, dproj), out_shape=(SDS((D_MODEL // 2, PROJ_W), F32),), grid=(N_SHARD, S // tk),
        in_specs=[pl.BlockSpec((D_MODEL // 2, tk), lambda s, k, sx: (half(sx), k)),
                  pl.BlockSpec((tk, W_IN_S), lambda s, k, sx: (k, s))],
        out_specs=(pl.BlockSpec((D_MODEL // 2, W_IN_S), lambda s, k, sx: (0, s)),),
        sem=("parallel", "arbitrary"), name="wgrad_in_kept" if kept else "wgrad_in_sent", exchanges=exchanges,
        prefetch=sidx)
    return g, xres


def _in_proj_bwd(dproj, w_in, x, g1, dx1, exchanges=()):
    S = x.shape[0]
    tm = 1024

    def body(d_ref, w_ref, x_ref, g_ref, dx1_ref, dx_ref, dgn_ref, acc):
        i, s = pl.program_id(0), pl.program_id(1)

        @pl.when(s == 0)
        def _():
            acc[...] = jnp.zeros_like(acc)

        @pl.when((i == 0) & (s == 0))
        def _():
            dgn_ref[...] = jnp.zeros_like(dgn_ref)

        acc[...] += _dot_nt(d_ref[...], w_ref[...])

        @pl.when(s == N_SHARD - 1)
        def _():
            xv = x_ref[...]
            r = lax.rsqrt(jnp.mean(xv * xv, axis=-1, keepdims=True) + NORM_EPS)
            xh = xv * r
            dh = acc[...]
            dgn_ref[...] += jnp.sum(dh * xh, axis=0, keepdims=True)
            dxh = dh * g_ref[...]
            dx_ref[...] = dx1_ref[...] + r * (dxh - xh * jnp.mean(dxh * xh, axis=-1, keepdims=True))

    row = pl.BlockSpec((tm, D_MODEL), lambda i, s: (i, 0))
    vec = pl.BlockSpec((1, D_MODEL), lambda i, s: (0, 0))
    (gx, dg), xres = _carrier_call(
        body, (dproj, w_in, x, g1, dx1),
        out_shape=(SDS((S, D_MODEL), F32), SDS((1, D_MODEL), F32)), grid=(S // tm, N_SHARD),
        in_specs=[pl.BlockSpec((tm, W_IN_S), lambda i, s: (i, s)),
                  pl.BlockSpec((D_MODEL, W_IN_S), lambda i, s: (0, s)), row, vec, row],
        out_specs=(row, vec), scratch_shapes=[pltpu.VMEM((tm, D_MODEL), F32)],
        sem=("arbitrary", "arbitrary"), name="in_proj_bwd", exchanges=exchanges)
    return gx, dg, xres


def _step(x, tgt, g1, g2, g3, comm):
    S = x.shape[0]
    tab_np = _tables(S)
    tab = jnp.asarray(tab_np)
    consts = _ret_consts()

    (h, ht, *casts), xres = _rms_fwd(x, g1, comm.to_cast(), comm.carry("rms_fwd"))
    comm.cast_done(casts)
    comm.took("rms_fwd", xres)
    w_in = comm.weight(0)
    proj, xres = _in_proj(h, w_in, tab, comm.carry("in_proj"))
    comm.took("in_proj", xres)
    qkvs, o_parts, lse_parts = [], [], []
    for gi, d in enumerate(DILATIONS):
        qkv = proj if d == 1 else _qkv_to_sub(proj, d, gi)
        (o_g, lse_g), xres = _attn_fwd(qkv, d, gi, comm.carry(f"attn_fwd_g{gi}"))
        comm.took(f"attn_fwd_g{gi}", xres)
        qkvs.append(qkv)
        o_parts.append(o_g)
        lse_parts.append(lse_g)
    att, lse_views = _attn_merge(o_parts, lse_parts)
    (yrin, rn, rstd, states), xres = _ret_fwd(proj, consts, comm.carry("ret_fwd"))
    comm.took("ret_fwd", xres)
    wa, wr, wo = comm.weight(1), comm.weight(2), comm.weight(3)
    (merged, ya, yr, x1, h2), xres = _mix_out(att, yrin, proj, wa, wr, wo, x, g2, comm.carry("mix_out"))
    comm.took("mix_out", xres)
    wg, wu = comm.weight(4), comm.weight(5)
    (gte, up, act), xres = _ffn_up(h2, wg, wu, comm.carry("ffn_up"))
    comm.took("ffn_up", xres)
    wd = comm.weight(6)
    dx2, dx2b, dg3, loss_p = _ffn_down_loss(act, wd, x1, g3, tgt)

    dgte, dup, dx1, dx1b, dg2 = _ffn_bwd(dx2b, dx2, wd, wg, wu, gte, up, x1, g2)
    tok3 = lambda w: (lambda tk: pl.BlockSpec((None, tk, w), lambda p, k: (p, k, 0)))
    tok2 = lambda w: (lambda tk: pl.BlockSpec((tk, w), lambda p, k: (k, 0)))
    g_d = _wgrad("wgrad_down", act, dx2b, tok3(HID_S), tok2(D_MODEL), (N_SHARD, HID_S, D_MODEL),
                 pl.BlockSpec((None, HID_S, D_MODEL), lambda p, k: (p, 0, 0)), N_SHARD, S)
    g_g = _wgrad("wgrad_gate", dgte, h2, tok3(HID_S), tok2(D_MODEL), (N_SHARD, HID_S, D_MODEL),
                 pl.BlockSpec((None, HID_S, D_MODEL), lambda p, k: (p, 0, 0)), N_SHARD, S)
    g_u = _wgrad("wgrad_up", dup, h2, tok3(HID_S), tok2(D_MODEL), (N_SHARD, HID_S, D_MODEL),
                 pl.BlockSpec((None, HID_S, D_MODEL), lambda p, k: (p, 0, 0)), N_SHARD, S)
    comm.grads({4: g_g, 5: g_u, 6: g_d})
    (dya, dyr, dproj, dyrin, datt_views, rho_views), xres = _mix_bwd(dx1b, wo, proj, ya, yr, wa, wr, att,
                                                                       comm.carry("mix_bwd"))
    comm.took("mix_bwd", xres)
    colblk = lambda w: (lambda tk: pl.BlockSpec((tk, w), lambda p, k: (k, p)))
    g_o = _wgrad("wgrad_out", merged, dx1b, colblk(256), tok2(D_MODEL), (D_MODEL, D_MODEL),
                 pl.BlockSpec((256, D_MODEL), lambda p, k: (p, 0)), 4, S)
    g_a = _wgrad("wgrad_attn", att, dya, tok2(512), colblk(512), (512, D_MODEL),
                 pl.BlockSpec((512, 512), lambda p, k: (0, p)), 2, S)
    g_r = _wgrad("wgrad_ret", yrin, dyr, colblk(256), tok2(D_MODEL), (D_MODEL, D_MODEL),
                 pl.BlockSpec((256, D_MODEL), lambda p, k: (p, 0)), 4, S)
    comm.grads({1: g_a, 2: g_r.reshape(N_SHARD, 256, D_MODEL), 3: g_o.reshape(N_SHARD, 256, D_MODEL)})
    dproj, xres = _ret_bwd(proj, rn, rstd, dyrin, states, tab, consts, dproj, comm.carry("ret_bwd"))
    comm.took("ret_bwd", xres)
    dqs, dks, dvs = [], [], []
    for gi, d in enumerate(DILATIONS):
        rtab = jnp.asarray(tab_np[0].reshape(3, S // d, d * 128))
        (dq, dk, dv), xres = _attn_bwd(qkvs[gi], datt_views[gi], lse_views[gi], rho_views[gi], rtab, d, gi,
                                       comm.carry(f"attn_bwd_g{gi}"))
        comm.took(f"attn_bwd_g{gi}", xres)
        dqs.append(dq)
        dks.append(dk)
        dvs.append(dv)
    dproj = _assemble_dproj((dqs, dks, dvs), dproj)
    g_sent, xres = _wgrad_in_half(ht, dproj, comm.sidx, False, comm.carry("wgrad_in_sent"))
    comm.took("wgrad_in_sent", xres)
    comm.grads({"in_sent": g_sent})
    g_kept, xres = _wgrad_in_half(ht, dproj, comm.sidx, True, comm.carry("wgrad_in_kept"))
    comm.grads({"in_kept": g_kept})
    comm.took("wgrad_in_kept", xres)
    grad_x, dg1, xres = _in_proj_bwd(dproj, w_in, x, g1, dx1, comm.carry("in_proj_bwd"))
    comm.took("in_proj_bwd", xres)
    return loss_p, grad_x, (dg1, dg2, dg3)


W_KINDS = ("col", "col", "lead", "lead", "lead", "lead", "lead")
W_SHARD = ((1024, W_IN_S), (512, 256), (256, 1024), (256, 1024), (HID_S, 1024), (HID_S, 1024), (HID_S, 1024))
W_TRANSPOSED = (4, 5)
N_W = len(W_KINDS)


def _full_shape(wi):
    R, C = W_SHARD[wi]
    return (R, N_SHARD * C) if W_KINDS[wi] == "col" else (N_SHARD, R, C)


def _view(ref, wi, s, half):
    R, C = W_SHARD[wi]
    rows = pl.ds(half * (R // 2), R // 2)
    if W_KINDS[wi] == "col":
        return ref.at[rows, pl.ds(pl.multiple_of(s * C, 128), C)]
    return ref.at[s, rows, :]


def _mesh_pos():
    x, y, c = lax.axis_index("x"), lax.axis_index("y"), lax.axis_index("c")
    chips = [(1 - x, y), (x, 1 - y), (1 - x, 1 - y)]
    return x, y, c, chips


def _cast_bf16(a):
    R, C = a.shape
    tr = R // 2 if R % 32 == 0 else R

    def body(a_ref, o_ref):
        o_ref[...] = a_ref[...].astype(BF16)

    spec = pl.BlockSpec((tr, C), lambda i: (i, 0))
    return pl.pallas_call(body, out_shape=SDS((R, C), BF16), grid=(R // tr,), in_specs=[spec], out_specs=spec,
                          compiler_params=_cparams("parallel"), name=f"cast_{R}x{C}")(a)


def _remote(send, recv, k, src, dst, to):
    return pltpu.make_async_remote_copy(src_ref=src, dst_ref=dst, send_sem=send.at[k], recv_sem=recv.at[k],
                                        device_id=to, device_id_type=MESH)


def _ex_gather_ring(wis, shards):
    n = len(wis)

    def build(sh, full, send, recv, loc):
        x, y, c, _ = _mesh_pos()
        s_me, sib = 2 * x + y, (x, y, 1 - c)
        xn, yn = (1 - x, y), (x, 1 - y)
        flip = lambda a, b: a + b - 2 * a * b
        via = (flip(x, 1 - c), flip(y, c))
        onto = (flip(x, c), flip(y, 1 - c))
        shard_of = lambda chip: 2 * chip[0] + chip[1]
        starts, waits, sent = [], [], []
        for i, wi in enumerate(wis):
            Rh = W_SHARD[wi][0] // 2
            for hf in range(2):
                cp = pltpu.make_async_copy(sh[i].at[pl.ds(hf * Rh, Rh), :], _view(full[i], wi, s_me, hf), loc.at[2 * i + hf])
                starts.append(cp)
                sent.append(cp.wait)
            for j, chip in enumerate((xn, yn)):
                cp = _remote(send, recv, 6 * i + j, sh[i].at[pl.ds(c * Rh, Rh), :], _view(full[i], wi, s_me, c), (*chip, c))
                starts.append(cp)
                sent.append(cp.wait_send)

        def pass_to_sibling(i, wi, k, s):
            mine = _view(full[i], wi, s, c)
            fw = _remote(send, recv, 6 * i + k, mine, mine, sib)
            waits.append(fw.start)
            sent.append(fw.wait_send)

        for i, wi in enumerate(wis):
            for j, chip in enumerate((xn, yn)):
                land = _view(full[i], wi, shard_of(chip), c)
                waits.append(_remote(send, recv, 6 * i + j, land, land, (*chip, c)).wait_recv)
                pass_to_sibling(i, wi, 3 + j, shard_of(chip))
            relay = _view(full[i], wi, shard_of(via), c)
            fw = _remote(send, recv, 6 * i + 2, relay, relay, (*onto, c))
            waits.append(fw.start)
            sent.append(fw.wait_send)
        s_diag = 2 * (1 - x) + (1 - y)
        for i, wi in enumerate(wis):
            land = _view(full[i], wi, s_diag, c)
            waits.append(_remote(send, recv, 6 * i + 2, land, land, (*onto, c)).wait_recv)
            pass_to_sibling(i, wi, 5, s_diag)
        for i, wi in enumerate(wis):
            for k, s in ((3, shard_of(xn)), (4, shard_of(yn)), (5, s_diag)):
                land = _view(full[i], wi, s, 1 - c)
                waits.append(_remote(send, recv, 6 * i + k, land, land, sib).wait_recv)
        return starts, waits + sent

    return _Exchange(shards, [SDS(_full_shape(wi), BF16) for wi in wis], {}, 6 * n, 2 * n, build)


def _ex_gather_ici(wis, shards, then_d2d=False):
    n = len(wis)

    def build(ins, outs, send, recv, loc):
        x, y, c, chips = _mesh_pos()
        s_me, sib = 2 * x + y, (x, y, 1 - c)
        starts, waits, after = [], [], []
        for i, wi in enumerate(wis):
            Rh = W_SHARD[wi][0] // 2
            for hf in range(2):
                cp = pltpu.make_async_copy(ins[i].at[pl.ds(hf * Rh, Rh), :], _view(outs[i], wi, s_me, hf), loc.at[2 * i + hf])
                starts.append(cp)
                waits.append(cp.wait)
            for j, chip in enumerate(chips):
                cp = _remote(send, recv, 3 * i + j, ins[i].at[pl.ds(c * Rh, Rh), :], _view(outs[i], wi, s_me, c), (*chip, c))
                land = _view(outs[i], wi, 2 * chip[0] + chip[1], c)
                starts.append(cp)
                waits += [cp.wait_send, _remote(send, recv, 3 * i + j, land, land, (*chip, c)).wait_recv]
                if then_d2d:
                    theirs = _view(outs[i], wi, 2 * chip[0] + chip[1], 1 - c)
                    fw = _remote(send, recv, 3 * n + 3 * i + j, land, land, sib)
                    waits.append(fw.start)
                    after += [fw.wait_send, _remote(send, recv, 3 * n + 3 * i + j, theirs, theirs, sib).wait_recv]
        return starts, waits + after

    return _Exchange(shards, [SDS(_full_shape(wi), BF16) for wi in wis], {}, (6 if then_d2d else 3) * n, 2 * n, build)


def _ex_gather_d2d(wis, fulls):
    def build(ins, outs, send, recv, loc):
        x, y, c, chips = _mesh_pos()
        sib = (x, y, 1 - c)
        starts, waits = [], []
        for i, wi in enumerate(wis):
            for j, chip in enumerate(chips):
                mine = _view(outs[i], wi, 2 * chip[0] + chip[1], c)
                theirs = _view(outs[i], wi, 2 * chip[0] + chip[1], 1 - c)
                cp = _remote(send, recv, 3 * i + j, mine, mine, sib)
                starts.append(cp)
                waits += [cp.wait_send, _remote(send, recv, 3 * i + j, theirs, theirs, sib).wait_recv]
        return starts, waits

    return _Exchange(fulls, [SDS(f.shape, BF16) for f in fulls], {i: i for i in range(len(wis))}, 3 * len(wis), 0, build)


def _half_shape(wi):
    R, C = W_SHARD[wi]
    return (R // 2, N_SHARD * C) if W_KINDS[wi] == "col" else (N_SHARD, R // 2, C)


def _ex_pair(wis, grads):
    def build(ins, outs, send, recv, loc):
        x, y, c, _ = _mesh_pos()
        starts, waits = [], []
        for i, wi in enumerate(wis):
            Rh = W_SHARD[wi][0] // 2
            rows = pl.ds((1 - c) * Rh, Rh)
            if tuple(ins[i].shape) == _half_shape(wi):
                src = ins[i]
            else:
                src = ins[i].at[rows, :] if W_KINDS[wi] == "col" else ins[i].at[:, rows, :]
            cp = _remote(send, recv, i, src, outs[i], (x, y, 1 - c))
            starts.append(cp)
            waits.append(cp.wait)
        return starts, waits

    return _Exchange(grads, [SDS(_half_shape(wi), F32) for wi in wis], {}, len(wis), 0, build)


def _ex_chip(wis, pbs):
    def build(ins, outs, send, recv, loc):
        x, y, c, chips = _mesh_pos()
        starts, waits = [], []
        for i, wi in enumerate(wis):
            for j, chip in enumerate(chips):
                cp = _remote(send, recv, 3 * i + j, ins[i].at[j], outs[i].at[j], (*chip, c))
                starts.append(cp)
                waits.append(cp.wait)
        return starts, waits

    shapes = [SDS((3, W_SHARD[wi][0] // 2, W_SHARD[wi][1]), BF16) for wi in wis]
    return _Exchange(pbs, shapes, {}, 3 * len(wis), 0, build)


def _ex_share(wis, halves):
    def build(ins, outs, send, recv, loc):
        x, y, c, _ = _mesh_pos()
        sib = (x, y, 1 - c)
        starts, waits = [], []
        for i, wi in enumerate(wis):
            cp = _remote(send, recv, i, outs[i].at[c], outs[i].at[c], sib)
            starts.append(cp)
            waits += [cp.wait_send, _remote(send, recv, i, outs[i].at[1 - c], outs[i].at[1 - c], sib).wait_recv]
        return starts, waits

    return _Exchange(halves, [SDS(h.shape, F32) for h in halves], {i: i for i in range(len(wis))}, len(wis), 0, build)


def _row_tile(rh, C):
    best = 16
    for t in range(16, rh + 1, 16):
        if rh % t == 0 and t * C * 4 <= (3 << 19):
            best = t
    return best


def _pair_sum(wi, g, ra, sidx):
    R, C = W_SHARD[wi]
    Rh = R // 2
    tr = _row_tile(Rh, C)
    nt = Rh // tr
    off = 0 if tuple(g.shape) == _half_shape(wi) else nt
    col = W_KINDS[wi] == "col"

    def body(sidx_ref, *refs):
        gs, rs = refs[:4], refs[4:8]
        own_ref, pb_ref = refs[8:]
        own_ref[...] = gs[0][...] + rs[0][...]
        for j in range(3):
            pb_ref[j] = (gs[1 + j][...] + rs[1 + j][...]).astype(BF16)

    def gspec(slot):
        if col:
            return pl.BlockSpec((tr, C), lambda i, sx: (sx[4] * off + i, sx[slot]))
        return pl.BlockSpec((None, tr, C), lambda i, sx: (sx[slot], sx[4] * off + i, 0))

    def rspec(slot):
        if col:
            return pl.BlockSpec((tr, C), lambda i, sx: (i, sx[slot]))
        return pl.BlockSpec((None, tr, C), lambda i, sx: (sx[slot], i, 0))

    return pl.pallas_call(
        body, out_shape=(SDS((Rh, C), F32), SDS((3, Rh, C), BF16)),
        grid_spec=pltpu.PrefetchScalarGridSpec(
            num_scalar_prefetch=1, grid=(nt,),
            in_specs=[gspec(k) for k in range(4)] + [rspec(k) for k in range(4)],
            out_specs=(pl.BlockSpec((tr, C), lambda i, sx: (i, 0)), pl.BlockSpec((3, tr, C), lambda i, sx: (0, i, 0)))),
        compiler_params=_cparams("arbitrary"), name=f"pair_sum_w{wi}")(sidx, g, g, g, g, ra, ra, ra, ra)


def _chip_sum(wi, own, rb, sidx):
    R, C = W_SHARD[wi]
    Rh = R // 2
    tr = _row_tile(Rh, C)

    def body(sidx_ref, own_ref, rb_ref, o_ref):
        o_ref[...] = ((own_ref[...] + rb_ref[0].astype(F32)) + rb_ref[1].astype(F32)) + rb_ref[2].astype(F32)

    return pl.pallas_call(
        body, out_shape=SDS((2, Rh, C), F32),
        grid_spec=pltpu.PrefetchScalarGridSpec(
            num_scalar_prefetch=1, grid=(Rh // tr,),
            in_specs=[pl.BlockSpec((tr, C), lambda i, sx: (i, 0)), pl.BlockSpec((3, tr, C), lambda i, sx: (0, i, 0))],
            out_specs=pl.BlockSpec((None, tr, C), lambda i, sx: (sx[4], i, 0))),
        compiler_params=_cparams("arbitrary"), name=f"chip_sum_w{wi}")(sidx, own, rb)


def _gain_allgather(blk, ex):
    m_per, n = blk.shape
    n_in, n_out = len(ex.ins), len(ex.out_shapes)

    def body(x_ref, *rest):
        xin, out_ref, xout = rest[:n_in], rest[n_in], rest[n_in + 1:n_in + 1 + n_out]
        send_sems, recv_sems, local_sem = rest[n_in + 1 + n_out:n_in + 4 + n_out]
        ex_starts, ex_waits = ex.build(xin, xout, *rest[n_in + 4 + n_out:])
        for cp in ex_starts:
            cp.start()
        x, y, c, chips = _mesh_pos()
        me, sibling = (x, y, c), (x, y, 1 - c)

        def rows(px, py, pc):
            return out_ref.at[pl.ds((4 * px + 2 * py + pc) * m_per, m_per), :]

        def copy(k, block, to, src=None):
            return pltpu.make_async_remote_copy(
                src_ref=rows(*block) if src is None else src, dst_ref=rows(*block),
                send_sem=send_sems.at[k], recv_sem=recv_sems.at[k], device_id=to, device_id_type=MESH)

        mine = pltpu.make_async_copy(x_ref, rows(*me), local_sem)
        mine.start()
        first = [copy(0, me, sibling, src=x_ref)]
        first += [copy(1 + j, me, (*chip, c), src=x_ref) for j, chip in enumerate(chips)]
        for cp in first:
            cp.start()
        passed = [copy(4 + j, (*chip, c), sibling) for j, chip in enumerate(chips)]
        for j, chip in enumerate(chips):
            copy(1 + j, (*chip, c), me).wait_recv()
            passed[j].start()
        copy(0, sibling, me).wait_recv()
        for j, chip in enumerate(chips):
            copy(4 + j, (*chip, 1 - c), me).wait_recv()
        for cp in first + passed:
            cp.wait_send()
        mine.wait()
        for w in ex_waits:
            w()

    vm = pl.BlockSpec(memory_space=pltpu.VMEM)
    res = pl.pallas_call(
        body, out_shape=(SDS((8 * m_per, n), blk.dtype), *ex.out_shapes),
        in_specs=[vm] + [ANY] * n_in, out_specs=(vm, *[ANY] * n_out),
        input_output_aliases={1 + a: 1 + o for a, o in ex.aliases.items()},
        scratch_shapes=[pltpu.SemaphoreType.DMA((7,)), pltpu.SemaphoreType.DMA((7,)), pltpu.SemaphoreType.DMA] + ex.sems(),
        name="gain_allgather")(blk, *ex.ins)
    return res[0], tuple(res[1:])


def _adam_math(w, g, m, v):
    mn = ADAM_B1 * m + (1.0 - ADAM_B1) * g
    vn = ADAM_B2 * v + (1.0 - ADAM_B2) * (g * g)
    mh = mn / (1.0 - ADAM_B1 ** ADAM_STEP)
    vh = vn / (1.0 - ADAM_B2 ** ADAM_STEP)
    return -ADAM_LR * (mh / (jnp.sqrt(vh) + ADAM_EPS) + ADAM_WD * w), mn, vn


def _adamw(name, ws, gs, ms, vs):
    n, steps = len(ws), 8

    def body(*refs):
        for k in range(n):
            w_ref, g_ref, m_ref, v_ref = refs[4 * k:4 * k + 4]
            go_ref, d_ref, mn_ref, vn_ref = refs[4 * n + 4 * k:4 * n + 4 * k + 4]
            g = g_ref[...]
            go_ref[...] = g
            d_ref[...], mn_ref[...], vn_ref[...] = _adam_math(w_ref[...], g, m_ref[...], v_ref[...])

    specs = [pl.BlockSpec((w.shape[0] // steps, w.shape[1]), lambda i: (i, 0)) for w in ws for _ in range(4)]
    res = pl.pallas_call(
        body, out_shape=tuple(SDS(w.shape, F32) for w in ws for _ in range(4)), grid=(steps,),
        in_specs=specs, out_specs=tuple(specs), compiler_params=_cparams("parallel"),
        name=name)(*[a for k in range(n) for a in (ws[k], gs[k], ms[k], vs[k])])
    return [tuple(res[4 * k:4 * k + 4]) for k in range(n)]


def _gain_update(gathered, w, m, v, through):
    def body(ga_ref, w_ref, m_ref, v_ref, t_ref, g_ref, d_ref, mn_ref, vn_ref, to_ref):
        g = ga_ref[0:8, :]
        for dev in range(1, 8):
            g = g + ga_ref[8 * dev:8 * dev + 8, :]
        g_ref[...] = g
        d_ref[...], mn_ref[...], vn_ref[...] = _adam_math(w_ref[...], g, m_ref[...], v_ref[...])

    vm = pl.BlockSpec(memory_space=pltpu.VMEM)
    return pl.pallas_call(body, out_shape=(SDS((8, 1024), F32),) * 4 + (SDS(through.shape, through.dtype),),
                          in_specs=[vm] * 4 + [ANY], out_specs=(vm,) * 4 + (ANY,), input_output_aliases={4: 4},
                          name="gain_update")(gathered, w, m, v, through)


GROUP_FFN, GROUP_MIX, GROUP_IN = (4, 5, 6), (1, 2, 3), (0,)
REST = GROUP_MIX + GROUP_FFN


class _MeshComm:
    SCHEDULE = {
        "rms_fwd": [("ring", GROUP_IN)],
        "in_proj": [("ici", (1, 2, 3, 4))],
        "ret_fwd": [("d2d", (1, 2, 3, 4)), ("ici", (5,))],
        "mix_out": [("d2d", (5,))],
        "ffn_up": [("both", (6,))],
        "mix_bwd": [("pair", GROUP_FFN)],
        "ret_bwd": [("pair", GROUP_MIX), ("chip", (4,))],
        "attn_bwd_g0": [("chip", (5,))],
        "attn_bwd_g1": [("chip", (6,))],
        "attn_bwd_g2": [("chip", GROUP_MIX)],
        "wgrad_in_kept": [("pair", GROUP_IN), ("share", GROUP_FFN + GROUP_MIX)],
        "in_proj_bwd": [("chip", GROUP_IN)],
    }

    def __init__(self, w_in_shard, rest_f32):
        xi, yi, ci = lax.axis_index("x"), lax.axis_index("y"), lax.axis_index("c")
        self.sidx = jnp.stack([2 * xi + yi, 2 * (1 - xi) + yi, 2 * xi + (1 - yi), 2 * (1 - xi) + (1 - yi), ci]).astype(jnp.int32)
        self.shards, self.rest_f32, self.full = {0: w_in_shard}, list(rest_f32), {}
        self.g, self.own, self.pb, self.half, self.red = {}, {}, {}, {}, {}

    def to_cast(self):
        return self.rest_f32

    def cast_done(self, casts):
        self.shards.update(zip(REST, casts))

    def weight(self, wi):
        return self.full[wi].reshape(D_MODEL, D_MODEL) if wi in (2, 3) else self.full[wi]

    def grads(self, by_wi):
        self.g.update(by_wi)

    def _exchange(self, stage, wis):
        pick = lambda table: [table[wi] for wi in wis]
        if stage == "ring":
            return _ex_gather_ring(wis, pick(self.shards))
        if stage == "ici":
            return _ex_gather_ici(wis, pick(self.shards))
        if stage == "both":
            return _ex_gather_ici(wis, pick(self.shards), then_d2d=True)
        if stage == "d2d":
            return _ex_gather_d2d(wis, pick(self.full))
        if stage == "pair":
            return _ex_pair(wis, [self.g["in_sent"] if wi == 0 else self.g[wi] for wi in wis])
        if stage == "chip":
            return _ex_chip(wis, pick(self.pb))
        return _ex_share(wis, pick(self.half))

    def _landed(self, stage, wis, res):
        for wi, r in zip(wis, res):
            if stage in ("ring", "ici", "d2d", "both"):
                self.full[wi] = r
            elif stage == "pair":
                self.own[wi], self.pb[wi] = _pair_sum(wi, self.g["in_kept"] if wi == 0 else self.g[wi], r, self.sidx)
            elif stage == "chip":
                self.half[wi] = _chip_sum(wi, self.own[wi], r, self.sidx)
            else:
                self.red[wi] = r

    def carry(self, point):
        return [self._exchange(stage, wis) for stage, wis in self.SCHEDULE.get(point, ())]

    def took(self, point, xres):
        for (stage, wis), res in zip(self.SCHEDULE.get(point, ()), xres):
            self._landed(stage, wis, res)

    def last_share(self):
        return self._exchange("share", GROUP_IN)

    def reduced(self, last_shared):
        self._landed("share", GROUP_IN, last_shared)
        return [self.red[wi] for wi in range(N_W)]


def kernel(x, norm_mix_g, w_in, w_out_attn, w_out_ret, w_out, norm_ffn_g, w_ffn_gate, w_ffn_up, w_ffn_down, norm_final_g, loss_target, m_norm_mix_g, m_w_in, m_w_out_attn, m_w_out_ret, m_w_out, m_norm_ffn_g, m_w_ffn_gate, m_w_ffn_up, m_w_ffn_down, m_norm_final_g, v_norm_mix_g, v_w_in, v_w_out_attn, v_w_out_ret, v_w_out, v_norm_ffn_g, v_w_ffn_gate, v_w_ffn_up, v_w_ffn_down, v_norm_final_g):
    ws = (w_in, w_out_attn, w_out_ret, w_out, w_ffn_gate, w_ffn_up, w_ffn_down)
    ms = (m_w_in, m_w_out_attn, m_w_out_ret, m_w_out, m_w_ffn_gate, m_w_ffn_up, m_w_ffn_down)
    vs = (v_w_in, v_w_out_attn, v_w_out_ret, v_w_out, v_w_ffn_gate, v_w_ffn_up, v_w_ffn_down)

    def shard2d(a, wi):
        return jnp.swapaxes(a[0], 0, 1) if wi in W_TRANSPOSED else a.reshape(W_SHARD[wi])

    def as_given(a2d, wi):
        return jnp.swapaxes(a2d, 0, 1)[None] if wi in W_TRANSPOSED else a2d.reshape(ws[wi].shape)

    comm = _MeshComm(_cast_bf16(shard2d(ws[0], 0)), [shard2d(ws[wi], wi) for wi in REST])
    g3 = norm_final_g.reshape(1, D_MODEL)
    loss_p, grad_x, gain_g = _step(x[0], loss_target[0], norm_mix_g, norm_ffn_g, g3, comm)

    pad8 = lambda rows: jnp.concatenate([r.reshape(1, D_MODEL) for r in rows]
                                        + [jnp.zeros((8 - len(rows), D_MODEL), F32)], axis=0)
    gathered, shared = _gain_allgather(pad8((*gain_g, jnp.tile(loss_p[0:1], (1, D_MODEL // 128)))), comm.last_share())
    gred = comm.reduced(shared)

    def adam(name, wis):
        two_d = lambda arrs: [shard2d(arrs[wi], wi) for wi in wis]
        return _adamw(name, two_d(ws), [gred[wi].reshape(W_SHARD[wi]) for wi in wis], two_d(ms), two_d(vs))

    updates = dict(zip(REST + GROUP_IN, adam("adamw_rest", REST) + adam("adamw_w_in", GROUP_IN)))
    outs_g, outs_d, outs_m, outs_v = ([as_given(updates[wi][k], wi) for wi in range(N_W)] for k in range(4))

    gg, gd, gm, gv, grad_x = _gain_update(gathered, pad8((norm_mix_g, norm_ffn_g, norm_final_g)),
                                          pad8((m_norm_mix_g, m_norm_ffn_g, m_norm_final_g)),
                                          pad8((v_norm_mix_g, v_norm_ffn_g, v_norm_final_g)), grad_x[None])
    loss = gg[3, 0]

    def assemble(gain_rows, wlist):
        return (gain_rows[0:1], wlist[0], wlist[1], wlist[2], wlist[3], gain_rows[1:2],
                wlist[4], wlist[5], wlist[6], gain_rows[2])

    return (loss, grad_x, *assemble(gg, outs_g), *assemble(gd, outs_d), *assemble(gm, outs_m), *assemble(gv, outs_v))
```

```python
import functools

import numpy as np
import jax
import jax.numpy as jnp
from jax import lax
from jax.experimental import pallas as pl
from jax.experimental.pallas import tpu as pltpu

F32, BF16 = jnp.float32, jnp.bfloat16
SDS = jax.ShapeDtypeStruct
MESH = pl.DeviceIdType.MESH

D_MODEL = 1024
PROJ_W = 9728
COLB = 512
N_COLB = PROJ_W // COLB
QA_B, KA_B, VA_B = 0, 3, 6
QR_B, KR_B = 9, 10
FFN_HID = 2816
N_SHARD = 4
HID_S = FFN_HID // N_SHARD
W_IN_S = PROJ_W // N_SHARD
DILATIONS = (1, 4, 16)
BLK = 128
RET_HEADS = 4
ROPE_THETA = 10000.0
NORM_EPS = 1e-6
ADAM_LR, ADAM_B1, ADAM_B2, ADAM_EPS, ADAM_WD, ADAM_STEP = 0.001, 0.9, 0.999, 1e-08, 0.01, 10
VMEM_LIMIT = 56 << 20


def _cparams(*sem):
    return pltpu.CompilerParams(dimension_semantics=sem or None, vmem_limit_bytes=VMEM_LIMIT)


def _dot(a, b):
    return jnp.dot(a, b, preferred_element_type=F32)


def _dot_nt(a, b):
    return lax.dot_general(a, b, (((1,), (1,)), ((), ())), preferred_element_type=F32)


def _dot_tn(a, b):
    return lax.dot_general(a, b, (((0,), (0,)), ((), ())), preferred_element_type=F32)


def _row_pieces(tm, sub=512):
    return [slice(i, i + sub) for i in range(0, tm, sub)]


def _sigmoid(z):
    return 0.5 * jnp.tanh(0.5 * z) + 0.5


ANY = pl.BlockSpec(memory_space=pl.ANY)


class _Exchange:
    def __init__(self, ins, out_shapes, aliases, n_sem, n_loc, build):
        self.ins, self.out_shapes, self.aliases = list(ins), list(out_shapes), dict(aliases)
        self.n_sem, self.n_loc, self.build = n_sem, n_loc, build

    def sems(self):
        return [pltpu.SemaphoreType.DMA((self.n_sem,)), pltpu.SemaphoreType.DMA((self.n_sem,)),
                pltpu.SemaphoreType.DMA((max(self.n_loc, 1),))]


def _carrier_call(body, args, *, out_shape, grid, in_specs, out_specs, scratch_shapes=(), sem, name, exchanges=(),
                  prefetch=None, in_out_aliases=None):
    out_shape, out_specs = tuple(out_shape), tuple(out_specs)
    n_in, n_out, n_scr = len(args), len(out_shape), len(scratch_shapes)
    n_pre = 0 if prefetch is None else 1
    x_args, x_outs, x_scr, spans = [], [], [], []
    aliases = {n_pre + a: o for a, o in (in_out_aliases or {}).items()}
    for ex in exchanges:
        i0, o0 = len(x_args), len(x_outs)
        for a, o in ex.aliases.items():
            aliases[n_pre + n_in + i0 + a] = n_out + o0 + o
        x_args += ex.ins
        x_outs += ex.out_shapes
        x_scr += ex.sems()
        spans.append((i0, len(ex.ins), o0, len(ex.out_shapes)))
    nx_in, nx_out = len(x_args), len(x_outs)

    def wrapped(*refs):
        refs = refs[n_pre:]
        ins, xin = refs[:n_in], refs[n_in:n_in + nx_in]
        o_base = n_in + nx_in
        outs, xout = refs[o_base:o_base + n_out], refs[o_base + n_out:o_base + n_out + nx_out]
        s_base = o_base + n_out + nx_out
        scr, xs = refs[s_base:s_base + n_scr], refs[s_base + n_scr:]

        def built(e):
            i0, ni, o0, no = spans[e]
            return exchanges[e].build(xin[i0:i0 + ni], xout[o0:o0 + no], *xs[3 * e:3 * e + 3])

        if exchanges:
            first = functools.reduce(jnp.logical_and, [pl.program_id(k) == 0 for k in range(len(grid))])
            last = functools.reduce(jnp.logical_and, [pl.program_id(k) == grid[k] - 1 for k in range(len(grid))])

            @pl.when(first)
            def _():
                for e in range(len(exchanges)):
                    for cp in built(e)[0]:
                        cp.start()

        body(*ins, *outs, *scr)

        if exchanges:
            @pl.when(last)
            def _():
                for e in range(len(exchanges)):
                    for w in built(e)[1]:
                        w()

    all_in, all_out = list(in_specs) + [ANY] * nx_in, out_specs + tuple([ANY] * nx_out)
    all_scr = list(scratch_shapes) + x_scr
    cparams = _cparams(*(sem if not exchanges else ("arbitrary",) * len(grid)))
    if prefetch is None:
        res = pl.pallas_call(wrapped, out_shape=out_shape + tuple(x_outs), grid=grid, in_specs=all_in, out_specs=all_out,
                             scratch_shapes=all_scr, input_output_aliases=aliases, compiler_params=cparams,
                             name=name)(*args, *x_args)
    else:
        gs = pltpu.PrefetchScalarGridSpec(num_scalar_prefetch=1, grid=grid, in_specs=all_in, out_specs=all_out,
                                          scratch_shapes=all_scr)
        res = pl.pallas_call(wrapped, out_shape=out_shape + tuple(x_outs), grid_spec=gs, input_output_aliases=aliases,
                             compiler_params=cparams, name=name)(prefetch, *args, *x_args)
    xres = [tuple(res[n_out + o0:n_out + o0 + no]) for (_, _, o0, no) in spans]
    return tuple(res[:n_out]), xres


def _tables(S):
    f32 = np.float32
    pos = np.arange(S, dtype=f32)
    lane = np.arange(128)
    inv = (f32(ROPE_THETA) ** (-np.arange(0, 64, 2, dtype=f32) / f32(64))).astype(f32)
    ang = (pos[:, None] * inv[None, :]).astype(np.float64)
    idx = (lane % 64) % 32
    c, s = np.cos(ang)[:, idx], np.sin(ang)[:, idx]
    first = ((lane % 64) < 32)[None, :]
    rope = np.stack([c, np.where(first, 0.0, s), np.where(first, -s, 0.0)])
    base = (f32(1.0) / (f32(ROPE_THETA) ** np.linspace(0.0, 1.0, 64, dtype=f32))).astype(f32)
    ang2 = (pos[:, None] * base[None, :]).astype(np.float64)
    c2, s2 = np.cos(ang2)[:, lane // 2], np.sin(ang2)[:, lane // 2]
    even = (lane % 2 == 0)[None, :]
    th = np.stack([c2, np.where(even, 0.0, s2), np.where(even, -s2, 0.0)])
    return np.stack([rope, th, th * (128 ** -0.5)]).astype(f32)


def _rot(a, c, sa, sb, shift):
    return a * c + pltpu.roll(a, shift, 1) * sa + pltpu.roll(a, 128 - shift, 1) * sb


def _unrot(g, c, sa, sb, shift):
    return g * c + pltpu.roll(g * sa, 128 - shift, 1) + pltpu.roll(g * sb, shift, 1)


def _ret_consts():
    h = np.arange(RET_HEADS, dtype=np.float64)
    log_g = np.log1p(-(2.0 ** (-5.0 - h)))
    idx = np.arange(BLK, dtype=np.float64)
    diff = idx[:, None] - idx[None, :]
    dmask = np.where(diff[None] >= 0, np.exp(np.maximum(diff, 0.0)[None] * log_g[:, None, None]), 0.0)
    zeta = np.exp((BLK - 1 - idx)[None, :] * log_g[:, None])
    xi = np.exp((idx + 1.0)[None, :] * log_g[:, None])
    dec = np.exp(BLK * log_g)
    rep = lambda v: np.broadcast_to(v[:, :, None], (RET_HEADS, BLK, 128))
    return (jnp.asarray(dmask, F32), jnp.asarray(rep(zeta), F32), jnp.asarray(rep(xi), F32),
            jnp.asarray(np.broadcast_to(dec[:, None, None], (RET_HEADS, 8, 256)), F32))


def _rms_fwd(x, g, to_cast=(), exchanges=()):
    S = x.shape[0]
    steps = 4
    tm = S // steps
    n_c = len(to_cast)

    def body(x_ref, g_ref, *refs):
        c_in, (h_ref, ht_ref), c_out = refs[:n_c], refs[n_c:n_c + 2], refs[n_c + 2:]
        for rows in _row_pieces(tm, 512):
            xv = x_ref[rows, :]
            r = lax.rsqrt(jnp.mean(xv * xv, axis=-1, keepdims=True) + NORM_EPS)
            h = xv * r * g_ref[...]
            h_ref[rows, :] = h.astype(BF16)
            ht_ref[:, rows] = h.T.astype(BF16)
        for a_ref, o_ref in zip(c_in, c_out):
            o_ref[...] = a_ref[...].astype(BF16)

    slab = lambda a: pl.BlockSpec((a.shape[0] // steps, a.shape[1]), lambda i: (i, 0))
    return _carrier_call(
        body, (x, g, *to_cast),
        out_shape=(SDS((S, D_MODEL), BF16), SDS((D_MODEL, S), BF16), *[SDS(a.shape, BF16) for a in to_cast]),
        grid=(steps,),
        in_specs=[pl.BlockSpec((tm, D_MODEL), lambda i: (i, 0)), pl.BlockSpec((1, D_MODEL), lambda i: (0, 0))]
        + [slab(a) for a in to_cast],
        out_specs=(pl.BlockSpec((tm, D_MODEL), lambda i: (i, 0)), pl.BlockSpec((D_MODEL, tm), lambda i: (0, i)),
                   *[slab(a) for a in to_cast]),
        sem=("parallel",), name="rms_fwd", exchanges=exchanges)


def _in_proj(h, w_in, tab, exchanges=()):
    S = h.shape[0]
    tm = min(S, 4096)

    def body(h_ref, w_ref, t_ref, o_ref):
        j = pl.program_id(1)
        is_rope = j < 6
        is_theta = (j == QR_B) | (j == KR_B)
        sub = 512

        def rotated(shift):
            for i in range(tm // sub):
                rows = slice(i * sub, (i + 1) * sub)
                acc = _dot(h_ref[rows, :], w_ref[...])
                c, sa, sb = t_ref[0, 0, rows, :], t_ref[0, 1, rows, :], t_ref[0, 2, rows, :]
                for k in range(COLB // 128):
                    sl = slice(k * 128, (k + 1) * 128)
                    o_ref[rows, sl] = _rot(acc[:, sl], c, sa, sb, shift).astype(BF16)

        @pl.when(is_rope)
        def _():
            rotated(32)

        @pl.when(is_theta)
        def _():
            rotated(1)

        @pl.when(jnp.logical_not(is_rope | is_theta))
        def _():
            o_ref[...] = _dot(h_ref[...], w_ref[...]).astype(BF16)

    def tab_map(i, j):
        return (jnp.where(j == QR_B, 1, jnp.where(j == KR_B, 2, 0)), 0, i, 0)

    (proj,), xres = _carrier_call(
        body, (h, w_in, tab), out_shape=(SDS((S, PROJ_W), BF16),), grid=(S // tm, N_COLB),
        in_specs=[pl.BlockSpec((tm, D_MODEL), lambda i, j: (i, 0)),
                  pl.BlockSpec((D_MODEL, COLB), lambda i, j: (0, j)),
                  pl.BlockSpec((1, 3, tm, 128), tab_map)],
        out_specs=(pl.BlockSpec((tm, COLB), lambda i, j: (i, j)),),
        sem=("parallel", "arbitrary"), name="in_proj", exchanges=exchanges)
    return proj, xres


def _band_mask(n):
    qi = lax.broadcasted_iota(jnp.int32, (BLK, 2 * BLK), 0)
    kj = lax.broadcasted_iota(jnp.int32, (BLK, 2 * BLK), 1)
    dist = BLK + qi - kj
    return (dist >= 0) & (dist <= BLK) & ((kj >= BLK) | (n > 0))


def _qkv_col(d, gi):
    if d == 1:
        return lambda t, r: 3 * t + gi
    return lambda t, r: 3 * r + t


def _attn_fwd(qkv, d, gi, exchanges=()):
    L = qkv.shape[0]
    nb = L // BLK
    qb = next(n for n in (4, 2, 1) if nb % n == 0)

    def body(q_ref, kc_ref, kp_ref, vc_ref, vp_ref, o_ref, lse_ref):
        i = pl.program_id(1)
        lane = lax.broadcasted_iota(jnp.int32, (BLK, 128), 1)
        lo = lane < 64
        chunks = [slice(c * 128, (c + 1) * 128) for c in range(4)]
        scores, vals, masks = [], [], []
        for b in range(qb):
            rows = slice(b * BLK, (b + 1) * BLK)
            before = slice((b - 1) * BLK, b * BLK)
            mask = _band_mask(qb * i + b)
            masks.append(jnp.concatenate([mask, mask], axis=0))
            for sl in chunks:
                q = q_ref[rows, sl]
                k = jnp.concatenate([kp_ref[:, sl] if b == 0 else kc_ref[before, sl], kc_ref[rows, sl]], axis=0)
                vals.append(jnp.concatenate([vp_ref[:, sl] if b == 0 else vc_ref[before, sl], vc_ref[rows, sl]], axis=0))
                q2 = jnp.concatenate([jnp.where(lo, q, jnp.zeros_like(q)), jnp.where(lo, jnp.zeros_like(q), q)], axis=0)
                scores.append(_dot_nt(q2, k))
        probs, lses = [], []
        for j, s in enumerate(scores):
            b, c = divmod(j, 4)
            s = jnp.where(masks[b], s * 0.125, jnp.float32(-1e30))
            m = jnp.max(s, axis=-1, keepdims=True)
            p = jnp.exp(s - m)
            l = jnp.sum(p, axis=-1, keepdims=True)
            probs.append((p * (1.0 / l)).astype(BF16))
            lses.append(m + jnp.log(l))
        for j, (p, v) in enumerate(zip(probs, vals)):
            b, c = divmod(j, 4)
            o2 = _dot(p, v)
            o_ref[b * BLK:(b + 1) * BLK, chunks[c]] = jnp.where(lo, o2[:BLK], o2[BLK:])
        for b in range(qb):
            lse_all = jnp.zeros((BLK, 128), F32)
            for c in range(4):
                lse = lses[4 * b + c]
                lse_all = jnp.where(lane // 16 == 2 * c, lse[:BLK], jnp.where(lane // 16 == 2 * c + 1, lse[BLK:], lse_all))
            lse_ref[b * BLK:(b + 1) * BLK, :] = lse_all

    prev = lambda i: jnp.maximum(qb * i - 1, 0)
    col = _qkv_col(d, gi)
    return _carrier_call(
        body, (qkv,) * 5, out_shape=(SDS((L, d * 512), F32), SDS((L, d * 128), F32)), grid=(d, nb // qb),
        in_specs=[pl.BlockSpec((qb * BLK, 512), lambda r, i: (i, col(0, r))),
                  pl.BlockSpec((qb * BLK, 512), lambda r, i: (i, col(1, r))),
                  pl.BlockSpec((BLK, 512), lambda r, i: (prev(i), col(1, r))),
                  pl.BlockSpec((qb * BLK, 512), lambda r, i: (i, col(2, r))),
                  pl.BlockSpec((BLK, 512), lambda r, i: (prev(i), col(2, r)))],
        out_specs=(pl.BlockSpec((qb * BLK, 512), lambda r, i: (i, r)),
                   pl.BlockSpec((qb * BLK, 128), lambda r, i: (i, r))),
        sem=("parallel", "arbitrary"), name=f"attn_fwd_g{gi}", exchanges=exchanges)


def _qkv_to_sub(proj, d, gi):
    S = proj.shape[0]
    tm = 512
    n = tm // d

    def body(q_ref, k_ref, v_ref, o_ref, scr):
        for t, ref in enumerate((q_ref, k_ref, v_ref)):
            for c in range(4):
                scr[c] = ref[:, c * 128:(c + 1) * 128].astype(F32)
            for r in range(d):
                for c in range(4):
                    col = (3 * r + t) * 512 + c * 128
                    o_ref[:, col:col + 128] = scr[c, pl.ds(r, n, stride=d), :].astype(BF16)

    return pl.pallas_call(
        body, out_shape=SDS((S // d, d * 1536), BF16), grid=(S // tm,),
        in_specs=[pl.BlockSpec((tm, 512), lambda i, b=b: (i, b + gi)) for b in (QA_B, KA_B, VA_B)],
        out_specs=pl.BlockSpec((n, d * 1536), lambda i: (i, 0)),
        scratch_shapes=[pltpu.VMEM((4, tm, 128), F32)],
        compiler_params=_cparams("parallel"), name=f"qkv_to_sub_g{gi}")(proj, proj, proj)


def _attn_merge(os_, lses):
    S = os_[0].shape[0]
    tm = 512

    def body(o0, o1, o2, l0, l1, l2, att_ref, lt_ref, lt1_ref, lt2_ref, so1, so2, sl1, sl2):
        lo = lax.broadcasted_iota(jnp.int32, (tm, 128), 1) < 64

        def natural(ref, d, scr, width):
            nch = width // 128
            if d == 1:
                return [ref[:, c * 128:(c + 1) * 128] for c in range(nch)]
            for r in range(d):
                for c in range(nch):
                    scr[c, pl.ds(r, tm // d, stride=d), :] = ref[:, r * width + c * 128:r * width + (c + 1) * 128]
            return [scr[c] for c in range(nch)]

        ls = [natural(l, d, s, 128)[0] for l, d, s in zip((l0, l1, l2), DILATIONS, (None, sl1, sl2))]
        m = jnp.maximum(jnp.maximum(ls[0], ls[1]), ls[2])
        es = [jnp.exp(v - m) for v in ls]
        z = es[0] + es[1] + es[2]
        lt = m + jnp.log(z)
        lt_ref[...] = lt
        sl1[0] = lt
        for ref, d in ((lt1_ref, DILATIONS[1]), (lt2_ref, DILATIONS[2])):
            for r in range(d):
                ref[:, r * 128:(r + 1) * 128] = sl1[0, pl.ds(r, tm // d, stride=d), :]
        ws = [e / z for e in es]
        o_nat = [natural(o, d, s, 512) for o, d, s in zip((o0, o1, o2), DILATIONS, (None, so1, so2))]
        for c in range(4):
            acc = jnp.zeros((tm, 128), F32)
            for g in range(3):
                w_lo = jnp.broadcast_to(ws[g][:, 32 * c:32 * c + 1], (tm, 128))
                w_hi = jnp.broadcast_to(ws[g][:, 32 * c + 16:32 * c + 17], (tm, 128))
                acc = acc + jnp.where(lo, w_lo, w_hi) * o_nat[g][c]
            att_ref[:, c * 128:(c + 1) * 128] = acc.astype(BF16)

    sub = lambda w: [pl.BlockSpec((tm // d, d * w), lambda i: (i, 0)) for d in DILATIONS]
    att, *lts = pl.pallas_call(
        body, out_shape=(SDS((S, 512), BF16), *[SDS((S // d, d * 128), F32) for d in DILATIONS]), grid=(S // tm,),
        in_specs=sub(512) + sub(128),
        out_specs=(pl.BlockSpec((tm, 512), lambda i: (i, 0)), *sub(128)),
        scratch_shapes=[pltpu.VMEM((4, tm, 128), F32), pltpu.VMEM((4, tm, 128), F32),
                        pltpu.VMEM((1, tm, 128), F32), pltpu.VMEM((1, tm, 128), F32)],
        compiler_params=_cparams("parallel"), name="attn_merge")(*os_, *lses)
    return att, lts


def _assemble_dproj(att_grads, dproj):
    S = dproj.shape[0]
    tm = 256

    def body(*refs):
        a = [refs[3 * t:3 * t + 3] for t in range(3)]
        dp_prev, o_ref, scr = refs[9:]
        for t in range(3):
            for g, d in enumerate(DILATIONS):
                base = (3 * t + g) * COLB
                if d == 1:
                    o_ref[:, base:base + COLB] = a[t][g][...]
                    continue
                for c in range(4):
                    for r in range(d):
                        scr[c, pl.ds(r, tm // d, stride=d), :] = a[t][g][:, r * 512 + c * 128:r * 512 + (c + 1) * 128].astype(F32)
                    o_ref[:, base + c * 128:base + (c + 1) * 128] = scr[c].astype(BF16)

    sub = [pl.BlockSpec((tm // d, d * 512), lambda i: (i, 0)) for d in DILATIONS]
    flat = [att_grads[t][g] for t in range(3) for g in range(3)]
    return pl.pallas_call(
        body, out_shape=SDS((S, PROJ_W), BF16), grid=(S // tm,),
        in_specs=sub * 3 + [ANY], out_specs=pl.BlockSpec((tm, 9 * COLB), lambda i: (i, 0)),
        scratch_shapes=[pltpu.VMEM((4, tm, 128), F32)], input_output_aliases={9: 0},
        compiler_params=_cparams("parallel"), name="assemble_dproj")(*flat, dproj)


def _ret_fwd(proj, consts, exchanges=()):
    S = proj.shape[0]
    nc = S // BLK
    dmask, zeta, xi, dec = consts

    def body(q_ref, k_ref, v0_ref, v1_ref, g0_ref, g1_ref, dm_ref, z_ref, x_ref, dec_ref,
             y_ref, rn_ref, rs_ref, st_ref, R):
        @pl.when(pl.program_id(0) == 0)
        def _():
            R[...] = jnp.zeros_like(R)

        lane16 = lax.broadcasted_iota(jnp.int32, (BLK, 128), 1) // 16
        rs_all = jnp.zeros((BLK, 128), F32)
        first = []
        for h in range(RET_HEADS):
            hs = slice(h * 128, (h + 1) * 128)
            q, k = q_ref[:, hs], k_ref[:, hs]
            v = (v0_ref if h < 2 else v1_ref)[:, (h % 2) * 256:(h % 2 + 1) * 256]
            Rb = R[h].astype(BF16)
            st_ref[h] = Rb
            kz = (k.astype(F32) * z_ref[h]).astype(BF16)
            first.append((v, _dot_nt(q, k), _dot((q.astype(F32) * x_ref[h]).astype(BF16), Rb), _dot_tn(kz, v)))
        masked = [(s * dm_ref[h]).astype(BF16) for h, (_, s, _, _) in enumerate(first)]
        for h in range(RET_HEADS):
            vs = slice((h % 2) * 256, (h % 2 + 1) * 256)
            os_ = slice(h * 256, (h + 1) * 256)
            v, _, cross, kv = first[h]
            o = _dot(masked[h], v) + cross
            R[h] = R[h] * dec_ref[h, 0:1, :] + kv
            mu = jnp.mean(o, axis=-1, keepdims=True)
            oc = o - mu
            rstd = lax.rsqrt(jnp.mean(oc * oc, axis=-1, keepdims=True) + NORM_EPS)
            rn = oc * rstd
            gr = (g0_ref if h < 2 else g1_ref)[:, vs].astype(F32)
            y_ref[:, os_] = (rn * gr * _sigmoid(gr)).astype(BF16)
            rn_ref[:, os_] = rn.astype(BF16)
            rs_all = jnp.where(lane16 == h, rstd, rs_all)
        rs_ref[...] = rs_all

    cst = lambda shape: pl.BlockSpec(shape, lambda c: (0, 0, 0))
    blk = lambda j: pl.BlockSpec((BLK, 512), lambda c: (c, j))
    return _carrier_call(
        body, (proj, proj, proj, proj, proj, proj, dmask, zeta, xi, dec),
        out_shape=(SDS((S, 1024), BF16), SDS((S, 1024), BF16), SDS((S, 128), F32), SDS((RET_HEADS, nc, BLK, 256), BF16)),
        grid=(nc,),
        in_specs=[blk(QR_B), blk(KR_B), blk(11), blk(12), blk(13), blk(14),
                  cst((RET_HEADS, BLK, BLK)), cst((RET_HEADS, BLK, 128)), cst((RET_HEADS, BLK, 128)), cst((RET_HEADS, 8, 256))],
        out_specs=(pl.BlockSpec((BLK, 1024), lambda c: (c, 0)), pl.BlockSpec((BLK, 1024), lambda c: (c, 0)),
                   pl.BlockSpec((BLK, 128), lambda c: (c, 0)),
                   pl.BlockSpec((RET_HEADS, None, BLK, 256), lambda c: (0, c, 0, 0))),
        scratch_shapes=[pltpu.VMEM((RET_HEADS, BLK, 256), F32)],
        sem=("arbitrary",), name="ret_fwd", exchanges=exchanges)


def _mix_out(att, yrin, proj, wa, wr, wo, x, g2, exchanges=()):
    S = x.shape[0]
    tm = 512
    gate0 = 15 * COLB

    def body(a_ref, y_ref, ga_ref, gr_ref, wa_ref, wr_ref, wo_ref, x_ref, g_ref, m_ref, ya_ref, yr_ref, x1_ref, h2_ref):
        pieces = _row_pieces(tm, 256)
        branches = [(_dot(a_ref[rows, :], wa_ref[...]), _dot(y_ref[rows, :], wr_ref[...])) for rows in pieces]
        merged = []
        for rows, (ya, yr) in zip(pieces, branches):
            m = (_sigmoid(ga_ref[rows, :].astype(F32)) * ya + _sigmoid(gr_ref[rows, :].astype(F32)) * yr).astype(BF16)
            m_ref[rows, :] = m
            ya_ref[rows, :] = ya.astype(BF16)
            yr_ref[rows, :] = yr.astype(BF16)
            merged.append(m)
        for rows, m in zip(pieces, merged):
            x1 = x_ref[rows, :] + _dot(m, wo_ref[...])
            x1_ref[rows, :] = x1
            r = lax.rsqrt(jnp.mean(x1 * x1, axis=-1, keepdims=True) + NORM_EPS)
            h2_ref[rows, :] = (x1 * r * g_ref[...]).astype(BF16)

    row = lambda w: pl.BlockSpec((tm, w), lambda i: (i, 0))
    cols = lambda c0: pl.BlockSpec((pl.Element(tm), pl.Element(D_MODEL)), lambda i: (i * tm, c0))
    resident = lambda r, c: pl.BlockSpec((r, c), lambda i: (0, 0), pipeline_mode=pl.Buffered(1))
    return _carrier_call(
        body, (att, yrin, proj, proj, wa, wr, wo, x, g2),
        out_shape=(SDS((S, D_MODEL), BF16),) * 3 + (SDS((S, D_MODEL), F32), SDS((S, D_MODEL), BF16)), grid=(S // tm,),
        in_specs=[row(512), row(D_MODEL), cols(gate0), cols(gate0 + D_MODEL), resident(512, D_MODEL),
                  resident(D_MODEL, D_MODEL), resident(D_MODEL, D_MODEL), row(D_MODEL),
                  pl.BlockSpec((1, D_MODEL), lambda i: (0, 0))],
        out_specs=(row(D_MODEL),) * 5, sem=("parallel",), name="mix_out", exchanges=exchanges)


def _ffn_up(h2, wg, wu, exchanges=()):
    S = h2.shape[0]
    tm = min(S, 2048)

    def body(h_ref, wg_ref, wu_ref, g_ref, u_ref, a_ref):
        for rows in _row_pieces(tm):
            hv = h_ref[rows, :]
            g = _dot_nt(hv, wg_ref[...])
            u = _dot_nt(hv, wu_ref[...])
            g_ref[rows, :] = g.astype(BF16)
            u_ref[rows, :] = u.astype(BF16)
            a_ref[rows, :] = (g * _sigmoid(g) * u).astype(BF16)

    wspec = pl.BlockSpec((None, HID_S, D_MODEL), lambda i, s: (s, 0, 0))
    ospec = pl.BlockSpec((None, tm, HID_S), lambda i, s: (s, i, 0))
    return _carrier_call(
        body, (h2, wg, wu), out_shape=(SDS((N_SHARD, S, HID_S), BF16),) * 3, grid=(S // tm, N_SHARD),
        in_specs=[pl.BlockSpec((tm, D_MODEL), lambda i, s: (i, 0)), wspec, wspec],
        out_specs=(ospec, ospec, ospec),
        sem=("parallel", "arbitrary"), name="ffn_up", exchanges=exchanges)


def _ffn_down_loss(act, wd, x1, g3, tgt):
    S = x1.shape[0]
    tm = 512

    def body(a_ref, w_ref, x_ref, g_ref, t_ref, dx_ref, dxb_ref, dg_ref, ls_ref):
        @pl.when(pl.program_id(0) == 0)
        def _():
            dg_ref[...] = jnp.zeros_like(dg_ref)
            ls_ref[...] = jnp.zeros_like(ls_ref)

        g = g_ref[...]
        for rows in _row_pieces(tm, 256):
            y = _dot(a_ref[0, rows, :], w_ref[0])
            for s in range(1, N_SHARD):
                y = y + _dot(a_ref[s, rows, :], w_ref[s])
            x2 = x_ref[rows, :] + y
            r = lax.rsqrt(jnp.mean(x2 * x2, axis=-1, keepdims=True) + NORM_EPS)
            xh = x2 * r
            err = xh * g - t_ref[rows, :]
            ls_ref[...] += jnp.sum(jnp.sum(err * err, axis=-1, keepdims=True), axis=0, keepdims=True) * (0.5 / D_MODEL)
            dy = err * (1.0 / D_MODEL)
            dg_ref[...] += jnp.sum(dy * xh, axis=0, keepdims=True)
            dxh = dy * g
            dx = r * (dxh - xh * jnp.mean(dxh * xh, axis=-1, keepdims=True))
            dx_ref[rows, :] = dx
            dxb_ref[rows, :] = dx.astype(BF16)

    row = pl.BlockSpec((tm, D_MODEL), lambda i: (i, 0))
    vec = pl.BlockSpec((1, D_MODEL), lambda i: (0, 0))
    return pl.pallas_call(
        body, out_shape=(SDS((S, D_MODEL), F32), SDS((S, D_MODEL), BF16), SDS((1, D_MODEL), F32), SDS((8, 128), F32)),
        grid=(S // tm,),
        in_specs=[pl.BlockSpec((N_SHARD, tm, HID_S), lambda i: (0, i, 0)),
                  pl.BlockSpec((N_SHARD, HID_S, D_MODEL), lambda i: (0, 0, 0), pipeline_mode=pl.Buffered(1)),
                  row, vec, row],
        out_specs=(row, row, vec, pl.BlockSpec((8, 128), lambda i: (0, 0))),
        compiler_params=_cparams("arbitrary"), name="ffn_down_loss")(act, wd, x1, g3, tgt)


def _ffn_bwd(dx2b, dx2, wd, wg, wu, gte, up, x1, g2):
    S = x1.shape[0]
    tm = 256

    def body(d_ref, dx2_ref, wd_ref, wg_ref, wu_ref, g_ref, u_ref, x_ref, gn_ref,
             dg_ref, du_ref, dx_ref, dxb_ref, dgn_ref):
        @pl.when(pl.program_id(0) == 0)
        def _():
            dgn_ref[...] = jnp.zeros_like(dgn_ref)

        d = d_ref[...]
        dacts = [_dot_nt(d, wd_ref[s]) for s in range(N_SHARD)]
        dgs, dus = [], []
        for s, da in enumerate(dacts):
            g = g_ref[s].astype(F32)
            sg = _sigmoid(g)
            dgs.append((da * u_ref[s].astype(F32) * sg * (1.0 + g * (1.0 - sg))).astype(BF16))
            dus.append((da * g * sg).astype(BF16))
            dg_ref[s] = dgs[s]
            du_ref[s] = dus[s]
        dh = _dot(dgs[0], wg_ref[0]) + _dot(dus[0], wu_ref[0])
        for s in range(1, N_SHARD):
            dh = dh + _dot(dgs[s], wg_ref[s]) + _dot(dus[s], wu_ref[s])
        xv = x_ref[...]
        r = lax.rsqrt(jnp.mean(xv * xv, axis=-1, keepdims=True) + NORM_EPS)
        xh = xv * r
        dgn_ref[...] += jnp.sum(dh * xh, axis=0, keepdims=True)
        dxh = dh * gn_ref[...]
        dx = dx2_ref[...] + r * (dxh - xh * jnp.mean(dxh * xh, axis=-1, keepdims=True))
        dx_ref[...] = dx
        dxb_ref[...] = dx.astype(BF16)

    row = pl.BlockSpec((tm, D_MODEL), lambda i: (i, 0))
    vec = pl.BlockSpec((1, D_MODEL), lambda i: (0, 0))
    aspec = pl.BlockSpec((N_SHARD, tm, HID_S), lambda i: (0, i, 0))
    resident = lambda shape: pl.BlockSpec(shape, lambda i: (0, 0, 0), pipeline_mode=pl.Buffered(1))
    return pl.pallas_call(
        body,
        out_shape=(SDS((N_SHARD, S, HID_S), BF16), SDS((N_SHARD, S, HID_S), BF16),
                   SDS((S, D_MODEL), F32), SDS((S, D_MODEL), BF16), SDS((1, D_MODEL), F32)),
        grid=(S // tm,),
        in_specs=[row, row, resident((N_SHARD, HID_S, D_MODEL)), resident((N_SHARD, HID_S, D_MODEL)),
                  resident((N_SHARD, HID_S, D_MODEL)), aspec, aspec, row, vec],
        out_specs=(aspec, aspec, row, row, vec),
        compiler_params=_cparams("arbitrary"), name="ffn_bwd")(dx2b, dx2, wd, wg, wu, gte, up, x1, g2)


def _wgrad(name, a, b, a_spec, b_spec, out_shape, out_spec, n_par, S):
    tk = min(S, 4096)

    def body(a_ref, b_ref, o_ref):
        @pl.when(pl.program_id(1) == 0)
        def _():
            o_ref[...] = jnp.zeros_like(o_ref)

        o_ref[...] += _dot_tn(a_ref[...], b_ref[...])

    return pl.pallas_call(
        body, out_shape=SDS(out_shape, F32), grid=(n_par, S // tk),
        in_specs=[a_spec(tk), b_spec(tk)], out_specs=out_spec,
        compiler_params=_cparams("parallel", "arbitrary"), name=name)(a, b)


def _mix_bwd(dx1b, wo, proj, ya, yr, wa, wr, att, exchanges=()):
    S = dx1b.shape[0]
    tm = 512
    gate0 = 15 * COLB

    def body(d_ref, wo_ref, ga_ref, gr_ref, ya_ref, yr_ref, wa_ref, wr_ref, att_ref,
             dya_ref, dyr_ref, dp_ref, dyi_ref, datt_ref, datt1_ref, datt2_ref, rho_ref, rho1_ref, rho2_ref,
             datt_scr, rho_scr):
        pieces = _row_pieces(tm, 256)
        dms = [_dot_nt(d_ref[rows, :], wo_ref[...]) for rows in pieces]
        branch = []
        for rows, dm in zip(pieces, dms):
            sa = _sigmoid(ga_ref[rows, :].astype(F32))
            sr = _sigmoid(gr_ref[rows, :].astype(F32))
            dya, dyr = (dm * sa).astype(BF16), (dm * sr).astype(BF16)
            dya_ref[rows, :] = dya
            dyr_ref[rows, :] = dyr
            dp_ref[rows, 0:D_MODEL] = (dm * ya_ref[rows, :].astype(F32) * sa * (1.0 - sa)).astype(BF16)
            dp_ref[rows, D_MODEL:2 * D_MODEL] = (dm * yr_ref[rows, :].astype(F32) * sr * (1.0 - sr)).astype(BF16)
            branch.append((dya, dyr))
        lane = lax.broadcasted_iota(jnp.int32, (256, 128), 1)
        lo = lane < 64
        for rows, (dya, dyr) in zip(pieces, branch):
            datt = _dot_nt(dya, wa_ref[...])
            datt_ref[rows, :] = datt.astype(BF16)
            dyi_ref[rows, :] = _dot_nt(dyr, wr_ref[...]).astype(BF16)
            prod = datt * att_ref[rows, :].astype(F32)
            rho = jnp.zeros((256, 128), F32)
            for c in range(4):
                pc = prod[:, c * 128:(c + 1) * 128]
                tot = jnp.sum(pc, axis=-1, keepdims=True)
                low = jnp.sum(jnp.where(lo, pc, 0.0), axis=-1, keepdims=True)
                rho = jnp.where(lane // 16 == 2 * c, low, jnp.where(lane // 16 == 2 * c + 1, tot - low, rho))
            rho_ref[rows, :] = rho
            rho_scr[0] = rho
            for c in range(4):
                datt_scr[c] = datt[:, c * 128:(c + 1) * 128]
            for d, dv_ref, rv_ref in ((DILATIONS[1], datt1_ref, rho1_ref), (DILATIONS[2], datt2_ref, rho2_ref)):
                n = 256 // d
                sub_rows = slice(rows.start // d, rows.start // d + n)
                for r in range(d):
                    rv_ref[sub_rows, r * 128:(r + 1) * 128] = rho_scr[0, pl.ds(r, n, stride=d), :]
                    for c in range(4):
                        col = r * 512 + c * 128
                        dv_ref[sub_rows, col:col + 128] = datt_scr[c, pl.ds(r, n, stride=d), :].astype(BF16)

    row = lambda w: pl.BlockSpec((tm, w), lambda i: (i, 0))
    sub = lambda w: [pl.BlockSpec((tm // d, d * w), lambda i: (i, 0)) for d in DILATIONS]
    cols = lambda c0, w: pl.BlockSpec((pl.Element(tm), pl.Element(w)), lambda i: (i * tm, c0))
    resident = lambda r, c: pl.BlockSpec((r, c), lambda i: (0, 0), pipeline_mode=pl.Buffered(1))
    (dya, dyr, dproj, dyrin, *views), xres = _carrier_call(
        body, (dx1b, wo, proj, proj, ya, yr, wa, wr, att),
        out_shape=(SDS((S, D_MODEL), BF16), SDS((S, D_MODEL), BF16), SDS((S, PROJ_W), BF16), SDS((S, D_MODEL), BF16),
                   *[SDS((S // d, d * 512), BF16) for d in DILATIONS], *[SDS((S // d, d * 128), F32) for d in DILATIONS]),
        grid=(S // tm,),
        in_specs=[row(D_MODEL), resident(D_MODEL, D_MODEL), cols(gate0, D_MODEL), cols(gate0 + D_MODEL, D_MODEL),
                  row(D_MODEL), row(D_MODEL), resident(512, D_MODEL), resident(D_MODEL, D_MODEL), row(512)],
        out_specs=(row(D_MODEL), row(D_MODEL), cols(gate0, 2 * D_MODEL), row(D_MODEL), *sub(512), *sub(128)),
        scratch_shapes=[pltpu.VMEM((4, 256, 128), F32), pltpu.VMEM((1, 256, 128), F32)],
        sem=("parallel",), name="mix_bwd", exchanges=exchanges)
    return (dya, dyr, dproj, dyrin, views[:3], views[3:]), xres


def _attn_bwd(qkv, datt, lse, rho, rtab, d, gi, exchanges=()):
    L = qkv.shape[0]
    nb = L // BLK
    T = d * nb

    def body(q_ref, kc_ref, kp_ref, vc_ref, vp_ref, do_ref, lse_ref, rho_ref, tq_ref, tk_ref,
             dq_ref, dk_ref, dv_ref, ck, cv):
        t = pl.program_id(0)
        n = jnp.minimum(t, T - 1) % nb

        @pl.when(t == 0)
        def _():
            ck[...] = jnp.zeros_like(ck)
            cv[...] = jnp.zeros_like(cv)

        def store_rot(ref, val, t_ref, c):
            sl = slice(c * 128, (c + 1) * 128)
            ref[:, sl] = _unrot(val, t_ref[0], t_ref[1], t_ref[2], 32).astype(BF16)

        @pl.when(t < T)
        def _():
            mask = _band_mask(n)
            mask2 = jnp.concatenate([mask, mask], axis=0)
            lo = lax.broadcasted_iota(jnp.int32, (BLK, 128), 1) < 64

            def stacked(a):
                return jnp.concatenate([jnp.where(lo, a, jnp.zeros_like(a)), jnp.where(lo, jnp.zeros_like(a), a)], axis=0)

            def head_cols(ref, c):
                return jnp.concatenate([jnp.broadcast_to(ref[:, 32 * c:32 * c + 1], (BLK, 2 * BLK)),
                                        jnp.broadcast_to(ref[:, 32 * c + 16:32 * c + 17], (BLK, 2 * BLK))], axis=0)

            ops, raw = [], []
            for c in range(4):
                sl = slice(c * 128, (c + 1) * 128)
                q2, do2 = stacked(q_ref[:, sl]), stacked(do_ref[:, sl])
                k = jnp.concatenate([kp_ref[:, sl], kc_ref[:, sl]], axis=0)
                v = jnp.concatenate([vp_ref[:, sl], vc_ref[:, sl]], axis=0)
                ops.append((q2, do2, k))
                raw.append((_dot_nt(q2, k), _dot_nt(do2, v)))
            grads = []
            for c, (s, dp) in enumerate(raw):
                p = jnp.where(mask2, jnp.exp(s * 0.125 - head_cols(lse_ref, c)), 0.0)
                grads.append(((p * (dp - head_cols(rho_ref, c)) * 0.125).astype(BF16), p.astype(BF16)))
            for c, ((q2, do2, k), (ds, pb)) in enumerate(zip(ops, grads)):
                sl = slice(c * 128, (c + 1) * 128)
                dq2 = _dot(ds, k)
                dq_c = jnp.where(lo, dq2[:BLK], dq2[BLK:])
                dk_c = _dot_tn(ds, q2)
                dv_c = _dot_tn(pb, do2)
                store_rot(dq_ref, dq_c, tq_ref, c)
                store_rot(dk_ref, ck[:, sl] + dk_c[:BLK], tk_ref, c)
                dv_ref[:, sl] = (cv[:, sl] + dv_c[:BLK]).astype(BF16)
                ck[:, sl] = dk_c[BLK:]
                cv[:, sl] = dv_c[BLK:]

        @pl.when(t == T)
        def _():
            for c in range(4):
                sl = slice(c * 128, (c + 1) * 128)
                store_rot(dk_ref, ck[:, sl], tk_ref, c)
            dv_ref[...] = cv[...].astype(BF16)

    blk_of = lambda t: (jnp.minimum(t, T - 1) % nb, jnp.minimum(t, T - 1) // nb)
    cur = lambda t: blk_of(t)
    prev = lambda t: (jnp.maximum(blk_of(t)[0] - 1, 0), blk_of(t)[1])
    fin = lambda t: blk_of(jnp.maximum(t - 1, 0))
    col = _qkv_col(d, gi)
    qkv_spec = lambda kind, which: pl.BlockSpec((BLK, 512), lambda t: (which(t)[0], col(kind, which(t)[1])))
    row_spec = lambda w, which: pl.BlockSpec((BLK, w), lambda t: which(t))
    tab_spec = lambda which: pl.BlockSpec((3, BLK, 128), lambda t: (0, *which(t)))
    return _carrier_call(
        body, (qkv, qkv, qkv, qkv, qkv, datt, lse, rho, rtab, rtab),
        out_shape=(SDS((L, d * 512), BF16),) * 3, grid=(T + 1,),
        in_specs=[qkv_spec(0, cur), qkv_spec(1, cur), qkv_spec(1, prev), qkv_spec(2, cur), qkv_spec(2, prev),
                  row_spec(512, cur), row_spec(128, cur), row_spec(128, cur), tab_spec(cur), tab_spec(fin)],
        out_specs=(row_spec(512, cur), row_spec(512, fin), row_spec(512, fin)),
        scratch_shapes=[pltpu.VMEM((BLK, 512), F32), pltpu.VMEM((BLK, 512), F32)],
        sem=("arbitrary",), name=f"attn_bwd_g{gi}", exchanges=exchanges)


def _ret_bwd(proj, rn, rstd, dyrin, states, tab, consts, dproj, exchanges=()):
    S = proj.shape[0]
    nc = S // BLK
    dmask, zeta, xi, dec = consts

    def body(q_ref, k_ref, v0_ref, v1_ref, g0_ref, g1_ref, rn_ref, rs_ref, dy_ref, st_ref, tq_ref, tk_ref,
             dm_ref, z_ref, x_ref, dec_ref, dp_prev, dp_ref, dR):
        dq_ref, dk_ref = dp_ref.at[:, 0:512], dp_ref.at[:, 512:1024]
        dv_ref, dgr_ref = dp_ref.at[:, 1024:2048], dp_ref.at[:, 2048:3072]

        @pl.when(pl.program_id(0) == 0)
        def _():
            dR[...] = jnp.zeros_like(dR)

        dobs = []
        for h in range(RET_HEADS):
            vs = slice((h % 2) * 256, (h % 2 + 1) * 256)
            os_ = slice(h * 256, (h + 1) * 256)
            gr = (g0_ref if h < 2 else g1_ref)[:, vs].astype(F32)
            sg = _sigmoid(gr)
            rn_v = rn_ref[:, os_].astype(F32)
            dyi = dy_ref[:, os_].astype(F32)
            dgr_ref[:, os_] = (dyi * rn_v * sg * (1.0 + gr * (1.0 - sg))).astype(BF16)
            drn = dyi * gr * sg
            rstd = jnp.broadcast_to(rs_ref[:, 16 * h:16 * h + 1], (BLK, 256))
            do = rstd * (drn - jnp.mean(drn, axis=-1, keepdims=True) - rn_v * jnp.mean(drn * rn_v, axis=-1, keepdims=True))
            dobs.append(do.astype(BF16))
        first = []
        for h in range(RET_HEADS):
            hs = slice(h * 128, (h + 1) * 128)
            q, k = q_ref[:, hs], k_ref[:, hs]
            v = (v0_ref if h < 2 else v1_ref)[:, (h % 2) * 256:(h % 2 + 1) * 256]
            dob, dRb = dobs[h], dR[h].astype(BF16)
            kz = (k.astype(F32) * z_ref[h]).astype(BF16)
            qx = (q.astype(F32) * x_ref[h]).astype(BF16)
            first.append((q, k, _dot_nt(q, k), _dot_nt(dob, v), _dot(kz, dRb), _dot_nt(dob, st_ref[h]),
                          _dot_nt(v, dRb), _dot_tn(qx, dob)))
        masked = [((s * dm_ref[h]).astype(BF16), (dsr * dm_ref[h]).astype(BF16))
                  for h, (_, _, s, dsr, _, _, _, _) in enumerate(first)]
        for h in range(RET_HEADS):
            hs = slice(h * 128, (h + 1) * 128)
            os_ = slice(h * 256, (h + 1) * 256)
            q, k, _, _, dv_state, dq_state, dk_state, dr_new = first[h]
            sD, dS = masked[h]
            dv_ref[:, os_] = (_dot_tn(sD, dobs[h]) + dv_state).astype(BF16)
            dq = _dot(dS, k) + dq_state * x_ref[h]
            dk = _dot_tn(dS, q) + dk_state * z_ref[h]
            dR[h] = dR[h] * dec_ref[h, 0:1, :] + dr_new
            dq_ref[:, hs] = _unrot(dq, tq_ref[0], tq_ref[1], tq_ref[2], 1).astype(BF16)
            dk_ref[:, hs] = _unrot(dk, tk_ref[0], tk_ref[1], tk_ref[2], 1).astype(BF16)

    rc = lambda c: nc - 1 - c
    cst = lambda shape: pl.BlockSpec(shape, lambda c: (0, 0, 0))
    blk = lambda j: pl.BlockSpec((BLK, 512), lambda c: (rc(c), j))
    row = lambda w: pl.BlockSpec((BLK, w), lambda c: (rc(c), 0))
    (dproj,), xres = _carrier_call(
        body, (proj, proj, proj, proj, proj, proj, rn, rstd, dyrin, states, tab, tab, dmask, zeta, xi, dec, dproj),
        out_shape=(SDS((S, PROJ_W), BF16),), grid=(nc,),
        in_specs=[blk(QR_B), blk(KR_B), blk(11), blk(12), blk(13), blk(14), row(1024), row(128), row(1024),
                  pl.BlockSpec((RET_HEADS, None, BLK, 256), lambda c: (0, rc(c), 0, 0)),
                  pl.BlockSpec((None, 3, BLK, 128), lambda c: (1, 0, rc(c), 0)),
                  pl.BlockSpec((None, 3, BLK, 128), lambda c: (2, 0, rc(c), 0)),
                  cst((RET_HEADS, BLK, BLK)), cst((RET_HEADS, BLK, 128)), cst((RET_HEADS, BLK, 128)), cst((RET_HEADS, 8, 256)),
                  ANY],
        out_specs=(pl.BlockSpec((pl.Element(BLK), pl.Element(6 * COLB)), lambda c: (rc(c) * BLK, QR_B * COLB)),),
        scratch_shapes=[pltpu.VMEM((RET_HEADS, BLK, 256), F32)],
        sem=("arbitrary",), name="ret_bwd", exchanges=exchanges, in_out_aliases={16: 0})
    return dproj, xres


def _wgrad_in_half(ht, dproj, sidx, kept, exchanges=()):
    S = dproj.shape[0]
    tk = 2048
    half = (lambda sx: sx[4]) if kept else (lambda sx: 1 - sx[4])

    def body(a_ref, b_ref, o_ref):
        @pl.when(pl.program_id(1) == 0)
        def _():
            o_ref[...] = jnp.zeros_like(o_ref)

        o_ref[...] += _dot(a_ref[...], b_ref[...])

    (g,), xres = _carrier_call(
        body, (ht, dproj), out_shape=(SDS((D_MODEL // 2, PROJ_W), F32),), grid=(N_SHARD, S // tk),
        in_specs=[pl.BlockSpec((D_MODEL // 2, tk), lambda s, k, sx: (half(sx), k)),
                  pl.BlockSpec((tk, W_IN_S), lambda s, k, sx: (k, s))],
        out_specs=(pl.BlockSpec((D_MODEL // 2, W_IN_S), lambda s, k, sx: (0, s)),),
        sem=("parallel", "arbitrary"), name="wgrad_in_kept" if kept else "wgrad_in_sent", exchanges=exchanges,
        prefetch=sidx)
    return g, xres


def _in_proj_bwd(dproj, w_in, x, g1, dx1, exchanges=()):
    S = x.shape[0]
    tm = 1024

    def body(d_ref, w_ref, x_ref, g_ref, dx1_ref, dx_ref, dgn_ref, acc):
        i, s = pl.program_id(0), pl.program_id(1)

        @pl.when(s == 0)
        def _():
            acc[...] = jnp.zeros_like(acc)

        @pl.when((i == 0) & (s == 0))
        def _():
            dgn_ref[...] = jnp.zeros_like(dgn_ref)

        acc[...] += _dot_nt(d_ref[...], w_ref[...])

        @pl.when(s == N_SHARD - 1)
        def _():
            xv = x_ref[...]
            r = lax.rsqrt(jnp.mean(xv * xv, axis=-1, keepdims=True) + NORM_EPS)
            xh = xv * r
            dh = acc[...]
            dgn_ref[...] += jnp.sum(dh * xh, axis=0, keepdims=True)
            dxh = dh * g_ref[...]
            dx_ref[...] = dx1_ref[...] + r * (dxh - xh * jnp.mean(dxh * xh, axis=-1, keepdims=True))

    row = pl.BlockSpec((tm, D_MODEL), lambda i, s: (i, 0))
    vec = pl.BlockSpec((1, D_MODEL), lambda i, s: (0, 0))
    (gx, dg), xres = _carrier_call(
        body, (dproj, w_in, x, g1, dx1),
        out_shape=(SDS((1, S, D_MODEL), F32), SDS((1, D_MODEL), F32)), grid=(S // tm, N_SHARD),
        in_specs=[pl.BlockSpec((tm, W_IN_S), lambda i, s: (i, s)),
                  pl.BlockSpec((D_MODEL, W_IN_S), lambda i, s: (0, s)), row, vec, row],
        out_specs=(pl.BlockSpec((None, tm, D_MODEL), lambda i, s: (0, i, 0)), vec),
        scratch_shapes=[pltpu.VMEM((tm, D_MODEL), F32)],
        sem=("arbitrary", "arbitrary"), name="in_proj_bwd", exchanges=exchanges)
    return gx, dg, xres


def _step(x, tgt, g1, g2, g3, comm):
    S = x.shape[0]
    tab_np = _tables(S)
    tab = jnp.asarray(tab_np)
    consts = _ret_consts()

    (h, ht, *casts), xres = _rms_fwd(x, g1, comm.to_cast(), comm.carry("rms_fwd"))
    comm.cast_done(casts)
    comm.took("rms_fwd", xres)
    w_in = comm.weight(0)
    proj, xres = _in_proj(h, w_in, tab, comm.carry("in_proj"))
    comm.took("in_proj", xres)
    qkvs, o_parts, lse_parts = [], [], []
    for gi, d in enumerate(DILATIONS):
        qkv = proj if d == 1 else _qkv_to_sub(proj, d, gi)
        (o_g, lse_g), xres = _attn_fwd(qkv, d, gi, comm.carry(f"attn_fwd_g{gi}"))
        comm.took(f"attn_fwd_g{gi}", xres)
        qkvs.append(qkv)
        o_parts.append(o_g)
        lse_parts.append(lse_g)
    att, lse_views = _attn_merge(o_parts, lse_parts)
    (yrin, rn, rstd, states), xres = _ret_fwd(proj, consts, comm.carry("ret_fwd"))
    comm.took("ret_fwd", xres)
    wa, wr, wo = comm.weight(1), comm.weight(2), comm.weight(3)
    (merged, ya, yr, x1, h2), xres = _mix_out(att, yrin, proj, wa, wr, wo, x, g2, comm.carry("mix_out"))
    comm.took("mix_out", xres)
    wg, wu = comm.weight(4), comm.weight(5)
    (gte, up, act), xres = _ffn_up(h2, wg, wu, comm.carry("ffn_up"))
    comm.took("ffn_up", xres)
    wd = comm.weight(6)
    dx2, dx2b, dg3, loss_p = _ffn_down_loss(act, wd, x1, g3, tgt)

    dgte, dup, dx1, dx1b, dg2 = _ffn_bwd(dx2b, dx2, wd, wg, wu, gte, up, x1, g2)
    tok3 = lambda w: (lambda tk: pl.BlockSpec((None, tk, w), lambda p, k: (p, k, 0)))
    tok2 = lambda w: (lambda tk: pl.BlockSpec((tk, w), lambda p, k: (k, 0)))
    g_d = _wgrad("wgrad_down", act, dx2b, tok3(HID_S), tok2(D_MODEL), (N_SHARD, HID_S, D_MODEL),
                 pl.BlockSpec((None, HID_S, D_MODEL), lambda p, k: (p, 0, 0)), N_SHARD, S)
    g_g = _wgrad("wgrad_gate", dgte, h2, tok3(HID_S), tok2(D_MODEL), (N_SHARD, HID_S, D_MODEL),
                 pl.BlockSpec((None, HID_S, D_MODEL), lambda p, k: (p, 0, 0)), N_SHARD, S)
    g_u = _wgrad("wgrad_up", dup, h2, tok3(HID_S), tok2(D_MODEL), (N_SHARD, HID_S, D_MODEL),
                 pl.BlockSpec((None, HID_S, D_MODEL), lambda p, k: (p, 0, 0)), N_SHARD, S)
    comm.grads({4: g_g, 5: g_u, 6: g_d})
    (dya, dyr, dproj, dyrin, datt_views, rho_views), xres = _mix_bwd(dx1b, wo, proj, ya, yr, wa, wr, att,
                                                                       comm.carry("mix_bwd"))
    comm.took("mix_bwd", xres)
    colblk = lambda w: (lambda tk: pl.BlockSpec((tk, w), lambda p, k: (k, p)))
    g_o = _wgrad("wgrad_out", merged, dx1b, colblk(256), tok2(D_MODEL), (D_MODEL, D_MODEL),
                 pl.BlockSpec((256, D_MODEL), lambda p, k: (p, 0)), 4, S)
    g_a = _wgrad("wgrad_attn", att, dya, tok2(512), colblk(512), (512, D_MODEL),
                 pl.BlockSpec((512, 512), lambda p, k: (0, p)), 2, S)
    g_r = _wgrad("wgrad_ret", yrin, dyr, colblk(256), tok2(D_MODEL), (D_MODEL, D_MODEL),
                 pl.BlockSpec((256, D_MODEL), lambda p, k: (p, 0)), 4, S)
    comm.grads({1: g_a, 2: g_r.reshape(N_SHARD, 256, D_MODEL), 3: g_o.reshape(N_SHARD, 256, D_MODEL)})
    dproj, xres = _ret_bwd(proj, rn, rstd, dyrin, states, tab, consts, dproj, comm.carry("ret_bwd"))
    comm.took("ret_bwd", xres)
    dqs, dks, dvs = [], [], []
    for gi, d in enumerate(DILATIONS):
        rtab = jnp.asarray(tab_np[0].reshape(3, S // d, d * 128))
        (dq, dk, dv), xres = _attn_bwd(qkvs[gi], datt_views[gi], lse_views[gi], rho_views[gi], rtab, d, gi,
                                       comm.carry(f"attn_bwd_g{gi}"))
        comm.took(f"attn_bwd_g{gi}", xres)
        dqs.append(dq)
        dks.append(dk)
        dvs.append(dv)
    dproj = _assemble_dproj((dqs, dks, dvs), dproj)
    g_sent, xres = _wgrad_in_half(ht, dproj, comm.sidx, False, comm.carry("wgrad_in_sent"))
    comm.took("wgrad_in_sent", xres)
    comm.grads({"in_sent": g_sent})
    g_kept, xres = _wgrad_in_half(ht, dproj, comm.sidx, True, comm.carry("wgrad_in_kept"))
    comm.grads({"in_kept": g_kept})
    comm.took("wgrad_in_kept", xres)
    grad_x, dg1, xres = _in_proj_bwd(dproj, w_in, x, g1, dx1, comm.carry("in_proj_bwd"))
    comm.took("in_proj_bwd", xres)
    return loss_p, grad_x, (dg1, dg2, dg3)


W_KINDS = ("col", "col", "lead", "lead", "lead", "lead", "lead")
W_SHARD = ((1024, W_IN_S), (512, 256), (256, 1024), (256, 1024), (HID_S, 1024), (HID_S, 1024), (HID_S, 1024))
W_TRANSPOSED = (4, 5)
N_W = len(W_KINDS)


def _full_shape(wi):
    R, C = W_SHARD[wi]
    return (R, N_SHARD * C) if W_KINDS[wi] == "col" else (N_SHARD, R, C)


def _view(ref, wi, s, half):
    R, C = W_SHARD[wi]
    rows = pl.ds(half * (R // 2), R // 2)
    if W_KINDS[wi] == "col":
        return ref.at[rows, pl.ds(pl.multiple_of(s * C, 128), C)]
    return ref.at[s, rows, :]


def _mesh_pos():
    x, y, c = lax.axis_index("x"), lax.axis_index("y"), lax.axis_index("c")
    chips = [(1 - x, y), (x, 1 - y), (1 - x, 1 - y)]
    return x, y, c, chips


def _cast_bf16(a):
    R, C = a.shape
    tr = R // 2 if R % 32 == 0 else R

    def body(a_ref, o_ref):
        o_ref[...] = a_ref[...].astype(BF16)

    spec = pl.BlockSpec((tr, C), lambda i: (i, 0))
    return pl.pallas_call(body, out_shape=SDS((R, C), BF16), grid=(R // tr,), in_specs=[spec], out_specs=spec,
                          compiler_params=_cparams("parallel"), name=f"cast_{R}x{C}")(a)


def _remote(send, recv, k, src, dst, to):
    return pltpu.make_async_remote_copy(src_ref=src, dst_ref=dst, send_sem=send.at[k], recv_sem=recv.at[k],
                                        device_id=to, device_id_type=MESH)


def _ex_gather_ring(wis, shards):
    n = len(wis)

    def build(sh, full, send, recv, loc):
        x, y, c, _ = _mesh_pos()
        s_me, sib = 2 * x + y, (x, y, 1 - c)
        xn, yn = (1 - x, y), (x, 1 - y)
        flip = lambda a, b: a + b - 2 * a * b
        via = (flip(x, 1 - c), flip(y, c))
        onto = (flip(x, c), flip(y, 1 - c))
        shard_of = lambda chip: 2 * chip[0] + chip[1]
        starts, waits, sent = [], [], []
        for i, wi in enumerate(wis):
            Rh = W_SHARD[wi][0] // 2
            for hf in range(2):
                cp = pltpu.make_async_copy(sh[i].at[pl.ds(hf * Rh, Rh), :], _view(full[i], wi, s_me, hf), loc.at[2 * i + hf])
                starts.append(cp)
                sent.append(cp.wait)
            for j, chip in enumerate((xn, yn)):
                cp = _remote(send, recv, 6 * i + j, sh[i].at[pl.ds(c * Rh, Rh), :], _view(full[i], wi, s_me, c), (*chip, c))
                starts.append(cp)
                sent.append(cp.wait_send)

        def pass_to_sibling(i, wi, k, s):
            mine = _view(full[i], wi, s, c)
            fw = _remote(send, recv, 6 * i + k, mine, mine, sib)
            waits.append(fw.start)
            sent.append(fw.wait_send)

        for i, wi in enumerate(wis):
            for j, chip in enumerate((xn, yn)):
                land = _view(full[i], wi, shard_of(chip), c)
                waits.append(_remote(send, recv, 6 * i + j, land, land, (*chip, c)).wait_recv)
                pass_to_sibling(i, wi, 3 + j, shard_of(chip))
            relay = _view(full[i], wi, shard_of(via), c)
            fw = _remote(send, recv, 6 * i + 2, relay, relay, (*onto, c))
            waits.append(fw.start)
            sent.append(fw.wait_send)
        s_diag = 2 * (1 - x) + (1 - y)
        for i, wi in enumerate(wis):
            land = _view(full[i], wi, s_diag, c)
            waits.append(_remote(send, recv, 6 * i + 2, land, land, (*onto, c)).wait_recv)
            pass_to_sibling(i, wi, 5, s_diag)
        for i, wi in enumerate(wis):
            for k, s in ((3, shard_of(xn)), (4, shard_of(yn)), (5, s_diag)):
                land = _view(full[i], wi, s, 1 - c)
                waits.append(_remote(send, recv, 6 * i + k, land, land, sib).wait_recv)
        return starts, waits + sent

    return _Exchange(shards, [SDS(_full_shape(wi), BF16) for wi in wis], {}, 6 * n, 2 * n, build)


def _ex_gather_ici(wis, shards, then_d2d=False):
    n = len(wis)

    def build(ins, outs, send, recv, loc):
        x, y, c, chips = _mesh_pos()
        s_me, sib = 2 * x + y, (x, y, 1 - c)
        starts, waits, after = [], [], []
        for i, wi in enumerate(wis):
            Rh = W_SHARD[wi][0] // 2
            for hf in range(2):
                cp = pltpu.make_async_copy(ins[i].at[pl.ds(hf * Rh, Rh), :], _view(outs[i], wi, s_me, hf), loc.at[2 * i + hf])
                starts.append(cp)
                waits.append(cp.wait)
            for j, chip in enumerate(chips):
                cp = _remote(send, recv, 3 * i + j, ins[i].at[pl.ds(c * Rh, Rh), :], _view(outs[i], wi, s_me, c), (*chip, c))
                land = _view(outs[i], wi, 2 * chip[0] + chip[1], c)
                starts.append(cp)
                waits += [cp.wait_send, _remote(send, recv, 3 * i + j, land, land, (*chip, c)).wait_recv]
                if then_d2d:
                    theirs = _view(outs[i], wi, 2 * chip[0] + chip[1], 1 - c)
                    fw = _remote(send, recv, 3 * n + 3 * i + j, land, land, sib)
                    waits.append(fw.start)
                    after += [fw.wait_send, _remote(send, recv, 3 * n + 3 * i + j, theirs, theirs, sib).wait_recv]
        return starts, waits + after

    return _Exchange(shards, [SDS(_full_shape(wi), BF16) for wi in wis], {}, (6 if then_d2d else 3) * n, 2 * n, build)


def _ex_gather_d2d(wis, fulls):
    def build(ins, outs, send, recv, loc):
        x, y, c, chips = _mesh_pos()
        sib = (x, y, 1 - c)
        starts, waits = [], []
        for i, wi in enumerate(wis):
            for j, chip in enumerate(chips):
                mine = _view(outs[i], wi, 2 * chip[0] + chip[1], c)
                theirs = _view(outs[i], wi, 2 * chip[0] + chip[1], 1 - c)
                cp = _remote(send, recv, 3 * i + j, mine, mine, sib)
                starts.append(cp)
                waits += [cp.wait_send, _remote(send, recv, 3 * i + j, theirs, theirs, sib).wait_recv]
        return starts, waits

    return _Exchange(fulls, [SDS(f.shape, BF16) for f in fulls], {i: i for i in range(len(wis))}, 3 * len(wis), 0, build)


def _half_shape(wi):
    R, C = W_SHARD[wi]
    return (R // 2, N_SHARD * C) if W_KINDS[wi] == "col" else (N_SHARD, R // 2, C)


def _ex_pair(wis, grads):
    def build(ins, outs, send, recv, loc):
        x, y, c, _ = _mesh_pos()
        starts, waits = [], []
        for i, wi in enumerate(wis):
            Rh = W_SHARD[wi][0] // 2
            rows = pl.ds((1 - c) * Rh, Rh)
            if tuple(ins[i].shape) == _half_shape(wi):
                src = ins[i]
            else:
                src = ins[i].at[rows, :] if W_KINDS[wi] == "col" else ins[i].at[:, rows, :]
            cp = _remote(send, recv, i, src, outs[i], (x, y, 1 - c))
            starts.append(cp)
            waits.append(cp.wait)
        return starts, waits

    return _Exchange(grads, [SDS(_half_shape(wi), F32) for wi in wis], {}, len(wis), 0, build)


def _ex_chip(wis, pbs):
    def build(ins, outs, send, recv, loc):
        x, y, c, chips = _mesh_pos()
        starts, waits = [], []
        for i, wi in enumerate(wis):
            for j, chip in enumerate(chips):
                cp = _remote(send, recv, 3 * i + j, ins[i].at[j], outs[i].at[j], (*chip, c))
                starts.append(cp)
                waits.append(cp.wait)
        return starts, waits

    shapes = [SDS((3, W_SHARD[wi][0] // 2, W_SHARD[wi][1]), BF16) for wi in wis]
    return _Exchange(pbs, shapes, {}, 3 * len(wis), 0, build)


def _ex_share(wis, halves):
    def build(ins, outs, send, recv, loc):
        x, y, c, _ = _mesh_pos()
        sib = (x, y, 1 - c)
        starts, waits = [], []
        for i, wi in enumerate(wis):
            cp = _remote(send, recv, i, outs[i].at[c], outs[i].at[c], sib)
            starts.append(cp)
            waits += [cp.wait_send, _remote(send, recv, i, outs[i].at[1 - c], outs[i].at[1 - c], sib).wait_recv]
        return starts, waits

    return _Exchange(halves, [SDS(h.shape, F32) for h in halves], {i: i for i in range(len(wis))}, len(wis), 0, build)


def _row_tile(rh, C):
    best = 16
    for t in range(16, rh + 1, 16):
        if rh % t == 0 and t * C * 4 <= (3 << 19):
            best = t
    return best


def _pair_sum(wi, g, ra, sidx):
    R, C = W_SHARD[wi]
    Rh = R // 2
    tr = _row_tile(Rh, C)
    nt = Rh // tr
    off = 0 if tuple(g.shape) == _half_shape(wi) else nt
    col = W_KINDS[wi] == "col"

    def body(sidx_ref, *refs):
        gs, rs = refs[:4], refs[4:8]
        own_ref, pb_ref = refs[8:]
        own_ref[...] = gs[0][...] + rs[0][...]
        for j in range(3):
            pb_ref[j] = (gs[1 + j][...] + rs[1 + j][...]).astype(BF16)

    def gspec(slot):
        if col:
            return pl.BlockSpec((tr, C), lambda i, sx: (sx[4] * off + i, sx[slot]))
        return pl.BlockSpec((None, tr, C), lambda i, sx: (sx[slot], sx[4] * off + i, 0))

    def rspec(slot):
        if col:
            return pl.BlockSpec((tr, C), lambda i, sx: (i, sx[slot]))
        return pl.BlockSpec((None, tr, C), lambda i, sx: (sx[slot], i, 0))

    return pl.pallas_call(
        body, out_shape=(SDS((Rh, C), F32), SDS((3, Rh, C), BF16)),
        grid_spec=pltpu.PrefetchScalarGridSpec(
            num_scalar_prefetch=1, grid=(nt,),
            in_specs=[gspec(k) for k in range(4)] + [rspec(k) for k in range(4)],
            out_specs=(pl.BlockSpec((tr, C), lambda i, sx: (i, 0)), pl.BlockSpec((3, tr, C), lambda i, sx: (0, i, 0)))),
        compiler_params=_cparams("arbitrary"), name=f"pair_sum_w{wi}")(sidx, g, g, g, g, ra, ra, ra, ra)


def _chip_sum(wi, own, rb, sidx):
    R, C = W_SHARD[wi]
    Rh = R // 2
    tr = _row_tile(Rh, C)

    def body(sidx_ref, own_ref, rb_ref, o_ref):
        o_ref[...] = ((own_ref[...] + rb_ref[0].astype(F32)) + rb_ref[1].astype(F32)) + rb_ref[2].astype(F32)

    return pl.pallas_call(
        body, out_shape=SDS((2, Rh, C), F32),
        grid_spec=pltpu.PrefetchScalarGridSpec(
            num_scalar_prefetch=1, grid=(Rh // tr,),
            in_specs=[pl.BlockSpec((tr, C), lambda i, sx: (i, 0)), pl.BlockSpec((3, tr, C), lambda i, sx: (0, i, 0))],
            out_specs=pl.BlockSpec((None, tr, C), lambda i, sx: (sx[4], i, 0))),
        compiler_params=_cparams("arbitrary"), name=f"chip_sum_w{wi}")(sidx, own, rb)


def _gain_allgather(blk, ex):
    m_per, n = blk.shape
    n_in, n_out = len(ex.ins), len(ex.out_shapes)

    def body(x_ref, *rest):
        xin, out_ref, xout = rest[:n_in], rest[n_in], rest[n_in + 1:n_in + 1 + n_out]
        send_sems, recv_sems, local_sem = rest[n_in + 1 + n_out:n_in + 4 + n_out]
        ex_starts, ex_waits = ex.build(xin, xout, *rest[n_in + 4 + n_out:])
        for cp in ex_starts:
            cp.start()
        x, y, c, chips = _mesh_pos()
        me, sibling = (x, y, c), (x, y, 1 - c)

        def rows(px, py, pc):
            return out_ref.at[pl.ds((4 * px + 2 * py + pc) * m_per, m_per), :]

        def copy(k, block, to, src=None):
            return pltpu.make_async_remote_copy(
                src_ref=rows(*block) if src is None else src, dst_ref=rows(*block),
                send_sem=send_sems.at[k], recv_sem=recv_sems.at[k], device_id=to, device_id_type=MESH)

        mine = pltpu.make_async_copy(x_ref, rows(*me), local_sem)
        mine.start()
        first = [copy(0, me, sibling, src=x_ref)]
        first += [copy(1 + j, me, (*chip, c), src=x_ref) for j, chip in enumerate(chips)]
        for cp in first:
            cp.start()
        passed = [copy(4 + j, (*chip, c), sibling) for j, chip in enumerate(chips)]
        for j, chip in enumerate(chips):
            copy(1 + j, (*chip, c), me).wait_recv()
            passed[j].start()
        copy(0, sibling, me).wait_recv()
        for j, chip in enumerate(chips):
            copy(4 + j, (*chip, 1 - c), me).wait_recv()
        for cp in first + passed:
            cp.wait_send()
        mine.wait()
        for w in ex_waits:
            w()

    vm = pl.BlockSpec(memory_space=pltpu.VMEM)
    res = pl.pallas_call(
        body, out_shape=(SDS((8 * m_per, n), blk.dtype), *ex.out_shapes),
        in_specs=[vm] + [ANY] * n_in, out_specs=(vm, *[ANY] * n_out),
        input_output_aliases={1 + a: 1 + o for a, o in ex.aliases.items()},
        scratch_shapes=[pltpu.SemaphoreType.DMA((7,)), pltpu.SemaphoreType.DMA((7,)), pltpu.SemaphoreType.DMA] + ex.sems(),
        name="gain_allgather")(blk, *ex.ins)
    return res[0], tuple(res[1:])


def _adam_math(w, g, m, v):
    mn = ADAM_B1 * m + (1.0 - ADAM_B1) * g
    vn = ADAM_B2 * v + (1.0 - ADAM_B2) * (g * g)
    mh = mn / (1.0 - ADAM_B1 ** ADAM_STEP)
    vh = vn / (1.0 - ADAM_B2 ** ADAM_STEP)
    return -ADAM_LR * (mh / (jnp.sqrt(vh) + ADAM_EPS) + ADAM_WD * w), mn, vn


def _adamw(name, ws, gs, ms, vs):
    n, steps = len(ws), 8

    def body(*refs):
        for k in range(n):
            w_ref, g_ref, m_ref, v_ref = refs[4 * k:4 * k + 4]
            go_ref, d_ref, mn_ref, vn_ref = refs[4 * n + 4 * k:4 * n + 4 * k + 4]
            g = g_ref[...]
            go_ref[...] = g
            d_ref[...], mn_ref[...], vn_ref[...] = _adam_math(w_ref[...], g, m_ref[...], v_ref[...])

    specs = [pl.BlockSpec((w.shape[0] // steps, w.shape[1]), lambda i: (i, 0)) for w in ws for _ in range(4)]
    res = pl.pallas_call(
        body, out_shape=tuple(SDS(w.shape, F32) for w in ws for _ in range(4)), grid=(steps,),
        in_specs=specs, out_specs=tuple(specs), compiler_params=_cparams("parallel"),
        name=name)(*[a for k in range(n) for a in (ws[k], gs[k], ms[k], vs[k])])
    return [tuple(res[4 * k:4 * k + 4]) for k in range(n)]


def _gain_update(gathered, w, m, v):
    def body(ga_ref, w_ref, m_ref, v_ref, g_ref, d_ref, mn_ref, vn_ref):
        g = ga_ref[0:8, :]
        for dev in range(1, 8):
            g = g + ga_ref[8 * dev:8 * dev + 8, :]
        g_ref[...] = g
        d_ref[...], mn_ref[...], vn_ref[...] = _adam_math(w_ref[...], g, m_ref[...], v_ref[...])

    return pl.pallas_call(body, out_shape=(SDS((8, 1024), F32),) * 4, name="gain_update")(gathered, w, m, v)


GROUP_FFN, GROUP_MIX, GROUP_IN = (4, 5, 6), (1, 2, 3), (0,)
REST = GROUP_MIX + GROUP_FFN


class _MeshComm:
    SCHEDULE = {
        "rms_fwd": [("ring", GROUP_IN)],
        "in_proj": [("ici", (1, 2, 3, 4))],
        "ret_fwd": [("d2d", (1, 2, 3, 4)), ("ici", (5,))],
        "mix_out": [("d2d", (5,))],
        "ffn_up": [("both", (6,))],
        "mix_bwd": [("pair", GROUP_FFN)],
        "ret_bwd": [("pair", GROUP_MIX), ("chip", (4,))],
        "attn_bwd_g0": [("chip", (5,))],
        "attn_bwd_g1": [("chip", (6,))],
        "attn_bwd_g2": [("chip", GROUP_MIX)],
        "wgrad_in_kept": [("pair", GROUP_IN), ("share", GROUP_FFN + GROUP_MIX)],
        "in_proj_bwd": [("chip", GROUP_IN)],
    }

    def __init__(self, w_in_shard, rest_f32):
        xi, yi, ci = lax.axis_index("x"), lax.axis_index("y"), lax.axis_index("c")
        self.sidx = jnp.stack([2 * xi + yi, 2 * (1 - xi) + yi, 2 * xi + (1 - yi), 2 * (1 - xi) + (1 - yi), ci]).astype(jnp.int32)
        self.shards, self.rest_f32, self.full = {0: w_in_shard}, list(rest_f32), {}
        self.g, self.own, self.pb, self.half, self.red = {}, {}, {}, {}, {}

    def to_cast(self):
        return self.rest_f32

    def cast_done(self, casts):
        self.shards.update(zip(REST, casts))

    def weight(self, wi):
        return self.full[wi].reshape(D_MODEL, D_MODEL) if wi in (2, 3) else self.full[wi]

    def grads(self, by_wi):
        self.g.update(by_wi)

    def _exchange(self, stage, wis):
        pick = lambda table: [table[wi] for wi in wis]
        if stage == "ring":
            return _ex_gather_ring(wis, pick(self.shards))
        if stage == "ici":
            return _ex_gather_ici(wis, pick(self.shards))
        if stage == "both":
            return _ex_gather_ici(wis, pick(self.shards), then_d2d=True)
        if stage == "d2d":
            return _ex_gather_d2d(wis, pick(self.full))
        if stage == "pair":
            return _ex_pair(wis, [self.g["in_sent"] if wi == 0 else self.g[wi] for wi in wis])
        if stage == "chip":
            return _ex_chip(wis, pick(self.pb))
        return _ex_share(wis, pick(self.half))

    def _landed(self, stage, wis, res):
        for wi, r in zip(wis, res):
            if stage in ("ring", "ici", "d2d", "both"):
                self.full[wi] = r
            elif stage == "pair":
                self.own[wi], self.pb[wi] = _pair_sum(wi, self.g["in_kept"] if wi == 0 else self.g[wi], r, self.sidx)
            elif stage == "chip":
                self.half[wi] = _chip_sum(wi, self.own[wi], r, self.sidx)
            else:
                self.red[wi] = r

    def carry(self, point):
        return [self._exchange(stage, wis) for stage, wis in self.SCHEDULE.get(point, ())]

    def took(self, point, xres):
        for (stage, wis), res in zip(self.SCHEDULE.get(point, ()), xres):
            self._landed(stage, wis, res)

    def last_share(self):
        return self._exchange("share", GROUP_IN)

    def reduced(self, last_shared):
        self._landed("share", GROUP_IN, last_shared)
        return [self.red[wi] for wi in range(N_W)]


def kernel(x, norm_mix_g, w_in, w_out_attn, w_out_ret, w_out, norm_ffn_g, w_ffn_gate, w_ffn_up, w_ffn_down, norm_final_g, loss_target, m_norm_mix_g, m_w_in, m_w_out_attn, m_w_out_ret, m_w_out, m_norm_ffn_g, m_w_ffn_gate, m_w_ffn_up, m_w_ffn_down, m_norm_final_g, v_norm_mix_g, v_w_in, v_w_out_attn, v_w_out_ret, v_w_out, v_norm_ffn_g, v_w_ffn_gate, v_w_ffn_up, v_w_ffn_down, v_norm_final_g):
    ws = (w_in, w_out_attn, w_out_ret, w_out, w_ffn_gate, w_ffn_up, w_ffn_down)
    ms = (m_w_in, m_w_out_attn, m_w_out_ret, m_w_out, m_w_ffn_gate, m_w_ffn_up, m_w_ffn_down)
    vs = (v_w_in, v_w_out_attn, v_w_out_ret, v_w_out, v_w_ffn_gate, v_w_ffn_up, v_w_ffn_down)

    def shard2d(a, wi):
        return jnp.swapaxes(a[0], 0, 1) if wi in W_TRANSPOSED else a.reshape(W_SHARD[wi])

    def as_given(a2d, wi):
        return jnp.swapaxes(a2d, 0, 1)[None] if wi in W_TRANSPOSED else a2d.reshape(ws[wi].shape)

    comm = _MeshComm(_cast_bf16(shard2d(ws[0], 0)), [shard2d(ws[wi], wi) for wi in REST])
    g3 = norm_final_g.reshape(1, D_MODEL)
    loss_p, grad_x, gain_g = _step(x[0], loss_target[0], norm_mix_g, norm_ffn_g, g3, comm)

    pad8 = lambda rows: jnp.concatenate([r.reshape(1, D_MODEL) for r in rows]
                                        + [jnp.zeros((8 - len(rows), D_MODEL), F32)], axis=0)
    gathered, shared = _gain_allgather(pad8((*gain_g, jnp.tile(loss_p[0:1], (1, D_MODEL // 128)))), comm.last_share())
    gred = comm.reduced(shared)

    def adam(name, wis):
        two_d = lambda arrs: [shard2d(arrs[wi], wi) for wi in wis]
        return _adamw(name, two_d(ws), [gred[wi].reshape(W_SHARD[wi]) for wi in wis], two_d(ms), two_d(vs))

    updates = dict(zip(REST + GROUP_IN, adam("adamw_rest", REST) + adam("adamw_w_in", GROUP_IN)))
    outs_g, outs_d, outs_m, outs_v = ([as_given(updates[wi][k], wi) for wi in range(N_W)] for k in range(4))

    gg, gd, gm, gv = _gain_update(gathered, pad8((norm_mix_g, norm_ffn_g, norm_final_g)),
                                  pad8((m_norm_mix_g, m_norm_ffn_g, m_norm_final_g)),
                                  pad8((v_norm_mix_g, v_norm_ffn_g, v_norm_final_g)))
    loss = gg[3, 0]

    def assemble(gain_rows, wlist):
        return (gain_rows[0:1], wlist[0], wlist[1], wlist[2], wlist[3], gain_rows[1:2],
                wlist[4], wlist[5], wlist[6], gain_rows[2])

    return (loss, grad_x, *assemble(gg, outs_g), *assemble(gd, outs_d), *assemble(gm, outs_m), *assemble(gv, outs_v))
```
